```python
import jax
import jax.numpy as jnp
from jax import lax
import numpy as np

D_MODEL = 1024
BATCH = 8
SEQ = 4096
DEPTH = 1

MEM_LEN = 256
SSM_GROUP = 16
SSM_WIDTH = 768
SSM_GROUPS = SSM_WIDTH // SSM_GROUP
SSM_STATE = 64
SSM_DT_MIN = 0.001
SSM_DT_MAX = 0.1
ATT_HEAD_DIM = 64
ATT_HEADS_PER_GROUP = 4
DILATION_PATTERN = ((128, 1), (512, 4), (2048, 16))
ATT_GROUPS = len(DILATION_PATTERN)
ATT_HEADS = ATT_GROUPS * ATT_HEADS_PER_GROUP
ATT_WIDTH = ATT_HEADS * ATT_HEAD_DIM
ATT_MERGED = ATT_HEADS_PER_GROUP * ATT_HEAD_DIM
ATT_SCALE = ATT_HEAD_DIM ** -0.5
ROT_DIM = ATT_HEAD_DIM // 4
ROPE_THETA = 500000.0
XATT_HEADS = 4
XATT_HEAD_DIM = D_MODEL // XATT_HEADS
XATT_SCALE = XATT_HEAD_DIM ** -0.5
D_FF = 4 * D_MODEL
DEEPNORM_ALPHA = (2 * DEPTH) ** 0.25
DEEPNORM_BETA = (8 * DEPTH) ** -0.25
LN_EPS = 1e-5
NEG_INF = -1e30
OFF_U = 0
OFF_Q = OFF_U + SSM_WIDTH
OFF_K = OFF_Q + ATT_WIDTH
OFF_V = OFF_K + ATT_WIDTH
OFF_GS = OFF_V + ATT_WIDTH
OFF_GA = OFF_GS + D_MODEL
IN_COLS = OFF_GA + D_MODEL

kernel_name = 'hybrid_s5_dilated_attn_block'


def layer_norm(x, g, b):
    xf = x.astype(jnp.float32)
    mu = jnp.mean(xf, axis=-1, keepdims=True)
    var = jnp.mean(jnp.square(xf - mu), axis=-1, keepdims=True)
    y = (xf - mu) * lax.rsqrt(var + LN_EPS) * g.astype(jnp.float32) + b.astype(jnp.float32)
    return y.astype(x.dtype)


def rope_partial(t, cos, sin):
    half = ROT_DIM // 2
    rot = t[..., :ROT_DIM].astype(jnp.float32)
    x1, x2 = rot[..., :half], rot[..., half:]
    c = cos[:, :, None, :]
    s = sin[:, :, None, :]
    rot = jnp.concatenate([x1 * c - x2 * s, x2 * c + x1 * s], axis=-1).astype(t.dtype)
    return jnp.concatenate([rot, t[..., ROT_DIM:]], axis=-1)


def s5_ssm(u, log_dt, a_re, a_im, b_re, b_im, c_re, c_im, d):
    f32 = jnp.float32
    bsz, s, _ = u.shape
    uf = u.astype(f32)
    ug = uf.reshape(bsz, s, SSM_GROUPS, SSM_GROUP)
    a_re = a_re.astype(f32)
    a_im = a_im.astype(f32)
    dt = jnp.exp(log_dt.astype(f32))[:, None]
    mag = jnp.exp(a_re * dt)
    ab_re = mag * jnp.cos(a_im * dt)
    ab_im = mag * jnp.sin(a_im * dt)
    den = jnp.square(a_re) + jnp.square(a_im)
    nr = ab_re - 1.0
    f_re = (nr * a_re + ab_im * a_im) / den
    f_im = (ab_im * a_re - nr * a_im) / den
    b_re = b_re.astype(f32)
    b_im = b_im.astype(f32)
    bb_re = f_re[..., None] * b_re - f_im[..., None] * b_im
    bb_im = f_re[..., None] * b_im + f_im[..., None] * b_re
    w_re = jnp.einsum('bsgc,gnc->bsgn', ug, bb_re)
    w_im = jnp.einsum('bsgc,gnc->bsgn', ug, bb_im)
    ar = jnp.broadcast_to(ab_re, w_re.shape)
    ai = jnp.broadcast_to(ab_im, w_im.shape)

    def combine(e1, e2):
        a1r, a1i, b1r, b1i = e1
        a2r, a2i, b2r, b2i = e2
        return (a2r * a1r - a2i * a1i,
                a2r * a1i + a2i * a1r,
                a2r * b1r - a2i * b1i + b2r,
                a2r * b1i + a2i * b1r + b2i)

    _, _, h_re, h_im = lax.associative_scan(combine, (ar, ai, w_re, w_im), axis=1)
    y = (jnp.einsum('bsgn,gcn->bsgc', h_re, c_re.astype(f32))
         - jnp.einsum('bsgn,gcn->bsgc', h_im, c_im.astype(f32)))
    y = y.reshape(bsz, s, SSM_WIDTH) + d.astype(f32) * uf
    return y.astype(u.dtype)


def dilated_window_attention(q, k, v, window, dilation):
    bsz, s, h, dh = q.shape
    span = window // dilation
    blk = span
    unit = blk * dilation
    length = -(-s // unit) * unit
    n_blk = length // unit
    pad = length - s

    def arrange(t):
        t = jnp.pad(t, ((0, 0), (0, pad), (0, 0), (0, 0)))
        t = t.reshape(bsz, length // dilation, dilation, h, dh)
        t = t.transpose(0, 2, 1, 3, 4)
        return t.reshape(bsz, dilation, n_blk, blk, h, dh)

    def with_prev(t):
        prev = jnp.pad(t, ((0, 0), (0, 0), (1, 0), (0, 0), (0, 0), (0, 0)))[:, :, :-1]
        return jnp.concatenate([prev, t], axis=3)

    qb = arrange(q)
    kw = with_prev(arrange(k))
    vw = with_prev(arrange(v))
    scores = jnp.einsum('brnqhd,brnkhd->brnhqk', qb, kw).astype(jnp.float32) * ATT_SCALE
    qi = jnp.arange(blk)[:, None]
    ki = jnp.arange(2 * blk)[None, :]
    steps = qi + blk - ki
    band = (steps >= 0) & (steps <= span)
    has_prev = (jnp.arange(n_blk) > 0)[:, None, None]
    valid = band[None] & (has_prev | (ki >= blk)[None])
    scores = jnp.where(valid[None, None, :, None], scores, NEG_INF)
    m = jnp.max(scores, axis=-1, keepdims=True)
    p = jnp.exp(scores - m)
    den = jnp.sum(p, axis=-1, keepdims=True)
    lse = (m + jnp.log(den))[..., 0]
    out = jnp.einsum('brnhqk,brnkhd->brnhqd', p, vw.astype(jnp.float32)) / den
    out = out.transpose(0, 1, 2, 4, 3, 5).reshape(bsz, dilation, length // dilation, h, dh)
    out = out.transpose(0, 2, 1, 3, 4).reshape(bsz, length, h, dh)[:, :s]
    lse = lse.transpose(0, 1, 2, 4, 3).reshape(bsz, dilation, length // dilation, h)
    lse = lse.transpose(0, 2, 1, 3).reshape(bsz, length, h)[:, :s]
    return out, lse


def memory_cross_attention(h, mem, w_xq, w_xkv, w_xo):
    bsz, s, _ = h.shape
    q = (h @ w_xq).reshape(bsz, s, XATT_HEADS, XATT_HEAD_DIM)
    kv = mem @ w_xkv
    k = kv[..., :D_MODEL].reshape(bsz, -1, XATT_HEADS, XATT_HEAD_DIM)
    v = kv[..., D_MODEL:].reshape(bsz, -1, XATT_HEADS, XATT_HEAD_DIM)
    scores = jnp.einsum('bshd,bmhd->bhsm', q, k).astype(jnp.float32) * XATT_SCALE
    p = jax.nn.softmax(scores, axis=-1)
    o = jnp.einsum('bhsm,bmhd->bshd', p, v.astype(jnp.float32)).astype(h.dtype)
    return o.reshape(bsz, s, D_MODEL) @ w_xo


def _fwd_setup_inputs(seed: int = 0) -> dict:
    key = jax.random.key(seed)
    ks = jax.random.split(key, 40)
    f32 = jnp.float32
    L, D, G, N, C = DEPTH, D_MODEL, SSM_GROUPS, SSM_STATE, SSM_GROUP

    def nrm(k, shape, scale):
        return jax.random.normal(k, shape, f32) * scale

    def gain(k, shape):
        return 1.0 + nrm(k, shape, 0.05)

    n_idx = jnp.arange(N, dtype=f32)
    inp = {
        'x': nrm(ks[0], (BATCH, SEQ, D), 1.0),
        'mem': nrm(ks[1], (BATCH, MEM_LEN, D), 1.0),
        'positions': jnp.broadcast_to(jnp.arange(SEQ, dtype=jnp.int32)[None, :], (BATCH, SEQ)),
        'ln_in_g': gain(ks[2], (D,)),
        'ln_in_b': nrm(ks[3], (D,), 0.02),
        'w_in': nrm(ks[4], (L, D, IN_COLS), D ** -0.5),
        'b_in': nrm(ks[5], (L, IN_COLS), 0.02),
        'ssm_log_dt': jax.random.uniform(ks[6], (L, G), f32, np.log(SSM_DT_MIN), np.log(SSM_DT_MAX)),
        'ssm_a_re': -0.5 + nrm(ks[7], (L, G, N), 0.01),
        'ssm_a_im': jnp.pi * n_idx + nrm(ks[8], (L, G, N), 0.01),
        'ssm_b_re': nrm(ks[9], (L, G, N, C), (0.5 / C) ** 0.5),
        'ssm_b_im': nrm(ks[10], (L, G, N, C), (0.5 / C) ** 0.5),
        'ssm_c_re': nrm(ks[11], (L, G, C, N), (0.5 / N) ** 0.5),
        'ssm_c_im': nrm(ks[12], (L, G, C, N), (0.5 / N) ** 0.5),
        'ssm_d': nrm(ks[13], (L, SSM_WIDTH), 1.0),
        'w_glu': nrm(ks[14], (L, SSM_WIDTH, 2 * D), SSM_WIDTH ** -0.5),
        'b_glu': nrm(ks[15], (L, 2 * D), 0.02),
        'w_att_up': nrm(ks[16], (L, ATT_MERGED, D), ATT_MERGED ** -0.5),
        'w_mix_out': nrm(ks[17], (L, D, D), DEEPNORM_BETA * D ** -0.5),
        'b_mix_out': nrm(ks[18], (L, D), 0.02),
        'ln1_g': gain(ks[19], (L, D)),
        'ln1_b': nrm(ks[20], (L, D), 0.02),
        'w_xq': nrm(ks[21], (L, D, D), D ** -0.5),
        'w_xkv': nrm(ks[22], (L, D, 2 * D), D ** -0.5),
        'w_xo': nrm(ks[23], (L, D, D), DEEPNORM_BETA * D ** -0.5),
        'ln2_g': gain(ks[24], (L, D)),
        'ln2_b': nrm(ks[25], (L, D), 0.02),
        'w_ff1': nrm(ks[26], (L, D, D_FF), D ** -0.5),
        'b_ff1': nrm(ks[27], (L, D_FF), 0.02),
        'w_ff2': nrm(ks[28], (L, D_FF, D), DEEPNORM_BETA * D_FF ** -0.5),
        'b_ff2': nrm(ks[29], (L, D), 0.02),
        'ln3_g': gain(ks[30], (L, D)),
        'ln3_b': nrm(ks[31], (L, D), 0.02),
    }
    return inp


def _fwd_reference(x, mem, positions, ln_in_g, ln_in_b, w_in, b_in, ssm_log_dt, ssm_a_re, ssm_a_im,
              ssm_b_re, ssm_b_im, ssm_c_re, ssm_c_im, ssm_d, w_glu, b_glu, w_att_up, w_mix_out,
              b_mix_out, ln1_g, ln1_b, w_xq, w_xkv, w_xo, ln2_g, ln2_b, w_ff1, b_ff1, w_ff2, b_ff2,
              ln3_g, ln3_b):
    bsz, s, _ = x.shape
    inv_freq = ROPE_THETA ** (-jnp.arange(0, ROT_DIM, 2, dtype=jnp.float32) / ROT_DIM)
    ang = positions.astype(jnp.float32)[..., None] * inv_freq
    cos, sin = jnp.cos(ang), jnp.sin(ang)

    h = layer_norm(x, ln_in_g, ln_in_b)
    for l in range(DEPTH):
        proj = h @ w_in[l] + b_in[l]
        u = proj[..., OFF_U:OFF_U + SSM_WIDTH]
        q = proj[..., OFF_Q:OFF_Q + ATT_WIDTH].reshape(bsz, s, ATT_HEADS, ATT_HEAD_DIM)
        k = proj[..., OFF_K:OFF_K + ATT_WIDTH].reshape(bsz, s, ATT_HEADS, ATT_HEAD_DIM)
        v = proj[..., OFF_V:OFF_V + ATT_WIDTH].reshape(bsz, s, ATT_HEADS, ATT_HEAD_DIM)
        g_ssm = proj[..., OFF_GS:OFF_GS + D_MODEL]
        g_att = proj[..., OFF_GA:OFF_GA + D_MODEL]

        y = s5_ssm(u, ssm_log_dt[l], ssm_a_re[l], ssm_a_im[l], ssm_b_re[l], ssm_b_im[l],
                   ssm_c_re[l], ssm_c_im[l], ssm_d[l])
        z = jax.nn.gelu(y) @ w_glu[l] + b_glu[l]
        b_ssm = z[..., :D_MODEL] * jax.nn.sigmoid(z[..., D_MODEL:])

        q = rope_partial(q, cos, sin)
        k = rope_partial(k, cos, sin)
        outs, lses = [], []
        for gi, (win, dil) in enumerate(DILATION_PATTERN):
            sl = slice(gi * ATT_HEADS_PER_GROUP, (gi + 1) * ATT_HEADS_PER_GROUP)
            o_g, lse_g = dilated_window_attention(q[:, :, sl], k[:, :, sl], v[:, :, sl], win, dil)
            outs.append(o_g)
            lses.append(lse_g)
        wts = jax.nn.softmax(jnp.stack(lses, axis=0), axis=0)
        att = jnp.einsum('gbsh,gbshd->bshd', wts, jnp.stack(outs, axis=0)).astype(h.dtype)
        b_att = att.reshape(bsz, s, ATT_MERGED) @ w_att_up[l]

        mixed = jax.nn.sigmoid(g_ssm) * b_ssm + jax.nn.sigmoid(g_att) * b_att
        h = layer_norm(DEEPNORM_ALPHA * h + (mixed @ w_mix_out[l] + b_mix_out[l]), ln1_g[l], ln1_b[l])

        xo = memory_cross_attention(h, mem, w_xq[l], w_xkv[l], w_xo[l])
        h = layer_norm(DEEPNORM_ALPHA * h + xo, ln2_g[l], ln2_b[l])

        ff = jnp.square(jax.nn.relu(h @ w_ff1[l] + b_ff1[l])) @ w_ff2[l] + b_ff2[l]
        h = layer_norm(DEEPNORM_ALPHA * h + ff, ln3_g[l], ln3_b[l])
    return h


import jax as _jax
import jax.numpy as _jnp

TWIN_FORMAT = 'train_step'
FWD_PARAMS = ['x', 'mem', 'positions', 'ln_in_g', 'ln_in_b', 'w_in', 'b_in', 'ssm_log_dt', 'ssm_a_re', 'ssm_a_im', 'ssm_b_re', 'ssm_b_im', 'ssm_c_re', 'ssm_c_im', 'ssm_d', 'w_glu', 'b_glu', 'w_att_up', 'w_mix_out', 'b_mix_out', 'ln1_g', 'ln1_b', 'w_xq', 'w_xkv', 'w_xo', 'ln2_g', 'ln2_b', 'w_ff1', 'b_ff1', 'w_ff2', 'b_ff2', 'ln3_g', 'ln3_b']
TWIN_WEIGHTS = ['ln_in_g', 'ln_in_b', 'w_in', 'b_in', 'ssm_log_dt', 'ssm_a_re', 'ssm_a_im', 'ssm_b_re', 'ssm_b_im', 'ssm_c_re', 'ssm_c_im', 'ssm_d', 'w_glu', 'b_glu', 'w_att_up', 'w_mix_out', 'b_mix_out', 'ln1_g', 'ln1_b', 'w_xq', 'w_xkv', 'w_xo', 'ln2_g', 'ln2_b', 'w_ff1', 'b_ff1', 'w_ff2', 'b_ff2', 'ln3_g', 'ln3_b']
TWIN_DIFF_INPUT = 'x'
TWIN_INPUTS = ['x', 'mem', 'positions', 'ln_in_g', 'ln_in_b', 'w_in', 'b_in', 'ssm_log_dt', 'ssm_a_re', 'ssm_a_im', 'ssm_b_re', 'ssm_b_im', 'ssm_c_re', 'ssm_c_im', 'ssm_d', 'w_glu', 'b_glu', 'w_att_up', 'w_mix_out', 'b_mix_out', 'ln1_g', 'ln1_b', 'w_xq', 'w_xkv', 'w_xo', 'ln2_g', 'ln2_b', 'w_ff1', 'b_ff1', 'w_ff2', 'b_ff2', 'ln3_g', 'ln3_b', 'loss_target', 'm_ln_in_g', 'm_ln_in_b', 'm_w_in', 'm_b_in', 'm_ssm_log_dt', 'm_ssm_a_re', 'm_ssm_a_im', 'm_ssm_b_re', 'm_ssm_b_im', 'm_ssm_c_re', 'm_ssm_c_im', 'm_ssm_d', 'm_w_glu', 'm_b_glu', 'm_w_att_up', 'm_w_mix_out', 'm_b_mix_out', 'm_ln1_g', 'm_ln1_b', 'm_w_xq', 'm_w_xkv', 'm_w_xo', 'm_ln2_g', 'm_ln2_b', 'm_w_ff1', 'm_b_ff1', 'm_w_ff2', 'm_b_ff2', 'm_ln3_g', 'm_ln3_b', 'v_ln_in_g', 'v_ln_in_b', 'v_w_in', 'v_b_in', 'v_ssm_log_dt', 'v_ssm_a_re', 'v_ssm_a_im', 'v_ssm_b_re', 'v_ssm_b_im', 'v_ssm_c_re', 'v_ssm_c_im', 'v_ssm_d', 'v_w_glu', 'v_b_glu', 'v_w_att_up', 'v_w_mix_out', 'v_b_mix_out', 'v_ln1_g', 'v_ln1_b', 'v_w_xq', 'v_w_xkv', 'v_w_xo', 'v_ln2_g', 'v_ln2_b', 'v_w_ff1', 'v_b_ff1', 'v_w_ff2', 'v_b_ff2', 'v_ln3_g', 'v_ln3_b']
TWIN_OUTPUTS = ['loss', 'grad_x', 'grad_ln_in_g', 'grad_ln_in_b', 'grad_w_in', 'grad_b_in', 'grad_ssm_log_dt', 'grad_ssm_a_re', 'grad_ssm_a_im', 'grad_ssm_b_re', 'grad_ssm_b_im', 'grad_ssm_c_re', 'grad_ssm_c_im', 'grad_ssm_d', 'grad_w_glu', 'grad_b_glu', 'grad_w_att_up', 'grad_w_mix_out', 'grad_b_mix_out', 'grad_ln1_g', 'grad_ln1_b', 'grad_w_xq', 'grad_w_xkv', 'grad_w_xo', 'grad_ln2_g', 'grad_ln2_b', 'grad_w_ff1', 'grad_b_ff1', 'grad_w_ff2', 'grad_b_ff2', 'grad_ln3_g', 'grad_ln3_b', 'delta_ln_in_g', 'delta_ln_in_b', 'delta_w_in', 'delta_b_in', 'delta_ssm_log_dt', 'delta_ssm_a_re', 'delta_ssm_a_im', 'delta_ssm_b_re', 'delta_ssm_b_im', 'delta_ssm_c_re', 'delta_ssm_c_im', 'delta_ssm_d', 'delta_w_glu', 'delta_b_glu', 'delta_w_att_up', 'delta_w_mix_out', 'delta_b_mix_out', 'delta_ln1_g', 'delta_ln1_b', 'delta_w_xq', 'delta_w_xkv', 'delta_w_xo', 'delta_ln2_g', 'delta_ln2_b', 'delta_w_ff1', 'delta_b_ff1', 'delta_w_ff2', 'delta_b_ff2', 'delta_ln3_g', 'delta_ln3_b', 'new_m_ln_in_g', 'new_m_ln_in_b', 'new_m_w_in', 'new_m_b_in', 'new_m_ssm_log_dt', 'new_m_ssm_a_re', 'new_m_ssm_a_im', 'new_m_ssm_b_re', 'new_m_ssm_b_im', 'new_m_ssm_c_re', 'new_m_ssm_c_im', 'new_m_ssm_d', 'new_m_w_glu', 'new_m_b_glu', 'new_m_w_att_up', 'new_m_w_mix_out', 'new_m_b_mix_out', 'new_m_ln1_g', 'new_m_ln1_b', 'new_m_w_xq', 'new_m_w_xkv', 'new_m_w_xo', 'new_m_ln2_g', 'new_m_ln2_b', 'new_m_w_ff1', 'new_m_b_ff1', 'new_m_w_ff2', 'new_m_b_ff2', 'new_m_ln3_g', 'new_m_ln3_b', 'new_v_ln_in_g', 'new_v_ln_in_b', 'new_v_w_in', 'new_v_b_in', 'new_v_ssm_log_dt', 'new_v_ssm_a_re', 'new_v_ssm_a_im', 'new_v_ssm_b_re', 'new_v_ssm_b_im', 'new_v_ssm_c_re', 'new_v_ssm_c_im', 'new_v_ssm_d', 'new_v_w_glu', 'new_v_b_glu', 'new_v_w_att_up', 'new_v_w_mix_out', 'new_v_b_mix_out', 'new_v_ln1_g', 'new_v_ln1_b', 'new_v_w_xq', 'new_v_w_xkv', 'new_v_w_xo', 'new_v_ln2_g', 'new_v_ln2_b', 'new_v_w_ff1', 'new_v_b_ff1', 'new_v_w_ff2', 'new_v_b_ff2', 'new_v_ln3_g', 'new_v_ln3_b']
TWIN_LEAF_KINDS = {'loss': 'loss', 'grad_x': 'grad_x', 'grad_ln_in_g': 'grad_w', 'grad_ln_in_b': 'grad_w', 'grad_w_in': 'grad_w', 'grad_b_in': 'grad_w', 'grad_ssm_log_dt': 'grad_w', 'grad_ssm_a_re': 'grad_w', 'grad_ssm_a_im': 'grad_w', 'grad_ssm_b_re': 'grad_w', 'grad_ssm_b_im': 'grad_w', 'grad_ssm_c_re': 'grad_w', 'grad_ssm_c_im': 'grad_w', 'grad_ssm_d': 'grad_w', 'grad_w_glu': 'grad_w', 'grad_b_glu': 'grad_w', 'grad_w_att_up': 'grad_w', 'grad_w_mix_out': 'grad_w', 'grad_b_mix_out': 'grad_w', 'grad_ln1_g': 'grad_w', 'grad_ln1_b': 'grad_w', 'grad_w_xq': 'grad_w', 'grad_w_xkv': 'grad_w', 'grad_w_xo': 'grad_w', 'grad_ln2_g': 'grad_w', 'grad_ln2_b': 'grad_w', 'grad_w_ff1': 'grad_w', 'grad_b_ff1': 'grad_w', 'grad_w_ff2': 'grad_w', 'grad_b_ff2': 'grad_w', 'grad_ln3_g': 'grad_w', 'grad_ln3_b': 'grad_w', 'delta_ln_in_g': 'delta_w', 'delta_ln_in_b': 'delta_w', 'delta_w_in': 'delta_w', 'delta_b_in': 'delta_w', 'delta_ssm_log_dt': 'delta_w', 'delta_ssm_a_re': 'delta_w', 'delta_ssm_a_im': 'delta_w', 'delta_ssm_b_re': 'delta_w', 'delta_ssm_b_im': 'delta_w', 'delta_ssm_c_re': 'delta_w', 'delta_ssm_c_im': 'delta_w', 'delta_ssm_d': 'delta_w', 'delta_w_glu': 'delta_w', 'delta_b_glu': 'delta_w', 'delta_w_att_up': 'delta_w', 'delta_w_mix_out': 'delta_w', 'delta_b_mix_out': 'delta_w', 'delta_ln1_g': 'delta_w', 'delta_ln1_b': 'delta_w', 'delta_w_xq': 'delta_w', 'delta_w_xkv': 'delta_w', 'delta_w_xo': 'delta_w', 'delta_ln2_g': 'delta_w', 'delta_ln2_b': 'delta_w', 'delta_w_ff1': 'delta_w', 'delta_b_ff1': 'delta_w', 'delta_w_ff2': 'delta_w', 'delta_b_ff2': 'delta_w', 'delta_ln3_g': 'delta_w', 'delta_ln3_b': 'delta_w', 'new_m_ln_in_g': 'new_m', 'new_m_ln_in_b': 'new_m', 'new_m_w_in': 'new_m', 'new_m_b_in': 'new_m', 'new_m_ssm_log_dt': 'new_m', 'new_m_ssm_a_re': 'new_m', 'new_m_ssm_a_im': 'new_m', 'new_m_ssm_b_re': 'new_m', 'new_m_ssm_b_im': 'new_m', 'new_m_ssm_c_re': 'new_m', 'new_m_ssm_c_im': 'new_m', 'new_m_ssm_d': 'new_m', 'new_m_w_glu': 'new_m', 'new_m_b_glu': 'new_m', 'new_m_w_att_up': 'new_m', 'new_m_w_mix_out': 'new_m', 'new_m_b_mix_out': 'new_m', 'new_m_ln1_g': 'new_m', 'new_m_ln1_b': 'new_m', 'new_m_w_xq': 'new_m', 'new_m_w_xkv': 'new_m', 'new_m_w_xo': 'new_m', 'new_m_ln2_g': 'new_m', 'new_m_ln2_b': 'new_m', 'new_m_w_ff1': 'new_m', 'new_m_b_ff1': 'new_m', 'new_m_w_ff2': 'new_m', 'new_m_b_ff2': 'new_m', 'new_m_ln3_g': 'new_m', 'new_m_ln3_b': 'new_m', 'new_v_ln_in_g': 'new_v', 'new_v_ln_in_b': 'new_v', 'new_v_w_in': 'new_v', 'new_v_b_in': 'new_v', 'new_v_ssm_log_dt': 'new_v', 'new_v_ssm_a_re': 'new_v', 'new_v_ssm_a_im': 'new_v', 'new_v_ssm_b_re': 'new_v', 'new_v_ssm_b_im': 'new_v', 'new_v_ssm_c_re': 'new_v', 'new_v_ssm_c_im': 'new_v', 'new_v_ssm_d': 'new_v', 'new_v_w_glu': 'new_v', 'new_v_b_glu': 'new_v', 'new_v_w_att_up': 'new_v', 'new_v_w_mix_out': 'new_v', 'new_v_b_mix_out': 'new_v', 'new_v_ln1_g': 'new_v', 'new_v_ln1_b': 'new_v', 'new_v_w_xq': 'new_v', 'new_v_w_xkv': 'new_v', 'new_v_w_xo': 'new_v', 'new_v_ln2_g': 'new_v', 'new_v_ln2_b': 'new_v', 'new_v_w_ff1': 'new_v', 'new_v_b_ff1': 'new_v', 'new_v_w_ff2': 'new_v', 'new_v_b_ff2': 'new_v', 'new_v_ln3_g': 'new_v', 'new_v_ln3_b': 'new_v'}


def _forward(args):
    return _fwd_reference(*[args[k] for k in FWD_PARAMS])


def _output_shape():
    def fwd():
        inp = _fwd_setup_inputs(0)
        return _fwd_reference(*[inp[k] for k in FWD_PARAMS])
    out = _jax.eval_shape(fwd)
    return out.shape, out.dtype

N_MICROBATCH = 1
ADAM_LR = 0.001
ADAM_B1 = 0.9
ADAM_B2 = 0.999
ADAM_EPS = 1e-08
ADAM_WD = 0.01
ADAM_STEP = 10
PER_EXAMPLE_BATCH_AXIS = {'x': 0, 'mem': 0, 'positions': 0, 'loss_target': 0}
SHARED_INPUTS = []
_WEIGHT_DTYPES = {'ln_in_g': _jnp.float32, 'ln_in_b': _jnp.float32, 'w_in': _jnp.float32, 'b_in': _jnp.float32, 'ssm_log_dt': _jnp.float32, 'ssm_a_re': _jnp.float32, 'ssm_a_im': _jnp.float32, 'ssm_b_re': _jnp.float32, 'ssm_b_im': _jnp.float32, 'ssm_c_re': _jnp.float32, 'ssm_c_im': _jnp.float32, 'ssm_d': _jnp.float32, 'w_glu': _jnp.float32, 'b_glu': _jnp.float32, 'w_att_up': _jnp.float32, 'w_mix_out': _jnp.float32, 'b_mix_out': _jnp.float32, 'ln1_g': _jnp.float32, 'ln1_b': _jnp.float32, 'w_xq': _jnp.float32, 'w_xkv': _jnp.float32, 'w_xo': _jnp.float32, 'ln2_g': _jnp.float32, 'ln2_b': _jnp.float32, 'w_ff1': _jnp.float32, 'b_ff1': _jnp.float32, 'w_ff2': _jnp.float32, 'b_ff2': _jnp.float32, 'ln3_g': _jnp.float32, 'ln3_b': _jnp.float32}
MOMENT_SCALE = {'ln_in_g': 2.240923e+00, 'ln_in_b': 8.129474e-01, 'w_in': 1.157994e-02, 'b_in': 5.965593e-02, 'ssm_log_dt': 7.660778e-01, 'ssm_a_re': 1.072408e-03, 'ssm_a_im': 1.094183e-03, 'ssm_b_re': 7.070231e-04, 'ssm_b_im': 7.008867e-04, 'ssm_c_re': 1.415206e-03, 'ssm_c_im': 1.440687e-03, 'ssm_d': 4.345203e-02, 'w_glu': 2.382699e-02, 'b_glu': 7.138452e-02, 'w_att_up': 9.604869e-03, 'w_mix_out': 5.426625e-02, 'b_mix_out': 6.713395e-01, 'ln1_g': 2.266855e+00, 'ln1_b': 8.030343e-01, 'w_xq': 9.727851e-03, 'w_xkv': 1.076926e-02, 'w_xo': 1.982527e-02, 'ln2_g': 2.281509e+00, 'ln2_b': 8.100767e-01, 'w_ff1': 5.513646e-02, 'b_ff1': 1.447788e-01, 'w_ff2': 3.108715e-01, 'b_ff2': 6.348598e-01, 'ln3_g': 3.232624e+01, 'ln3_b': 7.262465e+00}


def _to_microbatches(a, axis):
    t = _jnp.moveaxis(a, axis, 0)
    t = t.reshape((N_MICROBATCH, t.shape[0] // N_MICROBATCH) + t.shape[1:])
    return _jnp.moveaxis(t, 1, axis + 1)


def setup_inputs(seed: int = 0) -> dict:
    inp = _fwd_setup_inputs(seed)
    key = _jax.random.fold_in(_jax.random.key(seed), 7919)
    shape, _ = _output_shape()
    out = dict(inp)
    out["loss_target"] = _jax.random.normal(_jax.random.fold_in(key, 0), shape, _jnp.float32)
    for i, name in enumerate(TWIN_WEIGHTS):
        w = inp[name].astype(_jnp.float32)
        if MOMENT_SCALE is None:
            s = _jnp.sqrt(_jnp.mean(_jnp.square(w)) + 1e-30)
        else:
            s = MOMENT_SCALE[name]
        km, kv = _jax.random.split(_jax.random.fold_in(key, i + 1))
        out[name] = w
        out["m_" + name] = s * _jax.random.normal(km, w.shape, _jnp.float32)
        out["v_" + name] = (s * s) * _jax.random.uniform(kv, w.shape, _jnp.float32, 0.5, 1.5)
    if N_MICROBATCH > 1:
        for name, axis in PER_EXAMPLE_BATCH_AXIS.items():
            out[name] = _to_microbatches(out[name], axis)
    return {'x': out['x'], 'mem': out['mem'], 'positions': out['positions'], 'ln_in_g': out['ln_in_g'], 'ln_in_b': out['ln_in_b'], 'w_in': out['w_in'], 'b_in': out['b_in'], 'ssm_log_dt': out['ssm_log_dt'], 'ssm_a_re': out['ssm_a_re'], 'ssm_a_im': out['ssm_a_im'], 'ssm_b_re': out['ssm_b_re'], 'ssm_b_im': out['ssm_b_im'], 'ssm_c_re': out['ssm_c_re'], 'ssm_c_im': out['ssm_c_im'], 'ssm_d': out['ssm_d'], 'w_glu': out['w_glu'], 'b_glu': out['b_glu'], 'w_att_up': out['w_att_up'], 'w_mix_out': out['w_mix_out'], 'b_mix_out': out['b_mix_out'], 'ln1_g': out['ln1_g'], 'ln1_b': out['ln1_b'], 'w_xq': out['w_xq'], 'w_xkv': out['w_xkv'], 'w_xo': out['w_xo'], 'ln2_g': out['ln2_g'], 'ln2_b': out['ln2_b'], 'w_ff1': out['w_ff1'], 'b_ff1': out['b_ff1'], 'w_ff2': out['w_ff2'], 'b_ff2': out['b_ff2'], 'ln3_g': out['ln3_g'], 'ln3_b': out['ln3_b'], 'loss_target': out['loss_target'], 'm_ln_in_g': out['m_ln_in_g'], 'm_ln_in_b': out['m_ln_in_b'], 'm_w_in': out['m_w_in'], 'm_b_in': out['m_b_in'], 'm_ssm_log_dt': out['m_ssm_log_dt'], 'm_ssm_a_re': out['m_ssm_a_re'], 'm_ssm_a_im': out['m_ssm_a_im'], 'm_ssm_b_re': out['m_ssm_b_re'], 'm_ssm_b_im': out['m_ssm_b_im'], 'm_ssm_c_re': out['m_ssm_c_re'], 'm_ssm_c_im': out['m_ssm_c_im'], 'm_ssm_d': out['m_ssm_d'], 'm_w_glu': out['m_w_glu'], 'm_b_glu': out['m_b_glu'], 'm_w_att_up': out['m_w_att_up'], 'm_w_mix_out': out['m_w_mix_out'], 'm_b_mix_out': out['m_b_mix_out'], 'm_ln1_g': out['m_ln1_g'], 'm_ln1_b': out['m_ln1_b'], 'm_w_xq': out['m_w_xq'], 'm_w_xkv': out['m_w_xkv'], 'm_w_xo': out['m_w_xo'], 'm_ln2_g': out['m_ln2_g'], 'm_ln2_b': out['m_ln2_b'], 'm_w_ff1': out['m_w_ff1'], 'm_b_ff1': out['m_b_ff1'], 'm_w_ff2': out['m_w_ff2'], 'm_b_ff2': out['m_b_ff2'], 'm_ln3_g': out['m_ln3_g'], 'm_ln3_b': out['m_ln3_b'], 'v_ln_in_g': out['v_ln_in_g'], 'v_ln_in_b': out['v_ln_in_b'], 'v_w_in': out['v_w_in'], 'v_b_in': out['v_b_in'], 'v_ssm_log_dt': out['v_ssm_log_dt'], 'v_ssm_a_re': out['v_ssm_a_re'], 'v_ssm_a_im': out['v_ssm_a_im'], 'v_ssm_b_re': out['v_ssm_b_re'], 'v_ssm_b_im': out['v_ssm_b_im'], 'v_ssm_c_re': out['v_ssm_c_re'], 'v_ssm_c_im': out['v_ssm_c_im'], 'v_ssm_d': out['v_ssm_d'], 'v_w_glu': out['v_w_glu'], 'v_b_glu': out['v_b_glu'], 'v_w_att_up': out['v_w_att_up'], 'v_w_mix_out': out['v_w_mix_out'], 'v_b_mix_out': out['v_b_mix_out'], 'v_ln1_g': out['v_ln1_g'], 'v_ln1_b': out['v_ln1_b'], 'v_w_xq': out['v_w_xq'], 'v_w_xkv': out['v_w_xkv'], 'v_w_xo': out['v_w_xo'], 'v_ln2_g': out['v_ln2_g'], 'v_ln2_b': out['v_ln2_b'], 'v_w_ff1': out['v_w_ff1'], 'v_b_ff1': out['v_b_ff1'], 'v_w_ff2': out['v_w_ff2'], 'v_b_ff2': out['v_b_ff2'], 'v_ln3_g': out['v_ln3_g'], 'v_ln3_b': out['v_ln3_b']}


def _loss(weights, diff, rest, loss_target):
    with _jax.named_scope("forward"):
        args = {**rest, TWIN_DIFF_INPUT: diff, **{k: w.astype(_WEIGHT_DTYPES[k]) for k, w in weights.items()}}
        y = _forward(args)
    with _jax.named_scope("loss_head"):
        err = _jnp.square(y.astype(_jnp.float32) - loss_target)
        return 0.5 * _jnp.sum(_jnp.mean(err, axis=-1)) if err.ndim else 0.5 * err


def _adamw(w, g, m, v):
    m = ADAM_B1 * m + (1.0 - ADAM_B1) * g
    v = ADAM_B2 * v + (1.0 - ADAM_B2) * _jnp.square(g)
    m_hat = m / (1.0 - ADAM_B1 ** ADAM_STEP)
    v_hat = v / (1.0 - ADAM_B2 ** ADAM_STEP)
    delta = -ADAM_LR * (m_hat / (_jnp.sqrt(v_hat) + ADAM_EPS) + ADAM_WD * w)
    return delta, m, v


def reference(x, mem, positions, ln_in_g, ln_in_b, w_in, b_in, ssm_log_dt, ssm_a_re, ssm_a_im, ssm_b_re, ssm_b_im, ssm_c_re, ssm_c_im, ssm_d, w_glu, b_glu, w_att_up, w_mix_out, b_mix_out, ln1_g, ln1_b, w_xq, w_xkv, w_xo, ln2_g, ln2_b, w_ff1, b_ff1, w_ff2, b_ff2, ln3_g, ln3_b, loss_target, m_ln_in_g, m_ln_in_b, m_w_in, m_b_in, m_ssm_log_dt, m_ssm_a_re, m_ssm_a_im, m_ssm_b_re, m_ssm_b_im, m_ssm_c_re, m_ssm_c_im, m_ssm_d, m_w_glu, m_b_glu, m_w_att_up, m_w_mix_out, m_b_mix_out, m_ln1_g, m_ln1_b, m_w_xq, m_w_xkv, m_w_xo, m_ln2_g, m_ln2_b, m_w_ff1, m_b_ff1, m_w_ff2, m_b_ff2, m_ln3_g, m_ln3_b, v_ln_in_g, v_ln_in_b, v_w_in, v_b_in, v_ssm_log_dt, v_ssm_a_re, v_ssm_a_im, v_ssm_b_re, v_ssm_b_im, v_ssm_c_re, v_ssm_c_im, v_ssm_d, v_w_glu, v_b_glu, v_w_att_up, v_w_mix_out, v_b_mix_out, v_ln1_g, v_ln1_b, v_w_xq, v_w_xkv, v_w_xo, v_ln2_g, v_ln2_b, v_w_ff1, v_b_ff1, v_w_ff2, v_b_ff2, v_ln3_g, v_ln3_b):
    given = dict(x=x, mem=mem, positions=positions, ln_in_g=ln_in_g, ln_in_b=ln_in_b, w_in=w_in, b_in=b_in, ssm_log_dt=ssm_log_dt, ssm_a_re=ssm_a_re, ssm_a_im=ssm_a_im, ssm_b_re=ssm_b_re, ssm_b_im=ssm_b_im, ssm_c_re=ssm_c_re, ssm_c_im=ssm_c_im, ssm_d=ssm_d, w_glu=w_glu, b_glu=b_glu, w_att_up=w_att_up, w_mix_out=w_mix_out, b_mix_out=b_mix_out, ln1_g=ln1_g, ln1_b=ln1_b, w_xq=w_xq, w_xkv=w_xkv, w_xo=w_xo, ln2_g=ln2_g, ln2_b=ln2_b, w_ff1=w_ff1, b_ff1=b_ff1, w_ff2=w_ff2, b_ff2=b_ff2, ln3_g=ln3_g, ln3_b=ln3_b, loss_target=loss_target, m_ln_in_g=m_ln_in_g, m_ln_in_b=m_ln_in_b, m_w_in=m_w_in, m_b_in=m_b_in, m_ssm_log_dt=m_ssm_log_dt, m_ssm_a_re=m_ssm_a_re, m_ssm_a_im=m_ssm_a_im, m_ssm_b_re=m_ssm_b_re, m_ssm_b_im=m_ssm_b_im, m_ssm_c_re=m_ssm_c_re, m_ssm_c_im=m_ssm_c_im, m_ssm_d=m_ssm_d, m_w_glu=m_w_glu, m_b_glu=m_b_glu, m_w_att_up=m_w_att_up, m_w_mix_out=m_w_mix_out, m_b_mix_out=m_b_mix_out, m_ln1_g=m_ln1_g, m_ln1_b=m_ln1_b, m_w_xq=m_w_xq, m_w_xkv=m_w_xkv, m_w_xo=m_w_xo, m_ln2_g=m_ln2_g, m_ln2_b=m_ln2_b, m_w_ff1=m_w_ff1, m_b_ff1=m_b_ff1, m_w_ff2=m_w_ff2, m_b_ff2=m_b_ff2, m_ln3_g=m_ln3_g, m_ln3_b=m_ln3_b, v_ln_in_g=v_ln_in_g, v_ln_in_b=v_ln_in_b, v_w_in=v_w_in, v_b_in=v_b_in, v_ssm_log_dt=v_ssm_log_dt, v_ssm_a_re=v_ssm_a_re, v_ssm_a_im=v_ssm_a_im, v_ssm_b_re=v_ssm_b_re, v_ssm_b_im=v_ssm_b_im, v_ssm_c_re=v_ssm_c_re, v_ssm_c_im=v_ssm_c_im, v_ssm_d=v_ssm_d, v_w_glu=v_w_glu, v_b_glu=v_b_glu, v_w_att_up=v_w_att_up, v_w_mix_out=v_w_mix_out, v_b_mix_out=v_b_mix_out, v_ln1_g=v_ln1_g, v_ln1_b=v_ln1_b, v_w_xq=v_w_xq, v_w_xkv=v_w_xkv, v_w_xo=v_w_xo, v_ln2_g=v_ln2_g, v_ln2_b=v_ln2_b, v_w_ff1=v_w_ff1, v_b_ff1=v_b_ff1, v_w_ff2=v_w_ff2, v_b_ff2=v_b_ff2, v_ln3_g=v_ln3_g, v_ln3_b=v_ln3_b)
    weights = {n: given[n] for n in TWIN_WEIGHTS}
    shared = {n: given[n] for n in SHARED_INPUTS}
    per_example = {n: given[n] for n in ['x', 'mem', 'positions']}
    grad_fn = _jax.value_and_grad(_loss, argnums=(0, 1))

    def one_microbatch(ex, loss_target):
        ex = dict(ex)
        diff = ex.pop(TWIN_DIFF_INPUT)
        return grad_fn(weights, diff, {**shared, **ex}, loss_target)

    if N_MICROBATCH == 1:
        loss, (grad_w, grad_x) = one_microbatch(per_example, given["loss_target"])
    else:
        def body(carry, xs):
            loss_sum, grad_sum = carry
            l_k, (gw_k, gx_k) = one_microbatch(xs[0], xs[1])
            with _jax.named_scope("update"):
                return (loss_sum + l_k, _jax.tree.map(_jnp.add, grad_sum, gw_k)), gx_k

        init = (_jnp.zeros((), _jnp.float32), _jax.tree.map(_jnp.zeros_like, weights))
        (loss, grad_w), grad_x = _jax.lax.scan(body, init, (per_example, given["loss_target"]))
    with _jax.named_scope("update"):
        delta_w, new_m, new_v = {}, {}, {}
        for n in TWIN_WEIGHTS:
            delta_w[n], new_m[n], new_v[n] = _adamw(weights[n], grad_w[n], given["m_" + n], given["v_" + n])
    return (loss, grad_x, *[grad_w[n] for n in TWIN_WEIGHTS], *[delta_w[n] for n in TWIN_WEIGHTS],
            *[new_m[n] for n in TWIN_WEIGHTS], *[new_v[n] for n in TWIN_WEIGHTS])
```

```python
import functools
import math

import numpy as np
import jax
import jax.numpy as jnp
from jax import lax
from jax.experimental import pallas as pl
from jax.experimental.pallas import tpu as pltpu

F32 = jnp.float32
BF16 = jnp.bfloat16
MXU_DTYPE = jnp.bfloat16

D_MODEL = 1024
SSM_GROUP = 16
SSM_WIDTH = 768
SSM_GROUPS = 48
SSM_STATE = 64
N_STATE = SSM_GROUPS * SSM_STATE
SSM_CHUNKS = 6
CH_W = 128
CH_N = 512
ATT_HEAD_DIM = 64
ATT_HPG = 4
ATT_GROUPW = ATT_HPG * ATT_HEAD_DIM
DILATIONS = (1, 4, 16)
ATT_BLK = 128
ATT_SCALE = ATT_HEAD_DIM ** -0.5
ROT_DIM = 16
ROPE_THETA = 500000.0
XATT_HEADS = 4
XATT_HEAD_DIM = 256
XATT_SCALE = XATT_HEAD_DIM ** -0.5
D_FF = 4096
IN_COLS = 5120
DEEPNORM_ALPHA = 2.0 ** 0.25
LN_EPS = 1e-5
NEG_INF = -1e30
ADAM_LR = 0.001
ADAM_B1 = 0.9
ADAM_B2 = 0.999
ADAM_EPS = 1e-08
ADAM_WD = 0.01
ADAM_STEP = 10

N_SEG = 32
VMEM_LIMIT = 48 * 1024 * 1024
MESH = pl.DeviceIdType.MESH
HBM_SPEC = pl.BlockSpec(memory_space=pltpu.HBM)
VMEM_SPEC = pl.BlockSpec(memory_space=pltpu.VMEM)

BIG = (("w_in", 1024, 5120, 1), ("w_glu", 768, 2048, 1), ("w_att_up", 256, 1024, 1),
       ("w_mix_out", 1024, 1024, 0), ("w_xq", 1024, 1024, 0), ("w_xkv", 1024, 2048, 1),
       ("w_xo", 1024, 1024, 0), ("w_ff1", 1024, 4096, 1), ("w_ff2", 4096, 1024, 0))
SMALL = ("ln_in_g", "ln_in_b", "b_in", "ssm_log_dt", "ssm_a_re", "ssm_a_im", "ssm_b_re", "ssm_b_im",
         "ssm_c_re", "ssm_c_im", "ssm_d", "b_glu", "b_mix_out", "ln1_g", "ln1_b", "ln2_g", "ln2_b",
         "b_ff1", "b_ff2", "ln3_g", "ln3_b")
WEIGHT_ORDER = ("ln_in_g", "ln_in_b", "w_in", "b_in", "ssm_log_dt", "ssm_a_re", "ssm_a_im", "ssm_b_re",
                "ssm_b_im", "ssm_c_re", "ssm_c_im", "ssm_d", "w_glu", "b_glu", "w_att_up", "w_mix_out",
                "b_mix_out", "ln1_g", "ln1_b", "w_xq", "w_xkv", "w_xo", "ln2_g", "ln2_b", "w_ff1", "b_ff1",
                "w_ff2", "b_ff2", "ln3_g", "ln3_b")


def _cparams(n_axes):
    return pltpu.CompilerParams(dimension_semantics=("arbitrary",) * n_axes, vmem_limit_bytes=VMEM_LIMIT)


def _rowwise(name, fn, rows, consts, outs, reds=(), tm=256):
    n_rows = (rows[0][0] if isinstance(rows[0], tuple) else rows[0]).shape[-2]
    tm = min(tm, n_rows)
    assert n_rows % tm == 0, (name, n_rows, tm)
    specs, args = [], []
    for r in rows:
        if isinstance(r, tuple) and len(r) == 3:
            arr, width, cb = r
            specs.append(pl.BlockSpec((tm, width), functools.partial(lambda i, cb: (i, cb), cb=cb)))
        elif isinstance(r, tuple):
            arr, slot = r
            specs.append(pl.BlockSpec((None, tm, arr.shape[2]), functools.partial(lambda i, s: (s, i, 0), s=slot)))
        else:
            arr = r
            specs.append(pl.BlockSpec((tm, arr.shape[1]), lambda i: (i, 0)))
        args.append(arr)
        assert arr.shape[-2] == n_rows, (name, arr.shape, n_rows)
    for cst in consts:
        specs.append(pl.BlockSpec(cst.shape, lambda i: (0, 0)))
        args.append(cst)
    n_r, n_c, n_o, n_d = len(rows), len(consts), len(outs), len(reds)
    out_shape = [jax.ShapeDtypeStruct((n_rows, c), dt) for c, dt in outs]
    out_specs = [pl.BlockSpec((tm, c), lambda i: (i, 0)) for c, _ in outs]
    out_shape += [jax.ShapeDtypeStruct((1, c), F32) for c in reds]
    out_specs += [pl.BlockSpec((1, c), lambda i: (0, 0)) for c in reds]

    def body(*refs):
        ins = [r[...] for r in refs[:n_r + n_c]]
        o_refs = refs[n_r + n_c:n_r + n_c + n_o]
        d_refs = refs[n_r + n_c + n_o:]
        res = fn(*ins)
        res = res if isinstance(res, (tuple, list)) else (res,)
        assert len(res) == n_o + n_d, (name, len(res))
        for ref, val in zip(o_refs, res[:n_o]):
            ref[...] = val.astype(ref.dtype)
        first = pl.program_id(0) == 0
        for ref, val in zip(d_refs, res[n_o:]):
            @pl.when(first)
            def _(ref=ref, val=val):
                ref[...] = val

            @pl.when(jnp.logical_not(first))
            def _(ref=ref, val=val):
                ref[...] += val

    res = pl.pallas_call(body, name=name, grid=(n_rows // tm,), in_specs=specs, out_specs=out_specs,
                         out_shape=out_shape, compiler_params=_cparams(1))(*args)
    return res


def _colsum(v):
    return jnp.sum(v.astype(F32), axis=0, keepdims=True)


_DIMS = {"nn": (((1,), (0,)), ((), ())), "nt": (((1,), (1,)), ((), ())), "tn": (((0,), (0,)), ((), ()))}


def _tile(dim, want):
    if dim <= want:
        return dim
    return max(t for t in range(128, want + 1, 128) if dim % t == 0)


def _dot(a, b, mode):
    return lax.dot_general(a.astype(MXU_DTYPE), b.astype(MXU_DTYPE), _DIMS[mode], preferred_element_type=F32)


def _mm(name, a, b, mode, *, bias=None, extras=(), epilogue=None, out_dtypes=(F32,), tm=512, tn=512, tk=512):
    if mode == "nn":
        (m, k), (_, n) = a.shape, b.shape
    elif mode == "nt":
        (m, k), (n, _) = a.shape, b.shape
    else:
        (k, m), (_, n) = a.shape, b.shape
    tm, tn = _tile(m, tm), _tile(n, tn)
    if mode != "tn":
        tk = k if k <= 1024 else tk
    tk = _tile(k, tk)
    assert m % tm == 0 and n % tn == 0 and k % tk == 0, (name, m, n, k)
    nk = k // tk
    a_spec = {"nn": pl.BlockSpec((tm, tk), lambda i, j, kk: (i, kk)),
              "nt": pl.BlockSpec((tm, tk), lambda i, j, kk: (i, kk)),
              "tn": pl.BlockSpec((tk, tm), lambda i, j, kk: (kk, i))}[mode]
    b_spec = {"nn": pl.BlockSpec((tk, tn), lambda i, j, kk: (kk, j)),
              "nt": pl.BlockSpec((tn, tk), lambda i, j, kk: (j, kk)),
              "tn": pl.BlockSpec((tk, tn), lambda i, j, kk: (kk, j))}[mode]
    specs, args = [a_spec, b_spec], [a, b]
    if bias is not None:
        specs.append(pl.BlockSpec((1, tn), lambda i, j, kk: (0, j)))
        args.append(bias)
    for e in extras:
        specs.append(pl.BlockSpec((tm, tn), lambda i, j, kk: (i, j)))
        args.append(e)
    n_e, n_o = len(extras), len(out_dtypes)
    has_bias = bias is not None

    def body(*refs):
        a_ref, b_ref = refs[0], refs[1]
        pos = 2
        bias_ref = refs[pos] if has_bias else None
        pos += int(has_bias)
        e_refs = refs[pos:pos + n_e]
        o_refs = refs[pos + n_e:pos + n_e + n_o]
        acc_ref = refs[pos + n_e + n_o] if nk > 1 else None
        part = _dot(a_ref[...], b_ref[...], mode)

        def finish(r):
            if has_bias:
                r = r + bias_ref[...]
            res = epilogue(r, *[e[...] for e in e_refs]) if epilogue is not None else (r,)
            for ref, val in zip(o_refs, res):
                ref[...] = val.astype(ref.dtype)

        if nk == 1:
            finish(part)
        else:
            kk = pl.program_id(2)

            @pl.when(kk == 0)
            def _():
                acc_ref[...] = part

            @pl.when(kk > 0)
            def _():
                acc_ref[...] += part

            @pl.when(kk == nk - 1)
            def _():
                finish(acc_ref[...])

    res = pl.pallas_call(
        body, name=name, grid=(m // tm, n // tn, nk), in_specs=specs,
        out_specs=[pl.BlockSpec((tm, tn), lambda i, j, kk: (i, j)) for _ in out_dtypes],
        out_shape=[jax.ShapeDtypeStruct((m, n), dt) for dt in out_dtypes],
        scratch_shapes=[pltpu.VMEM((tm, tn), F32)] if nk > 1 else [],
        compiler_params=_cparams(3))(*args)
    return res[0] if n_o == 1 else res


def _ssm_expand(name, a, bmat, mode, tm=512):
    s = a.shape[0]
    tm = min(tm, s)

    def body(a_ref, b_ref, re_ref, im_ref):
        r = _dot(a_ref[...], b_ref[...], mode)
        re_ref[...] = r[:, :CH_N]
        im_ref[...] = r[:, CH_N:]

    return pl.pallas_call(
        body, name=name, grid=(s // tm, SSM_CHUNKS),
        in_specs=[pl.BlockSpec((tm, CH_W), lambda i, j: (i, j)),
                  pl.BlockSpec((None,) + bmat.shape[1:], lambda i, j: (j, 0, 0))],
        out_specs=[pl.BlockSpec((tm, CH_N), lambda i, j: (i, j))] * 2,
        out_shape=[jax.ShapeDtypeStruct((s, N_STATE), F32)] * 2,
        compiler_params=_cparams(2))(a, bmat)


def _ssm_contract(name, a_re, a_im, bmat, mode, d_row, extra, tm=512):
    s = a_re.shape[0]
    tm = min(tm, s)

    def body(re_ref, im_ref, b_ref, d_ref, e_ref, o_ref):
        b = b_ref[...]
        if mode == "nn":
            r = _dot(re_ref[...], b[:CH_N], "nn") + _dot(im_ref[...], b[CH_N:], "nn")
        else:
            r = _dot(re_ref[...], b[:, :CH_N], "nt") + _dot(im_ref[...], b[:, CH_N:], "nt")
        o_ref[...] = r + d_ref[...] * e_ref[...]

    return pl.pallas_call(
        body, name=name, grid=(s // tm, SSM_CHUNKS),
        in_specs=[pl.BlockSpec((tm, CH_N), lambda i, j: (i, j)), pl.BlockSpec((tm, CH_N), lambda i, j: (i, j)),
                  pl.BlockSpec((None,) + bmat.shape[1:], lambda i, j: (j, 0, 0)),
                  pl.BlockSpec((1, CH_W), lambda i, j: (0, j)), pl.BlockSpec((tm, CH_W), lambda i, j: (i, j))],
        out_specs=pl.BlockSpec((tm, CH_W), lambda i, j: (i, j)),
        out_shape=jax.ShapeDtypeStruct((s, SSM_WIDTH), F32),
        compiler_params=_cparams(2))(a_re, a_im, bmat, d_row, extra)


def _ssm_wgrad(name, chan, st_re, st_im, expand, tk=512):
    s = chan.shape[0]
    tk = min(tk, s)
    nk = s // tk
    oshape = (CH_W, 2 * CH_N) if expand else (2 * CH_N, CH_W)

    def body(c_ref, re_ref, im_ref, o_ref):
        c = c_ref[...]
        if expand:
            part = jnp.concatenate([_dot(c, re_ref[...], "tn"), _dot(c, im_ref[...], "tn")], axis=1)
        else:
            part = jnp.concatenate([_dot(re_ref[...], c, "tn"), _dot(im_ref[...], c, "tn")], axis=0)
        kk = pl.program_id(1)

        @pl.when(kk == 0)
        def _():
            o_ref[...] = part

        @pl.when(kk > 0)
        def _():
            o_ref[...] += part

    return pl.pallas_call(
        body, name=name, grid=(SSM_CHUNKS, nk),
        in_specs=[pl.BlockSpec((tk, CH_W), lambda j, kk: (kk, j)), pl.BlockSpec((tk, CH_N), lambda j, kk: (kk, j)),
                  pl.BlockSpec((tk, CH_N), lambda j, kk: (kk, j))],
        out_specs=pl.BlockSpec((None,) + oshape, lambda j, kk: (j, 0, 0)),
        out_shape=jax.ShapeDtypeStruct((SSM_CHUNKS,) + oshape, F32),
        compiler_params=_cparams(2))(chan, st_re, st_im)


SCAN_LB = 256


def _ssm_scan(name, w_re, w_im, a_re, a_im, reverse):
    s = w_re.shape[0]
    seg_len = s // N_SEG
    n_sq = int(math.log2(seg_len))
    assert 2 ** n_sq == seg_len

    def body(are_ref, aim_ref, wre_ref, wim_ref, hre_ref, him_ref, ere, eim, cre, cim):
        ar1 = are_ref[...]
        ai1 = -aim_ref[...] if reverse else aim_ref[...]
        ar = jnp.broadcast_to(ar1, (N_SEG, SCAN_LB))
        ai = jnp.broadcast_to(ai1, (N_SEG, SCAN_LB))

        def rows_of(k):
            kk = seg_len - 1 - k if reverse else k
            return pl.ds(pl.multiple_of(kk * N_SEG, N_SEG), N_SEG)

        def local(k, carry):
            hr, hi = carry
            rows = rows_of(k)
            nr = ar * hr - ai * hi + wre_ref[rows, :]
            ni = ar * hi + ai * hr + wim_ref[rows, :]
            hre_ref[rows, :] = nr
            him_ref[rows, :] = ni
            return nr, ni

        zero = jnp.zeros((N_SEG, SCAN_LB), F32)
        er, ei = lax.fori_loop(0, seg_len, local, (zero, zero))
        ere[...] = er
        eim[...] = ei
        pr, pi = ar1, ai1
        for _ in range(n_sq):
            pr, pi = pr * pr - pi * pi, 2.0 * pr * pi
        cr = jnp.zeros((1, SCAN_LB), F32)
        ci = jnp.zeros((1, SCAN_LB), F32)
        for jj in range(N_SEG):
            j = N_SEG - 1 - jj if reverse else jj
            cre[j:j + 1, :] = cr
            cim[j:j + 1, :] = ci
            er_j, ei_j = ere[j:j + 1, :], eim[j:j + 1, :]
            cr, ci = pr * cr - pi * ci + er_j, pr * ci + pi * cr + ei_j
        c_r, c_i = cre[...], cim[...]

        def fix(k, carry):
            qr, qi = carry
            rows = rows_of(k)
            hre_ref[rows, :] = hre_ref[rows, :] + (qr * c_r - qi * c_i)
            him_ref[rows, :] = him_ref[rows, :] + (qr * c_i + qi * c_r)
            return qr * ar - qi * ai, qr * ai + qi * ar

        lax.fori_loop(0, seg_len, fix, (ar, ai))

    nblk = N_STATE // SCAN_LB
    blk = pl.BlockSpec((s, SCAN_LB), lambda b: (0, b))
    row = pl.BlockSpec((1, SCAN_LB), lambda b: (0, b))
    return pl.pallas_call(
        body, name=name, grid=(nblk,), in_specs=[row, row, blk, blk], out_specs=[blk, blk],
        out_shape=[jax.ShapeDtypeStruct((s, N_STATE), F32)] * 2,
        scratch_shapes=[pltpu.VMEM((N_SEG, SCAN_LB), F32)] * 4,
        compiler_params=_cparams(1))(a_re, a_im, w_re, w_im)


def _ssm_da(g_re, g_im, h_re, h_im):
    s = g_re.shape[0]
    seg_len = s // N_SEG

    def body(gre_ref, gim_ref, hre_ref, him_ref, dre_ref, dim_ref):
        def rows_of(k):
            return pl.ds(pl.multiple_of(k * N_SEG, N_SEG), N_SEG)

        def step(k, carry):
            sr, si = carry
            gr, gi = gre_ref[rows_of(k), :], gim_ref[rows_of(k), :]
            pr, pi = hre_ref[rows_of(k - 1), :], him_ref[rows_of(k - 1), :]
            return sr + gr * pr + gi * pi, si + gi * pr - gr * pi

        zero = jnp.zeros((N_SEG, SCAN_LB), F32)
        sr, si = lax.fori_loop(1, seg_len, step, (zero, zero))
        last = pl.ds((seg_len - 1) * N_SEG, N_SEG)
        first_row = lax.broadcasted_iota(jnp.int32, (N_SEG, SCAN_LB), 0) == 0
        pr = jnp.where(first_row, 0.0, pltpu.roll(hre_ref[last, :], 1, 0))
        pi = jnp.where(first_row, 0.0, pltpu.roll(him_ref[last, :], 1, 0))
        gr, gi = gre_ref[pl.ds(0, N_SEG), :], gim_ref[pl.ds(0, N_SEG), :]
        sr = sr + gr * pr + gi * pi
        si = si + gi * pr - gr * pi
        dre_ref[...] = jnp.sum(sr, axis=0, keepdims=True)
        dim_ref[...] = jnp.sum(si, axis=0, keepdims=True)

    nblk = N_STATE // SCAN_LB
    blk = pl.BlockSpec((s, SCAN_LB), lambda b: (0, b))
    row = pl.BlockSpec((1, SCAN_LB), lambda b: (0, b))
    return pl.pallas_call(
        body, name="ssm_da", grid=(nblk,), in_specs=[blk] * 4, out_specs=[row, row],
        out_shape=[jax.ShapeDtypeStruct((1, N_STATE), F32)] * 2,
        compiler_params=_cparams(1))(g_re, g_im, h_re, h_im)


def _disc(ldt, are, aim, bre, bim):
    dt = jnp.exp(ldt)
    mag = jnp.exp(are * dt)
    abr = mag * jnp.cos(aim * dt)
    abi = mag * jnp.sin(aim * dt)
    den = jnp.square(are) + jnp.square(aim)
    nr = abr - 1.0
    fre = (nr * are + abi * aim) / den
    fim = (abi * are - nr * aim) / den
    return abr, abi, fre * bre - fim * bim, fre * bim + fim * bre


def _ssm_disc_fwd(ldt, are, aim, bre, bim):
    def body(l_ref, ar_ref, ai_ref, br_ref, bi_ref, o0, o1, o2, o3):
        res = _disc(l_ref[...], ar_ref[...], ai_ref[...], br_ref[...], bi_ref[...])
        for ref, val in zip((o0, o1, o2, o3), res):
            ref[...] = val

    col = jax.ShapeDtypeStruct((N_STATE, 1), F32)
    mat = jax.ShapeDtypeStruct((N_STATE, SSM_GROUP), F32)
    return pl.pallas_call(body, name="ssm_disc_fwd", out_shape=[col, col, mat, mat],
                          in_specs=[VMEM_SPEC] * 5, out_specs=[VMEM_SPEC] * 4)(ldt, are, aim, bre, bim)


def _ssm_disc_bwd(ldt, are, aim, bre, bim, d_abr, d_abi, d_bbr, d_bbi):
    def body(l_ref, ar_ref, ai_ref, br_ref, bi_ref, c0, c1, c2, c3, g_ldt, g_are, g_aim, g_bre, g_bim):
        _, vjp = jax.vjp(_disc, l_ref[...], ar_ref[...], ai_ref[...], br_ref[...], bi_ref[...])
        dl, dar, dai, dbr, dbi = vjp((c0[...], c1[...], c2[...], c3[...]))
        state = lax.broadcasted_iota(jnp.int32, (N_STATE, SSM_GROUPS), 0)
        group = lax.broadcasted_iota(jnp.int32, (N_STATE, SSM_GROUPS), 1)
        pick = jnp.right_shift(state, 6) == group
        g_ldt[...] = jnp.sum(jnp.where(pick, dl, 0.0), axis=0, keepdims=True)
        g_are[...] = dar
        g_aim[...] = dai
        g_bre[...] = dbr
        g_bim[...] = dbi

    col = jax.ShapeDtypeStruct((N_STATE, 1), F32)
    mat = jax.ShapeDtypeStruct((N_STATE, SSM_GROUP), F32)
    return pl.pallas_call(body, name="ssm_disc_bwd",
                          out_shape=[jax.ShapeDtypeStruct((1, SSM_GROUPS), F32), col, col, mat, mat],
                          in_specs=[VMEM_SPEC] * 9, out_specs=[VMEM_SPEC] * 5,
                          compiler_params=pltpu.CompilerParams(vmem_limit_bytes=VMEM_LIMIT))(
        ldt, are, aim, bre, bim, d_abr, d_abi, d_bbr, d_bbi)


_EYE8 = np.eye(8, dtype=np.float32)


def _blockdiag_b(bb):
    t = bb.reshape(SSM_CHUNKS, 8, SSM_STATE, SSM_GROUP).transpose(0, 1, 3, 2)
    return jnp.einsum("igcn,gh->igchn", t, _EYE8).reshape(SSM_CHUNKS, CH_W, CH_N)


def _diag_of_b(m):
    t = jnp.einsum("igchn,gh->igcn", m.reshape(SSM_CHUNKS, 8, SSM_GROUP, 8, SSM_STATE), _EYE8)
    return t.transpose(0, 1, 3, 2).reshape(N_STATE, SSM_GROUP)


def _blockdiag_c(c):
    t = c.reshape(SSM_CHUNKS, 8, SSM_GROUP, SSM_STATE).transpose(0, 1, 3, 2)
    return jnp.einsum("ignc,gh->ignhc", t, _EYE8).reshape(SSM_CHUNKS, CH_N, CH_W)


def _diag_of_c(m):
    t = jnp.einsum("ignhc,gh->ignc", m.reshape(SSM_CHUNKS, 8, SSM_STATE, 8, SSM_GROUP), _EYE8)
    return t.transpose(0, 1, 3, 2).reshape(SSM_GROUPS, SSM_GROUP, SSM_STATE)


def _time_perm(a):
    s, c = a.shape
    return a.reshape(N_SEG, s // N_SEG, c).transpose(1, 0, 2).reshape(s, c)


def _time_unperm(a):
    s, c = a.shape
    return a.reshape(s // N_SEG, N_SEG, c).transpose(1, 0, 2).reshape(s, c)


def _dilate(a, d):
    s, c = a.shape
    return a if d == 1 else a.reshape(s // d, d, c).transpose(1, 0, 2).reshape(s, c)


def _undilate(a, d):
    s, c = a.shape
    return a if d == 1 else a.reshape(d, s // d, c).transpose(1, 0, 2).reshape(s, c)


def _stack_dilated(parts):
    return jnp.stack([_dilate(p, d) for p, d in zip(parts, DILATIONS)], axis=0)


def _blocks_per_seq(g, n_blocks):
    return jnp.right_shift(jnp.int32(n_blocks), 2 * g)


def _attn_fwd(q, k, v):
    s = q.shape[1]
    nb = s // ATT_BLK

    def body(q_ref, kc_ref, kp_ref, vc_ref, vp_ref, o_ref, lse_ref):
        g, b = pl.program_id(0), pl.program_id(1)
        has_prev = lax.rem(b, _blocks_per_seq(g, nb)) > 0
        qi = lax.broadcasted_iota(jnp.int32, (ATT_BLK, 2 * ATT_BLK), 0)
        ki = lax.broadcasted_iota(jnp.int32, (ATT_BLK, 2 * ATT_BLK), 1)
        steps = qi + ATT_BLK - ki
        first_key = jnp.where(has_prev, 0, ATT_BLK)
        valid = (steps >= 0) & (steps <= ATT_BLK) & (ki >= first_key)
        for h in range(ATT_HPG):
            sl = slice(h * ATT_HEAD_DIM, (h + 1) * ATT_HEAD_DIM)
            kcat = jnp.concatenate([kp_ref[:, sl], kc_ref[:, sl]], axis=0)
            vcat = jnp.concatenate([vp_ref[:, sl], vc_ref[:, sl]], axis=0)
            sc = _dot(q_ref[:, sl], kcat, "nt") * ATT_SCALE
            sc = jnp.where(valid, sc, NEG_INF)
            m = jnp.max(sc, axis=-1, keepdims=True)
            p = jnp.exp(sc - m)
            den = jnp.sum(p, axis=-1, keepdims=True)
            o_ref[:, sl] = _dot(p, vcat, "nn") / den
            lse_ref[:, sl] = jnp.broadcast_to(m + jnp.log(den), (ATT_BLK, ATT_HEAD_DIM))

    cur = pl.BlockSpec((None, ATT_BLK, ATT_GROUPW), lambda g, b: (g, b, 0))
    prev = pl.BlockSpec((None, ATT_BLK, ATT_GROUPW), lambda g, b: (g, jnp.maximum(b - 1, 0), 0))
    return pl.pallas_call(
        body, name="attn_fwd", grid=(3, nb), in_specs=[cur, cur, prev, cur, prev], out_specs=[cur, cur],
        out_shape=[jax.ShapeDtypeStruct((3, s, ATT_GROUPW), F32)] * 2,
        compiler_params=_cparams(2))(q, k, k, v, v)


def _attn_dq(q, k, v, do, lse, delta):
    s = q.shape[1]
    nb = s // ATT_BLK

    def body(q_ref, kc_ref, kp_ref, vc_ref, vp_ref, do_ref, lse_ref, dl_ref, dq_ref):
        g, b = pl.program_id(0), pl.program_id(1)
        has_prev = lax.rem(b, _blocks_per_seq(g, nb)) > 0
        qi = lax.broadcasted_iota(jnp.int32, (ATT_BLK, 2 * ATT_BLK), 0)
        ki = lax.broadcasted_iota(jnp.int32, (ATT_BLK, 2 * ATT_BLK), 1)
        steps = qi + ATT_BLK - ki
        first_key = jnp.where(has_prev, 0, ATT_BLK)
        valid = (steps >= 0) & (steps <= ATT_BLK) & (ki >= first_key)
        for h in range(ATT_HPG):
            sl = slice(h * ATT_HEAD_DIM, (h + 1) * ATT_HEAD_DIM)
            one = slice(h * ATT_HEAD_DIM, h * ATT_HEAD_DIM + 1)
            kcat = jnp.concatenate([kp_ref[:, sl], kc_ref[:, sl]], axis=0)
            vcat = jnp.concatenate([vp_ref[:, sl], vc_ref[:, sl]], axis=0)
            sc = _dot(q_ref[:, sl], kcat, "nt") * ATT_SCALE
            p = jnp.exp(jnp.where(valid, sc, NEG_INF) - lse_ref[:, one])
            dp = _dot(do_ref[:, sl], vcat, "nt")
            ds = p * (dp - dl_ref[:, one]) * ATT_SCALE
            dq_ref[:, sl] = _dot(ds, kcat, "nn")

    cur = pl.BlockSpec((None, ATT_BLK, ATT_GROUPW), lambda g, b: (g, b, 0))
    prev = pl.BlockSpec((None, ATT_BLK, ATT_GROUPW), lambda g, b: (g, jnp.maximum(b - 1, 0), 0))
    return pl.pallas_call(
        body, name="attn_dq", grid=(3, nb), in_specs=[cur, cur, prev, cur, prev, cur, cur, cur], out_specs=cur,
        out_shape=jax.ShapeDtypeStruct((3, s, ATT_GROUPW), F32),
        compiler_params=_cparams(2))(q, k, k, v, v, do, lse, delta)


def _attn_dkv(q, k, v, do, lse, delta):
    s = q.shape[1]
    nb = s // ATT_BLK

    def body(k_ref, v_ref, qc_ref, qn_ref, doc_ref, don_ref, lc_ref, ln_ref, dc_ref, dn_ref, dk_ref, dv_ref):
        g, b = pl.program_id(0), pl.program_id(1)
        next_uses = (b + 1 < nb) & (lax.rem(b + 1, _blocks_per_seq(g, nb)) > 0)
        ri = lax.broadcasted_iota(jnp.int32, (2 * ATT_BLK, ATT_BLK), 0)
        ki = lax.broadcasted_iota(jnp.int32, (2 * ATT_BLK, ATT_BLK), 1)
        reach = jnp.where(next_uses, 0, 4 * ATT_BLK)
        valid = ((ri < ATT_BLK) & (ri >= ki)) | ((ri >= ATT_BLK) & (ki - ri + ATT_BLK >= reach))
        for h in range(ATT_HPG):
            sl = slice(h * ATT_HEAD_DIM, (h + 1) * ATT_HEAD_DIM)
            one = slice(h * ATT_HEAD_DIM, h * ATT_HEAD_DIM + 1)
            qcat = jnp.concatenate([qc_ref[:, sl], qn_ref[:, sl]], axis=0)
            docat = jnp.concatenate([doc_ref[:, sl], don_ref[:, sl]], axis=0)
            lcat = jnp.concatenate([lc_ref[:, one], ln_ref[:, one]], axis=0)
            dcat = jnp.concatenate([dc_ref[:, one], dn_ref[:, one]], axis=0)
            sc = _dot(qcat, k_ref[:, sl], "nt") * ATT_SCALE
            p = jnp.exp(jnp.where(valid, sc, NEG_INF) - lcat)
            dv_ref[:, sl] = _dot(p, docat, "tn")
            dp = _dot(docat, v_ref[:, sl], "nt")
            ds = p * (dp - dcat) * ATT_SCALE
            dk_ref[:, sl] = _dot(ds, qcat, "tn")

    cur = pl.BlockSpec((None, ATT_BLK, ATT_GROUPW), lambda g, b: (g, b, 0))
    nxt = pl.BlockSpec((None, ATT_BLK, ATT_GROUPW), lambda g, b: (g, jnp.minimum(b + 1, nb - 1), 0))
    return pl.pallas_call(
        body, name="attn_dkv", grid=(3, nb), in_specs=[cur, cur, cur, nxt, cur, nxt, cur, nxt, cur, nxt],
        out_specs=[cur, cur], out_shape=[jax.ShapeDtypeStruct((3, s, ATT_GROUPW), F32)] * 2,
        compiler_params=_cparams(2))(k, v, q, q, do, do, lse, lse, delta, delta)


def _xattn_probs(q, kh):
    sc = _dot(q, kh, "nt") * XATT_SCALE
    e = jnp.exp(sc - jnp.max(sc, axis=-1, keepdims=True))
    return e / jnp.sum(e, axis=-1, keepdims=True)


def _xattn_fwd(q, kv, tm=512):
    s = q.shape[0]
    tm = min(tm, s)

    def body(q_ref, kv_ref, o_ref):
        for h in range(XATT_HEADS):
            sl = slice(h * XATT_HEAD_DIM, (h + 1) * XATT_HEAD_DIM)
            vs = slice(D_MODEL + h * XATT_HEAD_DIM, D_MODEL + (h + 1) * XATT_HEAD_DIM)
            p = _xattn_probs(q_ref[:, sl], kv_ref[:, sl])
            o_ref[:, sl] = _dot(p, kv_ref[:, vs], "nn").astype(o_ref.dtype)

    return pl.pallas_call(
        body, name="xattn_fwd", grid=(s // tm,),
        in_specs=[pl.BlockSpec((tm, D_MODEL), lambda i: (i, 0)), pl.BlockSpec(kv.shape, lambda i: (0, 0))],
        out_specs=pl.BlockSpec((tm, D_MODEL), lambda i: (i, 0)),
        out_shape=jax.ShapeDtypeStruct((s, D_MODEL), MXU_DTYPE), compiler_params=_cparams(1))(q, kv)


def _xattn_bwd(q, kv, do, tm=512):
    s = q.shape[0]
    tm = min(tm, s)

    def body(q_ref, kv_ref, do_ref, dq_ref, dkv_ref):
        first = pl.program_id(0) == 0

        @pl.when(first)
        def _():
            dkv_ref[...] = jnp.zeros_like(dkv_ref)

        for h in range(XATT_HEADS):
            sl = slice(h * XATT_HEAD_DIM, (h + 1) * XATT_HEAD_DIM)
            vs = slice(D_MODEL + h * XATT_HEAD_DIM, D_MODEL + (h + 1) * XATT_HEAD_DIM)
            p = _xattn_probs(q_ref[:, sl], kv_ref[:, sl])
            dkv_ref[:, vs] += _dot(p, do_ref[:, sl], "tn")
            dp = _dot(do_ref[:, sl], kv_ref[:, vs], "nt")
            ds = p * (dp - jnp.sum(dp * p, axis=-1, keepdims=True)) * XATT_SCALE
            dq_ref[:, sl] = _dot(ds, kv_ref[:, sl], "nn").astype(dq_ref.dtype)
            dkv_ref[:, sl] += _dot(ds, q_ref[:, sl], "tn")

    row = pl.BlockSpec((tm, D_MODEL), lambda i: (i, 0))
    whole = pl.BlockSpec(kv.shape, lambda i: (0, 0))
    return pl.pallas_call(
        body, name="xattn_bwd", grid=(s // tm,), in_specs=[row, whole, row], out_specs=[row, whole],
        out_shape=[jax.ShapeDtypeStruct((s, D_MODEL), MXU_DTYPE), jax.ShapeDtypeStruct(kv.shape, F32)],
        compiler_params=_cparams(1))(q, kv, do)


def _ln(x, g, b):
    mu = jnp.mean(x, axis=-1, keepdims=True)
    xc = x - mu
    var = jnp.mean(jnp.square(xc), axis=-1, keepdims=True)
    return xc * lax.rsqrt(var + LN_EPS) * g + b


def _res_ln(h, o, g, b):
    return _ln(DEEPNORM_ALPHA * h + o, g, b)


def _gate(gs, ga, z1, z2, batt):
    return jax.nn.sigmoid(gs) * (z1 * jax.nn.sigmoid(z2)) + jax.nn.sigmoid(ga) * batt


def _rope_tables(pos, invf, m1, m2):
    ang = pos.astype(F32) * invf
    sin = jnp.sin(ang)
    return jnp.cos(ang), -sin * m1, sin * m2


def _rope(t, cos, s_up, s_dn):
    w = t.shape[-1]
    return t * cos + pltpu.roll(t, w - ROT_DIM // 2, 1) * s_up + pltpu.roll(t, ROT_DIM // 2, 1) * s_dn


def _rope_t(dt, cos, s_up, s_dn):
    w = dt.shape[-1]
    return dt * cos + pltpu.roll(dt * s_up, ROT_DIM // 2, 1) + pltpu.roll(dt * s_dn, w - ROT_DIM // 2, 1)


def _rope_consts():
    inv_freq = ROPE_THETA ** (-jnp.arange(0, ROT_DIM, 2, dtype=F32) / ROT_DIM)
    d = np.arange(ATT_GROUPW) % ATT_HEAD_DIM
    invf = jnp.where(d < ROT_DIM, inv_freq[d % (ROT_DIM // 2)], 0.0).reshape(1, ATT_GROUPW).astype(F32)
    m1 = jnp.asarray((d < ROT_DIM // 2).astype(np.float32)).reshape(1, ATT_GROUPW)
    m2 = jnp.asarray(((d >= ROT_DIM // 2) & (d < ROT_DIM)).astype(np.float32)).reshape(1, ATT_GROUPW)
    return invf, m1, m2


def _head_sum_matrix():
    d = np.arange(ATT_GROUPW) // ATT_HEAD_DIM
    return jnp.asarray((d[:, None] == d[None, :]).astype(np.float32))


def _adamw(w, g, m, v):
    m = ADAM_B1 * m + (1.0 - ADAM_B1) * g
    v = ADAM_B2 * v + (1.0 - ADAM_B2) * jnp.square(g)
    m_hat = m / (1.0 - ADAM_B1 ** ADAM_STEP)
    v_hat = v / (1.0 - ADAM_B2 ** ADAM_STEP)
    delta = -ADAM_LR * (m_hat / (jnp.sqrt(v_hat) + ADAM_EPS) + ADAM_WD * w)
    return delta, m, v


def _local_step(x, mem, pos, target, sp, wb):
    s = x.shape[0]
    al = DEEPNORM_ALPHA
    mx = MXU_DTYPE

    h0, h0b = _rowwise("ln_in", lambda x, g, b: (lambda h: (h, h))(_ln(x, g, b)), [x],
                       [sp["ln_in_g"], sp["ln_in_b"]], [(D_MODEL, F32), (D_MODEL, mx)])
    proj = _mm("proj", h0b, wb["w_in"], "nn", bias=sp["b_in"])

    ldt = jnp.repeat(sp["ssm_log_dt"].reshape(SSM_GROUPS), SSM_STATE).reshape(N_STATE, 1)
    are, aim = sp["ssm_a_re"].reshape(N_STATE, 1), sp["ssm_a_im"].reshape(N_STATE, 1)
    bre, bim = sp["ssm_b_re"].reshape(N_STATE, SSM_GROUP), sp["ssm_b_im"].reshape(N_STATE, SSM_GROUP)
    abr, abi, bbr, bbi = _ssm_disc_fwd(ldt, are, aim, bre, bim)
    a_re, a_im = abr.reshape(1, N_STATE), abi.reshape(1, N_STATE)
    bexp = jnp.concatenate([_blockdiag_b(bbr), _blockdiag_b(bbi)], axis=2).astype(mx)
    cexp = jnp.concatenate([_blockdiag_c(sp["ssm_c_re"].reshape(SSM_GROUPS, SSM_GROUP, SSM_STATE)),
                            -_blockdiag_c(sp["ssm_c_im"].reshape(SSM_GROUPS, SSM_GROUP, SSM_STATE))],
                           axis=1).astype(mx)
    u_p = _time_perm(proj[:, :SSM_WIDTH])
    w_re, w_im = _ssm_expand("ssm_bu", u_p, bexp, "nn")
    h_re, h_im = _ssm_scan("ssm_scan_fwd", w_re, w_im, a_re, a_im, reverse=False)
    y_p = _ssm_contract("ssm_ch", h_re, h_im, cexp, "nn", sp["ssm_d"], u_p)
    y = _time_unperm(y_p)
    ygb, = _rowwise("gelu", lambda y: jax.nn.gelu(y), [y], [], [(SSM_WIDTH, mx)])
    z = _mm("glu", ygb, wb["w_glu"], "nn", bias=sp["b_glu"])

    invf, m1, m2 = _rope_consts()

    def rope_fwd(pos, q0, q1, q2, k0, k1, k2, v0, v1, v2, invf, m1, m2):
        tabs = _rope_tables(pos, invf, m1, m2)
        return tuple(_rope(t, *tabs) for t in (q0, q1, q2, k0, k1, k2)) + (v0, v1, v2)

    qkv_cols = [(proj, ATT_GROUPW, 3 + i) for i in range(9)]
    qkv = _rowwise("rope", rope_fwd, [pos] + qkv_cols, [invf, m1, m2], [(ATT_GROUPW, mx)] * 9)
    q_d, k_d, v_d = _stack_dilated(qkv[0:3]), _stack_dilated(qkv[3:6]), _stack_dilated(qkv[6:9])
    o_d, lse_d = _attn_fwd(q_d, k_d, v_d)
    o_g = [_undilate(o_d[i], d) for i, d in enumerate(DILATIONS)]
    l_g = [_undilate(lse_d[i], d) for i, d in enumerate(DILATIONS)]

    def merge(o0, o1, o2, l0, l1, l2):
        m = jnp.maximum(jnp.maximum(l0, l1), l2)
        e0, e1, e2 = jnp.exp(l0 - m), jnp.exp(l1 - m), jnp.exp(l2 - m)
        tot = e0 + e1 + e2
        att = (e0 * o0 + e1 * o1 + e2 * o2) / tot
        return att, att, m + jnp.log(tot)

    att, attb, lse_tot = _rowwise("attn_merge", merge, o_g + l_g, [],
                                  [(ATT_GROUPW, F32), (ATT_GROUPW, mx), (ATT_GROUPW, F32)])
    batt = _mm("att_up", attb, wb["w_att_up"], "nn")

    gate_rows = [(proj, D_MODEL, 3), (proj, D_MODEL, 4), (z, D_MODEL, 0), (z, D_MODEL, 1), batt]
    mixedb, = _rowwise("gate", _gate, gate_rows, [], [(D_MODEL, mx)])
    o1 = _mm("mix_out", mixedb, wb["w_mix_out"], "nn", bias=sp["b_mix_out"])
    h1, h1b = _rowwise("ln1", lambda h, o, g, b: (lambda r: (r, r))(_res_ln(h, o, g, b)), [h0, o1],
                       [sp["ln1_g"], sp["ln1_b"]], [(D_MODEL, F32), (D_MODEL, mx)])

    qx = _mm("xq", h1b, wb["w_xq"], "nn", out_dtypes=(mx,))
    kvx = _mm("xkv", mem, wb["w_xkv"], "nn", out_dtypes=(mx,))
    oxb = _xattn_fwd(qx, kvx)
    o2 = _mm("xo", oxb, wb["w_xo"], "nn")
    h2, h2b = _rowwise("ln2", lambda h, o, g, b: (lambda r: (r, r))(_res_ln(h, o, g, b)), [h1, o2],
                       [sp["ln2_g"], sp["ln2_b"]], [(D_MODEL, F32), (D_MODEL, mx)])

    a_ff, fb = _mm("ff1", h2b, wb["w_ff1"], "nn", bias=sp["b_ff1"],
                   epilogue=lambda r: (r, jnp.square(jnp.maximum(r, 0.0))), out_dtypes=(F32, mx))
    o3 = _mm("ff2", fb, wb["w_ff2"], "nn", bias=sp["b_ff2"])

    def loss_bwd(h2, o3, tgt, g, b):
        def f(h2, o3, g, b):
            h3 = _res_ln(h2, o3, g, b)
            return 0.5 * jnp.sum(jnp.mean(jnp.square(h3 - tgt), axis=-1))

        loss, vjp = jax.vjp(f, h2, o3, g, b)
        _, dr, dg, db = vjp(jnp.ones((), F32))
        return dr, dr, dg, db, _colsum(dr), jnp.full((1, 128), loss, F32)

    dr3, dr3b, g_ln3_g, g_ln3_b, g_b_ff2, loss = _rowwise(
        "loss_ln3_bwd", loss_bwd, [h2, o3, target], [sp["ln3_g"], sp["ln3_b"]],
        [(D_MODEL, F32), (D_MODEL, mx)], [D_MODEL, D_MODEL, D_MODEL, 128])

    dab = _mm("ff2_dx", dr3b, wb["w_ff2"], "nt", extras=(a_ff,),
              epilogue=lambda r, a: (r * (2.0 * jnp.maximum(a, 0.0)),), out_dtypes=(mx,))
    g_w_ff2 = _mm("ff2_dw", fb, dr3b, "tn")
    g_b_ff1, = _rowwise("ff1_db", lambda v: (_colsum(v),), [dab], [], [], [D_FF])
    g_w_ff1 = _mm("ff1_dw", h2b, dab, "tn")
    dh2 = _mm("ff1_dx", dab, wb["w_ff1"], "nt", extras=(dr3,), epilogue=lambda r, d: (r + al * d,))

    def ln_bwd(h, o, dout, g, b):
        _, vjp = jax.vjp(_res_ln, h, o, g, b)
        _, dr, dg, db = vjp(dout)
        return dr, dr, dg, db, _colsum(dr)

    dr2, dr2b, g_ln2_g, g_ln2_b, _ = _rowwise(
        "ln2_bwd", ln_bwd, [h1, o2, dh2], [sp["ln2_g"], sp["ln2_b"]],
        [(D_MODEL, F32), (D_MODEL, mx)], [D_MODEL, D_MODEL, D_MODEL])
    g_w_xo = _mm("xo_dw", oxb, dr2b, "tn")
    doxb = _mm("xo_dx", dr2b, wb["w_xo"], "nt", out_dtypes=(mx,))
    dqxb, dkvx = _xattn_bwd(qx, kvx, doxb)
    g_w_xq = _mm("xq_dw", h1b, dqxb, "tn")
    dh1 = _mm("xq_dx", dqxb, wb["w_xq"], "nt", extras=(dr2,), epilogue=lambda r, d: (r + al * d,))
    g_w_xkv = _mm("xkv_dw", mem, dkvx, "tn")

    dr1, dr1b, g_ln1_g, g_ln1_b, g_b_mix = _rowwise(
        "ln1_bwd", ln_bwd, [h0, o1, dh1], [sp["ln1_g"], sp["ln1_b"]],
        [(D_MODEL, F32), (D_MODEL, mx)], [D_MODEL, D_MODEL, D_MODEL])
    g_w_mix = _mm("mix_dw", mixedb, dr1b, "tn")
    dmixed = _mm("mix_dx", dr1b, wb["w_mix_out"], "nt")

    def gate_bwd(gs, ga, z1, z2, batt, dm):
        _, vjp = jax.vjp(_gate, gs, ga, z1, z2, batt)
        dgs, dga, dz1, dz2, dbatt = vjp(dm)
        dz = jnp.concatenate([dz1, dz2], axis=-1)
        return dgs, dga, dz, dbatt, _colsum(dz)

    dgsb, dgab, dzb, dbattb, g_b_glu = _rowwise(
        "gate_bwd", gate_bwd, gate_rows + [dmixed], [],
        [(D_MODEL, mx), (D_MODEL, mx), (2 * D_MODEL, mx), (D_MODEL, mx)], [2 * D_MODEL])
    g_w_up = _mm("att_up_dw", attb, dbattb, "tn")
    datt = _mm("att_up_dx", dbattb, wb["w_att_up"], "nt")

    def att_delta(datt, att, hs):
        dl = jnp.dot(datt * att, hs, precision=lax.Precision.HIGHEST, preferred_element_type=F32)
        return datt, dl

    dattb, delta = _rowwise("attn_delta", att_delta, [datt, att], [_head_sum_matrix()],
                            [(ATT_GROUPW, mx), (ATT_GROUPW, F32)])
    do_d = _stack_dilated([dattb] * 3)
    lt_d = _stack_dilated([lse_tot] * 3)
    dl_d = _stack_dilated([delta] * 3)
    dq_d = _attn_dq(q_d, k_d, v_d, do_d, lt_d, dl_d)
    dk_d, dv_d = _attn_dkv(q_d, k_d, v_d, do_d, lt_d, dl_d)
    dqkv = [_undilate(t[i], d) for t in (dq_d, dk_d, dv_d) for i, d in enumerate(DILATIONS)]

    def rope_bwd(pos, q0, q1, q2, k0, k1, k2, v0, v1, v2, invf, m1, m2):
        tabs = _rope_tables(pos, invf, m1, m2)
        return jnp.concatenate([_rope_t(t, *tabs) for t in (q0, q1, q2, k0, k1, k2)] + [v0, v1, v2], axis=-1)

    dqkvb, = _rowwise("rope_bwd", rope_bwd, [pos] + dqkv, [invf, m1, m2], [(9 * ATT_GROUPW, mx)])

    g_w_glu = _mm("glu_dw", ygb, dzb, "tn")
    dyg = _mm("glu_dx", dzb, wb["w_glu"], "nt")

    def gelu_bwd(y, dyg):
        _, vjp = jax.vjp(jax.nn.gelu, y)
        return vjp(dyg)[0]

    dy, = _rowwise("gelu_bwd", gelu_bwd, [y, dyg], [], [(SSM_WIDTH, F32)])
    dy_p = _time_perm(dy)
    dh_re, dh_im = _ssm_expand("ssm_dh", dy_p, cexp, "nt")
    g_cexp = _ssm_wgrad("ssm_dc", dy_p, h_re, h_im, expand=False)
    s_re, s_im = _ssm_scan("ssm_scan_bwd", dh_re, dh_im, a_re, a_im, reverse=True)
    d_abr, d_abi = _ssm_da(s_re, s_im, h_re, h_im)
    g_bexp = _ssm_wgrad("ssm_db", u_p, s_re, s_im, expand=True)
    du_p = _ssm_contract("ssm_du", s_re, s_im, bexp, "nt", sp["ssm_d"], dy_p)
    g_ssm_d, = _rowwise("ssm_dd", lambda a, b: (_colsum(a * b),), [dy_p, u_p], [], [], [SSM_WIDTH])
    g_ldt, g_are, g_aim, g_bre, g_bim = _ssm_disc_bwd(
        ldt, are, aim, bre, bim, d_abr.reshape(N_STATE, 1), d_abi.reshape(N_STATE, 1),
        _diag_of_b(g_bexp[:, :, :CH_N]), _diag_of_b(g_bexp[:, :, CH_N:]))
    g_c_re = _diag_of_c(g_cexp[:, :CH_N, :])
    g_c_im = -_diag_of_c(g_cexp[:, CH_N:, :])
    dub = _time_unperm(du_p).astype(mx)

    dprojb = jnp.concatenate([dub, dqkvb, dgsb, dgab], axis=-1)
    g_b_in, = _rowwise("in_db", lambda v: (_colsum(v),), [dprojb], [], [], [IN_COLS])
    g_w_in = _mm("in_dw", h0b, dprojb, "tn")
    dh0 = _mm("in_dx", dprojb, wb["w_in"], "nt", extras=(dr1,), epilogue=lambda r, d: (r + al * d,))

    def ln_in_bwd(x, dout, g, b):
        _, vjp = jax.vjp(_ln, x, g, b)
        return vjp(dout)

    dx, g_ln_in_g, g_ln_in_b = _rowwise("ln_in_bwd", ln_in_bwd, [x, dh0], [sp["ln_in_g"], sp["ln_in_b"]],
                                        [(D_MODEL, F32)], [D_MODEL, D_MODEL])

    big = {"w_in": g_w_in, "w_glu": g_w_glu, "w_att_up": g_w_up, "w_mix_out": g_w_mix, "w_xq": g_w_xq,
           "w_xkv": g_w_xkv, "w_xo": g_w_xo, "w_ff1": g_w_ff1, "w_ff2": g_w_ff2}
    small = {"ln_in_g": g_ln_in_g, "ln_in_b": g_ln_in_b, "b_in": g_b_in, "ssm_log_dt": g_ldt, "ssm_a_re": g_are,
             "ssm_a_im": g_aim, "ssm_b_re": g_bre, "ssm_b_im": g_bim, "ssm_c_re": g_c_re, "ssm_c_im": g_c_im,
             "ssm_d": g_ssm_d, "b_glu": g_b_glu, "b_mix_out": g_b_mix, "ln1_g": g_ln1_g, "ln1_b": g_ln1_b,
             "ln2_g": g_ln2_g, "ln2_b": g_ln2_b, "b_ff1": g_b_ff1, "b_ff2": g_b_ff2, "ln3_g": g_ln3_g,
             "ln3_b": g_ln3_b}
    return loss, dx, big, small


def _piece_shape(k, n, axis):
    return (k // 2, n // 4) if axis == 1 else (k // 8, n)


def _aligned(v, m):
    return v if isinstance(v, int) else pl.multiple_of(v, m)


def _full_piece(ref, k, n, axis, chip, half):
    pr, pc = _piece_shape(k, n, axis)
    if axis == 1:
        return ref.at[pl.ds(_aligned(half * pr, 8), pr), pl.ds(_aligned(chip * pc, 128), pc)]
    return ref.at[pl.ds(_aligned(chip * (2 * pr) + half * pr, 8), pr), :]


def _full_shard(ref, k, n, axis, chip):
    if axis == 1:
        return ref.at[:, pl.ds(_aligned(chip * (n // 4), 128), n // 4)]
    return ref.at[pl.ds(_aligned(chip * (k // 4), 8), k // 4), :]


def _shard_piece(ref, k, n, axis, half):
    pr, _ = _piece_shape(k, n, axis)
    return ref.at[pl.ds(_aligned(half * pr, 8), pr), :]


def _mesh_pos():
    x, y, c = lax.axis_index("x"), lax.axis_index("y"), lax.axis_index("c")
    other_chips = [(1 - x, y), (x, 1 - y), (1 - x, 1 - y)]
    return x, y, c, other_chips


def _remote(src, dst, send_sem, recv_sem, dev):
    return pltpu.make_async_remote_copy(src_ref=src, dst_ref=dst, send_sem=send_sem, recv_sem=recv_sem,
                                        device_id=dev, device_id_type=MESH)


def _gather_weights(shards):
    nw = len(BIG)

    def body(*refs):
        sh, full = refs[:nw], refs[nw:2 * nw]
        send_sems, recv_sems, loc_sems = refs[2 * nw:]
        x, y, c, chips = _mesh_pos()
        me = 2 * x + y
        sib = (x, y, 1 - c)
        local, first, fwd = [], [], []
        for wi, (_, k, n, ax) in enumerate(BIG):
            cp = pltpu.make_async_copy(sh[wi], _full_shard(full[wi], k, n, ax, me), loc_sems.at[wi])
            cp.start()
            local.append(cp)
            for j, (qx, qy) in enumerate(chips):
                cp = _remote(_shard_piece(sh[wi], k, n, ax, c), _full_piece(full[wi], k, n, ax, me, c),
                             send_sems.at[wi * 6 + j], recv_sems.at[wi * 6 + j], (qx, qy, c))
                cp.start()
                first.append(cp)
        for wi, (_, k, n, ax) in enumerate(BIG):
            for j, (qx, qy) in enumerate(chips):
                piece = _full_piece(full[wi], k, n, ax, 2 * qx + qy, c)
                _remote(piece, piece, send_sems.at[wi * 6 + j], recv_sems.at[wi * 6 + j], (qx, qy, c)).wait_recv()
                cp = _remote(piece, piece, send_sems.at[wi * 6 + 3 + j], recv_sems.at[wi * 6 + 3 + j], sib)
                cp.start()
                fwd.append(cp)
        for wi, (_, k, n, ax) in enumerate(BIG):
            for j, (qx, qy) in enumerate(chips):
                piece = _full_piece(full[wi], k, n, ax, 2 * qx + qy, 1 - c)
                _remote(piece, piece, send_sems.at[wi * 6 + 3 + j], recv_sems.at[wi * 6 + 3 + j], sib).wait_recv()
        for cp in first + fwd:
            cp.wait_send()
        for cp in local:
            cp.wait()

    return pl.pallas_call(
        body, name="gather_weights", in_specs=[HBM_SPEC] * nw, out_specs=[HBM_SPEC] * nw,
        out_shape=[jax.ShapeDtypeStruct((k, n), MXU_DTYPE) for _, k, n, _ in BIG],
        scratch_shapes=[pltpu.SemaphoreType.DMA((6 * nw,)), pltpu.SemaphoreType.DMA((6 * nw,)),
                        pltpu.SemaphoreType.DMA((nw,))])(*shards)


def _reduce_swap_halves(grads):
    nw = len(BIG)

    def body(*refs):
        g, own, got = refs[:nw], refs[nw:2 * nw], refs[2 * nw:3 * nw]
        send_sems, recv_sems, loc_sems = refs[3 * nw:]
        x, y, c, _ = _mesh_pos()
        sib = (x, y, 1 - c)
        cps, loc = [], []
        for wi, (_, k, n, ax) in enumerate(BIG):
            for q in range(4):
                cp = pltpu.make_async_copy(_full_piece(g[wi], k, n, ax, q, c), own[wi].at[q], loc_sems.at[wi * 4 + q])
                cp.start()
                loc.append(cp)
                cp = _remote(_full_piece(g[wi], k, n, ax, q, 1 - c), got[wi].at[q],
                             send_sems.at[wi * 4 + q], recv_sems.at[wi * 4 + q], sib)
                cp.start()
                cps.append(cp)
        for cp in cps:
            cp.wait()
        for cp in loc:
            cp.wait()

    shapes = [jax.ShapeDtypeStruct((4,) + _piece_shape(k, n, ax), F32) for _, k, n, ax in BIG]
    res = pl.pallas_call(
        body, name="reduce_swap_halves", in_specs=[HBM_SPEC] * nw, out_specs=[HBM_SPEC] * (2 * nw),
        out_shape=shapes + shapes,
        scratch_shapes=[pltpu.SemaphoreType.DMA((4 * nw,)), pltpu.SemaphoreType.DMA((4 * nw,)),
                        pltpu.SemaphoreType.DMA((4 * nw,))])(*grads)
    return res[:nw], res[nw:]


def _reduce_to_owner(parts):
    nw = len(BIG)

    def body(*refs):
        p, out = refs[:nw], refs[nw:2 * nw]
        send_sems, recv_sems, loc_sems = refs[2 * nw:]
        x, y, c, chips = _mesh_pos()
        me = 2 * x + y
        cps, loc = [], []
        for wi in range(nw):
            cp = pltpu.make_async_copy(p[wi].at[me], out[wi].at[me], loc_sems.at[wi])
            cp.start()
            loc.append(cp)
            for j, (qx, qy) in enumerate(chips):
                cp = _remote(p[wi].at[2 * qx + qy], out[wi].at[me], send_sems.at[wi * 3 + j],
                             recv_sems.at[wi * 3 + j], (qx, qy, c))
                cp.start()
                cps.append(cp)
        for wi in range(nw):
            for j, (qx, qy) in enumerate(chips):
                slot = out[wi].at[2 * qx + qy]
                _remote(slot, slot, send_sems.at[wi * 3 + j], recv_sems.at[wi * 3 + j], (qx, qy, c)).wait_recv()
        for cp in cps:
            cp.wait_send()
        for cp in loc:
            cp.wait()

    return pl.pallas_call(
        body, name="reduce_to_owner", in_specs=[HBM_SPEC] * nw, out_specs=[HBM_SPEC] * nw,
        out_shape=[jax.ShapeDtypeStruct(p.shape, p.dtype) for p in parts],
        scratch_shapes=[pltpu.SemaphoreType.DMA((3 * nw,)), pltpu.SemaphoreType.DMA((3 * nw,)),
                        pltpu.SemaphoreType.DMA((nw,))])(*parts)


def _share_with_sibling(halves):
    nw = len(BIG)

    def body(*refs):
        t, out = refs[:nw], refs[nw:2 * nw]
        send_sems, recv_sems, loc_sems = refs[2 * nw:]
        x, y, c, _ = _mesh_pos()
        sib = (x, y, 1 - c)
        cps, loc = [], []
        for wi, (_, k, n, ax) in enumerate(BIG):
            cp = pltpu.make_async_copy(t[wi], _shard_piece(out[wi], k, n, ax, c), loc_sems.at[wi])
            cp.start()
            loc.append(cp)
            cp = _remote(t[wi], _shard_piece(out[wi], k, n, ax, c), send_sems.at[wi], recv_sems.at[wi], sib)
            cp.start()
            cps.append(cp)
        for wi, (_, k, n, ax) in enumerate(BIG):
            piece = _shard_piece(out[wi], k, n, ax, 1 - c)
            _remote(piece, piece, send_sems.at[wi], recv_sems.at[wi], sib).wait_recv()
        for cp in cps:
            cp.wait_send()
        for cp in loc:
            cp.wait()

    shard_shapes = [(k, n // 4) if ax == 1 else (k // 4, n) for _, k, n, ax in BIG]
    return pl.pallas_call(
        body, name="share_with_sibling", in_specs=[HBM_SPEC] * nw, out_specs=[HBM_SPEC] * nw,
        out_shape=[jax.ShapeDtypeStruct(sh, F32) for sh in shard_shapes],
        scratch_shapes=[pltpu.SemaphoreType.DMA((nw,)), pltpu.SemaphoreType.DMA((nw,)),
                        pltpu.SemaphoreType.DMA((nw,))])(*halves)


def _allreduce_small(v):
    r = v.shape[0]

    def body(v_ref, o_ref, buf, send_sems, recv_sems):
        x, y, c, _ = _mesh_pos()
        me = 4 * x + 2 * y + c
        buf[me] = v_ref[...]
        cps = []
        for rel in range(1, 8):
            fx, fy, fc = (rel >> 2) & 1, (rel >> 1) & 1, rel & 1
            dev = (x ^ fx, y ^ fy, c ^ fc)
            cp = _remote(v_ref, buf.at[me], send_sems.at[rel - 1], recv_sems.at[rel - 1], dev)
            cp.start()
            cps.append(cp)
        for rel in range(1, 8):
            fx, fy, fc = (rel >> 2) & 1, (rel >> 1) & 1, rel & 1
            peer = 4 * (x ^ fx) + 2 * (y ^ fy) + (c ^ fc)
            _remote(v_ref, buf.at[peer], send_sems.at[rel - 1], recv_sems.at[rel - 1], (x, y, c)).wait_recv()
        for cp in cps:
            cp.wait_send()
        acc = buf[0]
        for d in range(1, 8):
            acc = acc + buf[d]
        o_ref[...] = acc

    return pl.pallas_call(
        body, name="allreduce_small", in_specs=[VMEM_SPEC], out_specs=VMEM_SPEC,
        out_shape=jax.ShapeDtypeStruct((r, 128), F32),
        scratch_shapes=[pltpu.VMEM((8, r, 128), F32), pltpu.SemaphoreType.DMA((7,)), pltpu.SemaphoreType.DMA((7,))],
        compiler_params=pltpu.CompilerParams(vmem_limit_bytes=VMEM_LIMIT))(v)


def _as2d(a):
    a = a.reshape((-1, a.shape[-1])) if a.ndim > 1 else a.reshape(1, -1)
    return a


def _step(inputs):
    x, mem, positions, target = inputs["x"][0], inputs["mem"][0], inputs["positions"], inputs["loss_target"][0]
    pos = positions.reshape(-1, 1)

    shards = []
    for name, _, _, _ in BIG:
        w2 = inputs[name][0]
        shards.append(_rowwise("cast_" + name, lambda w: (w,), [w2], [], [(w2.shape[1], MXU_DTYPE)])[0])
    full = _gather_weights(shards)
    wb = {name: full[i] for i, (name, _, _, _) in enumerate(BIG)}
    sp = {name: _as2d(inputs[name]) for name in SMALL}
    memb, = _rowwise("cast_mem", lambda m: (m,), [mem], [], [(D_MODEL, MXU_DTYPE)])

    loss, dx, gbig, gsmall = _local_step(x, memb, pos, target, sp, wb)

    own, got = _reduce_swap_halves([gbig[name] for name, _, _, _ in BIG])
    parts = []
    for i, (name, _, _, _) in enumerate(BIG):
        pr, pc = own[i].shape[1:]
        parts.append(_rowwise("pair_sum_" + name, lambda a, b: (a + b,), [own[i].reshape(4 * pr, pc),
                                                                         got[i].reshape(4 * pr, pc)], [],
                              [(pc, F32)], tm=128)[0].reshape(4, pr, pc))
    landed = _reduce_to_owner(parts)
    halves = []
    for i, (name, _, _, _) in enumerate(BIG):
        pc = landed[i].shape[2]
        halves.append(_rowwise("chip_sum_" + name, lambda a, b, c, d: (((a + b) + c) + d,),
                               [(landed[i], q) for q in range(4)], [], [(pc, F32)], tm=128)[0])
    gshard = _share_with_sibling(halves)

    out = {}
    for i, (name, _, _, _) in enumerate(BIG):
        w2, m2, v2 = inputs[name][0], inputs["m_" + name][0], inputs["v_" + name][0]
        n = w2.shape[1]
        d, nm, nv = _rowwise("adamw_" + name, _adamw, [w2, gshard[i], m2, v2], [], [(n, F32)] * 3, tm=128)
        lead = inputs[name].shape
        out[name] = (gshard[i].reshape(lead), d.reshape(lead), nm.reshape(lead), nv.reshape(lead))

    flat = [loss[:, :1].reshape(-1)] + [gsmall[name].reshape(-1) for name in SMALL]
    sizes = [f.shape[0] for f in flat]
    total = sum(sizes)
    rows = -(-total // 1024) * 8
    buf = jnp.concatenate(flat + [jnp.zeros((rows * 128 - total,), F32)]).reshape(rows, 128)
    red = _allreduce_small(buf)
    w_flat = jnp.concatenate([jnp.zeros((1,), F32)] + [inputs[name].reshape(-1) for name in SMALL]
                             + [jnp.zeros((rows * 128 - total,), F32)]).reshape(rows, 128)
    m_flat = jnp.concatenate([jnp.zeros((1,), F32)] + [inputs["m_" + name].reshape(-1) for name in SMALL]
                             + [jnp.zeros((rows * 128 - total,), F32)]).reshape(rows, 128)
    v_flat = jnp.concatenate([jnp.ones((1,), F32)] + [inputs["v_" + name].reshape(-1) for name in SMALL]
                             + [jnp.ones((rows * 128 - total,), F32)]).reshape(rows, 128)
    d_s, m_s, v_s = _rowwise("adamw_small", _adamw, [w_flat, red, m_flat, v_flat], [], [(128, F32)] * 3, tm=rows)
    red_f, d_f, m_f, v_f = red.reshape(-1), d_s.reshape(-1), m_s.reshape(-1), v_s.reshape(-1)
    off = sizes[0]
    for name, sz in zip(SMALL, sizes[1:]):
        shp = inputs[name].shape
        out[name] = tuple(t[off:off + sz].reshape(shp) for t in (red_f, d_f, m_f, v_f))
        off += sz
    loss_total = red_f[0]
    return loss_total, dx.reshape(inputs["x"].shape), out


_ARG_NAMES = (("x", "mem", "positions") + WEIGHT_ORDER + ("loss_target",) + tuple("m_" + n for n in WEIGHT_ORDER)
              + tuple("v_" + n for n in WEIGHT_ORDER))


def kernel(x, mem, positions, ln_in_g, ln_in_b, w_in, b_in, ssm_log_dt, ssm_a_re, ssm_a_im, ssm_b_re, ssm_b_im, ssm_c_re, ssm_c_im, ssm_d, w_glu, b_glu, w_att_up, w_mix_out, b_mix_out, ln1_g, ln1_b, w_xq, w_xkv, w_xo, ln2_g, ln2_b, w_ff1, b_ff1, w_ff2, b_ff2, ln3_g, ln3_b, loss_target, m_ln_in_g, m_ln_in_b, m_w_in, m_b_in, m_ssm_log_dt, m_ssm_a_re, m_ssm_a_im, m_ssm_b_re, m_ssm_b_im, m_ssm_c_re, m_ssm_c_im, m_ssm_d, m_w_glu, m_b_glu, m_w_att_up, m_w_mix_out, m_b_mix_out, m_ln1_g, m_ln1_b, m_w_xq, m_w_xkv, m_w_xo, m_ln2_g, m_ln2_b, m_w_ff1, m_b_ff1, m_w_ff2, m_b_ff2, m_ln3_g, m_ln3_b, v_ln_in_g, v_ln_in_b, v_w_in, v_b_in, v_ssm_log_dt, v_ssm_a_re, v_ssm_a_im, v_ssm_b_re, v_ssm_b_im, v_ssm_c_re, v_ssm_c_im, v_ssm_d, v_w_glu, v_b_glu, v_w_att_up, v_w_mix_out, v_b_mix_out, v_ln1_g, v_ln1_b, v_w_xq, v_w_xkv, v_w_xo, v_ln2_g, v_ln2_b, v_w_ff1, v_b_ff1, v_w_ff2, v_b_ff2, v_ln3_g, v_ln3_b):
    args = (x, mem, positions, ln_in_g, ln_in_b, w_in, b_in, ssm_log_dt, ssm_a_re, ssm_a_im, ssm_b_re, ssm_b_im, ssm_c_re, ssm_c_im, ssm_d, w_glu, b_glu, w_att_up, w_mix_out, b_mix_out, ln1_g, ln1_b, w_xq, w_xkv, w_xo, ln2_g, ln2_b, w_ff1, b_ff1, w_ff2, b_ff2, ln3_g, ln3_b, loss_target, m_ln_in_g, m_ln_in_b, m_w_in, m_b_in, m_ssm_log_dt, m_ssm_a_re, m_ssm_a_im, m_ssm_b_re, m_ssm_b_im, m_ssm_c_re, m_ssm_c_im, m_ssm_d, m_w_glu, m_b_glu, m_w_att_up, m_w_mix_out, m_b_mix_out, m_ln1_g, m_ln1_b, m_w_xq, m_w_xkv, m_w_xo, m_ln2_g, m_ln2_b, m_w_ff1, m_b_ff1, m_w_ff2, m_b_ff2, m_ln3_g, m_ln3_b, v_ln_in_g, v_ln_in_b, v_w_in, v_b_in, v_ssm_log_dt, v_ssm_a_re, v_ssm_a_im, v_ssm_b_re, v_ssm_b_im, v_ssm_c_re, v_ssm_c_im, v_ssm_d, v_w_glu, v_b_glu, v_w_att_up, v_w_mix_out, v_b_mix_out, v_ln1_g, v_ln1_b, v_w_xq, v_w_xkv, v_w_xo, v_ln2_g, v_ln2_b, v_w_ff1, v_b_ff1, v_w_ff2, v_b_ff2, v_ln3_g, v_ln3_b)
    assert len(args) == len(_ARG_NAMES)
    inputs = dict(zip(_ARG_NAMES, args))
    loss, dx, out = _step(inputs)
    res = [loss, dx]
    for k in range(4):
        res += [out[name][k] for name in WEIGHT_ORDER]
    return tuple(res)
```

```python
import functools
import math

import numpy as np
import jax
import jax.numpy as jnp
from jax import lax
from jax.experimental import pallas as pl
from jax.experimental.pallas import tpu as pltpu

F32 = jnp.float32
BF16 = jnp.bfloat16
MXU_DTYPE = jnp.bfloat16

D_MODEL = 1024
SSM_GROUP = 16
SSM_WIDTH = 768
SSM_GROUPS = 48
SSM_STATE = 64
N_STATE = SSM_GROUPS * SSM_STATE
SSM_CHUNKS = 6
CH_W = 128
CH_N = 512
ATT_HEAD_DIM = 64
ATT_HPG = 4
ATT_GROUPW = ATT_HPG * ATT_HEAD_DIM
DILATIONS = (1, 4, 16)
ATT_BLK = 128
ATT_SCALE = ATT_HEAD_DIM ** -0.5
ROT_DIM = 16
ROPE_THETA = 500000.0
XATT_HEADS = 4
XATT_HEAD_DIM = 256
XATT_SCALE = XATT_HEAD_DIM ** -0.5
D_FF = 4096
IN_COLS = 5120
DEEPNORM_ALPHA = 2.0 ** 0.25
LN_EPS = 1e-5
NEG_INF = -1e30
ADAM_LR = 0.001
ADAM_B1 = 0.9
ADAM_B2 = 0.999
ADAM_EPS = 1e-08
ADAM_WD = 0.01
ADAM_STEP = 10

N_SEG = 32
VMEM_LIMIT = 48 * 1024 * 1024
MESH = pl.DeviceIdType.MESH
HBM_SPEC = pl.BlockSpec(memory_space=pltpu.HBM)
VMEM_SPEC = pl.BlockSpec(memory_space=pltpu.VMEM)

BIG = (("w_in", 1024, 5120, 1), ("w_glu", 768, 2048, 1), ("w_att_up", 256, 1024, 1),
       ("w_mix_out", 1024, 1024, 0), ("w_xq", 1024, 1024, 0), ("w_xkv", 1024, 2048, 1),
       ("w_xo", 1024, 1024, 0), ("w_ff1", 1024, 4096, 1), ("w_ff2", 4096, 1024, 0))
SMALL = ("ln_in_g", "ln_in_b", "b_in", "ssm_log_dt", "ssm_a_re", "ssm_a_im", "ssm_b_re", "ssm_b_im",
         "ssm_c_re", "ssm_c_im", "ssm_d", "b_glu", "b_mix_out", "ln1_g", "ln1_b", "ln2_g", "ln2_b",
         "b_ff1", "b_ff2", "ln3_g", "ln3_b")
WEIGHT_ORDER = ("ln_in_g", "ln_in_b", "w_in", "b_in", "ssm_log_dt", "ssm_a_re", "ssm_a_im", "ssm_b_re",
                "ssm_b_im", "ssm_c_re", "ssm_c_im", "ssm_d", "w_glu", "b_glu", "w_att_up", "w_mix_out",
                "b_mix_out", "ln1_g", "ln1_b", "w_xq", "w_xkv", "w_xo", "ln2_g", "ln2_b", "w_ff1", "b_ff1",
                "w_ff2", "b_ff2", "ln3_g", "ln3_b")


def _cparams(n_axes):
    return pltpu.CompilerParams(dimension_semantics=("arbitrary",) * n_axes, vmem_limit_bytes=VMEM_LIMIT)


def _rowwise(name, fn, rows, consts, outs, reds=(), tm=256):
    n_rows = (rows[0][0] if isinstance(rows[0], tuple) else rows[0]).shape[-2]
    tm = min(tm, n_rows)
    assert n_rows % tm == 0, (name, n_rows, tm)
    specs, args = [], []
    for r in rows:
        if isinstance(r, tuple) and len(r) == 3:
            arr, width, cb = r
            specs.append(pl.BlockSpec((tm, width), functools.partial(lambda i, cb: (i, cb), cb=cb)))
        elif isinstance(r, tuple):
            arr, slot = r
            specs.append(pl.BlockSpec((None, tm, arr.shape[2]), functools.partial(lambda i, s: (s, i, 0), s=slot)))
        else:
            arr = r
            specs.append(pl.BlockSpec((tm, arr.shape[1]), lambda i: (i, 0)))
        args.append(arr)
        assert arr.shape[-2] == n_rows, (name, arr.shape, n_rows)
    for cst in consts:
        specs.append(pl.BlockSpec(cst.shape, lambda i: (0, 0)))
        args.append(cst)
    n_r, n_c, n_o, n_d = len(rows), len(consts), len(outs), len(reds)
    out_shape = [jax.ShapeDtypeStruct((n_rows, c), dt) for c, dt in outs]
    out_specs = [pl.BlockSpec((tm, c), lambda i: (i, 0)) for c, _ in outs]
    out_shape += [jax.ShapeDtypeStruct((1, c), F32) for c in reds]
    out_specs += [pl.BlockSpec((1, c), lambda i: (0, 0)) for c in reds]

    def body(*refs):
        ins = [r[...] for r in refs[:n_r + n_c]]
        o_refs = refs[n_r + n_c:n_r + n_c + n_o]
        d_refs = refs[n_r + n_c + n_o:]
        res = fn(*ins)
        res = res if isinstance(res, (tuple, list)) else (res,)
        assert len(res) == n_o + n_d, (name, len(res))
        for ref, val in zip(o_refs, res[:n_o]):
            ref[...] = val.astype(ref.dtype)
        first = pl.program_id(0) == 0
        for ref, val in zip(d_refs, res[n_o:]):
            @pl.when(first)
            def _(ref=ref, val=val):
                ref[...] = val

            @pl.when(jnp.logical_not(first))
            def _(ref=ref, val=val):
                ref[...] += val

    res = pl.pallas_call(body, name=name, grid=(n_rows // tm,), in_specs=specs, out_specs=out_specs,
                         out_shape=out_shape, compiler_params=_cparams(1))(*args)
    return res


def _colsum(v):
    return jnp.sum(v.astype(F32), axis=0, keepdims=True)


_DIMS = {"nn": (((1,), (0,)), ((), ())), "nt": (((1,), (1,)), ((), ())), "tn": (((0,), (0,)), ((), ()))}


def _tile(dim, want):
    if dim <= want:
        return dim
    return max(t for t in range(128, want + 1, 128) if dim % t == 0)


def _dot(a, b, mode):
    return lax.dot_general(a.astype(MXU_DTYPE), b.astype(MXU_DTYPE), _DIMS[mode], preferred_element_type=F32)


def _mm(name, a, b, mode, *, bias=None, extras=(), epilogue=None, out_dtypes=(F32,), tm=1024, tn=1024, tk=1024):
    if mode == "nn":
        (m, k), (_, n) = a.shape, b.shape
    elif mode == "nt":
        (m, k), (n, _) = a.shape, b.shape
    else:
        (k, m), (_, n) = a.shape, b.shape
    tm, tn = _tile(m, tm), _tile(n, tn)
    if mode != "tn":
        tk = k if k <= 1024 else tk
    tk = _tile(k, tk)
    assert m % tm == 0 and n % tn == 0 and k % tk == 0, (name, m, n, k)
    nk = k // tk
    a_spec = {"nn": pl.BlockSpec((tm, tk), lambda i, j, kk: (i, kk)),
              "nt": pl.BlockSpec((tm, tk), lambda i, j, kk: (i, kk)),
              "tn": pl.BlockSpec((tk, tm), lambda i, j, kk: (kk, i))}[mode]
    b_spec = {"nn": pl.BlockSpec((tk, tn), lambda i, j, kk: (kk, j)),
              "nt": pl.BlockSpec((tn, tk), lambda i, j, kk: (j, kk)),
              "tn": pl.BlockSpec((tk, tn), lambda i, j, kk: (kk, j))}[mode]
    specs, args = [a_spec, b_spec], [a, b]
    if bias is not None:
        specs.append(pl.BlockSpec((1, tn), lambda i, j, kk: (0, j)))
        args.append(bias)
    for e in extras:
        specs.append(pl.BlockSpec((tm, tn), lambda i, j, kk: (i, j)))
        args.append(e)
    n_e, n_o = len(extras), len(out_dtypes)
    has_bias = bias is not None

    def body(*refs):
        a_ref, b_ref = refs[0], refs[1]
        pos = 2
        bias_ref = refs[pos] if has_bias else None
        pos += int(has_bias)
        e_refs = refs[pos:pos + n_e]
        o_refs = refs[pos + n_e:pos + n_e + n_o]
        acc_ref = refs[pos + n_e + n_o] if nk > 1 else None
        part = _dot(a_ref[...], b_ref[...], mode)

        def finish(r):
            if has_bias:
                r = r + bias_ref[...]
            res = epilogue(r, *[e[...] for e in e_refs]) if epilogue is not None else (r,)
            for ref, val in zip(o_refs, res):
                ref[...] = val.astype(ref.dtype)

        if nk == 1:
            finish(part)
        else:
            kk = pl.program_id(2)

            @pl.when(kk == 0)
            def _():
                acc_ref[...] = part

            @pl.when(kk > 0)
            def _():
                acc_ref[...] += part

            @pl.when(kk == nk - 1)
            def _():
                finish(acc_ref[...])

    res = pl.pallas_call(
        body, name=name, grid=(m // tm, n // tn, nk), in_specs=specs,
        out_specs=[pl.BlockSpec((tm, tn), lambda i, j, kk: (i, j)) for _ in out_dtypes],
        out_shape=[jax.ShapeDtypeStruct((m, n), dt) for dt in out_dtypes],
        scratch_shapes=[pltpu.VMEM((tm, tn), F32)] if nk > 1 else [],
        compiler_params=_cparams(3))(*args)
    return res[0] if n_o == 1 else res


def _ssm_expand(name, a, bmat, mode, tm=512):
    s = a.shape[0]
    tm = min(tm, s)

    def body(a_ref, b_ref, re_ref, im_ref):
        r = _dot(a_ref[...], b_ref[...], mode)
        re_ref[...] = r[:, :CH_N]
        im_ref[...] = r[:, CH_N:]

    return pl.pallas_call(
        body, name=name, grid=(s // tm, SSM_CHUNKS),
        in_specs=[pl.BlockSpec((tm, CH_W), lambda i, j: (i, j)),
                  pl.BlockSpec((None,) + bmat.shape[1:], lambda i, j: (j, 0, 0))],
        out_specs=[pl.BlockSpec((tm, CH_N), lambda i, j: (i, j))] * 2,
        out_shape=[jax.ShapeDtypeStruct((s, N_STATE), F32)] * 2,
        compiler_params=_cparams(2))(a, bmat)


def _ssm_contract(name, a_re, a_im, bmat, mode, d_row, extra, tm=512):
    s = a_re.shape[0]
    tm = min(tm, s)

    def body(re_ref, im_ref, b_ref, d_ref, e_ref, o_ref):
        b = b_ref[...]
        if mode == "nn":
            r = _dot(re_ref[...], b[:CH_N], "nn") + _dot(im_ref[...], b[CH_N:], "nn")
        else:
            r = _dot(re_ref[...], b[:, :CH_N], "nt") + _dot(im_ref[...], b[:, CH_N:], "nt")
        o_ref[...] = r + d_ref[...] * e_ref[...]

    return pl.pallas_call(
        body, name=name, grid=(s // tm, SSM_CHUNKS),
        in_specs=[pl.BlockSpec((tm, CH_N), lambda i, j: (i, j)), pl.BlockSpec((tm, CH_N), lambda i, j: (i, j)),
                  pl.BlockSpec((None,) + bmat.shape[1:], lambda i, j: (j, 0, 0)),
                  pl.BlockSpec((1, CH_W), lambda i, j: (0, j)), pl.BlockSpec((tm, CH_W), lambda i, j: (i, j))],
        out_specs=pl.BlockSpec((tm, CH_W), lambda i, j: (i, j)),
        out_shape=jax.ShapeDtypeStruct((s, SSM_WIDTH), F32),
        compiler_params=_cparams(2))(a_re, a_im, bmat, d_row, extra)


def _ssm_wgrad(name, chan, st_re, st_im, expand, tk=512):
    s = chan.shape[0]
    tk = min(tk, s)
    nk = s // tk
    oshape = (CH_W, 2 * CH_N) if expand else (2 * CH_N, CH_W)

    def body(c_ref, re_ref, im_ref, o_ref):
        c = c_ref[...]
        if expand:
            part = jnp.concatenate([_dot(c, re_ref[...], "tn"), _dot(c, im_ref[...], "tn")], axis=1)
        else:
            part = jnp.concatenate([_dot(re_ref[...], c, "tn"), _dot(im_ref[...], c, "tn")], axis=0)
        kk = pl.program_id(1)

        @pl.when(kk == 0)
        def _():
            o_ref[...] = part

        @pl.when(kk > 0)
        def _():
            o_ref[...] += part

    return pl.pallas_call(
        body, name=name, grid=(SSM_CHUNKS, nk),
        in_specs=[pl.BlockSpec((tk, CH_W), lambda j, kk: (kk, j)), pl.BlockSpec((tk, CH_N), lambda j, kk: (kk, j)),
                  pl.BlockSpec((tk, CH_N), lambda j, kk: (kk, j))],
        out_specs=pl.BlockSpec((None,) + oshape, lambda j, kk: (j, 0, 0)),
        out_shape=jax.ShapeDtypeStruct((SSM_CHUNKS,) + oshape, F32),
        compiler_params=_cparams(2))(chan, st_re, st_im)


SCAN_LB = 256


def _ssm_scan(name, w_re, w_im, a_re, a_im, reverse):
    s = w_re.shape[0]
    seg_len = s // N_SEG
    n_sq = int(math.log2(seg_len))
    assert 2 ** n_sq == seg_len

    def body(are_ref, aim_ref, wre_ref, wim_ref, hre_ref, him_ref, ere, eim, cre, cim):
        ar1 = are_ref[...]
        ai1 = -aim_ref[...] if reverse else aim_ref[...]
        ar = jnp.broadcast_to(ar1, (N_SEG, SCAN_LB))
        ai = jnp.broadcast_to(ai1, (N_SEG, SCAN_LB))

        def rows_of(k):
            kk = seg_len - 1 - k if reverse else k
            return pl.ds(pl.multiple_of(kk * N_SEG, N_SEG), N_SEG)

        def local(k, carry):
            hr, hi = carry
            rows = rows_of(k)
            nr = ar * hr - ai * hi + wre_ref[rows, :]
            ni = ar * hi + ai * hr + wim_ref[rows, :]
            hre_ref[rows, :] = nr
            him_ref[rows, :] = ni
            return nr, ni

        zero = jnp.zeros((N_SEG, SCAN_LB), F32)
        er, ei = lax.fori_loop(0, seg_len, local, (zero, zero))
        ere[...] = er
        eim[...] = ei
        pr, pi = ar1, ai1
        for _ in range(n_sq):
            pr, pi = pr * pr - pi * pi, 2.0 * pr * pi
        cr = jnp.zeros((1, SCAN_LB), F32)
        ci = jnp.zeros((1, SCAN_LB), F32)
        for jj in range(N_SEG):
            j = N_SEG - 1 - jj if reverse else jj
            cre[j:j + 1, :] = cr
            cim[j:j + 1, :] = ci
            er_j, ei_j = ere[j:j + 1, :], eim[j:j + 1, :]
            cr, ci = pr * cr - pi * ci + er_j, pr * ci + pi * cr + ei_j
        c_r, c_i = cre[...], cim[...]

        def fix(k, carry):
            qr, qi = carry
            rows = rows_of(k)
            hre_ref[rows, :] = hre_ref[rows, :] + (qr * c_r - qi * c_i)
            him_ref[rows, :] = him_ref[rows, :] + (qr * c_i + qi * c_r)
            return qr * ar - qi * ai, qr * ai + qi * ar

        lax.fori_loop(0, seg_len, fix, (ar, ai))

    nblk = N_STATE // SCAN_LB
    blk = pl.BlockSpec((s, SCAN_LB), lambda b: (0, b))
    row = pl.BlockSpec((1, SCAN_LB), lambda b: (0, b))
    return pl.pallas_call(
        body, name=name, grid=(nblk,), in_specs=[row, row, blk, blk], out_specs=[blk, blk],
        out_shape=[jax.ShapeDtypeStruct((s, N_STATE), F32)] * 2,
        scratch_shapes=[pltpu.VMEM((N_SEG, SCAN_LB), F32)] * 4,
        compiler_params=_cparams(1))(a_re, a_im, w_re, w_im)


def _ssm_da(g_re, g_im, h_re, h_im):
    s = g_re.shape[0]
    seg_len = s // N_SEG

    def body(gre_ref, gim_ref, hre_ref, him_ref, dre_ref, dim_ref):
        def rows_of(k):
            return pl.ds(pl.multiple_of(k * N_SEG, N_SEG), N_SEG)

        def step(k, carry):
            sr, si = carry
            gr, gi = gre_ref[rows_of(k), :], gim_ref[rows_of(k), :]
            pr, pi = hre_ref[rows_of(k - 1), :], him_ref[rows_of(k - 1), :]
            return sr + gr * pr + gi * pi, si + gi * pr - gr * pi

        zero = jnp.zeros((N_SEG, SCAN_LB), F32)
        sr, si = lax.fori_loop(1, seg_len, step, (zero, zero))
        last = pl.ds((seg_len - 1) * N_SEG, N_SEG)
        first_row = lax.broadcasted_iota(jnp.int32, (N_SEG, SCAN_LB), 0) == 0
        pr = jnp.where(first_row, 0.0, pltpu.roll(hre_ref[last, :], 1, 0))
        pi = jnp.where(first_row, 0.0, pltpu.roll(him_ref[last, :], 1, 0))
        gr, gi = gre_ref[pl.ds(0, N_SEG), :], gim_ref[pl.ds(0, N_SEG), :]
        sr = sr + gr * pr + gi * pi
        si = si + gi * pr - gr * pi
        dre_ref[...] = jnp.sum(sr, axis=0, keepdims=True)
        dim_ref[...] = jnp.sum(si, axis=0, keepdims=True)

    nblk = N_STATE // SCAN_LB
    blk = pl.BlockSpec((s, SCAN_LB), lambda b: (0, b))
    row = pl.BlockSpec((1, SCAN_LB), lambda b: (0, b))
    return pl.pallas_call(
        body, name="ssm_da", grid=(nblk,), in_specs=[blk] * 4, out_specs=[row, row],
        out_shape=[jax.ShapeDtypeStruct((1, N_STATE), F32)] * 2,
        compiler_params=_cparams(1))(g_re, g_im, h_re, h_im)


def _disc(ldt, are, aim, bre, bim):
    dt = jnp.exp(ldt)
    mag = jnp.exp(are * dt)
    abr = mag * jnp.cos(aim * dt)
    abi = mag * jnp.sin(aim * dt)
    den = jnp.square(are) + jnp.square(aim)
    nr = abr - 1.0
    fre = (nr * are + abi * aim) / den
    fim = (abi * are - nr * aim) / den
    return abr, abi, fre * bre - fim * bim, fre * bim + fim * bre


def _ssm_disc_fwd(ldt, are, aim, bre, bim):
    def body(l_ref, ar_ref, ai_ref, br_ref, bi_ref, o0, o1, o2, o3):
        res = _disc(l_ref[...], ar_ref[...], ai_ref[...], br_ref[...], bi_ref[...])
        for ref, val in zip((o0, o1, o2, o3), res):
            ref[...] = val

    col = jax.ShapeDtypeStruct((N_STATE, 1), F32)
    mat = jax.ShapeDtypeStruct((N_STATE, SSM_GROUP), F32)
    return pl.pallas_call(body, name="ssm_disc_fwd", out_shape=[col, col, mat, mat],
                          in_specs=[VMEM_SPEC] * 5, out_specs=[VMEM_SPEC] * 4)(ldt, are, aim, bre, bim)


def _ssm_disc_bwd(ldt, are, aim, bre, bim, d_abr, d_abi, d_bbr, d_bbi):
    def body(l_ref, ar_ref, ai_ref, br_ref, bi_ref, c0, c1, c2, c3, g_ldt, g_are, g_aim, g_bre, g_bim):
        _, vjp = jax.vjp(_disc, l_ref[...], ar_ref[...], ai_ref[...], br_ref[...], bi_ref[...])
        dl, dar, dai, dbr, dbi = vjp((c0[...], c1[...], c2[...], c3[...]))
        state = lax.broadcasted_iota(jnp.int32, (N_STATE, SSM_GROUPS), 0)
        group = lax.broadcasted_iota(jnp.int32, (N_STATE, SSM_GROUPS), 1)
        pick = jnp.right_shift(state, 6) == group
        g_ldt[...] = jnp.sum(jnp.where(pick, dl, 0.0), axis=0, keepdims=True)
        g_are[...] = dar
        g_aim[...] = dai
        g_bre[...] = dbr
        g_bim[...] = dbi

    col = jax.ShapeDtypeStruct((N_STATE, 1), F32)
    mat = jax.ShapeDtypeStruct((N_STATE, SSM_GROUP), F32)
    return pl.pallas_call(body, name="ssm_disc_bwd",
                          out_shape=[jax.ShapeDtypeStruct((1, SSM_GROUPS), F32), col, col, mat, mat],
                          in_specs=[VMEM_SPEC] * 9, out_specs=[VMEM_SPEC] * 5,
                          compiler_params=pltpu.CompilerParams(vmem_limit_bytes=VMEM_LIMIT))(
        ldt, are, aim, bre, bim, d_abr, d_abi, d_bbr, d_bbi)


_EYE8 = np.eye(8, dtype=np.float32)


def _blockdiag_b(bb):
    t = bb.reshape(SSM_CHUNKS, 8, SSM_STATE, SSM_GROUP).transpose(0, 1, 3, 2)
    return jnp.einsum("igcn,gh->igchn", t, _EYE8).reshape(SSM_CHUNKS, CH_W, CH_N)


def _diag_of_b(m):
    t = jnp.einsum("igchn,gh->igcn", m.reshape(SSM_CHUNKS, 8, SSM_GROUP, 8, SSM_STATE), _EYE8)
    return t.transpose(0, 1, 3, 2).reshape(N_STATE, SSM_GROUP)


def _blockdiag_c(c):
    t = c.reshape(SSM_CHUNKS, 8, SSM_GROUP, SSM_STATE).transpose(0, 1, 3, 2)
    return jnp.einsum("ignc,gh->ignhc", t, _EYE8).reshape(SSM_CHUNKS, CH_N, CH_W)


def _diag_of_c(m):
    t = jnp.einsum("ignhc,gh->ignc", m.reshape(SSM_CHUNKS, 8, SSM_STATE, 8, SSM_GROUP), _EYE8)
    return t.transpose(0, 1, 3, 2).reshape(SSM_GROUPS, SSM_GROUP, SSM_STATE)


def _time_perm(a):
    s, c = a.shape
    return a.reshape(N_SEG, s // N_SEG, c).transpose(1, 0, 2).reshape(s, c)


def _time_unperm(a):
    s, c = a.shape
    return a.reshape(s // N_SEG, N_SEG, c).transpose(1, 0, 2).reshape(s, c)


def _dilate(a, d):
    s, c = a.shape
    return a if d == 1 else a.reshape(s // d, d, c).transpose(1, 0, 2).reshape(s, c)


def _undilate(a, d):
    s, c = a.shape
    return a if d == 1 else a.reshape(d, s // d, c).transpose(1, 0, 2).reshape(s, c)


def _stack_dilated(parts):
    return jnp.stack([_dilate(p, d) for p, d in zip(parts, DILATIONS)], axis=0)


def _blocks_per_seq(g, n_blocks):
    return jnp.right_shift(jnp.int32(n_blocks), 2 * g)


def _attn_fwd(q, k, v):
    s = q.shape[1]
    nb = s // ATT_BLK

    def body(q_ref, kc_ref, kp_ref, vc_ref, vp_ref, o_ref, lse_ref):
        g, b = pl.program_id(0), pl.program_id(1)
        has_prev = lax.rem(b, _blocks_per_seq(g, nb)) > 0
        qi = lax.broadcasted_iota(jnp.int32, (ATT_BLK, 2 * ATT_BLK), 0)
        ki = lax.broadcasted_iota(jnp.int32, (ATT_BLK, 2 * ATT_BLK), 1)
        steps = qi + ATT_BLK - ki
        first_key = jnp.where(has_prev, 0, ATT_BLK)
        valid = (steps >= 0) & (steps <= ATT_BLK) & (ki >= first_key)
        for h in range(ATT_HPG):
            sl = slice(h * ATT_HEAD_DIM, (h + 1) * ATT_HEAD_DIM)
            kcat = jnp.concatenate([kp_ref[:, sl], kc_ref[:, sl]], axis=0)
            vcat = jnp.concatenate([vp_ref[:, sl], vc_ref[:, sl]], axis=0)
            sc = _dot(q_ref[:, sl], kcat, "nt") * ATT_SCALE
            sc = jnp.where(valid, sc, NEG_INF)
            m = jnp.max(sc, axis=-1, keepdims=True)
            p = jnp.exp(sc - m)
            den = jnp.sum(p, axis=-1, keepdims=True)
            o_ref[:, sl] = _dot(p, vcat, "nn") / den
            lse_ref[:, sl] = jnp.broadcast_to(m + jnp.log(den), (ATT_BLK, ATT_HEAD_DIM))

    cur = pl.BlockSpec((None, ATT_BLK, ATT_GROUPW), lambda g, b: (g, b, 0))
    prev = pl.BlockSpec((None, ATT_BLK, ATT_GROUPW), lambda g, b: (g, jnp.maximum(b - 1, 0), 0))
    return pl.pallas_call(
        body, name="attn_fwd", grid=(3, nb), in_specs=[cur, cur, prev, cur, prev], out_specs=[cur, cur],
        out_shape=[jax.ShapeDtypeStruct((3, s, ATT_GROUPW), F32)] * 2,
        compiler_params=_cparams(2))(q, k, k, v, v)


def _attn_dq(q, k, v, do, lse, delta):
    s = q.shape[1]
    nb = s // ATT_BLK

    def body(q_ref, kc_ref, kp_ref, vc_ref, vp_ref, do_ref, lse_ref, dl_ref, dq_ref):
        g, b = pl.program_id(0), pl.program_id(1)
        has_prev = lax.rem(b, _blocks_per_seq(g, nb)) > 0
        qi = lax.broadcasted_iota(jnp.int32, (ATT_BLK, 2 * ATT_BLK), 0)
        ki = lax.broadcasted_iota(jnp.int32, (ATT_BLK, 2 * ATT_BLK), 1)
        steps = qi + ATT_BLK - ki
        first_key = jnp.where(has_prev, 0, ATT_BLK)
        valid = (steps >= 0) & (steps <= ATT_BLK) & (ki >= first_key)
        for h in range(ATT_HPG):
            sl = slice(h * ATT_HEAD_DIM, (h + 1) * ATT_HEAD_DIM)
            one = slice(h * ATT_HEAD_DIM, h * ATT_HEAD_DIM + 1)
            kcat = jnp.concatenate([kp_ref[:, sl], kc_ref[:, sl]], axis=0)
            vcat = jnp.concatenate([vp_ref[:, sl], vc_ref[:, sl]], axis=0)
            sc = _dot(q_ref[:, sl], kcat, "nt") * ATT_SCALE
            p = jnp.exp(jnp.where(valid, sc, NEG_INF) - lse_ref[:, one])
            dp = _dot(do_ref[:, sl], vcat, "nt")
            ds = p * (dp - dl_ref[:, one]) * ATT_SCALE
            dq_ref[:, sl] = _dot(ds, kcat, "nn")

    cur = pl.BlockSpec((None, ATT_BLK, ATT_GROUPW), lambda g, b: (g, b, 0))
    prev = pl.BlockSpec((None, ATT_BLK, ATT_GROUPW), lambda g, b: (g, jnp.maximum(b - 1, 0), 0))
    return pl.pallas_call(
        body, name="attn_dq", grid=(3, nb), in_specs=[cur, cur, prev, cur, prev, cur, cur, cur], out_specs=cur,
        out_shape=jax.ShapeDtypeStruct((3, s, ATT_GROUPW), F32),
        compiler_params=_cparams(2))(q, k, k, v, v, do, lse, delta)


def _attn_dkv(q, k, v, do, lse, delta):
    s = q.shape[1]
    nb = s // ATT_BLK

    def body(k_ref, v_ref, qc_ref, qn_ref, doc_ref, don_ref, lc_ref, ln_ref, dc_ref, dn_ref, dk_ref, dv_ref):
        g, b = pl.program_id(0), pl.program_id(1)
        next_uses = (b + 1 < nb) & (lax.rem(b + 1, _blocks_per_seq(g, nb)) > 0)
        ri = lax.broadcasted_iota(jnp.int32, (2 * ATT_BLK, ATT_BLK), 0)
        ki = lax.broadcasted_iota(jnp.int32, (2 * ATT_BLK, ATT_BLK), 1)
        reach = jnp.where(next_uses, 0, 4 * ATT_BLK)
        valid = ((ri < ATT_BLK) & (ri >= ki)) | ((ri >= ATT_BLK) & (ki - ri + ATT_BLK >= reach))
        for h in range(ATT_HPG):
            sl = slice(h * ATT_HEAD_DIM, (h + 1) * ATT_HEAD_DIM)
            one = slice(h * ATT_HEAD_DIM, h * ATT_HEAD_DIM + 1)
            qcat = jnp.concatenate([qc_ref[:, sl], qn_ref[:, sl]], axis=0)
            docat = jnp.concatenate([doc_ref[:, sl], don_ref[:, sl]], axis=0)
            lcat = jnp.concatenate([lc_ref[:, one], ln_ref[:, one]], axis=0)
            dcat = jnp.concatenate([dc_ref[:, one], dn_ref[:, one]], axis=0)
            sc = _dot(qcat, k_ref[:, sl], "nt") * ATT_SCALE
            p = jnp.exp(jnp.where(valid, sc, NEG_INF) - lcat)
            dv_ref[:, sl] = _dot(p, docat, "tn")
            dp = _dot(docat, v_ref[:, sl], "nt")
            ds = p * (dp - dcat) * ATT_SCALE
            dk_ref[:, sl] = _dot(ds, qcat, "tn")

    cur = pl.BlockSpec((None, ATT_BLK, ATT_GROUPW), lambda g, b: (g, b, 0))
    nxt = pl.BlockSpec((None, ATT_BLK, ATT_GROUPW), lambda g, b: (g, jnp.minimum(b + 1, nb - 1), 0))
    return pl.pallas_call(
        body, name="attn_dkv", grid=(3, nb), in_specs=[cur, cur, cur, nxt, cur, nxt, cur, nxt, cur, nxt],
        out_specs=[cur, cur], out_shape=[jax.ShapeDtypeStruct((3, s, ATT_GROUPW), F32)] * 2,
        compiler_params=_cparams(2))(k, v, q, q, do, do, lse, lse, delta, delta)


def _xattn_probs(q, kh):
    sc = _dot(q, kh, "nt") * XATT_SCALE
    e = jnp.exp(sc - jnp.max(sc, axis=-1, keepdims=True))
    return e / jnp.sum(e, axis=-1, keepdims=True)


def _xattn_fwd(q, kv, tm=512):
    s = q.shape[0]
    tm = min(tm, s)

    def body(q_ref, kv_ref, o_ref):
        for h in range(XATT_HEADS):
            sl = slice(h * XATT_HEAD_DIM, (h + 1) * XATT_HEAD_DIM)
            vs = slice(D_MODEL + h * XATT_HEAD_DIM, D_MODEL + (h + 1) * XATT_HEAD_DIM)
            p = _xattn_probs(q_ref[:, sl], kv_ref[:, sl])
            o_ref[:, sl] = _dot(p, kv_ref[:, vs], "nn").astype(o_ref.dtype)

    return pl.pallas_call(
        body, name="xattn_fwd", grid=(s // tm,),
        in_specs=[pl.BlockSpec((tm, D_MODEL), lambda i: (i, 0)), pl.BlockSpec(kv.shape, lambda i: (0, 0))],
        out_specs=pl.BlockSpec((tm, D_MODEL), lambda i: (i, 0)),
        out_shape=jax.ShapeDtypeStruct((s, D_MODEL), MXU_DTYPE), compiler_params=_cparams(1))(q, kv)


def _xattn_bwd(q, kv, do, tm=512):
    s = q.shape[0]
    tm = min(tm, s)

    def body(q_ref, kv_ref, do_ref, dq_ref, dkv_ref):
        first = pl.program_id(0) == 0

        @pl.when(first)
        def _():
            dkv_ref[...] = jnp.zeros_like(dkv_ref)

        for h in range(XATT_HEADS):
            sl = slice(h * XATT_HEAD_DIM, (h + 1) * XATT_HEAD_DIM)
            vs = slice(D_MODEL + h * XATT_HEAD_DIM, D_MODEL + (h + 1) * XATT_HEAD_DIM)
            p = _xattn_probs(q_ref[:, sl], kv_ref[:, sl])
            dkv_ref[:, vs] += _dot(p, do_ref[:, sl], "tn")
            dp = _dot(do_ref[:, sl], kv_ref[:, vs], "nt")
            ds = p * (dp - jnp.sum(dp * p, axis=-1, keepdims=True)) * XATT_SCALE
            dq_ref[:, sl] = _dot(ds, kv_ref[:, sl], "nn").astype(dq_ref.dtype)
            dkv_ref[:, sl] += _dot(ds, q_ref[:, sl], "tn")

    row = pl.BlockSpec((tm, D_MODEL), lambda i: (i, 0))
    whole = pl.BlockSpec(kv.shape, lambda i: (0, 0))
    return pl.pallas_call(
        body, name="xattn_bwd", grid=(s // tm,), in_specs=[row, whole, row], out_specs=[row, whole],
        out_shape=[jax.ShapeDtypeStruct((s, D_MODEL), MXU_DTYPE), jax.ShapeDtypeStruct(kv.shape, F32)],
        compiler_params=_cparams(1))(q, kv, do)


def _ln(x, g, b):
    mu = jnp.mean(x, axis=-1, keepdims=True)
    xc = x - mu
    var = jnp.mean(jnp.square(xc), axis=-1, keepdims=True)
    return xc * lax.rsqrt(var + LN_EPS) * g + b


def _res_ln(h, o, g, b):
    return _ln(DEEPNORM_ALPHA * h + o, g, b)


def _gate(gs, ga, z1, z2, batt):
    return jax.nn.sigmoid(gs) * (z1 * jax.nn.sigmoid(z2)) + jax.nn.sigmoid(ga) * batt


def _rope_tables(pos, invf, m1, m2):
    ang = pos.astype(F32) * invf
    sin = jnp.sin(ang)
    return jnp.cos(ang), -sin * m1, sin * m2


def _rope(t, cos, s_up, s_dn):
    w = t.shape[-1]
    return t * cos + pltpu.roll(t, w - ROT_DIM // 2, 1) * s_up + pltpu.roll(t, ROT_DIM // 2, 1) * s_dn


def _rope_t(dt, cos, s_up, s_dn):
    w = dt.shape[-1]
    return dt * cos + pltpu.roll(dt * s_up, ROT_DIM // 2, 1) + pltpu.roll(dt * s_dn, w - ROT_DIM // 2, 1)


def _rope_consts():
    inv_freq = ROPE_THETA ** (-jnp.arange(0, ROT_DIM, 2, dtype=F32) / ROT_DIM)
    d = np.arange(ATT_GROUPW) % ATT_HEAD_DIM
    invf = jnp.where(d < ROT_DIM, inv_freq[d % (ROT_DIM // 2)], 0.0).reshape(1, ATT_GROUPW).astype(F32)
    m1 = jnp.asarray((d < ROT_DIM // 2).astype(np.float32)).reshape(1, ATT_GROUPW)
    m2 = jnp.asarray(((d >= ROT_DIM // 2) & (d < ROT_DIM)).astype(np.float32)).reshape(1, ATT_GROUPW)
    return invf, m1, m2


def _head_sum_matrix():
    d = np.arange(ATT_GROUPW) // ATT_HEAD_DIM
    return jnp.asarray((d[:, None] == d[None, :]).astype(np.float32))


def _adamw(w, g, m, v):
    m = ADAM_B1 * m + (1.0 - ADAM_B1) * g
    v = ADAM_B2 * v + (1.0 - ADAM_B2) * jnp.square(g)
    m_hat = m / (1.0 - ADAM_B1 ** ADAM_STEP)
    v_hat = v / (1.0 - ADAM_B2 ** ADAM_STEP)
    delta = -ADAM_LR * (m_hat / (jnp.sqrt(v_hat) + ADAM_EPS) + ADAM_WD * w)
    return delta, m, v


def _local_step(x, mem, pos, target, sp, wb):
    s = x.shape[0]
    al = DEEPNORM_ALPHA
    mx = MXU_DTYPE

    h0, h0b = _rowwise("ln_in", lambda x, g, b: (lambda h: (h, h))(_ln(x, g, b)), [x],
                       [sp["ln_in_g"], sp["ln_in_b"]], [(D_MODEL, F32), (D_MODEL, mx)])
    proj = _mm("proj", h0b, wb["w_in"], "nn", bias=sp["b_in"])

    ldt = jnp.repeat(sp["ssm_log_dt"].reshape(SSM_GROUPS), SSM_STATE).reshape(N_STATE, 1)
    are, aim = sp["ssm_a_re"].reshape(N_STATE, 1), sp["ssm_a_im"].reshape(N_STATE, 1)
    bre, bim = sp["ssm_b_re"].reshape(N_STATE, SSM_GROUP), sp["ssm_b_im"].reshape(N_STATE, SSM_GROUP)
    abr, abi, bbr, bbi = _ssm_disc_fwd(ldt, are, aim, bre, bim)
    a_re, a_im = abr.reshape(1, N_STATE), abi.reshape(1, N_STATE)
    bexp = jnp.concatenate([_blockdiag_b(bbr), _blockdiag_b(bbi)], axis=2).astype(mx)
    cexp = jnp.concatenate([_blockdiag_c(sp["ssm_c_re"].reshape(SSM_GROUPS, SSM_GROUP, SSM_STATE)),
                            -_blockdiag_c(sp["ssm_c_im"].reshape(SSM_GROUPS, SSM_GROUP, SSM_STATE))],
                           axis=1).astype(mx)
    u_p = _time_perm(proj[:, :SSM_WIDTH])
    w_re, w_im = _ssm_expand("ssm_bu", u_p, bexp, "nn")
    h_re, h_im = _ssm_scan("ssm_scan_fwd", w_re, w_im, a_re, a_im, reverse=False)
    y_p = _ssm_contract("ssm_ch", h_re, h_im, cexp, "nn", sp["ssm_d"], u_p)
    y = _time_unperm(y_p)
    ygb, = _rowwise("gelu", lambda y: jax.nn.gelu(y), [y], [], [(SSM_WIDTH, mx)])
    z = _mm("glu", ygb, wb["w_glu"], "nn", bias=sp["b_glu"])

    invf, m1, m2 = _rope_consts()

    def rope_fwd(pos, q0, q1, q2, k0, k1, k2, v0, v1, v2, invf, m1, m2):
        tabs = _rope_tables(pos, invf, m1, m2)
        return tuple(_rope(t, *tabs) for t in (q0, q1, q2, k0, k1, k2)) + (v0, v1, v2)

    qkv_cols = [(proj, ATT_GROUPW, 3 + i) for i in range(9)]
    qkv = _rowwise("rope", rope_fwd, [pos] + qkv_cols, [invf, m1, m2], [(ATT_GROUPW, mx)] * 9)
    q_d, k_d, v_d = _stack_dilated(qkv[0:3]), _stack_dilated(qkv[3:6]), _stack_dilated(qkv[6:9])
    o_d, lse_d = _attn_fwd(q_d, k_d, v_d)
    o_g = [_undilate(o_d[i], d) for i, d in enumerate(DILATIONS)]
    l_g = [_undilate(lse_d[i], d) for i, d in enumerate(DILATIONS)]

    def merge(o0, o1, o2, l0, l1, l2):
        m = jnp.maximum(jnp.maximum(l0, l1), l2)
        e0, e1, e2 = jnp.exp(l0 - m), jnp.exp(l1 - m), jnp.exp(l2 - m)
        tot = e0 + e1 + e2
        att = (e0 * o0 + e1 * o1 + e2 * o2) / tot
        return att, att, m + jnp.log(tot)

    att, attb, lse_tot = _rowwise("attn_merge", merge, o_g + l_g, [],
                                  [(ATT_GROUPW, F32), (ATT_GROUPW, mx), (ATT_GROUPW, F32)])
    batt = _mm("att_up", attb, wb["w_att_up"], "nn")

    gate_rows = [(proj, D_MODEL, 3), (proj, D_MODEL, 4), (z, D_MODEL, 0), (z, D_MODEL, 1), batt]
    mixedb, = _rowwise("gate", _gate, gate_rows, [], [(D_MODEL, mx)])
    o1 = _mm("mix_out", mixedb, wb["w_mix_out"], "nn", bias=sp["b_mix_out"])
    h1, h1b = _rowwise("ln1", lambda h, o, g, b: (lambda r: (r, r))(_res_ln(h, o, g, b)), [h0, o1],
                       [sp["ln1_g"], sp["ln1_b"]], [(D_MODEL, F32), (D_MODEL, mx)])

    qx = _mm("xq", h1b, wb["w_xq"], "nn", out_dtypes=(mx,))
    kvx = _mm("xkv", mem, wb["w_xkv"], "nn", out_dtypes=(mx,))
    oxb = _xattn_fwd(qx, kvx)
    o2 = _mm("xo", oxb, wb["w_xo"], "nn")
    h2, h2b = _rowwise("ln2", lambda h, o, g, b: (lambda r: (r, r))(_res_ln(h, o, g, b)), [h1, o2],
                       [sp["ln2_g"], sp["ln2_b"]], [(D_MODEL, F32), (D_MODEL, mx)])

    a_ff, fb = _mm("ff1", h2b, wb["w_ff1"], "nn", bias=sp["b_ff1"],
                   epilogue=lambda r: (r, jnp.square(jnp.maximum(r, 0.0))), out_dtypes=(F32, mx))
    o3 = _mm("ff2", fb, wb["w_ff2"], "nn", bias=sp["b_ff2"])

    def loss_bwd(h2, o3, tgt, g, b):
        def f(h2, o3, g, b):
            h3 = _res_ln(h2, o3, g, b)
            return 0.5 * jnp.sum(jnp.mean(jnp.square(h3 - tgt), axis=-1))

        loss, vjp = jax.vjp(f, h2, o3, g, b)
        _, dr, dg, db = vjp(jnp.ones((), F32))
        return dr, dr, dg, db, _colsum(dr), jnp.full((1, 128), loss, F32)

    dr3, dr3b, g_ln3_g, g_ln3_b, g_b_ff2, loss = _rowwise(
        "loss_ln3_bwd", loss_bwd, [h2, o3, target], [sp["ln3_g"], sp["ln3_b"]],
        [(D_MODEL, F32), (D_MODEL, mx)], [D_MODEL, D_MODEL, D_MODEL, 128])

    dab = _mm("ff2_dx", dr3b, wb["w_ff2"], "nt", extras=(a_ff,),
              epilogue=lambda r, a: (r * (2.0 * jnp.maximum(a, 0.0)),), out_dtypes=(mx,))
    g_w_ff2 = _mm("ff2_dw", fb, dr3b, "tn")
    g_b_ff1, = _rowwise("ff1_db", lambda v: (_colsum(v),), [dab], [], [], [D_FF])
    g_w_ff1 = _mm("ff1_dw", h2b, dab, "tn")
    dh2 = _mm("ff1_dx", dab, wb["w_ff1"], "nt", extras=(dr3,), epilogue=lambda r, d: (r + al * d,))

    def ln_bwd(h, o, dout, g, b):
        _, vjp = jax.vjp(_res_ln, h, o, g, b)
        _, dr, dg, db = vjp(dout)
        return dr, dr, dg, db, _colsum(dr)

    dr2, dr2b, g_ln2_g, g_ln2_b, _ = _rowwise(
        "ln2_bwd", ln_bwd, [h1, o2, dh2], [sp["ln2_g"], sp["ln2_b"]],
        [(D_MODEL, F32), (D_MODEL, mx)], [D_MODEL, D_MODEL, D_MODEL])
    g_w_xo = _mm("xo_dw", oxb, dr2b, "tn")
    doxb = _mm("xo_dx", dr2b, wb["w_xo"], "nt", out_dtypes=(mx,))
    dqxb, dkvx = _xattn_bwd(qx, kvx, doxb)
    g_w_xq = _mm("xq_dw", h1b, dqxb, "tn")
    dh1 = _mm("xq_dx", dqxb, wb["w_xq"], "nt", extras=(dr2,), epilogue=lambda r, d: (r + al * d,))
    g_w_xkv = _mm("xkv_dw", mem, dkvx, "tn")

    dr1, dr1b, g_ln1_g, g_ln1_b, g_b_mix = _rowwise(
        "ln1_bwd", ln_bwd, [h0, o1, dh1], [sp["ln1_g"], sp["ln1_b"]],
        [(D_MODEL, F32), (D_MODEL, mx)], [D_MODEL, D_MODEL, D_MODEL])
    g_w_mix = _mm("mix_dw", mixedb, dr1b, "tn")
    dmixed = _mm("mix_dx", dr1b, wb["w_mix_out"], "nt")

    def gate_bwd(gs, ga, z1, z2, batt, dm):
        _, vjp = jax.vjp(_gate, gs, ga, z1, z2, batt)
        dgs, dga, dz1, dz2, dbatt = vjp(dm)
        dz = jnp.concatenate([dz1, dz2], axis=-1)
        return dgs, dga, dz, dbatt, _colsum(dz)

    dgsb, dgab, dzb, dbattb, g_b_glu = _rowwise(
        "gate_bwd", gate_bwd, gate_rows + [dmixed], [],
        [(D_MODEL, mx), (D_MODEL, mx), (2 * D_MODEL, mx), (D_MODEL, mx)], [2 * D_MODEL])
    g_w_up = _mm("att_up_dw", attb, dbattb, "tn")
    datt = _mm("att_up_dx", dbattb, wb["w_att_up"], "nt")

    def att_delta(datt, att, hs):
        dl = jnp.dot(datt * att, hs, precision=lax.Precision.HIGHEST, preferred_element_type=F32)
        return datt, dl

    dattb, delta = _rowwise("attn_delta", att_delta, [datt, att], [_head_sum_matrix()],
                            [(ATT_GROUPW, mx), (ATT_GROUPW, F32)])
    do_d = _stack_dilated([dattb] * 3)
    lt_d = _stack_dilated([lse_tot] * 3)
    dl_d = _stack_dilated([delta] * 3)
    dq_d = _attn_dq(q_d, k_d, v_d, do_d, lt_d, dl_d)
    dk_d, dv_d = _attn_dkv(q_d, k_d, v_d, do_d, lt_d, dl_d)
    dqkv = [_undilate(t[i], d) for t in (dq_d, dk_d, dv_d) for i, d in enumerate(DILATIONS)]

    def rope_bwd(pos, q0, q1, q2, k0, k1, k2, v0, v1, v2, invf, m1, m2):
        tabs = _rope_tables(pos, invf, m1, m2)
        return jnp.concatenate([_rope_t(t, *tabs) for t in (q0, q1, q2, k0, k1, k2)] + [v0, v1, v2], axis=-1)

    dqkvb, = _rowwise("rope_bwd", rope_bwd, [pos] + dqkv, [invf, m1, m2], [(9 * ATT_GROUPW, mx)])

    g_w_glu = _mm("glu_dw", ygb, dzb, "tn")
    dyg = _mm("glu_dx", dzb, wb["w_glu"], "nt")

    def gelu_bwd(y, dyg):
        _, vjp = jax.vjp(jax.nn.gelu, y)
        return vjp(dyg)[0]

    dy, = _rowwise("gelu_bwd", gelu_bwd, [y, dyg], [], [(SSM_WIDTH, F32)])
    dy_p = _time_perm(dy)
    dh_re, dh_im = _ssm_expand("ssm_dh", dy_p, cexp, "nt")
    g_cexp = _ssm_wgrad("ssm_dc", dy_p, h_re, h_im, expand=False)
    s_re, s_im = _ssm_scan("ssm_scan_bwd", dh_re, dh_im, a_re, a_im, reverse=True)
    d_abr, d_abi = _ssm_da(s_re, s_im, h_re, h_im)
    g_bexp = _ssm_wgrad("ssm_db", u_p, s_re, s_im, expand=True)
    du_p = _ssm_contract("ssm_du", s_re, s_im, bexp, "nt", sp["ssm_d"], dy_p)
    g_ssm_d, = _rowwise("ssm_dd", lambda a, b: (_colsum(a * b),), [dy_p, u_p], [], [], [SSM_WIDTH])
    g_ldt, g_are, g_aim, g_bre, g_bim = _ssm_disc_bwd(
        ldt, are, aim, bre, bim, d_abr.reshape(N_STATE, 1), d_abi.reshape(N_STATE, 1),
        _diag_of_b(g_bexp[:, :, :CH_N]), _diag_of_b(g_bexp[:, :, CH_N:]))
    g_c_re = _diag_of_c(g_cexp[:, :CH_N, :])
    g_c_im = -_diag_of_c(g_cexp[:, CH_N:, :])
    dub = _time_unperm(du_p).astype(mx)

    dprojb = jnp.concatenate([dub, dqkvb, dgsb, dgab], axis=-1)
    g_b_in, = _rowwise("in_db", lambda v: (_colsum(v),), [dprojb], [], [], [IN_COLS])
    g_w_in = _mm("in_dw", h0b, dprojb, "tn")
    dh0 = _mm("in_dx", dprojb, wb["w_in"], "nt", extras=(dr1,), epilogue=lambda r, d: (r + al * d,))

    def ln_in_bwd(x, dout, g, b):
        _, vjp = jax.vjp(_ln, x, g, b)
        return vjp(dout)

    dx, g_ln_in_g, g_ln_in_b = _rowwise("ln_in_bwd", ln_in_bwd, [x, dh0], [sp["ln_in_g"], sp["ln_in_b"]],
                                        [(D_MODEL, F32)], [D_MODEL, D_MODEL])

    big = {"w_in": g_w_in, "w_glu": g_w_glu, "w_att_up": g_w_up, "w_mix_out": g_w_mix, "w_xq": g_w_xq,
           "w_xkv": g_w_xkv, "w_xo": g_w_xo, "w_ff1": g_w_ff1, "w_ff2": g_w_ff2}
    small = {"ln_in_g": g_ln_in_g, "ln_in_b": g_ln_in_b, "b_in": g_b_in, "ssm_log_dt": g_ldt, "ssm_a_re": g_are,
             "ssm_a_im": g_aim, "ssm_b_re": g_bre, "ssm_b_im": g_bim, "ssm_c_re": g_c_re, "ssm_c_im": g_c_im,
             "ssm_d": g_ssm_d, "b_glu": g_b_glu, "b_mix_out": g_b_mix, "ln1_g": g_ln1_g, "ln1_b": g_ln1_b,
             "ln2_g": g_ln2_g, "ln2_b": g_ln2_b, "b_ff1": g_b_ff1, "b_ff2": g_b_ff2, "ln3_g": g_ln3_g,
             "ln3_b": g_ln3_b}
    return loss, dx, big, small


def _piece_shape(k, n, axis):
    return (k // 2, n // 4) if axis == 1 else (k // 8, n)


def _aligned(v, m):
    return v if isinstance(v, int) else pl.multiple_of(v, m)


def _full_piece(ref, k, n, axis, chip, half):
    pr, pc = _piece_shape(k, n, axis)
    if axis == 1:
        return ref.at[pl.ds(_aligned(half * pr, 8), pr), pl.ds(_aligned(chip * pc, 128), pc)]
    return ref.at[pl.ds(_aligned(chip * (2 * pr) + half * pr, 8), pr), :]


def _full_shard(ref, k, n, axis, chip):
    if axis == 1:
        return ref.at[:, pl.ds(_aligned(chip * (n // 4), 128), n // 4)]
    return ref.at[pl.ds(_aligned(chip * (k // 4), 8), k // 4), :]


def _shard_piece(ref, k, n, axis, half):
    pr, _ = _piece_shape(k, n, axis)
    return ref.at[pl.ds(_aligned(half * pr, 8), pr), :]


def _mesh_pos():
    x, y, c = lax.axis_index("x"), lax.axis_index("y"), lax.axis_index("c")
    other_chips = [(1 - x, y), (x, 1 - y), (1 - x, 1 - y)]
    return x, y, c, other_chips


def _remote(src, dst, send_sem, recv_sem, dev):
    return pltpu.make_async_remote_copy(src_ref=src, dst_ref=dst, send_sem=send_sem, recv_sem=recv_sem,
                                        device_id=dev, device_id_type=MESH)


def _placed(name, fn, n_steps, where, ins, out_sds, out_block, out_index):
    def body(w_ref, *refs):
        o_ref = refs[-1]
        o_ref[...] = fn(*[r[...] for r in refs[:-1]]).astype(o_ref.dtype)

    grid_spec = pltpu.PrefetchScalarGridSpec(
        num_scalar_prefetch=1, grid=(n_steps,), in_specs=[pl.BlockSpec(bs, idx) for _, bs, idx in ins],
        out_specs=pl.BlockSpec(out_block, out_index))
    return pl.pallas_call(body, name=name, grid_spec=grid_spec, out_shape=out_sds,
                          compiler_params=_cparams(1))(where, *[a for a, _, _ in ins])


def _gather_weights(fulls):
    nw = len(BIG)

    def body(*refs):
        full = refs[nw:2 * nw]
        send_sems, recv_sems = refs[2 * nw:]
        x, y, c, chips = _mesh_pos()
        me = 2 * x + y
        sib = (x, y, 1 - c)
        first, fwd = [], []
        for wi, (_, k, n, ax) in enumerate(BIG):
            mine = _full_piece(full[wi], k, n, ax, me, c)
            for j, (qx, qy) in enumerate(chips):
                cp = _remote(mine, mine, send_sems.at[wi * 6 + j], recv_sems.at[wi * 6 + j], (qx, qy, c))
                cp.start()
                first.append(cp)
        for wi, (_, k, n, ax) in enumerate(BIG):
            for j, (qx, qy) in enumerate(chips):
                piece = _full_piece(full[wi], k, n, ax, 2 * qx + qy, c)
                _remote(piece, piece, send_sems.at[wi * 6 + j], recv_sems.at[wi * 6 + j], (qx, qy, c)).wait_recv()
                cp = _remote(piece, piece, send_sems.at[wi * 6 + 3 + j], recv_sems.at[wi * 6 + 3 + j], sib)
                cp.start()
                fwd.append(cp)
        for wi, (_, k, n, ax) in enumerate(BIG):
            for j, (qx, qy) in enumerate(chips):
                piece = _full_piece(full[wi], k, n, ax, 2 * qx + qy, 1 - c)
                _remote(piece, piece, send_sems.at[wi * 6 + 3 + j], recv_sems.at[wi * 6 + 3 + j], sib).wait_recv()
        for cp in first + fwd:
            cp.wait_send()

    return pl.pallas_call(
        body, name="gather_weights", in_specs=[HBM_SPEC] * nw, out_specs=[HBM_SPEC] * nw,
        out_shape=[jax.ShapeDtypeStruct((k, n), MXU_DTYPE) for _, k, n, _ in BIG],
        input_output_aliases={i: i for i in range(nw)},
        scratch_shapes=[pltpu.SemaphoreType.DMA((6 * nw,)), pltpu.SemaphoreType.DMA((6 * nw,))])(*fulls)


def _reduce_swap_halves(grads):
    nw = len(BIG)

    def body(*refs):
        g, got = refs[:nw], refs[nw:2 * nw]
        send_sems, recv_sems = refs[2 * nw:]
        x, y, c, _ = _mesh_pos()
        sib = (x, y, 1 - c)
        cps = []
        for wi, (_, k, n, ax) in enumerate(BIG):
            for q in range(4):
                cp = _remote(_full_piece(g[wi], k, n, ax, q, 1 - c), got[wi].at[q],
                             send_sems.at[wi * 4 + q], recv_sems.at[wi * 4 + q], sib)
                cp.start()
                cps.append(cp)
        for cp in cps:
            cp.wait()

    return pl.pallas_call(
        body, name="reduce_swap_halves", in_specs=[HBM_SPEC] * nw, out_specs=[HBM_SPEC] * nw,
        out_shape=[jax.ShapeDtypeStruct((4,) + _piece_shape(k, n, ax), F32) for _, k, n, ax in BIG],
        scratch_shapes=[pltpu.SemaphoreType.DMA((4 * nw,)), pltpu.SemaphoreType.DMA((4 * nw,))])(*grads)


def _reduce_to_owner(parts):
    nw = len(BIG)

    def body(*refs):
        p, out = refs[:nw], refs[nw:2 * nw]
        send_sems, recv_sems = refs[2 * nw:]
        x, y, c, chips = _mesh_pos()
        cps = []
        for wi in range(nw):
            for j, (qx, qy) in enumerate(chips):
                cp = _remote(p[wi].at[2 * qx + qy], out[wi].at[j], send_sems.at[wi * 3 + j],
                             recv_sems.at[wi * 3 + j], (qx, qy, c))
                cp.start()
                cps.append(cp)
        for cp in cps:
            cp.wait()

    return pl.pallas_call(
        body, name="reduce_to_owner", in_specs=[HBM_SPEC] * nw, out_specs=[HBM_SPEC] * nw,
        out_shape=[jax.ShapeDtypeStruct((3,) + p.shape[1:], p.dtype) for p in parts],
        scratch_shapes=[pltpu.SemaphoreType.DMA((3 * nw,)), pltpu.SemaphoreType.DMA((3 * nw,))])(*parts)


def _share_with_sibling(shards):
    nw = len(BIG)

    def body(*refs):
        out = refs[nw:2 * nw]
        send_sems, recv_sems = refs[2 * nw:]
        x, y, c, _ = _mesh_pos()
        sib = (x, y, 1 - c)
        cps = []
        for wi, (_, k, n, ax) in enumerate(BIG):
            mine = _shard_piece(out[wi], k, n, ax, c)
            cp = _remote(mine, mine, send_sems.at[wi], recv_sems.at[wi], sib)
            cp.start()
            cps.append(cp)
        for wi, (_, k, n, ax) in enumerate(BIG):
            piece = _shard_piece(out[wi], k, n, ax, 1 - c)
            _remote(piece, piece, send_sems.at[wi], recv_sems.at[wi], sib).wait_recv()
        for cp in cps:
            cp.wait_send()

    return pl.pallas_call(
        body, name="share_with_sibling", in_specs=[HBM_SPEC] * nw, out_specs=[HBM_SPEC] * nw,
        out_shape=[jax.ShapeDtypeStruct(sh.shape, sh.dtype) for sh in shards],
        input_output_aliases={i: i for i in range(nw)},
        scratch_shapes=[pltpu.SemaphoreType.DMA((nw,)), pltpu.SemaphoreType.DMA((nw,))])(*shards)


def _allreduce_small(v):
    r = v.shape[0]

    def body(v_ref, o_ref, buf, send_sems, recv_sems):
        x, y, c, _ = _mesh_pos()
        me = 4 * x + 2 * y + c
        buf[me] = v_ref[...]
        cps = []
        for rel in range(1, 8):
            fx, fy, fc = (rel >> 2) & 1, (rel >> 1) & 1, rel & 1
            dev = (x ^ fx, y ^ fy, c ^ fc)
            cp = _remote(v_ref, buf.at[me], send_sems.at[rel - 1], recv_sems.at[rel - 1], dev)
            cp.start()
            cps.append(cp)
        for rel in range(1, 8):
            fx, fy, fc = (rel >> 2) & 1, (rel >> 1) & 1, rel & 1
            peer = 4 * (x ^ fx) + 2 * (y ^ fy) + (c ^ fc)
            _remote(v_ref, buf.at[peer], send_sems.at[rel - 1], recv_sems.at[rel - 1], (x, y, c)).wait_recv()
        for cp in cps:
            cp.wait_send()
        acc = buf[0]
        for d in range(1, 8):
            acc = acc + buf[d]
        o_ref[...] = acc

    return pl.pallas_call(
        body, name="allreduce_small", in_specs=[VMEM_SPEC], out_specs=VMEM_SPEC,
        out_shape=jax.ShapeDtypeStruct((r, 128), F32),
        scratch_shapes=[pltpu.VMEM((8, r, 128), F32), pltpu.SemaphoreType.DMA((7,)), pltpu.SemaphoreType.DMA((7,))],
        compiler_params=pltpu.CompilerParams(vmem_limit_bytes=VMEM_LIMIT))(v)


def _as2d(a):
    a = a.reshape((-1, a.shape[-1])) if a.ndim > 1 else a.reshape(1, -1)
    return a


def _where():
    return jnp.stack([2 * lax.axis_index("x") + lax.axis_index("y"), lax.axis_index("c")]).astype(jnp.int32)


def _gather_all(inputs, where):
    fulls = []
    for name, k, n, ax in BIG:
        w2 = inputs[name][0]
        rs, cs = w2.shape
        tm = _tile(rs, 512)
        steps = rs // tm
        if ax == 1:
            blk, idx = (tm, cs), lambda i, w: (i, w[0])
        else:
            blk, idx = (tm, n), functools.partial(lambda i, w, steps: (w[0] * steps + i, 0), steps=steps)
        fulls.append(_placed("cast_" + name, lambda w: w, steps, where, [(w2, (tm, cs), lambda i, w: (i, 0))],
                             jax.ShapeDtypeStruct((k, n), MXU_DTYPE), blk, idx))
    return _gather_weights(fulls)


def _reduce_all(inputs, grads, where):
    got = _reduce_swap_halves(grads)
    parts, geom = [], []
    for i, (name, k, n, ax) in enumerate(BIG):
        pr, pc = _piece_shape(k, n, ax)
        tm = _tile(pr, 512)
        spp = pr // tm
        geom.append((pr, pc, tm, spp))
        if ax == 1:
            g_idx = functools.partial(lambda i, w, spp: (w[1] * spp + i % spp, i // spp), spp=spp)
        else:
            g_idx = functools.partial(lambda i, w, spp: ((i // spp) * 2 * spp + w[1] * spp + i % spp, 0), spp=spp)
        parts.append(_placed("pair_sum_" + name, lambda a, b: a + b, 4 * spp, where,
                             [(grads[i], (tm, pc), g_idx), (got[i].reshape(4 * pr, pc), (tm, pc), lambda i, w: (i, 0))],
                             jax.ShapeDtypeStruct((4 * pr, pc), BF16), (tm, pc), lambda i, w: (i, 0)).reshape(4, pr, pc))
    landed = _reduce_to_owner(parts)
    halves = []
    for i, (name, k, n, ax) in enumerate(BIG):
        pr, pc, tm, spp = geom[i]
        shard_shape = inputs[name].shape[1:]
        ins = [(parts[i], (None, tm, pc), lambda i, w: (w[0], i, 0))]
        ins += [(landed[i], (None, tm, pc), functools.partial(lambda i, w, j: (j, i, 0), j=j)) for j in range(3)]
        halves.append(_placed("chip_sum_" + name,
                              lambda a, b, c, d: ((a.astype(F32) + b.astype(F32)) + c.astype(F32)) + d.astype(F32),
                              spp, where, ins, jax.ShapeDtypeStruct(shard_shape, F32), (tm, pc),
                              functools.partial(lambda i, w, spp: (w[1] * spp + i, 0), spp=spp)))
    return _share_with_sibling(halves)


def _step(inputs):
    x, mem, positions, target = inputs["x"][0], inputs["mem"][0], inputs["positions"], inputs["loss_target"][0]
    pos = positions.reshape(-1, 1)
    where = _where()
    full = _gather_all(inputs, where)
    wb = {name: full[i] for i, (name, _, _, _) in enumerate(BIG)}
    sp = {name: _as2d(inputs[name]) for name in SMALL}
    memb, = _rowwise("cast_mem", lambda m: (m,), [mem], [], [(D_MODEL, MXU_DTYPE)])

    loss, dx, gbig, gsmall = _local_step(x, memb, pos, target, sp, wb)
    gshard = _reduce_all(inputs, [gbig[name] for name, _, _, _ in BIG], where)

    out = {}
    for i, (name, _, _, _) in enumerate(BIG):
        w2, m2, v2 = inputs[name][0], inputs["m_" + name][0], inputs["v_" + name][0]
        n = w2.shape[1]
        d, nm, nv = _rowwise("adamw_" + name, _adamw, [w2, gshard[i], m2, v2], [], [(n, F32)] * 3, tm=128)
        lead = inputs[name].shape
        out[name] = (gshard[i].reshape(lead), d.reshape(lead), nm.reshape(lead), nv.reshape(lead))

    flat = [loss[:, :1].reshape(-1)] + [gsmall[name].reshape(-1) for name in SMALL]
    sizes = [f.shape[0] for f in flat]
    total = sum(sizes)
    rows = -(-total // 1024) * 8
    buf = jnp.concatenate(flat + [jnp.zeros((rows * 128 - total,), F32)]).reshape(rows, 128)
    red = _allreduce_small(buf)
    w_flat = jnp.concatenate([jnp.zeros((1,), F32)] + [inputs[name].reshape(-1) for name in SMALL]
                             + [jnp.zeros((rows * 128 - total,), F32)]).reshape(rows, 128)
    m_flat = jnp.concatenate([jnp.zeros((1,), F32)] + [inputs["m_" + name].reshape(-1) for name in SMALL]
                             + [jnp.zeros((rows * 128 - total,), F32)]).reshape(rows, 128)
    v_flat = jnp.concatenate([jnp.ones((1,), F32)] + [inputs["v_" + name].reshape(-1) for name in SMALL]
                             + [jnp.ones((rows * 128 - total,), F32)]).reshape(rows, 128)
    d_s, m_s, v_s = _rowwise("adamw_small", _adamw, [w_flat, red, m_flat, v_flat], [], [(128, F32)] * 3, tm=rows)
    red_f, d_f, m_f, v_f = red.reshape(-1), d_s.reshape(-1), m_s.reshape(-1), v_s.reshape(-1)
    off = sizes[0]
    for name, sz in zip(SMALL, sizes[1:]):
        shp = inputs[name].shape
        out[name] = tuple(t[off:off + sz].reshape(shp) for t in (red_f, d_f, m_f, v_f))
        off += sz
    loss_total = red_f[0]
    return loss_total, dx.reshape(inputs["x"].shape), out


_ARG_NAMES = (("x", "mem", "positions") + WEIGHT_ORDER + ("loss_target",) + tuple("m_" + n for n in WEIGHT_ORDER)
              + tuple("v_" + n for n in WEIGHT_ORDER))


def kernel(x, mem, positions, ln_in_g, ln_in_b, w_in, b_in, ssm_log_dt, ssm_a_re, ssm_a_im, ssm_b_re, ssm_b_im, ssm_c_re, ssm_c_im, ssm_d, w_glu, b_glu, w_att_up, w_mix_out, b_mix_out, ln1_g, ln1_b, w_xq, w_xkv, w_xo, ln2_g, ln2_b, w_ff1, b_ff1, w_ff2, b_ff2, ln3_g, ln3_b, loss_target, m_ln_in_g, m_ln_in_b, m_w_in, m_b_in, m_ssm_log_dt, m_ssm_a_re, m_ssm_a_im, m_ssm_b_re, m_ssm_b_im, m_ssm_c_re, m_ssm_c_im, m_ssm_d, m_w_glu, m_b_glu, m_w_att_up, m_w_mix_out, m_b_mix_out, m_ln1_g, m_ln1_b, m_w_xq, m_w_xkv, m_w_xo, m_ln2_g, m_ln2_b, m_w_ff1, m_b_ff1, m_w_ff2, m_b_ff2, m_ln3_g, m_ln3_b, v_ln_in_g, v_ln_in_b, v_w_in, v_b_in, v_ssm_log_dt, v_ssm_a_re, v_ssm_a_im, v_ssm_b_re, v_ssm_b_im, v_ssm_c_re, v_ssm_c_im, v_ssm_d, v_w_glu, v_b_glu, v_w_att_up, v_w_mix_out, v_b_mix_out, v_ln1_g, v_ln1_b, v_w_xq, v_w_xkv, v_w_xo, v_ln2_g, v_ln2_b, v_w_ff1, v_b_ff1, v_w_ff2, v_b_ff2, v_ln3_g, v_ln3_b):
    args = (x, mem, positions, ln_in_g, ln_in_b, w_in, b_in, ssm_log_dt, ssm_a_re, ssm_a_im, ssm_b_re, ssm_b_im, ssm_c_re, ssm_c_im, ssm_d, w_glu, b_glu, w_att_up, w_mix_out, b_mix_out, ln1_g, ln1_b, w_xq, w_xkv, w_xo, ln2_g, ln2_b, w_ff1, b_ff1, w_ff2, b_ff2, ln3_g, ln3_b, loss_target, m_ln_in_g, m_ln_in_b, m_w_in, m_b_in, m_ssm_log_dt, m_ssm_a_re, m_ssm_a_im, m_ssm_b_re, m_ssm_b_im, m_ssm_c_re, m_ssm_c_im, m_ssm_d, m_w_glu, m_b_glu, m_w_att_up, m_w_mix_out, m_b_mix_out, m_ln1_g, m_ln1_b, m_w_xq, m_w_xkv, m_w_xo, m_ln2_g, m_ln2_b, m_w_ff1, m_b_ff1, m_w_ff2, m_b_ff2, m_ln3_g, m_ln3_b, v_ln_in_g, v_ln_in_b, v_w_in, v_b_in, v_ssm_log_dt, v_ssm_a_re, v_ssm_a_im, v_ssm_b_re, v_ssm_b_im, v_ssm_c_re, v_ssm_c_im, v_ssm_d, v_w_glu, v_b_glu, v_w_att_up, v_w_mix_out, v_b_mix_out, v_ln1_g, v_ln1_b, v_w_xq, v_w_xkv, v_w_xo, v_ln2_g, v_ln2_b, v_w_ff1, v_b_ff1, v_w_ff2, v_b_ff2, v_ln3_g, v_ln3_b)
    assert len(args) == len(_ARG_NAMES)
    inputs = dict(zip(_ARG_NAMES, args))
    loss, dx, out = _step(inputs)
    res = [loss, dx]
    for k in range(4):
        res += [out[name][k] for name in WEIGHT_ORDER]
    return tuple(res)
```

```python
import functools
import math

import numpy as np
import jax
import jax.numpy as jnp
from jax import lax
from jax.experimental import pallas as pl
from jax.experimental.pallas import tpu as pltpu

F32 = jnp.float32
BF16 = jnp.bfloat16
MXU_DTYPE = jnp.bfloat16

D_MODEL = 1024
SSM_GROUP = 16
SSM_WIDTH = 768
SSM_GROUPS = 48
SSM_STATE = 64
N_STATE = SSM_GROUPS * SSM_STATE
SSM_CHUNKS = 6
CH_W = 128
CH_N = 512
ATT_HEAD_DIM = 64
ATT_HPG = 4
ATT_GROUPW = ATT_HPG * ATT_HEAD_DIM
DILATIONS = (1, 4, 16)
ATT_BLK = 128
ATT_SCALE = ATT_HEAD_DIM ** -0.5
ROT_DIM = 16
ROPE_THETA = 500000.0
XATT_HEADS = 4
XATT_HEAD_DIM = 256
XATT_SCALE = XATT_HEAD_DIM ** -0.5
D_FF = 4096
IN_COLS = 5120
DEEPNORM_ALPHA = 2.0 ** 0.25
LN_EPS = 1e-5
NEG_INF = -1e30
ADAM_LR = 0.001
ADAM_B1 = 0.9
ADAM_B2 = 0.999
ADAM_EPS = 1e-08
ADAM_WD = 0.01
ADAM_STEP = 10

N_SEG = 32
VMEM_LIMIT = 48 * 1024 * 1024
MESH = pl.DeviceIdType.MESH
HBM_SPEC = pl.BlockSpec(memory_space=pltpu.HBM)
VMEM_SPEC = pl.BlockSpec(memory_space=pltpu.VMEM)

BIG = (("w_in", 1024, 5120, 1), ("w_glu", 768, 2048, 1), ("w_att_up", 256, 1024, 1),
       ("w_mix_out", 1024, 1024, 0), ("w_xq", 1024, 1024, 0), ("w_xkv", 1024, 2048, 1),
       ("w_xo", 1024, 1024, 0), ("w_ff1", 1024, 4096, 1), ("w_ff2", 4096, 1024, 0))
SMALL = ("ln_in_g", "ln_in_b", "b_in", "ssm_log_dt", "ssm_a_re", "ssm_a_im", "ssm_b_re", "ssm_b_im",
         "ssm_c_re", "ssm_c_im", "ssm_d", "b_glu", "b_mix_out", "ln1_g", "ln1_b", "ln2_g", "ln2_b",
         "b_ff1", "b_ff2", "ln3_g", "ln3_b")
WEIGHT_ORDER = ("ln_in_g", "ln_in_b", "w_in", "b_in", "ssm_log_dt", "ssm_a_re", "ssm_a_im", "ssm_b_re",
                "ssm_b_im", "ssm_c_re", "ssm_c_im", "ssm_d", "w_glu", "b_glu", "w_att_up", "w_mix_out",
                "b_mix_out", "ln1_g", "ln1_b", "w_xq", "w_xkv", "w_xo", "ln2_g", "ln2_b", "w_ff1", "b_ff1",
                "w_ff2", "b_ff2", "ln3_g", "ln3_b")


def _cparams(n_axes):
    return pltpu.CompilerParams(dimension_semantics=("arbitrary",) * n_axes, vmem_limit_bytes=VMEM_LIMIT)


def _rowwise(name, fn, rows, consts, outs, reds=(), tm=256):
    n_rows = (rows[0][0] if isinstance(rows[0], tuple) else rows[0]).shape[-2]
    tm = min(tm, n_rows)
    assert n_rows % tm == 0, (name, n_rows, tm)
    specs, args = [], []
    for r in rows:
        if isinstance(r, tuple) and len(r) == 3:
            arr, width, cb = r
            specs.append(pl.BlockSpec((tm, width), functools.partial(lambda i, cb: (i, cb), cb=cb)))
        elif isinstance(r, tuple):
            arr, slot = r
            specs.append(pl.BlockSpec((None, tm, arr.shape[2]), functools.partial(lambda i, s: (s, i, 0), s=slot)))
        else:
            arr = r
            specs.append(pl.BlockSpec((tm, arr.shape[1]), lambda i: (i, 0)))
        args.append(arr)
        assert arr.shape[-2] == n_rows, (name, arr.shape, n_rows)
    for cst in consts:
        specs.append(pl.BlockSpec(cst.shape, lambda i: (0, 0)))
        args.append(cst)
    n_r, n_c, n_o, n_d = len(rows), len(consts), len(outs), len(reds)
    out_shape = [jax.ShapeDtypeStruct((n_rows, c), dt) for c, dt in outs]
    out_specs = [pl.BlockSpec((tm, c), lambda i: (i, 0)) for c, _ in outs]
    out_shape += [jax.ShapeDtypeStruct((1, c), F32) for c in reds]
    out_specs += [pl.BlockSpec((1, c), lambda i: (0, 0)) for c in reds]

    def body(*refs):
        ins = [r[...] for r in refs[:n_r + n_c]]
        o_refs = refs[n_r + n_c:n_r + n_c + n_o]
        d_refs = refs[n_r + n_c + n_o:]
        res = fn(*ins)
        res = res if isinstance(res, (tuple, list)) else (res,)
        assert len(res) == n_o + n_d, (name, len(res))
        for ref, val in zip(o_refs, res[:n_o]):
            ref[...] = val.astype(ref.dtype)
        first = pl.program_id(0) == 0
        for ref, val in zip(d_refs, res[n_o:]):
            @pl.when(first)
            def _(ref=ref, val=val):
                ref[...] = val

            @pl.when(jnp.logical_not(first))
            def _(ref=ref, val=val):
                ref[...] += val

    res = pl.pallas_call(body, name=name, grid=(n_rows // tm,), in_specs=specs, out_specs=out_specs,
                         out_shape=out_shape, compiler_params=_cparams(1))(*args)
    return res


def _colsum(v):
    return jnp.sum(v.astype(F32), axis=0, keepdims=True)


_DIMS = {"nn": (((1,), (0,)), ((), ())), "nt": (((1,), (1,)), ((), ())), "tn": (((0,), (0,)), ((), ()))}


def _tile(dim, want):
    if dim <= want:
        return dim
    return max(t for t in range(128, want + 1, 128) if dim % t == 0)


def _dot(a, b, mode):
    return lax.dot_general(a.astype(MXU_DTYPE), b.astype(MXU_DTYPE), _DIMS[mode], preferred_element_type=F32)


def _mm(name, a, b, mode, *, bias=None, extras=(), epilogue=None, out_dtypes=(F32,), tm=1024, tn=1024, tk=1024):
    if mode == "nn":
        (m, k), (_, n) = a.shape, b.shape
    elif mode == "nt":
        (m, k), (n, _) = a.shape, b.shape
    else:
        (k, m), (_, n) = a.shape, b.shape
    tm, tn = _tile(m, tm), _tile(n, tn)
    if mode != "tn":
        tk = k if k <= 1024 else tk
    tk = _tile(k, tk)
    assert m % tm == 0 and n % tn == 0 and k % tk == 0, (name, m, n, k)
    nk = k // tk
    a_spec = {"nn": pl.BlockSpec((tm, tk), lambda i, j, kk: (i, kk)),
              "nt": pl.BlockSpec((tm, tk), lambda i, j, kk: (i, kk)),
              "tn": pl.BlockSpec((tk, tm), lambda i, j, kk: (kk, i))}[mode]
    b_spec = {"nn": pl.BlockSpec((tk, tn), lambda i, j, kk: (kk, j)),
              "nt": pl.BlockSpec((tn, tk), lambda i, j, kk: (j, kk)),
              "tn": pl.BlockSpec((tk, tn), lambda i, j, kk: (kk, j))}[mode]
    specs, args = [a_spec, b_spec], [a, b]
    if bias is not None:
        specs.append(pl.BlockSpec((1, tn), lambda i, j, kk: (0, j)))
        args.append(bias)
    for e in extras:
        specs.append(pl.BlockSpec((tm, tn), lambda i, j, kk: (i, j)))
        args.append(e)
    n_e, n_o = len(extras), len(out_dtypes)
    has_bias = bias is not None

    def body(*refs):
        a_ref, b_ref = refs[0], refs[1]
        pos = 2
        bias_ref = refs[pos] if has_bias else None
        pos += int(has_bias)
        e_refs = refs[pos:pos + n_e]
        o_refs = refs[pos + n_e:pos + n_e + n_o]
        acc_ref = refs[pos + n_e + n_o] if nk > 1 else None
        part = _dot(a_ref[...], b_ref[...], mode)

        def finish(r):
            if has_bias:
                r = r + bias_ref[...]
            res = epilogue(r, *[e[...] for e in e_refs]) if epilogue is not None else (r,)
            for ref, val in zip(o_refs, res):
                ref[...] = val.astype(ref.dtype)

        if nk == 1:
            finish(part)
        else:
            kk = pl.program_id(2)

            @pl.when(kk == 0)
            def _():
                acc_ref[...] = part

            @pl.when(kk > 0)
            def _():
                acc_ref[...] += part

            @pl.when(kk == nk - 1)
            def _():
                finish(acc_ref[...])

    res = pl.pallas_call(
        body, name=name, grid=(m // tm, n // tn, nk), in_specs=specs,
        out_specs=[pl.BlockSpec((tm, tn), lambda i, j, kk: (i, j)) for _ in out_dtypes],
        out_shape=[jax.ShapeDtypeStruct((m, n), dt) for dt in out_dtypes],
        scratch_shapes=[pltpu.VMEM((tm, tn), F32)] if nk > 1 else [],
        compiler_params=_cparams(3))(*args)
    return res[0] if n_o == 1 else res


def _ssm_expand(name, a, bmat, mode, tm=512):
    s = a.shape[0]
    tm = min(tm, s)

    def body(a_ref, b_ref, re_ref, im_ref):
        r = _dot(a_ref[...], b_ref[...], mode)
        re_ref[...] = r[:, :CH_N]
        im_ref[...] = r[:, CH_N:]

    return pl.pallas_call(
        body, name=name, grid=(s // tm, SSM_CHUNKS),
        in_specs=[pl.BlockSpec((tm, CH_W), lambda i, j: (i, j)),
                  pl.BlockSpec((None,) + bmat.shape[1:], lambda i, j: (j, 0, 0))],
        out_specs=[pl.BlockSpec((tm, CH_N), lambda i, j: (i, j))] * 2,
        out_shape=[jax.ShapeDtypeStruct((s, N_STATE), F32)] * 2,
        compiler_params=_cparams(2))(a, bmat)


def _ssm_contract(name, a_re, a_im, bmat, mode, d_row, extra, tm=512):
    s = a_re.shape[0]
    tm = min(tm, s)

    def body(re_ref, im_ref, b_ref, d_ref, e_ref, o_ref):
        b = b_ref[...]
        if mode == "nn":
            r = _dot(re_ref[...], b[:CH_N], "nn") + _dot(im_ref[...], b[CH_N:], "nn")
        else:
            r = _dot(re_ref[...], b[:, :CH_N], "nt") + _dot(im_ref[...], b[:, CH_N:], "nt")
        o_ref[...] = r + d_ref[...] * e_ref[...]

    return pl.pallas_call(
        body, name=name, grid=(s // tm, SSM_CHUNKS),
        in_specs=[pl.BlockSpec((tm, CH_N), lambda i, j: (i, j)), pl.BlockSpec((tm, CH_N), lambda i, j: (i, j)),
                  pl.BlockSpec((None,) + bmat.shape[1:], lambda i, j: (j, 0, 0)),
                  pl.BlockSpec((1, CH_W), lambda i, j: (0, j)), pl.BlockSpec((tm, CH_W), lambda i, j: (i, j))],
        out_specs=pl.BlockSpec((tm, CH_W), lambda i, j: (i, j)),
        out_shape=jax.ShapeDtypeStruct((s, SSM_WIDTH), F32),
        compiler_params=_cparams(2))(a_re, a_im, bmat, d_row, extra)


def _ssm_wgrad(name, chan, st_re, st_im, expand, tk=512):
    s = chan.shape[0]
    tk = min(tk, s)
    nk = s // tk
    oshape = (CH_W, 2 * CH_N) if expand else (2 * CH_N, CH_W)

    def body(c_ref, re_ref, im_ref, o_ref):
        c = c_ref[...]
        if expand:
            part = jnp.concatenate([_dot(c, re_ref[...], "tn"), _dot(c, im_ref[...], "tn")], axis=1)
        else:
            part = jnp.concatenate([_dot(re_ref[...], c, "tn"), _dot(im_ref[...], c, "tn")], axis=0)
        kk = pl.program_id(1)

        @pl.when(kk == 0)
        def _():
            o_ref[...] = part

        @pl.when(kk > 0)
        def _():
            o_ref[...] += part

    return pl.pallas_call(
        body, name=name, grid=(SSM_CHUNKS, nk),
        in_specs=[pl.BlockSpec((tk, CH_W), lambda j, kk: (kk, j)), pl.BlockSpec((tk, CH_N), lambda j, kk: (kk, j)),
                  pl.BlockSpec((tk, CH_N), lambda j, kk: (kk, j))],
        out_specs=pl.BlockSpec((None,) + oshape, lambda j, kk: (j, 0, 0)),
        out_shape=jax.ShapeDtypeStruct((SSM_CHUNKS,) + oshape, F32),
        compiler_params=_cparams(2))(chan, st_re, st_im)


SCAN_LB = 256


def _ssm_scan(name, w_re, w_im, a_re, a_im, reverse):
    s = w_re.shape[0]
    seg_len = s // N_SEG
    n_sq = int(math.log2(seg_len))
    assert 2 ** n_sq == seg_len

    def body(are_ref, aim_ref, wre_ref, wim_ref, hre_ref, him_ref, ere, eim, cre, cim):
        ar1 = are_ref[...]
        ai1 = -aim_ref[...] if reverse else aim_ref[...]
        ar = jnp.broadcast_to(ar1, (N_SEG, SCAN_LB))
        ai = jnp.broadcast_to(ai1, (N_SEG, SCAN_LB))

        def rows_of(k):
            kk = seg_len - 1 - k if reverse else k
            return pl.ds(pl.multiple_of(kk * N_SEG, N_SEG), N_SEG)

        def local(k, carry):
            hr, hi = carry
            rows = rows_of(k)
            nr = ar * hr - ai * hi + wre_ref[rows, :]
            ni = ar * hi + ai * hr + wim_ref[rows, :]
            hre_ref[rows, :] = nr
            him_ref[rows, :] = ni
            return nr, ni

        zero = jnp.zeros((N_SEG, SCAN_LB), F32)
        er, ei = lax.fori_loop(0, seg_len, local, (zero, zero))
        ere[...] = er
        eim[...] = ei
        pr, pi = ar1, ai1
        for _ in range(n_sq):
            pr, pi = pr * pr - pi * pi, 2.0 * pr * pi
        cr = jnp.zeros((1, SCAN_LB), F32)
        ci = jnp.zeros((1, SCAN_LB), F32)
        for jj in range(N_SEG):
            j = N_SEG - 1 - jj if reverse else jj
            cre[j:j + 1, :] = cr
            cim[j:j + 1, :] = ci
            er_j, ei_j = ere[j:j + 1, :], eim[j:j + 1, :]
            cr, ci = pr * cr - pi * ci + er_j, pr * ci + pi * cr + ei_j
        c_r, c_i = cre[...], cim[...]

        def fix(k, carry):
            qr, qi = carry
            rows = rows_of(k)
            hre_ref[rows, :] = hre_ref[rows, :] + (qr * c_r - qi * c_i)
            him_ref[rows, :] = him_ref[rows, :] + (qr * c_i + qi * c_r)
            return qr * ar - qi * ai, qr * ai + qi * ar

        lax.fori_loop(0, seg_len, fix, (ar, ai))

    nblk = N_STATE // SCAN_LB
    blk = pl.BlockSpec((s, SCAN_LB), lambda b: (0, b))
    row = pl.BlockSpec((1, SCAN_LB), lambda b: (0, b))
    return pl.pallas_call(
        body, name=name, grid=(nblk,), in_specs=[row, row, blk, blk], out_specs=[blk, blk],
        out_shape=[jax.ShapeDtypeStruct((s, N_STATE), F32)] * 2,
        scratch_shapes=[pltpu.VMEM((N_SEG, SCAN_LB), F32)] * 4,
        compiler_params=_cparams(1))(a_re, a_im, w_re, w_im)


def _ssm_da(g_re, g_im, h_re, h_im):
    s = g_re.shape[0]
    seg_len = s // N_SEG

    def body(gre_ref, gim_ref, hre_ref, him_ref, dre_ref, dim_ref):
        def rows_of(k):
            return pl.ds(pl.multiple_of(k * N_SEG, N_SEG), N_SEG)

        def step(k, carry):
            sr, si = carry
            gr, gi = gre_ref[rows_of(k), :], gim_ref[rows_of(k), :]
            pr, pi = hre_ref[rows_of(k - 1), :], him_ref[rows_of(k - 1), :]
            return sr + gr * pr + gi * pi, si + gi * pr - gr * pi

        zero = jnp.zeros((N_SEG, SCAN_LB), F32)
        sr, si = lax.fori_loop(1, seg_len, step, (zero, zero))
        last = pl.ds((seg_len - 1) * N_SEG, N_SEG)
        first_row = lax.broadcasted_iota(jnp.int32, (N_SEG, SCAN_LB), 0) == 0
        pr = jnp.where(first_row, 0.0, pltpu.roll(hre_ref[last, :], 1, 0))
        pi = jnp.where(first_row, 0.0, pltpu.roll(him_ref[last, :], 1, 0))
        gr, gi = gre_ref[pl.ds(0, N_SEG), :], gim_ref[pl.ds(0, N_SEG), :]
        sr = sr + gr * pr + gi * pi
        si = si + gi * pr - gr * pi
        dre_ref[...] = jnp.sum(sr, axis=0, keepdims=True)
        dim_ref[...] = jnp.sum(si, axis=0, keepdims=True)

    nblk = N_STATE // SCAN_LB
    blk = pl.BlockSpec((s, SCAN_LB), lambda b: (0, b))
    row = pl.BlockSpec((1, SCAN_LB), lambda b: (0, b))
    return pl.pallas_call(
        body, name="ssm_da", grid=(nblk,), in_specs=[blk] * 4, out_specs=[row, row],
        out_shape=[jax.ShapeDtypeStruct((1, N_STATE), F32)] * 2,
        compiler_params=_cparams(1))(g_re, g_im, h_re, h_im)


def _disc(ldt, are, aim, bre, bim):
    dt = jnp.exp(ldt)
    mag = jnp.exp(are * dt)
    abr = mag * jnp.cos(aim * dt)
    abi = mag * jnp.sin(aim * dt)
    den = jnp.square(are) + jnp.square(aim)
    nr = abr - 1.0
    fre = (nr * are + abi * aim) / den
    fim = (abi * are - nr * aim) / den
    return abr, abi, fre * bre - fim * bim, fre * bim + fim * bre


def _ssm_disc_fwd(ldt, are, aim, bre, bim):
    def body(l_ref, ar_ref, ai_ref, br_ref, bi_ref, o0, o1, o2, o3):
        res = _disc(l_ref[...], ar_ref[...], ai_ref[...], br_ref[...], bi_ref[...])
        for ref, val in zip((o0, o1, o2, o3), res):
            ref[...] = val

    col = jax.ShapeDtypeStruct((N_STATE, 1), F32)
    mat = jax.ShapeDtypeStruct((N_STATE, SSM_GROUP), F32)
    return pl.pallas_call(body, name="ssm_disc_fwd", out_shape=[col, col, mat, mat],
                          in_specs=[VMEM_SPEC] * 5, out_specs=[VMEM_SPEC] * 4)(ldt, are, aim, bre, bim)


def _ssm_disc_bwd(ldt, are, aim, bre, bim, d_abr, d_abi, d_bbr, d_bbi):
    def body(l_ref, ar_ref, ai_ref, br_ref, bi_ref, c0, c1, c2, c3, g_ldt, g_are, g_aim, g_bre, g_bim):
        _, vjp = jax.vjp(_disc, l_ref[...], ar_ref[...], ai_ref[...], br_ref[...], bi_ref[...])
        dl, dar, dai, dbr, dbi = vjp((c0[...], c1[...], c2[...], c3[...]))
        state = lax.broadcasted_iota(jnp.int32, (N_STATE, SSM_GROUPS), 0)
        group = lax.broadcasted_iota(jnp.int32, (N_STATE, SSM_GROUPS), 1)
        pick = jnp.right_shift(state, 6) == group
        g_ldt[...] = jnp.sum(jnp.where(pick, dl, 0.0), axis=0, keepdims=True)
        g_are[...] = dar
        g_aim[...] = dai
        g_bre[...] = dbr
        g_bim[...] = dbi

    col = jax.ShapeDtypeStruct((N_STATE, 1), F32)
    mat = jax.ShapeDtypeStruct((N_STATE, SSM_GROUP), F32)
    return pl.pallas_call(body, name="ssm_disc_bwd",
                          out_shape=[jax.ShapeDtypeStruct((1, SSM_GROUPS), F32), col, col, mat, mat],
                          in_specs=[VMEM_SPEC] * 9, out_specs=[VMEM_SPEC] * 5,
                          compiler_params=pltpu.CompilerParams(vmem_limit_bytes=VMEM_LIMIT))(
        ldt, are, aim, bre, bim, d_abr, d_abi, d_bbr, d_bbi)


_EYE8 = np.eye(8, dtype=np.float32)


def _blockdiag_b(bb):
    t = bb.reshape(SSM_CHUNKS, 8, SSM_STATE, SSM_GROUP).transpose(0, 1, 3, 2)
    return jnp.einsum("igcn,gh->igchn", t, _EYE8).reshape(SSM_CHUNKS, CH_W, CH_N)


def _diag_of_b(m):
    t = jnp.einsum("igchn,gh->igcn", m.reshape(SSM_CHUNKS, 8, SSM_GROUP, 8, SSM_STATE), _EYE8)
    return t.transpose(0, 1, 3, 2).reshape(N_STATE, SSM_GROUP)


def _blockdiag_c(c):
    t = c.reshape(SSM_CHUNKS, 8, SSM_GROUP, SSM_STATE).transpose(0, 1, 3, 2)
    return jnp.einsum("ignc,gh->ignhc", t, _EYE8).reshape(SSM_CHUNKS, CH_N, CH_W)


def _diag_of_c(m):
    t = jnp.einsum("ignhc,gh->ignc", m.reshape(SSM_CHUNKS, 8, SSM_STATE, 8, SSM_GROUP), _EYE8)
    return t.transpose(0, 1, 3, 2).reshape(SSM_GROUPS, SSM_GROUP, SSM_STATE)


def _time_perm(a):
    s, c = a.shape
    return a.reshape(N_SEG, s // N_SEG, c).transpose(1, 0, 2).reshape(s, c)


def _time_unperm(a):
    s, c = a.shape
    return a.reshape(s // N_SEG, N_SEG, c).transpose(1, 0, 2).reshape(s, c)


def _dilate(a, d):
    s, c = a.shape
    return a if d == 1 else a.reshape(s // d, d, c).transpose(1, 0, 2).reshape(s, c)


def _undilate(a, d):
    s, c = a.shape
    return a if d == 1 else a.reshape(d, s // d, c).transpose(1, 0, 2).reshape(s, c)


def _stack_dilated(parts):
    return jnp.stack([_dilate(p, d) for p, d in zip(parts, DILATIONS)], axis=0)


def _blocks_per_seq(g, n_blocks):
    return jnp.right_shift(jnp.int32(n_blocks), 2 * g)


ATT_T = 4
ATT_ROWS = ATT_T * ATT_BLK


def _window(edge_ref, cur_ref, i, sl, edge_first):
    lo = (i - 1) * ATT_BLK if edge_first else i * ATT_BLK
    if edge_first and i == 0:
        return jnp.concatenate([edge_ref[:, sl], cur_ref[0:ATT_BLK, sl]], axis=0)
    if not edge_first and i == ATT_T - 1:
        return jnp.concatenate([cur_ref[lo:lo + ATT_BLK, sl], edge_ref[:, sl]], axis=0)
    return cur_ref[lo:lo + 2 * ATT_BLK, sl]


def _band_valid(first_key):
    qi = lax.broadcasted_iota(jnp.int32, (ATT_BLK, 2 * ATT_BLK), 0)
    ki = lax.broadcasted_iota(jnp.int32, (ATT_BLK, 2 * ATT_BLK), 1)
    steps = qi + ATT_BLK - ki
    return (steps >= 0) & (steps <= ATT_BLK) & (ki >= first_key)


def _attn_specs(nb):
    cur = pl.BlockSpec((None, ATT_ROWS, ATT_GROUPW), lambda g, b: (g, b, 0))
    prev = pl.BlockSpec((None, ATT_BLK, ATT_GROUPW), lambda g, b: (g, jnp.maximum(b * ATT_T - 1, 0), 0))
    nxt = pl.BlockSpec((None, ATT_BLK, ATT_GROUPW), lambda g, b: (g, jnp.minimum((b + 1) * ATT_T, nb - 1), 0))
    return cur, prev, nxt


def _attn_fwd(q, k, v):
    s = q.shape[1]
    nb = s // ATT_BLK

    def body(q_ref, kc_ref, kp_ref, vc_ref, vp_ref, o_ref, lse_ref):
        g, bt = pl.program_id(0), pl.program_id(1)
        per_seq = _blocks_per_seq(g, nb)
        for i in range(ATT_T):
            has_prev = lax.rem(bt * ATT_T + i, per_seq) > 0
            valid = _band_valid(jnp.where(has_prev, 0, ATT_BLK))
            rows = slice(i * ATT_BLK, (i + 1) * ATT_BLK)
            for h in range(ATT_HPG):
                sl = slice(h * ATT_HEAD_DIM, (h + 1) * ATT_HEAD_DIM)
                kcat = _window(kp_ref, kc_ref, i, sl, True)
                vcat = _window(vp_ref, vc_ref, i, sl, True)
                sc = _dot(q_ref[rows, sl], kcat, "nt") * ATT_SCALE
                sc = jnp.where(valid, sc, NEG_INF)
                m = jnp.max(sc, axis=-1, keepdims=True)
                p = jnp.exp(sc - m)
                den = jnp.sum(p, axis=-1, keepdims=True)
                o_ref[rows, sl] = _dot(p, vcat, "nn") / den
                lse_ref[rows, sl] = jnp.broadcast_to(m + jnp.log(den), (ATT_BLK, ATT_HEAD_DIM))

    cur, prev, _ = _attn_specs(nb)
    return pl.pallas_call(
        body, name="attn_fwd", grid=(3, nb // ATT_T), in_specs=[cur, cur, prev, cur, prev], out_specs=[cur, cur],
        out_shape=[jax.ShapeDtypeStruct((3, s, ATT_GROUPW), F32)] * 2,
        compiler_params=_cparams(2))(q, k, k, v, v)


def _attn_dq(q, k, v, do, lse, delta):
    s = q.shape[1]
    nb = s // ATT_BLK

    def body(q_ref, kc_ref, kp_ref, vc_ref, vp_ref, do_ref, lse_ref, dl_ref, dq_ref):
        g, bt = pl.program_id(0), pl.program_id(1)
        per_seq = _blocks_per_seq(g, nb)
        for i in range(ATT_T):
            has_prev = lax.rem(bt * ATT_T + i, per_seq) > 0
            valid = _band_valid(jnp.where(has_prev, 0, ATT_BLK))
            rows = slice(i * ATT_BLK, (i + 1) * ATT_BLK)
            for h in range(ATT_HPG):
                sl = slice(h * ATT_HEAD_DIM, (h + 1) * ATT_HEAD_DIM)
                one = slice(h * ATT_HEAD_DIM, h * ATT_HEAD_DIM + 1)
                kcat = _window(kp_ref, kc_ref, i, sl, True)
                vcat = _window(vp_ref, vc_ref, i, sl, True)
                sc = _dot(q_ref[rows, sl], kcat, "nt") * ATT_SCALE
                p = jnp.exp(jnp.where(valid, sc, NEG_INF) - lse_ref[rows, one])
                dp = _dot(do_ref[rows, sl], vcat, "nt")
                ds = p * (dp - dl_ref[rows, one]) * ATT_SCALE
                dq_ref[rows, sl] = _dot(ds, kcat, "nn")

    cur, prev, _ = _attn_specs(nb)
    return pl.pallas_call(
        body, name="attn_dq", grid=(3, nb // ATT_T), in_specs=[cur, cur, prev, cur, prev, cur, cur, cur], out_specs=cur,
        out_shape=jax.ShapeDtypeStruct((3, s, ATT_GROUPW), F32),
        compiler_params=_cparams(2))(q, k, k, v, v, do, lse, delta)


def _attn_dkv(q, k, v, do, lse, delta):
    s = q.shape[1]
    nb = s // ATT_BLK

    def body(k_ref, v_ref, qc_ref, qn_ref, doc_ref, don_ref, lc_ref, ln_ref, dc_ref, dn_ref, dk_ref, dv_ref):
        g, bt = pl.program_id(0), pl.program_id(1)
        per_seq = _blocks_per_seq(g, nb)
        ri = lax.broadcasted_iota(jnp.int32, (2 * ATT_BLK, ATT_BLK), 0)
        ki = lax.broadcasted_iota(jnp.int32, (2 * ATT_BLK, ATT_BLK), 1)
        for i in range(ATT_T):
            b = bt * ATT_T + i
            next_uses = (b + 1 < nb) & (lax.rem(b + 1, per_seq) > 0)
            reach = jnp.where(next_uses, 0, 4 * ATT_BLK)
            valid = ((ri < ATT_BLK) & (ri >= ki)) | ((ri >= ATT_BLK) & (ki - ri + ATT_BLK >= reach))
            rows = slice(i * ATT_BLK, (i + 1) * ATT_BLK)
            for h in range(ATT_HPG):
                sl = slice(h * ATT_HEAD_DIM, (h + 1) * ATT_HEAD_DIM)
                one = slice(h * ATT_HEAD_DIM, h * ATT_HEAD_DIM + 1)
                qcat = _window(qn_ref, qc_ref, i, sl, False)
                docat = _window(don_ref, doc_ref, i, sl, False)
                lcat = _window(ln_ref, lc_ref, i, one, False)
                dcat = _window(dn_ref, dc_ref, i, one, False)
                sc = _dot(qcat, k_ref[rows, sl], "nt") * ATT_SCALE
                p = jnp.exp(jnp.where(valid, sc, NEG_INF) - lcat)
                dv_ref[rows, sl] = _dot(p, docat, "tn")
                dp = _dot(docat, v_ref[rows, sl], "nt")
                ds = p * (dp - dcat) * ATT_SCALE
                dk_ref[rows, sl] = _dot(ds, qcat, "tn")

    cur, _, nxt = _attn_specs(nb)
    return pl.pallas_call(
        body, name="attn_dkv", grid=(3, nb // ATT_T), in_specs=[cur, cur, cur, nxt, cur, nxt, cur, nxt, cur, nxt],
        out_specs=[cur, cur], out_shape=[jax.ShapeDtypeStruct((3, s, ATT_GROUPW), F32)] * 2,
        compiler_params=_cparams(2))(k, v, q, q, do, do, lse, lse, delta, delta)


def _xattn_probs(q, kh):
    sc = _dot(q, kh, "nt") * XATT_SCALE
    e = jnp.exp(sc - jnp.max(sc, axis=-1, keepdims=True))
    return e / jnp.sum(e, axis=-1, keepdims=True)


def _xattn_fwd(q, kv, tm=512):
    s = q.shape[0]
    tm = min(tm, s)

    def body(q_ref, kv_ref, o_ref):
        for h in range(XATT_HEADS):
            sl = slice(h * XATT_HEAD_DIM, (h + 1) * XATT_HEAD_DIM)
            vs = slice(D_MODEL + h * XATT_HEAD_DIM, D_MODEL + (h + 1) * XATT_HEAD_DIM)
            p = _xattn_probs(q_ref[:, sl], kv_ref[:, sl])
            o_ref[:, sl] = _dot(p, kv_ref[:, vs], "nn").astype(o_ref.dtype)

    return pl.pallas_call(
        body, name="xattn_fwd", grid=(s // tm,),
        in_specs=[pl.BlockSpec((tm, D_MODEL), lambda i: (i, 0)), pl.BlockSpec(kv.shape, lambda i: (0, 0))],
        out_specs=pl.BlockSpec((tm, D_MODEL), lambda i: (i, 0)),
        out_shape=jax.ShapeDtypeStruct((s, D_MODEL), MXU_DTYPE), compiler_params=_cparams(1))(q, kv)


def _xattn_bwd(q, kv, do, tm=512):
    s = q.shape[0]
    tm = min(tm, s)

    def body(q_ref, kv_ref, do_ref, dq_ref, dkv_ref):
        first = pl.program_id(0) == 0

        @pl.when(first)
        def _():
            dkv_ref[...] = jnp.zeros_like(dkv_ref)

        for h in range(XATT_HEADS):
            sl = slice(h * XATT_HEAD_DIM, (h + 1) * XATT_HEAD_DIM)
            vs = slice(D_MODEL + h * XATT_HEAD_DIM, D_MODEL + (h + 1) * XATT_HEAD_DIM)
            p = _xattn_probs(q_ref[:, sl], kv_ref[:, sl])
            dkv_ref[:, vs] += _dot(p, do_ref[:, sl], "tn")
            dp = _dot(do_ref[:, sl], kv_ref[:, vs], "nt")
            ds = p * (dp - jnp.sum(dp * p, axis=-1, keepdims=True)) * XATT_SCALE
            dq_ref[:, sl] = _dot(ds, kv_ref[:, sl], "nn").astype(dq_ref.dtype)
            dkv_ref[:, sl] += _dot(ds, q_ref[:, sl], "tn")

    row = pl.BlockSpec((tm, D_MODEL), lambda i: (i, 0))
    whole = pl.BlockSpec(kv.shape, lambda i: (0, 0))
    return pl.pallas_call(
        body, name="xattn_bwd", grid=(s // tm,), in_specs=[row, whole, row], out_specs=[row, whole],
        out_shape=[jax.ShapeDtypeStruct((s, D_MODEL), MXU_DTYPE), jax.ShapeDtypeStruct(kv.shape, F32)],
        compiler_params=_cparams(1))(q, kv, do)


def _ln(x, g, b):
    mu = jnp.mean(x, axis=-1, keepdims=True)
    xc = x - mu
    var = jnp.mean(jnp.square(xc), axis=-1, keepdims=True)
    return xc * lax.rsqrt(var + LN_EPS) * g + b


def _res_ln(h, o, g, b):
    return _ln(DEEPNORM_ALPHA * h + o, g, b)


def _gate(gs, ga, z1, z2, batt):
    return jax.nn.sigmoid(gs) * (z1 * jax.nn.sigmoid(z2)) + jax.nn.sigmoid(ga) * batt


def _rope_tables(pos, invf, m1, m2):
    ang = pos.astype(F32) * invf
    sin = jnp.sin(ang)
    return jnp.cos(ang), -sin * m1, sin * m2


def _rope(t, cos, s_up, s_dn):
    w = t.shape[-1]
    return t * cos + pltpu.roll(t, w - ROT_DIM // 2, 1) * s_up + pltpu.roll(t, ROT_DIM // 2, 1) * s_dn


def _rope_t(dt, cos, s_up, s_dn):
    w = dt.shape[-1]
    return dt * cos + pltpu.roll(dt * s_up, ROT_DIM // 2, 1) + pltpu.roll(dt * s_dn, w - ROT_DIM // 2, 1)


def _rope_consts():
    inv_freq = ROPE_THETA ** (-jnp.arange(0, ROT_DIM, 2, dtype=F32) / ROT_DIM)
    d = np.arange(ATT_GROUPW) % ATT_HEAD_DIM
    invf = jnp.where(d < ROT_DIM, inv_freq[d % (ROT_DIM // 2)], 0.0).reshape(1, ATT_GROUPW).astype(F32)
    m1 = jnp.asarray((d < ROT_DIM // 2).astype(np.float32)).reshape(1, ATT_GROUPW)
    m2 = jnp.asarray(((d >= ROT_DIM // 2) & (d < ROT_DIM)).astype(np.float32)).reshape(1, ATT_GROUPW)
    return invf, m1, m2


def _head_sum_matrix():
    d = np.arange(ATT_GROUPW) // ATT_HEAD_DIM
    return jnp.asarray((d[:, None] == d[None, :]).astype(np.float32))


def _adamw(w, g, m, v):
    m = ADAM_B1 * m + (1.0 - ADAM_B1) * g
    v = ADAM_B2 * v + (1.0 - ADAM_B2) * jnp.square(g)
    m_hat = m / (1.0 - ADAM_B1 ** ADAM_STEP)
    v_hat = v / (1.0 - ADAM_B2 ** ADAM_STEP)
    delta = -ADAM_LR * (m_hat / (jnp.sqrt(v_hat) + ADAM_EPS) + ADAM_WD * w)
    return delta, m, v


def _local_step(x, mem, pos, target, sp, wb):
    s = x.shape[0]
    al = DEEPNORM_ALPHA
    mx = MXU_DTYPE

    h0, h0b = _rowwise("ln_in", lambda x, g, b: (lambda h: (h, h))(_ln(x, g, b)), [x],
                       [sp["ln_in_g"], sp["ln_in_b"]], [(D_MODEL, F32), (D_MODEL, mx)])
    proj = _mm("proj", h0b, wb["w_in"], "nn", bias=sp["b_in"])

    ldt = jnp.repeat(sp["ssm_log_dt"].reshape(SSM_GROUPS), SSM_STATE).reshape(N_STATE, 1)
    are, aim = sp["ssm_a_re"].reshape(N_STATE, 1), sp["ssm_a_im"].reshape(N_STATE, 1)
    bre, bim = sp["ssm_b_re"].reshape(N_STATE, SSM_GROUP), sp["ssm_b_im"].reshape(N_STATE, SSM_GROUP)
    abr, abi, bbr, bbi = _ssm_disc_fwd(ldt, are, aim, bre, bim)
    a_re, a_im = abr.reshape(1, N_STATE), abi.reshape(1, N_STATE)
    bexp = jnp.concatenate([_blockdiag_b(bbr), _blockdiag_b(bbi)], axis=2).astype(mx)
    cexp = jnp.concatenate([_blockdiag_c(sp["ssm_c_re"].reshape(SSM_GROUPS, SSM_GROUP, SSM_STATE)),
                            -_blockdiag_c(sp["ssm_c_im"].reshape(SSM_GROUPS, SSM_GROUP, SSM_STATE))],
                           axis=1).astype(mx)
    u_p = _time_perm(proj[:, :SSM_WIDTH])
    w_re, w_im = _ssm_expand("ssm_bu", u_p, bexp, "nn")
    h_re, h_im = _ssm_scan("ssm_scan_fwd", w_re, w_im, a_re, a_im, reverse=False)
    y_p = _ssm_contract("ssm_ch", h_re, h_im, cexp, "nn", sp["ssm_d"], u_p)
    y = _time_unperm(y_p)
    ygb, = _rowwise("gelu", lambda y: jax.nn.gelu(y), [y], [], [(SSM_WIDTH, mx)])
    z = _mm("glu", ygb, wb["w_glu"], "nn", bias=sp["b_glu"])

    invf, m1, m2 = _rope_consts()

    def rope_fwd(pos, q0, q1, q2, k0, k1, k2, v0, v1, v2, invf, m1, m2):
        tabs = _rope_tables(pos, invf, m1, m2)
        return tuple(_rope(t, *tabs) for t in (q0, q1, q2, k0, k1, k2)) + (v0, v1, v2)

    qkv_cols = [(proj, ATT_GROUPW, 3 + i) for i in range(9)]
    qkv = _rowwise("rope", rope_fwd, [pos] + qkv_cols, [invf, m1, m2], [(ATT_GROUPW, mx)] * 9)
    q_d, k_d, v_d = _stack_dilated(qkv[0:3]), _stack_dilated(qkv[3:6]), _stack_dilated(qkv[6:9])
    o_d, lse_d = _attn_fwd(q_d, k_d, v_d)
    o_g = [_undilate(o_d[i], d) for i, d in enumerate(DILATIONS)]
    l_g = [_undilate(lse_d[i], d) for i, d in enumerate(DILATIONS)]

    def merge(o0, o1, o2, l0, l1, l2):
        m = jnp.maximum(jnp.maximum(l0, l1), l2)
        e0, e1, e2 = jnp.exp(l0 - m), jnp.exp(l1 - m), jnp.exp(l2 - m)
        tot = e0 + e1 + e2
        att = (e0 * o0 + e1 * o1 + e2 * o2) / tot
        return att, att, m + jnp.log(tot)

    att, attb, lse_tot = _rowwise("attn_merge", merge, o_g + l_g, [],
                                  [(ATT_GROUPW, F32), (ATT_GROUPW, mx), (ATT_GROUPW, F32)])
    batt = _mm("att_up", attb, wb["w_att_up"], "nn")

    gate_rows = [(proj, D_MODEL, 3), (proj, D_MODEL, 4), (z, D_MODEL, 0), (z, D_MODEL, 1), batt]
    mixedb, = _rowwise("gate", _gate, gate_rows, [], [(D_MODEL, mx)])
    o1 = _mm("mix_out", mixedb, wb["w_mix_out"], "nn", bias=sp["b_mix_out"])
    h1, h1b = _rowwise("ln1", lambda h, o, g, b: (lambda r: (r, r))(_res_ln(h, o, g, b)), [h0, o1],
                       [sp["ln1_g"], sp["ln1_b"]], [(D_MODEL, F32), (D_MODEL, mx)])

    qx = _mm("xq", h1b, wb["w_xq"], "nn", out_dtypes=(mx,))
    kvx = _mm("xkv", mem, wb["w_xkv"], "nn", out_dtypes=(mx,))
    oxb = _xattn_fwd(qx, kvx)
    o2 = _mm("xo", oxb, wb["w_xo"], "nn")
    h2, h2b = _rowwise("ln2", lambda h, o, g, b: (lambda r: (r, r))(_res_ln(h, o, g, b)), [h1, o2],
                       [sp["ln2_g"], sp["ln2_b"]], [(D_MODEL, F32), (D_MODEL, mx)])

    a_ff, fb = _mm("ff1", h2b, wb["w_ff1"], "nn", bias=sp["b_ff1"],
                   epilogue=lambda r: (r, jnp.square(jnp.maximum(r, 0.0))), out_dtypes=(F32, mx))
    o3 = _mm("ff2", fb, wb["w_ff2"], "nn", bias=sp["b_ff2"])

    def loss_bwd(h2, o3, tgt, g, b):
        def f(h2, o3, g, b):
            h3 = _res_ln(h2, o3, g, b)
            return 0.5 * jnp.sum(jnp.mean(jnp.square(h3 - tgt), axis=-1))

        loss, vjp = jax.vjp(f, h2, o3, g, b)
        _, dr, dg, db = vjp(jnp.ones((), F32))
        return dr, dr, dg, db, _colsum(dr), jnp.full((1, 128), loss, F32)

    dr3, dr3b, g_ln3_g, g_ln3_b, g_b_ff2, loss = _rowwise(
        "loss_ln3_bwd", loss_bwd, [h2, o3, target], [sp["ln3_g"], sp["ln3_b"]],
        [(D_MODEL, F32), (D_MODEL, mx)], [D_MODEL, D_MODEL, D_MODEL, 128])

    dab = _mm("ff2_dx", dr3b, wb["w_ff2"], "nt", extras=(a_ff,),
              epilogue=lambda r, a: (r * (2.0 * jnp.maximum(a, 0.0)),), out_dtypes=(mx,))
    g_w_ff2 = _mm("ff2_dw", fb, dr3b, "tn")
    g_b_ff1, = _rowwise("ff1_db", lambda v: (_colsum(v),), [dab], [], [], [D_FF])
    g_w_ff1 = _mm("ff1_dw", h2b, dab, "tn")
    dh2 = _mm("ff1_dx", dab, wb["w_ff1"], "nt", extras=(dr3,), epilogue=lambda r, d: (r + al * d,))

    def ln_bwd(h, o, dout, g, b):
        _, vjp = jax.vjp(_res_ln, h, o, g, b)
        _, dr, dg, db = vjp(dout)
        return dr, dr, dg, db, _colsum(dr)

    dr2, dr2b, g_ln2_g, g_ln2_b, _ = _rowwise(
        "ln2_bwd", ln_bwd, [h1, o2, dh2], [sp["ln2_g"], sp["ln2_b"]],
        [(D_MODEL, F32), (D_MODEL, mx)], [D_MODEL, D_MODEL, D_MODEL])
    g_w_xo = _mm("xo_dw", oxb, dr2b, "tn")
    doxb = _mm("xo_dx", dr2b, wb["w_xo"], "nt", out_dtypes=(mx,))
    dqxb, dkvx = _xattn_bwd(qx, kvx, doxb)
    g_w_xq = _mm("xq_dw", h1b, dqxb, "tn")
    dh1 = _mm("xq_dx", dqxb, wb["w_xq"], "nt", extras=(dr2,), epilogue=lambda r, d: (r + al * d,))
    g_w_xkv = _mm("xkv_dw", mem, dkvx, "tn")

    dr1, dr1b, g_ln1_g, g_ln1_b, g_b_mix = _rowwise(
        "ln1_bwd", ln_bwd, [h0, o1, dh1], [sp["ln1_g"], sp["ln1_b"]],
        [(D_MODEL, F32), (D_MODEL, mx)], [D_MODEL, D_MODEL, D_MODEL])
    g_w_mix = _mm("mix_dw", mixedb, dr1b, "tn")
    dmixed = _mm("mix_dx", dr1b, wb["w_mix_out"], "nt")

    def gate_bwd(gs, ga, z1, z2, batt, dm):
        _, vjp = jax.vjp(_gate, gs, ga, z1, z2, batt)
        dgs, dga, dz1, dz2, dbatt = vjp(dm)
        dz = jnp.concatenate([dz1, dz2], axis=-1)
        return dgs, dga, dz, dbatt, _colsum(dz)

    dgsb, dgab, dzb, dbattb, g_b_glu = _rowwise(
        "gate_bwd", gate_bwd, gate_rows + [dmixed], [],
        [(D_MODEL, mx), (D_MODEL, mx), (2 * D_MODEL, mx), (D_MODEL, mx)], [2 * D_MODEL])
    g_w_up = _mm("att_up_dw", attb, dbattb, "tn")
    datt = _mm("att_up_dx", dbattb, wb["w_att_up"], "nt")

    def att_delta(datt, att, hs):
        dl = jnp.dot(datt * att, hs, precision=lax.Precision.HIGHEST, preferred_element_type=F32)
        return datt, dl

    dattb, delta = _rowwise("attn_delta", att_delta, [datt, att], [_head_sum_matrix()],
                            [(ATT_GROUPW, mx), (ATT_GROUPW, F32)])
    do_d = _stack_dilated([dattb] * 3)
    lt_d = _stack_dilated([lse_tot] * 3)
    dl_d = _stack_dilated([delta] * 3)
    dq_d = _attn_dq(q_d, k_d, v_d, do_d, lt_d, dl_d)
    dk_d, dv_d = _attn_dkv(q_d, k_d, v_d, do_d, lt_d, dl_d)
    dqkv = [_undilate(t[i], d) for t in (dq_d, dk_d, dv_d) for i, d in enumerate(DILATIONS)]

    def rope_bwd(pos, q0, q1, q2, k0, k1, k2, v0, v1, v2, invf, m1, m2):
        tabs = _rope_tables(pos, invf, m1, m2)
        return jnp.concatenate([_rope_t(t, *tabs) for t in (q0, q1, q2, k0, k1, k2)] + [v0, v1, v2], axis=-1)

    dqkvb, = _rowwise("rope_bwd", rope_bwd, [pos] + dqkv, [invf, m1, m2], [(9 * ATT_GROUPW, mx)])

    g_w_glu = _mm("glu_dw", ygb, dzb, "tn")
    dyg = _mm("glu_dx", dzb, wb["w_glu"], "nt")

    def gelu_bwd(y, dyg):
        _, vjp = jax.vjp(jax.nn.gelu, y)
        return vjp(dyg)[0]

    dy, = _rowwise("gelu_bwd", gelu_bwd, [y, dyg], [], [(SSM_WIDTH, F32)])
    dy_p = _time_perm(dy)
    dh_re, dh_im = _ssm_expand("ssm_dh", dy_p, cexp, "nt")
    g_cexp = _ssm_wgrad("ssm_dc", dy_p, h_re, h_im, expand=False)
    s_re, s_im = _ssm_scan("ssm_scan_bwd", dh_re, dh_im, a_re, a_im, reverse=True)
    d_abr, d_abi = _ssm_da(s_re, s_im, h_re, h_im)
    g_bexp = _ssm_wgrad("ssm_db", u_p, s_re, s_im, expand=True)
    du_p = _ssm_contract("ssm_du", s_re, s_im, bexp, "nt", sp["ssm_d"], dy_p)
    g_ssm_d, = _rowwise("ssm_dd", lambda a, b: (_colsum(a * b),), [dy_p, u_p], [], [], [SSM_WIDTH])
    g_ldt, g_are, g_aim, g_bre, g_bim = _ssm_disc_bwd(
        ldt, are, aim, bre, bim, d_abr.reshape(N_STATE, 1), d_abi.reshape(N_STATE, 1),
        _diag_of_b(g_bexp[:, :, :CH_N]), _diag_of_b(g_bexp[:, :, CH_N:]))
    g_c_re = _diag_of_c(g_cexp[:, :CH_N, :])
    g_c_im = -_diag_of_c(g_cexp[:, CH_N:, :])
    dub = _time_unperm(du_p).astype(mx)

    dprojb = jnp.concatenate([dub, dqkvb, dgsb, dgab], axis=-1)
    g_b_in, = _rowwise("in_db", lambda v: (_colsum(v),), [dprojb], [], [], [IN_COLS])
    g_w_in = _mm("in_dw", h0b, dprojb, "tn")
    dh0 = _mm("in_dx", dprojb, wb["w_in"], "nt", extras=(dr1,), epilogue=lambda r, d: (r + al * d,))

    def ln_in_bwd(x, dout, g, b):
        _, vjp = jax.vjp(_ln, x, g, b)
        return vjp(dout)

    dx, g_ln_in_g, g_ln_in_b = _rowwise("ln_in_bwd", ln_in_bwd, [x, dh0], [sp["ln_in_g"], sp["ln_in_b"]],
                                        [(D_MODEL, F32)], [D_MODEL, D_MODEL])

    big = {"w_in": g_w_in, "w_glu": g_w_glu, "w_att_up": g_w_up, "w_mix_out": g_w_mix, "w_xq": g_w_xq,
           "w_xkv": g_w_xkv, "w_xo": g_w_xo, "w_ff1": g_w_ff1, "w_ff2": g_w_ff2}
    small = {"ln_in_g": g_ln_in_g, "ln_in_b": g_ln_in_b, "b_in": g_b_in, "ssm_log_dt": g_ldt, "ssm_a_re": g_are,
             "ssm_a_im": g_aim, "ssm_b_re": g_bre, "ssm_b_im": g_bim, "ssm_c_re": g_c_re, "ssm_c_im": g_c_im,
             "ssm_d": g_ssm_d, "b_glu": g_b_glu, "b_mix_out": g_b_mix, "ln1_g": g_ln1_g, "ln1_b": g_ln1_b,
             "ln2_g": g_ln2_g, "ln2_b": g_ln2_b, "b_ff1": g_b_ff1, "b_ff2": g_b_ff2, "ln3_g": g_ln3_g,
             "ln3_b": g_ln3_b}
    return loss, dx, big, small


def _piece_shape(k, n, axis):
    return (k // 2, n // 4) if axis == 1 else (k // 8, n)


def _aligned(v, m):
    return v if isinstance(v, int) else pl.multiple_of(v, m)


def _full_piece(ref, k, n, axis, chip, half):
    pr, pc = _piece_shape(k, n, axis)
    if axis == 1:
        return ref.at[pl.ds(_aligned(half * pr, 8), pr), pl.ds(_aligned(chip * pc, 128), pc)]
    return ref.at[pl.ds(_aligned(chip * (2 * pr) + half * pr, 8), pr), :]


def _full_shard(ref, k, n, axis, chip):
    if axis == 1:
        return ref.at[:, pl.ds(_aligned(chip * (n // 4), 128), n // 4)]
    return ref.at[pl.ds(_aligned(chip * (k // 4), 8), k // 4), :]


def _shard_piece(ref, k, n, axis, half):
    pr, _ = _piece_shape(k, n, axis)
    return ref.at[pl.ds(_aligned(half * pr, 8), pr), :]


def _mesh_pos():
    x, y, c = lax.axis_index("x"), lax.axis_index("y"), lax.axis_index("c")
    other_chips = [(1 - x, y), (x, 1 - y), (1 - x, 1 - y)]
    return x, y, c, other_chips


def _remote(src, dst, send_sem, recv_sem, dev):
    return pltpu.make_async_remote_copy(src_ref=src, dst_ref=dst, send_sem=send_sem, recv_sem=recv_sem,
                                        device_id=dev, device_id_type=MESH)


def _placed(name, fn, n_steps, where, ins, out_sds, out_block, out_index):
    def body(w_ref, *refs):
        o_ref = refs[-1]
        o_ref[...] = fn(*[r[...] for r in refs[:-1]]).astype(o_ref.dtype)

    grid_spec = pltpu.PrefetchScalarGridSpec(
        num_scalar_prefetch=1, grid=(n_steps,), in_specs=[pl.BlockSpec(bs, idx) for _, bs, idx in ins],
        out_specs=pl.BlockSpec(out_block, out_index))
    return pl.pallas_call(body, name=name, grid_spec=grid_spec, out_shape=out_sds,
                          compiler_params=_cparams(1))(where, *[a for a, _, _ in ins])


def _gather_weights(fulls):
    nw = len(BIG)

    def body(*refs):
        full = refs[nw:2 * nw]
        send_sems, recv_sems = refs[2 * nw:]
        x, y, c, chips = _mesh_pos()
        me = 2 * x + y
        sib = (x, y, 1 - c)
        first, fwd = [], []
        for wi, (_, k, n, ax) in enumerate(BIG):
            mine = _full_piece(full[wi], k, n, ax, me, c)
            for j, (qx, qy) in enumerate(chips):
                cp = _remote(mine, mine, send_sems.at[wi * 6 + j], recv_sems.at[wi * 6 + j], (qx, qy, c))
                cp.start()
                first.append(cp)
        for wi, (_, k, n, ax) in enumerate(BIG):
            for j, (qx, qy) in enumerate(chips):
                piece = _full_piece(full[wi], k, n, ax, 2 * qx + qy, c)
                _remote(piece, piece, send_sems.at[wi * 6 + j], recv_sems.at[wi * 6 + j], (qx, qy, c)).wait_recv()
                cp = _remote(piece, piece, send_sems.at[wi * 6 + 3 + j], recv_sems.at[wi * 6 + 3 + j], sib)
                cp.start()
                fwd.append(cp)
        for wi, (_, k, n, ax) in enumerate(BIG):
            for j, (qx, qy) in enumerate(chips):
                piece = _full_piece(full[wi], k, n, ax, 2 * qx + qy, 1 - c)
                _remote(piece, piece, send_sems.at[wi * 6 + 3 + j], recv_sems.at[wi * 6 + 3 + j], sib).wait_recv()
        for cp in first + fwd:
            cp.wait_send()

    return pl.pallas_call(
        body, name="gather_weights", in_specs=[HBM_SPEC] * nw, out_specs=[HBM_SPEC] * nw,
        out_shape=[jax.ShapeDtypeStruct((k, n), MXU_DTYPE) for _, k, n, _ in BIG],
        input_output_aliases={i: i for i in range(nw)},
        scratch_shapes=[pltpu.SemaphoreType.DMA((6 * nw,)), pltpu.SemaphoreType.DMA((6 * nw,))])(*fulls)


def _reduce_swap_halves(grads):
    nw = len(BIG)

    def body(*refs):
        g, got = refs[:nw], refs[nw:2 * nw]
        send_sems, recv_sems = refs[2 * nw:]
        x, y, c, _ = _mesh_pos()
        sib = (x, y, 1 - c)
        cps = []
        for wi, (_, k, n, ax) in enumerate(BIG):
            for q in range(4):
                cp = _remote(_full_piece(g[wi], k, n, ax, q, 1 - c), got[wi].at[q],
                             send_sems.at[wi * 4 + q], recv_sems.at[wi * 4 + q], sib)
                cp.start()
                cps.append(cp)
        for cp in cps:
            cp.wait()

    return pl.pallas_call(
        body, name="reduce_swap_halves", in_specs=[HBM_SPEC] * nw, out_specs=[HBM_SPEC] * nw,
        out_shape=[jax.ShapeDtypeStruct((4,) + _piece_shape(k, n, ax), F32) for _, k, n, ax in BIG],
        scratch_shapes=[pltpu.SemaphoreType.DMA((4 * nw,)), pltpu.SemaphoreType.DMA((4 * nw,))])(*grads)


def _reduce_to_owner(parts):
    nw = len(BIG)

    def body(*refs):
        p, out = refs[:nw], refs[nw:2 * nw]
        send_sems, recv_sems = refs[2 * nw:]
        x, y, c, chips = _mesh_pos()
        cps = []
        for wi in range(nw):
            for j, (qx, qy) in enumerate(chips):
                cp = _remote(p[wi].at[2 * qx + qy], out[wi].at[j], send_sems.at[wi * 3 + j],
                             recv_sems.at[wi * 3 + j], (qx, qy, c))
                cp.start()
                cps.append(cp)
        for cp in cps:
            cp.wait()

    return pl.pallas_call(
        body, name="reduce_to_owner", in_specs=[HBM_SPEC] * nw, out_specs=[HBM_SPEC] * nw,
        out_shape=[jax.ShapeDtypeStruct((3,) + p.shape[1:], p.dtype) for p in parts],
        scratch_shapes=[pltpu.SemaphoreType.DMA((3 * nw,)), pltpu.SemaphoreType.DMA((3 * nw,))])(*parts)


def _share_with_sibling(shards):
    nw = len(BIG)

    def body(*refs):
        out = refs[nw:2 * nw]
        send_sems, recv_sems = refs[2 * nw:]
        x, y, c, _ = _mesh_pos()
        sib = (x, y, 1 - c)
        cps = []
        for wi, (_, k, n, ax) in enumerate(BIG):
            mine = _shard_piece(out[wi], k, n, ax, c)
            cp = _remote(mine, mine, send_sems.at[wi], recv_sems.at[wi], sib)
            cp.start()
            cps.append(cp)
        for wi, (_, k, n, ax) in enumerate(BIG):
            piece = _shard_piece(out[wi], k, n, ax, 1 - c)
            _remote(piece, piece, send_sems.at[wi], recv_sems.at[wi], sib).wait_recv()
        for cp in cps:
            cp.wait_send()

    return pl.pallas_call(
        body, name="share_with_sibling", in_specs=[HBM_SPEC] * nw, out_specs=[HBM_SPEC] * nw,
        out_shape=[jax.ShapeDtypeStruct(sh.shape, sh.dtype) for sh in shards],
        input_output_aliases={i: i for i in range(nw)},
        scratch_shapes=[pltpu.SemaphoreType.DMA((nw,)), pltpu.SemaphoreType.DMA((nw,))])(*shards)


def _allreduce_small(v):
    r = v.shape[0]
    rh = r // 2
    assert rh % 8 == 0

    def body(v_ref, o_ref, sib_buf, chip_buf, send_sems, recv_sems):
        x, y, c, chips = _mesh_pos()
        me = 2 * x + y
        sib = (x, y, 1 - c)
        mine = pl.ds(pl.multiple_of(c * rh, 8), rh)
        other = pl.ds(pl.multiple_of((1 - c) * rh, 8), rh)
        swap = _remote(v_ref.at[other], sib_buf, send_sems.at[0], recv_sems.at[0], sib)
        swap.start()
        swap.wait()
        chip_buf[me] = v_ref[mine, :] + sib_buf[...]
        cps = []
        for j, (qx, qy) in enumerate(chips):
            cp = _remote(chip_buf.at[me], chip_buf.at[me], send_sems.at[1 + j], recv_sems.at[1 + j], (qx, qy, c))
            cp.start()
            cps.append(cp)
        for j, (qx, qy) in enumerate(chips):
            slot = chip_buf.at[2 * qx + qy]
            _remote(slot, slot, send_sems.at[1 + j], recv_sems.at[1 + j], (qx, qy, c)).wait_recv()
        for cp in cps:
            cp.wait_send()
        o_ref[mine, :] = ((chip_buf[0] + chip_buf[1]) + chip_buf[2]) + chip_buf[3]
        back = _remote(o_ref.at[mine], o_ref.at[mine], send_sems.at[4], recv_sems.at[4], sib)
        back.start()
        _remote(o_ref.at[other], o_ref.at[other], send_sems.at[4], recv_sems.at[4], sib).wait_recv()
        back.wait_send()

    return pl.pallas_call(
        body, name="allreduce_small", in_specs=[VMEM_SPEC], out_specs=VMEM_SPEC,
        out_shape=jax.ShapeDtypeStruct((r, 128), F32),
        scratch_shapes=[pltpu.VMEM((rh, 128), F32), pltpu.VMEM((4, rh, 128), F32),
                        pltpu.SemaphoreType.DMA((5,)), pltpu.SemaphoreType.DMA((5,))],
        compiler_params=pltpu.CompilerParams(vmem_limit_bytes=VMEM_LIMIT))(v)


def _as2d(a):
    a = a.reshape((-1, a.shape[-1])) if a.ndim > 1 else a.reshape(1, -1)
    return a


def _adamw_small(quads):
    n = len(quads)

    def body(*refs):
        for i in range(n):
            w, g, m, v = (r[...] for r in refs[4 * i:4 * i + 4])
            for ref, val in zip(refs[4 * n + 3 * i:4 * n + 3 * i + 3], _adamw(w, g, m, v)):
                ref[...] = val

    return pl.pallas_call(
        body, name="adamw_small", in_specs=[VMEM_SPEC] * (4 * n), out_specs=[VMEM_SPEC] * (3 * n),
        out_shape=[jax.ShapeDtypeStruct(q[0].shape, F32) for q in quads for _ in range(3)],
        compiler_params=pltpu.CompilerParams(vmem_limit_bytes=VMEM_LIMIT))(*[a for q in quads for a in q])


def _where():
    return jnp.stack([2 * lax.axis_index("x") + lax.axis_index("y"), lax.axis_index("c")]).astype(jnp.int32)


def _gather_all(inputs, where):
    fulls = []
    for name, k, n, ax in BIG:
        w2 = inputs[name][0]
        rs, cs = w2.shape
        tm = _tile(rs, 512)
        steps = rs // tm
        if ax == 1:
            blk, idx = (tm, cs), lambda i, w: (i, w[0])
        else:
            blk, idx = (tm, n), functools.partial(lambda i, w, steps: (w[0] * steps + i, 0), steps=steps)
        fulls.append(_placed("cast_" + name, lambda w: w, steps, where, [(w2, (tm, cs), lambda i, w: (i, 0))],
                             jax.ShapeDtypeStruct((k, n), MXU_DTYPE), blk, idx))
    return _gather_weights(fulls)


def _reduce_all(inputs, grads, where):
    got = _reduce_swap_halves(grads)
    parts, geom = [], []
    for i, (name, k, n, ax) in enumerate(BIG):
        pr, pc = _piece_shape(k, n, ax)
        tm = _tile(pr, 512)
        spp = pr // tm
        geom.append((pr, pc, tm, spp))
        if ax == 1:
            g_idx = functools.partial(lambda i, w, spp: (w[1] * spp + i % spp, i // spp), spp=spp)
        else:
            g_idx = functools.partial(lambda i, w, spp: ((i // spp) * 2 * spp + w[1] * spp + i % spp, 0), spp=spp)
        parts.append(_placed("pair_sum_" + name, lambda a, b: a + b, 4 * spp, where,
                             [(grads[i], (tm, pc), g_idx), (got[i].reshape(4 * pr, pc), (tm, pc), lambda i, w: (i, 0))],
                             jax.ShapeDtypeStruct((4 * pr, pc), BF16), (tm, pc), lambda i, w: (i, 0)).reshape(4, pr, pc))
    landed = _reduce_to_owner(parts)
    halves = []
    for i, (name, k, n, ax) in enumerate(BIG):
        pr, pc, tm, spp = geom[i]
        shard_shape = inputs[name].shape[1:]
        ins = [(parts[i], (None, tm, pc), lambda i, w: (w[0], i, 0))]
        ins += [(landed[i], (None, tm, pc), functools.partial(lambda i, w, j: (j, i, 0), j=j)) for j in range(3)]
        halves.append(_placed("chip_sum_" + name,
                              lambda a, b, c, d: ((a.astype(F32) + b.astype(F32)) + c.astype(F32)) + d.astype(F32),
                              spp, where, ins, jax.ShapeDtypeStruct(shard_shape, F32), (tm, pc),
                              functools.partial(lambda i, w, spp: (w[1] * spp + i, 0), spp=spp)))
    return _share_with_sibling(halves)


def _step(inputs):
    x, mem, positions, target = inputs["x"][0], inputs["mem"][0], inputs["positions"], inputs["loss_target"][0]
    pos = positions.reshape(-1, 1)
    where = _where()
    full = _gather_all(inputs, where)
    wb = {name: full[i] for i, (name, _, _, _) in enumerate(BIG)}
    sp = {name: _as2d(inputs[name]) for name in SMALL}
    memb, = _rowwise("cast_mem", lambda m: (m,), [mem], [], [(D_MODEL, MXU_DTYPE)])

    loss, dx, gbig, gsmall = _local_step(x, memb, pos, target, sp, wb)
    gshard = _reduce_all(inputs, [gbig[name] for name, _, _, _ in BIG], where)

    out = {}
    for i, (name, _, _, _) in enumerate(BIG):
        w2, m2, v2 = inputs[name][0], inputs["m_" + name][0], inputs["v_" + name][0]
        n = w2.shape[1]
        d, nm, nv = _rowwise("adamw_" + name, _adamw, [w2, gshard[i], m2, v2], [], [(n, F32)] * 3, tm=128)
        lead = inputs[name].shape
        out[name] = (gshard[i].reshape(lead), d.reshape(lead), nm.reshape(lead), nv.reshape(lead))

    def tiles(a):
        flat = a.reshape(-1)
        n = -(-flat.shape[0] // 1024) * 1024
        return jnp.pad(flat, (0, n - flat.shape[0])).reshape(n // 128, 128)

    pieces = [tiles(loss[:, :1])] + [tiles(gsmall[name]) for name in SMALL]
    if sum(p.shape[0] for p in pieces) % 16:
        pieces.append(jnp.zeros((8, 128), F32))
    red = _allreduce_small(jnp.concatenate(pieces, axis=0))
    loss_total = red[0, 0]
    grads, off = {}, pieces[0].shape[0]
    for name, p in zip(SMALL, pieces[1:]):
        shp = _as2d(inputs[name]).shape
        grads[name] = red[off:off + p.shape[0]].reshape(-1)[:shp[0] * shp[1]].reshape(shp)
        off += p.shape[0]
    upd = _adamw_small([(_as2d(inputs[n]), grads[n], _as2d(inputs["m_" + n]), _as2d(inputs["v_" + n])) for n in SMALL])
    for i, name in enumerate(SMALL):
        shp = inputs[name].shape
        out[name] = (grads[name].reshape(shp),) + tuple(t.reshape(shp) for t in upd[3 * i:3 * i + 3])
    return loss_total, dx.reshape(inputs["x"].shape), out


_ARG_NAMES = (("x", "mem", "positions") + WEIGHT_ORDER + ("loss_target",) + tuple("m_" + n for n in WEIGHT_ORDER)
              + tuple("v_" + n for n in WEIGHT_ORDER))


def kernel(x, mem, positions, ln_in_g, ln_in_b, w_in, b_in, ssm_log_dt, ssm_a_re, ssm_a_im, ssm_b_re, ssm_b_im, ssm_c_re, ssm_c_im, ssm_d, w_glu, b_glu, w_att_up, w_mix_out, b_mix_out, ln1_g, ln1_b, w_xq, w_xkv, w_xo, ln2_g, ln2_b, w_ff1, b_ff1, w_ff2, b_ff2, ln3_g, ln3_b, loss_target, m_ln_in_g, m_ln_in_b, m_w_in, m_b_in, m_ssm_log_dt, m_ssm_a_re, m_ssm_a_im, m_ssm_b_re, m_ssm_b_im, m_ssm_c_re, m_ssm_c_im, m_ssm_d, m_w_glu, m_b_glu, m_w_att_up, m_w_mix_out, m_b_mix_out, m_ln1_g, m_ln1_b, m_w_xq, m_w_xkv, m_w_xo, m_ln2_g, m_ln2_b, m_w_ff1, m_b_ff1, m_w_ff2, m_b_ff2, m_ln3_g, m_ln3_b, v_ln_in_g, v_ln_in_b, v_w_in, v_b_in, v_ssm_log_dt, v_ssm_a_re, v_ssm_a_im, v_ssm_b_re, v_ssm_b_im, v_ssm_c_re, v_ssm_c_im, v_ssm_d, v_w_glu, v_b_glu, v_w_att_up, v_w_mix_out, v_b_mix_out, v_ln1_g, v_ln1_b, v_w_xq, v_w_xkv, v_w_xo, v_ln2_g, v_ln2_b, v_w_ff1, v_b_ff1, v_w_ff2, v_b_ff2, v_ln3_g, v_ln3_b):
    args = (x, mem, positions, ln_in_g, ln_in_b, w_in, b_in, ssm_log_dt, ssm_a_re, ssm_a_im, ssm_b_re, ssm_b_im, ssm_c_re, ssm_c_im, ssm_d, w_glu, b_glu, w_att_up, w_mix_out, b_mix_out, ln1_g, ln1_b, w_xq, w_xkv, w_xo, ln2_g, ln2_b, w_ff1, b_ff1, w_ff2, b_ff2, ln3_g, ln3_b, loss_target, m_ln_in_g, m_ln_in_b, m_w_in, m_b_in, m_ssm_log_dt, m_ssm_a_re, m_ssm_a_im, m_ssm_b_re, m_ssm_b_im, m_ssm_c_re, m_ssm_c_im, m_ssm_d, m_w_glu, m_b_glu, m_w_att_up, m_w_mix_out, m_b_mix_out, m_ln1_g, m_ln1_b, m_w_xq, m_w_xkv, m_w_xo, m_ln2_g, m_ln2_b, m_w_ff1, m_b_ff1, m_w_ff2, m_b_ff2, m_ln3_g, m_ln3_b, v_ln_in_g, v_ln_in_b, v_w_in, v_b_in, v_ssm_log_dt, v_ssm_a_re, v_ssm_a_im, v_ssm_b_re, v_ssm_b_im, v_ssm_c_re, v_ssm_c_im, v_ssm_d, v_w_glu, v_b_glu, v_w_att_up, v_w_mix_out, v_b_mix_out, v_ln1_g, v_ln1_b, v_w_xq, v_w_xkv, v_w_xo, v_ln2_g, v_ln2_b, v_w_ff1, v_b_ff1, v_w_ff2, v_b_ff2, v_ln3_g, v_ln3_b)
    assert len(args) == len(_ARG_NAMES)
    inputs = dict(zip(_ARG_NAMES, args))
    loss, dx, out = _step(inputs)
    res = [loss, dx]
    for k in range(4):
        res += [out[name][k] for name in WEIGHT_ORDER]
    return tuple(res)
```

```python
import functools
import math

import numpy as np
import jax
import jax.numpy as jnp
from jax import lax
from jax.experimental import pallas as pl
from jax.experimental.pallas import tpu as pltpu

F32 = jnp.float32
BF16 = jnp.bfloat16
MXU_DTYPE = jnp.bfloat16

D_MODEL = 1024
SSM_GROUP = 16
SSM_WIDTH = 768
SSM_GROUPS = 48
SSM_STATE = 64
N_STATE = SSM_GROUPS * SSM_STATE
SSM_CHUNKS = 6
CH_W = 128
CH_N = 512
ATT_HEAD_DIM = 64
ATT_HPG = 4
ATT_GROUPW = ATT_HPG * ATT_HEAD_DIM
DILATIONS = (1, 4, 16)
ATT_BLK = 128
ATT_SCALE = ATT_HEAD_DIM ** -0.5
ROT_DIM = 16
ROPE_THETA = 500000.0
XATT_HEADS = 4
XATT_HEAD_DIM = 256
XATT_SCALE = XATT_HEAD_DIM ** -0.5
D_FF = 4096
IN_COLS = 5120
DEEPNORM_ALPHA = 2.0 ** 0.25
LN_EPS = 1e-5
NEG_INF = -1e30
ADAM_LR = 0.001
ADAM_B1 = 0.9
ADAM_B2 = 0.999
ADAM_EPS = 1e-08
ADAM_WD = 0.01
ADAM_STEP = 10

N_SEG = 32
VMEM_LIMIT = 48 * 1024 * 1024
MESH = pl.DeviceIdType.MESH
HBM_SPEC = pl.BlockSpec(memory_space=pltpu.HBM)
VMEM_SPEC = pl.BlockSpec(memory_space=pltpu.VMEM)

BIG = (("w_in", 1024, 5120, 1), ("w_glu", 768, 2048, 1), ("w_att_up", 256, 1024, 1),
       ("w_mix_out", 1024, 1024, 0), ("w_xq", 1024, 1024, 0), ("w_xkv", 1024, 2048, 1),
       ("w_xo", 1024, 1024, 0), ("w_ff1", 1024, 4096, 1), ("w_ff2", 4096, 1024, 0))
SMALL = ("ln_in_g", "ln_in_b", "b_in", "ssm_log_dt", "ssm_a_re", "ssm_a_im", "ssm_b_re", "ssm_b_im",
         "ssm_c_re", "ssm_c_im", "ssm_d", "b_glu", "b_mix_out", "ln1_g", "ln1_b", "ln2_g", "ln2_b",
         "b_ff1", "b_ff2", "ln3_g", "ln3_b")
WEIGHT_ORDER = ("ln_in_g", "ln_in_b", "w_in", "b_in", "ssm_log_dt", "ssm_a_re", "ssm_a_im", "ssm_b_re",
                "ssm_b_im", "ssm_c_re", "ssm_c_im", "ssm_d", "w_glu", "b_glu", "w_att_up", "w_mix_out",
                "b_mix_out", "ln1_g", "ln1_b", "w_xq", "w_xkv", "w_xo", "ln2_g", "ln2_b", "w_ff1", "b_ff1",
                "w_ff2", "b_ff2", "ln3_g", "ln3_b")


def _cparams(n_axes):
    return pltpu.CompilerParams(dimension_semantics=("arbitrary",) * n_axes, vmem_limit_bytes=VMEM_LIMIT)


def _rowwise(name, fn, rows, consts, outs, reds=(), tm=256, touts=()):
    n_rows = (rows[0][0] if isinstance(rows[0], tuple) else rows[0]).shape[-2]
    tm = min(tm, n_rows)
    assert n_rows % tm == 0, (name, n_rows, tm)
    specs, args = [], []
    for r in rows:
        if isinstance(r, tuple) and len(r) == 3:
            arr, width, cb = r
            specs.append(pl.BlockSpec((tm, width), functools.partial(lambda i, cb: (i, cb), cb=cb)))
        elif isinstance(r, tuple):
            arr, slot = r
            specs.append(pl.BlockSpec((None, tm, arr.shape[2]), functools.partial(lambda i, s: (s, i, 0), s=slot)))
        else:
            arr = r
            specs.append(pl.BlockSpec((tm, arr.shape[1]), lambda i: (i, 0)))
        args.append(arr)
        assert arr.shape[-2] == n_rows, (name, arr.shape, n_rows)
    for cst in consts:
        specs.append(pl.BlockSpec(cst.shape, lambda i: (0, 0)))
        args.append(cst)
    n_r, n_c, n_o, n_d = len(rows), len(consts), len(outs) + len(touts), len(reds)
    out_shape = [jax.ShapeDtypeStruct((n_rows, c), dt) for c, dt in outs]
    out_specs = [pl.BlockSpec((tm, c), lambda i: (i, 0)) for c, _ in outs]
    out_shape += [jax.ShapeDtypeStruct((r, n_rows), dt) for r, dt in touts]
    out_specs += [pl.BlockSpec((r, tm), lambda i: (0, i)) for r, _ in touts]
    out_shape += [jax.ShapeDtypeStruct((1, c), F32) for c in reds]
    out_specs += [pl.BlockSpec((1, c), lambda i: (0, 0)) for c in reds]

    def body(*refs):
        ins = [r[...] for r in refs[:n_r + n_c]]
        o_refs = refs[n_r + n_c:n_r + n_c + n_o]
        d_refs = refs[n_r + n_c + n_o:]
        res = fn(*ins)
        res = res if isinstance(res, (tuple, list)) else (res,)
        assert len(res) == n_o + n_d, (name, len(res))
        for ref, val in zip(o_refs, res[:n_o]):
            ref[...] = val.astype(ref.dtype)
        first = pl.program_id(0) == 0
        for ref, val in zip(d_refs, res[n_o:]):
            @pl.when(first)
            def _(ref=ref, val=val):
                ref[...] = val

            @pl.when(jnp.logical_not(first))
            def _(ref=ref, val=val):
                ref[...] += val

    res = pl.pallas_call(body, name=name, grid=(n_rows // tm,), in_specs=specs, out_specs=out_specs,
                         out_shape=out_shape, compiler_params=_cparams(1))(*args)
    return res


def _colsum(v):
    return jnp.sum(v.astype(F32), axis=0, keepdims=True)


_DIMS = {"nn": (((1,), (0,)), ((), ())), "nt": (((1,), (1,)), ((), ())), "tn": (((0,), (0,)), ((), ()))}


def _tile(dim, want):
    if dim <= want:
        return dim
    return max(t for t in range(128, want + 1, 128) if dim % t == 0)


def _dot(a, b, mode):
    return lax.dot_general(a.astype(MXU_DTYPE), b.astype(MXU_DTYPE), _DIMS[mode], preferred_element_type=F32)


def _mm(name, a, b, mode, *, bias=None, extras=(), epilogue=None, out_dtypes=(F32,), tm=1024, tn=1024, tk=1024):
    if mode == "nn":
        (m, k), (_, n) = a.shape, b.shape
    elif mode == "nt":
        (m, k), (n, _) = a.shape, b.shape
    else:
        (k, m), (_, n) = a.shape, b.shape
    tm, tn = _tile(m, tm), _tile(n, tn)
    if mode != "tn":
        tk = k if k <= 1024 else tk
    tk = _tile(k, tk)
    assert m % tm == 0 and n % tn == 0 and k % tk == 0, (name, m, n, k)
    nk = k // tk
    a_spec = {"nn": pl.BlockSpec((tm, tk), lambda i, j, kk: (i, kk)),
              "nt": pl.BlockSpec((tm, tk), lambda i, j, kk: (i, kk)),
              "tn": pl.BlockSpec((tk, tm), lambda i, j, kk: (kk, i))}[mode]
    b_spec = {"nn": pl.BlockSpec((tk, tn), lambda i, j, kk: (kk, j)),
              "nt": pl.BlockSpec((tn, tk), lambda i, j, kk: (j, kk)),
              "tn": pl.BlockSpec((tk, tn), lambda i, j, kk: (kk, j))}[mode]
    specs, args = [a_spec, b_spec], [a, b]
    if bias is not None:
        specs.append(pl.BlockSpec((1, tn), lambda i, j, kk: (0, j)))
        args.append(bias)
    for e in extras:
        specs.append(pl.BlockSpec((tm, tn), lambda i, j, kk: (i, j)))
        args.append(e)
    n_e, n_o = len(extras), len(out_dtypes)
    has_bias = bias is not None

    def body(*refs):
        a_ref, b_ref = refs[0], refs[1]
        pos = 2
        bias_ref = refs[pos] if has_bias else None
        pos += int(has_bias)
        e_refs = refs[pos:pos + n_e]
        o_refs = refs[pos + n_e:pos + n_e + n_o]
        acc_ref = refs[pos + n_e + n_o] if nk > 1 else None
        part = _dot(a_ref[...], b_ref[...], mode)

        def finish(r):
            if has_bias:
                r = r + bias_ref[...]
            res = epilogue(r, *[e[...] for e in e_refs]) if epilogue is not None else (r,)
            for ref, val in zip(o_refs, res):
                ref[...] = val.astype(ref.dtype)

        if nk == 1:
            finish(part)
        else:
            kk = pl.program_id(2)

            @pl.when(kk == 0)
            def _():
                acc_ref[...] = part

            @pl.when(kk > 0)
            def _():
                acc_ref[...] += part

            @pl.when(kk == nk - 1)
            def _():
                finish(acc_ref[...])

    res = pl.pallas_call(
        body, name=name, grid=(m // tm, n // tn, nk), in_specs=specs,
        out_specs=[pl.BlockSpec((tm, tn), lambda i, j, kk: (i, j)) for _ in out_dtypes],
        out_shape=[jax.ShapeDtypeStruct((m, n), dt) for dt in out_dtypes],
        scratch_shapes=[pltpu.VMEM((tm, tn), F32)] if nk > 1 else [],
        compiler_params=_cparams(3))(*args)
    return res[0] if n_o == 1 else res


def _ssm_expand(name, a, bmat, mode, tm=512):
    s = a.shape[0]
    tm = min(tm, s)

    def body(a_ref, b_ref, re_ref, im_ref):
        r = _dot(a_ref[...], b_ref[...], mode)
        re_ref[...] = r[:, :CH_N]
        im_ref[...] = r[:, CH_N:]

    return pl.pallas_call(
        body, name=name, grid=(s // tm, SSM_CHUNKS),
        in_specs=[pl.BlockSpec((tm, CH_W), lambda i, j: (i, j)),
                  pl.BlockSpec((None,) + bmat.shape[1:], lambda i, j: (j, 0, 0))],
        out_specs=[pl.BlockSpec((tm, CH_N), lambda i, j: (i, j))] * 2,
        out_shape=[jax.ShapeDtypeStruct((s, N_STATE), F32)] * 2,
        compiler_params=_cparams(2))(a, bmat)


def _ssm_contract(name, a_re, a_im, bmat, mode, d_row, extra, tm=512):
    s = a_re.shape[0]
    tm = min(tm, s)

    def body(re_ref, im_ref, b_ref, d_ref, e_ref, o_ref):
        b = b_ref[...]
        if mode == "nn":
            r = _dot(re_ref[...], b[:CH_N], "nn") + _dot(im_ref[...], b[CH_N:], "nn")
        else:
            r = _dot(re_ref[...], b[:, :CH_N], "nt") + _dot(im_ref[...], b[:, CH_N:], "nt")
        o_ref[...] = r + d_ref[...] * e_ref[...]

    return pl.pallas_call(
        body, name=name, grid=(s // tm, SSM_CHUNKS),
        in_specs=[pl.BlockSpec((tm, CH_N), lambda i, j: (i, j)), pl.BlockSpec((tm, CH_N), lambda i, j: (i, j)),
                  pl.BlockSpec((None,) + bmat.shape[1:], lambda i, j: (j, 0, 0)),
                  pl.BlockSpec((1, CH_W), lambda i, j: (0, j)), pl.BlockSpec((tm, CH_W), lambda i, j: (i, j))],
        out_specs=pl.BlockSpec((tm, CH_W), lambda i, j: (i, j)),
        out_shape=jax.ShapeDtypeStruct((s, SSM_WIDTH), F32),
        compiler_params=_cparams(2))(a_re, a_im, bmat, d_row, extra)


def _ssm_wgrad(name, chan, st_re, st_im, expand, tk=512):
    s = chan.shape[0]
    tk = min(tk, s)
    nk = s // tk
    oshape = (CH_W, 2 * CH_N) if expand else (2 * CH_N, CH_W)

    def body(c_ref, re_ref, im_ref, o_ref):
        c = c_ref[...]
        if expand:
            part = jnp.concatenate([_dot(c, re_ref[...], "tn"), _dot(c, im_ref[...], "tn")], axis=1)
        else:
            part = jnp.concatenate([_dot(re_ref[...], c, "tn"), _dot(im_ref[...], c, "tn")], axis=0)
        kk = pl.program_id(1)

        @pl.when(kk == 0)
        def _():
            o_ref[...] = part

        @pl.when(kk > 0)
        def _():
            o_ref[...] += part

    return pl.pallas_call(
        body, name=name, grid=(SSM_CHUNKS, nk),
        in_specs=[pl.BlockSpec((tk, CH_W), lambda j, kk: (kk, j)), pl.BlockSpec((tk, CH_N), lambda j, kk: (kk, j)),
                  pl.BlockSpec((tk, CH_N), lambda j, kk: (kk, j))],
        out_specs=pl.BlockSpec((None,) + oshape, lambda j, kk: (j, 0, 0)),
        out_shape=jax.ShapeDtypeStruct((SSM_CHUNKS,) + oshape, F32),
        compiler_params=_cparams(2))(chan, st_re, st_im)


SCAN_LB = 256


def _ssm_scan(name, w_re, w_im, a_re, a_im, reverse):
    s = w_re.shape[0]
    seg_len = s // N_SEG
    n_sq = int(math.log2(seg_len))
    assert 2 ** n_sq == seg_len

    def body(are_ref, aim_ref, wre_ref, wim_ref, hre_ref, him_ref, ere, eim, cre, cim):
        ar1 = are_ref[...]
        ai1 = -aim_ref[...] if reverse else aim_ref[...]
        ar = jnp.broadcast_to(ar1, (N_SEG, SCAN_LB))
        ai = jnp.broadcast_to(ai1, (N_SEG, SCAN_LB))

        def rows_of(k):
            kk = seg_len - 1 - k if reverse else k
            return pl.ds(pl.multiple_of(kk * N_SEG, N_SEG), N_SEG)

        def local(k, carry):
            hr, hi = carry
            rows = rows_of(k)
            nr = ar * hr - ai * hi + wre_ref[rows, :]
            ni = ar * hi + ai * hr + wim_ref[rows, :]
            hre_ref[rows, :] = nr
            him_ref[rows, :] = ni
            return nr, ni

        zero = jnp.zeros((N_SEG, SCAN_LB), F32)
        er, ei = lax.fori_loop(0, seg_len, local, (zero, zero))
        ere[...] = er
        eim[...] = ei
        pr, pi = ar1, ai1
        for _ in range(n_sq):
            pr, pi = pr * pr - pi * pi, 2.0 * pr * pi
        cr = jnp.zeros((1, SCAN_LB), F32)
        ci = jnp.zeros((1, SCAN_LB), F32)
        for jj in range(N_SEG):
            j = N_SEG - 1 - jj if reverse else jj
            cre[j:j + 1, :] = cr
            cim[j:j + 1, :] = ci
            er_j, ei_j = ere[j:j + 1, :], eim[j:j + 1, :]
            cr, ci = pr * cr - pi * ci + er_j, pr * ci + pi * cr + ei_j
        c_r, c_i = cre[...], cim[...]

        def fix(k, carry):
            qr, qi = carry
            rows = rows_of(k)
            hre_ref[rows, :] = hre_ref[rows, :] + (qr * c_r - qi * c_i)
            him_ref[rows, :] = him_ref[rows, :] + (qr * c_i + qi * c_r)
            return qr * ar - qi * ai, qr * ai + qi * ar

        lax.fori_loop(0, seg_len, fix, (ar, ai))

    nblk = N_STATE // SCAN_LB
    blk = pl.BlockSpec((s, SCAN_LB), lambda b: (0, b))
    row = pl.BlockSpec((1, SCAN_LB), lambda b: (0, b))
    return pl.pallas_call(
        body, name=name, grid=(nblk,), in_specs=[row, row, blk, blk], out_specs=[blk, blk],
        out_shape=[jax.ShapeDtypeStruct((s, N_STATE), F32)] * 2,
        scratch_shapes=[pltpu.VMEM((N_SEG, SCAN_LB), F32)] * 4,
        compiler_params=_cparams(1))(a_re, a_im, w_re, w_im)


def _ssm_da(g_re, g_im, h_re, h_im):
    s = g_re.shape[0]
    seg_len = s // N_SEG

    def body(gre_ref, gim_ref, hre_ref, him_ref, dre_ref, dim_ref):
        def rows_of(k):
            return pl.ds(pl.multiple_of(k * N_SEG, N_SEG), N_SEG)

        def step(k, carry):
            sr, si = carry
            gr, gi = gre_ref[rows_of(k), :], gim_ref[rows_of(k), :]
            pr, pi = hre_ref[rows_of(k - 1), :], him_ref[rows_of(k - 1), :]
            return sr + gr * pr + gi * pi, si + gi * pr - gr * pi

        zero = jnp.zeros((N_SEG, SCAN_LB), F32)
        sr, si = lax.fori_loop(1, seg_len, step, (zero, zero))
        last = pl.ds((seg_len - 1) * N_SEG, N_SEG)
        first_row = lax.broadcasted_iota(jnp.int32, (N_SEG, SCAN_LB), 0) == 0
        pr = jnp.where(first_row, 0.0, pltpu.roll(hre_ref[last, :], 1, 0))
        pi = jnp.where(first_row, 0.0, pltpu.roll(him_ref[last, :], 1, 0))
        gr, gi = gre_ref[pl.ds(0, N_SEG), :], gim_ref[pl.ds(0, N_SEG), :]
        sr = sr + gr * pr + gi * pi
        si = si + gi * pr - gr * pi
        dre_ref[...] = jnp.sum(sr, axis=0, keepdims=True)
        dim_ref[...] = jnp.sum(si, axis=0, keepdims=True)

    nblk = N_STATE // SCAN_LB
    blk = pl.BlockSpec((s, SCAN_LB), lambda b: (0, b))
    row = pl.BlockSpec((1, SCAN_LB), lambda b: (0, b))
    return pl.pallas_call(
        body, name="ssm_da", grid=(nblk,), in_specs=[blk] * 4, out_specs=[row, row],
        out_shape=[jax.ShapeDtypeStruct((1, N_STATE), F32)] * 2,
        compiler_params=_cparams(1))(g_re, g_im, h_re, h_im)


def _disc(ldt, are, aim, bre, bim):
    dt = jnp.exp(ldt)
    mag = jnp.exp(are * dt)
    abr = mag * jnp.cos(aim * dt)
    abi = mag * jnp.sin(aim * dt)
    den = jnp.square(are) + jnp.square(aim)
    nr = abr - 1.0
    fre = (nr * are + abi * aim) / den
    fim = (abi * are - nr * aim) / den
    return abr, abi, fre * bre - fim * bim, fre * bim + fim * bre


def _ssm_disc_fwd(ldt, are, aim, bre, bim):
    def body(l_ref, ar_ref, ai_ref, br_ref, bi_ref, o0, o1, o2, o3):
        res = _disc(l_ref[...], ar_ref[...], ai_ref[...], br_ref[...], bi_ref[...])
        for ref, val in zip((o0, o1, o2, o3), res):
            ref[...] = val

    col = jax.ShapeDtypeStruct((N_STATE, 1), F32)
    mat = jax.ShapeDtypeStruct((N_STATE, SSM_GROUP), F32)
    return pl.pallas_call(body, name="ssm_disc_fwd", out_shape=[col, col, mat, mat],
                          in_specs=[VMEM_SPEC] * 5, out_specs=[VMEM_SPEC] * 4)(ldt, are, aim, bre, bim)


def _ssm_disc_bwd(ldt, are, aim, bre, bim, d_abr, d_abi, d_bbr, d_bbi):
    def body(l_ref, ar_ref, ai_ref, br_ref, bi_ref, c0, c1, c2, c3, g_ldt, g_are, g_aim, g_bre, g_bim):
        _, vjp = jax.vjp(_disc, l_ref[...], ar_ref[...], ai_ref[...], br_ref[...], bi_ref[...])
        dl, dar, dai, dbr, dbi = vjp((c0[...], c1[...], c2[...], c3[...]))
        state = lax.broadcasted_iota(jnp.int32, (N_STATE, SSM_GROUPS), 0)
        group = lax.broadcasted_iota(jnp.int32, (N_STATE, SSM_GROUPS), 1)
        pick = jnp.right_shift(state, 6) == group
        g_ldt[...] = jnp.sum(jnp.where(pick, dl, 0.0), axis=0, keepdims=True)
        g_are[...] = dar
        g_aim[...] = dai
        g_bre[...] = dbr
        g_bim[...] = dbi

    col = jax.ShapeDtypeStruct((N_STATE, 1), F32)
    mat = jax.ShapeDtypeStruct((N_STATE, SSM_GROUP), F32)
    return pl.pallas_call(body, name="ssm_disc_bwd",
                          out_shape=[jax.ShapeDtypeStruct((1, SSM_GROUPS), F32), col, col, mat, mat],
                          in_specs=[VMEM_SPEC] * 9, out_specs=[VMEM_SPEC] * 5,
                          compiler_params=pltpu.CompilerParams(vmem_limit_bytes=VMEM_LIMIT))(
        ldt, are, aim, bre, bim, d_abr, d_abi, d_bbr, d_bbi)


_EYE8 = np.eye(8, dtype=np.float32)


def _blockdiag_b(bb):
    t = bb.reshape(SSM_CHUNKS, 8, SSM_STATE, SSM_GROUP).transpose(0, 1, 3, 2)
    return jnp.einsum("igcn,gh->igchn", t, _EYE8).reshape(SSM_CHUNKS, CH_W, CH_N)


def _diag_of_b(m):
    t = jnp.einsum("igchn,gh->igcn", m.reshape(SSM_CHUNKS, 8, SSM_GROUP, 8, SSM_STATE), _EYE8)
    return t.transpose(0, 1, 3, 2).reshape(N_STATE, SSM_GROUP)


def _blockdiag_c(c):
    t = c.reshape(SSM_CHUNKS, 8, SSM_GROUP, SSM_STATE).transpose(0, 1, 3, 2)
    return jnp.einsum("ignc,gh->ignhc", t, _EYE8).reshape(SSM_CHUNKS, CH_N, CH_W)


def _diag_of_c(m):
    t = jnp.einsum("ignhc,gh->ignc", m.reshape(SSM_CHUNKS, 8, SSM_STATE, 8, SSM_GROUP), _EYE8)
    return t.transpose(0, 1, 3, 2).reshape(SSM_GROUPS, SSM_GROUP, SSM_STATE)


def _time_perm(a):
    s, c = a.shape
    return a.reshape(N_SEG, s // N_SEG, c).transpose(1, 0, 2).reshape(s, c)


def _time_unperm(a):
    s, c = a.shape
    return a.reshape(s // N_SEG, N_SEG, c).transpose(1, 0, 2).reshape(s, c)


def _dilate(a, d):
    s, c = a.shape
    return a if d == 1 else a.reshape(s // d, d, c).transpose(1, 0, 2).reshape(s, c)


def _undilate(a, d):
    s, c = a.shape
    return a if d == 1 else a.reshape(d, s // d, c).transpose(1, 0, 2).reshape(s, c)


def _stack_dilated(parts):
    return jnp.stack([_dilate(p, d) for p, d in zip(parts, DILATIONS)], axis=0)


def _blocks_per_seq(g, n_blocks):
    return jnp.right_shift(jnp.int32(n_blocks), 2 * g)


ATT_T = 4
ATT_ROWS = ATT_T * ATT_BLK


def _window(edge_ref, cur_ref, i, sl, edge_first):
    lo = (i - 1) * ATT_BLK if edge_first else i * ATT_BLK
    if edge_first and i == 0:
        return jnp.concatenate([edge_ref[:, sl], cur_ref[0:ATT_BLK, sl]], axis=0)
    if not edge_first and i == ATT_T - 1:
        return jnp.concatenate([cur_ref[lo:lo + ATT_BLK, sl], edge_ref[:, sl]], axis=0)
    return cur_ref[lo:lo + 2 * ATT_BLK, sl]


def _band_valid(first_key):
    qi = lax.broadcasted_iota(jnp.int32, (ATT_BLK, 2 * ATT_BLK), 0)
    ki = lax.broadcasted_iota(jnp.int32, (ATT_BLK, 2 * ATT_BLK), 1)
    steps = qi + ATT_BLK - ki
    return (steps >= 0) & (steps <= ATT_BLK) & (ki >= first_key)


ATT_STATW = ATT_HPG * 128


def _stat(h):
    return slice(h * 128, (h + 1) * 128)


def _stat_rows(stat):
    n = stat.shape[0]
    heads = [stat[:, _stat(h)].T[0:1, :] for h in range(ATT_HPG)]
    return jnp.concatenate(heads + [jnp.zeros((8 - ATT_HPG, n), stat.dtype)], axis=0)


def _stack_dilated_rows(a):
    r, s = a.shape
    return jnp.stack([a if d == 1 else a.reshape(r, s // d, d).transpose(0, 2, 1).reshape(r, s) for d in DILATIONS])


def _attn_specs(nb, width=ATT_GROUPW):
    cur = pl.BlockSpec((None, ATT_ROWS, width), lambda g, b: (g, b, 0))
    prev = pl.BlockSpec((None, ATT_BLK, width), lambda g, b: (g, jnp.maximum(b * ATT_T - 1, 0), 0))
    nxt = pl.BlockSpec((None, ATT_BLK, width), lambda g, b: (g, jnp.minimum((b + 1) * ATT_T, nb - 1), 0))
    return cur, prev, nxt


def _attn_fwd(q, k, v):
    s = q.shape[1]
    nb = s // ATT_BLK

    def body(q_ref, kc_ref, kp_ref, vc_ref, vp_ref, o_ref, lse_ref):
        g, bt = pl.program_id(0), pl.program_id(1)
        per_seq = _blocks_per_seq(g, nb)
        for i in range(ATT_T):
            has_prev = lax.rem(bt * ATT_T + i, per_seq) > 0
            valid = _band_valid(jnp.where(has_prev, 0, ATT_BLK))
            rows = slice(i * ATT_BLK, (i + 1) * ATT_BLK)
            for h in range(ATT_HPG):
                sl = slice(h * ATT_HEAD_DIM, (h + 1) * ATT_HEAD_DIM)
                kcat = _window(kp_ref, kc_ref, i, sl, True)
                vcat = _window(vp_ref, vc_ref, i, sl, True)
                sc = _dot(q_ref[rows, sl], kcat, "nt") * ATT_SCALE
                sc = jnp.where(valid, sc, NEG_INF)
                m = jnp.max(sc, axis=-1, keepdims=True)
                p = jnp.exp(sc - m)
                den = jnp.sum(p, axis=-1, keepdims=True)
                o_ref[rows, sl] = _dot(p, vcat, "nn") / den
                lse_ref[rows, _stat(h)] = jnp.broadcast_to(m + jnp.log(den), (ATT_BLK, 128))

    cur, prev, _ = _attn_specs(nb)
    stat, _, _ = _attn_specs(nb, ATT_STATW)
    return pl.pallas_call(
        body, name="attn_fwd", grid=(3, nb // ATT_T), in_specs=[cur, cur, prev, cur, prev], out_specs=[cur, stat],
        out_shape=[jax.ShapeDtypeStruct((3, s, ATT_GROUPW), F32), jax.ShapeDtypeStruct((3, s, ATT_STATW), F32)],
        compiler_params=_cparams(2))(q, k, k, v, v)


def _attn_dq(q, k, v, do, lse, delta):
    s = q.shape[1]
    nb = s // ATT_BLK

    def body(q_ref, kc_ref, kp_ref, vc_ref, vp_ref, do_ref, lse_ref, dl_ref, dq_ref):
        g, bt = pl.program_id(0), pl.program_id(1)
        per_seq = _blocks_per_seq(g, nb)
        for i in range(ATT_T):
            has_prev = lax.rem(bt * ATT_T + i, per_seq) > 0
            valid = _band_valid(jnp.where(has_prev, 0, ATT_BLK))
            rows = slice(i * ATT_BLK, (i + 1) * ATT_BLK)
            for h in range(ATT_HPG):
                sl = slice(h * ATT_HEAD_DIM, (h + 1) * ATT_HEAD_DIM)
                kcat = _window(kp_ref, kc_ref, i, sl, True)
                vcat = _window(vp_ref, vc_ref, i, sl, True)
                lse = jnp.concatenate([lse_ref[rows, _stat(h)]] * 2, axis=1)
                dlt = jnp.concatenate([dl_ref[rows, _stat(h)]] * 2, axis=1)
                sc = _dot(q_ref[rows, sl], kcat, "nt") * ATT_SCALE
                p = jnp.exp(jnp.where(valid, sc, NEG_INF) - lse)
                dp = _dot(do_ref[rows, sl], vcat, "nt")
                ds = p * (dp - dlt) * ATT_SCALE
                dq_ref[rows, sl] = _dot(ds, kcat, "nn")

    cur, prev, _ = _attn_specs(nb)
    stat, _, _ = _attn_specs(nb, ATT_STATW)
    return pl.pallas_call(
        body, name="attn_dq", grid=(3, nb // ATT_T), in_specs=[cur, cur, prev, cur, prev, cur, stat, stat], out_specs=cur,
        out_shape=jax.ShapeDtypeStruct((3, s, ATT_GROUPW), F32),
        compiler_params=_cparams(2))(q, k, k, v, v, do, lse, delta)


def _attn_dkv(q, k, v, do, lse_t, delta_t):
    s = q.shape[1]
    nb = s // ATT_BLK

    def body(k_ref, v_ref, qc_ref, qn_ref, doc_ref, don_ref, lc_ref, ln_ref, dc_ref, dn_ref, dk_ref, dv_ref):
        g, bt = pl.program_id(0), pl.program_id(1)
        per_seq = _blocks_per_seq(g, nb)
        ki = lax.broadcasted_iota(jnp.int32, (ATT_BLK, 2 * ATT_BLK), 0)
        ci = lax.broadcasted_iota(jnp.int32, (ATT_BLK, 2 * ATT_BLK), 1)

        def pair(edge_ref, cur_ref, i, sl):
            if i == ATT_T - 1:
                return jnp.concatenate([cur_ref[i * ATT_BLK:(i + 1) * ATT_BLK, sl], edge_ref[:, sl]], axis=0)
            return cur_ref[i * ATT_BLK:(i + 2) * ATT_BLK, sl]

        def pair_row(edge_ref, cur_ref, i, h):
            if i == ATT_T - 1:
                row = jnp.concatenate([cur_ref[h:h + 1, i * ATT_BLK:(i + 1) * ATT_BLK], edge_ref[h:h + 1, :]], axis=1)
            else:
                row = cur_ref[h:h + 1, i * ATT_BLK:(i + 2) * ATT_BLK]
            return jnp.broadcast_to(row, (ATT_BLK, 2 * ATT_BLK))

        for i in range(ATT_T):
            b = bt * ATT_T + i
            next_uses = (b + 1 < nb) & (lax.rem(b + 1, per_seq) > 0)
            reach = jnp.where(next_uses, 0, 4 * ATT_BLK)
            valid = ((ci < ATT_BLK) & (ci >= ki)) | ((ci >= ATT_BLK) & (ki - ci + ATT_BLK >= reach))
            rows = slice(i * ATT_BLK, (i + 1) * ATT_BLK)
            for h in range(ATT_HPG):
                sl = slice(h * ATT_HEAD_DIM, (h + 1) * ATT_HEAD_DIM)
                qcat, docat = pair(qn_ref, qc_ref, i, sl), pair(don_ref, doc_ref, i, sl)
                sc = _dot(k_ref[rows, sl], qcat, "nt") * ATT_SCALE
                p = jnp.exp(jnp.where(valid, sc, NEG_INF) - pair_row(ln_ref, lc_ref, i, h))
                dv_ref[rows, sl] = _dot(p, docat, "nn")
                dp = _dot(v_ref[rows, sl], docat, "nt")
                ds = p * (dp - pair_row(dn_ref, dc_ref, i, h)) * ATT_SCALE
                dk_ref[rows, sl] = _dot(ds, qcat, "nn")

    cur, _, nxt = _attn_specs(nb)
    stat = pl.BlockSpec((None, 8, ATT_ROWS), lambda g, b: (g, 0, b))
    snxt = pl.BlockSpec((None, 8, ATT_BLK), lambda g, b: (g, 0, jnp.minimum((b + 1) * ATT_T, nb - 1)))
    return pl.pallas_call(
        body, name="attn_dkv", grid=(3, nb // ATT_T), in_specs=[cur, cur, cur, nxt, cur, nxt, stat, snxt, stat, snxt],
        out_specs=[cur, cur], out_shape=[jax.ShapeDtypeStruct((3, s, ATT_GROUPW), F32)] * 2,
        compiler_params=_cparams(2))(k, v, q, q, do, do, lse_t, lse_t, delta_t, delta_t)


def _xattn_probs(q, kh):
    sc = _dot(q, kh, "nt") * XATT_SCALE
    e = jnp.exp(sc - jnp.max(sc, axis=-1, keepdims=True))
    return e / jnp.sum(e, axis=-1, keepdims=True)


def _xattn_fwd(q, kv, tm=512):
    s = q.shape[0]
    tm = min(tm, s)

    def body(q_ref, kv_ref, o_ref):
        for h in range(XATT_HEADS):
            sl = slice(h * XATT_HEAD_DIM, (h + 1) * XATT_HEAD_DIM)
            vs = slice(D_MODEL + h * XATT_HEAD_DIM, D_MODEL + (h + 1) * XATT_HEAD_DIM)
            p = _xattn_probs(q_ref[:, sl], kv_ref[:, sl])
            o_ref[:, sl] = _dot(p, kv_ref[:, vs], "nn").astype(o_ref.dtype)

    return pl.pallas_call(
        body, name="xattn_fwd", grid=(s // tm,),
        in_specs=[pl.BlockSpec((tm, D_MODEL), lambda i: (i, 0)), pl.BlockSpec(kv.shape, lambda i: (0, 0))],
        out_specs=pl.BlockSpec((tm, D_MODEL), lambda i: (i, 0)),
        out_shape=jax.ShapeDtypeStruct((s, D_MODEL), MXU_DTYPE), compiler_params=_cparams(1))(q, kv)


def _xattn_bwd(q, kv, do, tm=512):
    s = q.shape[0]
    tm = min(tm, s)

    def body(q_ref, kv_ref, do_ref, dq_ref, dkv_ref):
        first = pl.program_id(0) == 0

        @pl.when(first)
        def _():
            dkv_ref[...] = jnp.zeros_like(dkv_ref)

        for h in range(XATT_HEADS):
            sl = slice(h * XATT_HEAD_DIM, (h + 1) * XATT_HEAD_DIM)
            vs = slice(D_MODEL + h * XATT_HEAD_DIM, D_MODEL + (h + 1) * XATT_HEAD_DIM)
            p = _xattn_probs(q_ref[:, sl], kv_ref[:, sl])
            dkv_ref[:, vs] += _dot(p, do_ref[:, sl], "tn")
            dp = _dot(do_ref[:, sl], kv_ref[:, vs], "nt")
            ds = p * (dp - jnp.sum(dp * p, axis=-1, keepdims=True)) * XATT_SCALE
            dq_ref[:, sl] = _dot(ds, kv_ref[:, sl], "nn").astype(dq_ref.dtype)
            dkv_ref[:, sl] += _dot(ds, q_ref[:, sl], "tn")

    row = pl.BlockSpec((tm, D_MODEL), lambda i: (i, 0))
    whole = pl.BlockSpec(kv.shape, lambda i: (0, 0))
    return pl.pallas_call(
        body, name="xattn_bwd", grid=(s // tm,), in_specs=[row, whole, row], out_specs=[row, whole],
        out_shape=[jax.ShapeDtypeStruct((s, D_MODEL), MXU_DTYPE), jax.ShapeDtypeStruct(kv.shape, F32)],
        compiler_params=_cparams(1))(q, kv, do)


def _ln(x, g, b):
    mu = jnp.mean(x, axis=-1, keepdims=True)
    xc = x - mu
    var = jnp.mean(jnp.square(xc), axis=-1, keepdims=True)
    return xc * lax.rsqrt(var + LN_EPS) * g + b


def _res_ln(h, o, g, b):
    return _ln(DEEPNORM_ALPHA * h + o, g, b)


def _gate(gs, ga, z1, z2, batt):
    return jax.nn.sigmoid(gs) * (z1 * jax.nn.sigmoid(z2)) + jax.nn.sigmoid(ga) * batt


def _rope_tables(pos, invf, m1, m2):
    ang = pos.astype(F32) * invf
    sin = jnp.sin(ang)
    return jnp.cos(ang), -sin * m1, sin * m2


def _rope(t, cos, s_up, s_dn):
    w = t.shape[-1]
    return t * cos + pltpu.roll(t, w - ROT_DIM // 2, 1) * s_up + pltpu.roll(t, ROT_DIM // 2, 1) * s_dn


def _rope_t(dt, cos, s_up, s_dn):
    w = dt.shape[-1]
    return dt * cos + pltpu.roll(dt * s_up, ROT_DIM // 2, 1) + pltpu.roll(dt * s_dn, w - ROT_DIM // 2, 1)


def _rope_consts():
    inv_freq = ROPE_THETA ** (-jnp.arange(0, ROT_DIM, 2, dtype=F32) / ROT_DIM)
    d = np.arange(ATT_GROUPW) % ATT_HEAD_DIM
    invf = jnp.where(d < ROT_DIM, inv_freq[d % (ROT_DIM // 2)], 0.0).reshape(1, ATT_GROUPW).astype(F32)
    m1 = jnp.asarray((d < ROT_DIM // 2).astype(np.float32)).reshape(1, ATT_GROUPW)
    m2 = jnp.asarray(((d >= ROT_DIM // 2) & (d < ROT_DIM)).astype(np.float32)).reshape(1, ATT_GROUPW)
    return invf, m1, m2


def _head_sum_matrix():
    d = np.arange(ATT_GROUPW) // ATT_HEAD_DIM
    s = np.arange(ATT_STATW) // 128
    return jnp.asarray((d[:, None] == s[None, :]).astype(np.float32))


def _adamw(w, g, m, v):
    m = ADAM_B1 * m + (1.0 - ADAM_B1) * g
    v = ADAM_B2 * v + (1.0 - ADAM_B2) * jnp.square(g)
    m_hat = m / (1.0 - ADAM_B1 ** ADAM_STEP)
    v_hat = v / (1.0 - ADAM_B2 ** ADAM_STEP)
    delta = -ADAM_LR * (m_hat / (jnp.sqrt(v_hat) + ADAM_EPS) + ADAM_WD * w)
    return delta, m, v


def _local_step(x, mem, pos, target, sp, wb):
    s = x.shape[0]
    al = DEEPNORM_ALPHA
    mx = MXU_DTYPE

    h0, h0b = _rowwise("ln_in", lambda x, g, b: (lambda h: (h, h))(_ln(x, g, b)), [x],
                       [sp["ln_in_g"], sp["ln_in_b"]], [(D_MODEL, F32), (D_MODEL, mx)])
    proj = _mm("proj", h0b, wb["w_in"], "nn", bias=sp["b_in"])

    ldt = jnp.repeat(sp["ssm_log_dt"].reshape(SSM_GROUPS), SSM_STATE).reshape(N_STATE, 1)
    are, aim = sp["ssm_a_re"].reshape(N_STATE, 1), sp["ssm_a_im"].reshape(N_STATE, 1)
    bre, bim = sp["ssm_b_re"].reshape(N_STATE, SSM_GROUP), sp["ssm_b_im"].reshape(N_STATE, SSM_GROUP)
    abr, abi, bbr, bbi = _ssm_disc_fwd(ldt, are, aim, bre, bim)
    a_re, a_im = abr.reshape(1, N_STATE), abi.reshape(1, N_STATE)
    bexp = jnp.concatenate([_blockdiag_b(bbr), _blockdiag_b(bbi)], axis=2).astype(mx)
    cexp = jnp.concatenate([_blockdiag_c(sp["ssm_c_re"].reshape(SSM_GROUPS, SSM_GROUP, SSM_STATE)),
                            -_blockdiag_c(sp["ssm_c_im"].reshape(SSM_GROUPS, SSM_GROUP, SSM_STATE))],
                           axis=1).astype(mx)
    u_p = _time_perm(proj[:, :SSM_WIDTH])
    w_re, w_im = _ssm_expand("ssm_bu", u_p, bexp, "nn")
    h_re, h_im = _ssm_scan("ssm_scan_fwd", w_re, w_im, a_re, a_im, reverse=False)
    y_p = _ssm_contract("ssm_ch", h_re, h_im, cexp, "nn", sp["ssm_d"], u_p)
    y = _time_unperm(y_p)
    ygb, = _rowwise("gelu", lambda y: jax.nn.gelu(y), [y], [], [(SSM_WIDTH, mx)])
    z = _mm("glu", ygb, wb["w_glu"], "nn", bias=sp["b_glu"])

    invf, m1, m2 = _rope_consts()

    def rope_fwd(pos, q0, q1, q2, k0, k1, k2, v0, v1, v2, invf, m1, m2):
        tabs = _rope_tables(pos, invf, m1, m2)
        return tuple(_rope(t, *tabs) for t in (q0, q1, q2, k0, k1, k2)) + (v0, v1, v2)

    qkv_cols = [(proj, ATT_GROUPW, 3 + i) for i in range(9)]
    qkv = _rowwise("rope", rope_fwd, [pos] + qkv_cols, [invf, m1, m2], [(ATT_GROUPW, mx)] * 9)
    q_d, k_d, v_d = _stack_dilated(qkv[0:3]), _stack_dilated(qkv[3:6]), _stack_dilated(qkv[6:9])
    o_d, lse_d = _attn_fwd(q_d, k_d, v_d)
    o_g = [_undilate(o_d[i], d) for i, d in enumerate(DILATIONS)]
    l_g = [_undilate(lse_d[i], d) for i, d in enumerate(DILATIONS)]

    def merge(o0, o1, o2, l0, l1, l2):
        m = jnp.maximum(jnp.maximum(l0, l1), l2)
        e0, e1, e2 = jnp.exp(l0 - m), jnp.exp(l1 - m), jnp.exp(l2 - m)
        tot = e0 + e1 + e2

        def per_dim(e):
            w = e / tot
            return jnp.concatenate([w[:, h * 128:h * 128 + ATT_HEAD_DIM] for h in range(ATT_HPG)], axis=1)

        att = per_dim(e0) * o0 + per_dim(e1) * o1 + per_dim(e2) * o2
        lse = m + jnp.log(tot)
        return att, att, lse, _stat_rows(lse)

    att, attb, lse_tot, lse_tot_t = _rowwise("attn_merge", merge, o_g + l_g, [],
                                             [(ATT_GROUPW, F32), (ATT_GROUPW, mx), (ATT_STATW, F32)], touts=[(8, F32)])
    batt = _mm("att_up", attb, wb["w_att_up"], "nn")

    gate_rows = [(proj, D_MODEL, 3), (proj, D_MODEL, 4), (z, D_MODEL, 0), (z, D_MODEL, 1), batt]
    mixedb, = _rowwise("gate", _gate, gate_rows, [], [(D_MODEL, mx)])
    o1 = _mm("mix_out", mixedb, wb["w_mix_out"], "nn", bias=sp["b_mix_out"])
    h1, h1b = _rowwise("ln1", lambda h, o, g, b: (lambda r: (r, r))(_res_ln(h, o, g, b)), [h0, o1],
                       [sp["ln1_g"], sp["ln1_b"]], [(D_MODEL, F32), (D_MODEL, mx)])

    qx = _mm("xq", h1b, wb["w_xq"], "nn", out_dtypes=(mx,))
    kvx = _mm("xkv", mem, wb["w_xkv"], "nn", out_dtypes=(mx,))
    oxb = _xattn_fwd(qx, kvx)
    o2 = _mm("xo", oxb, wb["w_xo"], "nn")
    h2, h2b = _rowwise("ln2", lambda h, o, g, b: (lambda r: (r, r))(_res_ln(h, o, g, b)), [h1, o2],
                       [sp["ln2_g"], sp["ln2_b"]], [(D_MODEL, F32), (D_MODEL, mx)])

    a_ff, fb = _mm("ff1", h2b, wb["w_ff1"], "nn", bias=sp["b_ff1"],
                   epilogue=lambda r: (r, jnp.square(jnp.maximum(r, 0.0))), out_dtypes=(F32, mx))
    o3 = _mm("ff2", fb, wb["w_ff2"], "nn", bias=sp["b_ff2"])

    def loss_bwd(h2, o3, tgt, g, b):
        def f(h2, o3, g, b):
            h3 = _res_ln(h2, o3, g, b)
            return 0.5 * jnp.sum(jnp.mean(jnp.square(h3 - tgt), axis=-1))

        loss, vjp = jax.vjp(f, h2, o3, g, b)
        _, dr, dg, db = vjp(jnp.ones((), F32))
        return dr, dr, dg, db, _colsum(dr), jnp.full((1, 128), loss, F32)

    dr3, dr3b, g_ln3_g, g_ln3_b, g_b_ff2, loss = _rowwise(
        "loss_ln3_bwd", loss_bwd, [h2, o3, target], [sp["ln3_g"], sp["ln3_b"]],
        [(D_MODEL, F32), (D_MODEL, mx)], [D_MODEL, D_MODEL, D_MODEL, 128])

    dab = _mm("ff2_dx", dr3b, wb["w_ff2"], "nt", extras=(a_ff,),
              epilogue=lambda r, a: (r * (2.0 * jnp.maximum(a, 0.0)),), out_dtypes=(mx,))
    g_w_ff2 = _mm("ff2_dw", fb, dr3b, "tn")
    g_b_ff1, = _rowwise("ff1_db", lambda v: (_colsum(v),), [dab], [], [], [D_FF])
    g_w_ff1 = _mm("ff1_dw", h2b, dab, "tn")
    dh2 = _mm("ff1_dx", dab, wb["w_ff1"], "nt", extras=(dr3,), epilogue=lambda r, d: (r + al * d,))

    def ln_bwd(h, o, dout, g, b):
        _, vjp = jax.vjp(_res_ln, h, o, g, b)
        _, dr, dg, db = vjp(dout)
        return dr, dr, dg, db, _colsum(dr)

    dr2, dr2b, g_ln2_g, g_ln2_b, _ = _rowwise(
        "ln2_bwd", ln_bwd, [h1, o2, dh2], [sp["ln2_g"], sp["ln2_b"]],
        [(D_MODEL, F32), (D_MODEL, mx)], [D_MODEL, D_MODEL, D_MODEL])
    g_w_xo = _mm("xo_dw", oxb, dr2b, "tn")
    doxb = _mm("xo_dx", dr2b, wb["w_xo"], "nt", out_dtypes=(mx,))
    dqxb, dkvx = _xattn_bwd(qx, kvx, doxb)
    g_w_xq = _mm("xq_dw", h1b, dqxb, "tn")
    dh1 = _mm("xq_dx", dqxb, wb["w_xq"], "nt", extras=(dr2,), epilogue=lambda r, d: (r + al * d,))
    g_w_xkv = _mm("xkv_dw", mem, dkvx, "tn")

    dr1, dr1b, g_ln1_g, g_ln1_b, g_b_mix = _rowwise(
        "ln1_bwd", ln_bwd, [h0, o1, dh1], [sp["ln1_g"], sp["ln1_b"]],
        [(D_MODEL, F32), (D_MODEL, mx)], [D_MODEL, D_MODEL, D_MODEL])
    g_w_mix = _mm("mix_dw", mixedb, dr1b, "tn")
    dmixed = _mm("mix_dx", dr1b, wb["w_mix_out"], "nt")

    def gate_bwd(gs, ga, z1, z2, batt, dm):
        _, vjp = jax.vjp(_gate, gs, ga, z1, z2, batt)
        dgs, dga, dz1, dz2, dbatt = vjp(dm)
        dz = jnp.concatenate([dz1, dz2], axis=-1)
        return dgs, dga, dz, dbatt, _colsum(dz)

    dgsb, dgab, dzb, dbattb, g_b_glu = _rowwise(
        "gate_bwd", gate_bwd, gate_rows + [dmixed], [],
        [(D_MODEL, mx), (D_MODEL, mx), (2 * D_MODEL, mx), (D_MODEL, mx)], [2 * D_MODEL])
    g_w_up = _mm("att_up_dw", attb, dbattb, "tn")
    datt = _mm("att_up_dx", dbattb, wb["w_att_up"], "nt")

    def att_delta(datt, att, hs):
        dl = jnp.dot(datt * att, hs, precision=lax.Precision.HIGHEST, preferred_element_type=F32)
        return datt, dl, _stat_rows(dl)

    dattb, delta, delta_t = _rowwise("attn_delta", att_delta, [datt, att], [_head_sum_matrix()],
                                     [(ATT_GROUPW, mx), (ATT_STATW, F32)], touts=[(8, F32)])
    do_d = _stack_dilated([dattb] * 3)
    lt_d = _stack_dilated([lse_tot] * 3)
    dl_d = _stack_dilated([delta] * 3)
    dq_d = _attn_dq(q_d, k_d, v_d, do_d, lt_d, dl_d)
    dk_d, dv_d = _attn_dkv(q_d, k_d, v_d, do_d, _stack_dilated_rows(lse_tot_t), _stack_dilated_rows(delta_t))
    dqkv = [_undilate(t[i], d) for t in (dq_d, dk_d, dv_d) for i, d in enumerate(DILATIONS)]

    def rope_bwd(pos, q0, q1, q2, k0, k1, k2, v0, v1, v2, invf, m1, m2):
        tabs = _rope_tables(pos, invf, m1, m2)
        return jnp.concatenate([_rope_t(t, *tabs) for t in (q0, q1, q2, k0, k1, k2)] + [v0, v1, v2], axis=-1)

    dqkvb, = _rowwise("rope_bwd", rope_bwd, [pos] + dqkv, [invf, m1, m2], [(9 * ATT_GROUPW, mx)])

    g_w_glu = _mm("glu_dw", ygb, dzb, "tn")
    dyg = _mm("glu_dx", dzb, wb["w_glu"], "nt")

    def gelu_bwd(y, dyg):
        _, vjp = jax.vjp(jax.nn.gelu, y)
        return vjp(dyg)[0]

    dy, = _rowwise("gelu_bwd", gelu_bwd, [y, dyg], [], [(SSM_WIDTH, F32)])
    dy_p = _time_perm(dy)
    dh_re, dh_im = _ssm_expand("ssm_dh", dy_p, cexp, "nt")
    g_cexp = _ssm_wgrad("ssm_dc", dy_p, h_re, h_im, expand=False)
    s_re, s_im = _ssm_scan("ssm_scan_bwd", dh_re, dh_im, a_re, a_im, reverse=True)
    d_abr, d_abi = _ssm_da(s_re, s_im, h_re, h_im)
    g_bexp = _ssm_wgrad("ssm_db", u_p, s_re, s_im, expand=True)
    du_p = _ssm_contract("ssm_du", s_re, s_im, bexp, "nt", sp["ssm_d"], dy_p)
    g_ssm_d, = _rowwise("ssm_dd", lambda a, b: (_colsum(a * b),), [dy_p, u_p], [], [], [SSM_WIDTH])
    g_ldt, g_are, g_aim, g_bre, g_bim = _ssm_disc_bwd(
        ldt, are, aim, bre, bim, d_abr.reshape(N_STATE, 1), d_abi.reshape(N_STATE, 1),
        _diag_of_b(g_bexp[:, :, :CH_N]), _diag_of_b(g_bexp[:, :, CH_N:]))
    g_c_re = _diag_of_c(g_cexp[:, :CH_N, :])
    g_c_im = -_diag_of_c(g_cexp[:, CH_N:, :])
    dub = _time_unperm(du_p).astype(mx)

    dprojb = jnp.concatenate([dub, dqkvb, dgsb, dgab], axis=-1)
    g_b_in, = _rowwise("in_db", lambda v: (_colsum(v),), [dprojb], [], [], [IN_COLS])
    g_w_in = _mm("in_dw", h0b, dprojb, "tn")
    dh0 = _mm("in_dx", dprojb, wb["w_in"], "nt", extras=(dr1,), epilogue=lambda r, d: (r + al * d,))

    def ln_in_bwd(x, dout, g, b):
        _, vjp = jax.vjp(_ln, x, g, b)
        return vjp(dout)

    dx, g_ln_in_g, g_ln_in_b = _rowwise("ln_in_bwd", ln_in_bwd, [x, dh0], [sp["ln_in_g"], sp["ln_in_b"]],
                                        [(D_MODEL, F32)], [D_MODEL, D_MODEL])

    big = {"w_in": g_w_in, "w_glu": g_w_glu, "w_att_up": g_w_up, "w_mix_out": g_w_mix, "w_xq": g_w_xq,
           "w_xkv": g_w_xkv, "w_xo": g_w_xo, "w_ff1": g_w_ff1, "w_ff2": g_w_ff2}
    small = {"ln_in_g": g_ln_in_g, "ln_in_b": g_ln_in_b, "b_in": g_b_in, "ssm_log_dt": g_ldt, "ssm_a_re": g_are,
             "ssm_a_im": g_aim, "ssm_b_re": g_bre, "ssm_b_im": g_bim, "ssm_c_re": g_c_re, "ssm_c_im": g_c_im,
             "ssm_d": g_ssm_d, "b_glu": g_b_glu, "b_mix_out": g_b_mix, "ln1_g": g_ln1_g, "ln1_b": g_ln1_b,
             "ln2_g": g_ln2_g, "ln2_b": g_ln2_b, "b_ff1": g_b_ff1, "b_ff2": g_b_ff2, "ln3_g": g_ln3_g,
             "ln3_b": g_ln3_b}
    return loss, dx, big, small


def _piece_shape(k, n, axis):
    return (k // 2, n // 4) if axis == 1 else (k // 8, n)


def _aligned(v, m):
    return v if isinstance(v, int) else pl.multiple_of(v, m)


def _full_piece(ref, k, n, axis, chip, half):
    pr, pc = _piece_shape(k, n, axis)
    if axis == 1:
        return ref.at[pl.ds(_aligned(half * pr, 8), pr), pl.ds(_aligned(chip * pc, 128), pc)]
    return ref.at[pl.ds(_aligned(chip * (2 * pr) + half * pr, 8), pr), :]


def _full_shard(ref, k, n, axis, chip):
    if axis == 1:
        return ref.at[:, pl.ds(_aligned(chip * (n // 4), 128), n // 4)]
    return ref.at[pl.ds(_aligned(chip * (k // 4), 8), k // 4), :]


def _shard_piece(ref, k, n, axis, half):
    pr, _ = _piece_shape(k, n, axis)
    return ref.at[pl.ds(_aligned(half * pr, 8), pr), :]


def _mesh_pos():
    x, y, c = lax.axis_index("x"), lax.axis_index("y"), lax.axis_index("c")
    other_chips = [(1 - x, y), (x, 1 - y), (1 - x, 1 - y)]
    return x, y, c, other_chips


def _remote(src, dst, send_sem, recv_sem, dev):
    return pltpu.make_async_remote_copy(src_ref=src, dst_ref=dst, send_sem=send_sem, recv_sem=recv_sem,
                                        device_id=dev, device_id_type=MESH)


def _placed(name, fn, n_steps, where, ins, out_sds, out_block, out_index):
    def body(w_ref, *refs):
        o_ref = refs[-1]
        o_ref[...] = fn(*[r[...] for r in refs[:-1]]).astype(o_ref.dtype)

    grid_spec = pltpu.PrefetchScalarGridSpec(
        num_scalar_prefetch=1, grid=(n_steps,), in_specs=[pl.BlockSpec(bs, idx) for _, bs, idx in ins],
        out_specs=pl.BlockSpec(out_block, out_index))
    return pl.pallas_call(body, name=name, grid_spec=grid_spec, out_shape=out_sds,
                          compiler_params=_cparams(1))(where, *[a for a, _, _ in ins])


def _gather_weights(fulls):
    nw = len(BIG)

    def body(*refs):
        full = refs[nw:2 * nw]
        send_sems, recv_sems = refs[2 * nw:]
        x, y, c, chips = _mesh_pos()
        me = 2 * x + y
        sib = (x, y, 1 - c)
        first, fwd = [], []
        for wi, (_, k, n, ax) in enumerate(BIG):
            mine = _full_piece(full[wi], k, n, ax, me, c)
            for j, (qx, qy) in enumerate(chips):
                cp = _remote(mine, mine, send_sems.at[wi * 6 + j], recv_sems.at[wi * 6 + j], (qx, qy, c))
                cp.start()
                first.append(cp)
        for wi, (_, k, n, ax) in enumerate(BIG):
            for j, (qx, qy) in enumerate(chips):
                piece = _full_piece(full[wi], k, n, ax, 2 * qx + qy, c)
                _remote(piece, piece, send_sems.at[wi * 6 + j], recv_sems.at[wi * 6 + j], (qx, qy, c)).wait_recv()
                cp = _remote(piece, piece, send_sems.at[wi * 6 + 3 + j], recv_sems.at[wi * 6 + 3 + j], sib)
                cp.start()
                fwd.append(cp)
        for wi, (_, k, n, ax) in enumerate(BIG):
            for j, (qx, qy) in enumerate(chips):
                piece = _full_piece(full[wi], k, n, ax, 2 * qx + qy, 1 - c)
                _remote(piece, piece, send_sems.at[wi * 6 + 3 + j], recv_sems.at[wi * 6 + 3 + j], sib).wait_recv()
        for cp in first + fwd:
            cp.wait_send()

    return pl.pallas_call(
        body, name="gather_weights", in_specs=[HBM_SPEC] * nw, out_specs=[HBM_SPEC] * nw,
        out_shape=[jax.ShapeDtypeStruct((k, n), MXU_DTYPE) for _, k, n, _ in BIG],
        input_output_aliases={i: i for i in range(nw)},
        scratch_shapes=[pltpu.SemaphoreType.DMA((6 * nw,)), pltpu.SemaphoreType.DMA((6 * nw,))])(*fulls)


def _reduce_swap_halves(grads):
    nw = len(BIG)

    def body(*refs):
        g, got = refs[:nw], refs[nw:2 * nw]
        send_sems, recv_sems = refs[2 * nw:]
        x, y, c, _ = _mesh_pos()
        sib = (x, y, 1 - c)
        cps = []
        for wi, (_, k, n, ax) in enumerate(BIG):
            for q in range(4):
                cp = _remote(_full_piece(g[wi], k, n, ax, q, 1 - c), got[wi].at[q],
                             send_sems.at[wi * 4 + q], recv_sems.at[wi * 4 + q], sib)
                cp.start()
                cps.append(cp)
        for cp in cps:
            cp.wait()

    return pl.pallas_call(
        body, name="reduce_swap_halves", in_specs=[HBM_SPEC] * nw, out_specs=[HBM_SPEC] * nw,
        out_shape=[jax.ShapeDtypeStruct((4,) + _piece_shape(k, n, ax), F32) for _, k, n, ax in BIG],
        scratch_shapes=[pltpu.SemaphoreType.DMA((4 * nw,)), pltpu.SemaphoreType.DMA((4 * nw,))])(*grads)


def _reduce_to_owner(parts):
    nw = len(BIG)

    def body(*refs):
        p, out = refs[:nw], refs[nw:2 * nw]
        send_sems, recv_sems = refs[2 * nw:]
        x, y, c, chips = _mesh_pos()
        cps = []
        for wi in range(nw):
            for j, (qx, qy) in enumerate(chips):
                cp = _remote(p[wi].at[2 * qx + qy], out[wi].at[j], send_sems.at[wi * 3 + j],
                             recv_sems.at[wi * 3 + j], (qx, qy, c))
                cp.start()
                cps.append(cp)
        for cp in cps:
            cp.wait()

    return pl.pallas_call(
        body, name="reduce_to_owner", in_specs=[HBM_SPEC] * nw, out_specs=[HBM_SPEC] * nw,
        out_shape=[jax.ShapeDtypeStruct((3,) + p.shape[1:], p.dtype) for p in parts],
        scratch_shapes=[pltpu.SemaphoreType.DMA((3 * nw,)), pltpu.SemaphoreType.DMA((3 * nw,))])(*parts)


def _share_with_sibling(shards):
    nw = len(BIG)

    def body(*refs):
        out = refs[nw:2 * nw]
        send_sems, recv_sems = refs[2 * nw:]
        x, y, c, _ = _mesh_pos()
        sib = (x, y, 1 - c)
        cps = []
        for wi, (_, k, n, ax) in enumerate(BIG):
            mine = _shard_piece(out[wi], k, n, ax, c)
            cp = _remote(mine, mine, send_sems.at[wi], recv_sems.at[wi], sib)
            cp.start()
            cps.append(cp)
        for wi, (_, k, n, ax) in enumerate(BIG):
            piece = _shard_piece(out[wi], k, n, ax, 1 - c)
            _remote(piece, piece, send_sems.at[wi], recv_sems.at[wi], sib).wait_recv()
        for cp in cps:
            cp.wait_send()

    return pl.pallas_call(
        body, name="share_with_sibling", in_specs=[HBM_SPEC] * nw, out_specs=[HBM_SPEC] * nw,
        out_shape=[jax.ShapeDtypeStruct(sh.shape, sh.dtype) for sh in shards],
        input_output_aliases={i: i for i in range(nw)},
        scratch_shapes=[pltpu.SemaphoreType.DMA((nw,)), pltpu.SemaphoreType.DMA((nw,))])(*shards)


def _allreduce_small(v):
    r = v.shape[0]
    rh = r // 2
    assert rh % 8 == 0

    def body(v_ref, o_ref, sib_buf, chip_buf, send_sems, recv_sems):
        x, y, c, chips = _mesh_pos()
        me = 2 * x + y
        sib = (x, y, 1 - c)
        mine = pl.ds(pl.multiple_of(c * rh, 8), rh)
        other = pl.ds(pl.multiple_of((1 - c) * rh, 8), rh)
        swap = _remote(v_ref.at[other], sib_buf, send_sems.at[0], recv_sems.at[0], sib)
        swap.start()
        swap.wait()
        chip_buf[me] = v_ref[mine, :] + sib_buf[...]
        cps = []
        for j, (qx, qy) in enumerate(chips):
            cp = _remote(chip_buf.at[me], chip_buf.at[me], send_sems.at[1 + j], recv_sems.at[1 + j], (qx, qy, c))
            cp.start()
            cps.append(cp)
        for j, (qx, qy) in enumerate(chips):
            slot = chip_buf.at[2 * qx + qy]
            _remote(slot, slot, send_sems.at[1 + j], recv_sems.at[1 + j], (qx, qy, c)).wait_recv()
        for cp in cps:
            cp.wait_send()
        o_ref[mine, :] = ((chip_buf[0] + chip_buf[1]) + chip_buf[2]) + chip_buf[3]
        back = _remote(o_ref.at[mine], o_ref.at[mine], send_sems.at[4], recv_sems.at[4], sib)
        back.start()
        _remote(o_ref.at[other], o_ref.at[other], send_sems.at[4], recv_sems.at[4], sib).wait_recv()
        back.wait_send()

    return pl.pallas_call(
        body, name="allreduce_small", in_specs=[VMEM_SPEC], out_specs=VMEM_SPEC,
        out_shape=jax.ShapeDtypeStruct((r, 128), F32),
        scratch_shapes=[pltpu.VMEM((rh, 128), F32), pltpu.VMEM((4, rh, 128), F32),
                        pltpu.SemaphoreType.DMA((5,)), pltpu.SemaphoreType.DMA((5,))],
        compiler_params=pltpu.CompilerParams(vmem_limit_bytes=VMEM_LIMIT))(v)


def _as2d(a):
    a = a.reshape((-1, a.shape[-1])) if a.ndim > 1 else a.reshape(1, -1)
    return a


def _adamw_small(quads):
    n = len(quads)

    def body(*refs):
        for i in range(n):
            w, g, m, v = (r[...] for r in refs[4 * i:4 * i + 4])
            for ref, val in zip(refs[4 * n + 3 * i:4 * n + 3 * i + 3], _adamw(w, g, m, v)):
                ref[...] = val

    return pl.pallas_call(
        body, name="adamw_small", in_specs=[VMEM_SPEC] * (4 * n), out_specs=[VMEM_SPEC] * (3 * n),
        out_shape=[jax.ShapeDtypeStruct(q[0].shape, F32) for q in quads for _ in range(3)],
        compiler_params=pltpu.CompilerParams(vmem_limit_bytes=VMEM_LIMIT))(*[a for q in quads for a in q])


def _where():
    return jnp.stack([2 * lax.axis_index("x") + lax.axis_index("y"), lax.axis_index("c")]).astype(jnp.int32)


def _gather_all(inputs, where):
    fulls = []
    for name, k, n, ax in BIG:
        w2 = inputs[name][0]
        rs, cs = w2.shape
        tm = _tile(rs, 512)
        steps = rs // tm
        if ax == 1:
            blk, idx = (tm, cs), lambda i, w: (i, w[0])
        else:
            blk, idx = (tm, n), functools.partial(lambda i, w, steps: (w[0] * steps + i, 0), steps=steps)
        fulls.append(_placed("cast_" + name, lambda w: w, steps, where, [(w2, (tm, cs), lambda i, w: (i, 0))],
                             jax.ShapeDtypeStruct((k, n), MXU_DTYPE), blk, idx))
    return _gather_weights(fulls)


def _reduce_all(inputs, grads, where):
    got = _reduce_swap_halves(grads)
    parts, geom = [], []
    for i, (name, k, n, ax) in enumerate(BIG):
        pr, pc = _piece_shape(k, n, ax)
        tm = _tile(pr, 512)
        spp = pr // tm
        geom.append((pr, pc, tm, spp))
        if ax == 1:
            g_idx = functools.partial(lambda i, w, spp: (w[1] * spp + i % spp, i // spp), spp=spp)
        else:
            g_idx = functools.partial(lambda i, w, spp: ((i // spp) * 2 * spp + w[1] * spp + i % spp, 0), spp=spp)
        parts.append(_placed("pair_sum_" + name, lambda a, b: a + b, 4 * spp, where,
                             [(grads[i], (tm, pc), g_idx), (got[i].reshape(4 * pr, pc), (tm, pc), lambda i, w: (i, 0))],
                             jax.ShapeDtypeStruct((4 * pr, pc), BF16), (tm, pc), lambda i, w: (i, 0)).reshape(4, pr, pc))
    landed = _reduce_to_owner(parts)
    halves = []
    for i, (name, k, n, ax) in enumerate(BIG):
        pr, pc, tm, spp = geom[i]
        shard_shape = inputs[name].shape[1:]
        ins = [(parts[i], (None, tm, pc), lambda i, w: (w[0], i, 0))]
        ins += [(landed[i], (None, tm, pc), functools.partial(lambda i, w, j: (j, i, 0), j=j)) for j in range(3)]
        halves.append(_placed("chip_sum_" + name,
                              lambda a, b, c, d: ((a.astype(F32) + b.astype(F32)) + c.astype(F32)) + d.astype(F32),
                              spp, where, ins, jax.ShapeDtypeStruct(shard_shape, F32), (tm, pc),
                              functools.partial(lambda i, w, spp: (w[1] * spp + i, 0), spp=spp)))
    return _share_with_sibling(halves)


def _step(inputs):
    x, mem, positions, target = inputs["x"][0], inputs["mem"][0], inputs["positions"], inputs["loss_target"][0]
    pos = positions.reshape(-1, 1)
    where = _where()
    full = _gather_all(inputs, where)
    wb = {name: full[i] for i, (name, _, _, _) in enumerate(BIG)}
    sp = {name: _as2d(inputs[name]) for name in SMALL}
    memb, = _rowwise("cast_mem", lambda m: (m,), [mem], [], [(D_MODEL, MXU_DTYPE)])

    loss, dx, gbig, gsmall = _local_step(x, memb, pos, target, sp, wb)
    gshard = _reduce_all(inputs, [gbig[name] for name, _, _, _ in BIG], where)

    out = {}
    for i, (name, _, _, _) in enumerate(BIG):
        w2, m2, v2 = inputs[name][0], inputs["m_" + name][0], inputs["v_" + name][0]
        n = w2.shape[1]
        d, nm, nv = _rowwise("adamw_" + name, _adamw, [w2, gshard[i], m2, v2], [], [(n, F32)] * 3, tm=128)
        lead = inputs[name].shape
        out[name] = (gshard[i].reshape(lead), d.reshape(lead), nm.reshape(lead), nv.reshape(lead))

    def tiles(a):
        flat = a.reshape(-1)
        n = -(-flat.shape[0] // 1024) * 1024
        return jnp.pad(flat, (0, n - flat.shape[0])).reshape(n // 128, 128)

    pieces = [tiles(loss[:, :1])] + [tiles(gsmall[name]) for name in SMALL]
    if sum(p.shape[0] for p in pieces) % 16:
        pieces.append(jnp.zeros((8, 128), F32))
    red = _allreduce_small(jnp.concatenate(pieces, axis=0))
    loss_total = red[0, 0]
    grads, off = {}, pieces[0].shape[0]
    for name, p in zip(SMALL, pieces[1:]):
        shp = _as2d(inputs[name]).shape
        grads[name] = red[off:off + p.shape[0]].reshape(-1)[:shp[0] * shp[1]].reshape(shp)
        off += p.shape[0]
    upd = _adamw_small([(_as2d(inputs[n]), grads[n], _as2d(inputs["m_" + n]), _as2d(inputs["v_" + n])) for n in SMALL])
    for i, name in enumerate(SMALL):
        shp = inputs[name].shape
        out[name] = (grads[name].reshape(shp),) + tuple(t.reshape(shp) for t in upd[3 * i:3 * i + 3])
    return loss_total, dx.reshape(inputs["x"].shape), out


_ARG_NAMES = (("x", "mem", "positions") + WEIGHT_ORDER + ("loss_target",) + tuple("m_" + n for n in WEIGHT_ORDER)
              + tuple("v_" + n for n in WEIGHT_ORDER))


def kernel(x, mem, positions, ln_in_g, ln_in_b, w_in, b_in, ssm_log_dt, ssm_a_re, ssm_a_im, ssm_b_re, ssm_b_im, ssm_c_re, ssm_c_im, ssm_d, w_glu, b_glu, w_att_up, w_mix_out, b_mix_out, ln1_g, ln1_b, w_xq, w_xkv, w_xo, ln2_g, ln2_b, w_ff1, b_ff1, w_ff2, b_ff2, ln3_g, ln3_b, loss_target, m_ln_in_g, m_ln_in_b, m_w_in, m_b_in, m_ssm_log_dt, m_ssm_a_re, m_ssm_a_im, m_ssm_b_re, m_ssm_b_im, m_ssm_c_re, m_ssm_c_im, m_ssm_d, m_w_glu, m_b_glu, m_w_att_up, m_w_mix_out, m_b_mix_out, m_ln1_g, m_ln1_b, m_w_xq, m_w_xkv, m_w_xo, m_ln2_g, m_ln2_b, m_w_ff1, m_b_ff1, m_w_ff2, m_b_ff2, m_ln3_g, m_ln3_b, v_ln_in_g, v_ln_in_b, v_w_in, v_b_in, v_ssm_log_dt, v_ssm_a_re, v_ssm_a_im, v_ssm_b_re, v_ssm_b_im, v_ssm_c_re, v_ssm_c_im, v_ssm_d, v_w_glu, v_b_glu, v_w_att_up, v_w_mix_out, v_b_mix_out, v_ln1_g, v_ln1_b, v_w_xq, v_w_xkv, v_w_xo, v_ln2_g, v_ln2_b, v_w_ff1, v_b_ff1, v_w_ff2, v_b_ff2, v_ln3_g, v_ln3_b):
    args = (x, mem, positions, ln_in_g, ln_in_b, w_in, b_in, ssm_log_dt, ssm_a_re, ssm_a_im, ssm_b_re, ssm_b_im, ssm_c_re, ssm_c_im, ssm_d, w_glu, b_glu, w_att_up, w_mix_out, b_mix_out, ln1_g, ln1_b, w_xq, w_xkv, w_xo, ln2_g, ln2_b, w_ff1, b_ff1, w_ff2, b_ff2, ln3_g, ln3_b, loss_target, m_ln_in_g, m_ln_in_b, m_w_in, m_b_in, m_ssm_log_dt, m_ssm_a_re, m_ssm_a_im, m_ssm_b_re, m_ssm_b_im, m_ssm_c_re, m_ssm_c_im, m_ssm_d, m_w_glu, m_b_glu, m_w_att_up, m_w_mix_out, m_b_mix_out, m_ln1_g, m_ln1_b, m_w_xq, m_w_xkv, m_w_xo, m_ln2_g, m_ln2_b, m_w_ff1, m_b_ff1, m_w_ff2, m_b_ff2, m_ln3_g, m_ln3_b, v_ln_in_g, v_ln_in_b, v_w_in, v_b_in, v_ssm_log_dt, v_ssm_a_re, v_ssm_a_im, v_ssm_b_re, v_ssm_b_im, v_ssm_c_re, v_ssm_c_im, v_ssm_d, v_w_glu, v_b_glu, v_w_att_up, v_w_mix_out, v_b_mix_out, v_ln1_g, v_ln1_b, v_w_xq, v_w_xkv, v_w_xo, v_ln2_g, v_ln2_b, v_w_ff1, v_b_ff1, v_w_ff2, v_b_ff2, v_ln3_g, v_ln3_b)
    assert len(args) == len(_ARG_NAMES)
    inputs = dict(zip(_ARG_NAMES, args))
    loss, dx, out = _step(inputs)
    res = [loss, dx]
    for k in range(4):
        res += [out[name][k] for name in WEIGHT_ORDER]
    return tuple(res)
```

```python
import functools
import math

import numpy as np
import jax
import jax.numpy as jnp
from jax import lax
from jax.experimental import pallas as pl
from jax.experimental.pallas import tpu as pltpu

F32 = jnp.float32
BF16 = jnp.bfloat16
MXU_DTYPE = jnp.bfloat16

D_MODEL = 1024
SSM_GROUP = 16
SSM_WIDTH = 768
SSM_GROUPS = 48
SSM_STATE = 64
N_STATE = SSM_GROUPS * SSM_STATE
SSM_CHUNKS = 6
CH_W = 128
CH_N = 512
ATT_HEAD_DIM = 64
ATT_HPG = 4
ATT_GROUPW = ATT_HPG * ATT_HEAD_DIM
DILATIONS = (1, 4, 16)
ATT_BLK = 128
ATT_SCALE = ATT_HEAD_DIM ** -0.5
ROT_DIM = 16
ROPE_THETA = 500000.0
XATT_HEADS = 4
XATT_HEAD_DIM = 256
XATT_SCALE = XATT_HEAD_DIM ** -0.5
D_FF = 4096
IN_COLS = 5120
DEEPNORM_ALPHA = 2.0 ** 0.25
LN_EPS = 1e-5
NEG_INF = -1e30
ADAM_LR = 0.001
ADAM_B1 = 0.9
ADAM_B2 = 0.999
ADAM_EPS = 1e-08
ADAM_WD = 0.01
ADAM_STEP = 10

N_SEG = 32
VMEM_LIMIT = 48 * 1024 * 1024
MESH = pl.DeviceIdType.MESH
HBM_SPEC = pl.BlockSpec(memory_space=pltpu.HBM)
VMEM_SPEC = pl.BlockSpec(memory_space=pltpu.VMEM)

BIG = (("w_in", 1024, 5120, 1), ("w_glu", 768, 2048, 1), ("w_att_up", 256, 1024, 1),
       ("w_mix_out", 1024, 1024, 0), ("w_xq", 1024, 1024, 0), ("w_xkv", 1024, 2048, 1),
       ("w_xo", 1024, 1024, 0), ("w_ff1", 1024, 4096, 1), ("w_ff2", 4096, 1024, 0))
SMALL = ("ln_in_g", "ln_in_b", "b_in", "ssm_log_dt", "ssm_a_re", "ssm_a_im", "ssm_b_re", "ssm_b_im",
         "ssm_c_re", "ssm_c_im", "ssm_d", "b_glu", "b_mix_out", "ln1_g", "ln1_b", "ln2_g", "ln2_b",
         "b_ff1", "b_ff2", "ln3_g", "ln3_b")
WEIGHT_ORDER = ("ln_in_g", "ln_in_b", "w_in", "b_in", "ssm_log_dt", "ssm_a_re", "ssm_a_im", "ssm_b_re",
                "ssm_b_im", "ssm_c_re", "ssm_c_im", "ssm_d", "w_glu", "b_glu", "w_att_up", "w_mix_out",
                "b_mix_out", "ln1_g", "ln1_b", "w_xq", "w_xkv", "w_xo", "ln2_g", "ln2_b", "w_ff1", "b_ff1",
                "w_ff2", "b_ff2", "ln3_g", "ln3_b")


def _cparams(n_axes):
    return pltpu.CompilerParams(dimension_semantics=("arbitrary",) * n_axes, vmem_limit_bytes=VMEM_LIMIT)


def _rowwise(name, fn, rows, consts, outs, reds=(), tm=256, touts=()):
    n_rows = (rows[0][0] if isinstance(rows[0], tuple) else rows[0]).shape[-2]
    tm = min(tm, n_rows)
    assert n_rows % tm == 0, (name, n_rows, tm)
    specs, args = [], []
    for r in rows:
        if isinstance(r, tuple) and len(r) == 3:
            arr, width, cb = r
            specs.append(pl.BlockSpec((tm, width), functools.partial(lambda i, cb: (i, cb), cb=cb)))
        elif isinstance(r, tuple):
            arr, slot = r
            specs.append(pl.BlockSpec((None, tm, arr.shape[2]), functools.partial(lambda i, s: (s, i, 0), s=slot)))
        else:
            arr = r
            specs.append(pl.BlockSpec((tm, arr.shape[1]), lambda i: (i, 0)))
        args.append(arr)
        assert arr.shape[-2] == n_rows, (name, arr.shape, n_rows)
    for cst in consts:
        specs.append(pl.BlockSpec(cst.shape, lambda i: (0, 0)))
        args.append(cst)
    n_r, n_c, n_o, n_d = len(rows), len(consts), len(outs) + len(touts), len(reds)
    out_shape = [jax.ShapeDtypeStruct((n_rows, c), dt) for c, dt in outs]
    out_specs = [pl.BlockSpec((tm, c), lambda i: (i, 0)) for c, _ in outs]
    out_shape += [jax.ShapeDtypeStruct((r, n_rows), dt) for r, dt in touts]
    out_specs += [pl.BlockSpec((r, tm), lambda i: (0, i)) for r, _ in touts]
    out_shape += [jax.ShapeDtypeStruct((1, c), F32) for c in reds]
    out_specs += [pl.BlockSpec((1, c), lambda i: (0, 0)) for c in reds]

    def body(*refs):
        ins = [r[...] for r in refs[:n_r + n_c]]
        o_refs = refs[n_r + n_c:n_r + n_c + n_o]
        d_refs = refs[n_r + n_c + n_o:]
        res = fn(*ins)
        res = res if isinstance(res, (tuple, list)) else (res,)
        assert len(res) == n_o + n_d, (name, len(res))
        for ref, val in zip(o_refs, res[:n_o]):
            ref[...] = val.astype(ref.dtype)
        first = pl.program_id(0) == 0
        for ref, val in zip(d_refs, res[n_o:]):
            @pl.when(first)
            def _(ref=ref, val=val):
                ref[...] = val

            @pl.when(jnp.logical_not(first))
            def _(ref=ref, val=val):
                ref[...] += val

    res = pl.pallas_call(body, name=name, grid=(n_rows // tm,), in_specs=specs, out_specs=out_specs,
                         out_shape=out_shape, compiler_params=_cparams(1))(*args)
    return res


def _colsum(v):
    return jnp.sum(v.astype(F32), axis=0, keepdims=True)


_DIMS = {"nn": (((1,), (0,)), ((), ())), "nt": (((1,), (1,)), ((), ())), "tn": (((0,), (0,)), ((), ()))}


def _tile(dim, want):
    if dim <= want:
        return dim
    return max(t for t in range(128, want + 1, 128) if dim % t == 0)


def _dot(a, b, mode):
    return lax.dot_general(a.astype(MXU_DTYPE), b.astype(MXU_DTYPE), _DIMS[mode], preferred_element_type=F32)


def _mm(name, a, b, mode, *, bias=None, extras=(), epilogue=None, out_dtypes=(F32,), tm=1024, tn=1024, tk=1024):
    if mode == "nn":
        (m, k), (_, n) = a.shape, b.shape
    elif mode == "nt":
        (m, k), (n, _) = a.shape, b.shape
    else:
        (k, m), (_, n) = a.shape, b.shape
    tm, tn = _tile(m, tm), _tile(n, tn)
    if mode != "tn":
        tk = k if k <= 1024 else tk
    tk = _tile(k, tk)
    assert m % tm == 0 and n % tn == 0 and k % tk == 0, (name, m, n, k)
    nk = k // tk
    a_spec = {"nn": pl.BlockSpec((tm, tk), lambda i, j, kk: (i, kk)),
              "nt": pl.BlockSpec((tm, tk), lambda i, j, kk: (i, kk)),
              "tn": pl.BlockSpec((tk, tm), lambda i, j, kk: (kk, i))}[mode]
    b_spec = {"nn": pl.BlockSpec((tk, tn), lambda i, j, kk: (kk, j)),
              "nt": pl.BlockSpec((tn, tk), lambda i, j, kk: (j, kk)),
              "tn": pl.BlockSpec((tk, tn), lambda i, j, kk: (kk, j))}[mode]
    specs, args = [a_spec, b_spec], [a, b]
    if bias is not None:
        specs.append(pl.BlockSpec((1, tn), lambda i, j, kk: (0, j)))
        args.append(bias)
    for e in extras:
        specs.append(pl.BlockSpec((tm, tn), lambda i, j, kk: (i, j)))
        args.append(e)
    n_e, n_o = len(extras), len(out_dtypes)
    has_bias = bias is not None

    def body(*refs):
        a_ref, b_ref = refs[0], refs[1]
        pos = 2
        bias_ref = refs[pos] if has_bias else None
        pos += int(has_bias)
        e_refs = refs[pos:pos + n_e]
        o_refs = refs[pos + n_e:pos + n_e + n_o]
        acc_ref = refs[pos + n_e + n_o] if nk > 1 else None
        part = _dot(a_ref[...], b_ref[...], mode)

        def finish(r):
            if has_bias:
                r = r + bias_ref[...]
            res = epilogue(r, *[e[...] for e in e_refs]) if epilogue is not None else (r,)
            for ref, val in zip(o_refs, res):
                ref[...] = val.astype(ref.dtype)

        if nk == 1:
            finish(part)
        else:
            kk = pl.program_id(2)

            @pl.when(kk == 0)
            def _():
                acc_ref[...] = part

            @pl.when(kk > 0)
            def _():
                acc_ref[...] += part

            @pl.when(kk == nk - 1)
            def _():
                finish(acc_ref[...])

    res = pl.pallas_call(
        body, name=name, grid=(m // tm, n // tn, nk), in_specs=specs,
        out_specs=[pl.BlockSpec((tm, tn), lambda i, j, kk: (i, j)) for _ in out_dtypes],
        out_shape=[jax.ShapeDtypeStruct((m, n), dt) for dt in out_dtypes],
        scratch_shapes=[pltpu.VMEM((tm, tn), F32)] if nk > 1 else [],
        compiler_params=_cparams(3))(*args)
    return res[0] if n_o == 1 else res


def _ssm_expand(name, a, bmat, mode, tm=512):
    s = a.shape[0]
    tm = min(tm, s)

    def body(a_ref, b_ref, re_ref, im_ref):
        r = _dot(a_ref[...], b_ref[...], mode)
        re_ref[...] = r[:, :CH_N]
        im_ref[...] = r[:, CH_N:]

    return pl.pallas_call(
        body, name=name, grid=(s // tm, SSM_CHUNKS),
        in_specs=[pl.BlockSpec((tm, CH_W), lambda i, j: (i, j)),
                  pl.BlockSpec((None,) + bmat.shape[1:], lambda i, j: (j, 0, 0))],
        out_specs=[pl.BlockSpec((tm, CH_N), lambda i, j: (i, j))] * 2,
        out_shape=[jax.ShapeDtypeStruct((s, N_STATE), F32)] * 2,
        compiler_params=_cparams(2))(a, bmat)


def _ssm_contract(name, a_re, a_im, bmat, mode, d_row, extra, tm=512):
    s = a_re.shape[0]
    tm = min(tm, s)

    def body(re_ref, im_ref, b_ref, d_ref, e_ref, o_ref):
        b = b_ref[...]
        if mode == "nn":
            r = _dot(re_ref[...], b[:CH_N], "nn") + _dot(im_ref[...], b[CH_N:], "nn")
        else:
            r = _dot(re_ref[...], b[:, :CH_N], "nt") + _dot(im_ref[...], b[:, CH_N:], "nt")
        o_ref[...] = r + d_ref[...] * e_ref[...]

    return pl.pallas_call(
        body, name=name, grid=(s // tm, SSM_CHUNKS),
        in_specs=[pl.BlockSpec((tm, CH_N), lambda i, j: (i, j)), pl.BlockSpec((tm, CH_N), lambda i, j: (i, j)),
                  pl.BlockSpec((None,) + bmat.shape[1:], lambda i, j: (j, 0, 0)),
                  pl.BlockSpec((1, CH_W), lambda i, j: (0, j)), pl.BlockSpec((tm, CH_W), lambda i, j: (i, j))],
        out_specs=pl.BlockSpec((tm, CH_W), lambda i, j: (i, j)),
        out_shape=jax.ShapeDtypeStruct((s, SSM_WIDTH), F32),
        compiler_params=_cparams(2))(a_re, a_im, bmat, d_row, extra)


def _ssm_wgrad(name, chan, st_re, st_im, expand, tk=512):
    s = chan.shape[0]
    tk = min(tk, s)
    nk = s // tk
    oshape = (CH_W, 2 * CH_N) if expand else (2 * CH_N, CH_W)

    def body(c_ref, re_ref, im_ref, o_ref):
        c = c_ref[...]
        if expand:
            part = jnp.concatenate([_dot(c, re_ref[...], "tn"), _dot(c, im_ref[...], "tn")], axis=1)
        else:
            part = jnp.concatenate([_dot(re_ref[...], c, "tn"), _dot(im_ref[...], c, "tn")], axis=0)
        kk = pl.program_id(1)

        @pl.when(kk == 0)
        def _():
            o_ref[...] = part

        @pl.when(kk > 0)
        def _():
            o_ref[...] += part

    return pl.pallas_call(
        body, name=name, grid=(SSM_CHUNKS, nk),
        in_specs=[pl.BlockSpec((tk, CH_W), lambda j, kk: (kk, j)), pl.BlockSpec((tk, CH_N), lambda j, kk: (kk, j)),
                  pl.BlockSpec((tk, CH_N), lambda j, kk: (kk, j))],
        out_specs=pl.BlockSpec((None,) + oshape, lambda j, kk: (j, 0, 0)),
        out_shape=jax.ShapeDtypeStruct((SSM_CHUNKS,) + oshape, F32),
        compiler_params=_cparams(2))(chan, st_re, st_im)


SCAN_LB = 256


def _ssm_scan(name, w_re, w_im, a_re, a_im, reverse):
    s = w_re.shape[0]
    seg_len = s // N_SEG
    n_sq = int(math.log2(seg_len))
    assert 2 ** n_sq == seg_len

    def body(are_ref, aim_ref, wre_ref, wim_ref, hre_ref, him_ref, ere, eim, cre, cim):
        ar1 = are_ref[...]
        ai1 = -aim_ref[...] if reverse else aim_ref[...]
        ar = jnp.broadcast_to(ar1, (N_SEG, SCAN_LB))
        ai = jnp.broadcast_to(ai1, (N_SEG, SCAN_LB))

        def rows_of(k):
            kk = seg_len - 1 - k if reverse else k
            return pl.ds(pl.multiple_of(kk * N_SEG, N_SEG), N_SEG)

        def local(k, carry):
            hr, hi = carry
            rows = rows_of(k)
            nr = ar * hr - ai * hi + wre_ref[rows, :]
            ni = ar * hi + ai * hr + wim_ref[rows, :]
            hre_ref[rows, :] = nr
            him_ref[rows, :] = ni
            return nr, ni

        zero = jnp.zeros((N_SEG, SCAN_LB), F32)
        er, ei = lax.fori_loop(0, seg_len, local, (zero, zero))
        ere[...] = er
        eim[...] = ei
        pr, pi = ar1, ai1
        for _ in range(n_sq):
            pr, pi = pr * pr - pi * pi, 2.0 * pr * pi
        cr = jnp.zeros((1, SCAN_LB), F32)
        ci = jnp.zeros((1, SCAN_LB), F32)
        for jj in range(N_SEG):
            j = N_SEG - 1 - jj if reverse else jj
            cre[j:j + 1, :] = cr
            cim[j:j + 1, :] = ci
            er_j, ei_j = ere[j:j + 1, :], eim[j:j + 1, :]
            cr, ci = pr * cr - pi * ci + er_j, pr * ci + pi * cr + ei_j
        c_r, c_i = cre[...], cim[...]

        def fix(k, carry):
            qr, qi = carry
            rows = rows_of(k)
            hre_ref[rows, :] = hre_ref[rows, :] + (qr * c_r - qi * c_i)
            him_ref[rows, :] = him_ref[rows, :] + (qr * c_i + qi * c_r)
            return qr * ar - qi * ai, qr * ai + qi * ar

        lax.fori_loop(0, seg_len, fix, (ar, ai))

    nblk = N_STATE // SCAN_LB
    blk = pl.BlockSpec((s, SCAN_LB), lambda b: (0, b))
    row = pl.BlockSpec((1, SCAN_LB), lambda b: (0, b))
    return pl.pallas_call(
        body, name=name, grid=(nblk,), in_specs=[row, row, blk, blk], out_specs=[blk, blk],
        out_shape=[jax.ShapeDtypeStruct((s, N_STATE), F32)] * 2,
        scratch_shapes=[pltpu.VMEM((N_SEG, SCAN_LB), F32)] * 4,
        compiler_params=_cparams(1))(a_re, a_im, w_re, w_im)


def _ssm_da(g_re, g_im, h_re, h_im):
    s = g_re.shape[0]
    seg_len = s // N_SEG

    def body(gre_ref, gim_ref, hre_ref, him_ref, dre_ref, dim_ref):
        def rows_of(k):
            return pl.ds(pl.multiple_of(k * N_SEG, N_SEG), N_SEG)

        def step(k, carry):
            sr, si = carry
            gr, gi = gre_ref[rows_of(k), :], gim_ref[rows_of(k), :]
            pr, pi = hre_ref[rows_of(k - 1), :], him_ref[rows_of(k - 1), :]
            return sr + gr * pr + gi * pi, si + gi * pr - gr * pi

        zero = jnp.zeros((N_SEG, SCAN_LB), F32)
        sr, si = lax.fori_loop(1, seg_len, step, (zero, zero))
        last = pl.ds((seg_len - 1) * N_SEG, N_SEG)
        first_row = lax.broadcasted_iota(jnp.int32, (N_SEG, SCAN_LB), 0) == 0
        pr = jnp.where(first_row, 0.0, pltpu.roll(hre_ref[last, :], 1, 0))
        pi = jnp.where(first_row, 0.0, pltpu.roll(him_ref[last, :], 1, 0))
        gr, gi = gre_ref[pl.ds(0, N_SEG), :], gim_ref[pl.ds(0, N_SEG), :]
        sr = sr + gr * pr + gi * pi
        si = si + gi * pr - gr * pi
        dre_ref[...] = jnp.sum(sr, axis=0, keepdims=True)
        dim_ref[...] = jnp.sum(si, axis=0, keepdims=True)

    nblk = N_STATE // SCAN_LB
    blk = pl.BlockSpec((s, SCAN_LB), lambda b: (0, b))
    row = pl.BlockSpec((1, SCAN_LB), lambda b: (0, b))
    return pl.pallas_call(
        body, name="ssm_da", grid=(nblk,), in_specs=[blk] * 4, out_specs=[row, row],
        out_shape=[jax.ShapeDtypeStruct((1, N_STATE), F32)] * 2,
        compiler_params=_cparams(1))(g_re, g_im, h_re, h_im)


def _disc(ldt, are, aim, bre, bim):
    dt = jnp.exp(ldt)
    mag = jnp.exp(are * dt)
    abr = mag * jnp.cos(aim * dt)
    abi = mag * jnp.sin(aim * dt)
    den = jnp.square(are) + jnp.square(aim)
    nr = abr - 1.0
    fre = (nr * are + abi * aim) / den
    fim = (abi * are - nr * aim) / den
    return abr, abi, fre * bre - fim * bim, fre * bim + fim * bre


def _ssm_disc_fwd(ldt, are, aim, bre, bim):
    def body(l_ref, ar_ref, ai_ref, br_ref, bi_ref, o0, o1, o2, o3):
        res = _disc(l_ref[...], ar_ref[...], ai_ref[...], br_ref[...], bi_ref[...])
        for ref, val in zip((o0, o1, o2, o3), res):
            ref[...] = val

    col = jax.ShapeDtypeStruct((N_STATE, 1), F32)
    mat = jax.ShapeDtypeStruct((N_STATE, SSM_GROUP), F32)
    return pl.pallas_call(body, name="ssm_disc_fwd", out_shape=[col, col, mat, mat],
                          in_specs=[VMEM_SPEC] * 5, out_specs=[VMEM_SPEC] * 4)(ldt, are, aim, bre, bim)


def _ssm_disc_bwd(ldt, are, aim, bre, bim, d_abr, d_abi, d_bbr, d_bbi):
    def body(l_ref, ar_ref, ai_ref, br_ref, bi_ref, c0, c1, c2, c3, g_ldt, g_are, g_aim, g_bre, g_bim):
        _, vjp = jax.vjp(_disc, l_ref[...], ar_ref[...], ai_ref[...], br_ref[...], bi_ref[...])
        dl, dar, dai, dbr, dbi = vjp((c0[...], c1[...], c2[...], c3[...]))
        state = lax.broadcasted_iota(jnp.int32, (N_STATE, SSM_GROUPS), 0)
        group = lax.broadcasted_iota(jnp.int32, (N_STATE, SSM_GROUPS), 1)
        pick = jnp.right_shift(state, 6) == group
        g_ldt[...] = jnp.sum(jnp.where(pick, dl, 0.0), axis=0, keepdims=True)
        g_are[...] = dar
        g_aim[...] = dai
        g_bre[...] = dbr
        g_bim[...] = dbi

    col = jax.ShapeDtypeStruct((N_STATE, 1), F32)
    mat = jax.ShapeDtypeStruct((N_STATE, SSM_GROUP), F32)
    return pl.pallas_call(body, name="ssm_disc_bwd",
                          out_shape=[jax.ShapeDtypeStruct((1, SSM_GROUPS), F32), col, col, mat, mat],
                          in_specs=[VMEM_SPEC] * 9, out_specs=[VMEM_SPEC] * 5,
                          compiler_params=pltpu.CompilerParams(vmem_limit_bytes=VMEM_LIMIT))(
        ldt, are, aim, bre, bim, d_abr, d_abi, d_bbr, d_bbi)


_EYE8 = np.eye(8, dtype=np.float32)


def _blockdiag_b(bb):
    t = bb.reshape(SSM_CHUNKS, 8, SSM_STATE, SSM_GROUP).transpose(0, 1, 3, 2)
    return jnp.einsum("igcn,gh->igchn", t, _EYE8).reshape(SSM_CHUNKS, CH_W, CH_N)


def _diag_of_b(m):
    t = jnp.einsum("igchn,gh->igcn", m.reshape(SSM_CHUNKS, 8, SSM_GROUP, 8, SSM_STATE), _EYE8)
    return t.transpose(0, 1, 3, 2).reshape(N_STATE, SSM_GROUP)


def _blockdiag_c(c):
    t = c.reshape(SSM_CHUNKS, 8, SSM_GROUP, SSM_STATE).transpose(0, 1, 3, 2)
    return jnp.einsum("ignc,gh->ignhc", t, _EYE8).reshape(SSM_CHUNKS, CH_N, CH_W)


def _diag_of_c(m):
    t = jnp.einsum("ignhc,gh->ignc", m.reshape(SSM_CHUNKS, 8, SSM_STATE, 8, SSM_GROUP), _EYE8)
    return t.transpose(0, 1, 3, 2).reshape(SSM_GROUPS, SSM_GROUP, SSM_STATE)


def _time_perm(a):
    s, c = a.shape
    return a.reshape(N_SEG, s // N_SEG, c).transpose(1, 0, 2).reshape(s, c)


def _time_unperm(a):
    s, c = a.shape
    return a.reshape(s // N_SEG, N_SEG, c).transpose(1, 0, 2).reshape(s, c)


def _dilate(a, d):
    s, c = a.shape
    return a if d == 1 else a.reshape(s // d, d, c).transpose(1, 0, 2).reshape(s, c)


def _undilate(a, d):
    s, c = a.shape
    return a if d == 1 else a.reshape(d, s // d, c).transpose(1, 0, 2).reshape(s, c)


def _dilate_rows(a, d):
    r, s = a.shape
    return a if d == 1 else a.reshape(r, s // d, d).transpose(0, 2, 1).reshape(r, s)


ATT_T = 4
ATT_ROWS = ATT_T * ATT_BLK


def _window(prev_ref, cur_ref, i, sl):
    if i == 0:
        return jnp.concatenate([prev_ref[:, sl], cur_ref[0:ATT_BLK, sl]], axis=0)
    return cur_ref[(i - 1) * ATT_BLK:(i + 1) * ATT_BLK, sl]


def _band_valid(first_key):
    qi = lax.broadcasted_iota(jnp.int32, (ATT_BLK, 2 * ATT_BLK), 0)
    ki = lax.broadcasted_iota(jnp.int32, (ATT_BLK, 2 * ATT_BLK), 1)
    steps = qi + ATT_BLK - ki
    return (steps >= 0) & (steps <= ATT_BLK) & (ki >= first_key)


ATT_STATW = ATT_HPG * 128


def _stat(h):
    return slice(h * 128, (h + 1) * 128)


def _stat_rows(stat):
    n = stat.shape[0]
    heads = [stat[:, _stat(h)].T[0:1, :] for h in range(ATT_HPG)]
    return jnp.concatenate(heads + [jnp.zeros((8 - ATT_HPG, n), stat.dtype)], axis=0)


def _attn_specs(nb, width=ATT_GROUPW):
    cur = pl.BlockSpec((ATT_ROWS, width), lambda b: (b, 0))
    prev = pl.BlockSpec((ATT_BLK, width), lambda b: (jnp.maximum(b * ATT_T - 1, 0), 0))
    nxt = pl.BlockSpec((ATT_BLK, width), lambda b: (jnp.minimum((b + 1) * ATT_T, nb - 1), 0))
    return cur, prev, nxt


def _attn_fwd(tag, per_seq, q, k, v):
    s = q.shape[0]
    nb = s // ATT_BLK

    def body(q_ref, kc_ref, kp_ref, vc_ref, vp_ref, o_ref, lse_ref):
        bt = pl.program_id(0)
        for i in range(ATT_T):
            has_prev = lax.rem(bt * ATT_T + i, per_seq) > 0
            valid = _band_valid(jnp.where(has_prev, 0, ATT_BLK))
            rows = slice(i * ATT_BLK, (i + 1) * ATT_BLK)
            for h in range(ATT_HPG):
                sl = slice(h * ATT_HEAD_DIM, (h + 1) * ATT_HEAD_DIM)
                kcat = _window(kp_ref, kc_ref, i, sl)
                vcat = _window(vp_ref, vc_ref, i, sl)
                sc = _dot(q_ref[rows, sl], kcat, "nt") * ATT_SCALE
                sc = jnp.where(valid, sc, NEG_INF)
                m = jnp.max(sc, axis=-1, keepdims=True)
                p = jnp.exp(sc - m)
                den = jnp.sum(p, axis=-1, keepdims=True)
                o_ref[rows, sl] = _dot(p, vcat, "nn") / den
                lse_ref[rows, _stat(h)] = jnp.broadcast_to(m + jnp.log(den), (ATT_BLK, 128))

    cur, prev, _ = _attn_specs(nb)
    stat, _, _ = _attn_specs(nb, ATT_STATW)
    return pl.pallas_call(
        body, name="attn_fwd_" + tag, grid=(nb // ATT_T,), in_specs=[cur, cur, prev, cur, prev], out_specs=[cur, stat],
        out_shape=[jax.ShapeDtypeStruct((s, ATT_GROUPW), F32), jax.ShapeDtypeStruct((s, ATT_STATW), F32)],
        compiler_params=_cparams(1))(q, k, k, v, v)


def _attn_dq(tag, per_seq, q, k, v, do, lse, delta):
    s = q.shape[0]
    nb = s // ATT_BLK

    def body(q_ref, kc_ref, kp_ref, vc_ref, vp_ref, do_ref, lse_ref, dl_ref, dq_ref):
        bt = pl.program_id(0)
        for i in range(ATT_T):
            has_prev = lax.rem(bt * ATT_T + i, per_seq) > 0
            valid = _band_valid(jnp.where(has_prev, 0, ATT_BLK))
            rows = slice(i * ATT_BLK, (i + 1) * ATT_BLK)
            for h in range(ATT_HPG):
                sl = slice(h * ATT_HEAD_DIM, (h + 1) * ATT_HEAD_DIM)
                kcat = _window(kp_ref, kc_ref, i, sl)
                vcat = _window(vp_ref, vc_ref, i, sl)
                lse = jnp.concatenate([lse_ref[rows, _stat(h)]] * 2, axis=1)
                dlt = jnp.concatenate([dl_ref[rows, _stat(h)]] * 2, axis=1)
                sc = _dot(q_ref[rows, sl], kcat, "nt") * ATT_SCALE
                p = jnp.exp(jnp.where(valid, sc, NEG_INF) - lse)
                dp = _dot(do_ref[rows, sl], vcat, "nt")
                ds = p * (dp - dlt) * ATT_SCALE
                dq_ref[rows, sl] = _dot(ds, kcat, "nn")

    cur, prev, _ = _attn_specs(nb)
    stat, _, _ = _attn_specs(nb, ATT_STATW)
    return pl.pallas_call(
        body, name="attn_dq_" + tag, grid=(nb // ATT_T,), in_specs=[cur, cur, prev, cur, prev, cur, stat, stat],
        out_specs=cur, out_shape=jax.ShapeDtypeStruct((s, ATT_GROUPW), F32),
        compiler_params=_cparams(1))(q, k, k, v, v, do, lse, delta)


def _attn_dkv(tag, per_seq, q, k, v, do, lse_t, delta_t):
    s = q.shape[0]
    nb = s // ATT_BLK

    def body(k_ref, v_ref, qc_ref, qn_ref, doc_ref, don_ref, lc_ref, ln_ref, dc_ref, dn_ref, dk_ref, dv_ref):
        bt = pl.program_id(0)
        ki = lax.broadcasted_iota(jnp.int32, (ATT_BLK, 2 * ATT_BLK), 0)
        ci = lax.broadcasted_iota(jnp.int32, (ATT_BLK, 2 * ATT_BLK), 1)

        def pair(edge_ref, cur_ref, i, sl):
            if i == ATT_T - 1:
                return jnp.concatenate([cur_ref[i * ATT_BLK:(i + 1) * ATT_BLK, sl], edge_ref[:, sl]], axis=0)
            return cur_ref[i * ATT_BLK:(i + 2) * ATT_BLK, sl]

        def pair_row(edge_ref, cur_ref, i, h):
            if i == ATT_T - 1:
                row = jnp.concatenate([cur_ref[h:h + 1, i * ATT_BLK:(i + 1) * ATT_BLK], edge_ref[h:h + 1, :]], axis=1)
            else:
                row = cur_ref[h:h + 1, i * ATT_BLK:(i + 2) * ATT_BLK]
            return jnp.broadcast_to(row, (ATT_BLK, 2 * ATT_BLK))

        for i in range(ATT_T):
            b = bt * ATT_T + i
            next_uses = (b + 1 < nb) & (lax.rem(b + 1, per_seq) > 0)
            reach = jnp.where(next_uses, 0, 4 * ATT_BLK)
            valid = ((ci < ATT_BLK) & (ci >= ki)) | ((ci >= ATT_BLK) & (ki - ci + ATT_BLK >= reach))
            rows = slice(i * ATT_BLK, (i + 1) * ATT_BLK)
            for h in range(ATT_HPG):
                sl = slice(h * ATT_HEAD_DIM, (h + 1) * ATT_HEAD_DIM)
                qcat, docat = pair(qn_ref, qc_ref, i, sl), pair(don_ref, doc_ref, i, sl)
                sc = _dot(k_ref[rows, sl], qcat, "nt") * ATT_SCALE
                p = jnp.exp(jnp.where(valid, sc, NEG_INF) - pair_row(ln_ref, lc_ref, i, h))
                dv_ref[rows, sl] = _dot(p, docat, "nn")
                dp = _dot(v_ref[rows, sl], docat, "nt")
                ds = p * (dp - pair_row(dn_ref, dc_ref, i, h)) * ATT_SCALE
                dk_ref[rows, sl] = _dot(ds, qcat, "nn")

    cur, _, nxt = _attn_specs(nb)
    stat = pl.BlockSpec((8, ATT_ROWS), lambda b: (0, b))
    snxt = pl.BlockSpec((8, ATT_BLK), lambda b: (0, jnp.minimum((b + 1) * ATT_T, nb - 1)))
    return pl.pallas_call(
        body, name="attn_dkv_" + tag, grid=(nb // ATT_T,), in_specs=[cur, cur, cur, nxt, cur, nxt, stat, snxt, stat, snxt],
        out_specs=[cur, cur], out_shape=[jax.ShapeDtypeStruct((s, ATT_GROUPW), F32)] * 2,
        compiler_params=_cparams(1))(k, v, q, q, do, do, lse_t, lse_t, delta_t, delta_t)


def _xattn_probs(q, kh):
    sc = _dot(q, kh, "nt") * XATT_SCALE
    e = jnp.exp(sc - jnp.max(sc, axis=-1, keepdims=True))
    return e / jnp.sum(e, axis=-1, keepdims=True)


def _xattn_fwd(q, kv, tm=512):
    s = q.shape[0]
    tm = min(tm, s)

    def body(q_ref, kv_ref, o_ref):
        for h in range(XATT_HEADS):
            sl = slice(h * XATT_HEAD_DIM, (h + 1) * XATT_HEAD_DIM)
            vs = slice(D_MODEL + h * XATT_HEAD_DIM, D_MODEL + (h + 1) * XATT_HEAD_DIM)
            p = _xattn_probs(q_ref[:, sl], kv_ref[:, sl])
            o_ref[:, sl] = _dot(p, kv_ref[:, vs], "nn").astype(o_ref.dtype)

    return pl.pallas_call(
        body, name="xattn_fwd", grid=(s // tm,),
        in_specs=[pl.BlockSpec((tm, D_MODEL), lambda i: (i, 0)), pl.BlockSpec(kv.shape, lambda i: (0, 0))],
        out_specs=pl.BlockSpec((tm, D_MODEL), lambda i: (i, 0)),
        out_shape=jax.ShapeDtypeStruct((s, D_MODEL), MXU_DTYPE), compiler_params=_cparams(1))(q, kv)


def _xattn_bwd(q, kv, do, tm=512):
    s = q.shape[0]
    tm = min(tm, s)

    def body(q_ref, kv_ref, do_ref, dq_ref, dkv_ref):
        first = pl.program_id(0) == 0

        @pl.when(first)
        def _():
            dkv_ref[...] = jnp.zeros_like(dkv_ref)

        for h in range(XATT_HEADS):
            sl = slice(h * XATT_HEAD_DIM, (h + 1) * XATT_HEAD_DIM)
            vs = slice(D_MODEL + h * XATT_HEAD_DIM, D_MODEL + (h + 1) * XATT_HEAD_DIM)
            p = _xattn_probs(q_ref[:, sl], kv_ref[:, sl])
            dkv_ref[:, vs] += _dot(p, do_ref[:, sl], "tn")
            dp = _dot(do_ref[:, sl], kv_ref[:, vs], "nt")
            ds = p * (dp - jnp.sum(dp * p, axis=-1, keepdims=True)) * XATT_SCALE
            dq_ref[:, sl] = _dot(ds, kv_ref[:, sl], "nn").astype(dq_ref.dtype)
            dkv_ref[:, sl] += _dot(ds, q_ref[:, sl], "tn")

    row = pl.BlockSpec((tm, D_MODEL), lambda i: (i, 0))
    whole = pl.BlockSpec(kv.shape, lambda i: (0, 0))
    return pl.pallas_call(
        body, name="xattn_bwd", grid=(s // tm,), in_specs=[row, whole, row], out_specs=[row, whole],
        out_shape=[jax.ShapeDtypeStruct((s, D_MODEL), MXU_DTYPE), jax.ShapeDtypeStruct(kv.shape, F32)],
        compiler_params=_cparams(1))(q, kv, do)


def _ln(x, g, b):
    mu = jnp.mean(x, axis=-1, keepdims=True)
    xc = x - mu
    var = jnp.mean(jnp.square(xc), axis=-1, keepdims=True)
    return xc * lax.rsqrt(var + LN_EPS) * g + b


def _res_ln(h, o, g, b):
    return _ln(DEEPNORM_ALPHA * h + o, g, b)


def _gate(gs, ga, z1, z2, batt):
    return jax.nn.sigmoid(gs) * (z1 * jax.nn.sigmoid(z2)) + jax.nn.sigmoid(ga) * batt


def _rope_tables(pos, invf, m1, m2):
    ang = pos.astype(F32) * invf
    sin = jnp.sin(ang)
    return jnp.cos(ang), -sin * m1, sin * m2


def _rope(t, cos, s_up, s_dn):
    w = t.shape[-1]
    return t * cos + pltpu.roll(t, w - ROT_DIM // 2, 1) * s_up + pltpu.roll(t, ROT_DIM // 2, 1) * s_dn


def _rope_t(dt, cos, s_up, s_dn):
    w = dt.shape[-1]
    return dt * cos + pltpu.roll(dt * s_up, ROT_DIM // 2, 1) + pltpu.roll(dt * s_dn, w - ROT_DIM // 2, 1)


def _rope_consts():
    inv_freq = ROPE_THETA ** (-jnp.arange(0, ROT_DIM, 2, dtype=F32) / ROT_DIM)
    d = np.arange(ATT_GROUPW) % ATT_HEAD_DIM
    invf = jnp.where(d < ROT_DIM, inv_freq[d % (ROT_DIM // 2)], 0.0).reshape(1, ATT_GROUPW).astype(F32)
    m1 = jnp.asarray((d < ROT_DIM // 2).astype(np.float32)).reshape(1, ATT_GROUPW)
    m2 = jnp.asarray(((d >= ROT_DIM // 2) & (d < ROT_DIM)).astype(np.float32)).reshape(1, ATT_GROUPW)
    return invf, m1, m2


def _head_sum_matrix():
    d = np.arange(ATT_GROUPW) // ATT_HEAD_DIM
    s = np.arange(ATT_STATW) // 128
    return jnp.asarray((d[:, None] == s[None, :]).astype(np.float32))


def _adamw(w, g, m, v):
    m = ADAM_B1 * m + (1.0 - ADAM_B1) * g
    v = ADAM_B2 * v + (1.0 - ADAM_B2) * jnp.square(g)
    m_hat = m / (1.0 - ADAM_B1 ** ADAM_STEP)
    v_hat = v / (1.0 - ADAM_B2 ** ADAM_STEP)
    delta = -ADAM_LR * (m_hat / (jnp.sqrt(v_hat) + ADAM_EPS) + ADAM_WD * w)
    return delta, m, v


def _local_step(x, mem, pos, target, sp, wb):
    s = x.shape[0]
    al = DEEPNORM_ALPHA
    mx = MXU_DTYPE

    h0, h0b = _rowwise("ln_in", lambda x, g, b: (lambda h: (h, h))(_ln(x, g, b)), [x],
                       [sp["ln_in_g"], sp["ln_in_b"]], [(D_MODEL, F32), (D_MODEL, mx)])
    proj = _mm("proj", h0b, wb["w_in"], "nn", bias=sp["b_in"])

    ldt = jnp.repeat(sp["ssm_log_dt"].reshape(SSM_GROUPS), SSM_STATE).reshape(N_STATE, 1)
    are, aim = sp["ssm_a_re"].reshape(N_STATE, 1), sp["ssm_a_im"].reshape(N_STATE, 1)
    bre, bim = sp["ssm_b_re"].reshape(N_STATE, SSM_GROUP), sp["ssm_b_im"].reshape(N_STATE, SSM_GROUP)
    abr, abi, bbr, bbi = _ssm_disc_fwd(ldt, are, aim, bre, bim)
    a_re, a_im = abr.reshape(1, N_STATE), abi.reshape(1, N_STATE)
    bexp = jnp.concatenate([_blockdiag_b(bbr), _blockdiag_b(bbi)], axis=2).astype(mx)
    cexp = jnp.concatenate([_blockdiag_c(sp["ssm_c_re"].reshape(SSM_GROUPS, SSM_GROUP, SSM_STATE)),
                            -_blockdiag_c(sp["ssm_c_im"].reshape(SSM_GROUPS, SSM_GROUP, SSM_STATE))],
                           axis=1).astype(mx)
    u_p = _time_perm(proj[:, :SSM_WIDTH])
    w_re, w_im = _ssm_expand("ssm_bu", u_p, bexp, "nn")
    h_re, h_im = _ssm_scan("ssm_scan_fwd", w_re, w_im, a_re, a_im, reverse=False)
    y_p = _ssm_contract("ssm_ch", h_re, h_im, cexp, "nn", sp["ssm_d"], u_p)
    y = _time_unperm(y_p)
    ygb, = _rowwise("gelu", lambda y: jax.nn.gelu(y), [y], [], [(SSM_WIDTH, mx)])
    z = _mm("glu", ygb, wb["w_glu"], "nn", bias=sp["b_glu"])

    invf, m1, m2 = _rope_consts()

    def rope_fwd(pos, q0, q1, q2, k0, k1, k2, v0, v1, v2, invf, m1, m2):
        tabs = _rope_tables(pos, invf, m1, m2)
        return tuple(_rope(t, *tabs) for t in (q0, q1, q2, k0, k1, k2)) + (v0, v1, v2)

    qkv_cols = [(proj, ATT_GROUPW, 3 + i) for i in range(9)]
    qkv = _rowwise("rope", rope_fwd, [pos] + qkv_cols, [invf, m1, m2], [(ATT_GROUPW, mx)] * 9)
    n_blocks = s // ATT_BLK
    groups = [(str(g), n_blocks // d, d) for g, d in enumerate(DILATIONS)]
    q_d = [_dilate(qkv[g], d) for g, d in enumerate(DILATIONS)]
    k_d = [_dilate(qkv[3 + g], d) for g, d in enumerate(DILATIONS)]
    v_d = [_dilate(qkv[6 + g], d) for g, d in enumerate(DILATIONS)]
    o_g, l_g = [], []
    for g, (tag, per_seq, d) in enumerate(groups):
        o, lse = _attn_fwd(tag, per_seq, q_d[g], k_d[g], v_d[g])
        o_g.append(_undilate(o, d))
        l_g.append(_undilate(lse, d))

    def merge(o0, o1, o2, l0, l1, l2):
        m = jnp.maximum(jnp.maximum(l0, l1), l2)
        e0, e1, e2 = jnp.exp(l0 - m), jnp.exp(l1 - m), jnp.exp(l2 - m)
        tot = e0 + e1 + e2

        def per_dim(e):
            w = e / tot
            return jnp.concatenate([w[:, h * 128:h * 128 + ATT_HEAD_DIM] for h in range(ATT_HPG)], axis=1)

        att = per_dim(e0) * o0 + per_dim(e1) * o1 + per_dim(e2) * o2
        lse = m + jnp.log(tot)
        return att, att, lse, _stat_rows(lse)

    att, attb, lse_tot, lse_tot_t = _rowwise("attn_merge", merge, o_g + l_g, [],
                                             [(ATT_GROUPW, F32), (ATT_GROUPW, mx), (ATT_STATW, F32)], touts=[(8, F32)])
    batt = _mm("att_up", attb, wb["w_att_up"], "nn")

    gate_rows = [(proj, D_MODEL, 3), (proj, D_MODEL, 4), (z, D_MODEL, 0), (z, D_MODEL, 1), batt]
    mixedb, = _rowwise("gate", _gate, gate_rows, [], [(D_MODEL, mx)])
    o1 = _mm("mix_out", mixedb, wb["w_mix_out"], "nn", bias=sp["b_mix_out"])
    h1, h1b = _rowwise("ln1", lambda h, o, g, b: (lambda r: (r, r))(_res_ln(h, o, g, b)), [h0, o1],
                       [sp["ln1_g"], sp["ln1_b"]], [(D_MODEL, F32), (D_MODEL, mx)])

    qx = _mm("xq", h1b, wb["w_xq"], "nn", out_dtypes=(mx,))
    kvx = _mm("xkv", mem, wb["w_xkv"], "nn", out_dtypes=(mx,))
    oxb = _xattn_fwd(qx, kvx)
    o2 = _mm("xo", oxb, wb["w_xo"], "nn")
    h2, h2b = _rowwise("ln2", lambda h, o, g, b: (lambda r: (r, r))(_res_ln(h, o, g, b)), [h1, o2],
                       [sp["ln2_g"], sp["ln2_b"]], [(D_MODEL, F32), (D_MODEL, mx)])

    a_ff, fb = _mm("ff1", h2b, wb["w_ff1"], "nn", bias=sp["b_ff1"],
                   epilogue=lambda r: (r, jnp.square(jnp.maximum(r, 0.0))), out_dtypes=(F32, mx))
    o3 = _mm("ff2", fb, wb["w_ff2"], "nn", bias=sp["b_ff2"])

    def loss_bwd(h2, o3, tgt, g, b):
        def f(h2, o3, g, b):
            h3 = _res_ln(h2, o3, g, b)
            return 0.5 * jnp.sum(jnp.mean(jnp.square(h3 - tgt), axis=-1))

        loss, vjp = jax.vjp(f, h2, o3, g, b)
        _, dr, dg, db = vjp(jnp.ones((), F32))
        return dr, dr, dg, db, _colsum(dr), jnp.full((1, 128), loss, F32)

    dr3, dr3b, g_ln3_g, g_ln3_b, g_b_ff2, loss = _rowwise(
        "loss_ln3_bwd", loss_bwd, [h2, o3, target], [sp["ln3_g"], sp["ln3_b"]],
        [(D_MODEL, F32), (D_MODEL, mx)], [D_MODEL, D_MODEL, D_MODEL, 128])

    dab = _mm("ff2_dx", dr3b, wb["w_ff2"], "nt", extras=(a_ff,),
              epilogue=lambda r, a: (r * (2.0 * jnp.maximum(a, 0.0)),), out_dtypes=(mx,))
    g_w_ff2 = _mm("ff2_dw", fb, dr3b, "tn")
    g_b_ff1, = _rowwise("ff1_db", lambda v: (_colsum(v),), [dab], [], [], [D_FF])
    g_w_ff1 = _mm("ff1_dw", h2b, dab, "tn")
    dh2 = _mm("ff1_dx", dab, wb["w_ff1"], "nt", extras=(dr3,), epilogue=lambda r, d: (r + al * d,))

    def ln_bwd(h, o, dout, g, b):
        _, vjp = jax.vjp(_res_ln, h, o, g, b)
        _, dr, dg, db = vjp(dout)
        return dr, dr, dg, db, _colsum(dr)

    dr2, dr2b, g_ln2_g, g_ln2_b, _ = _rowwise(
        "ln2_bwd", ln_bwd, [h1, o2, dh2], [sp["ln2_g"], sp["ln2_b"]],
        [(D_MODEL, F32), (D_MODEL, mx)], [D_MODEL, D_MODEL, D_MODEL])
    g_w_xo = _mm("xo_dw", oxb, dr2b, "tn")
    doxb = _mm("xo_dx", dr2b, wb["w_xo"], "nt", out_dtypes=(mx,))
    dqxb, dkvx = _xattn_bwd(qx, kvx, doxb)
    g_w_xq = _mm("xq_dw", h1b, dqxb, "tn")
    dh1 = _mm("xq_dx", dqxb, wb["w_xq"], "nt", extras=(dr2,), epilogue=lambda r, d: (r + al * d,))
    g_w_xkv = _mm("xkv_dw", mem, dkvx, "tn")

    dr1, dr1b, g_ln1_g, g_ln1_b, g_b_mix = _rowwise(
        "ln1_bwd", ln_bwd, [h0, o1, dh1], [sp["ln1_g"], sp["ln1_b"]],
        [(D_MODEL, F32), (D_MODEL, mx)], [D_MODEL, D_MODEL, D_MODEL])
    g_w_mix = _mm("mix_dw", mixedb, dr1b, "tn")
    dmixed = _mm("mix_dx", dr1b, wb["w_mix_out"], "nt")

    def gate_bwd(gs, ga, z1, z2, batt, dm):
        _, vjp = jax.vjp(_gate, gs, ga, z1, z2, batt)
        dgs, dga, dz1, dz2, dbatt = vjp(dm)
        dz = jnp.concatenate([dz1, dz2], axis=-1)
        return dgs, dga, dz, dbatt, _colsum(dz)

    dgsb, dgab, dzb, dbattb, g_b_glu = _rowwise(
        "gate_bwd", gate_bwd, gate_rows + [dmixed], [],
        [(D_MODEL, mx), (D_MODEL, mx), (2 * D_MODEL, mx), (D_MODEL, mx)], [2 * D_MODEL])
    g_w_up = _mm("att_up_dw", attb, dbattb, "tn")
    datt = _mm("att_up_dx", dbattb, wb["w_att_up"], "nt")

    def att_delta(datt, att, hs):
        dl = jnp.dot(datt * att, hs, precision=lax.Precision.HIGHEST, preferred_element_type=F32)
        return datt, dl, _stat_rows(dl)

    dattb, delta, delta_t = _rowwise("attn_delta", att_delta, [datt, att], [_head_sum_matrix()],
                                     [(ATT_GROUPW, mx), (ATT_STATW, F32)], touts=[(8, F32)])
    dq_g, dk_g, dv_g = [], [], []
    for g, (tag, per_seq, d) in enumerate(groups):
        do_d, lt_d, dl_d = _dilate(dattb, d), _dilate(lse_tot, d), _dilate(delta, d)
        dq_g.append(_undilate(_attn_dq(tag, per_seq, q_d[g], k_d[g], v_d[g], do_d, lt_d, dl_d), d))
        dk, dv = _attn_dkv(tag, per_seq, q_d[g], k_d[g], v_d[g], do_d, _dilate_rows(lse_tot_t, d), _dilate_rows(delta_t, d))
        dk_g.append(_undilate(dk, d))
        dv_g.append(_undilate(dv, d))
    dqkv = dq_g + dk_g + dv_g

    def rope_bwd(pos, q0, q1, q2, k0, k1, k2, v0, v1, v2, invf, m1, m2):
        tabs = _rope_tables(pos, invf, m1, m2)
        return jnp.concatenate([_rope_t(t, *tabs) for t in (q0, q1, q2, k0, k1, k2)] + [v0, v1, v2], axis=-1)

    dqkvb, = _rowwise("rope_bwd", rope_bwd, [pos] + dqkv, [invf, m1, m2], [(9 * ATT_GROUPW, mx)])

    g_w_glu = _mm("glu_dw", ygb, dzb, "tn")
    dyg = _mm("glu_dx", dzb, wb["w_glu"], "nt")

    def gelu_bwd(y, dyg):
        _, vjp = jax.vjp(jax.nn.gelu, y)
        return vjp(dyg)[0]

    dy, = _rowwise("gelu_bwd", gelu_bwd, [y, dyg], [], [(SSM_WIDTH, F32)])
    dy_p = _time_perm(dy)
    dh_re, dh_im = _ssm_expand("ssm_dh", dy_p, cexp, "nt")
    g_cexp = _ssm_wgrad("ssm_dc", dy_p, h_re, h_im, expand=False)
    s_re, s_im = _ssm_scan("ssm_scan_bwd", dh_re, dh_im, a_re, a_im, reverse=True)
    d_abr, d_abi = _ssm_da(s_re, s_im, h_re, h_im)
    g_bexp = _ssm_wgrad("ssm_db", u_p, s_re, s_im, expand=True)
    du_p = _ssm_contract("ssm_du", s_re, s_im, bexp, "nt", sp["ssm_d"], dy_p)
    g_ssm_d, = _rowwise("ssm_dd", lambda a, b: (_colsum(a * b),), [dy_p, u_p], [], [], [SSM_WIDTH])
    g_ldt, g_are, g_aim, g_bre, g_bim = _ssm_disc_bwd(
        ldt, are, aim, bre, bim, d_abr.reshape(N_STATE, 1), d_abi.reshape(N_STATE, 1),
        _diag_of_b(g_bexp[:, :, :CH_N]), _diag_of_b(g_bexp[:, :, CH_N:]))
    g_c_re = _diag_of_c(g_cexp[:, :CH_N, :])
    g_c_im = -_diag_of_c(g_cexp[:, CH_N:, :])
    dub = _time_unperm(du_p).astype(mx)

    dprojb = jnp.concatenate([dub, dqkvb, dgsb, dgab], axis=-1)
    g_b_in, = _rowwise("in_db", lambda v: (_colsum(v),), [dprojb], [], [], [IN_COLS])
    g_w_in = _mm("in_dw", h0b, dprojb, "tn")
    dh0 = _mm("in_dx", dprojb, wb["w_in"], "nt", extras=(dr1,), epilogue=lambda r, d: (r + al * d,))

    def ln_in_bwd(x, dout, g, b):
        _, vjp = jax.vjp(_ln, x, g, b)
        return vjp(dout)

    dx, g_ln_in_g, g_ln_in_b = _rowwise("ln_in_bwd", ln_in_bwd, [x, dh0], [sp["ln_in_g"], sp["ln_in_b"]],
                                        [(D_MODEL, F32)], [D_MODEL, D_MODEL])

    big = {"w_in": g_w_in, "w_glu": g_w_glu, "w_att_up": g_w_up, "w_mix_out": g_w_mix, "w_xq": g_w_xq,
           "w_xkv": g_w_xkv, "w_xo": g_w_xo, "w_ff1": g_w_ff1, "w_ff2": g_w_ff2}
    small = {"ln_in_g": g_ln_in_g, "ln_in_b": g_ln_in_b, "b_in": g_b_in, "ssm_log_dt": g_ldt, "ssm_a_re": g_are,
             "ssm_a_im": g_aim, "ssm_b_re": g_bre, "ssm_b_im": g_bim, "ssm_c_re": g_c_re, "ssm_c_im": g_c_im,
             "ssm_d": g_ssm_d, "b_glu": g_b_glu, "b_mix_out": g_b_mix, "ln1_g": g_ln1_g, "ln1_b": g_ln1_b,
             "ln2_g": g_ln2_g, "ln2_b": g_ln2_b, "b_ff1": g_b_ff1, "b_ff2": g_b_ff2, "ln3_g": g_ln3_g,
             "ln3_b": g_ln3_b}
    return loss, dx, big, small


def _piece_shape(k, n, axis):
    return (k // 2, n // 4) if axis == 1 else (k // 8, n)


def _aligned(v, m):
    return v if isinstance(v, int) else pl.multiple_of(v, m)


def _full_piece(ref, k, n, axis, chip, half):
    pr, pc = _piece_shape(k, n, axis)
    if axis == 1:
        return ref.at[pl.ds(_aligned(half * pr, 8), pr), pl.ds(_aligned(chip * pc, 128), pc)]
    return ref.at[pl.ds(_aligned(chip * (2 * pr) + half * pr, 8), pr), :]


def _full_shard(ref, k, n, axis, chip):
    if axis == 1:
        return ref.at[:, pl.ds(_aligned(chip * (n // 4), 128), n // 4)]
    return ref.at[pl.ds(_aligned(chip * (k // 4), 8), k // 4), :]


def _shard_piece(ref, k, n, axis, half):
    pr, _ = _piece_shape(k, n, axis)
    return ref.at[pl.ds(_aligned(half * pr, 8), pr), :]


def _mesh_pos():
    x, y, c = lax.axis_index("x"), lax.axis_index("y"), lax.axis_index("c")
    other_chips = [(1 - x, y), (x, 1 - y), (1 - x, 1 - y)]
    return x, y, c, other_chips


def _remote(src, dst, send_sem, recv_sem, dev):
    return pltpu.make_async_remote_copy(src_ref=src, dst_ref=dst, send_sem=send_sem, recv_sem=recv_sem,
                                        device_id=dev, device_id_type=MESH)


def _placed(name, fn, n_steps, where, ins, out_sds, out_block, out_index):
    def body(w_ref, *refs):
        o_ref = refs[-1]
        o_ref[...] = fn(*[r[...] for r in refs[:-1]]).astype(o_ref.dtype)

    grid_spec = pltpu.PrefetchScalarGridSpec(
        num_scalar_prefetch=1, grid=(n_steps,), in_specs=[pl.BlockSpec(bs, idx) for _, bs, idx in ins],
        out_specs=pl.BlockSpec(out_block, out_index))
    return pl.pallas_call(body, name=name, grid_spec=grid_spec, out_shape=out_sds,
                          compiler_params=_cparams(1))(where, *[a for a, _, _ in ins])


def _gather_weights(fulls):
    nw = len(BIG)

    def body(*refs):
        full = refs[nw:2 * nw]
        send_sems, recv_sems = refs[2 * nw:]
        x, y, c, chips = _mesh_pos()
        me = 2 * x + y
        sib = (x, y, 1 - c)
        first, fwd = [], []
        for wi, (_, k, n, ax) in enumerate(BIG):
            mine = _full_piece(full[wi], k, n, ax, me, c)
            for j, (qx, qy) in enumerate(chips):
                cp = _remote(mine, mine, send_sems.at[wi * 6 + j], recv_sems.at[wi * 6 + j], (qx, qy, c))
                cp.start()
                first.append(cp)
        for wi, (_, k, n, ax) in enumerate(BIG):
            for j, (qx, qy) in enumerate(chips):
                piece = _full_piece(full[wi], k, n, ax, 2 * qx + qy, c)
                _remote(piece, piece, send_sems.at[wi * 6 + j], recv_sems.at[wi * 6 + j], (qx, qy, c)).wait_recv()
                cp = _remote(piece, piece, send_sems.at[wi * 6 + 3 + j], recv_sems.at[wi * 6 + 3 + j], sib)
                cp.start()
                fwd.append(cp)
        for wi, (_, k, n, ax) in enumerate(BIG):
            for j, (qx, qy) in enumerate(chips):
                piece = _full_piece(full[wi], k, n, ax, 2 * qx + qy, 1 - c)
                _remote(piece, piece, send_sems.at[wi * 6 + 3 + j], recv_sems.at[wi * 6 + 3 + j], sib).wait_recv()
        for cp in first + fwd:
            cp.wait_send()

    return pl.pallas_call(
        body, name="gather_weights", in_specs=[HBM_SPEC] * nw, out_specs=[HBM_SPEC] * nw,
        out_shape=[jax.ShapeDtypeStruct((k, n), MXU_DTYPE) for _, k, n, _ in BIG],
        input_output_aliases={i: i for i in range(nw)},
        scratch_shapes=[pltpu.SemaphoreType.DMA((6 * nw,)), pltpu.SemaphoreType.DMA((6 * nw,))])(*fulls)


def _reduce_swap_halves(grads):
    nw = len(BIG)

    def body(*refs):
        g, got = refs[:nw], refs[nw:2 * nw]
        send_sems, recv_sems = refs[2 * nw:]
        x, y, c, _ = _mesh_pos()
        sib = (x, y, 1 - c)
        cps = []
        for wi, (_, k, n, ax) in enumerate(BIG):
            for q in range(4):
                cp = _remote(_full_piece(g[wi], k, n, ax, q, 1 - c), got[wi].at[q],
                             send_sems.at[wi * 4 + q], recv_sems.at[wi * 4 + q], sib)
                cp.start()
                cps.append(cp)
        for cp in cps:
            cp.wait()

    return pl.pallas_call(
        body, name="reduce_swap_halves", in_specs=[HBM_SPEC] * nw, out_specs=[HBM_SPEC] * nw,
        out_shape=[jax.ShapeDtypeStruct((4,) + _piece_shape(k, n, ax), F32) for _, k, n, ax in BIG],
        scratch_shapes=[pltpu.SemaphoreType.DMA((4 * nw,)), pltpu.SemaphoreType.DMA((4 * nw,))])(*grads)


def _reduce_to_owner(parts):
    nw = len(BIG)

    def body(*refs):
        p, out = refs[:nw], refs[nw:2 * nw]
        send_sems, recv_sems = refs[2 * nw:]
        x, y, c, chips = _mesh_pos()
        cps = []
        for wi in range(nw):
            for j, (qx, qy) in enumerate(chips):
                cp = _remote(p[wi].at[2 * qx + qy], out[wi].at[j], send_sems.at[wi * 3 + j],
                             recv_sems.at[wi * 3 + j], (qx, qy, c))
                cp.start()
                cps.append(cp)
        for cp in cps:
            cp.wait()

    return pl.pallas_call(
        body, name="reduce_to_owner", in_specs=[HBM_SPEC] * nw, out_specs=[HBM_SPEC] * nw,
        out_shape=[jax.ShapeDtypeStruct((3,) + p.shape[1:], p.dtype) for p in parts],
        scratch_shapes=[pltpu.SemaphoreType.DMA((3 * nw,)), pltpu.SemaphoreType.DMA((3 * nw,))])(*parts)


def _share_with_sibling(shards):
    nw = len(BIG)

    def body(*refs):
        out = refs[nw:2 * nw]
        send_sems, recv_sems = refs[2 * nw:]
        x, y, c, _ = _mesh_pos()
        sib = (x, y, 1 - c)
        cps = []
        for wi, (_, k, n, ax) in enumerate(BIG):
            mine = _shard_piece(out[wi], k, n, ax, c)
            cp = _remote(mine, mine, send_sems.at[wi], recv_sems.at[wi], sib)
            cp.start()
            cps.append(cp)
        for wi, (_, k, n, ax) in enumerate(BIG):
            piece = _shard_piece(out[wi], k, n, ax, 1 - c)
            _remote(piece, piece, send_sems.at[wi], recv_sems.at[wi], sib).wait_recv()
        for cp in cps:
            cp.wait_send()

    return pl.pallas_call(
        body, name="share_with_sibling", in_specs=[HBM_SPEC] * nw, out_specs=[HBM_SPEC] * nw,
        out_shape=[jax.ShapeDtypeStruct(sh.shape, sh.dtype) for sh in shards],
        input_output_aliases={i: i for i in range(nw)},
        scratch_shapes=[pltpu.SemaphoreType.DMA((nw,)), pltpu.SemaphoreType.DMA((nw,))])(*shards)


def _allreduce_small(v):
    r = v.shape[0]
    rh = r // 2
    assert rh % 8 == 0

    def body(v_ref, o_ref, sib_buf, chip_buf, send_sems, recv_sems):
        x, y, c, chips = _mesh_pos()
        me = 2 * x + y
        sib = (x, y, 1 - c)
        mine = pl.ds(pl.multiple_of(c * rh, 8), rh)
        other = pl.ds(pl.multiple_of((1 - c) * rh, 8), rh)
        swap = _remote(v_ref.at[other], sib_buf, send_sems.at[0], recv_sems.at[0], sib)
        swap.start()
        swap.wait()
        chip_buf[me] = v_ref[mine, :] + sib_buf[...]
        cps = []
        for j, (qx, qy) in enumerate(chips):
            cp = _remote(chip_buf.at[me], chip_buf.at[me], send_sems.at[1 + j], recv_sems.at[1 + j], (qx, qy, c))
            cp.start()
            cps.append(cp)
        for j, (qx, qy) in enumerate(chips):
            slot = chip_buf.at[2 * qx + qy]
            _remote(slot, slot, send_sems.at[1 + j], recv_sems.at[1 + j], (qx, qy, c)).wait_recv()
        for cp in cps:
            cp.wait_send()
        o_ref[mine, :] = ((chip_buf[0] + chip_buf[1]) + chip_buf[2]) + chip_buf[3]
        back = _remote(o_ref.at[mine], o_ref.at[mine], send_sems.at[4], recv_sems.at[4], sib)
        back.start()
        _remote(o_ref.at[other], o_ref.at[other], send_sems.at[4], recv_sems.at[4], sib).wait_recv()
        back.wait_send()

    return pl.pallas_call(
        body, name="allreduce_small", in_specs=[VMEM_SPEC], out_specs=VMEM_SPEC,
        out_shape=jax.ShapeDtypeStruct((r, 128), F32),
        scratch_shapes=[pltpu.VMEM((rh, 128), F32), pltpu.VMEM((4, rh, 128), F32),
                        pltpu.SemaphoreType.DMA((5,)), pltpu.SemaphoreType.DMA((5,))],
        compiler_params=pltpu.CompilerParams(vmem_limit_bytes=VMEM_LIMIT))(v)


def _as2d(a):
    a = a.reshape((-1, a.shape[-1])) if a.ndim > 1 else a.reshape(1, -1)
    return a


def _adamw_small(quads):
    n = len(quads)

    def body(*refs):
        for i in range(n):
            w, g, m, v = (r[...] for r in refs[4 * i:4 * i + 4])
            for ref, val in zip(refs[4 * n + 3 * i:4 * n + 3 * i + 3], _adamw(w, g, m, v)):
                ref[...] = val

    return pl.pallas_call(
        body, name="adamw_small", in_specs=[VMEM_SPEC] * (4 * n), out_specs=[VMEM_SPEC] * (3 * n),
        out_shape=[jax.ShapeDtypeStruct(q[0].shape, F32) for q in quads for _ in range(3)],
        compiler_params=pltpu.CompilerParams(vmem_limit_bytes=VMEM_LIMIT))(*[a for q in quads for a in q])


def _where():
    return jnp.stack([2 * lax.axis_index("x") + lax.axis_index("y"), lax.axis_index("c")]).astype(jnp.int32)


def _gather_all(inputs, where):
    fulls = []
    for name, k, n, ax in BIG:
        w2 = inputs[name][0]
        rs, cs = w2.shape
        tm = _tile(rs, 512)
        steps = rs // tm
        if ax == 1:
            blk, idx = (tm, cs), lambda i, w: (i, w[0])
        else:
            blk, idx = (tm, n), functools.partial(lambda i, w, steps: (w[0] * steps + i, 0), steps=steps)
        fulls.append(_placed("cast_" + name, lambda w: w, steps, where, [(w2, (tm, cs), lambda i, w: (i, 0))],
                             jax.ShapeDtypeStruct((k, n), MXU_DTYPE), blk, idx))
    return _gather_weights(fulls)


def _reduce_all(inputs, grads, where):
    got = _reduce_swap_halves(grads)
    parts, geom = [], []
    for i, (name, k, n, ax) in enumerate(BIG):
        pr, pc = _piece_shape(k, n, ax)
        tm = _tile(pr, 512)
        spp = pr // tm
        geom.append((pr, pc, tm, spp))
        if ax == 1:
            g_idx = functools.partial(lambda i, w, spp: (w[1] * spp + i % spp, i // spp), spp=spp)
        else:
            g_idx = functools.partial(lambda i, w, spp: ((i // spp) * 2 * spp + w[1] * spp + i % spp, 0), spp=spp)
        parts.append(_placed("pair_sum_" + name, lambda a, b: a + b, 4 * spp, where,
                             [(grads[i], (tm, pc), g_idx), (got[i].reshape(4 * pr, pc), (tm, pc), lambda i, w: (i, 0))],
                             jax.ShapeDtypeStruct((4 * pr, pc), BF16), (tm, pc), lambda i, w: (i, 0)).reshape(4, pr, pc))
    landed = _reduce_to_owner(parts)
    halves = []
    for i, (name, k, n, ax) in enumerate(BIG):
        pr, pc, tm, spp = geom[i]
        shard_shape = inputs[name].shape[1:]
        ins = [(parts[i], (None, tm, pc), lambda i, w: (w[0], i, 0))]
        ins += [(landed[i], (None, tm, pc), functools.partial(lambda i, w, j: (j, i, 0), j=j)) for j in range(3)]
        halves.append(_placed("chip_sum_" + name,
                              lambda a, b, c, d: ((a.astype(F32) + b.astype(F32)) + c.astype(F32)) + d.astype(F32),
                              spp, where, ins, jax.ShapeDtypeStruct(shard_shape, F32), (tm, pc),
                              functools.partial(lambda i, w, spp: (w[1] * spp + i, 0), spp=spp)))
    return _share_with_sibling(halves)


def _step(inputs):
    x, mem, positions, target = inputs["x"][0], inputs["mem"][0], inputs["positions"], inputs["loss_target"][0]
    pos = positions.reshape(-1, 1)
    where = _where()
    full = _gather_all(inputs, where)
    wb = {name: full[i] for i, (name, _, _, _) in enumerate(BIG)}
    sp = {name: _as2d(inputs[name]) for name in SMALL}
    memb, = _rowwise("cast_mem", lambda m: (m,), [mem], [], [(D_MODEL, MXU_DTYPE)])

    loss, dx, gbig, gsmall = _local_step(x, memb, pos, target, sp, wb)
    gshard = _reduce_all(inputs, [gbig[name] for name, _, _, _ in BIG], where)

    out = {}
    for i, (name, _, _, _) in enumerate(BIG):
        w2, m2, v2 = inputs[name][0], inputs["m_" + name][0], inputs["v_" + name][0]
        n = w2.shape[1]
        d, nm, nv = _rowwise("adamw_" + name, _adamw, [w2, gshard[i], m2, v2], [], [(n, F32)] * 3, tm=128)
        lead = inputs[name].shape
        out[name] = (gshard[i].reshape(lead), d.reshape(lead), nm.reshape(lead), nv.reshape(lead))

    def tiles(a):
        flat = a.reshape(-1)
        n = -(-flat.shape[0] // 1024) * 1024
        return jnp.pad(flat, (0, n - flat.shape[0])).reshape(n // 128, 128)

    pieces = [tiles(loss[:, :1])] + [tiles(gsmall[name]) for name in SMALL]
    if sum(p.shape[0] for p in pieces) % 16:
        pieces.append(jnp.zeros((8, 128), F32))
    red = _allreduce_small(jnp.concatenate(pieces, axis=0))
    loss_total = red[0, 0]
    grads, off = {}, pieces[0].shape[0]
    for name, p in zip(SMALL, pieces[1:]):
        shp = _as2d(inputs[name]).shape
        grads[name] = red[off:off + p.shape[0]].reshape(-1)[:shp[0] * shp[1]].reshape(shp)
        off += p.shape[0]
    upd = _adamw_small([(_as2d(inputs[n]), grads[n], _as2d(inputs["m_" + n]), _as2d(inputs["v_" + n])) for n in SMALL])
    for i, name in enumerate(SMALL):
        shp = inputs[name].shape
        out[name] = (grads[name].reshape(shp),) + tuple(t.reshape(shp) for t in upd[3 * i:3 * i + 3])
    return loss_total, dx.reshape(inputs["x"].shape), out


_ARG_NAMES = (("x", "mem", "positions") + WEIGHT_ORDER + ("loss_target",) + tuple("m_" + n for n in WEIGHT_ORDER)
              + tuple("v_" + n for n in WEIGHT_ORDER))


def kernel(x, mem, positions, ln_in_g, ln_in_b, w_in, b_in, ssm_log_dt, ssm_a_re, ssm_a_im, ssm_b_re, ssm_b_im, ssm_c_re, ssm_c_im, ssm_d, w_glu, b_glu, w_att_up, w_mix_out, b_mix_out, ln1_g, ln1_b, w_xq, w_xkv, w_xo, ln2_g, ln2_b, w_ff1, b_ff1, w_ff2, b_ff2, ln3_g, ln3_b, loss_target, m_ln_in_g, m_ln_in_b, m_w_in, m_b_in, m_ssm_log_dt, m_ssm_a_re, m_ssm_a_im, m_ssm_b_re, m_ssm_b_im, m_ssm_c_re, m_ssm_c_im, m_ssm_d, m_w_glu, m_b_glu, m_w_att_up, m_w_mix_out, m_b_mix_out, m_ln1_g, m_ln1_b, m_w_xq, m_w_xkv, m_w_xo, m_ln2_g, m_ln2_b, m_w_ff1, m_b_ff1, m_w_ff2, m_b_ff2, m_ln3_g, m_ln3_b, v_ln_in_g, v_ln_in_b, v_w_in, v_b_in, v_ssm_log_dt, v_ssm_a_re, v_ssm_a_im, v_ssm_b_re, v_ssm_b_im, v_ssm_c_re, v_ssm_c_im, v_ssm_d, v_w_glu, v_b_glu, v_w_att_up, v_w_mix_out, v_b_mix_out, v_ln1_g, v_ln1_b, v_w_xq, v_w_xkv, v_w_xo, v_ln2_g, v_ln2_b, v_w_ff1, v_b_ff1, v_w_ff2, v_b_ff2, v_ln3_g, v_ln3_b):
    args = (x, mem, positions, ln_in_g, ln_in_b, w_in, b_in, ssm_log_dt, ssm_a_re, ssm_a_im, ssm_b_re, ssm_b_im, ssm_c_re, ssm_c_im, ssm_d, w_glu, b_glu, w_att_up, w_mix_out, b_mix_out, ln1_g, ln1_b, w_xq, w_xkv, w_xo, ln2_g, ln2_b, w_ff1, b_ff1, w_ff2, b_ff2, ln3_g, ln3_b, loss_target, m_ln_in_g, m_ln_in_b, m_w_in, m_b_in, m_ssm_log_dt, m_ssm_a_re, m_ssm_a_im, m_ssm_b_re, m_ssm_b_im, m_ssm_c_re, m_ssm_c_im, m_ssm_d, m_w_glu, m_b_glu, m_w_att_up, m_w_mix_out, m_b_mix_out, m_ln1_g, m_ln1_b, m_w_xq, m_w_xkv, m_w_xo, m_ln2_g, m_ln2_b, m_w_ff1, m_b_ff1, m_w_ff2, m_b_ff2, m_ln3_g, m_ln3_b, v_ln_in_g, v_ln_in_b, v_w_in, v_b_in, v_ssm_log_dt, v_ssm_a_re, v_ssm_a_im, v_ssm_b_re, v_ssm_b_im, v_ssm_c_re, v_ssm_c_im, v_ssm_d, v_w_glu, v_b_glu, v_w_att_up, v_w_mix_out, v_b_mix_out, v_ln1_g, v_ln1_b, v_w_xq, v_w_xkv, v_w_xo, v_ln2_g, v_ln2_b, v_w_ff1, v_b_ff1, v_w_ff2, v_b_ff2, v_ln3_g, v_ln3_b)
    assert len(args) == len(_ARG_NAMES)
    inputs = dict(zip(_ARG_NAMES, args))
    loss, dx, out = _step(inputs)
    res = [loss, dx]
    for k in range(4):
        res += [out[name][k] for name in WEIGHT_ORDER]
    return tuple(res)
```

```python
import functools
import math

import numpy as np
import jax
import jax.numpy as jnp
from jax import lax
from jax.experimental import pallas as pl
from jax.experimental.pallas import tpu as pltpu

F32 = jnp.float32
BF16 = jnp.bfloat16
MXU_DTYPE = jnp.bfloat16

D_MODEL = 1024
SSM_GROUP = 16
SSM_WIDTH = 768
SSM_GROUPS = 48
SSM_STATE = 64
N_STATE = SSM_GROUPS * SSM_STATE
SSM_CHUNKS = 6
CH_W = 128
CH_N = 512
ATT_HEAD_DIM = 64
ATT_HPG = 4
ATT_GROUPW = ATT_HPG * ATT_HEAD_DIM
DILATIONS = (1, 4, 16)
ATT_BLK = 128
ATT_SCALE = ATT_HEAD_DIM ** -0.5
ROT_DIM = 16
ROPE_THETA = 500000.0
XATT_HEADS = 4
XATT_HEAD_DIM = 256
XATT_SCALE = XATT_HEAD_DIM ** -0.5
D_FF = 4096
IN_COLS = 5120
DEEPNORM_ALPHA = 2.0 ** 0.25
LN_EPS = 1e-5
NEG_INF = -1e30
ADAM_LR = 0.001
ADAM_B1 = 0.9
ADAM_B2 = 0.999
ADAM_EPS = 1e-08
ADAM_WD = 0.01
ADAM_STEP = 10

N_SEG = 32
VMEM_LIMIT = 48 * 1024 * 1024
MESH = pl.DeviceIdType.MESH
HBM_SPEC = pl.BlockSpec(memory_space=pltpu.HBM)
VMEM_SPEC = pl.BlockSpec(memory_space=pltpu.VMEM)

BIG = (("w_in", 1024, 5120, 1), ("w_glu", 768, 2048, 1), ("w_att_up", 256, 1024, 1),
       ("w_mix_out", 1024, 1024, 0), ("w_xq", 1024, 1024, 0), ("w_xkv", 1024, 2048, 1),
       ("w_xo", 1024, 1024, 0), ("w_ff1", 1024, 4096, 1), ("w_ff2", 4096, 1024, 0))
SMALL = ("ln_in_g", "ln_in_b", "b_in", "ssm_log_dt", "ssm_a_re", "ssm_a_im", "ssm_b_re", "ssm_b_im",
         "ssm_c_re", "ssm_c_im", "ssm_d", "b_glu", "b_mix_out", "ln1_g", "ln1_b", "ln2_g", "ln2_b",
         "b_ff1", "b_ff2", "ln3_g", "ln3_b")
WEIGHT_ORDER = ("ln_in_g", "ln_in_b", "w_in", "b_in", "ssm_log_dt", "ssm_a_re", "ssm_a_im", "ssm_b_re",
                "ssm_b_im", "ssm_c_re", "ssm_c_im", "ssm_d", "w_glu", "b_glu", "w_att_up", "w_mix_out",
                "b_mix_out", "ln1_g", "ln1_b", "w_xq", "w_xkv", "w_xo", "ln2_g", "ln2_b", "w_ff1", "b_ff1",
                "w_ff2", "b_ff2", "ln3_g", "ln3_b")


def _cparams(n_axes):
    return pltpu.CompilerParams(dimension_semantics=("arbitrary",) * n_axes, vmem_limit_bytes=VMEM_LIMIT)


class _Carry:
    def __init__(self, ins, outs, n_sems, start, finish, done):
        self.ins, self.outs, self.n_sems, self.start, self.finish, self.done = ins, outs, n_sems, start, finish, done


def _call(name, body, grid, in_specs, out_specs, out_shape, args, scratch_shapes=(), carry=None):
    in_specs, out_specs, out_shape = list(in_specs), list(out_specs), list(out_shape)
    params = _cparams(len(grid))
    if carry is None:
        return pl.pallas_call(body, name=name, grid=grid, in_specs=in_specs, out_specs=out_specs, out_shape=out_shape,
                              scratch_shapes=list(scratch_shapes), compiler_params=params)(*args)
    n_in, n_out, n_ci, n_co = len(in_specs), len(out_specs), len(carry.ins), len(carry.outs)
    n_scr = len(scratch_shapes)

    def wrapped(*refs):
        ins, c_in = refs[:n_in], refs[n_in:n_in + n_ci]
        outs, c_out = refs[n_in + n_ci:n_in + n_ci + n_out], refs[n_in + n_ci + n_out:n_in + n_ci + n_out + n_co]
        scratch = refs[n_in + n_ci + n_out + n_co:n_in + n_ci + n_out + n_co + n_scr]
        send_sems, recv_sems = refs[-2:]
        ids = [pl.program_id(a) for a in range(len(grid))]
        first = functools.reduce(jnp.logical_and, [i == 0 for i in ids])
        last = functools.reduce(jnp.logical_and, [i == g - 1 for i, g in zip(ids, grid)])

        @pl.when(first)
        def _():
            carry.start(c_in, c_out, send_sems, recv_sems)

        body(*ins, *outs, *scratch)

        @pl.when(last)
        def _():
            carry.finish(c_in, c_out, send_sems, recv_sems)

    c_shapes = [jax.ShapeDtypeStruct(carry.ins[o].shape, carry.ins[o].dtype) if isinstance(o, int) else o
                for o in carry.outs]
    aliases = {n_in + o: n_out + i for i, o in enumerate(carry.outs) if isinstance(o, int)}
    res = pl.pallas_call(
        wrapped, name=name, grid=grid, in_specs=in_specs + [HBM_SPEC] * n_ci, out_specs=out_specs + [HBM_SPEC] * n_co,
        out_shape=out_shape + c_shapes, input_output_aliases=aliases,
        scratch_shapes=list(scratch_shapes) + [pltpu.SemaphoreType.DMA((carry.n_sems,))] * 2,
        compiler_params=params)(*args, *carry.ins)
    carry.done(res[n_out:])
    return res[:n_out]


def _rowwise(name, fn, rows, consts, outs, reds=(), tm=256, touts=()):
    n_rows = (rows[0][0] if isinstance(rows[0], tuple) else rows[0]).shape[-2]
    tm = min(tm, n_rows)
    assert n_rows % tm == 0, (name, n_rows, tm)
    specs, args = [], []
    for r in rows:
        if isinstance(r, tuple) and len(r) == 3:
            arr, width, cb = r
            specs.append(pl.BlockSpec((tm, width), functools.partial(lambda i, cb: (i, cb), cb=cb)))
        elif isinstance(r, tuple):
            arr, slot = r
            specs.append(pl.BlockSpec((None, tm, arr.shape[2]), functools.partial(lambda i, s: (s, i, 0), s=slot)))
        else:
            arr = r
            specs.append(pl.BlockSpec((tm, arr.shape[1]), lambda i: (i, 0)))
        args.append(arr)
        assert arr.shape[-2] == n_rows, (name, arr.shape, n_rows)
    for cst in consts:
        specs.append(pl.BlockSpec(cst.shape, lambda i: (0, 0)))
        args.append(cst)
    n_r, n_c, n_o, n_d = len(rows), len(consts), len(outs) + len(touts), len(reds)
    out_shape = [jax.ShapeDtypeStruct((n_rows, c), dt) for c, dt in outs]
    out_specs = [pl.BlockSpec((tm, c), lambda i: (i, 0)) for c, _ in outs]
    out_shape += [jax.ShapeDtypeStruct((r, n_rows), dt) for r, dt in touts]
    out_specs += [pl.BlockSpec((r, tm), lambda i: (0, i)) for r, _ in touts]
    out_shape += [jax.ShapeDtypeStruct((1, c), F32) for c in reds]
    out_specs += [pl.BlockSpec((1, c), lambda i: (0, 0)) for c in reds]

    def body(*refs):
        ins = [r[...] for r in refs[:n_r + n_c]]
        o_refs = refs[n_r + n_c:n_r + n_c + n_o]
        d_refs = refs[n_r + n_c + n_o:]
        res = fn(*ins)
        res = res if isinstance(res, (tuple, list)) else (res,)
        assert len(res) == n_o + n_d, (name, len(res))
        for ref, val in zip(o_refs, res[:n_o]):
            ref[...] = val.astype(ref.dtype)
        first = pl.program_id(0) == 0
        for ref, val in zip(d_refs, res[n_o:]):
            @pl.when(first)
            def _(ref=ref, val=val):
                ref[...] = val

            @pl.when(jnp.logical_not(first))
            def _(ref=ref, val=val):
                ref[...] += val

    res = pl.pallas_call(body, name=name, grid=(n_rows // tm,), in_specs=specs, out_specs=out_specs,
                         out_shape=out_shape, compiler_params=_cparams(1))(*args)
    return res


def _colsum(v):
    return jnp.sum(v.astype(F32), axis=0, keepdims=True)


_DIMS = {"nn": (((1,), (0,)), ((), ())), "nt": (((1,), (1,)), ((), ())), "tn": (((0,), (0,)), ((), ()))}


def _tile(dim, want):
    if dim <= want:
        return dim
    return max(t for t in range(128, want + 1, 128) if dim % t == 0)


def _dot(a, b, mode):
    return lax.dot_general(a.astype(MXU_DTYPE), b.astype(MXU_DTYPE), _DIMS[mode], preferred_element_type=F32)


def _mm(name, a, b, mode, *, bias=None, extras=(), epilogue=None, out_dtypes=(F32,), tm=1024, tn=1024, tk=1024,
        carry=None):
    if mode == "nn":
        (m, k), (_, n) = a.shape, b.shape
    elif mode == "nt":
        (m, k), (n, _) = a.shape, b.shape
    else:
        (k, m), (_, n) = a.shape, b.shape
    tm, tn = _tile(m, tm), _tile(n, tn)
    if mode != "tn":
        tk = k if k <= 1024 else tk
    tk = _tile(k, tk)
    assert m % tm == 0 and n % tn == 0 and k % tk == 0, (name, m, n, k)
    nk = k // tk
    a_spec = {"nn": pl.BlockSpec((tm, tk), lambda i, j, kk: (i, kk)),
              "nt": pl.BlockSpec((tm, tk), lambda i, j, kk: (i, kk)),
              "tn": pl.BlockSpec((tk, tm), lambda i, j, kk: (kk, i))}[mode]
    b_spec = {"nn": pl.BlockSpec((tk, tn), lambda i, j, kk: (kk, j)),
              "nt": pl.BlockSpec((tn, tk), lambda i, j, kk: (j, kk)),
              "tn": pl.BlockSpec((tk, tn), lambda i, j, kk: (kk, j))}[mode]
    specs, args = [a_spec, b_spec], [a, b]
    if bias is not None:
        specs.append(pl.BlockSpec((1, tn), lambda i, j, kk: (0, j)))
        args.append(bias)
    for e in extras:
        specs.append(pl.BlockSpec((tm, tn), lambda i, j, kk: (i, j)))
        args.append(e)
    n_e, n_o = len(extras), len(out_dtypes)
    has_bias = bias is not None

    def body(*refs):
        a_ref, b_ref = refs[0], refs[1]
        pos = 2
        bias_ref = refs[pos] if has_bias else None
        pos += int(has_bias)
        e_refs = refs[pos:pos + n_e]
        o_refs = refs[pos + n_e:pos + n_e + n_o]
        acc_ref = refs[pos + n_e + n_o] if nk > 1 else None
        part = _dot(a_ref[...], b_ref[...], mode)

        def finish(r):
            if has_bias:
                r = r + bias_ref[...]
            res = epilogue(r, *[e[...] for e in e_refs]) if epilogue is not None else (r,)
            for ref, val in zip(o_refs, res):
                ref[...] = val.astype(ref.dtype)

        if nk == 1:
            finish(part)
        else:
            kk = pl.program_id(2)

            @pl.when(kk == 0)
            def _():
                acc_ref[...] = part

            @pl.when(kk > 0)
            def _():
                acc_ref[...] += part

            @pl.when(kk == nk - 1)
            def _():
                finish(acc_ref[...])

    res = _call(name, body, (m // tm, n // tn, nk), specs,
                [pl.BlockSpec((tm, tn), lambda i, j, kk: (i, j)) for _ in out_dtypes],
                [jax.ShapeDtypeStruct((m, n), dt) for dt in out_dtypes], args,
                scratch_shapes=[pltpu.VMEM((tm, tn), F32)] if nk > 1 else [], carry=carry)
    return res[0] if n_o == 1 else res


def _ssm_expand(name, a, bmat, mode, tm=512):
    s = a.shape[0]
    tm = min(tm, s)

    def body(a_ref, b_ref, re_ref, im_ref):
        r = _dot(a_ref[...], b_ref[...], mode)
        re_ref[...] = r[:, :CH_N]
        im_ref[...] = r[:, CH_N:]

    return pl.pallas_call(
        body, name=name, grid=(s // tm, SSM_CHUNKS),
        in_specs=[pl.BlockSpec((tm, CH_W), lambda i, j: (i, j)),
                  pl.BlockSpec((None,) + bmat.shape[1:], lambda i, j: (j, 0, 0))],
        out_specs=[pl.BlockSpec((tm, CH_N), lambda i, j: (i, j))] * 2,
        out_shape=[jax.ShapeDtypeStruct((s, N_STATE), F32)] * 2,
        compiler_params=_cparams(2))(a, bmat)


def _ssm_contract(name, a_re, a_im, bmat, mode, d_row, extra, tm=512, carry=None):
    s = a_re.shape[0]
    tm = min(tm, s)

    def body(re_ref, im_ref, b_ref, d_ref, e_ref, o_ref):
        b = b_ref[...]
        if mode == "nn":
            r = _dot(re_ref[...], b[:CH_N], "nn") + _dot(im_ref[...], b[CH_N:], "nn")
        else:
            r = _dot(re_ref[...], b[:, :CH_N], "nt") + _dot(im_ref[...], b[:, CH_N:], "nt")
        o_ref[...] = r + d_ref[...] * e_ref[...]

    return _call(
        name, body, (s // tm, SSM_CHUNKS),
        [pl.BlockSpec((tm, CH_N), lambda i, j: (i, j)), pl.BlockSpec((tm, CH_N), lambda i, j: (i, j)),
         pl.BlockSpec((None,) + bmat.shape[1:], lambda i, j: (j, 0, 0)),
         pl.BlockSpec((1, CH_W), lambda i, j: (0, j)), pl.BlockSpec((tm, CH_W), lambda i, j: (i, j))],
        [pl.BlockSpec((tm, CH_W), lambda i, j: (i, j))], [jax.ShapeDtypeStruct((s, SSM_WIDTH), F32)],
        (a_re, a_im, bmat, d_row, extra), carry=carry)[0]


def _ssm_wgrad(name, chan, st_re, st_im, expand, tk=512):
    s = chan.shape[0]
    tk = min(tk, s)
    nk = s // tk
    oshape = (CH_W, 2 * CH_N) if expand else (2 * CH_N, CH_W)

    def body(c_ref, re_ref, im_ref, o_ref):
        c = c_ref[...]
        if expand:
            part = jnp.concatenate([_dot(c, re_ref[...], "tn"), _dot(c, im_ref[...], "tn")], axis=1)
        else:
            part = jnp.concatenate([_dot(re_ref[...], c, "tn"), _dot(im_ref[...], c, "tn")], axis=0)
        kk = pl.program_id(1)

        @pl.when(kk == 0)
        def _():
            o_ref[...] = part

        @pl.when(kk > 0)
        def _():
            o_ref[...] += part

    return pl.pallas_call(
        body, name=name, grid=(SSM_CHUNKS, nk),
        in_specs=[pl.BlockSpec((tk, CH_W), lambda j, kk: (kk, j)), pl.BlockSpec((tk, CH_N), lambda j, kk: (kk, j)),
                  pl.BlockSpec((tk, CH_N), lambda j, kk: (kk, j))],
        out_specs=pl.BlockSpec((None,) + oshape, lambda j, kk: (j, 0, 0)),
        out_shape=jax.ShapeDtypeStruct((SSM_CHUNKS,) + oshape, F32),
        compiler_params=_cparams(2))(chan, st_re, st_im)


SCAN_LB = 256


def _ssm_scan(name, w_re, w_im, a_re, a_im, reverse, carry=None):
    s = w_re.shape[0]
    seg_len = s // N_SEG
    n_sq = int(math.log2(seg_len))
    assert 2 ** n_sq == seg_len

    def body(are_ref, aim_ref, wre_ref, wim_ref, hre_ref, him_ref, ere, eim, cre, cim):
        ar1 = are_ref[...]
        ai1 = -aim_ref[...] if reverse else aim_ref[...]
        ar = jnp.broadcast_to(ar1, (N_SEG, SCAN_LB))
        ai = jnp.broadcast_to(ai1, (N_SEG, SCAN_LB))

        def rows_of(k):
            kk = seg_len - 1 - k if reverse else k
            return pl.ds(pl.multiple_of(kk * N_SEG, N_SEG), N_SEG)

        def local(k, carry):
            hr, hi = carry
            rows = rows_of(k)
            nr = ar * hr - ai * hi + wre_ref[rows, :]
            ni = ar * hi + ai * hr + wim_ref[rows, :]
            hre_ref[rows, :] = nr
            him_ref[rows, :] = ni
            return nr, ni

        zero = jnp.zeros((N_SEG, SCAN_LB), F32)
        er, ei = lax.fori_loop(0, seg_len, local, (zero, zero))
        ere[...] = er
        eim[...] = ei
        pr, pi = ar1, ai1
        for _ in range(n_sq):
            pr, pi = pr * pr - pi * pi, 2.0 * pr * pi
        cr = jnp.zeros((1, SCAN_LB), F32)
        ci = jnp.zeros((1, SCAN_LB), F32)
        for jj in range(N_SEG):
            j = N_SEG - 1 - jj if reverse else jj
            cre[j:j + 1, :] = cr
            cim[j:j + 1, :] = ci
            er_j, ei_j = ere[j:j + 1, :], eim[j:j + 1, :]
            cr, ci = pr * cr - pi * ci + er_j, pr * ci + pi * cr + ei_j
        c_r, c_i = cre[...], cim[...]

        def fix(k, carry):
            qr, qi = carry
            rows = rows_of(k)
            hre_ref[rows, :] = hre_ref[rows, :] + (qr * c_r - qi * c_i)
            him_ref[rows, :] = him_ref[rows, :] + (qr * c_i + qi * c_r)
            return qr * ar - qi * ai, qr * ai + qi * ar

        lax.fori_loop(0, seg_len, fix, (ar, ai))

    nblk = N_STATE // SCAN_LB
    blk = pl.BlockSpec((s, SCAN_LB), lambda b: (0, b))
    row = pl.BlockSpec((1, SCAN_LB), lambda b: (0, b))
    return _call(name, body, (nblk,), [row, row, blk, blk], [blk, blk], [jax.ShapeDtypeStruct((s, N_STATE), F32)] * 2,
                 (a_re, a_im, w_re, w_im), scratch_shapes=[pltpu.VMEM((N_SEG, SCAN_LB), F32)] * 4, carry=carry)


def _ssm_da(g_re, g_im, h_re, h_im):
    s = g_re.shape[0]
    seg_len = s // N_SEG

    def body(gre_ref, gim_ref, hre_ref, him_ref, dre_ref, dim_ref):
        def rows_of(k):
            return pl.ds(pl.multiple_of(k * N_SEG, N_SEG), N_SEG)

        def step(k, carry):
            sr, si = carry
            gr, gi = gre_ref[rows_of(k), :], gim_ref[rows_of(k), :]
            pr, pi = hre_ref[rows_of(k - 1), :], him_ref[rows_of(k - 1), :]
            return sr + gr * pr + gi * pi, si + gi * pr - gr * pi

        zero = jnp.zeros((N_SEG, SCAN_LB), F32)
        sr, si = lax.fori_loop(1, seg_len, step, (zero, zero))
        last = pl.ds((seg_len - 1) * N_SEG, N_SEG)
        first_row = lax.broadcasted_iota(jnp.int32, (N_SEG, SCAN_LB), 0) == 0
        pr = jnp.where(first_row, 0.0, pltpu.roll(hre_ref[last, :], 1, 0))
        pi = jnp.where(first_row, 0.0, pltpu.roll(him_ref[last, :], 1, 0))
        gr, gi = gre_ref[pl.ds(0, N_SEG), :], gim_ref[pl.ds(0, N_SEG), :]
        sr = sr + gr * pr + gi * pi
        si = si + gi * pr - gr * pi
        dre_ref[...] = jnp.sum(sr, axis=0, keepdims=True)
        dim_ref[...] = jnp.sum(si, axis=0, keepdims=True)

    nblk = N_STATE // SCAN_LB
    blk = pl.BlockSpec((s, SCAN_LB), lambda b: (0, b))
    row = pl.BlockSpec((1, SCAN_LB), lambda b: (0, b))
    return pl.pallas_call(
        body, name="ssm_da", grid=(nblk,), in_specs=[blk] * 4, out_specs=[row, row],
        out_shape=[jax.ShapeDtypeStruct((1, N_STATE), F32)] * 2,
        compiler_params=_cparams(1))(g_re, g_im, h_re, h_im)


def _disc(ldt, are, aim, bre, bim):
    dt = jnp.exp(ldt)
    mag = jnp.exp(are * dt)
    abr = mag * jnp.cos(aim * dt)
    abi = mag * jnp.sin(aim * dt)
    den = jnp.square(are) + jnp.square(aim)
    nr = abr - 1.0
    fre = (nr * are + abi * aim) / den
    fim = (abi * are - nr * aim) / den
    return abr, abi, fre * bre - fim * bim, fre * bim + fim * bre


def _ssm_disc_fwd(ldt, are, aim, bre, bim):
    def body(l_ref, ar_ref, ai_ref, br_ref, bi_ref, o0, o1, o2, o3):
        res = _disc(l_ref[...], ar_ref[...], ai_ref[...], br_ref[...], bi_ref[...])
        for ref, val in zip((o0, o1, o2, o3), res):
            ref[...] = val

    col = jax.ShapeDtypeStruct((N_STATE, 1), F32)
    mat = jax.ShapeDtypeStruct((N_STATE, SSM_GROUP), F32)
    return pl.pallas_call(body, name="ssm_disc_fwd", out_shape=[col, col, mat, mat],
                          in_specs=[VMEM_SPEC] * 5, out_specs=[VMEM_SPEC] * 4)(ldt, are, aim, bre, bim)


def _ssm_disc_bwd(ldt, are, aim, bre, bim, d_abr, d_abi, d_bbr, d_bbi):
    def body(l_ref, ar_ref, ai_ref, br_ref, bi_ref, c0, c1, c2, c3, g_ldt, g_are, g_aim, g_bre, g_bim):
        _, vjp = jax.vjp(_disc, l_ref[...], ar_ref[...], ai_ref[...], br_ref[...], bi_ref[...])
        dl, dar, dai, dbr, dbi = vjp((c0[...], c1[...], c2[...], c3[...]))
        state = lax.broadcasted_iota(jnp.int32, (N_STATE, SSM_GROUPS), 0)
        group = lax.broadcasted_iota(jnp.int32, (N_STATE, SSM_GROUPS), 1)
        pick = jnp.right_shift(state, 6) == group
        g_ldt[...] = jnp.sum(jnp.where(pick, dl, 0.0), axis=0, keepdims=True)
        g_are[...] = dar
        g_aim[...] = dai
        g_bre[...] = dbr
        g_bim[...] = dbi

    col = jax.ShapeDtypeStruct((N_STATE, 1), F32)
    mat = jax.ShapeDtypeStruct((N_STATE, SSM_GROUP), F32)
    return pl.pallas_call(body, name="ssm_disc_bwd",
                          out_shape=[jax.ShapeDtypeStruct((1, SSM_GROUPS), F32), col, col, mat, mat],
                          in_specs=[VMEM_SPEC] * 9, out_specs=[VMEM_SPEC] * 5,
                          compiler_params=pltpu.CompilerParams(vmem_limit_bytes=VMEM_LIMIT))(
        ldt, are, aim, bre, bim, d_abr, d_abi, d_bbr, d_bbi)


_EYE8 = np.eye(8, dtype=np.float32)


def _blockdiag_b(bb):
    t = bb.reshape(SSM_CHUNKS, 8, SSM_STATE, SSM_GROUP).transpose(0, 1, 3, 2)
    return jnp.einsum("igcn,gh->igchn", t, _EYE8).reshape(SSM_CHUNKS, CH_W, CH_N)


def _diag_of_b(m):
    t = jnp.einsum("igchn,gh->igcn", m.reshape(SSM_CHUNKS, 8, SSM_GROUP, 8, SSM_STATE), _EYE8)
    return t.transpose(0, 1, 3, 2).reshape(N_STATE, SSM_GROUP)


def _blockdiag_c(c):
    t = c.reshape(SSM_CHUNKS, 8, SSM_GROUP, SSM_STATE).transpose(0, 1, 3, 2)
    return jnp.einsum("ignc,gh->ignhc", t, _EYE8).reshape(SSM_CHUNKS, CH_N, CH_W)


def _diag_of_c(m):
    t = jnp.einsum("ignhc,gh->ignc", m.reshape(SSM_CHUNKS, 8, SSM_STATE, 8, SSM_GROUP), _EYE8)
    return t.transpose(0, 1, 3, 2).reshape(SSM_GROUPS, SSM_GROUP, SSM_STATE)


def _time_perm(a):
    s, c = a.shape
    return a.reshape(N_SEG, s // N_SEG, c).transpose(1, 0, 2).reshape(s, c)


def _time_unperm(a):
    s, c = a.shape
    return a.reshape(s // N_SEG, N_SEG, c).transpose(1, 0, 2).reshape(s, c)


def _dilate(a, d):
    s, c = a.shape
    return a if d == 1 else a.reshape(s // d, d, c).transpose(1, 0, 2).reshape(s, c)


def _undilate(a, d):
    s, c = a.shape
    return a if d == 1 else a.reshape(d, s // d, c).transpose(1, 0, 2).reshape(s, c)


def _dilate_rows(a, d):
    r, s = a.shape
    return a if d == 1 else a.reshape(r, s // d, d).transpose(0, 2, 1).reshape(r, s)


ATT_T = 4
ATT_ROWS = ATT_T * ATT_BLK


def _window(prev_ref, cur_ref, i, sl):
    if i == 0:
        return jnp.concatenate([prev_ref[:, sl], cur_ref[0:ATT_BLK, sl]], axis=0)
    return cur_ref[(i - 1) * ATT_BLK:(i + 1) * ATT_BLK, sl]


def _band_valid(first_key):
    qi = lax.broadcasted_iota(jnp.int32, (ATT_BLK, 2 * ATT_BLK), 0)
    ki = lax.broadcasted_iota(jnp.int32, (ATT_BLK, 2 * ATT_BLK), 1)
    steps = qi + ATT_BLK - ki
    return (steps >= 0) & (steps <= ATT_BLK) & (ki >= first_key)


ATT_STATW = ATT_HPG * 128


def _stat(h):
    return slice(h * 128, (h + 1) * 128)


def _stat_rows(stat):
    n = stat.shape[0]
    heads = [stat[:, _stat(h)].T[0:1, :] for h in range(ATT_HPG)]
    return jnp.concatenate(heads + [jnp.zeros((8 - ATT_HPG, n), stat.dtype)], axis=0)


def _attn_specs(nb, width=ATT_GROUPW):
    cur = pl.BlockSpec((ATT_ROWS, width), lambda b: (b, 0))
    prev = pl.BlockSpec((ATT_BLK, width), lambda b: (jnp.maximum(b * ATT_T - 1, 0), 0))
    nxt = pl.BlockSpec((ATT_BLK, width), lambda b: (jnp.minimum((b + 1) * ATT_T, nb - 1), 0))
    return cur, prev, nxt


def _attn_fwd(tag, per_seq, q, k, v):
    s = q.shape[0]
    nb = s // ATT_BLK

    def body(q_ref, kc_ref, kp_ref, vc_ref, vp_ref, o_ref, lse_ref):
        bt = pl.program_id(0)
        for i in range(ATT_T):
            has_prev = lax.rem(bt * ATT_T + i, per_seq) > 0
            valid = _band_valid(jnp.where(has_prev, 0, ATT_BLK))
            rows = slice(i * ATT_BLK, (i + 1) * ATT_BLK)
            for h in range(ATT_HPG):
                sl = slice(h * ATT_HEAD_DIM, (h + 1) * ATT_HEAD_DIM)
                kcat = _window(kp_ref, kc_ref, i, sl)
                vcat = _window(vp_ref, vc_ref, i, sl)
                sc = _dot(q_ref[rows, sl], kcat, "nt") * ATT_SCALE
                sc = jnp.where(valid, sc, NEG_INF)
                m = jnp.max(sc, axis=-1, keepdims=True)
                p = jnp.exp(sc - m)
                den = jnp.sum(p, axis=-1, keepdims=True)
                o_ref[rows, sl] = _dot(p, vcat, "nn") / den
                lse_ref[rows, _stat(h)] = jnp.broadcast_to(m + jnp.log(den), (ATT_BLK, 128))

    cur, prev, _ = _attn_specs(nb)
    stat, _, _ = _attn_specs(nb, ATT_STATW)
    return pl.pallas_call(
        body, name="attn_fwd_" + tag, grid=(nb // ATT_T,), in_specs=[cur, cur, prev, cur, prev], out_specs=[cur, stat],
        out_shape=[jax.ShapeDtypeStruct((s, ATT_GROUPW), F32), jax.ShapeDtypeStruct((s, ATT_STATW), F32)],
        compiler_params=_cparams(1))(q, k, k, v, v)


def _attn_dq(tag, per_seq, q, k, v, do, lse, delta):
    s = q.shape[0]
    nb = s // ATT_BLK

    def body(q_ref, kc_ref, kp_ref, vc_ref, vp_ref, do_ref, lse_ref, dl_ref, dq_ref):
        bt = pl.program_id(0)
        for i in range(ATT_T):
            has_prev = lax.rem(bt * ATT_T + i, per_seq) > 0
            valid = _band_valid(jnp.where(has_prev, 0, ATT_BLK))
            rows = slice(i * ATT_BLK, (i + 1) * ATT_BLK)
            for h in range(ATT_HPG):
                sl = slice(h * ATT_HEAD_DIM, (h + 1) * ATT_HEAD_DIM)
                kcat = _window(kp_ref, kc_ref, i, sl)
                vcat = _window(vp_ref, vc_ref, i, sl)
                lse = jnp.concatenate([lse_ref[rows, _stat(h)]] * 2, axis=1)
                dlt = jnp.concatenate([dl_ref[rows, _stat(h)]] * 2, axis=1)
                sc = _dot(q_ref[rows, sl], kcat, "nt") * ATT_SCALE
                p = jnp.exp(jnp.where(valid, sc, NEG_INF) - lse)
                dp = _dot(do_ref[rows, sl], vcat, "nt")
                ds = p * (dp - dlt) * ATT_SCALE
                dq_ref[rows, sl] = _dot(ds, kcat, "nn")

    cur, prev, _ = _attn_specs(nb)
    stat, _, _ = _attn_specs(nb, ATT_STATW)
    return pl.pallas_call(
        body, name="attn_dq_" + tag, grid=(nb // ATT_T,), in_specs=[cur, cur, prev, cur, prev, cur, stat, stat],
        out_specs=cur, out_shape=jax.ShapeDtypeStruct((s, ATT_GROUPW), F32),
        compiler_params=_cparams(1))(q, k, k, v, v, do, lse, delta)


def _attn_dkv(tag, per_seq, q, k, v, do, lse_t, delta_t):
    s = q.shape[0]
    nb = s // ATT_BLK

    def body(k_ref, v_ref, qc_ref, qn_ref, doc_ref, don_ref, lc_ref, ln_ref, dc_ref, dn_ref, dk_ref, dv_ref):
        bt = pl.program_id(0)
        ki = lax.broadcasted_iota(jnp.int32, (ATT_BLK, 2 * ATT_BLK), 0)
        ci = lax.broadcasted_iota(jnp.int32, (ATT_BLK, 2 * ATT_BLK), 1)

        def pair(edge_ref, cur_ref, i, sl):
            if i == ATT_T - 1:
                return jnp.concatenate([cur_ref[i * ATT_BLK:(i + 1) * ATT_BLK, sl], edge_ref[:, sl]], axis=0)
            return cur_ref[i * ATT_BLK:(i + 2) * ATT_BLK, sl]

        def pair_row(edge_ref, cur_ref, i, h):
            if i == ATT_T - 1:
                row = jnp.concatenate([cur_ref[h:h + 1, i * ATT_BLK:(i + 1) * ATT_BLK], edge_ref[h:h + 1, :]], axis=1)
            else:
                row = cur_ref[h:h + 1, i * ATT_BLK:(i + 2) * ATT_BLK]
            return jnp.broadcast_to(row, (ATT_BLK, 2 * ATT_BLK))

        for i in range(ATT_T):
            b = bt * ATT_T + i
            next_uses = (b + 1 < nb) & (lax.rem(b + 1, per_seq) > 0)
            reach = jnp.where(next_uses, 0, 4 * ATT_BLK)
            valid = ((ci < ATT_BLK) & (ci >= ki)) | ((ci >= ATT_BLK) & (ki - ci + ATT_BLK >= reach))
            rows = slice(i * ATT_BLK, (i + 1) * ATT_BLK)
            for h in range(ATT_HPG):
                sl = slice(h * ATT_HEAD_DIM, (h + 1) * ATT_HEAD_DIM)
                qcat, docat = pair(qn_ref, qc_ref, i, sl), pair(don_ref, doc_ref, i, sl)
                sc = _dot(k_ref[rows, sl], qcat, "nt") * ATT_SCALE
                p = jnp.exp(jnp.where(valid, sc, NEG_INF) - pair_row(ln_ref, lc_ref, i, h))
                dv_ref[rows, sl] = _dot(p, docat, "nn")
                dp = _dot(v_ref[rows, sl], docat, "nt")
                ds = p * (dp - pair_row(dn_ref, dc_ref, i, h)) * ATT_SCALE
                dk_ref[rows, sl] = _dot(ds, qcat, "nn")

    cur, _, nxt = _attn_specs(nb)
    stat = pl.BlockSpec((8, ATT_ROWS), lambda b: (0, b))
    snxt = pl.BlockSpec((8, ATT_BLK), lambda b: (0, jnp.minimum((b + 1) * ATT_T, nb - 1)))
    return pl.pallas_call(
        body, name="attn_dkv_" + tag, grid=(nb // ATT_T,), in_specs=[cur, cur, cur, nxt, cur, nxt, stat, snxt, stat, snxt],
        out_specs=[cur, cur], out_shape=[jax.ShapeDtypeStruct((s, ATT_GROUPW), F32)] * 2,
        compiler_params=_cparams(1))(k, v, q, q, do, do, lse_t, lse_t, delta_t, delta_t)


def _xattn_probs(q, kh):
    sc = _dot(q, kh, "nt") * XATT_SCALE
    e = jnp.exp(sc - jnp.max(sc, axis=-1, keepdims=True))
    return e / jnp.sum(e, axis=-1, keepdims=True)


def _xattn_fwd(q, kv, tm=512):
    s = q.shape[0]
    tm = min(tm, s)

    def body(q_ref, kv_ref, o_ref):
        for h in range(XATT_HEADS):
            sl = slice(h * XATT_HEAD_DIM, (h + 1) * XATT_HEAD_DIM)
            vs = slice(D_MODEL + h * XATT_HEAD_DIM, D_MODEL + (h + 1) * XATT_HEAD_DIM)
            p = _xattn_probs(q_ref[:, sl], kv_ref[:, sl])
            o_ref[:, sl] = _dot(p, kv_ref[:, vs], "nn").astype(o_ref.dtype)

    return pl.pallas_call(
        body, name="xattn_fwd", grid=(s // tm,),
        in_specs=[pl.BlockSpec((tm, D_MODEL), lambda i: (i, 0)), pl.BlockSpec(kv.shape, lambda i: (0, 0))],
        out_specs=pl.BlockSpec((tm, D_MODEL), lambda i: (i, 0)),
        out_shape=jax.ShapeDtypeStruct((s, D_MODEL), MXU_DTYPE), compiler_params=_cparams(1))(q, kv)


def _xattn_bwd(q, kv, do, tm=512):
    s = q.shape[0]
    tm = min(tm, s)

    def body(q_ref, kv_ref, do_ref, dq_ref, dkv_ref):
        first = pl.program_id(0) == 0

        @pl.when(first)
        def _():
            dkv_ref[...] = jnp.zeros_like(dkv_ref)

        for h in range(XATT_HEADS):
            sl = slice(h * XATT_HEAD_DIM, (h + 1) * XATT_HEAD_DIM)
            vs = slice(D_MODEL + h * XATT_HEAD_DIM, D_MODEL + (h + 1) * XATT_HEAD_DIM)
            p = _xattn_probs(q_ref[:, sl], kv_ref[:, sl])
            dkv_ref[:, vs] += _dot(p, do_ref[:, sl], "tn")
            dp = _dot(do_ref[:, sl], kv_ref[:, vs], "nt")
            ds = p * (dp - jnp.sum(dp * p, axis=-1, keepdims=True)) * XATT_SCALE
            dq_ref[:, sl] = _dot(ds, kv_ref[:, sl], "nn").astype(dq_ref.dtype)
            dkv_ref[:, sl] += _dot(ds, q_ref[:, sl], "tn")

    row = pl.BlockSpec((tm, D_MODEL), lambda i: (i, 0))
    whole = pl.BlockSpec(kv.shape, lambda i: (0, 0))
    return pl.pallas_call(
        body, name="xattn_bwd", grid=(s // tm,), in_specs=[row, whole, row], out_specs=[row, whole],
        out_shape=[jax.ShapeDtypeStruct((s, D_MODEL), MXU_DTYPE), jax.ShapeDtypeStruct(kv.shape, F32)],
        compiler_params=_cparams(1))(q, kv, do)


def _ln(x, g, b):
    mu = jnp.mean(x, axis=-1, keepdims=True)
    xc = x - mu
    var = jnp.mean(jnp.square(xc), axis=-1, keepdims=True)
    return xc * lax.rsqrt(var + LN_EPS) * g + b


def _res_ln(h, o, g, b):
    return _ln(DEEPNORM_ALPHA * h + o, g, b)


def _gate(gs, ga, z1, z2, batt):
    return jax.nn.sigmoid(gs) * (z1 * jax.nn.sigmoid(z2)) + jax.nn.sigmoid(ga) * batt


def _rope_tables(pos, invf, m1, m2):
    ang = pos.astype(F32) * invf
    sin = jnp.sin(ang)
    return jnp.cos(ang), -sin * m1, sin * m2


def _rope(t, cos, s_up, s_dn):
    w = t.shape[-1]
    return t * cos + pltpu.roll(t, w - ROT_DIM // 2, 1) * s_up + pltpu.roll(t, ROT_DIM // 2, 1) * s_dn


def _rope_t(dt, cos, s_up, s_dn):
    w = dt.shape[-1]
    return dt * cos + pltpu.roll(dt * s_up, ROT_DIM // 2, 1) + pltpu.roll(dt * s_dn, w - ROT_DIM // 2, 1)


def _rope_consts():
    inv_freq = ROPE_THETA ** (-jnp.arange(0, ROT_DIM, 2, dtype=F32) / ROT_DIM)
    d = np.arange(ATT_GROUPW) % ATT_HEAD_DIM
    invf = jnp.where(d < ROT_DIM, inv_freq[d % (ROT_DIM // 2)], 0.0).reshape(1, ATT_GROUPW).astype(F32)
    m1 = jnp.asarray((d < ROT_DIM // 2).astype(np.float32)).reshape(1, ATT_GROUPW)
    m2 = jnp.asarray(((d >= ROT_DIM // 2) & (d < ROT_DIM)).astype(np.float32)).reshape(1, ATT_GROUPW)
    return invf, m1, m2


def _head_sum_matrix():
    d = np.arange(ATT_GROUPW) // ATT_HEAD_DIM
    s = np.arange(ATT_STATW) // 128
    return jnp.asarray((d[:, None] == s[None, :]).astype(np.float32))


def _adamw(w, g, m, v):
    m = ADAM_B1 * m + (1.0 - ADAM_B1) * g
    v = ADAM_B2 * v + (1.0 - ADAM_B2) * jnp.square(g)
    m_hat = m / (1.0 - ADAM_B1 ** ADAM_STEP)
    v_hat = v / (1.0 - ADAM_B2 ** ADAM_STEP)
    delta = -ADAM_LR * (m_hat / (jnp.sqrt(v_hat) + ADAM_EPS) + ADAM_WD * w)
    return delta, m, v


def _local_step(x, mem, pos, target, sp, ex):
    s = x.shape[0]
    al = DEEPNORM_ALPHA
    mx = MXU_DTYPE

    ex.gather_now(["w_in"])
    h0, h0b = _rowwise("ln_in", lambda x, g, b: (lambda h: (h, h))(_ln(x, g, b)), [x],
                       [sp["ln_in_g"], sp["ln_in_b"]], [(D_MODEL, F32), (D_MODEL, mx)])
    proj = _mm("proj", h0b, ex.weight("w_in"), "nn", bias=sp["b_in"],
               carry=ex.gather_carry(["w_glu", "w_att_up", "w_mix_out", "w_xq", "w_xkv"]))

    ldt = jnp.repeat(sp["ssm_log_dt"].reshape(SSM_GROUPS), SSM_STATE).reshape(N_STATE, 1)
    are, aim = sp["ssm_a_re"].reshape(N_STATE, 1), sp["ssm_a_im"].reshape(N_STATE, 1)
    bre, bim = sp["ssm_b_re"].reshape(N_STATE, SSM_GROUP), sp["ssm_b_im"].reshape(N_STATE, SSM_GROUP)
    abr, abi, bbr, bbi = _ssm_disc_fwd(ldt, are, aim, bre, bim)
    a_re, a_im = abr.reshape(1, N_STATE), abi.reshape(1, N_STATE)
    bexp = jnp.concatenate([_blockdiag_b(bbr), _blockdiag_b(bbi)], axis=2).astype(mx)
    cexp = jnp.concatenate([_blockdiag_c(sp["ssm_c_re"].reshape(SSM_GROUPS, SSM_GROUP, SSM_STATE)),
                            -_blockdiag_c(sp["ssm_c_im"].reshape(SSM_GROUPS, SSM_GROUP, SSM_STATE))],
                           axis=1).astype(mx)
    u_p = _time_perm(proj[:, :SSM_WIDTH])
    w_re, w_im = _ssm_expand("ssm_bu", u_p, bexp, "nn")
    h_re, h_im = _ssm_scan("ssm_scan_fwd", w_re, w_im, a_re, a_im, reverse=False, carry=ex.gather_carry(["w_ff2"]))
    y_p = _ssm_contract("ssm_ch", h_re, h_im, cexp, "nn", sp["ssm_d"], u_p, carry=ex.gather_carry(["w_ff1"]))
    y = _time_unperm(y_p)
    ygb, = _rowwise("gelu", lambda y: jax.nn.gelu(y), [y], [], [(SSM_WIDTH, mx)])
    z = _mm("glu", ygb, ex.weight("w_glu"), "nn", bias=sp["b_glu"], carry=ex.gather_carry(["w_xo"]))

    invf, m1, m2 = _rope_consts()

    def rope_fwd(pos, q0, q1, q2, k0, k1, k2, v0, v1, v2, invf, m1, m2):
        tabs = _rope_tables(pos, invf, m1, m2)
        return tuple(_rope(t, *tabs) for t in (q0, q1, q2, k0, k1, k2)) + (v0, v1, v2)

    qkv_cols = [(proj, ATT_GROUPW, 3 + i) for i in range(9)]
    qkv = _rowwise("rope", rope_fwd, [pos] + qkv_cols, [invf, m1, m2], [(ATT_GROUPW, mx)] * 9)
    n_blocks = s // ATT_BLK
    groups = [(str(g), n_blocks // d, d) for g, d in enumerate(DILATIONS)]
    q_d = [_dilate(qkv[g], d) for g, d in enumerate(DILATIONS)]
    k_d = [_dilate(qkv[3 + g], d) for g, d in enumerate(DILATIONS)]
    v_d = [_dilate(qkv[6 + g], d) for g, d in enumerate(DILATIONS)]
    o_g, l_g = [], []
    for g, (tag, per_seq, d) in enumerate(groups):
        o, lse = _attn_fwd(tag, per_seq, q_d[g], k_d[g], v_d[g])
        o_g.append(_undilate(o, d))
        l_g.append(_undilate(lse, d))

    def merge(o0, o1, o2, l0, l1, l2):
        m = jnp.maximum(jnp.maximum(l0, l1), l2)
        e0, e1, e2 = jnp.exp(l0 - m), jnp.exp(l1 - m), jnp.exp(l2 - m)
        tot = e0 + e1 + e2

        def per_dim(e):
            w = e / tot
            return jnp.concatenate([w[:, h * 128:h * 128 + ATT_HEAD_DIM] for h in range(ATT_HPG)], axis=1)

        att = per_dim(e0) * o0 + per_dim(e1) * o1 + per_dim(e2) * o2
        lse = m + jnp.log(tot)
        return att, att, lse, _stat_rows(lse)

    att, attb, lse_tot, lse_tot_t = _rowwise("attn_merge", merge, o_g + l_g, [],
                                             [(ATT_GROUPW, F32), (ATT_GROUPW, mx), (ATT_STATW, F32)], touts=[(8, F32)])
    batt = _mm("att_up", attb, ex.weight("w_att_up"), "nn")

    gate_rows = [(proj, D_MODEL, 3), (proj, D_MODEL, 4), (z, D_MODEL, 0), (z, D_MODEL, 1), batt]
    mixedb, = _rowwise("gate", _gate, gate_rows, [], [(D_MODEL, mx)])
    o1 = _mm("mix_out", mixedb, ex.weight("w_mix_out"), "nn", bias=sp["b_mix_out"])
    h1, h1b = _rowwise("ln1", lambda h, o, g, b: (lambda r: (r, r))(_res_ln(h, o, g, b)), [h0, o1],
                       [sp["ln1_g"], sp["ln1_b"]], [(D_MODEL, F32), (D_MODEL, mx)])

    qx = _mm("xq", h1b, ex.weight("w_xq"), "nn", out_dtypes=(mx,))
    kvx = _mm("xkv", mem, ex.weight("w_xkv"), "nn", out_dtypes=(mx,))
    oxb = _xattn_fwd(qx, kvx)
    o2 = _mm("xo", oxb, ex.weight("w_xo"), "nn")
    h2, h2b = _rowwise("ln2", lambda h, o, g, b: (lambda r: (r, r))(_res_ln(h, o, g, b)), [h1, o2],
                       [sp["ln2_g"], sp["ln2_b"]], [(D_MODEL, F32), (D_MODEL, mx)])

    a_ff, fb = _mm("ff1", h2b, ex.weight("w_ff1"), "nn", bias=sp["b_ff1"],
                   epilogue=lambda r: (r, jnp.square(jnp.maximum(r, 0.0))), out_dtypes=(F32, mx))
    o3 = _mm("ff2", fb, ex.weight("w_ff2"), "nn", bias=sp["b_ff2"])

    def loss_bwd(h2, o3, tgt, g, b):
        def f(h2, o3, g, b):
            h3 = _res_ln(h2, o3, g, b)
            return 0.5 * jnp.sum(jnp.mean(jnp.square(h3 - tgt), axis=-1))

        loss, vjp = jax.vjp(f, h2, o3, g, b)
        _, dr, dg, db = vjp(jnp.ones((), F32))
        return dr, dr, dg, db, _colsum(dr), jnp.full((1, 128), loss, F32)

    dr3, dr3b, g_ln3_g, g_ln3_b, g_b_ff2, loss = _rowwise(
        "loss_ln3_bwd", loss_bwd, [h2, o3, target], [sp["ln3_g"], sp["ln3_b"]],
        [(D_MODEL, F32), (D_MODEL, mx)], [D_MODEL, D_MODEL, D_MODEL, 128])

    dab = _mm("ff2_dx", dr3b, ex.weight("w_ff2"), "nt", extras=(a_ff,),
              epilogue=lambda r, a: (r * (2.0 * jnp.maximum(a, 0.0)),), out_dtypes=(mx,))
    ex.grad("w_ff2", _mm("ff2_dw", fb, dr3b, "tn"))
    g_b_ff1, = _rowwise("ff1_db", lambda v: (_colsum(v),), [dab], [], [], [D_FF])
    ex.grad("w_ff1", _mm("ff1_dw", h2b, dab, "tn", carry=ex.reduce_carry(["w_ff2"])))
    dh2 = _mm("ff1_dx", dab, ex.weight("w_ff1"), "nt", extras=(dr3,), epilogue=lambda r, d: (r + al * d,),
              carry=ex.reduce_carry(["w_ff1"]))

    def ln_bwd(h, o, dout, g, b):
        _, vjp = jax.vjp(_res_ln, h, o, g, b)
        _, dr, dg, db = vjp(dout)
        return dr, dr, dg, db, _colsum(dr)

    dr2, dr2b, g_ln2_g, g_ln2_b, _ = _rowwise(
        "ln2_bwd", ln_bwd, [h1, o2, dh2], [sp["ln2_g"], sp["ln2_b"]],
        [(D_MODEL, F32), (D_MODEL, mx)], [D_MODEL, D_MODEL, D_MODEL])
    ex.grad("w_xo", _mm("xo_dw", oxb, dr2b, "tn"))
    doxb = _mm("xo_dx", dr2b, ex.weight("w_xo"), "nt", out_dtypes=(mx,), carry=ex.reduce_carry(["w_xo"]))
    dqxb, dkvx = _xattn_bwd(qx, kvx, doxb)
    ex.grad("w_xq", _mm("xq_dw", h1b, dqxb, "tn"))
    dh1 = _mm("xq_dx", dqxb, ex.weight("w_xq"), "nt", extras=(dr2,), epilogue=lambda r, d: (r + al * d,))
    ex.grad("w_xkv", _mm("xkv_dw", mem, dkvx, "tn"))

    dr1, dr1b, g_ln1_g, g_ln1_b, g_b_mix = _rowwise(
        "ln1_bwd", ln_bwd, [h0, o1, dh1], [sp["ln1_g"], sp["ln1_b"]],
        [(D_MODEL, F32), (D_MODEL, mx)], [D_MODEL, D_MODEL, D_MODEL])
    ex.grad("w_mix_out", _mm("mix_dw", mixedb, dr1b, "tn", carry=ex.reduce_carry(["w_xq"])))
    dmixed = _mm("mix_dx", dr1b, ex.weight("w_mix_out"), "nt", carry=ex.reduce_carry(["w_xkv"]))

    def gate_bwd(gs, ga, z1, z2, batt, dm):
        _, vjp = jax.vjp(_gate, gs, ga, z1, z2, batt)
        dgs, dga, dz1, dz2, dbatt = vjp(dm)
        dz = jnp.concatenate([dz1, dz2], axis=-1)
        return dgs, dga, dz, dbatt, _colsum(dz)

    dgsb, dgab, dzb, dbattb, g_b_glu = _rowwise(
        "gate_bwd", gate_bwd, gate_rows + [dmixed], [],
        [(D_MODEL, mx), (D_MODEL, mx), (2 * D_MODEL, mx), (D_MODEL, mx)], [2 * D_MODEL])
    ex.grad("w_att_up", _mm("att_up_dw", attb, dbattb, "tn", carry=ex.reduce_carry(["w_mix_out"])))
    datt = _mm("att_up_dx", dbattb, ex.weight("w_att_up"), "nt")

    def att_delta(datt, att, hs):
        dl = jnp.dot(datt * att, hs, precision=lax.Precision.HIGHEST, preferred_element_type=F32)
        return datt, dl, _stat_rows(dl)

    dattb, delta, delta_t = _rowwise("attn_delta", att_delta, [datt, att], [_head_sum_matrix()],
                                     [(ATT_GROUPW, mx), (ATT_STATW, F32)], touts=[(8, F32)])
    dq_g, dk_g, dv_g = [], [], []
    for g, (tag, per_seq, d) in enumerate(groups):
        do_d, lt_d, dl_d = _dilate(dattb, d), _dilate(lse_tot, d), _dilate(delta, d)
        dq_g.append(_undilate(_attn_dq(tag, per_seq, q_d[g], k_d[g], v_d[g], do_d, lt_d, dl_d), d))
        dk, dv = _attn_dkv(tag, per_seq, q_d[g], k_d[g], v_d[g], do_d, _dilate_rows(lse_tot_t, d), _dilate_rows(delta_t, d))
        dk_g.append(_undilate(dk, d))
        dv_g.append(_undilate(dv, d))
    dqkv = dq_g + dk_g + dv_g

    def rope_bwd(pos, q0, q1, q2, k0, k1, k2, v0, v1, v2, invf, m1, m2):
        tabs = _rope_tables(pos, invf, m1, m2)
        return jnp.concatenate([_rope_t(t, *tabs) for t in (q0, q1, q2, k0, k1, k2)] + [v0, v1, v2], axis=-1)

    dqkvb, = _rowwise("rope_bwd", rope_bwd, [pos] + dqkv, [invf, m1, m2], [(9 * ATT_GROUPW, mx)])

    ex.grad("w_glu", _mm("glu_dw", ygb, dzb, "tn", carry=ex.reduce_carry(["w_att_up"])))
    dyg = _mm("glu_dx", dzb, ex.weight("w_glu"), "nt")

    def gelu_bwd(y, dyg):
        _, vjp = jax.vjp(jax.nn.gelu, y)
        return vjp(dyg)[0]

    dy, = _rowwise("gelu_bwd", gelu_bwd, [y, dyg], [], [(SSM_WIDTH, F32)])
    dy_p = _time_perm(dy)
    dh_re, dh_im = _ssm_expand("ssm_dh", dy_p, cexp, "nt")
    g_cexp = _ssm_wgrad("ssm_dc", dy_p, h_re, h_im, expand=False)
    s_re, s_im = _ssm_scan("ssm_scan_bwd", dh_re, dh_im, a_re, a_im, reverse=True, carry=ex.reduce_carry(["w_glu"]))
    d_abr, d_abi = _ssm_da(s_re, s_im, h_re, h_im)
    g_bexp = _ssm_wgrad("ssm_db", u_p, s_re, s_im, expand=True)
    du_p = _ssm_contract("ssm_du", s_re, s_im, bexp, "nt", sp["ssm_d"], dy_p)
    g_ssm_d, = _rowwise("ssm_dd", lambda a, b: (_colsum(a * b),), [dy_p, u_p], [], [], [SSM_WIDTH])
    g_ldt, g_are, g_aim, g_bre, g_bim = _ssm_disc_bwd(
        ldt, are, aim, bre, bim, d_abr.reshape(N_STATE, 1), d_abi.reshape(N_STATE, 1),
        _diag_of_b(g_bexp[:, :, :CH_N]), _diag_of_b(g_bexp[:, :, CH_N:]))
    g_c_re = _diag_of_c(g_cexp[:, :CH_N, :])
    g_c_im = -_diag_of_c(g_cexp[:, CH_N:, :])
    dub = _time_unperm(du_p).astype(mx)

    dprojb = jnp.concatenate([dub, dqkvb, dgsb, dgab], axis=-1)
    g_b_in, = _rowwise("in_db", lambda v: (_colsum(v),), [dprojb], [], [], [IN_COLS])
    ex.grad("w_in", _mm("in_dw", h0b, dprojb, "tn"))
    dh0 = _mm("in_dx", dprojb, ex.weight("w_in"), "nt", extras=(dr1,), epilogue=lambda r, d: (r + al * d,),
              carry=ex.reduce_carry(["w_in"]))

    def ln_in_bwd(x, dout, g, b):
        _, vjp = jax.vjp(_ln, x, g, b)
        return vjp(dout)

    dx, g_ln_in_g, g_ln_in_b = _rowwise("ln_in_bwd", ln_in_bwd, [x, dh0], [sp["ln_in_g"], sp["ln_in_b"]],
                                        [(D_MODEL, F32)], [D_MODEL, D_MODEL])

    small = {"ln_in_g": g_ln_in_g, "ln_in_b": g_ln_in_b, "b_in": g_b_in, "ssm_log_dt": g_ldt, "ssm_a_re": g_are,
             "ssm_a_im": g_aim, "ssm_b_re": g_bre, "ssm_b_im": g_bim, "ssm_c_re": g_c_re, "ssm_c_im": g_c_im,
             "ssm_d": g_ssm_d, "b_glu": g_b_glu, "b_mix_out": g_b_mix, "ln1_g": g_ln1_g, "ln1_b": g_ln1_b,
             "ln2_g": g_ln2_g, "ln2_b": g_ln2_b, "b_ff1": g_b_ff1, "b_ff2": g_b_ff2, "ln3_g": g_ln3_g,
             "ln3_b": g_ln3_b}
    return loss, dx, small


def _piece_shape(k, n, axis):
    return (k // 2, n // 4) if axis == 1 else (k // 8, n)


def _aligned(v, m):
    return v if isinstance(v, int) else pl.multiple_of(v, m)


def _full_piece(ref, k, n, axis, chip, half):
    pr, pc = _piece_shape(k, n, axis)
    if axis == 1:
        return ref.at[pl.ds(_aligned(half * pr, 8), pr), pl.ds(_aligned(chip * pc, 128), pc)]
    return ref.at[pl.ds(_aligned(chip * (2 * pr) + half * pr, 8), pr), :]


def _full_shard(ref, k, n, axis, chip):
    if axis == 1:
        return ref.at[:, pl.ds(_aligned(chip * (n // 4), 128), n // 4)]
    return ref.at[pl.ds(_aligned(chip * (k // 4), 8), k // 4), :]


def _shard_piece(ref, k, n, axis, half):
    pr, _ = _piece_shape(k, n, axis)
    return ref.at[pl.ds(_aligned(half * pr, 8), pr), :]


def _mesh_pos():
    x, y, c = lax.axis_index("x"), lax.axis_index("y"), lax.axis_index("c")
    other_chips = [(1 - x, y), (x, 1 - y), (1 - x, 1 - y)]
    return x, y, c, other_chips


def _remote(src, dst, send_sem, recv_sem, dev):
    return pltpu.make_async_remote_copy(src_ref=src, dst_ref=dst, send_sem=send_sem, recv_sem=recv_sem,
                                        device_id=dev, device_id_type=MESH)


def _placed(name, fn, n_steps, where, ins, out_sds, out_block, out_index):
    def body(w_ref, *refs):
        o_ref = refs[-1]
        o_ref[...] = fn(*[r[...] for r in refs[:-1]]).astype(o_ref.dtype)

    grid_spec = pltpu.PrefetchScalarGridSpec(
        num_scalar_prefetch=1, grid=(n_steps,), in_specs=[pl.BlockSpec(bs, idx) for _, bs, idx in ins],
        out_specs=pl.BlockSpec(out_block, out_index))
    return pl.pallas_call(body, name=name, grid_spec=grid_spec, out_shape=out_sds,
                          compiler_params=_cparams(1))(where, *[a for a, _, _ in ins])


def _gather_copies(widx):
    geo = [BIG[i][1:] for i in widx]

    def ici(full, wi, j, chip, send_sems, recv_sems, c, dev):
        k, n, ax = geo[wi]
        piece = _full_piece(full[wi], k, n, ax, chip, c)
        return _remote(piece, piece, send_sems.at[wi * 6 + j], recv_sems.at[wi * 6 + j], dev)

    def d2d(full, wi, j, chip, half, send_sems, recv_sems, sib):
        k, n, ax = geo[wi]
        piece = _full_piece(full[wi], k, n, ax, chip, half)
        return _remote(piece, piece, send_sems.at[wi * 6 + 3 + j], recv_sems.at[wi * 6 + 3 + j], sib)

    def start(_, full, send_sems, recv_sems):
        x, y, c, chips = _mesh_pos()
        for wi in range(len(geo)):
            for j, (qx, qy) in enumerate(chips):
                ici(full, wi, j, 2 * x + y, send_sems, recv_sems, c, (qx, qy, c)).start()

    def finish(_, full, send_sems, recv_sems):
        x, y, c, chips = _mesh_pos()
        sib = (x, y, 1 - c)
        for wi in range(len(geo)):
            for j, (qx, qy) in enumerate(chips):
                ici(full, wi, j, 2 * qx + qy, send_sems, recv_sems, c, (qx, qy, c)).wait_recv()
                d2d(full, wi, j, 2 * qx + qy, c, send_sems, recv_sems, sib).start()
        for wi in range(len(geo)):
            for j, (qx, qy) in enumerate(chips):
                d2d(full, wi, j, 2 * qx + qy, 1 - c, send_sems, recv_sems, sib).wait_recv()
        for wi in range(len(geo)):
            for j, (qx, qy) in enumerate(chips):
                ici(full, wi, j, 2 * x + y, send_sems, recv_sems, c, (qx, qy, c)).wait_send()
                d2d(full, wi, j, 2 * qx + qy, c, send_sems, recv_sems, sib).wait_send()

    return start, finish, 6 * len(geo)


def _gather_weights(tag, fulls, widx):
    nw = len(widx)
    start, finish, n_sems = _gather_copies(widx)

    def body(*refs):
        full = refs[nw:2 * nw]
        start(None, full, *refs[2 * nw:])
        finish(None, full, *refs[2 * nw:])

    return pl.pallas_call(
        body, name="gather_weights_" + tag, in_specs=[HBM_SPEC] * nw, out_specs=[HBM_SPEC] * nw,
        out_shape=[jax.ShapeDtypeStruct(f.shape, f.dtype) for f in fulls],
        input_output_aliases={i: i for i in range(nw)},
        scratch_shapes=[pltpu.SemaphoreType.DMA((n_sems,)), pltpu.SemaphoreType.DMA((n_sems,))])(*fulls)


def _reduce_swap_halves(tag, grads, widx):
    nw = len(widx)
    geo = [BIG[i][1:] for i in widx]

    def body(*refs):
        g, got = refs[:nw], refs[nw:2 * nw]
        send_sems, recv_sems = refs[2 * nw:]
        x, y, c, _ = _mesh_pos()
        sib = (x, y, 1 - c)
        cps = []
        for wi, (k, n, ax) in enumerate(geo):
            for q in range(4):
                cp = _remote(_full_piece(g[wi], k, n, ax, q, 1 - c), got[wi].at[q],
                             send_sems.at[wi * 4 + q], recv_sems.at[wi * 4 + q], sib)
                cp.start()
                cps.append(cp)
        for cp in cps:
            cp.wait()

    return pl.pallas_call(
        body, name="reduce_swap_halves_" + tag, in_specs=[HBM_SPEC] * nw, out_specs=[HBM_SPEC] * nw,
        out_shape=[jax.ShapeDtypeStruct((4,) + _piece_shape(k, n, ax), F32) for k, n, ax in geo],
        scratch_shapes=[pltpu.SemaphoreType.DMA((4 * nw,)), pltpu.SemaphoreType.DMA((4 * nw,))])(*grads)


def _owner_copies(nw):
    def copies(p, out, send_sems, recv_sems):
        x, y, c, chips = _mesh_pos()
        return [_remote(p[wi].at[2 * qx + qy], out[wi].at[j], send_sems.at[wi * 3 + j], recv_sems.at[wi * 3 + j],
                        (qx, qy, c)) for wi in range(nw) for j, (qx, qy) in enumerate(chips)]

    def start(p, out, send_sems, recv_sems):
        for cp in copies(p, out, send_sems, recv_sems):
            cp.start()

    def finish(p, out, send_sems, recv_sems):
        for cp in copies(p, out, send_sems, recv_sems):
            cp.wait()

    return start, finish, 3 * nw


def _share_with_sibling(shards):
    nw = len(BIG)

    def body(*refs):
        out = refs[nw:2 * nw]
        send_sems, recv_sems = refs[2 * nw:]
        x, y, c, _ = _mesh_pos()
        sib = (x, y, 1 - c)
        cps = []
        for wi, (_, k, n, ax) in enumerate(BIG):
            mine = _shard_piece(out[wi], k, n, ax, c)
            cp = _remote(mine, mine, send_sems.at[wi], recv_sems.at[wi], sib)
            cp.start()
            cps.append(cp)
        for wi, (_, k, n, ax) in enumerate(BIG):
            piece = _shard_piece(out[wi], k, n, ax, 1 - c)
            _remote(piece, piece, send_sems.at[wi], recv_sems.at[wi], sib).wait_recv()
        for cp in cps:
            cp.wait_send()

    return pl.pallas_call(
        body, name="share_with_sibling", in_specs=[HBM_SPEC] * nw, out_specs=[HBM_SPEC] * nw,
        out_shape=[jax.ShapeDtypeStruct(sh.shape, sh.dtype) for sh in shards],
        input_output_aliases={i: i for i in range(nw)},
        scratch_shapes=[pltpu.SemaphoreType.DMA((nw,)), pltpu.SemaphoreType.DMA((nw,))])(*shards)


def _allreduce_small(v):
    r = v.shape[0]
    rh = r // 2
    assert rh % 8 == 0

    def body(v_ref, o_ref, sib_buf, chip_buf, send_sems, recv_sems):
        x, y, c, chips = _mesh_pos()
        me = 2 * x + y
        sib = (x, y, 1 - c)
        mine = pl.ds(pl.multiple_of(c * rh, 8), rh)
        other = pl.ds(pl.multiple_of((1 - c) * rh, 8), rh)
        swap = _remote(v_ref.at[other], sib_buf, send_sems.at[0], recv_sems.at[0], sib)
        swap.start()
        swap.wait()
        chip_buf[me] = v_ref[mine, :] + sib_buf[...]
        cps = []
        for j, (qx, qy) in enumerate(chips):
            cp = _remote(chip_buf.at[me], chip_buf.at[me], send_sems.at[1 + j], recv_sems.at[1 + j], (qx, qy, c))
            cp.start()
            cps.append(cp)
        for j, (qx, qy) in enumerate(chips):
            slot = chip_buf.at[2 * qx + qy]
            _remote(slot, slot, send_sems.at[1 + j], recv_sems.at[1 + j], (qx, qy, c)).wait_recv()
        for cp in cps:
            cp.wait_send()
        o_ref[mine, :] = ((chip_buf[0] + chip_buf[1]) + chip_buf[2]) + chip_buf[3]
        back = _remote(o_ref.at[mine], o_ref.at[mine], send_sems.at[4], recv_sems.at[4], sib)
        back.start()
        _remote(o_ref.at[other], o_ref.at[other], send_sems.at[4], recv_sems.at[4], sib).wait_recv()
        back.wait_send()

    return pl.pallas_call(
        body, name="allreduce_small", in_specs=[VMEM_SPEC], out_specs=VMEM_SPEC,
        out_shape=jax.ShapeDtypeStruct((r, 128), F32),
        scratch_shapes=[pltpu.VMEM((rh, 128), F32), pltpu.VMEM((4, rh, 128), F32),
                        pltpu.SemaphoreType.DMA((5,)), pltpu.SemaphoreType.DMA((5,))],
        compiler_params=pltpu.CompilerParams(vmem_limit_bytes=VMEM_LIMIT))(v)


def _as2d(a):
    a = a.reshape((-1, a.shape[-1])) if a.ndim > 1 else a.reshape(1, -1)
    return a


def _adamw_small(quads):
    n = len(quads)

    def body(*refs):
        for i in range(n):
            w, g, m, v = (r[...] for r in refs[4 * i:4 * i + 4])
            for ref, val in zip(refs[4 * n + 3 * i:4 * n + 3 * i + 3], _adamw(w, g, m, v)):
                ref[...] = val

    return pl.pallas_call(
        body, name="adamw_small", in_specs=[VMEM_SPEC] * (4 * n), out_specs=[VMEM_SPEC] * (3 * n),
        out_shape=[jax.ShapeDtypeStruct(q[0].shape, F32) for q in quads for _ in range(3)],
        compiler_params=pltpu.CompilerParams(vmem_limit_bytes=VMEM_LIMIT))(*[a for q in quads for a in q])


def _where():
    return jnp.stack([2 * lax.axis_index("x") + lax.axis_index("y"), lax.axis_index("c")]).astype(jnp.int32)


_BIG_INDEX = {name: i for i, (name, _, _, _) in enumerate(BIG)}


class _LocalWeights:
    def __init__(self, weights):
        self.weights, self.grads = weights, {}

    def gather_now(self, names):
        pass

    def gather_carry(self, names):
        return None

    def weight(self, name):
        return self.weights[name]

    def grad(self, name, g):
        self.grads[name] = g

    def reduce_carry(self, names):
        return None


class _Exchange:
    def __init__(self, inputs, where):
        self.inputs, self.where = inputs, where
        self.full, self.ready = {}, set()
        self.parts, self.landed, self.geom = {}, {}, {}
        for name, k, n, ax in BIG:
            w2 = inputs[name][0]
            rs, cs = w2.shape
            tm = _tile(rs, 512)
            steps = rs // tm
            if ax == 1:
                blk, idx = (tm, cs), lambda i, w: (i, w[0])
            else:
                blk, idx = (tm, n), functools.partial(lambda i, w, steps: (w[0] * steps + i, 0), steps=steps)
            self.full[name] = _placed("cast_" + name, lambda w: w, steps, where, [(w2, (tm, cs), lambda i, w: (i, 0))],
                                      jax.ShapeDtypeStruct((k, n), MXU_DTYPE), blk, idx)

    def _gathered(self, names, outs):
        for name, o in zip(names, outs):
            self.full[name] = o
            self.ready.add(name)

    def gather_now(self, names):
        self._gathered(names, _gather_weights(names[0], [self.full[n] for n in names], [_BIG_INDEX[n] for n in names]))

    def gather_carry(self, names):
        start, finish, n_sems = _gather_copies([_BIG_INDEX[n] for n in names])
        return _Carry([self.full[n] for n in names], list(range(len(names))), n_sems, start, finish,
                      functools.partial(self._gathered, names))

    def weight(self, name):
        assert name in self.ready, name
        return self.full[name]

    def grad(self, name, g):
        i = _BIG_INDEX[name]
        _, k, n, ax = BIG[i]
        got, = _reduce_swap_halves(name, [g], [i])
        pr, pc = _piece_shape(k, n, ax)
        tm = _tile(pr, 512)
        spp = pr // tm
        self.geom[name] = (pr, pc, tm, spp)
        if ax == 1:
            g_idx = functools.partial(lambda i, w, spp: (w[1] * spp + i % spp, i // spp), spp=spp)
        else:
            g_idx = functools.partial(lambda i, w, spp: ((i // spp) * 2 * spp + w[1] * spp + i % spp, 0), spp=spp)
        self.parts[name] = _placed(
            "pair_sum_" + name, lambda a, b: a + b, 4 * spp, self.where,
            [(g, (tm, pc), g_idx), (got.reshape(4 * pr, pc), (tm, pc), lambda i, w: (i, 0))],
            jax.ShapeDtypeStruct((4 * pr, pc), BF16), (tm, pc), lambda i, w: (i, 0)).reshape(4, pr, pc)

    def _landed(self, names, outs):
        for name, o in zip(names, outs):
            self.landed[name] = o

    def reduce_carry(self, names):
        start, finish, n_sems = _owner_copies(len(names))
        parts = [self.parts[n] for n in names]
        outs = [jax.ShapeDtypeStruct((3,) + p.shape[1:], p.dtype) for p in parts]
        return _Carry(parts, outs, n_sems, start, finish, functools.partial(self._landed, names))

    def finish(self):
        halves = []
        for name, _, _, _ in BIG:
            pr, pc, tm, spp = self.geom[name]
            ins = [(self.parts[name], (None, tm, pc), lambda i, w: (w[0], i, 0))]
            ins += [(self.landed[name], (None, tm, pc), functools.partial(lambda i, w, j: (j, i, 0), j=j))
                    for j in range(3)]
            halves.append(_placed("chip_sum_" + name,
                                  lambda a, b, c, d: ((a.astype(F32) + b.astype(F32)) + c.astype(F32)) + d.astype(F32),
                                  spp, self.where, ins, jax.ShapeDtypeStruct(self.inputs[name].shape[1:], F32), (tm, pc),
                                  functools.partial(lambda i, w, spp: (w[1] * spp + i, 0), spp=spp)))
        return dict(zip([b[0] for b in BIG], _share_with_sibling(halves)))


def _step(inputs):
    x, mem, positions, target = inputs["x"][0], inputs["mem"][0], inputs["positions"], inputs["loss_target"][0]
    pos = positions.reshape(-1, 1)
    ex = _Exchange(inputs, _where())
    sp = {name: _as2d(inputs[name]) for name in SMALL}
    memb, = _rowwise("cast_mem", lambda m: (m,), [mem], [], [(D_MODEL, MXU_DTYPE)])

    loss, dx, gsmall = _local_step(x, memb, pos, target, sp, ex)
    gshard = ex.finish()

    out = {}
    for name, _, _, _ in BIG:
        w2, m2, v2 = inputs[name][0], inputs["m_" + name][0], inputs["v_" + name][0]
        n = w2.shape[1]
        d, nm, nv = _rowwise("adamw_" + name, _adamw, [w2, gshard[name], m2, v2], [], [(n, F32)] * 3, tm=128)
        lead = inputs[name].shape
        out[name] = (gshard[name].reshape(lead), d.reshape(lead), nm.reshape(lead), nv.reshape(lead))

    def tiles(a):
        flat = a.reshape(-1)
        n = -(-flat.shape[0] // 1024) * 1024
        return jnp.pad(flat, (0, n - flat.shape[0])).reshape(n // 128, 128)

    pieces = [tiles(loss[:, :1])] + [tiles(gsmall[name]) for name in SMALL]
    if sum(p.shape[0] for p in pieces) % 16:
        pieces.append(jnp.zeros((8, 128), F32))
    red = _allreduce_small(jnp.concatenate(pieces, axis=0))
    loss_total = red[0, 0]
    grads, off = {}, pieces[0].shape[0]
    for name, p in zip(SMALL, pieces[1:]):
        shp = _as2d(inputs[name]).shape
        grads[name] = red[off:off + p.shape[0]].reshape(-1)[:shp[0] * shp[1]].reshape(shp)
        off += p.shape[0]
    upd = _adamw_small([(_as2d(inputs[n]), grads[n], _as2d(inputs["m_" + n]), _as2d(inputs["v_" + n])) for n in SMALL])
    for i, name in enumerate(SMALL):
        shp = inputs[name].shape
        out[name] = (grads[name].reshape(shp),) + tuple(t.reshape(shp) for t in upd[3 * i:3 * i + 3])
    return loss_total, dx.reshape(inputs["x"].shape), out


_ARG_NAMES = (("x", "mem", "positions") + WEIGHT_ORDER + ("loss_target",) + tuple("m_" + n for n in WEIGHT_ORDER)
              + tuple("v_" + n for n in WEIGHT_ORDER))


def kernel(x, mem, positions, ln_in_g, ln_in_b, w_in, b_in, ssm_log_dt, ssm_a_re, ssm_a_im, ssm_b_re, ssm_b_im, ssm_c_re, ssm_c_im, ssm_d, w_glu, b_glu, w_att_up, w_mix_out, b_mix_out, ln1_g, ln1_b, w_xq, w_xkv, w_xo, ln2_g, ln2_b, w_ff1, b_ff1, w_ff2, b_ff2, ln3_g, ln3_b, loss_target, m_ln_in_g, m_ln_in_b, m_w_in, m_b_in, m_ssm_log_dt, m_ssm_a_re, m_ssm_a_im, m_ssm_b_re, m_ssm_b_im, m_ssm_c_re, m_ssm_c_im, m_ssm_d, m_w_glu, m_b_glu, m_w_att_up, m_w_mix_out, m_b_mix_out, m_ln1_g, m_ln1_b, m_w_xq, m_w_xkv, m_w_xo, m_ln2_g, m_ln2_b, m_w_ff1, m_b_ff1, m_w_ff2, m_b_ff2, m_ln3_g, m_ln3_b, v_ln_in_g, v_ln_in_b, v_w_in, v_b_in, v_ssm_log_dt, v_ssm_a_re, v_ssm_a_im, v_ssm_b_re, v_ssm_b_im, v_ssm_c_re, v_ssm_c_im, v_ssm_d, v_w_glu, v_b_glu, v_w_att_up, v_w_mix_out, v_b_mix_out, v_ln1_g, v_ln1_b, v_w_xq, v_w_xkv, v_w_xo, v_ln2_g, v_ln2_b, v_w_ff1, v_b_ff1, v_w_ff2, v_b_ff2, v_ln3_g, v_ln3_b):
    args = (x, mem, positions, ln_in_g, ln_in_b, w_in, b_in, ssm_log_dt, ssm_a_re, ssm_a_im, ssm_b_re, ssm_b_im, ssm_c_re, ssm_c_im, ssm_d, w_glu, b_glu, w_att_up, w_mix_out, b_mix_out, ln1_g, ln1_b, w_xq, w_xkv, w_xo, ln2_g, ln2_b, w_ff1, b_ff1, w_ff2, b_ff2, ln3_g, ln3_b, loss_target, m_ln_in_g, m_ln_in_b, m_w_in, m_b_in, m_ssm_log_dt, m_ssm_a_re, m_ssm_a_im, m_ssm_b_re, m_ssm_b_im, m_ssm_c_re, m_ssm_c_im, m_ssm_d, m_w_glu, m_b_glu, m_w_att_up, m_w_mix_out, m_b_mix_out, m_ln1_g, m_ln1_b, m_w_xq, m_w_xkv, m_w_xo, m_ln2_g, m_ln2_b, m_w_ff1, m_b_ff1, m_w_ff2, m_b_ff2, m_ln3_g, m_ln3_b, v_ln_in_g, v_ln_in_b, v_w_in, v_b_in, v_ssm_log_dt, v_ssm_a_re, v_ssm_a_im, v_ssm_b_re, v_ssm_b_im, v_ssm_c_re, v_ssm_c_im, v_ssm_d, v_w_glu, v_b_glu, v_w_att_up, v_w_mix_out, v_b_mix_out, v_ln1_g, v_ln1_b, v_w_xq, v_w_xkv, v_w_xo, v_ln2_g, v_ln2_b, v_w_ff1, v_b_ff1, v_w_ff2, v_b_ff2, v_ln3_g, v_ln3_b)
    assert len(args) == len(_ARG_NAMES)
    inputs = dict(zip(_ARG_NAMES, args))
    loss, dx, out = _step(inputs)
    res = [loss, dx]
    for k in range(4):
        res += [out[name][k] for name in WEIGHT_ORDER]
    return tuple(res)
```

```python
import functools
import math

import numpy as np
import jax
import jax.numpy as jnp
from jax import lax
from jax.experimental import pallas as pl
from jax.experimental.pallas import tpu as pltpu

F32 = jnp.float32
BF16 = jnp.bfloat16
MXU_DTYPE = jnp.bfloat16

D_MODEL = 1024
SSM_GROUP = 16
SSM_WIDTH = 768
SSM_GROUPS = 48
SSM_STATE = 64
N_STATE = SSM_GROUPS * SSM_STATE
SSM_CHUNKS = 6
CH_W = 128
CH_N = 512
ATT_HEAD_DIM = 64
ATT_HPG = 4
ATT_GROUPW = ATT_HPG * ATT_HEAD_DIM
DILATIONS = (1, 4, 16)
ATT_BLK = 128
ATT_SCALE = ATT_HEAD_DIM ** -0.5
ROT_DIM = 16
ROPE_THETA = 500000.0
XATT_HEADS = 4
XATT_HEAD_DIM = 256
XATT_SCALE = XATT_HEAD_DIM ** -0.5
D_FF = 4096
IN_COLS = 5120
DEEPNORM_ALPHA = 2.0 ** 0.25
LN_EPS = 1e-5
NEG_INF = -1e30
ADAM_LR = 0.001
ADAM_B1 = 0.9
ADAM_B2 = 0.999
ADAM_EPS = 1e-08
ADAM_WD = 0.01
ADAM_STEP = 10

N_SEG = 32
VMEM_LIMIT = 48 * 1024 * 1024
MESH = pl.DeviceIdType.MESH
HBM_SPEC = pl.BlockSpec(memory_space=pltpu.HBM)
VMEM_SPEC = pl.BlockSpec(memory_space=pltpu.VMEM)

BIG = (("w_in", 1024, 5120, 1), ("w_glu", 768, 2048, 1), ("w_att_up", 256, 1024, 1),
       ("w_mix_out", 1024, 1024, 0), ("w_xq", 1024, 1024, 0), ("w_xkv", 1024, 2048, 1),
       ("w_xo", 1024, 1024, 0), ("w_ff1", 1024, 4096, 1), ("w_ff2", 4096, 1024, 0))
SMALL = ("ln_in_g", "ln_in_b", "b_in", "ssm_log_dt", "ssm_a_re", "ssm_a_im", "ssm_b_re", "ssm_b_im",
         "ssm_c_re", "ssm_c_im", "ssm_d", "b_glu", "b_mix_out", "ln1_g", "ln1_b", "ln2_g", "ln2_b",
         "b_ff1", "b_ff2", "ln3_g", "ln3_b")
WEIGHT_ORDER = ("ln_in_g", "ln_in_b", "w_in", "b_in", "ssm_log_dt", "ssm_a_re", "ssm_a_im", "ssm_b_re",
                "ssm_b_im", "ssm_c_re", "ssm_c_im", "ssm_d", "w_glu", "b_glu", "w_att_up", "w_mix_out",
                "b_mix_out", "ln1_g", "ln1_b", "w_xq", "w_xkv", "w_xo", "ln2_g", "ln2_b", "w_ff1", "b_ff1",
                "w_ff2", "b_ff2", "ln3_g", "ln3_b")


def _cparams(n_axes):
    return pltpu.CompilerParams(dimension_semantics=("arbitrary",) * n_axes, vmem_limit_bytes=VMEM_LIMIT)


class _Carry:
    def __init__(self, ins, outs, n_sems, start, finish, done):
        self.ins, self.outs, self.n_sems, self.start, self.finish, self.done = ins, outs, n_sems, start, finish, done


def _call(name, body, grid, in_specs, out_specs, out_shape, args, scratch_shapes=(), carry=None):
    in_specs, out_specs, out_shape = list(in_specs), list(out_specs), list(out_shape)
    params = _cparams(len(grid))
    if carry is None:
        return pl.pallas_call(body, name=name, grid=grid, in_specs=in_specs, out_specs=out_specs, out_shape=out_shape,
                              scratch_shapes=list(scratch_shapes), compiler_params=params)(*args)
    n_in, n_out, n_ci, n_co = len(in_specs), len(out_specs), len(carry.ins), len(carry.outs)
    n_scr = len(scratch_shapes)

    def wrapped(*refs):
        ins, c_in = refs[:n_in], refs[n_in:n_in + n_ci]
        outs, c_out = refs[n_in + n_ci:n_in + n_ci + n_out], refs[n_in + n_ci + n_out:n_in + n_ci + n_out + n_co]
        scratch = refs[n_in + n_ci + n_out + n_co:n_in + n_ci + n_out + n_co + n_scr]
        send_sems, recv_sems = refs[-2:]
        ids = [pl.program_id(a) for a in range(len(grid))]
        first = functools.reduce(jnp.logical_and, [i == 0 for i in ids])
        last = functools.reduce(jnp.logical_and, [i == g - 1 for i, g in zip(ids, grid)])

        @pl.when(first)
        def _():
            carry.start(c_in, c_out, send_sems, recv_sems)

        body(*ins, *outs, *scratch)

        @pl.when(last)
        def _():
            carry.finish(c_in, c_out, send_sems, recv_sems)

    c_shapes = [jax.ShapeDtypeStruct(carry.ins[o].shape, carry.ins[o].dtype) if isinstance(o, int) else o
                for o in carry.outs]
    aliases = {n_in + o: n_out + i for i, o in enumerate(carry.outs) if isinstance(o, int)}
    res = pl.pallas_call(
        wrapped, name=name, grid=grid, in_specs=in_specs + [HBM_SPEC] * n_ci, out_specs=out_specs + [HBM_SPEC] * n_co,
        out_shape=out_shape + c_shapes, input_output_aliases=aliases,
        scratch_shapes=list(scratch_shapes) + [pltpu.SemaphoreType.DMA((carry.n_sems,))] * 2,
        compiler_params=params)(*args, *carry.ins)
    carry.done(res[n_out:])
    return res[:n_out]


def _rowwise(name, fn, rows, consts, outs, reds=(), tm=256, touts=()):
    n_rows = (rows[0][0] if isinstance(rows[0], tuple) else rows[0]).shape[-2]
    tm = min(tm, n_rows)
    assert n_rows % tm == 0, (name, n_rows, tm)
    specs, args = [], []
    for r in rows:
        if isinstance(r, tuple) and len(r) == 3:
            arr, width, cb = r
            specs.append(pl.BlockSpec((tm, width), functools.partial(lambda i, cb: (i, cb), cb=cb)))
        elif isinstance(r, tuple):
            arr, slot = r
            specs.append(pl.BlockSpec((None, tm, arr.shape[2]), functools.partial(lambda i, s: (s, i, 0), s=slot)))
        else:
            arr = r
            specs.append(pl.BlockSpec((tm, arr.shape[1]), lambda i: (i, 0)))
        args.append(arr)
        assert arr.shape[-2] == n_rows, (name, arr.shape, n_rows)
    for cst in consts:
        specs.append(pl.BlockSpec(cst.shape, lambda i: (0, 0)))
        args.append(cst)
    n_r, n_c, n_o, n_d = len(rows), len(consts), len(outs) + len(touts), len(reds)
    out_shape = [jax.ShapeDtypeStruct((n_rows, c), dt) for c, dt in outs]
    out_specs = [pl.BlockSpec((tm, c), lambda i: (i, 0)) for c, _ in outs]
    out_shape += [jax.ShapeDtypeStruct((r, n_rows), dt) for r, dt in touts]
    out_specs += [pl.BlockSpec((r, tm), lambda i: (0, i)) for r, _ in touts]
    out_shape += [jax.ShapeDtypeStruct((1, c), F32) for c in reds]
    out_specs += [pl.BlockSpec((1, c), lambda i: (0, 0)) for c in reds]

    def body(*refs):
        ins = [r[...] for r in refs[:n_r + n_c]]
        o_refs = refs[n_r + n_c:n_r + n_c + n_o]
        d_refs = refs[n_r + n_c + n_o:]
        res = fn(*ins)
        res = res if isinstance(res, (tuple, list)) else (res,)
        assert len(res) == n_o + n_d, (name, len(res))
        for ref, val in zip(o_refs, res[:n_o]):
            ref[...] = val.astype(ref.dtype)
        first = pl.program_id(0) == 0
        for ref, val in zip(d_refs, res[n_o:]):
            @pl.when(first)
            def _(ref=ref, val=val):
                ref[...] = val

            @pl.when(jnp.logical_not(first))
            def _(ref=ref, val=val):
                ref[...] += val

    res = pl.pallas_call(body, name=name, grid=(n_rows // tm,), in_specs=specs, out_specs=out_specs,
                         out_shape=out_shape, compiler_params=_cparams(1))(*args)
    return res


def _colsum(v):
    return jnp.sum(v.astype(F32), axis=0, keepdims=True)


_DIMS = {"nn": (((1,), (0,)), ((), ())), "nt": (((1,), (1,)), ((), ())), "tn": (((0,), (0,)), ((), ()))}


def _tile(dim, want):
    if dim <= want:
        return dim
    return max(t for t in range(128, want + 1, 128) if dim % t == 0)


def _dot(a, b, mode):
    return lax.dot_general(a.astype(MXU_DTYPE), b.astype(MXU_DTYPE), _DIMS[mode], preferred_element_type=F32)


def _mm(name, a, b, mode, *, bias=None, extras=(), epilogue=None, out_dtypes=(F32,), tm=1024, tn=1024, tk=1024,
        carry=None):
    if mode == "nn":
        (m, k), (_, n) = a.shape, b.shape
    elif mode == "nt":
        (m, k), (n, _) = a.shape, b.shape
    else:
        (k, m), (_, n) = a.shape, b.shape
    tm, tn = _tile(m, tm), _tile(n, tn)
    if mode != "tn":
        tk = k if k <= 1024 else tk
    tk = _tile(k, tk)
    assert m % tm == 0 and n % tn == 0 and k % tk == 0, (name, m, n, k)
    nk = k // tk
    a_spec = {"nn": pl.BlockSpec((tm, tk), lambda i, j, kk: (i, kk)),
              "nt": pl.BlockSpec((tm, tk), lambda i, j, kk: (i, kk)),
              "tn": pl.BlockSpec((tk, tm), lambda i, j, kk: (kk, i))}[mode]
    b_spec = {"nn": pl.BlockSpec((tk, tn), lambda i, j, kk: (kk, j)),
              "nt": pl.BlockSpec((tn, tk), lambda i, j, kk: (j, kk)),
              "tn": pl.BlockSpec((tk, tn), lambda i, j, kk: (kk, j))}[mode]
    specs, args = [a_spec, b_spec], [a, b]
    if bias is not None:
        specs.append(pl.BlockSpec((1, tn), lambda i, j, kk: (0, j)))
        args.append(bias)
    for e in extras:
        specs.append(pl.BlockSpec((tm, tn), lambda i, j, kk: (i, j)))
        args.append(e)
    n_e, n_o = len(extras), len(out_dtypes)
    has_bias = bias is not None

    def body(*refs):
        a_ref, b_ref = refs[0], refs[1]
        pos = 2
        bias_ref = refs[pos] if has_bias else None
        pos += int(has_bias)
        e_refs = refs[pos:pos + n_e]
        o_refs = refs[pos + n_e:pos + n_e + n_o]
        acc_ref = refs[pos + n_e + n_o] if nk > 1 else None
        part = _dot(a_ref[...], b_ref[...], mode)

        def finish(r):
            if has_bias:
                r = r + bias_ref[...]
            res = epilogue(r, *[e[...] for e in e_refs]) if epilogue is not None else (r,)
            for ref, val in zip(o_refs, res):
                ref[...] = val.astype(ref.dtype)

        if nk == 1:
            finish(part)
        else:
            kk = pl.program_id(2)

            @pl.when(kk == 0)
            def _():
                acc_ref[...] = part

            @pl.when(kk > 0)
            def _():
                acc_ref[...] += part

            @pl.when(kk == nk - 1)
            def _():
                finish(acc_ref[...])

    res = _call(name, body, (m // tm, n // tn, nk), specs,
                [pl.BlockSpec((tm, tn), lambda i, j, kk: (i, j)) for _ in out_dtypes],
                [jax.ShapeDtypeStruct((m, n), dt) for dt in out_dtypes], args,
                scratch_shapes=[pltpu.VMEM((tm, tn), F32)] if nk > 1 else [], carry=carry)
    return res[0] if n_o == 1 else res


def _ssm_wgrad(name, chan, st_re, st_im, expand, tk=512):
    s = chan.shape[0]
    tk = min(tk, s)
    nk = s // tk
    oshape = (CH_W, 2 * CH_N) if expand else (2 * CH_N, CH_W)

    def body(c_ref, re_ref, im_ref, o_ref):
        c = c_ref[...]
        if expand:
            part = jnp.concatenate([_dot(c, re_ref[...], "tn"), _dot(c, im_ref[...], "tn")], axis=1)
        else:
            part = jnp.concatenate([_dot(re_ref[...], c, "tn"), _dot(im_ref[...], c, "tn")], axis=0)
        kk = pl.program_id(1)

        @pl.when(kk == 0)
        def _():
            o_ref[...] = part

        @pl.when(kk > 0)
        def _():
            o_ref[...] += part

    return pl.pallas_call(
        body, name=name, grid=(SSM_CHUNKS, nk),
        in_specs=[pl.BlockSpec((tk, CH_W), lambda j, kk: (kk, j)), pl.BlockSpec((tk, CH_N), lambda j, kk: (kk, j)),
                  pl.BlockSpec((tk, CH_N), lambda j, kk: (kk, j))],
        out_specs=pl.BlockSpec((None,) + oshape, lambda j, kk: (j, 0, 0)),
        out_shape=jax.ShapeDtypeStruct((SSM_CHUNKS,) + oshape, F32),
        compiler_params=_cparams(2))(chan, st_re, st_im)


SCAN_LB = 256


def _split_by_scan_block(mat, axis):
    halves = []
    for l in range(CH_N // SCAN_LB):
        re = lax.slice_in_dim(mat, l * SCAN_LB, (l + 1) * SCAN_LB, axis=axis)
        im = lax.slice_in_dim(mat, CH_N + l * SCAN_LB, CH_N + (l + 1) * SCAN_LB, axis=axis)
        halves.append(jnp.concatenate([re, im], axis=axis))
    return jnp.stack(halves, axis=1).reshape((-1,) + halves[0].shape[1:])


def _ssm_scan(name, chan, expand12, contract12, a_re, a_im, d_row, reverse, carry=None):
    s = chan.shape[0]
    seg_len = s // N_SEG
    n_sq = int(math.log2(seg_len))
    assert 2 ** n_sq == seg_len
    rb = min(512, s)
    per_chunk = CH_N // SCAN_LB

    def body(are_ref, aim_ref, ch_ref, e_ref, k_ref, d_ref, hre_ref, him_ref, o_ref, wre_ref, wim_ref, ere, eim, cre, cim):
        e_mat, k_mat = e_ref[...], k_ref[...]
        for r in range(s // rb):
            rows = slice(r * rb, (r + 1) * rb)
            c = ch_ref[rows, :]
            if reverse:
                wre_ref[rows, :] = _dot(c, e_mat[:SCAN_LB], "nt")
                wim_ref[rows, :] = _dot(c, e_mat[SCAN_LB:], "nt")
            else:
                wre_ref[rows, :] = _dot(c, e_mat[:, :SCAN_LB], "nn")
                wim_ref[rows, :] = _dot(c, e_mat[:, SCAN_LB:], "nn")

        ar1 = are_ref[...]
        ai1 = -aim_ref[...] if reverse else aim_ref[...]
        ar = jnp.broadcast_to(ar1, (N_SEG, SCAN_LB))
        ai = jnp.broadcast_to(ai1, (N_SEG, SCAN_LB))

        def rows_of(k):
            kk = seg_len - 1 - k if reverse else k
            return pl.ds(pl.multiple_of(kk * N_SEG, N_SEG), N_SEG)

        def local(k, carry):
            hr, hi = carry
            rows = rows_of(k)
            nr = ar * hr - ai * hi + wre_ref[rows, :]
            ni = ar * hi + ai * hr + wim_ref[rows, :]
            hre_ref[rows, :] = nr
            him_ref[rows, :] = ni
            return nr, ni

        zero = jnp.zeros((N_SEG, SCAN_LB), F32)
        er, ei = lax.fori_loop(0, seg_len, local, (zero, zero))
        ere[...] = er
        eim[...] = ei
        pr, pi = ar1, ai1
        for _ in range(n_sq):
            pr, pi = pr * pr - pi * pi, 2.0 * pr * pi
        cr = jnp.zeros((1, SCAN_LB), F32)
        ci = jnp.zeros((1, SCAN_LB), F32)
        for jj in range(N_SEG):
            j = N_SEG - 1 - jj if reverse else jj
            cre[j:j + 1, :] = cr
            cim[j:j + 1, :] = ci
            er_j, ei_j = ere[j:j + 1, :], eim[j:j + 1, :]
            cr, ci = pr * cr - pi * ci + er_j, pr * ci + pi * cr + ei_j
        c_r, c_i = cre[...], cim[...]

        def fix(k, carry):
            qr, qi = carry
            rows = rows_of(k)
            hre_ref[rows, :] = hre_ref[rows, :] + (qr * c_r - qi * c_i)
            him_ref[rows, :] = him_ref[rows, :] + (qr * c_i + qi * c_r)
            return qr * ar - qi * ai, qr * ai + qi * ar

        lax.fori_loop(0, seg_len, fix, (ar, ai))

        first_of_chunk = lax.rem(pl.program_id(0), per_chunk) == 0
        for r in range(s // rb):
            rows = slice(r * rb, (r + 1) * rb)
            if reverse:
                part = (_dot(hre_ref[rows, :], k_mat[:, :SCAN_LB], "nt")
                        + _dot(him_ref[rows, :], k_mat[:, SCAN_LB:], "nt"))
            else:
                part = _dot(hre_ref[rows, :], k_mat[:SCAN_LB], "nn") + _dot(him_ref[rows, :], k_mat[SCAN_LB:], "nn")

            @pl.when(first_of_chunk)
            def _(rows=rows, part=part):
                o_ref[rows, :] = part + d_ref[...] * ch_ref[rows, :]

            @pl.when(jnp.logical_not(first_of_chunk))
            def _(rows=rows, part=part):
                o_ref[rows, :] += part

    nblk = N_STATE // SCAN_LB
    blk = pl.BlockSpec((s, SCAN_LB), lambda b: (0, b))
    row = pl.BlockSpec((1, SCAN_LB), lambda b: (0, b))
    chan_blk = pl.BlockSpec((s, CH_W), lambda b: (0, b // per_chunk))
    res = _call(name, body, (nblk,),
                [row, row, chan_blk, pl.BlockSpec((None,) + expand12.shape[1:], lambda b: (b, 0, 0)),
                 pl.BlockSpec((None,) + contract12.shape[1:], lambda b: (b, 0, 0)),
                 pl.BlockSpec((1, CH_W), lambda b: (0, b // per_chunk))],
                [blk, blk, chan_blk],
                [jax.ShapeDtypeStruct((s, N_STATE), F32)] * 2 + [jax.ShapeDtypeStruct((s, SSM_WIDTH), F32)],
                (a_re, a_im, chan, expand12, contract12, d_row),
                scratch_shapes=[pltpu.VMEM((s, SCAN_LB), F32)] * 2 + [pltpu.VMEM((N_SEG, SCAN_LB), F32)] * 4, carry=carry)
    return res[0], res[1], res[2]


def _ssm_da(g_re, g_im, h_re, h_im):
    s = g_re.shape[0]
    seg_len = s // N_SEG

    def body(gre_ref, gim_ref, hre_ref, him_ref, dre_ref, dim_ref):
        def rows_of(k):
            return pl.ds(pl.multiple_of(k * N_SEG, N_SEG), N_SEG)

        def step(k, carry):
            sr, si = carry
            gr, gi = gre_ref[rows_of(k), :], gim_ref[rows_of(k), :]
            pr, pi = hre_ref[rows_of(k - 1), :], him_ref[rows_of(k - 1), :]
            return sr + gr * pr + gi * pi, si + gi * pr - gr * pi

        zero = jnp.zeros((N_SEG, SCAN_LB), F32)
        sr, si = lax.fori_loop(1, seg_len, step, (zero, zero))
        last = pl.ds((seg_len - 1) * N_SEG, N_SEG)
        first_row = lax.broadcasted_iota(jnp.int32, (N_SEG, SCAN_LB), 0) == 0
        pr = jnp.where(first_row, 0.0, pltpu.roll(hre_ref[last, :], 1, 0))
        pi = jnp.where(first_row, 0.0, pltpu.roll(him_ref[last, :], 1, 0))
        gr, gi = gre_ref[pl.ds(0, N_SEG), :], gim_ref[pl.ds(0, N_SEG), :]
        sr = sr + gr * pr + gi * pi
        si = si + gi * pr - gr * pi
        dre_ref[...] = jnp.sum(sr, axis=0, keepdims=True)
        dim_ref[...] = jnp.sum(si, axis=0, keepdims=True)

    nblk = N_STATE // SCAN_LB
    blk = pl.BlockSpec((s, SCAN_LB), lambda b: (0, b))
    row = pl.BlockSpec((1, SCAN_LB), lambda b: (0, b))
    return pl.pallas_call(
        body, name="ssm_da", grid=(nblk,), in_specs=[blk] * 4, out_specs=[row, row],
        out_shape=[jax.ShapeDtypeStruct((1, N_STATE), F32)] * 2,
        compiler_params=_cparams(1))(g_re, g_im, h_re, h_im)


def _disc(ldt, are, aim, bre, bim):
    dt = jnp.exp(ldt)
    mag = jnp.exp(are * dt)
    abr = mag * jnp.cos(aim * dt)
    abi = mag * jnp.sin(aim * dt)
    den = jnp.square(are) + jnp.square(aim)
    nr = abr - 1.0
    fre = (nr * are + abi * aim) / den
    fim = (abi * are - nr * aim) / den
    return abr, abi, fre * bre - fim * bim, fre * bim + fim * bre


def _ssm_disc_fwd(ldt, are, aim, bre, bim):
    def body(l_ref, ar_ref, ai_ref, br_ref, bi_ref, o0, o1, o2, o3):
        res = _disc(l_ref[...], ar_ref[...], ai_ref[...], br_ref[...], bi_ref[...])
        for ref, val in zip((o0, o1, o2, o3), res):
            ref[...] = val

    col = jax.ShapeDtypeStruct((N_STATE, 1), F32)
    mat = jax.ShapeDtypeStruct((N_STATE, SSM_GROUP), F32)
    return pl.pallas_call(body, name="ssm_disc_fwd", out_shape=[col, col, mat, mat],
                          in_specs=[VMEM_SPEC] * 5, out_specs=[VMEM_SPEC] * 4)(ldt, are, aim, bre, bim)


def _ssm_disc_bwd(ldt, are, aim, bre, bim, d_abr, d_abi, d_bbr, d_bbi):
    def body(l_ref, ar_ref, ai_ref, br_ref, bi_ref, c0, c1, c2, c3, g_ldt, g_are, g_aim, g_bre, g_bim):
        _, vjp = jax.vjp(_disc, l_ref[...], ar_ref[...], ai_ref[...], br_ref[...], bi_ref[...])
        dl, dar, dai, dbr, dbi = vjp((c0[...], c1[...], c2[...], c3[...]))
        state = lax.broadcasted_iota(jnp.int32, (N_STATE, SSM_GROUPS), 0)
        group = lax.broadcasted_iota(jnp.int32, (N_STATE, SSM_GROUPS), 1)
        pick = jnp.right_shift(state, 6) == group
        g_ldt[...] = jnp.sum(jnp.where(pick, dl, 0.0), axis=0, keepdims=True)
        g_are[...] = dar
        g_aim[...] = dai
        g_bre[...] = dbr
        g_bim[...] = dbi

    col = jax.ShapeDtypeStruct((N_STATE, 1), F32)
    mat = jax.ShapeDtypeStruct((N_STATE, SSM_GROUP), F32)
    return pl.pallas_call(body, name="ssm_disc_bwd",
                          out_shape=[jax.ShapeDtypeStruct((1, SSM_GROUPS), F32), col, col, mat, mat],
                          in_specs=[VMEM_SPEC] * 9, out_specs=[VMEM_SPEC] * 5,
                          compiler_params=pltpu.CompilerParams(vmem_limit_bytes=VMEM_LIMIT))(
        ldt, are, aim, bre, bim, d_abr, d_abi, d_bbr, d_bbi)


_EYE8 = np.eye(8, dtype=np.float32)


def _blockdiag_b(bb):
    t = bb.reshape(SSM_CHUNKS, 8, SSM_STATE, SSM_GROUP).transpose(0, 1, 3, 2)
    return jnp.einsum("igcn,gh->igchn", t, _EYE8).reshape(SSM_CHUNKS, CH_W, CH_N)


def _diag_of_b(m):
    t = jnp.einsum("igchn,gh->igcn", m.reshape(SSM_CHUNKS, 8, SSM_GROUP, 8, SSM_STATE), _EYE8)
    return t.transpose(0, 1, 3, 2).reshape(N_STATE, SSM_GROUP)


def _blockdiag_c(c):
    t = c.reshape(SSM_CHUNKS, 8, SSM_GROUP, SSM_STATE).transpose(0, 1, 3, 2)
    return jnp.einsum("ignc,gh->ignhc", t, _EYE8).reshape(SSM_CHUNKS, CH_N, CH_W)


def _diag_of_c(m):
    t = jnp.einsum("ignhc,gh->ignc", m.reshape(SSM_CHUNKS, 8, SSM_STATE, 8, SSM_GROUP), _EYE8)
    return t.transpose(0, 1, 3, 2).reshape(SSM_GROUPS, SSM_GROUP, SSM_STATE)


def _time_perm(a):
    s, c = a.shape
    return a.reshape(N_SEG, s // N_SEG, c).transpose(1, 0, 2).reshape(s, c)


def _time_unperm(a):
    s, c = a.shape
    return a.reshape(s // N_SEG, N_SEG, c).transpose(1, 0, 2).reshape(s, c)


def _dilate(a, d):
    s, c = a.shape
    return a if d == 1 else a.reshape(s // d, d, c).transpose(1, 0, 2).reshape(s, c)


def _undilate(a, d):
    s, c = a.shape
    return a if d == 1 else a.reshape(d, s // d, c).transpose(1, 0, 2).reshape(s, c)


def _dilate_rows(a, d):
    r, s = a.shape
    return a if d == 1 else a.reshape(r, s // d, d).transpose(0, 2, 1).reshape(r, s)


ATT_T = 4
ATT_ROWS = ATT_T * ATT_BLK


def _window(prev_ref, cur_ref, i, sl):
    if i == 0:
        return jnp.concatenate([prev_ref[:, sl], cur_ref[0:ATT_BLK, sl]], axis=0)
    return cur_ref[(i - 1) * ATT_BLK:(i + 1) * ATT_BLK, sl]


def _band_valid(first_key):
    qi = lax.broadcasted_iota(jnp.int32, (ATT_BLK, 2 * ATT_BLK), 0)
    ki = lax.broadcasted_iota(jnp.int32, (ATT_BLK, 2 * ATT_BLK), 1)
    steps = qi + ATT_BLK - ki
    return (steps >= 0) & (steps <= ATT_BLK) & (ki >= first_key)


ATT_STATW = ATT_HPG * 128


def _stat(h):
    return slice(h * 128, (h + 1) * 128)


def _stat_rows(stat):
    n = stat.shape[0]
    heads = [stat[:, _stat(h)].T[0:1, :] for h in range(ATT_HPG)]
    return jnp.concatenate(heads + [jnp.zeros((8 - ATT_HPG, n), stat.dtype)], axis=0)


def _attn_specs(nb, width=ATT_GROUPW):
    cur = pl.BlockSpec((ATT_ROWS, width), lambda b: (b, 0))
    prev = pl.BlockSpec((ATT_BLK, width), lambda b: (jnp.maximum(b * ATT_T - 1, 0), 0))
    nxt = pl.BlockSpec((ATT_BLK, width), lambda b: (jnp.minimum((b + 1) * ATT_T, nb - 1), 0))
    return cur, prev, nxt


def _attn_fwd(tag, per_seq, q, k, v):
    s = q.shape[0]
    nb = s // ATT_BLK

    def body(q_ref, kc_ref, kp_ref, vc_ref, vp_ref, o_ref, lse_ref):
        bt = pl.program_id(0)
        for i in range(ATT_T):
            has_prev = lax.rem(bt * ATT_T + i, per_seq) > 0
            valid = _band_valid(jnp.where(has_prev, 0, ATT_BLK))
            rows = slice(i * ATT_BLK, (i + 1) * ATT_BLK)
            for h in range(ATT_HPG):
                sl = slice(h * ATT_HEAD_DIM, (h + 1) * ATT_HEAD_DIM)
                kcat = _window(kp_ref, kc_ref, i, sl)
                vcat = _window(vp_ref, vc_ref, i, sl)
                sc = _dot(q_ref[rows, sl], kcat, "nt") * ATT_SCALE
                sc = jnp.where(valid, sc, NEG_INF)
                m = jnp.max(sc, axis=-1, keepdims=True)
                p = jnp.exp(sc - m)
                den = jnp.sum(p, axis=-1, keepdims=True)
                o_ref[rows, sl] = _dot(p, vcat, "nn") / den
                lse_ref[rows, _stat(h)] = jnp.broadcast_to(m + jnp.log(den), (ATT_BLK, 128))

    cur, prev, _ = _attn_specs(nb)
    stat, _, _ = _attn_specs(nb, ATT_STATW)
    return pl.pallas_call(
        body, name="attn_fwd_" + tag, grid=(nb // ATT_T,), in_specs=[cur, cur, prev, cur, prev], out_specs=[cur, stat],
        out_shape=[jax.ShapeDtypeStruct((s, ATT_GROUPW), F32), jax.ShapeDtypeStruct((s, ATT_STATW), F32)],
        compiler_params=_cparams(1))(q, k, k, v, v)


def _attn_dq(tag, per_seq, q, k, v, do, lse, delta):
    s = q.shape[0]
    nb = s // ATT_BLK

    def body(q_ref, kc_ref, kp_ref, vc_ref, vp_ref, do_ref, lse_ref, dl_ref, dq_ref):
        bt = pl.program_id(0)
        for i in range(ATT_T):
            has_prev = lax.rem(bt * ATT_T + i, per_seq) > 0
            valid = _band_valid(jnp.where(has_prev, 0, ATT_BLK))
            rows = slice(i * ATT_BLK, (i + 1) * ATT_BLK)
            for h in range(ATT_HPG):
                sl = slice(h * ATT_HEAD_DIM, (h + 1) * ATT_HEAD_DIM)
                kcat = _window(kp_ref, kc_ref, i, sl)
                vcat = _window(vp_ref, vc_ref, i, sl)
                lse = jnp.concatenate([lse_ref[rows, _stat(h)]] * 2, axis=1)
                dlt = jnp.concatenate([dl_ref[rows, _stat(h)]] * 2, axis=1)
                sc = _dot(q_ref[rows, sl], kcat, "nt") * ATT_SCALE
                p = jnp.exp(jnp.where(valid, sc, NEG_INF) - lse)
                dp = _dot(do_ref[rows, sl], vcat, "nt")
                ds = p * (dp - dlt) * ATT_SCALE
                dq_ref[rows, sl] = _dot(ds, kcat, "nn")

    cur, prev, _ = _attn_specs(nb)
    stat, _, _ = _attn_specs(nb, ATT_STATW)
    return pl.pallas_call(
        body, name="attn_dq_" + tag, grid=(nb // ATT_T,), in_specs=[cur, cur, prev, cur, prev, cur, stat, stat],
        out_specs=cur, out_shape=jax.ShapeDtypeStruct((s, ATT_GROUPW), F32),
        compiler_params=_cparams(1))(q, k, k, v, v, do, lse, delta)


def _attn_dkv(tag, per_seq, q, k, v, do, lse_t, delta_t):
    s = q.shape[0]
    nb = s // ATT_BLK

    def body(k_ref, v_ref, qc_ref, qn_ref, doc_ref, don_ref, lc_ref, ln_ref, dc_ref, dn_ref, dk_ref, dv_ref):
        bt = pl.program_id(0)
        ki = lax.broadcasted_iota(jnp.int32, (ATT_BLK, 2 * ATT_BLK), 0)
        ci = lax.broadcasted_iota(jnp.int32, (ATT_BLK, 2 * ATT_BLK), 1)

        def pair(edge_ref, cur_ref, i, sl):
            if i == ATT_T - 1:
                return jnp.concatenate([cur_ref[i * ATT_BLK:(i + 1) * ATT_BLK, sl], edge_ref[:, sl]], axis=0)
            return cur_ref[i * ATT_BLK:(i + 2) * ATT_BLK, sl]

        def pair_row(edge_ref, cur_ref, i, h):
            if i == ATT_T - 1:
                row = jnp.concatenate([cur_ref[h:h + 1, i * ATT_BLK:(i + 1) * ATT_BLK], edge_ref[h:h + 1, :]], axis=1)
            else:
                row = cur_ref[h:h + 1, i * ATT_BLK:(i + 2) * ATT_BLK]
            return jnp.broadcast_to(row, (ATT_BLK, 2 * ATT_BLK))

        for i in range(ATT_T):
            b = bt * ATT_T + i
            next_uses = (b + 1 < nb) & (lax.rem(b + 1, per_seq) > 0)
            reach = jnp.where(next_uses, 0, 4 * ATT_BLK)
            valid = ((ci < ATT_BLK) & (ci >= ki)) | ((ci >= ATT_BLK) & (ki - ci + ATT_BLK >= reach))
            rows = slice(i * ATT_BLK, (i + 1) * ATT_BLK)
            for h in range(ATT_HPG):
                sl = slice(h * ATT_HEAD_DIM, (h + 1) * ATT_HEAD_DIM)
                qcat, docat = pair(qn_ref, qc_ref, i, sl), pair(don_ref, doc_ref, i, sl)
                sc = _dot(k_ref[rows, sl], qcat, "nt") * ATT_SCALE
                p = jnp.exp(jnp.where(valid, sc, NEG_INF) - pair_row(ln_ref, lc_ref, i, h))
                dv_ref[rows, sl] = _dot(p, docat, "nn")
                dp = _dot(v_ref[rows, sl], docat, "nt")
                ds = p * (dp - pair_row(dn_ref, dc_ref, i, h)) * ATT_SCALE
                dk_ref[rows, sl] = _dot(ds, qcat, "nn")

    cur, _, nxt = _attn_specs(nb)
    stat = pl.BlockSpec((8, ATT_ROWS), lambda b: (0, b))
    snxt = pl.BlockSpec((8, ATT_BLK), lambda b: (0, jnp.minimum((b + 1) * ATT_T, nb - 1)))
    return pl.pallas_call(
        body, name="attn_dkv_" + tag, grid=(nb // ATT_T,), in_specs=[cur, cur, cur, nxt, cur, nxt, stat, snxt, stat, snxt],
        out_specs=[cur, cur], out_shape=[jax.ShapeDtypeStruct((s, ATT_GROUPW), F32)] * 2,
        compiler_params=_cparams(1))(k, v, q, q, do, do, lse_t, lse_t, delta_t, delta_t)


def _xattn_probs(q, kh):
    sc = _dot(q, kh, "nt") * XATT_SCALE
    e = jnp.exp(sc - jnp.max(sc, axis=-1, keepdims=True))
    return e / jnp.sum(e, axis=-1, keepdims=True)


def _xattn_fwd(q, kv, tm=512):
    s = q.shape[0]
    tm = min(tm, s)

    def body(q_ref, kv_ref, o_ref):
        for h in range(XATT_HEADS):
            sl = slice(h * XATT_HEAD_DIM, (h + 1) * XATT_HEAD_DIM)
            vs = slice(D_MODEL + h * XATT_HEAD_DIM, D_MODEL + (h + 1) * XATT_HEAD_DIM)
            p = _xattn_probs(q_ref[:, sl], kv_ref[:, sl])
            o_ref[:, sl] = _dot(p, kv_ref[:, vs], "nn").astype(o_ref.dtype)

    return pl.pallas_call(
        body, name="xattn_fwd", grid=(s // tm,),
        in_specs=[pl.BlockSpec((tm, D_MODEL), lambda i: (i, 0)), pl.BlockSpec(kv.shape, lambda i: (0, 0))],
        out_specs=pl.BlockSpec((tm, D_MODEL), lambda i: (i, 0)),
        out_shape=jax.ShapeDtypeStruct((s, D_MODEL), MXU_DTYPE), compiler_params=_cparams(1))(q, kv)


def _xattn_bwd(q, kv, do, tm=512):
    s = q.shape[0]
    tm = min(tm, s)

    def body(q_ref, kv_ref, do_ref, dq_ref, dkv_ref):
        first = pl.program_id(0) == 0

        @pl.when(first)
        def _():
            dkv_ref[...] = jnp.zeros_like(dkv_ref)

        for h in range(XATT_HEADS):
            sl = slice(h * XATT_HEAD_DIM, (h + 1) * XATT_HEAD_DIM)
            vs = slice(D_MODEL + h * XATT_HEAD_DIM, D_MODEL + (h + 1) * XATT_HEAD_DIM)
            p = _xattn_probs(q_ref[:, sl], kv_ref[:, sl])
            dkv_ref[:, vs] += _dot(p, do_ref[:, sl], "tn")
            dp = _dot(do_ref[:, sl], kv_ref[:, vs], "nt")
            ds = p * (dp - jnp.sum(dp * p, axis=-1, keepdims=True)) * XATT_SCALE
            dq_ref[:, sl] = _dot(ds, kv_ref[:, sl], "nn").astype(dq_ref.dtype)
            dkv_ref[:, sl] += _dot(ds, q_ref[:, sl], "tn")

    row = pl.BlockSpec((tm, D_MODEL), lambda i: (i, 0))
    whole = pl.BlockSpec(kv.shape, lambda i: (0, 0))
    return pl.pallas_call(
        body, name="xattn_bwd", grid=(s // tm,), in_specs=[row, whole, row], out_specs=[row, whole],
        out_shape=[jax.ShapeDtypeStruct((s, D_MODEL), MXU_DTYPE), jax.ShapeDtypeStruct(kv.shape, F32)],
        compiler_params=_cparams(1))(q, kv, do)


def _ln(x, g, b):
    mu = jnp.mean(x, axis=-1, keepdims=True)
    xc = x - mu
    var = jnp.mean(jnp.square(xc), axis=-1, keepdims=True)
    return xc * lax.rsqrt(var + LN_EPS) * g + b


def _res_ln(h, o, g, b):
    return _ln(DEEPNORM_ALPHA * h + o, g, b)


def _gate(gs, ga, z1, z2, batt):
    return jax.nn.sigmoid(gs) * (z1 * jax.nn.sigmoid(z2)) + jax.nn.sigmoid(ga) * batt


def _rope_tables(pos, invf, m1, m2):
    ang = pos.astype(F32) * invf
    sin = jnp.sin(ang)
    return jnp.cos(ang), -sin * m1, sin * m2


def _rope(t, cos, s_up, s_dn):
    w = t.shape[-1]
    return t * cos + pltpu.roll(t, w - ROT_DIM // 2, 1) * s_up + pltpu.roll(t, ROT_DIM // 2, 1) * s_dn


def _rope_t(dt, cos, s_up, s_dn):
    w = dt.shape[-1]
    return dt * cos + pltpu.roll(dt * s_up, ROT_DIM // 2, 1) + pltpu.roll(dt * s_dn, w - ROT_DIM // 2, 1)


def _rope_consts():
    inv_freq = ROPE_THETA ** (-jnp.arange(0, ROT_DIM, 2, dtype=F32) / ROT_DIM)
    d = np.arange(ATT_GROUPW) % ATT_HEAD_DIM
    invf = jnp.where(d < ROT_DIM, inv_freq[d % (ROT_DIM // 2)], 0.0).reshape(1, ATT_GROUPW).astype(F32)
    m1 = jnp.asarray((d < ROT_DIM // 2).astype(np.float32)).reshape(1, ATT_GROUPW)
    m2 = jnp.asarray(((d >= ROT_DIM // 2) & (d < ROT_DIM)).astype(np.float32)).reshape(1, ATT_GROUPW)
    return invf, m1, m2


def _head_sum_matrix():
    d = np.arange(ATT_GROUPW) // ATT_HEAD_DIM
    s = np.arange(ATT_STATW) // 128
    return jnp.asarray((d[:, None] == s[None, :]).astype(np.float32))


def _adamw(w, g, m, v):
    m = ADAM_B1 * m + (1.0 - ADAM_B1) * g
    v = ADAM_B2 * v + (1.0 - ADAM_B2) * jnp.square(g)
    m_hat = m / (1.0 - ADAM_B1 ** ADAM_STEP)
    v_hat = v / (1.0 - ADAM_B2 ** ADAM_STEP)
    delta = -ADAM_LR * (m_hat / (jnp.sqrt(v_hat) + ADAM_EPS) + ADAM_WD * w)
    return delta, m, v


def _local_step(x, mem, pos, target, sp, ex):
    s = x.shape[0]
    al = DEEPNORM_ALPHA
    mx = MXU_DTYPE

    ex.gather_now(["w_in"])
    h0, h0b = _rowwise("ln_in", lambda x, g, b: (lambda h: (h, h))(_ln(x, g, b)), [x],
                       [sp["ln_in_g"], sp["ln_in_b"]], [(D_MODEL, F32), (D_MODEL, mx)])
    proj = _mm("proj", h0b, ex.weight("w_in"), "nn", bias=sp["b_in"],
               carry=ex.gather_carry(["w_glu", "w_att_up", "w_mix_out", "w_xq", "w_xkv"]))

    ldt = jnp.repeat(sp["ssm_log_dt"].reshape(SSM_GROUPS), SSM_STATE).reshape(N_STATE, 1)
    are, aim = sp["ssm_a_re"].reshape(N_STATE, 1), sp["ssm_a_im"].reshape(N_STATE, 1)
    bre, bim = sp["ssm_b_re"].reshape(N_STATE, SSM_GROUP), sp["ssm_b_im"].reshape(N_STATE, SSM_GROUP)
    abr, abi, bbr, bbi = _ssm_disc_fwd(ldt, are, aim, bre, bim)
    a_re, a_im = abr.reshape(1, N_STATE), abi.reshape(1, N_STATE)
    bexp = jnp.concatenate([_blockdiag_b(bbr), _blockdiag_b(bbi)], axis=2).astype(mx)
    cexp = jnp.concatenate([_blockdiag_c(sp["ssm_c_re"].reshape(SSM_GROUPS, SSM_GROUP, SSM_STATE)),
                            -_blockdiag_c(sp["ssm_c_im"].reshape(SSM_GROUPS, SSM_GROUP, SSM_STATE))],
                           axis=1).astype(mx)
    u_p = _time_perm(proj[:, :SSM_WIDTH])
    b12, c12 = _split_by_scan_block(bexp, 2), _split_by_scan_block(cexp, 1)
    h_re, h_im, y_p = _ssm_scan("ssm_scan_fwd", u_p, b12, c12, a_re, a_im, sp["ssm_d"], reverse=False,
                                carry=ex.gather_carry(["w_ff1", "w_ff2"]))
    y = _time_unperm(y_p)
    ygb, = _rowwise("gelu", lambda y: jax.nn.gelu(y), [y], [], [(SSM_WIDTH, mx)])
    z = _mm("glu", ygb, ex.weight("w_glu"), "nn", bias=sp["b_glu"], carry=ex.gather_carry(["w_xo"]))

    invf, m1, m2 = _rope_consts()

    def rope_fwd(pos, q0, q1, q2, k0, k1, k2, v0, v1, v2, invf, m1, m2):
        tabs = _rope_tables(pos, invf, m1, m2)
        return tuple(_rope(t, *tabs) for t in (q0, q1, q2, k0, k1, k2)) + (v0, v1, v2)

    qkv_cols = [(proj, ATT_GROUPW, 3 + i) for i in range(9)]
    qkv = _rowwise("rope", rope_fwd, [pos] + qkv_cols, [invf, m1, m2], [(ATT_GROUPW, mx)] * 9)
    n_blocks = s // ATT_BLK
    groups = [(str(g), n_blocks // d, d) for g, d in enumerate(DILATIONS)]
    q_d = [_dilate(qkv[g], d) for g, d in enumerate(DILATIONS)]
    k_d = [_dilate(qkv[3 + g], d) for g, d in enumerate(DILATIONS)]
    v_d = [_dilate(qkv[6 + g], d) for g, d in enumerate(DILATIONS)]
    o_g, l_g = [], []
    for g, (tag, per_seq, d) in enumerate(groups):
        o, lse = _attn_fwd(tag, per_seq, q_d[g], k_d[g], v_d[g])
        o_g.append(_undilate(o, d))
        l_g.append(_undilate(lse, d))

    def merge(o0, o1, o2, l0, l1, l2):
        m = jnp.maximum(jnp.maximum(l0, l1), l2)
        e0, e1, e2 = jnp.exp(l0 - m), jnp.exp(l1 - m), jnp.exp(l2 - m)
        tot = e0 + e1 + e2

        def per_dim(e):
            w = e / tot
            return jnp.concatenate([w[:, h * 128:h * 128 + ATT_HEAD_DIM] for h in range(ATT_HPG)], axis=1)

        att = per_dim(e0) * o0 + per_dim(e1) * o1 + per_dim(e2) * o2
        lse = m + jnp.log(tot)
        return att, att, lse, _stat_rows(lse)

    att, attb, lse_tot, lse_tot_t = _rowwise("attn_merge", merge, o_g + l_g, [],
                                             [(ATT_GROUPW, F32), (ATT_GROUPW, mx), (ATT_STATW, F32)], touts=[(8, F32)])
    batt = _mm("att_up", attb, ex.weight("w_att_up"), "nn")

    gate_rows = [(proj, D_MODEL, 3), (proj, D_MODEL, 4), (z, D_MODEL, 0), (z, D_MODEL, 1), batt]
    mixedb, = _rowwise("gate", _gate, gate_rows, [], [(D_MODEL, mx)])
    o1 = _mm("mix_out", mixedb, ex.weight("w_mix_out"), "nn", bias=sp["b_mix_out"])
    h1, h1b = _rowwise("ln1", lambda h, o, g, b: (lambda r: (r, r))(_res_ln(h, o, g, b)), [h0, o1],
                       [sp["ln1_g"], sp["ln1_b"]], [(D_MODEL, F32), (D_MODEL, mx)])

    qx = _mm("xq", h1b, ex.weight("w_xq"), "nn", out_dtypes=(mx,))
    kvx = _mm("xkv", mem, ex.weight("w_xkv"), "nn", out_dtypes=(mx,))
    oxb = _xattn_fwd(qx, kvx)
    o2 = _mm("xo", oxb, ex.weight("w_xo"), "nn")
    h2, h2b = _rowwise("ln2", lambda h, o, g, b: (lambda r: (r, r))(_res_ln(h, o, g, b)), [h1, o2],
                       [sp["ln2_g"], sp["ln2_b"]], [(D_MODEL, F32), (D_MODEL, mx)])

    a_ff, fb = _mm("ff1", h2b, ex.weight("w_ff1"), "nn", bias=sp["b_ff1"],
                   epilogue=lambda r: (r, jnp.square(jnp.maximum(r, 0.0))), out_dtypes=(F32, mx))
    o3 = _mm("ff2", fb, ex.weight("w_ff2"), "nn", bias=sp["b_ff2"])

    def loss_bwd(h2, o3, tgt, g, b):
        def f(h2, o3, g, b):
            h3 = _res_ln(h2, o3, g, b)
            return 0.5 * jnp.sum(jnp.mean(jnp.square(h3 - tgt), axis=-1))

        loss, vjp = jax.vjp(f, h2, o3, g, b)
        _, dr, dg, db = vjp(jnp.ones((), F32))
        return dr, dr, dg, db, _colsum(dr), jnp.full((1, 128), loss, F32)

    dr3, dr3b, g_ln3_g, g_ln3_b, g_b_ff2, loss = _rowwise(
        "loss_ln3_bwd", loss_bwd, [h2, o3, target], [sp["ln3_g"], sp["ln3_b"]],
        [(D_MODEL, F32), (D_MODEL, mx)], [D_MODEL, D_MODEL, D_MODEL, 128])

    dab = _mm("ff2_dx", dr3b, ex.weight("w_ff2"), "nt", extras=(a_ff,),
              epilogue=lambda r, a: (r * (2.0 * jnp.maximum(a, 0.0)),), out_dtypes=(mx,))
    ex.grad("w_ff2", _mm("ff2_dw", fb, dr3b, "tn"))
    g_b_ff1, = _rowwise("ff1_db", lambda v: (_colsum(v),), [dab], [], [], [D_FF])
    ex.grad("w_ff1", _mm("ff1_dw", h2b, dab, "tn", carry=ex.reduce_carry(["w_ff2"])))
    dh2 = _mm("ff1_dx", dab, ex.weight("w_ff1"), "nt", extras=(dr3,), epilogue=lambda r, d: (r + al * d,),
              carry=ex.reduce_carry(["w_ff1"]))

    def ln_bwd(h, o, dout, g, b):
        _, vjp = jax.vjp(_res_ln, h, o, g, b)
        _, dr, dg, db = vjp(dout)
        return dr, dr, dg, db, _colsum(dr)

    dr2, dr2b, g_ln2_g, g_ln2_b, _ = _rowwise(
        "ln2_bwd", ln_bwd, [h1, o2, dh2], [sp["ln2_g"], sp["ln2_b"]],
        [(D_MODEL, F32), (D_MODEL, mx)], [D_MODEL, D_MODEL, D_MODEL])
    ex.grad("w_xo", _mm("xo_dw", oxb, dr2b, "tn"))
    doxb = _mm("xo_dx", dr2b, ex.weight("w_xo"), "nt", out_dtypes=(mx,), carry=ex.reduce_carry(["w_xo"]))
    dqxb, dkvx = _xattn_bwd(qx, kvx, doxb)
    ex.grad("w_xq", _mm("xq_dw", h1b, dqxb, "tn"))
    dh1 = _mm("xq_dx", dqxb, ex.weight("w_xq"), "nt", extras=(dr2,), epilogue=lambda r, d: (r + al * d,))
    ex.grad("w_xkv", _mm("xkv_dw", mem, dkvx, "tn"))

    dr1, dr1b, g_ln1_g, g_ln1_b, g_b_mix = _rowwise(
        "ln1_bwd", ln_bwd, [h0, o1, dh1], [sp["ln1_g"], sp["ln1_b"]],
        [(D_MODEL, F32), (D_MODEL, mx)], [D_MODEL, D_MODEL, D_MODEL])
    ex.grad("w_mix_out", _mm("mix_dw", mixedb, dr1b, "tn", carry=ex.reduce_carry(["w_xq"])))
    dmixed = _mm("mix_dx", dr1b, ex.weight("w_mix_out"), "nt", carry=ex.reduce_carry(["w_xkv"]))

    def gate_bwd(gs, ga, z1, z2, batt, dm):
        _, vjp = jax.vjp(_gate, gs, ga, z1, z2, batt)
        dgs, dga, dz1, dz2, dbatt = vjp(dm)
        dz = jnp.concatenate([dz1, dz2], axis=-1)
        return dgs, dga, dz, dbatt, _colsum(dz)

    dgsb, dgab, dzb, dbattb, g_b_glu = _rowwise(
        "gate_bwd", gate_bwd, gate_rows + [dmixed], [],
        [(D_MODEL, mx), (D_MODEL, mx), (2 * D_MODEL, mx), (D_MODEL, mx)], [2 * D_MODEL])
    ex.grad("w_att_up", _mm("att_up_dw", attb, dbattb, "tn", carry=ex.reduce_carry(["w_mix_out"])))
    datt = _mm("att_up_dx", dbattb, ex.weight("w_att_up"), "nt")

    def att_delta(datt, att, hs):
        dl = jnp.dot(datt * att, hs, precision=lax.Precision.HIGHEST, preferred_element_type=F32)
        return datt, dl, _stat_rows(dl)

    dattb, delta, delta_t = _rowwise("attn_delta", att_delta, [datt, att], [_head_sum_matrix()],
                                     [(ATT_GROUPW, mx), (ATT_STATW, F32)], touts=[(8, F32)])
    dq_g, dk_g, dv_g = [], [], []
    for g, (tag, per_seq, d) in enumerate(groups):
        do_d, lt_d, dl_d = _dilate(dattb, d), _dilate(lse_tot, d), _dilate(delta, d)
        dq_g.append(_undilate(_attn_dq(tag, per_seq, q_d[g], k_d[g], v_d[g], do_d, lt_d, dl_d), d))
        dk, dv = _attn_dkv(tag, per_seq, q_d[g], k_d[g], v_d[g], do_d, _dilate_rows(lse_tot_t, d), _dilate_rows(delta_t, d))
        dk_g.append(_undilate(dk, d))
        dv_g.append(_undilate(dv, d))
    dqkv = dq_g + dk_g + dv_g

    def rope_bwd(pos, q0, q1, q2, k0, k1, k2, v0, v1, v2, invf, m1, m2):
        tabs = _rope_tables(pos, invf, m1, m2)
        return jnp.concatenate([_rope_t(t, *tabs) for t in (q0, q1, q2, k0, k1, k2)] + [v0, v1, v2], axis=-1)

    dqkvb, = _rowwise("rope_bwd", rope_bwd, [pos] + dqkv, [invf, m1, m2], [(9 * ATT_GROUPW, mx)])

    ex.grad("w_glu", _mm("glu_dw", ygb, dzb, "tn", carry=ex.reduce_carry(["w_att_up"])))
    dyg = _mm("glu_dx", dzb, ex.weight("w_glu"), "nt")

    def gelu_bwd(y, dyg):
        _, vjp = jax.vjp(jax.nn.gelu, y)
        return vjp(dyg)[0]

    dy, = _rowwise("gelu_bwd", gelu_bwd, [y, dyg], [], [(SSM_WIDTH, F32)])
    dy_p = _time_perm(dy)
    g_cexp = _ssm_wgrad("ssm_dc", dy_p, h_re, h_im, expand=False)
    s_re, s_im, du_p = _ssm_scan("ssm_scan_bwd", dy_p, c12, b12, a_re, a_im, sp["ssm_d"], reverse=True,
                                 carry=ex.reduce_carry(["w_glu"]))
    d_abr, d_abi = _ssm_da(s_re, s_im, h_re, h_im)
    g_bexp = _ssm_wgrad("ssm_db", u_p, s_re, s_im, expand=True)
    g_ssm_d, = _rowwise("ssm_dd", lambda a, b: (_colsum(a * b),), [dy_p, u_p], [], [], [SSM_WIDTH])
    g_ldt, g_are, g_aim, g_bre, g_bim = _ssm_disc_bwd(
        ldt, are, aim, bre, bim, d_abr.reshape(N_STATE, 1), d_abi.reshape(N_STATE, 1),
        _diag_of_b(g_bexp[:, :, :CH_N]), _diag_of_b(g_bexp[:, :, CH_N:]))
    g_c_re = _diag_of_c(g_cexp[:, :CH_N, :])
    g_c_im = -_diag_of_c(g_cexp[:, CH_N:, :])
    dub = _time_unperm(du_p).astype(mx)

    dprojb = jnp.concatenate([dub, dqkvb, dgsb, dgab], axis=-1)
    g_b_in, = _rowwise("in_db", lambda v: (_colsum(v),), [dprojb], [], [], [IN_COLS])
    ex.grad("w_in", _mm("in_dw", h0b, dprojb, "tn"))
    dh0 = _mm("in_dx", dprojb, ex.weight("w_in"), "nt", extras=(dr1,), epilogue=lambda r, d: (r + al * d,),
              carry=ex.reduce_carry(["w_in"]))

    def ln_in_bwd(x, dout, g, b):
        _, vjp = jax.vjp(_ln, x, g, b)
        return vjp(dout)

    dx, g_ln_in_g, g_ln_in_b = _rowwise("ln_in_bwd", ln_in_bwd, [x, dh0], [sp["ln_in_g"], sp["ln_in_b"]],
                                        [(D_MODEL, F32)], [D_MODEL, D_MODEL])

    small = {"ln_in_g": g_ln_in_g, "ln_in_b": g_ln_in_b, "b_in": g_b_in, "ssm_log_dt": g_ldt, "ssm_a_re": g_are,
             "ssm_a_im": g_aim, "ssm_b_re": g_bre, "ssm_b_im": g_bim, "ssm_c_re": g_c_re, "ssm_c_im": g_c_im,
             "ssm_d": g_ssm_d, "b_glu": g_b_glu, "b_mix_out": g_b_mix, "ln1_g": g_ln1_g, "ln1_b": g_ln1_b,
             "ln2_g": g_ln2_g, "ln2_b": g_ln2_b, "b_ff1": g_b_ff1, "b_ff2": g_b_ff2, "ln3_g": g_ln3_g,
             "ln3_b": g_ln3_b}
    return loss, dx, small


def _piece_shape(k, n, axis):
    return (k // 2, n // 4) if axis == 1 else (k // 8, n)


def _aligned(v, m):
    return v if isinstance(v, int) else pl.multiple_of(v, m)


def _full_piece(ref, k, n, axis, chip, half):
    pr, pc = _piece_shape(k, n, axis)
    if axis == 1:
        return ref.at[pl.ds(_aligned(half * pr, 8), pr), pl.ds(_aligned(chip * pc, 128), pc)]
    return ref.at[pl.ds(_aligned(chip * (2 * pr) + half * pr, 8), pr), :]


def _full_shard(ref, k, n, axis, chip):
    if axis == 1:
        return ref.at[:, pl.ds(_aligned(chip * (n // 4), 128), n // 4)]
    return ref.at[pl.ds(_aligned(chip * (k // 4), 8), k // 4), :]


def _shard_piece(ref, k, n, axis, half):
    pr, _ = _piece_shape(k, n, axis)
    return ref.at[pl.ds(_aligned(half * pr, 8), pr), :]


def _mesh_pos():
    x, y, c = lax.axis_index("x"), lax.axis_index("y"), lax.axis_index("c")
    other_chips = [(1 - x, y), (x, 1 - y), (1 - x, 1 - y)]
    return x, y, c, other_chips


def _remote(src, dst, send_sem, recv_sem, dev):
    return pltpu.make_async_remote_copy(src_ref=src, dst_ref=dst, send_sem=send_sem, recv_sem=recv_sem,
                                        device_id=dev, device_id_type=MESH)


def _placed(name, fn, n_steps, where, ins, out_sds, out_block, out_index):
    def body(w_ref, *refs):
        o_ref = refs[-1]
        o_ref[...] = fn(*[r[...] for r in refs[:-1]]).astype(o_ref.dtype)

    grid_spec = pltpu.PrefetchScalarGridSpec(
        num_scalar_prefetch=1, grid=(n_steps,), in_specs=[pl.BlockSpec(bs, idx) for _, bs, idx in ins],
        out_specs=pl.BlockSpec(out_block, out_index))
    return pl.pallas_call(body, name=name, grid_spec=grid_spec, out_shape=out_sds,
                          compiler_params=_cparams(1))(where, *[a for a, _, _ in ins])


def _gather_copies(widx):
    geo = [BIG[i][1:] for i in widx]

    def ici(full, wi, j, chip, send_sems, recv_sems, c, dev):
        k, n, ax = geo[wi]
        piece = _full_piece(full[wi], k, n, ax, chip, c)
        return _remote(piece, piece, send_sems.at[wi * 6 + j], recv_sems.at[wi * 6 + j], dev)

    def d2d(full, wi, j, chip, half, send_sems, recv_sems, sib):
        k, n, ax = geo[wi]
        piece = _full_piece(full[wi], k, n, ax, chip, half)
        return _remote(piece, piece, send_sems.at[wi * 6 + 3 + j], recv_sems.at[wi * 6 + 3 + j], sib)

    def start(_, full, send_sems, recv_sems):
        x, y, c, chips = _mesh_pos()
        for wi in range(len(geo)):
            for j, (qx, qy) in enumerate(chips):
                ici(full, wi, j, 2 * x + y, send_sems, recv_sems, c, (qx, qy, c)).start()

    def finish(_, full, send_sems, recv_sems):
        x, y, c, chips = _mesh_pos()
        sib = (x, y, 1 - c)
        for wi in range(len(geo)):
            for j, (qx, qy) in enumerate(chips):
                ici(full, wi, j, 2 * qx + qy, send_sems, recv_sems, c, (qx, qy, c)).wait_recv()
                d2d(full, wi, j, 2 * qx + qy, c, send_sems, recv_sems, sib).start()
        for wi in range(len(geo)):
            for j, (qx, qy) in enumerate(chips):
                d2d(full, wi, j, 2 * qx + qy, 1 - c, send_sems, recv_sems, sib).wait_recv()
        for wi in range(len(geo)):
            for j, (qx, qy) in enumerate(chips):
                ici(full, wi, j, 2 * x + y, send_sems, recv_sems, c, (qx, qy, c)).wait_send()
                d2d(full, wi, j, 2 * qx + qy, c, send_sems, recv_sems, sib).wait_send()

    return start, finish, 6 * len(geo)


def _gather_weights(tag, fulls, widx):
    nw = len(widx)
    start, finish, n_sems = _gather_copies(widx)

    def body(*refs):
        full = refs[nw:2 * nw]
        start(None, full, *refs[2 * nw:])
        finish(None, full, *refs[2 * nw:])

    return pl.pallas_call(
        body, name="gather_weights_" + tag, in_specs=[HBM_SPEC] * nw, out_specs=[HBM_SPEC] * nw,
        out_shape=[jax.ShapeDtypeStruct(f.shape, f.dtype) for f in fulls],
        input_output_aliases={i: i for i in range(nw)},
        scratch_shapes=[pltpu.SemaphoreType.DMA((n_sems,)), pltpu.SemaphoreType.DMA((n_sems,))])(*fulls)


def _reduce_swap_halves(tag, grads, widx):
    nw = len(widx)
    geo = [BIG[i][1:] for i in widx]

    def body(*refs):
        g, got = refs[:nw], refs[nw:2 * nw]
        send_sems, recv_sems = refs[2 * nw:]
        x, y, c, _ = _mesh_pos()
        sib = (x, y, 1 - c)
        cps = []
        for wi, (k, n, ax) in enumerate(geo):
            for q in range(4):
                cp = _remote(_full_piece(g[wi], k, n, ax, q, 1 - c), got[wi].at[q],
                             send_sems.at[wi * 4 + q], recv_sems.at[wi * 4 + q], sib)
                cp.start()
                cps.append(cp)
        for cp in cps:
            cp.wait()

    return pl.pallas_call(
        body, name="reduce_swap_halves_" + tag, in_specs=[HBM_SPEC] * nw, out_specs=[HBM_SPEC] * nw,
        out_shape=[jax.ShapeDtypeStruct((4,) + _piece_shape(k, n, ax), F32) for k, n, ax in geo],
        scratch_shapes=[pltpu.SemaphoreType.DMA((4 * nw,)), pltpu.SemaphoreType.DMA((4 * nw,))])(*grads)


def _owner_copies(nw):
    def copies(p, out, send_sems, recv_sems):
        x, y, c, chips = _mesh_pos()
        return [_remote(p[wi].at[2 * qx + qy], out[wi].at[j], send_sems.at[wi * 3 + j], recv_sems.at[wi * 3 + j],
                        (qx, qy, c)) for wi in range(nw) for j, (qx, qy) in enumerate(chips)]

    def start(p, out, send_sems, recv_sems):
        for cp in copies(p, out, send_sems, recv_sems):
            cp.start()

    def finish(p, out, send_sems, recv_sems):
        for cp in copies(p, out, send_sems, recv_sems):
            cp.wait()

    return start, finish, 3 * nw


def _share_with_sibling(shards):
    nw = len(BIG)

    def body(*refs):
        out = refs[nw:2 * nw]
        send_sems, recv_sems = refs[2 * nw:]
        x, y, c, _ = _mesh_pos()
        sib = (x, y, 1 - c)
        cps = []
        for wi, (_, k, n, ax) in enumerate(BIG):
            mine = _shard_piece(out[wi], k, n, ax, c)
            cp = _remote(mine, mine, send_sems.at[wi], recv_sems.at[wi], sib)
            cp.start()
            cps.append(cp)
        for wi, (_, k, n, ax) in enumerate(BIG):
            piece = _shard_piece(out[wi], k, n, ax, 1 - c)
            _remote(piece, piece, send_sems.at[wi], recv_sems.at[wi], sib).wait_recv()
        for cp in cps:
            cp.wait_send()

    return pl.pallas_call(
        body, name="share_with_sibling", in_specs=[HBM_SPEC] * nw, out_specs=[HBM_SPEC] * nw,
        out_shape=[jax.ShapeDtypeStruct(sh.shape, sh.dtype) for sh in shards],
        input_output_aliases={i: i for i in range(nw)},
        scratch_shapes=[pltpu.SemaphoreType.DMA((nw,)), pltpu.SemaphoreType.DMA((nw,))])(*shards)


def _allreduce_small(v):
    r = v.shape[0]
    rh = r // 2
    assert rh % 8 == 0

    def body(v_ref, o_ref, sib_buf, chip_buf, send_sems, recv_sems):
        x, y, c, chips = _mesh_pos()
        me = 2 * x + y
        sib = (x, y, 1 - c)
        mine = pl.ds(pl.multiple_of(c * rh, 8), rh)
        other = pl.ds(pl.multiple_of((1 - c) * rh, 8), rh)
        swap = _remote(v_ref.at[other], sib_buf, send_sems.at[0], recv_sems.at[0], sib)
        swap.start()
        swap.wait()
        chip_buf[me] = v_ref[mine, :] + sib_buf[...]
        cps = []
        for j, (qx, qy) in enumerate(chips):
            cp = _remote(chip_buf.at[me], chip_buf.at[me], send_sems.at[1 + j], recv_sems.at[1 + j], (qx, qy, c))
            cp.start()
            cps.append(cp)
        for j, (qx, qy) in enumerate(chips):
            slot = chip_buf.at[2 * qx + qy]
            _remote(slot, slot, send_sems.at[1 + j], recv_sems.at[1 + j], (qx, qy, c)).wait_recv()
        for cp in cps:
            cp.wait_send()
        o_ref[mine, :] = ((chip_buf[0] + chip_buf[1]) + chip_buf[2]) + chip_buf[3]
        back = _remote(o_ref.at[mine], o_ref.at[mine], send_sems.at[4], recv_sems.at[4], sib)
        back.start()
        _remote(o_ref.at[other], o_ref.at[other], send_sems.at[4], recv_sems.at[4], sib).wait_recv()
        back.wait_send()

    return pl.pallas_call(
        body, name="allreduce_small", in_specs=[VMEM_SPEC], out_specs=VMEM_SPEC,
        out_shape=jax.ShapeDtypeStruct((r, 128), F32),
        scratch_shapes=[pltpu.VMEM((rh, 128), F32), pltpu.VMEM((4, rh, 128), F32),
                        pltpu.SemaphoreType.DMA((5,)), pltpu.SemaphoreType.DMA((5,))],
        compiler_params=pltpu.CompilerParams(vmem_limit_bytes=VMEM_LIMIT))(v)


def _as2d(a):
    a = a.reshape((-1, a.shape[-1])) if a.ndim > 1 else a.reshape(1, -1)
    return a


def _adamw_small(quads):
    n = len(quads)

    def body(*refs):
        for i in range(n):
            w, g, m, v = (r[...] for r in refs[4 * i:4 * i + 4])
            for ref, val in zip(refs[4 * n + 3 * i:4 * n + 3 * i + 3], _adamw(w, g, m, v)):
                ref[...] = val

    return pl.pallas_call(
        body, name="adamw_small", in_specs=[VMEM_SPEC] * (4 * n), out_specs=[VMEM_SPEC] * (3 * n),
        out_shape=[jax.ShapeDtypeStruct(q[0].shape, F32) for q in quads for _ in range(3)],
        compiler_params=pltpu.CompilerParams(vmem_limit_bytes=VMEM_LIMIT))(*[a for q in quads for a in q])


def _where():
    return jnp.stack([2 * lax.axis_index("x") + lax.axis_index("y"), lax.axis_index("c")]).astype(jnp.int32)


_BIG_INDEX = {name: i for i, (name, _, _, _) in enumerate(BIG)}


class _LocalWeights:
    def __init__(self, weights):
        self.weights, self.grads = weights, {}

    def gather_now(self, names):
        pass

    def gather_carry(self, names):
        return None

    def weight(self, name):
        return self.weights[name]

    def grad(self, name, g):
        self.grads[name] = g

    def reduce_carry(self, names):
        return None


class _Exchange:
    def __init__(self, inputs, where):
        self.inputs, self.where = inputs, where
        self.full, self.ready = {}, set()
        self.parts, self.landed, self.geom = {}, {}, {}
        for name, k, n, ax in BIG:
            w2 = inputs[name][0]
            rs, cs = w2.shape
            tm = _tile(rs, 512)
            steps = rs // tm
            if ax == 1:
                blk, idx = (tm, cs), lambda i, w: (i, w[0])
            else:
                blk, idx = (tm, n), functools.partial(lambda i, w, steps: (w[0] * steps + i, 0), steps=steps)
            self.full[name] = _placed("cast_" + name, lambda w: w, steps, where, [(w2, (tm, cs), lambda i, w: (i, 0))],
                                      jax.ShapeDtypeStruct((k, n), MXU_DTYPE), blk, idx)

    def _gathered(self, names, outs):
        for name, o in zip(names, outs):
            self.full[name] = o
            self.ready.add(name)

    def gather_now(self, names):
        self._gathered(names, _gather_weights(names[0], [self.full[n] for n in names], [_BIG_INDEX[n] for n in names]))

    def gather_carry(self, names):
        start, finish, n_sems = _gather_copies([_BIG_INDEX[n] for n in names])
        return _Carry([self.full[n] for n in names], list(range(len(names))), n_sems, start, finish,
                      functools.partial(self._gathered, names))

    def weight(self, name):
        assert name in self.ready, name
        return self.full[name]

    def grad(self, name, g):
        i = _BIG_INDEX[name]
        _, k, n, ax = BIG[i]
        got, = _reduce_swap_halves(name, [g], [i])
        pr, pc = _piece_shape(k, n, ax)
        tm = _tile(pr, 512)
        spp = pr // tm
        self.geom[name] = (pr, pc, tm, spp)
        if ax == 1:
            g_idx = functools.partial(lambda i, w, spp: (w[1] * spp + i % spp, i // spp), spp=spp)
        else:
            g_idx = functools.partial(lambda i, w, spp: ((i // spp) * 2 * spp + w[1] * spp + i % spp, 0), spp=spp)
        self.parts[name] = _placed(
            "pair_sum_" + name, lambda a, b: a + b, 4 * spp, self.where,
            [(g, (tm, pc), g_idx), (got.reshape(4 * pr, pc), (tm, pc), lambda i, w: (i, 0))],
            jax.ShapeDtypeStruct((4 * pr, pc), BF16), (tm, pc), lambda i, w: (i, 0)).reshape(4, pr, pc)

    def _landed(self, names, outs):
        for name, o in zip(names, outs):
            self.landed[name] = o

    def reduce_carry(self, names):
        start, finish, n_sems = _owner_copies(len(names))
        parts = [self.parts[n] for n in names]
        outs = [jax.ShapeDtypeStruct((3,) + p.shape[1:], p.dtype) for p in parts]
        return _Carry(parts, outs, n_sems, start, finish, functools.partial(self._landed, names))

    def finish(self):
        halves = []
        for name, _, _, _ in BIG:
            pr, pc, tm, spp = self.geom[name]
            ins = [(self.parts[name], (None, tm, pc), lambda i, w: (w[0], i, 0))]
            ins += [(self.landed[name], (None, tm, pc), functools.partial(lambda i, w, j: (j, i, 0), j=j))
                    for j in range(3)]
            halves.append(_placed("chip_sum_" + name,
                                  lambda a, b, c, d: ((a.astype(F32) + b.astype(F32)) + c.astype(F32)) + d.astype(F32),
                                  spp, self.where, ins, jax.ShapeDtypeStruct(self.inputs[name].shape[1:], F32), (tm, pc),
                                  functools.partial(lambda i, w, spp: (w[1] * spp + i, 0), spp=spp)))
        return dict(zip([b[0] for b in BIG], _share_with_sibling(halves)))


def _step(inputs):
    x, mem, positions, target = inputs["x"][0], inputs["mem"][0], inputs["positions"], inputs["loss_target"][0]
    pos = positions.reshape(-1, 1)
    ex = _Exchange(inputs, _where())
    sp = {name: _as2d(inputs[name]) for name in SMALL}
    memb, = _rowwise("cast_mem", lambda m: (m,), [mem], [], [(D_MODEL, MXU_DTYPE)])

    loss, dx, gsmall = _local_step(x, memb, pos, target, sp, ex)
    gshard = ex.finish()

    out = {}
    for name, _, _, _ in BIG:
        w2, m2, v2 = inputs[name][0], inputs["m_" + name][0], inputs["v_" + name][0]
        n = w2.shape[1]
        d, nm, nv = _rowwise("adamw_" + name, _adamw, [w2, gshard[name], m2, v2], [], [(n, F32)] * 3, tm=128)
        lead = inputs[name].shape
        out[name] = (gshard[name].reshape(lead), d.reshape(lead), nm.reshape(lead), nv.reshape(lead))

    def tiles(a):
        flat = a.reshape(-1)
        n = -(-flat.shape[0] // 1024) * 1024
        return jnp.pad(flat, (0, n - flat.shape[0])).reshape(n // 128, 128)

    pieces = [tiles(loss[:, :1])] + [tiles(gsmall[name]) for name in SMALL]
    if sum(p.shape[0] for p in pieces) % 16:
        pieces.append(jnp.zeros((8, 128), F32))
    red = _allreduce_small(jnp.concatenate(pieces, axis=0))
    loss_total = red[0, 0]
    grads, off = {}, pieces[0].shape[0]
    for name, p in zip(SMALL, pieces[1:]):
        shp = _as2d(inputs[name]).shape
        grads[name] = red[off:off + p.shape[0]].reshape(-1)[:shp[0] * shp[1]].reshape(shp)
        off += p.shape[0]
    upd = _adamw_small([(_as2d(inputs[n]), grads[n], _as2d(inputs["m_" + n]), _as2d(inputs["v_" + n])) for n in SMALL])
    for i, name in enumerate(SMALL):
        shp = inputs[name].shape
        out[name] = (grads[name].reshape(shp),) + tuple(t.reshape(shp) for t in upd[3 * i:3 * i + 3])
    return loss_total, dx.reshape(inputs["x"].shape), out


_ARG_NAMES = (("x", "mem", "positions") + WEIGHT_ORDER + ("loss_target",) + tuple("m_" + n for n in WEIGHT_ORDER)
              + tuple("v_" + n for n in WEIGHT_ORDER))


def kernel(x, mem, positions, ln_in_g, ln_in_b, w_in, b_in, ssm_log_dt, ssm_a_re, ssm_a_im, ssm_b_re, ssm_b_im, ssm_c_re, ssm_c_im, ssm_d, w_glu, b_glu, w_att_up, w_mix_out, b_mix_out, ln1_g, ln1_b, w_xq, w_xkv, w_xo, ln2_g, ln2_b, w_ff1, b_ff1, w_ff2, b_ff2, ln3_g, ln3_b, loss_target, m_ln_in_g, m_ln_in_b, m_w_in, m_b_in, m_ssm_log_dt, m_ssm_a_re, m_ssm_a_im, m_ssm_b_re, m_ssm_b_im, m_ssm_c_re, m_ssm_c_im, m_ssm_d, m_w_glu, m_b_glu, m_w_att_up, m_w_mix_out, m_b_mix_out, m_ln1_g, m_ln1_b, m_w_xq, m_w_xkv, m_w_xo, m_ln2_g, m_ln2_b, m_w_ff1, m_b_ff1, m_w_ff2, m_b_ff2, m_ln3_g, m_ln3_b, v_ln_in_g, v_ln_in_b, v_w_in, v_b_in, v_ssm_log_dt, v_ssm_a_re, v_ssm_a_im, v_ssm_b_re, v_ssm_b_im, v_ssm_c_re, v_ssm_c_im, v_ssm_d, v_w_glu, v_b_glu, v_w_att_up, v_w_mix_out, v_b_mix_out, v_ln1_g, v_ln1_b, v_w_xq, v_w_xkv, v_w_xo, v_ln2_g, v_ln2_b, v_w_ff1, v_b_ff1, v_w_ff2, v_b_ff2, v_ln3_g, v_ln3_b):
    args = (x, mem, positions, ln_in_g, ln_in_b, w_in, b_in, ssm_log_dt, ssm_a_re, ssm_a_im, ssm_b_re, ssm_b_im, ssm_c_re, ssm_c_im, ssm_d, w_glu, b_glu, w_att_up, w_mix_out, b_mix_out, ln1_g, ln1_b, w_xq, w_xkv, w_xo, ln2_g, ln2_b, w_ff1, b_ff1, w_ff2, b_ff2, ln3_g, ln3_b, loss_target, m_ln_in_g, m_ln_in_b, m_w_in, m_b_in, m_ssm_log_dt, m_ssm_a_re, m_ssm_a_im, m_ssm_b_re, m_ssm_b_im, m_ssm_c_re, m_ssm_c_im, m_ssm_d, m_w_glu, m_b_glu, m_w_att_up, m_w_mix_out, m_b_mix_out, m_ln1_g, m_ln1_b, m_w_xq, m_w_xkv, m_w_xo, m_ln2_g, m_ln2_b, m_w_ff1, m_b_ff1, m_w_ff2, m_b_ff2, m_ln3_g, m_ln3_b, v_ln_in_g, v_ln_in_b, v_w_in, v_b_in, v_ssm_log_dt, v_ssm_a_re, v_ssm_a_im, v_ssm_b_re, v_ssm_b_im, v_ssm_c_re, v_ssm_c_im, v_ssm_d, v_w_glu, v_b_glu, v_w_att_up, v_w_mix_out, v_b_mix_out, v_ln1_g, v_ln1_b, v_w_xq, v_w_xkv, v_w_xo, v_ln2_g, v_ln2_b, v_w_ff1, v_b_ff1, v_w_ff2, v_b_ff2, v_ln3_g, v_ln3_b)
    assert len(args) == len(_ARG_NAMES)
    inputs = dict(zip(_ARG_NAMES, args))
    loss, dx, out = _step(inputs)
    res = [loss, dx]
    for k in range(4):
        res += [out[name][k] for name in WEIGHT_ORDER]
    return tuple(res)
```

```python
import functools
import math

import numpy as np
import jax
import jax.numpy as jnp
from jax import lax
from jax.experimental import pallas as pl
from jax.experimental.pallas import tpu as pltpu

F32 = jnp.float32
BF16 = jnp.bfloat16
MXU_DTYPE = jnp.bfloat16

D_MODEL = 1024
SSM_GROUP = 16
SSM_WIDTH = 768
SSM_GROUPS = 48
SSM_STATE = 64
N_STATE = SSM_GROUPS * SSM_STATE
SSM_CHUNKS = 6
CH_W = 128
CH_N = 512
ATT_HEAD_DIM = 64
ATT_HPG = 4
ATT_GROUPW = ATT_HPG * ATT_HEAD_DIM
DILATIONS = (1, 4, 16)
ATT_BLK = 128
ATT_SCALE = ATT_HEAD_DIM ** -0.5
ROT_DIM = 16
ROPE_THETA = 500000.0
XATT_HEADS = 4
XATT_HEAD_DIM = 256
XATT_SCALE = XATT_HEAD_DIM ** -0.5
D_FF = 4096
IN_COLS = 5120
DEEPNORM_ALPHA = 2.0 ** 0.25
LN_EPS = 1e-5
NEG_INF = -1e30
ADAM_LR = 0.001
ADAM_B1 = 0.9
ADAM_B2 = 0.999
ADAM_EPS = 1e-08
ADAM_WD = 0.01
ADAM_STEP = 10

N_SEG = 32
VMEM_LIMIT = 48 * 1024 * 1024
MESH = pl.DeviceIdType.MESH
HBM_SPEC = pl.BlockSpec(memory_space=pltpu.HBM)
VMEM_SPEC = pl.BlockSpec(memory_space=pltpu.VMEM)

BIG = (("w_in", 1024, 5120, 1), ("w_glu", 768, 2048, 1), ("w_att_up", 256, 1024, 1),
       ("w_mix_out", 1024, 1024, 0), ("w_xq", 1024, 1024, 0), ("w_xkv", 1024, 2048, 1),
       ("w_xo", 1024, 1024, 0), ("w_ff1", 1024, 4096, 1), ("w_ff2", 4096, 1024, 0))
SMALL = ("ln_in_g", "ln_in_b", "b_in", "ssm_log_dt", "ssm_a_re", "ssm_a_im", "ssm_b_re", "ssm_b_im",
         "ssm_c_re", "ssm_c_im", "ssm_d", "b_glu", "b_mix_out", "ln1_g", "ln1_b", "ln2_g", "ln2_b",
         "b_ff1", "b_ff2", "ln3_g", "ln3_b")
WEIGHT_ORDER = ("ln_in_g", "ln_in_b", "w_in", "b_in", "ssm_log_dt", "ssm_a_re", "ssm_a_im", "ssm_b_re",
                "ssm_b_im", "ssm_c_re", "ssm_c_im", "ssm_d", "w_glu", "b_glu", "w_att_up", "w_mix_out",
                "b_mix_out", "ln1_g", "ln1_b", "w_xq", "w_xkv", "w_xo", "ln2_g", "ln2_b", "w_ff1", "b_ff1",
                "w_ff2", "b_ff2", "ln3_g", "ln3_b")


def _cparams(n_axes):
    return pltpu.CompilerParams(dimension_semantics=("arbitrary",) * n_axes, vmem_limit_bytes=VMEM_LIMIT)


class _Carry:
    def __init__(self, ins, outs, n_sems, start, finish, done):
        self.ins, self.outs, self.n_sems, self.start, self.finish, self.done = ins, outs, n_sems, start, finish, done


def _call(name, body, grid, in_specs, out_specs, out_shape, args, scratch_shapes=(), carry=None):
    in_specs, out_specs, out_shape = list(in_specs), list(out_specs), list(out_shape)
    params = _cparams(len(grid))
    if carry is None:
        return pl.pallas_call(body, name=name, grid=grid, in_specs=in_specs, out_specs=out_specs, out_shape=out_shape,
                              scratch_shapes=list(scratch_shapes), compiler_params=params)(*args)
    n_in, n_out, n_ci, n_co = len(in_specs), len(out_specs), len(carry.ins), len(carry.outs)
    n_scr = len(scratch_shapes)

    def wrapped(*refs):
        ins, c_in = refs[:n_in], refs[n_in:n_in + n_ci]
        outs, c_out = refs[n_in + n_ci:n_in + n_ci + n_out], refs[n_in + n_ci + n_out:n_in + n_ci + n_out + n_co]
        scratch = refs[n_in + n_ci + n_out + n_co:n_in + n_ci + n_out + n_co + n_scr]
        send_sems, recv_sems = refs[-2:]
        ids = [pl.program_id(a) for a in range(len(grid))]
        first = functools.reduce(jnp.logical_and, [i == 0 for i in ids])
        last = functools.reduce(jnp.logical_and, [i == g - 1 for i, g in zip(ids, grid)])

        @pl.when(first)
        def _():
            carry.start(c_in, c_out, send_sems, recv_sems)

        body(*ins, *outs, *scratch)

        @pl.when(last)
        def _():
            carry.finish(c_in, c_out, send_sems, recv_sems)

    c_shapes = [jax.ShapeDtypeStruct(carry.ins[o].shape, carry.ins[o].dtype) if isinstance(o, int) else o
                for o in carry.outs]
    aliases = {n_in + o: n_out + i for i, o in enumerate(carry.outs) if isinstance(o, int)}
    res = pl.pallas_call(
        wrapped, name=name, grid=grid, in_specs=in_specs + [HBM_SPEC] * n_ci, out_specs=out_specs + [HBM_SPEC] * n_co,
        out_shape=out_shape + c_shapes, input_output_aliases=aliases,
        scratch_shapes=list(scratch_shapes) + [pltpu.SemaphoreType.DMA((carry.n_sems,))] * 2,
        compiler_params=params)(*args, *carry.ins)
    carry.done(res[n_out:])
    return res[:n_out]


def _rowwise(name, fn, rows, consts, outs, reds=(), tm=256, touts=()):
    n_rows = (rows[0][0] if isinstance(rows[0], tuple) else rows[0]).shape[-2]
    tm = min(tm, n_rows)
    assert n_rows % tm == 0, (name, n_rows, tm)
    specs, args = [], []
    for r in rows:
        if isinstance(r, tuple) and len(r) == 3:
            arr, width, cb = r
            specs.append(pl.BlockSpec((tm, width), functools.partial(lambda i, cb: (i, cb), cb=cb)))
        elif isinstance(r, tuple):
            arr, slot = r
            specs.append(pl.BlockSpec((None, tm, arr.shape[2]), functools.partial(lambda i, s: (s, i, 0), s=slot)))
        else:
            arr = r
            specs.append(pl.BlockSpec((tm, arr.shape[1]), lambda i: (i, 0)))
        args.append(arr)
        assert arr.shape[-2] == n_rows, (name, arr.shape, n_rows)
    for cst in consts:
        specs.append(pl.BlockSpec(cst.shape, lambda i: (0, 0)))
        args.append(cst)
    n_r, n_c, n_o, n_d = len(rows), len(consts), len(outs) + len(touts), len(reds)
    out_shape = [jax.ShapeDtypeStruct((n_rows, c), dt) for c, dt in outs]
    out_specs = [pl.BlockSpec((tm, c), lambda i: (i, 0)) for c, _ in outs]
    out_shape += [jax.ShapeDtypeStruct((r, n_rows), dt) for r, dt in touts]
    out_specs += [pl.BlockSpec((r, tm), lambda i: (0, i)) for r, _ in touts]
    out_shape += [jax.ShapeDtypeStruct((1, c), F32) for c in reds]
    out_specs += [pl.BlockSpec((1, c), lambda i: (0, 0)) for c in reds]

    def body(*refs):
        ins = [r[...] for r in refs[:n_r + n_c]]
        o_refs = refs[n_r + n_c:n_r + n_c + n_o]
        d_refs = refs[n_r + n_c + n_o:]
        res = fn(*ins)
        res = res if isinstance(res, (tuple, list)) else (res,)
        assert len(res) == n_o + n_d, (name, len(res))
        for ref, val in zip(o_refs, res[:n_o]):
            ref[...] = val.astype(ref.dtype)
        first = pl.program_id(0) == 0
        for ref, val in zip(d_refs, res[n_o:]):
            @pl.when(first)
            def _(ref=ref, val=val):
                ref[...] = val

            @pl.when(jnp.logical_not(first))
            def _(ref=ref, val=val):
                ref[...] += val

    res = pl.pallas_call(body, name=name, grid=(n_rows // tm,), in_specs=specs, out_specs=out_specs,
                         out_shape=out_shape, compiler_params=_cparams(1))(*args)
    return res


def _colsum(v):
    return jnp.sum(v.astype(F32), axis=0, keepdims=True)


_DIMS = {"nn": (((1,), (0,)), ((), ())), "nt": (((1,), (1,)), ((), ())), "tn": (((0,), (0,)), ((), ()))}


def _tile(dim, want):
    if dim <= want:
        return dim
    return max(t for t in range(128, want + 1, 128) if dim % t == 0)


def _dot(a, b, mode):
    return lax.dot_general(a.astype(MXU_DTYPE), b.astype(MXU_DTYPE), _DIMS[mode], preferred_element_type=F32)


def _mm(name, a, b, mode, *, bias=None, extras=(), epilogue=None, out_dtypes=(F32,), tm=1024, tn=1024, tk=1024,
        carry=None):
    if mode == "nn":
        (m, k), (_, n) = a.shape, b.shape
    elif mode == "nt":
        (m, k), (n, _) = a.shape, b.shape
    else:
        (k, m), (_, n) = a.shape, b.shape
    tn = _tile(n, tn)
    if mode != "tn":
        tk = k if k <= 1024 else tk
    tk = _tile(k, tk)
    nk = k // tk

    def vmem_bytes(rows):
        blocks = rows * tk * a.dtype.itemsize + tk * tn * b.dtype.itemsize
        blocks += sum(rows * tn * e.dtype.itemsize for e in extras)
        blocks += sum(rows * tn * jnp.dtype(dt).itemsize for dt in out_dtypes)
        return 2 * blocks + (rows * tn * 4 if nk > 1 else 0)

    tm = _tile(m, tm if mode == "tn" else 2 * tm)
    while vmem_bytes(tm) > 3 * VMEM_LIMIT // 4 and tm % 256 == 0:
        tm //= 2
    assert m % tm == 0 and n % tn == 0 and k % tk == 0, (name, m, n, k)
    a_spec = {"nn": pl.BlockSpec((tm, tk), lambda i, j, kk: (i, kk)),
              "nt": pl.BlockSpec((tm, tk), lambda i, j, kk: (i, kk)),
              "tn": pl.BlockSpec((tk, tm), lambda i, j, kk: (kk, i))}[mode]
    b_spec = {"nn": pl.BlockSpec((tk, tn), lambda i, j, kk: (kk, j)),
              "nt": pl.BlockSpec((tn, tk), lambda i, j, kk: (j, kk)),
              "tn": pl.BlockSpec((tk, tn), lambda i, j, kk: (kk, j))}[mode]
    specs, args = [a_spec, b_spec], [a, b]
    if bias is not None:
        specs.append(pl.BlockSpec((1, tn), lambda i, j, kk: (0, j)))
        args.append(bias)
    for e in extras:
        specs.append(pl.BlockSpec((tm, tn), lambda i, j, kk: (i, j)))
        args.append(e)
    n_e, n_o = len(extras), len(out_dtypes)
    has_bias = bias is not None

    def body(*refs):
        a_ref, b_ref = refs[0], refs[1]
        pos = 2
        bias_ref = refs[pos] if has_bias else None
        pos += int(has_bias)
        e_refs = refs[pos:pos + n_e]
        o_refs = refs[pos + n_e:pos + n_e + n_o]
        acc_ref = refs[pos + n_e + n_o] if nk > 1 else None
        part = _dot(a_ref[...], b_ref[...], mode)

        def finish(r):
            if has_bias:
                r = r + bias_ref[...]
            res = epilogue(r, *[e[...] for e in e_refs]) if epilogue is not None else (r,)
            for ref, val in zip(o_refs, res):
                ref[...] = val.astype(ref.dtype)

        if nk == 1:
            finish(part)
        else:
            kk = pl.program_id(2)

            @pl.when(kk == 0)
            def _():
                acc_ref[...] = part

            @pl.when(kk > 0)
            def _():
                acc_ref[...] += part

            @pl.when(kk == nk - 1)
            def _():
                finish(acc_ref[...])

    res = _call(name, body, (m // tm, n // tn, nk), specs,
                [pl.BlockSpec((tm, tn), lambda i, j, kk: (i, j)) for _ in out_dtypes],
                [jax.ShapeDtypeStruct((m, n), dt) for dt in out_dtypes], args,
                scratch_shapes=[pltpu.VMEM((tm, tn), F32)] if nk > 1 else [], carry=carry)
    return res[0] if n_o == 1 else res


def _ssm_wgrad(name, chan, st_re, st_im, expand, tk=512):
    s = chan.shape[0]
    tk = min(tk, s)
    nk = s // tk
    oshape = (CH_W, 2 * CH_N) if expand else (2 * CH_N, CH_W)

    def body(c_ref, re_ref, im_ref, o_ref):
        c = c_ref[...]
        if expand:
            part = jnp.concatenate([_dot(c, re_ref[...], "tn"), _dot(c, im_ref[...], "tn")], axis=1)
        else:
            part = jnp.concatenate([_dot(re_ref[...], c, "tn"), _dot(im_ref[...], c, "tn")], axis=0)
        kk = pl.program_id(1)

        @pl.when(kk == 0)
        def _():
            o_ref[...] = part

        @pl.when(kk > 0)
        def _():
            o_ref[...] += part

    return pl.pallas_call(
        body, name=name, grid=(SSM_CHUNKS, nk),
        in_specs=[pl.BlockSpec((tk, CH_W), lambda j, kk: (kk, j)), pl.BlockSpec((tk, CH_N), lambda j, kk: (kk, j)),
                  pl.BlockSpec((tk, CH_N), lambda j, kk: (kk, j))],
        out_specs=pl.BlockSpec((None,) + oshape, lambda j, kk: (j, 0, 0)),
        out_shape=jax.ShapeDtypeStruct((SSM_CHUNKS,) + oshape, F32),
        compiler_params=_cparams(2))(chan, st_re, st_im)


SCAN_LB = 256


def _split_by_scan_block(mat, axis):
    halves = []
    for l in range(CH_N // SCAN_LB):
        re = lax.slice_in_dim(mat, l * SCAN_LB, (l + 1) * SCAN_LB, axis=axis)
        im = lax.slice_in_dim(mat, CH_N + l * SCAN_LB, CH_N + (l + 1) * SCAN_LB, axis=axis)
        halves.append(jnp.concatenate([re, im], axis=axis))
    return jnp.stack(halves, axis=1).reshape((-1,) + halves[0].shape[1:])


def _ssm_scan(name, chan, expand12, contract12, a_re, a_im, d_row, reverse, carry=None):
    s = chan.shape[0]
    seg_len = s // N_SEG
    n_sq = int(math.log2(seg_len))
    assert 2 ** n_sq == seg_len
    rb = min(512, s)
    per_chunk = CH_N // SCAN_LB

    def body(are_ref, aim_ref, ch_ref, e_ref, k_ref, d_ref, hre_ref, him_ref, o_ref, wre_ref, wim_ref, ere, eim, cre, cim):
        e_mat, k_mat = e_ref[...], k_ref[...]
        for r in range(s // rb):
            rows = slice(r * rb, (r + 1) * rb)
            c = ch_ref[rows, :]
            if reverse:
                wre_ref[rows, :] = _dot(c, e_mat[:SCAN_LB], "nt")
                wim_ref[rows, :] = _dot(c, e_mat[SCAN_LB:], "nt")
            else:
                wre_ref[rows, :] = _dot(c, e_mat[:, :SCAN_LB], "nn")
                wim_ref[rows, :] = _dot(c, e_mat[:, SCAN_LB:], "nn")

        ar1 = are_ref[...]
        ai1 = -aim_ref[...] if reverse else aim_ref[...]
        ar = jnp.broadcast_to(ar1, (N_SEG, SCAN_LB))
        ai = jnp.broadcast_to(ai1, (N_SEG, SCAN_LB))

        def rows_of(k):
            kk = seg_len - 1 - k if reverse else k
            return pl.ds(pl.multiple_of(kk * N_SEG, N_SEG), N_SEG)

        def local(k, carry):
            hr, hi = carry
            rows = rows_of(k)
            nr = ar * hr - ai * hi + wre_ref[rows, :]
            ni = ar * hi + ai * hr + wim_ref[rows, :]
            hre_ref[rows, :] = nr
            him_ref[rows, :] = ni
            return nr, ni

        zero = jnp.zeros((N_SEG, SCAN_LB), F32)
        er, ei = lax.fori_loop(0, seg_len, local, (zero, zero))
        ere[...] = er
        eim[...] = ei
        pr, pi = ar1, ai1
        for _ in range(n_sq):
            pr, pi = pr * pr - pi * pi, 2.0 * pr * pi
        cr = jnp.zeros((1, SCAN_LB), F32)
        ci = jnp.zeros((1, SCAN_LB), F32)
        for jj in range(N_SEG):
            j = N_SEG - 1 - jj if reverse else jj
            cre[j:j + 1, :] = cr
            cim[j:j + 1, :] = ci
            er_j, ei_j = ere[j:j + 1, :], eim[j:j + 1, :]
            cr, ci = pr * cr - pi * ci + er_j, pr * ci + pi * cr + ei_j
        c_r, c_i = cre[...], cim[...]

        def fix(k, carry):
            qr, qi = carry
            rows = rows_of(k)
            hre_ref[rows, :] = hre_ref[rows, :] + (qr * c_r - qi * c_i)
            him_ref[rows, :] = him_ref[rows, :] + (qr * c_i + qi * c_r)
            return qr * ar - qi * ai, qr * ai + qi * ar

        lax.fori_loop(0, seg_len, fix, (ar, ai))

        first_of_chunk = lax.rem(pl.program_id(0), per_chunk) == 0
        for r in range(s // rb):
            rows = slice(r * rb, (r + 1) * rb)
            if reverse:
                part = (_dot(hre_ref[rows, :], k_mat[:, :SCAN_LB], "nt")
                        + _dot(him_ref[rows, :], k_mat[:, SCAN_LB:], "nt"))
            else:
                part = _dot(hre_ref[rows, :], k_mat[:SCAN_LB], "nn") + _dot(him_ref[rows, :], k_mat[SCAN_LB:], "nn")

            @pl.when(first_of_chunk)
            def _(rows=rows, part=part):
                o_ref[rows, :] = part + d_ref[...] * ch_ref[rows, :]

            @pl.when(jnp.logical_not(first_of_chunk))
            def _(rows=rows, part=part):
                o_ref[rows, :] += part

    nblk = N_STATE // SCAN_LB
    blk = pl.BlockSpec((s, SCAN_LB), lambda b: (0, b))
    row = pl.BlockSpec((1, SCAN_LB), lambda b: (0, b))
    chan_blk = pl.BlockSpec((s, CH_W), lambda b: (0, b // per_chunk))
    res = _call(name, body, (nblk,),
                [row, row, chan_blk, pl.BlockSpec((None,) + expand12.shape[1:], lambda b: (b, 0, 0)),
                 pl.BlockSpec((None,) + contract12.shape[1:], lambda b: (b, 0, 0)),
                 pl.BlockSpec((1, CH_W), lambda b: (0, b // per_chunk))],
                [blk, blk, chan_blk],
                [jax.ShapeDtypeStruct((s, N_STATE), F32)] * 2 + [jax.ShapeDtypeStruct((s, SSM_WIDTH), F32)],
                (a_re, a_im, chan, expand12, contract12, d_row),
                scratch_shapes=[pltpu.VMEM((s, SCAN_LB), F32)] * 2 + [pltpu.VMEM((N_SEG, SCAN_LB), F32)] * 4, carry=carry)
    return res[0], res[1], res[2]


def _ssm_da(g_re, g_im, h_re, h_im):
    s = g_re.shape[0]
    seg_len = s // N_SEG

    def body(gre_ref, gim_ref, hre_ref, him_ref, dre_ref, dim_ref):
        def rows_of(k):
            return pl.ds(pl.multiple_of(k * N_SEG, N_SEG), N_SEG)

        def step(k, carry):
            sr, si = carry
            gr, gi = gre_ref[rows_of(k), :], gim_ref[rows_of(k), :]
            pr, pi = hre_ref[rows_of(k - 1), :], him_ref[rows_of(k - 1), :]
            return sr + gr * pr + gi * pi, si + gi * pr - gr * pi

        zero = jnp.zeros((N_SEG, SCAN_LB), F32)
        sr, si = lax.fori_loop(1, seg_len, step, (zero, zero))
        last = pl.ds((seg_len - 1) * N_SEG, N_SEG)
        first_row = lax.broadcasted_iota(jnp.int32, (N_SEG, SCAN_LB), 0) == 0
        pr = jnp.where(first_row, 0.0, pltpu.roll(hre_ref[last, :], 1, 0))
        pi = jnp.where(first_row, 0.0, pltpu.roll(him_ref[last, :], 1, 0))
        gr, gi = gre_ref[pl.ds(0, N_SEG), :], gim_ref[pl.ds(0, N_SEG), :]
        sr = sr + gr * pr + gi * pi
        si = si + gi * pr - gr * pi
        dre_ref[...] = jnp.sum(sr, axis=0, keepdims=True)
        dim_ref[...] = jnp.sum(si, axis=0, keepdims=True)

    nblk = N_STATE // SCAN_LB
    blk = pl.BlockSpec((s, SCAN_LB), lambda b: (0, b))
    row = pl.BlockSpec((1, SCAN_LB), lambda b: (0, b))
    return pl.pallas_call(
        body, name="ssm_da", grid=(nblk,), in_specs=[blk] * 4, out_specs=[row, row],
        out_shape=[jax.ShapeDtypeStruct((1, N_STATE), F32)] * 2,
        compiler_params=_cparams(1))(g_re, g_im, h_re, h_im)


def _disc(ldt, are, aim, bre, bim):
    dt = jnp.exp(ldt)
    mag = jnp.exp(are * dt)
    abr = mag * jnp.cos(aim * dt)
    abi = mag * jnp.sin(aim * dt)
    den = jnp.square(are) + jnp.square(aim)
    nr = abr - 1.0
    fre = (nr * are + abi * aim) / den
    fim = (abi * are - nr * aim) / den
    return abr, abi, fre * bre - fim * bim, fre * bim + fim * bre


def _ssm_disc_fwd(ldt, are, aim, bre, bim):
    def body(l_ref, ar_ref, ai_ref, br_ref, bi_ref, o0, o1, o2, o3):
        res = _disc(l_ref[...], ar_ref[...], ai_ref[...], br_ref[...], bi_ref[...])
        for ref, val in zip((o0, o1, o2, o3), res):
            ref[...] = val

    col = jax.ShapeDtypeStruct((N_STATE, 1), F32)
    mat = jax.ShapeDtypeStruct((N_STATE, SSM_GROUP), F32)
    return pl.pallas_call(body, name="ssm_disc_fwd", out_shape=[col, col, mat, mat],
                          in_specs=[VMEM_SPEC] * 5, out_specs=[VMEM_SPEC] * 4)(ldt, are, aim, bre, bim)


def _ssm_disc_bwd(ldt, are, aim, bre, bim, d_abr, d_abi, d_bbr, d_bbi):
    def body(l_ref, ar_ref, ai_ref, br_ref, bi_ref, c0, c1, c2, c3, g_ldt, g_are, g_aim, g_bre, g_bim):
        _, vjp = jax.vjp(_disc, l_ref[...], ar_ref[...], ai_ref[...], br_ref[...], bi_ref[...])
        dl, dar, dai, dbr, dbi = vjp((c0[...], c1[...], c2[...], c3[...]))
        state = lax.broadcasted_iota(jnp.int32, (N_STATE, SSM_GROUPS), 0)
        group = lax.broadcasted_iota(jnp.int32, (N_STATE, SSM_GROUPS), 1)
        pick = jnp.right_shift(state, 6) == group
        g_ldt[...] = jnp.sum(jnp.where(pick, dl, 0.0), axis=0, keepdims=True)
        g_are[...] = dar
        g_aim[...] = dai
        g_bre[...] = dbr
        g_bim[...] = dbi

    col = jax.ShapeDtypeStruct((N_STATE, 1), F32)
    mat = jax.ShapeDtypeStruct((N_STATE, SSM_GROUP), F32)
    return pl.pallas_call(body, name="ssm_disc_bwd",
                          out_shape=[jax.ShapeDtypeStruct((1, SSM_GROUPS), F32), col, col, mat, mat],
                          in_specs=[VMEM_SPEC] * 9, out_specs=[VMEM_SPEC] * 5,
                          compiler_params=pltpu.CompilerParams(vmem_limit_bytes=VMEM_LIMIT))(
        ldt, are, aim, bre, bim, d_abr, d_abi, d_bbr, d_bbi)


_EYE8 = np.eye(8, dtype=np.float32)


def _blockdiag_b(bb):
    t = bb.reshape(SSM_CHUNKS, 8, SSM_STATE, SSM_GROUP).transpose(0, 1, 3, 2)
    return jnp.einsum("igcn,gh->igchn", t, _EYE8).reshape(SSM_CHUNKS, CH_W, CH_N)


def _diag_of_b(m):
    t = jnp.einsum("igchn,gh->igcn", m.reshape(SSM_CHUNKS, 8, SSM_GROUP, 8, SSM_STATE), _EYE8)
    return t.transpose(0, 1, 3, 2).reshape(N_STATE, SSM_GROUP)


def _blockdiag_c(c):
    t = c.reshape(SSM_CHUNKS, 8, SSM_GROUP, SSM_STATE).transpose(0, 1, 3, 2)
    return jnp.einsum("ignc,gh->ignhc", t, _EYE8).reshape(SSM_CHUNKS, CH_N, CH_W)


def _diag_of_c(m):
    t = jnp.einsum("ignhc,gh->ignc", m.reshape(SSM_CHUNKS, 8, SSM_STATE, 8, SSM_GROUP), _EYE8)
    return t.transpose(0, 1, 3, 2).reshape(SSM_GROUPS, SSM_GROUP, SSM_STATE)


def _time_perm(a):
    s, c = a.shape
    return a.reshape(N_SEG, s // N_SEG, c).transpose(1, 0, 2).reshape(s, c)


def _time_unperm(a):
    s, c = a.shape
    return a.reshape(s // N_SEG, N_SEG, c).transpose(1, 0, 2).reshape(s, c)


def _dilate(a, d):
    s, c = a.shape
    return a if d == 1 else a.reshape(s // d, d, c).transpose(1, 0, 2).reshape(s, c)


def _undilate(a, d):
    s, c = a.shape
    return a if d == 1 else a.reshape(d, s // d, c).transpose(1, 0, 2).reshape(s, c)


def _dilate_rows(a, d):
    r, s = a.shape
    return a if d == 1 else a.reshape(r, s // d, d).transpose(0, 2, 1).reshape(r, s)


ATT_T = 4
ATT_ROWS = ATT_T * ATT_BLK


def _window(prev_ref, cur_ref, i, sl):
    if i == 0:
        return jnp.concatenate([prev_ref[:, sl], cur_ref[0:ATT_BLK, sl]], axis=0)
    return cur_ref[(i - 1) * ATT_BLK:(i + 1) * ATT_BLK, sl]


def _band_valid(first_key):
    qi = lax.broadcasted_iota(jnp.int32, (ATT_BLK, 2 * ATT_BLK), 0)
    ki = lax.broadcasted_iota(jnp.int32, (ATT_BLK, 2 * ATT_BLK), 1)
    steps = qi + ATT_BLK - ki
    return (steps >= 0) & (steps <= ATT_BLK) & (ki >= first_key)


ATT_STATW = ATT_HPG * 128


def _stat(h):
    return slice(h * 128, (h + 1) * 128)


def _stat_rows(stat):
    n = stat.shape[0]
    heads = [stat[:, _stat(h)].T[0:1, :] for h in range(ATT_HPG)]
    return jnp.concatenate(heads + [jnp.zeros((8 - ATT_HPG, n), stat.dtype)], axis=0)


def _attn_specs(nb, width=ATT_GROUPW):
    cur = pl.BlockSpec((ATT_ROWS, width), lambda b: (b, 0))
    prev = pl.BlockSpec((ATT_BLK, width), lambda b: (jnp.maximum(b * ATT_T - 1, 0), 0))
    nxt = pl.BlockSpec((ATT_BLK, width), lambda b: (jnp.minimum((b + 1) * ATT_T, nb - 1), 0))
    return cur, prev, nxt


def _attn_fwd(tag, per_seq, q, k, v):
    s = q.shape[0]
    nb = s // ATT_BLK

    def body(q_ref, kc_ref, kp_ref, vc_ref, vp_ref, o_ref, lse_ref):
        bt = pl.program_id(0)
        for i in range(ATT_T):
            has_prev = lax.rem(bt * ATT_T + i, per_seq) > 0
            valid = _band_valid(jnp.where(has_prev, 0, ATT_BLK))
            rows = slice(i * ATT_BLK, (i + 1) * ATT_BLK)
            for h in range(ATT_HPG):
                sl = slice(h * ATT_HEAD_DIM, (h + 1) * ATT_HEAD_DIM)
                kcat = _window(kp_ref, kc_ref, i, sl)
                vcat = _window(vp_ref, vc_ref, i, sl)
                sc = _dot(q_ref[rows, sl], kcat, "nt") * ATT_SCALE
                sc = jnp.where(valid, sc, NEG_INF)
                m = jnp.max(sc, axis=-1, keepdims=True)
                p = jnp.exp(sc - m)
                den = jnp.sum(p, axis=-1, keepdims=True)
                o_ref[rows, sl] = _dot(p, vcat, "nn") / den
                lse_ref[rows, _stat(h)] = jnp.broadcast_to(m + jnp.log(den), (ATT_BLK, 128))

    cur, prev, _ = _attn_specs(nb)
    stat, _, _ = _attn_specs(nb, ATT_STATW)
    return pl.pallas_call(
        body, name="attn_fwd_" + tag, grid=(nb // ATT_T,), in_specs=[cur, cur, prev, cur, prev], out_specs=[cur, stat],
        out_shape=[jax.ShapeDtypeStruct((s, ATT_GROUPW), F32), jax.ShapeDtypeStruct((s, ATT_STATW), F32)],
        compiler_params=_cparams(1))(q, k, k, v, v)


def _attn_dq(tag, per_seq, q, k, v, do, lse, delta):
    s = q.shape[0]
    nb = s // ATT_BLK

    def body(q_ref, kc_ref, kp_ref, vc_ref, vp_ref, do_ref, lse_ref, dl_ref, dq_ref):
        bt = pl.program_id(0)
        for i in range(ATT_T):
            has_prev = lax.rem(bt * ATT_T + i, per_seq) > 0
            valid = _band_valid(jnp.where(has_prev, 0, ATT_BLK))
            rows = slice(i * ATT_BLK, (i + 1) * ATT_BLK)
            for h in range(ATT_HPG):
                sl = slice(h * ATT_HEAD_DIM, (h + 1) * ATT_HEAD_DIM)
                kcat = _window(kp_ref, kc_ref, i, sl)
                vcat = _window(vp_ref, vc_ref, i, sl)
                lse = jnp.concatenate([lse_ref[rows, _stat(h)]] * 2, axis=1)
                dlt = jnp.concatenate([dl_ref[rows, _stat(h)]] * 2, axis=1)
                sc = _dot(q_ref[rows, sl], kcat, "nt") * ATT_SCALE
                p = jnp.exp(jnp.where(valid, sc, NEG_INF) - lse)
                dp = _dot(do_ref[rows, sl], vcat, "nt")
                ds = p * (dp - dlt) * ATT_SCALE
                dq_ref[rows, sl] = _dot(ds, kcat, "nn")

    cur, prev, _ = _attn_specs(nb)
    stat, _, _ = _attn_specs(nb, ATT_STATW)
    return pl.pallas_call(
        body, name="attn_dq_" + tag, grid=(nb // ATT_T,), in_specs=[cur, cur, prev, cur, prev, cur, stat, stat],
        out_specs=cur, out_shape=jax.ShapeDtypeStruct((s, ATT_GROUPW), F32),
        compiler_params=_cparams(1))(q, k, k, v, v, do, lse, delta)


def _attn_dkv(tag, per_seq, q, k, v, do, lse_t, delta_t):
    s = q.shape[0]
    nb = s // ATT_BLK

    def body(k_ref, v_ref, qc_ref, qn_ref, doc_ref, don_ref, lc_ref, ln_ref, dc_ref, dn_ref, dk_ref, dv_ref):
        bt = pl.program_id(0)
        ki = lax.broadcasted_iota(jnp.int32, (ATT_BLK, 2 * ATT_BLK), 0)
        ci = lax.broadcasted_iota(jnp.int32, (ATT_BLK, 2 * ATT_BLK), 1)

        def pair(edge_ref, cur_ref, i, sl):
            if i == ATT_T - 1:
                return jnp.concatenate([cur_ref[i * ATT_BLK:(i + 1) * ATT_BLK, sl], edge_ref[:, sl]], axis=0)
            return cur_ref[i * ATT_BLK:(i + 2) * ATT_BLK, sl]

        def pair_row(edge_ref, cur_ref, i, h):
            if i == ATT_T - 1:
                row = jnp.concatenate([cur_ref[h:h + 1, i * ATT_BLK:(i + 1) * ATT_BLK], edge_ref[h:h + 1, :]], axis=1)
            else:
                row = cur_ref[h:h + 1, i * ATT_BLK:(i + 2) * ATT_BLK]
            return jnp.broadcast_to(row, (ATT_BLK, 2 * ATT_BLK))

        for i in range(ATT_T):
            b = bt * ATT_T + i
            next_uses = (b + 1 < nb) & (lax.rem(b + 1, per_seq) > 0)
            reach = jnp.where(next_uses, 0, 4 * ATT_BLK)
            valid = ((ci < ATT_BLK) & (ci >= ki)) | ((ci >= ATT_BLK) & (ki - ci + ATT_BLK >= reach))
            rows = slice(i * ATT_BLK, (i + 1) * ATT_BLK)
            for h in range(ATT_HPG):
                sl = slice(h * ATT_HEAD_DIM, (h + 1) * ATT_HEAD_DIM)
                qcat, docat = pair(qn_ref, qc_ref, i, sl), pair(don_ref, doc_ref, i, sl)
                sc = _dot(k_ref[rows, sl], qcat, "nt") * ATT_SCALE
                p = jnp.exp(jnp.where(valid, sc, NEG_INF) - pair_row(ln_ref, lc_ref, i, h))
                dv_ref[rows, sl] = _dot(p, docat, "nn")
                dp = _dot(v_ref[rows, sl], docat, "nt")
                ds = p * (dp - pair_row(dn_ref, dc_ref, i, h)) * ATT_SCALE
                dk_ref[rows, sl] = _dot(ds, qcat, "nn")

    cur, _, nxt = _attn_specs(nb)
    stat = pl.BlockSpec((8, ATT_ROWS), lambda b: (0, b))
    snxt = pl.BlockSpec((8, ATT_BLK), lambda b: (0, jnp.minimum((b + 1) * ATT_T, nb - 1)))
    return pl.pallas_call(
        body, name="attn_dkv_" + tag, grid=(nb // ATT_T,), in_specs=[cur, cur, cur, nxt, cur, nxt, stat, snxt, stat, snxt],
        out_specs=[cur, cur], out_shape=[jax.ShapeDtypeStruct((s, ATT_GROUPW), F32)] * 2,
        compiler_params=_cparams(1))(k, v, q, q, do, do, lse_t, lse_t, delta_t, delta_t)


def _xattn_probs(q, kh):
    sc = _dot(q, kh, "nt") * XATT_SCALE
    e = jnp.exp(sc - jnp.max(sc, axis=-1, keepdims=True))
    return e / jnp.sum(e, axis=-1, keepdims=True)


def _xattn_fwd(q, kv, tm=512):
    s = q.shape[0]
    tm = min(tm, s)

    def body(q_ref, kv_ref, o_ref):
        for h in range(XATT_HEADS):
            sl = slice(h * XATT_HEAD_DIM, (h + 1) * XATT_HEAD_DIM)
            vs = slice(D_MODEL + h * XATT_HEAD_DIM, D_MODEL + (h + 1) * XATT_HEAD_DIM)
            p = _xattn_probs(q_ref[:, sl], kv_ref[:, sl])
            o_ref[:, sl] = _dot(p, kv_ref[:, vs], "nn").astype(o_ref.dtype)

    return pl.pallas_call(
        body, name="xattn_fwd", grid=(s // tm,),
        in_specs=[pl.BlockSpec((tm, D_MODEL), lambda i: (i, 0)), pl.BlockSpec(kv.shape, lambda i: (0, 0))],
        out_specs=pl.BlockSpec((tm, D_MODEL), lambda i: (i, 0)),
        out_shape=jax.ShapeDtypeStruct((s, D_MODEL), MXU_DTYPE), compiler_params=_cparams(1))(q, kv)


def _xattn_bwd(q, kv, do, tm=512):
    s = q.shape[0]
    tm = min(tm, s)

    def body(q_ref, kv_ref, do_ref, dq_ref, dkv_ref):
        first = pl.program_id(0) == 0

        @pl.when(first)
        def _():
            dkv_ref[...] = jnp.zeros_like(dkv_ref)

        for h in range(XATT_HEADS):
            sl = slice(h * XATT_HEAD_DIM, (h + 1) * XATT_HEAD_DIM)
            vs = slice(D_MODEL + h * XATT_HEAD_DIM, D_MODEL + (h + 1) * XATT_HEAD_DIM)
            p = _xattn_probs(q_ref[:, sl], kv_ref[:, sl])
            dkv_ref[:, vs] += _dot(p, do_ref[:, sl], "tn")
            dp = _dot(do_ref[:, sl], kv_ref[:, vs], "nt")
            ds = p * (dp - jnp.sum(dp * p, axis=-1, keepdims=True)) * XATT_SCALE
            dq_ref[:, sl] = _dot(ds, kv_ref[:, sl], "nn").astype(dq_ref.dtype)
            dkv_ref[:, sl] += _dot(ds, q_ref[:, sl], "tn")

    row = pl.BlockSpec((tm, D_MODEL), lambda i: (i, 0))
    whole = pl.BlockSpec(kv.shape, lambda i: (0, 0))
    return pl.pallas_call(
        body, name="xattn_bwd", grid=(s // tm,), in_specs=[row, whole, row], out_specs=[row, whole],
        out_shape=[jax.ShapeDtypeStruct((s, D_MODEL), MXU_DTYPE), jax.ShapeDtypeStruct(kv.shape, F32)],
        compiler_params=_cparams(1))(q, kv, do)


def _ln(x, g, b):
    mu = jnp.mean(x, axis=-1, keepdims=True)
    xc = x - mu
    var = jnp.mean(jnp.square(xc), axis=-1, keepdims=True)
    return xc * lax.rsqrt(var + LN_EPS) * g + b


def _res_ln(h, o, g, b):
    return _ln(DEEPNORM_ALPHA * h + o, g, b)


def _gate(gs, ga, z1, z2, batt):
    return jax.nn.sigmoid(gs) * (z1 * jax.nn.sigmoid(z2)) + jax.nn.sigmoid(ga) * batt


def _rope_tables(pos, invf, m1, m2):
    ang = pos.astype(F32) * invf
    sin = jnp.sin(ang)
    return jnp.cos(ang), -sin * m1, sin * m2


def _rope(t, cos, s_up, s_dn):
    w = t.shape[-1]
    return t * cos + pltpu.roll(t, w - ROT_DIM // 2, 1) * s_up + pltpu.roll(t, ROT_DIM // 2, 1) * s_dn


def _rope_t(dt, cos, s_up, s_dn):
    w = dt.shape[-1]
    return dt * cos + pltpu.roll(dt * s_up, ROT_DIM // 2, 1) + pltpu.roll(dt * s_dn, w - ROT_DIM // 2, 1)


def _rope_consts():
    inv_freq = ROPE_THETA ** (-jnp.arange(0, ROT_DIM, 2, dtype=F32) / ROT_DIM)
    d = np.arange(ATT_GROUPW) % ATT_HEAD_DIM
    invf = jnp.where(d < ROT_DIM, inv_freq[d % (ROT_DIM // 2)], 0.0).reshape(1, ATT_GROUPW).astype(F32)
    m1 = jnp.asarray((d < ROT_DIM // 2).astype(np.float32)).reshape(1, ATT_GROUPW)
    m2 = jnp.asarray(((d >= ROT_DIM // 2) & (d < ROT_DIM)).astype(np.float32)).reshape(1, ATT_GROUPW)
    return invf, m1, m2


def _head_sum_matrix():
    d = np.arange(ATT_GROUPW) // ATT_HEAD_DIM
    s = np.arange(ATT_STATW) // 128
    return jnp.asarray((d[:, None] == s[None, :]).astype(np.float32))


def _adamw(w, g, m, v):
    m = ADAM_B1 * m + (1.0 - ADAM_B1) * g
    v = ADAM_B2 * v + (1.0 - ADAM_B2) * jnp.square(g)
    m_hat = m / (1.0 - ADAM_B1 ** ADAM_STEP)
    v_hat = v / (1.0 - ADAM_B2 ** ADAM_STEP)
    delta = -ADAM_LR * (m_hat / (jnp.sqrt(v_hat) + ADAM_EPS) + ADAM_WD * w)
    return delta, m, v


def _local_step(x, mem, pos, target, sp, ex):
    s = x.shape[0]
    al = DEEPNORM_ALPHA
    mx = MXU_DTYPE

    ex.gather_now(["w_in"])
    h0, h0b = _rowwise("ln_in", lambda x, g, b: (lambda h: (h, h))(_ln(x, g, b)), [x],
                       [sp["ln_in_g"], sp["ln_in_b"]], [(D_MODEL, F32), (D_MODEL, mx)])
    proj = _mm("proj", h0b, ex.weight("w_in"), "nn", bias=sp["b_in"],
               carry=ex.gather_carry(["w_glu", "w_att_up", "w_mix_out", "w_xq", "w_xkv"]))

    ldt = jnp.repeat(sp["ssm_log_dt"].reshape(SSM_GROUPS), SSM_STATE).reshape(N_STATE, 1)
    are, aim = sp["ssm_a_re"].reshape(N_STATE, 1), sp["ssm_a_im"].reshape(N_STATE, 1)
    bre, bim = sp["ssm_b_re"].reshape(N_STATE, SSM_GROUP), sp["ssm_b_im"].reshape(N_STATE, SSM_GROUP)
    abr, abi, bbr, bbi = _ssm_disc_fwd(ldt, are, aim, bre, bim)
    a_re, a_im = abr.reshape(1, N_STATE), abi.reshape(1, N_STATE)
    bexp = jnp.concatenate([_blockdiag_b(bbr), _blockdiag_b(bbi)], axis=2).astype(mx)
    cexp = jnp.concatenate([_blockdiag_c(sp["ssm_c_re"].reshape(SSM_GROUPS, SSM_GROUP, SSM_STATE)),
                            -_blockdiag_c(sp["ssm_c_im"].reshape(SSM_GROUPS, SSM_GROUP, SSM_STATE))],
                           axis=1).astype(mx)
    u_p = _time_perm(proj[:, :SSM_WIDTH])
    b12, c12 = _split_by_scan_block(bexp, 2), _split_by_scan_block(cexp, 1)
    h_re, h_im, y_p = _ssm_scan("ssm_scan_fwd", u_p, b12, c12, a_re, a_im, sp["ssm_d"], reverse=False,
                                carry=ex.gather_carry(["w_ff1", "w_ff2"]))
    y = _time_unperm(y_p)
    ygb, = _rowwise("gelu", lambda y: jax.nn.gelu(y), [y], [], [(SSM_WIDTH, mx)])
    z = _mm("glu", ygb, ex.weight("w_glu"), "nn", bias=sp["b_glu"], carry=ex.gather_carry(["w_xo"]))

    invf, m1, m2 = _rope_consts()

    def rope_fwd(pos, q0, q1, q2, k0, k1, k2, v0, v1, v2, invf, m1, m2):
        tabs = _rope_tables(pos, invf, m1, m2)
        return tuple(_rope(t, *tabs) for t in (q0, q1, q2, k0, k1, k2)) + (v0, v1, v2)

    qkv_cols = [(proj, ATT_GROUPW, 3 + i) for i in range(9)]
    qkv = _rowwise("rope", rope_fwd, [pos] + qkv_cols, [invf, m1, m2], [(ATT_GROUPW, mx)] * 9)
    n_blocks = s // ATT_BLK
    groups = [(str(g), n_blocks // d, d) for g, d in enumerate(DILATIONS)]
    q_d = [_dilate(qkv[g], d) for g, d in enumerate(DILATIONS)]
    k_d = [_dilate(qkv[3 + g], d) for g, d in enumerate(DILATIONS)]
    v_d = [_dilate(qkv[6 + g], d) for g, d in enumerate(DILATIONS)]
    o_g, l_g = [], []
    for g, (tag, per_seq, d) in enumerate(groups):
        o, lse = _attn_fwd(tag, per_seq, q_d[g], k_d[g], v_d[g])
        o_g.append(_undilate(o, d))
        l_g.append(_undilate(lse, d))

    def merge(o0, o1, o2, l0, l1, l2):
        m = jnp.maximum(jnp.maximum(l0, l1), l2)
        e0, e1, e2 = jnp.exp(l0 - m), jnp.exp(l1 - m), jnp.exp(l2 - m)
        tot = e0 + e1 + e2

        def per_dim(e):
            w = e / tot
            return jnp.concatenate([w[:, h * 128:h * 128 + ATT_HEAD_DIM] for h in range(ATT_HPG)], axis=1)

        att = per_dim(e0) * o0 + per_dim(e1) * o1 + per_dim(e2) * o2
        lse = m + jnp.log(tot)
        return att, att, lse, _stat_rows(lse)

    att, attb, lse_tot, lse_tot_t = _rowwise("attn_merge", merge, o_g + l_g, [],
                                             [(ATT_GROUPW, F32), (ATT_GROUPW, mx), (ATT_STATW, F32)], touts=[(8, F32)])
    batt = _mm("att_up", attb, ex.weight("w_att_up"), "nn")

    gate_rows = [(proj, D_MODEL, 3), (proj, D_MODEL, 4), (z, D_MODEL, 0), (z, D_MODEL, 1), batt]
    mixedb, = _rowwise("gate", _gate, gate_rows, [], [(D_MODEL, mx)])
    o1 = _mm("mix_out", mixedb, ex.weight("w_mix_out"), "nn", bias=sp["b_mix_out"])
    h1, h1b = _rowwise("ln1", lambda h, o, g, b: (lambda r: (r, r))(_res_ln(h, o, g, b)), [h0, o1],
                       [sp["ln1_g"], sp["ln1_b"]], [(D_MODEL, F32), (D_MODEL, mx)])

    qx = _mm("xq", h1b, ex.weight("w_xq"), "nn", out_dtypes=(mx,))
    kvx = _mm("xkv", mem, ex.weight("w_xkv"), "nn", out_dtypes=(mx,))
    oxb = _xattn_fwd(qx, kvx)
    o2 = _mm("xo", oxb, ex.weight("w_xo"), "nn")
    h2, h2b = _rowwise("ln2", lambda h, o, g, b: (lambda r: (r, r))(_res_ln(h, o, g, b)), [h1, o2],
                       [sp["ln2_g"], sp["ln2_b"]], [(D_MODEL, F32), (D_MODEL, mx)])

    a_ff, fb = _mm("ff1", h2b, ex.weight("w_ff1"), "nn", bias=sp["b_ff1"],
                   epilogue=lambda r: (r, jnp.square(jnp.maximum(r, 0.0))), out_dtypes=(F32, mx))
    o3 = _mm("ff2", fb, ex.weight("w_ff2"), "nn", bias=sp["b_ff2"])

    def loss_bwd(h2, o3, tgt, g, b):
        def f(h2, o3, g, b):
            h3 = _res_ln(h2, o3, g, b)
            return 0.5 * jnp.sum(jnp.mean(jnp.square(h3 - tgt), axis=-1))

        loss, vjp = jax.vjp(f, h2, o3, g, b)
        _, dr, dg, db = vjp(jnp.ones((), F32))
        return dr, dr, dg, db, _colsum(dr), jnp.full((1, 128), loss, F32)

    dr3, dr3b, g_ln3_g, g_ln3_b, g_b_ff2, loss = _rowwise(
        "loss_ln3_bwd", loss_bwd, [h2, o3, target], [sp["ln3_g"], sp["ln3_b"]],
        [(D_MODEL, F32), (D_MODEL, mx)], [D_MODEL, D_MODEL, D_MODEL, 128])

    dab = _mm("ff2_dx", dr3b, ex.weight("w_ff2"), "nt", extras=(a_ff,),
              epilogue=lambda r, a: (r * (2.0 * jnp.maximum(a, 0.0)),), out_dtypes=(mx,))
    ex.grad("w_ff2", _mm("ff2_dw", fb, dr3b, "tn"))
    g_b_ff1, = _rowwise("ff1_db", lambda v: (_colsum(v),), [dab], [], [], [D_FF])
    ex.grad("w_ff1", _mm("ff1_dw", h2b, dab, "tn", carry=ex.carry(swap=["w_ff2"])))
    dh2 = _mm("ff1_dx", dab, ex.weight("w_ff1"), "nt", extras=(dr3,), epilogue=lambda r, d: (r + al * d,),
              carry=ex.carry(swap=["w_ff1"], ici=["w_ff2"]))

    def ln_bwd(h, o, dout, g, b):
        _, vjp = jax.vjp(_res_ln, h, o, g, b)
        _, dr, dg, db = vjp(dout)
        return dr, dr, dg, db, _colsum(dr)

    dr2, dr2b, g_ln2_g, g_ln2_b, _ = _rowwise(
        "ln2_bwd", ln_bwd, [h1, o2, dh2], [sp["ln2_g"], sp["ln2_b"]],
        [(D_MODEL, F32), (D_MODEL, mx)], [D_MODEL, D_MODEL, D_MODEL])
    ex.grad("w_xo", _mm("xo_dw", oxb, dr2b, "tn", carry=ex.carry(ici=["w_ff1"])))
    doxb = _mm("xo_dx", dr2b, ex.weight("w_xo"), "nt", out_dtypes=(mx,), carry=ex.carry(swap=["w_xo"]))
    dqxb, dkvx = _xattn_bwd(qx, kvx, doxb)
    ex.grad("w_xq", _mm("xq_dw", h1b, dqxb, "tn", carry=ex.carry(ici=["w_xo"])))
    dh1 = _mm("xq_dx", dqxb, ex.weight("w_xq"), "nt", extras=(dr2,), epilogue=lambda r, d: (r + al * d,),
              carry=ex.carry(swap=["w_xq"]))
    ex.grad("w_xkv", _mm("xkv_dw", mem, dkvx, "tn"))

    dr1, dr1b, g_ln1_g, g_ln1_b, g_b_mix = _rowwise(
        "ln1_bwd", ln_bwd, [h0, o1, dh1], [sp["ln1_g"], sp["ln1_b"]],
        [(D_MODEL, F32), (D_MODEL, mx)], [D_MODEL, D_MODEL, D_MODEL])
    ex.grad("w_mix_out", _mm("mix_dw", mixedb, dr1b, "tn", carry=ex.carry(swap=["w_xkv"], ici=["w_xq"])))
    dmixed = _mm("mix_dx", dr1b, ex.weight("w_mix_out"), "nt", carry=ex.carry(swap=["w_mix_out"], ici=["w_xkv"]))

    def gate_bwd(gs, ga, z1, z2, batt, dm):
        _, vjp = jax.vjp(_gate, gs, ga, z1, z2, batt)
        dgs, dga, dz1, dz2, dbatt = vjp(dm)
        dz = jnp.concatenate([dz1, dz2], axis=-1)
        return dgs, dga, dz, dbatt, _colsum(dz)

    dgsb, dgab, dzb, dbattb, g_b_glu = _rowwise(
        "gate_bwd", gate_bwd, gate_rows + [dmixed], [],
        [(D_MODEL, mx), (D_MODEL, mx), (2 * D_MODEL, mx), (D_MODEL, mx)], [2 * D_MODEL])
    ex.grad("w_att_up", _mm("att_up_dw", attb, dbattb, "tn", carry=ex.carry(ici=["w_mix_out"])))
    datt = _mm("att_up_dx", dbattb, ex.weight("w_att_up"), "nt", carry=ex.carry(swap=["w_att_up"]))

    def att_delta(datt, att, hs):
        dl = jnp.dot(datt * att, hs, precision=lax.Precision.HIGHEST, preferred_element_type=F32)
        return datt, dl, _stat_rows(dl)

    dattb, delta, delta_t = _rowwise("attn_delta", att_delta, [datt, att], [_head_sum_matrix()],
                                     [(ATT_GROUPW, mx), (ATT_STATW, F32)], touts=[(8, F32)])
    dq_g, dk_g, dv_g = [], [], []
    for g, (tag, per_seq, d) in enumerate(groups):
        do_d, lt_d, dl_d = _dilate(dattb, d), _dilate(lse_tot, d), _dilate(delta, d)
        dq_g.append(_undilate(_attn_dq(tag, per_seq, q_d[g], k_d[g], v_d[g], do_d, lt_d, dl_d), d))
        dk, dv = _attn_dkv(tag, per_seq, q_d[g], k_d[g], v_d[g], do_d, _dilate_rows(lse_tot_t, d), _dilate_rows(delta_t, d))
        dk_g.append(_undilate(dk, d))
        dv_g.append(_undilate(dv, d))
    dqkv = dq_g + dk_g + dv_g

    def rope_bwd(pos, q0, q1, q2, k0, k1, k2, v0, v1, v2, invf, m1, m2):
        tabs = _rope_tables(pos, invf, m1, m2)
        return jnp.concatenate([_rope_t(t, *tabs) for t in (q0, q1, q2, k0, k1, k2)] + [v0, v1, v2], axis=-1)

    dqkvb, = _rowwise("rope_bwd", rope_bwd, [pos] + dqkv, [invf, m1, m2], [(9 * ATT_GROUPW, mx)])

    ex.grad("w_glu", _mm("glu_dw", ygb, dzb, "tn", carry=ex.carry(ici=["w_att_up"])))
    dyg = _mm("glu_dx", dzb, ex.weight("w_glu"), "nt", carry=ex.carry(swap=["w_glu"]))

    def gelu_bwd(y, dyg):
        _, vjp = jax.vjp(jax.nn.gelu, y)
        return vjp(dyg)[0]

    dy, = _rowwise("gelu_bwd", gelu_bwd, [y, dyg], [], [(SSM_WIDTH, F32)])
    dy_p = _time_perm(dy)
    g_cexp = _ssm_wgrad("ssm_dc", dy_p, h_re, h_im, expand=False)
    s_re, s_im, du_p = _ssm_scan("ssm_scan_bwd", dy_p, c12, b12, a_re, a_im, sp["ssm_d"], reverse=True,
                                 carry=ex.carry(ici=["w_glu"]))
    d_abr, d_abi = _ssm_da(s_re, s_im, h_re, h_im)
    g_bexp = _ssm_wgrad("ssm_db", u_p, s_re, s_im, expand=True)
    g_ssm_d, = _rowwise("ssm_dd", lambda a, b: (_colsum(a * b),), [dy_p, u_p], [], [], [SSM_WIDTH])
    g_ldt, g_are, g_aim, g_bre, g_bim = _ssm_disc_bwd(
        ldt, are, aim, bre, bim, d_abr.reshape(N_STATE, 1), d_abi.reshape(N_STATE, 1),
        _diag_of_b(g_bexp[:, :, :CH_N]), _diag_of_b(g_bexp[:, :, CH_N:]))
    g_c_re = _diag_of_c(g_cexp[:, :CH_N, :])
    g_c_im = -_diag_of_c(g_cexp[:, CH_N:, :])
    dub = _time_unperm(du_p).astype(mx)

    dprojb = jnp.concatenate([dub, dqkvb, dgsb, dgab], axis=-1)
    g_b_in, = _rowwise("in_db", lambda v: (_colsum(v),), [dprojb], [], [], [IN_COLS])
    ex.grad("w_in", _mm("in_dw", h0b, dprojb, "tn"))
    dh0 = _mm("in_dx", dprojb, ex.weight("w_in"), "nt", extras=(dr1,), epilogue=lambda r, d: (r + al * d,),
              carry=ex.carry(ici=["w_in"]))

    def ln_in_bwd(x, dout, g, b):
        _, vjp = jax.vjp(_ln, x, g, b)
        return vjp(dout)

    dx, g_ln_in_g, g_ln_in_b = _rowwise("ln_in_bwd", ln_in_bwd, [x, dh0], [sp["ln_in_g"], sp["ln_in_b"]],
                                        [(D_MODEL, F32)], [D_MODEL, D_MODEL])

    small = {"ln_in_g": g_ln_in_g, "ln_in_b": g_ln_in_b, "b_in": g_b_in, "ssm_log_dt": g_ldt, "ssm_a_re": g_are,
             "ssm_a_im": g_aim, "ssm_b_re": g_bre, "ssm_b_im": g_bim, "ssm_c_re": g_c_re, "ssm_c_im": g_c_im,
             "ssm_d": g_ssm_d, "b_glu": g_b_glu, "b_mix_out": g_b_mix, "ln1_g": g_ln1_g, "ln1_b": g_ln1_b,
             "ln2_g": g_ln2_g, "ln2_b": g_ln2_b, "b_ff1": g_b_ff1, "b_ff2": g_b_ff2, "ln3_g": g_ln3_g,
             "ln3_b": g_ln3_b}
    return loss, dx, small


def _piece_shape(k, n, axis):
    return (k // 2, n // 4) if axis == 1 else (k // 8, n)


def _aligned(v, m):
    return v if isinstance(v, int) else pl.multiple_of(v, m)


def _full_piece(ref, k, n, axis, chip, half):
    pr, pc = _piece_shape(k, n, axis)
    if axis == 1:
        return ref.at[pl.ds(_aligned(half * pr, 8), pr), pl.ds(_aligned(chip * pc, 128), pc)]
    return ref.at[pl.ds(_aligned(chip * (2 * pr) + half * pr, 8), pr), :]


def _full_shard(ref, k, n, axis, chip):
    if axis == 1:
        return ref.at[:, pl.ds(_aligned(chip * (n // 4), 128), n // 4)]
    return ref.at[pl.ds(_aligned(chip * (k // 4), 8), k // 4), :]


def _shard_piece(ref, k, n, axis, half):
    pr, _ = _piece_shape(k, n, axis)
    return ref.at[pl.ds(_aligned(half * pr, 8), pr), :]


def _mesh_pos():
    x, y, c = lax.axis_index("x"), lax.axis_index("y"), lax.axis_index("c")
    other_chips = [(1 - x, y), (x, 1 - y), (1 - x, 1 - y)]
    return x, y, c, other_chips


def _remote(src, dst, send_sem, recv_sem, dev):
    return pltpu.make_async_remote_copy(src_ref=src, dst_ref=dst, send_sem=send_sem, recv_sem=recv_sem,
                                        device_id=dev, device_id_type=MESH)


def _placed(name, fn, n_steps, where, ins, out_sds, out_block, out_index):
    def body(w_ref, *refs):
        o_ref = refs[-1]
        o_ref[...] = fn(*[r[...] for r in refs[:-1]]).astype(o_ref.dtype)

    grid_spec = pltpu.PrefetchScalarGridSpec(
        num_scalar_prefetch=1, grid=(n_steps,), in_specs=[pl.BlockSpec(bs, idx) for _, bs, idx in ins],
        out_specs=pl.BlockSpec(out_block, out_index))
    return pl.pallas_call(body, name=name, grid_spec=grid_spec, out_shape=out_sds,
                          compiler_params=_cparams(1))(where, *[a for a, _, _ in ins])


def _gather_copies(widx):
    geo = [BIG[i][1:] for i in widx]

    def ici(full, wi, j, chip, send_sems, recv_sems, c, dev):
        k, n, ax = geo[wi]
        piece = _full_piece(full[wi], k, n, ax, chip, c)
        return _remote(piece, piece, send_sems.at[wi * 6 + j], recv_sems.at[wi * 6 + j], dev)

    def d2d(full, wi, j, chip, half, send_sems, recv_sems, sib):
        k, n, ax = geo[wi]
        piece = _full_piece(full[wi], k, n, ax, chip, half)
        return _remote(piece, piece, send_sems.at[wi * 6 + 3 + j], recv_sems.at[wi * 6 + 3 + j], sib)

    def start(_, full, send_sems, recv_sems):
        x, y, c, chips = _mesh_pos()
        for wi in range(len(geo)):
            for j, (qx, qy) in enumerate(chips):
                ici(full, wi, j, 2 * x + y, send_sems, recv_sems, c, (qx, qy, c)).start()

    def finish(_, full, send_sems, recv_sems):
        x, y, c, chips = _mesh_pos()
        sib = (x, y, 1 - c)
        for wi in range(len(geo)):
            for j, (qx, qy) in enumerate(chips):
                ici(full, wi, j, 2 * qx + qy, send_sems, recv_sems, c, (qx, qy, c)).wait_recv()
                d2d(full, wi, j, 2 * qx + qy, c, send_sems, recv_sems, sib).start()
        for wi in range(len(geo)):
            for j, (qx, qy) in enumerate(chips):
                d2d(full, wi, j, 2 * qx + qy, 1 - c, send_sems, recv_sems, sib).wait_recv()
        for wi in range(len(geo)):
            for j, (qx, qy) in enumerate(chips):
                ici(full, wi, j, 2 * x + y, send_sems, recv_sems, c, (qx, qy, c)).wait_send()
                d2d(full, wi, j, 2 * qx + qy, c, send_sems, recv_sems, sib).wait_send()

    return start, finish, 6 * len(geo)


def _gather_weights(tag, fulls, widx):
    nw = len(widx)
    start, finish, n_sems = _gather_copies(widx)

    def body(*refs):
        full = refs[nw:2 * nw]
        start(None, full, *refs[2 * nw:])
        finish(None, full, *refs[2 * nw:])

    return pl.pallas_call(
        body, name="gather_weights_" + tag, in_specs=[HBM_SPEC] * nw, out_specs=[HBM_SPEC] * nw,
        out_shape=[jax.ShapeDtypeStruct(f.shape, f.dtype) for f in fulls],
        input_output_aliases={i: i for i in range(nw)},
        scratch_shapes=[pltpu.SemaphoreType.DMA((n_sems,)), pltpu.SemaphoreType.DMA((n_sems,))])(*fulls)


def _swap_copies(widx):
    geo = [BIG[i][1:] for i in widx]

    def copies(g, got, send_sems, recv_sems, base):
        x, y, c, _ = _mesh_pos()
        return [_remote(_full_piece(g[wi], k, n, ax, q, 1 - c), got[wi].at[q], send_sems.at[base + wi * 4 + q],
                        recv_sems.at[base + wi * 4 + q], (x, y, 1 - c))
                for wi, (k, n, ax) in enumerate(geo) for q in range(4)]

    def start(g, got, send_sems, recv_sems, base=0):
        for cp in copies(g, got, send_sems, recv_sems, base):
            cp.start()

    def finish(g, got, send_sems, recv_sems, base=0):
        for cp in copies(g, got, send_sems, recv_sems, base):
            cp.wait()

    return start, finish, 4 * len(geo)


def _swap_shapes(widx):
    return [jax.ShapeDtypeStruct((4,) + _piece_shape(*BIG[i][1:]), F32) for i in widx]


def _reduce_swap_halves(tag, grads, widx):
    nw = len(widx)
    start, finish, n_sems = _swap_copies(widx)

    def body(*refs):
        start(refs[:nw], refs[nw:2 * nw], *refs[2 * nw:])
        finish(refs[:nw], refs[nw:2 * nw], *refs[2 * nw:])

    return pl.pallas_call(
        body, name="reduce_swap_halves_" + tag, in_specs=[HBM_SPEC] * nw, out_specs=[HBM_SPEC] * nw,
        out_shape=_swap_shapes(widx),
        scratch_shapes=[pltpu.SemaphoreType.DMA((n_sems,)), pltpu.SemaphoreType.DMA((n_sems,))])(*grads)


def _owner_copies(nw):
    def copies(p, out, send_sems, recv_sems, base):
        x, y, c, chips = _mesh_pos()
        return [_remote(p[wi].at[2 * qx + qy], out[wi].at[j], send_sems.at[base + wi * 3 + j],
                        recv_sems.at[base + wi * 3 + j], (qx, qy, c))
                for wi in range(nw) for j, (qx, qy) in enumerate(chips)]

    def start(p, out, send_sems, recv_sems, base=0):
        for cp in copies(p, out, send_sems, recv_sems, base):
            cp.start()

    def finish(p, out, send_sems, recv_sems, base=0):
        for cp in copies(p, out, send_sems, recv_sems, base):
            cp.wait()

    return start, finish, 3 * nw


def _join_carries(a, b):
    if a is None or b is None:
        return a if b is None else b
    n_i, n_o = len(a.ins), len(a.outs)
    outs = list(a.outs) + [o + n_i if isinstance(o, int) else o for o in b.outs]

    def start(c_in, c_out, send_sems, recv_sems):
        a.start(c_in[:n_i], c_out[:n_o], send_sems, recv_sems)
        b.start(c_in[n_i:], c_out[n_o:], send_sems, recv_sems, base=a.n_sems)

    def finish(c_in, c_out, send_sems, recv_sems):
        a.finish(c_in[:n_i], c_out[:n_o], send_sems, recv_sems)
        b.finish(c_in[n_i:], c_out[n_o:], send_sems, recv_sems, base=a.n_sems)

    def done(res):
        a.done(res[:n_o])
        b.done(res[n_o:])

    return _Carry(a.ins + b.ins, outs, a.n_sems + b.n_sems, start, finish, done)


def _share_with_sibling(shards):
    nw = len(BIG)

    def body(*refs):
        out = refs[nw:2 * nw]
        send_sems, recv_sems = refs[2 * nw:]
        x, y, c, _ = _mesh_pos()
        sib = (x, y, 1 - c)
        cps = []
        for wi, (_, k, n, ax) in enumerate(BIG):
            mine = _shard_piece(out[wi], k, n, ax, c)
            cp = _remote(mine, mine, send_sems.at[wi], recv_sems.at[wi], sib)
            cp.start()
            cps.append(cp)
        for wi, (_, k, n, ax) in enumerate(BIG):
            piece = _shard_piece(out[wi], k, n, ax, 1 - c)
            _remote(piece, piece, send_sems.at[wi], recv_sems.at[wi], sib).wait_recv()
        for cp in cps:
            cp.wait_send()

    return pl.pallas_call(
        body, name="share_with_sibling", in_specs=[HBM_SPEC] * nw, out_specs=[HBM_SPEC] * nw,
        out_shape=[jax.ShapeDtypeStruct(sh.shape, sh.dtype) for sh in shards],
        input_output_aliases={i: i for i in range(nw)},
        scratch_shapes=[pltpu.SemaphoreType.DMA((nw,)), pltpu.SemaphoreType.DMA((nw,))])(*shards)


def _allreduce_small(v):
    r = v.shape[0]
    rh = r // 2
    assert rh % 8 == 0

    def body(v_ref, o_ref, sib_buf, chip_buf, send_sems, recv_sems):
        x, y, c, chips = _mesh_pos()
        me = 2 * x + y
        sib = (x, y, 1 - c)
        mine = pl.ds(pl.multiple_of(c * rh, 8), rh)
        other = pl.ds(pl.multiple_of((1 - c) * rh, 8), rh)
        swap = _remote(v_ref.at[other], sib_buf, send_sems.at[0], recv_sems.at[0], sib)
        swap.start()
        swap.wait()
        chip_buf[me] = v_ref[mine, :] + sib_buf[...]
        cps = []
        for j, (qx, qy) in enumerate(chips):
            cp = _remote(chip_buf.at[me], chip_buf.at[me], send_sems.at[1 + j], recv_sems.at[1 + j], (qx, qy, c))
            cp.start()
            cps.append(cp)
        for j, (qx, qy) in enumerate(chips):
            slot = chip_buf.at[2 * qx + qy]
            _remote(slot, slot, send_sems.at[1 + j], recv_sems.at[1 + j], (qx, qy, c)).wait_recv()
        for cp in cps:
            cp.wait_send()
        o_ref[mine, :] = ((chip_buf[0] + chip_buf[1]) + chip_buf[2]) + chip_buf[3]
        back = _remote(o_ref.at[mine], o_ref.at[mine], send_sems.at[4], recv_sems.at[4], sib)
        back.start()
        _remote(o_ref.at[other], o_ref.at[other], send_sems.at[4], recv_sems.at[4], sib).wait_recv()
        back.wait_send()

    return pl.pallas_call(
        body, name="allreduce_small", in_specs=[VMEM_SPEC], out_specs=VMEM_SPEC,
        out_shape=jax.ShapeDtypeStruct((r, 128), F32),
        scratch_shapes=[pltpu.VMEM((rh, 128), F32), pltpu.VMEM((4, rh, 128), F32),
                        pltpu.SemaphoreType.DMA((5,)), pltpu.SemaphoreType.DMA((5,))],
        compiler_params=pltpu.CompilerParams(vmem_limit_bytes=VMEM_LIMIT))(v)


def _as2d(a):
    a = a.reshape((-1, a.shape[-1])) if a.ndim > 1 else a.reshape(1, -1)
    return a


def _adamw_small(quads):
    n = len(quads)

    def body(*refs):
        for i in range(n):
            w, g, m, v = (r[...] for r in refs[4 * i:4 * i + 4])
            for ref, val in zip(refs[4 * n + 3 * i:4 * n + 3 * i + 3], _adamw(w, g, m, v)):
                ref[...] = val

    return pl.pallas_call(
        body, name="adamw_small", in_specs=[VMEM_SPEC] * (4 * n), out_specs=[VMEM_SPEC] * (3 * n),
        out_shape=[jax.ShapeDtypeStruct(q[0].shape, F32) for q in quads for _ in range(3)],
        compiler_params=pltpu.CompilerParams(vmem_limit_bytes=VMEM_LIMIT))(*[a for q in quads for a in q])


def _where():
    return jnp.stack([2 * lax.axis_index("x") + lax.axis_index("y"), lax.axis_index("c")]).astype(jnp.int32)


_BIG_INDEX = {name: i for i, (name, _, _, _) in enumerate(BIG)}


class _LocalWeights:
    def __init__(self, weights):
        self.weights, self.grads = weights, {}

    def gather_now(self, names):
        pass

    def gather_carry(self, names):
        return None

    def weight(self, name):
        return self.weights[name]

    def grad(self, name, g):
        self.grads[name] = g

    def carry(self, swap=(), ici=()):
        return None


class _Exchange:
    def __init__(self, inputs, where):
        self.inputs, self.where = inputs, where
        self.full, self.ready = {}, set()
        self.raw, self.got, self.parts, self.landed, self.geom = {}, {}, {}, {}, {}
        for name, k, n, ax in BIG:
            w2 = inputs[name][0]
            rs, cs = w2.shape
            tm = _tile(rs, 512)
            steps = rs // tm
            if ax == 1:
                blk, idx = (tm, cs), lambda i, w: (i, w[0])
            else:
                blk, idx = (tm, n), functools.partial(lambda i, w, steps: (w[0] * steps + i, 0), steps=steps)
            self.full[name] = _placed("cast_" + name, lambda w: w, steps, where, [(w2, (tm, cs), lambda i, w: (i, 0))],
                                      jax.ShapeDtypeStruct((k, n), MXU_DTYPE), blk, idx)

    def _gathered(self, names, outs):
        for name, o in zip(names, outs):
            self.full[name] = o
            self.ready.add(name)

    def gather_now(self, names):
        self._gathered(names, _gather_weights(names[0], [self.full[n] for n in names], [_BIG_INDEX[n] for n in names]))

    def gather_carry(self, names):
        start, finish, n_sems = _gather_copies([_BIG_INDEX[n] for n in names])
        return _Carry([self.full[n] for n in names], list(range(len(names))), n_sems, start, finish,
                      functools.partial(self._gathered, names))

    def weight(self, name):
        assert name in self.ready, name
        return self.full[name]

    def grad(self, name, g):
        self.raw[name] = g

    def _swapped(self, names, outs):
        for name, o in zip(names, outs):
            self.got[name] = o

    def _pair_sum(self, name):
        i = _BIG_INDEX[name]
        _, k, n, ax = BIG[i]
        g = self.raw[name]
        if name not in self.got:
            self._swapped([name], _reduce_swap_halves(name, [g], [i]))
        got = self.got[name]
        pr, pc = _piece_shape(k, n, ax)
        tm = _tile(pr, 512)
        spp = pr // tm
        self.geom[name] = (pr, pc, tm, spp)
        if ax == 1:
            g_idx = functools.partial(lambda i, w, spp: (w[1] * spp + i % spp, i // spp), spp=spp)
        else:
            g_idx = functools.partial(lambda i, w, spp: ((i // spp) * 2 * spp + w[1] * spp + i % spp, 0), spp=spp)
        self.parts[name] = _placed(
            "pair_sum_" + name, lambda a, b: a + b, 4 * spp, self.where,
            [(g, (tm, pc), g_idx), (got.reshape(4 * pr, pc), (tm, pc), lambda i, w: (i, 0))],
            jax.ShapeDtypeStruct((4 * pr, pc), BF16), (tm, pc), lambda i, w: (i, 0)).reshape(4, pr, pc)

    def _landed(self, names, outs):
        for name, o in zip(names, outs):
            self.landed[name] = o

    def carry(self, swap=(), ici=()):
        first = second = None
        if swap:
            widx = [_BIG_INDEX[n] for n in swap]
            start, finish, n_sems = _swap_copies(widx)
            first = _Carry([self.raw[n] for n in swap], _swap_shapes(widx), n_sems, start, finish,
                           functools.partial(self._swapped, list(swap)))
        if ici:
            for n in ici:
                self._pair_sum(n)
            start, finish, n_sems = _owner_copies(len(ici))
            parts = [self.parts[n] for n in ici]
            outs = [jax.ShapeDtypeStruct((3,) + p.shape[1:], p.dtype) for p in parts]
            second = _Carry(parts, outs, n_sems, start, finish, functools.partial(self._landed, list(ici)))
        return _join_carries(first, second)

    def finish(self):
        halves = []
        for name, _, _, _ in BIG:
            pr, pc, tm, spp = self.geom[name]
            ins = [(self.parts[name], (None, tm, pc), lambda i, w: (w[0], i, 0))]
            ins += [(self.landed[name], (None, tm, pc), functools.partial(lambda i, w, j: (j, i, 0), j=j))
                    for j in range(3)]
            halves.append(_placed("chip_sum_" + name,
                                  lambda a, b, c, d: ((a.astype(F32) + b.astype(F32)) + c.astype(F32)) + d.astype(F32),
                                  spp, self.where, ins, jax.ShapeDtypeStruct(self.inputs[name].shape[1:], F32), (tm, pc),
                                  functools.partial(lambda i, w, spp: (w[1] * spp + i, 0), spp=spp)))
        return dict(zip([b[0] for b in BIG], _share_with_sibling(halves)))


def _step(inputs):
    x, mem, positions, target = inputs["x"][0], inputs["mem"][0], inputs["positions"], inputs["loss_target"][0]
    pos = positions.reshape(-1, 1)
    ex = _Exchange(inputs, _where())
    sp = {name: _as2d(inputs[name]) for name in SMALL}
    memb, = _rowwise("cast_mem", lambda m: (m,), [mem], [], [(D_MODEL, MXU_DTYPE)])

    loss, dx, gsmall = _local_step(x, memb, pos, target, sp, ex)
    gshard = ex.finish()

    out = {}
    for name, _, _, _ in BIG:
        w2, m2, v2 = inputs[name][0], inputs["m_" + name][0], inputs["v_" + name][0]
        n = w2.shape[1]
        d, nm, nv = _rowwise("adamw_" + name, _adamw, [w2, gshard[name], m2, v2], [], [(n, F32)] * 3, tm=256)
        lead = inputs[name].shape
        out[name] = (gshard[name].reshape(lead), d.reshape(lead), nm.reshape(lead), nv.reshape(lead))

    def tiles(a):
        flat = a.reshape(-1)
        n = -(-flat.shape[0] // 1024) * 1024
        return jnp.pad(flat, (0, n - flat.shape[0])).reshape(n // 128, 128)

    pieces = [tiles(loss[:, :1])] + [tiles(gsmall[name]) for name in SMALL]
    if sum(p.shape[0] for p in pieces) % 16:
        pieces.append(jnp.zeros((8, 128), F32))
    red = _allreduce_small(jnp.concatenate(pieces, axis=0))
    loss_total = red[0, 0]
    grads, off = {}, pieces[0].shape[0]
    for name, p in zip(SMALL, pieces[1:]):
        shp = _as2d(inputs[name]).shape
        grads[name] = red[off:off + p.shape[0]].reshape(-1)[:shp[0] * shp[1]].reshape(shp)
        off += p.shape[0]
    upd = _adamw_small([(_as2d(inputs[n]), grads[n], _as2d(inputs["m_" + n]), _as2d(inputs["v_" + n])) for n in SMALL])
    for i, name in enumerate(SMALL):
        shp = inputs[name].shape
        out[name] = (grads[name].reshape(shp),) + tuple(t.reshape(shp) for t in upd[3 * i:3 * i + 3])
    return loss_total, dx.reshape(inputs["x"].shape), out


_ARG_NAMES = (("x", "mem", "positions") + WEIGHT_ORDER + ("loss_target",) + tuple("m_" + n for n in WEIGHT_ORDER)
              + tuple("v_" + n for n in WEIGHT_ORDER))


def kernel(x, mem, positions, ln_in_g, ln_in_b, w_in, b_in, ssm_log_dt, ssm_a_re, ssm_a_im, ssm_b_re, ssm_b_im, ssm_c_re, ssm_c_im, ssm_d, w_glu, b_glu, w_att_up, w_mix_out, b_mix_out, ln1_g, ln1_b, w_xq, w_xkv, w_xo, ln2_g, ln2_b, w_ff1, b_ff1, w_ff2, b_ff2, ln3_g, ln3_b, loss_target, m_ln_in_g, m_ln_in_b, m_w_in, m_b_in, m_ssm_log_dt, m_ssm_a_re, m_ssm_a_im, m_ssm_b_re, m_ssm_b_im, m_ssm_c_re, m_ssm_c_im, m_ssm_d, m_w_glu, m_b_glu, m_w_att_up, m_w_mix_out, m_b_mix_out, m_ln1_g, m_ln1_b, m_w_xq, m_w_xkv, m_w_xo, m_ln2_g, m_ln2_b, m_w_ff1, m_b_ff1, m_w_ff2, m_b_ff2, m_ln3_g, m_ln3_b, v_ln_in_g, v_ln_in_b, v_w_in, v_b_in, v_ssm_log_dt, v_ssm_a_re, v_ssm_a_im, v_ssm_b_re, v_ssm_b_im, v_ssm_c_re, v_ssm_c_im, v_ssm_d, v_w_glu, v_b_glu, v_w_att_up, v_w_mix_out, v_b_mix_out, v_ln1_g, v_ln1_b, v_w_xq, v_w_xkv, v_w_xo, v_ln2_g, v_ln2_b, v_w_ff1, v_b_ff1, v_w_ff2, v_b_ff2, v_ln3_g, v_ln3_b):
    args = (x, mem, positions, ln_in_g, ln_in_b, w_in, b_in, ssm_log_dt, ssm_a_re, ssm_a_im, ssm_b_re, ssm_b_im, ssm_c_re, ssm_c_im, ssm_d, w_glu, b_glu, w_att_up, w_mix_out, b_mix_out, ln1_g, ln1_b, w_xq, w_xkv, w_xo, ln2_g, ln2_b, w_ff1, b_ff1, w_ff2, b_ff2, ln3_g, ln3_b, loss_target, m_ln_in_g, m_ln_in_b, m_w_in, m_b_in, m_ssm_log_dt, m_ssm_a_re, m_ssm_a_im, m_ssm_b_re, m_ssm_b_im, m_ssm_c_re, m_ssm_c_im, m_ssm_d, m_w_glu, m_b_glu, m_w_att_up, m_w_mix_out, m_b_mix_out, m_ln1_g, m_ln1_b, m_w_xq, m_w_xkv, m_w_xo, m_ln2_g, m_ln2_b, m_w_ff1, m_b_ff1, m_w_ff2, m_b_ff2, m_ln3_g, m_ln3_b, v_ln_in_g, v_ln_in_b, v_w_in, v_b_in, v_ssm_log_dt, v_ssm_a_re, v_ssm_a_im, v_ssm_b_re, v_ssm_b_im, v_ssm_c_re, v_ssm_c_im, v_ssm_d, v_w_glu, v_b_glu, v_w_att_up, v_w_mix_out, v_b_mix_out, v_ln1_g, v_ln1_b, v_w_xq, v_w_xkv, v_w_xo, v_ln2_g, v_ln2_b, v_w_ff1, v_b_ff1, v_w_ff2, v_b_ff2, v_ln3_g, v_ln3_b)
    assert len(args) == len(_ARG_NAMES)
    inputs = dict(zip(_ARG_NAMES, args))
    loss, dx, out = _step(inputs)
    res = [loss, dx]
    for k in range(4):
        res += [out[name][k] for name in WEIGHT_ORDER]
    return tuple(res)
```

```python
import functools
import math

import numpy as np
import jax
import jax.numpy as jnp
from jax import lax
from jax.experimental import pallas as pl
from jax.experimental.pallas import tpu as pltpu

F32 = jnp.float32
BF16 = jnp.bfloat16
MXU_DTYPE = jnp.bfloat16

D_MODEL = 1024
SSM_GROUP = 16
SSM_WIDTH = 768
SSM_GROUPS = 48
SSM_STATE = 64
N_STATE = SSM_GROUPS * SSM_STATE
SSM_CHUNKS = 6
CH_W = 128
CH_N = 512
ATT_HEAD_DIM = 64
ATT_HPG = 4
ATT_GROUPW = ATT_HPG * ATT_HEAD_DIM
DILATIONS = (1, 4, 16)
ATT_BLK = 128
ATT_SCALE = ATT_HEAD_DIM ** -0.5
ROT_DIM = 16
ROPE_THETA = 500000.0
XATT_HEADS = 4
XATT_HEAD_DIM = 256
XATT_SCALE = XATT_HEAD_DIM ** -0.5
D_FF = 4096
IN_COLS = 5120
DEEPNORM_ALPHA = 2.0 ** 0.25
LN_EPS = 1e-5
NEG_INF = -1e30
ADAM_LR = 0.001
ADAM_B1 = 0.9
ADAM_B2 = 0.999
ADAM_EPS = 1e-08
ADAM_WD = 0.01
ADAM_STEP = 10

N_SEG = 32
VMEM_LIMIT = 48 * 1024 * 1024
MESH = pl.DeviceIdType.MESH
HBM_SPEC = pl.BlockSpec(memory_space=pltpu.HBM)
VMEM_SPEC = pl.BlockSpec(memory_space=pltpu.VMEM)

BIG = (("w_in", 1024, 5120, 1), ("w_glu", 768, 2048, 1), ("w_att_up", 256, 1024, 1),
       ("w_mix_out", 1024, 1024, 0), ("w_xq", 1024, 1024, 0), ("w_xkv", 1024, 2048, 1),
       ("w_xo", 1024, 1024, 0), ("w_ff1", 1024, 4096, 1), ("w_ff2", 4096, 1024, 0))
SMALL = ("ln_in_g", "ln_in_b", "b_in", "ssm_log_dt", "ssm_a_re", "ssm_a_im", "ssm_b_re", "ssm_b_im",
         "ssm_c_re", "ssm_c_im", "ssm_d", "b_glu", "b_mix_out", "ln1_g", "ln1_b", "ln2_g", "ln2_b",
         "b_ff1", "b_ff2", "ln3_g", "ln3_b")
WEIGHT_ORDER = ("ln_in_g", "ln_in_b", "w_in", "b_in", "ssm_log_dt", "ssm_a_re", "ssm_a_im", "ssm_b_re",
                "ssm_b_im", "ssm_c_re", "ssm_c_im", "ssm_d", "w_glu", "b_glu", "w_att_up", "w_mix_out",
                "b_mix_out", "ln1_g", "ln1_b", "w_xq", "w_xkv", "w_xo", "ln2_g", "ln2_b", "w_ff1", "b_ff1",
                "w_ff2", "b_ff2", "ln3_g", "ln3_b")


def _cparams(n_axes):
    return pltpu.CompilerParams(dimension_semantics=("arbitrary",) * n_axes, vmem_limit_bytes=VMEM_LIMIT)


class _Carry:
    def __init__(self, ins, outs, n_sems, start, finish, done):
        self.ins, self.outs, self.n_sems, self.start, self.finish, self.done = ins, outs, n_sems, start, finish, done


def _call(name, body, grid, in_specs, out_specs, out_shape, args, scratch_shapes=(), carry=None):
    in_specs, out_specs, out_shape = list(in_specs), list(out_specs), list(out_shape)
    params = _cparams(len(grid))
    if carry is None:
        return pl.pallas_call(body, name=name, grid=grid, in_specs=in_specs, out_specs=out_specs, out_shape=out_shape,
                              scratch_shapes=list(scratch_shapes), compiler_params=params)(*args)
    n_in, n_out, n_ci, n_co = len(in_specs), len(out_specs), len(carry.ins), len(carry.outs)
    n_scr = len(scratch_shapes)

    def wrapped(*refs):
        ins, c_in = refs[:n_in], refs[n_in:n_in + n_ci]
        outs, c_out = refs[n_in + n_ci:n_in + n_ci + n_out], refs[n_in + n_ci + n_out:n_in + n_ci + n_out + n_co]
        scratch = refs[n_in + n_ci + n_out + n_co:n_in + n_ci + n_out + n_co + n_scr]
        send_sems, recv_sems = refs[-2:]
        ids = [pl.program_id(a) for a in range(len(grid))]
        first = functools.reduce(jnp.logical_and, [i == 0 for i in ids])
        last = functools.reduce(jnp.logical_and, [i == g - 1 for i, g in zip(ids, grid)])

        @pl.when(first)
        def _():
            carry.start(c_in, c_out, send_sems, recv_sems)

        body(*ins, *outs, *scratch)

        @pl.when(last)
        def _():
            carry.finish(c_in, c_out, send_sems, recv_sems)

    c_shapes = [jax.ShapeDtypeStruct(carry.ins[o].shape, carry.ins[o].dtype) if isinstance(o, int) else o
                for o in carry.outs]
    aliases = {n_in + o: n_out + i for i, o in enumerate(carry.outs) if isinstance(o, int)}
    res = pl.pallas_call(
        wrapped, name=name, grid=grid, in_specs=in_specs + [HBM_SPEC] * n_ci, out_specs=out_specs + [HBM_SPEC] * n_co,
        out_shape=out_shape + c_shapes, input_output_aliases=aliases,
        scratch_shapes=list(scratch_shapes) + [pltpu.SemaphoreType.DMA((carry.n_sems,))] * 2,
        compiler_params=params)(*args, *carry.ins)
    carry.done(res[n_out:])
    return res[:n_out]


def _rowwise(name, fn, rows, consts, outs, reds=(), tm=256, touts=(), carry=None):
    n_rows = (rows[0][0] if isinstance(rows[0], tuple) else rows[0]).shape[-2]
    tm = min(tm, n_rows)
    assert n_rows % tm == 0, (name, n_rows, tm)
    specs, args = [], []
    for r in rows:
        if isinstance(r, tuple) and len(r) == 3:
            arr, width, cb = r
            specs.append(pl.BlockSpec((tm, width), functools.partial(lambda i, cb: (i, cb), cb=cb)))
        elif isinstance(r, tuple):
            arr, slot = r
            specs.append(pl.BlockSpec((None, tm, arr.shape[2]), functools.partial(lambda i, s: (s, i, 0), s=slot)))
        else:
            arr = r
            specs.append(pl.BlockSpec((tm, arr.shape[1]), lambda i: (i, 0)))
        args.append(arr)
        assert arr.shape[-2] == n_rows, (name, arr.shape, n_rows)
    for cst in consts:
        specs.append(pl.BlockSpec(cst.shape, lambda i: (0, 0)))
        args.append(cst)
    n_r, n_c, n_o, n_d = len(rows), len(consts), len(outs) + len(touts), len(reds)
    out_shape = [jax.ShapeDtypeStruct((n_rows, c), dt) for c, dt in outs]
    out_specs = [pl.BlockSpec((tm, c), lambda i: (i, 0)) for c, _ in outs]
    out_shape += [jax.ShapeDtypeStruct((r, n_rows), dt) for r, dt in touts]
    out_specs += [pl.BlockSpec((r, tm), lambda i: (0, i)) for r, _ in touts]
    out_shape += [jax.ShapeDtypeStruct((1, c), F32) for c in reds]
    out_specs += [pl.BlockSpec((1, c), lambda i: (0, 0)) for c in reds]

    def body(*refs):
        ins = [r[...] for r in refs[:n_r + n_c]]
        o_refs = refs[n_r + n_c:n_r + n_c + n_o]
        d_refs = refs[n_r + n_c + n_o:]
        res = fn(*ins)
        res = res if isinstance(res, (tuple, list)) else (res,)
        assert len(res) == n_o + n_d, (name, len(res))
        for ref, val in zip(o_refs, res[:n_o]):
            ref[...] = val.astype(ref.dtype)
        first = pl.program_id(0) == 0
        for ref, val in zip(d_refs, res[n_o:]):
            @pl.when(first)
            def _(ref=ref, val=val):
                ref[...] = val

            @pl.when(jnp.logical_not(first))
            def _(ref=ref, val=val):
                ref[...] += val

    return _call(name, body, (n_rows // tm,), specs, out_specs, out_shape, args, carry=carry)


def _colsum(v):
    return jnp.sum(v.astype(F32), axis=0, keepdims=True)


_DIMS = {"nn": (((1,), (0,)), ((), ())), "nt": (((1,), (1,)), ((), ())), "tn": (((0,), (0,)), ((), ()))}


def _tile(dim, want):
    if dim <= want:
        return dim
    return max(t for t in range(128, want + 1, 128) if dim % t == 0)


def _dot(a, b, mode):
    return lax.dot_general(a.astype(MXU_DTYPE), b.astype(MXU_DTYPE), _DIMS[mode], preferred_element_type=F32)


def _mm(name, a, b, mode, *, bias=None, extras=(), epilogue=None, out_dtypes=(F32,), tm=1024, tn=1024, tk=1024,
        carry=None):
    if mode == "nn":
        (m, k), (_, n) = a.shape, b.shape
    elif mode == "nt":
        (m, k), (n, _) = a.shape, b.shape
    else:
        (k, m), (_, n) = a.shape, b.shape
    tn = _tile(n, tn)
    if mode != "tn":
        tk = k if k <= 1024 else tk
    tk = _tile(k, tk)
    nk = k // tk

    def vmem_bytes(rows):
        blocks = rows * tk * a.dtype.itemsize + tk * tn * b.dtype.itemsize
        blocks += sum(rows * tn * e.dtype.itemsize for e in extras)
        blocks += sum(rows * tn * jnp.dtype(dt).itemsize for dt in out_dtypes)
        return 2 * blocks + (rows * tn * 4 if nk > 1 else 0)

    tm = _tile(m, tm if mode == "tn" else 2 * tm)
    while vmem_bytes(tm) > 3 * VMEM_LIMIT // 4 and tm % 256 == 0:
        tm //= 2
    assert m % tm == 0 and n % tn == 0 and k % tk == 0, (name, m, n, k)
    a_spec = {"nn": pl.BlockSpec((tm, tk), lambda i, j, kk: (i, kk)),
              "nt": pl.BlockSpec((tm, tk), lambda i, j, kk: (i, kk)),
              "tn": pl.BlockSpec((tk, tm), lambda i, j, kk: (kk, i))}[mode]
    b_spec = {"nn": pl.BlockSpec((tk, tn), lambda i, j, kk: (kk, j)),
              "nt": pl.BlockSpec((tn, tk), lambda i, j, kk: (j, kk)),
              "tn": pl.BlockSpec((tk, tn), lambda i, j, kk: (kk, j))}[mode]
    specs, args = [a_spec, b_spec], [a, b]
    if bias is not None:
        specs.append(pl.BlockSpec((1, tn), lambda i, j, kk: (0, j)))
        args.append(bias)
    for e in extras:
        specs.append(pl.BlockSpec((tm, tn), lambda i, j, kk: (i, j)))
        args.append(e)
    n_e, n_o = len(extras), len(out_dtypes)
    has_bias = bias is not None

    def body(*refs):
        a_ref, b_ref = refs[0], refs[1]
        pos = 2
        bias_ref = refs[pos] if has_bias else None
        pos += int(has_bias)
        e_refs = refs[pos:pos + n_e]
        o_refs = refs[pos + n_e:pos + n_e + n_o]
        acc_ref = refs[pos + n_e + n_o] if nk > 1 else None
        part = _dot(a_ref[...], b_ref[...], mode)

        def finish(r):
            if has_bias:
                r = r + bias_ref[...]
            res = epilogue(r, *[e[...] for e in e_refs]) if epilogue is not None else (r,)
            for ref, val in zip(o_refs, res):
                ref[...] = val.astype(ref.dtype)

        if nk == 1:
            finish(part)
        else:
            kk = pl.program_id(2)

            @pl.when(kk == 0)
            def _():
                acc_ref[...] = part

            @pl.when(kk > 0)
            def _():
                acc_ref[...] += part

            @pl.when(kk == nk - 1)
            def _():
                finish(acc_ref[...])

    res = _call(name, body, (m // tm, n // tn, nk), specs,
                [pl.BlockSpec((tm, tn), lambda i, j, kk: (i, j)) for _ in out_dtypes],
                [jax.ShapeDtypeStruct((m, n), dt) for dt in out_dtypes], args,
                scratch_shapes=[pltpu.VMEM((tm, tn), F32)] if nk > 1 else [], carry=carry)
    return res[0] if n_o == 1 else res


def _ssm_wgrad(name, chan, st_re, st_im, expand, tk=512):
    s = chan.shape[0]
    tk = min(tk, s)
    nk = s // tk
    oshape = (CH_W, 2 * CH_N) if expand else (2 * CH_N, CH_W)

    def body(c_ref, re_ref, im_ref, o_ref):
        c = c_ref[...]
        if expand:
            part = jnp.concatenate([_dot(c, re_ref[...], "tn"), _dot(c, im_ref[...], "tn")], axis=1)
        else:
            part = jnp.concatenate([_dot(re_ref[...], c, "tn"), _dot(im_ref[...], c, "tn")], axis=0)
        kk = pl.program_id(1)

        @pl.when(kk == 0)
        def _():
            o_ref[...] = part

        @pl.when(kk > 0)
        def _():
            o_ref[...] += part

    return pl.pallas_call(
        body, name=name, grid=(SSM_CHUNKS, nk),
        in_specs=[pl.BlockSpec((tk, CH_W), lambda j, kk: (kk, j)), pl.BlockSpec((tk, CH_N), lambda j, kk: (kk, j)),
                  pl.BlockSpec((tk, CH_N), lambda j, kk: (kk, j))],
        out_specs=pl.BlockSpec((None,) + oshape, lambda j, kk: (j, 0, 0)),
        out_shape=jax.ShapeDtypeStruct((SSM_CHUNKS,) + oshape, F32),
        compiler_params=_cparams(2))(chan, st_re, st_im)


SCAN_LB = 256


def _split_by_scan_block(mat, axis):
    halves = []
    for l in range(CH_N // SCAN_LB):
        re = lax.slice_in_dim(mat, l * SCAN_LB, (l + 1) * SCAN_LB, axis=axis)
        im = lax.slice_in_dim(mat, CH_N + l * SCAN_LB, CH_N + (l + 1) * SCAN_LB, axis=axis)
        halves.append(jnp.concatenate([re, im], axis=axis))
    return jnp.stack(halves, axis=1).reshape((-1,) + halves[0].shape[1:])


def _ssm_scan(name, chan, expand12, contract12, a_re, a_im, d_row, reverse, carry=None):
    s = chan.shape[0]
    seg_len = s // N_SEG
    n_sq = int(math.log2(seg_len))
    assert 2 ** n_sq == seg_len
    rb = min(512, s)
    per_chunk = CH_N // SCAN_LB

    def body(are_ref, aim_ref, ch_ref, e_ref, k_ref, d_ref, hre_ref, him_ref, o_ref, wre_ref, wim_ref, ere, eim, cre, cim):
        e_mat, k_mat = e_ref[...], k_ref[...]
        for r in range(s // rb):
            rows = slice(r * rb, (r + 1) * rb)
            c = ch_ref[rows, :]
            if reverse:
                wre_ref[rows, :] = _dot(c, e_mat[:SCAN_LB], "nt")
                wim_ref[rows, :] = _dot(c, e_mat[SCAN_LB:], "nt")
            else:
                wre_ref[rows, :] = _dot(c, e_mat[:, :SCAN_LB], "nn")
                wim_ref[rows, :] = _dot(c, e_mat[:, SCAN_LB:], "nn")

        ar1 = are_ref[...]
        ai1 = -aim_ref[...] if reverse else aim_ref[...]
        ar = jnp.broadcast_to(ar1, (N_SEG, SCAN_LB))
        ai = jnp.broadcast_to(ai1, (N_SEG, SCAN_LB))

        def rows_of(k):
            kk = seg_len - 1 - k if reverse else k
            return pl.ds(pl.multiple_of(kk * N_SEG, N_SEG), N_SEG)

        def local(k, carry):
            hr, hi = carry
            rows = rows_of(k)
            nr = ar * hr - ai * hi + wre_ref[rows, :]
            ni = ar * hi + ai * hr + wim_ref[rows, :]
            hre_ref[rows, :] = nr
            him_ref[rows, :] = ni
            return nr, ni

        zero = jnp.zeros((N_SEG, SCAN_LB), F32)
        er, ei = lax.fori_loop(0, seg_len, local, (zero, zero))
        ere[...] = er
        eim[...] = ei
        pr, pi = ar1, ai1
        for _ in range(n_sq):
            pr, pi = pr * pr - pi * pi, 2.0 * pr * pi
        cr = jnp.zeros((1, SCAN_LB), F32)
        ci = jnp.zeros((1, SCAN_LB), F32)
        for jj in range(N_SEG):
            j = N_SEG - 1 - jj if reverse else jj
            cre[j:j + 1, :] = cr
            cim[j:j + 1, :] = ci
            er_j, ei_j = ere[j:j + 1, :], eim[j:j + 1, :]
            cr, ci = pr * cr - pi * ci + er_j, pr * ci + pi * cr + ei_j
        c_r, c_i = cre[...], cim[...]

        def fix(k, carry):
            qr, qi = carry
            rows = rows_of(k)
            hre_ref[rows, :] = hre_ref[rows, :] + (qr * c_r - qi * c_i)
            him_ref[rows, :] = him_ref[rows, :] + (qr * c_i + qi * c_r)
            return qr * ar - qi * ai, qr * ai + qi * ar

        lax.fori_loop(0, seg_len, fix, (ar, ai))

        first_of_chunk = lax.rem(pl.program_id(0), per_chunk) == 0
        for r in range(s // rb):
            rows = slice(r * rb, (r + 1) * rb)
            if reverse:
                part = (_dot(hre_ref[rows, :], k_mat[:, :SCAN_LB], "nt")
                        + _dot(him_ref[rows, :], k_mat[:, SCAN_LB:], "nt"))
            else:
                part = _dot(hre_ref[rows, :], k_mat[:SCAN_LB], "nn") + _dot(him_ref[rows, :], k_mat[SCAN_LB:], "nn")

            @pl.when(first_of_chunk)
            def _(rows=rows, part=part):
                o_ref[rows, :] = part + d_ref[...] * ch_ref[rows, :]

            @pl.when(jnp.logical_not(first_of_chunk))
            def _(rows=rows, part=part):
                o_ref[rows, :] += part

    nblk = N_STATE // SCAN_LB
    blk = pl.BlockSpec((s, SCAN_LB), lambda b: (0, b))
    row = pl.BlockSpec((1, SCAN_LB), lambda b: (0, b))
    chan_blk = pl.BlockSpec((s, CH_W), lambda b: (0, b // per_chunk))
    res = _call(name, body, (nblk,),
                [row, row, chan_blk, pl.BlockSpec((None,) + expand12.shape[1:], lambda b: (b, 0, 0)),
                 pl.BlockSpec((None,) + contract12.shape[1:], lambda b: (b, 0, 0)),
                 pl.BlockSpec((1, CH_W), lambda b: (0, b // per_chunk))],
                [blk, blk, chan_blk],
                [jax.ShapeDtypeStruct((s, N_STATE), F32)] * 2 + [jax.ShapeDtypeStruct((s, SSM_WIDTH), F32)],
                (a_re, a_im, chan, expand12, contract12, d_row),
                scratch_shapes=[pltpu.VMEM((s, SCAN_LB), F32)] * 2 + [pltpu.VMEM((N_SEG, SCAN_LB), F32)] * 4, carry=carry)
    return res[0], res[1], res[2]


def _ssm_da(g_re, g_im, h_re, h_im):
    s = g_re.shape[0]
    seg_len = s // N_SEG

    def body(gre_ref, gim_ref, hre_ref, him_ref, dre_ref, dim_ref):
        def rows_of(k):
            return pl.ds(pl.multiple_of(k * N_SEG, N_SEG), N_SEG)

        def step(k, carry):
            sr, si = carry
            gr, gi = gre_ref[rows_of(k), :], gim_ref[rows_of(k), :]
            pr, pi = hre_ref[rows_of(k - 1), :], him_ref[rows_of(k - 1), :]
            return sr + gr * pr + gi * pi, si + gi * pr - gr * pi

        zero = jnp.zeros((N_SEG, SCAN_LB), F32)
        sr, si = lax.fori_loop(1, seg_len, step, (zero, zero))
        last = pl.ds((seg_len - 1) * N_SEG, N_SEG)
        first_row = lax.broadcasted_iota(jnp.int32, (N_SEG, SCAN_LB), 0) == 0
        pr = jnp.where(first_row, 0.0, pltpu.roll(hre_ref[last, :], 1, 0))
        pi = jnp.where(first_row, 0.0, pltpu.roll(him_ref[last, :], 1, 0))
        gr, gi = gre_ref[pl.ds(0, N_SEG), :], gim_ref[pl.ds(0, N_SEG), :]
        sr = sr + gr * pr + gi * pi
        si = si + gi * pr - gr * pi
        dre_ref[...] = jnp.sum(sr, axis=0, keepdims=True)
        dim_ref[...] = jnp.sum(si, axis=0, keepdims=True)

    nblk = N_STATE // SCAN_LB
    blk = pl.BlockSpec((s, SCAN_LB), lambda b: (0, b))
    row = pl.BlockSpec((1, SCAN_LB), lambda b: (0, b))
    return pl.pallas_call(
        body, name="ssm_da", grid=(nblk,), in_specs=[blk] * 4, out_specs=[row, row],
        out_shape=[jax.ShapeDtypeStruct((1, N_STATE), F32)] * 2,
        compiler_params=_cparams(1))(g_re, g_im, h_re, h_im)


def _disc(ldt, are, aim, bre, bim):
    dt = jnp.exp(ldt)
    mag = jnp.exp(are * dt)
    abr = mag * jnp.cos(aim * dt)
    abi = mag * jnp.sin(aim * dt)
    den = jnp.square(are) + jnp.square(aim)
    nr = abr - 1.0
    fre = (nr * are + abi * aim) / den
    fim = (abi * are - nr * aim) / den
    return abr, abi, fre * bre - fim * bim, fre * bim + fim * bre


def _ssm_disc_fwd(ldt, are, aim, bre, bim):
    def body(l_ref, ar_ref, ai_ref, br_ref, bi_ref, o0, o1, o2, o3):
        res = _disc(l_ref[...], ar_ref[...], ai_ref[...], br_ref[...], bi_ref[...])
        for ref, val in zip((o0, o1, o2, o3), res):
            ref[...] = val

    col = jax.ShapeDtypeStruct((N_STATE, 1), F32)
    mat = jax.ShapeDtypeStruct((N_STATE, SSM_GROUP), F32)
    return pl.pallas_call(body, name="ssm_disc_fwd", out_shape=[col, col, mat, mat],
                          in_specs=[VMEM_SPEC] * 5, out_specs=[VMEM_SPEC] * 4)(ldt, are, aim, bre, bim)


def _ssm_disc_bwd(ldt, are, aim, bre, bim, d_abr, d_abi, d_bbr, d_bbi):
    def body(l_ref, ar_ref, ai_ref, br_ref, bi_ref, c0, c1, c2, c3, g_ldt, g_are, g_aim, g_bre, g_bim):
        _, vjp = jax.vjp(_disc, l_ref[...], ar_ref[...], ai_ref[...], br_ref[...], bi_ref[...])
        dl, dar, dai, dbr, dbi = vjp((c0[...], c1[...], c2[...], c3[...]))
        state = lax.broadcasted_iota(jnp.int32, (N_STATE, SSM_GROUPS), 0)
        group = lax.broadcasted_iota(jnp.int32, (N_STATE, SSM_GROUPS), 1)
        pick = jnp.right_shift(state, 6) == group
        g_ldt[...] = jnp.sum(jnp.where(pick, dl, 0.0), axis=0, keepdims=True)
        g_are[...] = dar
        g_aim[...] = dai
        g_bre[...] = dbr
        g_bim[...] = dbi

    col = jax.ShapeDtypeStruct((N_STATE, 1), F32)
    mat = jax.ShapeDtypeStruct((N_STATE, SSM_GROUP), F32)
    return pl.pallas_call(body, name="ssm_disc_bwd",
                          out_shape=[jax.ShapeDtypeStruct((1, SSM_GROUPS), F32), col, col, mat, mat],
                          in_specs=[VMEM_SPEC] * 9, out_specs=[VMEM_SPEC] * 5,
                          compiler_params=pltpu.CompilerParams(vmem_limit_bytes=VMEM_LIMIT))(
        ldt, are, aim, bre, bim, d_abr, d_abi, d_bbr, d_bbi)


_EYE8 = np.eye(8, dtype=np.float32)


def _blockdiag_b(bb):
    t = bb.reshape(SSM_CHUNKS, 8, SSM_STATE, SSM_GROUP).transpose(0, 1, 3, 2)
    return jnp.einsum("igcn,gh->igchn", t, _EYE8).reshape(SSM_CHUNKS, CH_W, CH_N)


def _diag_of_b(m):
    t = jnp.einsum("igchn,gh->igcn", m.reshape(SSM_CHUNKS, 8, SSM_GROUP, 8, SSM_STATE), _EYE8)
    return t.transpose(0, 1, 3, 2).reshape(N_STATE, SSM_GROUP)


def _blockdiag_c(c):
    t = c.reshape(SSM_CHUNKS, 8, SSM_GROUP, SSM_STATE).transpose(0, 1, 3, 2)
    return jnp.einsum("ignc,gh->ignhc", t, _EYE8).reshape(SSM_CHUNKS, CH_N, CH_W)


def _diag_of_c(m):
    t = jnp.einsum("ignhc,gh->ignc", m.reshape(SSM_CHUNKS, 8, SSM_STATE, 8, SSM_GROUP), _EYE8)
    return t.transpose(0, 1, 3, 2).reshape(SSM_GROUPS, SSM_GROUP, SSM_STATE)


def _time_perm(a):
    s, c = a.shape
    return a.reshape(N_SEG, s // N_SEG, c).transpose(1, 0, 2).reshape(s, c)


def _time_unperm(a):
    s, c = a.shape
    return a.reshape(s // N_SEG, N_SEG, c).transpose(1, 0, 2).reshape(s, c)


def _dilate(a, d):
    s, c = a.shape
    return a if d == 1 else a.reshape(s // d, d, c).transpose(1, 0, 2).reshape(s, c)


def _undilate(a, d):
    s, c = a.shape
    return a if d == 1 else a.reshape(d, s // d, c).transpose(1, 0, 2).reshape(s, c)


def _dilate_rows(a, d):
    r, s = a.shape
    return a if d == 1 else a.reshape(r, s // d, d).transpose(0, 2, 1).reshape(r, s)


ATT_T = 4
ATT_ROWS = ATT_T * ATT_BLK


def _window(prev_ref, cur_ref, i, sl):
    if i == 0:
        return jnp.concatenate([prev_ref[:, sl], cur_ref[0:ATT_BLK, sl]], axis=0)
    return cur_ref[(i - 1) * ATT_BLK:(i + 1) * ATT_BLK, sl]


def _band_valid(first_key):
    qi = lax.broadcasted_iota(jnp.int32, (ATT_BLK, 2 * ATT_BLK), 0)
    ki = lax.broadcasted_iota(jnp.int32, (ATT_BLK, 2 * ATT_BLK), 1)
    steps = qi + ATT_BLK - ki
    return (steps >= 0) & (steps <= ATT_BLK) & (ki >= first_key)


ATT_STATW = ATT_HPG * 128


def _stat(h):
    return slice(h * 128, (h + 1) * 128)


def _stat_rows(stat):
    n = stat.shape[0]
    heads = [stat[:, _stat(h)].T[0:1, :] for h in range(ATT_HPG)]
    return jnp.concatenate(heads + [jnp.zeros((8 - ATT_HPG, n), stat.dtype)], axis=0)


def _attn_specs(nb, width=ATT_GROUPW):
    cur = pl.BlockSpec((ATT_ROWS, width), lambda b: (b, 0))
    prev = pl.BlockSpec((ATT_BLK, width), lambda b: (jnp.maximum(b * ATT_T - 1, 0), 0))
    nxt = pl.BlockSpec((ATT_BLK, width), lambda b: (jnp.minimum((b + 1) * ATT_T, nb - 1), 0))
    return cur, prev, nxt


def _attn_fwd(tag, per_seq, q, k, v):
    s = q.shape[0]
    nb = s // ATT_BLK

    def body(q_ref, kc_ref, kp_ref, vc_ref, vp_ref, o_ref, lse_ref):
        bt = pl.program_id(0)
        for i in range(ATT_T):
            has_prev = lax.rem(bt * ATT_T + i, per_seq) > 0
            valid = _band_valid(jnp.where(has_prev, 0, ATT_BLK))
            rows = slice(i * ATT_BLK, (i + 1) * ATT_BLK)
            for h in range(ATT_HPG):
                sl = slice(h * ATT_HEAD_DIM, (h + 1) * ATT_HEAD_DIM)
                kcat = _window(kp_ref, kc_ref, i, sl)
                vcat = _window(vp_ref, vc_ref, i, sl)
                sc = _dot(q_ref[rows, sl], kcat, "nt") * ATT_SCALE
                sc = jnp.where(valid, sc, NEG_INF)
                m = jnp.max(sc, axis=-1, keepdims=True)
                p = jnp.exp(sc - m)
                den = jnp.sum(p, axis=-1, keepdims=True)
                o_ref[rows, sl] = _dot(p, vcat, "nn") / den
                lse_ref[rows, _stat(h)] = jnp.broadcast_to(m + jnp.log(den), (ATT_BLK, 128))

    cur, prev, _ = _attn_specs(nb)
    stat, _, _ = _attn_specs(nb, ATT_STATW)
    return pl.pallas_call(
        body, name="attn_fwd_" + tag, grid=(nb // ATT_T,), in_specs=[cur, cur, prev, cur, prev], out_specs=[cur, stat],
        out_shape=[jax.ShapeDtypeStruct((s, ATT_GROUPW), F32), jax.ShapeDtypeStruct((s, ATT_STATW), F32)],
        compiler_params=_cparams(1))(q, k, k, v, v)


def _attn_dq(tag, per_seq, q, k, v, do, lse, delta):
    s = q.shape[0]
    nb = s // ATT_BLK

    def body(q_ref, kc_ref, kp_ref, vc_ref, vp_ref, do_ref, lse_ref, dl_ref, dq_ref):
        bt = pl.program_id(0)
        for i in range(ATT_T):
            has_prev = lax.rem(bt * ATT_T + i, per_seq) > 0
            valid = _band_valid(jnp.where(has_prev, 0, ATT_BLK))
            rows = slice(i * ATT_BLK, (i + 1) * ATT_BLK)
            for h in range(ATT_HPG):
                sl = slice(h * ATT_HEAD_DIM, (h + 1) * ATT_HEAD_DIM)
                kcat = _window(kp_ref, kc_ref, i, sl)
                vcat = _window(vp_ref, vc_ref, i, sl)
                lse = jnp.concatenate([lse_ref[rows, _stat(h)]] * 2, axis=1)
                dlt = jnp.concatenate([dl_ref[rows, _stat(h)]] * 2, axis=1)
                sc = _dot(q_ref[rows, sl], kcat, "nt") * ATT_SCALE
                p = jnp.exp(jnp.where(valid, sc, NEG_INF) - lse)
                dp = _dot(do_ref[rows, sl], vcat, "nt")
                ds = p * (dp - dlt) * ATT_SCALE
                dq_ref[rows, sl] = _dot(ds, kcat, "nn")

    cur, prev, _ = _attn_specs(nb)
    stat, _, _ = _attn_specs(nb, ATT_STATW)
    return pl.pallas_call(
        body, name="attn_dq_" + tag, grid=(nb // ATT_T,), in_specs=[cur, cur, prev, cur, prev, cur, stat, stat],
        out_specs=cur, out_shape=jax.ShapeDtypeStruct((s, ATT_GROUPW), F32),
        compiler_params=_cparams(1))(q, k, k, v, v, do, lse, delta)


def _attn_dkv(tag, per_seq, q, k, v, do, lse_t, delta_t):
    s = q.shape[0]
    nb = s // ATT_BLK

    def body(k_ref, v_ref, qc_ref, qn_ref, doc_ref, don_ref, lc_ref, ln_ref, dc_ref, dn_ref, dk_ref, dv_ref):
        bt = pl.program_id(0)
        ki = lax.broadcasted_iota(jnp.int32, (ATT_BLK, 2 * ATT_BLK), 0)
        ci = lax.broadcasted_iota(jnp.int32, (ATT_BLK, 2 * ATT_BLK), 1)

        def pair(edge_ref, cur_ref, i, sl):
            if i == ATT_T - 1:
                return jnp.concatenate([cur_ref[i * ATT_BLK:(i + 1) * ATT_BLK, sl], edge_ref[:, sl]], axis=0)
            return cur_ref[i * ATT_BLK:(i + 2) * ATT_BLK, sl]

        def pair_row(edge_ref, cur_ref, i, h):
            if i == ATT_T - 1:
                row = jnp.concatenate([cur_ref[h:h + 1, i * ATT_BLK:(i + 1) * ATT_BLK], edge_ref[h:h + 1, :]], axis=1)
            else:
                row = cur_ref[h:h + 1, i * ATT_BLK:(i + 2) * ATT_BLK]
            return jnp.broadcast_to(row, (ATT_BLK, 2 * ATT_BLK))

        for i in range(ATT_T):
            b = bt * ATT_T + i
            next_uses = (b + 1 < nb) & (lax.rem(b + 1, per_seq) > 0)
            reach = jnp.where(next_uses, 0, 4 * ATT_BLK)
            valid = ((ci < ATT_BLK) & (ci >= ki)) | ((ci >= ATT_BLK) & (ki - ci + ATT_BLK >= reach))
            rows = slice(i * ATT_BLK, (i + 1) * ATT_BLK)
            for h in range(ATT_HPG):
                sl = slice(h * ATT_HEAD_DIM, (h + 1) * ATT_HEAD_DIM)
                qcat, docat = pair(qn_ref, qc_ref, i, sl), pair(don_ref, doc_ref, i, sl)
                sc = _dot(k_ref[rows, sl], qcat, "nt") * ATT_SCALE
                p = jnp.exp(jnp.where(valid, sc, NEG_INF) - pair_row(ln_ref, lc_ref, i, h))
                dv_ref[rows, sl] = _dot(p, docat, "nn")
                dp = _dot(v_ref[rows, sl], docat, "nt")
                ds = p * (dp - pair_row(dn_ref, dc_ref, i, h)) * ATT_SCALE
                dk_ref[rows, sl] = _dot(ds, qcat, "nn")

    cur, _, nxt = _attn_specs(nb)
    stat = pl.BlockSpec((8, ATT_ROWS), lambda b: (0, b))
    snxt = pl.BlockSpec((8, ATT_BLK), lambda b: (0, jnp.minimum((b + 1) * ATT_T, nb - 1)))
    return pl.pallas_call(
        body, name="attn_dkv_" + tag, grid=(nb // ATT_T,), in_specs=[cur, cur, cur, nxt, cur, nxt, stat, snxt, stat, snxt],
        out_specs=[cur, cur], out_shape=[jax.ShapeDtypeStruct((s, ATT_GROUPW), F32)] * 2,
        compiler_params=_cparams(1))(k, v, q, q, do, do, lse_t, lse_t, delta_t, delta_t)


def _xattn_probs(q, kh):
    sc = _dot(q, kh, "nt") * XATT_SCALE
    e = jnp.exp(sc - jnp.max(sc, axis=-1, keepdims=True))
    return e / jnp.sum(e, axis=-1, keepdims=True)


def _xattn_fwd(q, kv, tm=512):
    s = q.shape[0]
    tm = min(tm, s)

    def body(q_ref, kv_ref, o_ref):
        for h in range(XATT_HEADS):
            sl = slice(h * XATT_HEAD_DIM, (h + 1) * XATT_HEAD_DIM)
            vs = slice(D_MODEL + h * XATT_HEAD_DIM, D_MODEL + (h + 1) * XATT_HEAD_DIM)
            p = _xattn_probs(q_ref[:, sl], kv_ref[:, sl])
            o_ref[:, sl] = _dot(p, kv_ref[:, vs], "nn").astype(o_ref.dtype)

    return pl.pallas_call(
        body, name="xattn_fwd", grid=(s // tm,),
        in_specs=[pl.BlockSpec((tm, D_MODEL), lambda i: (i, 0)), pl.BlockSpec(kv.shape, lambda i: (0, 0))],
        out_specs=pl.BlockSpec((tm, D_MODEL), lambda i: (i, 0)),
        out_shape=jax.ShapeDtypeStruct((s, D_MODEL), MXU_DTYPE), compiler_params=_cparams(1))(q, kv)


def _xattn_bwd(q, kv, do, tm=512):
    s = q.shape[0]
    tm = min(tm, s)

    def body(q_ref, kv_ref, do_ref, dq_ref, dkv_ref):
        first = pl.program_id(0) == 0

        @pl.when(first)
        def _():
            dkv_ref[...] = jnp.zeros_like(dkv_ref)

        for h in range(XATT_HEADS):
            sl = slice(h * XATT_HEAD_DIM, (h + 1) * XATT_HEAD_DIM)
            vs = slice(D_MODEL + h * XATT_HEAD_DIM, D_MODEL + (h + 1) * XATT_HEAD_DIM)
            p = _xattn_probs(q_ref[:, sl], kv_ref[:, sl])
            dkv_ref[:, vs] += _dot(p, do_ref[:, sl], "tn")
            dp = _dot(do_ref[:, sl], kv_ref[:, vs], "nt")
            ds = p * (dp - jnp.sum(dp * p, axis=-1, keepdims=True)) * XATT_SCALE
            dq_ref[:, sl] = _dot(ds, kv_ref[:, sl], "nn").astype(dq_ref.dtype)
            dkv_ref[:, sl] += _dot(ds, q_ref[:, sl], "tn")

    row = pl.BlockSpec((tm, D_MODEL), lambda i: (i, 0))
    whole = pl.BlockSpec(kv.shape, lambda i: (0, 0))
    return pl.pallas_call(
        body, name="xattn_bwd", grid=(s // tm,), in_specs=[row, whole, row], out_specs=[row, whole],
        out_shape=[jax.ShapeDtypeStruct((s, D_MODEL), MXU_DTYPE), jax.ShapeDtypeStruct(kv.shape, F32)],
        compiler_params=_cparams(1))(q, kv, do)


def _ln(x, g, b):
    mu = jnp.mean(x, axis=-1, keepdims=True)
    xc = x - mu
    var = jnp.mean(jnp.square(xc), axis=-1, keepdims=True)
    return xc * lax.rsqrt(var + LN_EPS) * g + b


def _res_ln(h, o, g, b):
    return _ln(DEEPNORM_ALPHA * h + o, g, b)


def _gate(gs, ga, z1, z2, batt):
    return jax.nn.sigmoid(gs) * (z1 * jax.nn.sigmoid(z2)) + jax.nn.sigmoid(ga) * batt


def _rope_tables(pos, invf, m1, m2):
    ang = pos.astype(F32) * invf
    sin = jnp.sin(ang)
    return jnp.cos(ang), -sin * m1, sin * m2


def _rope(t, cos, s_up, s_dn):
    w = t.shape[-1]
    return t * cos + pltpu.roll(t, w - ROT_DIM // 2, 1) * s_up + pltpu.roll(t, ROT_DIM // 2, 1) * s_dn


def _rope_t(dt, cos, s_up, s_dn):
    w = dt.shape[-1]
    return dt * cos + pltpu.roll(dt * s_up, ROT_DIM // 2, 1) + pltpu.roll(dt * s_dn, w - ROT_DIM // 2, 1)


def _rope_consts():
    inv_freq = ROPE_THETA ** (-jnp.arange(0, ROT_DIM, 2, dtype=F32) / ROT_DIM)
    d = np.arange(ATT_GROUPW) % ATT_HEAD_DIM
    invf = jnp.where(d < ROT_DIM, inv_freq[d % (ROT_DIM // 2)], 0.0).reshape(1, ATT_GROUPW).astype(F32)
    m1 = jnp.asarray((d < ROT_DIM // 2).astype(np.float32)).reshape(1, ATT_GROUPW)
    m2 = jnp.asarray(((d >= ROT_DIM // 2) & (d < ROT_DIM)).astype(np.float32)).reshape(1, ATT_GROUPW)
    return invf, m1, m2


def _head_sum_matrix():
    d = np.arange(ATT_GROUPW) // ATT_HEAD_DIM
    s = np.arange(ATT_STATW) // 128
    return jnp.asarray((d[:, None] == s[None, :]).astype(np.float32))


def _adamw(w, g, m, v):
    m = ADAM_B1 * m + (1.0 - ADAM_B1) * g
    v = ADAM_B2 * v + (1.0 - ADAM_B2) * jnp.square(g)
    m_hat = m / (1.0 - ADAM_B1 ** ADAM_STEP)
    v_hat = v / (1.0 - ADAM_B2 ** ADAM_STEP)
    delta = -ADAM_LR * (m_hat / (jnp.sqrt(v_hat) + ADAM_EPS) + ADAM_WD * w)
    return delta, m, v


def _local_step(x, mem, pos, target, sp, ex):
    s = x.shape[0]
    al = DEEPNORM_ALPHA
    mx = MXU_DTYPE

    h0, h0b = _rowwise("ln_in", lambda x, g, b: (lambda h: (h, h))(_ln(x, g, b)), [x],
                       [sp["ln_in_g"], sp["ln_in_b"]], [(D_MODEL, F32), (D_MODEL, mx)],
                       carry=ex.gather_carry(["w_in"]))
    proj = _mm("proj", h0b, ex.weight("w_in"), "nn", bias=sp["b_in"],
               carry=ex.gather_carry(["w_glu", "w_att_up", "w_mix_out", "w_xq", "w_xkv"]))

    ldt = jnp.repeat(sp["ssm_log_dt"].reshape(SSM_GROUPS), SSM_STATE).reshape(N_STATE, 1)
    are, aim = sp["ssm_a_re"].reshape(N_STATE, 1), sp["ssm_a_im"].reshape(N_STATE, 1)
    bre, bim = sp["ssm_b_re"].reshape(N_STATE, SSM_GROUP), sp["ssm_b_im"].reshape(N_STATE, SSM_GROUP)
    abr, abi, bbr, bbi = _ssm_disc_fwd(ldt, are, aim, bre, bim)
    a_re, a_im = abr.reshape(1, N_STATE), abi.reshape(1, N_STATE)
    bexp = jnp.concatenate([_blockdiag_b(bbr), _blockdiag_b(bbi)], axis=2).astype(mx)
    cexp = jnp.concatenate([_blockdiag_c(sp["ssm_c_re"].reshape(SSM_GROUPS, SSM_GROUP, SSM_STATE)),
                            -_blockdiag_c(sp["ssm_c_im"].reshape(SSM_GROUPS, SSM_GROUP, SSM_STATE))],
                           axis=1).astype(mx)
    u_p = _time_perm(proj[:, :SSM_WIDTH])
    b12, c12 = _split_by_scan_block(bexp, 2), _split_by_scan_block(cexp, 1)
    h_re, h_im, y_p = _ssm_scan("ssm_scan_fwd", u_p, b12, c12, a_re, a_im, sp["ssm_d"], reverse=False,
                                carry=ex.gather_carry(["w_ff1", "w_ff2"]))
    y = _time_unperm(y_p)
    ygb, = _rowwise("gelu", lambda y: jax.nn.gelu(y), [y], [], [(SSM_WIDTH, mx)])
    z = _mm("glu", ygb, ex.weight("w_glu"), "nn", bias=sp["b_glu"], carry=ex.gather_carry(["w_xo"]))

    invf, m1, m2 = _rope_consts()

    def rope_fwd(pos, q0, q1, q2, k0, k1, k2, v0, v1, v2, invf, m1, m2):
        tabs = _rope_tables(pos, invf, m1, m2)
        return tuple(_rope(t, *tabs) for t in (q0, q1, q2, k0, k1, k2)) + (v0, v1, v2)

    qkv_cols = [(proj, ATT_GROUPW, 3 + i) for i in range(9)]
    qkv = _rowwise("rope", rope_fwd, [pos] + qkv_cols, [invf, m1, m2], [(ATT_GROUPW, mx)] * 9)
    n_blocks = s // ATT_BLK
    groups = [(str(g), n_blocks // d, d) for g, d in enumerate(DILATIONS)]
    q_d = [_dilate(qkv[g], d) for g, d in enumerate(DILATIONS)]
    k_d = [_dilate(qkv[3 + g], d) for g, d in enumerate(DILATIONS)]
    v_d = [_dilate(qkv[6 + g], d) for g, d in enumerate(DILATIONS)]
    o_g, l_g = [], []
    for g, (tag, per_seq, d) in enumerate(groups):
        o, lse = _attn_fwd(tag, per_seq, q_d[g], k_d[g], v_d[g])
        o_g.append(_undilate(o, d))
        l_g.append(_undilate(lse, d))

    def merge(o0, o1, o2, l0, l1, l2):
        m = jnp.maximum(jnp.maximum(l0, l1), l2)
        e0, e1, e2 = jnp.exp(l0 - m), jnp.exp(l1 - m), jnp.exp(l2 - m)
        tot = e0 + e1 + e2

        def per_dim(e):
            w = e / tot
            return jnp.concatenate([w[:, h * 128:h * 128 + ATT_HEAD_DIM] for h in range(ATT_HPG)], axis=1)

        att = per_dim(e0) * o0 + per_dim(e1) * o1 + per_dim(e2) * o2
        lse = m + jnp.log(tot)
        return att, att, lse, _stat_rows(lse)

    att, attb, lse_tot, lse_tot_t = _rowwise("attn_merge", merge, o_g + l_g, [],
                                             [(ATT_GROUPW, F32), (ATT_GROUPW, mx), (ATT_STATW, F32)], touts=[(8, F32)])
    batt = _mm("att_up", attb, ex.weight("w_att_up"), "nn")

    gate_rows = [(proj, D_MODEL, 3), (proj, D_MODEL, 4), (z, D_MODEL, 0), (z, D_MODEL, 1), batt]
    mixedb, = _rowwise("gate", _gate, gate_rows, [], [(D_MODEL, mx)])
    o1 = _mm("mix_out", mixedb, ex.weight("w_mix_out"), "nn", bias=sp["b_mix_out"])
    h1, h1b = _rowwise("ln1", lambda h, o, g, b: (lambda r: (r, r))(_res_ln(h, o, g, b)), [h0, o1],
                       [sp["ln1_g"], sp["ln1_b"]], [(D_MODEL, F32), (D_MODEL, mx)])

    qx = _mm("xq", h1b, ex.weight("w_xq"), "nn", out_dtypes=(mx,))
    kvx = _mm("xkv", mem, ex.weight("w_xkv"), "nn", out_dtypes=(mx,))
    oxb = _xattn_fwd(qx, kvx)
    o2 = _mm("xo", oxb, ex.weight("w_xo"), "nn")
    h2, h2b = _rowwise("ln2", lambda h, o, g, b: (lambda r: (r, r))(_res_ln(h, o, g, b)), [h1, o2],
                       [sp["ln2_g"], sp["ln2_b"]], [(D_MODEL, F32), (D_MODEL, mx)])

    a_ff, fb = _mm("ff1", h2b, ex.weight("w_ff1"), "nn", bias=sp["b_ff1"],
                   epilogue=lambda r: (r, jnp.square(jnp.maximum(r, 0.0))), out_dtypes=(F32, mx))
    o3 = _mm("ff2", fb, ex.weight("w_ff2"), "nn", bias=sp["b_ff2"])

    def loss_bwd(h2, o3, tgt, g, b):
        def f(h2, o3, g, b):
            h3 = _res_ln(h2, o3, g, b)
            return 0.5 * jnp.sum(jnp.mean(jnp.square(h3 - tgt), axis=-1))

        loss, vjp = jax.vjp(f, h2, o3, g, b)
        _, dr, dg, db = vjp(jnp.ones((), F32))
        return dr, dr, dg, db, _colsum(dr), jnp.full((1, 128), loss, F32)

    dr3, dr3b, g_ln3_g, g_ln3_b, g_b_ff2, loss = _rowwise(
        "loss_ln3_bwd", loss_bwd, [h2, o3, target], [sp["ln3_g"], sp["ln3_b"]],
        [(D_MODEL, F32), (D_MODEL, mx)], [D_MODEL, D_MODEL, D_MODEL, 128])

    dab = _mm("ff2_dx", dr3b, ex.weight("w_ff2"), "nt", extras=(a_ff,),
              epilogue=lambda r, a: (r * (2.0 * jnp.maximum(a, 0.0)),), out_dtypes=(mx,))
    ex.grad("w_ff2", _mm("ff2_dw", fb, dr3b, "tn"))
    g_b_ff1, = _rowwise("ff1_db", lambda v: (_colsum(v),), [dab], [], [], [D_FF])
    ex.grad("w_ff1", _mm("ff1_dw", h2b, dab, "tn", carry=ex.carry(swap=["w_ff2"])))
    dh2 = _mm("ff1_dx", dab, ex.weight("w_ff1"), "nt", extras=(dr3,), epilogue=lambda r, d: (r + al * d,),
              carry=ex.carry(swap=["w_ff1"], ici=["w_ff2"]))

    def ln_bwd(h, o, dout, g, b):
        _, vjp = jax.vjp(_res_ln, h, o, g, b)
        _, dr, dg, db = vjp(dout)
        return dr, dr, dg, db, _colsum(dr)

    dr2, dr2b, g_ln2_g, g_ln2_b, _ = _rowwise(
        "ln2_bwd", ln_bwd, [h1, o2, dh2], [sp["ln2_g"], sp["ln2_b"]],
        [(D_MODEL, F32), (D_MODEL, mx)], [D_MODEL, D_MODEL, D_MODEL])
    ex.grad("w_xo", _mm("xo_dw", oxb, dr2b, "tn"))
    doxb = _mm("xo_dx", dr2b, ex.weight("w_xo"), "nt", out_dtypes=(mx,), carry=ex.carry(swap=["w_xo"]))
    dqxb, dkvx = _xattn_bwd(qx, kvx, doxb)
    ex.grad("w_xq", _mm("xq_dw", h1b, dqxb, "tn", carry=ex.carry(ici=["w_xo"])))
    dh1 = _mm("xq_dx", dqxb, ex.weight("w_xq"), "nt", extras=(dr2,), epilogue=lambda r, d: (r + al * d,),
              carry=ex.carry(swap=["w_xq"]))
    ex.grad("w_xkv", _mm("xkv_dw", mem, dkvx, "tn"))

    dr1, dr1b, g_ln1_g, g_ln1_b, g_b_mix = _rowwise(
        "ln1_bwd", ln_bwd, [h0, o1, dh1], [sp["ln1_g"], sp["ln1_b"]],
        [(D_MODEL, F32), (D_MODEL, mx)], [D_MODEL, D_MODEL, D_MODEL])
    ex.grad("w_mix_out", _mm("mix_dw", mixedb, dr1b, "tn", carry=ex.carry(swap=["w_xkv"], ici=["w_xq"])))
    dmixed = _mm("mix_dx", dr1b, ex.weight("w_mix_out"), "nt", carry=ex.carry(swap=["w_mix_out"]))

    def gate_bwd(gs, ga, z1, z2, batt, dm):
        _, vjp = jax.vjp(_gate, gs, ga, z1, z2, batt)
        dgs, dga, dz1, dz2, dbatt = vjp(dm)
        dz = jnp.concatenate([dz1, dz2], axis=-1)
        return dgs, dga, dz, dbatt, _colsum(dz)

    dgsb, dgab, dzb, dbattb, g_b_glu = _rowwise(
        "gate_bwd", gate_bwd, gate_rows + [dmixed], [],
        [(D_MODEL, mx), (D_MODEL, mx), (2 * D_MODEL, mx), (D_MODEL, mx)], [2 * D_MODEL])
    ex.grad("w_att_up", _mm("att_up_dw", attb, dbattb, "tn", carry=ex.carry(ici=["w_mix_out"])))
    datt = _mm("att_up_dx", dbattb, ex.weight("w_att_up"), "nt", carry=ex.carry(swap=["w_att_up"]))

    def att_delta(datt, att, hs):
        dl = jnp.dot(datt * att, hs, precision=lax.Precision.HIGHEST, preferred_element_type=F32)
        return datt, dl, _stat_rows(dl)

    dattb, delta, delta_t = _rowwise("attn_delta", att_delta, [datt, att], [_head_sum_matrix()],
                                     [(ATT_GROUPW, mx), (ATT_STATW, F32)], touts=[(8, F32)])
    dq_g, dk_g, dv_g = [], [], []
    for g, (tag, per_seq, d) in enumerate(groups):
        do_d, lt_d, dl_d = _dilate(dattb, d), _dilate(lse_tot, d), _dilate(delta, d)
        dq_g.append(_undilate(_attn_dq(tag, per_seq, q_d[g], k_d[g], v_d[g], do_d, lt_d, dl_d), d))
        dk, dv = _attn_dkv(tag, per_seq, q_d[g], k_d[g], v_d[g], do_d, _dilate_rows(lse_tot_t, d), _dilate_rows(delta_t, d))
        dk_g.append(_undilate(dk, d))
        dv_g.append(_undilate(dv, d))
    dqkv = dq_g + dk_g + dv_g

    def rope_bwd(pos, q0, q1, q2, k0, k1, k2, v0, v1, v2, invf, m1, m2):
        tabs = _rope_tables(pos, invf, m1, m2)
        return jnp.concatenate([_rope_t(t, *tabs) for t in (q0, q1, q2, k0, k1, k2)] + [v0, v1, v2], axis=-1)

    dqkvb, = _rowwise("rope_bwd", rope_bwd, [pos] + dqkv, [invf, m1, m2], [(9 * ATT_GROUPW, mx)])

    ex.grad("w_glu", _mm("glu_dw", ygb, dzb, "tn", carry=ex.carry(ici=["w_xkv", "w_att_up"])))
    dyg = _mm("glu_dx", dzb, ex.weight("w_glu"), "nt", carry=ex.carry(swap=["w_glu"]))

    def gelu_bwd(y, dyg):
        _, vjp = jax.vjp(jax.nn.gelu, y)
        return vjp(dyg)[0]

    dy, = _rowwise("gelu_bwd", gelu_bwd, [y, dyg], [], [(SSM_WIDTH, F32)])
    dy_p = _time_perm(dy)
    g_cexp = _ssm_wgrad("ssm_dc", dy_p, h_re, h_im, expand=False)
    s_re, s_im, du_p = _ssm_scan("ssm_scan_bwd", dy_p, c12, b12, a_re, a_im, sp["ssm_d"], reverse=True,
                                 carry=ex.carry(ici=["w_ff1", "w_glu"]))
    d_abr, d_abi = _ssm_da(s_re, s_im, h_re, h_im)
    g_bexp = _ssm_wgrad("ssm_db", u_p, s_re, s_im, expand=True)
    g_ssm_d, = _rowwise("ssm_dd", lambda a, b: (_colsum(a * b),), [dy_p, u_p], [], [], [SSM_WIDTH])
    g_ldt, g_are, g_aim, g_bre, g_bim = _ssm_disc_bwd(
        ldt, are, aim, bre, bim, d_abr.reshape(N_STATE, 1), d_abi.reshape(N_STATE, 1),
        _diag_of_b(g_bexp[:, :, :CH_N]), _diag_of_b(g_bexp[:, :, CH_N:]))
    g_c_re = _diag_of_c(g_cexp[:, :CH_N, :])
    g_c_im = -_diag_of_c(g_cexp[:, CH_N:, :])
    dub = _time_unperm(du_p).astype(mx)

    dprojb = jnp.concatenate([dub, dqkvb, dgsb, dgab], axis=-1)
    g_b_in, = _rowwise("in_db", lambda v: (_colsum(v),), [dprojb], [], [], [IN_COLS])
    ex.grad("w_in", _mm("in_dw", h0b, dprojb, "tn"))
    dh0 = _mm("in_dx", dprojb, ex.weight("w_in"), "nt", extras=(dr1,), epilogue=lambda r, d: (r + al * d,),
              carry=ex.carry(ici=["w_in"]))

    def ln_in_bwd(x, dout, g, b):
        _, vjp = jax.vjp(_ln, x, g, b)
        return vjp(dout)

    dx, g_ln_in_g, g_ln_in_b = _rowwise("ln_in_bwd", ln_in_bwd, [x, dh0], [sp["ln_in_g"], sp["ln_in_b"]],
                                        [(D_MODEL, F32)], [D_MODEL, D_MODEL])

    small = {"ln_in_g": g_ln_in_g, "ln_in_b": g_ln_in_b, "b_in": g_b_in, "ssm_log_dt": g_ldt, "ssm_a_re": g_are,
             "ssm_a_im": g_aim, "ssm_b_re": g_bre, "ssm_b_im": g_bim, "ssm_c_re": g_c_re, "ssm_c_im": g_c_im,
             "ssm_d": g_ssm_d, "b_glu": g_b_glu, "b_mix_out": g_b_mix, "ln1_g": g_ln1_g, "ln1_b": g_ln1_b,
             "ln2_g": g_ln2_g, "ln2_b": g_ln2_b, "b_ff1": g_b_ff1, "b_ff2": g_b_ff2, "ln3_g": g_ln3_g,
             "ln3_b": g_ln3_b}
    return loss, dx, small


def _piece_shape(k, n, axis):
    return (k // 2, n // 4) if axis == 1 else (k // 8, n)


def _aligned(v, m):
    return v if isinstance(v, int) else pl.multiple_of(v, m)


def _full_piece(ref, k, n, axis, chip, half):
    pr, pc = _piece_shape(k, n, axis)
    if axis == 1:
        return ref.at[pl.ds(_aligned(half * pr, 8), pr), pl.ds(_aligned(chip * pc, 128), pc)]
    return ref.at[pl.ds(_aligned(chip * (2 * pr) + half * pr, 8), pr), :]


def _full_shard(ref, k, n, axis, chip):
    if axis == 1:
        return ref.at[:, pl.ds(_aligned(chip * (n // 4), 128), n // 4)]
    return ref.at[pl.ds(_aligned(chip * (k // 4), 8), k // 4), :]


def _shard_piece(ref, k, n, axis, half):
    pr, _ = _piece_shape(k, n, axis)
    return ref.at[pl.ds(_aligned(half * pr, 8), pr), :]


def _mesh_pos():
    x, y, c = lax.axis_index("x"), lax.axis_index("y"), lax.axis_index("c")
    other_chips = [(1 - x, y), (x, 1 - y), (1 - x, 1 - y)]
    return x, y, c, other_chips


def _remote(src, dst, send_sem, recv_sem, dev):
    return pltpu.make_async_remote_copy(src_ref=src, dst_ref=dst, send_sem=send_sem, recv_sem=recv_sem,
                                        device_id=dev, device_id_type=MESH)


def _placed(name, fn, n_steps, where, ins, out_sds, out_block, out_index):
    def body(w_ref, *refs):
        o_ref = refs[-1]
        o_ref[...] = fn(*[r[...] for r in refs[:-1]]).astype(o_ref.dtype)

    grid_spec = pltpu.PrefetchScalarGridSpec(
        num_scalar_prefetch=1, grid=(n_steps,), in_specs=[pl.BlockSpec(bs, idx) for _, bs, idx in ins],
        out_specs=pl.BlockSpec(out_block, out_index))
    return pl.pallas_call(body, name=name, grid_spec=grid_spec, out_shape=out_sds,
                          compiler_params=_cparams(1))(where, *[a for a, _, _ in ins])


def _gather_copies(widx):
    geo = [BIG[i][1:] for i in widx]

    def ici(full, wi, j, chip, send_sems, recv_sems, c, dev):
        k, n, ax = geo[wi]
        piece = _full_piece(full[wi], k, n, ax, chip, c)
        return _remote(piece, piece, send_sems.at[wi * 6 + j], recv_sems.at[wi * 6 + j], dev)

    def d2d(full, wi, j, chip, half, send_sems, recv_sems, sib):
        k, n, ax = geo[wi]
        piece = _full_piece(full[wi], k, n, ax, chip, half)
        return _remote(piece, piece, send_sems.at[wi * 6 + 3 + j], recv_sems.at[wi * 6 + 3 + j], sib)

    def start(_, full, send_sems, recv_sems):
        x, y, c, chips = _mesh_pos()
        for wi in range(len(geo)):
            for j, (qx, qy) in enumerate(chips):
                ici(full, wi, j, 2 * x + y, send_sems, recv_sems, c, (qx, qy, c)).start()

    def finish(_, full, send_sems, recv_sems):
        x, y, c, chips = _mesh_pos()
        sib = (x, y, 1 - c)
        for wi in range(len(geo)):
            for j, (qx, qy) in enumerate(chips):
                ici(full, wi, j, 2 * qx + qy, send_sems, recv_sems, c, (qx, qy, c)).wait_recv()
                d2d(full, wi, j, 2 * qx + qy, c, send_sems, recv_sems, sib).start()
        for wi in range(len(geo)):
            for j, (qx, qy) in enumerate(chips):
                d2d(full, wi, j, 2 * qx + qy, 1 - c, send_sems, recv_sems, sib).wait_recv()
        for wi in range(len(geo)):
            for j, (qx, qy) in enumerate(chips):
                ici(full, wi, j, 2 * x + y, send_sems, recv_sems, c, (qx, qy, c)).wait_send()
                d2d(full, wi, j, 2 * qx + qy, c, send_sems, recv_sems, sib).wait_send()

    return start, finish, 6 * len(geo)


def _gather_weights(tag, fulls, widx):
    nw = len(widx)
    start, finish, n_sems = _gather_copies(widx)

    def body(*refs):
        full = refs[nw:2 * nw]
        start(None, full, *refs[2 * nw:])
        finish(None, full, *refs[2 * nw:])

    return pl.pallas_call(
        body, name="gather_weights_" + tag, in_specs=[HBM_SPEC] * nw, out_specs=[HBM_SPEC] * nw,
        out_shape=[jax.ShapeDtypeStruct(f.shape, f.dtype) for f in fulls],
        input_output_aliases={i: i for i in range(nw)},
        scratch_shapes=[pltpu.SemaphoreType.DMA((n_sems,)), pltpu.SemaphoreType.DMA((n_sems,))])(*fulls)


def _swap_copies(widx):
    geo = [BIG[i][1:] for i in widx]

    def copies(g, got, send_sems, recv_sems, base):
        x, y, c, _ = _mesh_pos()
        return [_remote(_full_piece(g[wi], k, n, ax, q, 1 - c), got[wi].at[q], send_sems.at[base + wi * 4 + q],
                        recv_sems.at[base + wi * 4 + q], (x, y, 1 - c))
                for wi, (k, n, ax) in enumerate(geo) for q in range(4)]

    def start(g, got, send_sems, recv_sems, base=0):
        for cp in copies(g, got, send_sems, recv_sems, base):
            cp.start()

    def finish(g, got, send_sems, recv_sems, base=0):
        for cp in copies(g, got, send_sems, recv_sems, base):
            cp.wait()

    return start, finish, 4 * len(geo)


def _swap_shapes(widx):
    return [jax.ShapeDtypeStruct((4,) + _piece_shape(*BIG[i][1:]), F32) for i in widx]


def _reduce_swap_halves(tag, grads, widx):
    nw = len(widx)
    start, finish, n_sems = _swap_copies(widx)

    def body(*refs):
        start(refs[:nw], refs[nw:2 * nw], *refs[2 * nw:])
        finish(refs[:nw], refs[nw:2 * nw], *refs[2 * nw:])

    return pl.pallas_call(
        body, name="reduce_swap_halves_" + tag, in_specs=[HBM_SPEC] * nw, out_specs=[HBM_SPEC] * nw,
        out_shape=_swap_shapes(widx),
        scratch_shapes=[pltpu.SemaphoreType.DMA((n_sems,)), pltpu.SemaphoreType.DMA((n_sems,))])(*grads)


def _owner_copies(nw):
    def copies(p, out, send_sems, recv_sems, base):
        x, y, c, chips = _mesh_pos()
        return [_remote(p[wi].at[2 * qx + qy], out[wi].at[j], send_sems.at[base + wi * 3 + j],
                        recv_sems.at[base + wi * 3 + j], (qx, qy, c))
                for wi in range(nw) for j, (qx, qy) in enumerate(chips)]

    def start(p, out, send_sems, recv_sems, base=0):
        for cp in copies(p, out, send_sems, recv_sems, base):
            cp.start()

    def finish(p, out, send_sems, recv_sems, base=0):
        for cp in copies(p, out, send_sems, recv_sems, base):
            cp.wait()

    return start, finish, 3 * nw


def _join_carries(a, b):
    if a is None or b is None:
        return a if b is None else b
    n_i, n_o = len(a.ins), len(a.outs)
    outs = list(a.outs) + [o + n_i if isinstance(o, int) else o for o in b.outs]

    def start(c_in, c_out, send_sems, recv_sems):
        a.start(c_in[:n_i], c_out[:n_o], send_sems, recv_sems)
        b.start(c_in[n_i:], c_out[n_o:], send_sems, recv_sems, base=a.n_sems)

    def finish(c_in, c_out, send_sems, recv_sems):
        a.finish(c_in[:n_i], c_out[:n_o], send_sems, recv_sems)
        b.finish(c_in[n_i:], c_out[n_o:], send_sems, recv_sems, base=a.n_sems)

    def done(res):
        a.done(res[:n_o])
        b.done(res[n_o:])

    return _Carry(a.ins + b.ins, outs, a.n_sems + b.n_sems, start, finish, done)


def _share_with_sibling(shards):
    nw = len(BIG)

    def body(*refs):
        out = refs[nw:2 * nw]
        send_sems, recv_sems = refs[2 * nw:]
        x, y, c, _ = _mesh_pos()
        sib = (x, y, 1 - c)
        cps = []
        for wi, (_, k, n, ax) in enumerate(BIG):
            mine = _shard_piece(out[wi], k, n, ax, c)
            cp = _remote(mine, mine, send_sems.at[wi], recv_sems.at[wi], sib)
            cp.start()
            cps.append(cp)
        for wi, (_, k, n, ax) in enumerate(BIG):
            piece = _shard_piece(out[wi], k, n, ax, 1 - c)
            _remote(piece, piece, send_sems.at[wi], recv_sems.at[wi], sib).wait_recv()
        for cp in cps:
            cp.wait_send()

    return pl.pallas_call(
        body, name="share_with_sibling", in_specs=[HBM_SPEC] * nw, out_specs=[HBM_SPEC] * nw,
        out_shape=[jax.ShapeDtypeStruct(sh.shape, sh.dtype) for sh in shards],
        input_output_aliases={i: i for i in range(nw)},
        scratch_shapes=[pltpu.SemaphoreType.DMA((nw,)), pltpu.SemaphoreType.DMA((nw,))])(*shards)


def _allreduce_small(v):
    r = v.shape[0]
    rh = r // 2
    assert rh % 8 == 0

    def body(v_ref, o_ref, sib_buf, chip_buf, send_sems, recv_sems):
        x, y, c, chips = _mesh_pos()
        me = 2 * x + y
        sib = (x, y, 1 - c)
        mine = pl.ds(pl.multiple_of(c * rh, 8), rh)
        other = pl.ds(pl.multiple_of((1 - c) * rh, 8), rh)
        swap = _remote(v_ref.at[other], sib_buf, send_sems.at[0], recv_sems.at[0], sib)
        swap.start()
        swap.wait()
        chip_buf[me] = v_ref[mine, :] + sib_buf[...]
        cps = []
        for j, (qx, qy) in enumerate(chips):
            cp = _remote(chip_buf.at[me], chip_buf.at[me], send_sems.at[1 + j], recv_sems.at[1 + j], (qx, qy, c))
            cp.start()
            cps.append(cp)
        for j, (qx, qy) in enumerate(chips):
            slot = chip_buf.at[2 * qx + qy]
            _remote(slot, slot, send_sems.at[1 + j], recv_sems.at[1 + j], (qx, qy, c)).wait_recv()
        for cp in cps:
            cp.wait_send()
        o_ref[mine, :] = ((chip_buf[0] + chip_buf[1]) + chip_buf[2]) + chip_buf[3]
        back = _remote(o_ref.at[mine], o_ref.at[mine], send_sems.at[4], recv_sems.at[4], sib)
        back.start()
        _remote(o_ref.at[other], o_ref.at[other], send_sems.at[4], recv_sems.at[4], sib).wait_recv()
        back.wait_send()

    return pl.pallas_call(
        body, name="allreduce_small", in_specs=[VMEM_SPEC], out_specs=VMEM_SPEC,
        out_shape=jax.ShapeDtypeStruct((r, 128), F32),
        scratch_shapes=[pltpu.VMEM((rh, 128), F32), pltpu.VMEM((4, rh, 128), F32),
                        pltpu.SemaphoreType.DMA((5,)), pltpu.SemaphoreType.DMA((5,))],
        compiler_params=pltpu.CompilerParams(vmem_limit_bytes=VMEM_LIMIT))(v)


def _as2d(a):
    a = a.reshape((-1, a.shape[-1])) if a.ndim > 1 else a.reshape(1, -1)
    return a


def _adamw_small(quads):
    n = len(quads)

    def body(*refs):
        for i in range(n):
            w, g, m, v = (r[...] for r in refs[4 * i:4 * i + 4])
            for ref, val in zip(refs[4 * n + 3 * i:4 * n + 3 * i + 3], _adamw(w, g, m, v)):
                ref[...] = val

    return pl.pallas_call(
        body, name="adamw_small", in_specs=[VMEM_SPEC] * (4 * n), out_specs=[VMEM_SPEC] * (3 * n),
        out_shape=[jax.ShapeDtypeStruct(q[0].shape, F32) for q in quads for _ in range(3)],
        compiler_params=pltpu.CompilerParams(vmem_limit_bytes=VMEM_LIMIT))(*[a for q in quads for a in q])


def _where():
    return jnp.stack([2 * lax.axis_index("x") + lax.axis_index("y"), lax.axis_index("c")]).astype(jnp.int32)


_BIG_INDEX = {name: i for i, (name, _, _, _) in enumerate(BIG)}


class _LocalWeights:
    def __init__(self, weights):
        self.weights, self.grads = weights, {}

    def gather_now(self, names):
        pass

    def gather_carry(self, names):
        return None

    def weight(self, name):
        return self.weights[name]

    def grad(self, name, g):
        self.grads[name] = g

    def carry(self, swap=(), ici=()):
        return None


class _Exchange:
    def __init__(self, inputs, where):
        self.inputs, self.where = inputs, where
        self.full, self.ready = {}, set()
        self.raw, self.got, self.parts, self.landed, self.geom = {}, {}, {}, {}, {}
        for name, k, n, ax in BIG:
            w2 = inputs[name][0]
            rs, cs = w2.shape
            tm = _tile(rs, 512)
            steps = rs // tm
            if ax == 1:
                blk, idx = (tm, cs), lambda i, w: (i, w[0])
            else:
                blk, idx = (tm, n), functools.partial(lambda i, w, steps: (w[0] * steps + i, 0), steps=steps)
            self.full[name] = _placed("cast_" + name, lambda w: w, steps, where, [(w2, (tm, cs), lambda i, w: (i, 0))],
                                      jax.ShapeDtypeStruct((k, n), MXU_DTYPE), blk, idx)

    def _gathered(self, names, outs):
        for name, o in zip(names, outs):
            self.full[name] = o
            self.ready.add(name)

    def gather_now(self, names):
        self._gathered(names, _gather_weights(names[0], [self.full[n] for n in names], [_BIG_INDEX[n] for n in names]))

    def gather_carry(self, names):
        start, finish, n_sems = _gather_copies([_BIG_INDEX[n] for n in names])
        return _Carry([self.full[n] for n in names], list(range(len(names))), n_sems, start, finish,
                      functools.partial(self._gathered, names))

    def weight(self, name):
        assert name in self.ready, name
        return self.full[name]

    def grad(self, name, g):
        self.raw[name] = g

    def _swapped(self, names, outs):
        for name, o in zip(names, outs):
            self.got[name] = o

    def _pair_sum(self, name):
        i = _BIG_INDEX[name]
        _, k, n, ax = BIG[i]
        g = self.raw[name]
        if name not in self.got:
            self._swapped([name], _reduce_swap_halves(name, [g], [i]))
        got = self.got[name]
        pr, pc = _piece_shape(k, n, ax)
        tm = _tile(pr, 512)
        spp = pr // tm
        self.geom[name] = (pr, pc, tm, spp)
        if ax == 1:
            g_idx = functools.partial(lambda i, w, spp: (w[1] * spp + i % spp, i // spp), spp=spp)
        else:
            g_idx = functools.partial(lambda i, w, spp: ((i // spp) * 2 * spp + w[1] * spp + i % spp, 0), spp=spp)
        self.parts[name] = _placed(
            "pair_sum_" + name, lambda a, b: a + b, 4 * spp, self.where,
            [(g, (tm, pc), g_idx), (got.reshape(4 * pr, pc), (tm, pc), lambda i, w: (i, 0))],
            jax.ShapeDtypeStruct((4 * pr, pc), BF16), (tm, pc), lambda i, w: (i, 0)).reshape(4, pr, pc)

    def _landed(self, names, outs):
        for name, o in zip(names, outs):
            self.landed[name] = o

    def carry(self, swap=(), ici=()):
        first = second = None
        if swap:
            widx = [_BIG_INDEX[n] for n in swap]
            start, finish, n_sems = _swap_copies(widx)
            first = _Carry([self.raw[n] for n in swap], _swap_shapes(widx), n_sems, start, finish,
                           functools.partial(self._swapped, list(swap)))
        if ici:
            for n in ici:
                self._pair_sum(n)
            start, finish, n_sems = _owner_copies(len(ici))
            parts = [self.parts[n] for n in ici]
            outs = [jax.ShapeDtypeStruct((3,) + p.shape[1:], p.dtype) for p in parts]
            second = _Carry(parts, outs, n_sems, start, finish, functools.partial(self._landed, list(ici)))
        return _join_carries(first, second)

    def finish(self):
        halves = []
        for name, _, _, _ in BIG:
            pr, pc, tm, spp = self.geom[name]
            ins = [(self.parts[name], (None, tm, pc), lambda i, w: (w[0], i, 0))]
            ins += [(self.landed[name], (None, tm, pc), functools.partial(lambda i, w, j: (j, i, 0), j=j))
                    for j in range(3)]
            halves.append(_placed("chip_sum_" + name,
                                  lambda a, b, c, d: ((a.astype(F32) + b.astype(F32)) + c.astype(F32)) + d.astype(F32),
                                  spp, self.where, ins, jax.ShapeDtypeStruct(self.inputs[name].shape[1:], F32), (tm, pc),
                                  functools.partial(lambda i, w, spp: (w[1] * spp + i, 0), spp=spp)))
        return dict(zip([b[0] for b in BIG], _share_with_sibling(halves)))


def _step(inputs):
    x, mem, positions, target = inputs["x"][0], inputs["mem"][0], inputs["positions"], inputs["loss_target"][0]
    pos = positions.reshape(-1, 1)
    ex = _Exchange(inputs, _where())
    sp = {name: _as2d(inputs[name]) for name in SMALL}
    memb, = _rowwise("cast_mem", lambda m: (m,), [mem], [], [(D_MODEL, MXU_DTYPE)])

    loss, dx, gsmall = _local_step(x, memb, pos, target, sp, ex)
    gshard = ex.finish()

    out = {}
    for name, _, _, _ in BIG:
        w2, m2, v2 = inputs[name][0], inputs["m_" + name][0], inputs["v_" + name][0]
        n = w2.shape[1]
        d, nm, nv = _rowwise("adamw_" + name, _adamw, [w2, gshard[name], m2, v2], [], [(n, F32)] * 3, tm=256)
        lead = inputs[name].shape
        out[name] = (gshard[name].reshape(lead), d.reshape(lead), nm.reshape(lead), nv.reshape(lead))

    def tiles(a):
        flat = a.reshape(-1)
        n = -(-flat.shape[0] // 1024) * 1024
        return jnp.pad(flat, (0, n - flat.shape[0])).reshape(n // 128, 128)

    pieces = [tiles(loss[:, :1])] + [tiles(gsmall[name]) for name in SMALL]
    if sum(p.shape[0] for p in pieces) % 16:
        pieces.append(jnp.zeros((8, 128), F32))
    red = _allreduce_small(jnp.concatenate(pieces, axis=0))
    loss_total = red[0, 0]
    grads, off = {}, pieces[0].shape[0]
    for name, p in zip(SMALL, pieces[1:]):
        shp = _as2d(inputs[name]).shape
        grads[name] = red[off:off + p.shape[0]].reshape(-1)[:shp[0] * shp[1]].reshape(shp)
        off += p.shape[0]
    upd = _adamw_small([(_as2d(inputs[n]), grads[n], _as2d(inputs["m_" + n]), _as2d(inputs["v_" + n])) for n in SMALL])
    for i, name in enumerate(SMALL):
        shp = inputs[name].shape
        out[name] = (grads[name].reshape(shp),) + tuple(t.reshape(shp) for t in upd[3 * i:3 * i + 3])
    return loss_total, dx.reshape(inputs["x"].shape), out


_ARG_NAMES = (("x", "mem", "positions") + WEIGHT_ORDER + ("loss_target",) + tuple("m_" + n for n in WEIGHT_ORDER)
              + tuple("v_" + n for n in WEIGHT_ORDER))


def kernel(x, mem, positions, ln_in_g, ln_in_b, w_in, b_in, ssm_log_dt, ssm_a_re, ssm_a_im, ssm_b_re, ssm_b_im, ssm_c_re, ssm_c_im, ssm_d, w_glu, b_glu, w_att_up, w_mix_out, b_mix_out, ln1_g, ln1_b, w_xq, w_xkv, w_xo, ln2_g, ln2_b, w_ff1, b_ff1, w_ff2, b_ff2, ln3_g, ln3_b, loss_target, m_ln_in_g, m_ln_in_b, m_w_in, m_b_in, m_ssm_log_dt, m_ssm_a_re, m_ssm_a_im, m_ssm_b_re, m_ssm_b_im, m_ssm_c_re, m_ssm_c_im, m_ssm_d, m_w_glu, m_b_glu, m_w_att_up, m_w_mix_out, m_b_mix_out, m_ln1_g, m_ln1_b, m_w_xq, m_w_xkv, m_w_xo, m_ln2_g, m_ln2_b, m_w_ff1, m_b_ff1, m_w_ff2, m_b_ff2, m_ln3_g, m_ln3_b, v_ln_in_g, v_ln_in_b, v_w_in, v_b_in, v_ssm_log_dt, v_ssm_a_re, v_ssm_a_im, v_ssm_b_re, v_ssm_b_im, v_ssm_c_re, v_ssm_c_im, v_ssm_d, v_w_glu, v_b_glu, v_w_att_up, v_w_mix_out, v_b_mix_out, v_ln1_g, v_ln1_b, v_w_xq, v_w_xkv, v_w_xo, v_ln2_g, v_ln2_b, v_w_ff1, v_b_ff1, v_w_ff2, v_b_ff2, v_ln3_g, v_ln3_b):
    args = (x, mem, positions, ln_in_g, ln_in_b, w_in, b_in, ssm_log_dt, ssm_a_re, ssm_a_im, ssm_b_re, ssm_b_im, ssm_c_re, ssm_c_im, ssm_d, w_glu, b_glu, w_att_up, w_mix_out, b_mix_out, ln1_g, ln1_b, w_xq, w_xkv, w_xo, ln2_g, ln2_b, w_ff1, b_ff1, w_ff2, b_ff2, ln3_g, ln3_b, loss_target, m_ln_in_g, m_ln_in_b, m_w_in, m_b_in, m_ssm_log_dt, m_ssm_a_re, m_ssm_a_im, m_ssm_b_re, m_ssm_b_im, m_ssm_c_re, m_ssm_c_im, m_ssm_d, m_w_glu, m_b_glu, m_w_att_up, m_w_mix_out, m_b_mix_out, m_ln1_g, m_ln1_b, m_w_xq, m_w_xkv, m_w_xo, m_ln2_g, m_ln2_b, m_w_ff1, m_b_ff1, m_w_ff2, m_b_ff2, m_ln3_g, m_ln3_b, v_ln_in_g, v_ln_in_b, v_w_in, v_b_in, v_ssm_log_dt, v_ssm_a_re, v_ssm_a_im, v_ssm_b_re, v_ssm_b_im, v_ssm_c_re, v_ssm_c_im, v_ssm_d, v_w_glu, v_b_glu, v_w_att_up, v_w_mix_out, v_b_mix_out, v_ln1_g, v_ln1_b, v_w_xq, v_w_xkv, v_w_xo, v_ln2_g, v_ln2_b, v_w_ff1, v_b_ff1, v_w_ff2, v_b_ff2, v_ln3_g, v_ln3_b)
    assert len(args) == len(_ARG_NAMES)
    inputs = dict(zip(_ARG_NAMES, args))
    loss, dx, out = _step(inputs)
    res = [loss, dx]
    for k in range(4):
        res += [out[name][k] for name in WEIGHT_ORDER]
    return tuple(res)
```

```python
import functools
import math

import numpy as np
import jax
import jax.numpy as jnp
from jax import lax
from jax.experimental import pallas as pl
from jax.experimental.pallas import tpu as pltpu

F32 = jnp.float32
BF16 = jnp.bfloat16
MXU_DTYPE = jnp.bfloat16

D_MODEL = 1024
SSM_GROUP = 16
SSM_WIDTH = 768
SSM_GROUPS = 48
SSM_STATE = 64
N_STATE = SSM_GROUPS * SSM_STATE
SSM_CHUNKS = 6
CH_W = 128
CH_N = 512
ATT_HEAD_DIM = 64
ATT_HPG = 4
ATT_GROUPW = ATT_HPG * ATT_HEAD_DIM
DILATIONS = (1, 4, 16)
ATT_BLK = 128
ATT_SCALE = ATT_HEAD_DIM ** -0.5
ROT_DIM = 16
ROPE_THETA = 500000.0
XATT_HEADS = 4
XATT_HEAD_DIM = 256
XATT_SCALE = XATT_HEAD_DIM ** -0.5
D_FF = 4096
IN_COLS = 5120
DEEPNORM_ALPHA = 2.0 ** 0.25
LN_EPS = 1e-5
NEG_INF = -1e30
ADAM_LR = 0.001
ADAM_B1 = 0.9
ADAM_B2 = 0.999
ADAM_EPS = 1e-08
ADAM_WD = 0.01
ADAM_STEP = 10

N_SEG = 32
VMEM_LIMIT = 48 * 1024 * 1024
MESH = pl.DeviceIdType.MESH
HBM_SPEC = pl.BlockSpec(memory_space=pltpu.HBM)
VMEM_SPEC = pl.BlockSpec(memory_space=pltpu.VMEM)

BIG = (("w_in", 1024, 5120, 1), ("w_glu", 768, 2048, 1), ("w_att_up", 256, 1024, 1),
       ("w_mix_out", 1024, 1024, 0), ("w_xq", 1024, 1024, 0), ("w_xkv", 1024, 2048, 1),
       ("w_xo", 1024, 1024, 0), ("w_ff1", 1024, 4096, 1), ("w_ff2", 4096, 1024, 0))
SMALL = ("ln_in_g", "ln_in_b", "b_in", "ssm_log_dt", "ssm_a_re", "ssm_a_im", "ssm_b_re", "ssm_b_im",
         "ssm_c_re", "ssm_c_im", "ssm_d", "b_glu", "b_mix_out", "ln1_g", "ln1_b", "ln2_g", "ln2_b",
         "b_ff1", "b_ff2", "ln3_g", "ln3_b")
WEIGHT_ORDER = ("ln_in_g", "ln_in_b", "w_in", "b_in", "ssm_log_dt", "ssm_a_re", "ssm_a_im", "ssm_b_re",
                "ssm_b_im", "ssm_c_re", "ssm_c_im", "ssm_d", "w_glu", "b_glu", "w_att_up", "w_mix_out",
                "b_mix_out", "ln1_g", "ln1_b", "w_xq", "w_xkv", "w_xo", "ln2_g", "ln2_b", "w_ff1", "b_ff1",
                "w_ff2", "b_ff2", "ln3_g", "ln3_b")


def _cparams(n_axes):
    return pltpu.CompilerParams(dimension_semantics=("arbitrary",) * n_axes, vmem_limit_bytes=VMEM_LIMIT)


class _Carry:
    def __init__(self, ins, outs, n_sems, start, finish, done):
        self.ins, self.outs, self.n_sems, self.start, self.finish, self.done = ins, outs, n_sems, start, finish, done


def _call(name, body, grid, in_specs, out_specs, out_shape, args, scratch_shapes=(), carry=None):
    in_specs, out_specs, out_shape = list(in_specs), list(out_specs), list(out_shape)
    params = _cparams(len(grid))
    if carry is None:
        return pl.pallas_call(body, name=name, grid=grid, in_specs=in_specs, out_specs=out_specs, out_shape=out_shape,
                              scratch_shapes=list(scratch_shapes), compiler_params=params)(*args)
    n_in, n_out, n_ci, n_co = len(in_specs), len(out_specs), len(carry.ins), len(carry.outs)
    n_scr = len(scratch_shapes)

    def wrapped(*refs):
        ins, c_in = refs[:n_in], refs[n_in:n_in + n_ci]
        outs, c_out = refs[n_in + n_ci:n_in + n_ci + n_out], refs[n_in + n_ci + n_out:n_in + n_ci + n_out + n_co]
        scratch = refs[n_in + n_ci + n_out + n_co:n_in + n_ci + n_out + n_co + n_scr]
        send_sems, recv_sems = refs[-2:]
        ids = [pl.program_id(a) for a in range(len(grid))]
        first = functools.reduce(jnp.logical_and, [i == 0 for i in ids])
        last = functools.reduce(jnp.logical_and, [i == g - 1 for i, g in zip(ids, grid)])

        @pl.when(first)
        def _():
            carry.start(c_in, c_out, send_sems, recv_sems)

        body(*ins, *outs, *scratch)

        @pl.when(last)
        def _():
            carry.finish(c_in, c_out, send_sems, recv_sems)

    c_shapes = [jax.ShapeDtypeStruct(carry.ins[o].shape, carry.ins[o].dtype) if isinstance(o, int) else o
                for o in carry.outs]
    aliases = {n_in + o: n_out + i for i, o in enumerate(carry.outs) if isinstance(o, int)}
    res = pl.pallas_call(
        wrapped, name=name, grid=grid, in_specs=in_specs + [HBM_SPEC] * n_ci, out_specs=out_specs + [HBM_SPEC] * n_co,
        out_shape=out_shape + c_shapes, input_output_aliases=aliases,
        scratch_shapes=list(scratch_shapes) + [pltpu.SemaphoreType.DMA((carry.n_sems,))] * 2,
        compiler_params=params)(*args, *carry.ins)
    carry.done(res[n_out:])
    return res[:n_out]


def _rowwise(name, fn, rows, consts, outs, reds=(), tm=256, touts=(), carry=None):
    n_rows = (rows[0][0] if isinstance(rows[0], tuple) else rows[0]).shape[-2]
    tm = min(tm, n_rows)
    assert n_rows % tm == 0, (name, n_rows, tm)
    specs, args = [], []
    for r in rows:
        if isinstance(r, tuple) and len(r) == 3:
            arr, width, cb = r
            specs.append(pl.BlockSpec((tm, width), functools.partial(lambda i, cb: (i, cb), cb=cb)))
        elif isinstance(r, tuple):
            arr, slot = r
            specs.append(pl.BlockSpec((None, tm, arr.shape[2]), functools.partial(lambda i, s: (s, i, 0), s=slot)))
        else:
            arr = r
            specs.append(pl.BlockSpec((tm, arr.shape[1]), lambda i: (i, 0)))
        args.append(arr)
        assert arr.shape[-2] == n_rows, (name, arr.shape, n_rows)
    for cst in consts:
        specs.append(pl.BlockSpec(cst.shape, lambda i: (0, 0)))
        args.append(cst)
    n_r, n_c, n_o, n_d = len(rows), len(consts), len(outs) + len(touts), len(reds)
    out_shape = [jax.ShapeDtypeStruct((n_rows, c), dt) for c, dt in outs]
    out_specs = [pl.BlockSpec((tm, c), lambda i: (i, 0)) for c, _ in outs]
    out_shape += [jax.ShapeDtypeStruct((r, n_rows), dt) for r, dt in touts]
    out_specs += [pl.BlockSpec((r, tm), lambda i: (0, i)) for r, _ in touts]
    out_shape += [jax.ShapeDtypeStruct((1, c), F32) for c in reds]
    out_specs += [pl.BlockSpec((1, c), lambda i: (0, 0)) for c in reds]

    def body(*refs):
        ins = [r[...] for r in refs[:n_r + n_c]]
        o_refs = refs[n_r + n_c:n_r + n_c + n_o]
        d_refs = refs[n_r + n_c + n_o:]
        res = fn(*ins)
        res = res if isinstance(res, (tuple, list)) else (res,)
        assert len(res) == n_o + n_d, (name, len(res))
        for ref, val in zip(o_refs, res[:n_o]):
            ref[...] = val.astype(ref.dtype)
        first = pl.program_id(0) == 0
        for ref, val in zip(d_refs, res[n_o:]):
            @pl.when(first)
            def _(ref=ref, val=val):
                ref[...] = val

            @pl.when(jnp.logical_not(first))
            def _(ref=ref, val=val):
                ref[...] += val

    return _call(name, body, (n_rows // tm,), specs, out_specs, out_shape, args, carry=carry)


def _colsum(v):
    return jnp.sum(v.astype(F32), axis=0, keepdims=True)


_DIMS = {"nn": (((1,), (0,)), ((), ())), "nt": (((1,), (1,)), ((), ())), "tn": (((0,), (0,)), ((), ()))}


def _tile(dim, want):
    if dim <= want:
        return dim
    return max(t for t in range(128, want + 1, 128) if dim % t == 0)


def _dot(a, b, mode):
    return lax.dot_general(a.astype(MXU_DTYPE), b.astype(MXU_DTYPE), _DIMS[mode], preferred_element_type=F32)


def _mm(name, a, b, mode, *, bias=None, extras=(), epilogue=None, out_dtypes=(F32,), tm=1024, tn=1024, tk=1024,
        carry=None):
    if mode == "nn":
        (m, k), (_, n) = a.shape, b.shape
    elif mode == "nt":
        (m, k), (n, _) = a.shape, b.shape
    else:
        (k, m), (_, n) = a.shape, b.shape
    tn = _tile(n, tn)
    if mode != "tn":
        tk = k if k <= 1024 else tk
    tk = _tile(k, tk)
    nk = k // tk

    def vmem_bytes(rows):
        blocks = rows * tk * a.dtype.itemsize + tk * tn * b.dtype.itemsize
        blocks += sum(rows * tn * e.dtype.itemsize for e in extras)
        blocks += sum(rows * tn * jnp.dtype(dt).itemsize for dt in out_dtypes)
        return 2 * blocks + (rows * tn * 4 if nk > 1 else 0)

    tm = _tile(m, tm if mode == "tn" else 2 * tm)
    while vmem_bytes(tm) > 3 * VMEM_LIMIT // 4 and tm % 256 == 0:
        tm //= 2
    assert m % tm == 0 and n % tn == 0 and k % tk == 0, (name, m, n, k)
    a_spec = {"nn": pl.BlockSpec((tm, tk), lambda i, j, kk: (i, kk)),
              "nt": pl.BlockSpec((tm, tk), lambda i, j, kk: (i, kk)),
              "tn": pl.BlockSpec((tk, tm), lambda i, j, kk: (kk, i))}[mode]
    b_spec = {"nn": pl.BlockSpec((tk, tn), lambda i, j, kk: (kk, j)),
              "nt": pl.BlockSpec((tn, tk), lambda i, j, kk: (j, kk)),
              "tn": pl.BlockSpec((tk, tn), lambda i, j, kk: (kk, j))}[mode]
    specs, args = [a_spec, b_spec], [a, b]
    if bias is not None:
        specs.append(pl.BlockSpec((1, tn), lambda i, j, kk: (0, j)))
        args.append(bias)
    for e in extras:
        specs.append(pl.BlockSpec((tm, tn), lambda i, j, kk: (i, j)))
        args.append(e)
    n_e, n_o = len(extras), len(out_dtypes)
    has_bias = bias is not None

    def body(*refs):
        a_ref, b_ref = refs[0], refs[1]
        pos = 2
        bias_ref = refs[pos] if has_bias else None
        pos += int(has_bias)
        e_refs = refs[pos:pos + n_e]
        o_refs = refs[pos + n_e:pos + n_e + n_o]
        acc_ref = refs[pos + n_e + n_o] if nk > 1 else None
        part = _dot(a_ref[...], b_ref[...], mode)

        def finish(r):
            if has_bias:
                r = r + bias_ref[...]
            res = epilogue(r, *[e[...] for e in e_refs]) if epilogue is not None else (r,)
            for ref, val in zip(o_refs, res):
                ref[...] = val.astype(ref.dtype)

        if nk == 1:
            finish(part)
        else:
            kk = pl.program_id(2)

            @pl.when(kk == 0)
            def _():
                acc_ref[...] = part

            @pl.when(kk > 0)
            def _():
                acc_ref[...] += part

            @pl.when(kk == nk - 1)
            def _():
                finish(acc_ref[...])

    res = _call(name, body, (m // tm, n // tn, nk), specs,
                [pl.BlockSpec((tm, tn), lambda i, j, kk: (i, j)) for _ in out_dtypes],
                [jax.ShapeDtypeStruct((m, n), dt) for dt in out_dtypes], args,
                scratch_shapes=[pltpu.VMEM((tm, tn), F32)] if nk > 1 else [], carry=carry)
    return res[0] if n_o == 1 else res


def _ssm_wgrads(u, dy, g_re, g_im, h_re, h_im, tk=512):
    s = u.shape[0]
    tk = min(tk, s)
    nk = s // tk
    assert tk % N_SEG == 0

    def body(u_ref, dy_ref, gre_ref, gim_ref, hre_ref, him_ref, lre_ref, lim_ref, db_ref, dc_ref, dar_ref, dai_ref,
             pre_ref, pim_ref):
        kk = pl.program_id(1)
        u_blk, dy_blk = u_ref[...], dy_ref[...]
        g_r, g_i, h_r, h_i = gre_ref[...], gim_ref[...], hre_ref[...], him_ref[...]
        d_b = jnp.concatenate([_dot(u_blk, g_r, "tn"), _dot(u_blk, g_i, "tn")], axis=1)
        d_c = jnp.concatenate([_dot(h_r, dy_blk, "tn"), _dot(h_i, dy_blk, "tn")], axis=0)

        @pl.when(kk == 0)
        def _():
            first_row = lax.broadcasted_iota(jnp.int32, (N_SEG, CH_N), 0) == 0
            pre_ref[...] = jnp.where(first_row, 0.0, pltpu.roll(lre_ref[...], 1, 0))
            pim_ref[...] = jnp.where(first_row, 0.0, pltpu.roll(lim_ref[...], 1, 0))

        p_r = jnp.concatenate([pre_ref[...], h_r[:tk - N_SEG]], axis=0)
        p_i = jnp.concatenate([pim_ref[...], h_i[:tk - N_SEG]], axis=0)
        pre_ref[...] = h_r[tk - N_SEG:]
        pim_ref[...] = h_i[tk - N_SEG:]
        d_ar = jnp.sum(g_r * p_r + g_i * p_i, axis=0, keepdims=True)
        d_ai = jnp.sum(g_i * p_r - g_r * p_i, axis=0, keepdims=True)

        @pl.when(kk == 0)
        def _():
            db_ref[...] = d_b
            dc_ref[...] = d_c
            dar_ref[...] = d_ar
            dai_ref[...] = d_ai

        @pl.when(kk > 0)
        def _():
            db_ref[...] += d_b
            dc_ref[...] += d_c
            dar_ref[...] += d_ar
            dai_ref[...] += d_ai

    chan = pl.BlockSpec((tk, CH_W), lambda j, kk: (kk, j))
    state = pl.BlockSpec((tk, CH_N), lambda j, kk: (kk, j))
    last = pl.BlockSpec((N_SEG, CH_N), lambda j, kk: (s // N_SEG - 1, j))
    row = pl.BlockSpec((1, CH_N), lambda j, kk: (0, j))
    return pl.pallas_call(
        body, name="ssm_wgrads", grid=(SSM_CHUNKS, nk),
        in_specs=[chan, chan, state, state, state, state, last, last],
        out_specs=[pl.BlockSpec((None, CH_W, 2 * CH_N), lambda j, kk: (j, 0, 0)),
                   pl.BlockSpec((None, 2 * CH_N, CH_W), lambda j, kk: (j, 0, 0)), row, row],
        out_shape=[jax.ShapeDtypeStruct((SSM_CHUNKS, CH_W, 2 * CH_N), F32),
                   jax.ShapeDtypeStruct((SSM_CHUNKS, 2 * CH_N, CH_W), F32),
                   jax.ShapeDtypeStruct((1, N_STATE), F32), jax.ShapeDtypeStruct((1, N_STATE), F32)],
        scratch_shapes=[pltpu.VMEM((N_SEG, CH_N), F32)] * 2,
        compiler_params=_cparams(2))(u, dy, g_re, g_im, h_re, h_im, h_re, h_im)


SCAN_LB = 256


def _split_by_scan_block(mat, axis):
    halves = []
    for l in range(CH_N // SCAN_LB):
        re = lax.slice_in_dim(mat, l * SCAN_LB, (l + 1) * SCAN_LB, axis=axis)
        im = lax.slice_in_dim(mat, CH_N + l * SCAN_LB, CH_N + (l + 1) * SCAN_LB, axis=axis)
        halves.append(jnp.concatenate([re, im], axis=axis))
    return jnp.stack(halves, axis=1).reshape((-1,) + halves[0].shape[1:])


def _ssm_scan(name, chan, expand12, contract12, a_re, a_im, d_row, reverse, carry=None):
    s = chan.shape[0]
    seg_len = s // N_SEG
    n_sq = int(math.log2(seg_len))
    assert 2 ** n_sq == seg_len
    rb = min(512, s)
    per_chunk = CH_N // SCAN_LB

    def body(are_ref, aim_ref, ch_ref, e_ref, k_ref, d_ref, hre_ref, him_ref, o_ref, wre_ref, wim_ref, ere, eim, cre, cim):
        e_mat, k_mat = e_ref[...], k_ref[...]
        for r in range(s // rb):
            rows = slice(r * rb, (r + 1) * rb)
            c = ch_ref[rows, :]
            if reverse:
                wre_ref[rows, :] = _dot(c, e_mat[:SCAN_LB], "nt")
                wim_ref[rows, :] = _dot(c, e_mat[SCAN_LB:], "nt")
            else:
                wre_ref[rows, :] = _dot(c, e_mat[:, :SCAN_LB], "nn")
                wim_ref[rows, :] = _dot(c, e_mat[:, SCAN_LB:], "nn")

        ar1 = are_ref[...]
        ai1 = -aim_ref[...] if reverse else aim_ref[...]
        ar = jnp.broadcast_to(ar1, (N_SEG, SCAN_LB))
        ai = jnp.broadcast_to(ai1, (N_SEG, SCAN_LB))

        def rows_of(k):
            kk = seg_len - 1 - k if reverse else k
            return pl.ds(pl.multiple_of(kk * N_SEG, N_SEG), N_SEG)

        def local(k, carry):
            hr, hi = carry
            rows = rows_of(k)
            nr = ar * hr - ai * hi + wre_ref[rows, :]
            ni = ar * hi + ai * hr + wim_ref[rows, :]
            hre_ref[rows, :] = nr
            him_ref[rows, :] = ni
            return nr, ni

        zero = jnp.zeros((N_SEG, SCAN_LB), F32)
        er, ei = lax.fori_loop(0, seg_len, local, (zero, zero))
        ere[...] = er
        eim[...] = ei
        pr, pi = ar1, ai1
        for _ in range(n_sq):
            pr, pi = pr * pr - pi * pi, 2.0 * pr * pi
        cr = jnp.zeros((1, SCAN_LB), F32)
        ci = jnp.zeros((1, SCAN_LB), F32)
        for jj in range(N_SEG):
            j = N_SEG - 1 - jj if reverse else jj
            cre[j:j + 1, :] = cr
            cim[j:j + 1, :] = ci
            er_j, ei_j = ere[j:j + 1, :], eim[j:j + 1, :]
            cr, ci = pr * cr - pi * ci + er_j, pr * ci + pi * cr + ei_j
        c_r, c_i = cre[...], cim[...]

        def fix(k, carry):
            qr, qi = carry
            rows = rows_of(k)
            hre_ref[rows, :] = hre_ref[rows, :] + (qr * c_r - qi * c_i)
            him_ref[rows, :] = him_ref[rows, :] + (qr * c_i + qi * c_r)
            return qr * ar - qi * ai, qr * ai + qi * ar

        lax.fori_loop(0, seg_len, fix, (ar, ai))

        first_of_chunk = lax.rem(pl.program_id(0), per_chunk) == 0
        for r in range(s // rb):
            rows = slice(r * rb, (r + 1) * rb)
            if reverse:
                part = (_dot(hre_ref[rows, :], k_mat[:, :SCAN_LB], "nt")
                        + _dot(him_ref[rows, :], k_mat[:, SCAN_LB:], "nt"))
            else:
                part = _dot(hre_ref[rows, :], k_mat[:SCAN_LB], "nn") + _dot(him_ref[rows, :], k_mat[SCAN_LB:], "nn")

            @pl.when(first_of_chunk)
            def _(rows=rows, part=part):
                o_ref[rows, :] = part + d_ref[...] * ch_ref[rows, :]

            @pl.when(jnp.logical_not(first_of_chunk))
            def _(rows=rows, part=part):
                o_ref[rows, :] += part

    nblk = N_STATE // SCAN_LB
    blk = pl.BlockSpec((s, SCAN_LB), lambda b: (0, b))
    row = pl.BlockSpec((1, SCAN_LB), lambda b: (0, b))
    chan_blk = pl.BlockSpec((s, CH_W), lambda b: (0, b // per_chunk))
    res = _call(name, body, (nblk,),
                [row, row, chan_blk, pl.BlockSpec((None,) + expand12.shape[1:], lambda b: (b, 0, 0)),
                 pl.BlockSpec((None,) + contract12.shape[1:], lambda b: (b, 0, 0)),
                 pl.BlockSpec((1, CH_W), lambda b: (0, b // per_chunk))],
                [blk, blk, chan_blk],
                [jax.ShapeDtypeStruct((s, N_STATE), F32)] * 2 + [jax.ShapeDtypeStruct((s, SSM_WIDTH), F32)],
                (a_re, a_im, chan, expand12, contract12, d_row),
                scratch_shapes=[pltpu.VMEM((s, SCAN_LB), F32)] * 2 + [pltpu.VMEM((N_SEG, SCAN_LB), F32)] * 4, carry=carry)
    return res[0], res[1], res[2]


def _disc(ldt, are, aim, bre, bim):
    dt = jnp.exp(ldt)
    mag = jnp.exp(are * dt)
    abr = mag * jnp.cos(aim * dt)
    abi = mag * jnp.sin(aim * dt)
    den = jnp.square(are) + jnp.square(aim)
    nr = abr - 1.0
    fre = (nr * are + abi * aim) / den
    fim = (abi * are - nr * aim) / den
    return abr, abi, fre * bre - fim * bim, fre * bim + fim * bre


def _ssm_disc_fwd(ldt, are, aim, bre, bim):
    def body(l_ref, ar_ref, ai_ref, br_ref, bi_ref, o0, o1, o2, o3):
        res = _disc(l_ref[...], ar_ref[...], ai_ref[...], br_ref[...], bi_ref[...])
        for ref, val in zip((o0, o1, o2, o3), res):
            ref[...] = val

    col = jax.ShapeDtypeStruct((N_STATE, 1), F32)
    mat = jax.ShapeDtypeStruct((N_STATE, SSM_GROUP), F32)
    return pl.pallas_call(body, name="ssm_disc_fwd", out_shape=[col, col, mat, mat],
                          in_specs=[VMEM_SPEC] * 5, out_specs=[VMEM_SPEC] * 4)(ldt, are, aim, bre, bim)


def _ssm_disc_bwd(ldt, are, aim, bre, bim, d_abr, d_abi, d_bbr, d_bbi):
    def body(l_ref, ar_ref, ai_ref, br_ref, bi_ref, c0, c1, c2, c3, g_ldt, g_are, g_aim, g_bre, g_bim):
        _, vjp = jax.vjp(_disc, l_ref[...], ar_ref[...], ai_ref[...], br_ref[...], bi_ref[...])
        dl, dar, dai, dbr, dbi = vjp((c0[...], c1[...], c2[...], c3[...]))
        state = lax.broadcasted_iota(jnp.int32, (N_STATE, SSM_GROUPS), 0)
        group = lax.broadcasted_iota(jnp.int32, (N_STATE, SSM_GROUPS), 1)
        pick = jnp.right_shift(state, 6) == group
        g_ldt[...] = jnp.sum(jnp.where(pick, dl, 0.0), axis=0, keepdims=True)
        g_are[...] = dar
        g_aim[...] = dai
        g_bre[...] = dbr
        g_bim[...] = dbi

    col = jax.ShapeDtypeStruct((N_STATE, 1), F32)
    mat = jax.ShapeDtypeStruct((N_STATE, SSM_GROUP), F32)
    return pl.pallas_call(body, name="ssm_disc_bwd",
                          out_shape=[jax.ShapeDtypeStruct((1, SSM_GROUPS), F32), col, col, mat, mat],
                          in_specs=[VMEM_SPEC] * 9, out_specs=[VMEM_SPEC] * 5,
                          compiler_params=pltpu.CompilerParams(vmem_limit_bytes=VMEM_LIMIT))(
        ldt, are, aim, bre, bim, d_abr, d_abi, d_bbr, d_bbi)


_EYE8 = np.eye(8, dtype=np.float32)


def _blockdiag_b(bb):
    t = bb.reshape(SSM_CHUNKS, 8, SSM_STATE, SSM_GROUP).transpose(0, 1, 3, 2)
    return jnp.einsum("igcn,gh->igchn", t, _EYE8).reshape(SSM_CHUNKS, CH_W, CH_N)


def _diag_of_b(m):
    t = jnp.einsum("igchn,gh->igcn", m.reshape(SSM_CHUNKS, 8, SSM_GROUP, 8, SSM_STATE), _EYE8)
    return t.transpose(0, 1, 3, 2).reshape(N_STATE, SSM_GROUP)


def _blockdiag_c(c):
    t = c.reshape(SSM_CHUNKS, 8, SSM_GROUP, SSM_STATE).transpose(0, 1, 3, 2)
    return jnp.einsum("ignc,gh->ignhc", t, _EYE8).reshape(SSM_CHUNKS, CH_N, CH_W)


def _diag_of_c(m):
    t = jnp.einsum("ignhc,gh->ignc", m.reshape(SSM_CHUNKS, 8, SSM_STATE, 8, SSM_GROUP), _EYE8)
    return t.transpose(0, 1, 3, 2).reshape(SSM_GROUPS, SSM_GROUP, SSM_STATE)


def _time_perm(a):
    s, c = a.shape
    return a.reshape(N_SEG, s // N_SEG, c).transpose(1, 0, 2).reshape(s, c)


def _time_unperm(a):
    s, c = a.shape
    return a.reshape(s // N_SEG, N_SEG, c).transpose(1, 0, 2).reshape(s, c)


def _dilate(a, d):
    s, c = a.shape
    return a if d == 1 else a.reshape(s // d, d, c).transpose(1, 0, 2).reshape(s, c)


def _undilate(a, d):
    s, c = a.shape
    return a if d == 1 else a.reshape(d, s // d, c).transpose(1, 0, 2).reshape(s, c)


def _dilate_rows(a, d):
    r, s = a.shape
    return a if d == 1 else a.reshape(r, s // d, d).transpose(0, 2, 1).reshape(r, s)


ATT_T = 4
ATT_ROWS = ATT_T * ATT_BLK


def _window(prev_ref, cur_ref, i, sl):
    if i == 0:
        return jnp.concatenate([prev_ref[:, sl], cur_ref[0:ATT_BLK, sl]], axis=0)
    return cur_ref[(i - 1) * ATT_BLK:(i + 1) * ATT_BLK, sl]


def _band_valid(first_key):
    qi = lax.broadcasted_iota(jnp.int32, (ATT_BLK, 2 * ATT_BLK), 0)
    ki = lax.broadcasted_iota(jnp.int32, (ATT_BLK, 2 * ATT_BLK), 1)
    steps = qi + ATT_BLK - ki
    return (steps >= 0) & (steps <= ATT_BLK) & (ki >= first_key)


ATT_STATW = ATT_HPG * 128


def _stat(h):
    return slice(h * 128, (h + 1) * 128)


def _stat_rows(stat):
    n = stat.shape[0]
    heads = [stat[:, _stat(h)].T[0:1, :] for h in range(ATT_HPG)]
    return jnp.concatenate(heads + [jnp.zeros((8 - ATT_HPG, n), stat.dtype)], axis=0)


def _attn_specs(nb, width=ATT_GROUPW):
    cur = pl.BlockSpec((ATT_ROWS, width), lambda b: (b, 0))
    prev = pl.BlockSpec((ATT_BLK, width), lambda b: (jnp.maximum(b * ATT_T - 1, 0), 0))
    nxt = pl.BlockSpec((ATT_BLK, width), lambda b: (jnp.minimum((b + 1) * ATT_T, nb - 1), 0))
    return cur, prev, nxt


def _attn_fwd(tag, per_seq, q, k, v):
    s = q.shape[0]
    nb = s // ATT_BLK

    def body(q_ref, kc_ref, kp_ref, vc_ref, vp_ref, o_ref, lse_ref):
        bt = pl.program_id(0)
        for i in range(ATT_T):
            has_prev = lax.rem(bt * ATT_T + i, per_seq) > 0
            valid = _band_valid(jnp.where(has_prev, 0, ATT_BLK))
            rows = slice(i * ATT_BLK, (i + 1) * ATT_BLK)
            for h in range(ATT_HPG):
                sl = slice(h * ATT_HEAD_DIM, (h + 1) * ATT_HEAD_DIM)
                kcat = _window(kp_ref, kc_ref, i, sl)
                vcat = _window(vp_ref, vc_ref, i, sl)
                sc = _dot(q_ref[rows, sl], kcat, "nt") * ATT_SCALE
                sc = jnp.where(valid, sc, NEG_INF)
                m = jnp.max(sc, axis=-1, keepdims=True)
                p = jnp.exp(sc - m)
                den = jnp.sum(p, axis=-1, keepdims=True)
                o_ref[rows, sl] = _dot(p, vcat, "nn") / den
                lse_ref[rows, _stat(h)] = jnp.broadcast_to(m + jnp.log(den), (ATT_BLK, 128))

    cur, prev, _ = _attn_specs(nb)
    stat, _, _ = _attn_specs(nb, ATT_STATW)
    return pl.pallas_call(
        body, name="attn_fwd_" + tag, grid=(nb // ATT_T,), in_specs=[cur, cur, prev, cur, prev], out_specs=[cur, stat],
        out_shape=[jax.ShapeDtypeStruct((s, ATT_GROUPW), F32), jax.ShapeDtypeStruct((s, ATT_STATW), F32)],
        compiler_params=_cparams(1))(q, k, k, v, v)


def _attn_dq(tag, per_seq, q, k, v, do, lse, delta):
    s = q.shape[0]
    nb = s // ATT_BLK

    def body(q_ref, kc_ref, kp_ref, vc_ref, vp_ref, do_ref, lse_ref, dl_ref, dq_ref):
        bt = pl.program_id(0)
        for i in range(ATT_T):
            has_prev = lax.rem(bt * ATT_T + i, per_seq) > 0
            valid = _band_valid(jnp.where(has_prev, 0, ATT_BLK))
            rows = slice(i * ATT_BLK, (i + 1) * ATT_BLK)
            for h in range(ATT_HPG):
                sl = slice(h * ATT_HEAD_DIM, (h + 1) * ATT_HEAD_DIM)
                kcat = _window(kp_ref, kc_ref, i, sl)
                vcat = _window(vp_ref, vc_ref, i, sl)
                lse = jnp.concatenate([lse_ref[rows, _stat(h)]] * 2, axis=1)
                dlt = jnp.concatenate([dl_ref[rows, _stat(h)]] * 2, axis=1)
                sc = _dot(q_ref[rows, sl], kcat, "nt") * ATT_SCALE
                p = jnp.exp(jnp.where(valid, sc, NEG_INF) - lse)
                dp = _dot(do_ref[rows, sl], vcat, "nt")
                ds = p * (dp - dlt) * ATT_SCALE
                dq_ref[rows, sl] = _dot(ds, kcat, "nn")

    cur, prev, _ = _attn_specs(nb)
    stat, _, _ = _attn_specs(nb, ATT_STATW)
    return pl.pallas_call(
        body, name="attn_dq_" + tag, grid=(nb // ATT_T,), in_specs=[cur, cur, prev, cur, prev, cur, stat, stat],
        out_specs=cur, out_shape=jax.ShapeDtypeStruct((s, ATT_GROUPW), F32),
        compiler_params=_cparams(1))(q, k, k, v, v, do, lse, delta)


def _attn_dkv(tag, per_seq, q, k, v, do, lse_t, delta_t):
    s = q.shape[0]
    nb = s // ATT_BLK

    def body(k_ref, v_ref, qc_ref, qn_ref, doc_ref, don_ref, lc_ref, ln_ref, dc_ref, dn_ref, dk_ref, dv_ref):
        bt = pl.program_id(0)
        ki = lax.broadcasted_iota(jnp.int32, (ATT_BLK, 2 * ATT_BLK), 0)
        ci = lax.broadcasted_iota(jnp.int32, (ATT_BLK, 2 * ATT_BLK), 1)

        def pair(edge_ref, cur_ref, i, sl):
            if i == ATT_T - 1:
                return jnp.concatenate([cur_ref[i * ATT_BLK:(i + 1) * ATT_BLK, sl], edge_ref[:, sl]], axis=0)
            return cur_ref[i * ATT_BLK:(i + 2) * ATT_BLK, sl]

        def pair_row(edge_ref, cur_ref, i, h):
            if i == ATT_T - 1:
                row = jnp.concatenate([cur_ref[h:h + 1, i * ATT_BLK:(i + 1) * ATT_BLK], edge_ref[h:h + 1, :]], axis=1)
            else:
                row = cur_ref[h:h + 1, i * ATT_BLK:(i + 2) * ATT_BLK]
            return jnp.broadcast_to(row, (ATT_BLK, 2 * ATT_BLK))

        for i in range(ATT_T):
            b = bt * ATT_T + i
            next_uses = (b + 1 < nb) & (lax.rem(b + 1, per_seq) > 0)
            reach = jnp.where(next_uses, 0, 4 * ATT_BLK)
            valid = ((ci < ATT_BLK) & (ci >= ki)) | ((ci >= ATT_BLK) & (ki - ci + ATT_BLK >= reach))
            rows = slice(i * ATT_BLK, (i + 1) * ATT_BLK)
            for h in range(ATT_HPG):
                sl = slice(h * ATT_HEAD_DIM, (h + 1) * ATT_HEAD_DIM)
                qcat, docat = pair(qn_ref, qc_ref, i, sl), pair(don_ref, doc_ref, i, sl)
                sc = _dot(k_ref[rows, sl], qcat, "nt") * ATT_SCALE
                p = jnp.exp(jnp.where(valid, sc, NEG_INF) - pair_row(ln_ref, lc_ref, i, h))
                dv_ref[rows, sl] = _dot(p, docat, "nn")
                dp = _dot(v_ref[rows, sl], docat, "nt")
                ds = p * (dp - pair_row(dn_ref, dc_ref, i, h)) * ATT_SCALE
                dk_ref[rows, sl] = _dot(ds, qcat, "nn")

    cur, _, nxt = _attn_specs(nb)
    stat = pl.BlockSpec((8, ATT_ROWS), lambda b: (0, b))
    snxt = pl.BlockSpec((8, ATT_BLK), lambda b: (0, jnp.minimum((b + 1) * ATT_T, nb - 1)))
    return pl.pallas_call(
        body, name="attn_dkv_" + tag, grid=(nb // ATT_T,), in_specs=[cur, cur, cur, nxt, cur, nxt, stat, snxt, stat, snxt],
        out_specs=[cur, cur], out_shape=[jax.ShapeDtypeStruct((s, ATT_GROUPW), F32)] * 2,
        compiler_params=_cparams(1))(k, v, q, q, do, do, lse_t, lse_t, delta_t, delta_t)


def _xattn_probs(q, kh):
    sc = _dot(q, kh, "nt") * XATT_SCALE
    e = jnp.exp(sc - jnp.max(sc, axis=-1, keepdims=True))
    return e / jnp.sum(e, axis=-1, keepdims=True)


def _xattn_fwd(q, kv, tm=512):
    s = q.shape[0]
    tm = min(tm, s)

    def body(q_ref, kv_ref, o_ref):
        for h in range(XATT_HEADS):
            sl = slice(h * XATT_HEAD_DIM, (h + 1) * XATT_HEAD_DIM)
            vs = slice(D_MODEL + h * XATT_HEAD_DIM, D_MODEL + (h + 1) * XATT_HEAD_DIM)
            p = _xattn_probs(q_ref[:, sl], kv_ref[:, sl])
            o_ref[:, sl] = _dot(p, kv_ref[:, vs], "nn").astype(o_ref.dtype)

    return pl.pallas_call(
        body, name="xattn_fwd", grid=(s // tm,),
        in_specs=[pl.BlockSpec((tm, D_MODEL), lambda i: (i, 0)), pl.BlockSpec(kv.shape, lambda i: (0, 0))],
        out_specs=pl.BlockSpec((tm, D_MODEL), lambda i: (i, 0)),
        out_shape=jax.ShapeDtypeStruct((s, D_MODEL), MXU_DTYPE), compiler_params=_cparams(1))(q, kv)


def _xattn_bwd(q, kv, do, tm=512):
    s = q.shape[0]
    tm = min(tm, s)

    def body(q_ref, kv_ref, do_ref, dq_ref, dkv_ref):
        first = pl.program_id(0) == 0

        @pl.when(first)
        def _():
            dkv_ref[...] = jnp.zeros_like(dkv_ref)

        for h in range(XATT_HEADS):
            sl = slice(h * XATT_HEAD_DIM, (h + 1) * XATT_HEAD_DIM)
            vs = slice(D_MODEL + h * XATT_HEAD_DIM, D_MODEL + (h + 1) * XATT_HEAD_DIM)
            p = _xattn_probs(q_ref[:, sl], kv_ref[:, sl])
            dkv_ref[:, vs] += _dot(p, do_ref[:, sl], "tn")
            dp = _dot(do_ref[:, sl], kv_ref[:, vs], "nt")
            ds = p * (dp - jnp.sum(dp * p, axis=-1, keepdims=True)) * XATT_SCALE
            dq_ref[:, sl] = _dot(ds, kv_ref[:, sl], "nn").astype(dq_ref.dtype)
            dkv_ref[:, sl] += _dot(ds, q_ref[:, sl], "tn")

    row = pl.BlockSpec((tm, D_MODEL), lambda i: (i, 0))
    whole = pl.BlockSpec(kv.shape, lambda i: (0, 0))
    return pl.pallas_call(
        body, name="xattn_bwd", grid=(s // tm,), in_specs=[row, whole, row], out_specs=[row, whole],
        out_shape=[jax.ShapeDtypeStruct((s, D_MODEL), MXU_DTYPE), jax.ShapeDtypeStruct(kv.shape, F32)],
        compiler_params=_cparams(1))(q, kv, do)


def _ln(x, g, b):
    mu = jnp.mean(x, axis=-1, keepdims=True)
    xc = x - mu
    var = jnp.mean(jnp.square(xc), axis=-1, keepdims=True)
    return xc * lax.rsqrt(var + LN_EPS) * g + b


def _res_ln(h, o, g, b):
    return _ln(DEEPNORM_ALPHA * h + o, g, b)


def _gate(gs, ga, z1, z2, batt):
    return jax.nn.sigmoid(gs) * (z1 * jax.nn.sigmoid(z2)) + jax.nn.sigmoid(ga) * batt


def _rope_tables(pos, invf, m1, m2):
    ang = pos.astype(F32) * invf
    sin = jnp.sin(ang)
    return jnp.cos(ang), -sin * m1, sin * m2


def _rope(t, cos, s_up, s_dn):
    w = t.shape[-1]
    return t * cos + pltpu.roll(t, w - ROT_DIM // 2, 1) * s_up + pltpu.roll(t, ROT_DIM // 2, 1) * s_dn


def _rope_t(dt, cos, s_up, s_dn):
    w = dt.shape[-1]
    return dt * cos + pltpu.roll(dt * s_up, ROT_DIM // 2, 1) + pltpu.roll(dt * s_dn, w - ROT_DIM // 2, 1)


def _rope_consts():
    inv_freq = ROPE_THETA ** (-jnp.arange(0, ROT_DIM, 2, dtype=F32) / ROT_DIM)
    d = np.arange(ATT_GROUPW) % ATT_HEAD_DIM
    invf = jnp.where(d < ROT_DIM, inv_freq[d % (ROT_DIM // 2)], 0.0).reshape(1, ATT_GROUPW).astype(F32)
    m1 = jnp.asarray((d < ROT_DIM // 2).astype(np.float32)).reshape(1, ATT_GROUPW)
    m2 = jnp.asarray(((d >= ROT_DIM // 2) & (d < ROT_DIM)).astype(np.float32)).reshape(1, ATT_GROUPW)
    return invf, m1, m2


def _head_sum_matrix():
    d = np.arange(ATT_GROUPW) // ATT_HEAD_DIM
    s = np.arange(ATT_STATW) // 128
    return jnp.asarray((d[:, None] == s[None, :]).astype(np.float32))


def _adamw(w, g, m, v):
    m = ADAM_B1 * m + (1.0 - ADAM_B1) * g
    v = ADAM_B2 * v + (1.0 - ADAM_B2) * jnp.square(g)
    m_hat = m / (1.0 - ADAM_B1 ** ADAM_STEP)
    v_hat = v / (1.0 - ADAM_B2 ** ADAM_STEP)
    delta = -ADAM_LR * (m_hat / (jnp.sqrt(v_hat) + ADAM_EPS) + ADAM_WD * w)
    return delta, m, v


def _local_step(x, mem, pos, target, sp, ex):
    s = x.shape[0]
    al = DEEPNORM_ALPHA
    mx = MXU_DTYPE

    h0, h0b = _rowwise("ln_in", lambda x, g, b: (lambda h: (h, h))(_ln(x, g, b)), [x],
                       [sp["ln_in_g"], sp["ln_in_b"]], [(D_MODEL, F32), (D_MODEL, mx)],
                       carry=ex.gather_carry(["w_in"]))
    proj = _mm("proj", h0b, ex.weight("w_in"), "nn", bias=sp["b_in"],
               carry=ex.gather_carry(["w_glu", "w_att_up", "w_mix_out", "w_xq", "w_xkv"]))

    ldt = jnp.repeat(sp["ssm_log_dt"].reshape(SSM_GROUPS), SSM_STATE).reshape(N_STATE, 1)
    are, aim = sp["ssm_a_re"].reshape(N_STATE, 1), sp["ssm_a_im"].reshape(N_STATE, 1)
    bre, bim = sp["ssm_b_re"].reshape(N_STATE, SSM_GROUP), sp["ssm_b_im"].reshape(N_STATE, SSM_GROUP)
    abr, abi, bbr, bbi = _ssm_disc_fwd(ldt, are, aim, bre, bim)
    a_re, a_im = abr.reshape(1, N_STATE), abi.reshape(1, N_STATE)
    bexp = jnp.concatenate([_blockdiag_b(bbr), _blockdiag_b(bbi)], axis=2).astype(mx)
    cexp = jnp.concatenate([_blockdiag_c(sp["ssm_c_re"].reshape(SSM_GROUPS, SSM_GROUP, SSM_STATE)),
                            -_blockdiag_c(sp["ssm_c_im"].reshape(SSM_GROUPS, SSM_GROUP, SSM_STATE))],
                           axis=1).astype(mx)
    u_p = _time_perm(proj[:, :SSM_WIDTH])
    b12, c12 = _split_by_scan_block(bexp, 2), _split_by_scan_block(cexp, 1)
    h_re, h_im, y_p = _ssm_scan("ssm_scan_fwd", u_p, b12, c12, a_re, a_im, sp["ssm_d"], reverse=False,
                                carry=ex.gather_carry(["w_ff1", "w_ff2"]))
    y = _time_unperm(y_p)
    ygb, = _rowwise("gelu", lambda y: jax.nn.gelu(y), [y], [], [(SSM_WIDTH, mx)])
    z = _mm("glu", ygb, ex.weight("w_glu"), "nn", bias=sp["b_glu"], carry=ex.gather_carry(["w_xo"]))

    invf, m1, m2 = _rope_consts()

    def rope_fwd(pos, q0, q1, q2, k0, k1, k2, v0, v1, v2, invf, m1, m2):
        tabs = _rope_tables(pos, invf, m1, m2)
        return tuple(_rope(t, *tabs) for t in (q0, q1, q2, k0, k1, k2)) + (v0, v1, v2)

    qkv_cols = [(proj, ATT_GROUPW, 3 + i) for i in range(9)]
    qkv = _rowwise("rope", rope_fwd, [pos] + qkv_cols, [invf, m1, m2], [(ATT_GROUPW, mx)] * 9)
    n_blocks = s // ATT_BLK
    groups = [(str(g), n_blocks // d, d) for g, d in enumerate(DILATIONS)]
    q_d = [_dilate(qkv[g], d) for g, d in enumerate(DILATIONS)]
    k_d = [_dilate(qkv[3 + g], d) for g, d in enumerate(DILATIONS)]
    v_d = [_dilate(qkv[6 + g], d) for g, d in enumerate(DILATIONS)]
    o_g, l_g = [], []
    for g, (tag, per_seq, d) in enumerate(groups):
        o, lse = _attn_fwd(tag, per_seq, q_d[g], k_d[g], v_d[g])
        o_g.append(_undilate(o, d))
        l_g.append(_undilate(lse, d))

    def merge(o0, o1, o2, l0, l1, l2):
        m = jnp.maximum(jnp.maximum(l0, l1), l2)
        e0, e1, e2 = jnp.exp(l0 - m), jnp.exp(l1 - m), jnp.exp(l2 - m)
        tot = e0 + e1 + e2

        def per_dim(e):
            w = e / tot
            return jnp.concatenate([w[:, h * 128:h * 128 + ATT_HEAD_DIM] for h in range(ATT_HPG)], axis=1)

        att = per_dim(e0) * o0 + per_dim(e1) * o1 + per_dim(e2) * o2
        lse = m + jnp.log(tot)
        return att, att, lse, _stat_rows(lse)

    att, attb, lse_tot, lse_tot_t = _rowwise("attn_merge", merge, o_g + l_g, [],
                                             [(ATT_GROUPW, F32), (ATT_GROUPW, mx), (ATT_STATW, F32)], touts=[(8, F32)])
    batt = _mm("att_up", attb, ex.weight("w_att_up"), "nn")

    gate_rows = [(proj, D_MODEL, 3), (proj, D_MODEL, 4), (z, D_MODEL, 0), (z, D_MODEL, 1), batt]
    mixedb, = _rowwise("gate", _gate, gate_rows, [], [(D_MODEL, mx)])
    o1 = _mm("mix_out", mixedb, ex.weight("w_mix_out"), "nn", bias=sp["b_mix_out"])
    h1, h1b = _rowwise("ln1", lambda h, o, g, b: (lambda r: (r, r))(_res_ln(h, o, g, b)), [h0, o1],
                       [sp["ln1_g"], sp["ln1_b"]], [(D_MODEL, F32), (D_MODEL, mx)])

    qx = _mm("xq", h1b, ex.weight("w_xq"), "nn", out_dtypes=(mx,))
    kvx = _mm("xkv", mem, ex.weight("w_xkv"), "nn", out_dtypes=(mx,))
    oxb = _xattn_fwd(qx, kvx)
    o2 = _mm("xo", oxb, ex.weight("w_xo"), "nn")
    h2, h2b = _rowwise("ln2", lambda h, o, g, b: (lambda r: (r, r))(_res_ln(h, o, g, b)), [h1, o2],
                       [sp["ln2_g"], sp["ln2_b"]], [(D_MODEL, F32), (D_MODEL, mx)])

    a_ff, fb = _mm("ff1", h2b, ex.weight("w_ff1"), "nn", bias=sp["b_ff1"],
                   epilogue=lambda r: (r, jnp.square(jnp.maximum(r, 0.0))), out_dtypes=(F32, mx))
    o3 = _mm("ff2", fb, ex.weight("w_ff2"), "nn", bias=sp["b_ff2"])

    def loss_bwd(h2, o3, tgt, g, b):
        def f(h2, o3, g, b):
            h3 = _res_ln(h2, o3, g, b)
            return 0.5 * jnp.sum(jnp.mean(jnp.square(h3 - tgt), axis=-1))

        loss, vjp = jax.vjp(f, h2, o3, g, b)
        _, dr, dg, db = vjp(jnp.ones((), F32))
        return dr, dr, dg, db, _colsum(dr), jnp.full((1, 128), loss, F32)

    dr3, dr3b, g_ln3_g, g_ln3_b, g_b_ff2, loss = _rowwise(
        "loss_ln3_bwd", loss_bwd, [h2, o3, target], [sp["ln3_g"], sp["ln3_b"]],
        [(D_MODEL, F32), (D_MODEL, mx)], [D_MODEL, D_MODEL, D_MODEL, 128])

    dab = _mm("ff2_dx", dr3b, ex.weight("w_ff2"), "nt", extras=(a_ff,),
              epilogue=lambda r, a: (r * (2.0 * jnp.maximum(a, 0.0)),), out_dtypes=(mx,))
    ex.grad("w_ff2", _mm("ff2_dw", fb, dr3b, "tn"))
    g_b_ff1, = _rowwise("ff1_db", lambda v: (_colsum(v),), [dab], [], [], [D_FF])
    ex.grad("w_ff1", _mm("ff1_dw", h2b, dab, "tn", carry=ex.carry(swap=["w_ff2"])))
    dh2 = _mm("ff1_dx", dab, ex.weight("w_ff1"), "nt", extras=(dr3,), epilogue=lambda r, d: (r + al * d,),
              carry=ex.carry(swap=["w_ff1"], ici=["w_ff2"]))

    def ln_bwd(h, o, dout, g, b):
        _, vjp = jax.vjp(_res_ln, h, o, g, b)
        _, dr, dg, db = vjp(dout)
        return dr, dr, dg, db, _colsum(dr)

    dr2, dr2b, g_ln2_g, g_ln2_b, _ = _rowwise(
        "ln2_bwd", ln_bwd, [h1, o2, dh2], [sp["ln2_g"], sp["ln2_b"]],
        [(D_MODEL, F32), (D_MODEL, mx)], [D_MODEL, D_MODEL, D_MODEL])
    ex.grad("w_xo", _mm("xo_dw", oxb, dr2b, "tn"))
    doxb = _mm("xo_dx", dr2b, ex.weight("w_xo"), "nt", out_dtypes=(mx,), carry=ex.carry(swap=["w_xo"]))
    dqxb, dkvx = _xattn_bwd(qx, kvx, doxb)
    ex.grad("w_xq", _mm("xq_dw", h1b, dqxb, "tn", carry=ex.carry(ici=["w_xo"])))
    dh1 = _mm("xq_dx", dqxb, ex.weight("w_xq"), "nt", extras=(dr2,), epilogue=lambda r, d: (r + al * d,),
              carry=ex.carry(swap=["w_xq"]))
    ex.grad("w_xkv", _mm("xkv_dw", mem, dkvx, "tn"))

    dr1, dr1b, g_ln1_g, g_ln1_b, g_b_mix = _rowwise(
        "ln1_bwd", ln_bwd, [h0, o1, dh1], [sp["ln1_g"], sp["ln1_b"]],
        [(D_MODEL, F32), (D_MODEL, mx)], [D_MODEL, D_MODEL, D_MODEL])
    ex.grad("w_mix_out", _mm("mix_dw", mixedb, dr1b, "tn", carry=ex.carry(swap=["w_xkv"], ici=["w_xq"])))
    dmixed = _mm("mix_dx", dr1b, ex.weight("w_mix_out"), "nt", carry=ex.carry(swap=["w_mix_out"]))

    def gate_bwd(gs, ga, z1, z2, batt, dm):
        _, vjp = jax.vjp(_gate, gs, ga, z1, z2, batt)
        dgs, dga, dz1, dz2, dbatt = vjp(dm)
        dz = jnp.concatenate([dz1, dz2], axis=-1)
        return dgs, dga, dz, dbatt, _colsum(dz)

    dgsb, dgab, dzb, dbattb, g_b_glu = _rowwise(
        "gate_bwd", gate_bwd, gate_rows + [dmixed], [],
        [(D_MODEL, mx), (D_MODEL, mx), (2 * D_MODEL, mx), (D_MODEL, mx)], [2 * D_MODEL])
    ex.grad("w_att_up", _mm("att_up_dw", attb, dbattb, "tn", carry=ex.carry(ici=["w_mix_out"])))
    datt = _mm("att_up_dx", dbattb, ex.weight("w_att_up"), "nt", carry=ex.carry(swap=["w_att_up"]))

    def att_delta(datt, att, hs):
        dl = jnp.dot(datt * att, hs, precision=lax.Precision.HIGHEST, preferred_element_type=F32)
        return datt, dl, _stat_rows(dl)

    dattb, delta, delta_t = _rowwise("attn_delta", att_delta, [datt, att], [_head_sum_matrix()],
                                     [(ATT_GROUPW, mx), (ATT_STATW, F32)], touts=[(8, F32)])
    dq_g, dk_g, dv_g = [], [], []
    for g, (tag, per_seq, d) in enumerate(groups):
        do_d, lt_d, dl_d = _dilate(dattb, d), _dilate(lse_tot, d), _dilate(delta, d)
        dq_g.append(_undilate(_attn_dq(tag, per_seq, q_d[g], k_d[g], v_d[g], do_d, lt_d, dl_d), d))
        dk, dv = _attn_dkv(tag, per_seq, q_d[g], k_d[g], v_d[g], do_d, _dilate_rows(lse_tot_t, d), _dilate_rows(delta_t, d))
        dk_g.append(_undilate(dk, d))
        dv_g.append(_undilate(dv, d))
    dqkv = dq_g + dk_g + dv_g

    def rope_bwd(pos, q0, q1, q2, k0, k1, k2, v0, v1, v2, invf, m1, m2):
        tabs = _rope_tables(pos, invf, m1, m2)
        return jnp.concatenate([_rope_t(t, *tabs) for t in (q0, q1, q2, k0, k1, k2)] + [v0, v1, v2], axis=-1)

    dqkvb, = _rowwise("rope_bwd", rope_bwd, [pos] + dqkv, [invf, m1, m2], [(9 * ATT_GROUPW, mx)])

    ex.grad("w_glu", _mm("glu_dw", ygb, dzb, "tn", carry=ex.carry(ici=["w_xkv", "w_att_up"])))
    dyg = _mm("glu_dx", dzb, ex.weight("w_glu"), "nt", carry=ex.carry(swap=["w_glu"]))

    def gelu_bwd(y, dyg):
        _, vjp = jax.vjp(jax.nn.gelu, y)
        return vjp(dyg)[0]

    dy, = _rowwise("gelu_bwd", gelu_bwd, [y, dyg], [], [(SSM_WIDTH, F32)])
    dy_p = _time_perm(dy)
    s_re, s_im, du_p = _ssm_scan("ssm_scan_bwd", dy_p, c12, b12, a_re, a_im, sp["ssm_d"], reverse=True,
                                 carry=ex.carry(ici=["w_ff1", "w_glu"]))
    g_bexp, g_cexp, d_abr, d_abi = _ssm_wgrads(u_p, dy_p, s_re, s_im, h_re, h_im)
    g_ssm_d, = _rowwise("ssm_dd", lambda a, b: (_colsum(a * b),), [dy_p, u_p], [], [], [SSM_WIDTH])
    g_ldt, g_are, g_aim, g_bre, g_bim = _ssm_disc_bwd(
        ldt, are, aim, bre, bim, d_abr.reshape(N_STATE, 1), d_abi.reshape(N_STATE, 1),
        _diag_of_b(g_bexp[:, :, :CH_N]), _diag_of_b(g_bexp[:, :, CH_N:]))
    g_c_re = _diag_of_c(g_cexp[:, :CH_N, :])
    g_c_im = -_diag_of_c(g_cexp[:, CH_N:, :])
    dub = _time_unperm(du_p).astype(mx)

    dprojb = jnp.concatenate([dub, dqkvb, dgsb, dgab], axis=-1)
    g_b_in, = _rowwise("in_db", lambda v: (_colsum(v),), [dprojb], [], [], [IN_COLS])
    ex.grad("w_in", _mm("in_dw", h0b, dprojb, "tn"))
    dh0 = _mm("in_dx", dprojb, ex.weight("w_in"), "nt", extras=(dr1,), epilogue=lambda r, d: (r + al * d,),
              carry=ex.carry(ici=["w_in"]))

    def ln_in_bwd(x, dout, g, b):
        _, vjp = jax.vjp(_ln, x, g, b)
        return vjp(dout)

    dx, g_ln_in_g, g_ln_in_b = _rowwise("ln_in_bwd", ln_in_bwd, [x, dh0], [sp["ln_in_g"], sp["ln_in_b"]],
                                        [(D_MODEL, F32)], [D_MODEL, D_MODEL])

    small = {"ln_in_g": g_ln_in_g, "ln_in_b": g_ln_in_b, "b_in": g_b_in, "ssm_log_dt": g_ldt, "ssm_a_re": g_are,
             "ssm_a_im": g_aim, "ssm_b_re": g_bre, "ssm_b_im": g_bim, "ssm_c_re": g_c_re, "ssm_c_im": g_c_im,
             "ssm_d": g_ssm_d, "b_glu": g_b_glu, "b_mix_out": g_b_mix, "ln1_g": g_ln1_g, "ln1_b": g_ln1_b,
             "ln2_g": g_ln2_g, "ln2_b": g_ln2_b, "b_ff1": g_b_ff1, "b_ff2": g_b_ff2, "ln3_g": g_ln3_g,
             "ln3_b": g_ln3_b}
    return loss, dx, small


def _piece_shape(k, n, axis):
    return (k // 2, n // 4) if axis == 1 else (k // 8, n)


def _aligned(v, m):
    return v if isinstance(v, int) else pl.multiple_of(v, m)


def _full_piece(ref, k, n, axis, chip, half):
    pr, pc = _piece_shape(k, n, axis)
    if axis == 1:
        return ref.at[pl.ds(_aligned(half * pr, 8), pr), pl.ds(_aligned(chip * pc, 128), pc)]
    return ref.at[pl.ds(_aligned(chip * (2 * pr) + half * pr, 8), pr), :]


def _full_shard(ref, k, n, axis, chip):
    if axis == 1:
        return ref.at[:, pl.ds(_aligned(chip * (n // 4), 128), n // 4)]
    return ref.at[pl.ds(_aligned(chip * (k // 4), 8), k // 4), :]


def _shard_piece(ref, k, n, axis, half):
    pr, _ = _piece_shape(k, n, axis)
    return ref.at[pl.ds(_aligned(half * pr, 8), pr), :]


def _mesh_pos():
    x, y, c = lax.axis_index("x"), lax.axis_index("y"), lax.axis_index("c")
    other_chips = [(1 - x, y), (x, 1 - y), (1 - x, 1 - y)]
    return x, y, c, other_chips


def _remote(src, dst, send_sem, recv_sem, dev):
    return pltpu.make_async_remote_copy(src_ref=src, dst_ref=dst, send_sem=send_sem, recv_sem=recv_sem,
                                        device_id=dev, device_id_type=MESH)


def _placed(name, fn, n_steps, where, ins, out_sds, out_block, out_index):
    def body(w_ref, *refs):
        o_ref = refs[-1]
        o_ref[...] = fn(*[r[...] for r in refs[:-1]]).astype(o_ref.dtype)

    grid_spec = pltpu.PrefetchScalarGridSpec(
        num_scalar_prefetch=1, grid=(n_steps,), in_specs=[pl.BlockSpec(bs, idx) for _, bs, idx in ins],
        out_specs=pl.BlockSpec(out_block, out_index))
    return pl.pallas_call(body, name=name, grid_spec=grid_spec, out_shape=out_sds,
                          compiler_params=_cparams(1))(where, *[a for a, _, _ in ins])


def _gather_copies(widx):
    geo = [BIG[i][1:] for i in widx]

    def ici(full, wi, j, chip, send_sems, recv_sems, c, dev):
        k, n, ax = geo[wi]
        piece = _full_piece(full[wi], k, n, ax, chip, c)
        return _remote(piece, piece, send_sems.at[wi * 6 + j], recv_sems.at[wi * 6 + j], dev)

    def d2d(full, wi, j, chip, half, send_sems, recv_sems, sib):
        k, n, ax = geo[wi]
        piece = _full_piece(full[wi], k, n, ax, chip, half)
        return _remote(piece, piece, send_sems.at[wi * 6 + 3 + j], recv_sems.at[wi * 6 + 3 + j], sib)

    def start(_, full, send_sems, recv_sems):
        x, y, c, chips = _mesh_pos()
        for wi in range(len(geo)):
            for j, (qx, qy) in enumerate(chips):
                ici(full, wi, j, 2 * x + y, send_sems, recv_sems, c, (qx, qy, c)).start()

    def finish(_, full, send_sems, recv_sems):
        x, y, c, chips = _mesh_pos()
        sib = (x, y, 1 - c)
        for wi in range(len(geo)):
            for j, (qx, qy) in enumerate(chips):
                ici(full, wi, j, 2 * qx + qy, send_sems, recv_sems, c, (qx, qy, c)).wait_recv()
                d2d(full, wi, j, 2 * qx + qy, c, send_sems, recv_sems, sib).start()
        for wi in range(len(geo)):
            for j, (qx, qy) in enumerate(chips):
                d2d(full, wi, j, 2 * qx + qy, 1 - c, send_sems, recv_sems, sib).wait_recv()
        for wi in range(len(geo)):
            for j, (qx, qy) in enumerate(chips):
                ici(full, wi, j, 2 * x + y, send_sems, recv_sems, c, (qx, qy, c)).wait_send()
                d2d(full, wi, j, 2 * qx + qy, c, send_sems, recv_sems, sib).wait_send()

    return start, finish, 6 * len(geo)


def _gather_weights(tag, fulls, widx):
    nw = len(widx)
    start, finish, n_sems = _gather_copies(widx)

    def body(*refs):
        full = refs[nw:2 * nw]
        start(None, full, *refs[2 * nw:])
        finish(None, full, *refs[2 * nw:])

    return pl.pallas_call(
        body, name="gather_weights_" + tag, in_specs=[HBM_SPEC] * nw, out_specs=[HBM_SPEC] * nw,
        out_shape=[jax.ShapeDtypeStruct(f.shape, f.dtype) for f in fulls],
        input_output_aliases={i: i for i in range(nw)},
        scratch_shapes=[pltpu.SemaphoreType.DMA((n_sems,)), pltpu.SemaphoreType.DMA((n_sems,))])(*fulls)


def _swap_copies(widx):
    geo = [BIG[i][1:] for i in widx]

    def copies(g, got, send_sems, recv_sems, base):
        x, y, c, _ = _mesh_pos()
        return [_remote(_full_piece(g[wi], k, n, ax, q, 1 - c), got[wi].at[q], send_sems.at[base + wi * 4 + q],
                        recv_sems.at[base + wi * 4 + q], (x, y, 1 - c))
                for wi, (k, n, ax) in enumerate(geo) for q in range(4)]

    def start(g, got, send_sems, recv_sems, base=0):
        for cp in copies(g, got, send_sems, recv_sems, base):
            cp.start()

    def finish(g, got, send_sems, recv_sems, base=0):
        for cp in copies(g, got, send_sems, recv_sems, base):
            cp.wait()

    return start, finish, 4 * len(geo)


def _swap_shapes(widx):
    return [jax.ShapeDtypeStruct((4,) + _piece_shape(*BIG[i][1:]), F32) for i in widx]


def _reduce_swap_halves(tag, grads, widx):
    nw = len(widx)
    start, finish, n_sems = _swap_copies(widx)

    def body(*refs):
        start(refs[:nw], refs[nw:2 * nw], *refs[2 * nw:])
        finish(refs[:nw], refs[nw:2 * nw], *refs[2 * nw:])

    return pl.pallas_call(
        body, name="reduce_swap_halves_" + tag, in_specs=[HBM_SPEC] * nw, out_specs=[HBM_SPEC] * nw,
        out_shape=_swap_shapes(widx),
        scratch_shapes=[pltpu.SemaphoreType.DMA((n_sems,)), pltpu.SemaphoreType.DMA((n_sems,))])(*grads)


def _owner_copies(nw):
    def copies(p, out, send_sems, recv_sems, base):
        x, y, c, chips = _mesh_pos()
        return [_remote(p[wi].at[2 * qx + qy], out[wi].at[j], send_sems.at[base + wi * 3 + j],
                        recv_sems.at[base + wi * 3 + j], (qx, qy, c))
                for wi in range(nw) for j, (qx, qy) in enumerate(chips)]

    def start(p, out, send_sems, recv_sems, base=0):
        for cp in copies(p, out, send_sems, recv_sems, base):
            cp.start()

    def finish(p, out, send_sems, recv_sems, base=0):
        for cp in copies(p, out, send_sems, recv_sems, base):
            cp.wait()

    return start, finish, 3 * nw


def _join_carries(a, b):
    if a is None or b is None:
        return a if b is None else b
    n_i, n_o = len(a.ins), len(a.outs)
    outs = list(a.outs) + [o + n_i if isinstance(o, int) else o for o in b.outs]

    def start(c_in, c_out, send_sems, recv_sems):
        a.start(c_in[:n_i], c_out[:n_o], send_sems, recv_sems)
        b.start(c_in[n_i:], c_out[n_o:], send_sems, recv_sems, base=a.n_sems)

    def finish(c_in, c_out, send_sems, recv_sems):
        a.finish(c_in[:n_i], c_out[:n_o], send_sems, recv_sems)
        b.finish(c_in[n_i:], c_out[n_o:], send_sems, recv_sems, base=a.n_sems)

    def done(res):
        a.done(res[:n_o])
        b.done(res[n_o:])

    return _Carry(a.ins + b.ins, outs, a.n_sems + b.n_sems, start, finish, done)


def _share_with_sibling(shards):
    nw = len(BIG)

    def body(*refs):
        out = refs[nw:2 * nw]
        send_sems, recv_sems = refs[2 * nw:]
        x, y, c, _ = _mesh_pos()
        sib = (x, y, 1 - c)
        cps = []
        for wi, (_, k, n, ax) in enumerate(BIG):
            mine = _shard_piece(out[wi], k, n, ax, c)
            cp = _remote(mine, mine, send_sems.at[wi], recv_sems.at[wi], sib)
            cp.start()
            cps.append(cp)
        for wi, (_, k, n, ax) in enumerate(BIG):
            piece = _shard_piece(out[wi], k, n, ax, 1 - c)
            _remote(piece, piece, send_sems.at[wi], recv_sems.at[wi], sib).wait_recv()
        for cp in cps:
            cp.wait_send()

    return pl.pallas_call(
        body, name="share_with_sibling", in_specs=[HBM_SPEC] * nw, out_specs=[HBM_SPEC] * nw,
        out_shape=[jax.ShapeDtypeStruct(sh.shape, sh.dtype) for sh in shards],
        input_output_aliases={i: i for i in range(nw)},
        scratch_shapes=[pltpu.SemaphoreType.DMA((nw,)), pltpu.SemaphoreType.DMA((nw,))])(*shards)


def _allreduce_small(v):
    r = v.shape[0]
    rh = r // 2
    assert rh % 8 == 0

    def body(v_ref, o_ref, sib_buf, chip_buf, send_sems, recv_sems):
        x, y, c, chips = _mesh_pos()
        me = 2 * x + y
        sib = (x, y, 1 - c)
        mine = pl.ds(pl.multiple_of(c * rh, 8), rh)
        other = pl.ds(pl.multiple_of((1 - c) * rh, 8), rh)
        swap = _remote(v_ref.at[other], sib_buf, send_sems.at[0], recv_sems.at[0], sib)
        swap.start()
        swap.wait()
        chip_buf[me] = v_ref[mine, :] + sib_buf[...]
        cps = []
        for j, (qx, qy) in enumerate(chips):
            cp = _remote(chip_buf.at[me], chip_buf.at[me], send_sems.at[1 + j], recv_sems.at[1 + j], (qx, qy, c))
            cp.start()
            cps.append(cp)
        for j, (qx, qy) in enumerate(chips):
            slot = chip_buf.at[2 * qx + qy]
            _remote(slot, slot, send_sems.at[1 + j], recv_sems.at[1 + j], (qx, qy, c)).wait_recv()
        for cp in cps:
            cp.wait_send()
        o_ref[mine, :] = ((chip_buf[0] + chip_buf[1]) + chip_buf[2]) + chip_buf[3]
        back = _remote(o_ref.at[mine], o_ref.at[mine], send_sems.at[4], recv_sems.at[4], sib)
        back.start()
        _remote(o_ref.at[other], o_ref.at[other], send_sems.at[4], recv_sems.at[4], sib).wait_recv()
        back.wait_send()

    return pl.pallas_call(
        body, name="allreduce_small", in_specs=[VMEM_SPEC], out_specs=VMEM_SPEC,
        out_shape=jax.ShapeDtypeStruct((r, 128), F32),
        scratch_shapes=[pltpu.VMEM((rh, 128), F32), pltpu.VMEM((4, rh, 128), F32),
                        pltpu.SemaphoreType.DMA((5,)), pltpu.SemaphoreType.DMA((5,))],
        compiler_params=pltpu.CompilerParams(vmem_limit_bytes=VMEM_LIMIT))(v)


def _as2d(a):
    a = a.reshape((-1, a.shape[-1])) if a.ndim > 1 else a.reshape(1, -1)
    return a


def _adamw_small(quads):
    n = len(quads)

    def body(*refs):
        for i in range(n):
            w, g, m, v = (r[...] for r in refs[4 * i:4 * i + 4])
            for ref, val in zip(refs[4 * n + 3 * i:4 * n + 3 * i + 3], _adamw(w, g, m, v)):
                ref[...] = val

    return pl.pallas_call(
        body, name="adamw_small", in_specs=[VMEM_SPEC] * (4 * n), out_specs=[VMEM_SPEC] * (3 * n),
        out_shape=[jax.ShapeDtypeStruct(q[0].shape, F32) for q in quads for _ in range(3)],
        compiler_params=pltpu.CompilerParams(vmem_limit_bytes=VMEM_LIMIT))(*[a for q in quads for a in q])


def _where():
    return jnp.stack([2 * lax.axis_index("x") + lax.axis_index("y"), lax.axis_index("c")]).astype(jnp.int32)


_BIG_INDEX = {name: i for i, (name, _, _, _) in enumerate(BIG)}


class _LocalWeights:
    def __init__(self, weights):
        self.weights, self.grads = weights, {}

    def gather_now(self, names):
        pass

    def gather_carry(self, names):
        return None

    def weight(self, name):
        return self.weights[name]

    def grad(self, name, g):
        self.grads[name] = g

    def carry(self, swap=(), ici=()):
        return None


class _Exchange:
    def __init__(self, inputs, where):
        self.inputs, self.where = inputs, where
        self.full, self.ready = {}, set()
        self.raw, self.got, self.parts, self.landed, self.geom = {}, {}, {}, {}, {}
        for name, k, n, ax in BIG:
            w2 = inputs[name][0]
            rs, cs = w2.shape
            tm = _tile(rs, 512)
            steps = rs // tm
            if ax == 1:
                blk, idx = (tm, cs), lambda i, w: (i, w[0])
            else:
                blk, idx = (tm, n), functools.partial(lambda i, w, steps: (w[0] * steps + i, 0), steps=steps)
            self.full[name] = _placed("cast_" + name, lambda w: w, steps, where, [(w2, (tm, cs), lambda i, w: (i, 0))],
                                      jax.ShapeDtypeStruct((k, n), MXU_DTYPE), blk, idx)

    def _gathered(self, names, outs):
        for name, o in zip(names, outs):
            self.full[name] = o
            self.ready.add(name)

    def gather_now(self, names):
        self._gathered(names, _gather_weights(names[0], [self.full[n] for n in names], [_BIG_INDEX[n] for n in names]))

    def gather_carry(self, names):
        start, finish, n_sems = _gather_copies([_BIG_INDEX[n] for n in names])
        return _Carry([self.full[n] for n in names], list(range(len(names))), n_sems, start, finish,
                      functools.partial(self._gathered, names))

    def weight(self, name):
        assert name in self.ready, name
        return self.full[name]

    def grad(self, name, g):
        self.raw[name] = g

    def _swapped(self, names, outs):
        for name, o in zip(names, outs):
            self.got[name] = o

    def _pair_sum(self, name):
        i = _BIG_INDEX[name]
        _, k, n, ax = BIG[i]
        g = self.raw[name]
        if name not in self.got:
            self._swapped([name], _reduce_swap_halves(name, [g], [i]))
        got = self.got[name]
        pr, pc = _piece_shape(k, n, ax)
        tm = _tile(pr, 512)
        spp = pr // tm
        self.geom[name] = (pr, pc, tm, spp)
        if ax == 1:
            g_idx = functools.partial(lambda i, w, spp: (w[1] * spp + i % spp, i // spp), spp=spp)
        else:
            g_idx = functools.partial(lambda i, w, spp: ((i // spp) * 2 * spp + w[1] * spp + i % spp, 0), spp=spp)
        self.parts[name] = _placed(
            "pair_sum_" + name, lambda a, b: a + b, 4 * spp, self.where,
            [(g, (tm, pc), g_idx), (got.reshape(4 * pr, pc), (tm, pc), lambda i, w: (i, 0))],
            jax.ShapeDtypeStruct((4 * pr, pc), BF16), (tm, pc), lambda i, w: (i, 0)).reshape(4, pr, pc)

    def _landed(self, names, outs):
        for name, o in zip(names, outs):
            self.landed[name] = o

    def carry(self, swap=(), ici=()):
        first = second = None
        if swap:
            widx = [_BIG_INDEX[n] for n in swap]
            start, finish, n_sems = _swap_copies(widx)
            first = _Carry([self.raw[n] for n in swap], _swap_shapes(widx), n_sems, start, finish,
                           functools.partial(self._swapped, list(swap)))
        if ici:
            for n in ici:
                self._pair_sum(n)
            start, finish, n_sems = _owner_copies(len(ici))
            parts = [self.parts[n] for n in ici]
            outs = [jax.ShapeDtypeStruct((3,) + p.shape[1:], p.dtype) for p in parts]
            second = _Carry(parts, outs, n_sems, start, finish, functools.partial(self._landed, list(ici)))
        return _join_carries(first, second)

    def finish(self):
        halves = []
        for name, _, _, _ in BIG:
            pr, pc, tm, spp = self.geom[name]
            ins = [(self.parts[name], (None, tm, pc), lambda i, w: (w[0], i, 0))]
            ins += [(self.landed[name], (None, tm, pc), functools.partial(lambda i, w, j: (j, i, 0), j=j))
                    for j in range(3)]
            halves.append(_placed("chip_sum_" + name,
                                  lambda a, b, c, d: ((a.astype(F32) + b.astype(F32)) + c.astype(F32)) + d.astype(F32),
                                  spp, self.where, ins, jax.ShapeDtypeStruct(self.inputs[name].shape[1:], F32), (tm, pc),
                                  functools.partial(lambda i, w, spp: (w[1] * spp + i, 0), spp=spp)))
        return dict(zip([b[0] for b in BIG], _share_with_sibling(halves)))


def _step(inputs):
    x, mem, positions, target = inputs["x"][0], inputs["mem"][0], inputs["positions"], inputs["loss_target"][0]
    pos = positions.reshape(-1, 1)
    ex = _Exchange(inputs, _where())
    sp = {name: _as2d(inputs[name]) for name in SMALL}
    memb, = _rowwise("cast_mem", lambda m: (m,), [mem], [], [(D_MODEL, MXU_DTYPE)])

    loss, dx, gsmall = _local_step(x, memb, pos, target, sp, ex)
    gshard = ex.finish()

    out = {}
    for name, _, _, _ in BIG:
        w2, m2, v2 = inputs[name][0], inputs["m_" + name][0], inputs["v_" + name][0]
        n = w2.shape[1]
        d, nm, nv = _rowwise("adamw_" + name, _adamw, [w2, gshard[name], m2, v2], [], [(n, F32)] * 3, tm=256)
        lead = inputs[name].shape
        out[name] = (gshard[name].reshape(lead), d.reshape(lead), nm.reshape(lead), nv.reshape(lead))

    def tiles(a):
        flat = a.reshape(-1)
        n = -(-flat.shape[0] // 1024) * 1024
        return jnp.pad(flat, (0, n - flat.shape[0])).reshape(n // 128, 128)

    pieces = [tiles(loss[:, :1])] + [tiles(gsmall[name]) for name in SMALL]
    if sum(p.shape[0] for p in pieces) % 16:
        pieces.append(jnp.zeros((8, 128), F32))
    red = _allreduce_small(jnp.concatenate(pieces, axis=0))
    loss_total = red[0, 0]
    grads, off = {}, pieces[0].shape[0]
    for name, p in zip(SMALL, pieces[1:]):
        shp = _as2d(inputs[name]).shape
        grads[name] = red[off:off + p.shape[0]].reshape(-1)[:shp[0] * shp[1]].reshape(shp)
        off += p.shape[0]
    upd = _adamw_small([(_as2d(inputs[n]), grads[n], _as2d(inputs["m_" + n]), _as2d(inputs["v_" + n])) for n in SMALL])
    for i, name in enumerate(SMALL):
        shp = inputs[name].shape
        out[name] = (grads[name].reshape(shp),) + tuple(t.reshape(shp) for t in upd[3 * i:3 * i + 3])
    return loss_total, dx.reshape(inputs["x"].shape), out


_ARG_NAMES = (("x", "mem", "positions") + WEIGHT_ORDER + ("loss_target",) + tuple("m_" + n for n in WEIGHT_ORDER)
              + tuple("v_" + n for n in WEIGHT_ORDER))


def kernel(x, mem, positions, ln_in_g, ln_in_b, w_in, b_in, ssm_log_dt, ssm_a_re, ssm_a_im, ssm_b_re, ssm_b_im, ssm_c_re, ssm_c_im, ssm_d, w_glu, b_glu, w_att_up, w_mix_out, b_mix_out, ln1_g, ln1_b, w_xq, w_xkv, w_xo, ln2_g, ln2_b, w_ff1, b_ff1, w_ff2, b_ff2, ln3_g, ln3_b, loss_target, m_ln_in_g, m_ln_in_b, m_w_in, m_b_in, m_ssm_log_dt, m_ssm_a_re, m_ssm_a_im, m_ssm_b_re, m_ssm_b_im, m_ssm_c_re, m_ssm_c_im, m_ssm_d, m_w_glu, m_b_glu, m_w_att_up, m_w_mix_out, m_b_mix_out, m_ln1_g, m_ln1_b, m_w_xq, m_w_xkv, m_w_xo, m_ln2_g, m_ln2_b, m_w_ff1, m_b_ff1, m_w_ff2, m_b_ff2, m_ln3_g, m_ln3_b, v_ln_in_g, v_ln_in_b, v_w_in, v_b_in, v_ssm_log_dt, v_ssm_a_re, v_ssm_a_im, v_ssm_b_re, v_ssm_b_im, v_ssm_c_re, v_ssm_c_im, v_ssm_d, v_w_glu, v_b_glu, v_w_att_up, v_w_mix_out, v_b_mix_out, v_ln1_g, v_ln1_b, v_w_xq, v_w_xkv, v_w_xo, v_ln2_g, v_ln2_b, v_w_ff1, v_b_ff1, v_w_ff2, v_b_ff2, v_ln3_g, v_ln3_b):
    args = (x, mem, positions, ln_in_g, ln_in_b, w_in, b_in, ssm_log_dt, ssm_a_re, ssm_a_im, ssm_b_re, ssm_b_im, ssm_c_re, ssm_c_im, ssm_d, w_glu, b_glu, w_att_up, w_mix_out, b_mix_out, ln1_g, ln1_b, w_xq, w_xkv, w_xo, ln2_g, ln2_b, w_ff1, b_ff1, w_ff2, b_ff2, ln3_g, ln3_b, loss_target, m_ln_in_g, m_ln_in_b, m_w_in, m_b_in, m_ssm_log_dt, m_ssm_a_re, m_ssm_a_im, m_ssm_b_re, m_ssm_b_im, m_ssm_c_re, m_ssm_c_im, m_ssm_d, m_w_glu, m_b_glu, m_w_att_up, m_w_mix_out, m_b_mix_out, m_ln1_g, m_ln1_b, m_w_xq, m_w_xkv, m_w_xo, m_ln2_g, m_ln2_b, m_w_ff1, m_b_ff1, m_w_ff2, m_b_ff2, m_ln3_g, m_ln3_b, v_ln_in_g, v_ln_in_b, v_w_in, v_b_in, v_ssm_log_dt, v_ssm_a_re, v_ssm_a_im, v_ssm_b_re, v_ssm_b_im, v_ssm_c_re, v_ssm_c_im, v_ssm_d, v_w_glu, v_b_glu, v_w_att_up, v_w_mix_out, v_b_mix_out, v_ln1_g, v_ln1_b, v_w_xq, v_w_xkv, v_w_xo, v_ln2_g, v_ln2_b, v_w_ff1, v_b_ff1, v_w_ff2, v_b_ff2, v_ln3_g, v_ln3_b)
    assert len(args) == len(_ARG_NAMES)
    inputs = dict(zip(_ARG_NAMES, args))
    loss, dx, out = _step(inputs)
    res = [loss, dx]
    for k in range(4):
        res += [out[name][k] for name in WEIGHT_ORDER]
    return tuple(res)
```

```python
import functools
import math

import numpy as np
import jax
import jax.numpy as jnp
from jax import lax
from jax.experimental import pallas as pl
from jax.experimental.pallas import tpu as pltpu

F32 = jnp.float32
BF16 = jnp.bfloat16
MXU_DTYPE = jnp.bfloat16

D_MODEL = 1024
SSM_GROUP = 16
SSM_WIDTH = 768
SSM_GROUPS = 48
SSM_STATE = 64
N_STATE = SSM_GROUPS * SSM_STATE
SSM_CHUNKS = 6
CH_W = 128
CH_N = 512
ATT_HEAD_DIM = 64
ATT_HPG = 4
ATT_GROUPW = ATT_HPG * ATT_HEAD_DIM
DILATIONS = (1, 4, 16)
ATT_BLK = 128
ATT_SCALE = ATT_HEAD_DIM ** -0.5
ROT_DIM = 16
ROPE_THETA = 500000.0
XATT_HEADS = 4
XATT_HEAD_DIM = 256
XATT_SCALE = XATT_HEAD_DIM ** -0.5
D_FF = 4096
IN_COLS = 5120
DEEPNORM_ALPHA = 2.0 ** 0.25
LN_EPS = 1e-5
NEG_INF = -1e30
ADAM_LR = 0.001
ADAM_B1 = 0.9
ADAM_B2 = 0.999
ADAM_EPS = 1e-08
ADAM_WD = 0.01
ADAM_STEP = 10

N_SEG = 32
VMEM_LIMIT = 48 * 1024 * 1024
MESH = pl.DeviceIdType.MESH
HBM_SPEC = pl.BlockSpec(memory_space=pltpu.HBM)
VMEM_SPEC = pl.BlockSpec(memory_space=pltpu.VMEM)

BIG = (("w_in", 1024, 5120, 1), ("w_glu", 768, 2048, 1), ("w_att_up", 256, 1024, 1),
       ("w_mix_out", 1024, 1024, 0), ("w_xq", 1024, 1024, 0), ("w_xkv", 1024, 2048, 1),
       ("w_xo", 1024, 1024, 0), ("w_ff1", 1024, 4096, 1), ("w_ff2", 4096, 1024, 0))
SMALL = ("ln_in_g", "ln_in_b", "b_in", "ssm_log_dt", "ssm_a_re", "ssm_a_im", "ssm_b_re", "ssm_b_im",
         "ssm_c_re", "ssm_c_im", "ssm_d", "b_glu", "b_mix_out", "ln1_g", "ln1_b", "ln2_g", "ln2_b",
         "b_ff1", "b_ff2", "ln3_g", "ln3_b")
WEIGHT_ORDER = ("ln_in_g", "ln_in_b", "w_in", "b_in", "ssm_log_dt", "ssm_a_re", "ssm_a_im", "ssm_b_re",
                "ssm_b_im", "ssm_c_re", "ssm_c_im", "ssm_d", "w_glu", "b_glu", "w_att_up", "w_mix_out",
                "b_mix_out", "ln1_g", "ln1_b", "w_xq", "w_xkv", "w_xo", "ln2_g", "ln2_b", "w_ff1", "b_ff1",
                "w_ff2", "b_ff2", "ln3_g", "ln3_b")


def _cparams(n_axes):
    return pltpu.CompilerParams(dimension_semantics=("arbitrary",) * n_axes, vmem_limit_bytes=VMEM_LIMIT)


class _Carry:
    def __init__(self, ins, outs, n_sems, start, finish, done):
        self.ins, self.outs, self.n_sems, self.start, self.finish, self.done = ins, outs, n_sems, start, finish, done


def _call(name, body, grid, in_specs, out_specs, out_shape, args, scratch_shapes=(), carry=None):
    in_specs, out_specs, out_shape = list(in_specs), list(out_specs), list(out_shape)
    params = _cparams(len(grid))
    if carry is None:
        return pl.pallas_call(body, name=name, grid=grid, in_specs=in_specs, out_specs=out_specs, out_shape=out_shape,
                              scratch_shapes=list(scratch_shapes), compiler_params=params)(*args)
    n_in, n_out, n_ci, n_co = len(in_specs), len(out_specs), len(carry.ins), len(carry.outs)
    n_scr = len(scratch_shapes)

    def wrapped(*refs):
        ins, c_in = refs[:n_in], refs[n_in:n_in + n_ci]
        outs, c_out = refs[n_in + n_ci:n_in + n_ci + n_out], refs[n_in + n_ci + n_out:n_in + n_ci + n_out + n_co]
        scratch = refs[n_in + n_ci + n_out + n_co:n_in + n_ci + n_out + n_co + n_scr]
        send_sems, recv_sems = refs[-2:]
        ids = [pl.program_id(a) for a in range(len(grid))]
        first = functools.reduce(jnp.logical_and, [i == 0 for i in ids])
        last = functools.reduce(jnp.logical_and, [i == g - 1 for i, g in zip(ids, grid)])

        @pl.when(first)
        def _():
            carry.start(c_in, c_out, send_sems, recv_sems)

        body(*ins, *outs, *scratch)

        @pl.when(last)
        def _():
            carry.finish(c_in, c_out, send_sems, recv_sems)

    c_shapes = [jax.ShapeDtypeStruct(carry.ins[o].shape, carry.ins[o].dtype) if isinstance(o, int) else o
                for o in carry.outs]
    aliases = {n_in + o: n_out + i for i, o in enumerate(carry.outs) if isinstance(o, int)}
    res = pl.pallas_call(
        wrapped, name=name, grid=grid, in_specs=in_specs + [HBM_SPEC] * n_ci, out_specs=out_specs + [HBM_SPEC] * n_co,
        out_shape=out_shape + c_shapes, input_output_aliases=aliases,
        scratch_shapes=list(scratch_shapes) + [pltpu.SemaphoreType.DMA((carry.n_sems,))] * 2,
        compiler_params=params)(*args, *carry.ins)
    carry.done(res[n_out:])
    return res[:n_out]


def _rowwise(name, fn, rows, consts, outs, reds=(), tm=256, touts=(), carry=None):
    n_rows = (rows[0][0] if isinstance(rows[0], tuple) else rows[0]).shape[-2]
    tm = min(tm, n_rows)
    assert n_rows % tm == 0, (name, n_rows, tm)
    specs, args = [], []
    for r in rows:
        if isinstance(r, tuple) and len(r) == 3:
            arr, width, cb = r
            specs.append(pl.BlockSpec((tm, width), functools.partial(lambda i, cb: (i, cb), cb=cb)))
        elif isinstance(r, tuple):
            arr, slot = r
            specs.append(pl.BlockSpec((None, tm, arr.shape[2]), functools.partial(lambda i, s: (s, i, 0), s=slot)))
        else:
            arr = r
            specs.append(pl.BlockSpec((tm, arr.shape[1]), lambda i: (i, 0)))
        args.append(arr)
        assert arr.shape[-2] == n_rows, (name, arr.shape, n_rows)
    for cst in consts:
        specs.append(pl.BlockSpec(cst.shape, lambda i: (0, 0)))
        args.append(cst)
    n_r, n_c, n_o, n_d = len(rows), len(consts), len(outs) + len(touts), len(reds)
    out_shape = [jax.ShapeDtypeStruct((n_rows, c), dt) for c, dt in outs]
    out_specs = [pl.BlockSpec((tm, c), lambda i: (i, 0)) for c, _ in outs]
    out_shape += [jax.ShapeDtypeStruct((r, n_rows), dt) for r, dt in touts]
    out_specs += [pl.BlockSpec((r, tm), lambda i: (0, i)) for r, _ in touts]
    out_shape += [jax.ShapeDtypeStruct((1, c), F32) for c in reds]
    out_specs += [pl.BlockSpec((1, c), lambda i: (0, 0)) for c in reds]

    def body(*refs):
        ins = [r[...] for r in refs[:n_r + n_c]]
        o_refs = refs[n_r + n_c:n_r + n_c + n_o]
        d_refs = refs[n_r + n_c + n_o:]
        res = fn(*ins)
        res = res if isinstance(res, (tuple, list)) else (res,)
        assert len(res) == n_o + n_d, (name, len(res))
        for ref, val in zip(o_refs, res[:n_o]):
            ref[...] = val.astype(ref.dtype)
        first = pl.program_id(0) == 0
        for ref, val in zip(d_refs, res[n_o:]):
            @pl.when(first)
            def _(ref=ref, val=val):
                ref[...] = val

            @pl.when(jnp.logical_not(first))
            def _(ref=ref, val=val):
                ref[...] += val

    return _call(name, body, (n_rows // tm,), specs, out_specs, out_shape, args, carry=carry)


def _colsum(v):
    return jnp.sum(v.astype(F32), axis=0, keepdims=True)


_DIMS = {"nn": (((1,), (0,)), ((), ())), "nt": (((1,), (1,)), ((), ())), "tn": (((0,), (0,)), ((), ()))}


def _tile(dim, want):
    if dim <= want:
        return dim
    return max(t for t in range(128, want + 1, 128) if dim % t == 0)


def _dot(a, b, mode):
    return lax.dot_general(a.astype(MXU_DTYPE), b.astype(MXU_DTYPE), _DIMS[mode], preferred_element_type=F32)


def _mm(name, a, b, mode, *, bias=None, extras=(), epilogue=None, out_dtypes=(F32,), tm=1024, tn=1024, tk=1024,
        carry=None):
    if mode == "nn":
        (m, k), (_, n) = a.shape, b.shape
    elif mode == "nt":
        (m, k), (n, _) = a.shape, b.shape
    else:
        (k, m), (_, n) = a.shape, b.shape
    if k > tk:
        tn, tk = tn // 2, 5 * tk
        if mode == "tn":
            tm = tm // 2
    tn = _tile(n, tn)
    tk = _tile(k, tk)
    nk = k // tk

    def vmem_bytes(rows):
        blocks = rows * tk * a.dtype.itemsize + tk * tn * b.dtype.itemsize
        blocks += sum(rows * tn * e.dtype.itemsize for e in extras)
        blocks += sum(rows * tn * jnp.dtype(dt).itemsize for dt in out_dtypes)
        return 2 * blocks + (rows * tn * 4 if nk > 1 else 0)

    tm = _tile(m, tm if mode == "tn" else 2 * tm)
    while vmem_bytes(tm) > 3 * VMEM_LIMIT // 4 and tm % 256 == 0:
        tm //= 2
    assert m % tm == 0 and n % tn == 0 and k % tk == 0, (name, m, n, k)
    a_spec = {"nn": pl.BlockSpec((tm, tk), lambda i, j, kk: (i, kk)),
              "nt": pl.BlockSpec((tm, tk), lambda i, j, kk: (i, kk)),
              "tn": pl.BlockSpec((tk, tm), lambda i, j, kk: (kk, i))}[mode]
    b_spec = {"nn": pl.BlockSpec((tk, tn), lambda i, j, kk: (kk, j)),
              "nt": pl.BlockSpec((tn, tk), lambda i, j, kk: (j, kk)),
              "tn": pl.BlockSpec((tk, tn), lambda i, j, kk: (kk, j))}[mode]
    specs, args = [a_spec, b_spec], [a, b]
    if bias is not None:
        specs.append(pl.BlockSpec((1, tn), lambda i, j, kk: (0, j)))
        args.append(bias)
    for e in extras:
        specs.append(pl.BlockSpec((tm, tn), lambda i, j, kk: (i, j)))
        args.append(e)
    n_e, n_o = len(extras), len(out_dtypes)
    has_bias = bias is not None

    def body(*refs):
        a_ref, b_ref = refs[0], refs[1]
        pos = 2
        bias_ref = refs[pos] if has_bias else None
        pos += int(has_bias)
        e_refs = refs[pos:pos + n_e]
        o_refs = refs[pos + n_e:pos + n_e + n_o]
        acc_ref = refs[pos + n_e + n_o] if nk > 1 else None
        part = _dot(a_ref[...], b_ref[...], mode)

        def finish(r):
            if has_bias:
                r = r + bias_ref[...]
            res = epilogue(r, *[e[...] for e in e_refs]) if epilogue is not None else (r,)
            for ref, val in zip(o_refs, res):
                ref[...] = val.astype(ref.dtype)

        if nk == 1:
            finish(part)
        else:
            kk = pl.program_id(2)

            @pl.when(kk == 0)
            def _():
                acc_ref[...] = part

            @pl.when(kk > 0)
            def _():
                acc_ref[...] += part

            @pl.when(kk == nk - 1)
            def _():
                finish(acc_ref[...])

    res = _call(name, body, (m // tm, n // tn, nk), specs,
                [pl.BlockSpec((tm, tn), lambda i, j, kk: (i, j)) for _ in out_dtypes],
                [jax.ShapeDtypeStruct((m, n), dt) for dt in out_dtypes], args,
                scratch_shapes=[pltpu.VMEM((tm, tn), F32)] if nk > 1 else [], carry=carry)
    return res[0] if n_o == 1 else res


def _ssm_wgrads(u, dy, g_re, g_im, h_re, h_im, tk=512):
    s = u.shape[0]
    tk = min(tk, s)
    nk = s // tk
    assert tk % N_SEG == 0

    def body(u_ref, dy_ref, gre_ref, gim_ref, hre_ref, him_ref, lre_ref, lim_ref, db_ref, dc_ref, dar_ref, dai_ref,
             pre_ref, pim_ref):
        kk = pl.program_id(1)
        u_blk, dy_blk = u_ref[...], dy_ref[...]
        g_r, g_i, h_r, h_i = gre_ref[...], gim_ref[...], hre_ref[...], him_ref[...]
        d_b = jnp.concatenate([_dot(u_blk, g_r, "tn"), _dot(u_blk, g_i, "tn")], axis=1)
        d_c = jnp.concatenate([_dot(h_r, dy_blk, "tn"), _dot(h_i, dy_blk, "tn")], axis=0)

        @pl.when(kk == 0)
        def _():
            first_row = lax.broadcasted_iota(jnp.int32, (N_SEG, CH_N), 0) == 0
            pre_ref[...] = jnp.where(first_row, 0.0, pltpu.roll(lre_ref[...], 1, 0))
            pim_ref[...] = jnp.where(first_row, 0.0, pltpu.roll(lim_ref[...], 1, 0))

        p_r = jnp.concatenate([pre_ref[...], h_r[:tk - N_SEG]], axis=0)
        p_i = jnp.concatenate([pim_ref[...], h_i[:tk - N_SEG]], axis=0)
        pre_ref[...] = h_r[tk - N_SEG:]
        pim_ref[...] = h_i[tk - N_SEG:]
        d_ar = jnp.sum(g_r * p_r + g_i * p_i, axis=0, keepdims=True)
        d_ai = jnp.sum(g_i * p_r - g_r * p_i, axis=0, keepdims=True)

        @pl.when(kk == 0)
        def _():
            db_ref[...] = d_b
            dc_ref[...] = d_c
            dar_ref[...] = d_ar
            dai_ref[...] = d_ai

        @pl.when(kk > 0)
        def _():
            db_ref[...] += d_b
            dc_ref[...] += d_c
            dar_ref[...] += d_ar
            dai_ref[...] += d_ai

    chan = pl.BlockSpec((tk, CH_W), lambda j, kk: (kk, j))
    state = pl.BlockSpec((tk, CH_N), lambda j, kk: (kk, j))
    last = pl.BlockSpec((N_SEG, CH_N), lambda j, kk: (s // N_SEG - 1, j))
    row = pl.BlockSpec((1, CH_N), lambda j, kk: (0, j))
    return pl.pallas_call(
        body, name="ssm_wgrads", grid=(SSM_CHUNKS, nk),
        in_specs=[chan, chan, state, state, state, state, last, last],
        out_specs=[pl.BlockSpec((None, CH_W, 2 * CH_N), lambda j, kk: (j, 0, 0)),
                   pl.BlockSpec((None, 2 * CH_N, CH_W), lambda j, kk: (j, 0, 0)), row, row],
        out_shape=[jax.ShapeDtypeStruct((SSM_CHUNKS, CH_W, 2 * CH_N), F32),
                   jax.ShapeDtypeStruct((SSM_CHUNKS, 2 * CH_N, CH_W), F32),
                   jax.ShapeDtypeStruct((1, N_STATE), F32), jax.ShapeDtypeStruct((1, N_STATE), F32)],
        scratch_shapes=[pltpu.VMEM((N_SEG, CH_N), F32)] * 2,
        compiler_params=_cparams(2))(u, dy, g_re, g_im, h_re, h_im, h_re, h_im)


SCAN_LB = 256


def _split_by_scan_block(mat, axis):
    halves = []
    for l in range(CH_N // SCAN_LB):
        re = lax.slice_in_dim(mat, l * SCAN_LB, (l + 1) * SCAN_LB, axis=axis)
        im = lax.slice_in_dim(mat, CH_N + l * SCAN_LB, CH_N + (l + 1) * SCAN_LB, axis=axis)
        halves.append(jnp.concatenate([re, im], axis=axis))
    return jnp.stack(halves, axis=1).reshape((-1,) + halves[0].shape[1:])


def _ssm_scan(name, chan, expand12, contract12, a_re, a_im, d_row, reverse, carry=None):
    s = chan.shape[0]
    seg_len = s // N_SEG
    n_sq = int(math.log2(seg_len))
    assert 2 ** n_sq == seg_len
    rb = min(512, s)
    per_chunk = CH_N // SCAN_LB

    def body(are_ref, aim_ref, ch_ref, e_ref, k_ref, d_ref, hre_ref, him_ref, o_ref, wre_ref, wim_ref, ere, eim, cre, cim):
        e_mat, k_mat = e_ref[...], k_ref[...]
        for r in range(s // rb):
            rows = slice(r * rb, (r + 1) * rb)
            c = ch_ref[rows, :]
            if reverse:
                wre_ref[rows, :] = _dot(c, e_mat[:SCAN_LB], "nt")
                wim_ref[rows, :] = _dot(c, e_mat[SCAN_LB:], "nt")
            else:
                wre_ref[rows, :] = _dot(c, e_mat[:, :SCAN_LB], "nn")
                wim_ref[rows, :] = _dot(c, e_mat[:, SCAN_LB:], "nn")

        ar1 = are_ref[...]
        ai1 = -aim_ref[...] if reverse else aim_ref[...]
        ar = jnp.broadcast_to(ar1, (N_SEG, SCAN_LB))
        ai = jnp.broadcast_to(ai1, (N_SEG, SCAN_LB))

        def rows_of(k):
            kk = seg_len - 1 - k if reverse else k
            return pl.ds(pl.multiple_of(kk * N_SEG, N_SEG), N_SEG)

        def local(k, carry):
            hr, hi = carry
            rows = rows_of(k)
            nr = ar * hr - ai * hi + wre_ref[rows, :]
            ni = ar * hi + ai * hr + wim_ref[rows, :]
            hre_ref[rows, :] = nr
            him_ref[rows, :] = ni
            return nr, ni

        zero = jnp.zeros((N_SEG, SCAN_LB), F32)
        er, ei = lax.fori_loop(0, seg_len, local, (zero, zero))
        ere[...] = er
        eim[...] = ei
        pr, pi = ar1, ai1
        for _ in range(n_sq):
            pr, pi = pr * pr - pi * pi, 2.0 * pr * pi
        cr = jnp.zeros((1, SCAN_LB), F32)
        ci = jnp.zeros((1, SCAN_LB), F32)
        for jj in range(N_SEG):
            j = N_SEG - 1 - jj if reverse else jj
            cre[j:j + 1, :] = cr
            cim[j:j + 1, :] = ci
            er_j, ei_j = ere[j:j + 1, :], eim[j:j + 1, :]
            cr, ci = pr * cr - pi * ci + er_j, pr * ci + pi * cr + ei_j
        c_r, c_i = cre[...], cim[...]

        def fix(k, carry):
            qr, qi = carry
            rows = rows_of(k)
            hre_ref[rows, :] = hre_ref[rows, :] + (qr * c_r - qi * c_i)
            him_ref[rows, :] = him_ref[rows, :] + (qr * c_i + qi * c_r)
            return qr * ar - qi * ai, qr * ai + qi * ar

        lax.fori_loop(0, seg_len, fix, (ar, ai))

        first_of_chunk = lax.rem(pl.program_id(0), per_chunk) == 0
        for r in range(s // rb):
            rows = slice(r * rb, (r + 1) * rb)
            if reverse:
                part = (_dot(hre_ref[rows, :], k_mat[:, :SCAN_LB], "nt")
                        + _dot(him_ref[rows, :], k_mat[:, SCAN_LB:], "nt"))
            else:
                part = _dot(hre_ref[rows, :], k_mat[:SCAN_LB], "nn") + _dot(him_ref[rows, :], k_mat[SCAN_LB:], "nn")

            @pl.when(first_of_chunk)
            def _(rows=rows, part=part):
                o_ref[rows, :] = part + d_ref[...] * ch_ref[rows, :]

            @pl.when(jnp.logical_not(first_of_chunk))
            def _(rows=rows, part=part):
                o_ref[rows, :] += part

    nblk = N_STATE // SCAN_LB
    blk = pl.BlockSpec((s, SCAN_LB), lambda b: (0, b))
    row = pl.BlockSpec((1, SCAN_LB), lambda b: (0, b))
    chan_blk = pl.BlockSpec((s, CH_W), lambda b: (0, b // per_chunk))
    res = _call(name, body, (nblk,),
                [row, row, chan_blk, pl.BlockSpec((None,) + expand12.shape[1:], lambda b: (b, 0, 0)),
                 pl.BlockSpec((None,) + contract12.shape[1:], lambda b: (b, 0, 0)),
                 pl.BlockSpec((1, CH_W), lambda b: (0, b // per_chunk))],
                [blk, blk, chan_blk],
                [jax.ShapeDtypeStruct((s, N_STATE), F32)] * 2 + [jax.ShapeDtypeStruct((s, SSM_WIDTH), F32)],
                (a_re, a_im, chan, expand12, contract12, d_row),
                scratch_shapes=[pltpu.VMEM((s, SCAN_LB), F32)] * 2 + [pltpu.VMEM((N_SEG, SCAN_LB), F32)] * 4, carry=carry)
    return res[0], res[1], res[2]


def _disc(ldt, are, aim, bre, bim):
    dt = jnp.exp(ldt)
    mag = jnp.exp(are * dt)
    abr = mag * jnp.cos(aim * dt)
    abi = mag * jnp.sin(aim * dt)
    den = jnp.square(are) + jnp.square(aim)
    nr = abr - 1.0
    fre = (nr * are + abi * aim) / den
    fim = (abi * are - nr * aim) / den
    return abr, abi, fre * bre - fim * bim, fre * bim + fim * bre


def _ssm_disc_fwd(ldt, are, aim, bre, bim):
    def body(l_ref, ar_ref, ai_ref, br_ref, bi_ref, o0, o1, o2, o3):
        res = _disc(l_ref[...], ar_ref[...], ai_ref[...], br_ref[...], bi_ref[...])
        for ref, val in zip((o0, o1, o2, o3), res):
            ref[...] = val

    col = jax.ShapeDtypeStruct((N_STATE, 1), F32)
    mat = jax.ShapeDtypeStruct((N_STATE, SSM_GROUP), F32)
    return pl.pallas_call(body, name="ssm_disc_fwd", out_shape=[col, col, mat, mat],
                          in_specs=[VMEM_SPEC] * 5, out_specs=[VMEM_SPEC] * 4)(ldt, are, aim, bre, bim)


def _ssm_disc_bwd(ldt, are, aim, bre, bim, d_abr, d_abi, d_bbr, d_bbi):
    def body(l_ref, ar_ref, ai_ref, br_ref, bi_ref, c0, c1, c2, c3, g_ldt, g_are, g_aim, g_bre, g_bim):
        _, vjp = jax.vjp(_disc, l_ref[...], ar_ref[...], ai_ref[...], br_ref[...], bi_ref[...])
        dl, dar, dai, dbr, dbi = vjp((c0[...], c1[...], c2[...], c3[...]))
        state = lax.broadcasted_iota(jnp.int32, (N_STATE, SSM_GROUPS), 0)
        group = lax.broadcasted_iota(jnp.int32, (N_STATE, SSM_GROUPS), 1)
        pick = jnp.right_shift(state, 6) == group
        g_ldt[...] = jnp.sum(jnp.where(pick, dl, 0.0), axis=0, keepdims=True)
        g_are[...] = dar
        g_aim[...] = dai
        g_bre[...] = dbr
        g_bim[...] = dbi

    col = jax.ShapeDtypeStruct((N_STATE, 1), F32)
    mat = jax.ShapeDtypeStruct((N_STATE, SSM_GROUP), F32)
    return pl.pallas_call(body, name="ssm_disc_bwd",
                          out_shape=[jax.ShapeDtypeStruct((1, SSM_GROUPS), F32), col, col, mat, mat],
                          in_specs=[VMEM_SPEC] * 9, out_specs=[VMEM_SPEC] * 5,
                          compiler_params=pltpu.CompilerParams(vmem_limit_bytes=VMEM_LIMIT))(
        ldt, are, aim, bre, bim, d_abr, d_abi, d_bbr, d_bbi)


_EYE8 = np.eye(8, dtype=np.float32)


def _blockdiag_b(bb):
    t = bb.reshape(SSM_CHUNKS, 8, SSM_STATE, SSM_GROUP).transpose(0, 1, 3, 2)
    return jnp.einsum("igcn,gh->igchn", t, _EYE8).reshape(SSM_CHUNKS, CH_W, CH_N)


def _diag_of_b(m):
    t = jnp.einsum("igchn,gh->igcn", m.reshape(SSM_CHUNKS, 8, SSM_GROUP, 8, SSM_STATE), _EYE8)
    return t.transpose(0, 1, 3, 2).reshape(N_STATE, SSM_GROUP)


def _blockdiag_c(c):
    t = c.reshape(SSM_CHUNKS, 8, SSM_GROUP, SSM_STATE).transpose(0, 1, 3, 2)
    return jnp.einsum("ignc,gh->ignhc", t, _EYE8).reshape(SSM_CHUNKS, CH_N, CH_W)


def _diag_of_c(m):
    t = jnp.einsum("ignhc,gh->ignc", m.reshape(SSM_CHUNKS, 8, SSM_STATE, 8, SSM_GROUP), _EYE8)
    return t.transpose(0, 1, 3, 2).reshape(SSM_GROUPS, SSM_GROUP, SSM_STATE)


def _time_perm(a):
    s, c = a.shape
    return a.reshape(N_SEG, s // N_SEG, c).transpose(1, 0, 2).reshape(s, c)


def _time_unperm(a):
    s, c = a.shape
    return a.reshape(s // N_SEG, N_SEG, c).transpose(1, 0, 2).reshape(s, c)


def _dilate(a, d):
    s, c = a.shape
    return a if d == 1 else a.reshape(s // d, d, c).transpose(1, 0, 2).reshape(s, c)


def _undilate(a, d):
    s, c = a.shape
    return a if d == 1 else a.reshape(d, s // d, c).transpose(1, 0, 2).reshape(s, c)


def _dilate_rows(a, d):
    r, s = a.shape
    return a if d == 1 else a.reshape(r, s // d, d).transpose(0, 2, 1).reshape(r, s)


ATT_T = 4
ATT_ROWS = ATT_T * ATT_BLK


def _window(prev_ref, cur_ref, i, sl):
    if i == 0:
        return jnp.concatenate([prev_ref[:, sl], cur_ref[0:ATT_BLK, sl]], axis=0)
    return cur_ref[(i - 1) * ATT_BLK:(i + 1) * ATT_BLK, sl]


def _band_valid(first_key):
    qi = lax.broadcasted_iota(jnp.int32, (ATT_BLK, 2 * ATT_BLK), 0)
    ki = lax.broadcasted_iota(jnp.int32, (ATT_BLK, 2 * ATT_BLK), 1)
    steps = qi + ATT_BLK - ki
    return (steps >= 0) & (steps <= ATT_BLK) & (ki >= first_key)


ATT_STATW = ATT_HPG * 128


def _stat(h):
    return slice(h * 128, (h + 1) * 128)


def _stat_rows(stat):
    n = stat.shape[0]
    heads = [stat[:, _stat(h)].T[0:1, :] for h in range(ATT_HPG)]
    return jnp.concatenate(heads + [jnp.zeros((8 - ATT_HPG, n), stat.dtype)], axis=0)


def _attn_specs(nb, width=ATT_GROUPW):
    cur = pl.BlockSpec((ATT_ROWS, width), lambda b: (b, 0))
    prev = pl.BlockSpec((ATT_BLK, width), lambda b: (jnp.maximum(b * ATT_T - 1, 0), 0))
    nxt = pl.BlockSpec((ATT_BLK, width), lambda b: (jnp.minimum((b + 1) * ATT_T, nb - 1), 0))
    return cur, prev, nxt


def _attn_fwd(tag, per_seq, q, k, v):
    s = q.shape[0]
    nb = s // ATT_BLK

    def body(q_ref, kc_ref, kp_ref, vc_ref, vp_ref, o_ref, lse_ref):
        bt = pl.program_id(0)
        for i in range(ATT_T):
            has_prev = lax.rem(bt * ATT_T + i, per_seq) > 0
            valid = _band_valid(jnp.where(has_prev, 0, ATT_BLK))
            rows = slice(i * ATT_BLK, (i + 1) * ATT_BLK)
            for h in range(ATT_HPG):
                sl = slice(h * ATT_HEAD_DIM, (h + 1) * ATT_HEAD_DIM)
                kcat = _window(kp_ref, kc_ref, i, sl)
                vcat = _window(vp_ref, vc_ref, i, sl)
                sc = _dot(q_ref[rows, sl], kcat, "nt") * ATT_SCALE
                sc = jnp.where(valid, sc, NEG_INF)
                m = jnp.max(sc, axis=-1, keepdims=True)
                p = jnp.exp(sc - m)
                den = jnp.sum(p, axis=-1, keepdims=True)
                o_ref[rows, sl] = _dot(p, vcat, "nn") / den
                lse_ref[rows, _stat(h)] = jnp.broadcast_to(m + jnp.log(den), (ATT_BLK, 128))

    cur, prev, _ = _attn_specs(nb)
    stat, _, _ = _attn_specs(nb, ATT_STATW)
    return pl.pallas_call(
        body, name="attn_fwd_" + tag, grid=(nb // ATT_T,), in_specs=[cur, cur, prev, cur, prev], out_specs=[cur, stat],
        out_shape=[jax.ShapeDtypeStruct((s, ATT_GROUPW), F32), jax.ShapeDtypeStruct((s, ATT_STATW), F32)],
        compiler_params=_cparams(1))(q, k, k, v, v)


def _attn_dq(tag, per_seq, q, k, v, do, lse, delta):
    s = q.shape[0]
    nb = s // ATT_BLK

    def body(q_ref, kc_ref, kp_ref, vc_ref, vp_ref, do_ref, lse_ref, dl_ref, dq_ref):
        bt = pl.program_id(0)
        for i in range(ATT_T):
            has_prev = lax.rem(bt * ATT_T + i, per_seq) > 0
            valid = _band_valid(jnp.where(has_prev, 0, ATT_BLK))
            rows = slice(i * ATT_BLK, (i + 1) * ATT_BLK)
            for h in range(ATT_HPG):
                sl = slice(h * ATT_HEAD_DIM, (h + 1) * ATT_HEAD_DIM)
                kcat = _window(kp_ref, kc_ref, i, sl)
                vcat = _window(vp_ref, vc_ref, i, sl)
                lse = jnp.concatenate([lse_ref[rows, _stat(h)]] * 2, axis=1)
                dlt = jnp.concatenate([dl_ref[rows, _stat(h)]] * 2, axis=1)
                sc = _dot(q_ref[rows, sl], kcat, "nt") * ATT_SCALE
                p = jnp.exp(jnp.where(valid, sc, NEG_INF) - lse)
                dp = _dot(do_ref[rows, sl], vcat, "nt")
                ds = p * (dp - dlt) * ATT_SCALE
                dq_ref[rows, sl] = _dot(ds, kcat, "nn")

    cur, prev, _ = _attn_specs(nb)
    stat, _, _ = _attn_specs(nb, ATT_STATW)
    return pl.pallas_call(
        body, name="attn_dq_" + tag, grid=(nb // ATT_T,), in_specs=[cur, cur, prev, cur, prev, cur, stat, stat],
        out_specs=cur, out_shape=jax.ShapeDtypeStruct((s, ATT_GROUPW), F32),
        compiler_params=_cparams(1))(q, k, k, v, v, do, lse, delta)


def _attn_dkv(tag, per_seq, q, k, v, do, lse_t, delta_t):
    s = q.shape[0]
    nb = s // ATT_BLK

    def body(k_ref, v_ref, qc_ref, qn_ref, doc_ref, don_ref, lc_ref, ln_ref, dc_ref, dn_ref, dk_ref, dv_ref):
        bt = pl.program_id(0)
        ki = lax.broadcasted_iota(jnp.int32, (ATT_BLK, 2 * ATT_BLK), 0)
        ci = lax.broadcasted_iota(jnp.int32, (ATT_BLK, 2 * ATT_BLK), 1)

        def pair(edge_ref, cur_ref, i, sl):
            if i == ATT_T - 1:
                return jnp.concatenate([cur_ref[i * ATT_BLK:(i + 1) * ATT_BLK, sl], edge_ref[:, sl]], axis=0)
            return cur_ref[i * ATT_BLK:(i + 2) * ATT_BLK, sl]

        def pair_row(edge_ref, cur_ref, i, h):
            if i == ATT_T - 1:
                row = jnp.concatenate([cur_ref[h:h + 1, i * ATT_BLK:(i + 1) * ATT_BLK], edge_ref[h:h + 1, :]], axis=1)
            else:
                row = cur_ref[h:h + 1, i * ATT_BLK:(i + 2) * ATT_BLK]
            return jnp.broadcast_to(row, (ATT_BLK, 2 * ATT_BLK))

        for i in range(ATT_T):
            b = bt * ATT_T + i
            next_uses = (b + 1 < nb) & (lax.rem(b + 1, per_seq) > 0)
            reach = jnp.where(next_uses, 0, 4 * ATT_BLK)
            valid = ((ci < ATT_BLK) & (ci >= ki)) | ((ci >= ATT_BLK) & (ki - ci + ATT_BLK >= reach))
            rows = slice(i * ATT_BLK, (i + 1) * ATT_BLK)
            for h in range(ATT_HPG):
                sl = slice(h * ATT_HEAD_DIM, (h + 1) * ATT_HEAD_DIM)
                qcat, docat = pair(qn_ref, qc_ref, i, sl), pair(don_ref, doc_ref, i, sl)
                sc = _dot(k_ref[rows, sl], qcat, "nt") * ATT_SCALE
                p = jnp.exp(jnp.where(valid, sc, NEG_INF) - pair_row(ln_ref, lc_ref, i, h))
                dv_ref[rows, sl] = _dot(p, docat, "nn")
                dp = _dot(v_ref[rows, sl], docat, "nt")
                ds = p * (dp - pair_row(dn_ref, dc_ref, i, h)) * ATT_SCALE
                dk_ref[rows, sl] = _dot(ds, qcat, "nn")

    cur, _, nxt = _attn_specs(nb)
    stat = pl.BlockSpec((8, ATT_ROWS), lambda b: (0, b))
    snxt = pl.BlockSpec((8, ATT_BLK), lambda b: (0, jnp.minimum((b + 1) * ATT_T, nb - 1)))
    return pl.pallas_call(
        body, name="attn_dkv_" + tag, grid=(nb // ATT_T,), in_specs=[cur, cur, cur, nxt, cur, nxt, stat, snxt, stat, snxt],
        out_specs=[cur, cur], out_shape=[jax.ShapeDtypeStruct((s, ATT_GROUPW), F32)] * 2,
        compiler_params=_cparams(1))(k, v, q, q, do, do, lse_t, lse_t, delta_t, delta_t)


def _xattn_probs(q, kh):
    sc = _dot(q, kh, "nt") * XATT_SCALE
    e = jnp.exp(sc - jnp.max(sc, axis=-1, keepdims=True))
    return e / jnp.sum(e, axis=-1, keepdims=True)


def _xattn_fwd(q, kv, tm=512):
    s = q.shape[0]
    tm = min(tm, s)

    def body(q_ref, kv_ref, o_ref):
        for h in range(XATT_HEADS):
            sl = slice(h * XATT_HEAD_DIM, (h + 1) * XATT_HEAD_DIM)
            vs = slice(D_MODEL + h * XATT_HEAD_DIM, D_MODEL + (h + 1) * XATT_HEAD_DIM)
            p = _xattn_probs(q_ref[:, sl], kv_ref[:, sl])
            o_ref[:, sl] = _dot(p, kv_ref[:, vs], "nn").astype(o_ref.dtype)

    return pl.pallas_call(
        body, name="xattn_fwd", grid=(s // tm,),
        in_specs=[pl.BlockSpec((tm, D_MODEL), lambda i: (i, 0)), pl.BlockSpec(kv.shape, lambda i: (0, 0))],
        out_specs=pl.BlockSpec((tm, D_MODEL), lambda i: (i, 0)),
        out_shape=jax.ShapeDtypeStruct((s, D_MODEL), MXU_DTYPE), compiler_params=_cparams(1))(q, kv)


def _xattn_bwd(q, kv, do, tm=512):
    s = q.shape[0]
    tm = min(tm, s)

    def body(q_ref, kv_ref, do_ref, dq_ref, dkv_ref):
        first = pl.program_id(0) == 0

        @pl.when(first)
        def _():
            dkv_ref[...] = jnp.zeros_like(dkv_ref)

        for h in range(XATT_HEADS):
            sl = slice(h * XATT_HEAD_DIM, (h + 1) * XATT_HEAD_DIM)
            vs = slice(D_MODEL + h * XATT_HEAD_DIM, D_MODEL + (h + 1) * XATT_HEAD_DIM)
            p = _xattn_probs(q_ref[:, sl], kv_ref[:, sl])
            dkv_ref[:, vs] += _dot(p, do_ref[:, sl], "tn")
            dp = _dot(do_ref[:, sl], kv_ref[:, vs], "nt")
            ds = p * (dp - jnp.sum(dp * p, axis=-1, keepdims=True)) * XATT_SCALE
            dq_ref[:, sl] = _dot(ds, kv_ref[:, sl], "nn").astype(dq_ref.dtype)
            dkv_ref[:, sl] += _dot(ds, q_ref[:, sl], "tn")

    row = pl.BlockSpec((tm, D_MODEL), lambda i: (i, 0))
    whole = pl.BlockSpec(kv.shape, lambda i: (0, 0))
    return pl.pallas_call(
        body, name="xattn_bwd", grid=(s // tm,), in_specs=[row, whole, row], out_specs=[row, whole],
        out_shape=[jax.ShapeDtypeStruct((s, D_MODEL), MXU_DTYPE), jax.ShapeDtypeStruct(kv.shape, F32)],
        compiler_params=_cparams(1))(q, kv, do)


def _ln(x, g, b):
    mu = jnp.mean(x, axis=-1, keepdims=True)
    xc = x - mu
    var = jnp.mean(jnp.square(xc), axis=-1, keepdims=True)
    return xc * lax.rsqrt(var + LN_EPS) * g + b


def _res_ln(h, o, g, b):
    return _ln(DEEPNORM_ALPHA * h + o, g, b)


def _gate(gs, ga, z1, z2, batt):
    return jax.nn.sigmoid(gs) * (z1 * jax.nn.sigmoid(z2)) + jax.nn.sigmoid(ga) * batt


def _rope_tables(pos, invf, m1, m2):
    ang = pos.astype(F32) * invf
    sin = jnp.sin(ang)
    return jnp.cos(ang), -sin * m1, sin * m2


def _rope(t, cos, s_up, s_dn):
    w = t.shape[-1]
    return t * cos + pltpu.roll(t, w - ROT_DIM // 2, 1) * s_up + pltpu.roll(t, ROT_DIM // 2, 1) * s_dn


def _rope_t(dt, cos, s_up, s_dn):
    w = dt.shape[-1]
    return dt * cos + pltpu.roll(dt * s_up, ROT_DIM // 2, 1) + pltpu.roll(dt * s_dn, w - ROT_DIM // 2, 1)


def _rope_consts():
    inv_freq = ROPE_THETA ** (-jnp.arange(0, ROT_DIM, 2, dtype=F32) / ROT_DIM)
    d = np.arange(ATT_GROUPW) % ATT_HEAD_DIM
    invf = jnp.where(d < ROT_DIM, inv_freq[d % (ROT_DIM // 2)], 0.0).reshape(1, ATT_GROUPW).astype(F32)
    m1 = jnp.asarray((d < ROT_DIM // 2).astype(np.float32)).reshape(1, ATT_GROUPW)
    m2 = jnp.asarray(((d >= ROT_DIM // 2) & (d < ROT_DIM)).astype(np.float32)).reshape(1, ATT_GROUPW)
    return invf, m1, m2


def _head_sum_matrix():
    d = np.arange(ATT_GROUPW) // ATT_HEAD_DIM
    s = np.arange(ATT_STATW) // 128
    return jnp.asarray((d[:, None] == s[None, :]).astype(np.float32))


def _adamw(w, g, m, v):
    m = ADAM_B1 * m + (1.0 - ADAM_B1) * g
    v = ADAM_B2 * v + (1.0 - ADAM_B2) * jnp.square(g)
    m_hat = m / (1.0 - ADAM_B1 ** ADAM_STEP)
    v_hat = v / (1.0 - ADAM_B2 ** ADAM_STEP)
    delta = -ADAM_LR * (m_hat / (jnp.sqrt(v_hat) + ADAM_EPS) + ADAM_WD * w)
    return delta, m, v


def _local_step(x, mem, pos, target, sp, ex):
    s = x.shape[0]
    al = DEEPNORM_ALPHA
    mx = MXU_DTYPE

    h0, h0b = _rowwise("ln_in", lambda x, g, b: (lambda h: (h, h))(_ln(x, g, b)), [x],
                       [sp["ln_in_g"], sp["ln_in_b"]], [(D_MODEL, F32), (D_MODEL, mx)],
                       carry=ex.gather_carry(["w_in"]))
    proj = _mm("proj", h0b, ex.weight("w_in"), "nn", bias=sp["b_in"],
               carry=ex.gather_carry(["w_glu", "w_att_up", "w_mix_out", "w_xq", "w_xkv"]))

    ldt = jnp.repeat(sp["ssm_log_dt"].reshape(SSM_GROUPS), SSM_STATE).reshape(N_STATE, 1)
    are, aim = sp["ssm_a_re"].reshape(N_STATE, 1), sp["ssm_a_im"].reshape(N_STATE, 1)
    bre, bim = sp["ssm_b_re"].reshape(N_STATE, SSM_GROUP), sp["ssm_b_im"].reshape(N_STATE, SSM_GROUP)
    abr, abi, bbr, bbi = _ssm_disc_fwd(ldt, are, aim, bre, bim)
    a_re, a_im = abr.reshape(1, N_STATE), abi.reshape(1, N_STATE)
    bexp = jnp.concatenate([_blockdiag_b(bbr), _blockdiag_b(bbi)], axis=2).astype(mx)
    cexp = jnp.concatenate([_blockdiag_c(sp["ssm_c_re"].reshape(SSM_GROUPS, SSM_GROUP, SSM_STATE)),
                            -_blockdiag_c(sp["ssm_c_im"].reshape(SSM_GROUPS, SSM_GROUP, SSM_STATE))],
                           axis=1).astype(mx)
    u_p = _time_perm(proj[:, :SSM_WIDTH])
    b12, c12 = _split_by_scan_block(bexp, 2), _split_by_scan_block(cexp, 1)
    h_re, h_im, y_p = _ssm_scan("ssm_scan_fwd", u_p, b12, c12, a_re, a_im, sp["ssm_d"], reverse=False,
                                carry=ex.gather_carry(["w_ff1", "w_ff2"]))
    y = _time_unperm(y_p)
    ygb, = _rowwise("gelu", lambda y: jax.nn.gelu(y), [y], [], [(SSM_WIDTH, mx)])
    z = _mm("glu", ygb, ex.weight("w_glu"), "nn", bias=sp["b_glu"], carry=ex.gather_carry(["w_xo"]))

    invf, m1, m2 = _rope_consts()

    def rope_fwd(pos, q0, q1, q2, k0, k1, k2, v0, v1, v2, invf, m1, m2):
        tabs = _rope_tables(pos, invf, m1, m2)
        return tuple(_rope(t, *tabs) for t in (q0, q1, q2, k0, k1, k2)) + (v0, v1, v2)

    qkv_cols = [(proj, ATT_GROUPW, 3 + i) for i in range(9)]
    qkv = _rowwise("rope", rope_fwd, [pos] + qkv_cols, [invf, m1, m2], [(ATT_GROUPW, mx)] * 9)
    n_blocks = s // ATT_BLK
    groups = [(str(g), n_blocks // d, d) for g, d in enumerate(DILATIONS)]
    q_d = [_dilate(qkv[g], d) for g, d in enumerate(DILATIONS)]
    k_d = [_dilate(qkv[3 + g], d) for g, d in enumerate(DILATIONS)]
    v_d = [_dilate(qkv[6 + g], d) for g, d in enumerate(DILATIONS)]
    o_g, l_g = [], []
    for g, (tag, per_seq, d) in enumerate(groups):
        o, lse = _attn_fwd(tag, per_seq, q_d[g], k_d[g], v_d[g])
        o_g.append(_undilate(o, d))
        l_g.append(_undilate(lse, d))

    def merge(o0, o1, o2, l0, l1, l2):
        m = jnp.maximum(jnp.maximum(l0, l1), l2)
        e0, e1, e2 = jnp.exp(l0 - m), jnp.exp(l1 - m), jnp.exp(l2 - m)
        tot = e0 + e1 + e2

        def per_dim(e):
            w = e / tot
            return jnp.concatenate([w[:, h * 128:h * 128 + ATT_HEAD_DIM] for h in range(ATT_HPG)], axis=1)

        att = per_dim(e0) * o0 + per_dim(e1) * o1 + per_dim(e2) * o2
        lse = m + jnp.log(tot)
        return att, att, lse, _stat_rows(lse)

    att, attb, lse_tot, lse_tot_t = _rowwise("attn_merge", merge, o_g + l_g, [],
                                             [(ATT_GROUPW, F32), (ATT_GROUPW, mx), (ATT_STATW, F32)], touts=[(8, F32)])
    batt = _mm("att_up", attb, ex.weight("w_att_up"), "nn")

    gate_rows = [(proj, D_MODEL, 3), (proj, D_MODEL, 4), (z, D_MODEL, 0), (z, D_MODEL, 1), batt]
    mixedb, = _rowwise("gate", _gate, gate_rows, [], [(D_MODEL, mx)])
    o1 = _mm("mix_out", mixedb, ex.weight("w_mix_out"), "nn", bias=sp["b_mix_out"])
    h1, h1b = _rowwise("ln1", lambda h, o, g, b: (lambda r: (r, r))(_res_ln(h, o, g, b)), [h0, o1],
                       [sp["ln1_g"], sp["ln1_b"]], [(D_MODEL, F32), (D_MODEL, mx)])

    qx = _mm("xq", h1b, ex.weight("w_xq"), "nn", out_dtypes=(mx,))
    kvx = _mm("xkv", mem, ex.weight("w_xkv"), "nn", out_dtypes=(mx,))
    oxb = _xattn_fwd(qx, kvx)
    o2 = _mm("xo", oxb, ex.weight("w_xo"), "nn")
    h2, h2b = _rowwise("ln2", lambda h, o, g, b: (lambda r: (r, r))(_res_ln(h, o, g, b)), [h1, o2],
                       [sp["ln2_g"], sp["ln2_b"]], [(D_MODEL, F32), (D_MODEL, mx)])

    a_ff, fb = _mm("ff1", h2b, ex.weight("w_ff1"), "nn", bias=sp["b_ff1"],
                   epilogue=lambda r: (r, jnp.square(jnp.maximum(r, 0.0))), out_dtypes=(F32, mx))
    o3 = _mm("ff2", fb, ex.weight("w_ff2"), "nn", bias=sp["b_ff2"])

    def loss_bwd(h2, o3, tgt, g, b):
        def f(h2, o3, g, b):
            h3 = _res_ln(h2, o3, g, b)
            return 0.5 * jnp.sum(jnp.mean(jnp.square(h3 - tgt), axis=-1))

        loss, vjp = jax.vjp(f, h2, o3, g, b)
        _, dr, dg, db = vjp(jnp.ones((), F32))
        return dr, dr, dg, db, _colsum(dr), jnp.full((1, 128), loss, F32)

    dr3, dr3b, g_ln3_g, g_ln3_b, g_b_ff2, loss = _rowwise(
        "loss_ln3_bwd", loss_bwd, [h2, o3, target], [sp["ln3_g"], sp["ln3_b"]],
        [(D_MODEL, F32), (D_MODEL, mx)], [D_MODEL, D_MODEL, D_MODEL, 128])

    dab = _mm("ff2_dx", dr3b, ex.weight("w_ff2"), "nt", extras=(a_ff,),
              epilogue=lambda r, a: (r * (2.0 * jnp.maximum(a, 0.0)),), out_dtypes=(mx,))
    ex.grad("w_ff2", _mm("ff2_dw", fb, dr3b, "tn"))
    g_b_ff1, = _rowwise("ff1_db", lambda v: (_colsum(v),), [dab], [], [], [D_FF])
    ex.grad("w_ff1", _mm("ff1_dw", h2b, dab, "tn", carry=ex.carry(swap=["w_ff2"])))
    dh2 = _mm("ff1_dx", dab, ex.weight("w_ff1"), "nt", extras=(dr3,), epilogue=lambda r, d: (r + al * d,),
              carry=ex.carry(swap=["w_ff1"], ici=["w_ff2"]))

    def ln_bwd(h, o, dout, g, b):
        _, vjp = jax.vjp(_res_ln, h, o, g, b)
        _, dr, dg, db = vjp(dout)
        return dr, dr, dg, db, _colsum(dr)

    dr2, dr2b, g_ln2_g, g_ln2_b, _ = _rowwise(
        "ln2_bwd", ln_bwd, [h1, o2, dh2], [sp["ln2_g"], sp["ln2_b"]],
        [(D_MODEL, F32), (D_MODEL, mx)], [D_MODEL, D_MODEL, D_MODEL])
    ex.grad("w_xo", _mm("xo_dw", oxb, dr2b, "tn"))
    doxb = _mm("xo_dx", dr2b, ex.weight("w_xo"), "nt", out_dtypes=(mx,), carry=ex.carry(swap=["w_xo"]))
    dqxb, dkvx = _xattn_bwd(qx, kvx, doxb)
    ex.grad("w_xq", _mm("xq_dw", h1b, dqxb, "tn", carry=ex.carry(ici=["w_xo"])))
    dh1 = _mm("xq_dx", dqxb, ex.weight("w_xq"), "nt", extras=(dr2,), epilogue=lambda r, d: (r + al * d,),
              carry=ex.carry(swap=["w_xq"]))
    ex.grad("w_xkv", _mm("xkv_dw", mem, dkvx, "tn"))

    dr1, dr1b, g_ln1_g, g_ln1_b, g_b_mix = _rowwise(
        "ln1_bwd", ln_bwd, [h0, o1, dh1], [sp["ln1_g"], sp["ln1_b"]],
        [(D_MODEL, F32), (D_MODEL, mx)], [D_MODEL, D_MODEL, D_MODEL])
    ex.grad("w_mix_out", _mm("mix_dw", mixedb, dr1b, "tn", carry=ex.carry(swap=["w_xkv"], ici=["w_xq"])))
    dmixed = _mm("mix_dx", dr1b, ex.weight("w_mix_out"), "nt", carry=ex.carry(swap=["w_mix_out"]))

    def gate_bwd(gs, ga, z1, z2, batt, dm):
        _, vjp = jax.vjp(_gate, gs, ga, z1, z2, batt)
        dgs, dga, dz1, dz2, dbatt = vjp(dm)
        dz = jnp.concatenate([dz1, dz2], axis=-1)
        return dgs, dga, dz, dbatt, _colsum(dz)

    dgsb, dgab, dzb, dbattb, g_b_glu = _rowwise(
        "gate_bwd", gate_bwd, gate_rows + [dmixed], [],
        [(D_MODEL, mx), (D_MODEL, mx), (2 * D_MODEL, mx), (D_MODEL, mx)], [2 * D_MODEL])
    ex.grad("w_att_up", _mm("att_up_dw", attb, dbattb, "tn", carry=ex.carry(ici=["w_mix_out"])))
    datt = _mm("att_up_dx", dbattb, ex.weight("w_att_up"), "nt", carry=ex.carry(swap=["w_att_up"]))

    def att_delta(datt, att, hs):
        dl = jnp.dot(datt * att, hs, precision=lax.Precision.HIGHEST, preferred_element_type=F32)
        return datt, dl, _stat_rows(dl)

    dattb, delta, delta_t = _rowwise("attn_delta", att_delta, [datt, att], [_head_sum_matrix()],
                                     [(ATT_GROUPW, mx), (ATT_STATW, F32)], touts=[(8, F32)])
    dq_g, dk_g, dv_g = [], [], []
    for g, (tag, per_seq, d) in enumerate(groups):
        do_d, lt_d, dl_d = _dilate(dattb, d), _dilate(lse_tot, d), _dilate(delta, d)
        dq_g.append(_undilate(_attn_dq(tag, per_seq, q_d[g], k_d[g], v_d[g], do_d, lt_d, dl_d), d))
        dk, dv = _attn_dkv(tag, per_seq, q_d[g], k_d[g], v_d[g], do_d, _dilate_rows(lse_tot_t, d), _dilate_rows(delta_t, d))
        dk_g.append(_undilate(dk, d))
        dv_g.append(_undilate(dv, d))
    dqkv = dq_g + dk_g + dv_g

    def rope_bwd(pos, q0, q1, q2, k0, k1, k2, v0, v1, v2, invf, m1, m2):
        tabs = _rope_tables(pos, invf, m1, m2)
        return jnp.concatenate([_rope_t(t, *tabs) for t in (q0, q1, q2, k0, k1, k2)] + [v0, v1, v2], axis=-1)

    dqkvb, = _rowwise("rope_bwd", rope_bwd, [pos] + dqkv, [invf, m1, m2], [(9 * ATT_GROUPW, mx)])

    ex.grad("w_glu", _mm("glu_dw", ygb, dzb, "tn", carry=ex.carry(ici=["w_xkv", "w_att_up"])))
    dyg = _mm("glu_dx", dzb, ex.weight("w_glu"), "nt", carry=ex.carry(swap=["w_glu"]))

    def gelu_bwd(y, dyg):
        _, vjp = jax.vjp(jax.nn.gelu, y)
        return vjp(dyg)[0]

    dy, = _rowwise("gelu_bwd", gelu_bwd, [y, dyg], [], [(SSM_WIDTH, F32)])
    dy_p = _time_perm(dy)
    s_re, s_im, du_p = _ssm_scan("ssm_scan_bwd", dy_p, c12, b12, a_re, a_im, sp["ssm_d"], reverse=True,
                                 carry=ex.carry(ici=["w_ff1", "w_glu"]))
    g_bexp, g_cexp, d_abr, d_abi = _ssm_wgrads(u_p, dy_p, s_re, s_im, h_re, h_im)
    g_ssm_d, = _rowwise("ssm_dd", lambda a, b: (_colsum(a * b),), [dy_p, u_p], [], [], [SSM_WIDTH])
    g_ldt, g_are, g_aim, g_bre, g_bim = _ssm_disc_bwd(
        ldt, are, aim, bre, bim, d_abr.reshape(N_STATE, 1), d_abi.reshape(N_STATE, 1),
        _diag_of_b(g_bexp[:, :, :CH_N]), _diag_of_b(g_bexp[:, :, CH_N:]))
    g_c_re = _diag_of_c(g_cexp[:, :CH_N, :])
    g_c_im = -_diag_of_c(g_cexp[:, CH_N:, :])
    dub = _time_unperm(du_p).astype(mx)

    dprojb = jnp.concatenate([dub, dqkvb, dgsb, dgab], axis=-1)
    g_b_in, = _rowwise("in_db", lambda v: (_colsum(v),), [dprojb], [], [], [IN_COLS])
    ex.grad("w_in", _mm("in_dw", h0b, dprojb, "tn"))
    dh0 = _mm("in_dx", dprojb, ex.weight("w_in"), "nt", extras=(dr1,), epilogue=lambda r, d: (r + al * d,),
              carry=ex.carry(ici=["w_in"]))

    def ln_in_bwd(x, dout, g, b):
        _, vjp = jax.vjp(_ln, x, g, b)
        return vjp(dout)

    dx, g_ln_in_g, g_ln_in_b = _rowwise("ln_in_bwd", ln_in_bwd, [x, dh0], [sp["ln_in_g"], sp["ln_in_b"]],
                                        [(D_MODEL, F32)], [D_MODEL, D_MODEL])

    small = {"ln_in_g": g_ln_in_g, "ln_in_b": g_ln_in_b, "b_in": g_b_in, "ssm_log_dt": g_ldt, "ssm_a_re": g_are,
             "ssm_a_im": g_aim, "ssm_b_re": g_bre, "ssm_b_im": g_bim, "ssm_c_re": g_c_re, "ssm_c_im": g_c_im,
             "ssm_d": g_ssm_d, "b_glu": g_b_glu, "b_mix_out": g_b_mix, "ln1_g": g_ln1_g, "ln1_b": g_ln1_b,
             "ln2_g": g_ln2_g, "ln2_b": g_ln2_b, "b_ff1": g_b_ff1, "b_ff2": g_b_ff2, "ln3_g": g_ln3_g,
             "ln3_b": g_ln3_b}
    return loss, dx, small


def _piece_shape(k, n, axis):
    return (k // 2, n // 4) if axis == 1 else (k // 8, n)


def _aligned(v, m):
    return v if isinstance(v, int) else pl.multiple_of(v, m)


def _full_piece(ref, k, n, axis, chip, half):
    pr, pc = _piece_shape(k, n, axis)
    if axis == 1:
        return ref.at[pl.ds(_aligned(half * pr, 8), pr), pl.ds(_aligned(chip * pc, 128), pc)]
    return ref.at[pl.ds(_aligned(chip * (2 * pr) + half * pr, 8), pr), :]


def _full_shard(ref, k, n, axis, chip):
    if axis == 1:
        return ref.at[:, pl.ds(_aligned(chip * (n // 4), 128), n // 4)]
    return ref.at[pl.ds(_aligned(chip * (k // 4), 8), k // 4), :]


def _shard_piece(ref, k, n, axis, half):
    pr, _ = _piece_shape(k, n, axis)
    return ref.at[pl.ds(_aligned(half * pr, 8), pr), :]


def _mesh_pos():
    x, y, c = lax.axis_index("x"), lax.axis_index("y"), lax.axis_index("c")
    other_chips = [(1 - x, y), (x, 1 - y), (1 - x, 1 - y)]
    return x, y, c, other_chips


def _remote(src, dst, send_sem, recv_sem, dev):
    return pltpu.make_async_remote_copy(src_ref=src, dst_ref=dst, send_sem=send_sem, recv_sem=recv_sem,
                                        device_id=dev, device_id_type=MESH)


def _placed(name, fn, n_steps, where, ins, out_sds, out_block, out_index):
    def body(w_ref, *refs):
        o_ref = refs[-1]
        o_ref[...] = fn(*[r[...] for r in refs[:-1]]).astype(o_ref.dtype)

    grid_spec = pltpu.PrefetchScalarGridSpec(
        num_scalar_prefetch=1, grid=(n_steps,), in_specs=[pl.BlockSpec(bs, idx) for _, bs, idx in ins],
        out_specs=pl.BlockSpec(out_block, out_index))
    return pl.pallas_call(body, name=name, grid_spec=grid_spec, out_shape=out_sds,
                          compiler_params=_cparams(1))(where, *[a for a, _, _ in ins])


def _gather_copies(widx):
    geo = [BIG[i][1:] for i in widx]

    def ici(full, wi, j, chip, send_sems, recv_sems, c, dev):
        k, n, ax = geo[wi]
        piece = _full_piece(full[wi], k, n, ax, chip, c)
        return _remote(piece, piece, send_sems.at[wi * 6 + j], recv_sems.at[wi * 6 + j], dev)

    def d2d(full, wi, j, chip, half, send_sems, recv_sems, sib):
        k, n, ax = geo[wi]
        piece = _full_piece(full[wi], k, n, ax, chip, half)
        return _remote(piece, piece, send_sems.at[wi * 6 + 3 + j], recv_sems.at[wi * 6 + 3 + j], sib)

    def start(_, full, send_sems, recv_sems):
        x, y, c, chips = _mesh_pos()
        for wi in range(len(geo)):
            for j, (qx, qy) in enumerate(chips):
                ici(full, wi, j, 2 * x + y, send_sems, recv_sems, c, (qx, qy, c)).start()

    def finish(_, full, send_sems, recv_sems):
        x, y, c, chips = _mesh_pos()
        sib = (x, y, 1 - c)
        for wi in range(len(geo)):
            for j, (qx, qy) in enumerate(chips):
                ici(full, wi, j, 2 * qx + qy, send_sems, recv_sems, c, (qx, qy, c)).wait_recv()
                d2d(full, wi, j, 2 * qx + qy, c, send_sems, recv_sems, sib).start()
        for wi in range(len(geo)):
            for j, (qx, qy) in enumerate(chips):
                d2d(full, wi, j, 2 * qx + qy, 1 - c, send_sems, recv_sems, sib).wait_recv()
        for wi in range(len(geo)):
            for j, (qx, qy) in enumerate(chips):
                ici(full, wi, j, 2 * x + y, send_sems, recv_sems, c, (qx, qy, c)).wait_send()
                d2d(full, wi, j, 2 * qx + qy, c, send_sems, recv_sems, sib).wait_send()

    return start, finish, 6 * len(geo)


def _gather_weights(tag, fulls, widx):
    nw = len(widx)
    start, finish, n_sems = _gather_copies(widx)

    def body(*refs):
        full = refs[nw:2 * nw]
        start(None, full, *refs[2 * nw:])
        finish(None, full, *refs[2 * nw:])

    return pl.pallas_call(
        body, name="gather_weights_" + tag, in_specs=[HBM_SPEC] * nw, out_specs=[HBM_SPEC] * nw,
        out_shape=[jax.ShapeDtypeStruct(f.shape, f.dtype) for f in fulls],
        input_output_aliases={i: i for i in range(nw)},
        scratch_shapes=[pltpu.SemaphoreType.DMA((n_sems,)), pltpu.SemaphoreType.DMA((n_sems,))])(*fulls)


def _swap_copies(widx):
    geo = [BIG[i][1:] for i in widx]

    def copies(g, got, send_sems, recv_sems, base):
        x, y, c, _ = _mesh_pos()
        return [_remote(_full_piece(g[wi], k, n, ax, q, 1 - c), got[wi].at[q], send_sems.at[base + wi * 4 + q],
                        recv_sems.at[base + wi * 4 + q], (x, y, 1 - c))
                for wi, (k, n, ax) in enumerate(geo) for q in range(4)]

    def start(g, got, send_sems, recv_sems, base=0):
        for cp in copies(g, got, send_sems, recv_sems, base):
            cp.start()

    def finish(g, got, send_sems, recv_sems, base=0):
        for cp in copies(g, got, send_sems, recv_sems, base):
            cp.wait()

    return start, finish, 4 * len(geo)


def _swap_shapes(widx):
    return [jax.ShapeDtypeStruct((4,) + _piece_shape(*BIG[i][1:]), F32) for i in widx]


def _reduce_swap_halves(tag, grads, widx):
    nw = len(widx)
    start, finish, n_sems = _swap_copies(widx)

    def body(*refs):
        start(refs[:nw], refs[nw:2 * nw], *refs[2 * nw:])
        finish(refs[:nw], refs[nw:2 * nw], *refs[2 * nw:])

    return pl.pallas_call(
        body, name="reduce_swap_halves_" + tag, in_specs=[HBM_SPEC] * nw, out_specs=[HBM_SPEC] * nw,
        out_shape=_swap_shapes(widx),
        scratch_shapes=[pltpu.SemaphoreType.DMA((n_sems,)), pltpu.SemaphoreType.DMA((n_sems,))])(*grads)


def _owner_copies(nw):
    def copies(p, out, send_sems, recv_sems, base):
        x, y, c, chips = _mesh_pos()
        return [_remote(p[wi].at[2 * qx + qy], out[wi].at[j], send_sems.at[base + wi * 3 + j],
                        recv_sems.at[base + wi * 3 + j], (qx, qy, c))
                for wi in range(nw) for j, (qx, qy) in enumerate(chips)]

    def start(p, out, send_sems, recv_sems, base=0):
        for cp in copies(p, out, send_sems, recv_sems, base):
            cp.start()

    def finish(p, out, send_sems, recv_sems, base=0):
        for cp in copies(p, out, send_sems, recv_sems, base):
            cp.wait()

    return start, finish, 3 * nw


def _join_carries(a, b):
    if a is None or b is None:
        return a if b is None else b
    n_i, n_o = len(a.ins), len(a.outs)
    outs = list(a.outs) + [o + n_i if isinstance(o, int) else o for o in b.outs]

    def start(c_in, c_out, send_sems, recv_sems):
        a.start(c_in[:n_i], c_out[:n_o], send_sems, recv_sems)
        b.start(c_in[n_i:], c_out[n_o:], send_sems, recv_sems, base=a.n_sems)

    def finish(c_in, c_out, send_sems, recv_sems):
        a.finish(c_in[:n_i], c_out[:n_o], send_sems, recv_sems)
        b.finish(c_in[n_i:], c_out[n_o:], send_sems, recv_sems, base=a.n_sems)

    def done(res):
        a.done(res[:n_o])
        b.done(res[n_o:])

    return _Carry(a.ins + b.ins, outs, a.n_sems + b.n_sems, start, finish, done)


def _share_with_sibling(shards):
    nw = len(BIG)

    def body(*refs):
        out = refs[nw:2 * nw]
        send_sems, recv_sems = refs[2 * nw:]
        x, y, c, _ = _mesh_pos()
        sib = (x, y, 1 - c)
        cps = []
        for wi, (_, k, n, ax) in enumerate(BIG):
            mine = _shard_piece(out[wi], k, n, ax, c)
            cp = _remote(mine, mine, send_sems.at[wi], recv_sems.at[wi], sib)
            cp.start()
            cps.append(cp)
        for wi, (_, k, n, ax) in enumerate(BIG):
            piece = _shard_piece(out[wi], k, n, ax, 1 - c)
            _remote(piece, piece, send_sems.at[wi], recv_sems.at[wi], sib).wait_recv()
        for cp in cps:
            cp.wait_send()

    return pl.pallas_call(
        body, name="share_with_sibling", in_specs=[HBM_SPEC] * nw, out_specs=[HBM_SPEC] * nw,
        out_shape=[jax.ShapeDtypeStruct(sh.shape, sh.dtype) for sh in shards],
        input_output_aliases={i: i for i in range(nw)},
        scratch_shapes=[pltpu.SemaphoreType.DMA((nw,)), pltpu.SemaphoreType.DMA((nw,))])(*shards)


def _allreduce_small(v):
    r = v.shape[0]
    rh = r // 2
    assert rh % 8 == 0

    def body(v_ref, o_ref, sib_buf, chip_buf, send_sems, recv_sems):
        x, y, c, chips = _mesh_pos()
        me = 2 * x + y
        sib = (x, y, 1 - c)
        mine = pl.ds(pl.multiple_of(c * rh, 8), rh)
        other = pl.ds(pl.multiple_of((1 - c) * rh, 8), rh)
        swap = _remote(v_ref.at[other], sib_buf, send_sems.at[0], recv_sems.at[0], sib)
        swap.start()
        swap.wait()
        chip_buf[me] = v_ref[mine, :] + sib_buf[...]
        cps = []
        for j, (qx, qy) in enumerate(chips):
            cp = _remote(chip_buf.at[me], chip_buf.at[me], send_sems.at[1 + j], recv_sems.at[1 + j], (qx, qy, c))
            cp.start()
            cps.append(cp)
        for j, (qx, qy) in enumerate(chips):
            slot = chip_buf.at[2 * qx + qy]
            _remote(slot, slot, send_sems.at[1 + j], recv_sems.at[1 + j], (qx, qy, c)).wait_recv()
        for cp in cps:
            cp.wait_send()
        o_ref[mine, :] = ((chip_buf[0] + chip_buf[1]) + chip_buf[2]) + chip_buf[3]
        back = _remote(o_ref.at[mine], o_ref.at[mine], send_sems.at[4], recv_sems.at[4], sib)
        back.start()
        _remote(o_ref.at[other], o_ref.at[other], send_sems.at[4], recv_sems.at[4], sib).wait_recv()
        back.wait_send()

    return pl.pallas_call(
        body, name="allreduce_small", in_specs=[VMEM_SPEC], out_specs=VMEM_SPEC,
        out_shape=jax.ShapeDtypeStruct((r, 128), F32),
        scratch_shapes=[pltpu.VMEM((rh, 128), F32), pltpu.VMEM((4, rh, 128), F32),
                        pltpu.SemaphoreType.DMA((5,)), pltpu.SemaphoreType.DMA((5,))],
        compiler_params=pltpu.CompilerParams(vmem_limit_bytes=VMEM_LIMIT))(v)


def _as2d(a):
    a = a.reshape((-1, a.shape[-1])) if a.ndim > 1 else a.reshape(1, -1)
    return a


def _adamw_small(quads):
    n = len(quads)

    def body(*refs):
        for i in range(n):
            w, g, m, v = (r[...] for r in refs[4 * i:4 * i + 4])
            for ref, val in zip(refs[4 * n + 3 * i:4 * n + 3 * i + 3], _adamw(w, g, m, v)):
                ref[...] = val

    return pl.pallas_call(
        body, name="adamw_small", in_specs=[VMEM_SPEC] * (4 * n), out_specs=[VMEM_SPEC] * (3 * n),
        out_shape=[jax.ShapeDtypeStruct(q[0].shape, F32) for q in quads for _ in range(3)],
        compiler_params=pltpu.CompilerParams(vmem_limit_bytes=VMEM_LIMIT))(*[a for q in quads for a in q])


def _where():
    return jnp.stack([2 * lax.axis_index("x") + lax.axis_index("y"), lax.axis_index("c")]).astype(jnp.int32)


_BIG_INDEX = {name: i for i, (name, _, _, _) in enumerate(BIG)}


class _LocalWeights:
    def __init__(self, weights):
        self.weights, self.grads = weights, {}

    def gather_now(self, names):
        pass

    def gather_carry(self, names):
        return None

    def weight(self, name):
        return self.weights[name]

    def grad(self, name, g):
        self.grads[name] = g

    def carry(self, swap=(), ici=()):
        return None


class _Exchange:
    def __init__(self, inputs, where):
        self.inputs, self.where = inputs, where
        self.full, self.ready = {}, set()
        self.raw, self.got, self.parts, self.landed, self.geom = {}, {}, {}, {}, {}
        for name, k, n, ax in BIG:
            w2 = inputs[name][0]
            rs, cs = w2.shape
            tm = _tile(rs, 512)
            steps = rs // tm
            if ax == 1:
                blk, idx = (tm, cs), lambda i, w: (i, w[0])
            else:
                blk, idx = (tm, n), functools.partial(lambda i, w, steps: (w[0] * steps + i, 0), steps=steps)
            self.full[name] = _placed("cast_" + name, lambda w: w, steps, where, [(w2, (tm, cs), lambda i, w: (i, 0))],
                                      jax.ShapeDtypeStruct((k, n), MXU_DTYPE), blk, idx)

    def _gathered(self, names, outs):
        for name, o in zip(names, outs):
            self.full[name] = o
            self.ready.add(name)

    def gather_now(self, names):
        self._gathered(names, _gather_weights(names[0], [self.full[n] for n in names], [_BIG_INDEX[n] for n in names]))

    def gather_carry(self, names):
        start, finish, n_sems = _gather_copies([_BIG_INDEX[n] for n in names])
        return _Carry([self.full[n] for n in names], list(range(len(names))), n_sems, start, finish,
                      functools.partial(self._gathered, names))

    def weight(self, name):
        assert name in self.ready, name
        return self.full[name]

    def grad(self, name, g):
        self.raw[name] = g

    def _swapped(self, names, outs):
        for name, o in zip(names, outs):
            self.got[name] = o

    def _pair_sum(self, name):
        i = _BIG_INDEX[name]
        _, k, n, ax = BIG[i]
        g = self.raw[name]
        if name not in self.got:
            self._swapped([name], _reduce_swap_halves(name, [g], [i]))
        got = self.got[name]
        pr, pc = _piece_shape(k, n, ax)
        tm = _tile(pr, 512)
        spp = pr // tm
        self.geom[name] = (pr, pc, tm, spp)
        if ax == 1:
            g_idx = functools.partial(lambda i, w, spp: (w[1] * spp + i % spp, i // spp), spp=spp)
        else:
            g_idx = functools.partial(lambda i, w, spp: ((i // spp) * 2 * spp + w[1] * spp + i % spp, 0), spp=spp)
        self.parts[name] = _placed(
            "pair_sum_" + name, lambda a, b: a + b, 4 * spp, self.where,
            [(g, (tm, pc), g_idx), (got.reshape(4 * pr, pc), (tm, pc), lambda i, w: (i, 0))],
            jax.ShapeDtypeStruct((4 * pr, pc), BF16), (tm, pc), lambda i, w: (i, 0)).reshape(4, pr, pc)

    def _landed(self, names, outs):
        for name, o in zip(names, outs):
            self.landed[name] = o

    def carry(self, swap=(), ici=()):
        first = second = None
        if swap:
            widx = [_BIG_INDEX[n] for n in swap]
            start, finish, n_sems = _swap_copies(widx)
            first = _Carry([self.raw[n] for n in swap], _swap_shapes(widx), n_sems, start, finish,
                           functools.partial(self._swapped, list(swap)))
        if ici:
            for n in ici:
                self._pair_sum(n)
            start, finish, n_sems = _owner_copies(len(ici))
            parts = [self.parts[n] for n in ici]
            outs = [jax.ShapeDtypeStruct((3,) + p.shape[1:], p.dtype) for p in parts]
            second = _Carry(parts, outs, n_sems, start, finish, functools.partial(self._landed, list(ici)))
        return _join_carries(first, second)

    def finish(self):
        halves = []
        for name, _, _, _ in BIG:
            pr, pc, tm, spp = self.geom[name]
            ins = [(self.parts[name], (None, tm, pc), lambda i, w: (w[0], i, 0))]
            ins += [(self.landed[name], (None, tm, pc), functools.partial(lambda i, w, j: (j, i, 0), j=j))
                    for j in range(3)]
            halves.append(_placed("chip_sum_" + name,
                                  lambda a, b, c, d: ((a.astype(F32) + b.astype(F32)) + c.astype(F32)) + d.astype(F32),
                                  spp, self.where, ins, jax.ShapeDtypeStruct(self.inputs[name].shape[1:], F32), (tm, pc),
                                  functools.partial(lambda i, w, spp: (w[1] * spp + i, 0), spp=spp)))
        return dict(zip([b[0] for b in BIG], _share_with_sibling(halves)))


def _step(inputs):
    x, mem, positions, target = inputs["x"][0], inputs["mem"][0], inputs["positions"], inputs["loss_target"][0]
    pos = positions.reshape(-1, 1)
    ex = _Exchange(inputs, _where())
    sp = {name: _as2d(inputs[name]) for name in SMALL}
    memb, = _rowwise("cast_mem", lambda m: (m,), [mem], [], [(D_MODEL, MXU_DTYPE)])

    loss, dx, gsmall = _local_step(x, memb, pos, target, sp, ex)
    gshard = ex.finish()

    out = {}
    for name, _, _, _ in BIG:
        w2, m2, v2 = inputs[name][0], inputs["m_" + name][0], inputs["v_" + name][0]
        n = w2.shape[1]
        d, nm, nv = _rowwise("adamw_" + name, _adamw, [w2, gshard[name], m2, v2], [], [(n, F32)] * 3, tm=256)
        lead = inputs[name].shape
        out[name] = (gshard[name].reshape(lead), d.reshape(lead), nm.reshape(lead), nv.reshape(lead))

    def tiles(a):
        flat = a.reshape(-1)
        n = -(-flat.shape[0] // 1024) * 1024
        return jnp.pad(flat, (0, n - flat.shape[0])).reshape(n // 128, 128)

    pieces = [tiles(loss[:, :1])] + [tiles(gsmall[name]) for name in SMALL]
    if sum(p.shape[0] for p in pieces) % 16:
        pieces.append(jnp.zeros((8, 128), F32))
    red = _allreduce_small(jnp.concatenate(pieces, axis=0))
    loss_total = red[0, 0]
    grads, off = {}, pieces[0].shape[0]
    for name, p in zip(SMALL, pieces[1:]):
        shp = _as2d(inputs[name]).shape
        grads[name] = red[off:off + p.shape[0]].reshape(-1)[:shp[0] * shp[1]].reshape(shp)
        off += p.shape[0]
    upd = _adamw_small([(_as2d(inputs[n]), grads[n], _as2d(inputs["m_" + n]), _as2d(inputs["v_" + n])) for n in SMALL])
    for i, name in enumerate(SMALL):
        shp = inputs[name].shape
        out[name] = (grads[name].reshape(shp),) + tuple(t.reshape(shp) for t in upd[3 * i:3 * i + 3])
    return loss_total, dx.reshape(inputs["x"].shape), out


_ARG_NAMES = (("x", "mem", "positions") + WEIGHT_ORDER + ("loss_target",) + tuple("m_" + n for n in WEIGHT_ORDER)
              + tuple("v_" + n for n in WEIGHT_ORDER))


def kernel(x, mem, positions, ln_in_g, ln_in_b, w_in, b_in, ssm_log_dt, ssm_a_re, ssm_a_im, ssm_b_re, ssm_b_im, ssm_c_re, ssm_c_im, ssm_d, w_glu, b_glu, w_att_up, w_mix_out, b_mix_out, ln1_g, ln1_b, w_xq, w_xkv, w_xo, ln2_g, ln2_b, w_ff1, b_ff1, w_ff2, b_ff2, ln3_g, ln3_b, loss_target, m_ln_in_g, m_ln_in_b, m_w_in, m_b_in, m_ssm_log_dt, m_ssm_a_re, m_ssm_a_im, m_ssm_b_re, m_ssm_b_im, m_ssm_c_re, m_ssm_c_im, m_ssm_d, m_w_glu, m_b_glu, m_w_att_up, m_w_mix_out, m_b_mix_out, m_ln1_g, m_ln1_b, m_w_xq, m_w_xkv, m_w_xo, m_ln2_g, m_ln2_b, m_w_ff1, m_b_ff1, m_w_ff2, m_b_ff2, m_ln3_g, m_ln3_b, v_ln_in_g, v_ln_in_b, v_w_in, v_b_in, v_ssm_log_dt, v_ssm_a_re, v_ssm_a_im, v_ssm_b_re, v_ssm_b_im, v_ssm_c_re, v_ssm_c_im, v_ssm_d, v_w_glu, v_b_glu, v_w_att_up, v_w_mix_out, v_b_mix_out, v_ln1_g, v_ln1_b, v_w_xq, v_w_xkv, v_w_xo, v_ln2_g, v_ln2_b, v_w_ff1, v_b_ff1, v_w_ff2, v_b_ff2, v_ln3_g, v_ln3_b):
    args = (x, mem, positions, ln_in_g, ln_in_b, w_in, b_in, ssm_log_dt, ssm_a_re, ssm_a_im, ssm_b_re, ssm_b_im, ssm_c_re, ssm_c_im, ssm_d, w_glu, b_glu, w_att_up, w_mix_out, b_mix_out, ln1_g, ln1_b, w_xq, w_xkv, w_xo, ln2_g, ln2_b, w_ff1, b_ff1, w_ff2, b_ff2, ln3_g, ln3_b, loss_target, m_ln_in_g, m_ln_in_b, m_w_in, m_b_in, m_ssm_log_dt, m_ssm_a_re, m_ssm_a_im, m_ssm_b_re, m_ssm_b_im, m_ssm_c_re, m_ssm_c_im, m_ssm_d, m_w_glu, m_b_glu, m_w_att_up, m_w_mix_out, m_b_mix_out, m_ln1_g, m_ln1_b, m_w_xq, m_w_xkv, m_w_xo, m_ln2_g, m_ln2_b, m_w_ff1, m_b_ff1, m_w_ff2, m_b_ff2, m_ln3_g, m_ln3_b, v_ln_in_g, v_ln_in_b, v_w_in, v_b_in, v_ssm_log_dt, v_ssm_a_re, v_ssm_a_im, v_ssm_b_re, v_ssm_b_im, v_ssm_c_re, v_ssm_c_im, v_ssm_d, v_w_glu, v_b_glu, v_w_att_up, v_w_mix_out, v_b_mix_out, v_ln1_g, v_ln1_b, v_w_xq, v_w_xkv, v_w_xo, v_ln2_g, v_ln2_b, v_w_ff1, v_b_ff1, v_w_ff2, v_b_ff2, v_ln3_g, v_ln3_b)
    assert len(args) == len(_ARG_NAMES)
    inputs = dict(zip(_ARG_NAMES, args))
    loss, dx, out = _step(inputs)
    res = [loss, dx]
    for k in range(4):
        res += [out[name][k] for name in WEIGHT_ORDER]
    return tuple(res)
```

```python
import functools
import math

import numpy as np
import jax
import jax.numpy as jnp
from jax import lax
from jax.experimental import pallas as pl
from jax.experimental.pallas import tpu as pltpu

F32 = jnp.float32
BF16 = jnp.bfloat16
MXU_DTYPE = jnp.bfloat16

D_MODEL = 1024
SSM_GROUP = 16
SSM_WIDTH = 768
SSM_GROUPS = 48
SSM_STATE = 64
N_STATE = SSM_GROUPS * SSM_STATE
SSM_CHUNKS = 6
CH_W = 128
CH_N = 512
ATT_HEAD_DIM = 64
ATT_HPG = 4
ATT_GROUPW = ATT_HPG * ATT_HEAD_DIM
DILATIONS = (1, 4, 16)
ATT_BLK = 128
ATT_SCALE = ATT_HEAD_DIM ** -0.5
ROT_DIM = 16
ROPE_THETA = 500000.0
XATT_HEADS = 4
XATT_HEAD_DIM = 256
XATT_SCALE = XATT_HEAD_DIM ** -0.5
D_FF = 4096
IN_COLS = 5120
DEEPNORM_ALPHA = 2.0 ** 0.25
LN_EPS = 1e-5
NEG_INF = -1e30
ADAM_LR = 0.001
ADAM_B1 = 0.9
ADAM_B2 = 0.999
ADAM_EPS = 1e-08
ADAM_WD = 0.01
ADAM_STEP = 10

N_SEG = 32
VMEM_LIMIT = 56 * 1024 * 1024
MESH = pl.DeviceIdType.MESH
HBM_SPEC = pl.BlockSpec(memory_space=pltpu.HBM)
VMEM_SPEC = pl.BlockSpec(memory_space=pltpu.VMEM)

BIG = (("w_in", 1024, 5120, 1), ("w_glu", 768, 2048, 1), ("w_att_up", 256, 1024, 1),
       ("w_mix_out", 1024, 1024, 0), ("w_xq", 1024, 1024, 0), ("w_xkv", 1024, 2048, 1),
       ("w_xo", 1024, 1024, 0), ("w_ff1", 1024, 4096, 1), ("w_ff2", 4096, 1024, 0))
SMALL = ("ln_in_g", "ln_in_b", "b_in", "ssm_log_dt", "ssm_a_re", "ssm_a_im", "ssm_b_re", "ssm_b_im",
         "ssm_c_re", "ssm_c_im", "ssm_d", "b_glu", "b_mix_out", "ln1_g", "ln1_b", "ln2_g", "ln2_b",
         "b_ff1", "b_ff2", "ln3_g", "ln3_b")
WEIGHT_ORDER = ("ln_in_g", "ln_in_b", "w_in", "b_in", "ssm_log_dt", "ssm_a_re", "ssm_a_im", "ssm_b_re",
                "ssm_b_im", "ssm_c_re", "ssm_c_im", "ssm_d", "w_glu", "b_glu", "w_att_up", "w_mix_out",
                "b_mix_out", "ln1_g", "ln1_b", "w_xq", "w_xkv", "w_xo", "ln2_g", "ln2_b", "w_ff1", "b_ff1",
                "w_ff2", "b_ff2", "ln3_g", "ln3_b")


def _cparams(n_axes):
    return pltpu.CompilerParams(dimension_semantics=("arbitrary",) * n_axes, vmem_limit_bytes=VMEM_LIMIT)


class _Carry:
    def __init__(self, ins, outs, n_sems, start, finish, done):
        self.ins, self.outs, self.n_sems, self.start, self.finish, self.done = ins, outs, n_sems, start, finish, done


def _call(name, body, grid, in_specs, out_specs, out_shape, args, scratch_shapes=(), carry=None):
    in_specs, out_specs, out_shape = list(in_specs), list(out_specs), list(out_shape)
    params = _cparams(len(grid))
    if carry is None:
        return pl.pallas_call(body, name=name, grid=grid, in_specs=in_specs, out_specs=out_specs, out_shape=out_shape,
                              scratch_shapes=list(scratch_shapes), compiler_params=params)(*args)
    n_in, n_out, n_ci, n_co = len(in_specs), len(out_specs), len(carry.ins), len(carry.outs)
    n_scr = len(scratch_shapes)

    def wrapped(*refs):
        ins, c_in = refs[:n_in], refs[n_in:n_in + n_ci]
        outs, c_out = refs[n_in + n_ci:n_in + n_ci + n_out], refs[n_in + n_ci + n_out:n_in + n_ci + n_out + n_co]
        scratch = refs[n_in + n_ci + n_out + n_co:n_in + n_ci + n_out + n_co + n_scr]
        send_sems, recv_sems = refs[-2:]
        ids = [pl.program_id(a) for a in range(len(grid))]
        first = functools.reduce(jnp.logical_and, [i == 0 for i in ids])
        last = functools.reduce(jnp.logical_and, [i == g - 1 for i, g in zip(ids, grid)])

        @pl.when(first)
        def _():
            carry.start(c_in, c_out, send_sems, recv_sems)

        body(*ins, *outs, *scratch)

        @pl.when(last)
        def _():
            carry.finish(c_in, c_out, send_sems, recv_sems)

    c_shapes = [jax.ShapeDtypeStruct(carry.ins[o].shape, carry.ins[o].dtype) if isinstance(o, int) else o
                for o in carry.outs]
    aliases = {n_in + o: n_out + i for i, o in enumerate(carry.outs) if isinstance(o, int)}
    res = pl.pallas_call(
        wrapped, name=name, grid=grid, in_specs=in_specs + [HBM_SPEC] * n_ci, out_specs=out_specs + [HBM_SPEC] * n_co,
        out_shape=out_shape + c_shapes, input_output_aliases=aliases,
        scratch_shapes=list(scratch_shapes) + [pltpu.SemaphoreType.DMA((carry.n_sems,))] * 2,
        compiler_params=params)(*args, *carry.ins)
    carry.done(res[n_out:])
    return res[:n_out]


def _rowwise(name, fn, rows, consts, outs, reds=(), tm=256, touts=(), carry=None):
    n_rows = (rows[0][0] if isinstance(rows[0], tuple) else rows[0]).shape[-2]
    tm = min(tm, n_rows)
    assert n_rows % tm == 0, (name, n_rows, tm)
    specs, args = [], []
    for r in rows:
        if isinstance(r, tuple) and len(r) == 3:
            arr, width, cb = r
            specs.append(pl.BlockSpec((tm, width), functools.partial(lambda i, cb: (i, cb), cb=cb)))
        elif isinstance(r, tuple):
            arr, slot = r
            specs.append(pl.BlockSpec((None, tm, arr.shape[2]), functools.partial(lambda i, s: (s, i, 0), s=slot)))
        else:
            arr = r
            specs.append(pl.BlockSpec((tm, arr.shape[1]), lambda i: (i, 0)))
        args.append(arr)
        assert arr.shape[-2] == n_rows, (name, arr.shape, n_rows)
    for cst in consts:
        specs.append(pl.BlockSpec(cst.shape, lambda i: (0, 0)))
        args.append(cst)
    n_r, n_c, n_o, n_d = len(rows), len(consts), len(outs) + len(touts), len(reds)
    out_shape = [jax.ShapeDtypeStruct((n_rows, c), dt) for c, dt in outs]
    out_specs = [pl.BlockSpec((tm, c), lambda i: (i, 0)) for c, _ in outs]
    out_shape += [jax.ShapeDtypeStruct((r, n_rows), dt) for r, dt in touts]
    out_specs += [pl.BlockSpec((r, tm), lambda i: (0, i)) for r, _ in touts]
    out_shape += [jax.ShapeDtypeStruct((1, c), F32) for c in reds]
    out_specs += [pl.BlockSpec((1, c), lambda i: (0, 0)) for c in reds]

    def body(*refs):
        ins = [r[...] for r in refs[:n_r + n_c]]
        o_refs = refs[n_r + n_c:n_r + n_c + n_o]
        d_refs = refs[n_r + n_c + n_o:]
        res = fn(*ins)
        res = res if isinstance(res, (tuple, list)) else (res,)
        assert len(res) == n_o + n_d, (name, len(res))
        for ref, val in zip(o_refs, res[:n_o]):
            ref[...] = val.astype(ref.dtype)
        first = pl.program_id(0) == 0
        for ref, val in zip(d_refs, res[n_o:]):
            @pl.when(first)
            def _(ref=ref, val=val):
                ref[...] = val

            @pl.when(jnp.logical_not(first))
            def _(ref=ref, val=val):
                ref[...] += val

    return _call(name, body, (n_rows // tm,), specs, out_specs, out_shape, args, carry=carry)


def _colsum(v):
    return jnp.sum(v.astype(F32), axis=0, keepdims=True)


_DIMS = {"nn": (((1,), (0,)), ((), ())), "nt": (((1,), (1,)), ((), ())), "tn": (((0,), (0,)), ((), ()))}


def _tile(dim, want):
    if dim <= want:
        return dim
    return max(t for t in range(128, want + 1, 128) if dim % t == 0)


def _dot(a, b, mode):
    return lax.dot_general(a.astype(MXU_DTYPE), b.astype(MXU_DTYPE), _DIMS[mode], preferred_element_type=F32)


def _mm(name, a, b, mode, *, bias=None, extras=(), epilogue=None, out_dtypes=(F32,), tm=1024, tn=1024, tk=1024,
        carry=None):
    if mode == "nn":
        (m, k), (_, n) = a.shape, b.shape
    elif mode == "nt":
        (m, k), (n, _) = a.shape, b.shape
    else:
        (k, m), (_, n) = a.shape, b.shape
    if k > tk:
        tk = 5 * tk
    tn = _tile(n, tn)
    tk = _tile(k, tk)
    nk = k // tk

    def vmem_bytes(rows):
        blocks = rows * tk * a.dtype.itemsize + tk * tn * b.dtype.itemsize
        blocks += sum(rows * tn * e.dtype.itemsize for e in extras)
        blocks += sum(rows * tn * jnp.dtype(dt).itemsize for dt in out_dtypes)
        return 2 * blocks + (rows * tn * 4 if nk > 1 else 0)

    tm = _tile(m, tm if mode == "tn" else 2 * tm)
    while vmem_bytes(tm) > 3 * VMEM_LIMIT // 4 and tm % 256 == 0:
        tm //= 2
    assert m % tm == 0 and n % tn == 0 and k % tk == 0, (name, m, n, k)
    a_spec = {"nn": pl.BlockSpec((tm, tk), lambda i, j, kk: (i, kk)),
              "nt": pl.BlockSpec((tm, tk), lambda i, j, kk: (i, kk)),
              "tn": pl.BlockSpec((tk, tm), lambda i, j, kk: (kk, i))}[mode]
    b_spec = {"nn": pl.BlockSpec((tk, tn), lambda i, j, kk: (kk, j)),
              "nt": pl.BlockSpec((tn, tk), lambda i, j, kk: (j, kk)),
              "tn": pl.BlockSpec((tk, tn), lambda i, j, kk: (kk, j))}[mode]
    specs, args = [a_spec, b_spec], [a, b]
    if bias is not None:
        specs.append(pl.BlockSpec((1, tn), lambda i, j, kk: (0, j)))
        args.append(bias)
    for e in extras:
        specs.append(pl.BlockSpec((tm, tn), lambda i, j, kk: (i, j)))
        args.append(e)
    n_e, n_o = len(extras), len(out_dtypes)
    has_bias = bias is not None

    def body(*refs):
        a_ref, b_ref = refs[0], refs[1]
        pos = 2
        bias_ref = refs[pos] if has_bias else None
        pos += int(has_bias)
        e_refs = refs[pos:pos + n_e]
        o_refs = refs[pos + n_e:pos + n_e + n_o]
        acc_ref = refs[pos + n_e + n_o] if nk > 1 else None
        part = _dot(a_ref[...], b_ref[...], mode)

        def finish(r):
            if has_bias:
                r = r + bias_ref[...]
            res = epilogue(r, *[e[...] for e in e_refs]) if epilogue is not None else (r,)
            for ref, val in zip(o_refs, res):
                ref[...] = val.astype(ref.dtype)

        if nk == 1:
            finish(part)
        else:
            kk = pl.program_id(2)

            @pl.when(kk == 0)
            def _():
                acc_ref[...] = part

            @pl.when(kk > 0)
            def _():
                acc_ref[...] += part

            @pl.when(kk == nk - 1)
            def _():
                finish(acc_ref[...])

    res = _call(name, body, (m // tm, n // tn, nk), specs,
                [pl.BlockSpec((tm, tn), lambda i, j, kk: (i, j)) for _ in out_dtypes],
                [jax.ShapeDtypeStruct((m, n), dt) for dt in out_dtypes], args,
                scratch_shapes=[pltpu.VMEM((tm, tn), F32)] if nk > 1 else [], carry=carry)
    return res[0] if n_o == 1 else res


def _ssm_wgrads(u, dy, g_re, g_im, h_re, h_im, tk=512):
    s = u.shape[0]
    tk = min(tk, s)
    nk = s // tk
    assert tk % N_SEG == 0

    def body(u_ref, dy_ref, gre_ref, gim_ref, hre_ref, him_ref, lre_ref, lim_ref, db_ref, dc_ref, dar_ref, dai_ref,
             pre_ref, pim_ref):
        kk = pl.program_id(1)
        u_blk, dy_blk = u_ref[...], dy_ref[...]
        g_r, g_i, h_r, h_i = gre_ref[...], gim_ref[...], hre_ref[...], him_ref[...]
        d_b = jnp.concatenate([_dot(u_blk, g_r, "tn"), _dot(u_blk, g_i, "tn")], axis=1)
        d_c = jnp.concatenate([_dot(h_r, dy_blk, "tn"), _dot(h_i, dy_blk, "tn")], axis=0)

        @pl.when(kk == 0)
        def _():
            first_row = lax.broadcasted_iota(jnp.int32, (N_SEG, CH_N), 0) == 0
            pre_ref[...] = jnp.where(first_row, 0.0, pltpu.roll(lre_ref[...], 1, 0))
            pim_ref[...] = jnp.where(first_row, 0.0, pltpu.roll(lim_ref[...], 1, 0))

        p_r = jnp.concatenate([pre_ref[...], h_r[:tk - N_SEG]], axis=0)
        p_i = jnp.concatenate([pim_ref[...], h_i[:tk - N_SEG]], axis=0)
        pre_ref[...] = h_r[tk - N_SEG:]
        pim_ref[...] = h_i[tk - N_SEG:]
        d_ar = jnp.sum(g_r * p_r + g_i * p_i, axis=0, keepdims=True)
        d_ai = jnp.sum(g_i * p_r - g_r * p_i, axis=0, keepdims=True)

        @pl.when(kk == 0)
        def _():
            db_ref[...] = d_b
            dc_ref[...] = d_c
            dar_ref[...] = d_ar
            dai_ref[...] = d_ai

        @pl.when(kk > 0)
        def _():
            db_ref[...] += d_b
            dc_ref[...] += d_c
            dar_ref[...] += d_ar
            dai_ref[...] += d_ai

    chan = pl.BlockSpec((tk, CH_W), lambda j, kk: (kk, j))
    state = pl.BlockSpec((tk, CH_N), lambda j, kk: (kk, j))
    last = pl.BlockSpec((N_SEG, CH_N), lambda j, kk: (s // N_SEG - 1, j))
    row = pl.BlockSpec((1, CH_N), lambda j, kk: (0, j))
    return pl.pallas_call(
        body, name="ssm_wgrads", grid=(SSM_CHUNKS, nk),
        in_specs=[chan, chan, state, state, state, state, last, last],
        out_specs=[pl.BlockSpec((None, CH_W, 2 * CH_N), lambda j, kk: (j, 0, 0)),
                   pl.BlockSpec((None, 2 * CH_N, CH_W), lambda j, kk: (j, 0, 0)), row, row],
        out_shape=[jax.ShapeDtypeStruct((SSM_CHUNKS, CH_W, 2 * CH_N), F32),
                   jax.ShapeDtypeStruct((SSM_CHUNKS, 2 * CH_N, CH_W), F32),
                   jax.ShapeDtypeStruct((1, N_STATE), F32), jax.ShapeDtypeStruct((1, N_STATE), F32)],
        scratch_shapes=[pltpu.VMEM((N_SEG, CH_N), F32)] * 2,
        compiler_params=_cparams(2))(u, dy, g_re, g_im, h_re, h_im, h_re, h_im)


SCAN_LB = 256


def _split_by_scan_block(mat, axis):
    halves = []
    for l in range(CH_N // SCAN_LB):
        re = lax.slice_in_dim(mat, l * SCAN_LB, (l + 1) * SCAN_LB, axis=axis)
        im = lax.slice_in_dim(mat, CH_N + l * SCAN_LB, CH_N + (l + 1) * SCAN_LB, axis=axis)
        halves.append(jnp.concatenate([re, im], axis=axis))
    return jnp.stack(halves, axis=1).reshape((-1,) + halves[0].shape[1:])


def _ssm_scan(name, chan, expand12, contract12, a_re, a_im, d_row, reverse, carry=None):
    s = chan.shape[0]
    seg_len = s // N_SEG
    n_sq = int(math.log2(seg_len))
    assert 2 ** n_sq == seg_len
    rb = min(512, s)
    per_chunk = CH_N // SCAN_LB

    def body(are_ref, aim_ref, ch_ref, e_ref, k_ref, d_ref, hre_ref, him_ref, o_ref, wre_ref, wim_ref, ere, eim, cre, cim):
        e_mat, k_mat = e_ref[...], k_ref[...]
        for r in range(s // rb):
            rows = slice(r * rb, (r + 1) * rb)
            c = ch_ref[rows, :]
            if reverse:
                wre_ref[rows, :] = _dot(c, e_mat[:SCAN_LB], "nt")
                wim_ref[rows, :] = _dot(c, e_mat[SCAN_LB:], "nt")
            else:
                wre_ref[rows, :] = _dot(c, e_mat[:, :SCAN_LB], "nn")
                wim_ref[rows, :] = _dot(c, e_mat[:, SCAN_LB:], "nn")

        ar1 = are_ref[...]
        ai1 = -aim_ref[...] if reverse else aim_ref[...]
        ar = jnp.broadcast_to(ar1, (N_SEG, SCAN_LB))
        ai = jnp.broadcast_to(ai1, (N_SEG, SCAN_LB))

        def rows_of(k):
            kk = seg_len - 1 - k if reverse else k
            return pl.ds(pl.multiple_of(kk * N_SEG, N_SEG), N_SEG)

        def local(k, carry):
            hr, hi = carry
            rows = rows_of(k)
            nr = ar * hr - ai * hi + wre_ref[rows, :]
            ni = ar * hi + ai * hr + wim_ref[rows, :]
            hre_ref[rows, :] = nr
            him_ref[rows, :] = ni
            return nr, ni

        zero = jnp.zeros((N_SEG, SCAN_LB), F32)
        er, ei = lax.fori_loop(0, seg_len, local, (zero, zero))
        ere[...] = er
        eim[...] = ei
        pr, pi = ar1, ai1
        for _ in range(n_sq):
            pr, pi = pr * pr - pi * pi, 2.0 * pr * pi
        cr = jnp.zeros((1, SCAN_LB), F32)
        ci = jnp.zeros((1, SCAN_LB), F32)
        for jj in range(N_SEG):
            j = N_SEG - 1 - jj if reverse else jj
            cre[j:j + 1, :] = cr
            cim[j:j + 1, :] = ci
            er_j, ei_j = ere[j:j + 1, :], eim[j:j + 1, :]
            cr, ci = pr * cr - pi * ci + er_j, pr * ci + pi * cr + ei_j
        c_r, c_i = cre[...], cim[...]

        def fix(k, carry):
            qr, qi = carry
            rows = rows_of(k)
            hre_ref[rows, :] = hre_ref[rows, :] + (qr * c_r - qi * c_i)
            him_ref[rows, :] = him_ref[rows, :] + (qr * c_i + qi * c_r)
            return qr * ar - qi * ai, qr * ai + qi * ar

        lax.fori_loop(0, seg_len, fix, (ar, ai))

        first_of_chunk = lax.rem(pl.program_id(0), per_chunk) == 0
        for r in range(s // rb):
            rows = slice(r * rb, (r + 1) * rb)
            if reverse:
                part = (_dot(hre_ref[rows, :], k_mat[:, :SCAN_LB], "nt")
                        + _dot(him_ref[rows, :], k_mat[:, SCAN_LB:], "nt"))
            else:
                part = _dot(hre_ref[rows, :], k_mat[:SCAN_LB], "nn") + _dot(him_ref[rows, :], k_mat[SCAN_LB:], "nn")

            @pl.when(first_of_chunk)
            def _(rows=rows, part=part):
                o_ref[rows, :] = part + d_ref[...] * ch_ref[rows, :]

            @pl.when(jnp.logical_not(first_of_chunk))
            def _(rows=rows, part=part):
                o_ref[rows, :] += part

    nblk = N_STATE // SCAN_LB
    blk = pl.BlockSpec((s, SCAN_LB), lambda b: (0, b))
    row = pl.BlockSpec((1, SCAN_LB), lambda b: (0, b))
    chan_blk = pl.BlockSpec((s, CH_W), lambda b: (0, b // per_chunk))
    res = _call(name, body, (nblk,),
                [row, row, chan_blk, pl.BlockSpec((None,) + expand12.shape[1:], lambda b: (b, 0, 0)),
                 pl.BlockSpec((None,) + contract12.shape[1:], lambda b: (b, 0, 0)),
                 pl.BlockSpec((1, CH_W), lambda b: (0, b // per_chunk))],
                [blk, blk, chan_blk],
                [jax.ShapeDtypeStruct((s, N_STATE), F32)] * 2 + [jax.ShapeDtypeStruct((s, SSM_WIDTH), F32)],
                (a_re, a_im, chan, expand12, contract12, d_row),
                scratch_shapes=[pltpu.VMEM((s, SCAN_LB), F32)] * 2 + [pltpu.VMEM((N_SEG, SCAN_LB), F32)] * 4, carry=carry)
    return res[0], res[1], res[2]


def _disc(ldt, are, aim, bre, bim):
    dt = jnp.exp(ldt)
    mag = jnp.exp(are * dt)
    abr = mag * jnp.cos(aim * dt)
    abi = mag * jnp.sin(aim * dt)
    den = jnp.square(are) + jnp.square(aim)
    nr = abr - 1.0
    fre = (nr * are + abi * aim) / den
    fim = (abi * are - nr * aim) / den
    return abr, abi, fre * bre - fim * bim, fre * bim + fim * bre


def _ssm_disc_fwd(ldt, are, aim, bre, bim):
    def body(l_ref, ar_ref, ai_ref, br_ref, bi_ref, o0, o1, o2, o3):
        res = _disc(l_ref[...], ar_ref[...], ai_ref[...], br_ref[...], bi_ref[...])
        for ref, val in zip((o0, o1, o2, o3), res):
            ref[...] = val

    col = jax.ShapeDtypeStruct((N_STATE, 1), F32)
    mat = jax.ShapeDtypeStruct((N_STATE, SSM_GROUP), F32)
    return pl.pallas_call(body, name="ssm_disc_fwd", out_shape=[col, col, mat, mat],
                          in_specs=[VMEM_SPEC] * 5, out_specs=[VMEM_SPEC] * 4)(ldt, are, aim, bre, bim)


def _ssm_disc_bwd(ldt, are, aim, bre, bim, d_abr, d_abi, d_bbr, d_bbi):
    def body(l_ref, ar_ref, ai_ref, br_ref, bi_ref, c0, c1, c2, c3, g_ldt, g_are, g_aim, g_bre, g_bim):
        _, vjp = jax.vjp(_disc, l_ref[...], ar_ref[...], ai_ref[...], br_ref[...], bi_ref[...])
        dl, dar, dai, dbr, dbi = vjp((c0[...], c1[...], c2[...], c3[...]))
        state = lax.broadcasted_iota(jnp.int32, (N_STATE, SSM_GROUPS), 0)
        group = lax.broadcasted_iota(jnp.int32, (N_STATE, SSM_GROUPS), 1)
        pick = jnp.right_shift(state, 6) == group
        g_ldt[...] = jnp.sum(jnp.where(pick, dl, 0.0), axis=0, keepdims=True)
        g_are[...] = dar
        g_aim[...] = dai
        g_bre[...] = dbr
        g_bim[...] = dbi

    col = jax.ShapeDtypeStruct((N_STATE, 1), F32)
    mat = jax.ShapeDtypeStruct((N_STATE, SSM_GROUP), F32)
    return pl.pallas_call(body, name="ssm_disc_bwd",
                          out_shape=[jax.ShapeDtypeStruct((1, SSM_GROUPS), F32), col, col, mat, mat],
                          in_specs=[VMEM_SPEC] * 9, out_specs=[VMEM_SPEC] * 5,
                          compiler_params=pltpu.CompilerParams(vmem_limit_bytes=VMEM_LIMIT))(
        ldt, are, aim, bre, bim, d_abr, d_abi, d_bbr, d_bbi)


_EYE8 = np.eye(8, dtype=np.float32)


def _blockdiag_b(bb):
    t = bb.reshape(SSM_CHUNKS, 8, SSM_STATE, SSM_GROUP).transpose(0, 1, 3, 2)
    return jnp.einsum("igcn,gh->igchn", t, _EYE8).reshape(SSM_CHUNKS, CH_W, CH_N)


def _diag_of_b(m):
    t = jnp.einsum("igchn,gh->igcn", m.reshape(SSM_CHUNKS, 8, SSM_GROUP, 8, SSM_STATE), _EYE8)
    return t.transpose(0, 1, 3, 2).reshape(N_STATE, SSM_GROUP)


def _blockdiag_c(c):
    t = c.reshape(SSM_CHUNKS, 8, SSM_GROUP, SSM_STATE).transpose(0, 1, 3, 2)
    return jnp.einsum("ignc,gh->ignhc", t, _EYE8).reshape(SSM_CHUNKS, CH_N, CH_W)


def _diag_of_c(m):
    t = jnp.einsum("ignhc,gh->ignc", m.reshape(SSM_CHUNKS, 8, SSM_STATE, 8, SSM_GROUP), _EYE8)
    return t.transpose(0, 1, 3, 2).reshape(SSM_GROUPS, SSM_GROUP, SSM_STATE)


def _time_perm(a):
    s, c = a.shape
    return a.reshape(N_SEG, s // N_SEG, c).transpose(1, 0, 2).reshape(s, c)


def _time_unperm(a):
    s, c = a.shape
    return a.reshape(s // N_SEG, N_SEG, c).transpose(1, 0, 2).reshape(s, c)


def _dilate(a, d):
    s, c = a.shape
    return a if d == 1 else a.reshape(s // d, d, c).transpose(1, 0, 2).reshape(s, c)


def _undilate(a, d):
    s, c = a.shape
    return a if d == 1 else a.reshape(d, s // d, c).transpose(1, 0, 2).reshape(s, c)


def _dilate_rows(a, d):
    r, s = a.shape
    return a if d == 1 else a.reshape(r, s // d, d).transpose(0, 2, 1).reshape(r, s)


ATT_T = 4
ATT_ROWS = ATT_T * ATT_BLK


def _window(prev_ref, cur_ref, i, sl):
    if i == 0:
        return jnp.concatenate([prev_ref[:, sl], cur_ref[0:ATT_BLK, sl]], axis=0)
    return cur_ref[(i - 1) * ATT_BLK:(i + 1) * ATT_BLK, sl]


def _band_valid(first_key):
    qi = lax.broadcasted_iota(jnp.int32, (ATT_BLK, 2 * ATT_BLK), 0)
    ki = lax.broadcasted_iota(jnp.int32, (ATT_BLK, 2 * ATT_BLK), 1)
    steps = qi + ATT_BLK - ki
    return (steps >= 0) & (steps <= ATT_BLK) & (ki >= first_key)


ATT_STATW = ATT_HPG * 128


def _stat(h):
    return slice(h * 128, (h + 1) * 128)


def _stat_rows(stat):
    n = stat.shape[0]
    heads = [stat[:, _stat(h)].T[0:1, :] for h in range(ATT_HPG)]
    return jnp.concatenate(heads + [jnp.zeros((8 - ATT_HPG, n), stat.dtype)], axis=0)


def _attn_specs(nb, width=ATT_GROUPW):
    cur = pl.BlockSpec((ATT_ROWS, width), lambda b: (b, 0))
    prev = pl.BlockSpec((ATT_BLK, width), lambda b: (jnp.maximum(b * ATT_T - 1, 0), 0))
    nxt = pl.BlockSpec((ATT_BLK, width), lambda b: (jnp.minimum((b + 1) * ATT_T, nb - 1), 0))
    return cur, prev, nxt


def _attn_fwd(tag, per_seq, q, k, v):
    s = q.shape[0]
    nb = s // ATT_BLK

    def body(q_ref, kc_ref, kp_ref, vc_ref, vp_ref, o_ref, lse_ref):
        bt = pl.program_id(0)
        for i in range(ATT_T):
            has_prev = lax.rem(bt * ATT_T + i, per_seq) > 0
            valid = _band_valid(jnp.where(has_prev, 0, ATT_BLK))
            rows = slice(i * ATT_BLK, (i + 1) * ATT_BLK)
            for h in range(ATT_HPG):
                sl = slice(h * ATT_HEAD_DIM, (h + 1) * ATT_HEAD_DIM)
                kcat = _window(kp_ref, kc_ref, i, sl)
                vcat = _window(vp_ref, vc_ref, i, sl)
                sc = _dot(q_ref[rows, sl], kcat, "nt") * ATT_SCALE
                sc = jnp.where(valid, sc, NEG_INF)
                m = jnp.max(sc, axis=-1, keepdims=True)
                p = jnp.exp(sc - m)
                den = jnp.sum(p, axis=-1, keepdims=True)
                o_ref[rows, sl] = _dot(p, vcat, "nn") / den
                lse_ref[rows, _stat(h)] = jnp.broadcast_to(m + jnp.log(den), (ATT_BLK, 128))

    cur, prev, _ = _attn_specs(nb)
    stat, _, _ = _attn_specs(nb, ATT_STATW)
    return pl.pallas_call(
        body, name="attn_fwd_" + tag, grid=(nb // ATT_T,), in_specs=[cur, cur, prev, cur, prev], out_specs=[cur, stat],
        out_shape=[jax.ShapeDtypeStruct((s, ATT_GROUPW), F32), jax.ShapeDtypeStruct((s, ATT_STATW), F32)],
        compiler_params=_cparams(1))(q, k, k, v, v)


def _attn_dq(tag, per_seq, q, k, v, do, lse, delta):
    s = q.shape[0]
    nb = s // ATT_BLK

    def body(q_ref, kc_ref, kp_ref, vc_ref, vp_ref, do_ref, lse_ref, dl_ref, dq_ref):
        bt = pl.program_id(0)
        for i in range(ATT_T):
            has_prev = lax.rem(bt * ATT_T + i, per_seq) > 0
            valid = _band_valid(jnp.where(has_prev, 0, ATT_BLK))
            rows = slice(i * ATT_BLK, (i + 1) * ATT_BLK)
            for h in range(ATT_HPG):
                sl = slice(h * ATT_HEAD_DIM, (h + 1) * ATT_HEAD_DIM)
                kcat = _window(kp_ref, kc_ref, i, sl)
                vcat = _window(vp_ref, vc_ref, i, sl)
                lse = jnp.concatenate([lse_ref[rows, _stat(h)]] * 2, axis=1)
                dlt = jnp.concatenate([dl_ref[rows, _stat(h)]] * 2, axis=1)
                sc = _dot(q_ref[rows, sl], kcat, "nt") * ATT_SCALE
                p = jnp.exp(jnp.where(valid, sc, NEG_INF) - lse)
                dp = _dot(do_ref[rows, sl], vcat, "nt")
                ds = p * (dp - dlt) * ATT_SCALE
                dq_ref[rows, sl] = _dot(ds, kcat, "nn")

    cur, prev, _ = _attn_specs(nb)
    stat, _, _ = _attn_specs(nb, ATT_STATW)
    return pl.pallas_call(
        body, name="attn_dq_" + tag, grid=(nb // ATT_T,), in_specs=[cur, cur, prev, cur, prev, cur, stat, stat],
        out_specs=cur, out_shape=jax.ShapeDtypeStruct((s, ATT_GROUPW), F32),
        compiler_params=_cparams(1))(q, k, k, v, v, do, lse, delta)


def _attn_dkv(tag, per_seq, q, k, v, do, lse_t, delta_t):
    s = q.shape[0]
    nb = s // ATT_BLK

    def body(k_ref, v_ref, qc_ref, qn_ref, doc_ref, don_ref, lc_ref, ln_ref, dc_ref, dn_ref, dk_ref, dv_ref):
        bt = pl.program_id(0)
        ki = lax.broadcasted_iota(jnp.int32, (ATT_BLK, 2 * ATT_BLK), 0)
        ci = lax.broadcasted_iota(jnp.int32, (ATT_BLK, 2 * ATT_BLK), 1)

        def pair(edge_ref, cur_ref, i, sl):
            if i == ATT_T - 1:
                return jnp.concatenate([cur_ref[i * ATT_BLK:(i + 1) * ATT_BLK, sl], edge_ref[:, sl]], axis=0)
            return cur_ref[i * ATT_BLK:(i + 2) * ATT_BLK, sl]

        def pair_row(edge_ref, cur_ref, i, h):
            if i == ATT_T - 1:
                row = jnp.concatenate([cur_ref[h:h + 1, i * ATT_BLK:(i + 1) * ATT_BLK], edge_ref[h:h + 1, :]], axis=1)
            else:
                row = cur_ref[h:h + 1, i * ATT_BLK:(i + 2) * ATT_BLK]
            return jnp.broadcast_to(row, (ATT_BLK, 2 * ATT_BLK))

        for i in range(ATT_T):
            b = bt * ATT_T + i
            next_uses = (b + 1 < nb) & (lax.rem(b + 1, per_seq) > 0)
            reach = jnp.where(next_uses, 0, 4 * ATT_BLK)
            valid = ((ci < ATT_BLK) & (ci >= ki)) | ((ci >= ATT_BLK) & (ki - ci + ATT_BLK >= reach))
            rows = slice(i * ATT_BLK, (i + 1) * ATT_BLK)
            for h in range(ATT_HPG):
                sl = slice(h * ATT_HEAD_DIM, (h + 1) * ATT_HEAD_DIM)
                qcat, docat = pair(qn_ref, qc_ref, i, sl), pair(don_ref, doc_ref, i, sl)
                sc = _dot(k_ref[rows, sl], qcat, "nt") * ATT_SCALE
                p = jnp.exp(jnp.where(valid, sc, NEG_INF) - pair_row(ln_ref, lc_ref, i, h))
                dv_ref[rows, sl] = _dot(p, docat, "nn")
                dp = _dot(v_ref[rows, sl], docat, "nt")
                ds = p * (dp - pair_row(dn_ref, dc_ref, i, h)) * ATT_SCALE
                dk_ref[rows, sl] = _dot(ds, qcat, "nn")

    cur, _, nxt = _attn_specs(nb)
    stat = pl.BlockSpec((8, ATT_ROWS), lambda b: (0, b))
    snxt = pl.BlockSpec((8, ATT_BLK), lambda b: (0, jnp.minimum((b + 1) * ATT_T, nb - 1)))
    return pl.pallas_call(
        body, name="attn_dkv_" + tag, grid=(nb // ATT_T,), in_specs=[cur, cur, cur, nxt, cur, nxt, stat, snxt, stat, snxt],
        out_specs=[cur, cur], out_shape=[jax.ShapeDtypeStruct((s, ATT_GROUPW), F32)] * 2,
        compiler_params=_cparams(1))(k, v, q, q, do, do, lse_t, lse_t, delta_t, delta_t)


def _xattn_probs(q, kh):
    sc = _dot(q, kh, "nt") * XATT_SCALE
    e = jnp.exp(sc - jnp.max(sc, axis=-1, keepdims=True))
    return e / jnp.sum(e, axis=-1, keepdims=True)


def _xattn_fwd(q, kv, tm=512):
    s = q.shape[0]
    tm = min(tm, s)

    def body(q_ref, kv_ref, o_ref):
        for h in range(XATT_HEADS):
            sl = slice(h * XATT_HEAD_DIM, (h + 1) * XATT_HEAD_DIM)
            vs = slice(D_MODEL + h * XATT_HEAD_DIM, D_MODEL + (h + 1) * XATT_HEAD_DIM)
            p = _xattn_probs(q_ref[:, sl], kv_ref[:, sl])
            o_ref[:, sl] = _dot(p, kv_ref[:, vs], "nn").astype(o_ref.dtype)

    return pl.pallas_call(
        body, name="xattn_fwd", grid=(s // tm,),
        in_specs=[pl.BlockSpec((tm, D_MODEL), lambda i: (i, 0)), pl.BlockSpec(kv.shape, lambda i: (0, 0))],
        out_specs=pl.BlockSpec((tm, D_MODEL), lambda i: (i, 0)),
        out_shape=jax.ShapeDtypeStruct((s, D_MODEL), MXU_DTYPE), compiler_params=_cparams(1))(q, kv)


def _xattn_bwd(q, kv, do, tm=512):
    s = q.shape[0]
    tm = min(tm, s)

    def body(q_ref, kv_ref, do_ref, dq_ref, dkv_ref):
        first = pl.program_id(0) == 0

        @pl.when(first)
        def _():
            dkv_ref[...] = jnp.zeros_like(dkv_ref)

        for h in range(XATT_HEADS):
            sl = slice(h * XATT_HEAD_DIM, (h + 1) * XATT_HEAD_DIM)
            vs = slice(D_MODEL + h * XATT_HEAD_DIM, D_MODEL + (h + 1) * XATT_HEAD_DIM)
            p = _xattn_probs(q_ref[:, sl], kv_ref[:, sl])
            dkv_ref[:, vs] += _dot(p, do_ref[:, sl], "tn")
            dp = _dot(do_ref[:, sl], kv_ref[:, vs], "nt")
            ds = p * (dp - jnp.sum(dp * p, axis=-1, keepdims=True)) * XATT_SCALE
            dq_ref[:, sl] = _dot(ds, kv_ref[:, sl], "nn").astype(dq_ref.dtype)
            dkv_ref[:, sl] += _dot(ds, q_ref[:, sl], "tn")

    row = pl.BlockSpec((tm, D_MODEL), lambda i: (i, 0))
    whole = pl.BlockSpec(kv.shape, lambda i: (0, 0))
    return pl.pallas_call(
        body, name="xattn_bwd", grid=(s // tm,), in_specs=[row, whole, row], out_specs=[row, whole],
        out_shape=[jax.ShapeDtypeStruct((s, D_MODEL), MXU_DTYPE), jax.ShapeDtypeStruct(kv.shape, F32)],
        compiler_params=_cparams(1))(q, kv, do)


def _ln(x, g, b):
    mu = jnp.mean(x, axis=-1, keepdims=True)
    xc = x - mu
    var = jnp.mean(jnp.square(xc), axis=-1, keepdims=True)
    return xc * lax.rsqrt(var + LN_EPS) * g + b


def _res_ln(h, o, g, b):
    return _ln(DEEPNORM_ALPHA * h + o, g, b)


def _gate(gs, ga, z1, z2, batt):
    return jax.nn.sigmoid(gs) * (z1 * jax.nn.sigmoid(z2)) + jax.nn.sigmoid(ga) * batt


def _rope_tables(pos, invf, m1, m2):
    ang = pos.astype(F32) * invf
    sin = jnp.sin(ang)
    return jnp.cos(ang), -sin * m1, sin * m2


def _rope(t, cos, s_up, s_dn):
    w = t.shape[-1]
    return t * cos + pltpu.roll(t, w - ROT_DIM // 2, 1) * s_up + pltpu.roll(t, ROT_DIM // 2, 1) * s_dn


def _rope_t(dt, cos, s_up, s_dn):
    w = dt.shape[-1]
    return dt * cos + pltpu.roll(dt * s_up, ROT_DIM // 2, 1) + pltpu.roll(dt * s_dn, w - ROT_DIM // 2, 1)


def _rope_consts():
    inv_freq = ROPE_THETA ** (-jnp.arange(0, ROT_DIM, 2, dtype=F32) / ROT_DIM)
    d = np.arange(ATT_GROUPW) % ATT_HEAD_DIM
    invf = jnp.where(d < ROT_DIM, inv_freq[d % (ROT_DIM // 2)], 0.0).reshape(1, ATT_GROUPW).astype(F32)
    m1 = jnp.asarray((d < ROT_DIM // 2).astype(np.float32)).reshape(1, ATT_GROUPW)
    m2 = jnp.asarray(((d >= ROT_DIM // 2) & (d < ROT_DIM)).astype(np.float32)).reshape(1, ATT_GROUPW)
    return invf, m1, m2


def _head_sum_matrix():
    d = np.arange(ATT_GROUPW) // ATT_HEAD_DIM
    s = np.arange(ATT_STATW) // 128
    return jnp.asarray((d[:, None] == s[None, :]).astype(np.float32))


def _adamw(w, g, m, v):
    m = ADAM_B1 * m + (1.0 - ADAM_B1) * g
    v = ADAM_B2 * v + (1.0 - ADAM_B2) * jnp.square(g)
    m_hat = m / (1.0 - ADAM_B1 ** ADAM_STEP)
    v_hat = v / (1.0 - ADAM_B2 ** ADAM_STEP)
    delta = -ADAM_LR * (m_hat / (jnp.sqrt(v_hat) + ADAM_EPS) + ADAM_WD * w)
    return delta, m, v


def _local_step(x, mem, pos, target, sp, ex):
    s = x.shape[0]
    al = DEEPNORM_ALPHA
    mx = MXU_DTYPE

    h0, h0b = _rowwise("ln_in", lambda x, g, b: (lambda h: (h, h))(_ln(x, g, b)), [x],
                       [sp["ln_in_g"], sp["ln_in_b"]], [(D_MODEL, F32), (D_MODEL, mx)],
                       carry=ex.gather_carry(["w_in"]))
    proj = _mm("proj", h0b, ex.weight("w_in"), "nn", bias=sp["b_in"],
               carry=ex.gather_carry(["w_glu", "w_att_up", "w_mix_out", "w_xq", "w_xkv"]))

    ldt = jnp.repeat(sp["ssm_log_dt"].reshape(SSM_GROUPS), SSM_STATE).reshape(N_STATE, 1)
    are, aim = sp["ssm_a_re"].reshape(N_STATE, 1), sp["ssm_a_im"].reshape(N_STATE, 1)
    bre, bim = sp["ssm_b_re"].reshape(N_STATE, SSM_GROUP), sp["ssm_b_im"].reshape(N_STATE, SSM_GROUP)
    abr, abi, bbr, bbi = _ssm_disc_fwd(ldt, are, aim, bre, bim)
    a_re, a_im = abr.reshape(1, N_STATE), abi.reshape(1, N_STATE)
    bexp = jnp.concatenate([_blockdiag_b(bbr), _blockdiag_b(bbi)], axis=2).astype(mx)
    cexp = jnp.concatenate([_blockdiag_c(sp["ssm_c_re"].reshape(SSM_GROUPS, SSM_GROUP, SSM_STATE)),
                            -_blockdiag_c(sp["ssm_c_im"].reshape(SSM_GROUPS, SSM_GROUP, SSM_STATE))],
                           axis=1).astype(mx)
    u_p = _time_perm(proj[:, :SSM_WIDTH])
    b12, c12 = _split_by_scan_block(bexp, 2), _split_by_scan_block(cexp, 1)
    h_re, h_im, y_p = _ssm_scan("ssm_scan_fwd", u_p, b12, c12, a_re, a_im, sp["ssm_d"], reverse=False,
                                carry=ex.gather_carry(["w_ff1", "w_ff2"]))
    y = _time_unperm(y_p)
    ygb, = _rowwise("gelu", lambda y: jax.nn.gelu(y), [y], [], [(SSM_WIDTH, mx)])
    z = _mm("glu", ygb, ex.weight("w_glu"), "nn", bias=sp["b_glu"], carry=ex.gather_carry(["w_xo"]))

    invf, m1, m2 = _rope_consts()

    def rope_fwd(pos, q0, q1, q2, k0, k1, k2, v0, v1, v2, invf, m1, m2):
        tabs = _rope_tables(pos, invf, m1, m2)
        return tuple(_rope(t, *tabs) for t in (q0, q1, q2, k0, k1, k2)) + (v0, v1, v2)

    qkv_cols = [(proj, ATT_GROUPW, 3 + i) for i in range(9)]
    qkv = _rowwise("rope", rope_fwd, [pos] + qkv_cols, [invf, m1, m2], [(ATT_GROUPW, mx)] * 9)
    n_blocks = s // ATT_BLK
    groups = [(str(g), n_blocks // d, d) for g, d in enumerate(DILATIONS)]
    q_d = [_dilate(qkv[g], d) for g, d in enumerate(DILATIONS)]
    k_d = [_dilate(qkv[3 + g], d) for g, d in enumerate(DILATIONS)]
    v_d = [_dilate(qkv[6 + g], d) for g, d in enumerate(DILATIONS)]
    o_g, l_g = [], []
    for g, (tag, per_seq, d) in enumerate(groups):
        o, lse = _attn_fwd(tag, per_seq, q_d[g], k_d[g], v_d[g])
        o_g.append(_undilate(o, d))
        l_g.append(_undilate(lse, d))

    def merge(o0, o1, o2, l0, l1, l2):
        m = jnp.maximum(jnp.maximum(l0, l1), l2)
        e0, e1, e2 = jnp.exp(l0 - m), jnp.exp(l1 - m), jnp.exp(l2 - m)
        tot = e0 + e1 + e2

        def per_dim(e):
            w = e / tot
            return jnp.concatenate([w[:, h * 128:h * 128 + ATT_HEAD_DIM] for h in range(ATT_HPG)], axis=1)

        att = per_dim(e0) * o0 + per_dim(e1) * o1 + per_dim(e2) * o2
        lse = m + jnp.log(tot)
        return att, att, lse, _stat_rows(lse)

    att, attb, lse_tot, lse_tot_t = _rowwise("attn_merge", merge, o_g + l_g, [],
                                             [(ATT_GROUPW, F32), (ATT_GROUPW, mx), (ATT_STATW, F32)], touts=[(8, F32)])
    batt = _mm("att_up", attb, ex.weight("w_att_up"), "nn")

    gate_rows = [(proj, D_MODEL, 3), (proj, D_MODEL, 4), (z, D_MODEL, 0), (z, D_MODEL, 1), batt]
    mixedb, = _rowwise("gate", _gate, gate_rows, [], [(D_MODEL, mx)])
    o1 = _mm("mix_out", mixedb, ex.weight("w_mix_out"), "nn", bias=sp["b_mix_out"])
    h1, h1b = _rowwise("ln1", lambda h, o, g, b: (lambda r: (r, r))(_res_ln(h, o, g, b)), [h0, o1],
                       [sp["ln1_g"], sp["ln1_b"]], [(D_MODEL, F32), (D_MODEL, mx)])

    qx = _mm("xq", h1b, ex.weight("w_xq"), "nn", out_dtypes=(mx,))
    kvx = _mm("xkv", mem, ex.weight("w_xkv"), "nn", out_dtypes=(mx,))
    oxb = _xattn_fwd(qx, kvx)
    o2 = _mm("xo", oxb, ex.weight("w_xo"), "nn")
    h2, h2b = _rowwise("ln2", lambda h, o, g, b: (lambda r: (r, r))(_res_ln(h, o, g, b)), [h1, o2],
                       [sp["ln2_g"], sp["ln2_b"]], [(D_MODEL, F32), (D_MODEL, mx)])

    a_ff, fb = _mm("ff1", h2b, ex.weight("w_ff1"), "nn", bias=sp["b_ff1"],
                   epilogue=lambda r: (r, jnp.square(jnp.maximum(r, 0.0))), out_dtypes=(F32, mx))
    o3 = _mm("ff2", fb, ex.weight("w_ff2"), "nn", bias=sp["b_ff2"])

    def loss_bwd(h2, o3, tgt, g, b):
        def f(h2, o3, g, b):
            h3 = _res_ln(h2, o3, g, b)
            return 0.5 * jnp.sum(jnp.mean(jnp.square(h3 - tgt), axis=-1))

        loss, vjp = jax.vjp(f, h2, o3, g, b)
        _, dr, dg, db = vjp(jnp.ones((), F32))
        return dr, dr, dg, db, _colsum(dr), jnp.full((1, 128), loss, F32)

    dr3, dr3b, g_ln3_g, g_ln3_b, g_b_ff2, loss = _rowwise(
        "loss_ln3_bwd", loss_bwd, [h2, o3, target], [sp["ln3_g"], sp["ln3_b"]],
        [(D_MODEL, F32), (D_MODEL, mx)], [D_MODEL, D_MODEL, D_MODEL, 128])

    dab = _mm("ff2_dx", dr3b, ex.weight("w_ff2"), "nt", extras=(a_ff,),
              epilogue=lambda r, a: (r * (2.0 * jnp.maximum(a, 0.0)),), out_dtypes=(mx,))
    ex.grad("w_ff2", _mm("ff2_dw", fb, dr3b, "tn"))
    g_b_ff1, = _rowwise("ff1_db", lambda v: (_colsum(v),), [dab], [], [], [D_FF])
    ex.grad("w_ff1", _mm("ff1_dw", h2b, dab, "tn", carry=ex.carry(swap=["w_ff2"])))
    dh2 = _mm("ff1_dx", dab, ex.weight("w_ff1"), "nt", extras=(dr3,), epilogue=lambda r, d: (r + al * d,),
              carry=ex.carry(swap=["w_ff1"], ici=["w_ff2"]))

    def ln_bwd(h, o, dout, g, b):
        _, vjp = jax.vjp(_res_ln, h, o, g, b)
        _, dr, dg, db = vjp(dout)
        return dr, dr, dg, db, _colsum(dr)

    dr2, dr2b, g_ln2_g, g_ln2_b, _ = _rowwise(
        "ln2_bwd", ln_bwd, [h1, o2, dh2], [sp["ln2_g"], sp["ln2_b"]],
        [(D_MODEL, F32), (D_MODEL, mx)], [D_MODEL, D_MODEL, D_MODEL])
    ex.grad("w_xo", _mm("xo_dw", oxb, dr2b, "tn"))
    doxb = _mm("xo_dx", dr2b, ex.weight("w_xo"), "nt", out_dtypes=(mx,), carry=ex.carry(swap=["w_xo"]))
    dqxb, dkvx = _xattn_bwd(qx, kvx, doxb)
    ex.grad("w_xq", _mm("xq_dw", h1b, dqxb, "tn", carry=ex.carry(ici=["w_xo"])))
    dh1 = _mm("xq_dx", dqxb, ex.weight("w_xq"), "nt", extras=(dr2,), epilogue=lambda r, d: (r + al * d,),
              carry=ex.carry(swap=["w_xq"]))
    ex.grad("w_xkv", _mm("xkv_dw", mem, dkvx, "tn"))

    dr1, dr1b, g_ln1_g, g_ln1_b, g_b_mix = _rowwise(
        "ln1_bwd", ln_bwd, [h0, o1, dh1], [sp["ln1_g"], sp["ln1_b"]],
        [(D_MODEL, F32), (D_MODEL, mx)], [D_MODEL, D_MODEL, D_MODEL])
    ex.grad("w_mix_out", _mm("mix_dw", mixedb, dr1b, "tn", carry=ex.carry(swap=["w_xkv"], ici=["w_xq"])))
    dmixed = _mm("mix_dx", dr1b, ex.weight("w_mix_out"), "nt", carry=ex.carry(swap=["w_mix_out"]))

    def gate_bwd(gs, ga, z1, z2, batt, dm):
        _, vjp = jax.vjp(_gate, gs, ga, z1, z2, batt)
        dgs, dga, dz1, dz2, dbatt = vjp(dm)
        dz = jnp.concatenate([dz1, dz2], axis=-1)
        return dgs, dga, dz, dbatt, _colsum(dz)

    dgsb, dgab, dzb, dbattb, g_b_glu = _rowwise(
        "gate_bwd", gate_bwd, gate_rows + [dmixed], [],
        [(D_MODEL, mx), (D_MODEL, mx), (2 * D_MODEL, mx), (D_MODEL, mx)], [2 * D_MODEL])
    ex.grad("w_att_up", _mm("att_up_dw", attb, dbattb, "tn", carry=ex.carry(ici=["w_mix_out"])))
    datt = _mm("att_up_dx", dbattb, ex.weight("w_att_up"), "nt", carry=ex.carry(swap=["w_att_up"]))

    def att_delta(datt, att, hs):
        dl = jnp.dot(datt * att, hs, precision=lax.Precision.HIGHEST, preferred_element_type=F32)
        return datt, dl, _stat_rows(dl)

    dattb, delta, delta_t = _rowwise("attn_delta", att_delta, [datt, att], [_head_sum_matrix()],
                                     [(ATT_GROUPW, mx), (ATT_STATW, F32)], touts=[(8, F32)])
    dq_g, dk_g, dv_g = [], [], []
    for g, (tag, per_seq, d) in enumerate(groups):
        do_d, lt_d, dl_d = _dilate(dattb, d), _dilate(lse_tot, d), _dilate(delta, d)
        dq_g.append(_undilate(_attn_dq(tag, per_seq, q_d[g], k_d[g], v_d[g], do_d, lt_d, dl_d), d))
        dk, dv = _attn_dkv(tag, per_seq, q_d[g], k_d[g], v_d[g], do_d, _dilate_rows(lse_tot_t, d), _dilate_rows(delta_t, d))
        dk_g.append(_undilate(dk, d))
        dv_g.append(_undilate(dv, d))
    dqkv = dq_g + dk_g + dv_g

    def rope_bwd(pos, q0, q1, q2, k0, k1, k2, v0, v1, v2, invf, m1, m2):
        tabs = _rope_tables(pos, invf, m1, m2)
        return jnp.concatenate([_rope_t(t, *tabs) for t in (q0, q1, q2, k0, k1, k2)] + [v0, v1, v2], axis=-1)

    dqkvb, = _rowwise("rope_bwd", rope_bwd, [pos] + dqkv, [invf, m1, m2], [(9 * ATT_GROUPW, mx)])

    ex.grad("w_glu", _mm("glu_dw", ygb, dzb, "tn", carry=ex.carry(ici=["w_xkv", "w_att_up"])))
    dyg = _mm("glu_dx", dzb, ex.weight("w_glu"), "nt", carry=ex.carry(swap=["w_glu"]))

    def gelu_bwd(y, dyg):
        _, vjp = jax.vjp(jax.nn.gelu, y)
        return vjp(dyg)[0]

    dy, = _rowwise("gelu_bwd", gelu_bwd, [y, dyg], [], [(SSM_WIDTH, F32)])
    dy_p = _time_perm(dy)
    s_re, s_im, du_p = _ssm_scan("ssm_scan_bwd", dy_p, c12, b12, a_re, a_im, sp["ssm_d"], reverse=True,
                                 carry=ex.carry(ici=["w_ff1", "w_glu"]))
    g_bexp, g_cexp, d_abr, d_abi = _ssm_wgrads(u_p, dy_p, s_re, s_im, h_re, h_im)
    g_ssm_d, = _rowwise("ssm_dd", lambda a, b: (_colsum(a * b),), [dy_p, u_p], [], [], [SSM_WIDTH])
    g_ldt, g_are, g_aim, g_bre, g_bim = _ssm_disc_bwd(
        ldt, are, aim, bre, bim, d_abr.reshape(N_STATE, 1), d_abi.reshape(N_STATE, 1),
        _diag_of_b(g_bexp[:, :, :CH_N]), _diag_of_b(g_bexp[:, :, CH_N:]))
    g_c_re = _diag_of_c(g_cexp[:, :CH_N, :])
    g_c_im = -_diag_of_c(g_cexp[:, CH_N:, :])
    dub = _time_unperm(du_p).astype(mx)

    dprojb = jnp.concatenate([dub, dqkvb, dgsb, dgab], axis=-1)
    g_b_in, = _rowwise("in_db", lambda v: (_colsum(v),), [dprojb], [], [], [IN_COLS])
    ex.grad("w_in", _mm("in_dw", h0b, dprojb, "tn"))
    dh0 = _mm("in_dx", dprojb, ex.weight("w_in"), "nt", extras=(dr1,), epilogue=lambda r, d: (r + al * d,),
              carry=ex.carry(ici=["w_in"]))

    def ln_in_bwd(x, dout, g, b):
        _, vjp = jax.vjp(_ln, x, g, b)
        return vjp(dout)

    dx, g_ln_in_g, g_ln_in_b = _rowwise("ln_in_bwd", ln_in_bwd, [x, dh0], [sp["ln_in_g"], sp["ln_in_b"]],
                                        [(D_MODEL, F32)], [D_MODEL, D_MODEL])

    small = {"ln_in_g": g_ln_in_g, "ln_in_b": g_ln_in_b, "b_in": g_b_in, "ssm_log_dt": g_ldt, "ssm_a_re": g_are,
             "ssm_a_im": g_aim, "ssm_b_re": g_bre, "ssm_b_im": g_bim, "ssm_c_re": g_c_re, "ssm_c_im": g_c_im,
             "ssm_d": g_ssm_d, "b_glu": g_b_glu, "b_mix_out": g_b_mix, "ln1_g": g_ln1_g, "ln1_b": g_ln1_b,
             "ln2_g": g_ln2_g, "ln2_b": g_ln2_b, "b_ff1": g_b_ff1, "b_ff2": g_b_ff2, "ln3_g": g_ln3_g,
             "ln3_b": g_ln3_b}
    return loss, dx, small


def _piece_shape(k, n, axis):
    return (k // 2, n // 4) if axis == 1 else (k // 8, n)


def _aligned(v, m):
    return v if isinstance(v, int) else pl.multiple_of(v, m)


def _full_piece(ref, k, n, axis, chip, half):
    pr, pc = _piece_shape(k, n, axis)
    if axis == 1:
        return ref.at[pl.ds(_aligned(half * pr, 8), pr), pl.ds(_aligned(chip * pc, 128), pc)]
    return ref.at[pl.ds(_aligned(chip * (2 * pr) + half * pr, 8), pr), :]


def _full_shard(ref, k, n, axis, chip):
    if axis == 1:
        return ref.at[:, pl.ds(_aligned(chip * (n // 4), 128), n // 4)]
    return ref.at[pl.ds(_aligned(chip * (k // 4), 8), k // 4), :]


def _shard_piece(ref, k, n, axis, half):
    pr, _ = _piece_shape(k, n, axis)
    return ref.at[pl.ds(_aligned(half * pr, 8), pr), :]


def _mesh_pos():
    x, y, c = lax.axis_index("x"), lax.axis_index("y"), lax.axis_index("c")
    other_chips = [(1 - x, y), (x, 1 - y), (1 - x, 1 - y)]
    return x, y, c, other_chips


def _remote(src, dst, send_sem, recv_sem, dev):
    return pltpu.make_async_remote_copy(src_ref=src, dst_ref=dst, send_sem=send_sem, recv_sem=recv_sem,
                                        device_id=dev, device_id_type=MESH)


def _placed(name, fn, n_steps, where, ins, out_sds, out_block, out_index):
    def body(w_ref, *refs):
        o_ref = refs[-1]
        o_ref[...] = fn(*[r[...] for r in refs[:-1]]).astype(o_ref.dtype)

    grid_spec = pltpu.PrefetchScalarGridSpec(
        num_scalar_prefetch=1, grid=(n_steps,), in_specs=[pl.BlockSpec(bs, idx) for _, bs, idx in ins],
        out_specs=pl.BlockSpec(out_block, out_index))
    return pl.pallas_call(body, name=name, grid_spec=grid_spec, out_shape=out_sds,
                          compiler_params=_cparams(1))(where, *[a for a, _, _ in ins])


def _gather_copies(widx):
    geo = [BIG[i][1:] for i in widx]

    def ici(full, wi, j, chip, send_sems, recv_sems, c, dev):
        k, n, ax = geo[wi]
        piece = _full_piece(full[wi], k, n, ax, chip, c)
        return _remote(piece, piece, send_sems.at[wi * 6 + j], recv_sems.at[wi * 6 + j], dev)

    def d2d(full, wi, j, chip, half, send_sems, recv_sems, sib):
        k, n, ax = geo[wi]
        piece = _full_piece(full[wi], k, n, ax, chip, half)
        return _remote(piece, piece, send_sems.at[wi * 6 + 3 + j], recv_sems.at[wi * 6 + 3 + j], sib)

    def start(_, full, send_sems, recv_sems):
        x, y, c, chips = _mesh_pos()
        for wi in range(len(geo)):
            for j, (qx, qy) in enumerate(chips):
                ici(full, wi, j, 2 * x + y, send_sems, recv_sems, c, (qx, qy, c)).start()

    def finish(_, full, send_sems, recv_sems):
        x, y, c, chips = _mesh_pos()
        sib = (x, y, 1 - c)
        for wi in range(len(geo)):
            for j, (qx, qy) in enumerate(chips):
                ici(full, wi, j, 2 * qx + qy, send_sems, recv_sems, c, (qx, qy, c)).wait_recv()
                d2d(full, wi, j, 2 * qx + qy, c, send_sems, recv_sems, sib).start()
        for wi in range(len(geo)):
            for j, (qx, qy) in enumerate(chips):
                d2d(full, wi, j, 2 * qx + qy, 1 - c, send_sems, recv_sems, sib).wait_recv()
        for wi in range(len(geo)):
            for j, (qx, qy) in enumerate(chips):
                ici(full, wi, j, 2 * x + y, send_sems, recv_sems, c, (qx, qy, c)).wait_send()
                d2d(full, wi, j, 2 * qx + qy, c, send_sems, recv_sems, sib).wait_send()

    return start, finish, 6 * len(geo)


def _gather_weights(tag, fulls, widx):
    nw = len(widx)
    start, finish, n_sems = _gather_copies(widx)

    def body(*refs):
        full = refs[nw:2 * nw]
        start(None, full, *refs[2 * nw:])
        finish(None, full, *refs[2 * nw:])

    return pl.pallas_call(
        body, name="gather_weights_" + tag, in_specs=[HBM_SPEC] * nw, out_specs=[HBM_SPEC] * nw,
        out_shape=[jax.ShapeDtypeStruct(f.shape, f.dtype) for f in fulls],
        input_output_aliases={i: i for i in range(nw)},
        scratch_shapes=[pltpu.SemaphoreType.DMA((n_sems,)), pltpu.SemaphoreType.DMA((n_sems,))])(*fulls)


def _swap_copies(widx):
    geo = [BIG[i][1:] for i in widx]

    def copies(g, got, send_sems, recv_sems, base):
        x, y, c, _ = _mesh_pos()
        return [_remote(_full_piece(g[wi], k, n, ax, q, 1 - c), got[wi].at[q], send_sems.at[base + wi * 4 + q],
                        recv_sems.at[base + wi * 4 + q], (x, y, 1 - c))
                for wi, (k, n, ax) in enumerate(geo) for q in range(4)]

    def start(g, got, send_sems, recv_sems, base=0):
        for cp in copies(g, got, send_sems, recv_sems, base):
            cp.start()

    def finish(g, got, send_sems, recv_sems, base=0):
        for cp in copies(g, got, send_sems, recv_sems, base):
            cp.wait()

    return start, finish, 4 * len(geo)


def _swap_shapes(widx):
    return [jax.ShapeDtypeStruct((4,) + _piece_shape(*BIG[i][1:]), F32) for i in widx]


def _reduce_swap_halves(tag, grads, widx):
    nw = len(widx)
    start, finish, n_sems = _swap_copies(widx)

    def body(*refs):
        start(refs[:nw], refs[nw:2 * nw], *refs[2 * nw:])
        finish(refs[:nw], refs[nw:2 * nw], *refs[2 * nw:])

    return pl.pallas_call(
        body, name="reduce_swap_halves_" + tag, in_specs=[HBM_SPEC] * nw, out_specs=[HBM_SPEC] * nw,
        out_shape=_swap_shapes(widx),
        scratch_shapes=[pltpu.SemaphoreType.DMA((n_sems,)), pltpu.SemaphoreType.DMA((n_sems,))])(*grads)


def _owner_copies(nw):
    def copies(p, out, send_sems, recv_sems, base):
        x, y, c, chips = _mesh_pos()
        return [_remote(p[wi].at[2 * qx + qy], out[wi].at[j], send_sems.at[base + wi * 3 + j],
                        recv_sems.at[base + wi * 3 + j], (qx, qy, c))
                for wi in range(nw) for j, (qx, qy) in enumerate(chips)]

    def start(p, out, send_sems, recv_sems, base=0):
        for cp in copies(p, out, send_sems, recv_sems, base):
            cp.start()

    def finish(p, out, send_sems, recv_sems, base=0):
        for cp in copies(p, out, send_sems, recv_sems, base):
            cp.wait()

    return start, finish, 3 * nw


def _join_carries(a, b):
    if a is None or b is None:
        return a if b is None else b
    n_i, n_o = len(a.ins), len(a.outs)
    outs = list(a.outs) + [o + n_i if isinstance(o, int) else o for o in b.outs]

    def start(c_in, c_out, send_sems, recv_sems):
        a.start(c_in[:n_i], c_out[:n_o], send_sems, recv_sems)
        b.start(c_in[n_i:], c_out[n_o:], send_sems, recv_sems, base=a.n_sems)

    def finish(c_in, c_out, send_sems, recv_sems):
        a.finish(c_in[:n_i], c_out[:n_o], send_sems, recv_sems)
        b.finish(c_in[n_i:], c_out[n_o:], send_sems, recv_sems, base=a.n_sems)

    def done(res):
        a.done(res[:n_o])
        b.done(res[n_o:])

    return _Carry(a.ins + b.ins, outs, a.n_sems + b.n_sems, start, finish, done)


def _share_with_sibling(shards):
    nw = len(BIG)

    def body(*refs):
        out = refs[nw:2 * nw]
        send_sems, recv_sems = refs[2 * nw:]
        x, y, c, _ = _mesh_pos()
        sib = (x, y, 1 - c)
        cps = []
        for wi, (_, k, n, ax) in enumerate(BIG):
            mine = _shard_piece(out[wi], k, n, ax, c)
            cp = _remote(mine, mine, send_sems.at[wi], recv_sems.at[wi], sib)
            cp.start()
            cps.append(cp)
        for wi, (_, k, n, ax) in enumerate(BIG):
            piece = _shard_piece(out[wi], k, n, ax, 1 - c)
            _remote(piece, piece, send_sems.at[wi], recv_sems.at[wi], sib).wait_recv()
        for cp in cps:
            cp.wait_send()

    return pl.pallas_call(
        body, name="share_with_sibling", in_specs=[HBM_SPEC] * nw, out_specs=[HBM_SPEC] * nw,
        out_shape=[jax.ShapeDtypeStruct(sh.shape, sh.dtype) for sh in shards],
        input_output_aliases={i: i for i in range(nw)},
        scratch_shapes=[pltpu.SemaphoreType.DMA((nw,)), pltpu.SemaphoreType.DMA((nw,))])(*shards)


def _allreduce_small(v):
    r = v.shape[0]
    rh = r // 2
    assert rh % 8 == 0

    def body(v_ref, o_ref, sib_buf, chip_buf, send_sems, recv_sems):
        x, y, c, chips = _mesh_pos()
        me = 2 * x + y
        sib = (x, y, 1 - c)
        mine = pl.ds(pl.multiple_of(c * rh, 8), rh)
        other = pl.ds(pl.multiple_of((1 - c) * rh, 8), rh)
        swap = _remote(v_ref.at[other], sib_buf, send_sems.at[0], recv_sems.at[0], sib)
        swap.start()
        swap.wait()
        chip_buf[me] = v_ref[mine, :] + sib_buf[...]
        cps = []
        for j, (qx, qy) in enumerate(chips):
            cp = _remote(chip_buf.at[me], chip_buf.at[me], send_sems.at[1 + j], recv_sems.at[1 + j], (qx, qy, c))
            cp.start()
            cps.append(cp)
        for j, (qx, qy) in enumerate(chips):
            slot = chip_buf.at[2 * qx + qy]
            _remote(slot, slot, send_sems.at[1 + j], recv_sems.at[1 + j], (qx, qy, c)).wait_recv()
        for cp in cps:
            cp.wait_send()
        o_ref[mine, :] = ((chip_buf[0] + chip_buf[1]) + chip_buf[2]) + chip_buf[3]
        back = _remote(o_ref.at[mine], o_ref.at[mine], send_sems.at[4], recv_sems.at[4], sib)
        back.start()
        _remote(o_ref.at[other], o_ref.at[other], send_sems.at[4], recv_sems.at[4], sib).wait_recv()
        back.wait_send()

    return pl.pallas_call(
        body, name="allreduce_small", in_specs=[VMEM_SPEC], out_specs=VMEM_SPEC,
        out_shape=jax.ShapeDtypeStruct((r, 128), F32),
        scratch_shapes=[pltpu.VMEM((rh, 128), F32), pltpu.VMEM((4, rh, 128), F32),
                        pltpu.SemaphoreType.DMA((5,)), pltpu.SemaphoreType.DMA((5,))],
        compiler_params=pltpu.CompilerParams(vmem_limit_bytes=VMEM_LIMIT))(v)


def _as2d(a):
    a = a.reshape((-1, a.shape[-1])) if a.ndim > 1 else a.reshape(1, -1)
    return a


def _adamw_small(quads):
    n = len(quads)

    def body(*refs):
        for i in range(n):
            w, g, m, v = (r[...] for r in refs[4 * i:4 * i + 4])
            for ref, val in zip(refs[4 * n + 3 * i:4 * n + 3 * i + 3], _adamw(w, g, m, v)):
                ref[...] = val

    return pl.pallas_call(
        body, name="adamw_small", in_specs=[VMEM_SPEC] * (4 * n), out_specs=[VMEM_SPEC] * (3 * n),
        out_shape=[jax.ShapeDtypeStruct(q[0].shape, F32) for q in quads for _ in range(3)],
        compiler_params=pltpu.CompilerParams(vmem_limit_bytes=VMEM_LIMIT))(*[a for q in quads for a in q])


def _where():
    return jnp.stack([2 * lax.axis_index("x") + lax.axis_index("y"), lax.axis_index("c")]).astype(jnp.int32)


_BIG_INDEX = {name: i for i, (name, _, _, _) in enumerate(BIG)}


class _LocalWeights:
    def __init__(self, weights):
        self.weights, self.grads = weights, {}

    def gather_now(self, names):
        pass

    def gather_carry(self, names):
        return None

    def weight(self, name):
        return self.weights[name]

    def grad(self, name, g):
        self.grads[name] = g

    def carry(self, swap=(), ici=()):
        return None


class _Exchange:
    def __init__(self, inputs, where):
        self.inputs, self.where = inputs, where
        self.full, self.ready = {}, set()
        self.raw, self.got, self.parts, self.landed, self.geom = {}, {}, {}, {}, {}
        for name, k, n, ax in BIG:
            w2 = inputs[name][0]
            rs, cs = w2.shape
            tm = _tile(rs, 512)
            steps = rs // tm
            if ax == 1:
                blk, idx = (tm, cs), lambda i, w: (i, w[0])
            else:
                blk, idx = (tm, n), functools.partial(lambda i, w, steps: (w[0] * steps + i, 0), steps=steps)
            self.full[name] = _placed("cast_" + name, lambda w: w, steps, where, [(w2, (tm, cs), lambda i, w: (i, 0))],
                                      jax.ShapeDtypeStruct((k, n), MXU_DTYPE), blk, idx)

    def _gathered(self, names, outs):
        for name, o in zip(names, outs):
            self.full[name] = o
            self.ready.add(name)

    def gather_now(self, names):
        self._gathered(names, _gather_weights(names[0], [self.full[n] for n in names], [_BIG_INDEX[n] for n in names]))

    def gather_carry(self, names):
        start, finish, n_sems = _gather_copies([_BIG_INDEX[n] for n in names])
        return _Carry([self.full[n] for n in names], list(range(len(names))), n_sems, start, finish,
                      functools.partial(self._gathered, names))

    def weight(self, name):
        assert name in self.ready, name
        return self.full[name]

    def grad(self, name, g):
        self.raw[name] = g

    def _swapped(self, names, outs):
        for name, o in zip(names, outs):
            self.got[name] = o

    def _pair_sum(self, name):
        i = _BIG_INDEX[name]
        _, k, n, ax = BIG[i]
        g = self.raw[name]
        if name not in self.got:
            self._swapped([name], _reduce_swap_halves(name, [g], [i]))
        got = self.got[name]
        pr, pc = _piece_shape(k, n, ax)
        tm = _tile(pr, 512)
        spp = pr // tm
        self.geom[name] = (pr, pc, tm, spp)
        if ax == 1:
            g_idx = functools.partial(lambda i, w, spp: (w[1] * spp + i % spp, i // spp), spp=spp)
        else:
            g_idx = functools.partial(lambda i, w, spp: ((i // spp) * 2 * spp + w[1] * spp + i % spp, 0), spp=spp)
        self.parts[name] = _placed(
            "pair_sum_" + name, lambda a, b: a + b, 4 * spp, self.where,
            [(g, (tm, pc), g_idx), (got.reshape(4 * pr, pc), (tm, pc), lambda i, w: (i, 0))],
            jax.ShapeDtypeStruct((4 * pr, pc), BF16), (tm, pc), lambda i, w: (i, 0)).reshape(4, pr, pc)

    def _landed(self, names, outs):
        for name, o in zip(names, outs):
            self.landed[name] = o

    def carry(self, swap=(), ici=()):
        first = second = None
        if swap:
            widx = [_BIG_INDEX[n] for n in swap]
            start, finish, n_sems = _swap_copies(widx)
            first = _Carry([self.raw[n] for n in swap], _swap_shapes(widx), n_sems, start, finish,
                           functools.partial(self._swapped, list(swap)))
        if ici:
            for n in ici:
                self._pair_sum(n)
            start, finish, n_sems = _owner_copies(len(ici))
            parts = [self.parts[n] for n in ici]
            outs = [jax.ShapeDtypeStruct((3,) + p.shape[1:], p.dtype) for p in parts]
            second = _Carry(parts, outs, n_sems, start, finish, functools.partial(self._landed, list(ici)))
        return _join_carries(first, second)

    def finish(self):
        halves = []
        for name, _, _, _ in BIG:
            pr, pc, tm, spp = self.geom[name]
            ins = [(self.parts[name], (None, tm, pc), lambda i, w: (w[0], i, 0))]
            ins += [(self.landed[name], (None, tm, pc), functools.partial(lambda i, w, j: (j, i, 0), j=j))
                    for j in range(3)]
            halves.append(_placed("chip_sum_" + name,
                                  lambda a, b, c, d: ((a.astype(F32) + b.astype(F32)) + c.astype(F32)) + d.astype(F32),
                                  spp, self.where, ins, jax.ShapeDtypeStruct(self.inputs[name].shape[1:], F32), (tm, pc),
                                  functools.partial(lambda i, w, spp: (w[1] * spp + i, 0), spp=spp)))
        return dict(zip([b[0] for b in BIG], _share_with_sibling(halves)))


def _step(inputs):
    x, mem, positions, target = inputs["x"][0], inputs["mem"][0], inputs["positions"], inputs["loss_target"][0]
    pos = positions.reshape(-1, 1)
    ex = _Exchange(inputs, _where())
    sp = {name: _as2d(inputs[name]) for name in SMALL}
    memb, = _rowwise("cast_mem", lambda m: (m,), [mem], [], [(D_MODEL, MXU_DTYPE)])

    loss, dx, gsmall = _local_step(x, memb, pos, target, sp, ex)
    gshard = ex.finish()

    out = {}
    for name, _, _, _ in BIG:
        w2, m2, v2 = inputs[name][0], inputs["m_" + name][0], inputs["v_" + name][0]
        n = w2.shape[1]
        d, nm, nv = _rowwise("adamw_" + name, _adamw, [w2, gshard[name], m2, v2], [], [(n, F32)] * 3, tm=256)
        lead = inputs[name].shape
        out[name] = (gshard[name].reshape(lead), d.reshape(lead), nm.reshape(lead), nv.reshape(lead))

    def tiles(a):
        flat = a.reshape(-1)
        n = -(-flat.shape[0] // 1024) * 1024
        return jnp.pad(flat, (0, n - flat.shape[0])).reshape(n // 128, 128)

    pieces = [tiles(loss[:, :1])] + [tiles(gsmall[name]) for name in SMALL]
    if sum(p.shape[0] for p in pieces) % 16:
        pieces.append(jnp.zeros((8, 128), F32))
    red = _allreduce_small(jnp.concatenate(pieces, axis=0))
    loss_total = red[0, 0]
    grads, off = {}, pieces[0].shape[0]
    for name, p in zip(SMALL, pieces[1:]):
        shp = _as2d(inputs[name]).shape
        grads[name] = red[off:off + p.shape[0]].reshape(-1)[:shp[0] * shp[1]].reshape(shp)
        off += p.shape[0]
    upd = _adamw_small([(_as2d(inputs[n]), grads[n], _as2d(inputs["m_" + n]), _as2d(inputs["v_" + n])) for n in SMALL])
    for i, name in enumerate(SMALL):
        shp = inputs[name].shape
        out[name] = (grads[name].reshape(shp),) + tuple(t.reshape(shp) for t in upd[3 * i:3 * i + 3])
    return loss_total, dx.reshape(inputs["x"].shape), out


_ARG_NAMES = (("x", "mem", "positions") + WEIGHT_ORDER + ("loss_target",) + tuple("m_" + n for n in WEIGHT_ORDER)
              + tuple("v_" + n for n in WEIGHT_ORDER))


def kernel(x, mem, positions, ln_in_g, ln_in_b, w_in, b_in, ssm_log_dt, ssm_a_re, ssm_a_im, ssm_b_re, ssm_b_im, ssm_c_re, ssm_c_im, ssm_d, w_glu, b_glu, w_att_up, w_mix_out, b_mix_out, ln1_g, ln1_b, w_xq, w_xkv, w_xo, ln2_g, ln2_b, w_ff1, b_ff1, w_ff2, b_ff2, ln3_g, ln3_b, loss_target, m_ln_in_g, m_ln_in_b, m_w_in, m_b_in, m_ssm_log_dt, m_ssm_a_re, m_ssm_a_im, m_ssm_b_re, m_ssm_b_im, m_ssm_c_re, m_ssm_c_im, m_ssm_d, m_w_glu, m_b_glu, m_w_att_up, m_w_mix_out, m_b_mix_out, m_ln1_g, m_ln1_b, m_w_xq, m_w_xkv, m_w_xo, m_ln2_g, m_ln2_b, m_w_ff1, m_b_ff1, m_w_ff2, m_b_ff2, m_ln3_g, m_ln3_b, v_ln_in_g, v_ln_in_b, v_w_in, v_b_in, v_ssm_log_dt, v_ssm_a_re, v_ssm_a_im, v_ssm_b_re, v_ssm_b_im, v_ssm_c_re, v_ssm_c_im, v_ssm_d, v_w_glu, v_b_glu, v_w_att_up, v_w_mix_out, v_b_mix_out, v_ln1_g, v_ln1_b, v_w_xq, v_w_xkv, v_w_xo, v_ln2_g, v_ln2_b, v_w_ff1, v_b_ff1, v_w_ff2, v_b_ff2, v_ln3_g, v_ln3_b):
    args = (x, mem, positions, ln_in_g, ln_in_b, w_in, b_in, ssm_log_dt, ssm_a_re, ssm_a_im, ssm_b_re, ssm_b_im, ssm_c_re, ssm_c_im, ssm_d, w_glu, b_glu, w_att_up, w_mix_out, b_mix_out, ln1_g, ln1_b, w_xq, w_xkv, w_xo, ln2_g, ln2_b, w_ff1, b_ff1, w_ff2, b_ff2, ln3_g, ln3_b, loss_target, m_ln_in_g, m_ln_in_b, m_w_in, m_b_in, m_ssm_log_dt, m_ssm_a_re, m_ssm_a_im, m_ssm_b_re, m_ssm_b_im, m_ssm_c_re, m_ssm_c_im, m_ssm_d, m_w_glu, m_b_glu, m_w_att_up, m_w_mix_out, m_b_mix_out, m_ln1_g, m_ln1_b, m_w_xq, m_w_xkv, m_w_xo, m_ln2_g, m_ln2_b, m_w_ff1, m_b_ff1, m_w_ff2, m_b_ff2, m_ln3_g, m_ln3_b, v_ln_in_g, v_ln_in_b, v_w_in, v_b_in, v_ssm_log_dt, v_ssm_a_re, v_ssm_a_im, v_ssm_b_re, v_ssm_b_im, v_ssm_c_re, v_ssm_c_im, v_ssm_d, v_w_glu, v_b_glu, v_w_att_up, v_w_mix_out, v_b_mix_out, v_ln1_g, v_ln1_b, v_w_xq, v_w_xkv, v_w_xo, v_ln2_g, v_ln2_b, v_w_ff1, v_b_ff1, v_w_ff2, v_b_ff2, v_ln3_g, v_ln3_b)
    assert len(args) == len(_ARG_NAMES)
    inputs = dict(zip(_ARG_NAMES, args))
    loss, dx, out = _step(inputs)
    res = [loss, dx]
    for k in range(4):
        res += [out[name][k] for name in WEIGHT_ORDER]
    return tuple(res)
```

```python
import functools
import math

import numpy as np
import jax
import jax.numpy as jnp
from jax import lax
from jax.experimental import pallas as pl
from jax.experimental.pallas import tpu as pltpu

F32 = jnp.float32
BF16 = jnp.bfloat16
MXU_DTYPE = jnp.bfloat16

D_MODEL = 1024
SSM_GROUP = 16
SSM_WIDTH = 768
SSM_GROUPS = 48
SSM_STATE = 64
N_STATE = SSM_GROUPS * SSM_STATE
SSM_CHUNKS = 6
CH_W = 128
CH_N = 512
ATT_HEAD_DIM = 64
ATT_HPG = 4
ATT_GROUPW = ATT_HPG * ATT_HEAD_DIM
DILATIONS = (1, 4, 16)
ATT_BLK = 128
ATT_SCALE = ATT_HEAD_DIM ** -0.5
ROT_DIM = 16
ROPE_THETA = 500000.0
XATT_HEADS = 4
XATT_HEAD_DIM = 256
XATT_SCALE = XATT_HEAD_DIM ** -0.5
D_FF = 4096
IN_COLS = 5120
DEEPNORM_ALPHA = 2.0 ** 0.25
LN_EPS = 1e-5
NEG_INF = -1e30
ADAM_LR = 0.001
ADAM_B1 = 0.9
ADAM_B2 = 0.999
ADAM_EPS = 1e-08
ADAM_WD = 0.01
ADAM_STEP = 10

N_SEG = 32
VMEM_LIMIT = 56 * 1024 * 1024
MESH = pl.DeviceIdType.MESH
HBM_SPEC = pl.BlockSpec(memory_space=pltpu.HBM)
VMEM_SPEC = pl.BlockSpec(memory_space=pltpu.VMEM)

BIG = (("w_in", 1024, 5120, 1), ("w_glu", 768, 2048, 1), ("w_att_up", 256, 1024, 1),
       ("w_mix_out", 1024, 1024, 0), ("w_xq", 1024, 1024, 0), ("w_xkv", 1024, 2048, 1),
       ("w_xo", 1024, 1024, 0), ("w_ff1", 1024, 4096, 1), ("w_ff2", 4096, 1024, 0))
SMALL = ("ln_in_g", "ln_in_b", "b_in", "ssm_log_dt", "ssm_a_re", "ssm_a_im", "ssm_b_re", "ssm_b_im",
         "ssm_c_re", "ssm_c_im", "ssm_d", "b_glu", "b_mix_out", "ln1_g", "ln1_b", "ln2_g", "ln2_b",
         "b_ff1", "b_ff2", "ln3_g", "ln3_b")
WEIGHT_ORDER = ("ln_in_g", "ln_in_b", "w_in", "b_in", "ssm_log_dt", "ssm_a_re", "ssm_a_im", "ssm_b_re",
                "ssm_b_im", "ssm_c_re", "ssm_c_im", "ssm_d", "w_glu", "b_glu", "w_att_up", "w_mix_out",
                "b_mix_out", "ln1_g", "ln1_b", "w_xq", "w_xkv", "w_xo", "ln2_g", "ln2_b", "w_ff1", "b_ff1",
                "w_ff2", "b_ff2", "ln3_g", "ln3_b")


def _cparams(n_axes):
    return pltpu.CompilerParams(dimension_semantics=("arbitrary",) * n_axes, vmem_limit_bytes=VMEM_LIMIT)


class _Carry:
    def __init__(self, ins, outs, n_sems, start, finish, done):
        self.ins, self.outs, self.n_sems, self.start, self.finish, self.done = ins, outs, n_sems, start, finish, done


def _call(name, body, grid, in_specs, out_specs, out_shape, args, scratch_shapes=(), carry=None):
    in_specs, out_specs, out_shape = list(in_specs), list(out_specs), list(out_shape)
    params = _cparams(len(grid))
    if carry is None:
        return pl.pallas_call(body, name=name, grid=grid, in_specs=in_specs, out_specs=out_specs, out_shape=out_shape,
                              scratch_shapes=list(scratch_shapes), compiler_params=params)(*args)
    n_in, n_out, n_ci, n_co = len(in_specs), len(out_specs), len(carry.ins), len(carry.outs)
    n_scr = len(scratch_shapes)

    def wrapped(*refs):
        ins, c_in = refs[:n_in], refs[n_in:n_in + n_ci]
        outs, c_out = refs[n_in + n_ci:n_in + n_ci + n_out], refs[n_in + n_ci + n_out:n_in + n_ci + n_out + n_co]
        scratch = refs[n_in + n_ci + n_out + n_co:n_in + n_ci + n_out + n_co + n_scr]
        send_sems, recv_sems = refs[-2:]
        ids = [pl.program_id(a) for a in range(len(grid))]
        first = functools.reduce(jnp.logical_and, [i == 0 for i in ids])
        last = functools.reduce(jnp.logical_and, [i == g - 1 for i, g in zip(ids, grid)])

        @pl.when(first)
        def _():
            carry.start(c_in, c_out, send_sems, recv_sems)

        body(*ins, *outs, *scratch)

        @pl.when(last)
        def _():
            carry.finish(c_in, c_out, send_sems, recv_sems)

    c_shapes = [jax.ShapeDtypeStruct(carry.ins[o].shape, carry.ins[o].dtype) if isinstance(o, int) else o
                for o in carry.outs]
    aliases = {n_in + o: n_out + i for i, o in enumerate(carry.outs) if isinstance(o, int)}
    res = pl.pallas_call(
        wrapped, name=name, grid=grid, in_specs=in_specs + [HBM_SPEC] * n_ci, out_specs=out_specs + [HBM_SPEC] * n_co,
        out_shape=out_shape + c_shapes, input_output_aliases=aliases,
        scratch_shapes=list(scratch_shapes) + [pltpu.SemaphoreType.DMA((carry.n_sems,))] * 2,
        compiler_params=params)(*args, *carry.ins)
    carry.done(res[n_out:])
    return res[:n_out]


def _rowwise(name, fn, rows, consts, outs, reds=(), tm=256, touts=(), carry=None):
    n_rows = (rows[0][0] if isinstance(rows[0], tuple) else rows[0]).shape[-2]
    tm = min(tm, n_rows)
    assert n_rows % tm == 0, (name, n_rows, tm)
    specs, args = [], []
    for r in rows:
        if isinstance(r, tuple) and len(r) == 3:
            arr, width, cb = r
            specs.append(pl.BlockSpec((tm, width), functools.partial(lambda i, cb: (i, cb), cb=cb)))
        elif isinstance(r, tuple):
            arr, slot = r
            specs.append(pl.BlockSpec((None, tm, arr.shape[2]), functools.partial(lambda i, s: (s, i, 0), s=slot)))
        else:
            arr = r
            specs.append(pl.BlockSpec((tm, arr.shape[1]), lambda i: (i, 0)))
        args.append(arr)
        assert arr.shape[-2] == n_rows, (name, arr.shape, n_rows)
    for cst in consts:
        specs.append(pl.BlockSpec(cst.shape, lambda i: (0, 0)))
        args.append(cst)
    n_r, n_c, n_o, n_d = len(rows), len(consts), len(outs) + len(touts), len(reds)
    out_shape = [jax.ShapeDtypeStruct((n_rows, c), dt) for c, dt in outs]
    out_specs = [pl.BlockSpec((tm, c), lambda i: (i, 0)) for c, _ in outs]
    out_shape += [jax.ShapeDtypeStruct((r, n_rows), dt) for r, dt in touts]
    out_specs += [pl.BlockSpec((r, tm), lambda i: (0, i)) for r, _ in touts]
    out_shape += [jax.ShapeDtypeStruct((1, c), F32) for c in reds]
    out_specs += [pl.BlockSpec((1, c), lambda i: (0, 0)) for c in reds]

    def body(*refs):
        ins = [r[...] for r in refs[:n_r + n_c]]
        o_refs = refs[n_r + n_c:n_r + n_c + n_o]
        d_refs = refs[n_r + n_c + n_o:]
        res = fn(*ins)
        res = res if isinstance(res, (tuple, list)) else (res,)
        assert len(res) == n_o + n_d, (name, len(res))
        for ref, val in zip(o_refs, res[:n_o]):
            ref[...] = val.astype(ref.dtype)
        first = pl.program_id(0) == 0
        for ref, val in zip(d_refs, res[n_o:]):
            @pl.when(first)
            def _(ref=ref, val=val):
                ref[...] = val

            @pl.when(jnp.logical_not(first))
            def _(ref=ref, val=val):
                ref[...] += val

    return _call(name, body, (n_rows // tm,), specs, out_specs, out_shape, args, carry=carry)


def _colsum(v):
    return jnp.sum(v.astype(F32), axis=0, keepdims=True)


_DIMS = {"nn": (((1,), (0,)), ((), ())), "nt": (((1,), (1,)), ((), ())), "tn": (((0,), (0,)), ((), ()))}


def _tile(dim, want):
    if dim <= want:
        return dim
    return max(t for t in range(128, want + 1, 128) if dim % t == 0)


def _dot(a, b, mode):
    return lax.dot_general(a.astype(MXU_DTYPE), b.astype(MXU_DTYPE), _DIMS[mode], preferred_element_type=F32)


def _mm(name, a, b, mode, *, bias=None, extras=(), epilogue=None, out_dtypes=(F32,), tm=1024, tn=1024, tk=1024,
        carry=None):
    if mode == "nn":
        (m, k), (_, n) = a.shape, b.shape
    elif mode == "nt":
        (m, k), (n, _) = a.shape, b.shape
    else:
        (k, m), (_, n) = a.shape, b.shape
    if k > tk:
        tk = 5 * tk
    tn = _tile(n, tn)
    tk = _tile(k, tk)
    nk = k // tk

    def vmem_bytes(rows):
        blocks = rows * tk * a.dtype.itemsize + tk * tn * b.dtype.itemsize
        blocks += sum(rows * tn * e.dtype.itemsize for e in extras)
        blocks += sum(rows * tn * jnp.dtype(dt).itemsize for dt in out_dtypes)
        return 2 * blocks + (rows * tn * 4 if nk > 1 else 0)

    tm = _tile(m, tm if mode == "tn" else 2 * tm)
    while vmem_bytes(tm) > 3 * VMEM_LIMIT // 4 and tm % 256 == 0:
        tm //= 2
    while nk == 1 and k > 1024 and (m // tm) * (n // tn) < 4 and tm % 256 == 0:
        tm //= 2
    assert m % tm == 0 and n % tn == 0 and k % tk == 0, (name, m, n, k)
    a_spec = {"nn": pl.BlockSpec((tm, tk), lambda i, j, kk: (i, kk)),
              "nt": pl.BlockSpec((tm, tk), lambda i, j, kk: (i, kk)),
              "tn": pl.BlockSpec((tk, tm), lambda i, j, kk: (kk, i))}[mode]
    b_spec = {"nn": pl.BlockSpec((tk, tn), lambda i, j, kk: (kk, j)),
              "nt": pl.BlockSpec((tn, tk), lambda i, j, kk: (j, kk)),
              "tn": pl.BlockSpec((tk, tn), lambda i, j, kk: (kk, j))}[mode]
    specs, args = [a_spec, b_spec], [a, b]
    if bias is not None:
        specs.append(pl.BlockSpec((1, tn), lambda i, j, kk: (0, j)))
        args.append(bias)
    for e in extras:
        specs.append(pl.BlockSpec((tm, tn), lambda i, j, kk: (i, j)))
        args.append(e)
    n_e, n_o = len(extras), len(out_dtypes)
    has_bias = bias is not None

    def body(*refs):
        a_ref, b_ref = refs[0], refs[1]
        pos = 2
        bias_ref = refs[pos] if has_bias else None
        pos += int(has_bias)
        e_refs = refs[pos:pos + n_e]
        o_refs = refs[pos + n_e:pos + n_e + n_o]
        acc_ref = refs[pos + n_e + n_o] if nk > 1 else None
        part = _dot(a_ref[...], b_ref[...], mode)

        def finish(r):
            if has_bias:
                r = r + bias_ref[...]
            res = epilogue(r, *[e[...] for e in e_refs]) if epilogue is not None else (r,)
            for ref, val in zip(o_refs, res):
                ref[...] = val.astype(ref.dtype)

        if nk == 1:
            finish(part)
        else:
            kk = pl.program_id(2)

            @pl.when(kk == 0)
            def _():
                acc_ref[...] = part

            @pl.when(kk > 0)
            def _():
                acc_ref[...] += part

            @pl.when(kk == nk - 1)
            def _():
                finish(acc_ref[...])

    res = _call(name, body, (m // tm, n // tn, nk), specs,
                [pl.BlockSpec((tm, tn), lambda i, j, kk: (i, j)) for _ in out_dtypes],
                [jax.ShapeDtypeStruct((m, n), dt) for dt in out_dtypes], args,
                scratch_shapes=[pltpu.VMEM((tm, tn), F32)] if nk > 1 else [], carry=carry)
    return res[0] if n_o == 1 else res


def _ssm_wgrads(u, dy, g_re, g_im, h_re, h_im, tk=1024):
    s = u.shape[0]
    tk = min(tk, s)
    nk = s // tk
    assert tk % N_SEG == 0

    def body(u_ref, dy_ref, gre_ref, gim_ref, hre_ref, him_ref, lre_ref, lim_ref, db_ref, dc_ref, dar_ref, dai_ref,
             pre_ref, pim_ref):
        kk = pl.program_id(1)
        u_blk, dy_blk = u_ref[...], dy_ref[...]
        g_r, g_i, h_r, h_i = gre_ref[...], gim_ref[...], hre_ref[...], him_ref[...]
        d_b = jnp.concatenate([_dot(u_blk, g_r, "tn"), _dot(u_blk, g_i, "tn")], axis=1)
        d_c = jnp.concatenate([_dot(h_r, dy_blk, "tn"), _dot(h_i, dy_blk, "tn")], axis=0)

        @pl.when(kk == 0)
        def _():
            first_row = lax.broadcasted_iota(jnp.int32, (N_SEG, CH_N), 0) == 0
            pre_ref[...] = jnp.where(first_row, 0.0, pltpu.roll(lre_ref[...], 1, 0))
            pim_ref[...] = jnp.where(first_row, 0.0, pltpu.roll(lim_ref[...], 1, 0))

        p_r = jnp.concatenate([pre_ref[...], h_r[:tk - N_SEG]], axis=0)
        p_i = jnp.concatenate([pim_ref[...], h_i[:tk - N_SEG]], axis=0)
        pre_ref[...] = h_r[tk - N_SEG:]
        pim_ref[...] = h_i[tk - N_SEG:]
        d_ar = jnp.sum(g_r * p_r + g_i * p_i, axis=0, keepdims=True)
        d_ai = jnp.sum(g_i * p_r - g_r * p_i, axis=0, keepdims=True)

        @pl.when(kk == 0)
        def _():
            db_ref[...] = d_b
            dc_ref[...] = d_c
            dar_ref[...] = d_ar
            dai_ref[...] = d_ai

        @pl.when(kk > 0)
        def _():
            db_ref[...] += d_b
            dc_ref[...] += d_c
            dar_ref[...] += d_ar
            dai_ref[...] += d_ai

    chan = pl.BlockSpec((tk, CH_W), lambda j, kk: (kk, j))
    state = pl.BlockSpec((tk, CH_N), lambda j, kk: (kk, j))
    last = pl.BlockSpec((N_SEG, CH_N), lambda j, kk: (s // N_SEG - 1, j))
    row = pl.BlockSpec((1, CH_N), lambda j, kk: (0, j))
    return pl.pallas_call(
        body, name="ssm_wgrads", grid=(SSM_CHUNKS, nk),
        in_specs=[chan, chan, state, state, state, state, last, last],
        out_specs=[pl.BlockSpec((None, CH_W, 2 * CH_N), lambda j, kk: (j, 0, 0)),
                   pl.BlockSpec((None, 2 * CH_N, CH_W), lambda j, kk: (j, 0, 0)), row, row],
        out_shape=[jax.ShapeDtypeStruct((SSM_CHUNKS, CH_W, 2 * CH_N), F32),
                   jax.ShapeDtypeStruct((SSM_CHUNKS, 2 * CH_N, CH_W), F32),
                   jax.ShapeDtypeStruct((1, N_STATE), F32), jax.ShapeDtypeStruct((1, N_STATE), F32)],
        scratch_shapes=[pltpu.VMEM((N_SEG, CH_N), F32)] * 2,
        compiler_params=_cparams(2))(u, dy, g_re, g_im, h_re, h_im, h_re, h_im)


SCAN_LB = 256


def _split_by_scan_block(mat, axis):
    halves = []
    for l in range(CH_N // SCAN_LB):
        re = lax.slice_in_dim(mat, l * SCAN_LB, (l + 1) * SCAN_LB, axis=axis)
        im = lax.slice_in_dim(mat, CH_N + l * SCAN_LB, CH_N + (l + 1) * SCAN_LB, axis=axis)
        halves.append(jnp.concatenate([re, im], axis=axis))
    return jnp.stack(halves, axis=1).reshape((-1,) + halves[0].shape[1:])


def _ssm_scan(name, chan, expand12, contract12, a_re, a_im, d_row, reverse, carry=None):
    s = chan.shape[0]
    seg_len = s // N_SEG
    n_sq = int(math.log2(seg_len))
    assert 2 ** n_sq == seg_len
    rb = min(512, s)
    per_chunk = CH_N // SCAN_LB

    def steps_of(r):
        return range(r * rb // N_SEG, (r + 1) * rb // N_SEG)

    def body(are_ref, aim_ref, ch_ref, e_ref, k_ref, d_ref, hre_ref, him_ref, o_ref, cp_ref, wre_ref, wim_ref,
             ere, eim, cre, cim):
        e_mat, k_mat = e_ref[...], k_ref[...]
        for r in range(s // rb):
            rows = slice(r * rb, (r + 1) * rb)
            c = jnp.concatenate([ch_ref[pl.ds(k, N_SEG, stride=seg_len), :] for k in steps_of(r)], axis=0)
            cp_ref[rows, :] = c
            w = _dot(c, e_mat, "nt" if reverse else "nn")
            wre_ref[rows, :] = w[:, :SCAN_LB]
            wim_ref[rows, :] = w[:, SCAN_LB:]

        ar1 = are_ref[...]
        ai1 = -aim_ref[...] if reverse else aim_ref[...]
        ar = jnp.broadcast_to(ar1, (N_SEG, SCAN_LB))
        ai = jnp.broadcast_to(ai1, (N_SEG, SCAN_LB))

        def rows_of(k):
            kk = seg_len - 1 - k if reverse else k
            return pl.ds(pl.multiple_of(kk * N_SEG, N_SEG), N_SEG)

        def local(k, carry):
            hr, hi = carry
            rows = rows_of(k)
            nr = ar * hr - ai * hi + wre_ref[rows, :]
            ni = ar * hi + ai * hr + wim_ref[rows, :]
            hre_ref[rows, :] = nr
            him_ref[rows, :] = ni
            return nr, ni

        zero = jnp.zeros((N_SEG, SCAN_LB), F32)
        er, ei = lax.fori_loop(0, seg_len, local, (zero, zero))
        ere[...] = er
        eim[...] = ei
        pr, pi = ar1, ai1
        for _ in range(n_sq):
            pr, pi = pr * pr - pi * pi, 2.0 * pr * pi
        cr = jnp.zeros((1, SCAN_LB), F32)
        ci = jnp.zeros((1, SCAN_LB), F32)
        for jj in range(N_SEG):
            j = N_SEG - 1 - jj if reverse else jj
            cre[j:j + 1, :] = cr
            cim[j:j + 1, :] = ci
            er_j, ei_j = ere[j:j + 1, :], eim[j:j + 1, :]
            cr, ci = pr * cr - pi * ci + er_j, pr * ci + pi * cr + ei_j
        c_r, c_i = cre[...], cim[...]

        def fix(k, carry):
            qr, qi = carry
            rows = rows_of(k)
            hre_ref[rows, :] = hre_ref[rows, :] + (qr * c_r - qi * c_i)
            him_ref[rows, :] = him_ref[rows, :] + (qr * c_i + qi * c_r)
            return qr * ar - qi * ai, qr * ai + qi * ar

        lax.fori_loop(0, seg_len, fix, (ar, ai))

        first_of_chunk = lax.rem(pl.program_id(0), per_chunk) == 0
        for r in range(s // rb):
            rows = slice(r * rb, (r + 1) * rb)
            h_cat = jnp.concatenate([hre_ref[rows, :], him_ref[rows, :]], axis=1)
            part = _dot(h_cat, k_mat, "nt" if reverse else "nn")

            @pl.when(first_of_chunk)
            def _(rows=rows, part=part, r=r):
                full = part + d_ref[...] * cp_ref[rows, :]
                for i, k in enumerate(steps_of(r)):
                    o_ref[pl.ds(k, N_SEG, stride=seg_len), :] = full[i * N_SEG:(i + 1) * N_SEG]

            @pl.when(jnp.logical_not(first_of_chunk))
            def _(part=part, r=r):
                for i, k in enumerate(steps_of(r)):
                    at = pl.ds(k, N_SEG, stride=seg_len)
                    o_ref[at, :] = o_ref[at, :] + part[i * N_SEG:(i + 1) * N_SEG]

    nblk = N_STATE // SCAN_LB
    blk = pl.BlockSpec((s, SCAN_LB), lambda b: (0, b))
    row = pl.BlockSpec((1, SCAN_LB), lambda b: (0, b))
    chan_blk = pl.BlockSpec((s, CH_W), lambda b: (0, b // per_chunk))
    res = _call(name, body, (nblk,),
                [row, row, chan_blk, pl.BlockSpec((None,) + expand12.shape[1:], lambda b: (b, 0, 0)),
                 pl.BlockSpec((None,) + contract12.shape[1:], lambda b: (b, 0, 0)),
                 pl.BlockSpec((1, CH_W), lambda b: (0, b // per_chunk))],
                [blk, blk, chan_blk, chan_blk],
                [jax.ShapeDtypeStruct((s, N_STATE), F32)] * 2 + [jax.ShapeDtypeStruct((s, SSM_WIDTH), F32)] * 2,
                (a_re, a_im, chan, expand12, contract12, d_row),
                scratch_shapes=[pltpu.VMEM((s, SCAN_LB), F32)] * 2 + [pltpu.VMEM((N_SEG, SCAN_LB), F32)] * 4, carry=carry)
    return res[0], res[1], res[2], res[3]


def _disc(ldt, are, aim, bre, bim):
    dt = jnp.exp(ldt)
    mag = jnp.exp(are * dt)
    abr = mag * jnp.cos(aim * dt)
    abi = mag * jnp.sin(aim * dt)
    den = jnp.square(are) + jnp.square(aim)
    nr = abr - 1.0
    fre = (nr * are + abi * aim) / den
    fim = (abi * are - nr * aim) / den
    return abr, abi, fre * bre - fim * bim, fre * bim + fim * bre


def _ssm_disc_fwd(ldt, are, aim, bre, bim):
    def body(l_ref, ar_ref, ai_ref, br_ref, bi_ref, o0, o1, o2, o3):
        res = _disc(l_ref[...], ar_ref[...], ai_ref[...], br_ref[...], bi_ref[...])
        for ref, val in zip((o0, o1, o2, o3), res):
            ref[...] = val

    col = jax.ShapeDtypeStruct((N_STATE, 1), F32)
    mat = jax.ShapeDtypeStruct((N_STATE, SSM_GROUP), F32)
    return pl.pallas_call(body, name="ssm_disc_fwd", out_shape=[col, col, mat, mat],
                          in_specs=[VMEM_SPEC] * 5, out_specs=[VMEM_SPEC] * 4)(ldt, are, aim, bre, bim)


def _ssm_disc_bwd(ldt, are, aim, bre, bim, d_abr, d_abi, d_bbr, d_bbi):
    def body(l_ref, ar_ref, ai_ref, br_ref, bi_ref, c0, c1, c2, c3, g_ldt, g_are, g_aim, g_bre, g_bim):
        _, vjp = jax.vjp(_disc, l_ref[...], ar_ref[...], ai_ref[...], br_ref[...], bi_ref[...])
        dl, dar, dai, dbr, dbi = vjp((c0[...], c1[...], c2[...], c3[...]))
        state = lax.broadcasted_iota(jnp.int32, (N_STATE, SSM_GROUPS), 0)
        group = lax.broadcasted_iota(jnp.int32, (N_STATE, SSM_GROUPS), 1)
        pick = jnp.right_shift(state, 6) == group
        g_ldt[...] = jnp.sum(jnp.where(pick, dl, 0.0), axis=0, keepdims=True)
        g_are[...] = dar
        g_aim[...] = dai
        g_bre[...] = dbr
        g_bim[...] = dbi

    col = jax.ShapeDtypeStruct((N_STATE, 1), F32)
    mat = jax.ShapeDtypeStruct((N_STATE, SSM_GROUP), F32)
    return pl.pallas_call(body, name="ssm_disc_bwd",
                          out_shape=[jax.ShapeDtypeStruct((1, SSM_GROUPS), F32), col, col, mat, mat],
                          in_specs=[VMEM_SPEC] * 9, out_specs=[VMEM_SPEC] * 5,
                          compiler_params=pltpu.CompilerParams(vmem_limit_bytes=VMEM_LIMIT))(
        ldt, are, aim, bre, bim, d_abr, d_abi, d_bbr, d_bbi)


_EYE8 = np.eye(8, dtype=np.float32)


def _blockdiag_b(bb):
    t = bb.reshape(SSM_CHUNKS, 8, SSM_STATE, SSM_GROUP).transpose(0, 1, 3, 2)
    return jnp.einsum("igcn,gh->igchn", t, _EYE8).reshape(SSM_CHUNKS, CH_W, CH_N)


def _diag_of_b(m):
    t = jnp.einsum("igchn,gh->igcn", m.reshape(SSM_CHUNKS, 8, SSM_GROUP, 8, SSM_STATE), _EYE8)
    return t.transpose(0, 1, 3, 2).reshape(N_STATE, SSM_GROUP)


def _blockdiag_c(c):
    t = c.reshape(SSM_CHUNKS, 8, SSM_GROUP, SSM_STATE).transpose(0, 1, 3, 2)
    return jnp.einsum("ignc,gh->ignhc", t, _EYE8).reshape(SSM_CHUNKS, CH_N, CH_W)


def _diag_of_c(m):
    t = jnp.einsum("ignhc,gh->ignc", m.reshape(SSM_CHUNKS, 8, SSM_STATE, 8, SSM_GROUP), _EYE8)
    return t.transpose(0, 1, 3, 2).reshape(SSM_GROUPS, SSM_GROUP, SSM_STATE)


def _dilate(a, d):
    s, c = a.shape
    return a if d == 1 else a.reshape(s // d, d, c).transpose(1, 0, 2).reshape(s, c)


def _undilate(a, d):
    s, c = a.shape
    return a if d == 1 else a.reshape(d, s // d, c).transpose(1, 0, 2).reshape(s, c)


def _dilate_rows(a, d):
    r, s = a.shape
    return a if d == 1 else a.reshape(r, s // d, d).transpose(0, 2, 1).reshape(r, s)


ATT_T = 4
ATT_ROWS = ATT_T * ATT_BLK


def _window(prev_ref, cur_ref, i, sl):
    if i == 0:
        return jnp.concatenate([prev_ref[:, sl], cur_ref[0:ATT_BLK, sl]], axis=0)
    return cur_ref[(i - 1) * ATT_BLK:(i + 1) * ATT_BLK, sl]


def _band_valid(first_key):
    qi = lax.broadcasted_iota(jnp.int32, (ATT_BLK, 2 * ATT_BLK), 0)
    ki = lax.broadcasted_iota(jnp.int32, (ATT_BLK, 2 * ATT_BLK), 1)
    steps = qi + ATT_BLK - ki
    return (steps >= 0) & (steps <= ATT_BLK) & (ki >= first_key)


ATT_STATW = ATT_HPG * 128


def _stat(h):
    return slice(h * 128, (h + 1) * 128)


def _stat_rows(stat):
    n = stat.shape[0]
    heads = [stat[:, _stat(h)].T[0:1, :] for h in range(ATT_HPG)]
    return jnp.concatenate(heads + [jnp.zeros((8 - ATT_HPG, n), stat.dtype)], axis=0)


def _attn_specs(nb, width=ATT_GROUPW):
    cur = pl.BlockSpec((ATT_ROWS, width), lambda b: (b, 0))
    prev = pl.BlockSpec((ATT_BLK, width), lambda b: (jnp.maximum(b * ATT_T - 1, 0), 0))
    nxt = pl.BlockSpec((ATT_BLK, width), lambda b: (jnp.minimum((b + 1) * ATT_T, nb - 1), 0))
    return cur, prev, nxt


def _attn_fwd(tag, per_seq, q, k, v):
    s = q.shape[0]
    nb = s // ATT_BLK

    def body(q_ref, kc_ref, kp_ref, vc_ref, vp_ref, o_ref, lse_ref):
        bt = pl.program_id(0)
        for i in range(ATT_T):
            has_prev = lax.rem(bt * ATT_T + i, per_seq) > 0
            valid = _band_valid(jnp.where(has_prev, 0, ATT_BLK))
            rows = slice(i * ATT_BLK, (i + 1) * ATT_BLK)
            for h in range(ATT_HPG):
                sl = slice(h * ATT_HEAD_DIM, (h + 1) * ATT_HEAD_DIM)
                kcat = _window(kp_ref, kc_ref, i, sl)
                vcat = _window(vp_ref, vc_ref, i, sl)
                sc = _dot(q_ref[rows, sl], kcat, "nt") * ATT_SCALE
                sc = jnp.where(valid, sc, NEG_INF)
                m = jnp.max(sc, axis=-1, keepdims=True)
                p = jnp.exp(sc - m)
                den = jnp.sum(p, axis=-1, keepdims=True)
                o_ref[rows, sl] = _dot(p, vcat, "nn") / den
                lse_ref[rows, _stat(h)] = jnp.broadcast_to(m + jnp.log(den), (ATT_BLK, 128))

    cur, prev, _ = _attn_specs(nb)
    stat, _, _ = _attn_specs(nb, ATT_STATW)
    return pl.pallas_call(
        body, name="attn_fwd_" + tag, grid=(nb // ATT_T,), in_specs=[cur, cur, prev, cur, prev], out_specs=[cur, stat],
        out_shape=[jax.ShapeDtypeStruct((s, ATT_GROUPW), F32), jax.ShapeDtypeStruct((s, ATT_STATW), F32)],
        compiler_params=_cparams(1))(q, k, k, v, v)


def _attn_dq(tag, per_seq, q, k, v, do, lse, delta):
    s = q.shape[0]
    nb = s // ATT_BLK

    def body(q_ref, kc_ref, kp_ref, vc_ref, vp_ref, do_ref, lse_ref, dl_ref, dq_ref):
        bt = pl.program_id(0)
        for i in range(ATT_T):
            has_prev = lax.rem(bt * ATT_T + i, per_seq) > 0
            valid = _band_valid(jnp.where(has_prev, 0, ATT_BLK))
            rows = slice(i * ATT_BLK, (i + 1) * ATT_BLK)
            for h in range(ATT_HPG):
                sl = slice(h * ATT_HEAD_DIM, (h + 1) * ATT_HEAD_DIM)
                kcat = _window(kp_ref, kc_ref, i, sl)
                vcat = _window(vp_ref, vc_ref, i, sl)
                lse = jnp.concatenate([lse_ref[rows, _stat(h)]] * 2, axis=1)
                dlt = jnp.concatenate([dl_ref[rows, _stat(h)]] * 2, axis=1)
                sc = _dot(q_ref[rows, sl], kcat, "nt") * ATT_SCALE
                p = jnp.exp(jnp.where(valid, sc, NEG_INF) - lse)
                dp = _dot(do_ref[rows, sl], vcat, "nt")
                ds = p * (dp - dlt) * ATT_SCALE
                dq_ref[rows, sl] = _dot(ds, kcat, "nn")

    cur, prev, _ = _attn_specs(nb)
    stat, _, _ = _attn_specs(nb, ATT_STATW)
    return pl.pallas_call(
        body, name="attn_dq_" + tag, grid=(nb // ATT_T,), in_specs=[cur, cur, prev, cur, prev, cur, stat, stat],
        out_specs=cur, out_shape=jax.ShapeDtypeStruct((s, ATT_GROUPW), F32),
        compiler_params=_cparams(1))(q, k, k, v, v, do, lse, delta)


def _attn_dkv(tag, per_seq, q, k, v, do, lse_t, delta_t):
    s = q.shape[0]
    nb = s // ATT_BLK

    def body(k_ref, v_ref, qc_ref, qn_ref, doc_ref, don_ref, lc_ref, ln_ref, dc_ref, dn_ref, dk_ref, dv_ref):
        bt = pl.program_id(0)
        ki = lax.broadcasted_iota(jnp.int32, (ATT_BLK, 2 * ATT_BLK), 0)
        ci = lax.broadcasted_iota(jnp.int32, (ATT_BLK, 2 * ATT_BLK), 1)

        def pair(edge_ref, cur_ref, i, sl):
            if i == ATT_T - 1:
                return jnp.concatenate([cur_ref[i * ATT_BLK:(i + 1) * ATT_BLK, sl], edge_ref[:, sl]], axis=0)
            return cur_ref[i * ATT_BLK:(i + 2) * ATT_BLK, sl]

        def pair_row(edge_ref, cur_ref, i, h):
            if i == ATT_T - 1:
                row = jnp.concatenate([cur_ref[h:h + 1, i * ATT_BLK:(i + 1) * ATT_BLK], edge_ref[h:h + 1, :]], axis=1)
            else:
                row = cur_ref[h:h + 1, i * ATT_BLK:(i + 2) * ATT_BLK]
            return jnp.broadcast_to(row, (ATT_BLK, 2 * ATT_BLK))

        for i in range(ATT_T):
            b = bt * ATT_T + i
            next_uses = (b + 1 < nb) & (lax.rem(b + 1, per_seq) > 0)
            reach = jnp.where(next_uses, 0, 4 * ATT_BLK)
            valid = ((ci < ATT_BLK) & (ci >= ki)) | ((ci >= ATT_BLK) & (ki - ci + ATT_BLK >= reach))
            rows = slice(i * ATT_BLK, (i + 1) * ATT_BLK)
            for h in range(ATT_HPG):
                sl = slice(h * ATT_HEAD_DIM, (h + 1) * ATT_HEAD_DIM)
                qcat, docat = pair(qn_ref, qc_ref, i, sl), pair(don_ref, doc_ref, i, sl)
                sc = _dot(k_ref[rows, sl], qcat, "nt") * ATT_SCALE
                p = jnp.exp(jnp.where(valid, sc, NEG_INF) - pair_row(ln_ref, lc_ref, i, h))
                dv_ref[rows, sl] = _dot(p, docat, "nn")
                dp = _dot(v_ref[rows, sl], docat, "nt")
                ds = p * (dp - pair_row(dn_ref, dc_ref, i, h)) * ATT_SCALE
                dk_ref[rows, sl] = _dot(ds, qcat, "nn")

    cur, _, nxt = _attn_specs(nb)
    stat = pl.BlockSpec((8, ATT_ROWS), lambda b: (0, b))
    snxt = pl.BlockSpec((8, ATT_BLK), lambda b: (0, jnp.minimum((b + 1) * ATT_T, nb - 1)))
    return pl.pallas_call(
        body, name="attn_dkv_" + tag, grid=(nb // ATT_T,), in_specs=[cur, cur, cur, nxt, cur, nxt, stat, snxt, stat, snxt],
        out_specs=[cur, cur], out_shape=[jax.ShapeDtypeStruct((s, ATT_GROUPW), F32)] * 2,
        compiler_params=_cparams(1))(k, v, q, q, do, do, lse_t, lse_t, delta_t, delta_t)


def _xattn_probs(q, kh):
    sc = _dot(q, kh, "nt") * XATT_SCALE
    e = jnp.exp(sc - jnp.max(sc, axis=-1, keepdims=True))
    return e / jnp.sum(e, axis=-1, keepdims=True)


def _xattn_fwd(q, kv, tm=512):
    s = q.shape[0]
    tm = min(tm, s)

    def body(q_ref, kv_ref, o_ref):
        for h in range(XATT_HEADS):
            sl = slice(h * XATT_HEAD_DIM, (h + 1) * XATT_HEAD_DIM)
            vs = slice(D_MODEL + h * XATT_HEAD_DIM, D_MODEL + (h + 1) * XATT_HEAD_DIM)
            p = _xattn_probs(q_ref[:, sl], kv_ref[:, sl])
            o_ref[:, sl] = _dot(p, kv_ref[:, vs], "nn").astype(o_ref.dtype)

    return pl.pallas_call(
        body, name="xattn_fwd", grid=(s // tm,),
        in_specs=[pl.BlockSpec((tm, D_MODEL), lambda i: (i, 0)), pl.BlockSpec(kv.shape, lambda i: (0, 0))],
        out_specs=pl.BlockSpec((tm, D_MODEL), lambda i: (i, 0)),
        out_shape=jax.ShapeDtypeStruct((s, D_MODEL), MXU_DTYPE), compiler_params=_cparams(1))(q, kv)


def _xattn_bwd(q, kv, do, tm=512):
    s = q.shape[0]
    tm = min(tm, s)

    def body(q_ref, kv_ref, do_ref, dq_ref, dkv_ref):
        first = pl.program_id(0) == 0

        @pl.when(first)
        def _():
            dkv_ref[...] = jnp.zeros_like(dkv_ref)

        for h in range(XATT_HEADS):
            sl = slice(h * XATT_HEAD_DIM, (h + 1) * XATT_HEAD_DIM)
            vs = slice(D_MODEL + h * XATT_HEAD_DIM, D_MODEL + (h + 1) * XATT_HEAD_DIM)
            p = _xattn_probs(q_ref[:, sl], kv_ref[:, sl])
            dkv_ref[:, vs] += _dot(p, do_ref[:, sl], "tn")
            dp = _dot(do_ref[:, sl], kv_ref[:, vs], "nt")
            ds = p * (dp - jnp.sum(dp * p, axis=-1, keepdims=True)) * XATT_SCALE
            dq_ref[:, sl] = _dot(ds, kv_ref[:, sl], "nn").astype(dq_ref.dtype)
            dkv_ref[:, sl] += _dot(ds, q_ref[:, sl], "tn")

    row = pl.BlockSpec((tm, D_MODEL), lambda i: (i, 0))
    whole = pl.BlockSpec(kv.shape, lambda i: (0, 0))
    return pl.pallas_call(
        body, name="xattn_bwd", grid=(s // tm,), in_specs=[row, whole, row], out_specs=[row, whole],
        out_shape=[jax.ShapeDtypeStruct((s, D_MODEL), MXU_DTYPE), jax.ShapeDtypeStruct(kv.shape, F32)],
        compiler_params=_cparams(1))(q, kv, do)


def _ln(x, g, b):
    mu = jnp.mean(x, axis=-1, keepdims=True)
    xc = x - mu
    var = jnp.mean(jnp.square(xc), axis=-1, keepdims=True)
    return xc * lax.rsqrt(var + LN_EPS) * g + b


def _res_ln(h, o, g, b):
    return _ln(DEEPNORM_ALPHA * h + o, g, b)


def _gate(gs, ga, z1, z2, batt):
    return jax.nn.sigmoid(gs) * (z1 * jax.nn.sigmoid(z2)) + jax.nn.sigmoid(ga) * batt


def _rope_tables(pos, invf, m1, m2):
    ang = pos.astype(F32) * invf
    sin = jnp.sin(ang)
    return jnp.cos(ang), -sin * m1, sin * m2


def _rope(t, cos, s_up, s_dn):
    w = t.shape[-1]
    return t * cos + pltpu.roll(t, w - ROT_DIM // 2, 1) * s_up + pltpu.roll(t, ROT_DIM // 2, 1) * s_dn


def _rope_t(dt, cos, s_up, s_dn):
    w = dt.shape[-1]
    return dt * cos + pltpu.roll(dt * s_up, ROT_DIM // 2, 1) + pltpu.roll(dt * s_dn, w - ROT_DIM // 2, 1)


def _rope_consts():
    inv_freq = ROPE_THETA ** (-jnp.arange(0, ROT_DIM, 2, dtype=F32) / ROT_DIM)
    d = np.arange(ATT_GROUPW) % ATT_HEAD_DIM
    invf = jnp.where(d < ROT_DIM, inv_freq[d % (ROT_DIM // 2)], 0.0).reshape(1, ATT_GROUPW).astype(F32)
    m1 = jnp.asarray((d < ROT_DIM // 2).astype(np.float32)).reshape(1, ATT_GROUPW)
    m2 = jnp.asarray(((d >= ROT_DIM // 2) & (d < ROT_DIM)).astype(np.float32)).reshape(1, ATT_GROUPW)
    return invf, m1, m2


def _head_sum_matrix():
    d = np.arange(ATT_GROUPW) // ATT_HEAD_DIM
    s = np.arange(ATT_STATW) // 128
    return jnp.asarray((d[:, None] == s[None, :]).astype(np.float32))


def _adamw(w, g, m, v):
    m = ADAM_B1 * m + (1.0 - ADAM_B1) * g
    v = ADAM_B2 * v + (1.0 - ADAM_B2) * jnp.square(g)
    m_hat = m / (1.0 - ADAM_B1 ** ADAM_STEP)
    v_hat = v / (1.0 - ADAM_B2 ** ADAM_STEP)
    delta = -ADAM_LR * (m_hat / (jnp.sqrt(v_hat) + ADAM_EPS) + ADAM_WD * w)
    return delta, m, v


def _local_step(x, mem, pos, target, sp, ex):
    s = x.shape[0]
    al = DEEPNORM_ALPHA
    mx = MXU_DTYPE

    h0, h0b = _rowwise("ln_in", lambda x, g, b: (lambda h: (h, h))(_ln(x, g, b)), [x],
                       [sp["ln_in_g"], sp["ln_in_b"]], [(D_MODEL, F32), (D_MODEL, mx)],
                       carry=ex.gather_carry(["w_in"]))
    proj = _mm("proj", h0b, ex.weight("w_in"), "nn", bias=sp["b_in"],
               carry=ex.gather_carry(["w_glu", "w_att_up", "w_mix_out", "w_xq", "w_xkv"]))

    ldt = jnp.repeat(sp["ssm_log_dt"].reshape(SSM_GROUPS), SSM_STATE).reshape(N_STATE, 1)
    are, aim = sp["ssm_a_re"].reshape(N_STATE, 1), sp["ssm_a_im"].reshape(N_STATE, 1)
    bre, bim = sp["ssm_b_re"].reshape(N_STATE, SSM_GROUP), sp["ssm_b_im"].reshape(N_STATE, SSM_GROUP)
    abr, abi, bbr, bbi = _ssm_disc_fwd(ldt, are, aim, bre, bim)
    a_re, a_im = abr.reshape(1, N_STATE), abi.reshape(1, N_STATE)
    bexp = jnp.concatenate([_blockdiag_b(bbr), _blockdiag_b(bbi)], axis=2).astype(mx)
    cexp = jnp.concatenate([_blockdiag_c(sp["ssm_c_re"].reshape(SSM_GROUPS, SSM_GROUP, SSM_STATE)),
                            -_blockdiag_c(sp["ssm_c_im"].reshape(SSM_GROUPS, SSM_GROUP, SSM_STATE))],
                           axis=1).astype(mx)
    b12, c12 = _split_by_scan_block(bexp, 2), _split_by_scan_block(cexp, 1)
    h_re, h_im, y, u_p = _ssm_scan("ssm_scan_fwd", proj, b12, c12, a_re, a_im, sp["ssm_d"], reverse=False,
                                   carry=ex.gather_carry(["w_ff1", "w_ff2"]))
    ygb, = _rowwise("gelu", lambda y: jax.nn.gelu(y), [y], [], [(SSM_WIDTH, mx)])
    z = _mm("glu", ygb, ex.weight("w_glu"), "nn", bias=sp["b_glu"], carry=ex.gather_carry(["w_xo"]))

    invf, m1, m2 = _rope_consts()

    def rope_fwd(pos, q0, q1, q2, k0, k1, k2, v0, v1, v2, invf, m1, m2):
        tabs = _rope_tables(pos, invf, m1, m2)
        return tuple(_rope(t, *tabs) for t in (q0, q1, q2, k0, k1, k2)) + (v0, v1, v2)

    qkv_cols = [(proj, ATT_GROUPW, 3 + i) for i in range(9)]
    qkv = _rowwise("rope", rope_fwd, [pos] + qkv_cols, [invf, m1, m2], [(ATT_GROUPW, mx)] * 9)
    n_blocks = s // ATT_BLK
    groups = [(str(g), n_blocks // d, d) for g, d in enumerate(DILATIONS)]
    q_d = [_dilate(qkv[g], d) for g, d in enumerate(DILATIONS)]
    k_d = [_dilate(qkv[3 + g], d) for g, d in enumerate(DILATIONS)]
    v_d = [_dilate(qkv[6 + g], d) for g, d in enumerate(DILATIONS)]
    o_g, l_g = [], []
    for g, (tag, per_seq, d) in enumerate(groups):
        o, lse = _attn_fwd(tag, per_seq, q_d[g], k_d[g], v_d[g])
        o_g.append(_undilate(o, d))
        l_g.append(_undilate(lse, d))

    def merge(o0, o1, o2, l0, l1, l2):
        m = jnp.maximum(jnp.maximum(l0, l1), l2)
        e0, e1, e2 = jnp.exp(l0 - m), jnp.exp(l1 - m), jnp.exp(l2 - m)
        tot = e0 + e1 + e2

        def per_dim(e):
            w = e / tot
            return jnp.concatenate([w[:, h * 128:h * 128 + ATT_HEAD_DIM] for h in range(ATT_HPG)], axis=1)

        att = per_dim(e0) * o0 + per_dim(e1) * o1 + per_dim(e2) * o2
        lse = m + jnp.log(tot)
        return att, att, lse, _stat_rows(lse)

    att, attb, lse_tot, lse_tot_t = _rowwise("attn_merge", merge, o_g + l_g, [],
                                             [(ATT_GROUPW, F32), (ATT_GROUPW, mx), (ATT_STATW, F32)], touts=[(8, F32)])
    batt = _mm("att_up", attb, ex.weight("w_att_up"), "nn")

    gate_rows = [(proj, D_MODEL, 3), (proj, D_MODEL, 4), (z, D_MODEL, 0), (z, D_MODEL, 1), batt]
    mixedb, = _rowwise("gate", _gate, gate_rows, [], [(D_MODEL, mx)])
    o1 = _mm("mix_out", mixedb, ex.weight("w_mix_out"), "nn", bias=sp["b_mix_out"])
    h1, h1b = _rowwise("ln1", lambda h, o, g, b: (lambda r: (r, r))(_res_ln(h, o, g, b)), [h0, o1],
                       [sp["ln1_g"], sp["ln1_b"]], [(D_MODEL, F32), (D_MODEL, mx)])

    qx = _mm("xq", h1b, ex.weight("w_xq"), "nn", out_dtypes=(mx,))
    kvx = _mm("xkv", mem, ex.weight("w_xkv"), "nn", out_dtypes=(mx,))
    oxb = _xattn_fwd(qx, kvx)
    o2 = _mm("xo", oxb, ex.weight("w_xo"), "nn")
    h2, h2b = _rowwise("ln2", lambda h, o, g, b: (lambda r: (r, r))(_res_ln(h, o, g, b)), [h1, o2],
                       [sp["ln2_g"], sp["ln2_b"]], [(D_MODEL, F32), (D_MODEL, mx)])

    a_ff, fb = _mm("ff1", h2b, ex.weight("w_ff1"), "nn", bias=sp["b_ff1"],
                   epilogue=lambda r: (r, jnp.square(jnp.maximum(r, 0.0))), out_dtypes=(F32, mx))
    o3 = _mm("ff2", fb, ex.weight("w_ff2"), "nn", bias=sp["b_ff2"])

    def loss_bwd(h2, o3, tgt, g, b):
        def f(h2, o3, g, b):
            h3 = _res_ln(h2, o3, g, b)
            return 0.5 * jnp.sum(jnp.mean(jnp.square(h3 - tgt), axis=-1))

        loss, vjp = jax.vjp(f, h2, o3, g, b)
        _, dr, dg, db = vjp(jnp.ones((), F32))
        return dr, dr, dg, db, _colsum(dr), jnp.full((1, 128), loss, F32)

    dr3, dr3b, g_ln3_g, g_ln3_b, g_b_ff2, loss = _rowwise(
        "loss_ln3_bwd", loss_bwd, [h2, o3, target], [sp["ln3_g"], sp["ln3_b"]],
        [(D_MODEL, F32), (D_MODEL, mx)], [D_MODEL, D_MODEL, D_MODEL, 128])

    dab = _mm("ff2_dx", dr3b, ex.weight("w_ff2"), "nt", extras=(a_ff,),
              epilogue=lambda r, a: (r * (2.0 * jnp.maximum(a, 0.0)),), out_dtypes=(mx,))
    ex.grad("w_ff2", _mm("ff2_dw", fb, dr3b, "tn"))
    g_b_ff1, = _rowwise("ff1_db", lambda v: (_colsum(v),), [dab], [], [], [D_FF])
    ex.grad("w_ff1", _mm("ff1_dw", h2b, dab, "tn", carry=ex.carry(swap=["w_ff2"])))
    dh2 = _mm("ff1_dx", dab, ex.weight("w_ff1"), "nt", extras=(dr3,), epilogue=lambda r, d: (r + al * d,),
              carry=ex.carry(swap=["w_ff1"], ici=["w_ff2"]))

    def ln_bwd(h, o, dout, g, b):
        _, vjp = jax.vjp(_res_ln, h, o, g, b)
        _, dr, dg, db = vjp(dout)
        return dr, dr, dg, db, _colsum(dr)

    dr2, dr2b, g_ln2_g, g_ln2_b, _ = _rowwise(
        "ln2_bwd", ln_bwd, [h1, o2, dh2], [sp["ln2_g"], sp["ln2_b"]],
        [(D_MODEL, F32), (D_MODEL, mx)], [D_MODEL, D_MODEL, D_MODEL])
    ex.grad("w_xo", _mm("xo_dw", oxb, dr2b, "tn"))
    doxb = _mm("xo_dx", dr2b, ex.weight("w_xo"), "nt", out_dtypes=(mx,), carry=ex.carry(swap=["w_xo"]))
    dqxb, dkvx = _xattn_bwd(qx, kvx, doxb)
    ex.grad("w_xq", _mm("xq_dw", h1b, dqxb, "tn", carry=ex.carry(ici=["w_xo"])))
    dh1 = _mm("xq_dx", dqxb, ex.weight("w_xq"), "nt", extras=(dr2,), epilogue=lambda r, d: (r + al * d,),
              carry=ex.carry(swap=["w_xq"]))
    ex.grad("w_xkv", _mm("xkv_dw", mem, dkvx, "tn"))

    dr1, dr1b, g_ln1_g, g_ln1_b, g_b_mix = _rowwise(
        "ln1_bwd", ln_bwd, [h0, o1, dh1], [sp["ln1_g"], sp["ln1_b"]],
        [(D_MODEL, F32), (D_MODEL, mx)], [D_MODEL, D_MODEL, D_MODEL])
    ex.grad("w_mix_out", _mm("mix_dw", mixedb, dr1b, "tn", carry=ex.carry(swap=["w_xkv"], ici=["w_xq"])))
    dmixed = _mm("mix_dx", dr1b, ex.weight("w_mix_out"), "nt", carry=ex.carry(swap=["w_mix_out"]))

    def gate_bwd(gs, ga, z1, z2, batt, dm):
        _, vjp = jax.vjp(_gate, gs, ga, z1, z2, batt)
        dgs, dga, dz1, dz2, dbatt = vjp(dm)
        dz = jnp.concatenate([dz1, dz2], axis=-1)
        return dgs, dga, dz, dbatt, _colsum(dz)

    dgsb, dgab, dzb, dbattb, g_b_glu = _rowwise(
        "gate_bwd", gate_bwd, gate_rows + [dmixed], [],
        [(D_MODEL, mx), (D_MODEL, mx), (2 * D_MODEL, mx), (D_MODEL, mx)], [2 * D_MODEL])
    ex.grad("w_att_up", _mm("att_up_dw", attb, dbattb, "tn", carry=ex.carry(ici=["w_mix_out"])))
    datt = _mm("att_up_dx", dbattb, ex.weight("w_att_up"), "nt", carry=ex.carry(swap=["w_att_up"]))

    def att_delta(datt, att, hs):
        dl = jnp.dot(datt * att, hs, precision=lax.Precision.HIGHEST, preferred_element_type=F32)
        return datt, dl, _stat_rows(dl)

    dattb, delta, delta_t = _rowwise("attn_delta", att_delta, [datt, att], [_head_sum_matrix()],
                                     [(ATT_GROUPW, mx), (ATT_STATW, F32)], touts=[(8, F32)])
    dq_g, dk_g, dv_g = [], [], []
    for g, (tag, per_seq, d) in enumerate(groups):
        do_d, lt_d, dl_d = _dilate(dattb, d), _dilate(lse_tot, d), _dilate(delta, d)
        dq_g.append(_undilate(_attn_dq(tag, per_seq, q_d[g], k_d[g], v_d[g], do_d, lt_d, dl_d), d))
        dk, dv = _attn_dkv(tag, per_seq, q_d[g], k_d[g], v_d[g], do_d, _dilate_rows(lse_tot_t, d), _dilate_rows(delta_t, d))
        dk_g.append(_undilate(dk, d))
        dv_g.append(_undilate(dv, d))
    dqkv = dq_g + dk_g + dv_g

    def rope_bwd(pos, q0, q1, q2, k0, k1, k2, v0, v1, v2, invf, m1, m2):
        tabs = _rope_tables(pos, invf, m1, m2)
        return jnp.concatenate([_rope_t(t, *tabs) for t in (q0, q1, q2, k0, k1, k2)] + [v0, v1, v2], axis=-1)

    dqkvb, = _rowwise("rope_bwd", rope_bwd, [pos] + dqkv, [invf, m1, m2], [(9 * ATT_GROUPW, mx)])

    ex.grad("w_glu", _mm("glu_dw", ygb, dzb, "tn", carry=ex.carry(ici=["w_xkv", "w_att_up"])))
    dyg = _mm("glu_dx", dzb, ex.weight("w_glu"), "nt", carry=ex.carry(swap=["w_glu"]))

    def gelu_bwd(y, dyg):
        _, vjp = jax.vjp(jax.nn.gelu, y)
        return vjp(dyg)[0]

    dy, = _rowwise("gelu_bwd", gelu_bwd, [y, dyg], [], [(SSM_WIDTH, F32)])
    s_re, s_im, du, dy_p = _ssm_scan("ssm_scan_bwd", dy, c12, b12, a_re, a_im, sp["ssm_d"], reverse=True,
                                     carry=ex.carry(ici=["w_ff1", "w_glu"]))
    g_bexp, g_cexp, d_abr, d_abi = _ssm_wgrads(u_p, dy_p, s_re, s_im, h_re, h_im)
    g_ssm_d, = _rowwise("ssm_dd", lambda a, b: (_colsum(a * b),), [dy_p, u_p], [], [], [SSM_WIDTH])
    g_ldt, g_are, g_aim, g_bre, g_bim = _ssm_disc_bwd(
        ldt, are, aim, bre, bim, d_abr.reshape(N_STATE, 1), d_abi.reshape(N_STATE, 1),
        _diag_of_b(g_bexp[:, :, :CH_N]), _diag_of_b(g_bexp[:, :, CH_N:]))
    g_c_re = _diag_of_c(g_cexp[:, :CH_N, :])
    g_c_im = -_diag_of_c(g_cexp[:, CH_N:, :])
    dub = du.astype(mx)

    dprojb = jnp.concatenate([dub, dqkvb, dgsb, dgab], axis=-1)
    g_b_in, = _rowwise("in_db", lambda v: (_colsum(v),), [dprojb], [], [], [IN_COLS])
    ex.grad("w_in", _mm("in_dw", h0b, dprojb, "tn"))
    dh0 = _mm("in_dx", dprojb, ex.weight("w_in"), "nt", extras=(dr1,), epilogue=lambda r, d: (r + al * d,),
              carry=ex.carry(ici=["w_in"]))

    def ln_in_bwd(x, dout, g, b):
        _, vjp = jax.vjp(_ln, x, g, b)
        return vjp(dout)

    dx, g_ln_in_g, g_ln_in_b = _rowwise("ln_in_bwd", ln_in_bwd, [x, dh0], [sp["ln_in_g"], sp["ln_in_b"]],
                                        [(D_MODEL, F32)], [D_MODEL, D_MODEL])

    small = {"ln_in_g": g_ln_in_g, "ln_in_b": g_ln_in_b, "b_in": g_b_in, "ssm_log_dt": g_ldt, "ssm_a_re": g_are,
             "ssm_a_im": g_aim, "ssm_b_re": g_bre, "ssm_b_im": g_bim, "ssm_c_re": g_c_re, "ssm_c_im": g_c_im,
             "ssm_d": g_ssm_d, "b_glu": g_b_glu, "b_mix_out": g_b_mix, "ln1_g": g_ln1_g, "ln1_b": g_ln1_b,
             "ln2_g": g_ln2_g, "ln2_b": g_ln2_b, "b_ff1": g_b_ff1, "b_ff2": g_b_ff2, "ln3_g": g_ln3_g,
             "ln3_b": g_ln3_b}
    return loss, dx, small


def _piece_shape(k, n, axis):
    return (k // 2, n // 4) if axis == 1 else (k // 8, n)


def _aligned(v, m):
    return v if isinstance(v, int) else pl.multiple_of(v, m)


def _full_piece(ref, k, n, axis, chip, half):
    pr, pc = _piece_shape(k, n, axis)
    if axis == 1:
        return ref.at[pl.ds(_aligned(half * pr, 8), pr), pl.ds(_aligned(chip * pc, 128), pc)]
    return ref.at[pl.ds(_aligned(chip * (2 * pr) + half * pr, 8), pr), :]


def _full_shard(ref, k, n, axis, chip):
    if axis == 1:
        return ref.at[:, pl.ds(_aligned(chip * (n // 4), 128), n // 4)]
    return ref.at[pl.ds(_aligned(chip * (k // 4), 8), k // 4), :]


def _shard_piece(ref, k, n, axis, half):
    pr, _ = _piece_shape(k, n, axis)
    return ref.at[pl.ds(_aligned(half * pr, 8), pr), :]


def _mesh_pos():
    x, y, c = lax.axis_index("x"), lax.axis_index("y"), lax.axis_index("c")
    other_chips = [(1 - x, y), (x, 1 - y), (1 - x, 1 - y)]
    return x, y, c, other_chips


def _remote(src, dst, send_sem, recv_sem, dev):
    return pltpu.make_async_remote_copy(src_ref=src, dst_ref=dst, send_sem=send_sem, recv_sem=recv_sem,
                                        device_id=dev, device_id_type=MESH)


def _placed(name, fn, n_steps, where, ins, out_sds, out_block, out_index):
    def body(w_ref, *refs):
        o_ref = refs[-1]
        o_ref[...] = fn(*[r[...] for r in refs[:-1]]).astype(o_ref.dtype)

    grid_spec = pltpu.PrefetchScalarGridSpec(
        num_scalar_prefetch=1, grid=(n_steps,), in_specs=[pl.BlockSpec(bs, idx) for _, bs, idx in ins],
        out_specs=pl.BlockSpec(out_block, out_index))
    return pl.pallas_call(body, name=name, grid_spec=grid_spec, out_shape=out_sds,
                          compiler_params=_cparams(1))(where, *[a for a, _, _ in ins])


def _gather_copies(widx):
    geo = [BIG[i][1:] for i in widx]

    def ici(full, wi, j, chip, send_sems, recv_sems, c, dev):
        k, n, ax = geo[wi]
        piece = _full_piece(full[wi], k, n, ax, chip, c)
        return _remote(piece, piece, send_sems.at[wi * 6 + j], recv_sems.at[wi * 6 + j], dev)

    def d2d(full, wi, j, chip, half, send_sems, recv_sems, sib):
        k, n, ax = geo[wi]
        piece = _full_piece(full[wi], k, n, ax, chip, half)
        return _remote(piece, piece, send_sems.at[wi * 6 + 3 + j], recv_sems.at[wi * 6 + 3 + j], sib)

    def start(_, full, send_sems, recv_sems):
        x, y, c, chips = _mesh_pos()
        for wi in range(len(geo)):
            for j, (qx, qy) in enumerate(chips):
                ici(full, wi, j, 2 * x + y, send_sems, recv_sems, c, (qx, qy, c)).start()

    def finish(_, full, send_sems, recv_sems):
        x, y, c, chips = _mesh_pos()
        sib = (x, y, 1 - c)
        for wi in range(len(geo)):
            for j, (qx, qy) in enumerate(chips):
                ici(full, wi, j, 2 * qx + qy, send_sems, recv_sems, c, (qx, qy, c)).wait_recv()
                d2d(full, wi, j, 2 * qx + qy, c, send_sems, recv_sems, sib).start()
        for wi in range(len(geo)):
            for j, (qx, qy) in enumerate(chips):
                d2d(full, wi, j, 2 * qx + qy, 1 - c, send_sems, recv_sems, sib).wait_recv()
        for wi in range(len(geo)):
            for j, (qx, qy) in enumerate(chips):
                ici(full, wi, j, 2 * x + y, send_sems, recv_sems, c, (qx, qy, c)).wait_send()
                d2d(full, wi, j, 2 * qx + qy, c, send_sems, recv_sems, sib).wait_send()

    return start, finish, 6 * len(geo)


def _gather_weights(tag, fulls, widx):
    nw = len(widx)
    start, finish, n_sems = _gather_copies(widx)

    def body(*refs):
        full = refs[nw:2 * nw]
        start(None, full, *refs[2 * nw:])
        finish(None, full, *refs[2 * nw:])

    return pl.pallas_call(
        body, name="gather_weights_" + tag, in_specs=[HBM_SPEC] * nw, out_specs=[HBM_SPEC] * nw,
        out_shape=[jax.ShapeDtypeStruct(f.shape, f.dtype) for f in fulls],
        input_output_aliases={i: i for i in range(nw)},
        scratch_shapes=[pltpu.SemaphoreType.DMA((n_sems,)), pltpu.SemaphoreType.DMA((n_sems,))])(*fulls)


def _swap_copies(widx):
    geo = [BIG[i][1:] for i in widx]

    def copies(g, got, send_sems, recv_sems, base):
        x, y, c, _ = _mesh_pos()
        return [_remote(_full_piece(g[wi], k, n, ax, q, 1 - c), got[wi].at[q], send_sems.at[base + wi * 4 + q],
                        recv_sems.at[base + wi * 4 + q], (x, y, 1 - c))
                for wi, (k, n, ax) in enumerate(geo) for q in range(4)]

    def start(g, got, send_sems, recv_sems, base=0):
        for cp in copies(g, got, send_sems, recv_sems, base):
            cp.start()

    def finish(g, got, send_sems, recv_sems, base=0):
        for cp in copies(g, got, send_sems, recv_sems, base):
            cp.wait()

    return start, finish, 4 * len(geo)


def _swap_shapes(widx):
    return [jax.ShapeDtypeStruct((4,) + _piece_shape(*BIG[i][1:]), F32) for i in widx]


def _reduce_swap_halves(tag, grads, widx):
    nw = len(widx)
    start, finish, n_sems = _swap_copies(widx)

    def body(*refs):
        start(refs[:nw], refs[nw:2 * nw], *refs[2 * nw:])
        finish(refs[:nw], refs[nw:2 * nw], *refs[2 * nw:])

    return pl.pallas_call(
        body, name="reduce_swap_halves_" + tag, in_specs=[HBM_SPEC] * nw, out_specs=[HBM_SPEC] * nw,
        out_shape=_swap_shapes(widx),
        scratch_shapes=[pltpu.SemaphoreType.DMA((n_sems,)), pltpu.SemaphoreType.DMA((n_sems,))])(*grads)


def _owner_copies(nw):
    def copies(p, out, send_sems, recv_sems, base):
        x, y, c, chips = _mesh_pos()
        return [_remote(p[wi].at[2 * qx + qy], out[wi].at[j], send_sems.at[base + wi * 3 + j],
                        recv_sems.at[base + wi * 3 + j], (qx, qy, c))
                for wi in range(nw) for j, (qx, qy) in enumerate(chips)]

    def start(p, out, send_sems, recv_sems, base=0):
        for cp in copies(p, out, send_sems, recv_sems, base):
            cp.start()

    def finish(p, out, send_sems, recv_sems, base=0):
        for cp in copies(p, out, send_sems, recv_sems, base):
            cp.wait()

    return start, finish, 3 * nw


def _join_carries(a, b):
    if a is None or b is None:
        return a if b is None else b
    n_i, n_o = len(a.ins), len(a.outs)
    outs = list(a.outs) + [o + n_i if isinstance(o, int) else o for o in b.outs]

    def start(c_in, c_out, send_sems, recv_sems):
        a.start(c_in[:n_i], c_out[:n_o], send_sems, recv_sems)
        b.start(c_in[n_i:], c_out[n_o:], send_sems, recv_sems, base=a.n_sems)

    def finish(c_in, c_out, send_sems, recv_sems):
        a.finish(c_in[:n_i], c_out[:n_o], send_sems, recv_sems)
        b.finish(c_in[n_i:], c_out[n_o:], send_sems, recv_sems, base=a.n_sems)

    def done(res):
        a.done(res[:n_o])
        b.done(res[n_o:])

    return _Carry(a.ins + b.ins, outs, a.n_sems + b.n_sems, start, finish, done)


def _share_with_sibling(shards):
    nw = len(BIG)

    def body(*refs):
        out = refs[nw:2 * nw]
        send_sems, recv_sems = refs[2 * nw:]
        x, y, c, _ = _mesh_pos()
        sib = (x, y, 1 - c)
        cps = []
        for wi, (_, k, n, ax) in enumerate(BIG):
            mine = _shard_piece(out[wi], k, n, ax, c)
            cp = _remote(mine, mine, send_sems.at[wi], recv_sems.at[wi], sib)
            cp.start()
            cps.append(cp)
        for wi, (_, k, n, ax) in enumerate(BIG):
            piece = _shard_piece(out[wi], k, n, ax, 1 - c)
            _remote(piece, piece, send_sems.at[wi], recv_sems.at[wi], sib).wait_recv()
        for cp in cps:
            cp.wait_send()

    return pl.pallas_call(
        body, name="share_with_sibling", in_specs=[HBM_SPEC] * nw, out_specs=[HBM_SPEC] * nw,
        out_shape=[jax.ShapeDtypeStruct(sh.shape, sh.dtype) for sh in shards],
        input_output_aliases={i: i for i in range(nw)},
        scratch_shapes=[pltpu.SemaphoreType.DMA((nw,)), pltpu.SemaphoreType.DMA((nw,))])(*shards)


def _allreduce_small(v):
    r = v.shape[0]
    rh = r // 2
    assert rh % 8 == 0

    def body(v_ref, o_ref, sib_buf, chip_buf, send_sems, recv_sems):
        x, y, c, chips = _mesh_pos()
        me = 2 * x + y
        sib = (x, y, 1 - c)
        mine = pl.ds(pl.multiple_of(c * rh, 8), rh)
        other = pl.ds(pl.multiple_of((1 - c) * rh, 8), rh)
        swap = _remote(v_ref.at[other], sib_buf, send_sems.at[0], recv_sems.at[0], sib)
        swap.start()
        swap.wait()
        chip_buf[me] = v_ref[mine, :] + sib_buf[...]
        cps = []
        for j, (qx, qy) in enumerate(chips):
            cp = _remote(chip_buf.at[me], chip_buf.at[me], send_sems.at[1 + j], recv_sems.at[1 + j], (qx, qy, c))
            cp.start()
            cps.append(cp)
        for j, (qx, qy) in enumerate(chips):
            slot = chip_buf.at[2 * qx + qy]
            _remote(slot, slot, send_sems.at[1 + j], recv_sems.at[1 + j], (qx, qy, c)).wait_recv()
        for cp in cps:
            cp.wait_send()
        o_ref[mine, :] = ((chip_buf[0] + chip_buf[1]) + chip_buf[2]) + chip_buf[3]
        back = _remote(o_ref.at[mine], o_ref.at[mine], send_sems.at[4], recv_sems.at[4], sib)
        back.start()
        _remote(o_ref.at[other], o_ref.at[other], send_sems.at[4], recv_sems.at[4], sib).wait_recv()
        back.wait_send()

    return pl.pallas_call(
        body, name="allreduce_small", in_specs=[VMEM_SPEC], out_specs=VMEM_SPEC,
        out_shape=jax.ShapeDtypeStruct((r, 128), F32),
        scratch_shapes=[pltpu.VMEM((rh, 128), F32), pltpu.VMEM((4, rh, 128), F32),
                        pltpu.SemaphoreType.DMA((5,)), pltpu.SemaphoreType.DMA((5,))],
        compiler_params=pltpu.CompilerParams(vmem_limit_bytes=VMEM_LIMIT))(v)


def _as2d(a):
    a = a.reshape((-1, a.shape[-1])) if a.ndim > 1 else a.reshape(1, -1)
    return a


def _adamw_small(quads):
    n = len(quads)

    def body(*refs):
        for i in range(n):
            w, g, m, v = (r[...] for r in refs[4 * i:4 * i + 4])
            for ref, val in zip(refs[4 * n + 3 * i:4 * n + 3 * i + 3], _adamw(w, g, m, v)):
                ref[...] = val

    return pl.pallas_call(
        body, name="adamw_small", in_specs=[VMEM_SPEC] * (4 * n), out_specs=[VMEM_SPEC] * (3 * n),
        out_shape=[jax.ShapeDtypeStruct(q[0].shape, F32) for q in quads for _ in range(3)],
        compiler_params=pltpu.CompilerParams(vmem_limit_bytes=VMEM_LIMIT))(*[a for q in quads for a in q])


def _where():
    return jnp.stack([2 * lax.axis_index("x") + lax.axis_index("y"), lax.axis_index("c")]).astype(jnp.int32)


_BIG_INDEX = {name: i for i, (name, _, _, _) in enumerate(BIG)}


class _LocalWeights:
    def __init__(self, weights):
        self.weights, self.grads = weights, {}

    def gather_now(self, names):
        pass

    def gather_carry(self, names):
        return None

    def weight(self, name):
        return self.weights[name]

    def grad(self, name, g):
        self.grads[name] = g

    def carry(self, swap=(), ici=()):
        return None


class _Exchange:
    def __init__(self, inputs, where):
        self.inputs, self.where = inputs, where
        self.full, self.ready = {}, set()
        self.raw, self.got, self.parts, self.landed, self.geom = {}, {}, {}, {}, {}
        for name, k, n, ax in BIG:
            w2 = inputs[name][0]
            rs, cs = w2.shape
            tm = _tile(rs, 512)
            steps = rs // tm
            if ax == 1:
                blk, idx = (tm, cs), lambda i, w: (i, w[0])
            else:
                blk, idx = (tm, n), functools.partial(lambda i, w, steps: (w[0] * steps + i, 0), steps=steps)
            self.full[name] = _placed("cast_" + name, lambda w: w, steps, where, [(w2, (tm, cs), lambda i, w: (i, 0))],
                                      jax.ShapeDtypeStruct((k, n), MXU_DTYPE), blk, idx)

    def _gathered(self, names, outs):
        for name, o in zip(names, outs):
            self.full[name] = o
            self.ready.add(name)

    def gather_now(self, names):
        self._gathered(names, _gather_weights(names[0], [self.full[n] for n in names], [_BIG_INDEX[n] for n in names]))

    def gather_carry(self, names):
        start, finish, n_sems = _gather_copies([_BIG_INDEX[n] for n in names])
        return _Carry([self.full[n] for n in names], list(range(len(names))), n_sems, start, finish,
                      functools.partial(self._gathered, names))

    def weight(self, name):
        assert name in self.ready, name
        return self.full[name]

    def grad(self, name, g):
        self.raw[name] = g

    def _swapped(self, names, outs):
        for name, o in zip(names, outs):
            self.got[name] = o

    def _pair_sum(self, name):
        i = _BIG_INDEX[name]
        _, k, n, ax = BIG[i]
        g = self.raw[name]
        if name not in self.got:
            self._swapped([name], _reduce_swap_halves(name, [g], [i]))
        got = self.got[name]
        pr, pc = _piece_shape(k, n, ax)
        tm = _tile(pr, 512)
        spp = pr // tm
        self.geom[name] = (pr, pc, tm, spp)
        if ax == 1:
            g_idx = functools.partial(lambda i, w, spp: (w[1] * spp + i % spp, i // spp), spp=spp)
        else:
            g_idx = functools.partial(lambda i, w, spp: ((i // spp) * 2 * spp + w[1] * spp + i % spp, 0), spp=spp)
        self.parts[name] = _placed(
            "pair_sum_" + name, lambda a, b: a + b, 4 * spp, self.where,
            [(g, (tm, pc), g_idx), (got.reshape(4 * pr, pc), (tm, pc), lambda i, w: (i, 0))],
            jax.ShapeDtypeStruct((4 * pr, pc), BF16), (tm, pc), lambda i, w: (i, 0)).reshape(4, pr, pc)

    def _landed(self, names, outs):
        for name, o in zip(names, outs):
            self.landed[name] = o

    def carry(self, swap=(), ici=()):
        first = second = None
        if swap:
            widx = [_BIG_INDEX[n] for n in swap]
            start, finish, n_sems = _swap_copies(widx)
            first = _Carry([self.raw[n] for n in swap], _swap_shapes(widx), n_sems, start, finish,
                           functools.partial(self._swapped, list(swap)))
        if ici:
            for n in ici:
                self._pair_sum(n)
            start, finish, n_sems = _owner_copies(len(ici))
            parts = [self.parts[n] for n in ici]
            outs = [jax.ShapeDtypeStruct((3,) + p.shape[1:], p.dtype) for p in parts]
            second = _Carry(parts, outs, n_sems, start, finish, functools.partial(self._landed, list(ici)))
        return _join_carries(first, second)

    def finish(self):
        halves = []
        for name, _, _, _ in BIG:
            pr, pc, tm, spp = self.geom[name]
            ins = [(self.parts[name], (None, tm, pc), lambda i, w: (w[0], i, 0))]
            ins += [(self.landed[name], (None, tm, pc), functools.partial(lambda i, w, j: (j, i, 0), j=j))
                    for j in range(3)]
            halves.append(_placed("chip_sum_" + name,
                                  lambda a, b, c, d: ((a.astype(F32) + b.astype(F32)) + c.astype(F32)) + d.astype(F32),
                                  spp, self.where, ins, jax.ShapeDtypeStruct(self.inputs[name].shape[1:], F32), (tm, pc),
                                  functools.partial(lambda i, w, spp: (w[1] * spp + i, 0), spp=spp)))
        return dict(zip([b[0] for b in BIG], _share_with_sibling(halves)))


def _step(inputs):
    x, mem, positions, target = inputs["x"][0], inputs["mem"][0], inputs["positions"], inputs["loss_target"][0]
    pos = positions.reshape(-1, 1)
    ex = _Exchange(inputs, _where())
    sp = {name: _as2d(inputs[name]) for name in SMALL}
    memb, = _rowwise("cast_mem", lambda m: (m,), [mem], [], [(D_MODEL, MXU_DTYPE)])

    loss, dx, gsmall = _local_step(x, memb, pos, target, sp, ex)
    gshard = ex.finish()

    out = {}
    for name, _, _, _ in BIG:
        w2, m2, v2 = inputs[name][0], inputs["m_" + name][0], inputs["v_" + name][0]
        n = w2.shape[1]
        d, nm, nv = _rowwise("adamw_" + name, _adamw, [w2, gshard[name], m2, v2], [], [(n, F32)] * 3, tm=256)
        lead = inputs[name].shape
        out[name] = (gshard[name].reshape(lead), d.reshape(lead), nm.reshape(lead), nv.reshape(lead))

    def tiles(a):
        flat = a.reshape(-1)
        n = -(-flat.shape[0] // 1024) * 1024
        return jnp.pad(flat, (0, n - flat.shape[0])).reshape(n // 128, 128)

    pieces = [tiles(loss[:, :1])] + [tiles(gsmall[name]) for name in SMALL]
    if sum(p.shape[0] for p in pieces) % 16:
        pieces.append(jnp.zeros((8, 128), F32))
    red = _allreduce_small(jnp.concatenate(pieces, axis=0))
    loss_total = red[0, 0]
    grads, off = {}, pieces[0].shape[0]
    for name, p in zip(SMALL, pieces[1:]):
        shp = _as2d(inputs[name]).shape
        grads[name] = red[off:off + p.shape[0]].reshape(-1)[:shp[0] * shp[1]].reshape(shp)
        off += p.shape[0]
    upd = _adamw_small([(_as2d(inputs[n]), grads[n], _as2d(inputs["m_" + n]), _as2d(inputs["v_" + n])) for n in SMALL])
    for i, name in enumerate(SMALL):
        shp = inputs[name].shape
        out[name] = (grads[name].reshape(shp),) + tuple(t.reshape(shp) for t in upd[3 * i:3 * i + 3])
    return loss_total, dx.reshape(inputs["x"].shape), out


_ARG_NAMES = (("x", "mem", "positions") + WEIGHT_ORDER + ("loss_target",) + tuple("m_" + n for n in WEIGHT_ORDER)
              + tuple("v_" + n for n in WEIGHT_ORDER))


def kernel(x, mem, positions, ln_in_g, ln_in_b, w_in, b_in, ssm_log_dt, ssm_a_re, ssm_a_im, ssm_b_re, ssm_b_im, ssm_c_re, ssm_c_im, ssm_d, w_glu, b_glu, w_att_up, w_mix_out, b_mix_out, ln1_g, ln1_b, w_xq, w_xkv, w_xo, ln2_g, ln2_b, w_ff1, b_ff1, w_ff2, b_ff2, ln3_g, ln3_b, loss_target, m_ln_in_g, m_ln_in_b, m_w_in, m_b_in, m_ssm_log_dt, m_ssm_a_re, m_ssm_a_im, m_ssm_b_re, m_ssm_b_im, m_ssm_c_re, m_ssm_c_im, m_ssm_d, m_w_glu, m_b_glu, m_w_att_up, m_w_mix_out, m_b_mix_out, m_ln1_g, m_ln1_b, m_w_xq, m_w_xkv, m_w_xo, m_ln2_g, m_ln2_b, m_w_ff1, m_b_ff1, m_w_ff2, m_b_ff2, m_ln3_g, m_ln3_b, v_ln_in_g, v_ln_in_b, v_w_in, v_b_in, v_ssm_log_dt, v_ssm_a_re, v_ssm_a_im, v_ssm_b_re, v_ssm_b_im, v_ssm_c_re, v_ssm_c_im, v_ssm_d, v_w_glu, v_b_glu, v_w_att_up, v_w_mix_out, v_b_mix_out, v_ln1_g, v_ln1_b, v_w_xq, v_w_xkv, v_w_xo, v_ln2_g, v_ln2_b, v_w_ff1, v_b_ff1, v_w_ff2, v_b_ff2, v_ln3_g, v_ln3_b):
    args = (x, mem, positions, ln_in_g, ln_in_b, w_in, b_in, ssm_log_dt, ssm_a_re, ssm_a_im, ssm_b_re, ssm_b_im, ssm_c_re, ssm_c_im, ssm_d, w_glu, b_glu, w_att_up, w_mix_out, b_mix_out, ln1_g, ln1_b, w_xq, w_xkv, w_xo, ln2_g, ln2_b, w_ff1, b_ff1, w_ff2, b_ff2, ln3_g, ln3_b, loss_target, m_ln_in_g, m_ln_in_b, m_w_in, m_b_in, m_ssm_log_dt, m_ssm_a_re, m_ssm_a_im, m_ssm_b_re, m_ssm_b_im, m_ssm_c_re, m_ssm_c_im, m_ssm_d, m_w_glu, m_b_glu, m_w_att_up, m_w_mix_out, m_b_mix_out, m_ln1_g, m_ln1_b, m_w_xq, m_w_xkv, m_w_xo, m_ln2_g, m_ln2_b, m_w_ff1, m_b_ff1, m_w_ff2, m_b_ff2, m_ln3_g, m_ln3_b, v_ln_in_g, v_ln_in_b, v_w_in, v_b_in, v_ssm_log_dt, v_ssm_a_re, v_ssm_a_im, v_ssm_b_re, v_ssm_b_im, v_ssm_c_re, v_ssm_c_im, v_ssm_d, v_w_glu, v_b_glu, v_w_att_up, v_w_mix_out, v_b_mix_out, v_ln1_g, v_ln1_b, v_w_xq, v_w_xkv, v_w_xo, v_ln2_g, v_ln2_b, v_w_ff1, v_b_ff1, v_w_ff2, v_b_ff2, v_ln3_g, v_ln3_b)
    assert len(args) == len(_ARG_NAMES)
    inputs = dict(zip(_ARG_NAMES, args))
    loss, dx, out = _step(inputs)
    res = [loss, dx]
    for k in range(4):
        res += [out[name][k] for name in WEIGHT_ORDER]
    return tuple(res)
```

```python
import functools
import math

import numpy as np
import jax
import jax.numpy as jnp
from jax import lax
from jax.experimental import pallas as pl
from jax.experimental.pallas import tpu as pltpu

F32 = jnp.float32
BF16 = jnp.bfloat16
MXU_DTYPE = jnp.bfloat16

D_MODEL = 1024
SSM_GROUP = 16
SSM_WIDTH = 768
SSM_GROUPS = 48
SSM_STATE = 64
N_STATE = SSM_GROUPS * SSM_STATE
SSM_CHUNKS = 6
CH_W = 128
CH_N = 512
ATT_HEAD_DIM = 64
ATT_HPG = 4
ATT_GROUPW = ATT_HPG * ATT_HEAD_DIM
DILATIONS = (1, 4, 16)
ATT_BLK = 128
ATT_SCALE = ATT_HEAD_DIM ** -0.5
ROT_DIM = 16
ROPE_THETA = 500000.0
XATT_HEADS = 4
XATT_HEAD_DIM = 256
XATT_SCALE = XATT_HEAD_DIM ** -0.5
D_FF = 4096
IN_COLS = 5120
DEEPNORM_ALPHA = 2.0 ** 0.25
LN_EPS = 1e-5
NEG_INF = -1e30
ADAM_LR = 0.001
ADAM_B1 = 0.9
ADAM_B2 = 0.999
ADAM_EPS = 1e-08
ADAM_WD = 0.01
ADAM_STEP = 10

N_SEG = 32
VMEM_LIMIT = 56 * 1024 * 1024
MESH = pl.DeviceIdType.MESH
HBM_SPEC = pl.BlockSpec(memory_space=pltpu.HBM)
VMEM_SPEC = pl.BlockSpec(memory_space=pltpu.VMEM)

BIG = (("w_in", 1024, 5120, 1), ("w_glu", 768, 2048, 1), ("w_att_up", 256, 1024, 1),
       ("w_mix_out", 1024, 1024, 0), ("w_xq", 1024, 1024, 0), ("w_xkv", 1024, 2048, 1),
       ("w_xo", 1024, 1024, 0), ("w_ff1", 1024, 4096, 1), ("w_ff2", 4096, 1024, 0))
SMALL = ("ln_in_g", "ln_in_b", "b_in", "ssm_log_dt", "ssm_a_re", "ssm_a_im", "ssm_b_re", "ssm_b_im",
         "ssm_c_re", "ssm_c_im", "ssm_d", "b_glu", "b_mix_out", "ln1_g", "ln1_b", "ln2_g", "ln2_b",
         "b_ff1", "b_ff2", "ln3_g", "ln3_b")
WEIGHT_ORDER = ("ln_in_g", "ln_in_b", "w_in", "b_in", "ssm_log_dt", "ssm_a_re", "ssm_a_im", "ssm_b_re",
                "ssm_b_im", "ssm_c_re", "ssm_c_im", "ssm_d", "w_glu", "b_glu", "w_att_up", "w_mix_out",
                "b_mix_out", "ln1_g", "ln1_b", "w_xq", "w_xkv", "w_xo", "ln2_g", "ln2_b", "w_ff1", "b_ff1",
                "w_ff2", "b_ff2", "ln3_g", "ln3_b")


def _cparams(n_axes):
    return pltpu.CompilerParams(dimension_semantics=("arbitrary",) * n_axes, vmem_limit_bytes=VMEM_LIMIT)


class _Carry:
    def __init__(self, ins, outs, n_sems, start, finish, done):
        self.ins, self.outs, self.n_sems, self.start, self.finish, self.done = ins, outs, n_sems, start, finish, done


def _call(name, body, grid, in_specs, out_specs, out_shape, args, scratch_shapes=(), carry=None):
    in_specs, out_specs, out_shape = list(in_specs), list(out_specs), list(out_shape)
    params = _cparams(len(grid))
    if carry is None:
        return pl.pallas_call(body, name=name, grid=grid, in_specs=in_specs, out_specs=out_specs, out_shape=out_shape,
                              scratch_shapes=list(scratch_shapes), compiler_params=params)(*args)
    n_in, n_out, n_ci, n_co = len(in_specs), len(out_specs), len(carry.ins), len(carry.outs)
    n_scr = len(scratch_shapes)

    def wrapped(*refs):
        ins, c_in = refs[:n_in], refs[n_in:n_in + n_ci]
        outs, c_out = refs[n_in + n_ci:n_in + n_ci + n_out], refs[n_in + n_ci + n_out:n_in + n_ci + n_out + n_co]
        scratch = refs[n_in + n_ci + n_out + n_co:n_in + n_ci + n_out + n_co + n_scr]
        send_sems, recv_sems = refs[-2:]
        ids = [pl.program_id(a) for a in range(len(grid))]
        first = functools.reduce(jnp.logical_and, [i == 0 for i in ids])
        last = functools.reduce(jnp.logical_and, [i == g - 1 for i, g in zip(ids, grid)])

        @pl.when(first)
        def _():
            carry.start(c_in, c_out, send_sems, recv_sems)

        body(*ins, *outs, *scratch)

        @pl.when(last)
        def _():
            carry.finish(c_in, c_out, send_sems, recv_sems)

    c_shapes = [jax.ShapeDtypeStruct(carry.ins[o].shape, carry.ins[o].dtype) if isinstance(o, int) else o
                for o in carry.outs]
    aliases = {n_in + o: n_out + i for i, o in enumerate(carry.outs) if isinstance(o, int)}
    res = pl.pallas_call(
        wrapped, name=name, grid=grid, in_specs=in_specs + [HBM_SPEC] * n_ci, out_specs=out_specs + [HBM_SPEC] * n_co,
        out_shape=out_shape + c_shapes, input_output_aliases=aliases,
        scratch_shapes=list(scratch_shapes) + [pltpu.SemaphoreType.DMA((carry.n_sems,))] * 2,
        compiler_params=params)(*args, *carry.ins)
    carry.done(res[n_out:])
    return res[:n_out]


def _rowwise(name, fn, rows, consts, outs, reds=(), tm=256, touts=(), carry=None):
    n_rows = (rows[0][0] if isinstance(rows[0], tuple) else rows[0]).shape[-2]
    tm = min(tm, n_rows)
    assert n_rows % tm == 0, (name, n_rows, tm)
    specs, args = [], []
    for r in rows:
        if isinstance(r, tuple) and len(r) == 3:
            arr, width, cb = r
            specs.append(pl.BlockSpec((tm, width), functools.partial(lambda i, cb: (i, cb), cb=cb)))
        elif isinstance(r, tuple):
            arr, slot = r
            specs.append(pl.BlockSpec((None, tm, arr.shape[2]), functools.partial(lambda i, s: (s, i, 0), s=slot)))
        else:
            arr = r
            specs.append(pl.BlockSpec((tm, arr.shape[1]), lambda i: (i, 0)))
        args.append(arr)
        assert arr.shape[-2] == n_rows, (name, arr.shape, n_rows)
    for cst in consts:
        specs.append(pl.BlockSpec(cst.shape, lambda i: (0, 0)))
        args.append(cst)
    n_r, n_c, n_o, n_d = len(rows), len(consts), len(outs) + len(touts), len(reds)
    out_shape = [jax.ShapeDtypeStruct((n_rows, c), dt) for c, dt in outs]
    out_specs = [pl.BlockSpec((tm, c), lambda i: (i, 0)) for c, _ in outs]
    out_shape += [jax.ShapeDtypeStruct((r, n_rows), dt) for r, dt in touts]
    out_specs += [pl.BlockSpec((r, tm), lambda i: (0, i)) for r, _ in touts]
    out_shape += [jax.ShapeDtypeStruct((1, c), F32) for c in reds]
    out_specs += [pl.BlockSpec((1, c), lambda i: (0, 0)) for c in reds]

    def body(*refs):
        ins = [r[...] for r in refs[:n_r + n_c]]
        o_refs = refs[n_r + n_c:n_r + n_c + n_o]
        d_refs = refs[n_r + n_c + n_o:]
        res = fn(*ins)
        res = res if isinstance(res, (tuple, list)) else (res,)
        assert len(res) == n_o + n_d, (name, len(res))
        for ref, val in zip(o_refs, res[:n_o]):
            ref[...] = val.astype(ref.dtype)
        first = pl.program_id(0) == 0
        for ref, val in zip(d_refs, res[n_o:]):
            @pl.when(first)
            def _(ref=ref, val=val):
                ref[...] = val

            @pl.when(jnp.logical_not(first))
            def _(ref=ref, val=val):
                ref[...] += val

    return _call(name, body, (n_rows // tm,), specs, out_specs, out_shape, args, carry=carry)


def _colsum(v):
    return jnp.sum(v.astype(F32), axis=0, keepdims=True)


_DIMS = {"nn": (((1,), (0,)), ((), ())), "nt": (((1,), (1,)), ((), ())), "tn": (((0,), (0,)), ((), ()))}


def _tile(dim, want):
    if dim <= want:
        return dim
    return max(t for t in range(128, want + 1, 128) if dim % t == 0)


def _dot(a, b, mode):
    return lax.dot_general(a.astype(MXU_DTYPE), b.astype(MXU_DTYPE), _DIMS[mode], preferred_element_type=F32)


def _mm(name, a, b, mode, *, bias=None, extras=(), epilogue=None, out_dtypes=(F32,), tm=1024, tn=1024, tk=1024,
        carry=None):
    if mode == "nn":
        (m, k), (_, n) = a.shape, b.shape
    elif mode == "nt":
        (m, k), (n, _) = a.shape, b.shape
    else:
        (k, m), (_, n) = a.shape, b.shape
    if k > tk:
        tk = 5 * tk
    tn = _tile(n, tn)
    tk = _tile(k, tk)
    nk = k // tk

    def vmem_bytes(rows):
        blocks = rows * tk * a.dtype.itemsize + tk * tn * b.dtype.itemsize
        blocks += sum(rows * tn * e.dtype.itemsize for e in extras)
        blocks += sum(rows * tn * jnp.dtype(dt).itemsize for dt in out_dtypes)
        return 2 * blocks + (rows * tn * 4 if nk > 1 else 0)

    tm = _tile(m, tm if mode == "tn" else 2 * tm)
    while vmem_bytes(tm) > 3 * VMEM_LIMIT // 4 and tm % 256 == 0:
        tm //= 2
    while nk == 1 and k > 1024 and (m // tm) * (n // tn) < 4 and tm % 256 == 0:
        tm //= 2
    assert m % tm == 0 and n % tn == 0 and k % tk == 0, (name, m, n, k)
    a_spec = {"nn": pl.BlockSpec((tm, tk), lambda i, j, kk: (i, kk)),
              "nt": pl.BlockSpec((tm, tk), lambda i, j, kk: (i, kk)),
              "tn": pl.BlockSpec((tk, tm), lambda i, j, kk: (kk, i))}[mode]
    b_spec = {"nn": pl.BlockSpec((tk, tn), lambda i, j, kk: (kk, j)),
              "nt": pl.BlockSpec((tn, tk), lambda i, j, kk: (j, kk)),
              "tn": pl.BlockSpec((tk, tn), lambda i, j, kk: (kk, j))}[mode]
    specs, args = [a_spec, b_spec], [a, b]
    if bias is not None:
        specs.append(pl.BlockSpec((1, tn), lambda i, j, kk: (0, j)))
        args.append(bias)
    for e in extras:
        specs.append(pl.BlockSpec((tm, tn), lambda i, j, kk: (i, j)))
        args.append(e)
    n_e, n_o = len(extras), len(out_dtypes)
    has_bias = bias is not None

    def body(*refs):
        a_ref, b_ref = refs[0], refs[1]
        pos = 2
        bias_ref = refs[pos] if has_bias else None
        pos += int(has_bias)
        e_refs = refs[pos:pos + n_e]
        o_refs = refs[pos + n_e:pos + n_e + n_o]
        acc_ref = refs[pos + n_e + n_o] if nk > 1 else None
        part = _dot(a_ref[...], b_ref[...], mode)

        def finish(r):
            if has_bias:
                r = r + bias_ref[...]
            res = epilogue(r, *[e[...] for e in e_refs]) if epilogue is not None else (r,)
            for ref, val in zip(o_refs, res):
                ref[...] = val.astype(ref.dtype)

        if nk == 1:
            finish(part)
        else:
            kk = pl.program_id(2)

            @pl.when(kk == 0)
            def _():
                acc_ref[...] = part

            @pl.when(kk > 0)
            def _():
                acc_ref[...] += part

            @pl.when(kk == nk - 1)
            def _():
                finish(acc_ref[...])

    res = _call(name, body, (m // tm, n // tn, nk), specs,
                [pl.BlockSpec((tm, tn), lambda i, j, kk: (i, j)) for _ in out_dtypes],
                [jax.ShapeDtypeStruct((m, n), dt) for dt in out_dtypes], args,
                scratch_shapes=[pltpu.VMEM((tm, tn), F32)] if nk > 1 else [], carry=carry)
    return res[0] if n_o == 1 else res


def _ssm_wgrads(u, dy, g_re, g_im, h_re, h_im, tk=1024):
    s = u.shape[0]
    tk = min(tk, s)
    nk = s // tk
    assert tk % N_SEG == 0

    def body(u_ref, dy_ref, gre_ref, gim_ref, hre_ref, him_ref, lre_ref, lim_ref, db_ref, dc_ref, dar_ref, dai_ref,
             pre_ref, pim_ref):
        kk = pl.program_id(1)
        u_blk, dy_blk = u_ref[...], dy_ref[...]
        g_r, g_i, h_r, h_i = gre_ref[...], gim_ref[...], hre_ref[...], him_ref[...]
        d_b = jnp.concatenate([_dot(u_blk, g_r, "tn"), _dot(u_blk, g_i, "tn")], axis=1)
        d_c = jnp.concatenate([_dot(h_r, dy_blk, "tn"), _dot(h_i, dy_blk, "tn")], axis=0)

        @pl.when(kk == 0)
        def _():
            first_row = lax.broadcasted_iota(jnp.int32, (N_SEG, CH_N), 0) == 0
            pre_ref[...] = jnp.where(first_row, 0.0, pltpu.roll(lre_ref[...], 1, 0))
            pim_ref[...] = jnp.where(first_row, 0.0, pltpu.roll(lim_ref[...], 1, 0))

        p_r = jnp.concatenate([pre_ref[...], h_r[:tk - N_SEG]], axis=0)
        p_i = jnp.concatenate([pim_ref[...], h_i[:tk - N_SEG]], axis=0)
        pre_ref[...] = h_r[tk - N_SEG:]
        pim_ref[...] = h_i[tk - N_SEG:]
        d_ar = jnp.sum(g_r * p_r + g_i * p_i, axis=0, keepdims=True)
        d_ai = jnp.sum(g_i * p_r - g_r * p_i, axis=0, keepdims=True)

        @pl.when(kk == 0)
        def _():
            db_ref[...] = d_b
            dc_ref[...] = d_c
            dar_ref[...] = d_ar
            dai_ref[...] = d_ai

        @pl.when(kk > 0)
        def _():
            db_ref[...] += d_b
            dc_ref[...] += d_c
            dar_ref[...] += d_ar
            dai_ref[...] += d_ai

    chan = pl.BlockSpec((tk, CH_W), lambda j, kk: (kk, j))
    state = pl.BlockSpec((tk, CH_N), lambda j, kk: (kk, j))
    last = pl.BlockSpec((N_SEG, CH_N), lambda j, kk: (s // N_SEG - 1, j))
    row = pl.BlockSpec((1, CH_N), lambda j, kk: (0, j))
    return pl.pallas_call(
        body, name="ssm_wgrads", grid=(SSM_CHUNKS, nk),
        in_specs=[chan, chan, state, state, state, state, last, last],
        out_specs=[pl.BlockSpec((None, CH_W, 2 * CH_N), lambda j, kk: (j, 0, 0)),
                   pl.BlockSpec((None, 2 * CH_N, CH_W), lambda j, kk: (j, 0, 0)), row, row],
        out_shape=[jax.ShapeDtypeStruct((SSM_CHUNKS, CH_W, 2 * CH_N), F32),
                   jax.ShapeDtypeStruct((SSM_CHUNKS, 2 * CH_N, CH_W), F32),
                   jax.ShapeDtypeStruct((1, N_STATE), F32), jax.ShapeDtypeStruct((1, N_STATE), F32)],
        scratch_shapes=[pltpu.VMEM((N_SEG, CH_N), F32)] * 2,
        compiler_params=_cparams(2))(u, dy, g_re, g_im, h_re, h_im, h_re, h_im)


SCAN_LB = 256


def _split_by_scan_block(mat, axis):
    halves = []
    for l in range(CH_N // SCAN_LB):
        re = lax.slice_in_dim(mat, l * SCAN_LB, (l + 1) * SCAN_LB, axis=axis)
        im = lax.slice_in_dim(mat, CH_N + l * SCAN_LB, CH_N + (l + 1) * SCAN_LB, axis=axis)
        halves.append(jnp.concatenate([re, im], axis=axis))
    return jnp.stack(halves, axis=1).reshape((-1,) + halves[0].shape[1:])


def _ssm_scan(name, chan, expand12, contract12, a_re, a_im, d_row, reverse, carry=None):
    s = chan.shape[0]
    seg_len = s // N_SEG
    n_sq = int(math.log2(seg_len))
    assert 2 ** n_sq == seg_len
    rb = min(512, s)
    per_chunk = CH_N // SCAN_LB

    def steps_of(r):
        return range(r * rb // N_SEG, (r + 1) * rb // N_SEG)

    def body(are_ref, aim_ref, ch_ref, e_ref, k_ref, d_ref, hre_ref, him_ref, o_ref, cp_ref, wre_ref, wim_ref, acc_ref,
             ere, eim, cre, cim):
        e_mat, k_mat = e_ref[...], k_ref[...]
        first_of_chunk = lax.rem(pl.program_id(0), per_chunk) == 0
        last_of_chunk = lax.rem(pl.program_id(0), per_chunk) == per_chunk - 1
        for r in range(s // rb):
            rows = slice(r * rb, (r + 1) * rb)

            @pl.when(first_of_chunk)
            def _(rows=rows, r=r):
                cp_ref[rows, :] = jnp.concatenate(
                    [ch_ref[pl.ds(k, N_SEG, stride=seg_len), :] for k in steps_of(r)], axis=0)

            w = _dot(cp_ref[rows, :], e_mat, "nt" if reverse else "nn")
            wre_ref[rows, :] = w[:, :SCAN_LB]
            wim_ref[rows, :] = w[:, SCAN_LB:]

        ar1 = are_ref[...]
        ai1 = -aim_ref[...] if reverse else aim_ref[...]
        ar = jnp.broadcast_to(ar1, (N_SEG, SCAN_LB))
        ai = jnp.broadcast_to(ai1, (N_SEG, SCAN_LB))

        def rows_of(k):
            kk = seg_len - 1 - k if reverse else k
            return pl.ds(pl.multiple_of(kk * N_SEG, N_SEG), N_SEG)

        def local(k, carry):
            hr, hi = carry
            rows = rows_of(k)
            nr = ar * hr - ai * hi + wre_ref[rows, :]
            ni = ar * hi + ai * hr + wim_ref[rows, :]
            hre_ref[rows, :] = nr
            him_ref[rows, :] = ni
            return nr, ni

        zero = jnp.zeros((N_SEG, SCAN_LB), F32)
        er, ei = lax.fori_loop(0, seg_len, local, (zero, zero))
        ere[...] = er
        eim[...] = ei
        pr, pi = ar1, ai1
        for _ in range(n_sq):
            pr, pi = pr * pr - pi * pi, 2.0 * pr * pi
        cr = jnp.zeros((1, SCAN_LB), F32)
        ci = jnp.zeros((1, SCAN_LB), F32)
        for jj in range(N_SEG):
            j = N_SEG - 1 - jj if reverse else jj
            cre[j:j + 1, :] = cr
            cim[j:j + 1, :] = ci
            er_j, ei_j = ere[j:j + 1, :], eim[j:j + 1, :]
            cr, ci = pr * cr - pi * ci + er_j, pr * ci + pi * cr + ei_j
        c_r, c_i = cre[...], cim[...]

        def fix(k, carry):
            qr, qi = carry
            rows = rows_of(k)
            hre_ref[rows, :] = hre_ref[rows, :] + (qr * c_r - qi * c_i)
            him_ref[rows, :] = him_ref[rows, :] + (qr * c_i + qi * c_r)
            return qr * ar - qi * ai, qr * ai + qi * ar

        lax.fori_loop(0, seg_len, fix, (ar, ai))

        for r in range(s // rb):
            rows = slice(r * rb, (r + 1) * rb)
            h_cat = jnp.concatenate([hre_ref[rows, :], him_ref[rows, :]], axis=1)
            part = _dot(h_cat, k_mat, "nt" if reverse else "nn")

            @pl.when(first_of_chunk)
            def _(rows=rows, part=part):
                acc_ref[rows, :] = part + d_ref[...] * cp_ref[rows, :]

            @pl.when(jnp.logical_not(first_of_chunk))
            def _(rows=rows, part=part):
                acc_ref[rows, :] += part

            @pl.when(last_of_chunk)
            def _(rows=rows, r=r):
                for i, k in enumerate(steps_of(r)):
                    o_ref[pl.ds(k, N_SEG, stride=seg_len), :] = acc_ref[rows.start + i * N_SEG:rows.start + (i + 1) * N_SEG, :]

    nblk = N_STATE // SCAN_LB
    blk = pl.BlockSpec((s, SCAN_LB), lambda b: (0, b))
    row = pl.BlockSpec((1, SCAN_LB), lambda b: (0, b))
    chan_blk = pl.BlockSpec((s, CH_W), lambda b: (0, b // per_chunk))
    res = _call(name, body, (nblk,),
                [row, row, chan_blk, pl.BlockSpec((None,) + expand12.shape[1:], lambda b: (b, 0, 0)),
                 pl.BlockSpec((None,) + contract12.shape[1:], lambda b: (b, 0, 0)),
                 pl.BlockSpec((1, CH_W), lambda b: (0, b // per_chunk))],
                [blk, blk, chan_blk, chan_blk],
                [jax.ShapeDtypeStruct((s, N_STATE), F32)] * 2 + [jax.ShapeDtypeStruct((s, SSM_WIDTH), F32)] * 2,
                (a_re, a_im, chan, expand12, contract12, d_row),
                scratch_shapes=[pltpu.VMEM((s, SCAN_LB), F32)] * 2 + [pltpu.VMEM((s, CH_W), F32)]
                + [pltpu.VMEM((N_SEG, SCAN_LB), F32)] * 4, carry=carry)
    return res[0], res[1], res[2], res[3]


def _disc(ldt, are, aim, bre, bim):
    dt = jnp.exp(ldt)
    mag = jnp.exp(are * dt)
    abr = mag * jnp.cos(aim * dt)
    abi = mag * jnp.sin(aim * dt)
    den = jnp.square(are) + jnp.square(aim)
    nr = abr - 1.0
    fre = (nr * are + abi * aim) / den
    fim = (abi * are - nr * aim) / den
    return abr, abi, fre * bre - fim * bim, fre * bim + fim * bre


def _ssm_disc_fwd(ldt, are, aim, bre, bim):
    def body(l_ref, ar_ref, ai_ref, br_ref, bi_ref, o0, o1, o2, o3):
        res = _disc(l_ref[...], ar_ref[...], ai_ref[...], br_ref[...], bi_ref[...])
        for ref, val in zip((o0, o1, o2, o3), res):
            ref[...] = val

    col = jax.ShapeDtypeStruct((N_STATE, 1), F32)
    mat = jax.ShapeDtypeStruct((N_STATE, SSM_GROUP), F32)
    return pl.pallas_call(body, name="ssm_disc_fwd", out_shape=[col, col, mat, mat],
                          in_specs=[VMEM_SPEC] * 5, out_specs=[VMEM_SPEC] * 4)(ldt, are, aim, bre, bim)


def _ssm_disc_bwd(ldt, are, aim, bre, bim, d_abr, d_abi, d_bbr, d_bbi):
    def body(l_ref, ar_ref, ai_ref, br_ref, bi_ref, c0, c1, c2, c3, g_ldt, g_are, g_aim, g_bre, g_bim):
        _, vjp = jax.vjp(_disc, l_ref[...], ar_ref[...], ai_ref[...], br_ref[...], bi_ref[...])
        dl, dar, dai, dbr, dbi = vjp((c0[...], c1[...], c2[...], c3[...]))
        state = lax.broadcasted_iota(jnp.int32, (N_STATE, SSM_GROUPS), 0)
        group = lax.broadcasted_iota(jnp.int32, (N_STATE, SSM_GROUPS), 1)
        pick = jnp.right_shift(state, 6) == group
        g_ldt[...] = jnp.sum(jnp.where(pick, dl, 0.0), axis=0, keepdims=True)
        g_are[...] = dar
        g_aim[...] = dai
        g_bre[...] = dbr
        g_bim[...] = dbi

    col = jax.ShapeDtypeStruct((N_STATE, 1), F32)
    mat = jax.ShapeDtypeStruct((N_STATE, SSM_GROUP), F32)
    return pl.pallas_call(body, name="ssm_disc_bwd",
                          out_shape=[jax.ShapeDtypeStruct((1, SSM_GROUPS), F32), col, col, mat, mat],
                          in_specs=[VMEM_SPEC] * 9, out_specs=[VMEM_SPEC] * 5,
                          compiler_params=pltpu.CompilerParams(vmem_limit_bytes=VMEM_LIMIT))(
        ldt, are, aim, bre, bim, d_abr, d_abi, d_bbr, d_bbi)


_EYE8 = np.eye(8, dtype=np.float32)


def _blockdiag_b(bb):
    t = bb.reshape(SSM_CHUNKS, 8, SSM_STATE, SSM_GROUP).transpose(0, 1, 3, 2)
    return jnp.einsum("igcn,gh->igchn", t, _EYE8).reshape(SSM_CHUNKS, CH_W, CH_N)


def _diag_of_b(m):
    t = jnp.einsum("igchn,gh->igcn", m.reshape(SSM_CHUNKS, 8, SSM_GROUP, 8, SSM_STATE), _EYE8)
    return t.transpose(0, 1, 3, 2).reshape(N_STATE, SSM_GROUP)


def _blockdiag_c(c):
    t = c.reshape(SSM_CHUNKS, 8, SSM_GROUP, SSM_STATE).transpose(0, 1, 3, 2)
    return jnp.einsum("ignc,gh->ignhc", t, _EYE8).reshape(SSM_CHUNKS, CH_N, CH_W)


def _diag_of_c(m):
    t = jnp.einsum("ignhc,gh->ignc", m.reshape(SSM_CHUNKS, 8, SSM_STATE, 8, SSM_GROUP), _EYE8)
    return t.transpose(0, 1, 3, 2).reshape(SSM_GROUPS, SSM_GROUP, SSM_STATE)


def _dilate(a, d):
    s, c = a.shape
    return a if d == 1 else a.reshape(s // d, d, c).transpose(1, 0, 2).reshape(s, c)


def _undilate(a, d):
    s, c = a.shape
    return a if d == 1 else a.reshape(d, s // d, c).transpose(1, 0, 2).reshape(s, c)


def _dilate_rows(a, d):
    r, s = a.shape
    return a if d == 1 else a.reshape(r, s // d, d).transpose(0, 2, 1).reshape(r, s)


ATT_T = 4
ATT_ROWS = ATT_T * ATT_BLK


def _window(prev_ref, cur_ref, i, sl):
    if i == 0:
        return jnp.concatenate([prev_ref[:, sl], cur_ref[0:ATT_BLK, sl]], axis=0)
    return cur_ref[(i - 1) * ATT_BLK:(i + 1) * ATT_BLK, sl]


def _band_valid(first_key):
    qi = lax.broadcasted_iota(jnp.int32, (ATT_BLK, 2 * ATT_BLK), 0)
    ki = lax.broadcasted_iota(jnp.int32, (ATT_BLK, 2 * ATT_BLK), 1)
    steps = qi + ATT_BLK - ki
    return (steps >= 0) & (steps <= ATT_BLK) & (ki >= first_key)


ATT_STATW = ATT_HPG * 128


def _stat(h):
    return slice(h * 128, (h + 1) * 128)


def _stat_rows(stat):
    n = stat.shape[0]
    heads = [stat[:, _stat(h)].T[0:1, :] for h in range(ATT_HPG)]
    return jnp.concatenate(heads + [jnp.zeros((8 - ATT_HPG, n), stat.dtype)], axis=0)


def _attn_specs(nb, width=ATT_GROUPW):
    cur = pl.BlockSpec((ATT_ROWS, width), lambda b: (b, 0))
    prev = pl.BlockSpec((ATT_BLK, width), lambda b: (jnp.maximum(b * ATT_T - 1, 0), 0))
    nxt = pl.BlockSpec((ATT_BLK, width), lambda b: (jnp.minimum((b + 1) * ATT_T, nb - 1), 0))
    return cur, prev, nxt


def _attn_fwd(tag, per_seq, q, k, v):
    s = q.shape[0]
    nb = s // ATT_BLK

    def body(q_ref, kc_ref, kp_ref, vc_ref, vp_ref, o_ref, lse_ref):
        bt = pl.program_id(0)
        for i in range(ATT_T):
            has_prev = lax.rem(bt * ATT_T + i, per_seq) > 0
            valid = _band_valid(jnp.where(has_prev, 0, ATT_BLK))
            rows = slice(i * ATT_BLK, (i + 1) * ATT_BLK)
            for h in range(ATT_HPG):
                sl = slice(h * ATT_HEAD_DIM, (h + 1) * ATT_HEAD_DIM)
                kcat = _window(kp_ref, kc_ref, i, sl)
                vcat = _window(vp_ref, vc_ref, i, sl)
                sc = _dot(q_ref[rows, sl], kcat, "nt") * ATT_SCALE
                sc = jnp.where(valid, sc, NEG_INF)
                m = jnp.max(sc, axis=-1, keepdims=True)
                p = jnp.exp(sc - m)
                den = jnp.sum(p, axis=-1, keepdims=True)
                o_ref[rows, sl] = _dot(p, vcat, "nn") / den
                lse_ref[rows, _stat(h)] = jnp.broadcast_to(m + jnp.log(den), (ATT_BLK, 128))

    cur, prev, _ = _attn_specs(nb)
    stat, _, _ = _attn_specs(nb, ATT_STATW)
    return pl.pallas_call(
        body, name="attn_fwd_" + tag, grid=(nb // ATT_T,), in_specs=[cur, cur, prev, cur, prev], out_specs=[cur, stat],
        out_shape=[jax.ShapeDtypeStruct((s, ATT_GROUPW), F32), jax.ShapeDtypeStruct((s, ATT_STATW), F32)],
        compiler_params=_cparams(1))(q, k, k, v, v)


def _attn_dq(tag, per_seq, q, k, v, do, lse, delta):
    s = q.shape[0]
    nb = s // ATT_BLK

    def body(q_ref, kc_ref, kp_ref, vc_ref, vp_ref, do_ref, lse_ref, dl_ref, dq_ref):
        bt = pl.program_id(0)
        for i in range(ATT_T):
            has_prev = lax.rem(bt * ATT_T + i, per_seq) > 0
            valid = _band_valid(jnp.where(has_prev, 0, ATT_BLK))
            rows = slice(i * ATT_BLK, (i + 1) * ATT_BLK)
            for h in range(ATT_HPG):
                sl = slice(h * ATT_HEAD_DIM, (h + 1) * ATT_HEAD_DIM)
                kcat = _window(kp_ref, kc_ref, i, sl)
                vcat = _window(vp_ref, vc_ref, i, sl)
                lse = jnp.concatenate([lse_ref[rows, _stat(h)]] * 2, axis=1)
                dlt = jnp.concatenate([dl_ref[rows, _stat(h)]] * 2, axis=1)
                sc = _dot(q_ref[rows, sl], kcat, "nt") * ATT_SCALE
                p = jnp.exp(jnp.where(valid, sc, NEG_INF) - lse)
                dp = _dot(do_ref[rows, sl], vcat, "nt")
                ds = p * (dp - dlt) * ATT_SCALE
                dq_ref[rows, sl] = _dot(ds, kcat, "nn")

    cur, prev, _ = _attn_specs(nb)
    stat, _, _ = _attn_specs(nb, ATT_STATW)
    return pl.pallas_call(
        body, name="attn_dq_" + tag, grid=(nb // ATT_T,), in_specs=[cur, cur, prev, cur, prev, cur, stat, stat],
        out_specs=cur, out_shape=jax.ShapeDtypeStruct((s, ATT_GROUPW), F32),
        compiler_params=_cparams(1))(q, k, k, v, v, do, lse, delta)


def _attn_dkv(tag, per_seq, q, k, v, do, lse_t, delta_t):
    s = q.shape[0]
    nb = s // ATT_BLK

    def body(k_ref, v_ref, qc_ref, qn_ref, doc_ref, don_ref, lc_ref, ln_ref, dc_ref, dn_ref, dk_ref, dv_ref):
        bt = pl.program_id(0)
        ki = lax.broadcasted_iota(jnp.int32, (ATT_BLK, 2 * ATT_BLK), 0)
        ci = lax.broadcasted_iota(jnp.int32, (ATT_BLK, 2 * ATT_BLK), 1)

        def pair(edge_ref, cur_ref, i, sl):
            if i == ATT_T - 1:
                return jnp.concatenate([cur_ref[i * ATT_BLK:(i + 1) * ATT_BLK, sl], edge_ref[:, sl]], axis=0)
            return cur_ref[i * ATT_BLK:(i + 2) * ATT_BLK, sl]

        def pair_row(edge_ref, cur_ref, i, h):
            if i == ATT_T - 1:
                row = jnp.concatenate([cur_ref[h:h + 1, i * ATT_BLK:(i + 1) * ATT_BLK], edge_ref[h:h + 1, :]], axis=1)
            else:
                row = cur_ref[h:h + 1, i * ATT_BLK:(i + 2) * ATT_BLK]
            return jnp.broadcast_to(row, (ATT_BLK, 2 * ATT_BLK))

        for i in range(ATT_T):
            b = bt * ATT_T + i
            next_uses = (b + 1 < nb) & (lax.rem(b + 1, per_seq) > 0)
            reach = jnp.where(next_uses, 0, 4 * ATT_BLK)
            valid = ((ci < ATT_BLK) & (ci >= ki)) | ((ci >= ATT_BLK) & (ki - ci + ATT_BLK >= reach))
            rows = slice(i * ATT_BLK, (i + 1) * ATT_BLK)
            for h in range(ATT_HPG):
                sl = slice(h * ATT_HEAD_DIM, (h + 1) * ATT_HEAD_DIM)
                qcat, docat = pair(qn_ref, qc_ref, i, sl), pair(don_ref, doc_ref, i, sl)
                sc = _dot(k_ref[rows, sl], qcat, "nt") * ATT_SCALE
                p = jnp.exp(jnp.where(valid, sc, NEG_INF) - pair_row(ln_ref, lc_ref, i, h))
                dv_ref[rows, sl] = _dot(p, docat, "nn")
                dp = _dot(v_ref[rows, sl], docat, "nt")
                ds = p * (dp - pair_row(dn_ref, dc_ref, i, h)) * ATT_SCALE
                dk_ref[rows, sl] = _dot(ds, qcat, "nn")

    cur, _, nxt = _attn_specs(nb)
    stat = pl.BlockSpec((8, ATT_ROWS), lambda b: (0, b))
    snxt = pl.BlockSpec((8, ATT_BLK), lambda b: (0, jnp.minimum((b + 1) * ATT_T, nb - 1)))
    return pl.pallas_call(
        body, name="attn_dkv_" + tag, grid=(nb // ATT_T,), in_specs=[cur, cur, cur, nxt, cur, nxt, stat, snxt, stat, snxt],
        out_specs=[cur, cur], out_shape=[jax.ShapeDtypeStruct((s, ATT_GROUPW), F32)] * 2,
        compiler_params=_cparams(1))(k, v, q, q, do, do, lse_t, lse_t, delta_t, delta_t)


def _xattn_probs(q, kh):
    sc = _dot(q, kh, "nt") * XATT_SCALE
    e = jnp.exp(sc - jnp.max(sc, axis=-1, keepdims=True))
    return e / jnp.sum(e, axis=-1, keepdims=True)


def _xattn_fwd(q, kv, tm=512):
    s = q.shape[0]
    tm = min(tm, s)

    def body(q_ref, kv_ref, o_ref):
        for h in range(XATT_HEADS):
            sl = slice(h * XATT_HEAD_DIM, (h + 1) * XATT_HEAD_DIM)
            vs = slice(D_MODEL + h * XATT_HEAD_DIM, D_MODEL + (h + 1) * XATT_HEAD_DIM)
            p = _xattn_probs(q_ref[:, sl], kv_ref[:, sl])
            o_ref[:, sl] = _dot(p, kv_ref[:, vs], "nn").astype(o_ref.dtype)

    return pl.pallas_call(
        body, name="xattn_fwd", grid=(s // tm,),
        in_specs=[pl.BlockSpec((tm, D_MODEL), lambda i: (i, 0)), pl.BlockSpec(kv.shape, lambda i: (0, 0))],
        out_specs=pl.BlockSpec((tm, D_MODEL), lambda i: (i, 0)),
        out_shape=jax.ShapeDtypeStruct((s, D_MODEL), MXU_DTYPE), compiler_params=_cparams(1))(q, kv)


def _xattn_bwd(q, kv, do, tm=512):
    s = q.shape[0]
    tm = min(tm, s)

    def body(q_ref, kv_ref, do_ref, dq_ref, dkv_ref):
        first = pl.program_id(0) == 0

        @pl.when(first)
        def _():
            dkv_ref[...] = jnp.zeros_like(dkv_ref)

        for h in range(XATT_HEADS):
            sl = slice(h * XATT_HEAD_DIM, (h + 1) * XATT_HEAD_DIM)
            vs = slice(D_MODEL + h * XATT_HEAD_DIM, D_MODEL + (h + 1) * XATT_HEAD_DIM)
            p = _xattn_probs(q_ref[:, sl], kv_ref[:, sl])
            dkv_ref[:, vs] += _dot(p, do_ref[:, sl], "tn")
            dp = _dot(do_ref[:, sl], kv_ref[:, vs], "nt")
            ds = p * (dp - jnp.sum(dp * p, axis=-1, keepdims=True)) * XATT_SCALE
            dq_ref[:, sl] = _dot(ds, kv_ref[:, sl], "nn").astype(dq_ref.dtype)
            dkv_ref[:, sl] += _dot(ds, q_ref[:, sl], "tn")

    row = pl.BlockSpec((tm, D_MODEL), lambda i: (i, 0))
    whole = pl.BlockSpec(kv.shape, lambda i: (0, 0))
    return pl.pallas_call(
        body, name="xattn_bwd", grid=(s // tm,), in_specs=[row, whole, row], out_specs=[row, whole],
        out_shape=[jax.ShapeDtypeStruct((s, D_MODEL), MXU_DTYPE), jax.ShapeDtypeStruct(kv.shape, F32)],
        compiler_params=_cparams(1))(q, kv, do)


def _ln(x, g, b):
    mu = jnp.mean(x, axis=-1, keepdims=True)
    xc = x - mu
    var = jnp.mean(jnp.square(xc), axis=-1, keepdims=True)
    return xc * lax.rsqrt(var + LN_EPS) * g + b


def _res_ln(h, o, g, b):
    return _ln(DEEPNORM_ALPHA * h + o, g, b)


def _gate(gs, ga, z1, z2, batt):
    return jax.nn.sigmoid(gs) * (z1 * jax.nn.sigmoid(z2)) + jax.nn.sigmoid(ga) * batt


def _rope_tables(pos, invf, m1, m2):
    ang = pos.astype(F32) * invf
    sin = jnp.sin(ang)
    return jnp.cos(ang), -sin * m1, sin * m2


def _rope(t, cos, s_up, s_dn):
    w = t.shape[-1]
    return t * cos + pltpu.roll(t, w - ROT_DIM // 2, 1) * s_up + pltpu.roll(t, ROT_DIM // 2, 1) * s_dn


def _rope_t(dt, cos, s_up, s_dn):
    w = dt.shape[-1]
    return dt * cos + pltpu.roll(dt * s_up, ROT_DIM // 2, 1) + pltpu.roll(dt * s_dn, w - ROT_DIM // 2, 1)


def _rope_consts():
    inv_freq = ROPE_THETA ** (-jnp.arange(0, ROT_DIM, 2, dtype=F32) / ROT_DIM)
    d = np.arange(ATT_GROUPW) % ATT_HEAD_DIM
    invf = jnp.where(d < ROT_DIM, inv_freq[d % (ROT_DIM // 2)], 0.0).reshape(1, ATT_GROUPW).astype(F32)
    m1 = jnp.asarray((d < ROT_DIM // 2).astype(np.float32)).reshape(1, ATT_GROUPW)
    m2 = jnp.asarray(((d >= ROT_DIM // 2) & (d < ROT_DIM)).astype(np.float32)).reshape(1, ATT_GROUPW)
    return invf, m1, m2


def _head_sum_matrix():
    d = np.arange(ATT_GROUPW) // ATT_HEAD_DIM
    s = np.arange(ATT_STATW) // 128
    return jnp.asarray((d[:, None] == s[None, :]).astype(np.float32))


def _adamw(w, g, m, v):
    m = ADAM_B1 * m + (1.0 - ADAM_B1) * g
    v = ADAM_B2 * v + (1.0 - ADAM_B2) * jnp.square(g)
    m_hat = m / (1.0 - ADAM_B1 ** ADAM_STEP)
    v_hat = v / (1.0 - ADAM_B2 ** ADAM_STEP)
    delta = -ADAM_LR * (m_hat / (jnp.sqrt(v_hat) + ADAM_EPS) + ADAM_WD * w)
    return delta, m, v


def _local_step(x, mem, pos, target, sp, ex):
    s = x.shape[0]
    al = DEEPNORM_ALPHA
    mx = MXU_DTYPE

    h0, h0b = _rowwise("ln_in", lambda x, g, b: (lambda h: (h, h))(_ln(x, g, b)), [x],
                       [sp["ln_in_g"], sp["ln_in_b"]], [(D_MODEL, F32), (D_MODEL, mx)],
                       carry=ex.gather_carry(["w_in"]))
    proj = _mm("proj", h0b, ex.weight("w_in"), "nn", bias=sp["b_in"],
               carry=ex.gather_carry(["w_glu", "w_att_up", "w_mix_out", "w_xq", "w_xkv"]))

    ldt = jnp.repeat(sp["ssm_log_dt"].reshape(SSM_GROUPS), SSM_STATE).reshape(N_STATE, 1)
    are, aim = sp["ssm_a_re"].reshape(N_STATE, 1), sp["ssm_a_im"].reshape(N_STATE, 1)
    bre, bim = sp["ssm_b_re"].reshape(N_STATE, SSM_GROUP), sp["ssm_b_im"].reshape(N_STATE, SSM_GROUP)
    abr, abi, bbr, bbi = _ssm_disc_fwd(ldt, are, aim, bre, bim)
    a_re, a_im = abr.reshape(1, N_STATE), abi.reshape(1, N_STATE)
    bexp = jnp.concatenate([_blockdiag_b(bbr), _blockdiag_b(bbi)], axis=2).astype(mx)
    cexp = jnp.concatenate([_blockdiag_c(sp["ssm_c_re"].reshape(SSM_GROUPS, SSM_GROUP, SSM_STATE)),
                            -_blockdiag_c(sp["ssm_c_im"].reshape(SSM_GROUPS, SSM_GROUP, SSM_STATE))],
                           axis=1).astype(mx)
    b12, c12 = _split_by_scan_block(bexp, 2), _split_by_scan_block(cexp, 1)
    h_re, h_im, y, u_p = _ssm_scan("ssm_scan_fwd", proj, b12, c12, a_re, a_im, sp["ssm_d"], reverse=False,
                                   carry=ex.gather_carry(["w_ff1", "w_ff2"]))
    ygb, = _rowwise("gelu", lambda y: jax.nn.gelu(y), [y], [], [(SSM_WIDTH, mx)])
    z = _mm("glu", ygb, ex.weight("w_glu"), "nn", bias=sp["b_glu"], carry=ex.gather_carry(["w_xo"]))

    invf, m1, m2 = _rope_consts()

    def rope_fwd(pos, q0, q1, q2, k0, k1, k2, v0, v1, v2, invf, m1, m2):
        tabs = _rope_tables(pos, invf, m1, m2)
        return tuple(_rope(t, *tabs) for t in (q0, q1, q2, k0, k1, k2)) + (v0, v1, v2)

    qkv_cols = [(proj, ATT_GROUPW, 3 + i) for i in range(9)]
    qkv = _rowwise("rope", rope_fwd, [pos] + qkv_cols, [invf, m1, m2], [(ATT_GROUPW, mx)] * 9)
    n_blocks = s // ATT_BLK
    groups = [(str(g), n_blocks // d, d) for g, d in enumerate(DILATIONS)]
    q_d = [_dilate(qkv[g], d) for g, d in enumerate(DILATIONS)]
    k_d = [_dilate(qkv[3 + g], d) for g, d in enumerate(DILATIONS)]
    v_d = [_dilate(qkv[6 + g], d) for g, d in enumerate(DILATIONS)]
    o_g, l_g = [], []
    for g, (tag, per_seq, d) in enumerate(groups):
        o, lse = _attn_fwd(tag, per_seq, q_d[g], k_d[g], v_d[g])
        o_g.append(_undilate(o, d))
        l_g.append(_undilate(lse, d))

    def merge(o0, o1, o2, l0, l1, l2):
        m = jnp.maximum(jnp.maximum(l0, l1), l2)
        e0, e1, e2 = jnp.exp(l0 - m), jnp.exp(l1 - m), jnp.exp(l2 - m)
        tot = e0 + e1 + e2

        def per_dim(e):
            w = e / tot
            return jnp.concatenate([w[:, h * 128:h * 128 + ATT_HEAD_DIM] for h in range(ATT_HPG)], axis=1)

        att = per_dim(e0) * o0 + per_dim(e1) * o1 + per_dim(e2) * o2
        lse = m + jnp.log(tot)
        return att, att, lse, _stat_rows(lse)

    att, attb, lse_tot, lse_tot_t = _rowwise("attn_merge", merge, o_g + l_g, [],
                                             [(ATT_GROUPW, F32), (ATT_GROUPW, mx), (ATT_STATW, F32)], touts=[(8, F32)])
    batt = _mm("att_up", attb, ex.weight("w_att_up"), "nn")

    gate_rows = [(proj, D_MODEL, 3), (proj, D_MODEL, 4), (z, D_MODEL, 0), (z, D_MODEL, 1), batt]
    mixedb, = _rowwise("gate", _gate, gate_rows, [], [(D_MODEL, mx)])
    o1 = _mm("mix_out", mixedb, ex.weight("w_mix_out"), "nn", bias=sp["b_mix_out"])
    h1, h1b = _rowwise("ln1", lambda h, o, g, b: (lambda r: (r, r))(_res_ln(h, o, g, b)), [h0, o1],
                       [sp["ln1_g"], sp["ln1_b"]], [(D_MODEL, F32), (D_MODEL, mx)])

    qx = _mm("xq", h1b, ex.weight("w_xq"), "nn", out_dtypes=(mx,))
    kvx = _mm("xkv", mem, ex.weight("w_xkv"), "nn", out_dtypes=(mx,))
    oxb = _xattn_fwd(qx, kvx)
    o2 = _mm("xo", oxb, ex.weight("w_xo"), "nn")
    h2, h2b = _rowwise("ln2", lambda h, o, g, b: (lambda r: (r, r))(_res_ln(h, o, g, b)), [h1, o2],
                       [sp["ln2_g"], sp["ln2_b"]], [(D_MODEL, F32), (D_MODEL, mx)])

    a_ff, fb = _mm("ff1", h2b, ex.weight("w_ff1"), "nn", bias=sp["b_ff1"],
                   epilogue=lambda r: (r, jnp.square(jnp.maximum(r, 0.0))), out_dtypes=(F32, mx))
    o3 = _mm("ff2", fb, ex.weight("w_ff2"), "nn", bias=sp["b_ff2"])

    def loss_bwd(h2, o3, tgt, g, b):
        def f(h2, o3, g, b):
            h3 = _res_ln(h2, o3, g, b)
            return 0.5 * jnp.sum(jnp.mean(jnp.square(h3 - tgt), axis=-1))

        loss, vjp = jax.vjp(f, h2, o3, g, b)
        _, dr, dg, db = vjp(jnp.ones((), F32))
        return dr, dr, dg, db, _colsum(dr), jnp.full((1, 128), loss, F32)

    dr3, dr3b, g_ln3_g, g_ln3_b, g_b_ff2, loss = _rowwise(
        "loss_ln3_bwd", loss_bwd, [h2, o3, target], [sp["ln3_g"], sp["ln3_b"]],
        [(D_MODEL, F32), (D_MODEL, mx)], [D_MODEL, D_MODEL, D_MODEL, 128])

    dab = _mm("ff2_dx", dr3b, ex.weight("w_ff2"), "nt", extras=(a_ff,),
              epilogue=lambda r, a: (r * (2.0 * jnp.maximum(a, 0.0)),), out_dtypes=(mx,))
    ex.grad("w_ff2", _mm("ff2_dw", fb, dr3b, "tn"))
    g_b_ff1, = _rowwise("ff1_db", lambda v: (_colsum(v),), [dab], [], [], [D_FF])
    ex.grad("w_ff1", _mm("ff1_dw", h2b, dab, "tn", carry=ex.carry(swap=["w_ff2"])))
    dh2 = _mm("ff1_dx", dab, ex.weight("w_ff1"), "nt", extras=(dr3,), epilogue=lambda r, d: (r + al * d,),
              carry=ex.carry(swap=["w_ff1"], ici=["w_ff2"]))

    def ln_bwd(h, o, dout, g, b):
        _, vjp = jax.vjp(_res_ln, h, o, g, b)
        _, dr, dg, db = vjp(dout)
        return dr, dr, dg, db, _colsum(dr)

    dr2, dr2b, g_ln2_g, g_ln2_b, _ = _rowwise(
        "ln2_bwd", ln_bwd, [h1, o2, dh2], [sp["ln2_g"], sp["ln2_b"]],
        [(D_MODEL, F32), (D_MODEL, mx)], [D_MODEL, D_MODEL, D_MODEL])
    ex.grad("w_xo", _mm("xo_dw", oxb, dr2b, "tn"))
    doxb = _mm("xo_dx", dr2b, ex.weight("w_xo"), "nt", out_dtypes=(mx,), carry=ex.carry(swap=["w_xo"]))
    dqxb, dkvx = _xattn_bwd(qx, kvx, doxb)
    ex.grad("w_xq", _mm("xq_dw", h1b, dqxb, "tn", carry=ex.carry(ici=["w_xo"])))
    dh1 = _mm("xq_dx", dqxb, ex.weight("w_xq"), "nt", extras=(dr2,), epilogue=lambda r, d: (r + al * d,),
              carry=ex.carry(swap=["w_xq"]))
    ex.grad("w_xkv", _mm("xkv_dw", mem, dkvx, "tn"))

    dr1, dr1b, g_ln1_g, g_ln1_b, g_b_mix = _rowwise(
        "ln1_bwd", ln_bwd, [h0, o1, dh1], [sp["ln1_g"], sp["ln1_b"]],
        [(D_MODEL, F32), (D_MODEL, mx)], [D_MODEL, D_MODEL, D_MODEL])
    ex.grad("w_mix_out", _mm("mix_dw", mixedb, dr1b, "tn", carry=ex.carry(swap=["w_xkv"], ici=["w_xq"])))
    dmixed = _mm("mix_dx", dr1b, ex.weight("w_mix_out"), "nt", carry=ex.carry(swap=["w_mix_out"]))

    def gate_bwd(gs, ga, z1, z2, batt, dm):
        _, vjp = jax.vjp(_gate, gs, ga, z1, z2, batt)
        dgs, dga, dz1, dz2, dbatt = vjp(dm)
        dz = jnp.concatenate([dz1, dz2], axis=-1)
        return dgs, dga, dz, dbatt, _colsum(dz)

    dgsb, dgab, dzb, dbattb, g_b_glu = _rowwise(
        "gate_bwd", gate_bwd, gate_rows + [dmixed], [],
        [(D_MODEL, mx), (D_MODEL, mx), (2 * D_MODEL, mx), (D_MODEL, mx)], [2 * D_MODEL])
    ex.grad("w_att_up", _mm("att_up_dw", attb, dbattb, "tn", carry=ex.carry(ici=["w_mix_out"])))
    datt = _mm("att_up_dx", dbattb, ex.weight("w_att_up"), "nt", carry=ex.carry(swap=["w_att_up"]))

    def att_delta(datt, att, hs):
        dl = jnp.dot(datt * att, hs, precision=lax.Precision.HIGHEST, preferred_element_type=F32)
        return datt, dl, _stat_rows(dl)

    dattb, delta, delta_t = _rowwise("attn_delta", att_delta, [datt, att], [_head_sum_matrix()],
                                     [(ATT_GROUPW, mx), (ATT_STATW, F32)], touts=[(8, F32)])
    dq_g, dk_g, dv_g = [], [], []
    for g, (tag, per_seq, d) in enumerate(groups):
        do_d, lt_d, dl_d = _dilate(dattb, d), _dilate(lse_tot, d), _dilate(delta, d)
        dq_g.append(_undilate(_attn_dq(tag, per_seq, q_d[g], k_d[g], v_d[g], do_d, lt_d, dl_d), d))
        dk, dv = _attn_dkv(tag, per_seq, q_d[g], k_d[g], v_d[g], do_d, _dilate_rows(lse_tot_t, d), _dilate_rows(delta_t, d))
        dk_g.append(_undilate(dk, d))
        dv_g.append(_undilate(dv, d))
    dqkv = dq_g + dk_g + dv_g

    def rope_bwd(pos, q0, q1, q2, k0, k1, k2, v0, v1, v2, invf, m1, m2):
        tabs = _rope_tables(pos, invf, m1, m2)
        return jnp.concatenate([_rope_t(t, *tabs) for t in (q0, q1, q2, k0, k1, k2)] + [v0, v1, v2], axis=-1)

    dqkvb, = _rowwise("rope_bwd", rope_bwd, [pos] + dqkv, [invf, m1, m2], [(9 * ATT_GROUPW, mx)])

    ex.grad("w_glu", _mm("glu_dw", ygb, dzb, "tn", carry=ex.carry(ici=["w_xkv", "w_att_up"])))
    dyg = _mm("glu_dx", dzb, ex.weight("w_glu"), "nt", carry=ex.carry(swap=["w_glu"]))

    def gelu_bwd(y, dyg):
        _, vjp = jax.vjp(jax.nn.gelu, y)
        return vjp(dyg)[0]

    dy, = _rowwise("gelu_bwd", gelu_bwd, [y, dyg], [], [(SSM_WIDTH, F32)])
    s_re, s_im, du, dy_p = _ssm_scan("ssm_scan_bwd", dy, c12, b12, a_re, a_im, sp["ssm_d"], reverse=True,
                                     carry=ex.carry(ici=["w_ff1", "w_glu"]))
    g_bexp, g_cexp, d_abr, d_abi = _ssm_wgrads(u_p, dy_p, s_re, s_im, h_re, h_im)
    g_ssm_d, = _rowwise("ssm_dd", lambda a, b: (_colsum(a * b),), [dy_p, u_p], [], [], [SSM_WIDTH])
    g_ldt, g_are, g_aim, g_bre, g_bim = _ssm_disc_bwd(
        ldt, are, aim, bre, bim, d_abr.reshape(N_STATE, 1), d_abi.reshape(N_STATE, 1),
        _diag_of_b(g_bexp[:, :, :CH_N]), _diag_of_b(g_bexp[:, :, CH_N:]))
    g_c_re = _diag_of_c(g_cexp[:, :CH_N, :])
    g_c_im = -_diag_of_c(g_cexp[:, CH_N:, :])
    dub = du.astype(mx)

    dprojb = jnp.concatenate([dub, dqkvb, dgsb, dgab], axis=-1)
    g_b_in, = _rowwise("in_db", lambda v: (_colsum(v),), [dprojb], [], [], [IN_COLS])
    ex.grad("w_in", _mm("in_dw", h0b, dprojb, "tn"))
    dh0 = _mm("in_dx", dprojb, ex.weight("w_in"), "nt", extras=(dr1,), epilogue=lambda r, d: (r + al * d,),
              carry=ex.carry(ici=["w_in"]))

    def ln_in_bwd(x, dout, g, b):
        _, vjp = jax.vjp(_ln, x, g, b)
        return vjp(dout)

    dx, g_ln_in_g, g_ln_in_b = _rowwise("ln_in_bwd", ln_in_bwd, [x, dh0], [sp["ln_in_g"], sp["ln_in_b"]],
                                        [(D_MODEL, F32)], [D_MODEL, D_MODEL])

    small = {"ln_in_g": g_ln_in_g, "ln_in_b": g_ln_in_b, "b_in": g_b_in, "ssm_log_dt": g_ldt, "ssm_a_re": g_are,
             "ssm_a_im": g_aim, "ssm_b_re": g_bre, "ssm_b_im": g_bim, "ssm_c_re": g_c_re, "ssm_c_im": g_c_im,
             "ssm_d": g_ssm_d, "b_glu": g_b_glu, "b_mix_out": g_b_mix, "ln1_g": g_ln1_g, "ln1_b": g_ln1_b,
             "ln2_g": g_ln2_g, "ln2_b": g_ln2_b, "b_ff1": g_b_ff1, "b_ff2": g_b_ff2, "ln3_g": g_ln3_g,
             "ln3_b": g_ln3_b}
    return loss, dx, small


def _piece_shape(k, n, axis):
    return (k // 2, n // 4) if axis == 1 else (k // 8, n)


def _aligned(v, m):
    return v if isinstance(v, int) else pl.multiple_of(v, m)


def _full_piece(ref, k, n, axis, chip, half):
    pr, pc = _piece_shape(k, n, axis)
    if axis == 1:
        return ref.at[pl.ds(_aligned(half * pr, 8), pr), pl.ds(_aligned(chip * pc, 128), pc)]
    return ref.at[pl.ds(_aligned(chip * (2 * pr) + half * pr, 8), pr), :]


def _full_shard(ref, k, n, axis, chip):
    if axis == 1:
        return ref.at[:, pl.ds(_aligned(chip * (n // 4), 128), n // 4)]
    return ref.at[pl.ds(_aligned(chip * (k // 4), 8), k // 4), :]


def _shard_piece(ref, k, n, axis, half):
    pr, _ = _piece_shape(k, n, axis)
    return ref.at[pl.ds(_aligned(half * pr, 8), pr), :]


def _mesh_pos():
    x, y, c = lax.axis_index("x"), lax.axis_index("y"), lax.axis_index("c")
    other_chips = [(1 - x, y), (x, 1 - y), (1 - x, 1 - y)]
    return x, y, c, other_chips


def _remote(src, dst, send_sem, recv_sem, dev):
    return pltpu.make_async_remote_copy(src_ref=src, dst_ref=dst, send_sem=send_sem, recv_sem=recv_sem,
                                        device_id=dev, device_id_type=MESH)


def _placed(name, fn, n_steps, where, ins, out_sds, out_block, out_index):
    def body(w_ref, *refs):
        o_ref = refs[-1]
        o_ref[...] = fn(*[r[...] for r in refs[:-1]]).astype(o_ref.dtype)

    grid_spec = pltpu.PrefetchScalarGridSpec(
        num_scalar_prefetch=1, grid=(n_steps,), in_specs=[pl.BlockSpec(bs, idx) for _, bs, idx in ins],
        out_specs=pl.BlockSpec(out_block, out_index))
    return pl.pallas_call(body, name=name, grid_spec=grid_spec, out_shape=out_sds,
                          compiler_params=_cparams(1))(where, *[a for a, _, _ in ins])


def _gather_copies(widx):
    geo = [BIG[i][1:] for i in widx]

    def ici(full, wi, j, chip, send_sems, recv_sems, c, dev):
        k, n, ax = geo[wi]
        piece = _full_piece(full[wi], k, n, ax, chip, c)
        return _remote(piece, piece, send_sems.at[wi * 6 + j], recv_sems.at[wi * 6 + j], dev)

    def d2d(full, wi, j, chip, half, send_sems, recv_sems, sib):
        k, n, ax = geo[wi]
        piece = _full_piece(full[wi], k, n, ax, chip, half)
        return _remote(piece, piece, send_sems.at[wi * 6 + 3 + j], recv_sems.at[wi * 6 + 3 + j], sib)

    def start(_, full, send_sems, recv_sems):
        x, y, c, chips = _mesh_pos()
        for wi in range(len(geo)):
            for j, (qx, qy) in enumerate(chips):
                ici(full, wi, j, 2 * x + y, send_sems, recv_sems, c, (qx, qy, c)).start()

    def finish(_, full, send_sems, recv_sems):
        x, y, c, chips = _mesh_pos()
        sib = (x, y, 1 - c)
        for wi in range(len(geo)):
            for j, (qx, qy) in enumerate(chips):
                ici(full, wi, j, 2 * qx + qy, send_sems, recv_sems, c, (qx, qy, c)).wait_recv()
                d2d(full, wi, j, 2 * qx + qy, c, send_sems, recv_sems, sib).start()
        for wi in range(len(geo)):
            for j, (qx, qy) in enumerate(chips):
                d2d(full, wi, j, 2 * qx + qy, 1 - c, send_sems, recv_sems, sib).wait_recv()
        for wi in range(len(geo)):
            for j, (qx, qy) in enumerate(chips):
                ici(full, wi, j, 2 * x + y, send_sems, recv_sems, c, (qx, qy, c)).wait_send()
                d2d(full, wi, j, 2 * qx + qy, c, send_sems, recv_sems, sib).wait_send()

    return start, finish, 6 * len(geo)


def _gather_weights(tag, fulls, widx):
    nw = len(widx)
    start, finish, n_sems = _gather_copies(widx)

    def body(*refs):
        full = refs[nw:2 * nw]
        start(None, full, *refs[2 * nw:])
        finish(None, full, *refs[2 * nw:])

    return pl.pallas_call(
        body, name="gather_weights_" + tag, in_specs=[HBM_SPEC] * nw, out_specs=[HBM_SPEC] * nw,
        out_shape=[jax.ShapeDtypeStruct(f.shape, f.dtype) for f in fulls],
        input_output_aliases={i: i for i in range(nw)},
        scratch_shapes=[pltpu.SemaphoreType.DMA((n_sems,)), pltpu.SemaphoreType.DMA((n_sems,))])(*fulls)


def _swap_copies(widx):
    geo = [BIG[i][1:] for i in widx]

    def copies(g, got, send_sems, recv_sems, base):
        x, y, c, _ = _mesh_pos()
        return [_remote(_full_piece(g[wi], k, n, ax, q, 1 - c), got[wi].at[q], send_sems.at[base + wi * 4 + q],
                        recv_sems.at[base + wi * 4 + q], (x, y, 1 - c))
                for wi, (k, n, ax) in enumerate(geo) for q in range(4)]

    def start(g, got, send_sems, recv_sems, base=0):
        for cp in copies(g, got, send_sems, recv_sems, base):
            cp.start()

    def finish(g, got, send_sems, recv_sems, base=0):
        for cp in copies(g, got, send_sems, recv_sems, base):
            cp.wait()

    return start, finish, 4 * len(geo)


def _swap_shapes(widx):
    return [jax.ShapeDtypeStruct((4,) + _piece_shape(*BIG[i][1:]), F32) for i in widx]


def _reduce_swap_halves(tag, grads, widx):
    nw = len(widx)
    start, finish, n_sems = _swap_copies(widx)

    def body(*refs):
        start(refs[:nw], refs[nw:2 * nw], *refs[2 * nw:])
        finish(refs[:nw], refs[nw:2 * nw], *refs[2 * nw:])

    return pl.pallas_call(
        body, name="reduce_swap_halves_" + tag, in_specs=[HBM_SPEC] * nw, out_specs=[HBM_SPEC] * nw,
        out_shape=_swap_shapes(widx),
        scratch_shapes=[pltpu.SemaphoreType.DMA((n_sems,)), pltpu.SemaphoreType.DMA((n_sems,))])(*grads)


def _owner_copies(nw):
    def copies(p, out, send_sems, recv_sems, base):
        x, y, c, chips = _mesh_pos()
        return [_remote(p[wi].at[2 * qx + qy], out[wi].at[j], send_sems.at[base + wi * 3 + j],
                        recv_sems.at[base + wi * 3 + j], (qx, qy, c))
                for wi in range(nw) for j, (qx, qy) in enumerate(chips)]

    def start(p, out, send_sems, recv_sems, base=0):
        for cp in copies(p, out, send_sems, recv_sems, base):
            cp.start()

    def finish(p, out, send_sems, recv_sems, base=0):
        for cp in copies(p, out, send_sems, recv_sems, base):
            cp.wait()

    return start, finish, 3 * nw


def _join_carries(a, b):
    if a is None or b is None:
        return a if b is None else b
    n_i, n_o = len(a.ins), len(a.outs)
    outs = list(a.outs) + [o + n_i if isinstance(o, int) else o for o in b.outs]

    def start(c_in, c_out, send_sems, recv_sems):
        a.start(c_in[:n_i], c_out[:n_o], send_sems, recv_sems)
        b.start(c_in[n_i:], c_out[n_o:], send_sems, recv_sems, base=a.n_sems)

    def finish(c_in, c_out, send_sems, recv_sems):
        a.finish(c_in[:n_i], c_out[:n_o], send_sems, recv_sems)
        b.finish(c_in[n_i:], c_out[n_o:], send_sems, recv_sems, base=a.n_sems)

    def done(res):
        a.done(res[:n_o])
        b.done(res[n_o:])

    return _Carry(a.ins + b.ins, outs, a.n_sems + b.n_sems, start, finish, done)


def _share_with_sibling(shards):
    nw = len(BIG)

    def body(*refs):
        out = refs[nw:2 * nw]
        send_sems, recv_sems = refs[2 * nw:]
        x, y, c, _ = _mesh_pos()
        sib = (x, y, 1 - c)
        cps = []
        for wi, (_, k, n, ax) in enumerate(BIG):
            mine = _shard_piece(out[wi], k, n, ax, c)
            cp = _remote(mine, mine, send_sems.at[wi], recv_sems.at[wi], sib)
            cp.start()
            cps.append(cp)
        for wi, (_, k, n, ax) in enumerate(BIG):
            piece = _shard_piece(out[wi], k, n, ax, 1 - c)
            _remote(piece, piece, send_sems.at[wi], recv_sems.at[wi], sib).wait_recv()
        for cp in cps:
            cp.wait_send()

    return pl.pallas_call(
        body, name="share_with_sibling", in_specs=[HBM_SPEC] * nw, out_specs=[HBM_SPEC] * nw,
        out_shape=[jax.ShapeDtypeStruct(sh.shape, sh.dtype) for sh in shards],
        input_output_aliases={i: i for i in range(nw)},
        scratch_shapes=[pltpu.SemaphoreType.DMA((nw,)), pltpu.SemaphoreType.DMA((nw,))])(*shards)


def _allreduce_small(v):
    r = v.shape[0]
    rh = r // 2
    assert rh % 8 == 0

    def body(v_ref, o_ref, sib_buf, chip_buf, send_sems, recv_sems):
        x, y, c, chips = _mesh_pos()
        me = 2 * x + y
        sib = (x, y, 1 - c)
        mine = pl.ds(pl.multiple_of(c * rh, 8), rh)
        other = pl.ds(pl.multiple_of((1 - c) * rh, 8), rh)
        swap = _remote(v_ref.at[other], sib_buf, send_sems.at[0], recv_sems.at[0], sib)
        swap.start()
        swap.wait()
        chip_buf[me] = v_ref[mine, :] + sib_buf[...]
        cps = []
        for j, (qx, qy) in enumerate(chips):
            cp = _remote(chip_buf.at[me], chip_buf.at[me], send_sems.at[1 + j], recv_sems.at[1 + j], (qx, qy, c))
            cp.start()
            cps.append(cp)
        for j, (qx, qy) in enumerate(chips):
            slot = chip_buf.at[2 * qx + qy]
            _remote(slot, slot, send_sems.at[1 + j], recv_sems.at[1 + j], (qx, qy, c)).wait_recv()
        for cp in cps:
            cp.wait_send()
        o_ref[mine, :] = ((chip_buf[0] + chip_buf[1]) + chip_buf[2]) + chip_buf[3]
        back = _remote(o_ref.at[mine], o_ref.at[mine], send_sems.at[4], recv_sems.at[4], sib)
        back.start()
        _remote(o_ref.at[other], o_ref.at[other], send_sems.at[4], recv_sems.at[4], sib).wait_recv()
        back.wait_send()

    return pl.pallas_call(
        body, name="allreduce_small", in_specs=[VMEM_SPEC], out_specs=VMEM_SPEC,
        out_shape=jax.ShapeDtypeStruct((r, 128), F32),
        scratch_shapes=[pltpu.VMEM((rh, 128), F32), pltpu.VMEM((4, rh, 128), F32),
                        pltpu.SemaphoreType.DMA((5,)), pltpu.SemaphoreType.DMA((5,))],
        compiler_params=pltpu.CompilerParams(vmem_limit_bytes=VMEM_LIMIT))(v)


def _as2d(a):
    a = a.reshape((-1, a.shape[-1])) if a.ndim > 1 else a.reshape(1, -1)
    return a


def _adamw_small(quads):
    n = len(quads)

    def body(*refs):
        for i in range(n):
            w, g, m, v = (r[...] for r in refs[4 * i:4 * i + 4])
            for ref, val in zip(refs[4 * n + 3 * i:4 * n + 3 * i + 3], _adamw(w, g, m, v)):
                ref[...] = val

    return pl.pallas_call(
        body, name="adamw_small", in_specs=[VMEM_SPEC] * (4 * n), out_specs=[VMEM_SPEC] * (3 * n),
        out_shape=[jax.ShapeDtypeStruct(q[0].shape, F32) for q in quads for _ in range(3)],
        compiler_params=pltpu.CompilerParams(vmem_limit_bytes=VMEM_LIMIT))(*[a for q in quads for a in q])


def _where():
    return jnp.stack([2 * lax.axis_index("x") + lax.axis_index("y"), lax.axis_index("c")]).astype(jnp.int32)


_BIG_INDEX = {name: i for i, (name, _, _, _) in enumerate(BIG)}


class _LocalWeights:
    def __init__(self, weights):
        self.weights, self.grads = weights, {}

    def gather_now(self, names):
        pass

    def gather_carry(self, names):
        return None

    def weight(self, name):
        return self.weights[name]

    def grad(self, name, g):
        self.grads[name] = g

    def carry(self, swap=(), ici=()):
        return None


class _Exchange:
    def __init__(self, inputs, where):
        self.inputs, self.where = inputs, where
        self.full, self.ready = {}, set()
        self.raw, self.got, self.parts, self.landed, self.geom = {}, {}, {}, {}, {}
        for name, k, n, ax in BIG:
            w2 = inputs[name][0]
            rs, cs = w2.shape
            tm = _tile(rs, 512)
            steps = rs // tm
            if ax == 1:
                blk, idx = (tm, cs), lambda i, w: (i, w[0])
            else:
                blk, idx = (tm, n), functools.partial(lambda i, w, steps: (w[0] * steps + i, 0), steps=steps)
            self.full[name] = _placed("cast_" + name, lambda w: w, steps, where, [(w2, (tm, cs), lambda i, w: (i, 0))],
                                      jax.ShapeDtypeStruct((k, n), MXU_DTYPE), blk, idx)

    def _gathered(self, names, outs):
        for name, o in zip(names, outs):
            self.full[name] = o
            self.ready.add(name)

    def gather_now(self, names):
        self._gathered(names, _gather_weights(names[0], [self.full[n] for n in names], [_BIG_INDEX[n] for n in names]))

    def gather_carry(self, names):
        start, finish, n_sems = _gather_copies([_BIG_INDEX[n] for n in names])
        return _Carry([self.full[n] for n in names], list(range(len(names))), n_sems, start, finish,
                      functools.partial(self._gathered, names))

    def weight(self, name):
        assert name in self.ready, name
        return self.full[name]

    def grad(self, name, g):
        self.raw[name] = g

    def _swapped(self, names, outs):
        for name, o in zip(names, outs):
            self.got[name] = o

    def _pair_sum(self, name):
        i = _BIG_INDEX[name]
        _, k, n, ax = BIG[i]
        g = self.raw[name]
        if name not in self.got:
            self._swapped([name], _reduce_swap_halves(name, [g], [i]))
        got = self.got[name]
        pr, pc = _piece_shape(k, n, ax)
        tm = _tile(pr, 512)
        spp = pr // tm
        self.geom[name] = (pr, pc, tm, spp)
        if ax == 1:
            g_idx = functools.partial(lambda i, w, spp: (w[1] * spp + i % spp, i // spp), spp=spp)
        else:
            g_idx = functools.partial(lambda i, w, spp: ((i // spp) * 2 * spp + w[1] * spp + i % spp, 0), spp=spp)
        self.parts[name] = _placed(
            "pair_sum_" + name, lambda a, b: a + b, 4 * spp, self.where,
            [(g, (tm, pc), g_idx), (got.reshape(4 * pr, pc), (tm, pc), lambda i, w: (i, 0))],
            jax.ShapeDtypeStruct((4 * pr, pc), BF16), (tm, pc), lambda i, w: (i, 0)).reshape(4, pr, pc)

    def _landed(self, names, outs):
        for name, o in zip(names, outs):
            self.landed[name] = o

    def carry(self, swap=(), ici=()):
        first = second = None
        if swap:
            widx = [_BIG_INDEX[n] for n in swap]
            start, finish, n_sems = _swap_copies(widx)
            first = _Carry([self.raw[n] for n in swap], _swap_shapes(widx), n_sems, start, finish,
                           functools.partial(self._swapped, list(swap)))
        if ici:
            for n in ici:
                self._pair_sum(n)
            start, finish, n_sems = _owner_copies(len(ici))
            parts = [self.parts[n] for n in ici]
            outs = [jax.ShapeDtypeStruct((3,) + p.shape[1:], p.dtype) for p in parts]
            second = _Carry(parts, outs, n_sems, start, finish, functools.partial(self._landed, list(ici)))
        return _join_carries(first, second)

    def finish(self):
        halves = []
        for name, _, _, _ in BIG:
            pr, pc, tm, spp = self.geom[name]
            ins = [(self.parts[name], (None, tm, pc), lambda i, w: (w[0], i, 0))]
            ins += [(self.landed[name], (None, tm, pc), functools.partial(lambda i, w, j: (j, i, 0), j=j))
                    for j in range(3)]
            halves.append(_placed("chip_sum_" + name,
                                  lambda a, b, c, d: ((a.astype(F32) + b.astype(F32)) + c.astype(F32)) + d.astype(F32),
                                  spp, self.where, ins, jax.ShapeDtypeStruct(self.inputs[name].shape[1:], F32), (tm, pc),
                                  functools.partial(lambda i, w, spp: (w[1] * spp + i, 0), spp=spp)))
        return dict(zip([b[0] for b in BIG], _share_with_sibling(halves)))


def _step(inputs):
    x, mem, positions, target = inputs["x"][0], inputs["mem"][0], inputs["positions"], inputs["loss_target"][0]
    pos = positions.reshape(-1, 1)
    ex = _Exchange(inputs, _where())
    sp = {name: _as2d(inputs[name]) for name in SMALL}
    memb, = _rowwise("cast_mem", lambda m: (m,), [mem], [], [(D_MODEL, MXU_DTYPE)])

    loss, dx, gsmall = _local_step(x, memb, pos, target, sp, ex)
    gshard = ex.finish()

    out = {}
    for name, _, _, _ in BIG:
        w2, m2, v2 = inputs[name][0], inputs["m_" + name][0], inputs["v_" + name][0]
        n = w2.shape[1]
        d, nm, nv = _rowwise("adamw_" + name, _adamw, [w2, gshard[name], m2, v2], [], [(n, F32)] * 3, tm=256)
        lead = inputs[name].shape
        out[name] = (gshard[name].reshape(lead), d.reshape(lead), nm.reshape(lead), nv.reshape(lead))

    def tiles(a):
        flat = a.reshape(-1)
        n = -(-flat.shape[0] // 1024) * 1024
        return jnp.pad(flat, (0, n - flat.shape[0])).reshape(n // 128, 128)

    pieces = [tiles(loss[:, :1])] + [tiles(gsmall[name]) for name in SMALL]
    if sum(p.shape[0] for p in pieces) % 16:
        pieces.append(jnp.zeros((8, 128), F32))
    red = _allreduce_small(jnp.concatenate(pieces, axis=0))
    loss_total = red[0, 0]
    grads, off = {}, pieces[0].shape[0]
    for name, p in zip(SMALL, pieces[1:]):
        shp = _as2d(inputs[name]).shape
        grads[name] = red[off:off + p.shape[0]].reshape(-1)[:shp[0] * shp[1]].reshape(shp)
        off += p.shape[0]
    upd = _adamw_small([(_as2d(inputs[n]), grads[n], _as2d(inputs["m_" + n]), _as2d(inputs["v_" + n])) for n in SMALL])
    for i, name in enumerate(SMALL):
        shp = inputs[name].shape
        out[name] = (grads[name].reshape(shp),) + tuple(t.reshape(shp) for t in upd[3 * i:3 * i + 3])
    return loss_total, dx.reshape(inputs["x"].shape), out


_ARG_NAMES = (("x", "mem", "positions") + WEIGHT_ORDER + ("loss_target",) + tuple("m_" + n for n in WEIGHT_ORDER)
              + tuple("v_" + n for n in WEIGHT_ORDER))


def kernel(x, mem, positions, ln_in_g, ln_in_b, w_in, b_in, ssm_log_dt, ssm_a_re, ssm_a_im, ssm_b_re, ssm_b_im, ssm_c_re, ssm_c_im, ssm_d, w_glu, b_glu, w_att_up, w_mix_out, b_mix_out, ln1_g, ln1_b, w_xq, w_xkv, w_xo, ln2_g, ln2_b, w_ff1, b_ff1, w_ff2, b_ff2, ln3_g, ln3_b, loss_target, m_ln_in_g, m_ln_in_b, m_w_in, m_b_in, m_ssm_log_dt, m_ssm_a_re, m_ssm_a_im, m_ssm_b_re, m_ssm_b_im, m_ssm_c_re, m_ssm_c_im, m_ssm_d, m_w_glu, m_b_glu, m_w_att_up, m_w_mix_out, m_b_mix_out, m_ln1_g, m_ln1_b, m_w_xq, m_w_xkv, m_w_xo, m_ln2_g, m_ln2_b, m_w_ff1, m_b_ff1, m_w_ff2, m_b_ff2, m_ln3_g, m_ln3_b, v_ln_in_g, v_ln_in_b, v_w_in, v_b_in, v_ssm_log_dt, v_ssm_a_re, v_ssm_a_im, v_ssm_b_re, v_ssm_b_im, v_ssm_c_re, v_ssm_c_im, v_ssm_d, v_w_glu, v_b_glu, v_w_att_up, v_w_mix_out, v_b_mix_out, v_ln1_g, v_ln1_b, v_w_xq, v_w_xkv, v_w_xo, v_ln2_g, v_ln2_b, v_w_ff1, v_b_ff1, v_w_ff2, v_b_ff2, v_ln3_g, v_ln3_b):
    args = (x, mem, positions, ln_in_g, ln_in_b, w_in, b_in, ssm_log_dt, ssm_a_re, ssm_a_im, ssm_b_re, ssm_b_im, ssm_c_re, ssm_c_im, ssm_d, w_glu, b_glu, w_att_up, w_mix_out, b_mix_out, ln1_g, ln1_b, w_xq, w_xkv, w_xo, ln2_g, ln2_b, w_ff1, b_ff1, w_ff2, b_ff2, ln3_g, ln3_b, loss_target, m_ln_in_g, m_ln_in_b, m_w_in, m_b_in, m_ssm_log_dt, m_ssm_a_re, m_ssm_a_im, m_ssm_b_re, m_ssm_b_im, m_ssm_c_re, m_ssm_c_im, m_ssm_d, m_w_glu, m_b_glu, m_w_att_up, m_w_mix_out, m_b_mix_out, m_ln1_g, m_ln1_b, m_w_xq, m_w_xkv, m_w_xo, m_ln2_g, m_ln2_b, m_w_ff1, m_b_ff1, m_w_ff2, m_b_ff2, m_ln3_g, m_ln3_b, v_ln_in_g, v_ln_in_b, v_w_in, v_b_in, v_ssm_log_dt, v_ssm_a_re, v_ssm_a_im, v_ssm_b_re, v_ssm_b_im, v_ssm_c_re, v_ssm_c_im, v_ssm_d, v_w_glu, v_b_glu, v_w_att_up, v_w_mix_out, v_b_mix_out, v_ln1_g, v_ln1_b, v_w_xq, v_w_xkv, v_w_xo, v_ln2_g, v_ln2_b, v_w_ff1, v_b_ff1, v_w_ff2, v_b_ff2, v_ln3_g, v_ln3_b)
    assert len(args) == len(_ARG_NAMES)
    inputs = dict(zip(_ARG_NAMES, args))
    loss, dx, out = _step(inputs)
    res = [loss, dx]
    for k in range(4):
        res += [out[name][k] for name in WEIGHT_ORDER]
    return tuple(res)
```

```python
import functools
import math

import numpy as np
import jax
import jax.numpy as jnp
from jax import lax
from jax.experimental import pallas as pl
from jax.experimental.pallas import tpu as pltpu

F32 = jnp.float32
BF16 = jnp.bfloat16
MXU_DTYPE = jnp.bfloat16

D_MODEL = 1024
SSM_GROUP = 16
SSM_WIDTH = 768
SSM_GROUPS = 48
SSM_STATE = 64
N_STATE = SSM_GROUPS * SSM_STATE
SSM_CHUNKS = 6
CH_W = 128
CH_N = 512
ATT_HEAD_DIM = 64
ATT_HPG = 4
ATT_GROUPW = ATT_HPG * ATT_HEAD_DIM
DILATIONS = (1, 4, 16)
ATT_BLK = 128
ATT_SCALE = ATT_HEAD_DIM ** -0.5
ROT_DIM = 16
ROPE_THETA = 500000.0
XATT_HEADS = 4
XATT_HEAD_DIM = 256
XATT_SCALE = XATT_HEAD_DIM ** -0.5
D_FF = 4096
IN_COLS = 5120
DEEPNORM_ALPHA = 2.0 ** 0.25
LN_EPS = 1e-5
NEG_INF = -1e30
ADAM_LR = 0.001
ADAM_B1 = 0.9
ADAM_B2 = 0.999
ADAM_EPS = 1e-08
ADAM_WD = 0.01
ADAM_STEP = 10

N_SEG = 32
VMEM_LIMIT = 56 * 1024 * 1024
MESH = pl.DeviceIdType.MESH
HBM_SPEC = pl.BlockSpec(memory_space=pltpu.HBM)
VMEM_SPEC = pl.BlockSpec(memory_space=pltpu.VMEM)

BIG = (("w_in", 1024, 5120, 1), ("w_glu", 768, 2048, 1), ("w_att_up", 256, 1024, 1),
       ("w_mix_out", 1024, 1024, 0), ("w_xq", 1024, 1024, 0), ("w_xkv", 1024, 2048, 1),
       ("w_xo", 1024, 1024, 0), ("w_ff1", 1024, 4096, 1), ("w_ff2", 4096, 1024, 0))
SMALL = ("ln_in_g", "ln_in_b", "b_in", "ssm_log_dt", "ssm_a_re", "ssm_a_im", "ssm_b_re", "ssm_b_im",
         "ssm_c_re", "ssm_c_im", "ssm_d", "b_glu", "b_mix_out", "ln1_g", "ln1_b", "ln2_g", "ln2_b",
         "b_ff1", "b_ff2", "ln3_g", "ln3_b")
WEIGHT_ORDER = ("ln_in_g", "ln_in_b", "w_in", "b_in", "ssm_log_dt", "ssm_a_re", "ssm_a_im", "ssm_b_re",
                "ssm_b_im", "ssm_c_re", "ssm_c_im", "ssm_d", "w_glu", "b_glu", "w_att_up", "w_mix_out",
                "b_mix_out", "ln1_g", "ln1_b", "w_xq", "w_xkv", "w_xo", "ln2_g", "ln2_b", "w_ff1", "b_ff1",
                "w_ff2", "b_ff2", "ln3_g", "ln3_b")


def _cparams(n_axes):
    return pltpu.CompilerParams(dimension_semantics=("arbitrary",) * n_axes, vmem_limit_bytes=VMEM_LIMIT)


class _Carry:
    def __init__(self, ins, outs, n_sems, start, finish, done):
        self.ins, self.outs, self.n_sems, self.start, self.finish, self.done = ins, outs, n_sems, start, finish, done


def _call(name, body, grid, in_specs, out_specs, out_shape, args, scratch_shapes=(), carry=None):
    in_specs, out_specs, out_shape = list(in_specs), list(out_specs), list(out_shape)
    params = _cparams(len(grid))
    if carry is None:
        return pl.pallas_call(body, name=name, grid=grid, in_specs=in_specs, out_specs=out_specs, out_shape=out_shape,
                              scratch_shapes=list(scratch_shapes), compiler_params=params)(*args)
    n_in, n_out, n_ci, n_co = len(in_specs), len(out_specs), len(carry.ins), len(carry.outs)
    n_scr = len(scratch_shapes)

    def wrapped(*refs):
        ins, c_in = refs[:n_in], refs[n_in:n_in + n_ci]
        outs, c_out = refs[n_in + n_ci:n_in + n_ci + n_out], refs[n_in + n_ci + n_out:n_in + n_ci + n_out + n_co]
        scratch = refs[n_in + n_ci + n_out + n_co:n_in + n_ci + n_out + n_co + n_scr]
        send_sems, recv_sems = refs[-2:]
        ids = [pl.program_id(a) for a in range(len(grid))]
        first = functools.reduce(jnp.logical_and, [i == 0 for i in ids])
        last = functools.reduce(jnp.logical_and, [i == g - 1 for i, g in zip(ids, grid)])

        @pl.when(first)
        def _():
            carry.start(c_in, c_out, send_sems, recv_sems)

        body(*ins, *outs, *scratch)

        @pl.when(last)
        def _():
            carry.finish(c_in, c_out, send_sems, recv_sems)

    c_shapes = [jax.ShapeDtypeStruct(carry.ins[o].shape, carry.ins[o].dtype) if isinstance(o, int) else o
                for o in carry.outs]
    aliases = {n_in + o: n_out + i for i, o in enumerate(carry.outs) if isinstance(o, int)}
    res = pl.pallas_call(
        wrapped, name=name, grid=grid, in_specs=in_specs + [HBM_SPEC] * n_ci, out_specs=out_specs + [HBM_SPEC] * n_co,
        out_shape=out_shape + c_shapes, input_output_aliases=aliases,
        scratch_shapes=list(scratch_shapes) + [pltpu.SemaphoreType.DMA((carry.n_sems,))] * 2,
        compiler_params=params)(*args, *carry.ins)
    carry.done(res[n_out:])
    return res[:n_out]


def _rowwise(name, fn, rows, consts, outs, reds=(), tm=512, touts=(), carry=None):
    n_rows = (rows[0][0] if isinstance(rows[0], tuple) else rows[0]).shape[-2]
    tm = min(tm, n_rows)
    assert n_rows % tm == 0, (name, n_rows, tm)
    specs, args = [], []
    for r in rows:
        if isinstance(r, tuple) and len(r) == 3:
            arr, width, cb = r
            specs.append(pl.BlockSpec((tm, width), functools.partial(lambda i, cb: (i, cb), cb=cb)))
        elif isinstance(r, tuple):
            arr, slot = r
            specs.append(pl.BlockSpec((None, tm, arr.shape[2]), functools.partial(lambda i, s: (s, i, 0), s=slot)))
        else:
            arr = r
            specs.append(pl.BlockSpec((tm, arr.shape[1]), lambda i: (i, 0)))
        args.append(arr)
        assert arr.shape[-2] == n_rows, (name, arr.shape, n_rows)
    for cst in consts:
        specs.append(pl.BlockSpec(cst.shape, lambda i: (0, 0)))
        args.append(cst)
    n_r, n_c, n_o, n_d = len(rows), len(consts), len(outs) + len(touts), len(reds)
    out_shape = [jax.ShapeDtypeStruct((n_rows, c), dt) for c, dt in outs]
    out_specs = [pl.BlockSpec((tm, c), lambda i: (i, 0)) for c, _ in outs]
    out_shape += [jax.ShapeDtypeStruct((r, n_rows), dt) for r, dt in touts]
    out_specs += [pl.BlockSpec((r, tm), lambda i: (0, i)) for r, _ in touts]
    out_shape += [jax.ShapeDtypeStruct((1, c), F32) for c in reds]
    out_specs += [pl.BlockSpec((1, c), lambda i: (0, 0)) for c in reds]

    def body(*refs):
        ins = [r[...] for r in refs[:n_r + n_c]]
        o_refs = refs[n_r + n_c:n_r + n_c + n_o]
        d_refs = refs[n_r + n_c + n_o:]
        res = fn(*ins)
        res = res if isinstance(res, (tuple, list)) else (res,)
        assert len(res) == n_o + n_d, (name, len(res))
        for ref, val in zip(o_refs, res[:n_o]):
            ref[...] = val.astype(ref.dtype)
        first = pl.program_id(0) == 0
        for ref, val in zip(d_refs, res[n_o:]):
            @pl.when(first)
            def _(ref=ref, val=val):
                ref[...] = val

            @pl.when(jnp.logical_not(first))
            def _(ref=ref, val=val):
                ref[...] += val

    return _call(name, body, (n_rows // tm,), specs, out_specs, out_shape, args, carry=carry)


def _colsum(v):
    return jnp.sum(v.astype(F32), axis=0, keepdims=True)


_DIMS = {"nn": (((1,), (0,)), ((), ())), "nt": (((1,), (1,)), ((), ())), "tn": (((0,), (0,)), ((), ()))}


def _tile(dim, want):
    if dim <= want:
        return dim
    return max(t for t in range(128, want + 1, 128) if dim % t == 0)


def _dot(a, b, mode):
    return lax.dot_general(a.astype(MXU_DTYPE), b.astype(MXU_DTYPE), _DIMS[mode], preferred_element_type=F32)


def _mm(name, a, b, mode, *, bias=None, extras=(), epilogue=None, out_dtypes=(F32,), tm=1024, tn=1024, tk=1024,
        carry=None):
    if mode == "nn":
        (m, k), (_, n) = a.shape, b.shape
    elif mode == "nt":
        (m, k), (n, _) = a.shape, b.shape
    else:
        (k, m), (_, n) = a.shape, b.shape
    if k > tk:
        tk = 5 * tk
    tn = _tile(n, tn)
    tk = _tile(k, tk)
    nk = k // tk

    def vmem_bytes(rows):
        blocks = rows * tk * a.dtype.itemsize + tk * tn * b.dtype.itemsize
        blocks += sum(rows * tn * e.dtype.itemsize for e in extras)
        blocks += sum(rows * tn * jnp.dtype(dt).itemsize for dt in out_dtypes)
        return 2 * blocks + (rows * tn * 4 if nk > 1 else 0)

    tm = _tile(m, tm if mode == "tn" else 2 * tm)
    while vmem_bytes(tm) > 3 * VMEM_LIMIT // 4 and tm % 256 == 0:
        tm //= 2
    while nk == 1 and k > 1024 and (m // tm) * (n // tn) < 4 and tm % 256 == 0:
        tm //= 2
    assert m % tm == 0 and n % tn == 0 and k % tk == 0, (name, m, n, k)
    a_spec = {"nn": pl.BlockSpec((tm, tk), lambda i, j, kk: (i, kk)),
              "nt": pl.BlockSpec((tm, tk), lambda i, j, kk: (i, kk)),
              "tn": pl.BlockSpec((tk, tm), lambda i, j, kk: (kk, i))}[mode]
    b_spec = {"nn": pl.BlockSpec((tk, tn), lambda i, j, kk: (kk, j)),
              "nt": pl.BlockSpec((tn, tk), lambda i, j, kk: (j, kk)),
              "tn": pl.BlockSpec((tk, tn), lambda i, j, kk: (kk, j))}[mode]
    specs, args = [a_spec, b_spec], [a, b]
    if bias is not None:
        specs.append(pl.BlockSpec((1, tn), lambda i, j, kk: (0, j)))
        args.append(bias)
    for e in extras:
        specs.append(pl.BlockSpec((tm, tn), lambda i, j, kk: (i, j)))
        args.append(e)
    n_e, n_o = len(extras), len(out_dtypes)
    has_bias = bias is not None

    def body(*refs):
        a_ref, b_ref = refs[0], refs[1]
        pos = 2
        bias_ref = refs[pos] if has_bias else None
        pos += int(has_bias)
        e_refs = refs[pos:pos + n_e]
        o_refs = refs[pos + n_e:pos + n_e + n_o]
        acc_ref = refs[pos + n_e + n_o] if nk > 1 else None
        part = _dot(a_ref[...], b_ref[...], mode)

        def finish(r):
            if has_bias:
                r = r + bias_ref[...]
            res = epilogue(r, *[e[...] for e in e_refs]) if epilogue is not None else (r,)
            for ref, val in zip(o_refs, res):
                ref[...] = val.astype(ref.dtype)

        if nk == 1:
            finish(part)
        else:
            kk = pl.program_id(2)

            @pl.when(kk == 0)
            def _():
                acc_ref[...] = part

            @pl.when(kk > 0)
            def _():
                acc_ref[...] += part

            @pl.when(kk == nk - 1)
            def _():
                finish(acc_ref[...])

    res = _call(name, body, (m // tm, n // tn, nk), specs,
                [pl.BlockSpec((tm, tn), lambda i, j, kk: (i, j)) for _ in out_dtypes],
                [jax.ShapeDtypeStruct((m, n), dt) for dt in out_dtypes], args,
                scratch_shapes=[pltpu.VMEM((tm, tn), F32)] if nk > 1 else [], carry=carry)
    return res[0] if n_o == 1 else res


def _ssm_wgrads(u, dy, g_re, g_im, h_re, h_im, tk=1024):
    s = u.shape[0]
    tk = min(tk, s)
    nk = s // tk
    assert tk % N_SEG == 0

    def body(u_ref, dy_ref, gre_ref, gim_ref, hre_ref, him_ref, lre_ref, lim_ref, db_ref, dc_ref, dar_ref, dai_ref,
             pre_ref, pim_ref):
        kk = pl.program_id(1)
        u_blk, dy_blk = u_ref[...], dy_ref[...]
        g_r, g_i, h_r, h_i = gre_ref[...], gim_ref[...], hre_ref[...], him_ref[...]
        d_b = jnp.concatenate([_dot(u_blk, g_r, "tn"), _dot(u_blk, g_i, "tn")], axis=1)
        d_c = jnp.concatenate([_dot(h_r, dy_blk, "tn"), _dot(h_i, dy_blk, "tn")], axis=0)

        @pl.when(kk == 0)
        def _():
            first_row = lax.broadcasted_iota(jnp.int32, (N_SEG, CH_N), 0) == 0
            pre_ref[...] = jnp.where(first_row, 0.0, pltpu.roll(lre_ref[...], 1, 0))
            pim_ref[...] = jnp.where(first_row, 0.0, pltpu.roll(lim_ref[...], 1, 0))

        p_r = jnp.concatenate([pre_ref[...], h_r[:tk - N_SEG]], axis=0)
        p_i = jnp.concatenate([pim_ref[...], h_i[:tk - N_SEG]], axis=0)
        pre_ref[...] = h_r[tk - N_SEG:]
        pim_ref[...] = h_i[tk - N_SEG:]
        d_ar = jnp.sum(g_r * p_r + g_i * p_i, axis=0, keepdims=True)
        d_ai = jnp.sum(g_i * p_r - g_r * p_i, axis=0, keepdims=True)

        @pl.when(kk == 0)
        def _():
            db_ref[...] = d_b
            dc_ref[...] = d_c
            dar_ref[...] = d_ar
            dai_ref[...] = d_ai

        @pl.when(kk > 0)
        def _():
            db_ref[...] += d_b
            dc_ref[...] += d_c
            dar_ref[...] += d_ar
            dai_ref[...] += d_ai

    chan = pl.BlockSpec((tk, CH_W), lambda j, kk: (kk, j))
    state = pl.BlockSpec((tk, CH_N), lambda j, kk: (kk, j))
    last = pl.BlockSpec((N_SEG, CH_N), lambda j, kk: (s // N_SEG - 1, j))
    row = pl.BlockSpec((1, CH_N), lambda j, kk: (0, j))
    return pl.pallas_call(
        body, name="ssm_wgrads", grid=(SSM_CHUNKS, nk),
        in_specs=[chan, chan, state, state, state, state, last, last],
        out_specs=[pl.BlockSpec((None, CH_W, 2 * CH_N), lambda j, kk: (j, 0, 0)),
                   pl.BlockSpec((None, 2 * CH_N, CH_W), lambda j, kk: (j, 0, 0)), row, row],
        out_shape=[jax.ShapeDtypeStruct((SSM_CHUNKS, CH_W, 2 * CH_N), F32),
                   jax.ShapeDtypeStruct((SSM_CHUNKS, 2 * CH_N, CH_W), F32),
                   jax.ShapeDtypeStruct((1, N_STATE), F32), jax.ShapeDtypeStruct((1, N_STATE), F32)],
        scratch_shapes=[pltpu.VMEM((N_SEG, CH_N), F32)] * 2,
        compiler_params=_cparams(2))(u, dy, g_re, g_im, h_re, h_im, h_re, h_im)


SCAN_LB = 256


def _split_by_scan_block(mat, axis):
    halves = []
    for l in range(CH_N // SCAN_LB):
        re = lax.slice_in_dim(mat, l * SCAN_LB, (l + 1) * SCAN_LB, axis=axis)
        im = lax.slice_in_dim(mat, CH_N + l * SCAN_LB, CH_N + (l + 1) * SCAN_LB, axis=axis)
        halves.append(jnp.concatenate([re, im], axis=axis))
    return jnp.stack(halves, axis=1).reshape((-1,) + halves[0].shape[1:])


def _ssm_scan(name, chan, expand12, contract12, a_re, a_im, d_row, reverse, carry=None):
    s = chan.shape[0]
    seg_len = s // N_SEG
    n_sq = int(math.log2(seg_len))
    assert 2 ** n_sq == seg_len
    rb = min(512, s)
    per_chunk = CH_N // SCAN_LB

    def body(are_ref, aim_ref, ch_ref, e_ref, k_ref, d_ref, hre_ref, him_ref, o_ref, wre_ref, wim_ref, ere, eim, cre, cim):
        e_mat, k_mat = e_ref[...], k_ref[...]
        for r in range(s // rb):
            rows = slice(r * rb, (r + 1) * rb)
            w = _dot(ch_ref[rows, :], e_mat, "nt" if reverse else "nn")
            wre_ref[rows, :] = w[:, :SCAN_LB]
            wim_ref[rows, :] = w[:, SCAN_LB:]

        ar1 = are_ref[...]
        ai1 = -aim_ref[...] if reverse else aim_ref[...]
        ar = jnp.broadcast_to(ar1, (N_SEG, SCAN_LB))
        ai = jnp.broadcast_to(ai1, (N_SEG, SCAN_LB))

        def rows_of(k):
            kk = seg_len - 1 - k if reverse else k
            return pl.ds(pl.multiple_of(kk * N_SEG, N_SEG), N_SEG)

        def local(k, carry):
            hr, hi = carry
            rows = rows_of(k)
            nr = ar * hr - ai * hi + wre_ref[rows, :]
            ni = ar * hi + ai * hr + wim_ref[rows, :]
            hre_ref[rows, :] = nr
            him_ref[rows, :] = ni
            return nr, ni

        zero = jnp.zeros((N_SEG, SCAN_LB), F32)
        er, ei = lax.fori_loop(0, seg_len, local, (zero, zero))
        ere[...] = er
        eim[...] = ei
        pr, pi = ar1, ai1
        for _ in range(n_sq):
            pr, pi = pr * pr - pi * pi, 2.0 * pr * pi
        cr = jnp.zeros((1, SCAN_LB), F32)
        ci = jnp.zeros((1, SCAN_LB), F32)
        for jj in range(N_SEG):
            j = N_SEG - 1 - jj if reverse else jj
            cre[j:j + 1, :] = cr
            cim[j:j + 1, :] = ci
            er_j, ei_j = ere[j:j + 1, :], eim[j:j + 1, :]
            cr, ci = pr * cr - pi * ci + er_j, pr * ci + pi * cr + ei_j
        c_r, c_i = cre[...], cim[...]

        def fix(k, carry):
            qr, qi = carry
            rows = rows_of(k)
            hre_ref[rows, :] = hre_ref[rows, :] + (qr * c_r - qi * c_i)
            him_ref[rows, :] = him_ref[rows, :] + (qr * c_i + qi * c_r)
            return qr * ar - qi * ai, qr * ai + qi * ar

        lax.fori_loop(0, seg_len, fix, (ar, ai))

        first_of_chunk = lax.rem(pl.program_id(0), per_chunk) == 0
        for r in range(s // rb):
            rows = slice(r * rb, (r + 1) * rb)
            h_cat = jnp.concatenate([hre_ref[rows, :], him_ref[rows, :]], axis=1)
            part = _dot(h_cat, k_mat, "nt" if reverse else "nn")

            @pl.when(first_of_chunk)
            def _(rows=rows, part=part):
                o_ref[rows, :] = part + d_ref[...] * ch_ref[rows, :]

            @pl.when(jnp.logical_not(first_of_chunk))
            def _(rows=rows, part=part):
                o_ref[rows, :] += part

    nblk = N_STATE // SCAN_LB
    blk = pl.BlockSpec((s, SCAN_LB), lambda b: (0, b))
    row = pl.BlockSpec((1, SCAN_LB), lambda b: (0, b))
    chan_blk = pl.BlockSpec((s, CH_W), lambda b: (0, b // per_chunk))
    res = _call(name, body, (nblk,),
                [row, row, chan_blk, pl.BlockSpec((None,) + expand12.shape[1:], lambda b: (b, 0, 0)),
                 pl.BlockSpec((None,) + contract12.shape[1:], lambda b: (b, 0, 0)),
                 pl.BlockSpec((1, CH_W), lambda b: (0, b // per_chunk))],
                [blk, blk, chan_blk],
                [jax.ShapeDtypeStruct((s, N_STATE), F32)] * 2 + [jax.ShapeDtypeStruct((s, SSM_WIDTH), F32)],
                (a_re, a_im, chan, expand12, contract12, d_row),
                scratch_shapes=[pltpu.VMEM((s, SCAN_LB), F32)] * 2 + [pltpu.VMEM((N_SEG, SCAN_LB), F32)] * 4, carry=carry)
    return res[0], res[1], res[2]


def _disc(ldt, are, aim, bre, bim):
    dt = jnp.exp(ldt)
    mag = jnp.exp(are * dt)
    abr = mag * jnp.cos(aim * dt)
    abi = mag * jnp.sin(aim * dt)
    den = jnp.square(are) + jnp.square(aim)
    nr = abr - 1.0
    fre = (nr * are + abi * aim) / den
    fim = (abi * are - nr * aim) / den
    return abr, abi, fre * bre - fim * bim, fre * bim + fim * bre


def _ssm_disc_fwd(ldt, are, aim, bre, bim):
    def body(l_ref, ar_ref, ai_ref, br_ref, bi_ref, o0, o1, o2, o3):
        res = _disc(l_ref[...], ar_ref[...], ai_ref[...], br_ref[...], bi_ref[...])
        for ref, val in zip((o0, o1, o2, o3), res):
            ref[...] = val

    col = jax.ShapeDtypeStruct((N_STATE, 1), F32)
    mat = jax.ShapeDtypeStruct((N_STATE, SSM_GROUP), F32)
    return pl.pallas_call(body, name="ssm_disc_fwd", out_shape=[col, col, mat, mat],
                          in_specs=[VMEM_SPEC] * 5, out_specs=[VMEM_SPEC] * 4)(ldt, are, aim, bre, bim)


def _ssm_disc_bwd(ldt, are, aim, bre, bim, d_abr, d_abi, d_bbr, d_bbi):
    def body(l_ref, ar_ref, ai_ref, br_ref, bi_ref, c0, c1, c2, c3, g_ldt, g_are, g_aim, g_bre, g_bim):
        _, vjp = jax.vjp(_disc, l_ref[...], ar_ref[...], ai_ref[...], br_ref[...], bi_ref[...])
        dl, dar, dai, dbr, dbi = vjp((c0[...], c1[...], c2[...], c3[...]))
        state = lax.broadcasted_iota(jnp.int32, (N_STATE, SSM_GROUPS), 0)
        group = lax.broadcasted_iota(jnp.int32, (N_STATE, SSM_GROUPS), 1)
        pick = jnp.right_shift(state, 6) == group
        g_ldt[...] = jnp.sum(jnp.where(pick, dl, 0.0), axis=0, keepdims=True)
        g_are[...] = dar
        g_aim[...] = dai
        g_bre[...] = dbr
        g_bim[...] = dbi

    col = jax.ShapeDtypeStruct((N_STATE, 1), F32)
    mat = jax.ShapeDtypeStruct((N_STATE, SSM_GROUP), F32)
    return pl.pallas_call(body, name="ssm_disc_bwd",
                          out_shape=[jax.ShapeDtypeStruct((1, SSM_GROUPS), F32), col, col, mat, mat],
                          in_specs=[VMEM_SPEC] * 9, out_specs=[VMEM_SPEC] * 5,
                          compiler_params=pltpu.CompilerParams(vmem_limit_bytes=VMEM_LIMIT))(
        ldt, are, aim, bre, bim, d_abr, d_abi, d_bbr, d_bbi)


_EYE8 = np.eye(8, dtype=np.float32)


def _blockdiag_b(bb):
    t = bb.reshape(SSM_CHUNKS, 8, SSM_STATE, SSM_GROUP).transpose(0, 1, 3, 2)
    return jnp.einsum("igcn,gh->igchn", t, _EYE8).reshape(SSM_CHUNKS, CH_W, CH_N)


def _diag_of_b(m):
    t = jnp.einsum("igchn,gh->igcn", m.reshape(SSM_CHUNKS, 8, SSM_GROUP, 8, SSM_STATE), _EYE8)
    return t.transpose(0, 1, 3, 2).reshape(N_STATE, SSM_GROUP)


def _blockdiag_c(c):
    t = c.reshape(SSM_CHUNKS, 8, SSM_GROUP, SSM_STATE).transpose(0, 1, 3, 2)
    return jnp.einsum("ignc,gh->ignhc", t, _EYE8).reshape(SSM_CHUNKS, CH_N, CH_W)


def _diag_of_c(m):
    t = jnp.einsum("ignhc,gh->ignc", m.reshape(SSM_CHUNKS, 8, SSM_STATE, 8, SSM_GROUP), _EYE8)
    return t.transpose(0, 1, 3, 2).reshape(SSM_GROUPS, SSM_GROUP, SSM_STATE)


def _time_perm(a):
    s, c = a.shape
    return a.reshape(N_SEG, s // N_SEG, c).transpose(1, 0, 2).reshape(s, c)


def _time_unperm(a):
    s, c = a.shape
    return a.reshape(s // N_SEG, N_SEG, c).transpose(1, 0, 2).reshape(s, c)


def _dilate(a, d):
    s, c = a.shape
    return a if d == 1 else a.reshape(s // d, d, c).transpose(1, 0, 2).reshape(s, c)


def _undilate(a, d):
    s, c = a.shape
    return a if d == 1 else a.reshape(d, s // d, c).transpose(1, 0, 2).reshape(s, c)


def _dilate_rows(a, d):
    r, s = a.shape
    return a if d == 1 else a.reshape(r, s // d, d).transpose(0, 2, 1).reshape(r, s)


ATT_T = 4
ATT_ROWS = ATT_T * ATT_BLK


def _window(prev_ref, cur_ref, i, sl):
    if i == 0:
        return jnp.concatenate([prev_ref[:, sl], cur_ref[0:ATT_BLK, sl]], axis=0)
    return cur_ref[(i - 1) * ATT_BLK:(i + 1) * ATT_BLK, sl]


def _band_valid(first_key):
    qi = lax.broadcasted_iota(jnp.int32, (ATT_BLK, 2 * ATT_BLK), 0)
    ki = lax.broadcasted_iota(jnp.int32, (ATT_BLK, 2 * ATT_BLK), 1)
    steps = qi + ATT_BLK - ki
    return (steps >= 0) & (steps <= ATT_BLK) & (ki >= first_key)


ATT_STATW = ATT_HPG * 128


def _stat(h):
    return slice(h * 128, (h + 1) * 128)


def _stat_rows(stat):
    n = stat.shape[0]
    heads = [stat[:, _stat(h)].T[0:1, :] for h in range(ATT_HPG)]
    return jnp.concatenate(heads + [jnp.zeros((8 - ATT_HPG, n), stat.dtype)], axis=0)


def _attn_specs(nb, width=ATT_GROUPW):
    cur = pl.BlockSpec((ATT_ROWS, width), lambda b: (b, 0))
    prev = pl.BlockSpec((ATT_BLK, width), lambda b: (jnp.maximum(b * ATT_T - 1, 0), 0))
    nxt = pl.BlockSpec((ATT_BLK, width), lambda b: (jnp.minimum((b + 1) * ATT_T, nb - 1), 0))
    return cur, prev, nxt


def _attn_fwd(tag, per_seq, q, k, v):
    s = q.shape[0]
    nb = s // ATT_BLK

    def body(q_ref, kc_ref, kp_ref, vc_ref, vp_ref, o_ref, lse_ref):
        bt = pl.program_id(0)
        for i in range(ATT_T):
            has_prev = lax.rem(bt * ATT_T + i, per_seq) > 0
            valid = _band_valid(jnp.where(has_prev, 0, ATT_BLK))
            rows = slice(i * ATT_BLK, (i + 1) * ATT_BLK)
            for h in range(ATT_HPG):
                sl = slice(h * ATT_HEAD_DIM, (h + 1) * ATT_HEAD_DIM)
                kcat = _window(kp_ref, kc_ref, i, sl)
                vcat = _window(vp_ref, vc_ref, i, sl)
                sc = _dot(q_ref[rows, sl], kcat, "nt") * ATT_SCALE
                sc = jnp.where(valid, sc, NEG_INF)
                m = jnp.max(sc, axis=-1, keepdims=True)
                p = jnp.exp(sc - m)
                den = jnp.sum(p, axis=-1, keepdims=True)
                o_ref[rows, sl] = _dot(p, vcat, "nn") / den
                lse_ref[rows, _stat(h)] = jnp.broadcast_to(m + jnp.log(den), (ATT_BLK, 128))

    cur, prev, _ = _attn_specs(nb)
    stat, _, _ = _attn_specs(nb, ATT_STATW)
    return pl.pallas_call(
        body, name="attn_fwd_" + tag, grid=(nb // ATT_T,), in_specs=[cur, cur, prev, cur, prev], out_specs=[cur, stat],
        out_shape=[jax.ShapeDtypeStruct((s, ATT_GROUPW), F32), jax.ShapeDtypeStruct((s, ATT_STATW), F32)],
        compiler_params=_cparams(1))(q, k, k, v, v)


def _attn_dq(tag, per_seq, q, k, v, do, lse, delta):
    s = q.shape[0]
    nb = s // ATT_BLK

    def body(q_ref, kc_ref, kp_ref, vc_ref, vp_ref, do_ref, lse_ref, dl_ref, dq_ref):
        bt = pl.program_id(0)
        for i in range(ATT_T):
            has_prev = lax.rem(bt * ATT_T + i, per_seq) > 0
            valid = _band_valid(jnp.where(has_prev, 0, ATT_BLK))
            rows = slice(i * ATT_BLK, (i + 1) * ATT_BLK)
            for h in range(ATT_HPG):
                sl = slice(h * ATT_HEAD_DIM, (h + 1) * ATT_HEAD_DIM)
                kcat = _window(kp_ref, kc_ref, i, sl)
                vcat = _window(vp_ref, vc_ref, i, sl)
                lse = jnp.concatenate([lse_ref[rows, _stat(h)]] * 2, axis=1)
                dlt = jnp.concatenate([dl_ref[rows, _stat(h)]] * 2, axis=1)
                sc = _dot(q_ref[rows, sl], kcat, "nt") * ATT_SCALE
                p = jnp.exp(jnp.where(valid, sc, NEG_INF) - lse)
                dp = _dot(do_ref[rows, sl], vcat, "nt")
                ds = p * (dp - dlt) * ATT_SCALE
                dq_ref[rows, sl] = _dot(ds, kcat, "nn")

    cur, prev, _ = _attn_specs(nb)
    stat, _, _ = _attn_specs(nb, ATT_STATW)
    return pl.pallas_call(
        body, name="attn_dq_" + tag, grid=(nb // ATT_T,), in_specs=[cur, cur, prev, cur, prev, cur, stat, stat],
        out_specs=cur, out_shape=jax.ShapeDtypeStruct((s, ATT_GROUPW), F32),
        compiler_params=_cparams(1))(q, k, k, v, v, do, lse, delta)


def _attn_dkv(tag, per_seq, q, k, v, do, lse_t, delta_t):
    s = q.shape[0]
    nb = s // ATT_BLK

    def body(k_ref, v_ref, qc_ref, qn_ref, doc_ref, don_ref, lc_ref, ln_ref, dc_ref, dn_ref, dk_ref, dv_ref):
        bt = pl.program_id(0)
        ki = lax.broadcasted_iota(jnp.int32, (ATT_BLK, 2 * ATT_BLK), 0)
        ci = lax.broadcasted_iota(jnp.int32, (ATT_BLK, 2 * ATT_BLK), 1)

        def pair(edge_ref, cur_ref, i, sl):
            if i == ATT_T - 1:
                return jnp.concatenate([cur_ref[i * ATT_BLK:(i + 1) * ATT_BLK, sl], edge_ref[:, sl]], axis=0)
            return cur_ref[i * ATT_BLK:(i + 2) * ATT_BLK, sl]

        def pair_row(edge_ref, cur_ref, i, h):
            if i == ATT_T - 1:
                row = jnp.concatenate([cur_ref[h:h + 1, i * ATT_BLK:(i + 1) * ATT_BLK], edge_ref[h:h + 1, :]], axis=1)
            else:
                row = cur_ref[h:h + 1, i * ATT_BLK:(i + 2) * ATT_BLK]
            return jnp.broadcast_to(row, (ATT_BLK, 2 * ATT_BLK))

        for i in range(ATT_T):
            b = bt * ATT_T + i
            next_uses = (b + 1 < nb) & (lax.rem(b + 1, per_seq) > 0)
            reach = jnp.where(next_uses, 0, 4 * ATT_BLK)
            valid = ((ci < ATT_BLK) & (ci >= ki)) | ((ci >= ATT_BLK) & (ki - ci + ATT_BLK >= reach))
            rows = slice(i * ATT_BLK, (i + 1) * ATT_BLK)
            for h in range(ATT_HPG):
                sl = slice(h * ATT_HEAD_DIM, (h + 1) * ATT_HEAD_DIM)
                qcat, docat = pair(qn_ref, qc_ref, i, sl), pair(don_ref, doc_ref, i, sl)
                sc = _dot(k_ref[rows, sl], qcat, "nt") * ATT_SCALE
                p = jnp.exp(jnp.where(valid, sc, NEG_INF) - pair_row(ln_ref, lc_ref, i, h))
                dv_ref[rows, sl] = _dot(p, docat, "nn")
                dp = _dot(v_ref[rows, sl], docat, "nt")
                ds = p * (dp - pair_row(dn_ref, dc_ref, i, h)) * ATT_SCALE
                dk_ref[rows, sl] = _dot(ds, qcat, "nn")

    cur, _, nxt = _attn_specs(nb)
    stat = pl.BlockSpec((8, ATT_ROWS), lambda b: (0, b))
    snxt = pl.BlockSpec((8, ATT_BLK), lambda b: (0, jnp.minimum((b + 1) * ATT_T, nb - 1)))
    return pl.pallas_call(
        body, name="attn_dkv_" + tag, grid=(nb // ATT_T,), in_specs=[cur, cur, cur, nxt, cur, nxt, stat, snxt, stat, snxt],
        out_specs=[cur, cur], out_shape=[jax.ShapeDtypeStruct((s, ATT_GROUPW), F32)] * 2,
        compiler_params=_cparams(1))(k, v, q, q, do, do, lse_t, lse_t, delta_t, delta_t)


def _xattn_probs(q, kh):
    sc = _dot(q, kh, "nt") * XATT_SCALE
    e = jnp.exp(sc - jnp.max(sc, axis=-1, keepdims=True))
    return e / jnp.sum(e, axis=-1, keepdims=True)


def _xattn_fwd(q, kv, tm=512):
    s = q.shape[0]
    tm = min(tm, s)

    def body(q_ref, kv_ref, o_ref):
        for h in range(XATT_HEADS):
            sl = slice(h * XATT_HEAD_DIM, (h + 1) * XATT_HEAD_DIM)
            vs = slice(D_MODEL + h * XATT_HEAD_DIM, D_MODEL + (h + 1) * XATT_HEAD_DIM)
            p = _xattn_probs(q_ref[:, sl], kv_ref[:, sl])
            o_ref[:, sl] = _dot(p, kv_ref[:, vs], "nn").astype(o_ref.dtype)

    return pl.pallas_call(
        body, name="xattn_fwd", grid=(s // tm,),
        in_specs=[pl.BlockSpec((tm, D_MODEL), lambda i: (i, 0)), pl.BlockSpec(kv.shape, lambda i: (0, 0))],
        out_specs=pl.BlockSpec((tm, D_MODEL), lambda i: (i, 0)),
        out_shape=jax.ShapeDtypeStruct((s, D_MODEL), MXU_DTYPE), compiler_params=_cparams(1))(q, kv)


def _xattn_bwd(q, kv, do, tm=512):
    s = q.shape[0]
    tm = min(tm, s)

    def body(q_ref, kv_ref, do_ref, dq_ref, dkv_ref):
        first = pl.program_id(0) == 0

        @pl.when(first)
        def _():
            dkv_ref[...] = jnp.zeros_like(dkv_ref)

        for h in range(XATT_HEADS):
            sl = slice(h * XATT_HEAD_DIM, (h + 1) * XATT_HEAD_DIM)
            vs = slice(D_MODEL + h * XATT_HEAD_DIM, D_MODEL + (h + 1) * XATT_HEAD_DIM)
            p = _xattn_probs(q_ref[:, sl], kv_ref[:, sl])
            dkv_ref[:, vs] += _dot(p, do_ref[:, sl], "tn")
            dp = _dot(do_ref[:, sl], kv_ref[:, vs], "nt")
            ds = p * (dp - jnp.sum(dp * p, axis=-1, keepdims=True)) * XATT_SCALE
            dq_ref[:, sl] = _dot(ds, kv_ref[:, sl], "nn").astype(dq_ref.dtype)
            dkv_ref[:, sl] += _dot(ds, q_ref[:, sl], "tn")

    row = pl.BlockSpec((tm, D_MODEL), lambda i: (i, 0))
    whole = pl.BlockSpec(kv.shape, lambda i: (0, 0))
    return pl.pallas_call(
        body, name="xattn_bwd", grid=(s // tm,), in_specs=[row, whole, row], out_specs=[row, whole],
        out_shape=[jax.ShapeDtypeStruct((s, D_MODEL), MXU_DTYPE), jax.ShapeDtypeStruct(kv.shape, F32)],
        compiler_params=_cparams(1))(q, kv, do)


def _ln(x, g, b):
    mu = jnp.mean(x, axis=-1, keepdims=True)
    xc = x - mu
    var = jnp.mean(jnp.square(xc), axis=-1, keepdims=True)
    return xc * lax.rsqrt(var + LN_EPS) * g + b


def _res_ln(h, o, g, b):
    return _ln(DEEPNORM_ALPHA * h + o, g, b)


def _gate(gs, ga, z1, z2, batt):
    return jax.nn.sigmoid(gs) * (z1 * jax.nn.sigmoid(z2)) + jax.nn.sigmoid(ga) * batt


def _rope_tables(pos, invf, m1, m2):
    ang = pos.astype(F32) * invf
    sin = jnp.sin(ang)
    return jnp.cos(ang), -sin * m1, sin * m2


def _rope(t, cos, s_up, s_dn):
    w = t.shape[-1]
    return t * cos + pltpu.roll(t, w - ROT_DIM // 2, 1) * s_up + pltpu.roll(t, ROT_DIM // 2, 1) * s_dn


def _rope_t(dt, cos, s_up, s_dn):
    w = dt.shape[-1]
    return dt * cos + pltpu.roll(dt * s_up, ROT_DIM // 2, 1) + pltpu.roll(dt * s_dn, w - ROT_DIM // 2, 1)


def _rope_consts():
    inv_freq = ROPE_THETA ** (-jnp.arange(0, ROT_DIM, 2, dtype=F32) / ROT_DIM)
    d = np.arange(ATT_GROUPW) % ATT_HEAD_DIM
    invf = jnp.where(d < ROT_DIM, inv_freq[d % (ROT_DIM // 2)], 0.0).reshape(1, ATT_GROUPW).astype(F32)
    m1 = jnp.asarray((d < ROT_DIM // 2).astype(np.float32)).reshape(1, ATT_GROUPW)
    m2 = jnp.asarray(((d >= ROT_DIM // 2) & (d < ROT_DIM)).astype(np.float32)).reshape(1, ATT_GROUPW)
    return invf, m1, m2


def _head_sum_matrix():
    d = np.arange(ATT_GROUPW) // ATT_HEAD_DIM
    s = np.arange(ATT_STATW) // 128
    return jnp.asarray((d[:, None] == s[None, :]).astype(np.float32))


def _adamw(w, g, m, v):
    m = ADAM_B1 * m + (1.0 - ADAM_B1) * g
    v = ADAM_B2 * v + (1.0 - ADAM_B2) * jnp.square(g)
    m_hat = m / (1.0 - ADAM_B1 ** ADAM_STEP)
    v_hat = v / (1.0 - ADAM_B2 ** ADAM_STEP)
    delta = -ADAM_LR * (m_hat / (jnp.sqrt(v_hat) + ADAM_EPS) + ADAM_WD * w)
    return delta, m, v


def _local_step(x, mem, pos, target, sp, ex):
    s = x.shape[0]
    al = DEEPNORM_ALPHA
    mx = MXU_DTYPE

    h0, h0b = _rowwise("ln_in", lambda x, g, b: (lambda h: (h, h))(_ln(x, g, b)), [x],
                       [sp["ln_in_g"], sp["ln_in_b"]], [(D_MODEL, F32), (D_MODEL, mx)],
                       carry=ex.gather_carry(["w_in"]))
    proj = _mm("proj", h0b, ex.weight("w_in"), "nn", bias=sp["b_in"],
               carry=ex.gather_carry(["w_glu", "w_att_up", "w_mix_out", "w_xq", "w_xkv"]))

    ldt = jnp.repeat(sp["ssm_log_dt"].reshape(SSM_GROUPS), SSM_STATE).reshape(N_STATE, 1)
    are, aim = sp["ssm_a_re"].reshape(N_STATE, 1), sp["ssm_a_im"].reshape(N_STATE, 1)
    bre, bim = sp["ssm_b_re"].reshape(N_STATE, SSM_GROUP), sp["ssm_b_im"].reshape(N_STATE, SSM_GROUP)
    abr, abi, bbr, bbi = _ssm_disc_fwd(ldt, are, aim, bre, bim)
    a_re, a_im = abr.reshape(1, N_STATE), abi.reshape(1, N_STATE)
    bexp = jnp.concatenate([_blockdiag_b(bbr), _blockdiag_b(bbi)], axis=2).astype(mx)
    cexp = jnp.concatenate([_blockdiag_c(sp["ssm_c_re"].reshape(SSM_GROUPS, SSM_GROUP, SSM_STATE)),
                            -_blockdiag_c(sp["ssm_c_im"].reshape(SSM_GROUPS, SSM_GROUP, SSM_STATE))],
                           axis=1).astype(mx)
    u_p = _time_perm(proj[:, :SSM_WIDTH])
    b12, c12 = _split_by_scan_block(bexp, 2), _split_by_scan_block(cexp, 1)
    h_re, h_im, y_p = _ssm_scan("ssm_scan_fwd", u_p, b12, c12, a_re, a_im, sp["ssm_d"], reverse=False,
                                carry=ex.gather_carry(["w_ff1", "w_ff2"]))
    y = _time_unperm(y_p)
    ygb, = _rowwise("gelu", lambda y: jax.nn.gelu(y), [y], [], [(SSM_WIDTH, mx)])
    z = _mm("glu", ygb, ex.weight("w_glu"), "nn", bias=sp["b_glu"], carry=ex.gather_carry(["w_xo"]))

    invf, m1, m2 = _rope_consts()

    def rope_fwd(pos, q0, q1, q2, k0, k1, k2, v0, v1, v2, invf, m1, m2):
        tabs = _rope_tables(pos, invf, m1, m2)
        return tuple(_rope(t, *tabs) for t in (q0, q1, q2, k0, k1, k2)) + (v0, v1, v2)

    qkv_cols = [(proj, ATT_GROUPW, 3 + i) for i in range(9)]
    qkv = _rowwise("rope", rope_fwd, [pos] + qkv_cols, [invf, m1, m2], [(ATT_GROUPW, mx)] * 9)
    n_blocks = s // ATT_BLK
    groups = [(str(g), n_blocks // d, d) for g, d in enumerate(DILATIONS)]
    q_d = [_dilate(qkv[g], d) for g, d in enumerate(DILATIONS)]
    k_d = [_dilate(qkv[3 + g], d) for g, d in enumerate(DILATIONS)]
    v_d = [_dilate(qkv[6 + g], d) for g, d in enumerate(DILATIONS)]
    o_g, l_g = [], []
    for g, (tag, per_seq, d) in enumerate(groups):
        o, lse = _attn_fwd(tag, per_seq, q_d[g], k_d[g], v_d[g])
        o_g.append(_undilate(o, d))
        l_g.append(_undilate(lse, d))

    def merge(o0, o1, o2, l0, l1, l2):
        m = jnp.maximum(jnp.maximum(l0, l1), l2)
        e0, e1, e2 = jnp.exp(l0 - m), jnp.exp(l1 - m), jnp.exp(l2 - m)
        tot = e0 + e1 + e2

        def per_dim(e):
            w = e / tot
            return jnp.concatenate([w[:, h * 128:h * 128 + ATT_HEAD_DIM] for h in range(ATT_HPG)], axis=1)

        att = per_dim(e0) * o0 + per_dim(e1) * o1 + per_dim(e2) * o2
        lse = m + jnp.log(tot)
        return att, att, lse, _stat_rows(lse)

    att, attb, lse_tot, lse_tot_t = _rowwise("attn_merge", merge, o_g + l_g, [],
                                             [(ATT_GROUPW, F32), (ATT_GROUPW, mx), (ATT_STATW, F32)], touts=[(8, F32)])
    batt = _mm("att_up", attb, ex.weight("w_att_up"), "nn")

    gate_rows = [(proj, D_MODEL, 3), (proj, D_MODEL, 4), (z, D_MODEL, 0), (z, D_MODEL, 1), batt]
    mixedb, = _rowwise("gate", _gate, gate_rows, [], [(D_MODEL, mx)])
    o1 = _mm("mix_out", mixedb, ex.weight("w_mix_out"), "nn", bias=sp["b_mix_out"])
    h1, h1b = _rowwise("ln1", lambda h, o, g, b: (lambda r: (r, r))(_res_ln(h, o, g, b)), [h0, o1],
                       [sp["ln1_g"], sp["ln1_b"]], [(D_MODEL, F32), (D_MODEL, mx)])

    qx = _mm("xq", h1b, ex.weight("w_xq"), "nn", out_dtypes=(mx,))
    kvx = _mm("xkv", mem, ex.weight("w_xkv"), "nn", out_dtypes=(mx,))
    oxb = _xattn_fwd(qx, kvx)
    o2 = _mm("xo", oxb, ex.weight("w_xo"), "nn")
    h2, h2b = _rowwise("ln2", lambda h, o, g, b: (lambda r: (r, r))(_res_ln(h, o, g, b)), [h1, o2],
                       [sp["ln2_g"], sp["ln2_b"]], [(D_MODEL, F32), (D_MODEL, mx)])

    a_ff, fb = _mm("ff1", h2b, ex.weight("w_ff1"), "nn", bias=sp["b_ff1"],
                   epilogue=lambda r: (r, jnp.square(jnp.maximum(r, 0.0))), out_dtypes=(F32, mx))
    o3 = _mm("ff2", fb, ex.weight("w_ff2"), "nn", bias=sp["b_ff2"])

    def loss_bwd(h2, o3, tgt, g, b):
        def f(h2, o3, g, b):
            h3 = _res_ln(h2, o3, g, b)
            return 0.5 * jnp.sum(jnp.mean(jnp.square(h3 - tgt), axis=-1))

        loss, vjp = jax.vjp(f, h2, o3, g, b)
        _, dr, dg, db = vjp(jnp.ones((), F32))
        return dr, dr, dg, db, _colsum(dr), jnp.full((1, 128), loss, F32)

    dr3, dr3b, g_ln3_g, g_ln3_b, g_b_ff2, loss = _rowwise(
        "loss_ln3_bwd", loss_bwd, [h2, o3, target], [sp["ln3_g"], sp["ln3_b"]],
        [(D_MODEL, F32), (D_MODEL, mx)], [D_MODEL, D_MODEL, D_MODEL, 128])

    dab = _mm("ff2_dx", dr3b, ex.weight("w_ff2"), "nt", extras=(a_ff,),
              epilogue=lambda r, a: (r * (2.0 * jnp.maximum(a, 0.0)),), out_dtypes=(mx,))
    ex.grad("w_ff2", _mm("ff2_dw", fb, dr3b, "tn"))
    g_b_ff1, = _rowwise("ff1_db", lambda v: (_colsum(v),), [dab], [], [], [D_FF])
    ex.grad("w_ff1", _mm("ff1_dw", h2b, dab, "tn", carry=ex.carry(swap=["w_ff2"])))
    dh2 = _mm("ff1_dx", dab, ex.weight("w_ff1"), "nt", extras=(dr3,), epilogue=lambda r, d: (r + al * d,),
              carry=ex.carry(swap=["w_ff1"], ici=["w_ff2"]))

    def ln_bwd(h, o, dout, g, b):
        _, vjp = jax.vjp(_res_ln, h, o, g, b)
        _, dr, dg, db = vjp(dout)
        return dr, dr, dg, db, _colsum(dr)

    dr2, dr2b, g_ln2_g, g_ln2_b, _ = _rowwise(
        "ln2_bwd", ln_bwd, [h1, o2, dh2], [sp["ln2_g"], sp["ln2_b"]],
        [(D_MODEL, F32), (D_MODEL, mx)], [D_MODEL, D_MODEL, D_MODEL])
    ex.grad("w_xo", _mm("xo_dw", oxb, dr2b, "tn"))
    doxb = _mm("xo_dx", dr2b, ex.weight("w_xo"), "nt", out_dtypes=(mx,), carry=ex.carry(swap=["w_xo"]))
    dqxb, dkvx = _xattn_bwd(qx, kvx, doxb)
    ex.grad("w_xq", _mm("xq_dw", h1b, dqxb, "tn", carry=ex.carry(ici=["w_xo"])))
    dh1 = _mm("xq_dx", dqxb, ex.weight("w_xq"), "nt", extras=(dr2,), epilogue=lambda r, d: (r + al * d,),
              carry=ex.carry(swap=["w_xq"]))
    ex.grad("w_xkv", _mm("xkv_dw", mem, dkvx, "tn"))

    dr1, dr1b, g_ln1_g, g_ln1_b, g_b_mix = _rowwise(
        "ln1_bwd", ln_bwd, [h0, o1, dh1], [sp["ln1_g"], sp["ln1_b"]],
        [(D_MODEL, F32), (D_MODEL, mx)], [D_MODEL, D_MODEL, D_MODEL])
    ex.grad("w_mix_out", _mm("mix_dw", mixedb, dr1b, "tn", carry=ex.carry(swap=["w_xkv"], ici=["w_xq"])))
    dmixed = _mm("mix_dx", dr1b, ex.weight("w_mix_out"), "nt", carry=ex.carry(swap=["w_mix_out"]))

    def gate_bwd(gs, ga, z1, z2, batt, dm):
        _, vjp = jax.vjp(_gate, gs, ga, z1, z2, batt)
        dgs, dga, dz1, dz2, dbatt = vjp(dm)
        dz = jnp.concatenate([dz1, dz2], axis=-1)
        return dgs, dga, dz, dbatt, _colsum(dz)

    dgsb, dgab, dzb, dbattb, g_b_glu = _rowwise(
        "gate_bwd", gate_bwd, gate_rows + [dmixed], [],
        [(D_MODEL, mx), (D_MODEL, mx), (2 * D_MODEL, mx), (D_MODEL, mx)], [2 * D_MODEL])
    ex.grad("w_att_up", _mm("att_up_dw", attb, dbattb, "tn", carry=ex.carry(ici=["w_mix_out"])))
    datt = _mm("att_up_dx", dbattb, ex.weight("w_att_up"), "nt", carry=ex.carry(swap=["w_att_up"]))

    def att_delta(datt, att, hs):
        dl = jnp.dot(datt * att, hs, precision=lax.Precision.HIGHEST, preferred_element_type=F32)
        return datt, dl, _stat_rows(dl)

    dattb, delta, delta_t = _rowwise("attn_delta", att_delta, [datt, att], [_head_sum_matrix()],
                                     [(ATT_GROUPW, mx), (ATT_STATW, F32)], touts=[(8, F32)])
    dq_g, dk_g, dv_g = [], [], []
    for g, (tag, per_seq, d) in enumerate(groups):
        do_d, lt_d, dl_d = _dilate(dattb, d), _dilate(lse_tot, d), _dilate(delta, d)
        dq_g.append(_undilate(_attn_dq(tag, per_seq, q_d[g], k_d[g], v_d[g], do_d, lt_d, dl_d), d))
        dk, dv = _attn_dkv(tag, per_seq, q_d[g], k_d[g], v_d[g], do_d, _dilate_rows(lse_tot_t, d), _dilate_rows(delta_t, d))
        dk_g.append(_undilate(dk, d))
        dv_g.append(_undilate(dv, d))
    dqkv = dq_g + dk_g + dv_g

    def rope_bwd(pos, q0, q1, q2, k0, k1, k2, v0, v1, v2, invf, m1, m2):
        tabs = _rope_tables(pos, invf, m1, m2)
        return jnp.concatenate([_rope_t(t, *tabs) for t in (q0, q1, q2, k0, k1, k2)] + [v0, v1, v2], axis=-1)

    dqkvb, = _rowwise("rope_bwd", rope_bwd, [pos] + dqkv, [invf, m1, m2], [(9 * ATT_GROUPW, mx)])

    ex.grad("w_glu", _mm("glu_dw", ygb, dzb, "tn", carry=ex.carry(ici=["w_xkv", "w_att_up"])))
    dyg = _mm("glu_dx", dzb, ex.weight("w_glu"), "nt", carry=ex.carry(swap=["w_glu"]))

    def gelu_bwd(y, dyg):
        _, vjp = jax.vjp(jax.nn.gelu, y)
        return vjp(dyg)[0]

    dy, = _rowwise("gelu_bwd", gelu_bwd, [y, dyg], [], [(SSM_WIDTH, F32)])
    dy_p = _time_perm(dy)
    s_re, s_im, du_p = _ssm_scan("ssm_scan_bwd", dy_p, c12, b12, a_re, a_im, sp["ssm_d"], reverse=True,
                                 carry=ex.carry(ici=["w_ff1", "w_glu"]))
    g_bexp, g_cexp, d_abr, d_abi = _ssm_wgrads(u_p, dy_p, s_re, s_im, h_re, h_im)
    g_ssm_d, = _rowwise("ssm_dd", lambda a, b: (_colsum(a * b),), [dy_p, u_p], [], [], [SSM_WIDTH])
    g_ldt, g_are, g_aim, g_bre, g_bim = _ssm_disc_bwd(
        ldt, are, aim, bre, bim, d_abr.reshape(N_STATE, 1), d_abi.reshape(N_STATE, 1),
        _diag_of_b(g_bexp[:, :, :CH_N]), _diag_of_b(g_bexp[:, :, CH_N:]))
    g_c_re = _diag_of_c(g_cexp[:, :CH_N, :])
    g_c_im = -_diag_of_c(g_cexp[:, CH_N:, :])
    dub = _time_unperm(du_p).astype(mx)

    dprojb = jnp.concatenate([dub, dqkvb, dgsb, dgab], axis=-1)
    g_b_in, = _rowwise("in_db", lambda v: (_colsum(v),), [dprojb], [], [], [IN_COLS])
    ex.grad("w_in", _mm("in_dw", h0b, dprojb, "tn"))
    dh0 = _mm("in_dx", dprojb, ex.weight("w_in"), "nt", extras=(dr1,), epilogue=lambda r, d: (r + al * d,),
              carry=ex.carry(ici=["w_in"]))

    def ln_in_bwd(x, dout, g, b):
        _, vjp = jax.vjp(_ln, x, g, b)
        return vjp(dout)

    dx, g_ln_in_g, g_ln_in_b = _rowwise("ln_in_bwd", ln_in_bwd, [x, dh0], [sp["ln_in_g"], sp["ln_in_b"]],
                                        [(D_MODEL, F32)], [D_MODEL, D_MODEL])

    small = {"ln_in_g": g_ln_in_g, "ln_in_b": g_ln_in_b, "b_in": g_b_in, "ssm_log_dt": g_ldt, "ssm_a_re": g_are,
             "ssm_a_im": g_aim, "ssm_b_re": g_bre, "ssm_b_im": g_bim, "ssm_c_re": g_c_re, "ssm_c_im": g_c_im,
             "ssm_d": g_ssm_d, "b_glu": g_b_glu, "b_mix_out": g_b_mix, "ln1_g": g_ln1_g, "ln1_b": g_ln1_b,
             "ln2_g": g_ln2_g, "ln2_b": g_ln2_b, "b_ff1": g_b_ff1, "b_ff2": g_b_ff2, "ln3_g": g_ln3_g,
             "ln3_b": g_ln3_b}
    return loss, dx, small


def _piece_shape(k, n, axis):
    return (k // 2, n // 4) if axis == 1 else (k // 8, n)


def _aligned(v, m):
    return v if isinstance(v, int) else pl.multiple_of(v, m)


def _full_piece(ref, k, n, axis, chip, half):
    pr, pc = _piece_shape(k, n, axis)
    if axis == 1:
        return ref.at[pl.ds(_aligned(half * pr, 8), pr), pl.ds(_aligned(chip * pc, 128), pc)]
    return ref.at[pl.ds(_aligned(chip * (2 * pr) + half * pr, 8), pr), :]


def _full_shard(ref, k, n, axis, chip):
    if axis == 1:
        return ref.at[:, pl.ds(_aligned(chip * (n // 4), 128), n // 4)]
    return ref.at[pl.ds(_aligned(chip * (k // 4), 8), k // 4), :]


def _shard_piece(ref, k, n, axis, half):
    pr, _ = _piece_shape(k, n, axis)
    return ref.at[pl.ds(_aligned(half * pr, 8), pr), :]


def _mesh_pos():
    x, y, c = lax.axis_index("x"), lax.axis_index("y"), lax.axis_index("c")
    other_chips = [(1 - x, y), (x, 1 - y), (1 - x, 1 - y)]
    return x, y, c, other_chips


def _remote(src, dst, send_sem, recv_sem, dev):
    return pltpu.make_async_remote_copy(src_ref=src, dst_ref=dst, send_sem=send_sem, recv_sem=recv_sem,
                                        device_id=dev, device_id_type=MESH)


def _placed(name, fn, n_steps, where, ins, out_sds, out_block, out_index):
    def body(w_ref, *refs):
        o_ref = refs[-1]
        o_ref[...] = fn(*[r[...] for r in refs[:-1]]).astype(o_ref.dtype)

    grid_spec = pltpu.PrefetchScalarGridSpec(
        num_scalar_prefetch=1, grid=(n_steps,), in_specs=[pl.BlockSpec(bs, idx) for _, bs, idx in ins],
        out_specs=pl.BlockSpec(out_block, out_index))
    return pl.pallas_call(body, name=name, grid_spec=grid_spec, out_shape=out_sds,
                          compiler_params=_cparams(1))(where, *[a for a, _, _ in ins])


def _gather_copies(widx):
    geo = [BIG[i][1:] for i in widx]

    def ici(full, wi, j, chip, send_sems, recv_sems, c, dev):
        k, n, ax = geo[wi]
        piece = _full_piece(full[wi], k, n, ax, chip, c)
        return _remote(piece, piece, send_sems.at[wi * 6 + j], recv_sems.at[wi * 6 + j], dev)

    def d2d(full, wi, j, chip, half, send_sems, recv_sems, sib):
        k, n, ax = geo[wi]
        piece = _full_piece(full[wi], k, n, ax, chip, half)
        return _remote(piece, piece, send_sems.at[wi * 6 + 3 + j], recv_sems.at[wi * 6 + 3 + j], sib)

    def start(_, full, send_sems, recv_sems):
        x, y, c, chips = _mesh_pos()
        for wi in range(len(geo)):
            for j, (qx, qy) in enumerate(chips):
                ici(full, wi, j, 2 * x + y, send_sems, recv_sems, c, (qx, qy, c)).start()

    def finish(_, full, send_sems, recv_sems):
        x, y, c, chips = _mesh_pos()
        sib = (x, y, 1 - c)
        for wi in range(len(geo)):
            for j, (qx, qy) in enumerate(chips):
                ici(full, wi, j, 2 * qx + qy, send_sems, recv_sems, c, (qx, qy, c)).wait_recv()
                d2d(full, wi, j, 2 * qx + qy, c, send_sems, recv_sems, sib).start()
        for wi in range(len(geo)):
            for j, (qx, qy) in enumerate(chips):
                d2d(full, wi, j, 2 * qx + qy, 1 - c, send_sems, recv_sems, sib).wait_recv()
        for wi in range(len(geo)):
            for j, (qx, qy) in enumerate(chips):
                ici(full, wi, j, 2 * x + y, send_sems, recv_sems, c, (qx, qy, c)).wait_send()
                d2d(full, wi, j, 2 * qx + qy, c, send_sems, recv_sems, sib).wait_send()

    return start, finish, 6 * len(geo)


def _gather_weights(tag, fulls, widx):
    nw = len(widx)
    start, finish, n_sems = _gather_copies(widx)

    def body(*refs):
        full = refs[nw:2 * nw]
        start(None, full, *refs[2 * nw:])
        finish(None, full, *refs[2 * nw:])

    return pl.pallas_call(
        body, name="gather_weights_" + tag, in_specs=[HBM_SPEC] * nw, out_specs=[HBM_SPEC] * nw,
        out_shape=[jax.ShapeDtypeStruct(f.shape, f.dtype) for f in fulls],
        input_output_aliases={i: i for i in range(nw)},
        scratch_shapes=[pltpu.SemaphoreType.DMA((n_sems,)), pltpu.SemaphoreType.DMA((n_sems,))])(*fulls)


def _swap_copies(widx):
    geo = [BIG[i][1:] for i in widx]

    def copies(g, got, send_sems, recv_sems, base):
        x, y, c, _ = _mesh_pos()
        return [_remote(_full_piece(g[wi], k, n, ax, q, 1 - c), got[wi].at[q], send_sems.at[base + wi * 4 + q],
                        recv_sems.at[base + wi * 4 + q], (x, y, 1 - c))
                for wi, (k, n, ax) in enumerate(geo) for q in range(4)]

    def start(g, got, send_sems, recv_sems, base=0):
        for cp in copies(g, got, send_sems, recv_sems, base):
            cp.start()

    def finish(g, got, send_sems, recv_sems, base=0):
        for cp in copies(g, got, send_sems, recv_sems, base):
            cp.wait()

    return start, finish, 4 * len(geo)


def _swap_shapes(widx):
    return [jax.ShapeDtypeStruct((4,) + _piece_shape(*BIG[i][1:]), F32) for i in widx]


def _reduce_swap_halves(tag, grads, widx):
    nw = len(widx)
    start, finish, n_sems = _swap_copies(widx)

    def body(*refs):
        start(refs[:nw], refs[nw:2 * nw], *refs[2 * nw:])
        finish(refs[:nw], refs[nw:2 * nw], *refs[2 * nw:])

    return pl.pallas_call(
        body, name="reduce_swap_halves_" + tag, in_specs=[HBM_SPEC] * nw, out_specs=[HBM_SPEC] * nw,
        out_shape=_swap_shapes(widx),
        scratch_shapes=[pltpu.SemaphoreType.DMA((n_sems,)), pltpu.SemaphoreType.DMA((n_sems,))])(*grads)


def _owner_copies(nw):
    def copies(p, out, send_sems, recv_sems, base):
        x, y, c, chips = _mesh_pos()
        return [_remote(p[wi].at[2 * qx + qy], out[wi].at[j], send_sems.at[base + wi * 3 + j],
                        recv_sems.at[base + wi * 3 + j], (qx, qy, c))
                for wi in range(nw) for j, (qx, qy) in enumerate(chips)]

    def start(p, out, send_sems, recv_sems, base=0):
        for cp in copies(p, out, send_sems, recv_sems, base):
            cp.start()

    def finish(p, out, send_sems, recv_sems, base=0):
        for cp in copies(p, out, send_sems, recv_sems, base):
            cp.wait()

    return start, finish, 3 * nw


def _join_carries(a, b):
    if a is None or b is None:
        return a if b is None else b
    n_i, n_o = len(a.ins), len(a.outs)
    outs = list(a.outs) + [o + n_i if isinstance(o, int) else o for o in b.outs]

    def start(c_in, c_out, send_sems, recv_sems):
        a.start(c_in[:n_i], c_out[:n_o], send_sems, recv_sems)
        b.start(c_in[n_i:], c_out[n_o:], send_sems, recv_sems, base=a.n_sems)

    def finish(c_in, c_out, send_sems, recv_sems):
        a.finish(c_in[:n_i], c_out[:n_o], send_sems, recv_sems)
        b.finish(c_in[n_i:], c_out[n_o:], send_sems, recv_sems, base=a.n_sems)

    def done(res):
        a.done(res[:n_o])
        b.done(res[n_o:])

    return _Carry(a.ins + b.ins, outs, a.n_sems + b.n_sems, start, finish, done)


def _share_with_sibling(shards):
    nw = len(BIG)

    def body(*refs):
        out = refs[nw:2 * nw]
        send_sems, recv_sems = refs[2 * nw:]
        x, y, c, _ = _mesh_pos()
        sib = (x, y, 1 - c)
        cps = []
        for wi, (_, k, n, ax) in enumerate(BIG):
            mine = _shard_piece(out[wi], k, n, ax, c)
            cp = _remote(mine, mine, send_sems.at[wi], recv_sems.at[wi], sib)
            cp.start()
            cps.append(cp)
        for wi, (_, k, n, ax) in enumerate(BIG):
            piece = _shard_piece(out[wi], k, n, ax, 1 - c)
            _remote(piece, piece, send_sems.at[wi], recv_sems.at[wi], sib).wait_recv()
        for cp in cps:
            cp.wait_send()

    return pl.pallas_call(
        body, name="share_with_sibling", in_specs=[HBM_SPEC] * nw, out_specs=[HBM_SPEC] * nw,
        out_shape=[jax.ShapeDtypeStruct(sh.shape, sh.dtype) for sh in shards],
        input_output_aliases={i: i for i in range(nw)},
        scratch_shapes=[pltpu.SemaphoreType.DMA((nw,)), pltpu.SemaphoreType.DMA((nw,))])(*shards)


def _allreduce_small(v):
    r = v.shape[0]
    rh = r // 2
    assert rh % 8 == 0

    def body(v_ref, o_ref, sib_buf, chip_buf, send_sems, recv_sems):
        x, y, c, chips = _mesh_pos()
        me = 2 * x + y
        sib = (x, y, 1 - c)
        mine = pl.ds(pl.multiple_of(c * rh, 8), rh)
        other = pl.ds(pl.multiple_of((1 - c) * rh, 8), rh)
        swap = _remote(v_ref.at[other], sib_buf, send_sems.at[0], recv_sems.at[0], sib)
        swap.start()
        swap.wait()
        chip_buf[me] = v_ref[mine, :] + sib_buf[...]
        cps = []
        for j, (qx, qy) in enumerate(chips):
            cp = _remote(chip_buf.at[me], chip_buf.at[me], send_sems.at[1 + j], recv_sems.at[1 + j], (qx, qy, c))
            cp.start()
            cps.append(cp)
        for j, (qx, qy) in enumerate(chips):
            slot = chip_buf.at[2 * qx + qy]
            _remote(slot, slot, send_sems.at[1 + j], recv_sems.at[1 + j], (qx, qy, c)).wait_recv()
        for cp in cps:
            cp.wait_send()
        o_ref[mine, :] = ((chip_buf[0] + chip_buf[1]) + chip_buf[2]) + chip_buf[3]
        back = _remote(o_ref.at[mine], o_ref.at[mine], send_sems.at[4], recv_sems.at[4], sib)
        back.start()
        _remote(o_ref.at[other], o_ref.at[other], send_sems.at[4], recv_sems.at[4], sib).wait_recv()
        back.wait_send()

    return pl.pallas_call(
        body, name="allreduce_small", in_specs=[VMEM_SPEC], out_specs=VMEM_SPEC,
        out_shape=jax.ShapeDtypeStruct((r, 128), F32),
        scratch_shapes=[pltpu.VMEM((rh, 128), F32), pltpu.VMEM((4, rh, 128), F32),
                        pltpu.SemaphoreType.DMA((5,)), pltpu.SemaphoreType.DMA((5,))],
        compiler_params=pltpu.CompilerParams(vmem_limit_bytes=VMEM_LIMIT))(v)


def _as2d(a):
    a = a.reshape((-1, a.shape[-1])) if a.ndim > 1 else a.reshape(1, -1)
    return a


def _adamw_small(quads):
    n = len(quads)

    def body(*refs):
        for i in range(n):
            w, g, m, v = (r[...] for r in refs[4 * i:4 * i + 4])
            for ref, val in zip(refs[4 * n + 3 * i:4 * n + 3 * i + 3], _adamw(w, g, m, v)):
                ref[...] = val

    return pl.pallas_call(
        body, name="adamw_small", in_specs=[VMEM_SPEC] * (4 * n), out_specs=[VMEM_SPEC] * (3 * n),
        out_shape=[jax.ShapeDtypeStruct(q[0].shape, F32) for q in quads for _ in range(3)],
        compiler_params=pltpu.CompilerParams(vmem_limit_bytes=VMEM_LIMIT))(*[a for q in quads for a in q])


def _where():
    return jnp.stack([2 * lax.axis_index("x") + lax.axis_index("y"), lax.axis_index("c")]).astype(jnp.int32)


_BIG_INDEX = {name: i for i, (name, _, _, _) in enumerate(BIG)}


class _LocalWeights:
    def __init__(self, weights):
        self.weights, self.grads = weights, {}

    def gather_now(self, names):
        pass

    def gather_carry(self, names):
        return None

    def weight(self, name):
        return self.weights[name]

    def grad(self, name, g):
        self.grads[name] = g

    def carry(self, swap=(), ici=()):
        return None


class _Exchange:
    def __init__(self, inputs, where):
        self.inputs, self.where = inputs, where
        self.full, self.ready = {}, set()
        self.raw, self.got, self.parts, self.landed, self.geom = {}, {}, {}, {}, {}
        for name, k, n, ax in BIG:
            w2 = inputs[name][0]
            rs, cs = w2.shape
            tm = _tile(rs, 512)
            steps = rs // tm
            if ax == 1:
                blk, idx = (tm, cs), lambda i, w: (i, w[0])
            else:
                blk, idx = (tm, n), functools.partial(lambda i, w, steps: (w[0] * steps + i, 0), steps=steps)
            self.full[name] = _placed("cast_" + name, lambda w: w, steps, where, [(w2, (tm, cs), lambda i, w: (i, 0))],
                                      jax.ShapeDtypeStruct((k, n), MXU_DTYPE), blk, idx)

    def _gathered(self, names, outs):
        for name, o in zip(names, outs):
            self.full[name] = o
            self.ready.add(name)

    def gather_now(self, names):
        self._gathered(names, _gather_weights(names[0], [self.full[n] for n in names], [_BIG_INDEX[n] for n in names]))

    def gather_carry(self, names):
        start, finish, n_sems = _gather_copies([_BIG_INDEX[n] for n in names])
        return _Carry([self.full[n] for n in names], list(range(len(names))), n_sems, start, finish,
                      functools.partial(self._gathered, names))

    def weight(self, name):
        assert name in self.ready, name
        return self.full[name]

    def grad(self, name, g):
        self.raw[name] = g

    def _swapped(self, names, outs):
        for name, o in zip(names, outs):
            self.got[name] = o

    def _pair_sum(self, name):
        i = _BIG_INDEX[name]
        _, k, n, ax = BIG[i]
        g = self.raw[name]
        if name not in self.got:
            self._swapped([name], _reduce_swap_halves(name, [g], [i]))
        got = self.got[name]
        pr, pc = _piece_shape(k, n, ax)
        tm = _tile(pr, 512)
        spp = pr // tm
        self.geom[name] = (pr, pc, tm, spp)
        if ax == 1:
            g_idx = functools.partial(lambda i, w, spp: (w[1] * spp + i % spp, i // spp), spp=spp)
        else:
            g_idx = functools.partial(lambda i, w, spp: ((i // spp) * 2 * spp + w[1] * spp + i % spp, 0), spp=spp)
        self.parts[name] = _placed(
            "pair_sum_" + name, lambda a, b: a + b, 4 * spp, self.where,
            [(g, (tm, pc), g_idx), (got.reshape(4 * pr, pc), (tm, pc), lambda i, w: (i, 0))],
            jax.ShapeDtypeStruct((4 * pr, pc), BF16), (tm, pc), lambda i, w: (i, 0)).reshape(4, pr, pc)

    def _landed(self, names, outs):
        for name, o in zip(names, outs):
            self.landed[name] = o

    def carry(self, swap=(), ici=()):
        first = second = None
        if swap:
            widx = [_BIG_INDEX[n] for n in swap]
            start, finish, n_sems = _swap_copies(widx)
            first = _Carry([self.raw[n] for n in swap], _swap_shapes(widx), n_sems, start, finish,
                           functools.partial(self._swapped, list(swap)))
        if ici:
            for n in ici:
                self._pair_sum(n)
            start, finish, n_sems = _owner_copies(len(ici))
            parts = [self.parts[n] for n in ici]
            outs = [jax.ShapeDtypeStruct((3,) + p.shape[1:], p.dtype) for p in parts]
            second = _Carry(parts, outs, n_sems, start, finish, functools.partial(self._landed, list(ici)))
        return _join_carries(first, second)

    def finish(self):
        halves = []
        for name, _, _, _ in BIG:
            pr, pc, tm, spp = self.geom[name]
            ins = [(self.parts[name], (None, tm, pc), lambda i, w: (w[0], i, 0))]
            ins += [(self.landed[name], (None, tm, pc), functools.partial(lambda i, w, j: (j, i, 0), j=j))
                    for j in range(3)]
            halves.append(_placed("chip_sum_" + name,
                                  lambda a, b, c, d: ((a.astype(F32) + b.astype(F32)) + c.astype(F32)) + d.astype(F32),
                                  spp, self.where, ins, jax.ShapeDtypeStruct(self.inputs[name].shape[1:], F32), (tm, pc),
                                  functools.partial(lambda i, w, spp: (w[1] * spp + i, 0), spp=spp)))
        return dict(zip([b[0] for b in BIG], _share_with_sibling(halves)))


def _step(inputs):
    x, mem, positions, target = inputs["x"][0], inputs["mem"][0], inputs["positions"], inputs["loss_target"][0]
    pos = positions.reshape(-1, 1)
    ex = _Exchange(inputs, _where())
    sp = {name: _as2d(inputs[name]) for name in SMALL}
    memb, = _rowwise("cast_mem", lambda m: (m,), [mem], [], [(D_MODEL, MXU_DTYPE)])

    loss, dx, gsmall = _local_step(x, memb, pos, target, sp, ex)
    gshard = ex.finish()

    out = {}
    for name, _, _, _ in BIG:
        w2, m2, v2 = inputs[name][0], inputs["m_" + name][0], inputs["v_" + name][0]
        n = w2.shape[1]
        d, nm, nv = _rowwise("adamw_" + name, _adamw, [w2, gshard[name], m2, v2], [], [(n, F32)] * 3, tm=256)
        lead = inputs[name].shape
        out[name] = (gshard[name].reshape(lead), d.reshape(lead), nm.reshape(lead), nv.reshape(lead))

    def tiles(a):
        flat = a.reshape(-1)
        n = -(-flat.shape[0] // 1024) * 1024
        return jnp.pad(flat, (0, n - flat.shape[0])).reshape(n // 128, 128)

    pieces = [tiles(loss[:, :1])] + [tiles(gsmall[name]) for name in SMALL]
    if sum(p.shape[0] for p in pieces) % 16:
        pieces.append(jnp.zeros((8, 128), F32))
    red = _allreduce_small(jnp.concatenate(pieces, axis=0))
    loss_total = red[0, 0]
    grads, off = {}, pieces[0].shape[0]
    for name, p in zip(SMALL, pieces[1:]):
        shp = _as2d(inputs[name]).shape
        grads[name] = red[off:off + p.shape[0]].reshape(-1)[:shp[0] * shp[1]].reshape(shp)
        off += p.shape[0]
    upd = _adamw_small([(_as2d(inputs[n]), grads[n], _as2d(inputs["m_" + n]), _as2d(inputs["v_" + n])) for n in SMALL])
    for i, name in enumerate(SMALL):
        shp = inputs[name].shape
        out[name] = (grads[name].reshape(shp),) + tuple(t.reshape(shp) for t in upd[3 * i:3 * i + 3])
    return loss_total, dx.reshape(inputs["x"].shape), out


_ARG_NAMES = (("x", "mem", "positions") + WEIGHT_ORDER + ("loss_target",) + tuple("m_" + n for n in WEIGHT_ORDER)
              + tuple("v_" + n for n in WEIGHT_ORDER))


def kernel(x, mem, positions, ln_in_g, ln_in_b, w_in, b_in, ssm_log_dt, ssm_a_re, ssm_a_im, ssm_b_re, ssm_b_im, ssm_c_re, ssm_c_im, ssm_d, w_glu, b_glu, w_att_up, w_mix_out, b_mix_out, ln1_g, ln1_b, w_xq, w_xkv, w_xo, ln2_g, ln2_b, w_ff1, b_ff1, w_ff2, b_ff2, ln3_g, ln3_b, loss_target, m_ln_in_g, m_ln_in_b, m_w_in, m_b_in, m_ssm_log_dt, m_ssm_a_re, m_ssm_a_im, m_ssm_b_re, m_ssm_b_im, m_ssm_c_re, m_ssm_c_im, m_ssm_d, m_w_glu, m_b_glu, m_w_att_up, m_w_mix_out, m_b_mix_out, m_ln1_g, m_ln1_b, m_w_xq, m_w_xkv, m_w_xo, m_ln2_g, m_ln2_b, m_w_ff1, m_b_ff1, m_w_ff2, m_b_ff2, m_ln3_g, m_ln3_b, v_ln_in_g, v_ln_in_b, v_w_in, v_b_in, v_ssm_log_dt, v_ssm_a_re, v_ssm_a_im, v_ssm_b_re, v_ssm_b_im, v_ssm_c_re, v_ssm_c_im, v_ssm_d, v_w_glu, v_b_glu, v_w_att_up, v_w_mix_out, v_b_mix_out, v_ln1_g, v_ln1_b, v_w_xq, v_w_xkv, v_w_xo, v_ln2_g, v_ln2_b, v_w_ff1, v_b_ff1, v_w_ff2, v_b_ff2, v_ln3_g, v_ln3_b):
    args = (x, mem, positions, ln_in_g, ln_in_b, w_in, b_in, ssm_log_dt, ssm_a_re, ssm_a_im, ssm_b_re, ssm_b_im, ssm_c_re, ssm_c_im, ssm_d, w_glu, b_glu, w_att_up, w_mix_out, b_mix_out, ln1_g, ln1_b, w_xq, w_xkv, w_xo, ln2_g, ln2_b, w_ff1, b_ff1, w_ff2, b_ff2, ln3_g, ln3_b, loss_target, m_ln_in_g, m_ln_in_b, m_w_in, m_b_in, m_ssm_log_dt, m_ssm_a_re, m_ssm_a_im, m_ssm_b_re, m_ssm_b_im, m_ssm_c_re, m_ssm_c_im, m_ssm_d, m_w_glu, m_b_glu, m_w_att_up, m_w_mix_out, m_b_mix_out, m_ln1_g, m_ln1_b, m_w_xq, m_w_xkv, m_w_xo, m_ln2_g, m_ln2_b, m_w_ff1, m_b_ff1, m_w_ff2, m_b_ff2, m_ln3_g, m_ln3_b, v_ln_in_g, v_ln_in_b, v_w_in, v_b_in, v_ssm_log_dt, v_ssm_a_re, v_ssm_a_im, v_ssm_b_re, v_ssm_b_im, v_ssm_c_re, v_ssm_c_im, v_ssm_d, v_w_glu, v_b_glu, v_w_att_up, v_w_mix_out, v_b_mix_out, v_ln1_g, v_ln1_b, v_w_xq, v_w_xkv, v_w_xo, v_ln2_g, v_ln2_b, v_w_ff1, v_b_ff1, v_w_ff2, v_b_ff2, v_ln3_g, v_ln3_b)
    assert len(args) == len(_ARG_NAMES)
    inputs = dict(zip(_ARG_NAMES, args))
    loss, dx, out = _step(inputs)
    res = [loss, dx]
    for k in range(4):
        res += [out[name][k] for name in WEIGHT_ORDER]
    return tuple(res)
```

```python
import functools
import math

import numpy as np
import jax
import jax.numpy as jnp
from jax import lax
from jax.experimental import pallas as pl
from jax.experimental.pallas import tpu as pltpu

F32 = jnp.float32
BF16 = jnp.bfloat16
MXU_DTYPE = jnp.bfloat16

D_MODEL = 1024
SSM_GROUP = 16
SSM_WIDTH = 768
SSM_GROUPS = 48
SSM_STATE = 64
N_STATE = SSM_GROUPS * SSM_STATE
SSM_CHUNKS = 6
CH_W = 128
CH_N = 512
ATT_HEAD_DIM = 64
ATT_HPG = 4
ATT_GROUPW = ATT_HPG * ATT_HEAD_DIM
DILATIONS = (1, 4, 16)
ATT_BLK = 128
ATT_SCALE = ATT_HEAD_DIM ** -0.5
ROT_DIM = 16
ROPE_THETA = 500000.0
XATT_HEADS = 4
XATT_HEAD_DIM = 256
XATT_SCALE = XATT_HEAD_DIM ** -0.5
D_FF = 4096
IN_COLS = 5120
DEEPNORM_ALPHA = 2.0 ** 0.25
LN_EPS = 1e-5
NEG_INF = -1e30
ADAM_LR = 0.001
ADAM_B1 = 0.9
ADAM_B2 = 0.999
ADAM_EPS = 1e-08
ADAM_WD = 0.01
ADAM_STEP = 10

N_SEG = 32
VMEM_LIMIT = 56 * 1024 * 1024
MESH = pl.DeviceIdType.MESH
HBM_SPEC = pl.BlockSpec(memory_space=pltpu.HBM)
VMEM_SPEC = pl.BlockSpec(memory_space=pltpu.VMEM)

BIG = (("w_in", 1024, 5120, 1), ("w_glu", 768, 2048, 1), ("w_att_up", 256, 1024, 1),
       ("w_mix_out", 1024, 1024, 0), ("w_xq", 1024, 1024, 0), ("w_xkv", 1024, 2048, 1),
       ("w_xo", 1024, 1024, 0), ("w_ff1", 1024, 4096, 1), ("w_ff2", 4096, 1024, 0))
SMALL = ("ln_in_g", "ln_in_b", "b_in", "ssm_log_dt", "ssm_a_re", "ssm_a_im", "ssm_b_re", "ssm_b_im",
         "ssm_c_re", "ssm_c_im", "ssm_d", "b_glu", "b_mix_out", "ln1_g", "ln1_b", "ln2_g", "ln2_b",
         "b_ff1", "b_ff2", "ln3_g", "ln3_b")
WEIGHT_ORDER = ("ln_in_g", "ln_in_b", "w_in", "b_in", "ssm_log_dt", "ssm_a_re", "ssm_a_im", "ssm_b_re",
                "ssm_b_im", "ssm_c_re", "ssm_c_im", "ssm_d", "w_glu", "b_glu", "w_att_up", "w_mix_out",
                "b_mix_out", "ln1_g", "ln1_b", "w_xq", "w_xkv", "w_xo", "ln2_g", "ln2_b", "w_ff1", "b_ff1",
                "w_ff2", "b_ff2", "ln3_g", "ln3_b")


def _cparams(n_axes):
    return pltpu.CompilerParams(dimension_semantics=("arbitrary",) * n_axes, vmem_limit_bytes=VMEM_LIMIT)


class _Carry:
    def __init__(self, ins, outs, n_sems, start, finish, done):
        self.ins, self.outs, self.n_sems, self.start, self.finish, self.done = ins, outs, n_sems, start, finish, done


def _call(name, body, grid, in_specs, out_specs, out_shape, args, scratch_shapes=(), carry=None):
    in_specs, out_specs, out_shape = list(in_specs), list(out_specs), list(out_shape)
    params = _cparams(len(grid))
    if carry is None:
        return pl.pallas_call(body, name=name, grid=grid, in_specs=in_specs, out_specs=out_specs, out_shape=out_shape,
                              scratch_shapes=list(scratch_shapes), compiler_params=params)(*args)
    n_in, n_out, n_ci, n_co = len(in_specs), len(out_specs), len(carry.ins), len(carry.outs)
    n_scr = len(scratch_shapes)

    def wrapped(*refs):
        ins, c_in = refs[:n_in], refs[n_in:n_in + n_ci]
        outs, c_out = refs[n_in + n_ci:n_in + n_ci + n_out], refs[n_in + n_ci + n_out:n_in + n_ci + n_out + n_co]
        scratch = refs[n_in + n_ci + n_out + n_co:n_in + n_ci + n_out + n_co + n_scr]
        send_sems, recv_sems = refs[-2:]
        ids = [pl.program_id(a) for a in range(len(grid))]
        first = functools.reduce(jnp.logical_and, [i == 0 for i in ids])
        last = functools.reduce(jnp.logical_and, [i == g - 1 for i, g in zip(ids, grid)])

        @pl.when(first)
        def _():
            carry.start(c_in, c_out, send_sems, recv_sems)

        body(*ins, *outs, *scratch)

        @pl.when(last)
        def _():
            carry.finish(c_in, c_out, send_sems, recv_sems)

    c_shapes = [jax.ShapeDtypeStruct(carry.ins[o].shape, carry.ins[o].dtype) if isinstance(o, int) else o
                for o in carry.outs]
    aliases = {n_in + o: n_out + i for i, o in enumerate(carry.outs) if isinstance(o, int)}
    res = pl.pallas_call(
        wrapped, name=name, grid=grid, in_specs=in_specs + [HBM_SPEC] * n_ci, out_specs=out_specs + [HBM_SPEC] * n_co,
        out_shape=out_shape + c_shapes, input_output_aliases=aliases,
        scratch_shapes=list(scratch_shapes) + [pltpu.SemaphoreType.DMA((carry.n_sems,))] * 2,
        compiler_params=params)(*args, *carry.ins)
    carry.done(res[n_out:])
    return res[:n_out]


def _rowwise(name, fn, rows, consts, outs, reds=(), tm=512, touts=(), carry=None):
    n_rows = (rows[0][0] if isinstance(rows[0], tuple) else rows[0]).shape[-2]
    tm = min(tm, n_rows)
    assert n_rows % tm == 0, (name, n_rows, tm)
    specs, args = [], []
    for r in rows:
        if isinstance(r, tuple) and len(r) == 3:
            arr, width, cb = r
            specs.append(pl.BlockSpec((tm, width), functools.partial(lambda i, cb: (i, cb), cb=cb)))
        elif isinstance(r, tuple):
            arr, slot = r
            specs.append(pl.BlockSpec((None, tm, arr.shape[2]), functools.partial(lambda i, s: (s, i, 0), s=slot)))
        else:
            arr = r
            specs.append(pl.BlockSpec((tm, arr.shape[1]), lambda i: (i, 0)))
        args.append(arr)
        assert arr.shape[-2] == n_rows, (name, arr.shape, n_rows)
    for cst in consts:
        specs.append(pl.BlockSpec(cst.shape, lambda i: (0, 0)))
        args.append(cst)
    n_r, n_c, n_o, n_d = len(rows), len(consts), len(outs) + len(touts), len(reds)
    out_shape = [jax.ShapeDtypeStruct((n_rows, c), dt) for c, dt in outs]
    out_specs = [pl.BlockSpec((tm, c), lambda i: (i, 0)) for c, _ in outs]
    out_shape += [jax.ShapeDtypeStruct((r, n_rows), dt) for r, dt in touts]
    out_specs += [pl.BlockSpec((r, tm), lambda i: (0, i)) for r, _ in touts]
    out_shape += [jax.ShapeDtypeStruct((1, c), F32) for c in reds]
    out_specs += [pl.BlockSpec((1, c), lambda i: (0, 0)) for c in reds]

    def body(*refs):
        ins = [r[...] for r in refs[:n_r + n_c]]
        o_refs = refs[n_r + n_c:n_r + n_c + n_o]
        d_refs = refs[n_r + n_c + n_o:]
        res = fn(*ins)
        res = res if isinstance(res, (tuple, list)) else (res,)
        assert len(res) == n_o + n_d, (name, len(res))
        for ref, val in zip(o_refs, res[:n_o]):
            ref[...] = val.astype(ref.dtype)
        first = pl.program_id(0) == 0
        for ref, val in zip(d_refs, res[n_o:]):
            @pl.when(first)
            def _(ref=ref, val=val):
                ref[...] = val

            @pl.when(jnp.logical_not(first))
            def _(ref=ref, val=val):
                ref[...] += val

    return _call(name, body, (n_rows // tm,), specs, out_specs, out_shape, args, carry=carry)


def _colsum(v):
    return jnp.sum(v.astype(F32), axis=0, keepdims=True)


_DIMS = {"nn": (((1,), (0,)), ((), ())), "nt": (((1,), (1,)), ((), ())), "tn": (((0,), (0,)), ((), ()))}


def _tile(dim, want):
    if dim <= want:
        return dim
    return max(t for t in range(128, want + 1, 128) if dim % t == 0)


def _dot(a, b, mode):
    return lax.dot_general(a.astype(MXU_DTYPE), b.astype(MXU_DTYPE), _DIMS[mode], preferred_element_type=F32)


def _mm(name, a, b, mode, *, bias=None, extras=(), epilogue=None, out_dtypes=(F32,), tm=1024, tn=1024, tk=1024,
        carry=None):
    if mode == "nn":
        (m, k), (_, n) = a.shape, b.shape
    elif mode == "nt":
        (m, k), (n, _) = a.shape, b.shape
    else:
        (k, m), (_, n) = a.shape, b.shape
    if k > tk:
        tk = 5 * tk
    tn = _tile(n, tn)
    tk = _tile(k, tk)
    nk = k // tk

    def vmem_bytes(rows):
        blocks = rows * tk * a.dtype.itemsize + tk * tn * b.dtype.itemsize
        blocks += sum(rows * tn * e.dtype.itemsize for e in extras)
        blocks += sum(rows * tn * jnp.dtype(dt).itemsize for dt in out_dtypes)
        return 2 * blocks + (rows * tn * 4 if nk > 1 else 0)

    tm = _tile(m, tm if mode == "tn" else 2 * tm)
    while vmem_bytes(tm) > 3 * VMEM_LIMIT // 4 and tm % 256 == 0:
        tm //= 2
    while nk == 1 and k > 1024 and (m // tm) * (n // tn) < 4 and tm % 256 == 0:
        tm //= 2
    assert m % tm == 0 and n % tn == 0 and k % tk == 0, (name, m, n, k)
    a_spec = {"nn": pl.BlockSpec((tm, tk), lambda i, j, kk: (i, kk)),
              "nt": pl.BlockSpec((tm, tk), lambda i, j, kk: (i, kk)),
              "tn": pl.BlockSpec((tk, tm), lambda i, j, kk: (kk, i))}[mode]
    b_spec = {"nn": pl.BlockSpec((tk, tn), lambda i, j, kk: (kk, j)),
              "nt": pl.BlockSpec((tn, tk), lambda i, j, kk: (j, kk)),
              "tn": pl.BlockSpec((tk, tn), lambda i, j, kk: (kk, j))}[mode]
    specs, args = [a_spec, b_spec], [a, b]
    if bias is not None:
        specs.append(pl.BlockSpec((1, tn), lambda i, j, kk: (0, j)))
        args.append(bias)
    for e in extras:
        specs.append(pl.BlockSpec((tm, tn), lambda i, j, kk: (i, j)))
        args.append(e)
    n_e, n_o = len(extras), len(out_dtypes)
    has_bias = bias is not None

    def body(*refs):
        a_ref, b_ref = refs[0], refs[1]
        pos = 2
        bias_ref = refs[pos] if has_bias else None
        pos += int(has_bias)
        e_refs = refs[pos:pos + n_e]
        o_refs = refs[pos + n_e:pos + n_e + n_o]
        acc_ref = refs[pos + n_e + n_o] if nk > 1 else None
        part = _dot(a_ref[...], b_ref[...], mode)

        def finish(r):
            if has_bias:
                r = r + bias_ref[...]
            res = epilogue(r, *[e[...] for e in e_refs]) if epilogue is not None else (r,)
            for ref, val in zip(o_refs, res):
                ref[...] = val.astype(ref.dtype)

        if nk == 1:
            finish(part)
        else:
            kk = pl.program_id(2)

            @pl.when(kk == 0)
            def _():
                acc_ref[...] = part

            @pl.when(kk > 0)
            def _():
                acc_ref[...] += part

            @pl.when(kk == nk - 1)
            def _():
                finish(acc_ref[...])

    res = _call(name, body, (m // tm, n // tn, nk), specs,
                [pl.BlockSpec((tm, tn), lambda i, j, kk: (i, j)) for _ in out_dtypes],
                [jax.ShapeDtypeStruct((m, n), dt) for dt in out_dtypes], args,
                scratch_shapes=[pltpu.VMEM((tm, tn), F32)] if nk > 1 else [], carry=carry)
    return res[0] if n_o == 1 else res


def _ssm_wgrads(u, dy, g_re, g_im, h_re, h_im, tk=1024):
    s = u.shape[0]
    tk = min(tk, s)
    nk = s // tk
    assert tk % N_SEG == 0

    def body(u_ref, dy_ref, gre_ref, gim_ref, hre_ref, him_ref, lre_ref, lim_ref, db_ref, dc_ref, dar_ref, dai_ref,
             pre_ref, pim_ref):
        kk = pl.program_id(1)
        u_blk, dy_blk = u_ref[...], dy_ref[...]
        g_r, g_i, h_r, h_i = gre_ref[...], gim_ref[...], hre_ref[...], him_ref[...]
        d_b = jnp.concatenate([_dot(u_blk, g_r, "tn"), _dot(u_blk, g_i, "tn")], axis=1)
        d_c = jnp.concatenate([_dot(h_r, dy_blk, "tn"), _dot(h_i, dy_blk, "tn")], axis=0)

        @pl.when(kk == 0)
        def _():
            first_row = lax.broadcasted_iota(jnp.int32, (N_SEG, CH_N), 0) == 0
            pre_ref[...] = jnp.where(first_row, 0.0, pltpu.roll(lre_ref[...], 1, 0))
            pim_ref[...] = jnp.where(first_row, 0.0, pltpu.roll(lim_ref[...], 1, 0))

        p_r = jnp.concatenate([pre_ref[...], h_r[:tk - N_SEG]], axis=0)
        p_i = jnp.concatenate([pim_ref[...], h_i[:tk - N_SEG]], axis=0)
        pre_ref[...] = h_r[tk - N_SEG:]
        pim_ref[...] = h_i[tk - N_SEG:]
        d_ar = jnp.sum(g_r * p_r + g_i * p_i, axis=0, keepdims=True)
        d_ai = jnp.sum(g_i * p_r - g_r * p_i, axis=0, keepdims=True)

        @pl.when(kk == 0)
        def _():
            db_ref[...] = d_b
            dc_ref[...] = d_c
            dar_ref[...] = d_ar
            dai_ref[...] = d_ai

        @pl.when(kk > 0)
        def _():
            db_ref[...] += d_b
            dc_ref[...] += d_c
            dar_ref[...] += d_ar
            dai_ref[...] += d_ai

    chan = pl.BlockSpec((tk, CH_W), lambda j, kk: (kk, j))
    state = pl.BlockSpec((tk, CH_N), lambda j, kk: (kk, j))
    last = pl.BlockSpec((N_SEG, CH_N), lambda j, kk: (s // N_SEG - 1, j))
    row = pl.BlockSpec((1, CH_N), lambda j, kk: (0, j))
    return pl.pallas_call(
        body, name="ssm_wgrads", grid=(SSM_CHUNKS, nk),
        in_specs=[chan, chan, state, state, state, state, last, last],
        out_specs=[pl.BlockSpec((None, CH_W, 2 * CH_N), lambda j, kk: (j, 0, 0)),
                   pl.BlockSpec((None, 2 * CH_N, CH_W), lambda j, kk: (j, 0, 0)), row, row],
        out_shape=[jax.ShapeDtypeStruct((SSM_CHUNKS, CH_W, 2 * CH_N), F32),
                   jax.ShapeDtypeStruct((SSM_CHUNKS, 2 * CH_N, CH_W), F32),
                   jax.ShapeDtypeStruct((1, N_STATE), F32), jax.ShapeDtypeStruct((1, N_STATE), F32)],
        scratch_shapes=[pltpu.VMEM((N_SEG, CH_N), F32)] * 2,
        compiler_params=_cparams(2))(u, dy, g_re, g_im, h_re, h_im, h_re, h_im)


SCAN_LB = 256


def _split_by_scan_block(mat, axis):
    halves = []
    for l in range(CH_N // SCAN_LB):
        re = lax.slice_in_dim(mat, l * SCAN_LB, (l + 1) * SCAN_LB, axis=axis)
        im = lax.slice_in_dim(mat, CH_N + l * SCAN_LB, CH_N + (l + 1) * SCAN_LB, axis=axis)
        halves.append(jnp.concatenate([re, im], axis=axis))
    return jnp.stack(halves, axis=1).reshape((-1,) + halves[0].shape[1:])


def _ssm_scan(name, chan, expand12, contract12, a_re, a_im, d_row, reverse, carry=None):
    s = chan.shape[0]
    seg_len = s // N_SEG
    n_sq = int(math.log2(seg_len))
    assert 2 ** n_sq == seg_len
    rb = min(512, s)
    per_chunk = CH_N // SCAN_LB

    def body(are_ref, aim_ref, ch_ref, e_ref, k_ref, d_ref, hre_ref, him_ref, o_ref, wre_ref, wim_ref, ere, eim, cre, cim):
        e_mat, k_mat = e_ref[...], k_ref[...]
        for r in range(s // rb):
            rows = slice(r * rb, (r + 1) * rb)
            w = _dot(ch_ref[rows, :], e_mat, "nt" if reverse else "nn")
            wre_ref[rows, :] = w[:, :SCAN_LB]
            wim_ref[rows, :] = w[:, SCAN_LB:]

        ar1 = are_ref[...]
        ai1 = -aim_ref[...] if reverse else aim_ref[...]
        ar = jnp.broadcast_to(ar1, (N_SEG, SCAN_LB))
        ai = jnp.broadcast_to(ai1, (N_SEG, SCAN_LB))

        def rows_of(k):
            kk = seg_len - 1 - k if reverse else k
            return pl.ds(pl.multiple_of(kk * N_SEG, N_SEG), N_SEG)

        def local(k, carry):
            hr, hi = carry
            rows = rows_of(k)
            nr = ar * hr - ai * hi + wre_ref[rows, :]
            ni = ar * hi + ai * hr + wim_ref[rows, :]
            hre_ref[rows, :] = nr
            him_ref[rows, :] = ni
            return nr, ni

        zero = jnp.zeros((N_SEG, SCAN_LB), F32)
        er, ei = lax.fori_loop(0, seg_len, local, (zero, zero))
        ere[...] = er
        eim[...] = ei
        pr, pi = ar1, ai1
        for _ in range(n_sq):
            pr, pi = pr * pr - pi * pi, 2.0 * pr * pi
        cr = jnp.zeros((1, SCAN_LB), F32)
        ci = jnp.zeros((1, SCAN_LB), F32)
        for jj in range(N_SEG):
            j = N_SEG - 1 - jj if reverse else jj
            cre[j:j + 1, :] = cr
            cim[j:j + 1, :] = ci
            er_j, ei_j = ere[j:j + 1, :], eim[j:j + 1, :]
            cr, ci = pr * cr - pi * ci + er_j, pr * ci + pi * cr + ei_j
        c_r, c_i = cre[...], cim[...]

        def fix(k, carry):
            qr, qi = carry
            rows = rows_of(k)
            hre_ref[rows, :] = hre_ref[rows, :] + (qr * c_r - qi * c_i)
            him_ref[rows, :] = him_ref[rows, :] + (qr * c_i + qi * c_r)
            return qr * ar - qi * ai, qr * ai + qi * ar

        lax.fori_loop(0, seg_len, fix, (ar, ai))

        first_of_chunk = lax.rem(pl.program_id(0), per_chunk) == 0
        for r in range(s // rb):
            rows = slice(r * rb, (r + 1) * rb)
            h_cat = jnp.concatenate([hre_ref[rows, :], him_ref[rows, :]], axis=1)
            part = _dot(h_cat, k_mat, "nt" if reverse else "nn")

            @pl.when(first_of_chunk)
            def _(rows=rows, part=part):
                o_ref[rows, :] = part + d_ref[...] * ch_ref[rows, :]

            @pl.when(jnp.logical_not(first_of_chunk))
            def _(rows=rows, part=part):
                o_ref[rows, :] += part

    nblk = N_STATE // SCAN_LB
    blk = pl.BlockSpec((s, SCAN_LB), lambda b: (0, b))
    row = pl.BlockSpec((1, SCAN_LB), lambda b: (0, b))
    chan_blk = pl.BlockSpec((s, CH_W), lambda b: (0, b // per_chunk))
    res = _call(name, body, (nblk,),
                [row, row, chan_blk, pl.BlockSpec((None,) + expand12.shape[1:], lambda b: (b, 0, 0)),
                 pl.BlockSpec((None,) + contract12.shape[1:], lambda b: (b, 0, 0)),
                 pl.BlockSpec((1, CH_W), lambda b: (0, b // per_chunk))],
                [blk, blk, chan_blk],
                [jax.ShapeDtypeStruct((s, N_STATE), F32)] * 2 + [jax.ShapeDtypeStruct((s, SSM_WIDTH), F32)],
                (a_re, a_im, chan, expand12, contract12, d_row),
                scratch_shapes=[pltpu.VMEM((s, SCAN_LB), F32)] * 2 + [pltpu.VMEM((N_SEG, SCAN_LB), F32)] * 4, carry=carry)
    return res[0], res[1], res[2]


def _disc(ldt, are, aim, bre, bim):
    dt = jnp.exp(ldt)
    mag = jnp.exp(are * dt)
    abr = mag * jnp.cos(aim * dt)
    abi = mag * jnp.sin(aim * dt)
    den = jnp.square(are) + jnp.square(aim)
    nr = abr - 1.0
    fre = (nr * are + abi * aim) / den
    fim = (abi * are - nr * aim) / den
    return abr, abi, fre * bre - fim * bim, fre * bim + fim * bre


def _ssm_disc_fwd(ldt, are, aim, bre, bim):
    def body(l_ref, ar_ref, ai_ref, br_ref, bi_ref, o0, o1, o2, o3):
        res = _disc(l_ref[...], ar_ref[...], ai_ref[...], br_ref[...], bi_ref[...])
        for ref, val in zip((o0, o1, o2, o3), res):
            ref[...] = val

    col = jax.ShapeDtypeStruct((N_STATE, 1), F32)
    mat = jax.ShapeDtypeStruct((N_STATE, SSM_GROUP), F32)
    return pl.pallas_call(body, name="ssm_disc_fwd", out_shape=[col, col, mat, mat],
                          in_specs=[VMEM_SPEC] * 5, out_specs=[VMEM_SPEC] * 4)(ldt, are, aim, bre, bim)


def _ssm_disc_bwd(ldt, are, aim, bre, bim, d_abr, d_abi, d_bbr, d_bbi):
    def body(l_ref, ar_ref, ai_ref, br_ref, bi_ref, c0, c1, c2, c3, g_ldt, g_are, g_aim, g_bre, g_bim):
        _, vjp = jax.vjp(_disc, l_ref[...], ar_ref[...], ai_ref[...], br_ref[...], bi_ref[...])
        dl, dar, dai, dbr, dbi = vjp((c0[...], c1[...], c2[...], c3[...]))
        state = lax.broadcasted_iota(jnp.int32, (N_STATE, SSM_GROUPS), 0)
        group = lax.broadcasted_iota(jnp.int32, (N_STATE, SSM_GROUPS), 1)
        pick = jnp.right_shift(state, 6) == group
        g_ldt[...] = jnp.sum(jnp.where(pick, dl, 0.0), axis=0, keepdims=True)
        g_are[...] = dar
        g_aim[...] = dai
        g_bre[...] = dbr
        g_bim[...] = dbi

    col = jax.ShapeDtypeStruct((N_STATE, 1), F32)
    mat = jax.ShapeDtypeStruct((N_STATE, SSM_GROUP), F32)
    return pl.pallas_call(body, name="ssm_disc_bwd",
                          out_shape=[jax.ShapeDtypeStruct((1, SSM_GROUPS), F32), col, col, mat, mat],
                          in_specs=[VMEM_SPEC] * 9, out_specs=[VMEM_SPEC] * 5,
                          compiler_params=pltpu.CompilerParams(vmem_limit_bytes=VMEM_LIMIT))(
        ldt, are, aim, bre, bim, d_abr, d_abi, d_bbr, d_bbi)


_EYE8 = np.eye(8, dtype=np.float32)


def _blockdiag_b(bb):
    t = bb.reshape(SSM_CHUNKS, 8, SSM_STATE, SSM_GROUP).transpose(0, 1, 3, 2)
    return jnp.einsum("igcn,gh->igchn", t, _EYE8).reshape(SSM_CHUNKS, CH_W, CH_N)


def _diag_of_b(m):
    t = jnp.einsum("igchn,gh->igcn", m.reshape(SSM_CHUNKS, 8, SSM_GROUP, 8, SSM_STATE), _EYE8)
    return t.transpose(0, 1, 3, 2).reshape(N_STATE, SSM_GROUP)


def _blockdiag_c(c):
    t = c.reshape(SSM_CHUNKS, 8, SSM_GROUP, SSM_STATE).transpose(0, 1, 3, 2)
    return jnp.einsum("ignc,gh->ignhc", t, _EYE8).reshape(SSM_CHUNKS, CH_N, CH_W)


def _diag_of_c(m):
    t = jnp.einsum("ignhc,gh->ignc", m.reshape(SSM_CHUNKS, 8, SSM_STATE, 8, SSM_GROUP), _EYE8)
    return t.transpose(0, 1, 3, 2).reshape(SSM_GROUPS, SSM_GROUP, SSM_STATE)


def _time_perm(a):
    s, c = a.shape
    return a.reshape(N_SEG, s // N_SEG, c).transpose(1, 0, 2).reshape(s, c)


def _time_unperm(a):
    s, c = a.shape
    return a.reshape(s // N_SEG, N_SEG, c).transpose(1, 0, 2).reshape(s, c)


def _dilate(a, d):
    s, c = a.shape
    return a if d == 1 else a.reshape(s // d, d, c).transpose(1, 0, 2).reshape(s, c)


def _undilate(a, d):
    s, c = a.shape
    return a if d == 1 else a.reshape(d, s // d, c).transpose(1, 0, 2).reshape(s, c)


def _dilate_rows(a, d):
    r, s = a.shape
    return a if d == 1 else a.reshape(r, s // d, d).transpose(0, 2, 1).reshape(r, s)


ATT_T = 4
ATT_ROWS = ATT_T * ATT_BLK


def _window(prev_ref, cur_ref, i, sl):
    if i == 0:
        return jnp.concatenate([prev_ref[:, sl], cur_ref[0:ATT_BLK, sl]], axis=0)
    return cur_ref[(i - 1) * ATT_BLK:(i + 1) * ATT_BLK, sl]


def _band_valid(first_key):
    qi = lax.broadcasted_iota(jnp.int32, (ATT_BLK, 2 * ATT_BLK), 0)
    ki = lax.broadcasted_iota(jnp.int32, (ATT_BLK, 2 * ATT_BLK), 1)
    steps = qi + ATT_BLK - ki
    return (steps >= 0) & (steps <= ATT_BLK) & (ki >= first_key)


ATT_STATW = ATT_HPG * 128


def _stat(h):
    return slice(h * 128, (h + 1) * 128)


def _stat_rows(stat):
    n = stat.shape[0]
    heads = [stat[:, _stat(h)].T[0:1, :] for h in range(ATT_HPG)]
    return jnp.concatenate(heads + [jnp.zeros((8 - ATT_HPG, n), stat.dtype)], axis=0)


def _attn_specs(nb, width=ATT_GROUPW):
    cur = pl.BlockSpec((ATT_ROWS, width), lambda b: (b, 0))
    prev = pl.BlockSpec((ATT_BLK, width), lambda b: (jnp.maximum(b * ATT_T - 1, 0), 0))
    nxt = pl.BlockSpec((ATT_BLK, width), lambda b: (jnp.minimum((b + 1) * ATT_T, nb - 1), 0))
    return cur, prev, nxt


def _attn_fwd(tag, per_seq, q, k, v):
    s = q.shape[0]
    nb = s // ATT_BLK

    def body(q_ref, kc_ref, kp_ref, vc_ref, vp_ref, o_ref, lse_ref):
        bt = pl.program_id(0)
        for i in range(ATT_T):
            has_prev = lax.rem(bt * ATT_T + i, per_seq) > 0
            valid = _band_valid(jnp.where(has_prev, 0, ATT_BLK))
            rows = slice(i * ATT_BLK, (i + 1) * ATT_BLK)
            for h in range(ATT_HPG):
                sl = slice(h * ATT_HEAD_DIM, (h + 1) * ATT_HEAD_DIM)
                kcat = _window(kp_ref, kc_ref, i, sl)
                vcat = _window(vp_ref, vc_ref, i, sl)
                sc = _dot(q_ref[rows, sl], kcat, "nt") * ATT_SCALE
                sc = jnp.where(valid, sc, NEG_INF)
                m = jnp.max(sc, axis=-1, keepdims=True)
                p = jnp.exp(sc - m)
                den = jnp.sum(p, axis=-1, keepdims=True)
                o_ref[rows, sl] = _dot(p, vcat, "nn") / den
                lse_ref[rows, _stat(h)] = jnp.broadcast_to(m + jnp.log(den), (ATT_BLK, 128))

    cur, prev, _ = _attn_specs(nb)
    stat, _, _ = _attn_specs(nb, ATT_STATW)
    return pl.pallas_call(
        body, name="attn_fwd_" + tag, grid=(nb // ATT_T,), in_specs=[cur, cur, prev, cur, prev], out_specs=[cur, stat],
        out_shape=[jax.ShapeDtypeStruct((s, ATT_GROUPW), F32), jax.ShapeDtypeStruct((s, ATT_STATW), F32)],
        compiler_params=_cparams(1))(q, k, k, v, v)


def _attn_dq(tag, per_seq, q, k, v, do, lse, delta):
    s = q.shape[0]
    nb = s // ATT_BLK

    def body(q_ref, kc_ref, kp_ref, vc_ref, vp_ref, do_ref, lse_ref, dl_ref, dq_ref):
        bt = pl.program_id(0)
        for i in range(ATT_T):
            has_prev = lax.rem(bt * ATT_T + i, per_seq) > 0
            valid = _band_valid(jnp.where(has_prev, 0, ATT_BLK))
            rows = slice(i * ATT_BLK, (i + 1) * ATT_BLK)
            for h in range(ATT_HPG):
                sl = slice(h * ATT_HEAD_DIM, (h + 1) * ATT_HEAD_DIM)
                kcat = _window(kp_ref, kc_ref, i, sl)
                vcat = _window(vp_ref, vc_ref, i, sl)
                lse = jnp.concatenate([lse_ref[rows, _stat(h)]] * 2, axis=1)
                dlt = jnp.concatenate([dl_ref[rows, _stat(h)]] * 2, axis=1)
                sc = _dot(q_ref[rows, sl], kcat, "nt") * ATT_SCALE
                p = jnp.exp(jnp.where(valid, sc, NEG_INF) - lse)
                dp = _dot(do_ref[rows, sl], vcat, "nt")
                ds = p * (dp - dlt) * ATT_SCALE
                dq_ref[rows, sl] = _dot(ds, kcat, "nn")

    cur, prev, _ = _attn_specs(nb)
    stat, _, _ = _attn_specs(nb, ATT_STATW)
    return pl.pallas_call(
        body, name="attn_dq_" + tag, grid=(nb // ATT_T,), in_specs=[cur, cur, prev, cur, prev, cur, stat, stat],
        out_specs=cur, out_shape=jax.ShapeDtypeStruct((s, ATT_GROUPW), F32),
        compiler_params=_cparams(1))(q, k, k, v, v, do, lse, delta)


def _attn_dkv(tag, per_seq, q, k, v, do, lse_t, delta_t):
    s = q.shape[0]
    nb = s // ATT_BLK

    def body(k_ref, v_ref, qc_ref, qn_ref, doc_ref, don_ref, lc_ref, ln_ref, dc_ref, dn_ref, dk_ref, dv_ref):
        bt = pl.program_id(0)
        ki = lax.broadcasted_iota(jnp.int32, (ATT_BLK, 2 * ATT_BLK), 0)
        ci = lax.broadcasted_iota(jnp.int32, (ATT_BLK, 2 * ATT_BLK), 1)

        def pair(edge_ref, cur_ref, i, sl):
            if i == ATT_T - 1:
                return jnp.concatenate([cur_ref[i * ATT_BLK:(i + 1) * ATT_BLK, sl], edge_ref[:, sl]], axis=0)
            return cur_ref[i * ATT_BLK:(i + 2) * ATT_BLK, sl]

        def pair_row(edge_ref, cur_ref, i, h):
            if i == ATT_T - 1:
                row = jnp.concatenate([cur_ref[h:h + 1, i * ATT_BLK:(i + 1) * ATT_BLK], edge_ref[h:h + 1, :]], axis=1)
            else:
                row = cur_ref[h:h + 1, i * ATT_BLK:(i + 2) * ATT_BLK]
            return jnp.broadcast_to(row, (ATT_BLK, 2 * ATT_BLK))

        for i in range(ATT_T):
            b = bt * ATT_T + i
            next_uses = (b + 1 < nb) & (lax.rem(b + 1, per_seq) > 0)
            reach = jnp.where(next_uses, 0, 4 * ATT_BLK)
            valid = ((ci < ATT_BLK) & (ci >= ki)) | ((ci >= ATT_BLK) & (ki - ci + ATT_BLK >= reach))
            rows = slice(i * ATT_BLK, (i + 1) * ATT_BLK)
            for h in range(ATT_HPG):
                sl = slice(h * ATT_HEAD_DIM, (h + 1) * ATT_HEAD_DIM)
                qcat, docat = pair(qn_ref, qc_ref, i, sl), pair(don_ref, doc_ref, i, sl)
                sc = _dot(k_ref[rows, sl], qcat, "nt") * ATT_SCALE
                p = jnp.exp(jnp.where(valid, sc, NEG_INF) - pair_row(ln_ref, lc_ref, i, h))
                dv_ref[rows, sl] = _dot(p, docat, "nn")
                dp = _dot(v_ref[rows, sl], docat, "nt")
                ds = p * (dp - pair_row(dn_ref, dc_ref, i, h)) * ATT_SCALE
                dk_ref[rows, sl] = _dot(ds, qcat, "nn")

    cur, _, nxt = _attn_specs(nb)
    stat = pl.BlockSpec((8, ATT_ROWS), lambda b: (0, b))
    snxt = pl.BlockSpec((8, ATT_BLK), lambda b: (0, jnp.minimum((b + 1) * ATT_T, nb - 1)))
    return pl.pallas_call(
        body, name="attn_dkv_" + tag, grid=(nb // ATT_T,), in_specs=[cur, cur, cur, nxt, cur, nxt, stat, snxt, stat, snxt],
        out_specs=[cur, cur], out_shape=[jax.ShapeDtypeStruct((s, ATT_GROUPW), F32)] * 2,
        compiler_params=_cparams(1))(k, v, q, q, do, do, lse_t, lse_t, delta_t, delta_t)


def _xattn_probs(q, kh):
    sc = _dot(q, kh, "nt") * XATT_SCALE
    e = jnp.exp(sc - jnp.max(sc, axis=-1, keepdims=True))
    return e / jnp.sum(e, axis=-1, keepdims=True)


def _xattn_fwd(q, kv, tm=512):
    s = q.shape[0]
    tm = min(tm, s)

    def body(q_ref, kv_ref, o_ref):
        for h in range(XATT_HEADS):
            sl = slice(h * XATT_HEAD_DIM, (h + 1) * XATT_HEAD_DIM)
            vs = slice(D_MODEL + h * XATT_HEAD_DIM, D_MODEL + (h + 1) * XATT_HEAD_DIM)
            p = _xattn_probs(q_ref[:, sl], kv_ref[:, sl])
            o_ref[:, sl] = _dot(p, kv_ref[:, vs], "nn").astype(o_ref.dtype)

    return pl.pallas_call(
        body, name="xattn_fwd", grid=(s // tm,),
        in_specs=[pl.BlockSpec((tm, D_MODEL), lambda i: (i, 0)), pl.BlockSpec(kv.shape, lambda i: (0, 0))],
        out_specs=pl.BlockSpec((tm, D_MODEL), lambda i: (i, 0)),
        out_shape=jax.ShapeDtypeStruct((s, D_MODEL), MXU_DTYPE), compiler_params=_cparams(1))(q, kv)


def _xattn_bwd(q, kv, do, tm=512):
    s = q.shape[0]
    tm = min(tm, s)

    def body(q_ref, kv_ref, do_ref, dq_ref, dkv_ref):
        first = pl.program_id(0) == 0

        @pl.when(first)
        def _():
            dkv_ref[...] = jnp.zeros_like(dkv_ref)

        for h in range(XATT_HEADS):
            sl = slice(h * XATT_HEAD_DIM, (h + 1) * XATT_HEAD_DIM)
            vs = slice(D_MODEL + h * XATT_HEAD_DIM, D_MODEL + (h + 1) * XATT_HEAD_DIM)
            p = _xattn_probs(q_ref[:, sl], kv_ref[:, sl])
            dkv_ref[:, vs] += _dot(p, do_ref[:, sl], "tn")
            dp = _dot(do_ref[:, sl], kv_ref[:, vs], "nt")
            ds = p * (dp - jnp.sum(dp * p, axis=-1, keepdims=True)) * XATT_SCALE
            dq_ref[:, sl] = _dot(ds, kv_ref[:, sl], "nn").astype(dq_ref.dtype)
            dkv_ref[:, sl] += _dot(ds, q_ref[:, sl], "tn")

    row = pl.BlockSpec((tm, D_MODEL), lambda i: (i, 0))
    whole = pl.BlockSpec(kv.shape, lambda i: (0, 0))
    return pl.pallas_call(
        body, name="xattn_bwd", grid=(s // tm,), in_specs=[row, whole, row], out_specs=[row, whole],
        out_shape=[jax.ShapeDtypeStruct((s, D_MODEL), MXU_DTYPE), jax.ShapeDtypeStruct(kv.shape, F32)],
        compiler_params=_cparams(1))(q, kv, do)


def _ln(x, g, b):
    mu = jnp.mean(x, axis=-1, keepdims=True)
    xc = x - mu
    var = jnp.mean(jnp.square(xc), axis=-1, keepdims=True)
    return xc * lax.rsqrt(var + LN_EPS) * g + b


def _res_ln(h, o, g, b):
    return _ln(DEEPNORM_ALPHA * h + o, g, b)


def _gate(gs, ga, z1, z2, batt):
    return jax.nn.sigmoid(gs) * (z1 * jax.nn.sigmoid(z2)) + jax.nn.sigmoid(ga) * batt


ROPE_TW = 2 * ATT_HEAD_DIM


def _rope_tables(pos, invf, m1, m2):
    ang = pos.astype(F32) * invf
    sin = jnp.sin(ang)
    return jnp.cos(ang), -sin * m1, sin * m2


def _widen(tab):
    return jnp.concatenate([tab] * (ATT_GROUPW // ROPE_TW), axis=1)


def _rope(t, cos, s_up, s_dn):
    w = t.shape[-1]
    return t * cos + pltpu.roll(t, w - ROT_DIM // 2, 1) * s_up + pltpu.roll(t, ROT_DIM // 2, 1) * s_dn


def _rope_t(dt, cos, s_up, s_dn):
    w = dt.shape[-1]
    return dt * cos + pltpu.roll(dt * s_up, ROT_DIM // 2, 1) + pltpu.roll(dt * s_dn, w - ROT_DIM // 2, 1)


def _rope_consts():
    inv_freq = ROPE_THETA ** (-jnp.arange(0, ROT_DIM, 2, dtype=F32) / ROT_DIM)
    d = np.arange(ROPE_TW) % ATT_HEAD_DIM
    invf = jnp.where(d < ROT_DIM, inv_freq[d % (ROT_DIM // 2)], 0.0).reshape(1, ROPE_TW).astype(F32)
    m1 = jnp.asarray((d < ROT_DIM // 2).astype(np.float32)).reshape(1, ROPE_TW)
    m2 = jnp.asarray(((d >= ROT_DIM // 2) & (d < ROT_DIM)).astype(np.float32)).reshape(1, ROPE_TW)
    return invf, m1, m2


def _head_sum_matrix():
    d = np.arange(ATT_GROUPW) // ATT_HEAD_DIM
    s = np.arange(ATT_STATW) // 128
    return jnp.asarray((d[:, None] == s[None, :]).astype(np.float32))


def _adamw(w, g, m, v):
    m = ADAM_B1 * m + (1.0 - ADAM_B1) * g
    v = ADAM_B2 * v + (1.0 - ADAM_B2) * jnp.square(g)
    m_hat = m / (1.0 - ADAM_B1 ** ADAM_STEP)
    v_hat = v / (1.0 - ADAM_B2 ** ADAM_STEP)
    delta = -ADAM_LR * (m_hat / (jnp.sqrt(v_hat) + ADAM_EPS) + ADAM_WD * w)
    return delta, m, v


def _local_step(x, mem, pos, target, sp, ex):
    s = x.shape[0]
    al = DEEPNORM_ALPHA
    mx = MXU_DTYPE

    h0, h0b = _rowwise("ln_in", lambda x, g, b: (lambda h: (h, h))(_ln(x, g, b)), [x],
                       [sp["ln_in_g"], sp["ln_in_b"]], [(D_MODEL, F32), (D_MODEL, mx)],
                       carry=ex.gather_carry(["w_in"]))
    proj = _mm("proj", h0b, ex.weight("w_in"), "nn", bias=sp["b_in"],
               carry=ex.gather_carry(["w_glu", "w_att_up", "w_mix_out", "w_xq", "w_xkv"]))

    ldt = jnp.repeat(sp["ssm_log_dt"].reshape(SSM_GROUPS), SSM_STATE).reshape(N_STATE, 1)
    are, aim = sp["ssm_a_re"].reshape(N_STATE, 1), sp["ssm_a_im"].reshape(N_STATE, 1)
    bre, bim = sp["ssm_b_re"].reshape(N_STATE, SSM_GROUP), sp["ssm_b_im"].reshape(N_STATE, SSM_GROUP)
    abr, abi, bbr, bbi = _ssm_disc_fwd(ldt, are, aim, bre, bim)
    a_re, a_im = abr.reshape(1, N_STATE), abi.reshape(1, N_STATE)
    bexp = jnp.concatenate([_blockdiag_b(bbr), _blockdiag_b(bbi)], axis=2).astype(mx)
    cexp = jnp.concatenate([_blockdiag_c(sp["ssm_c_re"].reshape(SSM_GROUPS, SSM_GROUP, SSM_STATE)),
                            -_blockdiag_c(sp["ssm_c_im"].reshape(SSM_GROUPS, SSM_GROUP, SSM_STATE))],
                           axis=1).astype(mx)
    u_p = _time_perm(proj[:, :SSM_WIDTH])
    b12, c12 = _split_by_scan_block(bexp, 2), _split_by_scan_block(cexp, 1)
    h_re, h_im, y_p = _ssm_scan("ssm_scan_fwd", u_p, b12, c12, a_re, a_im, sp["ssm_d"], reverse=False,
                                carry=ex.gather_carry(["w_ff1", "w_ff2"]))
    y = _time_unperm(y_p)
    ygb, = _rowwise("gelu", lambda y: jax.nn.gelu(y), [y], [], [(SSM_WIDTH, mx)])
    z = _mm("glu", ygb, ex.weight("w_glu"), "nn", bias=sp["b_glu"], carry=ex.gather_carry(["w_xo"]))

    invf, m1, m2 = _rope_consts()

    def rope_fwd(pos, q0, q1, q2, k0, k1, k2, v0, v1, v2, invf, m1, m2):
        narrow = _rope_tables(pos, invf, m1, m2)
        tabs = [_widen(t) for t in narrow]
        return tuple(_rope(t, *tabs) for t in (q0, q1, q2, k0, k1, k2)) + (v0, v1, v2) + tuple(narrow)

    qkv_cols = [(proj, ATT_GROUPW, 3 + i) for i in range(9)]
    qkv = _rowwise("rope", rope_fwd, [pos] + qkv_cols, [invf, m1, m2], [(ATT_GROUPW, mx)] * 9 + [(ROPE_TW, F32)] * 3)
    rope_tabs = qkv[9:]
    n_blocks = s // ATT_BLK
    groups = [(str(g), n_blocks // d, d) for g, d in enumerate(DILATIONS)]
    q_d = [_dilate(qkv[g], d) for g, d in enumerate(DILATIONS)]
    k_d = [_dilate(qkv[3 + g], d) for g, d in enumerate(DILATIONS)]
    v_d = [_dilate(qkv[6 + g], d) for g, d in enumerate(DILATIONS)]
    o_g, l_g = [], []
    for g, (tag, per_seq, d) in enumerate(groups):
        o, lse = _attn_fwd(tag, per_seq, q_d[g], k_d[g], v_d[g])
        o_g.append(_undilate(o, d))
        l_g.append(_undilate(lse, d))

    def merge(o0, o1, o2, l0, l1, l2):
        m = jnp.maximum(jnp.maximum(l0, l1), l2)
        e0, e1, e2 = jnp.exp(l0 - m), jnp.exp(l1 - m), jnp.exp(l2 - m)
        tot = e0 + e1 + e2

        def per_dim(e):
            w = e / tot
            return jnp.concatenate([w[:, h * 128:h * 128 + ATT_HEAD_DIM] for h in range(ATT_HPG)], axis=1)

        att = per_dim(e0) * o0 + per_dim(e1) * o1 + per_dim(e2) * o2
        lse = m + jnp.log(tot)
        return att, att, lse, _stat_rows(lse)

    att, attb, lse_tot, lse_tot_t = _rowwise("attn_merge", merge, o_g + l_g, [],
                                             [(ATT_GROUPW, F32), (ATT_GROUPW, mx), (ATT_STATW, F32)], touts=[(8, F32)])
    batt = _mm("att_up", attb, ex.weight("w_att_up"), "nn")

    gate_rows = [(proj, D_MODEL, 3), (proj, D_MODEL, 4), (z, D_MODEL, 0), (z, D_MODEL, 1), batt]
    mixedb, = _rowwise("gate", _gate, gate_rows, [], [(D_MODEL, mx)])
    o1 = _mm("mix_out", mixedb, ex.weight("w_mix_out"), "nn", bias=sp["b_mix_out"])
    h1, h1b = _rowwise("ln1", lambda h, o, g, b: (lambda r: (r, r))(_res_ln(h, o, g, b)), [h0, o1],
                       [sp["ln1_g"], sp["ln1_b"]], [(D_MODEL, F32), (D_MODEL, mx)])

    qx = _mm("xq", h1b, ex.weight("w_xq"), "nn", out_dtypes=(mx,))
    kvx = _mm("xkv", mem, ex.weight("w_xkv"), "nn", out_dtypes=(mx,))
    oxb = _xattn_fwd(qx, kvx)
    o2 = _mm("xo", oxb, ex.weight("w_xo"), "nn")
    h2, h2b = _rowwise("ln2", lambda h, o, g, b: (lambda r: (r, r))(_res_ln(h, o, g, b)), [h1, o2],
                       [sp["ln2_g"], sp["ln2_b"]], [(D_MODEL, F32), (D_MODEL, mx)])

    a_ff, fb = _mm("ff1", h2b, ex.weight("w_ff1"), "nn", bias=sp["b_ff1"],
                   epilogue=lambda r: (r, jnp.square(jnp.maximum(r, 0.0))), out_dtypes=(F32, mx))
    o3 = _mm("ff2", fb, ex.weight("w_ff2"), "nn", bias=sp["b_ff2"])

    def loss_bwd(h2, o3, tgt, g, b):
        def f(h2, o3, g, b):
            h3 = _res_ln(h2, o3, g, b)
            return 0.5 * jnp.sum(jnp.mean(jnp.square(h3 - tgt), axis=-1))

        loss, vjp = jax.vjp(f, h2, o3, g, b)
        _, dr, dg, db = vjp(jnp.ones((), F32))
        return dr, dr, dg, db, _colsum(dr), jnp.full((1, 128), loss, F32)

    dr3, dr3b, g_ln3_g, g_ln3_b, g_b_ff2, loss = _rowwise(
        "loss_ln3_bwd", loss_bwd, [h2, o3, target], [sp["ln3_g"], sp["ln3_b"]],
        [(D_MODEL, F32), (D_MODEL, mx)], [D_MODEL, D_MODEL, D_MODEL, 128])

    dab = _mm("ff2_dx", dr3b, ex.weight("w_ff2"), "nt", extras=(a_ff,),
              epilogue=lambda r, a: (r * (2.0 * jnp.maximum(a, 0.0)),), out_dtypes=(mx,))
    ex.grad("w_ff2", _mm("ff2_dw", fb, dr3b, "tn"))
    g_b_ff1, = _rowwise("ff1_db", lambda v: (_colsum(v),), [dab], [], [], [D_FF])
    ex.grad("w_ff1", _mm("ff1_dw", h2b, dab, "tn", carry=ex.carry(swap=["w_ff2"])))
    dh2 = _mm("ff1_dx", dab, ex.weight("w_ff1"), "nt", extras=(dr3,), epilogue=lambda r, d: (r + al * d,),
              carry=ex.carry(swap=["w_ff1"], ici=["w_ff2"]))

    def ln_bwd(h, o, dout, g, b):
        _, vjp = jax.vjp(_res_ln, h, o, g, b)
        _, dr, dg, db = vjp(dout)
        return dr, dr, dg, db, _colsum(dr)

    dr2, dr2b, g_ln2_g, g_ln2_b, _ = _rowwise(
        "ln2_bwd", ln_bwd, [h1, o2, dh2], [sp["ln2_g"], sp["ln2_b"]],
        [(D_MODEL, F32), (D_MODEL, mx)], [D_MODEL, D_MODEL, D_MODEL])
    ex.grad("w_xo", _mm("xo_dw", oxb, dr2b, "tn"))
    doxb = _mm("xo_dx", dr2b, ex.weight("w_xo"), "nt", out_dtypes=(mx,), carry=ex.carry(swap=["w_xo"]))
    dqxb, dkvx = _xattn_bwd(qx, kvx, doxb)
    ex.grad("w_xq", _mm("xq_dw", h1b, dqxb, "tn", carry=ex.carry(ici=["w_xo"])))
    dh1 = _mm("xq_dx", dqxb, ex.weight("w_xq"), "nt", extras=(dr2,), epilogue=lambda r, d: (r + al * d,),
              carry=ex.carry(swap=["w_xq"]))
    ex.grad("w_xkv", _mm("xkv_dw", mem, dkvx, "tn"))

    dr1, dr1b, g_ln1_g, g_ln1_b, g_b_mix = _rowwise(
        "ln1_bwd", ln_bwd, [h0, o1, dh1], [sp["ln1_g"], sp["ln1_b"]],
        [(D_MODEL, F32), (D_MODEL, mx)], [D_MODEL, D_MODEL, D_MODEL])
    ex.grad("w_mix_out", _mm("mix_dw", mixedb, dr1b, "tn", carry=ex.carry(swap=["w_xkv"], ici=["w_xq"])))
    dmixed = _mm("mix_dx", dr1b, ex.weight("w_mix_out"), "nt", carry=ex.carry(swap=["w_mix_out"]))

    def gate_bwd(gs, ga, z1, z2, batt, dm):
        _, vjp = jax.vjp(_gate, gs, ga, z1, z2, batt)
        dgs, dga, dz1, dz2, dbatt = vjp(dm)
        dz = jnp.concatenate([dz1, dz2], axis=-1)
        return dgs, dga, dz, dbatt, _colsum(dz)

    dgsb, dgab, dzb, dbattb, g_b_glu = _rowwise(
        "gate_bwd", gate_bwd, gate_rows + [dmixed], [],
        [(D_MODEL, mx), (D_MODEL, mx), (2 * D_MODEL, mx), (D_MODEL, mx)], [2 * D_MODEL])
    ex.grad("w_att_up", _mm("att_up_dw", attb, dbattb, "tn", carry=ex.carry(ici=["w_mix_out"])))
    datt = _mm("att_up_dx", dbattb, ex.weight("w_att_up"), "nt", carry=ex.carry(swap=["w_att_up"]))

    def att_delta(datt, att, hs):
        dl = jnp.dot(datt * att, hs, precision=lax.Precision.HIGHEST, preferred_element_type=F32)
        return datt, dl, _stat_rows(dl)

    dattb, delta, delta_t = _rowwise("attn_delta", att_delta, [datt, att], [_head_sum_matrix()],
                                     [(ATT_GROUPW, mx), (ATT_STATW, F32)], touts=[(8, F32)])
    dq_g, dk_g, dv_g = [], [], []
    for g, (tag, per_seq, d) in enumerate(groups):
        do_d, lt_d, dl_d = _dilate(dattb, d), _dilate(lse_tot, d), _dilate(delta, d)
        dq_g.append(_undilate(_attn_dq(tag, per_seq, q_d[g], k_d[g], v_d[g], do_d, lt_d, dl_d), d))
        dk, dv = _attn_dkv(tag, per_seq, q_d[g], k_d[g], v_d[g], do_d, _dilate_rows(lse_tot_t, d), _dilate_rows(delta_t, d))
        dk_g.append(_undilate(dk, d))
        dv_g.append(_undilate(dv, d))
    dqkv = dq_g + dk_g + dv_g

    def rope_bwd(q0, q1, q2, k0, k1, k2, v0, v1, v2, cos, s_up, s_dn):
        tabs = [_widen(t) for t in (cos, s_up, s_dn)]
        return jnp.concatenate([_rope_t(t, *tabs) for t in (q0, q1, q2, k0, k1, k2)] + [v0, v1, v2], axis=-1)

    dqkvb, = _rowwise("rope_bwd", rope_bwd, dqkv + list(rope_tabs), [], [(9 * ATT_GROUPW, mx)])

    ex.grad("w_glu", _mm("glu_dw", ygb, dzb, "tn", carry=ex.carry(ici=["w_xkv", "w_att_up"])))
    dyg = _mm("glu_dx", dzb, ex.weight("w_glu"), "nt", carry=ex.carry(swap=["w_glu"]))

    def gelu_bwd(y, dyg):
        _, vjp = jax.vjp(jax.nn.gelu, y)
        return vjp(dyg)[0]

    dy, = _rowwise("gelu_bwd", gelu_bwd, [y, dyg], [], [(SSM_WIDTH, F32)])
    dy_p = _time_perm(dy)
    s_re, s_im, du_p = _ssm_scan("ssm_scan_bwd", dy_p, c12, b12, a_re, a_im, sp["ssm_d"], reverse=True,
                                 carry=ex.carry(ici=["w_ff1", "w_glu"]))
    g_bexp, g_cexp, d_abr, d_abi = _ssm_wgrads(u_p, dy_p, s_re, s_im, h_re, h_im)
    g_ssm_d, = _rowwise("ssm_dd", lambda a, b: (_colsum(a * b),), [dy_p, u_p], [], [], [SSM_WIDTH])
    g_ldt, g_are, g_aim, g_bre, g_bim = _ssm_disc_bwd(
        ldt, are, aim, bre, bim, d_abr.reshape(N_STATE, 1), d_abi.reshape(N_STATE, 1),
        _diag_of_b(g_bexp[:, :, :CH_N]), _diag_of_b(g_bexp[:, :, CH_N:]))
    g_c_re = _diag_of_c(g_cexp[:, :CH_N, :])
    g_c_im = -_diag_of_c(g_cexp[:, CH_N:, :])

    def assemble(du, dqkv, dgs, dga):
        row = jnp.concatenate([du.astype(mx), dqkv, dgs, dga], axis=-1)
        return row, _colsum(row)

    dprojb, g_b_in = _rowwise("in_assemble", assemble, [_time_unperm(du_p), dqkvb, dgsb, dgab], [],
                              [(IN_COLS, mx)], [IN_COLS])
    ex.grad("w_in", _mm("in_dw", h0b, dprojb, "tn"))
    dh0 = _mm("in_dx", dprojb, ex.weight("w_in"), "nt", extras=(dr1,), epilogue=lambda r, d: (r + al * d,),
              carry=ex.carry(ici=["w_in"]))

    def ln_in_bwd(x, dout, g, b):
        _, vjp = jax.vjp(_ln, x, g, b)
        return vjp(dout)

    dx, g_ln_in_g, g_ln_in_b = _rowwise("ln_in_bwd", ln_in_bwd, [x, dh0], [sp["ln_in_g"], sp["ln_in_b"]],
                                        [(D_MODEL, F32)], [D_MODEL, D_MODEL])

    small = {"ln_in_g": g_ln_in_g, "ln_in_b": g_ln_in_b, "b_in": g_b_in, "ssm_log_dt": g_ldt, "ssm_a_re": g_are,
             "ssm_a_im": g_aim, "ssm_b_re": g_bre, "ssm_b_im": g_bim, "ssm_c_re": g_c_re, "ssm_c_im": g_c_im,
             "ssm_d": g_ssm_d, "b_glu": g_b_glu, "b_mix_out": g_b_mix, "ln1_g": g_ln1_g, "ln1_b": g_ln1_b,
             "ln2_g": g_ln2_g, "ln2_b": g_ln2_b, "b_ff1": g_b_ff1, "b_ff2": g_b_ff2, "ln3_g": g_ln3_g,
             "ln3_b": g_ln3_b}
    return loss, dx, small


def _piece_shape(k, n, axis):
    return (k // 2, n // 4) if axis == 1 else (k // 8, n)


def _aligned(v, m):
    return v if isinstance(v, int) else pl.multiple_of(v, m)


def _full_piece(ref, k, n, axis, chip, half):
    pr, pc = _piece_shape(k, n, axis)
    if axis == 1:
        return ref.at[pl.ds(_aligned(half * pr, 8), pr), pl.ds(_aligned(chip * pc, 128), pc)]
    return ref.at[pl.ds(_aligned(chip * (2 * pr) + half * pr, 8), pr), :]


def _full_shard(ref, k, n, axis, chip):
    if axis == 1:
        return ref.at[:, pl.ds(_aligned(chip * (n // 4), 128), n // 4)]
    return ref.at[pl.ds(_aligned(chip * (k // 4), 8), k // 4), :]


def _shard_piece(ref, k, n, axis, half):
    pr, _ = _piece_shape(k, n, axis)
    return ref.at[pl.ds(_aligned(half * pr, 8), pr), :]


def _mesh_pos():
    x, y, c = lax.axis_index("x"), lax.axis_index("y"), lax.axis_index("c")
    other_chips = [(1 - x, y), (x, 1 - y), (1 - x, 1 - y)]
    return x, y, c, other_chips


def _remote(src, dst, send_sem, recv_sem, dev):
    return pltpu.make_async_remote_copy(src_ref=src, dst_ref=dst, send_sem=send_sem, recv_sem=recv_sem,
                                        device_id=dev, device_id_type=MESH)


def _placed(name, fn, n_steps, where, ins, out_sds, out_block, out_index):
    def body(w_ref, *refs):
        o_ref = refs[-1]
        o_ref[...] = fn(*[r[...] for r in refs[:-1]]).astype(o_ref.dtype)

    grid_spec = pltpu.PrefetchScalarGridSpec(
        num_scalar_prefetch=1, grid=(n_steps,), in_specs=[pl.BlockSpec(bs, idx) for _, bs, idx in ins],
        out_specs=pl.BlockSpec(out_block, out_index))
    return pl.pallas_call(body, name=name, grid_spec=grid_spec, out_shape=out_sds,
                          compiler_params=_cparams(1))(where, *[a for a, _, _ in ins])


def _gather_copies(widx):
    geo = [BIG[i][1:] for i in widx]

    def ici(full, wi, j, chip, send_sems, recv_sems, c, dev):
        k, n, ax = geo[wi]
        piece = _full_piece(full[wi], k, n, ax, chip, c)
        return _remote(piece, piece, send_sems.at[wi * 6 + j], recv_sems.at[wi * 6 + j], dev)

    def d2d(full, wi, j, chip, half, send_sems, recv_sems, sib):
        k, n, ax = geo[wi]
        piece = _full_piece(full[wi], k, n, ax, chip, half)
        return _remote(piece, piece, send_sems.at[wi * 6 + 3 + j], recv_sems.at[wi * 6 + 3 + j], sib)

    def start(_, full, send_sems, recv_sems):
        x, y, c, chips = _mesh_pos()
        for wi in range(len(geo)):
            for j, (qx, qy) in enumerate(chips):
                ici(full, wi, j, 2 * x + y, send_sems, recv_sems, c, (qx, qy, c)).start()

    def finish(_, full, send_sems, recv_sems):
        x, y, c, chips = _mesh_pos()
        sib = (x, y, 1 - c)
        for wi in range(len(geo)):
            for j, (qx, qy) in enumerate(chips):
                ici(full, wi, j, 2 * qx + qy, send_sems, recv_sems, c, (qx, qy, c)).wait_recv()
                d2d(full, wi, j, 2 * qx + qy, c, send_sems, recv_sems, sib).start()
        for wi in range(len(geo)):
            for j, (qx, qy) in enumerate(chips):
                d2d(full, wi, j, 2 * qx + qy, 1 - c, send_sems, recv_sems, sib).wait_recv()
        for wi in range(len(geo)):
            for j, (qx, qy) in enumerate(chips):
                ici(full, wi, j, 2 * x + y, send_sems, recv_sems, c, (qx, qy, c)).wait_send()
                d2d(full, wi, j, 2 * qx + qy, c, send_sems, recv_sems, sib).wait_send()

    return start, finish, 6 * len(geo)


def _gather_weights(tag, fulls, widx):
    nw = len(widx)
    start, finish, n_sems = _gather_copies(widx)

    def body(*refs):
        full = refs[nw:2 * nw]
        start(None, full, *refs[2 * nw:])
        finish(None, full, *refs[2 * nw:])

    return pl.pallas_call(
        body, name="gather_weights_" + tag, in_specs=[HBM_SPEC] * nw, out_specs=[HBM_SPEC] * nw,
        out_shape=[jax.ShapeDtypeStruct(f.shape, f.dtype) for f in fulls],
        input_output_aliases={i: i for i in range(nw)},
        scratch_shapes=[pltpu.SemaphoreType.DMA((n_sems,)), pltpu.SemaphoreType.DMA((n_sems,))])(*fulls)


def _swap_copies(widx):
    geo = [BIG[i][1:] for i in widx]

    def copies(g, got, send_sems, recv_sems, base):
        x, y, c, _ = _mesh_pos()
        return [_remote(_full_piece(g[wi], k, n, ax, q, 1 - c), got[wi].at[q], send_sems.at[base + wi * 4 + q],
                        recv_sems.at[base + wi * 4 + q], (x, y, 1 - c))
                for wi, (k, n, ax) in enumerate(geo) for q in range(4)]

    def start(g, got, send_sems, recv_sems, base=0):
        for cp in copies(g, got, send_sems, recv_sems, base):
            cp.start()

    def finish(g, got, send_sems, recv_sems, base=0):
        for cp in copies(g, got, send_sems, recv_sems, base):
            cp.wait()

    return start, finish, 4 * len(geo)


def _swap_shapes(widx):
    return [jax.ShapeDtypeStruct((4,) + _piece_shape(*BIG[i][1:]), F32) for i in widx]


def _reduce_swap_halves(tag, grads, widx):
    nw = len(widx)
    start, finish, n_sems = _swap_copies(widx)

    def body(*refs):
        start(refs[:nw], refs[nw:2 * nw], *refs[2 * nw:])
        finish(refs[:nw], refs[nw:2 * nw], *refs[2 * nw:])

    return pl.pallas_call(
        body, name="reduce_swap_halves_" + tag, in_specs=[HBM_SPEC] * nw, out_specs=[HBM_SPEC] * nw,
        out_shape=_swap_shapes(widx),
        scratch_shapes=[pltpu.SemaphoreType.DMA((n_sems,)), pltpu.SemaphoreType.DMA((n_sems,))])(*grads)


def _owner_copies(nw):
    def copies(p, out, send_sems, recv_sems, base):
        x, y, c, chips = _mesh_pos()
        return [_remote(p[wi].at[2 * qx + qy], out[wi].at[j], send_sems.at[base + wi * 3 + j],
                        recv_sems.at[base + wi * 3 + j], (qx, qy, c))
                for wi in range(nw) for j, (qx, qy) in enumerate(chips)]

    def start(p, out, send_sems, recv_sems, base=0):
        for cp in copies(p, out, send_sems, recv_sems, base):
            cp.start()

    def finish(p, out, send_sems, recv_sems, base=0):
        for cp in copies(p, out, send_sems, recv_sems, base):
            cp.wait()

    return start, finish, 3 * nw


def _join_carries(a, b):
    if a is None or b is None:
        return a if b is None else b
    n_i, n_o = len(a.ins), len(a.outs)
    outs = list(a.outs) + [o + n_i if isinstance(o, int) else o for o in b.outs]

    def start(c_in, c_out, send_sems, recv_sems):
        a.start(c_in[:n_i], c_out[:n_o], send_sems, recv_sems)
        b.start(c_in[n_i:], c_out[n_o:], send_sems, recv_sems, base=a.n_sems)

    def finish(c_in, c_out, send_sems, recv_sems):
        a.finish(c_in[:n_i], c_out[:n_o], send_sems, recv_sems)
        b.finish(c_in[n_i:], c_out[n_o:], send_sems, recv_sems, base=a.n_sems)

    def done(res):
        a.done(res[:n_o])
        b.done(res[n_o:])

    return _Carry(a.ins + b.ins, outs, a.n_sems + b.n_sems, start, finish, done)


def _share_with_sibling(shards):
    nw = len(BIG)

    def body(*refs):
        out = refs[nw:2 * nw]
        send_sems, recv_sems = refs[2 * nw:]
        x, y, c, _ = _mesh_pos()
        sib = (x, y, 1 - c)
        cps = []
        for wi, (_, k, n, ax) in enumerate(BIG):
            mine = _shard_piece(out[wi], k, n, ax, c)
            cp = _remote(mine, mine, send_sems.at[wi], recv_sems.at[wi], sib)
            cp.start()
            cps.append(cp)
        for wi, (_, k, n, ax) in enumerate(BIG):
            piece = _shard_piece(out[wi], k, n, ax, 1 - c)
            _remote(piece, piece, send_sems.at[wi], recv_sems.at[wi], sib).wait_recv()
        for cp in cps:
            cp.wait_send()

    return pl.pallas_call(
        body, name="share_with_sibling", in_specs=[HBM_SPEC] * nw, out_specs=[HBM_SPEC] * nw,
        out_shape=[jax.ShapeDtypeStruct(sh.shape, sh.dtype) for sh in shards],
        input_output_aliases={i: i for i in range(nw)},
        scratch_shapes=[pltpu.SemaphoreType.DMA((nw,)), pltpu.SemaphoreType.DMA((nw,))])(*shards)


def _allreduce_small(v):
    r = v.shape[0]
    rh = r // 2
    assert rh % 8 == 0

    def body(v_ref, o_ref, sib_buf, chip_buf, send_sems, recv_sems):
        x, y, c, chips = _mesh_pos()
        me = 2 * x + y
        sib = (x, y, 1 - c)
        mine = pl.ds(pl.multiple_of(c * rh, 8), rh)
        other = pl.ds(pl.multiple_of((1 - c) * rh, 8), rh)
        swap = _remote(v_ref.at[other], sib_buf, send_sems.at[0], recv_sems.at[0], sib)
        swap.start()
        swap.wait()
        chip_buf[me] = v_ref[mine, :] + sib_buf[...]
        cps = []
        for j, (qx, qy) in enumerate(chips):
            cp = _remote(chip_buf.at[me], chip_buf.at[me], send_sems.at[1 + j], recv_sems.at[1 + j], (qx, qy, c))
            cp.start()
            cps.append(cp)
        for j, (qx, qy) in enumerate(chips):
            slot = chip_buf.at[2 * qx + qy]
            _remote(slot, slot, send_sems.at[1 + j], recv_sems.at[1 + j], (qx, qy, c)).wait_recv()
        for cp in cps:
            cp.wait_send()
        o_ref[mine, :] = ((chip_buf[0] + chip_buf[1]) + chip_buf[2]) + chip_buf[3]
        back = _remote(o_ref.at[mine], o_ref.at[mine], send_sems.at[4], recv_sems.at[4], sib)
        back.start()
        _remote(o_ref.at[other], o_ref.at[other], send_sems.at[4], recv_sems.at[4], sib).wait_recv()
        back.wait_send()

    return pl.pallas_call(
        body, name="allreduce_small", in_specs=[VMEM_SPEC], out_specs=VMEM_SPEC,
        out_shape=jax.ShapeDtypeStruct((r, 128), F32),
        scratch_shapes=[pltpu.VMEM((rh, 128), F32), pltpu.VMEM((4, rh, 128), F32),
                        pltpu.SemaphoreType.DMA((5,)), pltpu.SemaphoreType.DMA((5,))],
        compiler_params=pltpu.CompilerParams(vmem_limit_bytes=VMEM_LIMIT))(v)


def _as2d(a):
    a = a.reshape((-1, a.shape[-1])) if a.ndim > 1 else a.reshape(1, -1)
    return a


def _adamw_small(quads):
    n = len(quads)

    def body(*refs):
        for i in range(n):
            w, g, m, v = (r[...] for r in refs[4 * i:4 * i + 4])
            for ref, val in zip(refs[4 * n + 3 * i:4 * n + 3 * i + 3], _adamw(w, g, m, v)):
                ref[...] = val

    return pl.pallas_call(
        body, name="adamw_small", in_specs=[VMEM_SPEC] * (4 * n), out_specs=[VMEM_SPEC] * (3 * n),
        out_shape=[jax.ShapeDtypeStruct(q[0].shape, F32) for q in quads for _ in range(3)],
        compiler_params=pltpu.CompilerParams(vmem_limit_bytes=VMEM_LIMIT))(*[a for q in quads for a in q])


def _where():
    return jnp.stack([2 * lax.axis_index("x") + lax.axis_index("y"), lax.axis_index("c")]).astype(jnp.int32)


_BIG_INDEX = {name: i for i, (name, _, _, _) in enumerate(BIG)}


class _LocalWeights:
    def __init__(self, weights):
        self.weights, self.grads = weights, {}

    def gather_now(self, names):
        pass

    def gather_carry(self, names):
        return None

    def weight(self, name):
        return self.weights[name]

    def grad(self, name, g):
        self.grads[name] = g

    def carry(self, swap=(), ici=()):
        return None


class _Exchange:
    def __init__(self, inputs, where):
        self.inputs, self.where = inputs, where
        self.full, self.ready = {}, set()
        self.raw, self.got, self.parts, self.landed, self.geom = {}, {}, {}, {}, {}
        for name, k, n, ax in BIG:
            w2 = inputs[name][0]
            rs, cs = w2.shape
            tm = _tile(rs, 512)
            steps = rs // tm
            if ax == 1:
                blk, idx = (tm, cs), lambda i, w: (i, w[0])
            else:
                blk, idx = (tm, n), functools.partial(lambda i, w, steps: (w[0] * steps + i, 0), steps=steps)
            self.full[name] = _placed("cast_" + name, lambda w: w, steps, where, [(w2, (tm, cs), lambda i, w: (i, 0))],
                                      jax.ShapeDtypeStruct((k, n), MXU_DTYPE), blk, idx)

    def _gathered(self, names, outs):
        for name, o in zip(names, outs):
            self.full[name] = o
            self.ready.add(name)

    def gather_now(self, names):
        self._gathered(names, _gather_weights(names[0], [self.full[n] for n in names], [_BIG_INDEX[n] for n in names]))

    def gather_carry(self, names):
        start, finish, n_sems = _gather_copies([_BIG_INDEX[n] for n in names])
        return _Carry([self.full[n] for n in names], list(range(len(names))), n_sems, start, finish,
                      functools.partial(self._gathered, names))

    def weight(self, name):
        assert name in self.ready, name
        return self.full[name]

    def grad(self, name, g):
        self.raw[name] = g

    def _swapped(self, names, outs):
        for name, o in zip(names, outs):
            self.got[name] = o

    def _pair_sum(self, name):
        i = _BIG_INDEX[name]
        _, k, n, ax = BIG[i]
        g = self.raw[name]
        if name not in self.got:
            self._swapped([name], _reduce_swap_halves(name, [g], [i]))
        got = self.got[name]
        pr, pc = _piece_shape(k, n, ax)
        tm = _tile(pr, 512)
        spp = pr // tm
        self.geom[name] = (pr, pc, tm, spp)
        if ax == 1:
            g_idx = functools.partial(lambda i, w, spp: (w[1] * spp + i % spp, i // spp), spp=spp)
        else:
            g_idx = functools.partial(lambda i, w, spp: ((i // spp) * 2 * spp + w[1] * spp + i % spp, 0), spp=spp)
        self.parts[name] = _placed(
            "pair_sum_" + name, lambda a, b: a + b, 4 * spp, self.where,
            [(g, (tm, pc), g_idx), (got.reshape(4 * pr, pc), (tm, pc), lambda i, w: (i, 0))],
            jax.ShapeDtypeStruct((4 * pr, pc), BF16), (tm, pc), lambda i, w: (i, 0)).reshape(4, pr, pc)

    def _landed(self, names, outs):
        for name, o in zip(names, outs):
            self.landed[name] = o

    def carry(self, swap=(), ici=()):
        first = second = None
        if swap:
            widx = [_BIG_INDEX[n] for n in swap]
            start, finish, n_sems = _swap_copies(widx)
            first = _Carry([self.raw[n] for n in swap], _swap_shapes(widx), n_sems, start, finish,
                           functools.partial(self._swapped, list(swap)))
        if ici:
            for n in ici:
                self._pair_sum(n)
            start, finish, n_sems = _owner_copies(len(ici))
            parts = [self.parts[n] for n in ici]
            outs = [jax.ShapeDtypeStruct((3,) + p.shape[1:], p.dtype) for p in parts]
            second = _Carry(parts, outs, n_sems, start, finish, functools.partial(self._landed, list(ici)))
        return _join_carries(first, second)

    def finish(self):
        halves = []
        for name, _, _, _ in BIG:
            pr, pc, tm, spp = self.geom[name]
            ins = [(self.parts[name], (None, tm, pc), lambda i, w: (w[0], i, 0))]
            ins += [(self.landed[name], (None, tm, pc), functools.partial(lambda i, w, j: (j, i, 0), j=j))
                    for j in range(3)]
            halves.append(_placed("chip_sum_" + name,
                                  lambda a, b, c, d: ((a.astype(F32) + b.astype(F32)) + c.astype(F32)) + d.astype(F32),
                                  spp, self.where, ins, jax.ShapeDtypeStruct(self.inputs[name].shape[1:], F32), (tm, pc),
                                  functools.partial(lambda i, w, spp: (w[1] * spp + i, 0), spp=spp)))
        return dict(zip([b[0] for b in BIG], _share_with_sibling(halves)))


def _step(inputs):
    x, mem, positions, target = inputs["x"][0], inputs["mem"][0], inputs["positions"], inputs["loss_target"][0]
    pos = positions.reshape(-1, 1)
    ex = _Exchange(inputs, _where())
    sp = {name: _as2d(inputs[name]) for name in SMALL}
    memb, = _rowwise("cast_mem", lambda m: (m,), [mem], [], [(D_MODEL, MXU_DTYPE)])

    loss, dx, gsmall = _local_step(x, memb, pos, target, sp, ex)
    gshard = ex.finish()

    out = {}
    for name, _, _, _ in BIG:
        w2, m2, v2 = inputs[name][0], inputs["m_" + name][0], inputs["v_" + name][0]
        n = w2.shape[1]
        d, nm, nv = _rowwise("adamw_" + name, _adamw, [w2, gshard[name], m2, v2], [], [(n, F32)] * 3, tm=256)
        lead = inputs[name].shape
        out[name] = (gshard[name].reshape(lead), d.reshape(lead), nm.reshape(lead), nv.reshape(lead))

    def tiles(a):
        flat = a.reshape(-1)
        n = -(-flat.shape[0] // 1024) * 1024
        return jnp.pad(flat, (0, n - flat.shape[0])).reshape(n // 128, 128)

    pieces = [tiles(loss[:, :1])] + [tiles(gsmall[name]) for name in SMALL]
    if sum(p.shape[0] for p in pieces) % 16:
        pieces.append(jnp.zeros((8, 128), F32))
    red = _allreduce_small(jnp.concatenate(pieces, axis=0))
    loss_total = red[0, 0]
    grads, off = {}, pieces[0].shape[0]
    for name, p in zip(SMALL, pieces[1:]):
        shp = _as2d(inputs[name]).shape
        grads[name] = red[off:off + p.shape[0]].reshape(-1)[:shp[0] * shp[1]].reshape(shp)
        off += p.shape[0]
    upd = _adamw_small([(_as2d(inputs[n]), grads[n], _as2d(inputs["m_" + n]), _as2d(inputs["v_" + n])) for n in SMALL])
    for i, name in enumerate(SMALL):
        shp = inputs[name].shape
        out[name] = (grads[name].reshape(shp),) + tuple(t.reshape(shp) for t in upd[3 * i:3 * i + 3])
    return loss_total, dx.reshape(inputs["x"].shape), out


_ARG_NAMES = (("x", "mem", "positions") + WEIGHT_ORDER + ("loss_target",) + tuple("m_" + n for n in WEIGHT_ORDER)
              + tuple("v_" + n for n in WEIGHT_ORDER))


def kernel(x, mem, positions, ln_in_g, ln_in_b, w_in, b_in, ssm_log_dt, ssm_a_re, ssm_a_im, ssm_b_re, ssm_b_im, ssm_c_re, ssm_c_im, ssm_d, w_glu, b_glu, w_att_up, w_mix_out, b_mix_out, ln1_g, ln1_b, w_xq, w_xkv, w_xo, ln2_g, ln2_b, w_ff1, b_ff1, w_ff2, b_ff2, ln3_g, ln3_b, loss_target, m_ln_in_g, m_ln_in_b, m_w_in, m_b_in, m_ssm_log_dt, m_ssm_a_re, m_ssm_a_im, m_ssm_b_re, m_ssm_b_im, m_ssm_c_re, m_ssm_c_im, m_ssm_d, m_w_glu, m_b_glu, m_w_att_up, m_w_mix_out, m_b_mix_out, m_ln1_g, m_ln1_b, m_w_xq, m_w_xkv, m_w_xo, m_ln2_g, m_ln2_b, m_w_ff1, m_b_ff1, m_w_ff2, m_b_ff2, m_ln3_g, m_ln3_b, v_ln_in_g, v_ln_in_b, v_w_in, v_b_in, v_ssm_log_dt, v_ssm_a_re, v_ssm_a_im, v_ssm_b_re, v_ssm_b_im, v_ssm_c_re, v_ssm_c_im, v_ssm_d, v_w_glu, v_b_glu, v_w_att_up, v_w_mix_out, v_b_mix_out, v_ln1_g, v_ln1_b, v_w_xq, v_w_xkv, v_w_xo, v_ln2_g, v_ln2_b, v_w_ff1, v_b_ff1, v_w_ff2, v_b_ff2, v_ln3_g, v_ln3_b):
    args = (x, mem, positions, ln_in_g, ln_in_b, w_in, b_in, ssm_log_dt, ssm_a_re, ssm_a_im, ssm_b_re, ssm_b_im, ssm_c_re, ssm_c_im, ssm_d, w_glu, b_glu, w_att_up, w_mix_out, b_mix_out, ln1_g, ln1_b, w_xq, w_xkv, w_xo, ln2_g, ln2_b, w_ff1, b_ff1, w_ff2, b_ff2, ln3_g, ln3_b, loss_target, m_ln_in_g, m_ln_in_b, m_w_in, m_b_in, m_ssm_log_dt, m_ssm_a_re, m_ssm_a_im, m_ssm_b_re, m_ssm_b_im, m_ssm_c_re, m_ssm_c_im, m_ssm_d, m_w_glu, m_b_glu, m_w_att_up, m_w_mix_out, m_b_mix_out, m_ln1_g, m_ln1_b, m_w_xq, m_w_xkv, m_w_xo, m_ln2_g, m_ln2_b, m_w_ff1, m_b_ff1, m_w_ff2, m_b_ff2, m_ln3_g, m_ln3_b, v_ln_in_g, v_ln_in_b, v_w_in, v_b_in, v_ssm_log_dt, v_ssm_a_re, v_ssm_a_im, v_ssm_b_re, v_ssm_b_im, v_ssm_c_re, v_ssm_c_im, v_ssm_d, v_w_glu, v_b_glu, v_w_att_up, v_w_mix_out, v_b_mix_out, v_ln1_g, v_ln1_b, v_w_xq, v_w_xkv, v_w_xo, v_ln2_g, v_ln2_b, v_w_ff1, v_b_ff1, v_w_ff2, v_b_ff2, v_ln3_g, v_ln3_b)
    assert len(args) == len(_ARG_NAMES)
    inputs = dict(zip(_ARG_NAMES, args))
    loss, dx, out = _step(inputs)
    res = [loss, dx]
    for k in range(4):
        res += [out[name][k] for name in WEIGHT_ORDER]
    return tuple(res)
```

```python
import functools
import math

import numpy as np
import jax
import jax.numpy as jnp
from jax import lax
from jax.experimental import pallas as pl
from jax.experimental.pallas import tpu as pltpu

F32 = jnp.float32
BF16 = jnp.bfloat16
MXU_DTYPE = jnp.bfloat16

D_MODEL = 1024
SSM_GROUP = 16
SSM_WIDTH = 768
SSM_GROUPS = 48
SSM_STATE = 64
N_STATE = SSM_GROUPS * SSM_STATE
SSM_CHUNKS = 6
CH_W = 128
CH_N = 512
ATT_HEAD_DIM = 64
ATT_HPG = 4
ATT_GROUPW = ATT_HPG * ATT_HEAD_DIM
DILATIONS = (1, 4, 16)
ATT_BLK = 128
ATT_SCALE = ATT_HEAD_DIM ** -0.5
ROT_DIM = 16
ROPE_THETA = 500000.0
XATT_HEADS = 4
XATT_HEAD_DIM = 256
XATT_SCALE = XATT_HEAD_DIM ** -0.5
D_FF = 4096
IN_COLS = 5120
DEEPNORM_ALPHA = 2.0 ** 0.25
LN_EPS = 1e-5
NEG_INF = -1e30
ADAM_LR = 0.001
ADAM_B1 = 0.9
ADAM_B2 = 0.999
ADAM_EPS = 1e-08
ADAM_WD = 0.01
ADAM_STEP = 10

N_SEG = 32
VMEM_LIMIT = 56 * 1024 * 1024
MESH = pl.DeviceIdType.MESH
HBM_SPEC = pl.BlockSpec(memory_space=pltpu.HBM)
VMEM_SPEC = pl.BlockSpec(memory_space=pltpu.VMEM)

BIG = (("w_in", 1024, 5120, 1), ("w_glu", 768, 2048, 1), ("w_att_up", 256, 1024, 1),
       ("w_mix_out", 1024, 1024, 0), ("w_xq", 1024, 1024, 0), ("w_xkv", 1024, 2048, 1),
       ("w_xo", 1024, 1024, 0), ("w_ff1", 1024, 4096, 1), ("w_ff2", 4096, 1024, 0))
SMALL = ("ln_in_g", "ln_in_b", "b_in", "ssm_log_dt", "ssm_a_re", "ssm_a_im", "ssm_b_re", "ssm_b_im",
         "ssm_c_re", "ssm_c_im", "ssm_d", "b_glu", "b_mix_out", "ln1_g", "ln1_b", "ln2_g", "ln2_b",
         "b_ff1", "b_ff2", "ln3_g", "ln3_b")
WEIGHT_ORDER = ("ln_in_g", "ln_in_b", "w_in", "b_in", "ssm_log_dt", "ssm_a_re", "ssm_a_im", "ssm_b_re",
                "ssm_b_im", "ssm_c_re", "ssm_c_im", "ssm_d", "w_glu", "b_glu", "w_att_up", "w_mix_out",
                "b_mix_out", "ln1_g", "ln1_b", "w_xq", "w_xkv", "w_xo", "ln2_g", "ln2_b", "w_ff1", "b_ff1",
                "w_ff2", "b_ff2", "ln3_g", "ln3_b")


def _cparams(n_axes):
    return pltpu.CompilerParams(dimension_semantics=("arbitrary",) * n_axes, vmem_limit_bytes=VMEM_LIMIT)


class _Carry:
    def __init__(self, ins, outs, n_sems, start, finish, done):
        self.ins, self.outs, self.n_sems, self.start, self.finish, self.done = ins, outs, n_sems, start, finish, done


def _call(name, body, grid, in_specs, out_specs, out_shape, args, scratch_shapes=(), carry=None):
    in_specs, out_specs, out_shape = list(in_specs), list(out_specs), list(out_shape)
    params = _cparams(len(grid))
    if carry is None:
        return pl.pallas_call(body, name=name, grid=grid, in_specs=in_specs, out_specs=out_specs, out_shape=out_shape,
                              scratch_shapes=list(scratch_shapes), compiler_params=params)(*args)
    n_in, n_out, n_ci, n_co = len(in_specs), len(out_specs), len(carry.ins), len(carry.outs)
    n_scr = len(scratch_shapes)

    def wrapped(*refs):
        ins, c_in = refs[:n_in], refs[n_in:n_in + n_ci]
        outs, c_out = refs[n_in + n_ci:n_in + n_ci + n_out], refs[n_in + n_ci + n_out:n_in + n_ci + n_out + n_co]
        scratch = refs[n_in + n_ci + n_out + n_co:n_in + n_ci + n_out + n_co + n_scr]
        send_sems, recv_sems = refs[-2:]
        ids = [pl.program_id(a) for a in range(len(grid))]
        first = functools.reduce(jnp.logical_and, [i == 0 for i in ids])
        last = functools.reduce(jnp.logical_and, [i == g - 1 for i, g in zip(ids, grid)])

        @pl.when(first)
        def _():
            carry.start(c_in, c_out, send_sems, recv_sems)

        body(*ins, *outs, *scratch)

        @pl.when(last)
        def _():
            carry.finish(c_in, c_out, send_sems, recv_sems)

    c_shapes = [jax.ShapeDtypeStruct(carry.ins[o].shape, carry.ins[o].dtype) if isinstance(o, int) else o
                for o in carry.outs]
    aliases = {n_in + o: n_out + i for i, o in enumerate(carry.outs) if isinstance(o, int)}
    res = pl.pallas_call(
        wrapped, name=name, grid=grid, in_specs=in_specs + [HBM_SPEC] * n_ci, out_specs=out_specs + [HBM_SPEC] * n_co,
        out_shape=out_shape + c_shapes, input_output_aliases=aliases,
        scratch_shapes=list(scratch_shapes) + [pltpu.SemaphoreType.DMA((carry.n_sems,))] * 2,
        compiler_params=params)(*args, *carry.ins)
    carry.done(res[n_out:])
    return res[:n_out]


def _rowwise(name, fn, rows, consts, outs, reds=(), tm=512, touts=(), carry=None):
    n_rows = (rows[0][0] if isinstance(rows[0], tuple) else rows[0]).shape[-2]
    tm = min(tm, n_rows)
    assert n_rows % tm == 0, (name, n_rows, tm)
    specs, args = [], []
    for r in rows:
        if isinstance(r, tuple) and len(r) == 3:
            arr, width, cb = r
            specs.append(pl.BlockSpec((tm, width), functools.partial(lambda i, cb: (i, cb), cb=cb)))
        elif isinstance(r, tuple):
            arr, slot = r
            specs.append(pl.BlockSpec((None, tm, arr.shape[2]), functools.partial(lambda i, s: (s, i, 0), s=slot)))
        else:
            arr = r
            specs.append(pl.BlockSpec((tm, arr.shape[1]), lambda i: (i, 0)))
        args.append(arr)
        assert arr.shape[-2] == n_rows, (name, arr.shape, n_rows)
    for cst in consts:
        specs.append(pl.BlockSpec(cst.shape, lambda i: (0, 0)))
        args.append(cst)
    n_r, n_c, n_o, n_d = len(rows), len(consts), len(outs) + len(touts), len(reds)
    out_shape = [jax.ShapeDtypeStruct((n_rows, c), dt) for c, dt in outs]
    out_specs = [pl.BlockSpec((tm, c), lambda i: (i, 0)) for c, _ in outs]
    out_shape += [jax.ShapeDtypeStruct((r, n_rows), dt) for r, dt in touts]
    out_specs += [pl.BlockSpec((r, tm), lambda i: (0, i)) for r, _ in touts]
    out_shape += [jax.ShapeDtypeStruct((1, c), F32) for c in reds]
    out_specs += [pl.BlockSpec((1, c), lambda i: (0, 0)) for c in reds]

    def body(*refs):
        ins = [r[...] for r in refs[:n_r + n_c]]
        o_refs = refs[n_r + n_c:n_r + n_c + n_o]
        d_refs = refs[n_r + n_c + n_o:]
        res = fn(*ins)
        res = res if isinstance(res, (tuple, list)) else (res,)
        assert len(res) == n_o + n_d, (name, len(res))
        for ref, val in zip(o_refs, res[:n_o]):
            ref[...] = val.astype(ref.dtype)
        first = pl.program_id(0) == 0
        for ref, val in zip(d_refs, res[n_o:]):
            @pl.when(first)
            def _(ref=ref, val=val):
                ref[...] = val

            @pl.when(jnp.logical_not(first))
            def _(ref=ref, val=val):
                ref[...] += val

    return _call(name, body, (n_rows // tm,), specs, out_specs, out_shape, args, carry=carry)


def _colsum(v):
    return jnp.sum(v.astype(F32), axis=0, keepdims=True)


_DIMS = {"nn": (((1,), (0,)), ((), ())), "nt": (((1,), (1,)), ((), ())), "tn": (((0,), (0,)), ((), ()))}


def _tile(dim, want):
    if dim <= want:
        return dim
    return max(t for t in range(128, want + 1, 128) if dim % t == 0)


def _dot(a, b, mode):
    return lax.dot_general(a.astype(MXU_DTYPE), b.astype(MXU_DTYPE), _DIMS[mode], preferred_element_type=F32)


def _mm(name, a, b, mode, *, bias=None, extras=(), epilogue=None, out_dtypes=(F32,), tm=1024, tn=1024, tk=1024,
        carry=None, colsum=False):
    if mode == "nn":
        (m, k), (_, n) = a.shape, b.shape
    elif mode == "nt":
        (m, k), (n, _) = a.shape, b.shape
    else:
        (k, m), (_, n) = a.shape, b.shape
    if k > tk:
        tk = 5 * tk
    tn = _tile(n, tn)
    tk = _tile(k, tk)
    nk = k // tk

    def vmem_bytes(rows):
        blocks = rows * tk * a.dtype.itemsize + tk * tn * b.dtype.itemsize
        blocks += sum(rows * tn * e.dtype.itemsize for e in extras)
        blocks += sum(rows * tn * jnp.dtype(dt).itemsize for dt in out_dtypes)
        return 2 * blocks + (rows * tn * 4 if nk > 1 else 0)

    tm = _tile(m, tm if mode == "tn" else 2 * tm)
    while vmem_bytes(tm) > 3 * VMEM_LIMIT // 4 and tm % 256 == 0:
        tm //= 2
    while nk == 1 and k > 1024 and (m // tm) * (n // tn) < 4 and tm % 256 == 0:
        tm //= 2
    assert m % tm == 0 and n % tn == 0 and k % tk == 0, (name, m, n, k)
    a_spec = {"nn": pl.BlockSpec((tm, tk), lambda i, j, kk: (i, kk)),
              "nt": pl.BlockSpec((tm, tk), lambda i, j, kk: (i, kk)),
              "tn": pl.BlockSpec((tk, tm), lambda i, j, kk: (kk, i))}[mode]
    b_spec = {"nn": pl.BlockSpec((tk, tn), lambda i, j, kk: (kk, j)),
              "nt": pl.BlockSpec((tn, tk), lambda i, j, kk: (j, kk)),
              "tn": pl.BlockSpec((tk, tn), lambda i, j, kk: (kk, j))}[mode]
    specs, args = [a_spec, b_spec], [a, b]
    if bias is not None:
        specs.append(pl.BlockSpec((1, tn), lambda i, j, kk: (0, j)))
        args.append(bias)
    for e in extras:
        specs.append(pl.BlockSpec((tm, tn), lambda i, j, kk: (i, j)))
        args.append(e)
    n_e, n_o = len(extras), len(out_dtypes)
    has_bias = bias is not None

    def body(*refs):
        a_ref, b_ref = refs[0], refs[1]
        pos = 2
        bias_ref = refs[pos] if has_bias else None
        pos += int(has_bias)
        e_refs = refs[pos:pos + n_e]
        o_refs = refs[pos + n_e:pos + n_e + n_o]
        sum_ref = refs[pos + n_e + n_o] if colsum else None
        acc_ref = refs[pos + n_e + n_o + int(colsum)] if nk > 1 else None
        part = _dot(a_ref[...], b_ref[...], mode)

        def finish(r):
            if has_bias:
                r = r + bias_ref[...]
            res = epilogue(r, *[e[...] for e in e_refs]) if epilogue is not None else (r,)
            for ref, val in zip(o_refs, res):
                ref[...] = val.astype(ref.dtype)
            if colsum:
                sum_ref[...] = _colsum(res[0])

        if nk == 1:
            finish(part)
        else:
            kk = pl.program_id(2)

            @pl.when(kk == 0)
            def _():
                acc_ref[...] = part

            @pl.when(kk > 0)
            def _():
                acc_ref[...] += part

            @pl.when(kk == nk - 1)
            def _():
                finish(acc_ref[...])

    out_specs = [pl.BlockSpec((tm, tn), lambda i, j, kk: (i, j)) for _ in out_dtypes]
    out_shape = [jax.ShapeDtypeStruct((m, n), dt) for dt in out_dtypes]
    if colsum:
        out_specs.append(pl.BlockSpec((None, 1, tn), lambda i, j, kk: (i, 0, j)))
        out_shape.append(jax.ShapeDtypeStruct((m // tm, 1, n), F32))
    res = _call(name, body, (m // tm, n // tn, nk), specs, out_specs, out_shape, args,
                scratch_shapes=[pltpu.VMEM((tm, tn), F32)] if nk > 1 else [], carry=carry)
    return res[0] if len(res) == 1 else res


def _ssm_wgrads(u, dy, g_re, g_im, h_re, h_im, tk=1024):
    s = u.shape[0]
    tk = min(tk, s)
    nk = s // tk
    assert tk % N_SEG == 0

    def body(u_ref, dy_ref, gre_ref, gim_ref, hre_ref, him_ref, lre_ref, lim_ref, db_ref, dc_ref, dar_ref, dai_ref,
             pre_ref, pim_ref):
        kk = pl.program_id(1)
        u_blk, dy_blk = u_ref[...], dy_ref[...]
        g_r, g_i, h_r, h_i = gre_ref[...], gim_ref[...], hre_ref[...], him_ref[...]
        d_b = jnp.concatenate([_dot(u_blk, g_r, "tn"), _dot(u_blk, g_i, "tn")], axis=1)
        d_c = jnp.concatenate([_dot(h_r, dy_blk, "tn"), _dot(h_i, dy_blk, "tn")], axis=0)

        @pl.when(kk == 0)
        def _():
            first_row = lax.broadcasted_iota(jnp.int32, (N_SEG, CH_N), 0) == 0
            pre_ref[...] = jnp.where(first_row, 0.0, pltpu.roll(lre_ref[...], 1, 0))
            pim_ref[...] = jnp.where(first_row, 0.0, pltpu.roll(lim_ref[...], 1, 0))

        p_r = jnp.concatenate([pre_ref[...], h_r[:tk - N_SEG]], axis=0)
        p_i = jnp.concatenate([pim_ref[...], h_i[:tk - N_SEG]], axis=0)
        pre_ref[...] = h_r[tk - N_SEG:]
        pim_ref[...] = h_i[tk - N_SEG:]
        d_ar = jnp.sum(g_r * p_r + g_i * p_i, axis=0, keepdims=True)
        d_ai = jnp.sum(g_i * p_r - g_r * p_i, axis=0, keepdims=True)

        @pl.when(kk == 0)
        def _():
            db_ref[...] = d_b
            dc_ref[...] = d_c
            dar_ref[...] = d_ar
            dai_ref[...] = d_ai

        @pl.when(kk > 0)
        def _():
            db_ref[...] += d_b
            dc_ref[...] += d_c
            dar_ref[...] += d_ar
            dai_ref[...] += d_ai

    chan = pl.BlockSpec((tk, CH_W), lambda j, kk: (kk, j))
    state = pl.BlockSpec((tk, CH_N), lambda j, kk: (kk, j))
    last = pl.BlockSpec((N_SEG, CH_N), lambda j, kk: (s // N_SEG - 1, j))
    row = pl.BlockSpec((1, CH_N), lambda j, kk: (0, j))
    return pl.pallas_call(
        body, name="ssm_wgrads", grid=(SSM_CHUNKS, nk),
        in_specs=[chan, chan, state, state, state, state, last, last],
        out_specs=[pl.BlockSpec((None, CH_W, 2 * CH_N), lambda j, kk: (j, 0, 0)),
                   pl.BlockSpec((None, 2 * CH_N, CH_W), lambda j, kk: (j, 0, 0)), row, row],
        out_shape=[jax.ShapeDtypeStruct((SSM_CHUNKS, CH_W, 2 * CH_N), F32),
                   jax.ShapeDtypeStruct((SSM_CHUNKS, 2 * CH_N, CH_W), F32),
                   jax.ShapeDtypeStruct((1, N_STATE), F32), jax.ShapeDtypeStruct((1, N_STATE), F32)],
        scratch_shapes=[pltpu.VMEM((N_SEG, CH_N), F32)] * 2,
        compiler_params=_cparams(2))(u, dy, g_re, g_im, h_re, h_im, h_re, h_im)


SCAN_LB = 256


def _split_by_scan_block(mat, axis):
    halves = []
    for l in range(CH_N // SCAN_LB):
        re = lax.slice_in_dim(mat, l * SCAN_LB, (l + 1) * SCAN_LB, axis=axis)
        im = lax.slice_in_dim(mat, CH_N + l * SCAN_LB, CH_N + (l + 1) * SCAN_LB, axis=axis)
        halves.append(jnp.concatenate([re, im], axis=axis))
    return jnp.stack(halves, axis=1).reshape((-1,) + halves[0].shape[1:])


def _ssm_scan(name, chan, expand12, contract12, a_re, a_im, d_row, reverse, carry=None):
    s = chan.shape[0]
    seg_len = s // N_SEG
    n_sq = int(math.log2(seg_len))
    assert 2 ** n_sq == seg_len
    rb = min(512, s)
    per_chunk = CH_N // SCAN_LB

    def body(are_ref, aim_ref, ch_ref, e_ref, k_ref, d_ref, hre_ref, him_ref, o_ref, wre_ref, wim_ref, ere, eim, cre, cim):
        e_mat, k_mat = e_ref[...], k_ref[...]
        for r in range(s // rb):
            rows = slice(r * rb, (r + 1) * rb)
            w = _dot(ch_ref[rows, :], e_mat, "nt" if reverse else "nn")
            wre_ref[rows, :] = w[:, :SCAN_LB]
            wim_ref[rows, :] = w[:, SCAN_LB:]

        ar1 = are_ref[...]
        ai1 = -aim_ref[...] if reverse else aim_ref[...]
        ar = jnp.broadcast_to(ar1, (N_SEG, SCAN_LB))
        ai = jnp.broadcast_to(ai1, (N_SEG, SCAN_LB))

        def rows_of(k):
            kk = seg_len - 1 - k if reverse else k
            return pl.ds(pl.multiple_of(kk * N_SEG, N_SEG), N_SEG)

        def local(k, carry):
            hr, hi = carry
            rows = rows_of(k)
            nr = ar * hr - ai * hi + wre_ref[rows, :]
            ni = ar * hi + ai * hr + wim_ref[rows, :]
            hre_ref[rows, :] = nr
            him_ref[rows, :] = ni
            return nr, ni

        zero = jnp.zeros((N_SEG, SCAN_LB), F32)
        er, ei = lax.fori_loop(0, seg_len, local, (zero, zero))
        ere[...] = er
        eim[...] = ei
        pr, pi = ar1, ai1
        for _ in range(n_sq):
            pr, pi = pr * pr - pi * pi, 2.0 * pr * pi
        cr = jnp.zeros((1, SCAN_LB), F32)
        ci = jnp.zeros((1, SCAN_LB), F32)
        for jj in range(N_SEG):
            j = N_SEG - 1 - jj if reverse else jj
            cre[j:j + 1, :] = cr
            cim[j:j + 1, :] = ci
            er_j, ei_j = ere[j:j + 1, :], eim[j:j + 1, :]
            cr, ci = pr * cr - pi * ci + er_j, pr * ci + pi * cr + ei_j
        c_r, c_i = cre[...], cim[...]

        def fix(k, carry):
            qr, qi = carry
            rows = rows_of(k)
            hre_ref[rows, :] = hre_ref[rows, :] + (qr * c_r - qi * c_i)
            him_ref[rows, :] = him_ref[rows, :] + (qr * c_i + qi * c_r)
            return qr * ar - qi * ai, qr * ai + qi * ar

        lax.fori_loop(0, seg_len, fix, (ar, ai))

        first_of_chunk = lax.rem(pl.program_id(0), per_chunk) == 0
        for r in range(s // rb):
            rows = slice(r * rb, (r + 1) * rb)
            h_cat = jnp.concatenate([hre_ref[rows, :], him_ref[rows, :]], axis=1)
            part = _dot(h_cat, k_mat, "nt" if reverse else "nn")

            @pl.when(first_of_chunk)
            def _(rows=rows, part=part):
                o_ref[rows, :] = part + d_ref[...] * ch_ref[rows, :]

            @pl.when(jnp.logical_not(first_of_chunk))
            def _(rows=rows, part=part):
                o_ref[rows, :] += part

    nblk = N_STATE // SCAN_LB
    blk = pl.BlockSpec((s, SCAN_LB), lambda b: (0, b))
    row = pl.BlockSpec((1, SCAN_LB), lambda b: (0, b))
    chan_blk = pl.BlockSpec((s, CH_W), lambda b: (0, b // per_chunk))
    res = _call(name, body, (nblk,),
                [row, row, chan_blk, pl.BlockSpec((None,) + expand12.shape[1:], lambda b: (b, 0, 0)),
                 pl.BlockSpec((None,) + contract12.shape[1:], lambda b: (b, 0, 0)),
                 pl.BlockSpec((1, CH_W), lambda b: (0, b // per_chunk))],
                [blk, blk, chan_blk],
                [jax.ShapeDtypeStruct((s, N_STATE), F32)] * 2 + [jax.ShapeDtypeStruct((s, SSM_WIDTH), F32)],
                (a_re, a_im, chan, expand12, contract12, d_row),
                scratch_shapes=[pltpu.VMEM((s, SCAN_LB), F32)] * 2 + [pltpu.VMEM((N_SEG, SCAN_LB), F32)] * 4, carry=carry)
    return res[0], res[1], res[2]


def _disc(ldt, are, aim, bre, bim):
    dt = jnp.exp(ldt)
    mag = jnp.exp(are * dt)
    abr = mag * jnp.cos(aim * dt)
    abi = mag * jnp.sin(aim * dt)
    den = jnp.square(are) + jnp.square(aim)
    nr = abr - 1.0
    fre = (nr * are + abi * aim) / den
    fim = (abi * are - nr * aim) / den
    return abr, abi, fre * bre - fim * bim, fre * bim + fim * bre


def _ssm_disc_fwd(ldt, are, aim, bre, bim):
    def body(l_ref, ar_ref, ai_ref, br_ref, bi_ref, o0, o1, o2, o3):
        res = _disc(l_ref[...], ar_ref[...], ai_ref[...], br_ref[...], bi_ref[...])
        for ref, val in zip((o0, o1, o2, o3), res):
            ref[...] = val

    col = jax.ShapeDtypeStruct((N_STATE, 1), F32)
    mat = jax.ShapeDtypeStruct((N_STATE, SSM_GROUP), F32)
    return pl.pallas_call(body, name="ssm_disc_fwd", out_shape=[col, col, mat, mat],
                          in_specs=[VMEM_SPEC] * 5, out_specs=[VMEM_SPEC] * 4)(ldt, are, aim, bre, bim)


def _ssm_disc_bwd(ldt, are, aim, bre, bim, d_abr, d_abi, d_bbr, d_bbi):
    def body(l_ref, ar_ref, ai_ref, br_ref, bi_ref, c0, c1, c2, c3, g_ldt, g_are, g_aim, g_bre, g_bim):
        _, vjp = jax.vjp(_disc, l_ref[...], ar_ref[...], ai_ref[...], br_ref[...], bi_ref[...])
        dl, dar, dai, dbr, dbi = vjp((c0[...], c1[...], c2[...], c3[...]))
        state = lax.broadcasted_iota(jnp.int32, (N_STATE, SSM_GROUPS), 0)
        group = lax.broadcasted_iota(jnp.int32, (N_STATE, SSM_GROUPS), 1)
        pick = jnp.right_shift(state, 6) == group
        g_ldt[...] = jnp.sum(jnp.where(pick, dl, 0.0), axis=0, keepdims=True)
        g_are[...] = dar
        g_aim[...] = dai
        g_bre[...] = dbr
        g_bim[...] = dbi

    col = jax.ShapeDtypeStruct((N_STATE, 1), F32)
    mat = jax.ShapeDtypeStruct((N_STATE, SSM_GROUP), F32)
    return pl.pallas_call(body, name="ssm_disc_bwd",
                          out_shape=[jax.ShapeDtypeStruct((1, SSM_GROUPS), F32), col, col, mat, mat],
                          in_specs=[VMEM_SPEC] * 9, out_specs=[VMEM_SPEC] * 5,
                          compiler_params=pltpu.CompilerParams(vmem_limit_bytes=VMEM_LIMIT))(
        ldt, are, aim, bre, bim, d_abr, d_abi, d_bbr, d_bbi)


_EYE8 = np.eye(8, dtype=np.float32)


def _blockdiag_b(bb):
    t = bb.reshape(SSM_CHUNKS, 8, SSM_STATE, SSM_GROUP).transpose(0, 1, 3, 2)
    return jnp.einsum("igcn,gh->igchn", t, _EYE8).reshape(SSM_CHUNKS, CH_W, CH_N)


def _diag_of_b(m):
    t = jnp.einsum("igchn,gh->igcn", m.reshape(SSM_CHUNKS, 8, SSM_GROUP, 8, SSM_STATE), _EYE8)
    return t.transpose(0, 1, 3, 2).reshape(N_STATE, SSM_GROUP)


def _blockdiag_c(c):
    t = c.reshape(SSM_CHUNKS, 8, SSM_GROUP, SSM_STATE).transpose(0, 1, 3, 2)
    return jnp.einsum("ignc,gh->ignhc", t, _EYE8).reshape(SSM_CHUNKS, CH_N, CH_W)


def _diag_of_c(m):
    t = jnp.einsum("ignhc,gh->ignc", m.reshape(SSM_CHUNKS, 8, SSM_STATE, 8, SSM_GROUP), _EYE8)
    return t.transpose(0, 1, 3, 2).reshape(SSM_GROUPS, SSM_GROUP, SSM_STATE)


def _time_perm(a):
    s, c = a.shape
    return a.reshape(N_SEG, s // N_SEG, c).transpose(1, 0, 2).reshape(s, c)


def _time_unperm(a):
    s, c = a.shape
    return a.reshape(s // N_SEG, N_SEG, c).transpose(1, 0, 2).reshape(s, c)


def _dilate(a, d):
    s, c = a.shape
    return a if d == 1 else a.reshape(s // d, d, c).transpose(1, 0, 2).reshape(s, c)


def _undilate(a, d):
    s, c = a.shape
    return a if d == 1 else a.reshape(d, s // d, c).transpose(1, 0, 2).reshape(s, c)


def _dilate_rows(a, d):
    r, s = a.shape
    return a if d == 1 else a.reshape(r, s // d, d).transpose(0, 2, 1).reshape(r, s)


ATT_T = 4
ATT_ROWS = ATT_T * ATT_BLK


def _window(prev_ref, cur_ref, i, sl):
    if i == 0:
        return jnp.concatenate([prev_ref[:, sl], cur_ref[0:ATT_BLK, sl]], axis=0)
    return cur_ref[(i - 1) * ATT_BLK:(i + 1) * ATT_BLK, sl]


def _band_valid(first_key):
    qi = lax.broadcasted_iota(jnp.int32, (ATT_BLK, 2 * ATT_BLK), 0)
    ki = lax.broadcasted_iota(jnp.int32, (ATT_BLK, 2 * ATT_BLK), 1)
    steps = qi + ATT_BLK - ki
    return (steps >= 0) & (steps <= ATT_BLK) & (ki >= first_key)


ATT_STATW = ATT_HPG * 128


def _stat(h):
    return slice(h * 128, (h + 1) * 128)


def _stat_rows(stat):
    n = stat.shape[0]
    heads = [stat[:, _stat(h)].T[0:1, :] for h in range(ATT_HPG)]
    return jnp.concatenate(heads + [jnp.zeros((8 - ATT_HPG, n), stat.dtype)], axis=0)


def _attn_specs(nb, width=ATT_GROUPW):
    cur = pl.BlockSpec((ATT_ROWS, width), lambda b: (b, 0))
    prev = pl.BlockSpec((ATT_BLK, width), lambda b: (jnp.maximum(b * ATT_T - 1, 0), 0))
    nxt = pl.BlockSpec((ATT_BLK, width), lambda b: (jnp.minimum((b + 1) * ATT_T, nb - 1), 0))
    return cur, prev, nxt


def _attn_fwd(tag, per_seq, q, k, v):
    s = q.shape[0]
    nb = s // ATT_BLK

    def body(q_ref, kc_ref, kp_ref, vc_ref, vp_ref, o_ref, lse_ref):
        bt = pl.program_id(0)
        for i in range(ATT_T):
            has_prev = lax.rem(bt * ATT_T + i, per_seq) > 0
            valid = _band_valid(jnp.where(has_prev, 0, ATT_BLK))
            rows = slice(i * ATT_BLK, (i + 1) * ATT_BLK)
            for h in range(ATT_HPG):
                sl = slice(h * ATT_HEAD_DIM, (h + 1) * ATT_HEAD_DIM)
                kcat = _window(kp_ref, kc_ref, i, sl)
                vcat = _window(vp_ref, vc_ref, i, sl)
                sc = _dot(q_ref[rows, sl], kcat, "nt") * ATT_SCALE
                sc = jnp.where(valid, sc, NEG_INF)
                m = jnp.max(sc, axis=-1, keepdims=True)
                p = jnp.exp(sc - m)
                den = jnp.sum(p, axis=-1, keepdims=True)
                o_ref[rows, sl] = _dot(p, vcat, "nn") / den
                lse_ref[rows, _stat(h)] = jnp.broadcast_to(m + jnp.log(den), (ATT_BLK, 128))

    cur, prev, _ = _attn_specs(nb)
    stat, _, _ = _attn_specs(nb, ATT_STATW)
    return pl.pallas_call(
        body, name="attn_fwd_" + tag, grid=(nb // ATT_T,), in_specs=[cur, cur, prev, cur, prev], out_specs=[cur, stat],
        out_shape=[jax.ShapeDtypeStruct((s, ATT_GROUPW), F32), jax.ShapeDtypeStruct((s, ATT_STATW), F32)],
        compiler_params=_cparams(1))(q, k, k, v, v)


def _attn_dq(tag, per_seq, q, k, v, do, lse, delta):
    s = q.shape[0]
    nb = s // ATT_BLK

    def body(q_ref, kc_ref, kp_ref, vc_ref, vp_ref, do_ref, lse_ref, dl_ref, dq_ref):
        bt = pl.program_id(0)
        for i in range(ATT_T):
            has_prev = lax.rem(bt * ATT_T + i, per_seq) > 0
            valid = _band_valid(jnp.where(has_prev, 0, ATT_BLK))
            rows = slice(i * ATT_BLK, (i + 1) * ATT_BLK)
            for h in range(ATT_HPG):
                sl = slice(h * ATT_HEAD_DIM, (h + 1) * ATT_HEAD_DIM)
                kcat = _window(kp_ref, kc_ref, i, sl)
                vcat = _window(vp_ref, vc_ref, i, sl)
                lse = jnp.concatenate([lse_ref[rows, _stat(h)]] * 2, axis=1)
                dlt = jnp.concatenate([dl_ref[rows, _stat(h)]] * 2, axis=1)
                sc = _dot(q_ref[rows, sl], kcat, "nt") * ATT_SCALE
                p = jnp.exp(jnp.where(valid, sc, NEG_INF) - lse)
                dp = _dot(do_ref[rows, sl], vcat, "nt")
                ds = p * (dp - dlt) * ATT_SCALE
                dq_ref[rows, sl] = _dot(ds, kcat, "nn")

    cur, prev, _ = _attn_specs(nb)
    stat, _, _ = _attn_specs(nb, ATT_STATW)
    return pl.pallas_call(
        body, name="attn_dq_" + tag, grid=(nb // ATT_T,), in_specs=[cur, cur, prev, cur, prev, cur, stat, stat],
        out_specs=cur, out_shape=jax.ShapeDtypeStruct((s, ATT_GROUPW), F32),
        compiler_params=_cparams(1))(q, k, k, v, v, do, lse, delta)


def _attn_dkv(tag, per_seq, q, k, v, do, lse_t, delta_t):
    s = q.shape[0]
    nb = s // ATT_BLK

    def body(k_ref, v_ref, qc_ref, qn_ref, doc_ref, don_ref, lc_ref, ln_ref, dc_ref, dn_ref, dk_ref, dv_ref):
        bt = pl.program_id(0)
        ki = lax.broadcasted_iota(jnp.int32, (ATT_BLK, 2 * ATT_BLK), 0)
        ci = lax.broadcasted_iota(jnp.int32, (ATT_BLK, 2 * ATT_BLK), 1)

        def pair(edge_ref, cur_ref, i, sl):
            if i == ATT_T - 1:
                return jnp.concatenate([cur_ref[i * ATT_BLK:(i + 1) * ATT_BLK, sl], edge_ref[:, sl]], axis=0)
            return cur_ref[i * ATT_BLK:(i + 2) * ATT_BLK, sl]

        def pair_row(edge_ref, cur_ref, i, h):
            if i == ATT_T - 1:
                row = jnp.concatenate([cur_ref[h:h + 1, i * ATT_BLK:(i + 1) * ATT_BLK], edge_ref[h:h + 1, :]], axis=1)
            else:
                row = cur_ref[h:h + 1, i * ATT_BLK:(i + 2) * ATT_BLK]
            return jnp.broadcast_to(row, (ATT_BLK, 2 * ATT_BLK))

        for i in range(ATT_T):
            b = bt * ATT_T + i
            next_uses = (b + 1 < nb) & (lax.rem(b + 1, per_seq) > 0)
            reach = jnp.where(next_uses, 0, 4 * ATT_BLK)
            valid = ((ci < ATT_BLK) & (ci >= ki)) | ((ci >= ATT_BLK) & (ki - ci + ATT_BLK >= reach))
            rows = slice(i * ATT_BLK, (i + 1) * ATT_BLK)
            for h in range(ATT_HPG):
                sl = slice(h * ATT_HEAD_DIM, (h + 1) * ATT_HEAD_DIM)
                qcat, docat = pair(qn_ref, qc_ref, i, sl), pair(don_ref, doc_ref, i, sl)
                sc = _dot(k_ref[rows, sl], qcat, "nt") * ATT_SCALE
                p = jnp.exp(jnp.where(valid, sc, NEG_INF) - pair_row(ln_ref, lc_ref, i, h))
                dv_ref[rows, sl] = _dot(p, docat, "nn")
                dp = _dot(v_ref[rows, sl], docat, "nt")
                ds = p * (dp - pair_row(dn_ref, dc_ref, i, h)) * ATT_SCALE
                dk_ref[rows, sl] = _dot(ds, qcat, "nn")

    cur, _, nxt = _attn_specs(nb)
    stat = pl.BlockSpec((8, ATT_ROWS), lambda b: (0, b))
    snxt = pl.BlockSpec((8, ATT_BLK), lambda b: (0, jnp.minimum((b + 1) * ATT_T, nb - 1)))
    return pl.pallas_call(
        body, name="attn_dkv_" + tag, grid=(nb // ATT_T,), in_specs=[cur, cur, cur, nxt, cur, nxt, stat, snxt, stat, snxt],
        out_specs=[cur, cur], out_shape=[jax.ShapeDtypeStruct((s, ATT_GROUPW), F32)] * 2,
        compiler_params=_cparams(1))(k, v, q, q, do, do, lse_t, lse_t, delta_t, delta_t)


def _xattn_probs(q, kh):
    sc = _dot(q, kh, "nt") * XATT_SCALE
    e = jnp.exp(sc - jnp.max(sc, axis=-1, keepdims=True))
    return e / jnp.sum(e, axis=-1, keepdims=True)


def _xattn_fwd(q, kv, tm=512):
    s = q.shape[0]
    tm = min(tm, s)

    def body(q_ref, kv_ref, o_ref):
        for h in range(XATT_HEADS):
            sl = slice(h * XATT_HEAD_DIM, (h + 1) * XATT_HEAD_DIM)
            vs = slice(D_MODEL + h * XATT_HEAD_DIM, D_MODEL + (h + 1) * XATT_HEAD_DIM)
            p = _xattn_probs(q_ref[:, sl], kv_ref[:, sl])
            o_ref[:, sl] = _dot(p, kv_ref[:, vs], "nn").astype(o_ref.dtype)

    return pl.pallas_call(
        body, name="xattn_fwd", grid=(s // tm,),
        in_specs=[pl.BlockSpec((tm, D_MODEL), lambda i: (i, 0)), pl.BlockSpec(kv.shape, lambda i: (0, 0))],
        out_specs=pl.BlockSpec((tm, D_MODEL), lambda i: (i, 0)),
        out_shape=jax.ShapeDtypeStruct((s, D_MODEL), MXU_DTYPE), compiler_params=_cparams(1))(q, kv)


def _xattn_bwd(q, kv, do, tm=512):
    s = q.shape[0]
    tm = min(tm, s)

    def body(q_ref, kv_ref, do_ref, dq_ref, dkv_ref):
        first = pl.program_id(0) == 0

        @pl.when(first)
        def _():
            dkv_ref[...] = jnp.zeros_like(dkv_ref)

        for h in range(XATT_HEADS):
            sl = slice(h * XATT_HEAD_DIM, (h + 1) * XATT_HEAD_DIM)
            vs = slice(D_MODEL + h * XATT_HEAD_DIM, D_MODEL + (h + 1) * XATT_HEAD_DIM)
            p = _xattn_probs(q_ref[:, sl], kv_ref[:, sl])
            dkv_ref[:, vs] += _dot(p, do_ref[:, sl], "tn")
            dp = _dot(do_ref[:, sl], kv_ref[:, vs], "nt")
            ds = p * (dp - jnp.sum(dp * p, axis=-1, keepdims=True)) * XATT_SCALE
            dq_ref[:, sl] = _dot(ds, kv_ref[:, sl], "nn").astype(dq_ref.dtype)
            dkv_ref[:, sl] += _dot(ds, q_ref[:, sl], "tn")

    row = pl.BlockSpec((tm, D_MODEL), lambda i: (i, 0))
    whole = pl.BlockSpec(kv.shape, lambda i: (0, 0))
    return pl.pallas_call(
        body, name="xattn_bwd", grid=(s // tm,), in_specs=[row, whole, row], out_specs=[row, whole],
        out_shape=[jax.ShapeDtypeStruct((s, D_MODEL), MXU_DTYPE), jax.ShapeDtypeStruct(kv.shape, F32)],
        compiler_params=_cparams(1))(q, kv, do)


def _ln(x, g, b):
    mu = jnp.mean(x, axis=-1, keepdims=True)
    xc = x - mu
    var = jnp.mean(jnp.square(xc), axis=-1, keepdims=True)
    return xc * lax.rsqrt(var + LN_EPS) * g + b


def _res_ln(h, o, g, b):
    return _ln(DEEPNORM_ALPHA * h + o, g, b)


def _gate(gs, ga, z1, z2, batt):
    return jax.nn.sigmoid(gs) * (z1 * jax.nn.sigmoid(z2)) + jax.nn.sigmoid(ga) * batt


ROPE_TW = 2 * ATT_HEAD_DIM


def _rope_tables(pos, invf, m1, m2):
    ang = pos.astype(F32) * invf
    sin = jnp.sin(ang)
    return jnp.cos(ang), -sin * m1, sin * m2


def _widen(tab):
    return jnp.concatenate([tab] * (ATT_GROUPW // ROPE_TW), axis=1)


def _rope(t, cos, s_up, s_dn):
    w = t.shape[-1]
    return t * cos + pltpu.roll(t, w - ROT_DIM // 2, 1) * s_up + pltpu.roll(t, ROT_DIM // 2, 1) * s_dn


def _rope_t(dt, cos, s_up, s_dn):
    w = dt.shape[-1]
    return dt * cos + pltpu.roll(dt * s_up, ROT_DIM // 2, 1) + pltpu.roll(dt * s_dn, w - ROT_DIM // 2, 1)


def _rope_consts():
    inv_freq = ROPE_THETA ** (-jnp.arange(0, ROT_DIM, 2, dtype=F32) / ROT_DIM)
    d = np.arange(ROPE_TW) % ATT_HEAD_DIM
    invf = jnp.where(d < ROT_DIM, inv_freq[d % (ROT_DIM // 2)], 0.0).reshape(1, ROPE_TW).astype(F32)
    m1 = jnp.asarray((d < ROT_DIM // 2).astype(np.float32)).reshape(1, ROPE_TW)
    m2 = jnp.asarray(((d >= ROT_DIM // 2) & (d < ROT_DIM)).astype(np.float32)).reshape(1, ROPE_TW)
    return invf, m1, m2


def _head_sum_matrix():
    d = np.arange(ATT_GROUPW) // ATT_HEAD_DIM
    s = np.arange(ATT_STATW) // 128
    return jnp.asarray((d[:, None] == s[None, :]).astype(np.float32))


def _adamw(w, g, m, v):
    m = ADAM_B1 * m + (1.0 - ADAM_B1) * g
    v = ADAM_B2 * v + (1.0 - ADAM_B2) * jnp.square(g)
    m_hat = m / (1.0 - ADAM_B1 ** ADAM_STEP)
    v_hat = v / (1.0 - ADAM_B2 ** ADAM_STEP)
    delta = -ADAM_LR * (m_hat / (jnp.sqrt(v_hat) + ADAM_EPS) + ADAM_WD * w)
    return delta, m, v


def _local_step(x, mem, pos, target, sp, ex):
    s = x.shape[0]
    al = DEEPNORM_ALPHA
    mx = MXU_DTYPE

    h0, h0b = _rowwise("ln_in", lambda x, g, b: (lambda h: (h, h))(_ln(x, g, b)), [x],
                       [sp["ln_in_g"], sp["ln_in_b"]], [(D_MODEL, F32), (D_MODEL, mx)],
                       carry=ex.gather_carry(["w_in"]))
    proj = _mm("proj", h0b, ex.weight("w_in"), "nn", bias=sp["b_in"],
               carry=ex.gather_carry(["w_glu", "w_att_up", "w_mix_out", "w_xq", "w_xkv"]))

    ldt = jnp.repeat(sp["ssm_log_dt"].reshape(SSM_GROUPS), SSM_STATE).reshape(N_STATE, 1)
    are, aim = sp["ssm_a_re"].reshape(N_STATE, 1), sp["ssm_a_im"].reshape(N_STATE, 1)
    bre, bim = sp["ssm_b_re"].reshape(N_STATE, SSM_GROUP), sp["ssm_b_im"].reshape(N_STATE, SSM_GROUP)
    abr, abi, bbr, bbi = _ssm_disc_fwd(ldt, are, aim, bre, bim)
    a_re, a_im = abr.reshape(1, N_STATE), abi.reshape(1, N_STATE)
    bexp = jnp.concatenate([_blockdiag_b(bbr), _blockdiag_b(bbi)], axis=2).astype(mx)
    cexp = jnp.concatenate([_blockdiag_c(sp["ssm_c_re"].reshape(SSM_GROUPS, SSM_GROUP, SSM_STATE)),
                            -_blockdiag_c(sp["ssm_c_im"].reshape(SSM_GROUPS, SSM_GROUP, SSM_STATE))],
                           axis=1).astype(mx)
    u_p = _time_perm(proj[:, :SSM_WIDTH])
    b12, c12 = _split_by_scan_block(bexp, 2), _split_by_scan_block(cexp, 1)
    h_re, h_im, y_p = _ssm_scan("ssm_scan_fwd", u_p, b12, c12, a_re, a_im, sp["ssm_d"], reverse=False,
                                carry=ex.gather_carry(["w_ff1", "w_ff2"]))
    y = _time_unperm(y_p)
    ygb, = _rowwise("gelu", lambda y: jax.nn.gelu(y), [y], [], [(SSM_WIDTH, mx)])
    z = _mm("glu", ygb, ex.weight("w_glu"), "nn", bias=sp["b_glu"], carry=ex.gather_carry(["w_xo"]))

    invf, m1, m2 = _rope_consts()

    def rope_fwd(pos, q0, q1, q2, k0, k1, k2, v0, v1, v2, invf, m1, m2):
        narrow = _rope_tables(pos, invf, m1, m2)
        tabs = [_widen(t) for t in narrow]
        return tuple(_rope(t, *tabs) for t in (q0, q1, q2, k0, k1, k2)) + (v0, v1, v2) + tuple(narrow)

    qkv_cols = [(proj, ATT_GROUPW, 3 + i) for i in range(9)]
    qkv = _rowwise("rope", rope_fwd, [pos] + qkv_cols, [invf, m1, m2], [(ATT_GROUPW, mx)] * 9 + [(ROPE_TW, F32)] * 3)
    rope_tabs = qkv[9:]
    n_blocks = s // ATT_BLK
    groups = [(str(g), n_blocks // d, d) for g, d in enumerate(DILATIONS)]
    q_d = [_dilate(qkv[g], d) for g, d in enumerate(DILATIONS)]
    k_d = [_dilate(qkv[3 + g], d) for g, d in enumerate(DILATIONS)]
    v_d = [_dilate(qkv[6 + g], d) for g, d in enumerate(DILATIONS)]
    o_g, l_g = [], []
    for g, (tag, per_seq, d) in enumerate(groups):
        o, lse = _attn_fwd(tag, per_seq, q_d[g], k_d[g], v_d[g])
        o_g.append(_undilate(o, d))
        l_g.append(_undilate(lse, d))

    def merge(o0, o1, o2, l0, l1, l2):
        m = jnp.maximum(jnp.maximum(l0, l1), l2)
        e0, e1, e2 = jnp.exp(l0 - m), jnp.exp(l1 - m), jnp.exp(l2 - m)
        tot = e0 + e1 + e2

        def per_dim(e):
            w = e / tot
            return jnp.concatenate([w[:, h * 128:h * 128 + ATT_HEAD_DIM] for h in range(ATT_HPG)], axis=1)

        att = per_dim(e0) * o0 + per_dim(e1) * o1 + per_dim(e2) * o2
        lse = m + jnp.log(tot)
        return att, att, lse, _stat_rows(lse)

    att, attb, lse_tot, lse_tot_t = _rowwise("attn_merge", merge, o_g + l_g, [],
                                             [(ATT_GROUPW, F32), (ATT_GROUPW, mx), (ATT_STATW, F32)], touts=[(8, F32)])
    batt = _mm("att_up", attb, ex.weight("w_att_up"), "nn")

    gate_rows = [(proj, D_MODEL, 3), (proj, D_MODEL, 4), (z, D_MODEL, 0), (z, D_MODEL, 1), batt]
    mixedb, = _rowwise("gate", _gate, gate_rows, [], [(D_MODEL, mx)])
    o1 = _mm("mix_out", mixedb, ex.weight("w_mix_out"), "nn", bias=sp["b_mix_out"])
    h1, h1b = _rowwise("ln1", lambda h, o, g, b: (lambda r: (r, r))(_res_ln(h, o, g, b)), [h0, o1],
                       [sp["ln1_g"], sp["ln1_b"]], [(D_MODEL, F32), (D_MODEL, mx)])

    qx = _mm("xq", h1b, ex.weight("w_xq"), "nn", out_dtypes=(mx,))
    kvx = _mm("xkv", mem, ex.weight("w_xkv"), "nn", out_dtypes=(mx,))
    oxb = _xattn_fwd(qx, kvx)
    o2 = _mm("xo", oxb, ex.weight("w_xo"), "nn")
    h2, h2b = _rowwise("ln2", lambda h, o, g, b: (lambda r: (r, r))(_res_ln(h, o, g, b)), [h1, o2],
                       [sp["ln2_g"], sp["ln2_b"]], [(D_MODEL, F32), (D_MODEL, mx)])

    a_ff, fb = _mm("ff1", h2b, ex.weight("w_ff1"), "nn", bias=sp["b_ff1"],
                   epilogue=lambda r: (r, jnp.square(jnp.maximum(r, 0.0))), out_dtypes=(F32, mx))
    o3 = _mm("ff2", fb, ex.weight("w_ff2"), "nn", bias=sp["b_ff2"])

    def loss_bwd(h2, o3, tgt, g, b):
        def f(h2, o3, g, b):
            h3 = _res_ln(h2, o3, g, b)
            return 0.5 * jnp.sum(jnp.mean(jnp.square(h3 - tgt), axis=-1))

        loss, vjp = jax.vjp(f, h2, o3, g, b)
        _, dr, dg, db = vjp(jnp.ones((), F32))
        return dr, dr, dg, db, _colsum(dr), jnp.full((1, 128), loss, F32)

    dr3, dr3b, g_ln3_g, g_ln3_b, g_b_ff2, loss = _rowwise(
        "loss_ln3_bwd", loss_bwd, [h2, o3, target], [sp["ln3_g"], sp["ln3_b"]],
        [(D_MODEL, F32), (D_MODEL, mx)], [D_MODEL, D_MODEL, D_MODEL, 128])

    dab, da_sums = _mm("ff2_dx", dr3b, ex.weight("w_ff2"), "nt", extras=(a_ff,),
                       epilogue=lambda r, a: (r * (2.0 * jnp.maximum(a, 0.0)),), out_dtypes=(mx,), colsum=True)
    g_b_ff1 = jnp.sum(da_sums, axis=0)
    ex.grad("w_ff2", _mm("ff2_dw", fb, dr3b, "tn"))
    ex.grad("w_ff1", _mm("ff1_dw", h2b, dab, "tn", carry=ex.carry(swap=["w_ff2"])))
    dh2 = _mm("ff1_dx", dab, ex.weight("w_ff1"), "nt", extras=(dr3,), epilogue=lambda r, d: (r + al * d,),
              carry=ex.carry(swap=["w_ff1"], ici=["w_ff2"]))

    def ln_bwd(h, o, dout, g, b):
        _, vjp = jax.vjp(_res_ln, h, o, g, b)
        _, dr, dg, db = vjp(dout)
        return dr, dr, dg, db, _colsum(dr)

    dr2, dr2b, g_ln2_g, g_ln2_b, _ = _rowwise(
        "ln2_bwd", ln_bwd, [h1, o2, dh2], [sp["ln2_g"], sp["ln2_b"]],
        [(D_MODEL, F32), (D_MODEL, mx)], [D_MODEL, D_MODEL, D_MODEL])
    ex.grad("w_xo", _mm("xo_dw", oxb, dr2b, "tn"))
    doxb = _mm("xo_dx", dr2b, ex.weight("w_xo"), "nt", out_dtypes=(mx,), carry=ex.carry(swap=["w_xo"]))
    dqxb, dkvx = _xattn_bwd(qx, kvx, doxb)
    ex.grad("w_xq", _mm("xq_dw", h1b, dqxb, "tn", carry=ex.carry(ici=["w_xo"])))
    dh1 = _mm("xq_dx", dqxb, ex.weight("w_xq"), "nt", extras=(dr2,), epilogue=lambda r, d: (r + al * d,),
              carry=ex.carry(swap=["w_xq"]))
    ex.grad("w_xkv", _mm("xkv_dw", mem, dkvx, "tn"))

    dr1, dr1b, g_ln1_g, g_ln1_b, g_b_mix = _rowwise(
        "ln1_bwd", ln_bwd, [h0, o1, dh1], [sp["ln1_g"], sp["ln1_b"]],
        [(D_MODEL, F32), (D_MODEL, mx)], [D_MODEL, D_MODEL, D_MODEL])
    ex.grad("w_mix_out", _mm("mix_dw", mixedb, dr1b, "tn", carry=ex.carry(swap=["w_xkv"], ici=["w_xq"])))
    dmixed = _mm("mix_dx", dr1b, ex.weight("w_mix_out"), "nt", carry=ex.carry(swap=["w_mix_out"]))

    def gate_bwd(gs, ga, z1, z2, batt, dm):
        _, vjp = jax.vjp(_gate, gs, ga, z1, z2, batt)
        dgs, dga, dz1, dz2, dbatt = vjp(dm)
        dz = jnp.concatenate([dz1, dz2], axis=-1)
        return dgs, dga, dz, dbatt, _colsum(dz)

    dgsb, dgab, dzb, dbattb, g_b_glu = _rowwise(
        "gate_bwd", gate_bwd, gate_rows + [dmixed], [],
        [(D_MODEL, mx), (D_MODEL, mx), (2 * D_MODEL, mx), (D_MODEL, mx)], [2 * D_MODEL])
    ex.grad("w_att_up", _mm("att_up_dw", attb, dbattb, "tn", carry=ex.carry(ici=["w_mix_out"])))
    datt = _mm("att_up_dx", dbattb, ex.weight("w_att_up"), "nt", carry=ex.carry(swap=["w_att_up"]))

    def att_delta(datt, att, hs):
        dl = jnp.dot(datt * att, hs, precision=lax.Precision.HIGHEST, preferred_element_type=F32)
        return datt, dl, _stat_rows(dl)

    dattb, delta, delta_t = _rowwise("attn_delta", att_delta, [datt, att], [_head_sum_matrix()],
                                     [(ATT_GROUPW, mx), (ATT_STATW, F32)], touts=[(8, F32)])
    dq_g, dk_g, dv_g = [], [], []
    for g, (tag, per_seq, d) in enumerate(groups):
        do_d, lt_d, dl_d = _dilate(dattb, d), _dilate(lse_tot, d), _dilate(delta, d)
        dq_g.append(_undilate(_attn_dq(tag, per_seq, q_d[g], k_d[g], v_d[g], do_d, lt_d, dl_d), d))
        dk, dv = _attn_dkv(tag, per_seq, q_d[g], k_d[g], v_d[g], do_d, _dilate_rows(lse_tot_t, d), _dilate_rows(delta_t, d))
        dk_g.append(_undilate(dk, d))
        dv_g.append(_undilate(dv, d))
    dqkv = dq_g + dk_g + dv_g

    def rope_bwd(q0, q1, q2, k0, k1, k2, v0, v1, v2, cos, s_up, s_dn):
        tabs = [_widen(t) for t in (cos, s_up, s_dn)]
        return jnp.concatenate([_rope_t(t, *tabs) for t in (q0, q1, q2, k0, k1, k2)] + [v0, v1, v2], axis=-1)

    dqkvb, = _rowwise("rope_bwd", rope_bwd, dqkv + list(rope_tabs), [], [(9 * ATT_GROUPW, mx)])

    ex.grad("w_glu", _mm("glu_dw", ygb, dzb, "tn", carry=ex.carry(ici=["w_xkv", "w_att_up"])))
    dyg = _mm("glu_dx", dzb, ex.weight("w_glu"), "nt", carry=ex.carry(swap=["w_glu"]))

    def gelu_bwd(y, dyg):
        _, vjp = jax.vjp(jax.nn.gelu, y)
        return vjp(dyg)[0]

    dy, = _rowwise("gelu_bwd", gelu_bwd, [y, dyg], [], [(SSM_WIDTH, F32)])
    dy_p = _time_perm(dy)
    s_re, s_im, du_p = _ssm_scan("ssm_scan_bwd", dy_p, c12, b12, a_re, a_im, sp["ssm_d"], reverse=True,
                                 carry=ex.carry(ici=["w_ff1", "w_glu"]))
    g_bexp, g_cexp, d_abr, d_abi = _ssm_wgrads(u_p, dy_p, s_re, s_im, h_re, h_im)
    g_ssm_d, = _rowwise("ssm_dd", lambda a, b: (_colsum(a * b),), [dy_p, u_p], [], [], [SSM_WIDTH])
    g_ldt, g_are, g_aim, g_bre, g_bim = _ssm_disc_bwd(
        ldt, are, aim, bre, bim, d_abr.reshape(N_STATE, 1), d_abi.reshape(N_STATE, 1),
        _diag_of_b(g_bexp[:, :, :CH_N]), _diag_of_b(g_bexp[:, :, CH_N:]))
    g_c_re = _diag_of_c(g_cexp[:, :CH_N, :])
    g_c_im = -_diag_of_c(g_cexp[:, CH_N:, :])

    def assemble(du, dqkv, dgs, dga):
        row = jnp.concatenate([du.astype(mx), dqkv, dgs, dga], axis=-1)
        return row, _colsum(row)

    dprojb, g_b_in = _rowwise("in_assemble", assemble, [_time_unperm(du_p), dqkvb, dgsb, dgab], [],
                              [(IN_COLS, mx)], [IN_COLS])
    ex.grad("w_in", _mm("in_dw", h0b, dprojb, "tn"))
    dh0 = _mm("in_dx", dprojb, ex.weight("w_in"), "nt", extras=(dr1,), epilogue=lambda r, d: (r + al * d,),
              carry=ex.carry(ici=["w_in"]))

    def ln_in_bwd(x, dout, g, b):
        _, vjp = jax.vjp(_ln, x, g, b)
        return vjp(dout)

    dx, g_ln_in_g, g_ln_in_b = _rowwise("ln_in_bwd", ln_in_bwd, [x, dh0], [sp["ln_in_g"], sp["ln_in_b"]],
                                        [(D_MODEL, F32)], [D_MODEL, D_MODEL], carry=ex.finish_carry())

    small = {"ln_in_g": g_ln_in_g, "ln_in_b": g_ln_in_b, "b_in": g_b_in, "ssm_log_dt": g_ldt, "ssm_a_re": g_are,
             "ssm_a_im": g_aim, "ssm_b_re": g_bre, "ssm_b_im": g_bim, "ssm_c_re": g_c_re, "ssm_c_im": g_c_im,
             "ssm_d": g_ssm_d, "b_glu": g_b_glu, "b_mix_out": g_b_mix, "ln1_g": g_ln1_g, "ln1_b": g_ln1_b,
             "ln2_g": g_ln2_g, "ln2_b": g_ln2_b, "b_ff1": g_b_ff1, "b_ff2": g_b_ff2, "ln3_g": g_ln3_g,
             "ln3_b": g_ln3_b}
    return loss, dx, small


def _piece_shape(k, n, axis):
    return (k // 2, n // 4) if axis == 1 else (k // 8, n)


def _aligned(v, m):
    return v if isinstance(v, int) else pl.multiple_of(v, m)


def _full_piece(ref, k, n, axis, chip, half):
    pr, pc = _piece_shape(k, n, axis)
    if axis == 1:
        return ref.at[pl.ds(_aligned(half * pr, 8), pr), pl.ds(_aligned(chip * pc, 128), pc)]
    return ref.at[pl.ds(_aligned(chip * (2 * pr) + half * pr, 8), pr), :]


def _full_shard(ref, k, n, axis, chip):
    if axis == 1:
        return ref.at[:, pl.ds(_aligned(chip * (n // 4), 128), n // 4)]
    return ref.at[pl.ds(_aligned(chip * (k // 4), 8), k // 4), :]


def _shard_piece(ref, k, n, axis, half):
    pr, _ = _piece_shape(k, n, axis)
    return ref.at[pl.ds(_aligned(half * pr, 8), pr), :]


def _mesh_pos():
    x, y, c = lax.axis_index("x"), lax.axis_index("y"), lax.axis_index("c")
    other_chips = [(1 - x, y), (x, 1 - y), (1 - x, 1 - y)]
    return x, y, c, other_chips


def _remote(src, dst, send_sem, recv_sem, dev):
    return pltpu.make_async_remote_copy(src_ref=src, dst_ref=dst, send_sem=send_sem, recv_sem=recv_sem,
                                        device_id=dev, device_id_type=MESH)


def _placed(name, fn, n_steps, where, ins, out_sds, out_block, out_index):
    def body(w_ref, *refs):
        o_ref = refs[-1]
        o_ref[...] = fn(*[r[...] for r in refs[:-1]]).astype(o_ref.dtype)

    grid_spec = pltpu.PrefetchScalarGridSpec(
        num_scalar_prefetch=1, grid=(n_steps,), in_specs=[pl.BlockSpec(bs, idx) for _, bs, idx in ins],
        out_specs=pl.BlockSpec(out_block, out_index))
    return pl.pallas_call(body, name=name, grid_spec=grid_spec, out_shape=out_sds,
                          compiler_params=_cparams(1))(where, *[a for a, _, _ in ins])


def _gather_copies(widx):
    geo = [BIG[i][1:] for i in widx]

    def ici(full, wi, j, chip, send_sems, recv_sems, c, dev):
        k, n, ax = geo[wi]
        piece = _full_piece(full[wi], k, n, ax, chip, c)
        return _remote(piece, piece, send_sems.at[wi * 6 + j], recv_sems.at[wi * 6 + j], dev)

    def d2d(full, wi, j, chip, half, send_sems, recv_sems, sib):
        k, n, ax = geo[wi]
        piece = _full_piece(full[wi], k, n, ax, chip, half)
        return _remote(piece, piece, send_sems.at[wi * 6 + 3 + j], recv_sems.at[wi * 6 + 3 + j], sib)

    def start(_, full, send_sems, recv_sems):
        x, y, c, chips = _mesh_pos()
        for wi in range(len(geo)):
            for j, (qx, qy) in enumerate(chips):
                ici(full, wi, j, 2 * x + y, send_sems, recv_sems, c, (qx, qy, c)).start()

    def finish(_, full, send_sems, recv_sems):
        x, y, c, chips = _mesh_pos()
        sib = (x, y, 1 - c)
        for wi in range(len(geo)):
            for j, (qx, qy) in enumerate(chips):
                ici(full, wi, j, 2 * qx + qy, send_sems, recv_sems, c, (qx, qy, c)).wait_recv()
                d2d(full, wi, j, 2 * qx + qy, c, send_sems, recv_sems, sib).start()
        for wi in range(len(geo)):
            for j, (qx, qy) in enumerate(chips):
                d2d(full, wi, j, 2 * qx + qy, 1 - c, send_sems, recv_sems, sib).wait_recv()
        for wi in range(len(geo)):
            for j, (qx, qy) in enumerate(chips):
                ici(full, wi, j, 2 * x + y, send_sems, recv_sems, c, (qx, qy, c)).wait_send()
                d2d(full, wi, j, 2 * qx + qy, c, send_sems, recv_sems, sib).wait_send()

    return start, finish, 6 * len(geo)


def _gather_weights(tag, fulls, widx):
    nw = len(widx)
    start, finish, n_sems = _gather_copies(widx)

    def body(*refs):
        full = refs[nw:2 * nw]
        start(None, full, *refs[2 * nw:])
        finish(None, full, *refs[2 * nw:])

    return pl.pallas_call(
        body, name="gather_weights_" + tag, in_specs=[HBM_SPEC] * nw, out_specs=[HBM_SPEC] * nw,
        out_shape=[jax.ShapeDtypeStruct(f.shape, f.dtype) for f in fulls],
        input_output_aliases={i: i for i in range(nw)},
        scratch_shapes=[pltpu.SemaphoreType.DMA((n_sems,)), pltpu.SemaphoreType.DMA((n_sems,))])(*fulls)


def _swap_copies(widx):
    geo = [BIG[i][1:] for i in widx]

    def copies(g, got, send_sems, recv_sems, base):
        x, y, c, _ = _mesh_pos()
        return [_remote(_full_piece(g[wi], k, n, ax, q, 1 - c), got[wi].at[q], send_sems.at[base + wi * 4 + q],
                        recv_sems.at[base + wi * 4 + q], (x, y, 1 - c))
                for wi, (k, n, ax) in enumerate(geo) for q in range(4)]

    def start(g, got, send_sems, recv_sems, base=0):
        for cp in copies(g, got, send_sems, recv_sems, base):
            cp.start()

    def finish(g, got, send_sems, recv_sems, base=0):
        for cp in copies(g, got, send_sems, recv_sems, base):
            cp.wait()

    return start, finish, 4 * len(geo)


def _swap_shapes(widx):
    return [jax.ShapeDtypeStruct((4,) + _piece_shape(*BIG[i][1:]), F32) for i in widx]


def _reduce_swap_halves(tag, grads, widx):
    nw = len(widx)
    start, finish, n_sems = _swap_copies(widx)

    def body(*refs):
        start(refs[:nw], refs[nw:2 * nw], *refs[2 * nw:])
        finish(refs[:nw], refs[nw:2 * nw], *refs[2 * nw:])

    return pl.pallas_call(
        body, name="reduce_swap_halves_" + tag, in_specs=[HBM_SPEC] * nw, out_specs=[HBM_SPEC] * nw,
        out_shape=_swap_shapes(widx),
        scratch_shapes=[pltpu.SemaphoreType.DMA((n_sems,)), pltpu.SemaphoreType.DMA((n_sems,))])(*grads)


def _owner_copies(nw):
    def copies(p, out, send_sems, recv_sems, base):
        x, y, c, chips = _mesh_pos()
        return [_remote(p[wi].at[2 * qx + qy], out[wi].at[j], send_sems.at[base + wi * 3 + j],
                        recv_sems.at[base + wi * 3 + j], (qx, qy, c))
                for wi in range(nw) for j, (qx, qy) in enumerate(chips)]

    def start(p, out, send_sems, recv_sems, base=0):
        for cp in copies(p, out, send_sems, recv_sems, base):
            cp.start()

    def finish(p, out, send_sems, recv_sems, base=0):
        for cp in copies(p, out, send_sems, recv_sems, base):
            cp.wait()

    return start, finish, 3 * nw


def _join_carries(a, b):
    if a is None or b is None:
        return a if b is None else b
    n_i, n_o = len(a.ins), len(a.outs)
    outs = list(a.outs) + [o + n_i if isinstance(o, int) else o for o in b.outs]

    def start(c_in, c_out, send_sems, recv_sems):
        a.start(c_in[:n_i], c_out[:n_o], send_sems, recv_sems)
        b.start(c_in[n_i:], c_out[n_o:], send_sems, recv_sems, base=a.n_sems)

    def finish(c_in, c_out, send_sems, recv_sems):
        a.finish(c_in[:n_i], c_out[:n_o], send_sems, recv_sems)
        b.finish(c_in[n_i:], c_out[n_o:], send_sems, recv_sems, base=a.n_sems)

    def done(res):
        a.done(res[:n_o])
        b.done(res[n_o:])

    return _Carry(a.ins + b.ins, outs, a.n_sems + b.n_sems, start, finish, done)


def _share_copies():
    def copy(out, wi, half, send_sems, recv_sems, sib):
        _, k, n, ax = BIG[wi]
        piece = _shard_piece(out[wi], k, n, ax, half)
        return _remote(piece, piece, send_sems.at[wi], recv_sems.at[wi], sib)

    def start(_, out, send_sems, recv_sems):
        x, y, c, _ = _mesh_pos()
        for wi in range(len(BIG)):
            copy(out, wi, c, send_sems, recv_sems, (x, y, 1 - c)).start()

    def finish(_, out, send_sems, recv_sems):
        x, y, c, _ = _mesh_pos()
        for wi in range(len(BIG)):
            copy(out, wi, 1 - c, send_sems, recv_sems, (x, y, 1 - c)).wait_recv()
            copy(out, wi, c, send_sems, recv_sems, (x, y, 1 - c)).wait_send()

    return start, finish, len(BIG)


def _allreduce_small(v):
    r = v.shape[0]
    rh = r // 2
    assert rh % 8 == 0

    def body(v_ref, o_ref, sib_buf, chip_buf, send_sems, recv_sems):
        x, y, c, chips = _mesh_pos()
        me = 2 * x + y
        sib = (x, y, 1 - c)
        mine = pl.ds(pl.multiple_of(c * rh, 8), rh)
        other = pl.ds(pl.multiple_of((1 - c) * rh, 8), rh)
        swap = _remote(v_ref.at[other], sib_buf, send_sems.at[0], recv_sems.at[0], sib)
        swap.start()
        swap.wait()
        chip_buf[me] = v_ref[mine, :] + sib_buf[...]
        cps = []
        for j, (qx, qy) in enumerate(chips):
            cp = _remote(chip_buf.at[me], chip_buf.at[me], send_sems.at[1 + j], recv_sems.at[1 + j], (qx, qy, c))
            cp.start()
            cps.append(cp)
        for j, (qx, qy) in enumerate(chips):
            slot = chip_buf.at[2 * qx + qy]
            _remote(slot, slot, send_sems.at[1 + j], recv_sems.at[1 + j], (qx, qy, c)).wait_recv()
        for cp in cps:
            cp.wait_send()
        o_ref[mine, :] = ((chip_buf[0] + chip_buf[1]) + chip_buf[2]) + chip_buf[3]
        back = _remote(o_ref.at[mine], o_ref.at[mine], send_sems.at[4], recv_sems.at[4], sib)
        back.start()
        _remote(o_ref.at[other], o_ref.at[other], send_sems.at[4], recv_sems.at[4], sib).wait_recv()
        back.wait_send()

    return pl.pallas_call(
        body, name="allreduce_small", in_specs=[VMEM_SPEC], out_specs=VMEM_SPEC,
        out_shape=jax.ShapeDtypeStruct((r, 128), F32),
        scratch_shapes=[pltpu.VMEM((rh, 128), F32), pltpu.VMEM((4, rh, 128), F32),
                        pltpu.SemaphoreType.DMA((5,)), pltpu.SemaphoreType.DMA((5,))],
        compiler_params=pltpu.CompilerParams(vmem_limit_bytes=VMEM_LIMIT))(v)


def _as2d(a):
    a = a.reshape((-1, a.shape[-1])) if a.ndim > 1 else a.reshape(1, -1)
    return a


def _adamw_small(quads):
    n = len(quads)

    def body(*refs):
        for i in range(n):
            w, g, m, v = (r[...] for r in refs[4 * i:4 * i + 4])
            for ref, val in zip(refs[4 * n + 3 * i:4 * n + 3 * i + 3], _adamw(w, g, m, v)):
                ref[...] = val

    return pl.pallas_call(
        body, name="adamw_small", in_specs=[VMEM_SPEC] * (4 * n), out_specs=[VMEM_SPEC] * (3 * n),
        out_shape=[jax.ShapeDtypeStruct(q[0].shape, F32) for q in quads for _ in range(3)],
        compiler_params=pltpu.CompilerParams(vmem_limit_bytes=VMEM_LIMIT))(*[a for q in quads for a in q])


def _where():
    return jnp.stack([2 * lax.axis_index("x") + lax.axis_index("y"), lax.axis_index("c")]).astype(jnp.int32)


_BIG_INDEX = {name: i for i, (name, _, _, _) in enumerate(BIG)}


class _LocalWeights:
    def __init__(self, weights):
        self.weights, self.grads = weights, {}

    def gather_now(self, names):
        pass

    def gather_carry(self, names):
        return None

    def weight(self, name):
        return self.weights[name]

    def grad(self, name, g):
        self.grads[name] = g

    def carry(self, swap=(), ici=()):
        return None

    def finish_carry(self):
        return None


class _Exchange:
    def __init__(self, inputs, where):
        self.inputs, self.where = inputs, where
        self.full, self.ready = {}, set()
        self.raw, self.got, self.parts, self.landed, self.geom = {}, {}, {}, {}, {}
        for name, k, n, ax in BIG:
            w2 = inputs[name][0]
            rs, cs = w2.shape
            tm = _tile(rs, 512)
            steps = rs // tm
            if ax == 1:
                blk, idx = (tm, cs), lambda i, w: (i, w[0])
            else:
                blk, idx = (tm, n), functools.partial(lambda i, w, steps: (w[0] * steps + i, 0), steps=steps)
            self.full[name] = _placed("cast_" + name, lambda w: w, steps, where, [(w2, (tm, cs), lambda i, w: (i, 0))],
                                      jax.ShapeDtypeStruct((k, n), MXU_DTYPE), blk, idx)

    def _gathered(self, names, outs):
        for name, o in zip(names, outs):
            self.full[name] = o
            self.ready.add(name)

    def gather_now(self, names):
        self._gathered(names, _gather_weights(names[0], [self.full[n] for n in names], [_BIG_INDEX[n] for n in names]))

    def gather_carry(self, names):
        start, finish, n_sems = _gather_copies([_BIG_INDEX[n] for n in names])
        return _Carry([self.full[n] for n in names], list(range(len(names))), n_sems, start, finish,
                      functools.partial(self._gathered, names))

    def weight(self, name):
        assert name in self.ready, name
        return self.full[name]

    def grad(self, name, g):
        self.raw[name] = g

    def _swapped(self, names, outs):
        for name, o in zip(names, outs):
            self.got[name] = o

    def _pair_sum(self, name):
        i = _BIG_INDEX[name]
        _, k, n, ax = BIG[i]
        g = self.raw[name]
        if name not in self.got:
            self._swapped([name], _reduce_swap_halves(name, [g], [i]))
        got = self.got[name]
        pr, pc = _piece_shape(k, n, ax)
        tm = _tile(pr, 512)
        spp = pr // tm
        self.geom[name] = (pr, pc, tm, spp)
        if ax == 1:
            g_idx = functools.partial(lambda i, w, spp: (w[1] * spp + i % spp, i // spp), spp=spp)
        else:
            g_idx = functools.partial(lambda i, w, spp: ((i // spp) * 2 * spp + w[1] * spp + i % spp, 0), spp=spp)
        self.parts[name] = _placed(
            "pair_sum_" + name, lambda a, b: a + b, 4 * spp, self.where,
            [(g, (tm, pc), g_idx), (got.reshape(4 * pr, pc), (tm, pc), lambda i, w: (i, 0))],
            jax.ShapeDtypeStruct((4 * pr, pc), BF16), (tm, pc), lambda i, w: (i, 0)).reshape(4, pr, pc)

    def _landed(self, names, outs):
        for name, o in zip(names, outs):
            self.landed[name] = o

    def carry(self, swap=(), ici=()):
        first = second = None
        if swap:
            widx = [_BIG_INDEX[n] for n in swap]
            start, finish, n_sems = _swap_copies(widx)
            first = _Carry([self.raw[n] for n in swap], _swap_shapes(widx), n_sems, start, finish,
                           functools.partial(self._swapped, list(swap)))
        if ici:
            for n in ici:
                self._pair_sum(n)
            start, finish, n_sems = _owner_copies(len(ici))
            parts = [self.parts[n] for n in ici]
            outs = [jax.ShapeDtypeStruct((3,) + p.shape[1:], p.dtype) for p in parts]
            second = _Carry(parts, outs, n_sems, start, finish, functools.partial(self._landed, list(ici)))
        return _join_carries(first, second)

    def _shared(self, outs):
        self.shards = dict(zip([b[0] for b in BIG], outs))

    def finish_carry(self):
        halves = []
        for name, _, _, _ in BIG:
            pr, pc, tm, spp = self.geom[name]
            ins = [(self.parts[name], (None, tm, pc), lambda i, w: (w[0], i, 0))]
            ins += [(self.landed[name], (None, tm, pc), functools.partial(lambda i, w, j: (j, i, 0), j=j))
                    for j in range(3)]
            halves.append(_placed("chip_sum_" + name,
                                  lambda a, b, c, d: ((a.astype(F32) + b.astype(F32)) + c.astype(F32)) + d.astype(F32),
                                  spp, self.where, ins, jax.ShapeDtypeStruct(self.inputs[name].shape[1:], F32), (tm, pc),
                                  functools.partial(lambda i, w, spp: (w[1] * spp + i, 0), spp=spp)))
        start, finish, n_sems = _share_copies()
        return _Carry(halves, list(range(len(halves))), n_sems, start, finish, self._shared)


def _step(inputs):
    x, mem, positions, target = inputs["x"][0], inputs["mem"][0], inputs["positions"], inputs["loss_target"][0]
    pos = positions.reshape(-1, 1)
    ex = _Exchange(inputs, _where())
    sp = {name: _as2d(inputs[name]) for name in SMALL}
    memb, = _rowwise("cast_mem", lambda m: (m,), [mem], [], [(D_MODEL, MXU_DTYPE)])

    loss, dx, gsmall = _local_step(x, memb, pos, target, sp, ex)
    gshard = ex.shards

    out = {}
    for name, _, _, _ in BIG:
        w2, m2, v2 = inputs[name][0], inputs["m_" + name][0], inputs["v_" + name][0]
        n = w2.shape[1]
        d, nm, nv = _rowwise("adamw_" + name, _adamw, [w2, gshard[name], m2, v2], [], [(n, F32)] * 3, tm=256)
        lead = inputs[name].shape
        out[name] = (gshard[name].reshape(lead), d.reshape(lead), nm.reshape(lead), nv.reshape(lead))

    def tiles(a):
        flat = a.reshape(-1)
        n = -(-flat.shape[0] // 1024) * 1024
        return jnp.pad(flat, (0, n - flat.shape[0])).reshape(n // 128, 128)

    pieces = [tiles(loss[:, :1])] + [tiles(gsmall[name]) for name in SMALL]
    if sum(p.shape[0] for p in pieces) % 16:
        pieces.append(jnp.zeros((8, 128), F32))
    red = _allreduce_small(jnp.concatenate(pieces, axis=0))
    loss_total = red[0, 0]
    grads, off = {}, pieces[0].shape[0]
    for name, p in zip(SMALL, pieces[1:]):
        shp = _as2d(inputs[name]).shape
        grads[name] = red[off:off + p.shape[0]].reshape(-1)[:shp[0] * shp[1]].reshape(shp)
        off += p.shape[0]
    upd = _adamw_small([(_as2d(inputs[n]), grads[n], _as2d(inputs["m_" + n]), _as2d(inputs["v_" + n])) for n in SMALL])
    for i, name in enumerate(SMALL):
        shp = inputs[name].shape
        out[name] = (grads[name].reshape(shp),) + tuple(t.reshape(shp) for t in upd[3 * i:3 * i + 3])
    return loss_total, dx.reshape(inputs["x"].shape), out


_ARG_NAMES = (("x", "mem", "positions") + WEIGHT_ORDER + ("loss_target",) + tuple("m_" + n for n in WEIGHT_ORDER)
              + tuple("v_" + n for n in WEIGHT_ORDER))


def kernel(x, mem, positions, ln_in_g, ln_in_b, w_in, b_in, ssm_log_dt, ssm_a_re, ssm_a_im, ssm_b_re, ssm_b_im, ssm_c_re, ssm_c_im, ssm_d, w_glu, b_glu, w_att_up, w_mix_out, b_mix_out, ln1_g, ln1_b, w_xq, w_xkv, w_xo, ln2_g, ln2_b, w_ff1, b_ff1, w_ff2, b_ff2, ln3_g, ln3_b, loss_target, m_ln_in_g, m_ln_in_b, m_w_in, m_b_in, m_ssm_log_dt, m_ssm_a_re, m_ssm_a_im, m_ssm_b_re, m_ssm_b_im, m_ssm_c_re, m_ssm_c_im, m_ssm_d, m_w_glu, m_b_glu, m_w_att_up, m_w_mix_out, m_b_mix_out, m_ln1_g, m_ln1_b, m_w_xq, m_w_xkv, m_w_xo, m_ln2_g, m_ln2_b, m_w_ff1, m_b_ff1, m_w_ff2, m_b_ff2, m_ln3_g, m_ln3_b, v_ln_in_g, v_ln_in_b, v_w_in, v_b_in, v_ssm_log_dt, v_ssm_a_re, v_ssm_a_im, v_ssm_b_re, v_ssm_b_im, v_ssm_c_re, v_ssm_c_im, v_ssm_d, v_w_glu, v_b_glu, v_w_att_up, v_w_mix_out, v_b_mix_out, v_ln1_g, v_ln1_b, v_w_xq, v_w_xkv, v_w_xo, v_ln2_g, v_ln2_b, v_w_ff1, v_b_ff1, v_w_ff2, v_b_ff2, v_ln3_g, v_ln3_b):
    args = (x, mem, positions, ln_in_g, ln_in_b, w_in, b_in, ssm_log_dt, ssm_a_re, ssm_a_im, ssm_b_re, ssm_b_im, ssm_c_re, ssm_c_im, ssm_d, w_glu, b_glu, w_att_up, w_mix_out, b_mix_out, ln1_g, ln1_b, w_xq, w_xkv, w_xo, ln2_g, ln2_b, w_ff1, b_ff1, w_ff2, b_ff2, ln3_g, ln3_b, loss_target, m_ln_in_g, m_ln_in_b, m_w_in, m_b_in, m_ssm_log_dt, m_ssm_a_re, m_ssm_a_im, m_ssm_b_re, m_ssm_b_im, m_ssm_c_re, m_ssm_c_im, m_ssm_d, m_w_glu, m_b_glu, m_w_att_up, m_w_mix_out, m_b_mix_out, m_ln1_g, m_ln1_b, m_w_xq, m_w_xkv, m_w_xo, m_ln2_g, m_ln2_b, m_w_ff1, m_b_ff1, m_w_ff2, m_b_ff2, m_ln3_g, m_ln3_b, v_ln_in_g, v_ln_in_b, v_w_in, v_b_in, v_ssm_log_dt, v_ssm_a_re, v_ssm_a_im, v_ssm_b_re, v_ssm_b_im, v_ssm_c_re, v_ssm_c_im, v_ssm_d, v_w_glu, v_b_glu, v_w_att_up, v_w_mix_out, v_b_mix_out, v_ln1_g, v_ln1_b, v_w_xq, v_w_xkv, v_w_xo, v_ln2_g, v_ln2_b, v_w_ff1, v_b_ff1, v_w_ff2, v_b_ff2, v_ln3_g, v_ln3_b)
    assert len(args) == len(_ARG_NAMES)
    inputs = dict(zip(_ARG_NAMES, args))
    loss, dx, out = _step(inputs)
    res = [loss, dx]
    for k in range(4):
        res += [out[name][k] for name in WEIGHT_ORDER]
    return tuple(res)
```

```python
import functools
import math

import numpy as np
import jax
import jax.numpy as jnp
from jax import lax
from jax.experimental import pallas as pl
from jax.experimental.pallas import tpu as pltpu

F32 = jnp.float32
BF16 = jnp.bfloat16
MXU_DTYPE = jnp.bfloat16

D_MODEL = 1024
SSM_GROUP = 16
SSM_WIDTH = 768
SSM_GROUPS = 48
SSM_STATE = 64
N_STATE = SSM_GROUPS * SSM_STATE
SSM_CHUNKS = 6
CH_W = 128
CH_N = 512
ATT_HEAD_DIM = 64
ATT_HPG = 4
ATT_GROUPW = ATT_HPG * ATT_HEAD_DIM
DILATIONS = (1, 4, 16)
ATT_BLK = 128
ATT_SCALE = ATT_HEAD_DIM ** -0.5
ROT_DIM = 16
ROPE_THETA = 500000.0
XATT_HEADS = 4
XATT_HEAD_DIM = 256
XATT_SCALE = XATT_HEAD_DIM ** -0.5
D_FF = 4096
IN_COLS = 5120
DEEPNORM_ALPHA = 2.0 ** 0.25
LN_EPS = 1e-5
NEG_INF = -1e30
ADAM_LR = 0.001
ADAM_B1 = 0.9
ADAM_B2 = 0.999
ADAM_EPS = 1e-08
ADAM_WD = 0.01
ADAM_STEP = 10

N_SEG = 32
VMEM_LIMIT = 56 * 1024 * 1024
MESH = pl.DeviceIdType.MESH
HBM_SPEC = pl.BlockSpec(memory_space=pltpu.HBM)
VMEM_SPEC = pl.BlockSpec(memory_space=pltpu.VMEM)

BIG = (("w_in", 1024, 5120, 1), ("w_glu", 768, 2048, 1), ("w_att_up", 256, 1024, 1),
       ("w_mix_out", 1024, 1024, 0), ("w_xq", 1024, 1024, 0), ("w_xkv", 1024, 2048, 1),
       ("w_xo", 1024, 1024, 0), ("w_ff1", 1024, 4096, 1), ("w_ff2", 4096, 1024, 0))
SMALL = ("ln_in_g", "ln_in_b", "b_in", "ssm_log_dt", "ssm_a_re", "ssm_a_im", "ssm_b_re", "ssm_b_im",
         "ssm_c_re", "ssm_c_im", "ssm_d", "b_glu", "b_mix_out", "ln1_g", "ln1_b", "ln2_g", "ln2_b",
         "b_ff1", "b_ff2", "ln3_g", "ln3_b")
WEIGHT_ORDER = ("ln_in_g", "ln_in_b", "w_in", "b_in", "ssm_log_dt", "ssm_a_re", "ssm_a_im", "ssm_b_re",
                "ssm_b_im", "ssm_c_re", "ssm_c_im", "ssm_d", "w_glu", "b_glu", "w_att_up", "w_mix_out",
                "b_mix_out", "ln1_g", "ln1_b", "w_xq", "w_xkv", "w_xo", "ln2_g", "ln2_b", "w_ff1", "b_ff1",
                "w_ff2", "b_ff2", "ln3_g", "ln3_b")


def _cparams(n_axes):
    return pltpu.CompilerParams(dimension_semantics=("arbitrary",) * n_axes, vmem_limit_bytes=VMEM_LIMIT)


class _Carry:
    def __init__(self, ins, outs, n_sems, start, finish, done):
        self.ins, self.outs, self.n_sems, self.start, self.finish, self.done = ins, outs, n_sems, start, finish, done


def _call(name, body, grid, in_specs, out_specs, out_shape, args, scratch_shapes=(), carry=None):
    in_specs, out_specs, out_shape = list(in_specs), list(out_specs), list(out_shape)
    params = _cparams(len(grid))
    if carry is None:
        return pl.pallas_call(body, name=name, grid=grid, in_specs=in_specs, out_specs=out_specs, out_shape=out_shape,
                              scratch_shapes=list(scratch_shapes), compiler_params=params)(*args)
    n_in, n_out, n_ci, n_co = len(in_specs), len(out_specs), len(carry.ins), len(carry.outs)
    n_scr = len(scratch_shapes)

    def wrapped(*refs):
        ins, c_in = refs[:n_in], refs[n_in:n_in + n_ci]
        outs, c_out = refs[n_in + n_ci:n_in + n_ci + n_out], refs[n_in + n_ci + n_out:n_in + n_ci + n_out + n_co]
        scratch = refs[n_in + n_ci + n_out + n_co:n_in + n_ci + n_out + n_co + n_scr]
        send_sems, recv_sems = refs[-2:]
        ids = [pl.program_id(a) for a in range(len(grid))]
        first = functools.reduce(jnp.logical_and, [i == 0 for i in ids])
        last = functools.reduce(jnp.logical_and, [i == g - 1 for i, g in zip(ids, grid)])

        @pl.when(first)
        def _():
            carry.start(c_in, c_out, send_sems, recv_sems)

        body(*ins, *outs, *scratch)

        @pl.when(last)
        def _():
            carry.finish(c_in, c_out, send_sems, recv_sems)

    c_shapes = [jax.ShapeDtypeStruct(carry.ins[o].shape, carry.ins[o].dtype) if isinstance(o, int) else o
                for o in carry.outs]
    aliases = {n_in + o: n_out + i for i, o in enumerate(carry.outs) if isinstance(o, int)}
    res = pl.pallas_call(
        wrapped, name=name, grid=grid, in_specs=in_specs + [HBM_SPEC] * n_ci, out_specs=out_specs + [HBM_SPEC] * n_co,
        out_shape=out_shape + c_shapes, input_output_aliases=aliases,
        scratch_shapes=list(scratch_shapes) + [pltpu.SemaphoreType.DMA((carry.n_sems,))] * 2,
        compiler_params=params)(*args, *carry.ins)
    carry.done(res[n_out:])
    return res[:n_out]


def _rowwise(name, fn, rows, consts, outs, reds=(), tm=512, touts=(), carry=None):
    n_rows = (rows[0][0] if isinstance(rows[0], tuple) else rows[0]).shape[-2]
    tm = min(tm, n_rows)
    assert n_rows % tm == 0, (name, n_rows, tm)
    specs, args = [], []
    for r in rows:
        if isinstance(r, tuple) and len(r) == 3:
            arr, width, cb = r
            specs.append(pl.BlockSpec((tm, width), functools.partial(lambda i, cb: (i, cb), cb=cb)))
        elif isinstance(r, tuple):
            arr, slot = r
            specs.append(pl.BlockSpec((None, tm, arr.shape[2]), functools.partial(lambda i, s: (s, i, 0), s=slot)))
        else:
            arr = r
            specs.append(pl.BlockSpec((tm, arr.shape[1]), lambda i: (i, 0)))
        args.append(arr)
        assert arr.shape[-2] == n_rows, (name, arr.shape, n_rows)
    for cst in consts:
        specs.append(pl.BlockSpec(cst.shape, lambda i: (0, 0)))
        args.append(cst)
    n_r, n_c, n_o, n_d = len(rows), len(consts), len(outs) + len(touts), len(reds)
    out_shape = [jax.ShapeDtypeStruct((n_rows, c), dt) for c, dt in outs]
    out_specs = [pl.BlockSpec((tm, c), lambda i: (i, 0)) for c, _ in outs]
    out_shape += [jax.ShapeDtypeStruct((r, n_rows), dt) for r, dt in touts]
    out_specs += [pl.BlockSpec((r, tm), lambda i: (0, i)) for r, _ in touts]
    out_shape += [jax.ShapeDtypeStruct((1, c), F32) for c in reds]
    out_specs += [pl.BlockSpec((1, c), lambda i: (0, 0)) for c in reds]

    def body(*refs):
        ins = [r[...] for r in refs[:n_r + n_c]]
        o_refs = refs[n_r + n_c:n_r + n_c + n_o]
        d_refs = refs[n_r + n_c + n_o:]
        res = fn(*ins)
        res = res if isinstance(res, (tuple, list)) else (res,)
        assert len(res) == n_o + n_d, (name, len(res))
        for ref, val in zip(o_refs, res[:n_o]):
            ref[...] = val.astype(ref.dtype)
        first = pl.program_id(0) == 0
        for ref, val in zip(d_refs, res[n_o:]):
            @pl.when(first)
            def _(ref=ref, val=val):
                ref[...] = val

            @pl.when(jnp.logical_not(first))
            def _(ref=ref, val=val):
                ref[...] += val

    return _call(name, body, (n_rows // tm,), specs, out_specs, out_shape, args, carry=carry)


def _colsum(v):
    return jnp.sum(v.astype(F32), axis=0, keepdims=True)


_DIMS = {"nn": (((1,), (0,)), ((), ())), "nt": (((1,), (1,)), ((), ())), "tn": (((0,), (0,)), ((), ()))}


def _tile(dim, want):
    if dim <= want:
        return dim
    return max(t for t in range(128, want + 1, 128) if dim % t == 0)


def _dot(a, b, mode):
    return lax.dot_general(a.astype(MXU_DTYPE), b.astype(MXU_DTYPE), _DIMS[mode], preferred_element_type=F32)


def _mm(name, a, b, mode, *, bias=None, extras=(), epilogue=None, out_dtypes=(F32,), tm=1024, tn=1024, tk=1024,
        carry=None, colsum=False):
    if mode == "nn":
        (m, k), (_, n) = a.shape, b.shape
    elif mode == "nt":
        (m, k), (n, _) = a.shape, b.shape
    else:
        (k, m), (_, n) = a.shape, b.shape
    if k > tk:
        tk = 5 * tk
    tn = _tile(n, tn)
    tk = _tile(k, tk)
    nk = k // tk

    def vmem_bytes(rows):
        blocks = rows * tk * a.dtype.itemsize + tk * tn * b.dtype.itemsize
        blocks += sum(rows * tn * e.dtype.itemsize for e in extras)
        blocks += sum(rows * tn * jnp.dtype(dt).itemsize for dt in out_dtypes)
        return 2 * blocks + (rows * tn * 4 if nk > 1 else 0)

    tm = _tile(m, tm if mode == "tn" else 2 * tm)
    while vmem_bytes(tm) > 3 * VMEM_LIMIT // 4 and tm % 256 == 0:
        tm //= 2
    while nk == 1 and k > 1024 and (m // tm) * (n // tn) < 4 and tm % 256 == 0:
        tm //= 2
    assert m % tm == 0 and n % tn == 0 and k % tk == 0, (name, m, n, k)
    a_spec = {"nn": pl.BlockSpec((tm, tk), lambda i, j, kk: (i, kk)),
              "nt": pl.BlockSpec((tm, tk), lambda i, j, kk: (i, kk)),
              "tn": pl.BlockSpec((tk, tm), lambda i, j, kk: (kk, i))}[mode]
    b_spec = {"nn": pl.BlockSpec((tk, tn), lambda i, j, kk: (kk, j)),
              "nt": pl.BlockSpec((tn, tk), lambda i, j, kk: (j, kk)),
              "tn": pl.BlockSpec((tk, tn), lambda i, j, kk: (kk, j))}[mode]
    specs, args = [a_spec, b_spec], [a, b]
    if bias is not None:
        specs.append(pl.BlockSpec((1, tn), lambda i, j, kk: (0, j)))
        args.append(bias)
    for e in extras:
        specs.append(pl.BlockSpec((tm, tn), lambda i, j, kk: (i, j)))
        args.append(e)
    n_e, n_o = len(extras), len(out_dtypes)
    has_bias = bias is not None

    def body(*refs):
        a_ref, b_ref = refs[0], refs[1]
        pos = 2
        bias_ref = refs[pos] if has_bias else None
        pos += int(has_bias)
        e_refs = refs[pos:pos + n_e]
        o_refs = refs[pos + n_e:pos + n_e + n_o]
        sum_ref = refs[pos + n_e + n_o] if colsum else None
        acc_ref = refs[pos + n_e + n_o + int(colsum)] if nk > 1 else None
        part = _dot(a_ref[...], b_ref[...], mode)

        def finish(r):
            if has_bias:
                r = r + bias_ref[...]
            res = epilogue(r, *[e[...] for e in e_refs]) if epilogue is not None else (r,)
            for ref, val in zip(o_refs, res):
                ref[...] = val.astype(ref.dtype)
            if colsum:
                sum_ref[...] = _colsum(res[0])

        if nk == 1:
            finish(part)
        else:
            kk = pl.program_id(2)

            @pl.when(kk == 0)
            def _():
                acc_ref[...] = part

            @pl.when(kk > 0)
            def _():
                acc_ref[...] += part

            @pl.when(kk == nk - 1)
            def _():
                finish(acc_ref[...])

    out_specs = [pl.BlockSpec((tm, tn), lambda i, j, kk: (i, j)) for _ in out_dtypes]
    out_shape = [jax.ShapeDtypeStruct((m, n), dt) for dt in out_dtypes]
    if colsum:
        out_specs.append(pl.BlockSpec((None, 1, tn), lambda i, j, kk: (i, 0, j)))
        out_shape.append(jax.ShapeDtypeStruct((m // tm, 1, n), F32))
    res = _call(name, body, (m // tm, n // tn, nk), specs, out_specs, out_shape, args,
                scratch_shapes=[pltpu.VMEM((tm, tn), F32)] if nk > 1 else [], carry=carry)
    return res[0] if len(res) == 1 else res


def _ssm_wgrads(u, dy, g_re, g_im, h_re, h_im, tk=1024):
    s = u.shape[0]
    tk = min(tk, s)
    nk = s // tk
    assert tk % N_SEG == 0

    def body(u_ref, dy_ref, gre_ref, gim_ref, hre_ref, him_ref, lre_ref, lim_ref, db_ref, dc_ref, dar_ref, dai_ref,
             pre_ref, pim_ref):
        kk = pl.program_id(1)
        u_blk, dy_blk = u_ref[...], dy_ref[...]
        g_r, g_i, h_r, h_i = gre_ref[...], gim_ref[...], hre_ref[...], him_ref[...]
        d_b = jnp.concatenate([_dot(u_blk, g_r, "tn"), _dot(u_blk, g_i, "tn")], axis=1)
        d_c = jnp.concatenate([_dot(h_r, dy_blk, "tn"), _dot(h_i, dy_blk, "tn")], axis=0)

        @pl.when(kk == 0)
        def _():
            first_row = lax.broadcasted_iota(jnp.int32, (N_SEG, CH_N), 0) == 0
            pre_ref[...] = jnp.where(first_row, 0.0, pltpu.roll(lre_ref[...], 1, 0))
            pim_ref[...] = jnp.where(first_row, 0.0, pltpu.roll(lim_ref[...], 1, 0))

        p_r = jnp.concatenate([pre_ref[...], h_r[:tk - N_SEG]], axis=0)
        p_i = jnp.concatenate([pim_ref[...], h_i[:tk - N_SEG]], axis=0)
        pre_ref[...] = h_r[tk - N_SEG:]
        pim_ref[...] = h_i[tk - N_SEG:]
        d_ar = jnp.sum(g_r * p_r + g_i * p_i, axis=0, keepdims=True)
        d_ai = jnp.sum(g_i * p_r - g_r * p_i, axis=0, keepdims=True)

        @pl.when(kk == 0)
        def _():
            db_ref[...] = d_b
            dc_ref[...] = d_c
            dar_ref[...] = d_ar
            dai_ref[...] = d_ai

        @pl.when(kk > 0)
        def _():
            db_ref[...] += d_b
            dc_ref[...] += d_c
            dar_ref[...] += d_ar
            dai_ref[...] += d_ai

    chan = pl.BlockSpec((tk, CH_W), lambda j, kk: (kk, j))
    state = pl.BlockSpec((tk, CH_N), lambda j, kk: (kk, j))
    last = pl.BlockSpec((N_SEG, CH_N), lambda j, kk: (s // N_SEG - 1, j))
    row = pl.BlockSpec((1, CH_N), lambda j, kk: (0, j))
    return pl.pallas_call(
        body, name="ssm_wgrads", grid=(SSM_CHUNKS, nk),
        in_specs=[chan, chan, state, state, state, state, last, last],
        out_specs=[pl.BlockSpec((None, CH_W, 2 * CH_N), lambda j, kk: (j, 0, 0)),
                   pl.BlockSpec((None, 2 * CH_N, CH_W), lambda j, kk: (j, 0, 0)), row, row],
        out_shape=[jax.ShapeDtypeStruct((SSM_CHUNKS, CH_W, 2 * CH_N), F32),
                   jax.ShapeDtypeStruct((SSM_CHUNKS, 2 * CH_N, CH_W), F32),
                   jax.ShapeDtypeStruct((1, N_STATE), F32), jax.ShapeDtypeStruct((1, N_STATE), F32)],
        scratch_shapes=[pltpu.VMEM((N_SEG, CH_N), F32)] * 2,
        compiler_params=_cparams(2))(u, dy, g_re, g_im, h_re, h_im, h_re, h_im)


SCAN_LB = 256


def _split_by_scan_block(mat, axis):
    halves = []
    for l in range(CH_N // SCAN_LB):
        re = lax.slice_in_dim(mat, l * SCAN_LB, (l + 1) * SCAN_LB, axis=axis)
        im = lax.slice_in_dim(mat, CH_N + l * SCAN_LB, CH_N + (l + 1) * SCAN_LB, axis=axis)
        halves.append(jnp.concatenate([re, im], axis=axis))
    return jnp.stack(halves, axis=1).reshape((-1,) + halves[0].shape[1:])


def _ssm_scan(name, chan, expand12, contract12, a_re, a_im, d_row, reverse, carry=None):
    s = chan.shape[0]
    seg_len = s // N_SEG
    n_sq = int(math.log2(seg_len))
    assert 2 ** n_sq == seg_len
    rb = min(512, s)
    per_chunk = CH_N // SCAN_LB

    def body(are_ref, aim_ref, ch_ref, e_ref, k_ref, d_ref, hre_ref, him_ref, o_ref, wre_ref, wim_ref, ere, eim, cre, cim):
        e_mat, k_mat = e_ref[...], k_ref[...]
        for r in range(s // rb):
            rows = slice(r * rb, (r + 1) * rb)
            w = _dot(ch_ref[rows, :], e_mat, "nt" if reverse else "nn")
            wre_ref[rows, :] = w[:, :SCAN_LB]
            wim_ref[rows, :] = w[:, SCAN_LB:]

        ar1 = are_ref[...]
        ai1 = -aim_ref[...] if reverse else aim_ref[...]
        ar = jnp.broadcast_to(ar1, (N_SEG, SCAN_LB))
        ai = jnp.broadcast_to(ai1, (N_SEG, SCAN_LB))

        def rows_of(k):
            kk = seg_len - 1 - k if reverse else k
            return pl.ds(pl.multiple_of(kk * N_SEG, N_SEG), N_SEG)

        def local(k, carry):
            hr, hi = carry
            rows = rows_of(k)
            nr = ar * hr - ai * hi + wre_ref[rows, :]
            ni = ar * hi + ai * hr + wim_ref[rows, :]
            hre_ref[rows, :] = nr
            him_ref[rows, :] = ni
            return nr, ni

        zero = jnp.zeros((N_SEG, SCAN_LB), F32)
        er, ei = lax.fori_loop(0, seg_len, local, (zero, zero))
        ere[...] = er
        eim[...] = ei
        pr, pi = ar1, ai1
        for _ in range(n_sq):
            pr, pi = pr * pr - pi * pi, 2.0 * pr * pi
        cr = jnp.zeros((1, SCAN_LB), F32)
        ci = jnp.zeros((1, SCAN_LB), F32)
        for jj in range(N_SEG):
            j = N_SEG - 1 - jj if reverse else jj
            cre[j:j + 1, :] = cr
            cim[j:j + 1, :] = ci
            er_j, ei_j = ere[j:j + 1, :], eim[j:j + 1, :]
            cr, ci = pr * cr - pi * ci + er_j, pr * ci + pi * cr + ei_j
        c_r, c_i = cre[...], cim[...]

        def fix(k, carry):
            qr, qi = carry
            rows = rows_of(k)
            hre_ref[rows, :] = hre_ref[rows, :] + (qr * c_r - qi * c_i)
            him_ref[rows, :] = him_ref[rows, :] + (qr * c_i + qi * c_r)
            return qr * ar - qi * ai, qr * ai + qi * ar

        lax.fori_loop(0, seg_len, fix, (ar, ai))

        first_of_chunk = lax.rem(pl.program_id(0), per_chunk) == 0
        for r in range(s // rb):
            rows = slice(r * rb, (r + 1) * rb)
            h_cat = jnp.concatenate([hre_ref[rows, :], him_ref[rows, :]], axis=1)
            part = _dot(h_cat, k_mat, "nt" if reverse else "nn")

            @pl.when(first_of_chunk)
            def _(rows=rows, part=part):
                o_ref[rows, :] = part + d_ref[...] * ch_ref[rows, :]

            @pl.when(jnp.logical_not(first_of_chunk))
            def _(rows=rows, part=part):
                o_ref[rows, :] += part

    nblk = N_STATE // SCAN_LB
    blk = pl.BlockSpec((s, SCAN_LB), lambda b: (0, b))
    row = pl.BlockSpec((1, SCAN_LB), lambda b: (0, b))
    chan_blk = pl.BlockSpec((s, CH_W), lambda b: (0, b // per_chunk))
    res = _call(name, body, (nblk,),
                [row, row, chan_blk, pl.BlockSpec((None,) + expand12.shape[1:], lambda b: (b, 0, 0)),
                 pl.BlockSpec((None,) + contract12.shape[1:], lambda b: (b, 0, 0)),
                 pl.BlockSpec((1, CH_W), lambda b: (0, b // per_chunk))],
                [blk, blk, chan_blk],
                [jax.ShapeDtypeStruct((s, N_STATE), F32)] * 2 + [jax.ShapeDtypeStruct((s, SSM_WIDTH), F32)],
                (a_re, a_im, chan, expand12, contract12, d_row),
                scratch_shapes=[pltpu.VMEM((s, SCAN_LB), F32)] * 2 + [pltpu.VMEM((N_SEG, SCAN_LB), F32)] * 4, carry=carry)
    return res[0], res[1], res[2]


def _disc(ldt, are, aim, bre, bim):
    dt = jnp.exp(ldt)
    mag = jnp.exp(are * dt)
    abr = mag * jnp.cos(aim * dt)
    abi = mag * jnp.sin(aim * dt)
    den = jnp.square(are) + jnp.square(aim)
    nr = abr - 1.0
    fre = (nr * are + abi * aim) / den
    fim = (abi * are - nr * aim) / den
    return abr, abi, fre * bre - fim * bim, fre * bim + fim * bre


def _ssm_disc_fwd(ldt, are, aim, bre, bim):
    def body(l_ref, ar_ref, ai_ref, br_ref, bi_ref, o0, o1, o2, o3):
        res = _disc(l_ref[...], ar_ref[...], ai_ref[...], br_ref[...], bi_ref[...])
        for ref, val in zip((o0, o1, o2, o3), res):
            ref[...] = val

    col = jax.ShapeDtypeStruct((N_STATE, 1), F32)
    mat = jax.ShapeDtypeStruct((N_STATE, SSM_GROUP), F32)
    return pl.pallas_call(body, name="ssm_disc_fwd", out_shape=[col, col, mat, mat],
                          in_specs=[VMEM_SPEC] * 5, out_specs=[VMEM_SPEC] * 4)(ldt, are, aim, bre, bim)


def _ssm_disc_bwd(ldt, are, aim, bre, bim, d_abr, d_abi, d_bbr, d_bbi):
    def body(l_ref, ar_ref, ai_ref, br_ref, bi_ref, c0, c1, c2, c3, g_ldt, g_are, g_aim, g_bre, g_bim):
        _, vjp = jax.vjp(_disc, l_ref[...], ar_ref[...], ai_ref[...], br_ref[...], bi_ref[...])
        dl, dar, dai, dbr, dbi = vjp((c0[...], c1[...], c2[...], c3[...]))
        state = lax.broadcasted_iota(jnp.int32, (N_STATE, SSM_GROUPS), 0)
        group = lax.broadcasted_iota(jnp.int32, (N_STATE, SSM_GROUPS), 1)
        pick = jnp.right_shift(state, 6) == group
        g_ldt[...] = jnp.sum(jnp.where(pick, dl, 0.0), axis=0, keepdims=True)
        g_are[...] = dar
        g_aim[...] = dai
        g_bre[...] = dbr
        g_bim[...] = dbi

    col = jax.ShapeDtypeStruct((N_STATE, 1), F32)
    mat = jax.ShapeDtypeStruct((N_STATE, SSM_GROUP), F32)
    return pl.pallas_call(body, name="ssm_disc_bwd",
                          out_shape=[jax.ShapeDtypeStruct((1, SSM_GROUPS), F32), col, col, mat, mat],
                          in_specs=[VMEM_SPEC] * 9, out_specs=[VMEM_SPEC] * 5,
                          compiler_params=pltpu.CompilerParams(vmem_limit_bytes=VMEM_LIMIT))(
        ldt, are, aim, bre, bim, d_abr, d_abi, d_bbr, d_bbi)


_EYE8 = np.eye(8, dtype=np.float32)


def _blockdiag_b(bb):
    t = bb.reshape(SSM_CHUNKS, 8, SSM_STATE, SSM_GROUP).transpose(0, 1, 3, 2)
    return jnp.einsum("igcn,gh->igchn", t, _EYE8).reshape(SSM_CHUNKS, CH_W, CH_N)


def _diag_of_b(m):
    t = jnp.einsum("igchn,gh->igcn", m.reshape(SSM_CHUNKS, 8, SSM_GROUP, 8, SSM_STATE), _EYE8)
    return t.transpose(0, 1, 3, 2).reshape(N_STATE, SSM_GROUP)


def _blockdiag_c(c):
    t = c.reshape(SSM_CHUNKS, 8, SSM_GROUP, SSM_STATE).transpose(0, 1, 3, 2)
    return jnp.einsum("ignc,gh->ignhc", t, _EYE8).reshape(SSM_CHUNKS, CH_N, CH_W)


def _diag_of_c(m):
    t = jnp.einsum("ignhc,gh->ignc", m.reshape(SSM_CHUNKS, 8, SSM_STATE, 8, SSM_GROUP), _EYE8)
    return t.transpose(0, 1, 3, 2).reshape(SSM_GROUPS, SSM_GROUP, SSM_STATE)


def _time_perm(a):
    s, c = a.shape
    return a.reshape(N_SEG, s // N_SEG, c).transpose(1, 0, 2).reshape(s, c)


def _time_unperm(a):
    s, c = a.shape
    return a.reshape(s // N_SEG, N_SEG, c).transpose(1, 0, 2).reshape(s, c)


def _dilate(a, d):
    s, c = a.shape
    return a if d == 1 else a.reshape(s // d, d, c).transpose(1, 0, 2).reshape(s, c)


def _undilate(a, d):
    s, c = a.shape
    return a if d == 1 else a.reshape(d, s // d, c).transpose(1, 0, 2).reshape(s, c)


def _dilate_rows(a, d):
    r, s = a.shape
    return a if d == 1 else a.reshape(r, s // d, d).transpose(0, 2, 1).reshape(r, s)


ATT_T = 4
ATT_ROWS = ATT_T * ATT_BLK


def _window(prev_ref, cur_ref, i, sl):
    if i == 0:
        return jnp.concatenate([prev_ref[:, sl], cur_ref[0:ATT_BLK, sl]], axis=0)
    return cur_ref[(i - 1) * ATT_BLK:(i + 1) * ATT_BLK, sl]


def _band_valid(first_key):
    qi = lax.broadcasted_iota(jnp.int32, (ATT_BLK, 2 * ATT_BLK), 0)
    ki = lax.broadcasted_iota(jnp.int32, (ATT_BLK, 2 * ATT_BLK), 1)
    steps = qi + ATT_BLK - ki
    return (steps >= 0) & (steps <= ATT_BLK) & (ki >= first_key)


ATT_STATW = ATT_HPG * 128


def _stat(h):
    return slice(h * 128, (h + 1) * 128)


def _stat_rows(stat):
    n = stat.shape[0]
    heads = [stat[:, _stat(h)].T[0:1, :] for h in range(ATT_HPG)]
    return jnp.concatenate(heads + [jnp.zeros((8 - ATT_HPG, n), stat.dtype)], axis=0)


def _attn_specs(nb, width=ATT_GROUPW):
    cur = pl.BlockSpec((ATT_ROWS, width), lambda b: (b, 0))
    prev = pl.BlockSpec((ATT_BLK, width), lambda b: (jnp.maximum(b * ATT_T - 1, 0), 0))
    nxt = pl.BlockSpec((ATT_BLK, width), lambda b: (jnp.minimum((b + 1) * ATT_T, nb - 1), 0))
    return cur, prev, nxt


def _attn_fwd(tag, per_seq, q, k, v):
    s = q.shape[0]
    nb = s // ATT_BLK

    def body(q_ref, kc_ref, kp_ref, vc_ref, vp_ref, o_ref, lse_ref):
        bt = pl.program_id(0)
        for i in range(ATT_T):
            has_prev = lax.rem(bt * ATT_T + i, per_seq) > 0
            valid = _band_valid(jnp.where(has_prev, 0, ATT_BLK))
            rows = slice(i * ATT_BLK, (i + 1) * ATT_BLK)
            for h in range(ATT_HPG):
                sl = slice(h * ATT_HEAD_DIM, (h + 1) * ATT_HEAD_DIM)
                kcat = _window(kp_ref, kc_ref, i, sl)
                vcat = _window(vp_ref, vc_ref, i, sl)
                sc = _dot(q_ref[rows, sl], kcat, "nt") * ATT_SCALE
                sc = jnp.where(valid, sc, NEG_INF)
                m = jnp.max(sc, axis=-1, keepdims=True)
                p = jnp.exp(sc - m)
                den = jnp.sum(p, axis=-1, keepdims=True)
                o_ref[rows, sl] = _dot(p, vcat, "nn") / den
                lse_ref[rows, _stat(h)] = jnp.broadcast_to(m + jnp.log(den), (ATT_BLK, 128))

    cur, prev, _ = _attn_specs(nb)
    stat, _, _ = _attn_specs(nb, ATT_STATW)
    return pl.pallas_call(
        body, name="attn_fwd_" + tag, grid=(nb // ATT_T,), in_specs=[cur, cur, prev, cur, prev], out_specs=[cur, stat],
        out_shape=[jax.ShapeDtypeStruct((s, ATT_GROUPW), F32), jax.ShapeDtypeStruct((s, ATT_STATW), F32)],
        compiler_params=_cparams(1))(q, k, k, v, v)


def _attn_dq(tag, per_seq, q, k, v, do, lse, delta):
    s = q.shape[0]
    nb = s // ATT_BLK

    def body(q_ref, kc_ref, kp_ref, vc_ref, vp_ref, do_ref, lse_ref, dl_ref, dq_ref):
        bt = pl.program_id(0)
        for i in range(ATT_T):
            has_prev = lax.rem(bt * ATT_T + i, per_seq) > 0
            valid = _band_valid(jnp.where(has_prev, 0, ATT_BLK))
            rows = slice(i * ATT_BLK, (i + 1) * ATT_BLK)
            for h in range(ATT_HPG):
                sl = slice(h * ATT_HEAD_DIM, (h + 1) * ATT_HEAD_DIM)
                kcat = _window(kp_ref, kc_ref, i, sl)
                vcat = _window(vp_ref, vc_ref, i, sl)
                lse = jnp.concatenate([lse_ref[rows, _stat(h)]] * 2, axis=1)
                dlt = jnp.concatenate([dl_ref[rows, _stat(h)]] * 2, axis=1)
                sc = _dot(q_ref[rows, sl], kcat, "nt") * ATT_SCALE
                p = jnp.exp(jnp.where(valid, sc, NEG_INF) - lse)
                dp = _dot(do_ref[rows, sl], vcat, "nt")
                ds = p * (dp - dlt) * ATT_SCALE
                dq_ref[rows, sl] = _dot(ds, kcat, "nn")

    cur, prev, _ = _attn_specs(nb)
    stat, _, _ = _attn_specs(nb, ATT_STATW)
    return pl.pallas_call(
        body, name="attn_dq_" + tag, grid=(nb // ATT_T,), in_specs=[cur, cur, prev, cur, prev, cur, stat, stat],
        out_specs=cur, out_shape=jax.ShapeDtypeStruct((s, ATT_GROUPW), F32),
        compiler_params=_cparams(1))(q, k, k, v, v, do, lse, delta)


def _attn_dkv(tag, per_seq, q, k, v, do, lse_t, delta_t):
    s = q.shape[0]
    nb = s // ATT_BLK

    def body(k_ref, v_ref, qc_ref, qn_ref, doc_ref, don_ref, lc_ref, ln_ref, dc_ref, dn_ref, dk_ref, dv_ref):
        bt = pl.program_id(0)
        ki = lax.broadcasted_iota(jnp.int32, (ATT_BLK, 2 * ATT_BLK), 0)
        ci = lax.broadcasted_iota(jnp.int32, (ATT_BLK, 2 * ATT_BLK), 1)

        def pair(edge_ref, cur_ref, i, sl):
            if i == ATT_T - 1:
                return jnp.concatenate([cur_ref[i * ATT_BLK:(i + 1) * ATT_BLK, sl], edge_ref[:, sl]], axis=0)
            return cur_ref[i * ATT_BLK:(i + 2) * ATT_BLK, sl]

        def pair_row(edge_ref, cur_ref, i, h):
            if i == ATT_T - 1:
                row = jnp.concatenate([cur_ref[h:h + 1, i * ATT_BLK:(i + 1) * ATT_BLK], edge_ref[h:h + 1, :]], axis=1)
            else:
                row = cur_ref[h:h + 1, i * ATT_BLK:(i + 2) * ATT_BLK]
            return jnp.broadcast_to(row, (ATT_BLK, 2 * ATT_BLK))

        for i in range(ATT_T):
            b = bt * ATT_T + i
            next_uses = (b + 1 < nb) & (lax.rem(b + 1, per_seq) > 0)
            reach = jnp.where(next_uses, 0, 4 * ATT_BLK)
            valid = ((ci < ATT_BLK) & (ci >= ki)) | ((ci >= ATT_BLK) & (ki - ci + ATT_BLK >= reach))
            rows = slice(i * ATT_BLK, (i + 1) * ATT_BLK)
            for h in range(ATT_HPG):
                sl = slice(h * ATT_HEAD_DIM, (h + 1) * ATT_HEAD_DIM)
                qcat, docat = pair(qn_ref, qc_ref, i, sl), pair(don_ref, doc_ref, i, sl)
                sc = _dot(k_ref[rows, sl], qcat, "nt") * ATT_SCALE
                p = jnp.exp(jnp.where(valid, sc, NEG_INF) - pair_row(ln_ref, lc_ref, i, h))
                dv_ref[rows, sl] = _dot(p, docat, "nn")
                dp = _dot(v_ref[rows, sl], docat, "nt")
                ds = p * (dp - pair_row(dn_ref, dc_ref, i, h)) * ATT_SCALE
                dk_ref[rows, sl] = _dot(ds, qcat, "nn")

    cur, _, nxt = _attn_specs(nb)
    stat = pl.BlockSpec((8, ATT_ROWS), lambda b: (0, b))
    snxt = pl.BlockSpec((8, ATT_BLK), lambda b: (0, jnp.minimum((b + 1) * ATT_T, nb - 1)))
    return pl.pallas_call(
        body, name="attn_dkv_" + tag, grid=(nb // ATT_T,), in_specs=[cur, cur, cur, nxt, cur, nxt, stat, snxt, stat, snxt],
        out_specs=[cur, cur], out_shape=[jax.ShapeDtypeStruct((s, ATT_GROUPW), F32)] * 2,
        compiler_params=_cparams(1))(k, v, q, q, do, do, lse_t, lse_t, delta_t, delta_t)


def _xattn_probs(q, kh):
    sc = _dot(q, kh, "nt") * XATT_SCALE
    e = jnp.exp(sc - jnp.max(sc, axis=-1, keepdims=True))
    return e / jnp.sum(e, axis=-1, keepdims=True)


def _xattn_fwd(q, kv, tm=512):
    s = q.shape[0]
    tm = min(tm, s)

    def body(q_ref, kv_ref, o_ref):
        for h in range(XATT_HEADS):
            sl = slice(h * XATT_HEAD_DIM, (h + 1) * XATT_HEAD_DIM)
            vs = slice(D_MODEL + h * XATT_HEAD_DIM, D_MODEL + (h + 1) * XATT_HEAD_DIM)
            p = _xattn_probs(q_ref[:, sl], kv_ref[:, sl])
            o_ref[:, sl] = _dot(p, kv_ref[:, vs], "nn").astype(o_ref.dtype)

    return pl.pallas_call(
        body, name="xattn_fwd", grid=(s // tm,),
        in_specs=[pl.BlockSpec((tm, D_MODEL), lambda i: (i, 0)), pl.BlockSpec(kv.shape, lambda i: (0, 0))],
        out_specs=pl.BlockSpec((tm, D_MODEL), lambda i: (i, 0)),
        out_shape=jax.ShapeDtypeStruct((s, D_MODEL), MXU_DTYPE), compiler_params=_cparams(1))(q, kv)


def _xattn_bwd(q, kv, do, tm=512):
    s = q.shape[0]
    tm = min(tm, s)

    def body(q_ref, kv_ref, do_ref, dq_ref, dkv_ref):
        first = pl.program_id(0) == 0

        @pl.when(first)
        def _():
            dkv_ref[...] = jnp.zeros_like(dkv_ref)

        for h in range(XATT_HEADS):
            sl = slice(h * XATT_HEAD_DIM, (h + 1) * XATT_HEAD_DIM)
            vs = slice(D_MODEL + h * XATT_HEAD_DIM, D_MODEL + (h + 1) * XATT_HEAD_DIM)
            p = _xattn_probs(q_ref[:, sl], kv_ref[:, sl])
            dkv_ref[:, vs] += _dot(p, do_ref[:, sl], "tn")
            dp = _dot(do_ref[:, sl], kv_ref[:, vs], "nt")
            ds = p * (dp - jnp.sum(dp * p, axis=-1, keepdims=True)) * XATT_SCALE
            dq_ref[:, sl] = _dot(ds, kv_ref[:, sl], "nn").astype(dq_ref.dtype)
            dkv_ref[:, sl] += _dot(ds, q_ref[:, sl], "tn")

    row = pl.BlockSpec((tm, D_MODEL), lambda i: (i, 0))
    whole = pl.BlockSpec(kv.shape, lambda i: (0, 0))
    return pl.pallas_call(
        body, name="xattn_bwd", grid=(s // tm,), in_specs=[row, whole, row], out_specs=[row, whole],
        out_shape=[jax.ShapeDtypeStruct((s, D_MODEL), MXU_DTYPE), jax.ShapeDtypeStruct(kv.shape, F32)],
        compiler_params=_cparams(1))(q, kv, do)


def _ln(x, g, b):
    mu = jnp.mean(x, axis=-1, keepdims=True)
    xc = x - mu
    var = jnp.mean(jnp.square(xc), axis=-1, keepdims=True)
    return xc * lax.rsqrt(var + LN_EPS) * g + b


def _res_ln(h, o, g, b):
    return _ln(DEEPNORM_ALPHA * h + o, g, b)


def _gate(gs, ga, z1, z2, batt):
    return jax.nn.sigmoid(gs) * (z1 * jax.nn.sigmoid(z2)) + jax.nn.sigmoid(ga) * batt


ROPE_TW = 2 * ATT_HEAD_DIM


def _rope_tables(pos, invf, m1, m2):
    ang = pos.astype(F32) * invf
    sin = jnp.sin(ang)
    return jnp.cos(ang), -sin * m1, sin * m2


def _widen(tab):
    return jnp.concatenate([tab] * (ATT_GROUPW // ROPE_TW), axis=1)


def _rope(t, cos, s_up, s_dn):
    w = t.shape[-1]
    return t * cos + pltpu.roll(t, w - ROT_DIM // 2, 1) * s_up + pltpu.roll(t, ROT_DIM // 2, 1) * s_dn


def _rope_t(dt, cos, s_up, s_dn):
    w = dt.shape[-1]
    return dt * cos + pltpu.roll(dt * s_up, ROT_DIM // 2, 1) + pltpu.roll(dt * s_dn, w - ROT_DIM // 2, 1)


def _rope_consts():
    inv_freq = ROPE_THETA ** (-jnp.arange(0, ROT_DIM, 2, dtype=F32) / ROT_DIM)
    d = np.arange(ROPE_TW) % ATT_HEAD_DIM
    invf = jnp.where(d < ROT_DIM, inv_freq[d % (ROT_DIM // 2)], 0.0).reshape(1, ROPE_TW).astype(F32)
    m1 = jnp.asarray((d < ROT_DIM // 2).astype(np.float32)).reshape(1, ROPE_TW)
    m2 = jnp.asarray(((d >= ROT_DIM // 2) & (d < ROT_DIM)).astype(np.float32)).reshape(1, ROPE_TW)
    return invf, m1, m2


def _head_sum_matrix():
    d = np.arange(ATT_GROUPW) // ATT_HEAD_DIM
    s = np.arange(ATT_STATW) // 128
    return jnp.asarray((d[:, None] == s[None, :]).astype(np.float32))


def _adamw(w, g, m, v):
    m = ADAM_B1 * m + (1.0 - ADAM_B1) * g
    v = ADAM_B2 * v + (1.0 - ADAM_B2) * jnp.square(g)
    m_hat = m / (1.0 - ADAM_B1 ** ADAM_STEP)
    v_hat = v / (1.0 - ADAM_B2 ** ADAM_STEP)
    delta = -ADAM_LR * (m_hat / (jnp.sqrt(v_hat) + ADAM_EPS) + ADAM_WD * w)
    return delta, m, v


def _local_step(x, mem, pos, target, sp, ex):
    s = x.shape[0]
    al = DEEPNORM_ALPHA
    mx = MXU_DTYPE

    h0, h0b = _rowwise("ln_in", lambda x, g, b: (lambda h: (h, h))(_ln(x, g, b)), [x],
                       [sp["ln_in_g"], sp["ln_in_b"]], [(D_MODEL, F32), (D_MODEL, mx)],
                       carry=ex.gather_carry(["w_in"]))
    proj = _mm("proj", h0b, ex.weight("w_in"), "nn", bias=sp["b_in"],
               carry=ex.gather_carry(["w_glu", "w_att_up", "w_mix_out", "w_xq"]))

    ldt = jnp.repeat(sp["ssm_log_dt"].reshape(SSM_GROUPS), SSM_STATE).reshape(N_STATE, 1)
    are, aim = sp["ssm_a_re"].reshape(N_STATE, 1), sp["ssm_a_im"].reshape(N_STATE, 1)
    bre, bim = sp["ssm_b_re"].reshape(N_STATE, SSM_GROUP), sp["ssm_b_im"].reshape(N_STATE, SSM_GROUP)
    abr, abi, bbr, bbi = _ssm_disc_fwd(ldt, are, aim, bre, bim)
    a_re, a_im = abr.reshape(1, N_STATE), abi.reshape(1, N_STATE)
    bexp = jnp.concatenate([_blockdiag_b(bbr), _blockdiag_b(bbi)], axis=2).astype(mx)
    cexp = jnp.concatenate([_blockdiag_c(sp["ssm_c_re"].reshape(SSM_GROUPS, SSM_GROUP, SSM_STATE)),
                            -_blockdiag_c(sp["ssm_c_im"].reshape(SSM_GROUPS, SSM_GROUP, SSM_STATE))],
                           axis=1).astype(mx)
    u_p = _time_perm(proj[:, :SSM_WIDTH])
    b12, c12 = _split_by_scan_block(bexp, 2), _split_by_scan_block(cexp, 1)
    h_re, h_im, y_p = _ssm_scan("ssm_scan_fwd", u_p, b12, c12, a_re, a_im, sp["ssm_d"], reverse=False,
                                carry=ex.gather_carry(["w_xkv", "w_ff1", "w_ff2"]))
    y = _time_unperm(y_p)
    ygb, = _rowwise("gelu", lambda y: jax.nn.gelu(y), [y], [], [(SSM_WIDTH, mx)])
    z = _mm("glu", ygb, ex.weight("w_glu"), "nn", bias=sp["b_glu"], carry=ex.gather_carry(["w_xo"]))

    invf, m1, m2 = _rope_consts()

    def rope_fwd(pos, q0, q1, q2, k0, k1, k2, v0, v1, v2, invf, m1, m2):
        narrow = _rope_tables(pos, invf, m1, m2)
        tabs = [_widen(t) for t in narrow]
        return tuple(_rope(t, *tabs) for t in (q0, q1, q2, k0, k1, k2)) + (v0, v1, v2) + tuple(narrow)

    qkv_cols = [(proj, ATT_GROUPW, 3 + i) for i in range(9)]
    qkv = _rowwise("rope", rope_fwd, [pos] + qkv_cols, [invf, m1, m2], [(ATT_GROUPW, mx)] * 9 + [(ROPE_TW, F32)] * 3)
    rope_tabs = qkv[9:]
    n_blocks = s // ATT_BLK
    groups = [(str(g), n_blocks // d, d) for g, d in enumerate(DILATIONS)]
    q_d = [_dilate(qkv[g], d) for g, d in enumerate(DILATIONS)]
    k_d = [_dilate(qkv[3 + g], d) for g, d in enumerate(DILATIONS)]
    v_d = [_dilate(qkv[6 + g], d) for g, d in enumerate(DILATIONS)]
    o_g, l_g = [], []
    for g, (tag, per_seq, d) in enumerate(groups):
        o, lse = _attn_fwd(tag, per_seq, q_d[g], k_d[g], v_d[g])
        o_g.append(_undilate(o, d))
        l_g.append(_undilate(lse, d))

    def merge(o0, o1, o2, l0, l1, l2):
        m = jnp.maximum(jnp.maximum(l0, l1), l2)
        e0, e1, e2 = jnp.exp(l0 - m), jnp.exp(l1 - m), jnp.exp(l2 - m)
        tot = e0 + e1 + e2

        def per_dim(e):
            w = e / tot
            return jnp.concatenate([w[:, h * 128:h * 128 + ATT_HEAD_DIM] for h in range(ATT_HPG)], axis=1)

        att = per_dim(e0) * o0 + per_dim(e1) * o1 + per_dim(e2) * o2
        lse = m + jnp.log(tot)
        return att, att, lse, _stat_rows(lse)

    att, attb, lse_tot, lse_tot_t = _rowwise("attn_merge", merge, o_g + l_g, [],
                                             [(ATT_GROUPW, F32), (ATT_GROUPW, mx), (ATT_STATW, F32)], touts=[(8, F32)])
    batt = _mm("att_up", attb, ex.weight("w_att_up"), "nn")

    gate_rows = [(proj, D_MODEL, 3), (proj, D_MODEL, 4), (z, D_MODEL, 0), (z, D_MODEL, 1), batt]
    mixedb, = _rowwise("gate", _gate, gate_rows, [], [(D_MODEL, mx)])
    o1 = _mm("mix_out", mixedb, ex.weight("w_mix_out"), "nn", bias=sp["b_mix_out"])
    h1, h1b = _rowwise("ln1", lambda h, o, g, b: (lambda r: (r, r))(_res_ln(h, o, g, b)), [h0, o1],
                       [sp["ln1_g"], sp["ln1_b"]], [(D_MODEL, F32), (D_MODEL, mx)])

    qx = _mm("xq", h1b, ex.weight("w_xq"), "nn", out_dtypes=(mx,))
    kvx = _mm("xkv", mem, ex.weight("w_xkv"), "nn", out_dtypes=(mx,))
    oxb = _xattn_fwd(qx, kvx)
    o2 = _mm("xo", oxb, ex.weight("w_xo"), "nn")
    h2, h2b = _rowwise("ln2", lambda h, o, g, b: (lambda r: (r, r))(_res_ln(h, o, g, b)), [h1, o2],
                       [sp["ln2_g"], sp["ln2_b"]], [(D_MODEL, F32), (D_MODEL, mx)])

    a_ff, fb = _mm("ff1", h2b, ex.weight("w_ff1"), "nn", bias=sp["b_ff1"],
                   epilogue=lambda r: (r, jnp.square(jnp.maximum(r, 0.0))), out_dtypes=(F32, mx))
    o3 = _mm("ff2", fb, ex.weight("w_ff2"), "nn", bias=sp["b_ff2"])

    def loss_bwd(h2, o3, tgt, g, b):
        def f(h2, o3, g, b):
            h3 = _res_ln(h2, o3, g, b)
            return 0.5 * jnp.sum(jnp.mean(jnp.square(h3 - tgt), axis=-1))

        loss, vjp = jax.vjp(f, h2, o3, g, b)
        _, dr, dg, db = vjp(jnp.ones((), F32))
        return dr, dr, dg, db, _colsum(dr), jnp.full((1, 128), loss, F32)

    dr3, dr3b, g_ln3_g, g_ln3_b, g_b_ff2, loss = _rowwise(
        "loss_ln3_bwd", loss_bwd, [h2, o3, target], [sp["ln3_g"], sp["ln3_b"]],
        [(D_MODEL, F32), (D_MODEL, mx)], [D_MODEL, D_MODEL, D_MODEL, 128])

    dab, da_sums = _mm("ff2_dx", dr3b, ex.weight("w_ff2"), "nt", extras=(a_ff,),
                       epilogue=lambda r, a: (r * (2.0 * jnp.maximum(a, 0.0)),), out_dtypes=(mx,), colsum=True)
    g_b_ff1 = jnp.sum(da_sums, axis=0)
    ex.grad("w_ff2", _mm("ff2_dw", fb, dr3b, "tn"))
    ex.grad("w_ff1", _mm("ff1_dw", h2b, dab, "tn", carry=ex.carry(swap=["w_ff2"])))
    dh2 = _mm("ff1_dx", dab, ex.weight("w_ff1"), "nt", extras=(dr3,), epilogue=lambda r, d: (r + al * d,),
              carry=ex.carry(swap=["w_ff1"], ici=["w_ff2"]))

    def ln_bwd(h, o, dout, g, b):
        _, vjp = jax.vjp(_res_ln, h, o, g, b)
        _, dr, dg, db = vjp(dout)
        return dr, dr, dg, db, _colsum(dr)

    dr2, dr2b, g_ln2_g, g_ln2_b, _ = _rowwise(
        "ln2_bwd", ln_bwd, [h1, o2, dh2], [sp["ln2_g"], sp["ln2_b"]],
        [(D_MODEL, F32), (D_MODEL, mx)], [D_MODEL, D_MODEL, D_MODEL])
    ex.grad("w_xo", _mm("xo_dw", oxb, dr2b, "tn"))
    doxb = _mm("xo_dx", dr2b, ex.weight("w_xo"), "nt", out_dtypes=(mx,), carry=ex.carry(swap=["w_xo"]))
    dqxb, dkvx = _xattn_bwd(qx, kvx, doxb)
    ex.grad("w_xq", _mm("xq_dw", h1b, dqxb, "tn"))
    dh1 = _mm("xq_dx", dqxb, ex.weight("w_xq"), "nt", extras=(dr2,), epilogue=lambda r, d: (r + al * d,),
              carry=ex.carry(swap=["w_xq"]))
    ex.grad("w_xkv", _mm("xkv_dw", mem, dkvx, "tn"))

    dr1, dr1b, g_ln1_g, g_ln1_b, g_b_mix = _rowwise(
        "ln1_bwd", ln_bwd, [h0, o1, dh1], [sp["ln1_g"], sp["ln1_b"]],
        [(D_MODEL, F32), (D_MODEL, mx)], [D_MODEL, D_MODEL, D_MODEL])
    ex.grad("w_mix_out", _mm("mix_dw", mixedb, dr1b, "tn", carry=ex.carry(swap=["w_xkv"])))
    dmixed = _mm("mix_dx", dr1b, ex.weight("w_mix_out"), "nt", carry=ex.carry(swap=["w_mix_out"]))

    def gate_bwd(gs, ga, z1, z2, batt, dm):
        _, vjp = jax.vjp(_gate, gs, ga, z1, z2, batt)
        dgs, dga, dz1, dz2, dbatt = vjp(dm)
        dz = jnp.concatenate([dz1, dz2], axis=-1)
        return dgs, dga, dz, dbatt, _colsum(dz)

    dgsb, dgab, dzb, dbattb, g_b_glu = _rowwise(
        "gate_bwd", gate_bwd, gate_rows + [dmixed], [],
        [(D_MODEL, mx), (D_MODEL, mx), (2 * D_MODEL, mx), (D_MODEL, mx)], [2 * D_MODEL])
    ex.grad("w_att_up", _mm("att_up_dw", attb, dbattb, "tn"))
    datt = _mm("att_up_dx", dbattb, ex.weight("w_att_up"), "nt", carry=ex.carry(swap=["w_att_up"]))

    def att_delta(datt, att, hs):
        dl = jnp.dot(datt * att, hs, precision=lax.Precision.HIGHEST, preferred_element_type=F32)
        return datt, dl, _stat_rows(dl)

    dattb, delta, delta_t = _rowwise("attn_delta", att_delta, [datt, att], [_head_sum_matrix()],
                                     [(ATT_GROUPW, mx), (ATT_STATW, F32)], touts=[(8, F32)])
    dq_g, dk_g, dv_g = [], [], []
    for g, (tag, per_seq, d) in enumerate(groups):
        do_d, lt_d, dl_d = _dilate(dattb, d), _dilate(lse_tot, d), _dilate(delta, d)
        dq_g.append(_undilate(_attn_dq(tag, per_seq, q_d[g], k_d[g], v_d[g], do_d, lt_d, dl_d), d))
        dk, dv = _attn_dkv(tag, per_seq, q_d[g], k_d[g], v_d[g], do_d, _dilate_rows(lse_tot_t, d), _dilate_rows(delta_t, d))
        dk_g.append(_undilate(dk, d))
        dv_g.append(_undilate(dv, d))
    dqkv = dq_g + dk_g + dv_g

    def rope_bwd(q0, q1, q2, k0, k1, k2, v0, v1, v2, cos, s_up, s_dn):
        tabs = [_widen(t) for t in (cos, s_up, s_dn)]
        return jnp.concatenate([_rope_t(t, *tabs) for t in (q0, q1, q2, k0, k1, k2)] + [v0, v1, v2], axis=-1)

    dqkvb, = _rowwise("rope_bwd", rope_bwd, dqkv + list(rope_tabs), [], [(9 * ATT_GROUPW, mx)])

    ex.grad("w_glu", _mm("glu_dw", ygb, dzb, "tn"))
    dyg = _mm("glu_dx", dzb, ex.weight("w_glu"), "nt", carry=ex.carry(swap=["w_glu"]))

    def gelu_bwd(y, dyg):
        _, vjp = jax.vjp(jax.nn.gelu, y)
        return vjp(dyg)[0]

    dy, = _rowwise("gelu_bwd", gelu_bwd, [y, dyg], [], [(SSM_WIDTH, F32)])
    dy_p = _time_perm(dy)
    s_re, s_im, du_p = _ssm_scan("ssm_scan_bwd", dy_p, c12, b12, a_re, a_im, sp["ssm_d"], reverse=True,
                                 carry=ex.carry(ici=["w_ff1", "w_xkv", "w_glu"]))
    g_bexp, g_cexp, d_abr, d_abi = _ssm_wgrads(u_p, dy_p, s_re, s_im, h_re, h_im)
    g_ssm_d, = _rowwise("ssm_dd", lambda a, b: (_colsum(a * b),), [dy_p, u_p], [], [], [SSM_WIDTH])
    g_ldt, g_are, g_aim, g_bre, g_bim = _ssm_disc_bwd(
        ldt, are, aim, bre, bim, d_abr.reshape(N_STATE, 1), d_abi.reshape(N_STATE, 1),
        _diag_of_b(g_bexp[:, :, :CH_N]), _diag_of_b(g_bexp[:, :, CH_N:]))
    g_c_re = _diag_of_c(g_cexp[:, :CH_N, :])
    g_c_im = -_diag_of_c(g_cexp[:, CH_N:, :])

    def assemble(du, dqkv, dgs, dga):
        row = jnp.concatenate([du.astype(mx), dqkv, dgs, dga], axis=-1)
        return row, _colsum(row)

    dprojb, g_b_in = _rowwise("in_assemble", assemble, [_time_unperm(du_p), dqkvb, dgsb, dgab], [],
                              [(IN_COLS, mx)], [IN_COLS])
    ex.grad("w_in", _mm("in_dw", h0b, dprojb, "tn",
                        carry=ex.carry(ici=["w_xo", "w_xq", "w_mix_out", "w_att_up"])))
    dh0 = _mm("in_dx", dprojb, ex.weight("w_in"), "nt", extras=(dr1,), epilogue=lambda r, d: (r + al * d,),
              carry=ex.carry(ici=["w_in"]))

    def ln_in_bwd(x, dout, g, b):
        _, vjp = jax.vjp(_ln, x, g, b)
        return vjp(dout)

    dx, g_ln_in_g, g_ln_in_b = _rowwise("ln_in_bwd", ln_in_bwd, [x, dh0], [sp["ln_in_g"], sp["ln_in_b"]],
                                        [(D_MODEL, F32)], [D_MODEL, D_MODEL], carry=ex.finish_carry())

    small = {"ln_in_g": g_ln_in_g, "ln_in_b": g_ln_in_b, "b_in": g_b_in, "ssm_log_dt": g_ldt, "ssm_a_re": g_are,
             "ssm_a_im": g_aim, "ssm_b_re": g_bre, "ssm_b_im": g_bim, "ssm_c_re": g_c_re, "ssm_c_im": g_c_im,
             "ssm_d": g_ssm_d, "b_glu": g_b_glu, "b_mix_out": g_b_mix, "ln1_g": g_ln1_g, "ln1_b": g_ln1_b,
             "ln2_g": g_ln2_g, "ln2_b": g_ln2_b, "b_ff1": g_b_ff1, "b_ff2": g_b_ff2, "ln3_g": g_ln3_g,
             "ln3_b": g_ln3_b}
    return loss, dx, small


def _piece_shape(k, n, axis):
    return (k // 2, n // 4) if axis == 1 else (k // 8, n)


def _aligned(v, m):
    return v if isinstance(v, int) else pl.multiple_of(v, m)


def _full_piece(ref, k, n, axis, chip, half):
    pr, pc = _piece_shape(k, n, axis)
    if axis == 1:
        return ref.at[pl.ds(_aligned(half * pr, 8), pr), pl.ds(_aligned(chip * pc, 128), pc)]
    return ref.at[pl.ds(_aligned(chip * (2 * pr) + half * pr, 8), pr), :]


def _shard_piece(ref, k, n, axis, half):
    pr, _ = _piece_shape(k, n, axis)
    return ref.at[pl.ds(_aligned(half * pr, 8), pr), :]


def _mesh_pos():
    x, y, c = lax.axis_index("x"), lax.axis_index("y"), lax.axis_index("c")
    other_chips = [(1 - x, y), (x, 1 - y), (1 - x, 1 - y)]
    return x, y, c, other_chips


def _remote(src, dst, send_sem, recv_sem, dev):
    return pltpu.make_async_remote_copy(src_ref=src, dst_ref=dst, send_sem=send_sem, recv_sem=recv_sem,
                                        device_id=dev, device_id_type=MESH)


def _placed(name, fn, n_steps, where, ins, out_sds, out_block, out_index):
    def body(w_ref, *refs):
        o_ref = refs[-1]
        o_ref[...] = fn(*[r[...] for r in refs[:-1]]).astype(o_ref.dtype)

    grid_spec = pltpu.PrefetchScalarGridSpec(
        num_scalar_prefetch=1, grid=(n_steps,), in_specs=[pl.BlockSpec(bs, idx) for _, bs, idx in ins],
        out_specs=pl.BlockSpec(out_block, out_index))
    return pl.pallas_call(body, name=name, grid_spec=grid_spec, out_shape=out_sds,
                          compiler_params=_cparams(1))(where, *[a for a, _, _ in ins])


def _gather_copies(widx):
    geo = [BIG[i][1:] for i in widx]

    def ici(full, wi, j, chip, send_sems, recv_sems, c, dev):
        k, n, ax = geo[wi]
        piece = _full_piece(full[wi], k, n, ax, chip, c)
        return _remote(piece, piece, send_sems.at[wi * 6 + j], recv_sems.at[wi * 6 + j], dev)

    def d2d(full, wi, j, chip, half, send_sems, recv_sems, sib):
        k, n, ax = geo[wi]
        piece = _full_piece(full[wi], k, n, ax, chip, half)
        return _remote(piece, piece, send_sems.at[wi * 6 + 3 + j], recv_sems.at[wi * 6 + 3 + j], sib)

    def start(_, full, send_sems, recv_sems):
        x, y, c, chips = _mesh_pos()
        for wi in range(len(geo)):
            for j, (qx, qy) in enumerate(chips):
                ici(full, wi, j, 2 * x + y, send_sems, recv_sems, c, (qx, qy, c)).start()

    def finish(_, full, send_sems, recv_sems):
        x, y, c, chips = _mesh_pos()
        sib = (x, y, 1 - c)
        for wi in range(len(geo)):
            for j, (qx, qy) in enumerate(chips):
                ici(full, wi, j, 2 * qx + qy, send_sems, recv_sems, c, (qx, qy, c)).wait_recv()
                d2d(full, wi, j, 2 * qx + qy, c, send_sems, recv_sems, sib).start()
        for wi in range(len(geo)):
            for j, (qx, qy) in enumerate(chips):
                d2d(full, wi, j, 2 * qx + qy, 1 - c, send_sems, recv_sems, sib).wait_recv()
        for wi in range(len(geo)):
            for j, (qx, qy) in enumerate(chips):
                ici(full, wi, j, 2 * x + y, send_sems, recv_sems, c, (qx, qy, c)).wait_send()
                d2d(full, wi, j, 2 * qx + qy, c, send_sems, recv_sems, sib).wait_send()

    return start, finish, 6 * len(geo)


def _swap_copies(widx):
    geo = [BIG[i][1:] for i in widx]

    def copies(g, got, send_sems, recv_sems, base):
        x, y, c, _ = _mesh_pos()
        return [_remote(_full_piece(g[wi], k, n, ax, q, 1 - c), got[wi].at[q], send_sems.at[base + wi * 4 + q],
                        recv_sems.at[base + wi * 4 + q], (x, y, 1 - c))
                for wi, (k, n, ax) in enumerate(geo) for q in range(4)]

    def start(g, got, send_sems, recv_sems, base=0):
        for cp in copies(g, got, send_sems, recv_sems, base):
            cp.start()

    def finish(g, got, send_sems, recv_sems, base=0):
        for cp in copies(g, got, send_sems, recv_sems, base):
            cp.wait()

    return start, finish, 4 * len(geo)


def _swap_shapes(widx):
    return [jax.ShapeDtypeStruct((4,) + _piece_shape(*BIG[i][1:]), F32) for i in widx]


def _reduce_swap_halves(tag, grads, widx):
    nw = len(widx)
    start, finish, n_sems = _swap_copies(widx)

    def body(*refs):
        start(refs[:nw], refs[nw:2 * nw], *refs[2 * nw:])
        finish(refs[:nw], refs[nw:2 * nw], *refs[2 * nw:])

    return pl.pallas_call(
        body, name="reduce_swap_halves_" + tag, in_specs=[HBM_SPEC] * nw, out_specs=[HBM_SPEC] * nw,
        out_shape=_swap_shapes(widx),
        scratch_shapes=[pltpu.SemaphoreType.DMA((n_sems,)), pltpu.SemaphoreType.DMA((n_sems,))])(*grads)


def _owner_copies(nw):
    def copies(p, out, send_sems, recv_sems, base):
        x, y, c, chips = _mesh_pos()
        return [_remote(p[wi].at[2 * qx + qy], out[wi].at[j], send_sems.at[base + wi * 3 + j],
                        recv_sems.at[base + wi * 3 + j], (qx, qy, c))
                for wi in range(nw) for j, (qx, qy) in enumerate(chips)]

    def start(p, out, send_sems, recv_sems, base=0):
        for cp in copies(p, out, send_sems, recv_sems, base):
            cp.start()

    def finish(p, out, send_sems, recv_sems, base=0):
        for cp in copies(p, out, send_sems, recv_sems, base):
            cp.wait()

    return start, finish, 3 * nw


def _join_carries(a, b):
    if a is None or b is None:
        return a if b is None else b
    n_i, n_o = len(a.ins), len(a.outs)
    outs = list(a.outs) + [o + n_i if isinstance(o, int) else o for o in b.outs]

    def start(c_in, c_out, send_sems, recv_sems):
        a.start(c_in[:n_i], c_out[:n_o], send_sems, recv_sems)
        b.start(c_in[n_i:], c_out[n_o:], send_sems, recv_sems, base=a.n_sems)

    def finish(c_in, c_out, send_sems, recv_sems):
        a.finish(c_in[:n_i], c_out[:n_o], send_sems, recv_sems)
        b.finish(c_in[n_i:], c_out[n_o:], send_sems, recv_sems, base=a.n_sems)

    def done(res):
        a.done(res[:n_o])
        b.done(res[n_o:])

    return _Carry(a.ins + b.ins, outs, a.n_sems + b.n_sems, start, finish, done)


def _share_copies():
    def copy(out, wi, half, send_sems, recv_sems, sib):
        _, k, n, ax = BIG[wi]
        piece = _shard_piece(out[wi], k, n, ax, half)
        return _remote(piece, piece, send_sems.at[wi], recv_sems.at[wi], sib)

    def start(_, out, send_sems, recv_sems):
        x, y, c, _ = _mesh_pos()
        for wi in range(len(BIG)):
            copy(out, wi, c, send_sems, recv_sems, (x, y, 1 - c)).start()

    def finish(_, out, send_sems, recv_sems):
        x, y, c, _ = _mesh_pos()
        for wi in range(len(BIG)):
            copy(out, wi, 1 - c, send_sems, recv_sems, (x, y, 1 - c)).wait_recv()
            copy(out, wi, c, send_sems, recv_sems, (x, y, 1 - c)).wait_send()

    return start, finish, len(BIG)


def _allreduce_small(v):
    r = v.shape[0]
    rh = r // 2
    assert rh % 8 == 0

    def body(v_ref, o_ref, sib_buf, chip_buf, send_sems, recv_sems):
        x, y, c, chips = _mesh_pos()
        me = 2 * x + y
        sib = (x, y, 1 - c)
        mine = pl.ds(pl.multiple_of(c * rh, 8), rh)
        other = pl.ds(pl.multiple_of((1 - c) * rh, 8), rh)
        swap = _remote(v_ref.at[other], sib_buf, send_sems.at[0], recv_sems.at[0], sib)
        swap.start()
        swap.wait()
        chip_buf[me] = v_ref[mine, :] + sib_buf[...]
        cps = []
        for j, (qx, qy) in enumerate(chips):
            cp = _remote(chip_buf.at[me], chip_buf.at[me], send_sems.at[1 + j], recv_sems.at[1 + j], (qx, qy, c))
            cp.start()
            cps.append(cp)
        for j, (qx, qy) in enumerate(chips):
            slot = chip_buf.at[2 * qx + qy]
            _remote(slot, slot, send_sems.at[1 + j], recv_sems.at[1 + j], (qx, qy, c)).wait_recv()
        for cp in cps:
            cp.wait_send()
        o_ref[mine, :] = ((chip_buf[0] + chip_buf[1]) + chip_buf[2]) + chip_buf[3]
        back = _remote(o_ref.at[mine], o_ref.at[mine], send_sems.at[4], recv_sems.at[4], sib)
        back.start()
        _remote(o_ref.at[other], o_ref.at[other], send_sems.at[4], recv_sems.at[4], sib).wait_recv()
        back.wait_send()

    return pl.pallas_call(
        body, name="allreduce_small", in_specs=[VMEM_SPEC], out_specs=VMEM_SPEC,
        out_shape=jax.ShapeDtypeStruct((r, 128), F32),
        scratch_shapes=[pltpu.VMEM((rh, 128), F32), pltpu.VMEM((4, rh, 128), F32),
                        pltpu.SemaphoreType.DMA((5,)), pltpu.SemaphoreType.DMA((5,))],
        compiler_params=pltpu.CompilerParams(vmem_limit_bytes=VMEM_LIMIT))(v)


def _as2d(a):
    a = a.reshape((-1, a.shape[-1])) if a.ndim > 1 else a.reshape(1, -1)
    return a


def _adamw_small(quads):
    n = len(quads)

    def body(*refs):
        for i in range(n):
            w, g, m, v = (r[...] for r in refs[4 * i:4 * i + 4])
            for ref, val in zip(refs[4 * n + 3 * i:4 * n + 3 * i + 3], _adamw(w, g, m, v)):
                ref[...] = val

    return pl.pallas_call(
        body, name="adamw_small", in_specs=[VMEM_SPEC] * (4 * n), out_specs=[VMEM_SPEC] * (3 * n),
        out_shape=[jax.ShapeDtypeStruct(q[0].shape, F32) for q in quads for _ in range(3)],
        compiler_params=pltpu.CompilerParams(vmem_limit_bytes=VMEM_LIMIT))(*[a for q in quads for a in q])


def _where():
    return jnp.stack([2 * lax.axis_index("x") + lax.axis_index("y"), lax.axis_index("c")]).astype(jnp.int32)


_BIG_INDEX = {name: i for i, (name, _, _, _) in enumerate(BIG)}


class _Exchange:
    def __init__(self, inputs, where):
        self.inputs, self.where = inputs, where
        self.full, self.ready = {}, set()
        self.raw, self.got, self.parts, self.landed, self.geom = {}, {}, {}, {}, {}
        for name, k, n, ax in BIG:
            w2 = inputs[name][0]
            rs, cs = w2.shape
            tm = _tile(rs, 512)
            steps = rs // tm
            if ax == 1:
                blk, idx = (tm, cs), lambda i, w: (i, w[0])
            else:
                blk, idx = (tm, n), functools.partial(lambda i, w, steps: (w[0] * steps + i, 0), steps=steps)
            self.full[name] = _placed("cast_" + name, lambda w: w, steps, where, [(w2, (tm, cs), lambda i, w: (i, 0))],
                                      jax.ShapeDtypeStruct((k, n), MXU_DTYPE), blk, idx)

    def _gathered(self, names, outs):
        for name, o in zip(names, outs):
            self.full[name] = o
            self.ready.add(name)

    def gather_carry(self, names):
        start, finish, n_sems = _gather_copies([_BIG_INDEX[n] for n in names])
        return _Carry([self.full[n] for n in names], list(range(len(names))), n_sems, start, finish,
                      functools.partial(self._gathered, names))

    def weight(self, name):
        assert name in self.ready, name
        return self.full[name]

    def grad(self, name, g):
        self.raw[name] = g

    def _swapped(self, names, outs):
        for name, o in zip(names, outs):
            self.got[name] = o

    def _pair_sum(self, name):
        i = _BIG_INDEX[name]
        _, k, n, ax = BIG[i]
        g = self.raw[name]
        if name not in self.got:
            self._swapped([name], _reduce_swap_halves(name, [g], [i]))
        got = self.got[name]
        pr, pc = _piece_shape(k, n, ax)
        tm = _tile(pr, 512)
        spp = pr // tm
        self.geom[name] = (pr, pc, tm, spp)
        if ax == 1:
            g_idx = functools.partial(lambda i, w, spp: (w[1] * spp + i % spp, i // spp), spp=spp)
        else:
            g_idx = functools.partial(lambda i, w, spp: ((i // spp) * 2 * spp + w[1] * spp + i % spp, 0), spp=spp)
        self.parts[name] = _placed(
            "pair_sum_" + name, lambda a, b: a + b, 4 * spp, self.where,
            [(g, (tm, pc), g_idx), (got.reshape(4 * pr, pc), (tm, pc), lambda i, w: (i, 0))],
            jax.ShapeDtypeStruct((4 * pr, pc), BF16), (tm, pc), lambda i, w: (i, 0)).reshape(4, pr, pc)

    def _landed(self, names, outs):
        for name, o in zip(names, outs):
            self.landed[name] = o

    def carry(self, swap=(), ici=()):
        first = second = None
        if swap:
            widx = [_BIG_INDEX[n] for n in swap]
            start, finish, n_sems = _swap_copies(widx)
            first = _Carry([self.raw[n] for n in swap], _swap_shapes(widx), n_sems, start, finish,
                           functools.partial(self._swapped, list(swap)))
        if ici:
            for n in ici:
                self._pair_sum(n)
            start, finish, n_sems = _owner_copies(len(ici))
            parts = [self.parts[n] for n in ici]
            outs = [jax.ShapeDtypeStruct((3,) + p.shape[1:], p.dtype) for p in parts]
            second = _Carry(parts, outs, n_sems, start, finish, functools.partial(self._landed, list(ici)))
        return _join_carries(first, second)

    def _shared(self, outs):
        self.shards = dict(zip([b[0] for b in BIG], outs))

    def finish_carry(self):
        halves = []
        for name, _, _, _ in BIG:
            pr, pc, tm, spp = self.geom[name]
            ins = [(self.parts[name], (None, tm, pc), lambda i, w: (w[0], i, 0))]
            ins += [(self.landed[name], (None, tm, pc), functools.partial(lambda i, w, j: (j, i, 0), j=j))
                    for j in range(3)]
            halves.append(_placed("chip_sum_" + name,
                                  lambda a, b, c, d: ((a.astype(F32) + b.astype(F32)) + c.astype(F32)) + d.astype(F32),
                                  spp, self.where, ins, jax.ShapeDtypeStruct(self.inputs[name].shape[1:], F32), (tm, pc),
                                  functools.partial(lambda i, w, spp: (w[1] * spp + i, 0), spp=spp)))
        start, finish, n_sems = _share_copies()
        return _Carry(halves, list(range(len(halves))), n_sems, start, finish, self._shared)


def _step(inputs):
    x, mem, positions, target = inputs["x"][0], inputs["mem"][0], inputs["positions"], inputs["loss_target"][0]
    pos = positions.reshape(-1, 1)
    ex = _Exchange(inputs, _where())
    sp = {name: _as2d(inputs[name]) for name in SMALL}
    memb, = _rowwise("cast_mem", lambda m: (m,), [mem], [], [(D_MODEL, MXU_DTYPE)])

    loss, dx, gsmall = _local_step(x, memb, pos, target, sp, ex)
    gshard = ex.shards

    out = {}
    for name, _, _, _ in BIG:
        w2, m2, v2 = inputs[name][0], inputs["m_" + name][0], inputs["v_" + name][0]
        n = w2.shape[1]
        d, nm, nv = _rowwise("adamw_" + name, _adamw, [w2, gshard[name], m2, v2], [], [(n, F32)] * 3, tm=_tile(w2.shape[0], 512))
        lead = inputs[name].shape
        out[name] = (gshard[name].reshape(lead), d.reshape(lead), nm.reshape(lead), nv.reshape(lead))

    def tiles(a):
        flat = a.reshape(-1)
        n = -(-flat.shape[0] // 1024) * 1024
        return jnp.pad(flat, (0, n - flat.shape[0])).reshape(n // 128, 128)

    pieces = [tiles(loss[:, :1])] + [tiles(gsmall[name]) for name in SMALL]
    if sum(p.shape[0] for p in pieces) % 16:
        pieces.append(jnp.zeros((8, 128), F32))
    red = _allreduce_small(jnp.concatenate(pieces, axis=0))
    loss_total = red[0, 0]
    grads, off = {}, pieces[0].shape[0]
    for name, p in zip(SMALL, pieces[1:]):
        shp = _as2d(inputs[name]).shape
        grads[name] = red[off:off + p.shape[0]].reshape(-1)[:shp[0] * shp[1]].reshape(shp)
        off += p.shape[0]
    upd = _adamw_small([(_as2d(inputs[n]), grads[n], _as2d(inputs["m_" + n]), _as2d(inputs["v_" + n])) for n in SMALL])
    for i, name in enumerate(SMALL):
        shp = inputs[name].shape
        out[name] = (grads[name].reshape(shp),) + tuple(t.reshape(shp) for t in upd[3 * i:3 * i + 3])
    return loss_total, dx.reshape(inputs["x"].shape), out


_ARG_NAMES = (("x", "mem", "positions") + WEIGHT_ORDER + ("loss_target",) + tuple("m_" + n for n in WEIGHT_ORDER)
              + tuple("v_" + n for n in WEIGHT_ORDER))


def kernel(x, mem, positions, ln_in_g, ln_in_b, w_in, b_in, ssm_log_dt, ssm_a_re, ssm_a_im, ssm_b_re, ssm_b_im, ssm_c_re, ssm_c_im, ssm_d, w_glu, b_glu, w_att_up, w_mix_out, b_mix_out, ln1_g, ln1_b, w_xq, w_xkv, w_xo, ln2_g, ln2_b, w_ff1, b_ff1, w_ff2, b_ff2, ln3_g, ln3_b, loss_target, m_ln_in_g, m_ln_in_b, m_w_in, m_b_in, m_ssm_log_dt, m_ssm_a_re, m_ssm_a_im, m_ssm_b_re, m_ssm_b_im, m_ssm_c_re, m_ssm_c_im, m_ssm_d, m_w_glu, m_b_glu, m_w_att_up, m_w_mix_out, m_b_mix_out, m_ln1_g, m_ln1_b, m_w_xq, m_w_xkv, m_w_xo, m_ln2_g, m_ln2_b, m_w_ff1, m_b_ff1, m_w_ff2, m_b_ff2, m_ln3_g, m_ln3_b, v_ln_in_g, v_ln_in_b, v_w_in, v_b_in, v_ssm_log_dt, v_ssm_a_re, v_ssm_a_im, v_ssm_b_re, v_ssm_b_im, v_ssm_c_re, v_ssm_c_im, v_ssm_d, v_w_glu, v_b_glu, v_w_att_up, v_w_mix_out, v_b_mix_out, v_ln1_g, v_ln1_b, v_w_xq, v_w_xkv, v_w_xo, v_ln2_g, v_ln2_b, v_w_ff1, v_b_ff1, v_w_ff2, v_b_ff2, v_ln3_g, v_ln3_b):
    args = (x, mem, positions, ln_in_g, ln_in_b, w_in, b_in, ssm_log_dt, ssm_a_re, ssm_a_im, ssm_b_re, ssm_b_im, ssm_c_re, ssm_c_im, ssm_d, w_glu, b_glu, w_att_up, w_mix_out, b_mix_out, ln1_g, ln1_b, w_xq, w_xkv, w_xo, ln2_g, ln2_b, w_ff1, b_ff1, w_ff2, b_ff2, ln3_g, ln3_b, loss_target, m_ln_in_g, m_ln_in_b, m_w_in, m_b_in, m_ssm_log_dt, m_ssm_a_re, m_ssm_a_im, m_ssm_b_re, m_ssm_b_im, m_ssm_c_re, m_ssm_c_im, m_ssm_d, m_w_glu, m_b_glu, m_w_att_up, m_w_mix_out, m_b_mix_out, m_ln1_g, m_ln1_b, m_w_xq, m_w_xkv, m_w_xo, m_ln2_g, m_ln2_b, m_w_ff1, m_b_ff1, m_w_ff2, m_b_ff2, m_ln3_g, m_ln3_b, v_ln_in_g, v_ln_in_b, v_w_in, v_b_in, v_ssm_log_dt, v_ssm_a_re, v_ssm_a_im, v_ssm_b_re, v_ssm_b_im, v_ssm_c_re, v_ssm_c_im, v_ssm_d, v_w_glu, v_b_glu, v_w_att_up, v_w_mix_out, v_b_mix_out, v_ln1_g, v_ln1_b, v_w_xq, v_w_xkv, v_w_xo, v_ln2_g, v_ln2_b, v_w_ff1, v_b_ff1, v_w_ff2, v_b_ff2, v_ln3_g, v_ln3_b)
    assert len(args) == len(_ARG_NAMES)
    inputs = dict(zip(_ARG_NAMES, args))
    loss, dx, out = _step(inputs)
    res = [loss, dx]
    for k in range(4):
        res += [out[name][k] for name in WEIGHT_ORDER]
    return tuple(res)
```

```python
import functools
import math

import numpy as np
import jax
import jax.numpy as jnp
from jax import lax
from jax.experimental import pallas as pl
from jax.experimental.pallas import tpu as pltpu

F32 = jnp.float32
BF16 = jnp.bfloat16
MXU_DTYPE = jnp.bfloat16

D_MODEL = 1024
SSM_GROUP = 16
SSM_WIDTH = 768
SSM_GROUPS = 48
SSM_STATE = 64
N_STATE = SSM_GROUPS * SSM_STATE
SSM_CHUNKS = 6
CH_W = 128
CH_N = 512
ATT_HEAD_DIM = 64
ATT_HPG = 4
ATT_GROUPW = ATT_HPG * ATT_HEAD_DIM
DILATIONS = (1, 4, 16)
ATT_BLK = 128
ATT_SCALE = ATT_HEAD_DIM ** -0.5
ROT_DIM = 16
ROPE_THETA = 500000.0
XATT_HEADS = 4
XATT_HEAD_DIM = 256
XATT_SCALE = XATT_HEAD_DIM ** -0.5
D_FF = 4096
IN_COLS = 5120
DEEPNORM_ALPHA = 2.0 ** 0.25
LN_EPS = 1e-5
NEG_INF = -1e30
ADAM_LR = 0.001
ADAM_B1 = 0.9
ADAM_B2 = 0.999
ADAM_EPS = 1e-08
ADAM_WD = 0.01
ADAM_STEP = 10

N_SEG = 32
VMEM_LIMIT = 56 * 1024 * 1024
MESH = pl.DeviceIdType.MESH
HBM_SPEC = pl.BlockSpec(memory_space=pltpu.HBM)
VMEM_SPEC = pl.BlockSpec(memory_space=pltpu.VMEM)

BIG = (("w_in", 1024, 5120, 1), ("w_glu", 768, 2048, 1), ("w_att_up", 256, 1024, 1),
       ("w_mix_out", 1024, 1024, 0), ("w_xq", 1024, 1024, 0), ("w_xkv", 1024, 2048, 1),
       ("w_xo", 1024, 1024, 0), ("w_ff1", 1024, 4096, 1), ("w_ff2", 4096, 1024, 0))
SMALL = ("ln_in_g", "ln_in_b", "b_in", "ssm_log_dt", "ssm_a_re", "ssm_a_im", "ssm_b_re", "ssm_b_im",
         "ssm_c_re", "ssm_c_im", "ssm_d", "b_glu", "b_mix_out", "ln1_g", "ln1_b", "ln2_g", "ln2_b",
         "b_ff1", "b_ff2", "ln3_g", "ln3_b")
WEIGHT_ORDER = ("ln_in_g", "ln_in_b", "w_in", "b_in", "ssm_log_dt", "ssm_a_re", "ssm_a_im", "ssm_b_re",
                "ssm_b_im", "ssm_c_re", "ssm_c_im", "ssm_d", "w_glu", "b_glu", "w_att_up", "w_mix_out",
                "b_mix_out", "ln1_g", "ln1_b", "w_xq", "w_xkv", "w_xo", "ln2_g", "ln2_b", "w_ff1", "b_ff1",
                "w_ff2", "b_ff2", "ln3_g", "ln3_b")


def _cparams(n_axes):
    return pltpu.CompilerParams(dimension_semantics=("arbitrary",) * n_axes, vmem_limit_bytes=VMEM_LIMIT)


class _Carry:
    def __init__(self, ins, outs, n_sems, start, finish, done):
        self.ins, self.outs, self.n_sems, self.start, self.finish, self.done = ins, outs, n_sems, start, finish, done


def _call(name, body, grid, in_specs, out_specs, out_shape, args, scratch_shapes=(), carry=None):
    in_specs, out_specs, out_shape = list(in_specs), list(out_specs), list(out_shape)
    params = _cparams(len(grid))
    if carry is None:
        return pl.pallas_call(body, name=name, grid=grid, in_specs=in_specs, out_specs=out_specs, out_shape=out_shape,
                              scratch_shapes=list(scratch_shapes), compiler_params=params)(*args)
    n_in, n_out, n_ci, n_co = len(in_specs), len(out_specs), len(carry.ins), len(carry.outs)
    n_scr = len(scratch_shapes)

    def wrapped(*refs):
        ins, c_in = refs[:n_in], refs[n_in:n_in + n_ci]
        outs, c_out = refs[n_in + n_ci:n_in + n_ci + n_out], refs[n_in + n_ci + n_out:n_in + n_ci + n_out + n_co]
        scratch = refs[n_in + n_ci + n_out + n_co:n_in + n_ci + n_out + n_co + n_scr]
        send_sems, recv_sems = refs[-2:]
        ids = [pl.program_id(a) for a in range(len(grid))]
        first = functools.reduce(jnp.logical_and, [i == 0 for i in ids])
        last = functools.reduce(jnp.logical_and, [i == g - 1 for i, g in zip(ids, grid)])

        @pl.when(first)
        def _():
            carry.start(c_in, c_out, send_sems, recv_sems)

        body(*ins, *outs, *scratch)

        @pl.when(last)
        def _():
            carry.finish(c_in, c_out, send_sems, recv_sems)

    c_shapes = [jax.ShapeDtypeStruct(carry.ins[o].shape, carry.ins[o].dtype) if isinstance(o, int) else o
                for o in carry.outs]
    aliases = {n_in + o: n_out + i for i, o in enumerate(carry.outs) if isinstance(o, int)}
    res = pl.pallas_call(
        wrapped, name=name, grid=grid, in_specs=in_specs + [HBM_SPEC] * n_ci, out_specs=out_specs + [HBM_SPEC] * n_co,
        out_shape=out_shape + c_shapes, input_output_aliases=aliases,
        scratch_shapes=list(scratch_shapes) + [pltpu.SemaphoreType.DMA((carry.n_sems,))] * 2,
        compiler_params=params)(*args, *carry.ins)
    carry.done(res[n_out:])
    return res[:n_out]


def _rowwise(name, fn, rows, consts, outs, reds=(), tm=512, touts=(), carry=None):
    n_rows = (rows[0][0] if isinstance(rows[0], tuple) else rows[0]).shape[-2]
    tm = min(tm, n_rows)
    assert n_rows % tm == 0, (name, n_rows, tm)
    specs, args = [], []
    for r in rows:
        if isinstance(r, tuple) and len(r) == 3:
            arr, width, cb = r
            specs.append(pl.BlockSpec((tm, width), functools.partial(lambda i, cb: (i, cb), cb=cb)))
        elif isinstance(r, tuple):
            arr, slot = r
            specs.append(pl.BlockSpec((None, tm, arr.shape[2]), functools.partial(lambda i, s: (s, i, 0), s=slot)))
        else:
            arr = r
            specs.append(pl.BlockSpec((tm, arr.shape[1]), lambda i: (i, 0)))
        args.append(arr)
        assert arr.shape[-2] == n_rows, (name, arr.shape, n_rows)
    for cst in consts:
        specs.append(pl.BlockSpec(cst.shape, lambda i: (0, 0)))
        args.append(cst)
    n_r, n_c, n_o, n_d = len(rows), len(consts), len(outs) + len(touts), len(reds)
    out_shape = [jax.ShapeDtypeStruct((n_rows, c), dt) for c, dt in outs]
    out_specs = [pl.BlockSpec((tm, c), lambda i: (i, 0)) for c, _ in outs]
    out_shape += [jax.ShapeDtypeStruct((r, n_rows), dt) for r, dt in touts]
    out_specs += [pl.BlockSpec((r, tm), lambda i: (0, i)) for r, _ in touts]
    out_shape += [jax.ShapeDtypeStruct((1, c), F32) for c in reds]
    out_specs += [pl.BlockSpec((1, c), lambda i: (0, 0)) for c in reds]

    def body(*refs):
        ins = [r[...] for r in refs[:n_r + n_c]]
        o_refs = refs[n_r + n_c:n_r + n_c + n_o]
        d_refs = refs[n_r + n_c + n_o:]
        res = fn(*ins)
        res = res if isinstance(res, (tuple, list)) else (res,)
        assert len(res) == n_o + n_d, (name, len(res))
        for ref, val in zip(o_refs, res[:n_o]):
            ref[...] = val.astype(ref.dtype)
        first = pl.program_id(0) == 0
        for ref, val in zip(d_refs, res[n_o:]):
            @pl.when(first)
            def _(ref=ref, val=val):
                ref[...] = val

            @pl.when(jnp.logical_not(first))
            def _(ref=ref, val=val):
                ref[...] += val

    return _call(name, body, (n_rows // tm,), specs, out_specs, out_shape, args, carry=carry)


def _colsum(v):
    return jnp.sum(v.astype(F32), axis=0, keepdims=True)


_DIMS = {"nn": (((1,), (0,)), ((), ())), "nt": (((1,), (1,)), ((), ())), "tn": (((0,), (0,)), ((), ()))}


def _tile(dim, want):
    if dim <= want:
        return dim
    return max(t for t in range(128, want + 1, 128) if dim % t == 0)


def _dot(a, b, mode):
    return lax.dot_general(a.astype(MXU_DTYPE), b.astype(MXU_DTYPE), _DIMS[mode], preferred_element_type=F32)


def _mm(name, a, b, mode, *, bias=None, extras=(), epilogue=None, out_dtypes=(F32,), tm=1024, tn=1024, tk=1024,
        carry=None, colsum=False):
    if mode == "nn":
        (m, k), (_, n) = a.shape, b.shape
    elif mode == "nt":
        (m, k), (n, _) = a.shape, b.shape
    else:
        (k, m), (_, n) = a.shape, b.shape
    if k > tk:
        tk = 5 * tk
    tn = _tile(n, tn)
    tk = _tile(k, tk)
    nk = k // tk

    def vmem_bytes(rows):
        blocks = rows * tk * a.dtype.itemsize + tk * tn * b.dtype.itemsize
        blocks += sum(rows * tn * e.dtype.itemsize for e in extras)
        blocks += sum(rows * tn * jnp.dtype(dt).itemsize for dt in out_dtypes)
        return 2 * blocks + (rows * tn * 4 if nk > 1 else 0)

    tm = _tile(m, tm if mode == "tn" else 2 * tm)
    while vmem_bytes(tm) > 3 * VMEM_LIMIT // 4 and tm % 256 == 0:
        tm //= 2
    while nk == 1 and k > 1024 and (m // tm) * (n // tn) < 4 and tm % 256 == 0:
        tm //= 2
    assert m % tm == 0 and n % tn == 0 and k % tk == 0, (name, m, n, k)
    a_spec = {"nn": pl.BlockSpec((tm, tk), lambda i, j, kk: (i, kk)),
              "nt": pl.BlockSpec((tm, tk), lambda i, j, kk: (i, kk)),
              "tn": pl.BlockSpec((tk, tm), lambda i, j, kk: (kk, i))}[mode]
    b_spec = {"nn": pl.BlockSpec((tk, tn), lambda i, j, kk: (kk, j)),
              "nt": pl.BlockSpec((tn, tk), lambda i, j, kk: (j, kk)),
              "tn": pl.BlockSpec((tk, tn), lambda i, j, kk: (kk, j))}[mode]
    specs, args = [a_spec, b_spec], [a, b]
    if bias is not None:
        specs.append(pl.BlockSpec((1, tn), lambda i, j, kk: (0, j)))
        args.append(bias)
    for e in extras:
        specs.append(pl.BlockSpec((tm, tn), lambda i, j, kk: (i, j)))
        args.append(e)
    n_e, n_o = len(extras), len(out_dtypes)
    has_bias = bias is not None

    def body(*refs):
        a_ref, b_ref = refs[0], refs[1]
        pos = 2
        bias_ref = refs[pos] if has_bias else None
        pos += int(has_bias)
        e_refs = refs[pos:pos + n_e]
        o_refs = refs[pos + n_e:pos + n_e + n_o]
        sum_ref = refs[pos + n_e + n_o] if colsum else None
        acc_ref = refs[pos + n_e + n_o + int(colsum)] if nk > 1 else None
        part = _dot(a_ref[...], b_ref[...], mode)

        def finish(r):
            if has_bias:
                r = r + bias_ref[...]
            res = epilogue(r, *[e[...] for e in e_refs]) if epilogue is not None else (r,)
            for ref, val in zip(o_refs, res):
                ref[...] = val.astype(ref.dtype)
            if colsum:
                sum_ref[...] = _colsum(res[0])

        if nk == 1:
            finish(part)
        else:
            kk = pl.program_id(2)

            @pl.when(kk == 0)
            def _():
                acc_ref[...] = part

            @pl.when(kk > 0)
            def _():
                acc_ref[...] += part

            @pl.when(kk == nk - 1)
            def _():
                finish(acc_ref[...])

    out_specs = [pl.BlockSpec((tm, tn), lambda i, j, kk: (i, j)) for _ in out_dtypes]
    out_shape = [jax.ShapeDtypeStruct((m, n), dt) for dt in out_dtypes]
    if colsum:
        out_specs.append(pl.BlockSpec((None, 1, tn), lambda i, j, kk: (i, 0, j)))
        out_shape.append(jax.ShapeDtypeStruct((m // tm, 1, n), F32))
    res = _call(name, body, (m // tm, n // tn, nk), specs, out_specs, out_shape, args,
                scratch_shapes=[pltpu.VMEM((tm, tn), F32)] if nk > 1 else [], carry=carry)
    return res[0] if len(res) == 1 else res


def _ssm_wgrads(u, dy, g_re, g_im, h_re, h_im, tk=1024):
    s = u.shape[0]
    tk = min(tk, s)
    nk = s // tk
    assert tk % N_SEG == 0

    def body(u_ref, dy_ref, gre_ref, gim_ref, hre_ref, him_ref, lre_ref, lim_ref, db_ref, dc_ref, dar_ref, dai_ref,
             pre_ref, pim_ref):
        kk = pl.program_id(1)
        u_blk, dy_blk = u_ref[...], dy_ref[...]
        g_r, g_i, h_r, h_i = gre_ref[...], gim_ref[...], hre_ref[...], him_ref[...]
        d_b = jnp.concatenate([_dot(u_blk, g_r, "tn"), _dot(u_blk, g_i, "tn")], axis=1)
        d_c = jnp.concatenate([_dot(h_r, dy_blk, "tn"), _dot(h_i, dy_blk, "tn")], axis=0)

        @pl.when(kk == 0)
        def _():
            first_row = lax.broadcasted_iota(jnp.int32, (N_SEG, CH_N), 0) == 0
            pre_ref[...] = jnp.where(first_row, 0.0, pltpu.roll(lre_ref[...], 1, 0))
            pim_ref[...] = jnp.where(first_row, 0.0, pltpu.roll(lim_ref[...], 1, 0))

        p_r = jnp.concatenate([pre_ref[...], h_r[:tk - N_SEG]], axis=0)
        p_i = jnp.concatenate([pim_ref[...], h_i[:tk - N_SEG]], axis=0)
        pre_ref[...] = h_r[tk - N_SEG:]
        pim_ref[...] = h_i[tk - N_SEG:]
        d_ar = jnp.sum(g_r * p_r + g_i * p_i, axis=0, keepdims=True)
        d_ai = jnp.sum(g_i * p_r - g_r * p_i, axis=0, keepdims=True)

        @pl.when(kk == 0)
        def _():
            db_ref[...] = d_b
            dc_ref[...] = d_c
            dar_ref[...] = d_ar
            dai_ref[...] = d_ai

        @pl.when(kk > 0)
        def _():
            db_ref[...] += d_b
            dc_ref[...] += d_c
            dar_ref[...] += d_ar
            dai_ref[...] += d_ai

    chan = pl.BlockSpec((tk, CH_W), lambda j, kk: (kk, j))
    state = pl.BlockSpec((tk, CH_N), lambda j, kk: (kk, j))
    last = pl.BlockSpec((N_SEG, CH_N), lambda j, kk: (s // N_SEG - 1, j))
    row = pl.BlockSpec((1, CH_N), lambda j, kk: (0, j))
    return pl.pallas_call(
        body, name="ssm_wgrads", grid=(SSM_CHUNKS, nk),
        in_specs=[chan, chan, state, state, state, state, last, last],
        out_specs=[pl.BlockSpec((None, CH_W, 2 * CH_N), lambda j, kk: (j, 0, 0)),
                   pl.BlockSpec((None, 2 * CH_N, CH_W), lambda j, kk: (j, 0, 0)), row, row],
        out_shape=[jax.ShapeDtypeStruct((SSM_CHUNKS, CH_W, 2 * CH_N), F32),
                   jax.ShapeDtypeStruct((SSM_CHUNKS, 2 * CH_N, CH_W), F32),
                   jax.ShapeDtypeStruct((1, N_STATE), F32), jax.ShapeDtypeStruct((1, N_STATE), F32)],
        scratch_shapes=[pltpu.VMEM((N_SEG, CH_N), F32)] * 2,
        compiler_params=_cparams(2))(u, dy, g_re, g_im, h_re, h_im, h_re, h_im)


SCAN_LB = 256


def _split_by_scan_block(mat, axis):
    halves = []
    for l in range(CH_N // SCAN_LB):
        re = lax.slice_in_dim(mat, l * SCAN_LB, (l + 1) * SCAN_LB, axis=axis)
        im = lax.slice_in_dim(mat, CH_N + l * SCAN_LB, CH_N + (l + 1) * SCAN_LB, axis=axis)
        halves.append(jnp.concatenate([re, im], axis=axis))
    return jnp.stack(halves, axis=1).reshape((-1,) + halves[0].shape[1:])


def _ssm_scan(name, chan, expand12, contract12, a_re, a_im, d_row, reverse, carry=None):
    s = chan.shape[0]
    seg_len = s // N_SEG
    n_sq = int(math.log2(seg_len))
    assert 2 ** n_sq == seg_len
    rb = min(512, s)
    per_chunk = CH_N // SCAN_LB

    def body(are_ref, aim_ref, ch_ref, e_ref, k_ref, d_ref, hre_ref, him_ref, o_ref, wre_ref, wim_ref, ere, eim, cre, cim):
        e_mat, k_mat = e_ref[...], k_ref[...]
        for r in range(s // rb):
            rows = slice(r * rb, (r + 1) * rb)
            w = _dot(ch_ref[rows, :], e_mat, "nt" if reverse else "nn")
            wre_ref[rows, :] = w[:, :SCAN_LB]
            wim_ref[rows, :] = w[:, SCAN_LB:]

        ar1 = are_ref[...]
        ai1 = -aim_ref[...] if reverse else aim_ref[...]
        ar = jnp.broadcast_to(ar1, (N_SEG, SCAN_LB))
        ai = jnp.broadcast_to(ai1, (N_SEG, SCAN_LB))

        def rows_of(k):
            kk = seg_len - 1 - k if reverse else k
            return pl.ds(pl.multiple_of(kk * N_SEG, N_SEG), N_SEG)

        def local(k, carry):
            hr, hi = carry
            rows = rows_of(k)
            nr = ar * hr - ai * hi + wre_ref[rows, :]
            ni = ar * hi + ai * hr + wim_ref[rows, :]
            hre_ref[rows, :] = nr
            him_ref[rows, :] = ni
            return nr, ni

        zero = jnp.zeros((N_SEG, SCAN_LB), F32)
        er, ei = lax.fori_loop(0, seg_len, local, (zero, zero))
        ere[...] = er
        eim[...] = ei
        pr, pi = ar1, ai1
        for _ in range(n_sq):
            pr, pi = pr * pr - pi * pi, 2.0 * pr * pi
        cr = jnp.zeros((1, SCAN_LB), F32)
        ci = jnp.zeros((1, SCAN_LB), F32)
        for jj in range(N_SEG):
            j = N_SEG - 1 - jj if reverse else jj
            cre[j:j + 1, :] = cr
            cim[j:j + 1, :] = ci
            er_j, ei_j = ere[j:j + 1, :], eim[j:j + 1, :]
            cr, ci = pr * cr - pi * ci + er_j, pr * ci + pi * cr + ei_j
        c_r, c_i = cre[...], cim[...]

        def fix(k, carry):
            qr, qi = carry
            rows = rows_of(k)
            hre_ref[rows, :] = hre_ref[rows, :] + (qr * c_r - qi * c_i)
            him_ref[rows, :] = him_ref[rows, :] + (qr * c_i + qi * c_r)
            return qr * ar - qi * ai, qr * ai + qi * ar

        lax.fori_loop(0, seg_len, fix, (ar, ai))

        first_of_chunk = lax.rem(pl.program_id(0), per_chunk) == 0
        for r in range(s // rb):
            rows = slice(r * rb, (r + 1) * rb)
            h_cat = jnp.concatenate([hre_ref[rows, :], him_ref[rows, :]], axis=1)
            part = _dot(h_cat, k_mat, "nt" if reverse else "nn")

            @pl.when(first_of_chunk)
            def _(rows=rows, part=part):
                o_ref[rows, :] = part + d_ref[...] * ch_ref[rows, :]

            @pl.when(jnp.logical_not(first_of_chunk))
            def _(rows=rows, part=part):
                o_ref[rows, :] += part

    nblk = N_STATE // SCAN_LB
    blk = pl.BlockSpec((s, SCAN_LB), lambda b: (0, b))
    row = pl.BlockSpec((1, SCAN_LB), lambda b: (0, b))
    chan_blk = pl.BlockSpec((s, CH_W), lambda b: (0, b // per_chunk))
    res = _call(name, body, (nblk,),
                [row, row, chan_blk, pl.BlockSpec((None,) + expand12.shape[1:], lambda b: (b, 0, 0)),
                 pl.BlockSpec((None,) + contract12.shape[1:], lambda b: (b, 0, 0)),
                 pl.BlockSpec((1, CH_W), lambda b: (0, b // per_chunk))],
                [blk, blk, chan_blk],
                [jax.ShapeDtypeStruct((s, N_STATE), F32)] * 2 + [jax.ShapeDtypeStruct((s, SSM_WIDTH), F32)],
                (a_re, a_im, chan, expand12, contract12, d_row),
                scratch_shapes=[pltpu.VMEM((s, SCAN_LB), F32)] * 2 + [pltpu.VMEM((N_SEG, SCAN_LB), F32)] * 4, carry=carry)
    return res[0], res[1], res[2]


def _disc(ldt, are, aim, bre, bim):
    dt = jnp.exp(ldt)
    mag = jnp.exp(are * dt)
    abr = mag * jnp.cos(aim * dt)
    abi = mag * jnp.sin(aim * dt)
    den = jnp.square(are) + jnp.square(aim)
    nr = abr - 1.0
    fre = (nr * are + abi * aim) / den
    fim = (abi * are - nr * aim) / den
    return abr, abi, fre * bre - fim * bim, fre * bim + fim * bre


def _ssm_disc_fwd(ldt, are, aim, bre, bim):
    def body(l_ref, ar_ref, ai_ref, br_ref, bi_ref, o0, o1, o2, o3):
        res = _disc(l_ref[...], ar_ref[...], ai_ref[...], br_ref[...], bi_ref[...])
        for ref, val in zip((o0, o1, o2, o3), res):
            ref[...] = val

    col = jax.ShapeDtypeStruct((N_STATE, 1), F32)
    mat = jax.ShapeDtypeStruct((N_STATE, SSM_GROUP), F32)
    return pl.pallas_call(body, name="ssm_disc_fwd", out_shape=[col, col, mat, mat],
                          in_specs=[VMEM_SPEC] * 5, out_specs=[VMEM_SPEC] * 4)(ldt, are, aim, bre, bim)


def _ssm_disc_bwd(ldt, are, aim, bre, bim, d_abr, d_abi, d_bbr, d_bbi):
    def body(l_ref, ar_ref, ai_ref, br_ref, bi_ref, c0, c1, c2, c3, g_ldt, g_are, g_aim, g_bre, g_bim):
        _, vjp = jax.vjp(_disc, l_ref[...], ar_ref[...], ai_ref[...], br_ref[...], bi_ref[...])
        dl, dar, dai, dbr, dbi = vjp((c0[...], c1[...], c2[...], c3[...]))
        state = lax.broadcasted_iota(jnp.int32, (N_STATE, SSM_GROUPS), 0)
        group = lax.broadcasted_iota(jnp.int32, (N_STATE, SSM_GROUPS), 1)
        pick = jnp.right_shift(state, 6) == group
        g_ldt[...] = jnp.sum(jnp.where(pick, dl, 0.0), axis=0, keepdims=True)
        g_are[...] = dar
        g_aim[...] = dai
        g_bre[...] = dbr
        g_bim[...] = dbi

    col = jax.ShapeDtypeStruct((N_STATE, 1), F32)
    mat = jax.ShapeDtypeStruct((N_STATE, SSM_GROUP), F32)
    return pl.pallas_call(body, name="ssm_disc_bwd",
                          out_shape=[jax.ShapeDtypeStruct((1, SSM_GROUPS), F32), col, col, mat, mat],
                          in_specs=[VMEM_SPEC] * 9, out_specs=[VMEM_SPEC] * 5,
                          compiler_params=pltpu.CompilerParams(vmem_limit_bytes=VMEM_LIMIT))(
        ldt, are, aim, bre, bim, d_abr, d_abi, d_bbr, d_bbi)


_EYE8 = np.eye(8, dtype=np.float32)


def _blockdiag_b(bb):
    t = bb.reshape(SSM_CHUNKS, 8, SSM_STATE, SSM_GROUP).transpose(0, 1, 3, 2)
    return jnp.einsum("igcn,gh->igchn", t, _EYE8).reshape(SSM_CHUNKS, CH_W, CH_N)


def _diag_of_b(m):
    t = jnp.einsum("igchn,gh->igcn", m.reshape(SSM_CHUNKS, 8, SSM_GROUP, 8, SSM_STATE), _EYE8)
    return t.transpose(0, 1, 3, 2).reshape(N_STATE, SSM_GROUP)


def _blockdiag_c(c):
    t = c.reshape(SSM_CHUNKS, 8, SSM_GROUP, SSM_STATE).transpose(0, 1, 3, 2)
    return jnp.einsum("ignc,gh->ignhc", t, _EYE8).reshape(SSM_CHUNKS, CH_N, CH_W)


def _diag_of_c(m):
    t = jnp.einsum("ignhc,gh->ignc", m.reshape(SSM_CHUNKS, 8, SSM_STATE, 8, SSM_GROUP), _EYE8)
    return t.transpose(0, 1, 3, 2).reshape(SSM_GROUPS, SSM_GROUP, SSM_STATE)


def _time_perm(a):
    s, c = a.shape
    return a.reshape(N_SEG, s // N_SEG, c).transpose(1, 0, 2).reshape(s, c)


def _time_unperm(a):
    s, c = a.shape
    return a.reshape(s // N_SEG, N_SEG, c).transpose(1, 0, 2).reshape(s, c)


def _dilate(a, d):
    s, c = a.shape
    return a if d == 1 else a.reshape(s // d, d, c).transpose(1, 0, 2).reshape(s, c)


def _undilate(a, d):
    s, c = a.shape
    return a if d == 1 else a.reshape(d, s // d, c).transpose(1, 0, 2).reshape(s, c)


def _dilate_rows(a, d):
    r, s = a.shape
    return a if d == 1 else a.reshape(r, s // d, d).transpose(0, 2, 1).reshape(r, s)


ATT_T_FWD = 4
ATT_T_BWD = 8


def _window(prev_ref, cur_ref, i, sl):
    if i == 0:
        return jnp.concatenate([prev_ref[:, sl], cur_ref[0:ATT_BLK, sl]], axis=0)
    return cur_ref[(i - 1) * ATT_BLK:(i + 1) * ATT_BLK, sl]


def _band_valid(first_key):
    qi = lax.broadcasted_iota(jnp.int32, (ATT_BLK, 2 * ATT_BLK), 0)
    ki = lax.broadcasted_iota(jnp.int32, (ATT_BLK, 2 * ATT_BLK), 1)
    steps = qi + ATT_BLK - ki
    return (steps >= 0) & (steps <= ATT_BLK) & (ki >= first_key)


ATT_STATW = ATT_HPG * 128


def _stat(h):
    return slice(h * 128, (h + 1) * 128)


def _stat_rows(stat):
    n = stat.shape[0]
    heads = [stat[:, _stat(h)].T[0:1, :] for h in range(ATT_HPG)]
    return jnp.concatenate(heads + [jnp.zeros((8 - ATT_HPG, n), stat.dtype)], axis=0)


def _attn_specs(nb, t, width=ATT_GROUPW):
    cur = pl.BlockSpec((t * ATT_BLK, width), lambda b: (b, 0))
    prev = pl.BlockSpec((ATT_BLK, width), lambda b: (jnp.maximum(b * t - 1, 0), 0))
    nxt = pl.BlockSpec((ATT_BLK, width), lambda b: (jnp.minimum((b + 1) * t, nb - 1), 0))
    return cur, prev, nxt


def _attn_fwd(tag, per_seq, q, k, v):
    s = q.shape[0]
    nb = s // ATT_BLK

    def body(q_ref, kc_ref, kp_ref, vc_ref, vp_ref, o_ref, lse_ref):
        bt = pl.program_id(0)
        for i in range(ATT_T_FWD):
            has_prev = lax.rem(bt * ATT_T_FWD + i, per_seq) > 0
            valid = _band_valid(jnp.where(has_prev, 0, ATT_BLK))
            rows = slice(i * ATT_BLK, (i + 1) * ATT_BLK)
            for h in range(ATT_HPG):
                sl = slice(h * ATT_HEAD_DIM, (h + 1) * ATT_HEAD_DIM)
                kcat = _window(kp_ref, kc_ref, i, sl)
                vcat = _window(vp_ref, vc_ref, i, sl)
                sc = _dot(q_ref[rows, sl], kcat, "nt") * ATT_SCALE
                sc = jnp.where(valid, sc, NEG_INF)
                m = jnp.max(sc, axis=-1, keepdims=True)
                p = jnp.exp(sc - m)
                den = jnp.sum(p, axis=-1, keepdims=True)
                o_ref[rows, sl] = _dot(p, vcat, "nn") / den
                lse_ref[rows, _stat(h)] = jnp.broadcast_to(m + jnp.log(den), (ATT_BLK, 128))

    cur, prev, _ = _attn_specs(nb, ATT_T_FWD)
    stat, _, _ = _attn_specs(nb, ATT_T_FWD, ATT_STATW)
    return pl.pallas_call(
        body, name="attn_fwd_" + tag, grid=(nb // ATT_T_FWD,), in_specs=[cur, cur, prev, cur, prev], out_specs=[cur, stat],
        out_shape=[jax.ShapeDtypeStruct((s, ATT_GROUPW), F32), jax.ShapeDtypeStruct((s, ATT_STATW), F32)],
        compiler_params=_cparams(1))(q, k, k, v, v)


def _attn_dq(tag, per_seq, q, k, v, do, lse, delta):
    s = q.shape[0]
    nb = s // ATT_BLK

    def body(q_ref, kc_ref, kp_ref, vc_ref, vp_ref, do_ref, lse_ref, dl_ref, dq_ref):
        bt = pl.program_id(0)
        for i in range(ATT_T_BWD):
            has_prev = lax.rem(bt * ATT_T_BWD + i, per_seq) > 0
            valid = _band_valid(jnp.where(has_prev, 0, ATT_BLK))
            rows = slice(i * ATT_BLK, (i + 1) * ATT_BLK)
            for h in range(ATT_HPG):
                sl = slice(h * ATT_HEAD_DIM, (h + 1) * ATT_HEAD_DIM)
                kcat = _window(kp_ref, kc_ref, i, sl)
                vcat = _window(vp_ref, vc_ref, i, sl)
                lse = jnp.concatenate([lse_ref[rows, _stat(h)]] * 2, axis=1)
                dlt = jnp.concatenate([dl_ref[rows, _stat(h)]] * 2, axis=1)
                sc = _dot(q_ref[rows, sl], kcat, "nt") * ATT_SCALE
                p = jnp.exp(jnp.where(valid, sc, NEG_INF) - lse)
                dp = _dot(do_ref[rows, sl], vcat, "nt")
                ds = p * (dp - dlt) * ATT_SCALE
                dq_ref[rows, sl] = _dot(ds, kcat, "nn")

    cur, prev, _ = _attn_specs(nb, ATT_T_BWD)
    stat, _, _ = _attn_specs(nb, ATT_T_BWD, ATT_STATW)
    return pl.pallas_call(
        body, name="attn_dq_" + tag, grid=(nb // ATT_T_BWD,), in_specs=[cur, cur, prev, cur, prev, cur, stat, stat],
        out_specs=cur, out_shape=jax.ShapeDtypeStruct((s, ATT_GROUPW), F32),
        compiler_params=_cparams(1))(q, k, k, v, v, do, lse, delta)


def _attn_dkv(tag, per_seq, q, k, v, do, lse_t, delta_t):
    s = q.shape[0]
    nb = s // ATT_BLK

    def body(k_ref, v_ref, qc_ref, qn_ref, doc_ref, don_ref, lc_ref, ln_ref, dc_ref, dn_ref, dk_ref, dv_ref):
        bt = pl.program_id(0)
        ki = lax.broadcasted_iota(jnp.int32, (ATT_BLK, 2 * ATT_BLK), 0)
        ci = lax.broadcasted_iota(jnp.int32, (ATT_BLK, 2 * ATT_BLK), 1)

        def pair(edge_ref, cur_ref, i, sl):
            if i == ATT_T_BWD - 1:
                return jnp.concatenate([cur_ref[i * ATT_BLK:(i + 1) * ATT_BLK, sl], edge_ref[:, sl]], axis=0)
            return cur_ref[i * ATT_BLK:(i + 2) * ATT_BLK, sl]

        def pair_row(edge_ref, cur_ref, i, h):
            if i == ATT_T_BWD - 1:
                row = jnp.concatenate([cur_ref[h:h + 1, i * ATT_BLK:(i + 1) * ATT_BLK], edge_ref[h:h + 1, :]], axis=1)
            else:
                row = cur_ref[h:h + 1, i * ATT_BLK:(i + 2) * ATT_BLK]
            return jnp.broadcast_to(row, (ATT_BLK, 2 * ATT_BLK))

        for i in range(ATT_T_BWD):
            b = bt * ATT_T_BWD + i
            next_uses = (b + 1 < nb) & (lax.rem(b + 1, per_seq) > 0)
            reach = jnp.where(next_uses, 0, 4 * ATT_BLK)
            valid = ((ci < ATT_BLK) & (ci >= ki)) | ((ci >= ATT_BLK) & (ki - ci + ATT_BLK >= reach))
            rows = slice(i * ATT_BLK, (i + 1) * ATT_BLK)
            for h in range(ATT_HPG):
                sl = slice(h * ATT_HEAD_DIM, (h + 1) * ATT_HEAD_DIM)
                qcat, docat = pair(qn_ref, qc_ref, i, sl), pair(don_ref, doc_ref, i, sl)
                sc = _dot(k_ref[rows, sl], qcat, "nt") * ATT_SCALE
                p = jnp.exp(jnp.where(valid, sc, NEG_INF) - pair_row(ln_ref, lc_ref, i, h))
                dv_ref[rows, sl] = _dot(p, docat, "nn")
                dp = _dot(v_ref[rows, sl], docat, "nt")
                ds = p * (dp - pair_row(dn_ref, dc_ref, i, h)) * ATT_SCALE
                dk_ref[rows, sl] = _dot(ds, qcat, "nn")

    cur, _, nxt = _attn_specs(nb, ATT_T_BWD)
    stat = pl.BlockSpec((8, ATT_T_BWD * ATT_BLK), lambda b: (0, b))
    snxt = pl.BlockSpec((8, ATT_BLK), lambda b: (0, jnp.minimum((b + 1) * ATT_T_BWD, nb - 1)))
    return pl.pallas_call(
        body, name="attn_dkv_" + tag, grid=(nb // ATT_T_BWD,), in_specs=[cur, cur, cur, nxt, cur, nxt, stat, snxt, stat, snxt],
        out_specs=[cur, cur], out_shape=[jax.ShapeDtypeStruct((s, ATT_GROUPW), F32)] * 2,
        compiler_params=_cparams(1))(k, v, q, q, do, do, lse_t, lse_t, delta_t, delta_t)


def _xattn_probs(q, kh):
    sc = _dot(q, kh, "nt") * XATT_SCALE
    e = jnp.exp(sc - jnp.max(sc, axis=-1, keepdims=True))
    return e / jnp.sum(e, axis=-1, keepdims=True)


def _xattn_fwd(q, kv, tm=512):
    s = q.shape[0]
    tm = min(tm, s)

    def body(q_ref, kv_ref, o_ref):
        for h in range(XATT_HEADS):
            sl = slice(h * XATT_HEAD_DIM, (h + 1) * XATT_HEAD_DIM)
            vs = slice(D_MODEL + h * XATT_HEAD_DIM, D_MODEL + (h + 1) * XATT_HEAD_DIM)
            p = _xattn_probs(q_ref[:, sl], kv_ref[:, sl])
            o_ref[:, sl] = _dot(p, kv_ref[:, vs], "nn").astype(o_ref.dtype)

    return pl.pallas_call(
        body, name="xattn_fwd", grid=(s // tm,),
        in_specs=[pl.BlockSpec((tm, D_MODEL), lambda i: (i, 0)), pl.BlockSpec(kv.shape, lambda i: (0, 0))],
        out_specs=pl.BlockSpec((tm, D_MODEL), lambda i: (i, 0)),
        out_shape=jax.ShapeDtypeStruct((s, D_MODEL), MXU_DTYPE), compiler_params=_cparams(1))(q, kv)


def _xattn_bwd(q, kv, do, tm=512):
    s = q.shape[0]
    tm = min(tm, s)

    def body(q_ref, kv_ref, do_ref, dq_ref, dkv_ref):
        first = pl.program_id(0) == 0

        @pl.when(first)
        def _():
            dkv_ref[...] = jnp.zeros_like(dkv_ref)

        for h in range(XATT_HEADS):
            sl = slice(h * XATT_HEAD_DIM, (h + 1) * XATT_HEAD_DIM)
            vs = slice(D_MODEL + h * XATT_HEAD_DIM, D_MODEL + (h + 1) * XATT_HEAD_DIM)
            p = _xattn_probs(q_ref[:, sl], kv_ref[:, sl])
            dkv_ref[:, vs] += _dot(p, do_ref[:, sl], "tn")
            dp = _dot(do_ref[:, sl], kv_ref[:, vs], "nt")
            ds = p * (dp - jnp.sum(dp * p, axis=-1, keepdims=True)) * XATT_SCALE
            dq_ref[:, sl] = _dot(ds, kv_ref[:, sl], "nn").astype(dq_ref.dtype)
            dkv_ref[:, sl] += _dot(ds, q_ref[:, sl], "tn")

    row = pl.BlockSpec((tm, D_MODEL), lambda i: (i, 0))
    whole = pl.BlockSpec(kv.shape, lambda i: (0, 0))
    return pl.pallas_call(
        body, name="xattn_bwd", grid=(s // tm,), in_specs=[row, whole, row], out_specs=[row, whole],
        out_shape=[jax.ShapeDtypeStruct((s, D_MODEL), MXU_DTYPE), jax.ShapeDtypeStruct(kv.shape, F32)],
        compiler_params=_cparams(1))(q, kv, do)


def _ln(x, g, b):
    mu = jnp.mean(x, axis=-1, keepdims=True)
    xc = x - mu
    var = jnp.mean(jnp.square(xc), axis=-1, keepdims=True)
    return xc * lax.rsqrt(var + LN_EPS) * g + b


def _res_ln(h, o, g, b):
    return _ln(DEEPNORM_ALPHA * h + o, g, b)


def _gate(gs, ga, z1, z2, batt):
    return jax.nn.sigmoid(gs) * (z1 * jax.nn.sigmoid(z2)) + jax.nn.sigmoid(ga) * batt


ROPE_TW = 2 * ATT_HEAD_DIM


def _rope_tables(pos, invf, m1, m2):
    ang = pos.astype(F32) * invf
    sin = jnp.sin(ang)
    return jnp.cos(ang), -sin * m1, sin * m2


def _widen(tab):
    return jnp.concatenate([tab] * (ATT_GROUPW // ROPE_TW), axis=1)


def _rope(t, cos, s_up, s_dn):
    w = t.shape[-1]
    return t * cos + pltpu.roll(t, w - ROT_DIM // 2, 1) * s_up + pltpu.roll(t, ROT_DIM // 2, 1) * s_dn


def _rope_t(dt, cos, s_up, s_dn):
    w = dt.shape[-1]
    return dt * cos + pltpu.roll(dt * s_up, ROT_DIM // 2, 1) + pltpu.roll(dt * s_dn, w - ROT_DIM // 2, 1)


def _rope_consts():
    inv_freq = ROPE_THETA ** (-jnp.arange(0, ROT_DIM, 2, dtype=F32) / ROT_DIM)
    d = np.arange(ROPE_TW) % ATT_HEAD_DIM
    invf = jnp.where(d < ROT_DIM, inv_freq[d % (ROT_DIM // 2)], 0.0).reshape(1, ROPE_TW).astype(F32)
    m1 = jnp.asarray((d < ROT_DIM // 2).astype(np.float32)).reshape(1, ROPE_TW)
    m2 = jnp.asarray(((d >= ROT_DIM // 2) & (d < ROT_DIM)).astype(np.float32)).reshape(1, ROPE_TW)
    return invf, m1, m2


def _head_sum_matrix():
    d = np.arange(ATT_GROUPW) // ATT_HEAD_DIM
    s = np.arange(ATT_STATW) // 128
    return jnp.asarray((d[:, None] == s[None, :]).astype(np.float32))


def _adamw(w, g, m, v):
    m = ADAM_B1 * m + (1.0 - ADAM_B1) * g
    v = ADAM_B2 * v + (1.0 - ADAM_B2) * jnp.square(g)
    m_hat = m / (1.0 - ADAM_B1 ** ADAM_STEP)
    v_hat = v / (1.0 - ADAM_B2 ** ADAM_STEP)
    delta = -ADAM_LR * (m_hat / (jnp.sqrt(v_hat) + ADAM_EPS) + ADAM_WD * w)
    return delta, m, v


def _local_step(x, mem, pos, target, sp, ex):
    s = x.shape[0]
    al = DEEPNORM_ALPHA
    mx = MXU_DTYPE

    h0, h0b = _rowwise("ln_in", lambda x, g, b: (lambda h: (h, h))(_ln(x, g, b)), [x],
                       [sp["ln_in_g"], sp["ln_in_b"]], [(D_MODEL, F32), (D_MODEL, mx)],
                       carry=ex.gather_carry(["w_in"]))
    proj = _mm("proj", h0b, ex.weight("w_in"), "nn", bias=sp["b_in"],
               carry=ex.gather_carry(["w_glu", "w_att_up", "w_mix_out", "w_xq"]))

    ldt = jnp.repeat(sp["ssm_log_dt"].reshape(SSM_GROUPS), SSM_STATE).reshape(N_STATE, 1)
    are, aim = sp["ssm_a_re"].reshape(N_STATE, 1), sp["ssm_a_im"].reshape(N_STATE, 1)
    bre, bim = sp["ssm_b_re"].reshape(N_STATE, SSM_GROUP), sp["ssm_b_im"].reshape(N_STATE, SSM_GROUP)
    abr, abi, bbr, bbi = _ssm_disc_fwd(ldt, are, aim, bre, bim)
    a_re, a_im = abr.reshape(1, N_STATE), abi.reshape(1, N_STATE)
    bexp = jnp.concatenate([_blockdiag_b(bbr), _blockdiag_b(bbi)], axis=2).astype(mx)
    cexp = jnp.concatenate([_blockdiag_c(sp["ssm_c_re"].reshape(SSM_GROUPS, SSM_GROUP, SSM_STATE)),
                            -_blockdiag_c(sp["ssm_c_im"].reshape(SSM_GROUPS, SSM_GROUP, SSM_STATE))],
                           axis=1).astype(mx)
    u_p = _time_perm(proj[:, :SSM_WIDTH])
    b12, c12 = _split_by_scan_block(bexp, 2), _split_by_scan_block(cexp, 1)
    h_re, h_im, y_p = _ssm_scan("ssm_scan_fwd", u_p, b12, c12, a_re, a_im, sp["ssm_d"], reverse=False,
                                carry=ex.gather_carry(["w_xkv", "w_ff1", "w_ff2"]))
    y = _time_unperm(y_p)
    ygb, = _rowwise("gelu", lambda y: jax.nn.gelu(y), [y], [], [(SSM_WIDTH, mx)])
    z = _mm("glu", ygb, ex.weight("w_glu"), "nn", bias=sp["b_glu"], carry=ex.gather_carry(["w_xo"]))

    invf, m1, m2 = _rope_consts()

    def rope_fwd(pos, q0, q1, q2, k0, k1, k2, v0, v1, v2, invf, m1, m2):
        narrow = _rope_tables(pos, invf, m1, m2)
        tabs = [_widen(t) for t in narrow]
        return tuple(_rope(t, *tabs) for t in (q0, q1, q2, k0, k1, k2)) + (v0, v1, v2) + tuple(narrow)

    qkv_cols = [(proj, ATT_GROUPW, 3 + i) for i in range(9)]
    qkv = _rowwise("rope", rope_fwd, [pos] + qkv_cols, [invf, m1, m2], [(ATT_GROUPW, mx)] * 9 + [(ROPE_TW, F32)] * 3)
    rope_tabs = qkv[9:]
    n_blocks = s // ATT_BLK
    groups = [(str(g), n_blocks // d, d) for g, d in enumerate(DILATIONS)]
    q_d = [_dilate(qkv[g], d) for g, d in enumerate(DILATIONS)]
    k_d = [_dilate(qkv[3 + g], d) for g, d in enumerate(DILATIONS)]
    v_d = [_dilate(qkv[6 + g], d) for g, d in enumerate(DILATIONS)]
    o_g, l_g = [], []
    for g, (tag, per_seq, d) in enumerate(groups):
        o, lse = _attn_fwd(tag, per_seq, q_d[g], k_d[g], v_d[g])
        o_g.append(_undilate(o, d))
        l_g.append(_undilate(lse, d))

    def merge(o0, o1, o2, l0, l1, l2):
        m = jnp.maximum(jnp.maximum(l0, l1), l2)
        e0, e1, e2 = jnp.exp(l0 - m), jnp.exp(l1 - m), jnp.exp(l2 - m)
        tot = e0 + e1 + e2

        def per_dim(e):
            w = e / tot
            return jnp.concatenate([w[:, h * 128:h * 128 + ATT_HEAD_DIM] for h in range(ATT_HPG)], axis=1)

        att = per_dim(e0) * o0 + per_dim(e1) * o1 + per_dim(e2) * o2
        lse = m + jnp.log(tot)
        return att, att, lse, _stat_rows(lse)

    att, attb, lse_tot, lse_tot_t = _rowwise("attn_merge", merge, o_g + l_g, [],
                                             [(ATT_GROUPW, F32), (ATT_GROUPW, mx), (ATT_STATW, F32)], touts=[(8, F32)])
    batt = _mm("att_up", attb, ex.weight("w_att_up"), "nn")

    gate_rows = [(proj, D_MODEL, 3), (proj, D_MODEL, 4), (z, D_MODEL, 0), (z, D_MODEL, 1), batt]
    mixedb, = _rowwise("gate", _gate, gate_rows, [], [(D_MODEL, mx)])
    o1 = _mm("mix_out", mixedb, ex.weight("w_mix_out"), "nn", bias=sp["b_mix_out"])
    h1, h1b = _rowwise("ln1", lambda h, o, g, b: (lambda r: (r, r))(_res_ln(h, o, g, b)), [h0, o1],
                       [sp["ln1_g"], sp["ln1_b"]], [(D_MODEL, F32), (D_MODEL, mx)])

    qx = _mm("xq", h1b, ex.weight("w_xq"), "nn", out_dtypes=(mx,))
    kvx = _mm("xkv", mem, ex.weight("w_xkv"), "nn", out_dtypes=(mx,))
    oxb = _xattn_fwd(qx, kvx)
    o2 = _mm("xo", oxb, ex.weight("w_xo"), "nn")
    h2, h2b = _rowwise("ln2", lambda h, o, g, b: (lambda r: (r, r))(_res_ln(h, o, g, b)), [h1, o2],
                       [sp["ln2_g"], sp["ln2_b"]], [(D_MODEL, F32), (D_MODEL, mx)])

    a_ff, fb = _mm("ff1", h2b, ex.weight("w_ff1"), "nn", bias=sp["b_ff1"],
                   epilogue=lambda r: (r, jnp.square(jnp.maximum(r, 0.0))), out_dtypes=(F32, mx))
    o3 = _mm("ff2", fb, ex.weight("w_ff2"), "nn", bias=sp["b_ff2"])

    def loss_bwd(h2, o3, tgt, g, b):
        def f(h2, o3, g, b):
            h3 = _res_ln(h2, o3, g, b)
            return 0.5 * jnp.sum(jnp.mean(jnp.square(h3 - tgt), axis=-1))

        loss, vjp = jax.vjp(f, h2, o3, g, b)
        _, dr, dg, db = vjp(jnp.ones((), F32))
        return dr, dr, dg, db, _colsum(dr), jnp.full((1, 128), loss, F32)

    dr3, dr3b, g_ln3_g, g_ln3_b, g_b_ff2, loss = _rowwise(
        "loss_ln3_bwd", loss_bwd, [h2, o3, target], [sp["ln3_g"], sp["ln3_b"]],
        [(D_MODEL, F32), (D_MODEL, mx)], [D_MODEL, D_MODEL, D_MODEL, 128])

    dab, da_sums = _mm("ff2_dx", dr3b, ex.weight("w_ff2"), "nt", extras=(a_ff,),
                       epilogue=lambda r, a: (r * (2.0 * jnp.maximum(a, 0.0)),), out_dtypes=(mx,), colsum=True)
    g_b_ff1 = jnp.sum(da_sums, axis=0)
    ex.grad("w_ff2", _mm("ff2_dw", fb, dr3b, "tn"))
    ex.grad("w_ff1", _mm("ff1_dw", h2b, dab, "tn", carry=ex.carry(swap=["w_ff2"])))
    dh2 = _mm("ff1_dx", dab, ex.weight("w_ff1"), "nt", extras=(dr3,), epilogue=lambda r, d: (r + al * d,),
              carry=ex.carry(swap=["w_ff1"], ici=["w_ff2"]))

    def ln_bwd(h, o, dout, g, b):
        _, vjp = jax.vjp(_res_ln, h, o, g, b)
        _, dr, dg, db = vjp(dout)
        return dr, dr, dg, db, _colsum(dr)

    dr2, dr2b, g_ln2_g, g_ln2_b, _ = _rowwise(
        "ln2_bwd", ln_bwd, [h1, o2, dh2], [sp["ln2_g"], sp["ln2_b"]],
        [(D_MODEL, F32), (D_MODEL, mx)], [D_MODEL, D_MODEL, D_MODEL])
    ex.grad("w_xo", _mm("xo_dw", oxb, dr2b, "tn"))
    doxb = _mm("xo_dx", dr2b, ex.weight("w_xo"), "nt", out_dtypes=(mx,), carry=ex.carry(swap=["w_xo"]))
    dqxb, dkvx = _xattn_bwd(qx, kvx, doxb)
    ex.grad("w_xq", _mm("xq_dw", h1b, dqxb, "tn"))
    dh1 = _mm("xq_dx", dqxb, ex.weight("w_xq"), "nt", extras=(dr2,), epilogue=lambda r, d: (r + al * d,),
              carry=ex.carry(swap=["w_xq"]))
    ex.grad("w_xkv", _mm("xkv_dw", mem, dkvx, "tn"))

    dr1, dr1b, g_ln1_g, g_ln1_b, g_b_mix = _rowwise(
        "ln1_bwd", ln_bwd, [h0, o1, dh1], [sp["ln1_g"], sp["ln1_b"]],
        [(D_MODEL, F32), (D_MODEL, mx)], [D_MODEL, D_MODEL, D_MODEL])
    ex.grad("w_mix_out", _mm("mix_dw", mixedb, dr1b, "tn", carry=ex.carry(swap=["w_xkv"])))
    dmixed = _mm("mix_dx", dr1b, ex.weight("w_mix_out"), "nt", carry=ex.carry(swap=["w_mix_out"]))

    def gate_bwd(gs, ga, z1, z2, batt, dm):
        _, vjp = jax.vjp(_gate, gs, ga, z1, z2, batt)
        dgs, dga, dz1, dz2, dbatt = vjp(dm)
        dz = jnp.concatenate([dz1, dz2], axis=-1)
        return dgs, dga, dz, dbatt, _colsum(dz)

    dgsb, dgab, dzb, dbattb, g_b_glu = _rowwise(
        "gate_bwd", gate_bwd, gate_rows + [dmixed], [],
        [(D_MODEL, mx), (D_MODEL, mx), (2 * D_MODEL, mx), (D_MODEL, mx)], [2 * D_MODEL])
    ex.grad("w_att_up", _mm("att_up_dw", attb, dbattb, "tn"))
    datt = _mm("att_up_dx", dbattb, ex.weight("w_att_up"), "nt", carry=ex.carry(swap=["w_att_up"]))

    def att_delta(datt, att, hs):
        dl = jnp.dot(datt * att, hs, precision=lax.Precision.HIGHEST, preferred_element_type=F32)
        return datt, dl, _stat_rows(dl)

    dattb, delta, delta_t = _rowwise("attn_delta", att_delta, [datt, att], [_head_sum_matrix()],
                                     [(ATT_GROUPW, mx), (ATT_STATW, F32)], touts=[(8, F32)])
    dq_g, dk_g, dv_g = [], [], []
    for g, (tag, per_seq, d) in enumerate(groups):
        do_d, lt_d, dl_d = _dilate(dattb, d), _dilate(lse_tot, d), _dilate(delta, d)
        dq_g.append(_undilate(_attn_dq(tag, per_seq, q_d[g], k_d[g], v_d[g], do_d, lt_d, dl_d), d))
        dk, dv = _attn_dkv(tag, per_seq, q_d[g], k_d[g], v_d[g], do_d, _dilate_rows(lse_tot_t, d), _dilate_rows(delta_t, d))
        dk_g.append(_undilate(dk, d))
        dv_g.append(_undilate(dv, d))
    dqkv = dq_g + dk_g + dv_g

    def rope_bwd(q0, q1, q2, k0, k1, k2, v0, v1, v2, cos, s_up, s_dn):
        tabs = [_widen(t) for t in (cos, s_up, s_dn)]
        return jnp.concatenate([_rope_t(t, *tabs) for t in (q0, q1, q2, k0, k1, k2)] + [v0, v1, v2], axis=-1)

    dqkvb, = _rowwise("rope_bwd", rope_bwd, dqkv + list(rope_tabs), [], [(9 * ATT_GROUPW, mx)])

    ex.grad("w_glu", _mm("glu_dw", ygb, dzb, "tn"))
    dyg = _mm("glu_dx", dzb, ex.weight("w_glu"), "nt", carry=ex.carry(swap=["w_glu"]))

    def gelu_bwd(y, dyg):
        _, vjp = jax.vjp(jax.nn.gelu, y)
        return vjp(dyg)[0]

    dy, = _rowwise("gelu_bwd", gelu_bwd, [y, dyg], [], [(SSM_WIDTH, F32)])
    dy_p = _time_perm(dy)
    s_re, s_im, du_p = _ssm_scan("ssm_scan_bwd", dy_p, c12, b12, a_re, a_im, sp["ssm_d"], reverse=True,
                                 carry=ex.carry(ici=["w_ff1", "w_xkv", "w_glu"]))
    g_bexp, g_cexp, d_abr, d_abi = _ssm_wgrads(u_p, dy_p, s_re, s_im, h_re, h_im)
    g_ssm_d, = _rowwise("ssm_dd", lambda a, b: (_colsum(a * b),), [dy_p, u_p], [], [], [SSM_WIDTH])
    g_ldt, g_are, g_aim, g_bre, g_bim = _ssm_disc_bwd(
        ldt, are, aim, bre, bim, d_abr.reshape(N_STATE, 1), d_abi.reshape(N_STATE, 1),
        _diag_of_b(g_bexp[:, :, :CH_N]), _diag_of_b(g_bexp[:, :, CH_N:]))
    g_c_re = _diag_of_c(g_cexp[:, :CH_N, :])
    g_c_im = -_diag_of_c(g_cexp[:, CH_N:, :])

    def assemble(du, dqkv, dgs, dga):
        row = jnp.concatenate([du.astype(mx), dqkv, dgs, dga], axis=-1)
        return row, _colsum(row)

    dprojb, g_b_in = _rowwise("in_assemble", assemble, [_time_unperm(du_p), dqkvb, dgsb, dgab], [],
                              [(IN_COLS, mx)], [IN_COLS])
    ex.grad("w_in", _mm("in_dw", h0b, dprojb, "tn",
                        carry=ex.carry(ici=["w_xo", "w_xq", "w_mix_out", "w_att_up"])))
    dh0 = _mm("in_dx", dprojb, ex.weight("w_in"), "nt", extras=(dr1,), epilogue=lambda r, d: (r + al * d,),
              carry=ex.carry(ici=["w_in"]))

    def ln_in_bwd(x, dout, g, b):
        _, vjp = jax.vjp(_ln, x, g, b)
        return vjp(dout)

    dx, g_ln_in_g, g_ln_in_b = _rowwise("ln_in_bwd", ln_in_bwd, [x, dh0], [sp["ln_in_g"], sp["ln_in_b"]],
                                        [(D_MODEL, F32)], [D_MODEL, D_MODEL], carry=ex.finish_carry())

    small = {"ln_in_g": g_ln_in_g, "ln_in_b": g_ln_in_b, "b_in": g_b_in, "ssm_log_dt": g_ldt, "ssm_a_re": g_are,
             "ssm_a_im": g_aim, "ssm_b_re": g_bre, "ssm_b_im": g_bim, "ssm_c_re": g_c_re, "ssm_c_im": g_c_im,
             "ssm_d": g_ssm_d, "b_glu": g_b_glu, "b_mix_out": g_b_mix, "ln1_g": g_ln1_g, "ln1_b": g_ln1_b,
             "ln2_g": g_ln2_g, "ln2_b": g_ln2_b, "b_ff1": g_b_ff1, "b_ff2": g_b_ff2, "ln3_g": g_ln3_g,
             "ln3_b": g_ln3_b}
    return loss, dx, small


def _piece_shape(k, n, axis):
    return (k // 2, n // 4) if axis == 1 else (k // 8, n)


def _aligned(v, m):
    return v if isinstance(v, int) else pl.multiple_of(v, m)


def _full_piece(ref, k, n, axis, chip, half):
    pr, pc = _piece_shape(k, n, axis)
    if axis == 1:
        return ref.at[pl.ds(_aligned(half * pr, 8), pr), pl.ds(_aligned(chip * pc, 128), pc)]
    return ref.at[pl.ds(_aligned(chip * (2 * pr) + half * pr, 8), pr), :]


def _shard_piece(ref, k, n, axis, half):
    pr, _ = _piece_shape(k, n, axis)
    return ref.at[pl.ds(_aligned(half * pr, 8), pr), :]


def _mesh_pos():
    x, y, c = lax.axis_index("x"), lax.axis_index("y"), lax.axis_index("c")
    other_chips = [(1 - x, y), (x, 1 - y), (1 - x, 1 - y)]
    return x, y, c, other_chips


def _remote(src, dst, send_sem, recv_sem, dev):
    return pltpu.make_async_remote_copy(src_ref=src, dst_ref=dst, send_sem=send_sem, recv_sem=recv_sem,
                                        device_id=dev, device_id_type=MESH)


def _placed(name, fn, n_steps, where, ins, out_sds, out_block, out_index):
    def body(w_ref, *refs):
        o_ref = refs[-1]
        o_ref[...] = fn(*[r[...] for r in refs[:-1]]).astype(o_ref.dtype)

    grid_spec = pltpu.PrefetchScalarGridSpec(
        num_scalar_prefetch=1, grid=(n_steps,), in_specs=[pl.BlockSpec(bs, idx) for _, bs, idx in ins],
        out_specs=pl.BlockSpec(out_block, out_index))
    return pl.pallas_call(body, name=name, grid_spec=grid_spec, out_shape=out_sds,
                          compiler_params=_cparams(1))(where, *[a for a, _, _ in ins])


def _gather_copies(widx):
    geo = [BIG[i][1:] for i in widx]

    def ici(full, wi, j, chip, send_sems, recv_sems, c, dev):
        k, n, ax = geo[wi]
        piece = _full_piece(full[wi], k, n, ax, chip, c)
        return _remote(piece, piece, send_sems.at[wi * 6 + j], recv_sems.at[wi * 6 + j], dev)

    def d2d(full, wi, j, chip, half, send_sems, recv_sems, sib):
        k, n, ax = geo[wi]
        piece = _full_piece(full[wi], k, n, ax, chip, half)
        return _remote(piece, piece, send_sems.at[wi * 6 + 3 + j], recv_sems.at[wi * 6 + 3 + j], sib)

    def start(_, full, send_sems, recv_sems):
        x, y, c, chips = _mesh_pos()
        for wi in range(len(geo)):
            for j, (qx, qy) in enumerate(chips):
                ici(full, wi, j, 2 * x + y, send_sems, recv_sems, c, (qx, qy, c)).start()

    def finish(_, full, send_sems, recv_sems):
        x, y, c, chips = _mesh_pos()
        sib = (x, y, 1 - c)
        for wi in range(len(geo)):
            for j, (qx, qy) in enumerate(chips):
                ici(full, wi, j, 2 * qx + qy, send_sems, recv_sems, c, (qx, qy, c)).wait_recv()
                d2d(full, wi, j, 2 * qx + qy, c, send_sems, recv_sems, sib).start()
        for wi in range(len(geo)):
            for j, (qx, qy) in enumerate(chips):
                d2d(full, wi, j, 2 * qx + qy, 1 - c, send_sems, recv_sems, sib).wait_recv()
        for wi in range(len(geo)):
            for j, (qx, qy) in enumerate(chips):
                ici(full, wi, j, 2 * x + y, send_sems, recv_sems, c, (qx, qy, c)).wait_send()
                d2d(full, wi, j, 2 * qx + qy, c, send_sems, recv_sems, sib).wait_send()

    return start, finish, 6 * len(geo)


def _swap_copies(widx):
    geo = [BIG[i][1:] for i in widx]

    def copies(g, got, send_sems, recv_sems, base):
        x, y, c, _ = _mesh_pos()
        return [_remote(_full_piece(g[wi], k, n, ax, q, 1 - c), got[wi].at[q], send_sems.at[base + wi * 4 + q],
                        recv_sems.at[base + wi * 4 + q], (x, y, 1 - c))
                for wi, (k, n, ax) in enumerate(geo) for q in range(4)]

    def start(g, got, send_sems, recv_sems, base=0):
        for cp in copies(g, got, send_sems, recv_sems, base):
            cp.start()

    def finish(g, got, send_sems, recv_sems, base=0):
        for cp in copies(g, got, send_sems, recv_sems, base):
            cp.wait()

    return start, finish, 4 * len(geo)


def _swap_shapes(widx):
    return [jax.ShapeDtypeStruct((4,) + _piece_shape(*BIG[i][1:]), F32) for i in widx]


def _reduce_swap_halves(tag, grads, widx):
    nw = len(widx)
    start, finish, n_sems = _swap_copies(widx)

    def body(*refs):
        start(refs[:nw], refs[nw:2 * nw], *refs[2 * nw:])
        finish(refs[:nw], refs[nw:2 * nw], *refs[2 * nw:])

    return pl.pallas_call(
        body, name="reduce_swap_halves_" + tag, in_specs=[HBM_SPEC] * nw, out_specs=[HBM_SPEC] * nw,
        out_shape=_swap_shapes(widx),
        scratch_shapes=[pltpu.SemaphoreType.DMA((n_sems,)), pltpu.SemaphoreType.DMA((n_sems,))])(*grads)


def _owner_copies(nw):
    def copies(p, out, send_sems, recv_sems, base):
        x, y, c, chips = _mesh_pos()
        return [_remote(p[wi].at[2 * qx + qy], out[wi].at[j], send_sems.at[base + wi * 3 + j],
                        recv_sems.at[base + wi * 3 + j], (qx, qy, c))
                for wi in range(nw) for j, (qx, qy) in enumerate(chips)]

    def start(p, out, send_sems, recv_sems, base=0):
        for cp in copies(p, out, send_sems, recv_sems, base):
            cp.start()

    def finish(p, out, send_sems, recv_sems, base=0):
        for cp in copies(p, out, send_sems, recv_sems, base):
            cp.wait()

    return start, finish, 3 * nw


def _join_carries(a, b):
    if a is None or b is None:
        return a if b is None else b
    n_i, n_o = len(a.ins), len(a.outs)
    outs = list(a.outs) + [o + n_i if isinstance(o, int) else o for o in b.outs]

    def start(c_in, c_out, send_sems, recv_sems):
        a.start(c_in[:n_i], c_out[:n_o], send_sems, recv_sems)
        b.start(c_in[n_i:], c_out[n_o:], send_sems, recv_sems, base=a.n_sems)

    def finish(c_in, c_out, send_sems, recv_sems):
        a.finish(c_in[:n_i], c_out[:n_o], send_sems, recv_sems)
        b.finish(c_in[n_i:], c_out[n_o:], send_sems, recv_sems, base=a.n_sems)

    def done(res):
        a.done(res[:n_o])
        b.done(res[n_o:])

    return _Carry(a.ins + b.ins, outs, a.n_sems + b.n_sems, start, finish, done)


def _share_copies():
    def copy(out, wi, half, send_sems, recv_sems, sib):
        _, k, n, ax = BIG[wi]
        piece = _shard_piece(out[wi], k, n, ax, half)
        return _remote(piece, piece, send_sems.at[wi], recv_sems.at[wi], sib)

    def start(_, out, send_sems, recv_sems):
        x, y, c, _ = _mesh_pos()
        for wi in range(len(BIG)):
            copy(out, wi, c, send_sems, recv_sems, (x, y, 1 - c)).start()

    def finish(_, out, send_sems, recv_sems):
        x, y, c, _ = _mesh_pos()
        for wi in range(len(BIG)):
            copy(out, wi, 1 - c, send_sems, recv_sems, (x, y, 1 - c)).wait_recv()
            copy(out, wi, c, send_sems, recv_sems, (x, y, 1 - c)).wait_send()

    return start, finish, len(BIG)


def _allreduce_small(v):
    r = v.shape[0]
    rh = r // 2
    assert rh % 8 == 0

    def body(v_ref, o_ref, sib_buf, chip_buf, send_sems, recv_sems):
        x, y, c, chips = _mesh_pos()
        me = 2 * x + y
        sib = (x, y, 1 - c)
        mine = pl.ds(pl.multiple_of(c * rh, 8), rh)
        other = pl.ds(pl.multiple_of((1 - c) * rh, 8), rh)
        swap = _remote(v_ref.at[other], sib_buf, send_sems.at[0], recv_sems.at[0], sib)
        swap.start()
        swap.wait()
        chip_buf[me] = v_ref[mine, :] + sib_buf[...]
        cps = []
        for j, (qx, qy) in enumerate(chips):
            cp = _remote(chip_buf.at[me], chip_buf.at[me], send_sems.at[1 + j], recv_sems.at[1 + j], (qx, qy, c))
            cp.start()
            cps.append(cp)
        for j, (qx, qy) in enumerate(chips):
            slot = chip_buf.at[2 * qx + qy]
            _remote(slot, slot, send_sems.at[1 + j], recv_sems.at[1 + j], (qx, qy, c)).wait_recv()
        for cp in cps:
            cp.wait_send()
        o_ref[mine, :] = ((chip_buf[0] + chip_buf[1]) + chip_buf[2]) + chip_buf[3]
        back = _remote(o_ref.at[mine], o_ref.at[mine], send_sems.at[4], recv_sems.at[4], sib)
        back.start()
        _remote(o_ref.at[other], o_ref.at[other], send_sems.at[4], recv_sems.at[4], sib).wait_recv()
        back.wait_send()

    return pl.pallas_call(
        body, name="allreduce_small", in_specs=[VMEM_SPEC], out_specs=VMEM_SPEC,
        out_shape=jax.ShapeDtypeStruct((r, 128), F32),
        scratch_shapes=[pltpu.VMEM((rh, 128), F32), pltpu.VMEM((4, rh, 128), F32),
                        pltpu.SemaphoreType.DMA((5,)), pltpu.SemaphoreType.DMA((5,))],
        compiler_params=pltpu.CompilerParams(vmem_limit_bytes=VMEM_LIMIT))(v)


def _as2d(a):
    a = a.reshape((-1, a.shape[-1])) if a.ndim > 1 else a.reshape(1, -1)
    return a


def _adamw_small(quads):
    n = len(quads)

    def body(*refs):
        for i in range(n):
            w, g, m, v = (r[...] for r in refs[4 * i:4 * i + 4])
            for ref, val in zip(refs[4 * n + 3 * i:4 * n + 3 * i + 3], _adamw(w, g, m, v)):
                ref[...] = val

    return pl.pallas_call(
        body, name="adamw_small", in_specs=[VMEM_SPEC] * (4 * n), out_specs=[VMEM_SPEC] * (3 * n),
        out_shape=[jax.ShapeDtypeStruct(q[0].shape, F32) for q in quads for _ in range(3)],
        compiler_params=pltpu.CompilerParams(vmem_limit_bytes=VMEM_LIMIT))(*[a for q in quads for a in q])


def _where():
    return jnp.stack([2 * lax.axis_index("x") + lax.axis_index("y"), lax.axis_index("c")]).astype(jnp.int32)


_BIG_INDEX = {name: i for i, (name, _, _, _) in enumerate(BIG)}


class _Exchange:
    def __init__(self, inputs, where):
        self.inputs, self.where = inputs, where
        self.full, self.ready = {}, set()
        self.raw, self.got, self.parts, self.landed, self.geom = {}, {}, {}, {}, {}
        for name, k, n, ax in BIG:
            w2 = inputs[name][0]
            rs, cs = w2.shape
            tm = _tile(rs, 512)
            steps = rs // tm
            if ax == 1:
                blk, idx = (tm, cs), lambda i, w: (i, w[0])
            else:
                blk, idx = (tm, n), functools.partial(lambda i, w, steps: (w[0] * steps + i, 0), steps=steps)
            self.full[name] = _placed("cast_" + name, lambda w: w, steps, where, [(w2, (tm, cs), lambda i, w: (i, 0))],
                                      jax.ShapeDtypeStruct((k, n), MXU_DTYPE), blk, idx)

    def _gathered(self, names, outs):
        for name, o in zip(names, outs):
            self.full[name] = o
            self.ready.add(name)

    def gather_carry(self, names):
        start, finish, n_sems = _gather_copies([_BIG_INDEX[n] for n in names])
        return _Carry([self.full[n] for n in names], list(range(len(names))), n_sems, start, finish,
                      functools.partial(self._gathered, names))

    def weight(self, name):
        assert name in self.ready, name
        return self.full[name]

    def grad(self, name, g):
        self.raw[name] = g

    def _swapped(self, names, outs):
        for name, o in zip(names, outs):
            self.got[name] = o

    def _pair_sum(self, name):
        i = _BIG_INDEX[name]
        _, k, n, ax = BIG[i]
        g = self.raw[name]
        if name not in self.got:
            self._swapped([name], _reduce_swap_halves(name, [g], [i]))
        got = self.got[name]
        pr, pc = _piece_shape(k, n, ax)
        tm = _tile(pr, 512)
        spp = pr // tm
        self.geom[name] = (pr, pc, tm, spp)
        if ax == 1:
            g_idx = functools.partial(lambda i, w, spp: (w[1] * spp + i % spp, i // spp), spp=spp)
        else:
            g_idx = functools.partial(lambda i, w, spp: ((i // spp) * 2 * spp + w[1] * spp + i % spp, 0), spp=spp)
        self.parts[name] = _placed(
            "pair_sum_" + name, lambda a, b: a + b, 4 * spp, self.where,
            [(g, (tm, pc), g_idx), (got.reshape(4 * pr, pc), (tm, pc), lambda i, w: (i, 0))],
            jax.ShapeDtypeStruct((4 * pr, pc), BF16), (tm, pc), lambda i, w: (i, 0)).reshape(4, pr, pc)

    def _landed(self, names, outs):
        for name, o in zip(names, outs):
            self.landed[name] = o

    def carry(self, swap=(), ici=()):
        first = second = None
        if swap:
            widx = [_BIG_INDEX[n] for n in swap]
            start, finish, n_sems = _swap_copies(widx)
            first = _Carry([self.raw[n] for n in swap], _swap_shapes(widx), n_sems, start, finish,
                           functools.partial(self._swapped, list(swap)))
        if ici:
            for n in ici:
                self._pair_sum(n)
            start, finish, n_sems = _owner_copies(len(ici))
            parts = [self.parts[n] for n in ici]
            outs = [jax.ShapeDtypeStruct((3,) + p.shape[1:], p.dtype) for p in parts]
            second = _Carry(parts, outs, n_sems, start, finish, functools.partial(self._landed, list(ici)))
        return _join_carries(first, second)

    def _shared(self, outs):
        self.shards = dict(zip([b[0] for b in BIG], outs))

    def finish_carry(self):
        halves = []
        for name, _, _, _ in BIG:
            pr, pc, tm, spp = self.geom[name]
            ins = [(self.parts[name], (None, tm, pc), lambda i, w: (w[0], i, 0))]
            ins += [(self.landed[name], (None, tm, pc), functools.partial(lambda i, w, j: (j, i, 0), j=j))
                    for j in range(3)]
            halves.append(_placed("chip_sum_" + name,
                                  lambda a, b, c, d: ((a.astype(F32) + b.astype(F32)) + c.astype(F32)) + d.astype(F32),
                                  spp, self.where, ins, jax.ShapeDtypeStruct(self.inputs[name].shape[1:], F32), (tm, pc),
                                  functools.partial(lambda i, w, spp: (w[1] * spp + i, 0), spp=spp)))
        start, finish, n_sems = _share_copies()
        return _Carry(halves, list(range(len(halves))), n_sems, start, finish, self._shared)


def _step(inputs):
    x, mem, positions, target = inputs["x"][0], inputs["mem"][0], inputs["positions"], inputs["loss_target"][0]
    pos = positions.reshape(-1, 1)
    ex = _Exchange(inputs, _where())
    sp = {name: _as2d(inputs[name]) for name in SMALL}
    memb, = _rowwise("cast_mem", lambda m: (m,), [mem], [], [(D_MODEL, MXU_DTYPE)])

    loss, dx, gsmall = _local_step(x, memb, pos, target, sp, ex)
    gshard = ex.shards

    out = {}
    for name, _, _, _ in BIG:
        w2, m2, v2 = inputs[name][0], inputs["m_" + name][0], inputs["v_" + name][0]
        n = w2.shape[1]
        d, nm, nv = _rowwise("adamw_" + name, _adamw, [w2, gshard[name], m2, v2], [], [(n, F32)] * 3, tm=_tile(w2.shape[0], 512))
        lead = inputs[name].shape
        out[name] = (gshard[name].reshape(lead), d.reshape(lead), nm.reshape(lead), nv.reshape(lead))

    def tiles(a):
        flat = a.reshape(-1)
        n = -(-flat.shape[0] // 1024) * 1024
        return jnp.pad(flat, (0, n - flat.shape[0])).reshape(n // 128, 128)

    pieces = [tiles(loss[:, :1])] + [tiles(gsmall[name]) for name in SMALL]
    if sum(p.shape[0] for p in pieces) % 16:
        pieces.append(jnp.zeros((8, 128), F32))
    red = _allreduce_small(jnp.concatenate(pieces, axis=0))
    loss_total = red[0, 0]
    grads, off = {}, pieces[0].shape[0]
    for name, p in zip(SMALL, pieces[1:]):
        shp = _as2d(inputs[name]).shape
        grads[name] = red[off:off + p.shape[0]].reshape(-1)[:shp[0] * shp[1]].reshape(shp)
        off += p.shape[0]
    upd = _adamw_small([(_as2d(inputs[n]), grads[n], _as2d(inputs["m_" + n]), _as2d(inputs["v_" + n])) for n in SMALL])
    for i, name in enumerate(SMALL):
        shp = inputs[name].shape
        out[name] = (grads[name].reshape(shp),) + tuple(t.reshape(shp) for t in upd[3 * i:3 * i + 3])
    return loss_total, dx.reshape(inputs["x"].shape), out


_ARG_NAMES = (("x", "mem", "positions") + WEIGHT_ORDER + ("loss_target",) + tuple("m_" + n for n in WEIGHT_ORDER)
              + tuple("v_" + n for n in WEIGHT_ORDER))


def kernel(x, mem, positions, ln_in_g, ln_in_b, w_in, b_in, ssm_log_dt, ssm_a_re, ssm_a_im, ssm_b_re, ssm_b_im, ssm_c_re, ssm_c_im, ssm_d, w_glu, b_glu, w_att_up, w_mix_out, b_mix_out, ln1_g, ln1_b, w_xq, w_xkv, w_xo, ln2_g, ln2_b, w_ff1, b_ff1, w_ff2, b_ff2, ln3_g, ln3_b, loss_target, m_ln_in_g, m_ln_in_b, m_w_in, m_b_in, m_ssm_log_dt, m_ssm_a_re, m_ssm_a_im, m_ssm_b_re, m_ssm_b_im, m_ssm_c_re, m_ssm_c_im, m_ssm_d, m_w_glu, m_b_glu, m_w_att_up, m_w_mix_out, m_b_mix_out, m_ln1_g, m_ln1_b, m_w_xq, m_w_xkv, m_w_xo, m_ln2_g, m_ln2_b, m_w_ff1, m_b_ff1, m_w_ff2, m_b_ff2, m_ln3_g, m_ln3_b, v_ln_in_g, v_ln_in_b, v_w_in, v_b_in, v_ssm_log_dt, v_ssm_a_re, v_ssm_a_im, v_ssm_b_re, v_ssm_b_im, v_ssm_c_re, v_ssm_c_im, v_ssm_d, v_w_glu, v_b_glu, v_w_att_up, v_w_mix_out, v_b_mix_out, v_ln1_g, v_ln1_b, v_w_xq, v_w_xkv, v_w_xo, v_ln2_g, v_ln2_b, v_w_ff1, v_b_ff1, v_w_ff2, v_b_ff2, v_ln3_g, v_ln3_b):
    args = (x, mem, positions, ln_in_g, ln_in_b, w_in, b_in, ssm_log_dt, ssm_a_re, ssm_a_im, ssm_b_re, ssm_b_im, ssm_c_re, ssm_c_im, ssm_d, w_glu, b_glu, w_att_up, w_mix_out, b_mix_out, ln1_g, ln1_b, w_xq, w_xkv, w_xo, ln2_g, ln2_b, w_ff1, b_ff1, w_ff2, b_ff2, ln3_g, ln3_b, loss_target, m_ln_in_g, m_ln_in_b, m_w_in, m_b_in, m_ssm_log_dt, m_ssm_a_re, m_ssm_a_im, m_ssm_b_re, m_ssm_b_im, m_ssm_c_re, m_ssm_c_im, m_ssm_d, m_w_glu, m_b_glu, m_w_att_up, m_w_mix_out, m_b_mix_out, m_ln1_g, m_ln1_b, m_w_xq, m_w_xkv, m_w_xo, m_ln2_g, m_ln2_b, m_w_ff1, m_b_ff1, m_w_ff2, m_b_ff2, m_ln3_g, m_ln3_b, v_ln_in_g, v_ln_in_b, v_w_in, v_b_in, v_ssm_log_dt, v_ssm_a_re, v_ssm_a_im, v_ssm_b_re, v_ssm_b_im, v_ssm_c_re, v_ssm_c_im, v_ssm_d, v_w_glu, v_b_glu, v_w_att_up, v_w_mix_out, v_b_mix_out, v_ln1_g, v_ln1_b, v_w_xq, v_w_xkv, v_w_xo, v_ln2_g, v_ln2_b, v_w_ff1, v_b_ff1, v_w_ff2, v_b_ff2, v_ln3_g, v_ln3_b)
    assert len(args) == len(_ARG_NAMES)
    inputs = dict(zip(_ARG_NAMES, args))
    loss, dx, out = _step(inputs)
    res = [loss, dx]
    for k in range(4):
        res += [out[name][k] for name in WEIGHT_ORDER]
    return tuple(res)
```

```python
import functools
import math

import numpy as np
import jax
import jax.numpy as jnp
from jax import lax
from jax.experimental import pallas as pl
from jax.experimental.pallas import tpu as pltpu

F32 = jnp.float32
BF16 = jnp.bfloat16
MXU_DTYPE = jnp.bfloat16

D_MODEL = 1024
SSM_GROUP = 16
SSM_WIDTH = 768
SSM_GROUPS = 48
SSM_STATE = 64
N_STATE = SSM_GROUPS * SSM_STATE
SSM_CHUNKS = 6
CH_W = 128
CH_N = 512
ATT_HEAD_DIM = 64
ATT_HPG = 4
ATT_GROUPW = ATT_HPG * ATT_HEAD_DIM
DILATIONS = (1, 4, 16)
ATT_BLK = 128
ATT_SCALE = ATT_HEAD_DIM ** -0.5
ROT_DIM = 16
ROPE_THETA = 500000.0
XATT_HEADS = 4
XATT_HEAD_DIM = 256
XATT_SCALE = XATT_HEAD_DIM ** -0.5
D_FF = 4096
IN_COLS = 5120
DEEPNORM_ALPHA = 2.0 ** 0.25
LN_EPS = 1e-5
NEG_INF = -1e30
ADAM_LR = 0.001
ADAM_B1 = 0.9
ADAM_B2 = 0.999
ADAM_EPS = 1e-08
ADAM_WD = 0.01
ADAM_STEP = 10

N_SEG = 32
VMEM_LIMIT = 56 * 1024 * 1024
MESH = pl.DeviceIdType.MESH
HBM_SPEC = pl.BlockSpec(memory_space=pltpu.HBM)
VMEM_SPEC = pl.BlockSpec(memory_space=pltpu.VMEM)

BIG = (("w_in", 1024, 5120, 1), ("w_glu", 768, 2048, 1), ("w_att_up", 256, 1024, 1),
       ("w_mix_out", 1024, 1024, 0), ("w_xq", 1024, 1024, 0), ("w_xkv", 1024, 2048, 1),
       ("w_xo", 1024, 1024, 0), ("w_ff1", 1024, 4096, 1), ("w_ff2", 4096, 1024, 0))
SMALL = ("ln_in_g", "ln_in_b", "b_in", "ssm_log_dt", "ssm_a_re", "ssm_a_im", "ssm_b_re", "ssm_b_im",
         "ssm_c_re", "ssm_c_im", "ssm_d", "b_glu", "b_mix_out", "ln1_g", "ln1_b", "ln2_g", "ln2_b",
         "b_ff1", "b_ff2", "ln3_g", "ln3_b")
WEIGHT_ORDER = ("ln_in_g", "ln_in_b", "w_in", "b_in", "ssm_log_dt", "ssm_a_re", "ssm_a_im", "ssm_b_re",
                "ssm_b_im", "ssm_c_re", "ssm_c_im", "ssm_d", "w_glu", "b_glu", "w_att_up", "w_mix_out",
                "b_mix_out", "ln1_g", "ln1_b", "w_xq", "w_xkv", "w_xo", "ln2_g", "ln2_b", "w_ff1", "b_ff1",
                "w_ff2", "b_ff2", "ln3_g", "ln3_b")


def _cparams(n_axes):
    return pltpu.CompilerParams(dimension_semantics=("arbitrary",) * n_axes, vmem_limit_bytes=VMEM_LIMIT)


class _Carry:
    def __init__(self, ins, outs, n_sems, start, finish, done):
        self.ins, self.outs, self.n_sems, self.start, self.finish, self.done = ins, outs, n_sems, start, finish, done


def _call(name, body, grid, in_specs, out_specs, out_shape, args, scratch_shapes=(), carry=None):
    in_specs, out_specs, out_shape = list(in_specs), list(out_specs), list(out_shape)
    params = _cparams(len(grid))
    if carry is None:
        return pl.pallas_call(body, name=name, grid=grid, in_specs=in_specs, out_specs=out_specs, out_shape=out_shape,
                              scratch_shapes=list(scratch_shapes), compiler_params=params)(*args)
    n_in, n_out, n_ci, n_co = len(in_specs), len(out_specs), len(carry.ins), len(carry.outs)
    n_scr = len(scratch_shapes)

    def wrapped(*refs):
        ins, c_in = refs[:n_in], refs[n_in:n_in + n_ci]
        outs, c_out = refs[n_in + n_ci:n_in + n_ci + n_out], refs[n_in + n_ci + n_out:n_in + n_ci + n_out + n_co]
        scratch = refs[n_in + n_ci + n_out + n_co:n_in + n_ci + n_out + n_co + n_scr]
        send_sems, recv_sems = refs[-2:]
        ids = [pl.program_id(a) for a in range(len(grid))]
        first = functools.reduce(jnp.logical_and, [i == 0 for i in ids])
        last = functools.reduce(jnp.logical_and, [i == g - 1 for i, g in zip(ids, grid)])

        @pl.when(first)
        def _():
            carry.start(c_in, c_out, send_sems, recv_sems)

        body(*ins, *outs, *scratch)

        @pl.when(last)
        def _():
            carry.finish(c_in, c_out, send_sems, recv_sems)

    c_shapes = [jax.ShapeDtypeStruct(carry.ins[o].shape, carry.ins[o].dtype) if isinstance(o, int) else o
                for o in carry.outs]
    aliases = {n_in + o: n_out + i for i, o in enumerate(carry.outs) if isinstance(o, int)}
    res = pl.pallas_call(
        wrapped, name=name, grid=grid, in_specs=in_specs + [HBM_SPEC] * n_ci, out_specs=out_specs + [HBM_SPEC] * n_co,
        out_shape=out_shape + c_shapes, input_output_aliases=aliases,
        scratch_shapes=list(scratch_shapes) + [pltpu.SemaphoreType.DMA((carry.n_sems,))] * 2,
        compiler_params=params)(*args, *carry.ins)
    carry.done(res[n_out:])
    return res[:n_out]


def _rowwise(name, fn, rows, consts, outs, reds=(), tm=512, touts=(), carry=None):
    n_rows = (rows[0][0] if isinstance(rows[0], tuple) else rows[0]).shape[-2]
    tm = min(tm, n_rows)
    assert n_rows % tm == 0, (name, n_rows, tm)
    specs, args = [], []
    for r in rows:
        if isinstance(r, tuple) and len(r) == 3:
            arr, width, cb = r
            specs.append(pl.BlockSpec((tm, width), functools.partial(lambda i, cb: (i, cb), cb=cb)))
        elif isinstance(r, tuple):
            arr, slot = r
            specs.append(pl.BlockSpec((None, tm, arr.shape[2]), functools.partial(lambda i, s: (s, i, 0), s=slot)))
        else:
            arr = r
            specs.append(pl.BlockSpec((tm, arr.shape[1]), lambda i: (i, 0)))
        args.append(arr)
        assert arr.shape[-2] == n_rows, (name, arr.shape, n_rows)
    for cst in consts:
        specs.append(pl.BlockSpec(cst.shape, lambda i: (0, 0)))
        args.append(cst)
    n_r, n_c, n_o, n_d = len(rows), len(consts), len(outs) + len(touts), len(reds)
    out_shape = [jax.ShapeDtypeStruct((n_rows, c), dt) for c, dt in outs]
    out_specs = [pl.BlockSpec((tm, c), lambda i: (i, 0)) for c, _ in outs]
    out_shape += [jax.ShapeDtypeStruct((r, n_rows), dt) for r, dt in touts]
    out_specs += [pl.BlockSpec((r, tm), lambda i: (0, i)) for r, _ in touts]
    out_shape += [jax.ShapeDtypeStruct((1, c), F32) for c in reds]
    out_specs += [pl.BlockSpec((1, c), lambda i: (0, 0)) for c in reds]

    def body(*refs):
        ins = [r[...] for r in refs[:n_r + n_c]]
        o_refs = refs[n_r + n_c:n_r + n_c + n_o]
        d_refs = refs[n_r + n_c + n_o:]
        res = fn(*ins)
        res = res if isinstance(res, (tuple, list)) else (res,)
        assert len(res) == n_o + n_d, (name, len(res))
        for ref, val in zip(o_refs, res[:n_o]):
            ref[...] = val.astype(ref.dtype)
        first = pl.program_id(0) == 0
        for ref, val in zip(d_refs, res[n_o:]):
            @pl.when(first)
            def _(ref=ref, val=val):
                ref[...] = val

            @pl.when(jnp.logical_not(first))
            def _(ref=ref, val=val):
                ref[...] += val

    return _call(name, body, (n_rows // tm,), specs, out_specs, out_shape, args, carry=carry)


def _colsum(v):
    return jnp.sum(v.astype(F32), axis=0, keepdims=True)


_DIMS = {"nn": (((1,), (0,)), ((), ())), "nt": (((1,), (1,)), ((), ())), "tn": (((0,), (0,)), ((), ()))}


def _tile(dim, want):
    if dim <= want:
        return dim
    return max(t for t in range(128, want + 1, 128) if dim % t == 0)


def _dot(a, b, mode):
    return lax.dot_general(a.astype(MXU_DTYPE), b.astype(MXU_DTYPE), _DIMS[mode], preferred_element_type=F32)


def _mm(name, a, b, mode, *, bias=None, extras=(), epilogue=None, out_dtypes=(F32,), tm=1024, tn=1024, tk=1024,
        carry=None, colsum=False):
    if mode == "nn":
        (m, k), (_, n) = a.shape, b.shape
    elif mode == "nt":
        (m, k), (n, _) = a.shape, b.shape
    else:
        (k, m), (_, n) = a.shape, b.shape
    if k > tk:
        tk = 5 * tk
    tn = _tile(n, tn)
    tk = _tile(k, tk)
    nk = k // tk

    def vmem_bytes(rows):
        blocks = rows * tk * a.dtype.itemsize + tk * tn * b.dtype.itemsize
        blocks += sum(rows * tn * e.dtype.itemsize for e in extras)
        blocks += sum(rows * tn * jnp.dtype(dt).itemsize for dt in out_dtypes)
        return 2 * blocks + (rows * tn * 4 if nk > 1 else 0)

    tm = _tile(m, tm if mode == "tn" else 2 * tm)
    while vmem_bytes(tm) > 3 * VMEM_LIMIT // 4 and tm % 256 == 0:
        tm //= 2
    while nk == 1 and k > 1024 and (m // tm) * (n // tn) < 4 and tm % 256 == 0:
        tm //= 2
    assert m % tm == 0 and n % tn == 0 and k % tk == 0, (name, m, n, k)
    a_spec = {"nn": pl.BlockSpec((tm, tk), lambda i, j, kk: (i, kk)),
              "nt": pl.BlockSpec((tm, tk), lambda i, j, kk: (i, kk)),
              "tn": pl.BlockSpec((tk, tm), lambda i, j, kk: (kk, i))}[mode]
    b_spec = {"nn": pl.BlockSpec((tk, tn), lambda i, j, kk: (kk, j)),
              "nt": pl.BlockSpec((tn, tk), lambda i, j, kk: (j, kk)),
              "tn": pl.BlockSpec((tk, tn), lambda i, j, kk: (kk, j))}[mode]
    specs, args = [a_spec, b_spec], [a, b]
    if bias is not None:
        specs.append(pl.BlockSpec((1, tn), lambda i, j, kk: (0, j)))
        args.append(bias)
    for e in extras:
        specs.append(pl.BlockSpec((tm, tn), lambda i, j, kk: (i, j)))
        args.append(e)
    n_e, n_o = len(extras), len(out_dtypes)
    has_bias = bias is not None

    def body(*refs):
        a_ref, b_ref = refs[0], refs[1]
        pos = 2
        bias_ref = refs[pos] if has_bias else None
        pos += int(has_bias)
        e_refs = refs[pos:pos + n_e]
        o_refs = refs[pos + n_e:pos + n_e + n_o]
        sum_ref = refs[pos + n_e + n_o] if colsum else None
        acc_ref = refs[pos + n_e + n_o + int(colsum)] if nk > 1 else None
        part = _dot(a_ref[...], b_ref[...], mode)

        def finish(r):
            if has_bias:
                r = r + bias_ref[...]
            res = epilogue(r, *[e[...] for e in e_refs]) if epilogue is not None else (r,)
            for ref, val in zip(o_refs, res):
                ref[...] = val.astype(ref.dtype)
            if colsum:
                sum_ref[...] = _colsum(res[0])

        if nk == 1:
            finish(part)
        else:
            kk = pl.program_id(2)

            @pl.when(kk == 0)
            def _():
                acc_ref[...] = part

            @pl.when(kk > 0)
            def _():
                acc_ref[...] += part

            @pl.when(kk == nk - 1)
            def _():
                finish(acc_ref[...])

    out_specs = [pl.BlockSpec((tm, tn), lambda i, j, kk: (i, j)) for _ in out_dtypes]
    out_shape = [jax.ShapeDtypeStruct((m, n), dt) for dt in out_dtypes]
    if colsum:
        out_specs.append(pl.BlockSpec((None, 1, tn), lambda i, j, kk: (i, 0, j)))
        out_shape.append(jax.ShapeDtypeStruct((m // tm, 1, n), F32))
    res = _call(name, body, (m // tm, n // tn, nk), specs, out_specs, out_shape, args,
                scratch_shapes=[pltpu.VMEM((tm, tn), F32)] if nk > 1 else [], carry=carry)
    return res[0] if len(res) == 1 else res


def _ssm_wgrads(u, dy, g_re, g_im, h_re, h_im, tk=2048):
    s = u.shape[0]
    tk = min(tk, s)
    nk = s // tk
    assert tk % N_SEG == 0

    def body(u_ref, dy_ref, gre_ref, gim_ref, hre_ref, him_ref, lre_ref, lim_ref, db_ref, dc_ref, dar_ref, dai_ref,
             pre_ref, pim_ref):
        kk = pl.program_id(1)
        u_blk, dy_blk = u_ref[...], dy_ref[...]
        g_r, g_i, h_r, h_i = gre_ref[...], gim_ref[...], hre_ref[...], him_ref[...]
        d_b = jnp.concatenate([_dot(u_blk, g_r, "tn"), _dot(u_blk, g_i, "tn")], axis=1)
        d_c = jnp.concatenate([_dot(h_r, dy_blk, "tn"), _dot(h_i, dy_blk, "tn")], axis=0)

        @pl.when(kk == 0)
        def _():
            first_row = lax.broadcasted_iota(jnp.int32, (N_SEG, CH_N), 0) == 0
            pre_ref[...] = jnp.where(first_row, 0.0, pltpu.roll(lre_ref[...], 1, 0))
            pim_ref[...] = jnp.where(first_row, 0.0, pltpu.roll(lim_ref[...], 1, 0))

        p_r = jnp.concatenate([pre_ref[...], h_r[:tk - N_SEG]], axis=0)
        p_i = jnp.concatenate([pim_ref[...], h_i[:tk - N_SEG]], axis=0)
        pre_ref[...] = h_r[tk - N_SEG:]
        pim_ref[...] = h_i[tk - N_SEG:]
        d_ar = jnp.sum(g_r * p_r + g_i * p_i, axis=0, keepdims=True)
        d_ai = jnp.sum(g_i * p_r - g_r * p_i, axis=0, keepdims=True)

        @pl.when(kk == 0)
        def _():
            db_ref[...] = d_b
            dc_ref[...] = d_c
            dar_ref[...] = d_ar
            dai_ref[...] = d_ai

        @pl.when(kk > 0)
        def _():
            db_ref[...] += d_b
            dc_ref[...] += d_c
            dar_ref[...] += d_ar
            dai_ref[...] += d_ai

    chan = pl.BlockSpec((tk, CH_W), lambda j, kk: (kk, j))
    state = pl.BlockSpec((tk, CH_N), lambda j, kk: (kk, j))
    last = pl.BlockSpec((N_SEG, CH_N), lambda j, kk: (s // N_SEG - 1, j))
    row = pl.BlockSpec((1, CH_N), lambda j, kk: (0, j))
    return pl.pallas_call(
        body, name="ssm_wgrads", grid=(SSM_CHUNKS, nk),
        in_specs=[chan, chan, state, state, state, state, last, last],
        out_specs=[pl.BlockSpec((None, CH_W, 2 * CH_N), lambda j, kk: (j, 0, 0)),
                   pl.BlockSpec((None, 2 * CH_N, CH_W), lambda j, kk: (j, 0, 0)), row, row],
        out_shape=[jax.ShapeDtypeStruct((SSM_CHUNKS, CH_W, 2 * CH_N), F32),
                   jax.ShapeDtypeStruct((SSM_CHUNKS, 2 * CH_N, CH_W), F32),
                   jax.ShapeDtypeStruct((1, N_STATE), F32), jax.ShapeDtypeStruct((1, N_STATE), F32)],
        scratch_shapes=[pltpu.VMEM((N_SEG, CH_N), F32)] * 2,
        compiler_params=_cparams(2))(u, dy, g_re, g_im, h_re, h_im, h_re, h_im)


SCAN_LB = 256


def _split_by_scan_block(mat, axis):
    halves = []
    for l in range(CH_N // SCAN_LB):
        re = lax.slice_in_dim(mat, l * SCAN_LB, (l + 1) * SCAN_LB, axis=axis)
        im = lax.slice_in_dim(mat, CH_N + l * SCAN_LB, CH_N + (l + 1) * SCAN_LB, axis=axis)
        halves.append(jnp.concatenate([re, im], axis=axis))
    return jnp.stack(halves, axis=1).reshape((-1,) + halves[0].shape[1:])


def _ssm_scan(name, chan, expand12, contract12, a_re, a_im, d_row, reverse, carry=None):
    s = chan.shape[0]
    seg_len = s // N_SEG
    n_sq = int(math.log2(seg_len))
    assert 2 ** n_sq == seg_len
    rb = min(512, s)
    per_chunk = CH_N // SCAN_LB

    def body(are_ref, aim_ref, ch_ref, e_ref, k_ref, d_ref, hre_ref, him_ref, o_ref, wre_ref, wim_ref, ere, eim, cre, cim):
        e_mat, k_mat = e_ref[...], k_ref[...]
        for r in range(s // rb):
            rows = slice(r * rb, (r + 1) * rb)
            w = _dot(ch_ref[rows, :], e_mat, "nt" if reverse else "nn")
            wre_ref[rows, :] = w[:, :SCAN_LB]
            wim_ref[rows, :] = w[:, SCAN_LB:]

        ar1 = are_ref[...]
        ai1 = -aim_ref[...] if reverse else aim_ref[...]
        ar = jnp.broadcast_to(ar1, (N_SEG, SCAN_LB))
        ai = jnp.broadcast_to(ai1, (N_SEG, SCAN_LB))

        def rows_of(k):
            kk = seg_len - 1 - k if reverse else k
            return pl.ds(pl.multiple_of(kk * N_SEG, N_SEG), N_SEG)

        def local(k, carry):
            hr, hi = carry
            rows = rows_of(k)
            nr = ar * hr - ai * hi + wre_ref[rows, :]
            ni = ar * hi + ai * hr + wim_ref[rows, :]
            hre_ref[rows, :] = nr
            him_ref[rows, :] = ni
            return nr, ni

        zero = jnp.zeros((N_SEG, SCAN_LB), F32)
        er, ei = lax.fori_loop(0, seg_len, local, (zero, zero))
        ere[...] = er
        eim[...] = ei
        pr, pi = ar1, ai1
        for _ in range(n_sq):
            pr, pi = pr * pr - pi * pi, 2.0 * pr * pi
        cr = jnp.zeros((1, SCAN_LB), F32)
        ci = jnp.zeros((1, SCAN_LB), F32)
        for jj in range(N_SEG):
            j = N_SEG - 1 - jj if reverse else jj
            cre[j:j + 1, :] = cr
            cim[j:j + 1, :] = ci
            er_j, ei_j = ere[j:j + 1, :], eim[j:j + 1, :]
            cr, ci = pr * cr - pi * ci + er_j, pr * ci + pi * cr + ei_j
        c_r, c_i = cre[...], cim[...]

        def fix(k, carry):
            qr, qi = carry
            rows = rows_of(k)
            hre_ref[rows, :] = hre_ref[rows, :] + (qr * c_r - qi * c_i)
            him_ref[rows, :] = him_ref[rows, :] + (qr * c_i + qi * c_r)
            return qr * ar - qi * ai, qr * ai + qi * ar

        lax.fori_loop(0, seg_len, fix, (ar, ai))

        first_of_chunk = lax.rem(pl.program_id(0), per_chunk) == 0
        for r in range(s // rb):
            rows = slice(r * rb, (r + 1) * rb)
            h_cat = jnp.concatenate([hre_ref[rows, :], him_ref[rows, :]], axis=1)
            part = _dot(h_cat, k_mat, "nt" if reverse else "nn")

            @pl.when(first_of_chunk)
            def _(rows=rows, part=part):
                o_ref[rows, :] = part + d_ref[...] * ch_ref[rows, :]

            @pl.when(jnp.logical_not(first_of_chunk))
            def _(rows=rows, part=part):
                o_ref[rows, :] += part

    nblk = N_STATE // SCAN_LB
    blk = pl.BlockSpec((s, SCAN_LB), lambda b: (0, b))
    row = pl.BlockSpec((1, SCAN_LB), lambda b: (0, b))
    chan_blk = pl.BlockSpec((s, CH_W), lambda b: (0, b // per_chunk))
    res = _call(name, body, (nblk,),
                [row, row, chan_blk, pl.BlockSpec((None,) + expand12.shape[1:], lambda b: (b, 0, 0)),
                 pl.BlockSpec((None,) + contract12.shape[1:], lambda b: (b, 0, 0)),
                 pl.BlockSpec((1, CH_W), lambda b: (0, b // per_chunk))],
                [blk, blk, chan_blk],
                [jax.ShapeDtypeStruct((s, N_STATE), F32)] * 2 + [jax.ShapeDtypeStruct((s, SSM_WIDTH), F32)],
                (a_re, a_im, chan, expand12, contract12, d_row),
                scratch_shapes=[pltpu.VMEM((s, SCAN_LB), F32)] * 2 + [pltpu.VMEM((N_SEG, SCAN_LB), F32)] * 4, carry=carry)
    return res[0], res[1], res[2]


def _disc(ldt, are, aim, bre, bim):
    dt = jnp.exp(ldt)
    mag = jnp.exp(are * dt)
    abr = mag * jnp.cos(aim * dt)
    abi = mag * jnp.sin(aim * dt)
    den = jnp.square(are) + jnp.square(aim)
    nr = abr - 1.0
    fre = (nr * are + abi * aim) / den
    fim = (abi * are - nr * aim) / den
    return abr, abi, fre * bre - fim * bim, fre * bim + fim * bre


def _ssm_disc_fwd(ldt, are, aim, bre, bim):
    def body(l_ref, ar_ref, ai_ref, br_ref, bi_ref, o0, o1, o2, o3):
        res = _disc(l_ref[...], ar_ref[...], ai_ref[...], br_ref[...], bi_ref[...])
        for ref, val in zip((o0, o1, o2, o3), res):
            ref[...] = val

    col = jax.ShapeDtypeStruct((N_STATE, 1), F32)
    mat = jax.ShapeDtypeStruct((N_STATE, SSM_GROUP), F32)
    return pl.pallas_call(body, name="ssm_disc_fwd", out_shape=[col, col, mat, mat],
                          in_specs=[VMEM_SPEC] * 5, out_specs=[VMEM_SPEC] * 4)(ldt, are, aim, bre, bim)


def _ssm_disc_bwd(ldt, are, aim, bre, bim, d_abr, d_abi, d_bbr, d_bbi):
    def body(l_ref, ar_ref, ai_ref, br_ref, bi_ref, c0, c1, c2, c3, g_ldt, g_are, g_aim, g_bre, g_bim):
        _, vjp = jax.vjp(_disc, l_ref[...], ar_ref[...], ai_ref[...], br_ref[...], bi_ref[...])
        dl, dar, dai, dbr, dbi = vjp((c0[...], c1[...], c2[...], c3[...]))
        state = lax.broadcasted_iota(jnp.int32, (N_STATE, SSM_GROUPS), 0)
        group = lax.broadcasted_iota(jnp.int32, (N_STATE, SSM_GROUPS), 1)
        pick = jnp.right_shift(state, 6) == group
        g_ldt[...] = jnp.sum(jnp.where(pick, dl, 0.0), axis=0, keepdims=True)
        g_are[...] = dar
        g_aim[...] = dai
        g_bre[...] = dbr
        g_bim[...] = dbi

    col = jax.ShapeDtypeStruct((N_STATE, 1), F32)
    mat = jax.ShapeDtypeStruct((N_STATE, SSM_GROUP), F32)
    return pl.pallas_call(body, name="ssm_disc_bwd",
                          out_shape=[jax.ShapeDtypeStruct((1, SSM_GROUPS), F32), col, col, mat, mat],
                          in_specs=[VMEM_SPEC] * 9, out_specs=[VMEM_SPEC] * 5,
                          compiler_params=pltpu.CompilerParams(vmem_limit_bytes=VMEM_LIMIT))(
        ldt, are, aim, bre, bim, d_abr, d_abi, d_bbr, d_bbi)


_EYE8 = np.eye(8, dtype=np.float32)


def _blockdiag_b(bb):
    t = bb.reshape(SSM_CHUNKS, 8, SSM_STATE, SSM_GROUP).transpose(0, 1, 3, 2)
    return jnp.einsum("igcn,gh->igchn", t, _EYE8).reshape(SSM_CHUNKS, CH_W, CH_N)


def _diag_of_b(m):
    t = jnp.einsum("igchn,gh->igcn", m.reshape(SSM_CHUNKS, 8, SSM_GROUP, 8, SSM_STATE), _EYE8)
    return t.transpose(0, 1, 3, 2).reshape(N_STATE, SSM_GROUP)


def _blockdiag_c(c):
    t = c.reshape(SSM_CHUNKS, 8, SSM_GROUP, SSM_STATE).transpose(0, 1, 3, 2)
    return jnp.einsum("ignc,gh->ignhc", t, _EYE8).reshape(SSM_CHUNKS, CH_N, CH_W)


def _diag_of_c(m):
    t = jnp.einsum("ignhc,gh->ignc", m.reshape(SSM_CHUNKS, 8, SSM_STATE, 8, SSM_GROUP), _EYE8)
    return t.transpose(0, 1, 3, 2).reshape(SSM_GROUPS, SSM_GROUP, SSM_STATE)


def _time_perm(a):
    s, c = a.shape
    return a.reshape(N_SEG, s // N_SEG, c).transpose(1, 0, 2).reshape(s, c)


def _time_unperm(a):
    s, c = a.shape
    return a.reshape(s // N_SEG, N_SEG, c).transpose(1, 0, 2).reshape(s, c)


def _dilate(a, d):
    s, c = a.shape
    return a if d == 1 else a.reshape(s // d, d, c).transpose(1, 0, 2).reshape(s, c)


def _undilate(a, d):
    s, c = a.shape
    return a if d == 1 else a.reshape(d, s // d, c).transpose(1, 0, 2).reshape(s, c)


def _dilate_rows(a, d):
    r, s = a.shape
    return a if d == 1 else a.reshape(r, s // d, d).transpose(0, 2, 1).reshape(r, s)


ATT_T_FWD = 4
ATT_T_BWD = 8


def _window(prev_ref, cur_ref, i, sl):
    if i == 0:
        return jnp.concatenate([prev_ref[:, sl], cur_ref[0:ATT_BLK, sl]], axis=0)
    return cur_ref[(i - 1) * ATT_BLK:(i + 1) * ATT_BLK, sl]


def _band_valid(first_key):
    qi = lax.broadcasted_iota(jnp.int32, (ATT_BLK, 2 * ATT_BLK), 0)
    ki = lax.broadcasted_iota(jnp.int32, (ATT_BLK, 2 * ATT_BLK), 1)
    steps = qi + ATT_BLK - ki
    return (steps >= 0) & (steps <= ATT_BLK) & (ki >= first_key)


ATT_STATW = ATT_HPG * 128


def _stat(h):
    return slice(h * 128, (h + 1) * 128)


def _stat_rows(stat):
    n = stat.shape[0]
    heads = [stat[:, _stat(h)].T[0:1, :] for h in range(ATT_HPG)]
    return jnp.concatenate(heads + [jnp.zeros((8 - ATT_HPG, n), stat.dtype)], axis=0)


def _attn_specs(nb, t, width=ATT_GROUPW):
    cur = pl.BlockSpec((t * ATT_BLK, width), lambda b: (b, 0))
    prev = pl.BlockSpec((ATT_BLK, width), lambda b: (jnp.maximum(b * t - 1, 0), 0))
    nxt = pl.BlockSpec((ATT_BLK, width), lambda b: (jnp.minimum((b + 1) * t, nb - 1), 0))
    return cur, prev, nxt


def _attn_fwd(tag, per_seq, q, k, v):
    s = q.shape[0]
    nb = s // ATT_BLK

    def body(q_ref, kc_ref, kp_ref, vc_ref, vp_ref, o_ref, lse_ref):
        bt = pl.program_id(0)
        for i in range(ATT_T_FWD):
            has_prev = lax.rem(bt * ATT_T_FWD + i, per_seq) > 0
            valid = _band_valid(jnp.where(has_prev, 0, ATT_BLK))
            rows = slice(i * ATT_BLK, (i + 1) * ATT_BLK)
            for h in range(ATT_HPG):
                sl = slice(h * ATT_HEAD_DIM, (h + 1) * ATT_HEAD_DIM)
                kcat = _window(kp_ref, kc_ref, i, sl)
                vcat = _window(vp_ref, vc_ref, i, sl)
                sc = _dot(q_ref[rows, sl], kcat, "nt") * ATT_SCALE
                sc = jnp.where(valid, sc, NEG_INF)
                m = jnp.max(sc, axis=-1, keepdims=True)
                p = jnp.exp(sc - m)
                den = jnp.sum(p, axis=-1, keepdims=True)
                o_ref[rows, sl] = _dot(p, vcat, "nn") / den
                lse_ref[rows, _stat(h)] = jnp.broadcast_to(m + jnp.log(den), (ATT_BLK, 128))

    cur, prev, _ = _attn_specs(nb, ATT_T_FWD)
    stat, _, _ = _attn_specs(nb, ATT_T_FWD, ATT_STATW)
    return pl.pallas_call(
        body, name="attn_fwd_" + tag, grid=(nb // ATT_T_FWD,), in_specs=[cur, cur, prev, cur, prev], out_specs=[cur, stat],
        out_shape=[jax.ShapeDtypeStruct((s, ATT_GROUPW), F32), jax.ShapeDtypeStruct((s, ATT_STATW), F32)],
        compiler_params=_cparams(1))(q, k, k, v, v)


def _attn_dq(tag, per_seq, q, k, v, do, lse, delta):
    s = q.shape[0]
    nb = s // ATT_BLK

    def body(q_ref, kc_ref, kp_ref, vc_ref, vp_ref, do_ref, lse_ref, dl_ref, dq_ref):
        bt = pl.program_id(0)
        for i in range(ATT_T_BWD):
            has_prev = lax.rem(bt * ATT_T_BWD + i, per_seq) > 0
            valid = _band_valid(jnp.where(has_prev, 0, ATT_BLK))
            rows = slice(i * ATT_BLK, (i + 1) * ATT_BLK)
            for h in range(ATT_HPG):
                sl = slice(h * ATT_HEAD_DIM, (h + 1) * ATT_HEAD_DIM)
                kcat = _window(kp_ref, kc_ref, i, sl)
                vcat = _window(vp_ref, vc_ref, i, sl)
                lse = jnp.concatenate([lse_ref[rows, _stat(h)]] * 2, axis=1)
                dlt = jnp.concatenate([dl_ref[rows, _stat(h)]] * 2, axis=1)
                sc = _dot(q_ref[rows, sl], kcat, "nt") * ATT_SCALE
                p = jnp.exp(jnp.where(valid, sc, NEG_INF) - lse)
                dp = _dot(do_ref[rows, sl], vcat, "nt")
                ds = p * (dp - dlt) * ATT_SCALE
                dq_ref[rows, sl] = _dot(ds, kcat, "nn")

    cur, prev, _ = _attn_specs(nb, ATT_T_BWD)
    stat, _, _ = _attn_specs(nb, ATT_T_BWD, ATT_STATW)
    return pl.pallas_call(
        body, name="attn_dq_" + tag, grid=(nb // ATT_T_BWD,), in_specs=[cur, cur, prev, cur, prev, cur, stat, stat],
        out_specs=cur, out_shape=jax.ShapeDtypeStruct((s, ATT_GROUPW), F32),
        compiler_params=_cparams(1))(q, k, k, v, v, do, lse, delta)


def _attn_dkv(tag, per_seq, q, k, v, do, lse_t, delta_t):
    s = q.shape[0]
    nb = s // ATT_BLK

    def body(k_ref, v_ref, qc_ref, qn_ref, doc_ref, don_ref, lc_ref, ln_ref, dc_ref, dn_ref, dk_ref, dv_ref):
        bt = pl.program_id(0)
        ki = lax.broadcasted_iota(jnp.int32, (ATT_BLK, 2 * ATT_BLK), 0)
        ci = lax.broadcasted_iota(jnp.int32, (ATT_BLK, 2 * ATT_BLK), 1)

        def pair(edge_ref, cur_ref, i, sl):
            if i == ATT_T_BWD - 1:
                return jnp.concatenate([cur_ref[i * ATT_BLK:(i + 1) * ATT_BLK, sl], edge_ref[:, sl]], axis=0)
            return cur_ref[i * ATT_BLK:(i + 2) * ATT_BLK, sl]

        def pair_row(edge_ref, cur_ref, i, h):
            if i == ATT_T_BWD - 1:
                row = jnp.concatenate([cur_ref[h:h + 1, i * ATT_BLK:(i + 1) * ATT_BLK], edge_ref[h:h + 1, :]], axis=1)
            else:
                row = cur_ref[h:h + 1, i * ATT_BLK:(i + 2) * ATT_BLK]
            return jnp.broadcast_to(row, (ATT_BLK, 2 * ATT_BLK))

        for i in range(ATT_T_BWD):
            b = bt * ATT_T_BWD + i
            next_uses = (b + 1 < nb) & (lax.rem(b + 1, per_seq) > 0)
            reach = jnp.where(next_uses, 0, 4 * ATT_BLK)
            valid = ((ci < ATT_BLK) & (ci >= ki)) | ((ci >= ATT_BLK) & (ki - ci + ATT_BLK >= reach))
            rows = slice(i * ATT_BLK, (i + 1) * ATT_BLK)
            for h in range(ATT_HPG):
                sl = slice(h * ATT_HEAD_DIM, (h + 1) * ATT_HEAD_DIM)
                qcat, docat = pair(qn_ref, qc_ref, i, sl), pair(don_ref, doc_ref, i, sl)
                sc = _dot(k_ref[rows, sl], qcat, "nt") * ATT_SCALE
                p = jnp.exp(jnp.where(valid, sc, NEG_INF) - pair_row(ln_ref, lc_ref, i, h))
                dv_ref[rows, sl] = _dot(p, docat, "nn")
                dp = _dot(v_ref[rows, sl], docat, "nt")
                ds = p * (dp - pair_row(dn_ref, dc_ref, i, h)) * ATT_SCALE
                dk_ref[rows, sl] = _dot(ds, qcat, "nn")

    cur, _, nxt = _attn_specs(nb, ATT_T_BWD)
    stat = pl.BlockSpec((8, ATT_T_BWD * ATT_BLK), lambda b: (0, b))
    snxt = pl.BlockSpec((8, ATT_BLK), lambda b: (0, jnp.minimum((b + 1) * ATT_T_BWD, nb - 1)))
    return pl.pallas_call(
        body, name="attn_dkv_" + tag, grid=(nb // ATT_T_BWD,), in_specs=[cur, cur, cur, nxt, cur, nxt, stat, snxt, stat, snxt],
        out_specs=[cur, cur], out_shape=[jax.ShapeDtypeStruct((s, ATT_GROUPW), F32)] * 2,
        compiler_params=_cparams(1))(k, v, q, q, do, do, lse_t, lse_t, delta_t, delta_t)


def _xattn_probs(q, kh):
    sc = _dot(q, kh, "nt") * XATT_SCALE
    e = jnp.exp(sc - jnp.max(sc, axis=-1, keepdims=True))
    return e / jnp.sum(e, axis=-1, keepdims=True)


def _xattn_fwd(q, kv, tm=512):
    s = q.shape[0]
    tm = min(tm, s)

    def body(q_ref, kv_ref, o_ref):
        for h in range(XATT_HEADS):
            sl = slice(h * XATT_HEAD_DIM, (h + 1) * XATT_HEAD_DIM)
            vs = slice(D_MODEL + h * XATT_HEAD_DIM, D_MODEL + (h + 1) * XATT_HEAD_DIM)
            p = _xattn_probs(q_ref[:, sl], kv_ref[:, sl])
            o_ref[:, sl] = _dot(p, kv_ref[:, vs], "nn").astype(o_ref.dtype)

    return pl.pallas_call(
        body, name="xattn_fwd", grid=(s // tm,),
        in_specs=[pl.BlockSpec((tm, D_MODEL), lambda i: (i, 0)), pl.BlockSpec(kv.shape, lambda i: (0, 0))],
        out_specs=pl.BlockSpec((tm, D_MODEL), lambda i: (i, 0)),
        out_shape=jax.ShapeDtypeStruct((s, D_MODEL), MXU_DTYPE), compiler_params=_cparams(1))(q, kv)


def _xattn_bwd(q, kv, do, tm=1024):
    s = q.shape[0]
    tm = min(tm, s)

    def body(q_ref, kv_ref, do_ref, dq_ref, dkv_ref):
        first = pl.program_id(0) == 0

        @pl.when(first)
        def _():
            dkv_ref[...] = jnp.zeros_like(dkv_ref)

        for h in range(XATT_HEADS):
            sl = slice(h * XATT_HEAD_DIM, (h + 1) * XATT_HEAD_DIM)
            vs = slice(D_MODEL + h * XATT_HEAD_DIM, D_MODEL + (h + 1) * XATT_HEAD_DIM)
            p = _xattn_probs(q_ref[:, sl], kv_ref[:, sl])
            dkv_ref[:, vs] += _dot(p, do_ref[:, sl], "tn")
            dp = _dot(do_ref[:, sl], kv_ref[:, vs], "nt")
            ds = p * (dp - jnp.sum(dp * p, axis=-1, keepdims=True)) * XATT_SCALE
            dq_ref[:, sl] = _dot(ds, kv_ref[:, sl], "nn").astype(dq_ref.dtype)
            dkv_ref[:, sl] += _dot(ds, q_ref[:, sl], "tn")

    row = pl.BlockSpec((tm, D_MODEL), lambda i: (i, 0))
    whole = pl.BlockSpec(kv.shape, lambda i: (0, 0))
    return pl.pallas_call(
        body, name="xattn_bwd", grid=(s // tm,), in_specs=[row, whole, row], out_specs=[row, whole],
        out_shape=[jax.ShapeDtypeStruct((s, D_MODEL), MXU_DTYPE), jax.ShapeDtypeStruct(kv.shape, F32)],
        compiler_params=_cparams(1))(q, kv, do)


def _ln(x, g, b):
    mu = jnp.mean(x, axis=-1, keepdims=True)
    xc = x - mu
    var = jnp.mean(jnp.square(xc), axis=-1, keepdims=True)
    return xc * lax.rsqrt(var + LN_EPS) * g + b


def _res_ln(h, o, g, b):
    return _ln(DEEPNORM_ALPHA * h + o, g, b)


def _gate(gs, ga, z1, z2, batt):
    return jax.nn.sigmoid(gs) * (z1 * jax.nn.sigmoid(z2)) + jax.nn.sigmoid(ga) * batt


ROPE_TW = 2 * ATT_HEAD_DIM


def _rope_tables(pos, invf, m1, m2):
    ang = pos.astype(F32) * invf
    sin = jnp.sin(ang)
    return jnp.cos(ang), -sin * m1, sin * m2


def _widen(tab):
    return jnp.concatenate([tab] * (ATT_GROUPW // ROPE_TW), axis=1)


def _rope(t, cos, s_up, s_dn):
    w = t.shape[-1]
    return t * cos + pltpu.roll(t, w - ROT_DIM // 2, 1) * s_up + pltpu.roll(t, ROT_DIM // 2, 1) * s_dn


def _rope_t(dt, cos, s_up, s_dn):
    w = dt.shape[-1]
    return dt * cos + pltpu.roll(dt * s_up, ROT_DIM // 2, 1) + pltpu.roll(dt * s_dn, w - ROT_DIM // 2, 1)


def _rope_consts():
    inv_freq = ROPE_THETA ** (-jnp.arange(0, ROT_DIM, 2, dtype=F32) / ROT_DIM)
    d = np.arange(ROPE_TW) % ATT_HEAD_DIM
    invf = jnp.where(d < ROT_DIM, inv_freq[d % (ROT_DIM // 2)], 0.0).reshape(1, ROPE_TW).astype(F32)
    m1 = jnp.asarray((d < ROT_DIM // 2).astype(np.float32)).reshape(1, ROPE_TW)
    m2 = jnp.asarray(((d >= ROT_DIM // 2) & (d < ROT_DIM)).astype(np.float32)).reshape(1, ROPE_TW)
    return invf, m1, m2


def _head_sum_matrix():
    d = np.arange(ATT_GROUPW) // ATT_HEAD_DIM
    s = np.arange(ATT_STATW) // 128
    return jnp.asarray((d[:, None] == s[None, :]).astype(np.float32))


def _adamw(w, g, m, v):
    m = ADAM_B1 * m + (1.0 - ADAM_B1) * g
    v = ADAM_B2 * v + (1.0 - ADAM_B2) * jnp.square(g)
    m_hat = m / (1.0 - ADAM_B1 ** ADAM_STEP)
    v_hat = v / (1.0 - ADAM_B2 ** ADAM_STEP)
    delta = -ADAM_LR * (m_hat / (jnp.sqrt(v_hat) + ADAM_EPS) + ADAM_WD * w)
    return delta, m, v


def _local_step(x, mem, pos, target, sp, ex):
    s = x.shape[0]
    al = DEEPNORM_ALPHA
    mx = MXU_DTYPE

    h0, h0b = _rowwise("ln_in", lambda x, g, b: (lambda h: (h, h))(_ln(x, g, b)), [x],
                       [sp["ln_in_g"], sp["ln_in_b"]], [(D_MODEL, F32), (D_MODEL, mx)],
                       carry=ex.gather_carry(["w_in"]))
    proj = _mm("proj", h0b, ex.weight("w_in"), "nn", bias=sp["b_in"],
               carry=ex.gather_carry(["w_glu", "w_att_up", "w_mix_out", "w_xq"]))

    ldt = jnp.repeat(sp["ssm_log_dt"].reshape(SSM_GROUPS), SSM_STATE).reshape(N_STATE, 1)
    are, aim = sp["ssm_a_re"].reshape(N_STATE, 1), sp["ssm_a_im"].reshape(N_STATE, 1)
    bre, bim = sp["ssm_b_re"].reshape(N_STATE, SSM_GROUP), sp["ssm_b_im"].reshape(N_STATE, SSM_GROUP)
    abr, abi, bbr, bbi = _ssm_disc_fwd(ldt, are, aim, bre, bim)
    a_re, a_im = abr.reshape(1, N_STATE), abi.reshape(1, N_STATE)
    bexp = jnp.concatenate([_blockdiag_b(bbr), _blockdiag_b(bbi)], axis=2).astype(mx)
    cexp = jnp.concatenate([_blockdiag_c(sp["ssm_c_re"].reshape(SSM_GROUPS, SSM_GROUP, SSM_STATE)),
                            -_blockdiag_c(sp["ssm_c_im"].reshape(SSM_GROUPS, SSM_GROUP, SSM_STATE))],
                           axis=1).astype(mx)
    u_p = _time_perm(proj[:, :SSM_WIDTH])
    b12, c12 = _split_by_scan_block(bexp, 2), _split_by_scan_block(cexp, 1)
    h_re, h_im, y_p = _ssm_scan("ssm_scan_fwd", u_p, b12, c12, a_re, a_im, sp["ssm_d"], reverse=False,
                                carry=ex.gather_carry(["w_xkv", "w_xo", "w_ff1", "w_ff2"]))
    y = _time_unperm(y_p)
    ygb, = _rowwise("gelu", lambda y: jax.nn.gelu(y), [y], [], [(SSM_WIDTH, mx)])
    z = _mm("glu", ygb, ex.weight("w_glu"), "nn", bias=sp["b_glu"])

    invf, m1, m2 = _rope_consts()

    def rope_fwd(pos, q0, q1, q2, k0, k1, k2, v0, v1, v2, invf, m1, m2):
        narrow = _rope_tables(pos, invf, m1, m2)
        tabs = [_widen(t) for t in narrow]
        return tuple(_rope(t, *tabs) for t in (q0, q1, q2, k0, k1, k2)) + (v0, v1, v2) + tuple(narrow)

    qkv_cols = [(proj, ATT_GROUPW, 3 + i) for i in range(9)]
    qkv = _rowwise("rope", rope_fwd, [pos] + qkv_cols, [invf, m1, m2], [(ATT_GROUPW, mx)] * 9 + [(ROPE_TW, F32)] * 3)
    rope_tabs = qkv[9:]
    n_blocks = s // ATT_BLK
    groups = [(str(g), n_blocks // d, d) for g, d in enumerate(DILATIONS)]
    q_d = [_dilate(qkv[g], d) for g, d in enumerate(DILATIONS)]
    k_d = [_dilate(qkv[3 + g], d) for g, d in enumerate(DILATIONS)]
    v_d = [_dilate(qkv[6 + g], d) for g, d in enumerate(DILATIONS)]
    o_g, l_g = [], []
    for g, (tag, per_seq, d) in enumerate(groups):
        o, lse = _attn_fwd(tag, per_seq, q_d[g], k_d[g], v_d[g])
        o_g.append(_undilate(o, d))
        l_g.append(_undilate(lse, d))

    def merge(o0, o1, o2, l0, l1, l2):
        m = jnp.maximum(jnp.maximum(l0, l1), l2)
        e0, e1, e2 = jnp.exp(l0 - m), jnp.exp(l1 - m), jnp.exp(l2 - m)
        tot = e0 + e1 + e2

        def per_dim(e):
            w = e / tot
            return jnp.concatenate([w[:, h * 128:h * 128 + ATT_HEAD_DIM] for h in range(ATT_HPG)], axis=1)

        att = per_dim(e0) * o0 + per_dim(e1) * o1 + per_dim(e2) * o2
        lse = m + jnp.log(tot)
        return att, att, lse, _stat_rows(lse)

    att, attb, lse_tot, lse_tot_t = _rowwise("attn_merge", merge, o_g + l_g, [],
                                             [(ATT_GROUPW, F32), (ATT_GROUPW, mx), (ATT_STATW, F32)], touts=[(8, F32)])
    batt = _mm("att_up", attb, ex.weight("w_att_up"), "nn")

    gate_rows = [(proj, D_MODEL, 3), (proj, D_MODEL, 4), (z, D_MODEL, 0), (z, D_MODEL, 1), batt]
    mixedb, = _rowwise("gate", _gate, gate_rows, [], [(D_MODEL, mx)])
    o1 = _mm("mix_out", mixedb, ex.weight("w_mix_out"), "nn", bias=sp["b_mix_out"])
    h1, h1b = _rowwise("ln1", lambda h, o, g, b: (lambda r: (r, r))(_res_ln(h, o, g, b)), [h0, o1],
                       [sp["ln1_g"], sp["ln1_b"]], [(D_MODEL, F32), (D_MODEL, mx)])

    qx = _mm("xq", h1b, ex.weight("w_xq"), "nn", out_dtypes=(mx,))
    kvx = _mm("xkv", mem, ex.weight("w_xkv"), "nn", out_dtypes=(mx,))
    oxb = _xattn_fwd(qx, kvx)
    o2 = _mm("xo", oxb, ex.weight("w_xo"), "nn")
    h2, h2b = _rowwise("ln2", lambda h, o, g, b: (lambda r: (r, r))(_res_ln(h, o, g, b)), [h1, o2],
                       [sp["ln2_g"], sp["ln2_b"]], [(D_MODEL, F32), (D_MODEL, mx)])

    a_ff, fb = _mm("ff1", h2b, ex.weight("w_ff1"), "nn", bias=sp["b_ff1"],
                   epilogue=lambda r: (r, jnp.square(jnp.maximum(r, 0.0))), out_dtypes=(F32, mx))
    o3 = _mm("ff2", fb, ex.weight("w_ff2"), "nn", bias=sp["b_ff2"])

    def loss_bwd(h2, o3, tgt, g, b):
        def f(h2, o3, g, b):
            h3 = _res_ln(h2, o3, g, b)
            return 0.5 * jnp.sum(jnp.mean(jnp.square(h3 - tgt), axis=-1))

        loss, vjp = jax.vjp(f, h2, o3, g, b)
        _, dr, dg, db = vjp(jnp.ones((), F32))
        return dr, dr, dg, db, _colsum(dr), jnp.full((1, 128), loss, F32)

    dr3, dr3b, g_ln3_g, g_ln3_b, g_b_ff2, loss = _rowwise(
        "loss_ln3_bwd", loss_bwd, [h2, o3, target], [sp["ln3_g"], sp["ln3_b"]],
        [(D_MODEL, F32), (D_MODEL, mx)], [D_MODEL, D_MODEL, D_MODEL, 128])

    dab, da_sums = _mm("ff2_dx", dr3b, ex.weight("w_ff2"), "nt", extras=(a_ff,),
                       epilogue=lambda r, a: (r * (2.0 * jnp.maximum(a, 0.0)),), out_dtypes=(mx,), colsum=True)
    g_b_ff1 = jnp.sum(da_sums, axis=0)
    ex.grad("w_ff2", _mm("ff2_dw", fb, dr3b, "tn"))
    ex.grad("w_ff1", _mm("ff1_dw", h2b, dab, "tn", carry=ex.carry(swap=["w_ff2"])))
    dh2 = _mm("ff1_dx", dab, ex.weight("w_ff1"), "nt", extras=(dr3,), epilogue=lambda r, d: (r + al * d,),
              carry=ex.carry(swap=["w_ff1"], ici=["w_ff2"]))

    def ln_bwd(h, o, dout, g, b):
        _, vjp = jax.vjp(_res_ln, h, o, g, b)
        _, dr, dg, db = vjp(dout)
        return dr, dr, dg, db, _colsum(dr)

    dr2, dr2b, g_ln2_g, g_ln2_b, _ = _rowwise(
        "ln2_bwd", ln_bwd, [h1, o2, dh2], [sp["ln2_g"], sp["ln2_b"]],
        [(D_MODEL, F32), (D_MODEL, mx)], [D_MODEL, D_MODEL, D_MODEL])
    ex.grad("w_xo", _mm("xo_dw", oxb, dr2b, "tn"))
    doxb = _mm("xo_dx", dr2b, ex.weight("w_xo"), "nt", out_dtypes=(mx,), carry=ex.carry(swap=["w_xo"]))
    dqxb, dkvx = _xattn_bwd(qx, kvx, doxb)
    ex.grad("w_xq", _mm("xq_dw", h1b, dqxb, "tn"))
    dh1 = _mm("xq_dx", dqxb, ex.weight("w_xq"), "nt", extras=(dr2,), epilogue=lambda r, d: (r + al * d,),
              carry=ex.carry(swap=["w_xq"]))
    ex.grad("w_xkv", _mm("xkv_dw", mem, dkvx, "tn"))

    dr1, dr1b, g_ln1_g, g_ln1_b, g_b_mix = _rowwise(
        "ln1_bwd", ln_bwd, [h0, o1, dh1], [sp["ln1_g"], sp["ln1_b"]],
        [(D_MODEL, F32), (D_MODEL, mx)], [D_MODEL, D_MODEL, D_MODEL])
    ex.grad("w_mix_out", _mm("mix_dw", mixedb, dr1b, "tn", carry=ex.carry(swap=["w_xkv"])))
    dmixed = _mm("mix_dx", dr1b, ex.weight("w_mix_out"), "nt", carry=ex.carry(swap=["w_mix_out"]))

    def gate_bwd(gs, ga, z1, z2, batt, dm):
        _, vjp = jax.vjp(_gate, gs, ga, z1, z2, batt)
        dgs, dga, dz1, dz2, dbatt = vjp(dm)
        dz = jnp.concatenate([dz1, dz2], axis=-1)
        return dgs, dga, dz, dbatt, _colsum(dz)

    dgsb, dgab, dzb, dbattb, g_b_glu = _rowwise(
        "gate_bwd", gate_bwd, gate_rows + [dmixed], [],
        [(D_MODEL, mx), (D_MODEL, mx), (2 * D_MODEL, mx), (D_MODEL, mx)], [2 * D_MODEL])
    ex.grad("w_att_up", _mm("att_up_dw", attb, dbattb, "tn"))
    datt = _mm("att_up_dx", dbattb, ex.weight("w_att_up"), "nt", carry=ex.carry(swap=["w_att_up"]))

    def att_delta(datt, att, hs):
        dl = jnp.dot(datt * att, hs, precision=lax.Precision.HIGHEST, preferred_element_type=F32)
        return datt, dl, _stat_rows(dl)

    dattb, delta, delta_t = _rowwise("attn_delta", att_delta, [datt, att], [_head_sum_matrix()],
                                     [(ATT_GROUPW, mx), (ATT_STATW, F32)], touts=[(8, F32)])
    dq_g, dk_g, dv_g = [], [], []
    for g, (tag, per_seq, d) in enumerate(groups):
        do_d, lt_d, dl_d = _dilate(dattb, d), _dilate(lse_tot, d), _dilate(delta, d)
        dq_g.append(_undilate(_attn_dq(tag, per_seq, q_d[g], k_d[g], v_d[g], do_d, lt_d, dl_d), d))
        dk, dv = _attn_dkv(tag, per_seq, q_d[g], k_d[g], v_d[g], do_d, _dilate_rows(lse_tot_t, d), _dilate_rows(delta_t, d))
        dk_g.append(_undilate(dk, d))
        dv_g.append(_undilate(dv, d))
    dqkv = dq_g + dk_g + dv_g

    def rope_bwd(q0, q1, q2, k0, k1, k2, v0, v1, v2, cos, s_up, s_dn):
        tabs = [_widen(t) for t in (cos, s_up, s_dn)]
        return jnp.concatenate([_rope_t(t, *tabs) for t in (q0, q1, q2, k0, k1, k2)] + [v0, v1, v2], axis=-1)

    dqkvb, = _rowwise("rope_bwd", rope_bwd, dqkv + list(rope_tabs), [], [(9 * ATT_GROUPW, mx)])

    ex.grad("w_glu", _mm("glu_dw", ygb, dzb, "tn"))
    dyg = _mm("glu_dx", dzb, ex.weight("w_glu"), "nt", carry=ex.carry(swap=["w_glu"]))

    def gelu_bwd(y, dyg):
        _, vjp = jax.vjp(jax.nn.gelu, y)
        return vjp(dyg)[0]

    dy, = _rowwise("gelu_bwd", gelu_bwd, [y, dyg], [], [(SSM_WIDTH, F32)])
    dy_p = _time_perm(dy)
    s_re, s_im, du_p = _ssm_scan("ssm_scan_bwd", dy_p, c12, b12, a_re, a_im, sp["ssm_d"], reverse=True,
                                 carry=ex.carry(ici=["w_ff1", "w_xkv", "w_glu"]))
    g_bexp, g_cexp, d_abr, d_abi = _ssm_wgrads(u_p, dy_p, s_re, s_im, h_re, h_im)
    g_ssm_d, = _rowwise("ssm_dd", lambda a, b: (_colsum(a * b),), [dy_p, u_p], [], [], [SSM_WIDTH])
    g_ldt, g_are, g_aim, g_bre, g_bim = _ssm_disc_bwd(
        ldt, are, aim, bre, bim, d_abr.reshape(N_STATE, 1), d_abi.reshape(N_STATE, 1),
        _diag_of_b(g_bexp[:, :, :CH_N]), _diag_of_b(g_bexp[:, :, CH_N:]))
    g_c_re = _diag_of_c(g_cexp[:, :CH_N, :])
    g_c_im = -_diag_of_c(g_cexp[:, CH_N:, :])

    def assemble(du, dqkv, dgs, dga):
        row = jnp.concatenate([du.astype(mx), dqkv, dgs, dga], axis=-1)
        return row, _colsum(row)

    dprojb, g_b_in = _rowwise("in_assemble", assemble, [_time_unperm(du_p), dqkvb, dgsb, dgab], [],
                              [(IN_COLS, mx)], [IN_COLS])
    ex.grad("w_in", _mm("in_dw", h0b, dprojb, "tn",
                        carry=ex.carry(ici=["w_xo", "w_xq", "w_mix_out", "w_att_up"])))
    dh0 = _mm("in_dx", dprojb, ex.weight("w_in"), "nt", extras=(dr1,), epilogue=lambda r, d: (r + al * d,),
              carry=ex.carry(ici=["w_in"]))

    def ln_in_bwd(x, dout, g, b):
        _, vjp = jax.vjp(_ln, x, g, b)
        return vjp(dout)

    dx, g_ln_in_g, g_ln_in_b = _rowwise("ln_in_bwd", ln_in_bwd, [x, dh0], [sp["ln_in_g"], sp["ln_in_b"]],
                                        [(D_MODEL, F32)], [D_MODEL, D_MODEL], carry=ex.finish_carry())

    small = {"ln_in_g": g_ln_in_g, "ln_in_b": g_ln_in_b, "b_in": g_b_in, "ssm_log_dt": g_ldt, "ssm_a_re": g_are,
             "ssm_a_im": g_aim, "ssm_b_re": g_bre, "ssm_b_im": g_bim, "ssm_c_re": g_c_re, "ssm_c_im": g_c_im,
             "ssm_d": g_ssm_d, "b_glu": g_b_glu, "b_mix_out": g_b_mix, "ln1_g": g_ln1_g, "ln1_b": g_ln1_b,
             "ln2_g": g_ln2_g, "ln2_b": g_ln2_b, "b_ff1": g_b_ff1, "b_ff2": g_b_ff2, "ln3_g": g_ln3_g,
             "ln3_b": g_ln3_b}
    return loss, dx, small


def _piece_shape(k, n, axis):
    return (k // 2, n // 4) if axis == 1 else (k // 8, n)


def _aligned(v, m):
    return v if isinstance(v, int) else pl.multiple_of(v, m)


def _full_piece(ref, k, n, axis, chip, half):
    pr, pc = _piece_shape(k, n, axis)
    if axis == 1:
        return ref.at[pl.ds(_aligned(half * pr, 8), pr), pl.ds(_aligned(chip * pc, 128), pc)]
    return ref.at[pl.ds(_aligned(chip * (2 * pr) + half * pr, 8), pr), :]


def _shard_piece(ref, k, n, axis, half):
    pr, _ = _piece_shape(k, n, axis)
    return ref.at[pl.ds(_aligned(half * pr, 8), pr), :]


def _mesh_pos():
    x, y, c = lax.axis_index("x"), lax.axis_index("y"), lax.axis_index("c")
    other_chips = [(1 - x, y), (x, 1 - y), (1 - x, 1 - y)]
    return x, y, c, other_chips


def _remote(src, dst, send_sem, recv_sem, dev):
    return pltpu.make_async_remote_copy(src_ref=src, dst_ref=dst, send_sem=send_sem, recv_sem=recv_sem,
                                        device_id=dev, device_id_type=MESH)


def _placed(name, fn, n_steps, where, ins, out_sds, out_block, out_index):
    def body(w_ref, *refs):
        o_ref = refs[-1]
        o_ref[...] = fn(*[r[...] for r in refs[:-1]]).astype(o_ref.dtype)

    grid_spec = pltpu.PrefetchScalarGridSpec(
        num_scalar_prefetch=1, grid=(n_steps,), in_specs=[pl.BlockSpec(bs, idx) for _, bs, idx in ins],
        out_specs=pl.BlockSpec(out_block, out_index))
    return pl.pallas_call(body, name=name, grid_spec=grid_spec, out_shape=out_sds,
                          compiler_params=_cparams(1))(where, *[a for a, _, _ in ins])


def _gather_copies(widx):
    geo = [BIG[i][1:] for i in widx]

    def ici(full, wi, j, chip, send_sems, recv_sems, c, dev):
        k, n, ax = geo[wi]
        piece = _full_piece(full[wi], k, n, ax, chip, c)
        return _remote(piece, piece, send_sems.at[wi * 6 + j], recv_sems.at[wi * 6 + j], dev)

    def d2d(full, wi, j, chip, half, send_sems, recv_sems, sib):
        k, n, ax = geo[wi]
        piece = _full_piece(full[wi], k, n, ax, chip, half)
        return _remote(piece, piece, send_sems.at[wi * 6 + 3 + j], recv_sems.at[wi * 6 + 3 + j], sib)

    def start(_, full, send_sems, recv_sems):
        x, y, c, chips = _mesh_pos()
        for wi in range(len(geo)):
            for j, (qx, qy) in enumerate(chips):
                ici(full, wi, j, 2 * x + y, send_sems, recv_sems, c, (qx, qy, c)).start()

    def finish(_, full, send_sems, recv_sems):
        x, y, c, chips = _mesh_pos()
        sib = (x, y, 1 - c)
        for wi in range(len(geo)):
            for j, (qx, qy) in enumerate(chips):
                ici(full, wi, j, 2 * qx + qy, send_sems, recv_sems, c, (qx, qy, c)).wait_recv()
                d2d(full, wi, j, 2 * qx + qy, c, send_sems, recv_sems, sib).start()
        for wi in range(len(geo)):
            for j, (qx, qy) in enumerate(chips):
                d2d(full, wi, j, 2 * qx + qy, 1 - c, send_sems, recv_sems, sib).wait_recv()
        for wi in range(len(geo)):
            for j, (qx, qy) in enumerate(chips):
                ici(full, wi, j, 2 * x + y, send_sems, recv_sems, c, (qx, qy, c)).wait_send()
                d2d(full, wi, j, 2 * qx + qy, c, send_sems, recv_sems, sib).wait_send()

    return start, finish, 6 * len(geo)


def _swap_copies(widx):
    geo = [BIG[i][1:] for i in widx]

    def copies(g, got, send_sems, recv_sems, base):
        x, y, c, _ = _mesh_pos()
        return [_remote(_full_piece(g[wi], k, n, ax, q, 1 - c), got[wi].at[q], send_sems.at[base + wi * 4 + q],
                        recv_sems.at[base + wi * 4 + q], (x, y, 1 - c))
                for wi, (k, n, ax) in enumerate(geo) for q in range(4)]

    def start(g, got, send_sems, recv_sems, base=0):
        for cp in copies(g, got, send_sems, recv_sems, base):
            cp.start()

    def finish(g, got, send_sems, recv_sems, base=0):
        for cp in copies(g, got, send_sems, recv_sems, base):
            cp.wait()

    return start, finish, 4 * len(geo)


def _swap_shapes(widx):
    return [jax.ShapeDtypeStruct((4,) + _piece_shape(*BIG[i][1:]), F32) for i in widx]


def _reduce_swap_halves(tag, grads, widx):
    nw = len(widx)
    start, finish, n_sems = _swap_copies(widx)

    def body(*refs):
        start(refs[:nw], refs[nw:2 * nw], *refs[2 * nw:])
        finish(refs[:nw], refs[nw:2 * nw], *refs[2 * nw:])

    return pl.pallas_call(
        body, name="reduce_swap_halves_" + tag, in_specs=[HBM_SPEC] * nw, out_specs=[HBM_SPEC] * nw,
        out_shape=_swap_shapes(widx),
        scratch_shapes=[pltpu.SemaphoreType.DMA((n_sems,)), pltpu.SemaphoreType.DMA((n_sems,))])(*grads)


def _owner_copies(nw):
    def copies(p, out, send_sems, recv_sems, base):
        x, y, c, chips = _mesh_pos()
        return [_remote(p[wi].at[2 * qx + qy], out[wi].at[j], send_sems.at[base + wi * 3 + j],
                        recv_sems.at[base + wi * 3 + j], (qx, qy, c))
                for wi in range(nw) for j, (qx, qy) in enumerate(chips)]

    def start(p, out, send_sems, recv_sems, base=0):
        for cp in copies(p, out, send_sems, recv_sems, base):
            cp.start()

    def finish(p, out, send_sems, recv_sems, base=0):
        for cp in copies(p, out, send_sems, recv_sems, base):
            cp.wait()

    return start, finish, 3 * nw


def _join_carries(a, b):
    if a is None or b is None:
        return a if b is None else b
    n_i, n_o = len(a.ins), len(a.outs)
    outs = list(a.outs) + [o + n_i if isinstance(o, int) else o for o in b.outs]

    def start(c_in, c_out, send_sems, recv_sems):
        a.start(c_in[:n_i], c_out[:n_o], send_sems, recv_sems)
        b.start(c_in[n_i:], c_out[n_o:], send_sems, recv_sems, base=a.n_sems)

    def finish(c_in, c_out, send_sems, recv_sems):
        a.finish(c_in[:n_i], c_out[:n_o], send_sems, recv_sems)
        b.finish(c_in[n_i:], c_out[n_o:], send_sems, recv_sems, base=a.n_sems)

    def done(res):
        a.done(res[:n_o])
        b.done(res[n_o:])

    return _Carry(a.ins + b.ins, outs, a.n_sems + b.n_sems, start, finish, done)


def _share_copies():
    def copy(out, wi, half, send_sems, recv_sems, sib):
        _, k, n, ax = BIG[wi]
        piece = _shard_piece(out[wi], k, n, ax, half)
        return _remote(piece, piece, send_sems.at[wi], recv_sems.at[wi], sib)

    def start(_, out, send_sems, recv_sems):
        x, y, c, _ = _mesh_pos()
        for wi in range(len(BIG)):
            copy(out, wi, c, send_sems, recv_sems, (x, y, 1 - c)).start()

    def finish(_, out, send_sems, recv_sems):
        x, y, c, _ = _mesh_pos()
        for wi in range(len(BIG)):
            copy(out, wi, 1 - c, send_sems, recv_sems, (x, y, 1 - c)).wait_recv()
            copy(out, wi, c, send_sems, recv_sems, (x, y, 1 - c)).wait_send()

    return start, finish, len(BIG)


def _allreduce_small(v):
    r = v.shape[0]
    rh = r // 2
    assert rh % 8 == 0

    def body(v_ref, o_ref, sib_buf, chip_buf, send_sems, recv_sems):
        x, y, c, chips = _mesh_pos()
        me = 2 * x + y
        sib = (x, y, 1 - c)
        mine = pl.ds(pl.multiple_of(c * rh, 8), rh)
        other = pl.ds(pl.multiple_of((1 - c) * rh, 8), rh)
        swap = _remote(v_ref.at[other], sib_buf, send_sems.at[0], recv_sems.at[0], sib)
        swap.start()
        swap.wait()
        chip_buf[me] = v_ref[mine, :] + sib_buf[...]
        cps = []
        for j, (qx, qy) in enumerate(chips):
            cp = _remote(chip_buf.at[me], chip_buf.at[me], send_sems.at[1 + j], recv_sems.at[1 + j], (qx, qy, c))
            cp.start()
            cps.append(cp)
        for j, (qx, qy) in enumerate(chips):
            slot = chip_buf.at[2 * qx + qy]
            _remote(slot, slot, send_sems.at[1 + j], recv_sems.at[1 + j], (qx, qy, c)).wait_recv()
        for cp in cps:
            cp.wait_send()
        o_ref[mine, :] = ((chip_buf[0] + chip_buf[1]) + chip_buf[2]) + chip_buf[3]
        back = _remote(o_ref.at[mine], o_ref.at[mine], send_sems.at[4], recv_sems.at[4], sib)
        back.start()
        _remote(o_ref.at[other], o_ref.at[other], send_sems.at[4], recv_sems.at[4], sib).wait_recv()
        back.wait_send()

    return pl.pallas_call(
        body, name="allreduce_small", in_specs=[VMEM_SPEC], out_specs=VMEM_SPEC,
        out_shape=jax.ShapeDtypeStruct((r, 128), F32),
        scratch_shapes=[pltpu.VMEM((rh, 128), F32), pltpu.VMEM((4, rh, 128), F32),
                        pltpu.SemaphoreType.DMA((5,)), pltpu.SemaphoreType.DMA((5,))],
        compiler_params=pltpu.CompilerParams(vmem_limit_bytes=VMEM_LIMIT))(v)


def _as2d(a):
    a = a.reshape((-1, a.shape[-1])) if a.ndim > 1 else a.reshape(1, -1)
    return a


def _adamw_small(quads):
    n = len(quads)

    def body(*refs):
        for i in range(n):
            w, g, m, v = (r[...] for r in refs[4 * i:4 * i + 4])
            for ref, val in zip(refs[4 * n + 3 * i:4 * n + 3 * i + 3], _adamw(w, g, m, v)):
                ref[...] = val

    return pl.pallas_call(
        body, name="adamw_small", in_specs=[VMEM_SPEC] * (4 * n), out_specs=[VMEM_SPEC] * (3 * n),
        out_shape=[jax.ShapeDtypeStruct(q[0].shape, F32) for q in quads for _ in range(3)],
        compiler_params=pltpu.CompilerParams(vmem_limit_bytes=VMEM_LIMIT))(*[a for q in quads for a in q])


def _where():
    return jnp.stack([2 * lax.axis_index("x") + lax.axis_index("y"), lax.axis_index("c")]).astype(jnp.int32)


_BIG_INDEX = {name: i for i, (name, _, _, _) in enumerate(BIG)}


class _Exchange:
    def __init__(self, inputs, where):
        self.inputs, self.where = inputs, where
        self.full, self.ready = {}, set()
        self.raw, self.got, self.parts, self.landed, self.geom = {}, {}, {}, {}, {}
        for name, k, n, ax in BIG:
            w2 = inputs[name][0]
            rs, cs = w2.shape
            tm = _tile(rs, 512)
            steps = rs // tm
            if ax == 1:
                blk, idx = (tm, cs), lambda i, w: (i, w[0])
            else:
                blk, idx = (tm, n), functools.partial(lambda i, w, steps: (w[0] * steps + i, 0), steps=steps)
            self.full[name] = _placed("cast_" + name, lambda w: w, steps, where, [(w2, (tm, cs), lambda i, w: (i, 0))],
                                      jax.ShapeDtypeStruct((k, n), MXU_DTYPE), blk, idx)

    def _gathered(self, names, outs):
        for name, o in zip(names, outs):
            self.full[name] = o
            self.ready.add(name)

    def gather_carry(self, names):
        start, finish, n_sems = _gather_copies([_BIG_INDEX[n] for n in names])
        return _Carry([self.full[n] for n in names], list(range(len(names))), n_sems, start, finish,
                      functools.partial(self._gathered, names))

    def weight(self, name):
        assert name in self.ready, name
        return self.full[name]

    def grad(self, name, g):
        self.raw[name] = g

    def _swapped(self, names, outs):
        for name, o in zip(names, outs):
            self.got[name] = o

    def _pair_sum(self, name):
        i = _BIG_INDEX[name]
        _, k, n, ax = BIG[i]
        g = self.raw[name]
        if name not in self.got:
            self._swapped([name], _reduce_swap_halves(name, [g], [i]))
        got = self.got[name]
        pr, pc = _piece_shape(k, n, ax)
        tm = _tile(pr, 512)
        spp = pr // tm
        self.geom[name] = (pr, pc, tm, spp)
        if ax == 1:
            g_idx = functools.partial(lambda i, w, spp: (w[1] * spp + i % spp, i // spp), spp=spp)
        else:
            g_idx = functools.partial(lambda i, w, spp: ((i // spp) * 2 * spp + w[1] * spp + i % spp, 0), spp=spp)
        self.parts[name] = _placed(
            "pair_sum_" + name, lambda a, b: a + b, 4 * spp, self.where,
            [(g, (tm, pc), g_idx), (got.reshape(4 * pr, pc), (tm, pc), lambda i, w: (i, 0))],
            jax.ShapeDtypeStruct((4 * pr, pc), BF16), (tm, pc), lambda i, w: (i, 0)).reshape(4, pr, pc)

    def _landed(self, names, outs):
        for name, o in zip(names, outs):
            self.landed[name] = o

    def carry(self, swap=(), ici=()):
        first = second = None
        if swap:
            widx = [_BIG_INDEX[n] for n in swap]
            start, finish, n_sems = _swap_copies(widx)
            first = _Carry([self.raw[n] for n in swap], _swap_shapes(widx), n_sems, start, finish,
                           functools.partial(self._swapped, list(swap)))
        if ici:
            for n in ici:
                self._pair_sum(n)
            start, finish, n_sems = _owner_copies(len(ici))
            parts = [self.parts[n] for n in ici]
            outs = [jax.ShapeDtypeStruct((3,) + p.shape[1:], p.dtype) for p in parts]
            second = _Carry(parts, outs, n_sems, start, finish, functools.partial(self._landed, list(ici)))
        return _join_carries(first, second)

    def _shared(self, outs):
        self.shards = dict(zip([b[0] for b in BIG], outs))

    def finish_carry(self):
        halves = []
        for name, _, _, _ in BIG:
            pr, pc, tm, spp = self.geom[name]
            ins = [(self.parts[name], (None, tm, pc), lambda i, w: (w[0], i, 0))]
            ins += [(self.landed[name], (None, tm, pc), functools.partial(lambda i, w, j: (j, i, 0), j=j))
                    for j in range(3)]
            halves.append(_placed("chip_sum_" + name,
                                  lambda a, b, c, d: ((a.astype(F32) + b.astype(F32)) + c.astype(F32)) + d.astype(F32),
                                  spp, self.where, ins, jax.ShapeDtypeStruct(self.inputs[name].shape[1:], F32), (tm, pc),
                                  functools.partial(lambda i, w, spp: (w[1] * spp + i, 0), spp=spp)))
        start, finish, n_sems = _share_copies()
        return _Carry(halves, list(range(len(halves))), n_sems, start, finish, self._shared)


def _step(inputs):
    x, mem, positions, target = inputs["x"][0], inputs["mem"][0], inputs["positions"], inputs["loss_target"][0]
    pos = positions.reshape(-1, 1)
    ex = _Exchange(inputs, _where())
    sp = {name: _as2d(inputs[name]) for name in SMALL}
    memb, = _rowwise("cast_mem", lambda m: (m,), [mem], [], [(D_MODEL, MXU_DTYPE)])

    loss, dx, gsmall = _local_step(x, memb, pos, target, sp, ex)
    gshard = ex.shards

    out = {}
    for name, _, _, _ in BIG:
        w2, m2, v2 = inputs[name][0], inputs["m_" + name][0], inputs["v_" + name][0]
        n = w2.shape[1]
        d, nm, nv = _rowwise("adamw_" + name, _adamw, [w2, gshard[name], m2, v2], [], [(n, F32)] * 3, tm=_tile(w2.shape[0], 512))
        lead = inputs[name].shape
        out[name] = (gshard[name].reshape(lead), d.reshape(lead), nm.reshape(lead), nv.reshape(lead))

    def tiles(a):
        flat = a.reshape(-1)
        n = -(-flat.shape[0] // 1024) * 1024
        return jnp.pad(flat, (0, n - flat.shape[0])).reshape(n // 128, 128)

    pieces = [tiles(loss[:, :1])] + [tiles(gsmall[name]) for name in SMALL]
    if sum(p.shape[0] for p in pieces) % 16:
        pieces.append(jnp.zeros((8, 128), F32))
    red = _allreduce_small(jnp.concatenate(pieces, axis=0))
    loss_total = red[0, 0]
    grads, off = {}, pieces[0].shape[0]
    for name, p in zip(SMALL, pieces[1:]):
        shp = _as2d(inputs[name]).shape
        grads[name] = red[off:off + p.shape[0]].reshape(-1)[:shp[0] * shp[1]].reshape(shp)
        off += p.shape[0]
    upd = _adamw_small([(_as2d(inputs[n]), grads[n], _as2d(inputs["m_" + n]), _as2d(inputs["v_" + n])) for n in SMALL])
    for i, name in enumerate(SMALL):
        shp = inputs[name].shape
        out[name] = (grads[name].reshape(shp),) + tuple(t.reshape(shp) for t in upd[3 * i:3 * i + 3])
    return loss_total, dx.reshape(inputs["x"].shape), out


_ARG_NAMES = (("x", "mem", "positions") + WEIGHT_ORDER + ("loss_target",) + tuple("m_" + n for n in WEIGHT_ORDER)
              + tuple("v_" + n for n in WEIGHT_ORDER))


def kernel(x, mem, positions, ln_in_g, ln_in_b, w_in, b_in, ssm_log_dt, ssm_a_re, ssm_a_im, ssm_b_re, ssm_b_im, ssm_c_re, ssm_c_im, ssm_d, w_glu, b_glu, w_att_up, w_mix_out, b_mix_out, ln1_g, ln1_b, w_xq, w_xkv, w_xo, ln2_g, ln2_b, w_ff1, b_ff1, w_ff2, b_ff2, ln3_g, ln3_b, loss_target, m_ln_in_g, m_ln_in_b, m_w_in, m_b_in, m_ssm_log_dt, m_ssm_a_re, m_ssm_a_im, m_ssm_b_re, m_ssm_b_im, m_ssm_c_re, m_ssm_c_im, m_ssm_d, m_w_glu, m_b_glu, m_w_att_up, m_w_mix_out, m_b_mix_out, m_ln1_g, m_ln1_b, m_w_xq, m_w_xkv, m_w_xo, m_ln2_g, m_ln2_b, m_w_ff1, m_b_ff1, m_w_ff2, m_b_ff2, m_ln3_g, m_ln3_b, v_ln_in_g, v_ln_in_b, v_w_in, v_b_in, v_ssm_log_dt, v_ssm_a_re, v_ssm_a_im, v_ssm_b_re, v_ssm_b_im, v_ssm_c_re, v_ssm_c_im, v_ssm_d, v_w_glu, v_b_glu, v_w_att_up, v_w_mix_out, v_b_mix_out, v_ln1_g, v_ln1_b, v_w_xq, v_w_xkv, v_w_xo, v_ln2_g, v_ln2_b, v_w_ff1, v_b_ff1, v_w_ff2, v_b_ff2, v_ln3_g, v_ln3_b):
    args = (x, mem, positions, ln_in_g, ln_in_b, w_in, b_in, ssm_log_dt, ssm_a_re, ssm_a_im, ssm_b_re, ssm_b_im, ssm_c_re, ssm_c_im, ssm_d, w_glu, b_glu, w_att_up, w_mix_out, b_mix_out, ln1_g, ln1_b, w_xq, w_xkv, w_xo, ln2_g, ln2_b, w_ff1, b_ff1, w_ff2, b_ff2, ln3_g, ln3_b, loss_target, m_ln_in_g, m_ln_in_b, m_w_in, m_b_in, m_ssm_log_dt, m_ssm_a_re, m_ssm_a_im, m_ssm_b_re, m_ssm_b_im, m_ssm_c_re, m_ssm_c_im, m_ssm_d, m_w_glu, m_b_glu, m_w_att_up, m_w_mix_out, m_b_mix_out, m_ln1_g, m_ln1_b, m_w_xq, m_w_xkv, m_w_xo, m_ln2_g, m_ln2_b, m_w_ff1, m_b_ff1, m_w_ff2, m_b_ff2, m_ln3_g, m_ln3_b, v_ln_in_g, v_ln_in_b, v_w_in, v_b_in, v_ssm_log_dt, v_ssm_a_re, v_ssm_a_im, v_ssm_b_re, v_ssm_b_im, v_ssm_c_re, v_ssm_c_im, v_ssm_d, v_w_glu, v_b_glu, v_w_att_up, v_w_mix_out, v_b_mix_out, v_ln1_g, v_ln1_b, v_w_xq, v_w_xkv, v_w_xo, v_ln2_g, v_ln2_b, v_w_ff1, v_b_ff1, v_w_ff2, v_b_ff2, v_ln3_g, v_ln3_b)
    assert len(args) == len(_ARG_NAMES)
    inputs = dict(zip(_ARG_NAMES, args))
    loss, dx, out = _step(inputs)
    res = [loss, dx]
    for k in range(4):
        res += [out[name][k] for name in WEIGHT_ORDER]
    return tuple(res)
```

```python
import functools
import math

import numpy as np
import jax
import jax.numpy as jnp
from jax import lax
from jax.experimental import pallas as pl
from jax.experimental.pallas import tpu as pltpu

F32 = jnp.float32
BF16 = jnp.bfloat16
MXU_DTYPE = jnp.bfloat16

D_MODEL = 1024
SSM_GROUP = 16
SSM_WIDTH = 768
SSM_GROUPS = 48
SSM_STATE = 64
N_STATE = SSM_GROUPS * SSM_STATE
SSM_CHUNKS = 6
CH_W = 128
CH_N = 512
ATT_HEAD_DIM = 64
ATT_HPG = 4
ATT_GROUPW = ATT_HPG * ATT_HEAD_DIM
DILATIONS = (1, 4, 16)
ATT_BLK = 128
ATT_SCALE = ATT_HEAD_DIM ** -0.5
ROT_DIM = 16
ROPE_THETA = 500000.0
XATT_HEADS = 4
XATT_HEAD_DIM = 256
XATT_SCALE = XATT_HEAD_DIM ** -0.5
D_FF = 4096
IN_COLS = 5120
DEEPNORM_ALPHA = 2.0 ** 0.25
LN_EPS = 1e-5
NEG_INF = -1e30
ADAM_LR = 0.001
ADAM_B1 = 0.9
ADAM_B2 = 0.999
ADAM_EPS = 1e-08
ADAM_WD = 0.01
ADAM_STEP = 10

N_SEG = 32
VMEM_LIMIT = 56 * 1024 * 1024
MESH = pl.DeviceIdType.MESH
HBM_SPEC = pl.BlockSpec(memory_space=pltpu.HBM)
VMEM_SPEC = pl.BlockSpec(memory_space=pltpu.VMEM)

BIG = (("w_in", 1024, 5120, 1), ("w_glu", 768, 2048, 1), ("w_att_up", 256, 1024, 1),
       ("w_mix_out", 1024, 1024, 0), ("w_xq", 1024, 1024, 0), ("w_xkv", 1024, 2048, 1),
       ("w_xo", 1024, 1024, 0), ("w_ff1", 1024, 4096, 1), ("w_ff2", 4096, 1024, 0))
SMALL = ("ln_in_g", "ln_in_b", "b_in", "ssm_log_dt", "ssm_a_re", "ssm_a_im", "ssm_b_re", "ssm_b_im",
         "ssm_c_re", "ssm_c_im", "ssm_d", "b_glu", "b_mix_out", "ln1_g", "ln1_b", "ln2_g", "ln2_b",
         "b_ff1", "b_ff2", "ln3_g", "ln3_b")
WEIGHT_ORDER = ("ln_in_g", "ln_in_b", "w_in", "b_in", "ssm_log_dt", "ssm_a_re", "ssm_a_im", "ssm_b_re",
                "ssm_b_im", "ssm_c_re", "ssm_c_im", "ssm_d", "w_glu", "b_glu", "w_att_up", "w_mix_out",
                "b_mix_out", "ln1_g", "ln1_b", "w_xq", "w_xkv", "w_xo", "ln2_g", "ln2_b", "w_ff1", "b_ff1",
                "w_ff2", "b_ff2", "ln3_g", "ln3_b")


def _cparams(n_axes):
    return pltpu.CompilerParams(dimension_semantics=("arbitrary",) * n_axes, vmem_limit_bytes=VMEM_LIMIT)


class _Carry:
    def __init__(self, ins, outs, n_sems, start, finish, done):
        self.ins, self.outs, self.n_sems, self.start, self.finish, self.done = ins, outs, n_sems, start, finish, done


def _call(name, body, grid, in_specs, out_specs, out_shape, args, scratch_shapes=(), carry=None):
    in_specs, out_specs, out_shape = list(in_specs), list(out_specs), list(out_shape)
    params = _cparams(len(grid))
    if carry is None:
        return pl.pallas_call(body, name=name, grid=grid, in_specs=in_specs, out_specs=out_specs, out_shape=out_shape,
                              scratch_shapes=list(scratch_shapes), compiler_params=params)(*args)
    n_in, n_out, n_ci, n_co = len(in_specs), len(out_specs), len(carry.ins), len(carry.outs)
    n_scr = len(scratch_shapes)

    def wrapped(*refs):
        ins, c_in = refs[:n_in], refs[n_in:n_in + n_ci]
        outs, c_out = refs[n_in + n_ci:n_in + n_ci + n_out], refs[n_in + n_ci + n_out:n_in + n_ci + n_out + n_co]
        scratch = refs[n_in + n_ci + n_out + n_co:n_in + n_ci + n_out + n_co + n_scr]
        send_sems, recv_sems = refs[-2:]
        ids = [pl.program_id(a) for a in range(len(grid))]
        first = functools.reduce(jnp.logical_and, [i == 0 for i in ids])
        last = functools.reduce(jnp.logical_and, [i == g - 1 for i, g in zip(ids, grid)])

        @pl.when(first)
        def _():
            carry.start(c_in, c_out, send_sems, recv_sems)

        body(*ins, *outs, *scratch)

        @pl.when(last)
        def _():
            carry.finish(c_in, c_out, send_sems, recv_sems)

    c_shapes = [jax.ShapeDtypeStruct(carry.ins[o].shape, carry.ins[o].dtype) if isinstance(o, int) else o
                for o in carry.outs]
    aliases = {n_in + o: n_out + i for i, o in enumerate(carry.outs) if isinstance(o, int)}
    res = pl.pallas_call(
        wrapped, name=name, grid=grid, in_specs=in_specs + [HBM_SPEC] * n_ci, out_specs=out_specs + [HBM_SPEC] * n_co,
        out_shape=out_shape + c_shapes, input_output_aliases=aliases,
        scratch_shapes=list(scratch_shapes) + [pltpu.SemaphoreType.DMA((carry.n_sems,))] * 2,
        compiler_params=params)(*args, *carry.ins)
    carry.done(res[n_out:])
    return res[:n_out]


def _rowwise(name, fn, rows, consts, outs, reds=(), tm=512, touts=(), carry=None):
    n_rows = (rows[0][0] if isinstance(rows[0], tuple) else rows[0]).shape[-2]
    tm = min(tm, n_rows)
    assert n_rows % tm == 0, (name, n_rows, tm)
    specs, args = [], []
    for r in rows:
        if isinstance(r, tuple) and len(r) == 3:
            arr, width, cb = r
            specs.append(pl.BlockSpec((tm, width), functools.partial(lambda i, cb: (i, cb), cb=cb)))
        elif isinstance(r, tuple):
            arr, slot = r
            specs.append(pl.BlockSpec((None, tm, arr.shape[2]), functools.partial(lambda i, s: (s, i, 0), s=slot)))
        else:
            arr = r
            specs.append(pl.BlockSpec((tm, arr.shape[1]), lambda i: (i, 0)))
        args.append(arr)
        assert arr.shape[-2] == n_rows, (name, arr.shape, n_rows)
    for cst in consts:
        specs.append(pl.BlockSpec(cst.shape, lambda i: (0, 0)))
        args.append(cst)
    n_r, n_c, n_o, n_d = len(rows), len(consts), len(outs) + len(touts), len(reds)
    out_shape = [jax.ShapeDtypeStruct((n_rows, c), dt) for c, dt in outs]
    out_specs = [pl.BlockSpec((tm, c), lambda i: (i, 0)) for c, _ in outs]
    out_shape += [jax.ShapeDtypeStruct((r, n_rows), dt) for r, dt in touts]
    out_specs += [pl.BlockSpec((r, tm), lambda i: (0, i)) for r, _ in touts]
    out_shape += [jax.ShapeDtypeStruct((1, c), F32) for c in reds]
    out_specs += [pl.BlockSpec((1, c), lambda i: (0, 0)) for c in reds]

    def body(*refs):
        ins = [r[...] for r in refs[:n_r + n_c]]
        o_refs = refs[n_r + n_c:n_r + n_c + n_o]
        d_refs = refs[n_r + n_c + n_o:]
        res = fn(*ins)
        res = res if isinstance(res, (tuple, list)) else (res,)
        assert len(res) == n_o + n_d, (name, len(res))
        for ref, val in zip(o_refs, res[:n_o]):
            ref[...] = val.astype(ref.dtype)
        first = pl.program_id(0) == 0
        for ref, val in zip(d_refs, res[n_o:]):
            @pl.when(first)
            def _(ref=ref, val=val):
                ref[...] = val

            @pl.when(jnp.logical_not(first))
            def _(ref=ref, val=val):
                ref[...] += val

    return _call(name, body, (n_rows // tm,), specs, out_specs, out_shape, args, carry=carry)


def _colsum(v):
    return jnp.sum(v.astype(F32), axis=0, keepdims=True)


_DIMS = {"nn": (((1,), (0,)), ((), ())), "nt": (((1,), (1,)), ((), ())), "tn": (((0,), (0,)), ((), ()))}


def _tile(dim, want):
    if dim <= want:
        return dim
    return max(t for t in range(128, want + 1, 128) if dim % t == 0)


def _dot(a, b, mode):
    return lax.dot_general(a.astype(MXU_DTYPE), b.astype(MXU_DTYPE), _DIMS[mode], preferred_element_type=F32)


def _mm(name, a, b, mode, *, bias=None, extras=(), epilogue=None, out_dtypes=(F32,), tm=1024, tn=1024, tk=1024,
        carry=None, colsum=False):
    if mode == "nn":
        (m, k), (_, n) = a.shape, b.shape
    elif mode == "nt":
        (m, k), (n, _) = a.shape, b.shape
    else:
        (k, m), (_, n) = a.shape, b.shape
    if k > tk:
        tk = 5 * tk
    tn = _tile(n, tn)
    tk = _tile(k, tk)
    nk = k // tk

    def vmem_bytes(rows):
        blocks = rows * tk * a.dtype.itemsize + tk * tn * b.dtype.itemsize
        blocks += sum(rows * tn * e.dtype.itemsize for e in extras)
        blocks += sum(rows * tn * jnp.dtype(dt).itemsize for dt in out_dtypes)
        return 2 * blocks + (rows * tn * 4 if nk > 1 else 0)

    tm = _tile(m, tm if mode == "tn" else 2 * tm)
    while vmem_bytes(tm) > 3 * VMEM_LIMIT // 4 and tm % 256 == 0:
        tm //= 2
    while nk == 1 and k > 1024 and (m // tm) * (n // tn) < 4 and tm % 256 == 0:
        tm //= 2
    assert m % tm == 0 and n % tn == 0 and k % tk == 0, (name, m, n, k)
    a_spec = {"nn": pl.BlockSpec((tm, tk), lambda i, j, kk: (i, kk)),
              "nt": pl.BlockSpec((tm, tk), lambda i, j, kk: (i, kk)),
              "tn": pl.BlockSpec((tk, tm), lambda i, j, kk: (kk, i))}[mode]
    b_spec = {"nn": pl.BlockSpec((tk, tn), lambda i, j, kk: (kk, j)),
              "nt": pl.BlockSpec((tn, tk), lambda i, j, kk: (j, kk)),
              "tn": pl.BlockSpec((tk, tn), lambda i, j, kk: (kk, j))}[mode]
    specs, args = [a_spec, b_spec], [a, b]
    if bias is not None:
        specs.append(pl.BlockSpec((1, tn), lambda i, j, kk: (0, j)))
        args.append(bias)
    for e in extras:
        specs.append(pl.BlockSpec((tm, tn), lambda i, j, kk: (i, j)))
        args.append(e)
    n_e, n_o = len(extras), len(out_dtypes)
    has_bias = bias is not None

    def body(*refs):
        a_ref, b_ref = refs[0], refs[1]
        pos = 2
        bias_ref = refs[pos] if has_bias else None
        pos += int(has_bias)
        e_refs = refs[pos:pos + n_e]
        o_refs = refs[pos + n_e:pos + n_e + n_o]
        sum_ref = refs[pos + n_e + n_o] if colsum else None
        acc_ref = refs[pos + n_e + n_o + int(colsum)] if nk > 1 else None
        part = _dot(a_ref[...], b_ref[...], mode)

        def finish(r):
            if has_bias:
                r = r + bias_ref[...]
            res = epilogue(r, *[e[...] for e in e_refs]) if epilogue is not None else (r,)
            for ref, val in zip(o_refs, res):
                ref[...] = val.astype(ref.dtype)
            if colsum:
                sum_ref[...] = _colsum(res[0])

        if nk == 1:
            finish(part)
        else:
            kk = pl.program_id(2)

            @pl.when(kk == 0)
            def _():
                acc_ref[...] = part

            @pl.when(kk > 0)
            def _():
                acc_ref[...] += part

            @pl.when(kk == nk - 1)
            def _():
                finish(acc_ref[...])

    out_specs = [pl.BlockSpec((tm, tn), lambda i, j, kk: (i, j)) for _ in out_dtypes]
    out_shape = [jax.ShapeDtypeStruct((m, n), dt) for dt in out_dtypes]
    if colsum:
        out_specs.append(pl.BlockSpec((None, 1, tn), lambda i, j, kk: (i, 0, j)))
        out_shape.append(jax.ShapeDtypeStruct((m // tm, 1, n), F32))
    res = _call(name, body, (m // tm, n // tn, nk), specs, out_specs, out_shape, args,
                scratch_shapes=[pltpu.VMEM((tm, tn), F32)] if nk > 1 else [], carry=carry)
    return res[0] if len(res) == 1 else res


def _ssm_wgrads(u, dy, g_re, g_im, h_re, h_im, tk=2048, carry=None):
    s = u.shape[0]
    tk = min(tk, s)
    nk = s // tk
    assert tk % N_SEG == 0

    def body(u_ref, dy_ref, gre_ref, gim_ref, hre_ref, him_ref, lre_ref, lim_ref, db_ref, dc_ref, dar_ref, dai_ref,
             pre_ref, pim_ref):
        kk = pl.program_id(1)
        u_blk, dy_blk = u_ref[...], dy_ref[...]
        g_r, g_i, h_r, h_i = gre_ref[...], gim_ref[...], hre_ref[...], him_ref[...]
        d_b = jnp.concatenate([_dot(u_blk, g_r, "tn"), _dot(u_blk, g_i, "tn")], axis=1)
        d_c = jnp.concatenate([_dot(h_r, dy_blk, "tn"), _dot(h_i, dy_blk, "tn")], axis=0)

        @pl.when(kk == 0)
        def _():
            first_row = lax.broadcasted_iota(jnp.int32, (N_SEG, CH_N), 0) == 0
            pre_ref[...] = jnp.where(first_row, 0.0, pltpu.roll(lre_ref[...], 1, 0))
            pim_ref[...] = jnp.where(first_row, 0.0, pltpu.roll(lim_ref[...], 1, 0))

        p_r = jnp.concatenate([pre_ref[...], h_r[:tk - N_SEG]], axis=0)
        p_i = jnp.concatenate([pim_ref[...], h_i[:tk - N_SEG]], axis=0)
        pre_ref[...] = h_r[tk - N_SEG:]
        pim_ref[...] = h_i[tk - N_SEG:]
        d_ar = jnp.sum(g_r * p_r + g_i * p_i, axis=0, keepdims=True)
        d_ai = jnp.sum(g_i * p_r - g_r * p_i, axis=0, keepdims=True)

        @pl.when(kk == 0)
        def _():
            db_ref[...] = d_b
            dc_ref[...] = d_c
            dar_ref[...] = d_ar
            dai_ref[...] = d_ai

        @pl.when(kk > 0)
        def _():
            db_ref[...] += d_b
            dc_ref[...] += d_c
            dar_ref[...] += d_ar
            dai_ref[...] += d_ai

    chan = pl.BlockSpec((tk, CH_W), lambda j, kk: (kk, j))
    state = pl.BlockSpec((tk, CH_N), lambda j, kk: (kk, j))
    last = pl.BlockSpec((N_SEG, CH_N), lambda j, kk: (s // N_SEG - 1, j))
    row = pl.BlockSpec((1, CH_N), lambda j, kk: (0, j))
    return _call(
        "ssm_wgrads", body, (SSM_CHUNKS, nk), [chan, chan, state, state, state, state, last, last],
        [pl.BlockSpec((None, CH_W, 2 * CH_N), lambda j, kk: (j, 0, 0)),
         pl.BlockSpec((None, 2 * CH_N, CH_W), lambda j, kk: (j, 0, 0)), row, row],
        [jax.ShapeDtypeStruct((SSM_CHUNKS, CH_W, 2 * CH_N), F32), jax.ShapeDtypeStruct((SSM_CHUNKS, 2 * CH_N, CH_W), F32),
         jax.ShapeDtypeStruct((1, N_STATE), F32), jax.ShapeDtypeStruct((1, N_STATE), F32)],
        (u, dy, g_re, g_im, h_re, h_im, h_re, h_im), scratch_shapes=[pltpu.VMEM((N_SEG, CH_N), F32)] * 2, carry=carry)


SCAN_LB = 256


def _split_by_scan_block(mat, axis):
    halves = []
    for l in range(CH_N // SCAN_LB):
        re = lax.slice_in_dim(mat, l * SCAN_LB, (l + 1) * SCAN_LB, axis=axis)
        im = lax.slice_in_dim(mat, CH_N + l * SCAN_LB, CH_N + (l + 1) * SCAN_LB, axis=axis)
        halves.append(jnp.concatenate([re, im], axis=axis))
    return jnp.stack(halves, axis=1).reshape((-1,) + halves[0].shape[1:])


def _ssm_scan(name, chan, expand12, contract12, a_re, a_im, d_row, reverse, carry=None):
    s = chan.shape[0]
    seg_len = s // N_SEG
    n_sq = int(math.log2(seg_len))
    assert 2 ** n_sq == seg_len
    rb = min(512, s)
    per_chunk = CH_N // SCAN_LB

    def body(are_ref, aim_ref, ch_ref, e_ref, k_ref, d_ref, hre_ref, him_ref, o_ref, wre_ref, wim_ref, ere, eim, cre, cim):
        e_mat, k_mat = e_ref[...], k_ref[...]
        for r in range(s // rb):
            rows = slice(r * rb, (r + 1) * rb)
            w = _dot(ch_ref[rows, :], e_mat, "nt" if reverse else "nn")
            wre_ref[rows, :] = w[:, :SCAN_LB]
            wim_ref[rows, :] = w[:, SCAN_LB:]

        ar1 = are_ref[...]
        ai1 = -aim_ref[...] if reverse else aim_ref[...]
        ar = jnp.broadcast_to(ar1, (N_SEG, SCAN_LB))
        ai = jnp.broadcast_to(ai1, (N_SEG, SCAN_LB))

        def rows_of(k):
            kk = seg_len - 1 - k if reverse else k
            return pl.ds(pl.multiple_of(kk * N_SEG, N_SEG), N_SEG)

        def local(k, carry):
            hr, hi = carry
            rows = rows_of(k)
            nr = ar * hr - ai * hi + wre_ref[rows, :]
            ni = ar * hi + ai * hr + wim_ref[rows, :]
            hre_ref[rows, :] = nr
            him_ref[rows, :] = ni
            return nr, ni

        zero = jnp.zeros((N_SEG, SCAN_LB), F32)
        er, ei = lax.fori_loop(0, seg_len, local, (zero, zero))
        ere[...] = er
        eim[...] = ei
        pr, pi = ar1, ai1
        for _ in range(n_sq):
            pr, pi = pr * pr - pi * pi, 2.0 * pr * pi
        cr = jnp.zeros((1, SCAN_LB), F32)
        ci = jnp.zeros((1, SCAN_LB), F32)
        for jj in range(N_SEG):
            j = N_SEG - 1 - jj if reverse else jj
            cre[j:j + 1, :] = cr
            cim[j:j + 1, :] = ci
            er_j, ei_j = ere[j:j + 1, :], eim[j:j + 1, :]
            cr, ci = pr * cr - pi * ci + er_j, pr * ci + pi * cr + ei_j
        c_r, c_i = cre[...], cim[...]

        def fix(k, carry):
            qr, qi = carry
            rows = rows_of(k)
            hre_ref[rows, :] = hre_ref[rows, :] + (qr * c_r - qi * c_i)
            him_ref[rows, :] = him_ref[rows, :] + (qr * c_i + qi * c_r)
            return qr * ar - qi * ai, qr * ai + qi * ar

        lax.fori_loop(0, seg_len, fix, (ar, ai))

        first_of_chunk = lax.rem(pl.program_id(0), per_chunk) == 0
        for r in range(s // rb):
            rows = slice(r * rb, (r + 1) * rb)
            h_cat = jnp.concatenate([hre_ref[rows, :], him_ref[rows, :]], axis=1)
            part = _dot(h_cat, k_mat, "nt" if reverse else "nn")

            @pl.when(first_of_chunk)
            def _(rows=rows, part=part):
                o_ref[rows, :] = part + d_ref[...] * ch_ref[rows, :]

            @pl.when(jnp.logical_not(first_of_chunk))
            def _(rows=rows, part=part):
                o_ref[rows, :] += part

    nblk = N_STATE // SCAN_LB
    blk = pl.BlockSpec((s, SCAN_LB), lambda b: (0, b))
    row = pl.BlockSpec((1, SCAN_LB), lambda b: (0, b))
    chan_blk = pl.BlockSpec((s, CH_W), lambda b: (0, b // per_chunk))
    res = _call(name, body, (nblk,),
                [row, row, chan_blk, pl.BlockSpec((None,) + expand12.shape[1:], lambda b: (b, 0, 0)),
                 pl.BlockSpec((None,) + contract12.shape[1:], lambda b: (b, 0, 0)),
                 pl.BlockSpec((1, CH_W), lambda b: (0, b // per_chunk))],
                [blk, blk, chan_blk],
                [jax.ShapeDtypeStruct((s, N_STATE), F32)] * 2 + [jax.ShapeDtypeStruct((s, SSM_WIDTH), F32)],
                (a_re, a_im, chan, expand12, contract12, d_row),
                scratch_shapes=[pltpu.VMEM((s, SCAN_LB), F32)] * 2 + [pltpu.VMEM((N_SEG, SCAN_LB), F32)] * 4, carry=carry)
    return res[0], res[1], res[2]


def _disc(ldt, are, aim, bre, bim):
    dt = jnp.exp(ldt)
    mag = jnp.exp(are * dt)
    abr = mag * jnp.cos(aim * dt)
    abi = mag * jnp.sin(aim * dt)
    den = jnp.square(are) + jnp.square(aim)
    nr = abr - 1.0
    fre = (nr * are + abi * aim) / den
    fim = (abi * are - nr * aim) / den
    return abr, abi, fre * bre - fim * bim, fre * bim + fim * bre


def _ssm_disc_fwd(ldt, are, aim, bre, bim):
    def body(l_ref, ar_ref, ai_ref, br_ref, bi_ref, o0, o1, o2, o3):
        res = _disc(l_ref[...], ar_ref[...], ai_ref[...], br_ref[...], bi_ref[...])
        for ref, val in zip((o0, o1, o2, o3), res):
            ref[...] = val

    col = jax.ShapeDtypeStruct((N_STATE, 1), F32)
    mat = jax.ShapeDtypeStruct((N_STATE, SSM_GROUP), F32)
    return pl.pallas_call(body, name="ssm_disc_fwd", out_shape=[col, col, mat, mat],
                          in_specs=[VMEM_SPEC] * 5, out_specs=[VMEM_SPEC] * 4)(ldt, are, aim, bre, bim)


def _ssm_disc_bwd(ldt, are, aim, bre, bim, d_abr, d_abi, d_bbr, d_bbi):
    def body(l_ref, ar_ref, ai_ref, br_ref, bi_ref, c0, c1, c2, c3, g_ldt, g_are, g_aim, g_bre, g_bim):
        _, vjp = jax.vjp(_disc, l_ref[...], ar_ref[...], ai_ref[...], br_ref[...], bi_ref[...])
        dl, dar, dai, dbr, dbi = vjp((c0[...], c1[...], c2[...], c3[...]))
        state = lax.broadcasted_iota(jnp.int32, (N_STATE, SSM_GROUPS), 0)
        group = lax.broadcasted_iota(jnp.int32, (N_STATE, SSM_GROUPS), 1)
        pick = jnp.right_shift(state, 6) == group
        g_ldt[...] = jnp.sum(jnp.where(pick, dl, 0.0), axis=0, keepdims=True)
        g_are[...] = dar
        g_aim[...] = dai
        g_bre[...] = dbr
        g_bim[...] = dbi

    col = jax.ShapeDtypeStruct((N_STATE, 1), F32)
    mat = jax.ShapeDtypeStruct((N_STATE, SSM_GROUP), F32)
    return pl.pallas_call(body, name="ssm_disc_bwd",
                          out_shape=[jax.ShapeDtypeStruct((1, SSM_GROUPS), F32), col, col, mat, mat],
                          in_specs=[VMEM_SPEC] * 9, out_specs=[VMEM_SPEC] * 5,
                          compiler_params=pltpu.CompilerParams(vmem_limit_bytes=VMEM_LIMIT))(
        ldt, are, aim, bre, bim, d_abr, d_abi, d_bbr, d_bbi)


_EYE8 = np.eye(8, dtype=np.float32)


def _blockdiag_b(bb):
    t = bb.reshape(SSM_CHUNKS, 8, SSM_STATE, SSM_GROUP).transpose(0, 1, 3, 2)
    return jnp.einsum("igcn,gh->igchn", t, _EYE8).reshape(SSM_CHUNKS, CH_W, CH_N)


def _diag_of_b(m):
    t = jnp.einsum("igchn,gh->igcn", m.reshape(SSM_CHUNKS, 8, SSM_GROUP, 8, SSM_STATE), _EYE8)
    return t.transpose(0, 1, 3, 2).reshape(N_STATE, SSM_GROUP)


def _blockdiag_c(c):
    t = c.reshape(SSM_CHUNKS, 8, SSM_GROUP, SSM_STATE).transpose(0, 1, 3, 2)
    return jnp.einsum("ignc,gh->ignhc", t, _EYE8).reshape(SSM_CHUNKS, CH_N, CH_W)


def _diag_of_c(m):
    t = jnp.einsum("ignhc,gh->ignc", m.reshape(SSM_CHUNKS, 8, SSM_STATE, 8, SSM_GROUP), _EYE8)
    return t.transpose(0, 1, 3, 2).reshape(SSM_GROUPS, SSM_GROUP, SSM_STATE)


def _time_perm(a):
    s, c = a.shape
    return a.reshape(N_SEG, s // N_SEG, c).transpose(1, 0, 2).reshape(s, c)


def _time_unperm(a):
    s, c = a.shape
    return a.reshape(s // N_SEG, N_SEG, c).transpose(1, 0, 2).reshape(s, c)


def _dilate(a, d):
    s, c = a.shape
    return a if d == 1 else a.reshape(s // d, d, c).transpose(1, 0, 2).reshape(s, c)


def _undilate(a, d):
    s, c = a.shape
    return a if d == 1 else a.reshape(d, s // d, c).transpose(1, 0, 2).reshape(s, c)


def _dilate_rows(a, d):
    r, s = a.shape
    return a if d == 1 else a.reshape(r, s // d, d).transpose(0, 2, 1).reshape(r, s)


ATT_T_FWD = 4
ATT_T_BWD = 8


def _window(prev_ref, cur_ref, i, sl):
    if i == 0:
        return jnp.concatenate([prev_ref[:, sl], cur_ref[0:ATT_BLK, sl]], axis=0)
    return cur_ref[(i - 1) * ATT_BLK:(i + 1) * ATT_BLK, sl]


def _band_valid(first_key):
    qi = lax.broadcasted_iota(jnp.int32, (ATT_BLK, 2 * ATT_BLK), 0)
    ki = lax.broadcasted_iota(jnp.int32, (ATT_BLK, 2 * ATT_BLK), 1)
    steps = qi + ATT_BLK - ki
    return (steps >= 0) & (steps <= ATT_BLK) & (ki >= first_key)


ATT_STATW = ATT_HPG * 128


def _stat(h):
    return slice(h * 128, (h + 1) * 128)


def _stat_rows(stat):
    n = stat.shape[0]
    heads = [stat[:, _stat(h)].T[0:1, :] for h in range(ATT_HPG)]
    return jnp.concatenate(heads + [jnp.zeros((8 - ATT_HPG, n), stat.dtype)], axis=0)


def _attn_specs(nb, t, width=ATT_GROUPW):
    cur = pl.BlockSpec((t * ATT_BLK, width), lambda b: (b, 0))
    prev = pl.BlockSpec((ATT_BLK, width), lambda b: (jnp.maximum(b * t - 1, 0), 0))
    nxt = pl.BlockSpec((ATT_BLK, width), lambda b: (jnp.minimum((b + 1) * t, nb - 1), 0))
    return cur, prev, nxt


def _attn_fwd(tag, per_seq, q, k, v):
    s = q.shape[0]
    nb = s // ATT_BLK

    def body(q_ref, kc_ref, kp_ref, vc_ref, vp_ref, o_ref, lse_ref):
        bt = pl.program_id(0)
        for i in range(ATT_T_FWD):
            has_prev = lax.rem(bt * ATT_T_FWD + i, per_seq) > 0
            valid = _band_valid(jnp.where(has_prev, 0, ATT_BLK))
            rows = slice(i * ATT_BLK, (i + 1) * ATT_BLK)
            for h in range(ATT_HPG):
                sl = slice(h * ATT_HEAD_DIM, (h + 1) * ATT_HEAD_DIM)
                kcat = _window(kp_ref, kc_ref, i, sl)
                vcat = _window(vp_ref, vc_ref, i, sl)
                sc = _dot(q_ref[rows, sl], kcat, "nt") * ATT_SCALE
                sc = jnp.where(valid, sc, NEG_INF)
                m = jnp.max(sc, axis=-1, keepdims=True)
                p = jnp.exp(sc - m)
                den = jnp.sum(p, axis=-1, keepdims=True)
                o_ref[rows, sl] = _dot(p, vcat, "nn") / den
                lse_ref[rows, _stat(h)] = jnp.broadcast_to(m + jnp.log(den), (ATT_BLK, 128))

    cur, prev, _ = _attn_specs(nb, ATT_T_FWD)
    stat, _, _ = _attn_specs(nb, ATT_T_FWD, ATT_STATW)
    return pl.pallas_call(
        body, name="attn_fwd_" + tag, grid=(nb // ATT_T_FWD,), in_specs=[cur, cur, prev, cur, prev], out_specs=[cur, stat],
        out_shape=[jax.ShapeDtypeStruct((s, ATT_GROUPW), F32), jax.ShapeDtypeStruct((s, ATT_STATW), F32)],
        compiler_params=_cparams(1))(q, k, k, v, v)


def _attn_dq(tag, per_seq, q, k, v, do, lse, delta):
    s = q.shape[0]
    nb = s // ATT_BLK

    def body(q_ref, kc_ref, kp_ref, vc_ref, vp_ref, do_ref, lse_ref, dl_ref, dq_ref):
        bt = pl.program_id(0)
        for i in range(ATT_T_BWD):
            has_prev = lax.rem(bt * ATT_T_BWD + i, per_seq) > 0
            valid = _band_valid(jnp.where(has_prev, 0, ATT_BLK))
            rows = slice(i * ATT_BLK, (i + 1) * ATT_BLK)
            for h in range(ATT_HPG):
                sl = slice(h * ATT_HEAD_DIM, (h + 1) * ATT_HEAD_DIM)
                kcat = _window(kp_ref, kc_ref, i, sl)
                vcat = _window(vp_ref, vc_ref, i, sl)
                lse = jnp.concatenate([lse_ref[rows, _stat(h)]] * 2, axis=1)
                dlt = jnp.concatenate([dl_ref[rows, _stat(h)]] * 2, axis=1)
                sc = _dot(q_ref[rows, sl], kcat, "nt") * ATT_SCALE
                p = jnp.exp(jnp.where(valid, sc, NEG_INF) - lse)
                dp = _dot(do_ref[rows, sl], vcat, "nt")
                ds = p * (dp - dlt) * ATT_SCALE
                dq_ref[rows, sl] = _dot(ds, kcat, "nn")

    cur, prev, _ = _attn_specs(nb, ATT_T_BWD)
    stat, _, _ = _attn_specs(nb, ATT_T_BWD, ATT_STATW)
    return pl.pallas_call(
        body, name="attn_dq_" + tag, grid=(nb // ATT_T_BWD,), in_specs=[cur, cur, prev, cur, prev, cur, stat, stat],
        out_specs=cur, out_shape=jax.ShapeDtypeStruct((s, ATT_GROUPW), F32),
        compiler_params=_cparams(1))(q, k, k, v, v, do, lse, delta)


def _attn_dkv(tag, per_seq, q, k, v, do, lse_t, delta_t):
    s = q.shape[0]
    nb = s // ATT_BLK

    def body(k_ref, v_ref, qc_ref, qn_ref, doc_ref, don_ref, lc_ref, ln_ref, dc_ref, dn_ref, dk_ref, dv_ref):
        bt = pl.program_id(0)
        ki = lax.broadcasted_iota(jnp.int32, (ATT_BLK, 2 * ATT_BLK), 0)
        ci = lax.broadcasted_iota(jnp.int32, (ATT_BLK, 2 * ATT_BLK), 1)

        def pair(edge_ref, cur_ref, i, sl):
            if i == ATT_T_BWD - 1:
                return jnp.concatenate([cur_ref[i * ATT_BLK:(i + 1) * ATT_BLK, sl], edge_ref[:, sl]], axis=0)
            return cur_ref[i * ATT_BLK:(i + 2) * ATT_BLK, sl]

        def pair_row(edge_ref, cur_ref, i, h):
            if i == ATT_T_BWD - 1:
                row = jnp.concatenate([cur_ref[h:h + 1, i * ATT_BLK:(i + 1) * ATT_BLK], edge_ref[h:h + 1, :]], axis=1)
            else:
                row = cur_ref[h:h + 1, i * ATT_BLK:(i + 2) * ATT_BLK]
            return jnp.broadcast_to(row, (ATT_BLK, 2 * ATT_BLK))

        for i in range(ATT_T_BWD):
            b = bt * ATT_T_BWD + i
            next_uses = (b + 1 < nb) & (lax.rem(b + 1, per_seq) > 0)
            reach = jnp.where(next_uses, 0, 4 * ATT_BLK)
            valid = ((ci < ATT_BLK) & (ci >= ki)) | ((ci >= ATT_BLK) & (ki - ci + ATT_BLK >= reach))
            rows = slice(i * ATT_BLK, (i + 1) * ATT_BLK)
            for h in range(ATT_HPG):
                sl = slice(h * ATT_HEAD_DIM, (h + 1) * ATT_HEAD_DIM)
                qcat, docat = pair(qn_ref, qc_ref, i, sl), pair(don_ref, doc_ref, i, sl)
                sc = _dot(k_ref[rows, sl], qcat, "nt") * ATT_SCALE
                p = jnp.exp(jnp.where(valid, sc, NEG_INF) - pair_row(ln_ref, lc_ref, i, h))
                dv_ref[rows, sl] = _dot(p, docat, "nn")
                dp = _dot(v_ref[rows, sl], docat, "nt")
                ds = p * (dp - pair_row(dn_ref, dc_ref, i, h)) * ATT_SCALE
                dk_ref[rows, sl] = _dot(ds, qcat, "nn")

    cur, _, nxt = _attn_specs(nb, ATT_T_BWD)
    stat = pl.BlockSpec((8, ATT_T_BWD * ATT_BLK), lambda b: (0, b))
    snxt = pl.BlockSpec((8, ATT_BLK), lambda b: (0, jnp.minimum((b + 1) * ATT_T_BWD, nb - 1)))
    return pl.pallas_call(
        body, name="attn_dkv_" + tag, grid=(nb // ATT_T_BWD,), in_specs=[cur, cur, cur, nxt, cur, nxt, stat, snxt, stat, snxt],
        out_specs=[cur, cur], out_shape=[jax.ShapeDtypeStruct((s, ATT_GROUPW), F32)] * 2,
        compiler_params=_cparams(1))(k, v, q, q, do, do, lse_t, lse_t, delta_t, delta_t)


def _xattn_probs(q, kh):
    sc = _dot(q, kh, "nt") * XATT_SCALE
    e = jnp.exp(sc - jnp.max(sc, axis=-1, keepdims=True))
    return e / jnp.sum(e, axis=-1, keepdims=True)


def _xattn_fwd(q, kv, tm=512):
    s = q.shape[0]
    tm = min(tm, s)

    def body(q_ref, kv_ref, o_ref):
        for h in range(XATT_HEADS):
            sl = slice(h * XATT_HEAD_DIM, (h + 1) * XATT_HEAD_DIM)
            vs = slice(D_MODEL + h * XATT_HEAD_DIM, D_MODEL + (h + 1) * XATT_HEAD_DIM)
            p = _xattn_probs(q_ref[:, sl], kv_ref[:, sl])
            o_ref[:, sl] = _dot(p, kv_ref[:, vs], "nn").astype(o_ref.dtype)

    return pl.pallas_call(
        body, name="xattn_fwd", grid=(s // tm,),
        in_specs=[pl.BlockSpec((tm, D_MODEL), lambda i: (i, 0)), pl.BlockSpec(kv.shape, lambda i: (0, 0))],
        out_specs=pl.BlockSpec((tm, D_MODEL), lambda i: (i, 0)),
        out_shape=jax.ShapeDtypeStruct((s, D_MODEL), MXU_DTYPE), compiler_params=_cparams(1))(q, kv)


def _xattn_bwd(q, kv, do, tm=1024):
    s = q.shape[0]
    tm = min(tm, s)

    def body(q_ref, kv_ref, do_ref, dq_ref, dkv_ref):
        first = pl.program_id(0) == 0

        @pl.when(first)
        def _():
            dkv_ref[...] = jnp.zeros_like(dkv_ref)

        for h in range(XATT_HEADS):
            sl = slice(h * XATT_HEAD_DIM, (h + 1) * XATT_HEAD_DIM)
            vs = slice(D_MODEL + h * XATT_HEAD_DIM, D_MODEL + (h + 1) * XATT_HEAD_DIM)
            p = _xattn_probs(q_ref[:, sl], kv_ref[:, sl])
            dkv_ref[:, vs] += _dot(p, do_ref[:, sl], "tn")
            dp = _dot(do_ref[:, sl], kv_ref[:, vs], "nt")
            ds = p * (dp - jnp.sum(dp * p, axis=-1, keepdims=True)) * XATT_SCALE
            dq_ref[:, sl] = _dot(ds, kv_ref[:, sl], "nn").astype(dq_ref.dtype)
            dkv_ref[:, sl] += _dot(ds, q_ref[:, sl], "tn")

    row = pl.BlockSpec((tm, D_MODEL), lambda i: (i, 0))
    whole = pl.BlockSpec(kv.shape, lambda i: (0, 0))
    return pl.pallas_call(
        body, name="xattn_bwd", grid=(s // tm,), in_specs=[row, whole, row], out_specs=[row, whole],
        out_shape=[jax.ShapeDtypeStruct((s, D_MODEL), MXU_DTYPE), jax.ShapeDtypeStruct(kv.shape, F32)],
        compiler_params=_cparams(1))(q, kv, do)


def _ln(x, g, b):
    mu = jnp.mean(x, axis=-1, keepdims=True)
    xc = x - mu
    var = jnp.mean(jnp.square(xc), axis=-1, keepdims=True)
    return xc * lax.rsqrt(var + LN_EPS) * g + b


def _res_ln(h, o, g, b):
    return _ln(DEEPNORM_ALPHA * h + o, g, b)


def _gate(gs, ga, z1, z2, batt):
    return jax.nn.sigmoid(gs) * (z1 * jax.nn.sigmoid(z2)) + jax.nn.sigmoid(ga) * batt


ROPE_TW = 2 * ATT_HEAD_DIM


def _rope_tables(pos, invf, m1, m2):
    ang = pos.astype(F32) * invf
    sin = jnp.sin(ang)
    return jnp.cos(ang), -sin * m1, sin * m2


def _widen(tab):
    return jnp.concatenate([tab] * (ATT_GROUPW // ROPE_TW), axis=1)


def _rope(t, cos, s_up, s_dn):
    w = t.shape[-1]
    return t * cos + pltpu.roll(t, w - ROT_DIM // 2, 1) * s_up + pltpu.roll(t, ROT_DIM // 2, 1) * s_dn


def _rope_t(dt, cos, s_up, s_dn):
    w = dt.shape[-1]
    return dt * cos + pltpu.roll(dt * s_up, ROT_DIM // 2, 1) + pltpu.roll(dt * s_dn, w - ROT_DIM // 2, 1)


def _rope_consts():
    inv_freq = ROPE_THETA ** (-jnp.arange(0, ROT_DIM, 2, dtype=F32) / ROT_DIM)
    d = np.arange(ROPE_TW) % ATT_HEAD_DIM
    invf = jnp.where(d < ROT_DIM, inv_freq[d % (ROT_DIM // 2)], 0.0).reshape(1, ROPE_TW).astype(F32)
    m1 = jnp.asarray((d < ROT_DIM // 2).astype(np.float32)).reshape(1, ROPE_TW)
    m2 = jnp.asarray(((d >= ROT_DIM // 2) & (d < ROT_DIM)).astype(np.float32)).reshape(1, ROPE_TW)
    return invf, m1, m2


def _head_sum_matrix():
    d = np.arange(ATT_GROUPW) // ATT_HEAD_DIM
    s = np.arange(ATT_STATW) // 128
    return jnp.asarray((d[:, None] == s[None, :]).astype(np.float32))


def _adamw(w, g, m, v):
    m = ADAM_B1 * m + (1.0 - ADAM_B1) * g
    v = ADAM_B2 * v + (1.0 - ADAM_B2) * jnp.square(g)
    m_hat = m / (1.0 - ADAM_B1 ** ADAM_STEP)
    v_hat = v / (1.0 - ADAM_B2 ** ADAM_STEP)
    delta = -ADAM_LR * (m_hat / (jnp.sqrt(v_hat) + ADAM_EPS) + ADAM_WD * w)
    return delta, m, v


def _local_step(x, mem, pos, target, sp, ex):
    s = x.shape[0]
    al = DEEPNORM_ALPHA
    mx = MXU_DTYPE

    h0, h0b = _rowwise("ln_in", lambda x, g, b: (lambda h: (h, h))(_ln(x, g, b)), [x],
                       [sp["ln_in_g"], sp["ln_in_b"]], [(D_MODEL, F32), (D_MODEL, mx)],
                       carry=ex.gather_carry(["w_in"]))
    proj = _mm("proj", h0b, ex.weight("w_in"), "nn", bias=sp["b_in"],
               carry=ex.gather_carry(["w_glu", "w_att_up", "w_mix_out", "w_xq"]))

    ldt = jnp.repeat(sp["ssm_log_dt"].reshape(SSM_GROUPS), SSM_STATE).reshape(N_STATE, 1)
    are, aim = sp["ssm_a_re"].reshape(N_STATE, 1), sp["ssm_a_im"].reshape(N_STATE, 1)
    bre, bim = sp["ssm_b_re"].reshape(N_STATE, SSM_GROUP), sp["ssm_b_im"].reshape(N_STATE, SSM_GROUP)
    abr, abi, bbr, bbi = _ssm_disc_fwd(ldt, are, aim, bre, bim)
    a_re, a_im = abr.reshape(1, N_STATE), abi.reshape(1, N_STATE)
    bexp = jnp.concatenate([_blockdiag_b(bbr), _blockdiag_b(bbi)], axis=2).astype(mx)
    cexp = jnp.concatenate([_blockdiag_c(sp["ssm_c_re"].reshape(SSM_GROUPS, SSM_GROUP, SSM_STATE)),
                            -_blockdiag_c(sp["ssm_c_im"].reshape(SSM_GROUPS, SSM_GROUP, SSM_STATE))],
                           axis=1).astype(mx)
    u_p = _time_perm(proj[:, :SSM_WIDTH])
    b12, c12 = _split_by_scan_block(bexp, 2), _split_by_scan_block(cexp, 1)
    h_re, h_im, y_p = _ssm_scan("ssm_scan_fwd", u_p, b12, c12, a_re, a_im, sp["ssm_d"], reverse=False,
                                carry=ex.gather_carry(["w_xkv", "w_xo", "w_ff1", "w_ff2"]))
    y = _time_unperm(y_p)
    ygb, = _rowwise("gelu", lambda y: jax.nn.gelu(y), [y], [], [(SSM_WIDTH, mx)])
    z = _mm("glu", ygb, ex.weight("w_glu"), "nn", bias=sp["b_glu"])

    invf, m1, m2 = _rope_consts()

    def rope_fwd(pos, q0, q1, q2, k0, k1, k2, v0, v1, v2, invf, m1, m2):
        narrow = _rope_tables(pos, invf, m1, m2)
        tabs = [_widen(t) for t in narrow]
        return tuple(_rope(t, *tabs) for t in (q0, q1, q2, k0, k1, k2)) + (v0, v1, v2) + tuple(narrow)

    qkv_cols = [(proj, ATT_GROUPW, 3 + i) for i in range(9)]
    qkv = _rowwise("rope", rope_fwd, [pos] + qkv_cols, [invf, m1, m2], [(ATT_GROUPW, mx)] * 9 + [(ROPE_TW, F32)] * 3)
    rope_tabs = qkv[9:]
    n_blocks = s // ATT_BLK
    groups = [(str(g), n_blocks // d, d) for g, d in enumerate(DILATIONS)]
    q_d = [_dilate(qkv[g], d) for g, d in enumerate(DILATIONS)]
    k_d = [_dilate(qkv[3 + g], d) for g, d in enumerate(DILATIONS)]
    v_d = [_dilate(qkv[6 + g], d) for g, d in enumerate(DILATIONS)]
    o_g, l_g = [], []
    for g, (tag, per_seq, d) in enumerate(groups):
        o, lse = _attn_fwd(tag, per_seq, q_d[g], k_d[g], v_d[g])
        o_g.append(_undilate(o, d))
        l_g.append(_undilate(lse, d))

    def merge(o0, o1, o2, l0, l1, l2):
        m = jnp.maximum(jnp.maximum(l0, l1), l2)
        e0, e1, e2 = jnp.exp(l0 - m), jnp.exp(l1 - m), jnp.exp(l2 - m)
        tot = e0 + e1 + e2

        def per_dim(e):
            w = e / tot
            return jnp.concatenate([w[:, h * 128:h * 128 + ATT_HEAD_DIM] for h in range(ATT_HPG)], axis=1)

        att = per_dim(e0) * o0 + per_dim(e1) * o1 + per_dim(e2) * o2
        lse = m + jnp.log(tot)
        return att, att, lse, _stat_rows(lse)

    att, attb, lse_tot, lse_tot_t = _rowwise("attn_merge", merge, o_g + l_g, [],
                                             [(ATT_GROUPW, F32), (ATT_GROUPW, mx), (ATT_STATW, F32)], touts=[(8, F32)])
    batt = _mm("att_up", attb, ex.weight("w_att_up"), "nn")

    gate_rows = [(proj, D_MODEL, 3), (proj, D_MODEL, 4), (z, D_MODEL, 0), (z, D_MODEL, 1), batt]
    mixedb, = _rowwise("gate", _gate, gate_rows, [], [(D_MODEL, mx)])
    o1 = _mm("mix_out", mixedb, ex.weight("w_mix_out"), "nn", bias=sp["b_mix_out"])
    h1, h1b = _rowwise("ln1", lambda h, o, g, b: (lambda r: (r, r))(_res_ln(h, o, g, b)), [h0, o1],
                       [sp["ln1_g"], sp["ln1_b"]], [(D_MODEL, F32), (D_MODEL, mx)])

    qx = _mm("xq", h1b, ex.weight("w_xq"), "nn", out_dtypes=(mx,))
    kvx = _mm("xkv", mem, ex.weight("w_xkv"), "nn", out_dtypes=(mx,))
    oxb = _xattn_fwd(qx, kvx)
    o2 = _mm("xo", oxb, ex.weight("w_xo"), "nn")
    h2, h2b = _rowwise("ln2", lambda h, o, g, b: (lambda r: (r, r))(_res_ln(h, o, g, b)), [h1, o2],
                       [sp["ln2_g"], sp["ln2_b"]], [(D_MODEL, F32), (D_MODEL, mx)])

    a_ff, fb = _mm("ff1", h2b, ex.weight("w_ff1"), "nn", bias=sp["b_ff1"],
                   epilogue=lambda r: (r, jnp.square(jnp.maximum(r, 0.0))), out_dtypes=(F32, mx))
    o3 = _mm("ff2", fb, ex.weight("w_ff2"), "nn", bias=sp["b_ff2"])

    def loss_bwd(h2, o3, tgt, g, b):
        def f(h2, o3, g, b):
            h3 = _res_ln(h2, o3, g, b)
            return 0.5 * jnp.sum(jnp.mean(jnp.square(h3 - tgt), axis=-1))

        loss, vjp = jax.vjp(f, h2, o3, g, b)
        _, dr, dg, db = vjp(jnp.ones((), F32))
        return dr, dr, dg, db, _colsum(dr), jnp.full((1, 128), loss, F32)

    dr3, dr3b, g_ln3_g, g_ln3_b, g_b_ff2, loss = _rowwise(
        "loss_ln3_bwd", loss_bwd, [h2, o3, target], [sp["ln3_g"], sp["ln3_b"]],
        [(D_MODEL, F32), (D_MODEL, mx)], [D_MODEL, D_MODEL, D_MODEL, 128])

    dab, da_sums = _mm("ff2_dx", dr3b, ex.weight("w_ff2"), "nt", extras=(a_ff,),
                       epilogue=lambda r, a: (r * (2.0 * jnp.maximum(a, 0.0)),), out_dtypes=(mx,), colsum=True)
    g_b_ff1 = jnp.sum(da_sums, axis=0)
    ex.grad("w_ff2", _mm("ff2_dw", fb, dr3b, "tn"))
    ex.grad("w_ff1", _mm("ff1_dw", h2b, dab, "tn", carry=ex.carry(swap=["w_ff2"])))
    dh2 = _mm("ff1_dx", dab, ex.weight("w_ff1"), "nt", extras=(dr3,), epilogue=lambda r, d: (r + al * d,),
              carry=ex.carry(swap=["w_ff1"]))

    def ln_bwd(h, o, dout, g, b):
        _, vjp = jax.vjp(_res_ln, h, o, g, b)
        _, dr, dg, db = vjp(dout)
        return dr, dr, dg, db, _colsum(dr)

    dr2, dr2b, g_ln2_g, g_ln2_b, _ = _rowwise(
        "ln2_bwd", ln_bwd, [h1, o2, dh2], [sp["ln2_g"], sp["ln2_b"]],
        [(D_MODEL, F32), (D_MODEL, mx)], [D_MODEL, D_MODEL, D_MODEL])
    ex.grad("w_xo", _mm("xo_dw", oxb, dr2b, "tn"))
    doxb = _mm("xo_dx", dr2b, ex.weight("w_xo"), "nt", out_dtypes=(mx,), carry=ex.carry(swap=["w_xo"]))
    dqxb, dkvx = _xattn_bwd(qx, kvx, doxb)
    ex.grad("w_xq", _mm("xq_dw", h1b, dqxb, "tn"))
    dh1 = _mm("xq_dx", dqxb, ex.weight("w_xq"), "nt", extras=(dr2,), epilogue=lambda r, d: (r + al * d,),
              carry=ex.carry(swap=["w_xq"]))
    ex.grad("w_xkv", _mm("xkv_dw", mem, dkvx, "tn"))

    dr1, dr1b, g_ln1_g, g_ln1_b, g_b_mix = _rowwise(
        "ln1_bwd", ln_bwd, [h0, o1, dh1], [sp["ln1_g"], sp["ln1_b"]],
        [(D_MODEL, F32), (D_MODEL, mx)], [D_MODEL, D_MODEL, D_MODEL])
    ex.grad("w_mix_out", _mm("mix_dw", mixedb, dr1b, "tn", carry=ex.carry(swap=["w_xkv"])))
    dmixed = _mm("mix_dx", dr1b, ex.weight("w_mix_out"), "nt", carry=ex.carry(swap=["w_mix_out"]))

    def gate_bwd(gs, ga, z1, z2, batt, dm):
        _, vjp = jax.vjp(_gate, gs, ga, z1, z2, batt)
        dgs, dga, dz1, dz2, dbatt = vjp(dm)
        dz = jnp.concatenate([dz1, dz2], axis=-1)
        return dgs, dga, dz, dbatt, _colsum(dz)

    dgsb, dgab, dzb, dbattb, g_b_glu = _rowwise(
        "gate_bwd", gate_bwd, gate_rows + [dmixed], [],
        [(D_MODEL, mx), (D_MODEL, mx), (2 * D_MODEL, mx), (D_MODEL, mx)], [2 * D_MODEL])
    ex.grad("w_att_up", _mm("att_up_dw", attb, dbattb, "tn"))
    datt = _mm("att_up_dx", dbattb, ex.weight("w_att_up"), "nt", carry=ex.carry(swap=["w_att_up"]))

    def att_delta(datt, att, hs):
        dl = jnp.dot(datt * att, hs, precision=lax.Precision.HIGHEST, preferred_element_type=F32)
        return datt, dl, _stat_rows(dl)

    dattb, delta, delta_t = _rowwise("attn_delta", att_delta, [datt, att], [_head_sum_matrix()],
                                     [(ATT_GROUPW, mx), (ATT_STATW, F32)], touts=[(8, F32)])
    dq_g, dk_g, dv_g = [], [], []
    for g, (tag, per_seq, d) in enumerate(groups):
        do_d, lt_d, dl_d = _dilate(dattb, d), _dilate(lse_tot, d), _dilate(delta, d)
        dq_g.append(_undilate(_attn_dq(tag, per_seq, q_d[g], k_d[g], v_d[g], do_d, lt_d, dl_d), d))
        dk, dv = _attn_dkv(tag, per_seq, q_d[g], k_d[g], v_d[g], do_d, _dilate_rows(lse_tot_t, d), _dilate_rows(delta_t, d))
        dk_g.append(_undilate(dk, d))
        dv_g.append(_undilate(dv, d))
    dqkv = dq_g + dk_g + dv_g

    def rope_bwd(q0, q1, q2, k0, k1, k2, v0, v1, v2, cos, s_up, s_dn):
        tabs = [_widen(t) for t in (cos, s_up, s_dn)]
        return jnp.concatenate([_rope_t(t, *tabs) for t in (q0, q1, q2, k0, k1, k2)] + [v0, v1, v2], axis=-1)

    dqkvb, = _rowwise("rope_bwd", rope_bwd, dqkv + list(rope_tabs), [], [(9 * ATT_GROUPW, mx)])

    ex.grad("w_glu", _mm("glu_dw", ygb, dzb, "tn"))
    dyg = _mm("glu_dx", dzb, ex.weight("w_glu"), "nt", carry=ex.carry(swap=["w_glu"]))

    def gelu_bwd(y, dyg):
        _, vjp = jax.vjp(jax.nn.gelu, y)
        return vjp(dyg)[0]

    dy, = _rowwise("gelu_bwd", gelu_bwd, [y, dyg], [], [(SSM_WIDTH, F32)])
    dy_p = _time_perm(dy)
    s_re, s_im, du_p = _ssm_scan("ssm_scan_bwd", dy_p, c12, b12, a_re, a_im, sp["ssm_d"], reverse=True,
                                 carry=ex.carry(ici=["w_ff1", "w_xkv", "w_glu"]))
    g_bexp, g_cexp, d_abr, d_abi = _ssm_wgrads(u_p, dy_p, s_re, s_im, h_re, h_im, carry=ex.carry(ici=["w_ff2"]))
    g_ssm_d, = _rowwise("ssm_dd", lambda a, b: (_colsum(a * b),), [dy_p, u_p], [], [], [SSM_WIDTH])
    g_ldt, g_are, g_aim, g_bre, g_bim = _ssm_disc_bwd(
        ldt, are, aim, bre, bim, d_abr.reshape(N_STATE, 1), d_abi.reshape(N_STATE, 1),
        _diag_of_b(g_bexp[:, :, :CH_N]), _diag_of_b(g_bexp[:, :, CH_N:]))
    g_c_re = _diag_of_c(g_cexp[:, :CH_N, :])
    g_c_im = -_diag_of_c(g_cexp[:, CH_N:, :])

    def assemble(du, dqkv, dgs, dga):
        row = jnp.concatenate([du.astype(mx), dqkv, dgs, dga], axis=-1)
        return row, _colsum(row)

    dprojb, g_b_in = _rowwise("in_assemble", assemble, [_time_unperm(du_p), dqkvb, dgsb, dgab], [],
                              [(IN_COLS, mx)], [IN_COLS])
    ex.grad("w_in", _mm("in_dw", h0b, dprojb, "tn",
                        carry=ex.carry(ici=["w_xo", "w_xq", "w_mix_out", "w_att_up"])))
    dh0 = _mm("in_dx", dprojb, ex.weight("w_in"), "nt", extras=(dr1,), epilogue=lambda r, d: (r + al * d,),
              carry=ex.carry(ici=["w_in"]))

    def ln_in_bwd(x, dout, g, b):
        _, vjp = jax.vjp(_ln, x, g, b)
        return vjp(dout)

    dx, g_ln_in_g, g_ln_in_b = _rowwise("ln_in_bwd", ln_in_bwd, [x, dh0], [sp["ln_in_g"], sp["ln_in_b"]],
                                        [(D_MODEL, F32)], [D_MODEL, D_MODEL], carry=ex.finish_carry())

    small = {"ln_in_g": g_ln_in_g, "ln_in_b": g_ln_in_b, "b_in": g_b_in, "ssm_log_dt": g_ldt, "ssm_a_re": g_are,
             "ssm_a_im": g_aim, "ssm_b_re": g_bre, "ssm_b_im": g_bim, "ssm_c_re": g_c_re, "ssm_c_im": g_c_im,
             "ssm_d": g_ssm_d, "b_glu": g_b_glu, "b_mix_out": g_b_mix, "ln1_g": g_ln1_g, "ln1_b": g_ln1_b,
             "ln2_g": g_ln2_g, "ln2_b": g_ln2_b, "b_ff1": g_b_ff1, "b_ff2": g_b_ff2, "ln3_g": g_ln3_g,
             "ln3_b": g_ln3_b}
    return loss, dx, small


def _piece_shape(k, n, axis):
    return (k // 2, n // 4) if axis == 1 else (k // 8, n)


def _aligned(v, m):
    return v if isinstance(v, int) else pl.multiple_of(v, m)


def _full_piece(ref, k, n, axis, chip, half):
    pr, pc = _piece_shape(k, n, axis)
    if axis == 1:
        return ref.at[pl.ds(_aligned(half * pr, 8), pr), pl.ds(_aligned(chip * pc, 128), pc)]
    return ref.at[pl.ds(_aligned(chip * (2 * pr) + half * pr, 8), pr), :]


def _shard_piece(ref, k, n, axis, half):
    pr, _ = _piece_shape(k, n, axis)
    return ref.at[pl.ds(_aligned(half * pr, 8), pr), :]


def _mesh_pos():
    x, y, c = lax.axis_index("x"), lax.axis_index("y"), lax.axis_index("c")
    other_chips = [(1 - x, y), (x, 1 - y), (1 - x, 1 - y)]
    return x, y, c, other_chips


def _remote(src, dst, send_sem, recv_sem, dev):
    return pltpu.make_async_remote_copy(src_ref=src, dst_ref=dst, send_sem=send_sem, recv_sem=recv_sem,
                                        device_id=dev, device_id_type=MESH)


def _placed(name, fn, n_steps, where, ins, out_sds, out_block, out_index):
    def body(w_ref, *refs):
        o_ref = refs[-1]
        o_ref[...] = fn(*[r[...] for r in refs[:-1]]).astype(o_ref.dtype)

    grid_spec = pltpu.PrefetchScalarGridSpec(
        num_scalar_prefetch=1, grid=(n_steps,), in_specs=[pl.BlockSpec(bs, idx) for _, bs, idx in ins],
        out_specs=pl.BlockSpec(out_block, out_index))
    return pl.pallas_call(body, name=name, grid_spec=grid_spec, out_shape=out_sds,
                          compiler_params=_cparams(1))(where, *[a for a, _, _ in ins])


def _gather_copies(widx):
    geo = [BIG[i][1:] for i in widx]

    def ici(full, wi, j, chip, send_sems, recv_sems, c, dev):
        k, n, ax = geo[wi]
        piece = _full_piece(full[wi], k, n, ax, chip, c)
        return _remote(piece, piece, send_sems.at[wi * 6 + j], recv_sems.at[wi * 6 + j], dev)

    def d2d(full, wi, j, chip, half, send_sems, recv_sems, sib):
        k, n, ax = geo[wi]
        piece = _full_piece(full[wi], k, n, ax, chip, half)
        return _remote(piece, piece, send_sems.at[wi * 6 + 3 + j], recv_sems.at[wi * 6 + 3 + j], sib)

    def start(_, full, send_sems, recv_sems):
        x, y, c, chips = _mesh_pos()
        for wi in range(len(geo)):
            for j, (qx, qy) in enumerate(chips):
                ici(full, wi, j, 2 * x + y, send_sems, recv_sems, c, (qx, qy, c)).start()

    def finish(_, full, send_sems, recv_sems):
        x, y, c, chips = _mesh_pos()
        sib = (x, y, 1 - c)
        for wi in range(len(geo)):
            for j, (qx, qy) in enumerate(chips):
                ici(full, wi, j, 2 * qx + qy, send_sems, recv_sems, c, (qx, qy, c)).wait_recv()
                d2d(full, wi, j, 2 * qx + qy, c, send_sems, recv_sems, sib).start()
        for wi in range(len(geo)):
            for j, (qx, qy) in enumerate(chips):
                d2d(full, wi, j, 2 * qx + qy, 1 - c, send_sems, recv_sems, sib).wait_recv()
        for wi in range(len(geo)):
            for j, (qx, qy) in enumerate(chips):
                ici(full, wi, j, 2 * x + y, send_sems, recv_sems, c, (qx, qy, c)).wait_send()
                d2d(full, wi, j, 2 * qx + qy, c, send_sems, recv_sems, sib).wait_send()

    return start, finish, 6 * len(geo)


def _swap_copies(widx):
    geo = [BIG[i][1:] for i in widx]

    def copies(g, got, send_sems, recv_sems, base):
        x, y, c, _ = _mesh_pos()
        return [_remote(_full_piece(g[wi], k, n, ax, q, 1 - c), got[wi].at[q], send_sems.at[base + wi * 4 + q],
                        recv_sems.at[base + wi * 4 + q], (x, y, 1 - c))
                for wi, (k, n, ax) in enumerate(geo) for q in range(4)]

    def start(g, got, send_sems, recv_sems, base=0):
        for cp in copies(g, got, send_sems, recv_sems, base):
            cp.start()

    def finish(g, got, send_sems, recv_sems, base=0):
        for cp in copies(g, got, send_sems, recv_sems, base):
            cp.wait()

    return start, finish, 4 * len(geo)


def _swap_shapes(widx):
    return [jax.ShapeDtypeStruct((4,) + _piece_shape(*BIG[i][1:]), F32) for i in widx]


def _reduce_swap_halves(tag, grads, widx):
    nw = len(widx)
    start, finish, n_sems = _swap_copies(widx)

    def body(*refs):
        start(refs[:nw], refs[nw:2 * nw], *refs[2 * nw:])
        finish(refs[:nw], refs[nw:2 * nw], *refs[2 * nw:])

    return pl.pallas_call(
        body, name="reduce_swap_halves_" + tag, in_specs=[HBM_SPEC] * nw, out_specs=[HBM_SPEC] * nw,
        out_shape=_swap_shapes(widx),
        scratch_shapes=[pltpu.SemaphoreType.DMA((n_sems,)), pltpu.SemaphoreType.DMA((n_sems,))])(*grads)


def _owner_copies(nw):
    def copies(p, out, send_sems, recv_sems, base):
        x, y, c, chips = _mesh_pos()
        return [_remote(p[wi].at[2 * qx + qy], out[wi].at[j], send_sems.at[base + wi * 3 + j],
                        recv_sems.at[base + wi * 3 + j], (qx, qy, c))
                for wi in range(nw) for j, (qx, qy) in enumerate(chips)]

    def start(p, out, send_sems, recv_sems, base=0):
        for cp in copies(p, out, send_sems, recv_sems, base):
            cp.start()

    def finish(p, out, send_sems, recv_sems, base=0):
        for cp in copies(p, out, send_sems, recv_sems, base):
            cp.wait()

    return start, finish, 3 * nw


def _join_carries(a, b):
    if a is None or b is None:
        return a if b is None else b
    n_i, n_o = len(a.ins), len(a.outs)
    outs = list(a.outs) + [o + n_i if isinstance(o, int) else o for o in b.outs]

    def start(c_in, c_out, send_sems, recv_sems):
        a.start(c_in[:n_i], c_out[:n_o], send_sems, recv_sems)
        b.start(c_in[n_i:], c_out[n_o:], send_sems, recv_sems, base=a.n_sems)

    def finish(c_in, c_out, send_sems, recv_sems):
        a.finish(c_in[:n_i], c_out[:n_o], send_sems, recv_sems)
        b.finish(c_in[n_i:], c_out[n_o:], send_sems, recv_sems, base=a.n_sems)

    def done(res):
        a.done(res[:n_o])
        b.done(res[n_o:])

    return _Carry(a.ins + b.ins, outs, a.n_sems + b.n_sems, start, finish, done)


def _share_copies():
    def copy(out, wi, half, send_sems, recv_sems, sib):
        _, k, n, ax = BIG[wi]
        piece = _shard_piece(out[wi], k, n, ax, half)
        return _remote(piece, piece, send_sems.at[wi], recv_sems.at[wi], sib)

    def start(_, out, send_sems, recv_sems):
        x, y, c, _ = _mesh_pos()
        for wi in range(len(BIG)):
            copy(out, wi, c, send_sems, recv_sems, (x, y, 1 - c)).start()

    def finish(_, out, send_sems, recv_sems):
        x, y, c, _ = _mesh_pos()
        for wi in range(len(BIG)):
            copy(out, wi, 1 - c, send_sems, recv_sems, (x, y, 1 - c)).wait_recv()
            copy(out, wi, c, send_sems, recv_sems, (x, y, 1 - c)).wait_send()

    return start, finish, len(BIG)


def _allreduce_small(v):
    r = v.shape[0]
    rh = r // 2
    assert rh % 8 == 0

    def body(v_ref, o_ref, sib_buf, chip_buf, send_sems, recv_sems):
        x, y, c, chips = _mesh_pos()
        me = 2 * x + y
        sib = (x, y, 1 - c)
        mine = pl.ds(pl.multiple_of(c * rh, 8), rh)
        other = pl.ds(pl.multiple_of((1 - c) * rh, 8), rh)
        swap = _remote(v_ref.at[other], sib_buf, send_sems.at[0], recv_sems.at[0], sib)
        swap.start()
        swap.wait()
        chip_buf[me] = v_ref[mine, :] + sib_buf[...]
        cps = []
        for j, (qx, qy) in enumerate(chips):
            cp = _remote(chip_buf.at[me], chip_buf.at[me], send_sems.at[1 + j], recv_sems.at[1 + j], (qx, qy, c))
            cp.start()
            cps.append(cp)
        for j, (qx, qy) in enumerate(chips):
            slot = chip_buf.at[2 * qx + qy]
            _remote(slot, slot, send_sems.at[1 + j], recv_sems.at[1 + j], (qx, qy, c)).wait_recv()
        for cp in cps:
            cp.wait_send()
        o_ref[mine, :] = ((chip_buf[0] + chip_buf[1]) + chip_buf[2]) + chip_buf[3]
        back = _remote(o_ref.at[mine], o_ref.at[mine], send_sems.at[4], recv_sems.at[4], sib)
        back.start()
        _remote(o_ref.at[other], o_ref.at[other], send_sems.at[4], recv_sems.at[4], sib).wait_recv()
        back.wait_send()

    return pl.pallas_call(
        body, name="allreduce_small", in_specs=[VMEM_SPEC], out_specs=VMEM_SPEC,
        out_shape=jax.ShapeDtypeStruct((r, 128), F32),
        scratch_shapes=[pltpu.VMEM((rh, 128), F32), pltpu.VMEM((4, rh, 128), F32),
                        pltpu.SemaphoreType.DMA((5,)), pltpu.SemaphoreType.DMA((5,))],
        compiler_params=pltpu.CompilerParams(vmem_limit_bytes=VMEM_LIMIT))(v)


def _as2d(a):
    a = a.reshape((-1, a.shape[-1])) if a.ndim > 1 else a.reshape(1, -1)
    return a


def _adamw_small(quads):
    n = len(quads)

    def body(*refs):
        for i in range(n):
            w, g, m, v = (r[...] for r in refs[4 * i:4 * i + 4])
            for ref, val in zip(refs[4 * n + 3 * i:4 * n + 3 * i + 3], _adamw(w, g, m, v)):
                ref[...] = val

    return pl.pallas_call(
        body, name="adamw_small", in_specs=[VMEM_SPEC] * (4 * n), out_specs=[VMEM_SPEC] * (3 * n),
        out_shape=[jax.ShapeDtypeStruct(q[0].shape, F32) for q in quads for _ in range(3)],
        compiler_params=pltpu.CompilerParams(vmem_limit_bytes=VMEM_LIMIT))(*[a for q in quads for a in q])


def _where():
    return jnp.stack([2 * lax.axis_index("x") + lax.axis_index("y"), lax.axis_index("c")]).astype(jnp.int32)


_BIG_INDEX = {name: i for i, (name, _, _, _) in enumerate(BIG)}


class _Exchange:
    def __init__(self, inputs, where):
        self.inputs, self.where = inputs, where
        self.full, self.ready = {}, set()
        self.raw, self.got, self.parts, self.landed, self.geom = {}, {}, {}, {}, {}
        for name, k, n, ax in BIG:
            w2 = inputs[name][0]
            rs, cs = w2.shape
            tm = _tile(rs, 512)
            steps = rs // tm
            if ax == 1:
                blk, idx = (tm, cs), lambda i, w: (i, w[0])
            else:
                blk, idx = (tm, n), functools.partial(lambda i, w, steps: (w[0] * steps + i, 0), steps=steps)
            self.full[name] = _placed("cast_" + name, lambda w: w, steps, where, [(w2, (tm, cs), lambda i, w: (i, 0))],
                                      jax.ShapeDtypeStruct((k, n), MXU_DTYPE), blk, idx)

    def _gathered(self, names, outs):
        for name, o in zip(names, outs):
            self.full[name] = o
            self.ready.add(name)

    def gather_carry(self, names):
        start, finish, n_sems = _gather_copies([_BIG_INDEX[n] for n in names])
        return _Carry([self.full[n] for n in names], list(range(len(names))), n_sems, start, finish,
                      functools.partial(self._gathered, names))

    def weight(self, name):
        assert name in self.ready, name
        return self.full[name]

    def grad(self, name, g):
        self.raw[name] = g

    def _swapped(self, names, outs):
        for name, o in zip(names, outs):
            self.got[name] = o

    def _pair_sum(self, name):
        i = _BIG_INDEX[name]
        _, k, n, ax = BIG[i]
        g = self.raw[name]
        if name not in self.got:
            self._swapped([name], _reduce_swap_halves(name, [g], [i]))
        got = self.got[name]
        pr, pc = _piece_shape(k, n, ax)
        tm = _tile(pr, 512)
        spp = pr // tm
        self.geom[name] = (pr, pc, tm, spp)
        if ax == 1:
            g_idx = functools.partial(lambda i, w, spp: (w[1] * spp + i % spp, i // spp), spp=spp)
        else:
            g_idx = functools.partial(lambda i, w, spp: ((i // spp) * 2 * spp + w[1] * spp + i % spp, 0), spp=spp)
        self.parts[name] = _placed(
            "pair_sum_" + name, lambda a, b: a + b, 4 * spp, self.where,
            [(g, (tm, pc), g_idx), (got.reshape(4 * pr, pc), (tm, pc), lambda i, w: (i, 0))],
            jax.ShapeDtypeStruct((4 * pr, pc), BF16), (tm, pc), lambda i, w: (i, 0)).reshape(4, pr, pc)

    def _landed(self, names, outs):
        for name, o in zip(names, outs):
            self.landed[name] = o

    def carry(self, swap=(), ici=()):
        first = second = None
        if swap:
            widx = [_BIG_INDEX[n] for n in swap]
            start, finish, n_sems = _swap_copies(widx)
            first = _Carry([self.raw[n] for n in swap], _swap_shapes(widx), n_sems, start, finish,
                           functools.partial(self._swapped, list(swap)))
        if ici:
            for n in ici:
                self._pair_sum(n)
            start, finish, n_sems = _owner_copies(len(ici))
            parts = [self.parts[n] for n in ici]
            outs = [jax.ShapeDtypeStruct((3,) + p.shape[1:], p.dtype) for p in parts]
            second = _Carry(parts, outs, n_sems, start, finish, functools.partial(self._landed, list(ici)))
        return _join_carries(first, second)

    def _shared(self, outs):
        self.shards = dict(zip([b[0] for b in BIG], outs))

    def finish_carry(self):
        halves = []
        for name, _, _, _ in BIG:
            pr, pc, tm, spp = self.geom[name]
            ins = [(self.parts[name], (None, tm, pc), lambda i, w: (w[0], i, 0))]
            ins += [(self.landed[name], (None, tm, pc), functools.partial(lambda i, w, j: (j, i, 0), j=j))
                    for j in range(3)]
            halves.append(_placed("chip_sum_" + name,
                                  lambda a, b, c, d: ((a.astype(F32) + b.astype(F32)) + c.astype(F32)) + d.astype(F32),
                                  spp, self.where, ins, jax.ShapeDtypeStruct(self.inputs[name].shape[1:], F32), (tm, pc),
                                  functools.partial(lambda i, w, spp: (w[1] * spp + i, 0), spp=spp)))
        start, finish, n_sems = _share_copies()
        return _Carry(halves, list(range(len(halves))), n_sems, start, finish, self._shared)


def _step(inputs):
    x, mem, positions, target = inputs["x"][0], inputs["mem"][0], inputs["positions"], inputs["loss_target"][0]
    pos = positions.reshape(-1, 1)
    ex = _Exchange(inputs, _where())
    sp = {name: _as2d(inputs[name]) for name in SMALL}
    memb, = _rowwise("cast_mem", lambda m: (m,), [mem], [], [(D_MODEL, MXU_DTYPE)])

    loss, dx, gsmall = _local_step(x, memb, pos, target, sp, ex)
    gshard = ex.shards

    out = {}
    for name, _, _, _ in BIG:
        w2, m2, v2 = inputs[name][0], inputs["m_" + name][0], inputs["v_" + name][0]
        n = w2.shape[1]
        d, nm, nv = _rowwise("adamw_" + name, _adamw, [w2, gshard[name], m2, v2], [], [(n, F32)] * 3, tm=_tile(w2.shape[0], 512))
        lead = inputs[name].shape
        out[name] = (gshard[name].reshape(lead), d.reshape(lead), nm.reshape(lead), nv.reshape(lead))

    def tiles(a):
        flat = a.reshape(-1)
        n = -(-flat.shape[0] // 1024) * 1024
        return jnp.pad(flat, (0, n - flat.shape[0])).reshape(n // 128, 128)

    pieces = [tiles(loss[:, :1])] + [tiles(gsmall[name]) for name in SMALL]
    if sum(p.shape[0] for p in pieces) % 16:
        pieces.append(jnp.zeros((8, 128), F32))
    red = _allreduce_small(jnp.concatenate(pieces, axis=0))
    loss_total = red[0, 0]
    grads, off = {}, pieces[0].shape[0]
    for name, p in zip(SMALL, pieces[1:]):
        shp = _as2d(inputs[name]).shape
        grads[name] = red[off:off + p.shape[0]].reshape(-1)[:shp[0] * shp[1]].reshape(shp)
        off += p.shape[0]
    upd = _adamw_small([(_as2d(inputs[n]), grads[n], _as2d(inputs["m_" + n]), _as2d(inputs["v_" + n])) for n in SMALL])
    for i, name in enumerate(SMALL):
        shp = inputs[name].shape
        out[name] = (grads[name].reshape(shp),) + tuple(t.reshape(shp) for t in upd[3 * i:3 * i + 3])
    return loss_total, dx.reshape(inputs["x"].shape), out


_ARG_NAMES = (("x", "mem", "positions") + WEIGHT_ORDER + ("loss_target",) + tuple("m_" + n for n in WEIGHT_ORDER)
              + tuple("v_" + n for n in WEIGHT_ORDER))


def kernel(x, mem, positions, ln_in_g, ln_in_b, w_in, b_in, ssm_log_dt, ssm_a_re, ssm_a_im, ssm_b_re, ssm_b_im, ssm_c_re, ssm_c_im, ssm_d, w_glu, b_glu, w_att_up, w_mix_out, b_mix_out, ln1_g, ln1_b, w_xq, w_xkv, w_xo, ln2_g, ln2_b, w_ff1, b_ff1, w_ff2, b_ff2, ln3_g, ln3_b, loss_target, m_ln_in_g, m_ln_in_b, m_w_in, m_b_in, m_ssm_log_dt, m_ssm_a_re, m_ssm_a_im, m_ssm_b_re, m_ssm_b_im, m_ssm_c_re, m_ssm_c_im, m_ssm_d, m_w_glu, m_b_glu, m_w_att_up, m_w_mix_out, m_b_mix_out, m_ln1_g, m_ln1_b, m_w_xq, m_w_xkv, m_w_xo, m_ln2_g, m_ln2_b, m_w_ff1, m_b_ff1, m_w_ff2, m_b_ff2, m_ln3_g, m_ln3_b, v_ln_in_g, v_ln_in_b, v_w_in, v_b_in, v_ssm_log_dt, v_ssm_a_re, v_ssm_a_im, v_ssm_b_re, v_ssm_b_im, v_ssm_c_re, v_ssm_c_im, v_ssm_d, v_w_glu, v_b_glu, v_w_att_up, v_w_mix_out, v_b_mix_out, v_ln1_g, v_ln1_b, v_w_xq, v_w_xkv, v_w_xo, v_ln2_g, v_ln2_b, v_w_ff1, v_b_ff1, v_w_ff2, v_b_ff2, v_ln3_g, v_ln3_b):
    args = (x, mem, positions, ln_in_g, ln_in_b, w_in, b_in, ssm_log_dt, ssm_a_re, ssm_a_im, ssm_b_re, ssm_b_im, ssm_c_re, ssm_c_im, ssm_d, w_glu, b_glu, w_att_up, w_mix_out, b_mix_out, ln1_g, ln1_b, w_xq, w_xkv, w_xo, ln2_g, ln2_b, w_ff1, b_ff1, w_ff2, b_ff2, ln3_g, ln3_b, loss_target, m_ln_in_g, m_ln_in_b, m_w_in, m_b_in, m_ssm_log_dt, m_ssm_a_re, m_ssm_a_im, m_ssm_b_re, m_ssm_b_im, m_ssm_c_re, m_ssm_c_im, m_ssm_d, m_w_glu, m_b_glu, m_w_att_up, m_w_mix_out, m_b_mix_out, m_ln1_g, m_ln1_b, m_w_xq, m_w_xkv, m_w_xo, m_ln2_g, m_ln2_b, m_w_ff1, m_b_ff1, m_w_ff2, m_b_ff2, m_ln3_g, m_ln3_b, v_ln_in_g, v_ln_in_b, v_w_in, v_b_in, v_ssm_log_dt, v_ssm_a_re, v_ssm_a_im, v_ssm_b_re, v_ssm_b_im, v_ssm_c_re, v_ssm_c_im, v_ssm_d, v_w_glu, v_b_glu, v_w_att_up, v_w_mix_out, v_b_mix_out, v_ln1_g, v_ln1_b, v_w_xq, v_w_xkv, v_w_xo, v_ln2_g, v_ln2_b, v_w_ff1, v_b_ff1, v_w_ff2, v_b_ff2, v_ln3_g, v_ln3_b)
    assert len(args) == len(_ARG_NAMES)
    inputs = dict(zip(_ARG_NAMES, args))
    loss, dx, out = _step(inputs)
    res = [loss, dx]
    for k in range(4):
        res += [out[name][k] for name in WEIGHT_ORDER]
    return tuple(res)
```

```python
import functools
import math

import numpy as np
import jax
import jax.numpy as jnp
from jax import lax
from jax.experimental import pallas as pl
from jax.experimental.pallas import tpu as pltpu

F32 = jnp.float32
BF16 = jnp.bfloat16
MXU_DTYPE = jnp.bfloat16

D_MODEL = 1024
SSM_GROUP = 16
SSM_WIDTH = 768
SSM_GROUPS = 48
SSM_STATE = 64
N_STATE = SSM_GROUPS * SSM_STATE
SSM_CHUNKS = 6
CH_W = 128
CH_N = 512
ATT_HEAD_DIM = 64
ATT_HPG = 4
ATT_GROUPW = ATT_HPG * ATT_HEAD_DIM
DILATIONS = (1, 4, 16)
ATT_BLK = 128
ATT_SCALE = ATT_HEAD_DIM ** -0.5
ROT_DIM = 16
ROPE_THETA = 500000.0
XATT_HEADS = 4
XATT_HEAD_DIM = 256
XATT_SCALE = XATT_HEAD_DIM ** -0.5
D_FF = 4096
IN_COLS = 5120
DEEPNORM_ALPHA = 2.0 ** 0.25
LN_EPS = 1e-5
NEG_INF = -1e30
ADAM_LR = 0.001
ADAM_B1 = 0.9
ADAM_B2 = 0.999
ADAM_EPS = 1e-08
ADAM_WD = 0.01
ADAM_STEP = 10

N_SEG = 32
VMEM_LIMIT = 56 * 1024 * 1024
MESH = pl.DeviceIdType.MESH
HBM_SPEC = pl.BlockSpec(memory_space=pltpu.HBM)
VMEM_SPEC = pl.BlockSpec(memory_space=pltpu.VMEM)

BIG = (("w_in", 1024, 5120, 1), ("w_glu", 768, 2048, 1), ("w_att_up", 256, 1024, 1),
       ("w_mix_out", 1024, 1024, 0), ("w_xq", 1024, 1024, 0), ("w_xkv", 1024, 2048, 1),
       ("w_xo", 1024, 1024, 0), ("w_ff1", 1024, 4096, 1), ("w_ff2", 4096, 1024, 0))
SMALL = ("ln_in_g", "ln_in_b", "b_in", "ssm_log_dt", "ssm_a_re", "ssm_a_im", "ssm_b_re", "ssm_b_im",
         "ssm_c_re", "ssm_c_im", "ssm_d", "b_glu", "b_mix_out", "ln1_g", "ln1_b", "ln2_g", "ln2_b",
         "b_ff1", "b_ff2", "ln3_g", "ln3_b")
WEIGHT_ORDER = ("ln_in_g", "ln_in_b", "w_in", "b_in", "ssm_log_dt", "ssm_a_re", "ssm_a_im", "ssm_b_re",
                "ssm_b_im", "ssm_c_re", "ssm_c_im", "ssm_d", "w_glu", "b_glu", "w_att_up", "w_mix_out",
                "b_mix_out", "ln1_g", "ln1_b", "w_xq", "w_xkv", "w_xo", "ln2_g", "ln2_b", "w_ff1", "b_ff1",
                "w_ff2", "b_ff2", "ln3_g", "ln3_b")


def _cparams(n_axes):
    return pltpu.CompilerParams(dimension_semantics=("arbitrary",) * n_axes, vmem_limit_bytes=VMEM_LIMIT)


class _Carry:
    def __init__(self, ins, outs, n_sems, start, finish, done):
        self.ins, self.outs, self.n_sems, self.start, self.finish, self.done = ins, outs, n_sems, start, finish, done


def _call(name, body, grid, in_specs, out_specs, out_shape, args, scratch_shapes=(), carry=None):
    in_specs, out_specs, out_shape = list(in_specs), list(out_specs), list(out_shape)
    params = _cparams(len(grid))
    if carry is None:
        return pl.pallas_call(body, name=name, grid=grid, in_specs=in_specs, out_specs=out_specs, out_shape=out_shape,
                              scratch_shapes=list(scratch_shapes), compiler_params=params)(*args)
    n_in, n_out, n_ci, n_co = len(in_specs), len(out_specs), len(carry.ins), len(carry.outs)
    n_scr = len(scratch_shapes)

    def wrapped(*refs):
        ins, c_in = refs[:n_in], refs[n_in:n_in + n_ci]
        outs, c_out = refs[n_in + n_ci:n_in + n_ci + n_out], refs[n_in + n_ci + n_out:n_in + n_ci + n_out + n_co]
        scratch = refs[n_in + n_ci + n_out + n_co:n_in + n_ci + n_out + n_co + n_scr]
        send_sems, recv_sems = refs[-2:]
        ids = [pl.program_id(a) for a in range(len(grid))]
        first = functools.reduce(jnp.logical_and, [i == 0 for i in ids])
        last = functools.reduce(jnp.logical_and, [i == g - 1 for i, g in zip(ids, grid)])

        @pl.when(first)
        def _():
            carry.start(c_in, c_out, send_sems, recv_sems)

        body(*ins, *outs, *scratch)

        @pl.when(last)
        def _():
            carry.finish(c_in, c_out, send_sems, recv_sems)

    c_shapes = [jax.ShapeDtypeStruct(carry.ins[o].shape, carry.ins[o].dtype) if isinstance(o, int) else o
                for o in carry.outs]
    aliases = {n_in + o: n_out + i for i, o in enumerate(carry.outs) if isinstance(o, int)}
    res = pl.pallas_call(
        wrapped, name=name, grid=grid, in_specs=in_specs + [HBM_SPEC] * n_ci, out_specs=out_specs + [HBM_SPEC] * n_co,
        out_shape=out_shape + c_shapes, input_output_aliases=aliases,
        scratch_shapes=list(scratch_shapes) + [pltpu.SemaphoreType.DMA((carry.n_sems,))] * 2,
        compiler_params=params)(*args, *carry.ins)
    carry.done(res[n_out:])
    return res[:n_out]


def _rowwise(name, fn, rows, consts, outs, reds=(), tm=512, touts=(), carry=None):
    n_rows = (rows[0][0] if isinstance(rows[0], tuple) else rows[0]).shape[-2]
    tm = min(tm, n_rows)
    assert n_rows % tm == 0, (name, n_rows, tm)
    specs, args = [], []
    for r in rows:
        if isinstance(r, tuple) and len(r) == 3:
            arr, width, cb = r
            specs.append(pl.BlockSpec((tm, width), functools.partial(lambda i, cb: (i, cb), cb=cb)))
        elif isinstance(r, tuple):
            arr, slot = r
            specs.append(pl.BlockSpec((None, tm, arr.shape[2]), functools.partial(lambda i, s: (s, i, 0), s=slot)))
        else:
            arr = r
            specs.append(pl.BlockSpec((tm, arr.shape[1]), lambda i: (i, 0)))
        args.append(arr)
        assert arr.shape[-2] == n_rows, (name, arr.shape, n_rows)
    for cst in consts:
        specs.append(pl.BlockSpec(cst.shape, lambda i: (0, 0)))
        args.append(cst)
    n_r, n_c, n_o, n_d = len(rows), len(consts), len(outs) + len(touts), len(reds)
    out_shape = [jax.ShapeDtypeStruct((n_rows, c), dt) for c, dt in outs]
    out_specs = [pl.BlockSpec((tm, c), lambda i: (i, 0)) for c, _ in outs]
    out_shape += [jax.ShapeDtypeStruct((r, n_rows), dt) for r, dt in touts]
    out_specs += [pl.BlockSpec((r, tm), lambda i: (0, i)) for r, _ in touts]
    out_shape += [jax.ShapeDtypeStruct((1, c), F32) for c in reds]
    out_specs += [pl.BlockSpec((1, c), lambda i: (0, 0)) for c in reds]

    def body(*refs):
        ins = [r[...] for r in refs[:n_r + n_c]]
        o_refs = refs[n_r + n_c:n_r + n_c + n_o]
        d_refs = refs[n_r + n_c + n_o:]
        res = fn(*ins)
        res = res if isinstance(res, (tuple, list)) else (res,)
        assert len(res) == n_o + n_d, (name, len(res))
        for ref, val in zip(o_refs, res[:n_o]):
            ref[...] = val.astype(ref.dtype)
        first = pl.program_id(0) == 0
        for ref, val in zip(d_refs, res[n_o:]):
            @pl.when(first)
            def _(ref=ref, val=val):
                ref[...] = val

            @pl.when(jnp.logical_not(first))
            def _(ref=ref, val=val):
                ref[...] += val

    return _call(name, body, (n_rows // tm,), specs, out_specs, out_shape, args, carry=carry)


def _colsum(v):
    return jnp.sum(v.astype(F32), axis=0, keepdims=True)


_DIMS = {"nn": (((1,), (0,)), ((), ())), "nt": (((1,), (1,)), ((), ())), "tn": (((0,), (0,)), ((), ()))}


def _tile(dim, want):
    if dim <= want:
        return dim
    return max(t for t in range(128, want + 1, 128) if dim % t == 0)


def _dot(a, b, mode):
    return lax.dot_general(a.astype(MXU_DTYPE), b.astype(MXU_DTYPE), _DIMS[mode], preferred_element_type=F32)


def _mm(name, a, b, mode, *, bias=None, extras=(), epilogue=None, out_dtypes=(F32,), tm=1024, tn=1024, tk=1024,
        carry=None, colsum=False):
    if mode == "nn":
        (m, k), (_, n) = a.shape, b.shape
    elif mode == "nt":
        (m, k), (n, _) = a.shape, b.shape
    else:
        (k, m), (_, n) = a.shape, b.shape
    if k > tk:
        tk = 5 * tk
    tn = _tile(n, tn)
    tk = _tile(k, tk)
    nk = k // tk

    def vmem_bytes(rows):
        blocks = rows * tk * a.dtype.itemsize + tk * tn * b.dtype.itemsize
        blocks += sum(rows * tn * e.dtype.itemsize for e in extras)
        blocks += sum(rows * tn * jnp.dtype(dt).itemsize for dt in out_dtypes)
        return 2 * blocks + (rows * tn * 4 if nk > 1 else 0)

    tm = _tile(m, tm if mode == "tn" else 2 * tm)
    while vmem_bytes(tm) > 3 * VMEM_LIMIT // 4 and tm % 256 == 0:
        tm //= 2
    while nk == 1 and k > 1024 and (m // tm) * (n // tn) < 4 and tm % 256 == 0:
        tm //= 2
    assert m % tm == 0 and n % tn == 0 and k % tk == 0, (name, m, n, k)
    a_spec = {"nn": pl.BlockSpec((tm, tk), lambda i, j, kk: (i, kk)),
              "nt": pl.BlockSpec((tm, tk), lambda i, j, kk: (i, kk)),
              "tn": pl.BlockSpec((tk, tm), lambda i, j, kk: (kk, i))}[mode]
    b_spec = {"nn": pl.BlockSpec((tk, tn), lambda i, j, kk: (kk, j)),
              "nt": pl.BlockSpec((tn, tk), lambda i, j, kk: (j, kk)),
              "tn": pl.BlockSpec((tk, tn), lambda i, j, kk: (kk, j))}[mode]
    specs, args = [a_spec, b_spec], [a, b]
    if bias is not None:
        specs.append(pl.BlockSpec((1, tn), lambda i, j, kk: (0, j)))
        args.append(bias)
    for e in extras:
        specs.append(pl.BlockSpec((tm, tn), lambda i, j, kk: (i, j)))
        args.append(e)
    n_e, n_o = len(extras), len(out_dtypes)
    has_bias = bias is not None

    def body(*refs):
        a_ref, b_ref = refs[0], refs[1]
        pos = 2
        bias_ref = refs[pos] if has_bias else None
        pos += int(has_bias)
        e_refs = refs[pos:pos + n_e]
        o_refs = refs[pos + n_e:pos + n_e + n_o]
        sum_ref = refs[pos + n_e + n_o] if colsum else None
        acc_ref = refs[pos + n_e + n_o + int(colsum)] if nk > 1 else None
        part = _dot(a_ref[...], b_ref[...], mode)

        def finish(r):
            if has_bias:
                r = r + bias_ref[...]
            res = epilogue(r, *[e[...] for e in e_refs]) if epilogue is not None else (r,)
            for ref, val in zip(o_refs, res):
                ref[...] = val.astype(ref.dtype)
            if colsum:
                sum_ref[...] = _colsum(res[0])

        if nk == 1:
            finish(part)
        else:
            kk = pl.program_id(2)

            @pl.when(kk == 0)
            def _():
                acc_ref[...] = part

            @pl.when(kk > 0)
            def _():
                acc_ref[...] += part

            @pl.when(kk == nk - 1)
            def _():
                finish(acc_ref[...])

    out_specs = [pl.BlockSpec((tm, tn), lambda i, j, kk: (i, j)) for _ in out_dtypes]
    out_shape = [jax.ShapeDtypeStruct((m, n), dt) for dt in out_dtypes]
    if colsum:
        out_specs.append(pl.BlockSpec((None, 1, tn), lambda i, j, kk: (i, 0, j)))
        out_shape.append(jax.ShapeDtypeStruct((m // tm, 1, n), F32))
    res = _call(name, body, (m // tm, n // tn, nk), specs, out_specs, out_shape, args,
                scratch_shapes=[pltpu.VMEM((tm, tn), F32)] if nk > 1 else [], carry=carry)
    return res[0] if len(res) == 1 else res


def _ssm_wgrads(u, dy, g_re, g_im, h_re, h_im, tk=2048, carry=None):
    s = u.shape[0]
    tk = min(tk, s)
    nk = s // tk
    assert tk % N_SEG == 0

    def body(u_ref, dy_ref, gre_ref, gim_ref, hre_ref, him_ref, lre_ref, lim_ref, db_ref, dc_ref, dar_ref, dai_ref,
             pre_ref, pim_ref):
        kk = pl.program_id(1)
        u_blk, dy_blk = u_ref[...], dy_ref[...]
        g_r, g_i, h_r, h_i = gre_ref[...], gim_ref[...], hre_ref[...], him_ref[...]
        d_b = jnp.concatenate([_dot(u_blk, g_r, "tn"), _dot(u_blk, g_i, "tn")], axis=1)
        d_c = jnp.concatenate([_dot(h_r, dy_blk, "tn"), _dot(h_i, dy_blk, "tn")], axis=0)

        @pl.when(kk == 0)
        def _():
            first_row = lax.broadcasted_iota(jnp.int32, (N_SEG, CH_N), 0) == 0
            pre_ref[...] = jnp.where(first_row, 0.0, pltpu.roll(lre_ref[...], 1, 0))
            pim_ref[...] = jnp.where(first_row, 0.0, pltpu.roll(lim_ref[...], 1, 0))

        p_r = jnp.concatenate([pre_ref[...], h_r[:tk - N_SEG]], axis=0)
        p_i = jnp.concatenate([pim_ref[...], h_i[:tk - N_SEG]], axis=0)
        pre_ref[...] = h_r[tk - N_SEG:]
        pim_ref[...] = h_i[tk - N_SEG:]
        d_ar = jnp.sum(g_r * p_r + g_i * p_i, axis=0, keepdims=True)
        d_ai = jnp.sum(g_i * p_r - g_r * p_i, axis=0, keepdims=True)

        @pl.when(kk == 0)
        def _():
            db_ref[...] = d_b
            dc_ref[...] = d_c
            dar_ref[...] = d_ar
            dai_ref[...] = d_ai

        @pl.when(kk > 0)
        def _():
            db_ref[...] += d_b
            dc_ref[...] += d_c
            dar_ref[...] += d_ar
            dai_ref[...] += d_ai

    chan = pl.BlockSpec((tk, CH_W), lambda j, kk: (kk, j))
    state = pl.BlockSpec((tk, CH_N), lambda j, kk: (kk, j))
    last = pl.BlockSpec((N_SEG, CH_N), lambda j, kk: (s // N_SEG - 1, j))
    row = pl.BlockSpec((1, CH_N), lambda j, kk: (0, j))
    return _call(
        "ssm_wgrads", body, (SSM_CHUNKS, nk), [chan, chan, state, state, state, state, last, last],
        [pl.BlockSpec((None, CH_W, 2 * CH_N), lambda j, kk: (j, 0, 0)),
         pl.BlockSpec((None, 2 * CH_N, CH_W), lambda j, kk: (j, 0, 0)), row, row],
        [jax.ShapeDtypeStruct((SSM_CHUNKS, CH_W, 2 * CH_N), F32), jax.ShapeDtypeStruct((SSM_CHUNKS, 2 * CH_N, CH_W), F32),
         jax.ShapeDtypeStruct((1, N_STATE), F32), jax.ShapeDtypeStruct((1, N_STATE), F32)],
        (u, dy, g_re, g_im, h_re, h_im, h_re, h_im), scratch_shapes=[pltpu.VMEM((N_SEG, CH_N), F32)] * 2, carry=carry)


SCAN_LB = 256


def _split_by_scan_block(mat, axis):
    halves = []
    for l in range(CH_N // SCAN_LB):
        re = lax.slice_in_dim(mat, l * SCAN_LB, (l + 1) * SCAN_LB, axis=axis)
        im = lax.slice_in_dim(mat, CH_N + l * SCAN_LB, CH_N + (l + 1) * SCAN_LB, axis=axis)
        halves.append(jnp.concatenate([re, im], axis=axis))
    return jnp.stack(halves, axis=1).reshape((-1,) + halves[0].shape[1:])


def _ssm_scan(name, chan, expand12, contract12, a_re, a_im, d_row, reverse, carry=None):
    s = chan.shape[0]
    seg_len = s // N_SEG
    n_sq = int(math.log2(seg_len))
    assert 2 ** n_sq == seg_len
    rb = min(512, s)
    per_chunk = CH_N // SCAN_LB

    def body(are_ref, aim_ref, ch_ref, e_ref, k_ref, d_ref, hre_ref, him_ref, o_ref, wre_ref, wim_ref, ere, eim, cre, cim):
        e_mat, k_mat = e_ref[...], k_ref[...]
        for r in range(s // rb):
            rows = slice(r * rb, (r + 1) * rb)
            w = _dot(ch_ref[rows, :], e_mat, "nt" if reverse else "nn")
            wre_ref[rows, :] = w[:, :SCAN_LB]
            wim_ref[rows, :] = w[:, SCAN_LB:]

        ar1 = are_ref[...]
        ai1 = -aim_ref[...] if reverse else aim_ref[...]
        ar = jnp.broadcast_to(ar1, (N_SEG, SCAN_LB))
        ai = jnp.broadcast_to(ai1, (N_SEG, SCAN_LB))

        def rows_of(k):
            kk = seg_len - 1 - k if reverse else k
            return pl.ds(pl.multiple_of(kk * N_SEG, N_SEG), N_SEG)

        def local(k, carry):
            hr, hi = carry
            rows = rows_of(k)
            nr = ar * hr - ai * hi + wre_ref[rows, :]
            ni = ar * hi + ai * hr + wim_ref[rows, :]
            hre_ref[rows, :] = nr
            him_ref[rows, :] = ni
            return nr, ni

        zero = jnp.zeros((N_SEG, SCAN_LB), F32)
        er, ei = lax.fori_loop(0, seg_len, local, (zero, zero))
        ere[...] = er
        eim[...] = ei
        pr, pi = ar1, ai1
        for _ in range(n_sq):
            pr, pi = pr * pr - pi * pi, 2.0 * pr * pi
        cr = jnp.zeros((1, SCAN_LB), F32)
        ci = jnp.zeros((1, SCAN_LB), F32)
        for jj in range(N_SEG):
            j = N_SEG - 1 - jj if reverse else jj
            cre[j:j + 1, :] = cr
            cim[j:j + 1, :] = ci
            er_j, ei_j = ere[j:j + 1, :], eim[j:j + 1, :]
            cr, ci = pr * cr - pi * ci + er_j, pr * ci + pi * cr + ei_j
        c_r, c_i = cre[...], cim[...]

        def fix(k, carry):
            qr, qi = carry
            rows = rows_of(k)
            hre_ref[rows, :] = hre_ref[rows, :] + (qr * c_r - qi * c_i)
            him_ref[rows, :] = him_ref[rows, :] + (qr * c_i + qi * c_r)
            return qr * ar - qi * ai, qr * ai + qi * ar

        lax.fori_loop(0, seg_len, fix, (ar, ai))

        first_of_chunk = lax.rem(pl.program_id(0), per_chunk) == 0
        for r in range(s // rb):
            rows = slice(r * rb, (r + 1) * rb)
            h_cat = jnp.concatenate([hre_ref[rows, :], him_ref[rows, :]], axis=1)
            part = _dot(h_cat, k_mat, "nt" if reverse else "nn")

            @pl.when(first_of_chunk)
            def _(rows=rows, part=part):
                o_ref[rows, :] = part + d_ref[...] * ch_ref[rows, :]

            @pl.when(jnp.logical_not(first_of_chunk))
            def _(rows=rows, part=part):
                o_ref[rows, :] += part

    nblk = N_STATE // SCAN_LB
    blk = pl.BlockSpec((s, SCAN_LB), lambda b: (0, b))
    row = pl.BlockSpec((1, SCAN_LB), lambda b: (0, b))
    chan_blk = pl.BlockSpec((s, CH_W), lambda b: (0, b // per_chunk))
    res = _call(name, body, (nblk,),
                [row, row, chan_blk, pl.BlockSpec((None,) + expand12.shape[1:], lambda b: (b, 0, 0)),
                 pl.BlockSpec((None,) + contract12.shape[1:], lambda b: (b, 0, 0)),
                 pl.BlockSpec((1, CH_W), lambda b: (0, b // per_chunk))],
                [blk, blk, chan_blk],
                [jax.ShapeDtypeStruct((s, N_STATE), F32)] * 2 + [jax.ShapeDtypeStruct((s, SSM_WIDTH), F32)],
                (a_re, a_im, chan, expand12, contract12, d_row),
                scratch_shapes=[pltpu.VMEM((s, SCAN_LB), F32)] * 2 + [pltpu.VMEM((N_SEG, SCAN_LB), F32)] * 4, carry=carry)
    return res[0], res[1], res[2]


def _disc(ldt, are, aim, bre, bim):
    dt = jnp.exp(ldt)
    mag = jnp.exp(are * dt)
    abr = mag * jnp.cos(aim * dt)
    abi = mag * jnp.sin(aim * dt)
    den = jnp.square(are) + jnp.square(aim)
    nr = abr - 1.0
    fre = (nr * are + abi * aim) / den
    fim = (abi * are - nr * aim) / den
    return abr, abi, fre * bre - fim * bim, fre * bim + fim * bre


def _ssm_disc_fwd(ldt, are, aim, bre, bim):
    def body(l_ref, ar_ref, ai_ref, br_ref, bi_ref, o0, o1, o2, o3):
        res = _disc(l_ref[...], ar_ref[...], ai_ref[...], br_ref[...], bi_ref[...])
        for ref, val in zip((o0, o1, o2, o3), res):
            ref[...] = val

    col = jax.ShapeDtypeStruct((N_STATE, 1), F32)
    mat = jax.ShapeDtypeStruct((N_STATE, SSM_GROUP), F32)
    return pl.pallas_call(body, name="ssm_disc_fwd", out_shape=[col, col, mat, mat],
                          in_specs=[VMEM_SPEC] * 5, out_specs=[VMEM_SPEC] * 4)(ldt, are, aim, bre, bim)


def _ssm_disc_bwd(ldt, are, aim, bre, bim, d_abr, d_abi, d_bbr, d_bbi):
    def body(l_ref, ar_ref, ai_ref, br_ref, bi_ref, c0, c1, c2, c3, g_ldt, g_are, g_aim, g_bre, g_bim):
        _, vjp = jax.vjp(_disc, l_ref[...], ar_ref[...], ai_ref[...], br_ref[...], bi_ref[...])
        dl, dar, dai, dbr, dbi = vjp((c0[...], c1[...], c2[...], c3[...]))
        state = lax.broadcasted_iota(jnp.int32, (N_STATE, SSM_GROUPS), 0)
        group = lax.broadcasted_iota(jnp.int32, (N_STATE, SSM_GROUPS), 1)
        pick = jnp.right_shift(state, 6) == group
        g_ldt[...] = jnp.sum(jnp.where(pick, dl, 0.0), axis=0, keepdims=True)
        g_are[...] = dar
        g_aim[...] = dai
        g_bre[...] = dbr
        g_bim[...] = dbi

    col = jax.ShapeDtypeStruct((N_STATE, 1), F32)
    mat = jax.ShapeDtypeStruct((N_STATE, SSM_GROUP), F32)
    return pl.pallas_call(body, name="ssm_disc_bwd",
                          out_shape=[jax.ShapeDtypeStruct((1, SSM_GROUPS), F32), col, col, mat, mat],
                          in_specs=[VMEM_SPEC] * 9, out_specs=[VMEM_SPEC] * 5,
                          compiler_params=pltpu.CompilerParams(vmem_limit_bytes=VMEM_LIMIT))(
        ldt, are, aim, bre, bim, d_abr, d_abi, d_bbr, d_bbi)


_EYE8 = np.eye(8, dtype=np.float32)


def _blockdiag_b(bb):
    t = bb.reshape(SSM_CHUNKS, 8, SSM_STATE, SSM_GROUP).transpose(0, 1, 3, 2)
    return jnp.einsum("igcn,gh->igchn", t, _EYE8).reshape(SSM_CHUNKS, CH_W, CH_N)


def _diag_of_b(m):
    t = jnp.einsum("igchn,gh->igcn", m.reshape(SSM_CHUNKS, 8, SSM_GROUP, 8, SSM_STATE), _EYE8)
    return t.transpose(0, 1, 3, 2).reshape(N_STATE, SSM_GROUP)


def _blockdiag_c(c):
    t = c.reshape(SSM_CHUNKS, 8, SSM_GROUP, SSM_STATE).transpose(0, 1, 3, 2)
    return jnp.einsum("ignc,gh->ignhc", t, _EYE8).reshape(SSM_CHUNKS, CH_N, CH_W)


def _diag_of_c(m):
    t = jnp.einsum("ignhc,gh->ignc", m.reshape(SSM_CHUNKS, 8, SSM_STATE, 8, SSM_GROUP), _EYE8)
    return t.transpose(0, 1, 3, 2).reshape(SSM_GROUPS, SSM_GROUP, SSM_STATE)


def _time_perm(a):
    s, c = a.shape
    return a.reshape(N_SEG, s // N_SEG, c).transpose(1, 0, 2).reshape(s, c)


def _time_unperm(a):
    s, c = a.shape
    return a.reshape(s // N_SEG, N_SEG, c).transpose(1, 0, 2).reshape(s, c)


def _dilate(a, d):
    s, c = a.shape
    return a if d == 1 else a.reshape(s // d, d, c).transpose(1, 0, 2).reshape(s, c)


def _undilate(a, d):
    s, c = a.shape
    return a if d == 1 else a.reshape(d, s // d, c).transpose(1, 0, 2).reshape(s, c)


def _dilate_rows(a, d):
    r, s = a.shape
    return a if d == 1 else a.reshape(r, s // d, d).transpose(0, 2, 1).reshape(r, s)


ATT_T_FWD = 4
ATT_T_BWD = 8


def _window(prev_ref, cur_ref, i, sl):
    if i == 0:
        return jnp.concatenate([prev_ref[:, sl], cur_ref[0:ATT_BLK, sl]], axis=0)
    return cur_ref[(i - 1) * ATT_BLK:(i + 1) * ATT_BLK, sl]


def _band_valid(first_key):
    qi = lax.broadcasted_iota(jnp.int32, (ATT_BLK, 2 * ATT_BLK), 0)
    ki = lax.broadcasted_iota(jnp.int32, (ATT_BLK, 2 * ATT_BLK), 1)
    steps = qi + ATT_BLK - ki
    return (steps >= 0) & (steps <= ATT_BLK) & (ki >= first_key)


ATT_STATW = ATT_HPG * 128


def _stat(h):
    return slice(h * 128, (h + 1) * 128)


def _stat_rows(stat):
    n = stat.shape[0]
    heads = [stat[:, _stat(h)].T[0:1, :] for h in range(ATT_HPG)]
    return jnp.concatenate(heads + [jnp.zeros((8 - ATT_HPG, n), stat.dtype)], axis=0)


def _attn_specs(nb, t, width=ATT_GROUPW):
    cur = pl.BlockSpec((t * ATT_BLK, width), lambda b: (b, 0))
    prev = pl.BlockSpec((ATT_BLK, width), lambda b: (jnp.maximum(b * t - 1, 0), 0))
    nxt = pl.BlockSpec((ATT_BLK, width), lambda b: (jnp.minimum((b + 1) * t, nb - 1), 0))
    return cur, prev, nxt


def _attn_fwd(tag, per_seq, q, k, v):
    s = q.shape[0]
    nb = s // ATT_BLK

    def body(q_ref, kc_ref, kp_ref, vc_ref, vp_ref, o_ref, lse_ref):
        bt = pl.program_id(0)
        for i in range(ATT_T_FWD):
            has_prev = lax.rem(bt * ATT_T_FWD + i, per_seq) > 0
            valid = _band_valid(jnp.where(has_prev, 0, ATT_BLK))
            rows = slice(i * ATT_BLK, (i + 1) * ATT_BLK)
            for h in range(ATT_HPG):
                sl = slice(h * ATT_HEAD_DIM, (h + 1) * ATT_HEAD_DIM)
                kcat = _window(kp_ref, kc_ref, i, sl)
                vcat = _window(vp_ref, vc_ref, i, sl)
                sc = _dot(q_ref[rows, sl], kcat, "nt") * ATT_SCALE
                sc = jnp.where(valid, sc, NEG_INF)
                m = jnp.max(sc, axis=-1, keepdims=True)
                p = jnp.exp(sc - m)
                den = jnp.sum(p, axis=-1, keepdims=True)
                o_ref[rows, sl] = _dot(p, vcat, "nn") / den
                lse_ref[rows, _stat(h)] = jnp.broadcast_to(m + jnp.log(den), (ATT_BLK, 128))

    cur, prev, _ = _attn_specs(nb, ATT_T_FWD)
    stat, _, _ = _attn_specs(nb, ATT_T_FWD, ATT_STATW)
    return pl.pallas_call(
        body, name="attn_fwd_" + tag, grid=(nb // ATT_T_FWD,), in_specs=[cur, cur, prev, cur, prev], out_specs=[cur, stat],
        out_shape=[jax.ShapeDtypeStruct((s, ATT_GROUPW), F32), jax.ShapeDtypeStruct((s, ATT_STATW), F32)],
        compiler_params=_cparams(1))(q, k, k, v, v)


def _attn_dq(tag, per_seq, q, k, v, do, lse, delta):
    s = q.shape[0]
    nb = s // ATT_BLK

    def body(q_ref, kc_ref, kp_ref, vc_ref, vp_ref, do_ref, lse_ref, dl_ref, dq_ref):
        bt = pl.program_id(0)
        for i in range(ATT_T_BWD):
            has_prev = lax.rem(bt * ATT_T_BWD + i, per_seq) > 0
            valid = _band_valid(jnp.where(has_prev, 0, ATT_BLK))
            rows = slice(i * ATT_BLK, (i + 1) * ATT_BLK)
            for h in range(ATT_HPG):
                sl = slice(h * ATT_HEAD_DIM, (h + 1) * ATT_HEAD_DIM)
                kcat = _window(kp_ref, kc_ref, i, sl)
                vcat = _window(vp_ref, vc_ref, i, sl)
                lse = jnp.concatenate([lse_ref[rows, _stat(h)]] * 2, axis=1)
                dlt = jnp.concatenate([dl_ref[rows, _stat(h)]] * 2, axis=1)
                sc = _dot(q_ref[rows, sl], kcat, "nt") * ATT_SCALE
                p = jnp.exp(jnp.where(valid, sc, NEG_INF) - lse)
                dp = _dot(do_ref[rows, sl], vcat, "nt")
                ds = p * (dp - dlt) * ATT_SCALE
                dq_ref[rows, sl] = _dot(ds, kcat, "nn")

    cur, prev, _ = _attn_specs(nb, ATT_T_BWD)
    stat, _, _ = _attn_specs(nb, ATT_T_BWD, ATT_STATW)
    return pl.pallas_call(
        body, name="attn_dq_" + tag, grid=(nb // ATT_T_BWD,), in_specs=[cur, cur, prev, cur, prev, cur, stat, stat],
        out_specs=cur, out_shape=jax.ShapeDtypeStruct((s, ATT_GROUPW), F32),
        compiler_params=_cparams(1))(q, k, k, v, v, do, lse, delta)


def _attn_dkv(tag, per_seq, q, k, v, do, lse_t, delta_t):
    s = q.shape[0]
    nb = s // ATT_BLK

    def body(k_ref, v_ref, qc_ref, qn_ref, doc_ref, don_ref, lc_ref, ln_ref, dc_ref, dn_ref, dk_ref, dv_ref):
        bt = pl.program_id(0)
        ki = lax.broadcasted_iota(jnp.int32, (ATT_BLK, 2 * ATT_BLK), 0)
        ci = lax.broadcasted_iota(jnp.int32, (ATT_BLK, 2 * ATT_BLK), 1)

        def pair(edge_ref, cur_ref, i, sl):
            if i == ATT_T_BWD - 1:
                return jnp.concatenate([cur_ref[i * ATT_BLK:(i + 1) * ATT_BLK, sl], edge_ref[:, sl]], axis=0)
            return cur_ref[i * ATT_BLK:(i + 2) * ATT_BLK, sl]

        def pair_row(edge_ref, cur_ref, i, h):
            if i == ATT_T_BWD - 1:
                row = jnp.concatenate([cur_ref[h:h + 1, i * ATT_BLK:(i + 1) * ATT_BLK], edge_ref[h:h + 1, :]], axis=1)
            else:
                row = cur_ref[h:h + 1, i * ATT_BLK:(i + 2) * ATT_BLK]
            return jnp.broadcast_to(row, (ATT_BLK, 2 * ATT_BLK))

        for i in range(ATT_T_BWD):
            b = bt * ATT_T_BWD + i
            next_uses = (b + 1 < nb) & (lax.rem(b + 1, per_seq) > 0)
            reach = jnp.where(next_uses, 0, 4 * ATT_BLK)
            valid = ((ci < ATT_BLK) & (ci >= ki)) | ((ci >= ATT_BLK) & (ki - ci + ATT_BLK >= reach))
            rows = slice(i * ATT_BLK, (i + 1) * ATT_BLK)
            for h in range(ATT_HPG):
                sl = slice(h * ATT_HEAD_DIM, (h + 1) * ATT_HEAD_DIM)
                qcat, docat = pair(qn_ref, qc_ref, i, sl), pair(don_ref, doc_ref, i, sl)
                sc = _dot(k_ref[rows, sl], qcat, "nt") * ATT_SCALE
                p = jnp.exp(jnp.where(valid, sc, NEG_INF) - pair_row(ln_ref, lc_ref, i, h))
                dv_ref[rows, sl] = _dot(p, docat, "nn")
                dp = _dot(v_ref[rows, sl], docat, "nt")
                ds = p * (dp - pair_row(dn_ref, dc_ref, i, h)) * ATT_SCALE
                dk_ref[rows, sl] = _dot(ds, qcat, "nn")

    cur, _, nxt = _attn_specs(nb, ATT_T_BWD)
    stat = pl.BlockSpec((8, ATT_T_BWD * ATT_BLK), lambda b: (0, b))
    snxt = pl.BlockSpec((8, ATT_BLK), lambda b: (0, jnp.minimum((b + 1) * ATT_T_BWD, nb - 1)))
    return pl.pallas_call(
        body, name="attn_dkv_" + tag, grid=(nb // ATT_T_BWD,), in_specs=[cur, cur, cur, nxt, cur, nxt, stat, snxt, stat, snxt],
        out_specs=[cur, cur], out_shape=[jax.ShapeDtypeStruct((s, ATT_GROUPW), F32)] * 2,
        compiler_params=_cparams(1))(k, v, q, q, do, do, lse_t, lse_t, delta_t, delta_t)


def _xattn_probs(q, kh):
    sc = _dot(q, kh, "nt") * XATT_SCALE
    e = jnp.exp(sc - jnp.max(sc, axis=-1, keepdims=True))
    return e / jnp.sum(e, axis=-1, keepdims=True)


def _xattn_fwd(q, kv, tm=512):
    s = q.shape[0]
    tm = min(tm, s)

    def body(q_ref, kv_ref, o_ref):
        for h in range(XATT_HEADS):
            sl = slice(h * XATT_HEAD_DIM, (h + 1) * XATT_HEAD_DIM)
            vs = slice(D_MODEL + h * XATT_HEAD_DIM, D_MODEL + (h + 1) * XATT_HEAD_DIM)
            p = _xattn_probs(q_ref[:, sl], kv_ref[:, sl])
            o_ref[:, sl] = _dot(p, kv_ref[:, vs], "nn").astype(o_ref.dtype)

    return pl.pallas_call(
        body, name="xattn_fwd", grid=(s // tm,),
        in_specs=[pl.BlockSpec((tm, D_MODEL), lambda i: (i, 0)), pl.BlockSpec(kv.shape, lambda i: (0, 0))],
        out_specs=pl.BlockSpec((tm, D_MODEL), lambda i: (i, 0)),
        out_shape=jax.ShapeDtypeStruct((s, D_MODEL), MXU_DTYPE), compiler_params=_cparams(1))(q, kv)


def _xattn_bwd(q, kv, do, tm=1024):
    s = q.shape[0]
    tm = min(tm, s)

    def body(q_ref, kv_ref, do_ref, dq_ref, dkv_ref):
        first = pl.program_id(0) == 0

        @pl.when(first)
        def _():
            dkv_ref[...] = jnp.zeros_like(dkv_ref)

        for h in range(XATT_HEADS):
            sl = slice(h * XATT_HEAD_DIM, (h + 1) * XATT_HEAD_DIM)
            vs = slice(D_MODEL + h * XATT_HEAD_DIM, D_MODEL + (h + 1) * XATT_HEAD_DIM)
            p = _xattn_probs(q_ref[:, sl], kv_ref[:, sl])
            dkv_ref[:, vs] += _dot(p, do_ref[:, sl], "tn")
            dp = _dot(do_ref[:, sl], kv_ref[:, vs], "nt")
            ds = p * (dp - jnp.sum(dp * p, axis=-1, keepdims=True)) * XATT_SCALE
            dq_ref[:, sl] = _dot(ds, kv_ref[:, sl], "nn").astype(dq_ref.dtype)
            dkv_ref[:, sl] += _dot(ds, q_ref[:, sl], "tn")

    row = pl.BlockSpec((tm, D_MODEL), lambda i: (i, 0))
    whole = pl.BlockSpec(kv.shape, lambda i: (0, 0))
    return pl.pallas_call(
        body, name="xattn_bwd", grid=(s // tm,), in_specs=[row, whole, row], out_specs=[row, whole],
        out_shape=[jax.ShapeDtypeStruct((s, D_MODEL), MXU_DTYPE), jax.ShapeDtypeStruct(kv.shape, F32)],
        compiler_params=_cparams(1))(q, kv, do)


def _ln(x, g, b):
    mu = jnp.mean(x, axis=-1, keepdims=True)
    xc = x - mu
    var = jnp.mean(jnp.square(xc), axis=-1, keepdims=True)
    return xc * lax.rsqrt(var + LN_EPS) * g + b


def _res_ln(h, o, g, b):
    return _ln(DEEPNORM_ALPHA * h + o, g, b)


def _gate(gs, ga, z1, z2, batt):
    return jax.nn.sigmoid(gs) * (z1 * jax.nn.sigmoid(z2)) + jax.nn.sigmoid(ga) * batt


ROPE_TW = 2 * ATT_HEAD_DIM


def _rope_tables(pos, invf, m1, m2):
    ang = pos.astype(F32) * invf
    sin = jnp.sin(ang)
    return jnp.cos(ang), -sin * m1, sin * m2


def _widen(tab):
    return jnp.concatenate([tab] * (ATT_GROUPW // ROPE_TW), axis=1)


def _rope(t, cos, s_up, s_dn):
    w = t.shape[-1]
    return t * cos + pltpu.roll(t, w - ROT_DIM // 2, 1) * s_up + pltpu.roll(t, ROT_DIM // 2, 1) * s_dn


def _rope_t(dt, cos, s_up, s_dn):
    w = dt.shape[-1]
    return dt * cos + pltpu.roll(dt * s_up, ROT_DIM // 2, 1) + pltpu.roll(dt * s_dn, w - ROT_DIM // 2, 1)


def _rope_consts():
    inv_freq = ROPE_THETA ** (-jnp.arange(0, ROT_DIM, 2, dtype=F32) / ROT_DIM)
    d = np.arange(ROPE_TW) % ATT_HEAD_DIM
    invf = jnp.where(d < ROT_DIM, inv_freq[d % (ROT_DIM // 2)], 0.0).reshape(1, ROPE_TW).astype(F32)
    m1 = jnp.asarray((d < ROT_DIM // 2).astype(np.float32)).reshape(1, ROPE_TW)
    m2 = jnp.asarray(((d >= ROT_DIM // 2) & (d < ROT_DIM)).astype(np.float32)).reshape(1, ROPE_TW)
    return invf, m1, m2


def _head_sum_matrix():
    d = np.arange(ATT_GROUPW) // ATT_HEAD_DIM
    s = np.arange(ATT_STATW) // 128
    return jnp.asarray((d[:, None] == s[None, :]).astype(np.float32))


def _adamw(w, g, m, v):
    m = ADAM_B1 * m + (1.0 - ADAM_B1) * g
    v = ADAM_B2 * v + (1.0 - ADAM_B2) * jnp.square(g)
    m_hat = m / (1.0 - ADAM_B1 ** ADAM_STEP)
    v_hat = v / (1.0 - ADAM_B2 ** ADAM_STEP)
    delta = -ADAM_LR * (m_hat / (jnp.sqrt(v_hat) + ADAM_EPS) + ADAM_WD * w)
    return delta, m, v


def _local_step(x, mem, pos, target, sp, ex):
    s = x.shape[0]
    al = DEEPNORM_ALPHA
    mx = MXU_DTYPE

    h0, h0b = _rowwise("ln_in", lambda x, g, b: (lambda h: (h, h))(_ln(x, g, b)), [x],
                       [sp["ln_in_g"], sp["ln_in_b"]], [(D_MODEL, F32), (D_MODEL, mx)],
                       carry=ex.gather_carry(["w_in"]))
    proj = _mm("proj", h0b, ex.weight("w_in"), "nn", bias=sp["b_in"],
               carry=ex.gather_carry(["w_glu", "w_att_up", "w_mix_out", "w_xq"]))

    ldt = jnp.repeat(sp["ssm_log_dt"].reshape(SSM_GROUPS), SSM_STATE).reshape(N_STATE, 1)
    are, aim = sp["ssm_a_re"].reshape(N_STATE, 1), sp["ssm_a_im"].reshape(N_STATE, 1)
    bre, bim = sp["ssm_b_re"].reshape(N_STATE, SSM_GROUP), sp["ssm_b_im"].reshape(N_STATE, SSM_GROUP)
    abr, abi, bbr, bbi = _ssm_disc_fwd(ldt, are, aim, bre, bim)
    a_re, a_im = abr.reshape(1, N_STATE), abi.reshape(1, N_STATE)
    bexp = jnp.concatenate([_blockdiag_b(bbr), _blockdiag_b(bbi)], axis=2).astype(mx)
    cexp = jnp.concatenate([_blockdiag_c(sp["ssm_c_re"].reshape(SSM_GROUPS, SSM_GROUP, SSM_STATE)),
                            -_blockdiag_c(sp["ssm_c_im"].reshape(SSM_GROUPS, SSM_GROUP, SSM_STATE))],
                           axis=1).astype(mx)
    u_p = _time_perm(proj[:, :SSM_WIDTH])
    b12, c12 = _split_by_scan_block(bexp, 2), _split_by_scan_block(cexp, 1)
    h_re, h_im, y_p = _ssm_scan("ssm_scan_fwd", u_p, b12, c12, a_re, a_im, sp["ssm_d"], reverse=False,
                                carry=ex.gather_carry(["w_xkv", "w_xo", "w_ff1", "w_ff2"]))
    y = _time_unperm(y_p)
    ygb, = _rowwise("gelu", lambda y: jax.nn.gelu(y), [y], [], [(SSM_WIDTH, mx)])
    z = _mm("glu", ygb, ex.weight("w_glu"), "nn", bias=sp["b_glu"])

    invf, m1, m2 = _rope_consts()

    def rope_fwd(pos, q0, q1, q2, k0, k1, k2, v0, v1, v2, invf, m1, m2):
        narrow = _rope_tables(pos, invf, m1, m2)
        tabs = [_widen(t) for t in narrow]
        return tuple(_rope(t, *tabs) for t in (q0, q1, q2, k0, k1, k2)) + (v0, v1, v2) + tuple(narrow)

    qkv_cols = [(proj, ATT_GROUPW, 3 + i) for i in range(9)]
    qkv = _rowwise("rope", rope_fwd, [pos] + qkv_cols, [invf, m1, m2], [(ATT_GROUPW, mx)] * 9 + [(ROPE_TW, F32)] * 3)
    rope_tabs = qkv[9:]
    n_blocks = s // ATT_BLK
    groups = [(str(g), n_blocks // d, d) for g, d in enumerate(DILATIONS)]
    q_d = [_dilate(qkv[g], d) for g, d in enumerate(DILATIONS)]
    k_d = [_dilate(qkv[3 + g], d) for g, d in enumerate(DILATIONS)]
    v_d = [_dilate(qkv[6 + g], d) for g, d in enumerate(DILATIONS)]
    o_g, l_g = [], []
    for g, (tag, per_seq, d) in enumerate(groups):
        o, lse = _attn_fwd(tag, per_seq, q_d[g], k_d[g], v_d[g])
        o_g.append(_undilate(o, d))
        l_g.append(_undilate(lse, d))

    def merge(o0, o1, o2, l0, l1, l2):
        m = jnp.maximum(jnp.maximum(l0, l1), l2)
        e0, e1, e2 = jnp.exp(l0 - m), jnp.exp(l1 - m), jnp.exp(l2 - m)
        tot = e0 + e1 + e2

        def per_dim(e):
            w = e / tot
            return jnp.concatenate([w[:, h * 128:h * 128 + ATT_HEAD_DIM] for h in range(ATT_HPG)], axis=1)

        att = per_dim(e0) * o0 + per_dim(e1) * o1 + per_dim(e2) * o2
        lse = m + jnp.log(tot)
        return att, att, lse, _stat_rows(lse)

    att, attb, lse_tot, lse_tot_t = _rowwise("attn_merge", merge, o_g + l_g, [],
                                             [(ATT_GROUPW, F32), (ATT_GROUPW, mx), (ATT_STATW, F32)], touts=[(8, F32)])
    batt = _mm("att_up", attb, ex.weight("w_att_up"), "nn")

    gate_rows = [(proj, D_MODEL, 3), (proj, D_MODEL, 4), (z, D_MODEL, 0), (z, D_MODEL, 1), batt]
    mixedb, = _rowwise("gate", _gate, gate_rows, [], [(D_MODEL, mx)])
    o1 = _mm("mix_out", mixedb, ex.weight("w_mix_out"), "nn", bias=sp["b_mix_out"])
    h1, h1b = _rowwise("ln1", lambda h, o, g, b: (lambda r: (r, r))(_res_ln(h, o, g, b)), [h0, o1],
                       [sp["ln1_g"], sp["ln1_b"]], [(D_MODEL, F32), (D_MODEL, mx)])

    qx = _mm("xq", h1b, ex.weight("w_xq"), "nn", out_dtypes=(mx,))
    kvx = _mm("xkv", mem, ex.weight("w_xkv"), "nn", out_dtypes=(mx,))
    oxb = _xattn_fwd(qx, kvx)
    o2 = _mm("xo", oxb, ex.weight("w_xo"), "nn")
    h2, h2b = _rowwise("ln2", lambda h, o, g, b: (lambda r: (r, r))(_res_ln(h, o, g, b)), [h1, o2],
                       [sp["ln2_g"], sp["ln2_b"]], [(D_MODEL, F32), (D_MODEL, mx)])

    a_ff, fb = _mm("ff1", h2b, ex.weight("w_ff1"), "nn", bias=sp["b_ff1"],
                   epilogue=lambda r: (r, jnp.square(jnp.maximum(r, 0.0))), out_dtypes=(F32, mx))
    o3 = _mm("ff2", fb, ex.weight("w_ff2"), "nn", bias=sp["b_ff2"])

    def loss_bwd(h2, o3, tgt, g, b):
        def f(h2, o3, g, b):
            h3 = _res_ln(h2, o3, g, b)
            return 0.5 * jnp.sum(jnp.mean(jnp.square(h3 - tgt), axis=-1))

        loss, vjp = jax.vjp(f, h2, o3, g, b)
        _, dr, dg, db = vjp(jnp.ones((), F32))
        return dr, dr, dg, db, _colsum(dr), jnp.full((1, 128), loss, F32)

    dr3, dr3b, g_ln3_g, g_ln3_b, g_b_ff2, loss = _rowwise(
        "loss_ln3_bwd", loss_bwd, [h2, o3, target], [sp["ln3_g"], sp["ln3_b"]],
        [(D_MODEL, F32), (D_MODEL, mx)], [D_MODEL, D_MODEL, D_MODEL, 128])

    dab, da_sums = _mm("ff2_dx", dr3b, ex.weight("w_ff2"), "nt", extras=(a_ff,),
                       epilogue=lambda r, a: (r * (2.0 * jnp.maximum(a, 0.0)),), out_dtypes=(mx,), colsum=True)
    g_b_ff1 = jnp.sum(da_sums, axis=0)
    ex.grad("w_ff2", _mm("ff2_dw", fb, dr3b, "tn"))
    ex.grad("w_ff1", _mm("ff1_dw", h2b, dab, "tn", carry=ex.carry(swap=["w_ff2"])))
    dh2 = _mm("ff1_dx", dab, ex.weight("w_ff1"), "nt", extras=(dr3,), epilogue=lambda r, d: (r + al * d,),
              carry=ex.carry(swap=["w_ff1"]))

    def ln_bwd(h, o, dout, g, b):
        _, vjp = jax.vjp(_res_ln, h, o, g, b)
        _, dr, dg, db = vjp(dout)
        return dr, dr, dg, db, _colsum(dr)

    dr2, dr2b, g_ln2_g, g_ln2_b, _ = _rowwise(
        "ln2_bwd", ln_bwd, [h1, o2, dh2], [sp["ln2_g"], sp["ln2_b"]],
        [(D_MODEL, F32), (D_MODEL, mx)], [D_MODEL, D_MODEL, D_MODEL])
    ex.grad("w_xo", _mm("xo_dw", oxb, dr2b, "tn"))
    doxb = _mm("xo_dx", dr2b, ex.weight("w_xo"), "nt", out_dtypes=(mx,))
    dqxb, dkvx = _xattn_bwd(qx, kvx, doxb)
    ex.grad("w_xq", _mm("xq_dw", h1b, dqxb, "tn"))
    dh1 = _mm("xq_dx", dqxb, ex.weight("w_xq"), "nt", extras=(dr2,), epilogue=lambda r, d: (r + al * d,))
    ex.grad("w_xkv", _mm("xkv_dw", mem, dkvx, "tn"))

    dr1, dr1b, g_ln1_g, g_ln1_b, g_b_mix = _rowwise(
        "ln1_bwd", ln_bwd, [h0, o1, dh1], [sp["ln1_g"], sp["ln1_b"]],
        [(D_MODEL, F32), (D_MODEL, mx)], [D_MODEL, D_MODEL, D_MODEL])
    ex.grad("w_mix_out", _mm("mix_dw", mixedb, dr1b, "tn"))
    dmixed = _mm("mix_dx", dr1b, ex.weight("w_mix_out"), "nt")

    def gate_bwd(gs, ga, z1, z2, batt, dm):
        _, vjp = jax.vjp(_gate, gs, ga, z1, z2, batt)
        dgs, dga, dz1, dz2, dbatt = vjp(dm)
        dz = jnp.concatenate([dz1, dz2], axis=-1)
        return dgs, dga, dz, dbatt, _colsum(dz)

    dgsb, dgab, dzb, dbattb, g_b_glu = _rowwise(
        "gate_bwd", gate_bwd, gate_rows + [dmixed], [],
        [(D_MODEL, mx), (D_MODEL, mx), (2 * D_MODEL, mx), (D_MODEL, mx)], [2 * D_MODEL])
    ex.grad("w_att_up", _mm("att_up_dw", attb, dbattb, "tn"))
    datt = _mm("att_up_dx", dbattb, ex.weight("w_att_up"), "nt")

    def att_delta(datt, att, hs):
        dl = jnp.dot(datt * att, hs, precision=lax.Precision.HIGHEST, preferred_element_type=F32)
        return datt, dl, _stat_rows(dl)

    dattb, delta, delta_t = _rowwise("attn_delta", att_delta, [datt, att], [_head_sum_matrix()],
                                     [(ATT_GROUPW, mx), (ATT_STATW, F32)], touts=[(8, F32)])
    dq_g, dk_g, dv_g = [], [], []
    for g, (tag, per_seq, d) in enumerate(groups):
        do_d, lt_d, dl_d = _dilate(dattb, d), _dilate(lse_tot, d), _dilate(delta, d)
        dq_g.append(_undilate(_attn_dq(tag, per_seq, q_d[g], k_d[g], v_d[g], do_d, lt_d, dl_d), d))
        dk, dv = _attn_dkv(tag, per_seq, q_d[g], k_d[g], v_d[g], do_d, _dilate_rows(lse_tot_t, d), _dilate_rows(delta_t, d))
        dk_g.append(_undilate(dk, d))
        dv_g.append(_undilate(dv, d))
    dqkv = dq_g + dk_g + dv_g

    def rope_bwd(q0, q1, q2, k0, k1, k2, v0, v1, v2, cos, s_up, s_dn):
        tabs = [_widen(t) for t in (cos, s_up, s_dn)]
        return jnp.concatenate([_rope_t(t, *tabs) for t in (q0, q1, q2, k0, k1, k2)] + [v0, v1, v2], axis=-1)

    dqkvb, = _rowwise("rope_bwd", rope_bwd, dqkv + list(rope_tabs), [], [(9 * ATT_GROUPW, mx)])

    ex.grad("w_glu", _mm("glu_dw", ygb, dzb, "tn",
                         carry=ex.carry(swap=["w_xo", "w_xq", "w_xkv", "w_mix_out", "w_att_up"])))
    dyg = _mm("glu_dx", dzb, ex.weight("w_glu"), "nt", carry=ex.carry(swap=["w_glu"]))

    def gelu_bwd(y, dyg):
        _, vjp = jax.vjp(jax.nn.gelu, y)
        return vjp(dyg)[0]

    dy, = _rowwise("gelu_bwd", gelu_bwd, [y, dyg], [], [(SSM_WIDTH, F32)])
    dy_p = _time_perm(dy)
    s_re, s_im, du_p = _ssm_scan("ssm_scan_bwd", dy_p, c12, b12, a_re, a_im, sp["ssm_d"], reverse=True,
                                 carry=ex.carry(ici=["w_ff1", "w_xkv", "w_glu"]))
    g_bexp, g_cexp, d_abr, d_abi = _ssm_wgrads(u_p, dy_p, s_re, s_im, h_re, h_im, carry=ex.carry(ici=["w_ff2"]))
    g_ssm_d, = _rowwise("ssm_dd", lambda a, b: (_colsum(a * b),), [dy_p, u_p], [], [], [SSM_WIDTH])
    g_ldt, g_are, g_aim, g_bre, g_bim = _ssm_disc_bwd(
        ldt, are, aim, bre, bim, d_abr.reshape(N_STATE, 1), d_abi.reshape(N_STATE, 1),
        _diag_of_b(g_bexp[:, :, :CH_N]), _diag_of_b(g_bexp[:, :, CH_N:]))
    g_c_re = _diag_of_c(g_cexp[:, :CH_N, :])
    g_c_im = -_diag_of_c(g_cexp[:, CH_N:, :])

    def assemble(du, dqkv, dgs, dga):
        row = jnp.concatenate([du.astype(mx), dqkv, dgs, dga], axis=-1)
        return row, _colsum(row)

    dprojb, g_b_in = _rowwise("in_assemble", assemble, [_time_unperm(du_p), dqkvb, dgsb, dgab], [],
                              [(IN_COLS, mx)], [IN_COLS])
    ex.grad("w_in", _mm("in_dw", h0b, dprojb, "tn",
                        carry=ex.carry(ici=["w_xo", "w_xq", "w_mix_out", "w_att_up"])))
    dh0 = _mm("in_dx", dprojb, ex.weight("w_in"), "nt", extras=(dr1,), epilogue=lambda r, d: (r + al * d,),
              carry=ex.carry(ici=["w_in"]))

    def ln_in_bwd(x, dout, g, b):
        _, vjp = jax.vjp(_ln, x, g, b)
        return vjp(dout)

    dx, g_ln_in_g, g_ln_in_b = _rowwise("ln_in_bwd", ln_in_bwd, [x, dh0], [sp["ln_in_g"], sp["ln_in_b"]],
                                        [(D_MODEL, F32)], [D_MODEL, D_MODEL], carry=ex.finish_carry())

    small = {"ln_in_g": g_ln_in_g, "ln_in_b": g_ln_in_b, "b_in": g_b_in, "ssm_log_dt": g_ldt, "ssm_a_re": g_are,
             "ssm_a_im": g_aim, "ssm_b_re": g_bre, "ssm_b_im": g_bim, "ssm_c_re": g_c_re, "ssm_c_im": g_c_im,
             "ssm_d": g_ssm_d, "b_glu": g_b_glu, "b_mix_out": g_b_mix, "ln1_g": g_ln1_g, "ln1_b": g_ln1_b,
             "ln2_g": g_ln2_g, "ln2_b": g_ln2_b, "b_ff1": g_b_ff1, "b_ff2": g_b_ff2, "ln3_g": g_ln3_g,
             "ln3_b": g_ln3_b}
    return loss, dx, small


def _piece_shape(k, n, axis):
    return (k // 2, n // 4) if axis == 1 else (k // 8, n)


def _aligned(v, m):
    return v if isinstance(v, int) else pl.multiple_of(v, m)


def _full_piece(ref, k, n, axis, chip, half):
    pr, pc = _piece_shape(k, n, axis)
    if axis == 1:
        return ref.at[pl.ds(_aligned(half * pr, 8), pr), pl.ds(_aligned(chip * pc, 128), pc)]
    return ref.at[pl.ds(_aligned(chip * (2 * pr) + half * pr, 8), pr), :]


def _shard_piece(ref, k, n, axis, half):
    pr, _ = _piece_shape(k, n, axis)
    return ref.at[pl.ds(_aligned(half * pr, 8), pr), :]


def _mesh_pos():
    x, y, c = lax.axis_index("x"), lax.axis_index("y"), lax.axis_index("c")
    other_chips = [(1 - x, y), (x, 1 - y), (1 - x, 1 - y)]
    return x, y, c, other_chips


def _remote(src, dst, send_sem, recv_sem, dev):
    return pltpu.make_async_remote_copy(src_ref=src, dst_ref=dst, send_sem=send_sem, recv_sem=recv_sem,
                                        device_id=dev, device_id_type=MESH)


def _placed(name, fn, n_steps, where, ins, out_sds, out_block, out_index):
    def body(w_ref, *refs):
        o_ref = refs[-1]
        o_ref[...] = fn(*[r[...] for r in refs[:-1]]).astype(o_ref.dtype)

    grid_spec = pltpu.PrefetchScalarGridSpec(
        num_scalar_prefetch=1, grid=(n_steps,), in_specs=[pl.BlockSpec(bs, idx) for _, bs, idx in ins],
        out_specs=pl.BlockSpec(out_block, out_index))
    return pl.pallas_call(body, name=name, grid_spec=grid_spec, out_shape=out_sds,
                          compiler_params=_cparams(1))(where, *[a for a, _, _ in ins])


def _gather_copies(widx):
    geo = [BIG[i][1:] for i in widx]

    def ici(full, wi, j, chip, send_sems, recv_sems, c, dev):
        k, n, ax = geo[wi]
        piece = _full_piece(full[wi], k, n, ax, chip, c)
        return _remote(piece, piece, send_sems.at[wi * 6 + j], recv_sems.at[wi * 6 + j], dev)

    def d2d(full, wi, j, chip, half, send_sems, recv_sems, sib):
        k, n, ax = geo[wi]
        piece = _full_piece(full[wi], k, n, ax, chip, half)
        return _remote(piece, piece, send_sems.at[wi * 6 + 3 + j], recv_sems.at[wi * 6 + 3 + j], sib)

    def start(_, full, send_sems, recv_sems):
        x, y, c, chips = _mesh_pos()
        for wi in range(len(geo)):
            for j, (qx, qy) in enumerate(chips):
                ici(full, wi, j, 2 * x + y, send_sems, recv_sems, c, (qx, qy, c)).start()

    def finish(_, full, send_sems, recv_sems):
        x, y, c, chips = _mesh_pos()
        sib = (x, y, 1 - c)
        for wi in range(len(geo)):
            for j, (qx, qy) in enumerate(chips):
                ici(full, wi, j, 2 * qx + qy, send_sems, recv_sems, c, (qx, qy, c)).wait_recv()
                d2d(full, wi, j, 2 * qx + qy, c, send_sems, recv_sems, sib).start()
        for wi in range(len(geo)):
            for j, (qx, qy) in enumerate(chips):
                d2d(full, wi, j, 2 * qx + qy, 1 - c, send_sems, recv_sems, sib).wait_recv()
        for wi in range(len(geo)):
            for j, (qx, qy) in enumerate(chips):
                ici(full, wi, j, 2 * x + y, send_sems, recv_sems, c, (qx, qy, c)).wait_send()
                d2d(full, wi, j, 2 * qx + qy, c, send_sems, recv_sems, sib).wait_send()

    return start, finish, 6 * len(geo)


def _swap_copies(widx):
    geo = [BIG[i][1:] for i in widx]

    def copies(g, got, send_sems, recv_sems, base):
        x, y, c, _ = _mesh_pos()
        return [_remote(_full_piece(g[wi], k, n, ax, q, 1 - c), got[wi].at[q], send_sems.at[base + wi * 4 + q],
                        recv_sems.at[base + wi * 4 + q], (x, y, 1 - c))
                for wi, (k, n, ax) in enumerate(geo) for q in range(4)]

    def start(g, got, send_sems, recv_sems, base=0):
        for cp in copies(g, got, send_sems, recv_sems, base):
            cp.start()

    def finish(g, got, send_sems, recv_sems, base=0):
        for cp in copies(g, got, send_sems, recv_sems, base):
            cp.wait()

    return start, finish, 4 * len(geo)


def _swap_shapes(widx):
    return [jax.ShapeDtypeStruct((4,) + _piece_shape(*BIG[i][1:]), F32) for i in widx]


def _reduce_swap_halves(tag, grads, widx):
    nw = len(widx)
    start, finish, n_sems = _swap_copies(widx)

    def body(*refs):
        start(refs[:nw], refs[nw:2 * nw], *refs[2 * nw:])
        finish(refs[:nw], refs[nw:2 * nw], *refs[2 * nw:])

    return pl.pallas_call(
        body, name="reduce_swap_halves_" + tag, in_specs=[HBM_SPEC] * nw, out_specs=[HBM_SPEC] * nw,
        out_shape=_swap_shapes(widx),
        scratch_shapes=[pltpu.SemaphoreType.DMA((n_sems,)), pltpu.SemaphoreType.DMA((n_sems,))])(*grads)


def _owner_copies(nw):
    def copies(p, out, send_sems, recv_sems, base):
        x, y, c, chips = _mesh_pos()
        return [_remote(p[wi].at[2 * qx + qy], out[wi].at[j], send_sems.at[base + wi * 3 + j],
                        recv_sems.at[base + wi * 3 + j], (qx, qy, c))
                for wi in range(nw) for j, (qx, qy) in enumerate(chips)]

    def start(p, out, send_sems, recv_sems, base=0):
        for cp in copies(p, out, send_sems, recv_sems, base):
            cp.start()

    def finish(p, out, send_sems, recv_sems, base=0):
        for cp in copies(p, out, send_sems, recv_sems, base):
            cp.wait()

    return start, finish, 3 * nw


def _join_carries(a, b):
    if a is None or b is None:
        return a if b is None else b
    n_i, n_o = len(a.ins), len(a.outs)
    outs = list(a.outs) + [o + n_i if isinstance(o, int) else o for o in b.outs]

    def start(c_in, c_out, send_sems, recv_sems):
        a.start(c_in[:n_i], c_out[:n_o], send_sems, recv_sems)
        b.start(c_in[n_i:], c_out[n_o:], send_sems, recv_sems, base=a.n_sems)

    def finish(c_in, c_out, send_sems, recv_sems):
        a.finish(c_in[:n_i], c_out[:n_o], send_sems, recv_sems)
        b.finish(c_in[n_i:], c_out[n_o:], send_sems, recv_sems, base=a.n_sems)

    def done(res):
        a.done(res[:n_o])
        b.done(res[n_o:])

    return _Carry(a.ins + b.ins, outs, a.n_sems + b.n_sems, start, finish, done)


def _share_copies():
    def copy(out, wi, half, send_sems, recv_sems, sib):
        _, k, n, ax = BIG[wi]
        piece = _shard_piece(out[wi], k, n, ax, half)
        return _remote(piece, piece, send_sems.at[wi], recv_sems.at[wi], sib)

    def start(_, out, send_sems, recv_sems):
        x, y, c, _ = _mesh_pos()
        for wi in range(len(BIG)):
            copy(out, wi, c, send_sems, recv_sems, (x, y, 1 - c)).start()

    def finish(_, out, send_sems, recv_sems):
        x, y, c, _ = _mesh_pos()
        for wi in range(len(BIG)):
            copy(out, wi, 1 - c, send_sems, recv_sems, (x, y, 1 - c)).wait_recv()
            copy(out, wi, c, send_sems, recv_sems, (x, y, 1 - c)).wait_send()

    return start, finish, len(BIG)


def _allreduce_small(v):
    r = v.shape[0]
    rh = r // 2
    assert rh % 8 == 0

    def body(v_ref, o_ref, sib_buf, chip_buf, send_sems, recv_sems):
        x, y, c, chips = _mesh_pos()
        me = 2 * x + y
        sib = (x, y, 1 - c)
        mine = pl.ds(pl.multiple_of(c * rh, 8), rh)
        other = pl.ds(pl.multiple_of((1 - c) * rh, 8), rh)
        swap = _remote(v_ref.at[other], sib_buf, send_sems.at[0], recv_sems.at[0], sib)
        swap.start()
        swap.wait()
        chip_buf[me] = v_ref[mine, :] + sib_buf[...]
        cps = []
        for j, (qx, qy) in enumerate(chips):
            cp = _remote(chip_buf.at[me], chip_buf.at[me], send_sems.at[1 + j], recv_sems.at[1 + j], (qx, qy, c))
            cp.start()
            cps.append(cp)
        for j, (qx, qy) in enumerate(chips):
            slot = chip_buf.at[2 * qx + qy]
            _remote(slot, slot, send_sems.at[1 + j], recv_sems.at[1 + j], (qx, qy, c)).wait_recv()
        for cp in cps:
            cp.wait_send()
        o_ref[mine, :] = ((chip_buf[0] + chip_buf[1]) + chip_buf[2]) + chip_buf[3]
        back = _remote(o_ref.at[mine], o_ref.at[mine], send_sems.at[4], recv_sems.at[4], sib)
        back.start()
        _remote(o_ref.at[other], o_ref.at[other], send_sems.at[4], recv_sems.at[4], sib).wait_recv()
        back.wait_send()

    return pl.pallas_call(
        body, name="allreduce_small", in_specs=[VMEM_SPEC], out_specs=VMEM_SPEC,
        out_shape=jax.ShapeDtypeStruct((r, 128), F32),
        scratch_shapes=[pltpu.VMEM((rh, 128), F32), pltpu.VMEM((4, rh, 128), F32),
                        pltpu.SemaphoreType.DMA((5,)), pltpu.SemaphoreType.DMA((5,))],
        compiler_params=pltpu.CompilerParams(vmem_limit_bytes=VMEM_LIMIT))(v)


def _as2d(a):
    a = a.reshape((-1, a.shape[-1])) if a.ndim > 1 else a.reshape(1, -1)
    return a


def _adamw_small(quads):
    n = len(quads)

    def body(*refs):
        for i in range(n):
            w, g, m, v = (r[...] for r in refs[4 * i:4 * i + 4])
            for ref, val in zip(refs[4 * n + 3 * i:4 * n + 3 * i + 3], _adamw(w, g, m, v)):
                ref[...] = val

    return pl.pallas_call(
        body, name="adamw_small", in_specs=[VMEM_SPEC] * (4 * n), out_specs=[VMEM_SPEC] * (3 * n),
        out_shape=[jax.ShapeDtypeStruct(q[0].shape, F32) for q in quads for _ in range(3)],
        compiler_params=pltpu.CompilerParams(vmem_limit_bytes=VMEM_LIMIT))(*[a for q in quads for a in q])


def _where():
    return jnp.stack([2 * lax.axis_index("x") + lax.axis_index("y"), lax.axis_index("c")]).astype(jnp.int32)


_BIG_INDEX = {name: i for i, (name, _, _, _) in enumerate(BIG)}


class _Exchange:
    def __init__(self, inputs, where):
        self.inputs, self.where = inputs, where
        self.full, self.ready = {}, set()
        self.raw, self.got, self.parts, self.landed, self.geom = {}, {}, {}, {}, {}
        for name, k, n, ax in BIG:
            w2 = inputs[name][0]
            rs, cs = w2.shape
            tm = _tile(rs, 512)
            steps = rs // tm
            if ax == 1:
                blk, idx = (tm, cs), lambda i, w: (i, w[0])
            else:
                blk, idx = (tm, n), functools.partial(lambda i, w, steps: (w[0] * steps + i, 0), steps=steps)
            self.full[name] = _placed("cast_" + name, lambda w: w, steps, where, [(w2, (tm, cs), lambda i, w: (i, 0))],
                                      jax.ShapeDtypeStruct((k, n), MXU_DTYPE), blk, idx)

    def _gathered(self, names, outs):
        for name, o in zip(names, outs):
            self.full[name] = o
            self.ready.add(name)

    def gather_carry(self, names):
        start, finish, n_sems = _gather_copies([_BIG_INDEX[n] for n in names])
        return _Carry([self.full[n] for n in names], list(range(len(names))), n_sems, start, finish,
                      functools.partial(self._gathered, names))

    def weight(self, name):
        assert name in self.ready, name
        return self.full[name]

    def grad(self, name, g):
        self.raw[name] = g

    def _swapped(self, names, outs):
        for name, o in zip(names, outs):
            self.got[name] = o

    def _pair_sum(self, name):
        i = _BIG_INDEX[name]
        _, k, n, ax = BIG[i]
        g = self.raw[name]
        if name not in self.got:
            self._swapped([name], _reduce_swap_halves(name, [g], [i]))
        got = self.got[name]
        pr, pc = _piece_shape(k, n, ax)
        tm = _tile(pr, 512)
        spp = pr // tm
        self.geom[name] = (pr, pc, tm, spp)
        if ax == 1:
            g_idx = functools.partial(lambda i, w, spp: (w[1] * spp + i % spp, i // spp), spp=spp)
        else:
            g_idx = functools.partial(lambda i, w, spp: ((i // spp) * 2 * spp + w[1] * spp + i % spp, 0), spp=spp)
        self.parts[name] = _placed(
            "pair_sum_" + name, lambda a, b: a + b, 4 * spp, self.where,
            [(g, (tm, pc), g_idx), (got.reshape(4 * pr, pc), (tm, pc), lambda i, w: (i, 0))],
            jax.ShapeDtypeStruct((4 * pr, pc), BF16), (tm, pc), lambda i, w: (i, 0)).reshape(4, pr, pc)

    def _landed(self, names, outs):
        for name, o in zip(names, outs):
            self.landed[name] = o

    def carry(self, swap=(), ici=()):
        first = second = None
        if swap:
            widx = [_BIG_INDEX[n] for n in swap]
            start, finish, n_sems = _swap_copies(widx)
            first = _Carry([self.raw[n] for n in swap], _swap_shapes(widx), n_sems, start, finish,
                           functools.partial(self._swapped, list(swap)))
        if ici:
            for n in ici:
                self._pair_sum(n)
            start, finish, n_sems = _owner_copies(len(ici))
            parts = [self.parts[n] for n in ici]
            outs = [jax.ShapeDtypeStruct((3,) + p.shape[1:], p.dtype) for p in parts]
            second = _Carry(parts, outs, n_sems, start, finish, functools.partial(self._landed, list(ici)))
        return _join_carries(first, second)

    def _shared(self, outs):
        self.shards = dict(zip([b[0] for b in BIG], outs))

    def finish_carry(self):
        halves = []
        for name, _, _, _ in BIG:
            pr, pc, tm, spp = self.geom[name]
            ins = [(self.parts[name], (None, tm, pc), lambda i, w: (w[0], i, 0))]
            ins += [(self.landed[name], (None, tm, pc), functools.partial(lambda i, w, j: (j, i, 0), j=j))
                    for j in range(3)]
            halves.append(_placed("chip_sum_" + name,
                                  lambda a, b, c, d: ((a.astype(F32) + b.astype(F32)) + c.astype(F32)) + d.astype(F32),
                                  spp, self.where, ins, jax.ShapeDtypeStruct(self.inputs[name].shape[1:], F32), (tm, pc),
                                  functools.partial(lambda i, w, spp: (w[1] * spp + i, 0), spp=spp)))
        start, finish, n_sems = _share_copies()
        return _Carry(halves, list(range(len(halves))), n_sems, start, finish, self._shared)


def _step(inputs):
    x, mem, positions, target = inputs["x"][0], inputs["mem"][0], inputs["positions"], inputs["loss_target"][0]
    pos = positions.reshape(-1, 1)
    ex = _Exchange(inputs, _where())
    sp = {name: _as2d(inputs[name]) for name in SMALL}
    memb, = _rowwise("cast_mem", lambda m: (m,), [mem], [], [(D_MODEL, MXU_DTYPE)])

    loss, dx, gsmall = _local_step(x, memb, pos, target, sp, ex)
    gshard = ex.shards

    out = {}
    for name, _, _, _ in BIG:
        w2, m2, v2 = inputs[name][0], inputs["m_" + name][0], inputs["v_" + name][0]
        n = w2.shape[1]
        d, nm, nv = _rowwise("adamw_" + name, _adamw, [w2, gshard[name], m2, v2], [], [(n, F32)] * 3, tm=_tile(w2.shape[0], 512))
        lead = inputs[name].shape
        out[name] = (gshard[name].reshape(lead), d.reshape(lead), nm.reshape(lead), nv.reshape(lead))

    def tiles(a):
        flat = a.reshape(-1)
        n = -(-flat.shape[0] // 1024) * 1024
        return jnp.pad(flat, (0, n - flat.shape[0])).reshape(n // 128, 128)

    pieces = [tiles(loss[:, :1])] + [tiles(gsmall[name]) for name in SMALL]
    if sum(p.shape[0] for p in pieces) % 16:
        pieces.append(jnp.zeros((8, 128), F32))
    red = _allreduce_small(jnp.concatenate(pieces, axis=0))
    loss_total = red[0, 0]
    grads, off = {}, pieces[0].shape[0]
    for name, p in zip(SMALL, pieces[1:]):
        shp = _as2d(inputs[name]).shape
        grads[name] = red[off:off + p.shape[0]].reshape(-1)[:shp[0] * shp[1]].reshape(shp)
        off += p.shape[0]
    upd = _adamw_small([(_as2d(inputs[n]), grads[n], _as2d(inputs["m_" + n]), _as2d(inputs["v_" + n])) for n in SMALL])
    for i, name in enumerate(SMALL):
        shp = inputs[name].shape
        out[name] = (grads[name].reshape(shp),) + tuple(t.reshape(shp) for t in upd[3 * i:3 * i + 3])
    return loss_total, dx.reshape(inputs["x"].shape), out


_ARG_NAMES = (("x", "mem", "positions") + WEIGHT_ORDER + ("loss_target",) + tuple("m_" + n for n in WEIGHT_ORDER)
              + tuple("v_" + n for n in WEIGHT_ORDER))


def kernel(x, mem, positions, ln_in_g, ln_in_b, w_in, b_in, ssm_log_dt, ssm_a_re, ssm_a_im, ssm_b_re, ssm_b_im, ssm_c_re, ssm_c_im, ssm_d, w_glu, b_glu, w_att_up, w_mix_out, b_mix_out, ln1_g, ln1_b, w_xq, w_xkv, w_xo, ln2_g, ln2_b, w_ff1, b_ff1, w_ff2, b_ff2, ln3_g, ln3_b, loss_target, m_ln_in_g, m_ln_in_b, m_w_in, m_b_in, m_ssm_log_dt, m_ssm_a_re, m_ssm_a_im, m_ssm_b_re, m_ssm_b_im, m_ssm_c_re, m_ssm_c_im, m_ssm_d, m_w_glu, m_b_glu, m_w_att_up, m_w_mix_out, m_b_mix_out, m_ln1_g, m_ln1_b, m_w_xq, m_w_xkv, m_w_xo, m_ln2_g, m_ln2_b, m_w_ff1, m_b_ff1, m_w_ff2, m_b_ff2, m_ln3_g, m_ln3_b, v_ln_in_g, v_ln_in_b, v_w_in, v_b_in, v_ssm_log_dt, v_ssm_a_re, v_ssm_a_im, v_ssm_b_re, v_ssm_b_im, v_ssm_c_re, v_ssm_c_im, v_ssm_d, v_w_glu, v_b_glu, v_w_att_up, v_w_mix_out, v_b_mix_out, v_ln1_g, v_ln1_b, v_w_xq, v_w_xkv, v_w_xo, v_ln2_g, v_ln2_b, v_w_ff1, v_b_ff1, v_w_ff2, v_b_ff2, v_ln3_g, v_ln3_b):
    args = (x, mem, positions, ln_in_g, ln_in_b, w_in, b_in, ssm_log_dt, ssm_a_re, ssm_a_im, ssm_b_re, ssm_b_im, ssm_c_re, ssm_c_im, ssm_d, w_glu, b_glu, w_att_up, w_mix_out, b_mix_out, ln1_g, ln1_b, w_xq, w_xkv, w_xo, ln2_g, ln2_b, w_ff1, b_ff1, w_ff2, b_ff2, ln3_g, ln3_b, loss_target, m_ln_in_g, m_ln_in_b, m_w_in, m_b_in, m_ssm_log_dt, m_ssm_a_re, m_ssm_a_im, m_ssm_b_re, m_ssm_b_im, m_ssm_c_re, m_ssm_c_im, m_ssm_d, m_w_glu, m_b_glu, m_w_att_up, m_w_mix_out, m_b_mix_out, m_ln1_g, m_ln1_b, m_w_xq, m_w_xkv, m_w_xo, m_ln2_g, m_ln2_b, m_w_ff1, m_b_ff1, m_w_ff2, m_b_ff2, m_ln3_g, m_ln3_b, v_ln_in_g, v_ln_in_b, v_w_in, v_b_in, v_ssm_log_dt, v_ssm_a_re, v_ssm_a_im, v_ssm_b_re, v_ssm_b_im, v_ssm_c_re, v_ssm_c_im, v_ssm_d, v_w_glu, v_b_glu, v_w_att_up, v_w_mix_out, v_b_mix_out, v_ln1_g, v_ln1_b, v_w_xq, v_w_xkv, v_w_xo, v_ln2_g, v_ln2_b, v_w_ff1, v_b_ff1, v_w_ff2, v_b_ff2, v_ln3_g, v_ln3_b)
    assert len(args) == len(_ARG_NAMES)
    inputs = dict(zip(_ARG_NAMES, args))
    loss, dx, out = _step(inputs)
    res = [loss, dx]
    for k in range(4):
        res += [out[name][k] for name in WEIGHT_ORDER]
    return tuple(res)
```

```python
import functools
import math

import numpy as np
import jax
import jax.numpy as jnp
from jax import lax
from jax.experimental import pallas as pl
from jax.experimental.pallas import tpu as pltpu

F32 = jnp.float32
BF16 = jnp.bfloat16
MXU_DTYPE = jnp.bfloat16

D_MODEL = 1024
SSM_GROUP = 16
SSM_WIDTH = 768
SSM_GROUPS = 48
SSM_STATE = 64
N_STATE = SSM_GROUPS * SSM_STATE
SSM_CHUNKS = 6
CH_W = 128
CH_N = 512
ATT_HEAD_DIM = 64
ATT_HPG = 4
ATT_GROUPW = ATT_HPG * ATT_HEAD_DIM
DILATIONS = (1, 4, 16)
ATT_BLK = 128
ATT_SCALE = ATT_HEAD_DIM ** -0.5
ROT_DIM = 16
ROPE_THETA = 500000.0
XATT_HEADS = 4
XATT_HEAD_DIM = 256
XATT_SCALE = XATT_HEAD_DIM ** -0.5
D_FF = 4096
IN_COLS = 5120
DEEPNORM_ALPHA = 2.0 ** 0.25
LN_EPS = 1e-5
NEG_INF = -1e30
ADAM_LR = 0.001
ADAM_B1 = 0.9
ADAM_B2 = 0.999
ADAM_EPS = 1e-08
ADAM_WD = 0.01
ADAM_STEP = 10

N_SEG = 32
VMEM_LIMIT = 56 * 1024 * 1024
MESH = pl.DeviceIdType.MESH
HBM_SPEC = pl.BlockSpec(memory_space=pltpu.HBM)
VMEM_SPEC = pl.BlockSpec(memory_space=pltpu.VMEM)

BIG = (("w_in", 1024, 5120, 1), ("w_glu", 768, 2048, 1), ("w_att_up", 256, 1024, 1),
       ("w_mix_out", 1024, 1024, 0), ("w_xq", 1024, 1024, 0), ("w_xkv", 1024, 2048, 1),
       ("w_xo", 1024, 1024, 0), ("w_ff1", 1024, 4096, 1), ("w_ff2", 4096, 1024, 0))
SMALL = ("ln_in_g", "ln_in_b", "b_in", "ssm_log_dt", "ssm_a_re", "ssm_a_im", "ssm_b_re", "ssm_b_im",
         "ssm_c_re", "ssm_c_im", "ssm_d", "b_glu", "b_mix_out", "ln1_g", "ln1_b", "ln2_g", "ln2_b",
         "b_ff1", "b_ff2", "ln3_g", "ln3_b")
WEIGHT_ORDER = ("ln_in_g", "ln_in_b", "w_in", "b_in", "ssm_log_dt", "ssm_a_re", "ssm_a_im", "ssm_b_re",
                "ssm_b_im", "ssm_c_re", "ssm_c_im", "ssm_d", "w_glu", "b_glu", "w_att_up", "w_mix_out",
                "b_mix_out", "ln1_g", "ln1_b", "w_xq", "w_xkv", "w_xo", "ln2_g", "ln2_b", "w_ff1", "b_ff1",
                "w_ff2", "b_ff2", "ln3_g", "ln3_b")


def _cparams(n_axes):
    return pltpu.CompilerParams(dimension_semantics=("arbitrary",) * n_axes, vmem_limit_bytes=VMEM_LIMIT)


class _Carry:
    def __init__(self, ins, outs, n_sems, start, finish, done):
        self.ins, self.outs, self.n_sems, self.start, self.finish, self.done = ins, outs, n_sems, start, finish, done


def _call(name, body, grid, in_specs, out_specs, out_shape, args, scratch_shapes=(), carry=None):
    in_specs, out_specs, out_shape = list(in_specs), list(out_specs), list(out_shape)
    params = _cparams(len(grid))
    if carry is None:
        return pl.pallas_call(body, name=name, grid=grid, in_specs=in_specs, out_specs=out_specs, out_shape=out_shape,
                              scratch_shapes=list(scratch_shapes), compiler_params=params)(*args)
    n_in, n_out, n_ci, n_co = len(in_specs), len(out_specs), len(carry.ins), len(carry.outs)
    n_scr = len(scratch_shapes)

    def wrapped(*refs):
        ins, c_in = refs[:n_in], refs[n_in:n_in + n_ci]
        outs, c_out = refs[n_in + n_ci:n_in + n_ci + n_out], refs[n_in + n_ci + n_out:n_in + n_ci + n_out + n_co]
        scratch = refs[n_in + n_ci + n_out + n_co:n_in + n_ci + n_out + n_co + n_scr]
        send_sems, recv_sems = refs[-2:]
        ids = [pl.program_id(a) for a in range(len(grid))]
        first = functools.reduce(jnp.logical_and, [i == 0 for i in ids])
        last = functools.reduce(jnp.logical_and, [i == g - 1 for i, g in zip(ids, grid)])

        @pl.when(first)
        def _():
            carry.start(c_in, c_out, send_sems, recv_sems)

        body(*ins, *outs, *scratch)

        @pl.when(last)
        def _():
            carry.finish(c_in, c_out, send_sems, recv_sems)

    c_shapes = [jax.ShapeDtypeStruct(carry.ins[o].shape, carry.ins[o].dtype) if isinstance(o, int) else o
                for o in carry.outs]
    aliases = {n_in + o: n_out + i for i, o in enumerate(carry.outs) if isinstance(o, int)}
    res = pl.pallas_call(
        wrapped, name=name, grid=grid, in_specs=in_specs + [HBM_SPEC] * n_ci, out_specs=out_specs + [HBM_SPEC] * n_co,
        out_shape=out_shape + c_shapes, input_output_aliases=aliases,
        scratch_shapes=list(scratch_shapes) + [pltpu.SemaphoreType.DMA((carry.n_sems,))] * 2,
        compiler_params=params)(*args, *carry.ins)
    carry.done(res[n_out:])
    return res[:n_out]


def _rowwise(name, fn, rows, consts, outs, reds=(), tm=512, touts=(), carry=None):
    n_rows = (rows[0][0] if isinstance(rows[0], tuple) else rows[0]).shape[-2]
    tm = min(tm, n_rows)
    assert n_rows % tm == 0, (name, n_rows, tm)
    specs, args = [], []
    for r in rows:
        if isinstance(r, tuple) and len(r) == 3:
            arr, width, cb = r
            specs.append(pl.BlockSpec((tm, width), functools.partial(lambda i, cb: (i, cb), cb=cb)))
        elif isinstance(r, tuple):
            arr, slot = r
            specs.append(pl.BlockSpec((None, tm, arr.shape[2]), functools.partial(lambda i, s: (s, i, 0), s=slot)))
        else:
            arr = r
            specs.append(pl.BlockSpec((tm, arr.shape[1]), lambda i: (i, 0)))
        args.append(arr)
        assert arr.shape[-2] == n_rows, (name, arr.shape, n_rows)
    for cst in consts:
        specs.append(pl.BlockSpec(cst.shape, lambda i: (0, 0)))
        args.append(cst)
    n_r, n_c, n_o, n_d = len(rows), len(consts), len(outs) + len(touts), len(reds)
    out_shape = [jax.ShapeDtypeStruct((n_rows, c), dt) for c, dt in outs]
    out_specs = [pl.BlockSpec((tm, c), lambda i: (i, 0)) for c, _ in outs]
    out_shape += [jax.ShapeDtypeStruct((r, n_rows), dt) for r, dt in touts]
    out_specs += [pl.BlockSpec((r, tm), lambda i: (0, i)) for r, _ in touts]
    out_shape += [jax.ShapeDtypeStruct((1, c), F32) for c in reds]
    out_specs += [pl.BlockSpec((1, c), lambda i: (0, 0)) for c in reds]

    def body(*refs):
        ins = [r[...] for r in refs[:n_r + n_c]]
        o_refs = refs[n_r + n_c:n_r + n_c + n_o]
        d_refs = refs[n_r + n_c + n_o:]
        res = fn(*ins)
        res = res if isinstance(res, (tuple, list)) else (res,)
        assert len(res) == n_o + n_d, (name, len(res))
        for ref, val in zip(o_refs, res[:n_o]):
            ref[...] = val.astype(ref.dtype)
        first = pl.program_id(0) == 0
        for ref, val in zip(d_refs, res[n_o:]):
            @pl.when(first)
            def _(ref=ref, val=val):
                ref[...] = val

            @pl.when(jnp.logical_not(first))
            def _(ref=ref, val=val):
                ref[...] += val

    return _call(name, body, (n_rows // tm,), specs, out_specs, out_shape, args, carry=carry)


def _colsum(v):
    return jnp.sum(v.astype(F32), axis=0, keepdims=True)


_DIMS = {"nn": (((1,), (0,)), ((), ())), "nt": (((1,), (1,)), ((), ())), "tn": (((0,), (0,)), ((), ()))}


def _tile(dim, want):
    if dim <= want:
        return dim
    return max(t for t in range(128, want + 1, 128) if dim % t == 0)


def _dot(a, b, mode):
    return lax.dot_general(a.astype(MXU_DTYPE), b.astype(MXU_DTYPE), _DIMS[mode], preferred_element_type=F32)


def _mm(name, a, b, mode, *, bias=None, extras=(), epilogue=None, out_dtypes=(F32,), tm=1024, tn=1024, tk=1024,
        carry=None, colsum=False):
    if mode == "nn":
        (m, k), (_, n) = a.shape, b.shape
    elif mode == "nt":
        (m, k), (n, _) = a.shape, b.shape
    else:
        (k, m), (_, n) = a.shape, b.shape
    if k > tk:
        tk = 5 * tk
    tn = _tile(n, tn)
    tk = _tile(k, tk)
    nk = k // tk

    def vmem_bytes(rows):
        blocks = rows * tk * a.dtype.itemsize + tk * tn * b.dtype.itemsize
        blocks += sum(rows * tn * e.dtype.itemsize for e in extras)
        blocks += sum(rows * tn * jnp.dtype(dt).itemsize for dt in out_dtypes)
        return 2 * blocks + (rows * tn * 4 if nk > 1 else 0)

    tm = _tile(m, tm if mode == "tn" else 2 * tm)
    while vmem_bytes(tm) > 3 * VMEM_LIMIT // 4 and tm % 256 == 0:
        tm //= 2
    while nk == 1 and k > 1024 and (m // tm) * (n // tn) < 4 and tm % 256 == 0:
        tm //= 2
    assert m % tm == 0 and n % tn == 0 and k % tk == 0, (name, m, n, k)
    a_spec = {"nn": pl.BlockSpec((tm, tk), lambda i, j, kk: (i, kk)),
              "nt": pl.BlockSpec((tm, tk), lambda i, j, kk: (i, kk)),
              "tn": pl.BlockSpec((tk, tm), lambda i, j, kk: (kk, i))}[mode]
    b_spec = {"nn": pl.BlockSpec((tk, tn), lambda i, j, kk: (kk, j)),
              "nt": pl.BlockSpec((tn, tk), lambda i, j, kk: (j, kk)),
              "tn": pl.BlockSpec((tk, tn), lambda i, j, kk: (kk, j))}[mode]
    specs, args = [a_spec, b_spec], [a, b]
    if bias is not None:
        specs.append(pl.BlockSpec((1, tn), lambda i, j, kk: (0, j)))
        args.append(bias)
    for e in extras:
        specs.append(pl.BlockSpec((tm, tn), lambda i, j, kk: (i, j)))
        args.append(e)
    n_e, n_o = len(extras), len(out_dtypes)
    has_bias = bias is not None

    def body(*refs):
        a_ref, b_ref = refs[0], refs[1]
        pos = 2
        bias_ref = refs[pos] if has_bias else None
        pos += int(has_bias)
        e_refs = refs[pos:pos + n_e]
        o_refs = refs[pos + n_e:pos + n_e + n_o]
        sum_ref = refs[pos + n_e + n_o] if colsum else None
        acc_ref = refs[pos + n_e + n_o + int(colsum)] if nk > 1 else None
        part = _dot(a_ref[...], b_ref[...], mode)

        def finish(r):
            if has_bias:
                r = r + bias_ref[...]
            res = epilogue(r, *[e[...] for e in e_refs]) if epilogue is not None else (r,)
            for ref, val in zip(o_refs, res):
                ref[...] = val.astype(ref.dtype)
            if colsum:
                sum_ref[...] = _colsum(res[0])

        if nk == 1:
            finish(part)
        else:
            kk = pl.program_id(2)

            @pl.when(kk == 0)
            def _():
                acc_ref[...] = part

            @pl.when(kk > 0)
            def _():
                acc_ref[...] += part

            @pl.when(kk == nk - 1)
            def _():
                finish(acc_ref[...])

    out_specs = [pl.BlockSpec((tm, tn), lambda i, j, kk: (i, j)) for _ in out_dtypes]
    out_shape = [jax.ShapeDtypeStruct((m, n), dt) for dt in out_dtypes]
    if colsum:
        out_specs.append(pl.BlockSpec((None, 1, tn), lambda i, j, kk: (i, 0, j)))
        out_shape.append(jax.ShapeDtypeStruct((m // tm, 1, n), F32))
    res = _call(name, body, (m // tm, n // tn, nk), specs, out_specs, out_shape, args,
                scratch_shapes=[pltpu.VMEM((tm, tn), F32)] if nk > 1 else [], carry=carry)
    return res[0] if len(res) == 1 else res


def _ssm_wgrads(u, dy, g_re, g_im, h_re, h_im, tk=2048, carry=None):
    s = u.shape[0]
    tk = min(tk, s)
    nk = s // tk
    assert tk % N_SEG == 0

    def body(u_ref, dy_ref, gre_ref, gim_ref, hre_ref, him_ref, lre_ref, lim_ref, db_ref, dc_ref, dar_ref, dai_ref,
             pre_ref, pim_ref):
        kk = pl.program_id(1)
        u_blk, dy_blk = u_ref[...], dy_ref[...]
        g_r, g_i, h_r, h_i = gre_ref[...], gim_ref[...], hre_ref[...], him_ref[...]
        d_b = jnp.concatenate([_dot(u_blk, g_r, "tn"), _dot(u_blk, g_i, "tn")], axis=1)
        d_c = jnp.concatenate([_dot(h_r, dy_blk, "tn"), _dot(h_i, dy_blk, "tn")], axis=0)

        @pl.when(kk == 0)
        def _():
            first_row = lax.broadcasted_iota(jnp.int32, (N_SEG, CH_N), 0) == 0
            pre_ref[...] = jnp.where(first_row, 0.0, pltpu.roll(lre_ref[...], 1, 0))
            pim_ref[...] = jnp.where(first_row, 0.0, pltpu.roll(lim_ref[...], 1, 0))

        p_r = jnp.concatenate([pre_ref[...], h_r[:tk - N_SEG]], axis=0)
        p_i = jnp.concatenate([pim_ref[...], h_i[:tk - N_SEG]], axis=0)
        pre_ref[...] = h_r[tk - N_SEG:]
        pim_ref[...] = h_i[tk - N_SEG:]
        d_ar = jnp.sum(g_r * p_r + g_i * p_i, axis=0, keepdims=True)
        d_ai = jnp.sum(g_i * p_r - g_r * p_i, axis=0, keepdims=True)

        @pl.when(kk == 0)
        def _():
            db_ref[...] = d_b
            dc_ref[...] = d_c
            dar_ref[...] = d_ar
            dai_ref[...] = d_ai

        @pl.when(kk > 0)
        def _():
            db_ref[...] += d_b
            dc_ref[...] += d_c
            dar_ref[...] += d_ar
            dai_ref[...] += d_ai

    chan = pl.BlockSpec((tk, CH_W), lambda j, kk: (kk, j))
    state = pl.BlockSpec((tk, CH_N), lambda j, kk: (kk, j))
    last = pl.BlockSpec((N_SEG, CH_N), lambda j, kk: (s // N_SEG - 1, j))
    row = pl.BlockSpec((1, CH_N), lambda j, kk: (0, j))
    return _call(
        "ssm_wgrads", body, (SSM_CHUNKS, nk), [chan, chan, state, state, state, state, last, last],
        [pl.BlockSpec((None, CH_W, 2 * CH_N), lambda j, kk: (j, 0, 0)),
         pl.BlockSpec((None, 2 * CH_N, CH_W), lambda j, kk: (j, 0, 0)), row, row],
        [jax.ShapeDtypeStruct((SSM_CHUNKS, CH_W, 2 * CH_N), F32), jax.ShapeDtypeStruct((SSM_CHUNKS, 2 * CH_N, CH_W), F32),
         jax.ShapeDtypeStruct((1, N_STATE), F32), jax.ShapeDtypeStruct((1, N_STATE), F32)],
        (u, dy, g_re, g_im, h_re, h_im, h_re, h_im), scratch_shapes=[pltpu.VMEM((N_SEG, CH_N), F32)] * 2, carry=carry)


SCAN_LB = 256


def _split_by_scan_block(mat, axis):
    halves = []
    for l in range(CH_N // SCAN_LB):
        re = lax.slice_in_dim(mat, l * SCAN_LB, (l + 1) * SCAN_LB, axis=axis)
        im = lax.slice_in_dim(mat, CH_N + l * SCAN_LB, CH_N + (l + 1) * SCAN_LB, axis=axis)
        halves.append(jnp.concatenate([re, im], axis=axis))
    return jnp.stack(halves, axis=1).reshape((-1,) + halves[0].shape[1:])


def _ssm_scan(name, chan, expand12, contract12, a_re, a_im, d_row, reverse, carry=None):
    s = chan.shape[0]
    seg_len = s // N_SEG
    n_sq = int(math.log2(seg_len))
    assert 2 ** n_sq == seg_len
    rb = min(512, s)
    per_chunk = CH_N // SCAN_LB

    def body(are_ref, aim_ref, ch_ref, e_ref, k_ref, d_ref, hre_ref, him_ref, o_ref, wre_ref, wim_ref, ere, eim, cre, cim):
        e_mat, k_mat = e_ref[...], k_ref[...]
        for r in range(s // rb):
            rows = slice(r * rb, (r + 1) * rb)
            w = _dot(ch_ref[rows, :], e_mat, "nt" if reverse else "nn")
            wre_ref[rows, :] = w[:, :SCAN_LB]
            wim_ref[rows, :] = w[:, SCAN_LB:]

        ar1 = are_ref[...]
        ai1 = -aim_ref[...] if reverse else aim_ref[...]
        ar = jnp.broadcast_to(ar1, (N_SEG, SCAN_LB))
        ai = jnp.broadcast_to(ai1, (N_SEG, SCAN_LB))

        def rows_of(k):
            kk = seg_len - 1 - k if reverse else k
            return pl.ds(pl.multiple_of(kk * N_SEG, N_SEG), N_SEG)

        def local(k, carry):
            hr, hi = carry
            rows = rows_of(k)
            nr = ar * hr - ai * hi + wre_ref[rows, :]
            ni = ar * hi + ai * hr + wim_ref[rows, :]
            hre_ref[rows, :] = nr
            him_ref[rows, :] = ni
            return nr, ni

        zero = jnp.zeros((N_SEG, SCAN_LB), F32)
        er, ei = lax.fori_loop(0, seg_len, local, (zero, zero))
        ere[...] = er
        eim[...] = ei
        pr, pi = ar1, ai1
        for _ in range(n_sq):
            pr, pi = pr * pr - pi * pi, 2.0 * pr * pi
        cr = jnp.zeros((1, SCAN_LB), F32)
        ci = jnp.zeros((1, SCAN_LB), F32)
        for jj in range(N_SEG):
            j = N_SEG - 1 - jj if reverse else jj
            cre[j:j + 1, :] = cr
            cim[j:j + 1, :] = ci
            er_j, ei_j = ere[j:j + 1, :], eim[j:j + 1, :]
            cr, ci = pr * cr - pi * ci + er_j, pr * ci + pi * cr + ei_j
        c_r, c_i = cre[...], cim[...]

        def fix(k, carry):
            qr, qi = carry
            rows = rows_of(k)
            hre_ref[rows, :] = hre_ref[rows, :] + (qr * c_r - qi * c_i)
            him_ref[rows, :] = him_ref[rows, :] + (qr * c_i + qi * c_r)
            return qr * ar - qi * ai, qr * ai + qi * ar

        lax.fori_loop(0, seg_len, fix, (ar, ai))

        first_of_chunk = lax.rem(pl.program_id(0), per_chunk) == 0
        for r in range(s // rb):
            rows = slice(r * rb, (r + 1) * rb)
            h_cat = jnp.concatenate([hre_ref[rows, :], him_ref[rows, :]], axis=1)
            part = _dot(h_cat, k_mat, "nt" if reverse else "nn")

            @pl.when(first_of_chunk)
            def _(rows=rows, part=part):
                o_ref[rows, :] = part + d_ref[...] * ch_ref[rows, :]

            @pl.when(jnp.logical_not(first_of_chunk))
            def _(rows=rows, part=part):
                o_ref[rows, :] += part

    nblk = N_STATE // SCAN_LB
    blk = pl.BlockSpec((s, SCAN_LB), lambda b: (0, b))
    row = pl.BlockSpec((1, SCAN_LB), lambda b: (0, b))
    chan_blk = pl.BlockSpec((s, CH_W), lambda b: (0, b // per_chunk))
    res = _call(name, body, (nblk,),
                [row, row, chan_blk, pl.BlockSpec((None,) + expand12.shape[1:], lambda b: (b, 0, 0)),
                 pl.BlockSpec((None,) + contract12.shape[1:], lambda b: (b, 0, 0)),
                 pl.BlockSpec((1, CH_W), lambda b: (0, b // per_chunk))],
                [blk, blk, chan_blk],
                [jax.ShapeDtypeStruct((s, N_STATE), F32)] * 2 + [jax.ShapeDtypeStruct((s, SSM_WIDTH), F32)],
                (a_re, a_im, chan, expand12, contract12, d_row),
                scratch_shapes=[pltpu.VMEM((s, SCAN_LB), F32)] * 2 + [pltpu.VMEM((N_SEG, SCAN_LB), F32)] * 4, carry=carry)
    return res[0], res[1], res[2]


def _disc(ldt, are, aim, bre, bim):
    dt = jnp.exp(ldt)
    mag = jnp.exp(are * dt)
    abr = mag * jnp.cos(aim * dt)
    abi = mag * jnp.sin(aim * dt)
    den = jnp.square(are) + jnp.square(aim)
    nr = abr - 1.0
    fre = (nr * are + abi * aim) / den
    fim = (abi * are - nr * aim) / den
    return abr, abi, fre * bre - fim * bim, fre * bim + fim * bre


def _ssm_disc_fwd(ldt, are, aim, bre, bim):
    def body(l_ref, ar_ref, ai_ref, br_ref, bi_ref, o0, o1, o2, o3):
        res = _disc(l_ref[...], ar_ref[...], ai_ref[...], br_ref[...], bi_ref[...])
        for ref, val in zip((o0, o1, o2, o3), res):
            ref[...] = val

    col = jax.ShapeDtypeStruct((N_STATE, 1), F32)
    mat = jax.ShapeDtypeStruct((N_STATE, SSM_GROUP), F32)
    return pl.pallas_call(body, name="ssm_disc_fwd", out_shape=[col, col, mat, mat],
                          in_specs=[VMEM_SPEC] * 5, out_specs=[VMEM_SPEC] * 4)(ldt, are, aim, bre, bim)


def _ssm_disc_bwd(ldt, are, aim, bre, bim, d_abr, d_abi, d_bbr, d_bbi):
    def body(l_ref, ar_ref, ai_ref, br_ref, bi_ref, c0, c1, c2, c3, g_ldt, g_are, g_aim, g_bre, g_bim):
        _, vjp = jax.vjp(_disc, l_ref[...], ar_ref[...], ai_ref[...], br_ref[...], bi_ref[...])
        dl, dar, dai, dbr, dbi = vjp((c0[...], c1[...], c2[...], c3[...]))
        state = lax.broadcasted_iota(jnp.int32, (N_STATE, SSM_GROUPS), 0)
        group = lax.broadcasted_iota(jnp.int32, (N_STATE, SSM_GROUPS), 1)
        pick = jnp.right_shift(state, 6) == group
        g_ldt[...] = jnp.sum(jnp.where(pick, dl, 0.0), axis=0, keepdims=True)
        g_are[...] = dar
        g_aim[...] = dai
        g_bre[...] = dbr
        g_bim[...] = dbi

    col = jax.ShapeDtypeStruct((N_STATE, 1), F32)
    mat = jax.ShapeDtypeStruct((N_STATE, SSM_GROUP), F32)
    return pl.pallas_call(body, name="ssm_disc_bwd",
                          out_shape=[jax.ShapeDtypeStruct((1, SSM_GROUPS), F32), col, col, mat, mat],
                          in_specs=[VMEM_SPEC] * 9, out_specs=[VMEM_SPEC] * 5,
                          compiler_params=pltpu.CompilerParams(vmem_limit_bytes=VMEM_LIMIT))(
        ldt, are, aim, bre, bim, d_abr, d_abi, d_bbr, d_bbi)


_EYE8 = np.eye(8, dtype=np.float32)


def _blockdiag_b(bb):
    t = bb.reshape(SSM_CHUNKS, 8, SSM_STATE, SSM_GROUP).transpose(0, 1, 3, 2)
    return jnp.einsum("igcn,gh->igchn", t, _EYE8).reshape(SSM_CHUNKS, CH_W, CH_N)


def _diag_of_b(m):
    t = jnp.einsum("igchn,gh->igcn", m.reshape(SSM_CHUNKS, 8, SSM_GROUP, 8, SSM_STATE), _EYE8)
    return t.transpose(0, 1, 3, 2).reshape(N_STATE, SSM_GROUP)


def _blockdiag_c(c):
    t = c.reshape(SSM_CHUNKS, 8, SSM_GROUP, SSM_STATE).transpose(0, 1, 3, 2)
    return jnp.einsum("ignc,gh->ignhc", t, _EYE8).reshape(SSM_CHUNKS, CH_N, CH_W)


def _diag_of_c(m):
    t = jnp.einsum("ignhc,gh->ignc", m.reshape(SSM_CHUNKS, 8, SSM_STATE, 8, SSM_GROUP), _EYE8)
    return t.transpose(0, 1, 3, 2).reshape(SSM_GROUPS, SSM_GROUP, SSM_STATE)


def _time_perm(a):
    s, c = a.shape
    return a.reshape(N_SEG, s // N_SEG, c).transpose(1, 0, 2).reshape(s, c)


def _time_unperm(a):
    s, c = a.shape
    return a.reshape(s // N_SEG, N_SEG, c).transpose(1, 0, 2).reshape(s, c)


def _dilate(a, d):
    s, c = a.shape
    return a if d == 1 else a.reshape(s // d, d, c).transpose(1, 0, 2).reshape(s, c)


def _undilate(a, d):
    s, c = a.shape
    return a if d == 1 else a.reshape(d, s // d, c).transpose(1, 0, 2).reshape(s, c)


def _dilate_rows(a, d):
    r, s = a.shape
    return a if d == 1 else a.reshape(r, s // d, d).transpose(0, 2, 1).reshape(r, s)


ATT_T_FWD = 4
ATT_T_BWD = 8


def _window(prev_ref, cur_ref, i, sl):
    if i == 0:
        return jnp.concatenate([prev_ref[:, sl], cur_ref[0:ATT_BLK, sl]], axis=0)
    return cur_ref[(i - 1) * ATT_BLK:(i + 1) * ATT_BLK, sl]


def _band_valid(first_key):
    qi = lax.broadcasted_iota(jnp.int32, (ATT_BLK, 2 * ATT_BLK), 0)
    ki = lax.broadcasted_iota(jnp.int32, (ATT_BLK, 2 * ATT_BLK), 1)
    steps = qi + ATT_BLK - ki
    return (steps >= 0) & (steps <= ATT_BLK) & (ki >= first_key)


ATT_STATW = ATT_HPG * 128


def _stat(h):
    return slice(h * 128, (h + 1) * 128)


def _stat_rows(stat):
    n = stat.shape[0]
    heads = [stat[:, _stat(h)].T[0:1, :] for h in range(ATT_HPG)]
    return jnp.concatenate(heads + [jnp.zeros((8 - ATT_HPG, n), stat.dtype)], axis=0)


def _attn_specs(nb, t, width=ATT_GROUPW):
    cur = pl.BlockSpec((t * ATT_BLK, width), lambda b: (b, 0))
    prev = pl.BlockSpec((ATT_BLK, width), lambda b: (jnp.maximum(b * t - 1, 0), 0))
    nxt = pl.BlockSpec((ATT_BLK, width), lambda b: (jnp.minimum((b + 1) * t, nb - 1), 0))
    return cur, prev, nxt


def _attn_fwd(tag, per_seq, q, k, v):
    s = q.shape[0]
    nb = s // ATT_BLK

    def body(q_ref, kc_ref, kp_ref, vc_ref, vp_ref, o_ref, lse_ref):
        bt = pl.program_id(0)
        for i in range(ATT_T_FWD):
            has_prev = lax.rem(bt * ATT_T_FWD + i, per_seq) > 0
            valid = _band_valid(jnp.where(has_prev, 0, ATT_BLK))
            rows = slice(i * ATT_BLK, (i + 1) * ATT_BLK)
            for h in range(ATT_HPG):
                sl = slice(h * ATT_HEAD_DIM, (h + 1) * ATT_HEAD_DIM)
                kcat = _window(kp_ref, kc_ref, i, sl)
                vcat = _window(vp_ref, vc_ref, i, sl)
                sc = _dot(q_ref[rows, sl], kcat, "nt") * ATT_SCALE
                sc = jnp.where(valid, sc, NEG_INF)
                m = jnp.max(sc, axis=-1, keepdims=True)
                p = jnp.exp(sc - m)
                den = jnp.sum(p, axis=-1, keepdims=True)
                o_ref[rows, sl] = _dot(p, vcat, "nn") / den
                lse_ref[rows, _stat(h)] = jnp.broadcast_to(m + jnp.log(den), (ATT_BLK, 128))

    cur, prev, _ = _attn_specs(nb, ATT_T_FWD)
    stat, _, _ = _attn_specs(nb, ATT_T_FWD, ATT_STATW)
    return pl.pallas_call(
        body, name="attn_fwd_" + tag, grid=(nb // ATT_T_FWD,), in_specs=[cur, cur, prev, cur, prev], out_specs=[cur, stat],
        out_shape=[jax.ShapeDtypeStruct((s, ATT_GROUPW), F32), jax.ShapeDtypeStruct((s, ATT_STATW), F32)],
        compiler_params=_cparams(1))(q, k, k, v, v)


def _attn_dq(tag, per_seq, q, k, v, do, lse, delta):
    s = q.shape[0]
    nb = s // ATT_BLK

    def body(q_ref, kc_ref, kp_ref, vc_ref, vp_ref, do_ref, lse_ref, dl_ref, dq_ref):
        bt = pl.program_id(0)
        for i in range(ATT_T_BWD):
            has_prev = lax.rem(bt * ATT_T_BWD + i, per_seq) > 0
            valid = _band_valid(jnp.where(has_prev, 0, ATT_BLK))
            rows = slice(i * ATT_BLK, (i + 1) * ATT_BLK)
            for h in range(ATT_HPG):
                sl = slice(h * ATT_HEAD_DIM, (h + 1) * ATT_HEAD_DIM)
                kcat = _window(kp_ref, kc_ref, i, sl)
                vcat = _window(vp_ref, vc_ref, i, sl)
                lse = jnp.concatenate([lse_ref[rows, _stat(h)]] * 2, axis=1)
                dlt = jnp.concatenate([dl_ref[rows, _stat(h)]] * 2, axis=1)
                sc = _dot(q_ref[rows, sl], kcat, "nt") * ATT_SCALE
                p = jnp.exp(jnp.where(valid, sc, NEG_INF) - lse)
                dp = _dot(do_ref[rows, sl], vcat, "nt")
                ds = p * (dp - dlt) * ATT_SCALE
                dq_ref[rows, sl] = _dot(ds, kcat, "nn")

    cur, prev, _ = _attn_specs(nb, ATT_T_BWD)
    stat, _, _ = _attn_specs(nb, ATT_T_BWD, ATT_STATW)
    return pl.pallas_call(
        body, name="attn_dq_" + tag, grid=(nb // ATT_T_BWD,), in_specs=[cur, cur, prev, cur, prev, cur, stat, stat],
        out_specs=cur, out_shape=jax.ShapeDtypeStruct((s, ATT_GROUPW), F32),
        compiler_params=_cparams(1))(q, k, k, v, v, do, lse, delta)


def _attn_dkv(tag, per_seq, q, k, v, do, lse_t, delta_t):
    s = q.shape[0]
    nb = s // ATT_BLK

    def body(k_ref, v_ref, qc_ref, qn_ref, doc_ref, don_ref, lc_ref, ln_ref, dc_ref, dn_ref, dk_ref, dv_ref):
        bt = pl.program_id(0)
        ki = lax.broadcasted_iota(jnp.int32, (ATT_BLK, 2 * ATT_BLK), 0)
        ci = lax.broadcasted_iota(jnp.int32, (ATT_BLK, 2 * ATT_BLK), 1)

        def pair(edge_ref, cur_ref, i, sl):
            if i == ATT_T_BWD - 1:
                return jnp.concatenate([cur_ref[i * ATT_BLK:(i + 1) * ATT_BLK, sl], edge_ref[:, sl]], axis=0)
            return cur_ref[i * ATT_BLK:(i + 2) * ATT_BLK, sl]

        def pair_row(edge_ref, cur_ref, i, h):
            if i == ATT_T_BWD - 1:
                row = jnp.concatenate([cur_ref[h:h + 1, i * ATT_BLK:(i + 1) * ATT_BLK], edge_ref[h:h + 1, :]], axis=1)
            else:
                row = cur_ref[h:h + 1, i * ATT_BLK:(i + 2) * ATT_BLK]
            return jnp.broadcast_to(row, (ATT_BLK, 2 * ATT_BLK))

        for i in range(ATT_T_BWD):
            b = bt * ATT_T_BWD + i
            next_uses = (b + 1 < nb) & (lax.rem(b + 1, per_seq) > 0)
            reach = jnp.where(next_uses, 0, 4 * ATT_BLK)
            valid = ((ci < ATT_BLK) & (ci >= ki)) | ((ci >= ATT_BLK) & (ki - ci + ATT_BLK >= reach))
            rows = slice(i * ATT_BLK, (i + 1) * ATT_BLK)
            for h in range(ATT_HPG):
                sl = slice(h * ATT_HEAD_DIM, (h + 1) * ATT_HEAD_DIM)
                qcat, docat = pair(qn_ref, qc_ref, i, sl), pair(don_ref, doc_ref, i, sl)
                sc = _dot(k_ref[rows, sl], qcat, "nt") * ATT_SCALE
                p = jnp.exp(jnp.where(valid, sc, NEG_INF) - pair_row(ln_ref, lc_ref, i, h))
                dv_ref[rows, sl] = _dot(p, docat, "nn")
                dp = _dot(v_ref[rows, sl], docat, "nt")
                ds = p * (dp - pair_row(dn_ref, dc_ref, i, h)) * ATT_SCALE
                dk_ref[rows, sl] = _dot(ds, qcat, "nn")

    cur, _, nxt = _attn_specs(nb, ATT_T_BWD)
    stat = pl.BlockSpec((8, ATT_T_BWD * ATT_BLK), lambda b: (0, b))
    snxt = pl.BlockSpec((8, ATT_BLK), lambda b: (0, jnp.minimum((b + 1) * ATT_T_BWD, nb - 1)))
    return pl.pallas_call(
        body, name="attn_dkv_" + tag, grid=(nb // ATT_T_BWD,), in_specs=[cur, cur, cur, nxt, cur, nxt, stat, snxt, stat, snxt],
        out_specs=[cur, cur], out_shape=[jax.ShapeDtypeStruct((s, ATT_GROUPW), F32)] * 2,
        compiler_params=_cparams(1))(k, v, q, q, do, do, lse_t, lse_t, delta_t, delta_t)


def _xattn_probs(q, kh):
    sc = _dot(q, kh, "nt") * XATT_SCALE
    e = jnp.exp(sc - jnp.max(sc, axis=-1, keepdims=True))
    return e / jnp.sum(e, axis=-1, keepdims=True)


def _xattn_fwd(q, kv, tm=512):
    s = q.shape[0]
    tm = min(tm, s)

    def body(q_ref, kv_ref, o_ref):
        for h in range(XATT_HEADS):
            sl = slice(h * XATT_HEAD_DIM, (h + 1) * XATT_HEAD_DIM)
            vs = slice(D_MODEL + h * XATT_HEAD_DIM, D_MODEL + (h + 1) * XATT_HEAD_DIM)
            p = _xattn_probs(q_ref[:, sl], kv_ref[:, sl])
            o_ref[:, sl] = _dot(p, kv_ref[:, vs], "nn").astype(o_ref.dtype)

    return pl.pallas_call(
        body, name="xattn_fwd", grid=(s // tm,),
        in_specs=[pl.BlockSpec((tm, D_MODEL), lambda i: (i, 0)), pl.BlockSpec(kv.shape, lambda i: (0, 0))],
        out_specs=pl.BlockSpec((tm, D_MODEL), lambda i: (i, 0)),
        out_shape=jax.ShapeDtypeStruct((s, D_MODEL), MXU_DTYPE), compiler_params=_cparams(1))(q, kv)


def _xattn_bwd(q, kv, do, tm=1024):
    s = q.shape[0]
    tm = min(tm, s)

    def body(q_ref, kv_ref, do_ref, dq_ref, dkv_ref):
        first = pl.program_id(0) == 0

        @pl.when(first)
        def _():
            dkv_ref[...] = jnp.zeros_like(dkv_ref)

        for h in range(XATT_HEADS):
            sl = slice(h * XATT_HEAD_DIM, (h + 1) * XATT_HEAD_DIM)
            vs = slice(D_MODEL + h * XATT_HEAD_DIM, D_MODEL + (h + 1) * XATT_HEAD_DIM)
            p = _xattn_probs(q_ref[:, sl], kv_ref[:, sl])
            dkv_ref[:, vs] += _dot(p, do_ref[:, sl], "tn")
            dp = _dot(do_ref[:, sl], kv_ref[:, vs], "nt")
            ds = p * (dp - jnp.sum(dp * p, axis=-1, keepdims=True)) * XATT_SCALE
            dq_ref[:, sl] = _dot(ds, kv_ref[:, sl], "nn").astype(dq_ref.dtype)
            dkv_ref[:, sl] += _dot(ds, q_ref[:, sl], "tn")

    row = pl.BlockSpec((tm, D_MODEL), lambda i: (i, 0))
    whole = pl.BlockSpec(kv.shape, lambda i: (0, 0))
    return pl.pallas_call(
        body, name="xattn_bwd", grid=(s // tm,), in_specs=[row, whole, row], out_specs=[row, whole],
        out_shape=[jax.ShapeDtypeStruct((s, D_MODEL), MXU_DTYPE), jax.ShapeDtypeStruct(kv.shape, F32)],
        compiler_params=_cparams(1))(q, kv, do)


def _ln(x, g, b):
    mu = jnp.mean(x, axis=-1, keepdims=True)
    xc = x - mu
    var = jnp.mean(jnp.square(xc), axis=-1, keepdims=True)
    return xc * lax.rsqrt(var + LN_EPS) * g + b


def _res_ln(h, o, g, b):
    return _ln(DEEPNORM_ALPHA * h + o, g, b)


def _gate(gs, ga, z1, z2, batt):
    return jax.nn.sigmoid(gs) * (z1 * jax.nn.sigmoid(z2)) + jax.nn.sigmoid(ga) * batt


ROPE_TW = 2 * ATT_HEAD_DIM


def _rope_tables(pos, invf, m1, m2):
    ang = pos.astype(F32) * invf
    sin = jnp.sin(ang)
    return jnp.cos(ang), -sin * m1, sin * m2


def _widen(tab):
    return jnp.concatenate([tab] * (ATT_GROUPW // ROPE_TW), axis=1)


def _rope(t, cos, s_up, s_dn):
    w = t.shape[-1]
    return t * cos + pltpu.roll(t, w - ROT_DIM // 2, 1) * s_up + pltpu.roll(t, ROT_DIM // 2, 1) * s_dn


def _rope_t(dt, cos, s_up, s_dn):
    w = dt.shape[-1]
    return dt * cos + pltpu.roll(dt * s_up, ROT_DIM // 2, 1) + pltpu.roll(dt * s_dn, w - ROT_DIM // 2, 1)


def _rope_consts():
    inv_freq = ROPE_THETA ** (-jnp.arange(0, ROT_DIM, 2, dtype=F32) / ROT_DIM)
    d = np.arange(ROPE_TW) % ATT_HEAD_DIM
    invf = jnp.where(d < ROT_DIM, inv_freq[d % (ROT_DIM // 2)], 0.0).reshape(1, ROPE_TW).astype(F32)
    m1 = jnp.asarray((d < ROT_DIM // 2).astype(np.float32)).reshape(1, ROPE_TW)
    m2 = jnp.asarray(((d >= ROT_DIM // 2) & (d < ROT_DIM)).astype(np.float32)).reshape(1, ROPE_TW)
    return invf, m1, m2


def _head_sum_matrix():
    d = np.arange(ATT_GROUPW) // ATT_HEAD_DIM
    s = np.arange(ATT_STATW) // 128
    return jnp.asarray((d[:, None] == s[None, :]).astype(np.float32))


def _adamw(w, g, m, v):
    m = ADAM_B1 * m + (1.0 - ADAM_B1) * g
    v = ADAM_B2 * v + (1.0 - ADAM_B2) * jnp.square(g)
    m_hat = m / (1.0 - ADAM_B1 ** ADAM_STEP)
    v_hat = v / (1.0 - ADAM_B2 ** ADAM_STEP)
    delta = -ADAM_LR * (m_hat / (jnp.sqrt(v_hat) + ADAM_EPS) + ADAM_WD * w)
    return delta, m, v


def _local_step(x, mem, pos, target, sp, ex):
    s = x.shape[0]
    al = DEEPNORM_ALPHA
    mx = MXU_DTYPE

    h0, h0b = _rowwise("ln_in", lambda x, g, b: (lambda h: (h, h))(_ln(x, g, b)), [x],
                       [sp["ln_in_g"], sp["ln_in_b"]], [(D_MODEL, F32), (D_MODEL, mx)],
                       carry=ex.gather_carry(["w_in"]))
    proj = _mm("proj", h0b, ex.weight("w_in"), "nn", bias=sp["b_in"],
               carry=ex.gather_carry(["w_glu", "w_att_up", "w_mix_out", "w_xq"]))

    ldt = jnp.repeat(sp["ssm_log_dt"].reshape(SSM_GROUPS), SSM_STATE).reshape(N_STATE, 1)
    are, aim = sp["ssm_a_re"].reshape(N_STATE, 1), sp["ssm_a_im"].reshape(N_STATE, 1)
    bre, bim = sp["ssm_b_re"].reshape(N_STATE, SSM_GROUP), sp["ssm_b_im"].reshape(N_STATE, SSM_GROUP)
    abr, abi, bbr, bbi = _ssm_disc_fwd(ldt, are, aim, bre, bim)
    a_re, a_im = abr.reshape(1, N_STATE), abi.reshape(1, N_STATE)
    bexp = jnp.concatenate([_blockdiag_b(bbr), _blockdiag_b(bbi)], axis=2).astype(mx)
    cexp = jnp.concatenate([_blockdiag_c(sp["ssm_c_re"].reshape(SSM_GROUPS, SSM_GROUP, SSM_STATE)),
                            -_blockdiag_c(sp["ssm_c_im"].reshape(SSM_GROUPS, SSM_GROUP, SSM_STATE))],
                           axis=1).astype(mx)
    u_p = _time_perm(proj[:, :SSM_WIDTH])
    b12, c12 = _split_by_scan_block(bexp, 2), _split_by_scan_block(cexp, 1)
    h_re, h_im, y_p = _ssm_scan("ssm_scan_fwd", u_p, b12, c12, a_re, a_im, sp["ssm_d"], reverse=False,
                                carry=ex.gather_carry(["w_xo", "w_ff1", "w_ff2"]))
    y = _time_unperm(y_p)
    ygb, = _rowwise("gelu", lambda y: jax.nn.gelu(y), [y], [], [(SSM_WIDTH, mx)])
    z = _mm("glu", ygb, ex.weight("w_glu"), "nn", bias=sp["b_glu"], carry=ex.gather_carry(["w_xkv"]))

    invf, m1, m2 = _rope_consts()

    def rope_fwd(pos, q0, q1, q2, k0, k1, k2, v0, v1, v2, invf, m1, m2):
        narrow = _rope_tables(pos, invf, m1, m2)
        tabs = [_widen(t) for t in narrow]
        return tuple(_rope(t, *tabs) for t in (q0, q1, q2, k0, k1, k2)) + (v0, v1, v2) + tuple(narrow)

    qkv_cols = [(proj, ATT_GROUPW, 3 + i) for i in range(9)]
    qkv = _rowwise("rope", rope_fwd, [pos] + qkv_cols, [invf, m1, m2], [(ATT_GROUPW, mx)] * 9 + [(ROPE_TW, F32)] * 3)
    rope_tabs = qkv[9:]
    n_blocks = s // ATT_BLK
    groups = [(str(g), n_blocks // d, d) for g, d in enumerate(DILATIONS)]
    q_d = [_dilate(qkv[g], d) for g, d in enumerate(DILATIONS)]
    k_d = [_dilate(qkv[3 + g], d) for g, d in enumerate(DILATIONS)]
    v_d = [_dilate(qkv[6 + g], d) for g, d in enumerate(DILATIONS)]
    o_g, l_g = [], []
    for g, (tag, per_seq, d) in enumerate(groups):
        o, lse = _attn_fwd(tag, per_seq, q_d[g], k_d[g], v_d[g])
        o_g.append(_undilate(o, d))
        l_g.append(_undilate(lse, d))

    def merge(o0, o1, o2, l0, l1, l2):
        m = jnp.maximum(jnp.maximum(l0, l1), l2)
        e0, e1, e2 = jnp.exp(l0 - m), jnp.exp(l1 - m), jnp.exp(l2 - m)
        tot = e0 + e1 + e2

        def per_dim(e):
            w = e / tot
            return jnp.concatenate([w[:, h * 128:h * 128 + ATT_HEAD_DIM] for h in range(ATT_HPG)], axis=1)

        att = per_dim(e0) * o0 + per_dim(e1) * o1 + per_dim(e2) * o2
        lse = m + jnp.log(tot)
        return att, att, lse, _stat_rows(lse)

    att, attb, lse_tot, lse_tot_t = _rowwise("attn_merge", merge, o_g + l_g, [],
                                             [(ATT_GROUPW, F32), (ATT_GROUPW, mx), (ATT_STATW, F32)], touts=[(8, F32)])
    batt = _mm("att_up", attb, ex.weight("w_att_up"), "nn")

    gate_rows = [(proj, D_MODEL, 3), (proj, D_MODEL, 4), (z, D_MODEL, 0), (z, D_MODEL, 1), batt]
    mixedb, = _rowwise("gate", _gate, gate_rows, [], [(D_MODEL, mx)])
    o1 = _mm("mix_out", mixedb, ex.weight("w_mix_out"), "nn", bias=sp["b_mix_out"])
    h1, h1b = _rowwise("ln1", lambda h, o, g, b: (lambda r: (r, r))(_res_ln(h, o, g, b)), [h0, o1],
                       [sp["ln1_g"], sp["ln1_b"]], [(D_MODEL, F32), (D_MODEL, mx)])

    qx = _mm("xq", h1b, ex.weight("w_xq"), "nn", out_dtypes=(mx,))
    kvx = _mm("xkv", mem, ex.weight("w_xkv"), "nn", out_dtypes=(mx,))
    oxb = _xattn_fwd(qx, kvx)
    o2 = _mm("xo", oxb, ex.weight("w_xo"), "nn")
    h2, h2b = _rowwise("ln2", lambda h, o, g, b: (lambda r: (r, r))(_res_ln(h, o, g, b)), [h1, o2],
                       [sp["ln2_g"], sp["ln2_b"]], [(D_MODEL, F32), (D_MODEL, mx)])

    a_ff, fb = _mm("ff1", h2b, ex.weight("w_ff1"), "nn", bias=sp["b_ff1"],
                   epilogue=lambda r: (r, jnp.square(jnp.maximum(r, 0.0))), out_dtypes=(F32, mx))
    o3 = _mm("ff2", fb, ex.weight("w_ff2"), "nn", bias=sp["b_ff2"])

    def loss_bwd(h2, o3, tgt, g, b):
        def f(h2, o3, g, b):
            h3 = _res_ln(h2, o3, g, b)
            return 0.5 * jnp.sum(jnp.mean(jnp.square(h3 - tgt), axis=-1))

        loss, vjp = jax.vjp(f, h2, o3, g, b)
        _, dr, dg, db = vjp(jnp.ones((), F32))
        return dr, dr, dg, db, _colsum(dr), jnp.full((1, 128), loss, F32)

    dr3, dr3b, g_ln3_g, g_ln3_b, g_b_ff2, loss = _rowwise(
        "loss_ln3_bwd", loss_bwd, [h2, o3, target], [sp["ln3_g"], sp["ln3_b"]],
        [(D_MODEL, F32), (D_MODEL, mx)], [D_MODEL, D_MODEL, D_MODEL, 128])

    dab, da_sums = _mm("ff2_dx", dr3b, ex.weight("w_ff2"), "nt", extras=(a_ff,),
                       epilogue=lambda r, a: (r * (2.0 * jnp.maximum(a, 0.0)),), out_dtypes=(mx,), colsum=True)
    g_b_ff1 = jnp.sum(da_sums, axis=0)
    ex.grad("w_ff2", _mm("ff2_dw", fb, dr3b, "tn"))
    ex.grad("w_ff1", _mm("ff1_dw", h2b, dab, "tn", carry=ex.carry(swap=["w_ff2"])))
    dh2 = _mm("ff1_dx", dab, ex.weight("w_ff1"), "nt", extras=(dr3,), epilogue=lambda r, d: (r + al * d,),
              carry=ex.carry(swap=["w_ff1"]))

    def ln_bwd(h, o, dout, g, b):
        _, vjp = jax.vjp(_res_ln, h, o, g, b)
        _, dr, dg, db = vjp(dout)
        return dr, dr, dg, db, _colsum(dr)

    dr2, dr2b, g_ln2_g, g_ln2_b, _ = _rowwise(
        "ln2_bwd", ln_bwd, [h1, o2, dh2], [sp["ln2_g"], sp["ln2_b"]],
        [(D_MODEL, F32), (D_MODEL, mx)], [D_MODEL, D_MODEL, D_MODEL])
    ex.grad("w_xo", _mm("xo_dw", oxb, dr2b, "tn"))
    doxb = _mm("xo_dx", dr2b, ex.weight("w_xo"), "nt", out_dtypes=(mx,))
    dqxb, dkvx = _xattn_bwd(qx, kvx, doxb)
    ex.grad("w_xq", _mm("xq_dw", h1b, dqxb, "tn"))
    dh1 = _mm("xq_dx", dqxb, ex.weight("w_xq"), "nt", extras=(dr2,), epilogue=lambda r, d: (r + al * d,))
    ex.grad("w_xkv", _mm("xkv_dw", mem, dkvx, "tn"))

    dr1, dr1b, g_ln1_g, g_ln1_b, g_b_mix = _rowwise(
        "ln1_bwd", ln_bwd, [h0, o1, dh1], [sp["ln1_g"], sp["ln1_b"]],
        [(D_MODEL, F32), (D_MODEL, mx)], [D_MODEL, D_MODEL, D_MODEL])
    ex.grad("w_mix_out", _mm("mix_dw", mixedb, dr1b, "tn"))
    dmixed = _mm("mix_dx", dr1b, ex.weight("w_mix_out"), "nt")

    def gate_bwd(gs, ga, z1, z2, batt, dm):
        _, vjp = jax.vjp(_gate, gs, ga, z1, z2, batt)
        dgs, dga, dz1, dz2, dbatt = vjp(dm)
        dz = jnp.concatenate([dz1, dz2], axis=-1)
        return dgs, dga, dz, dbatt, _colsum(dz)

    dgsb, dgab, dzb, dbattb, g_b_glu = _rowwise(
        "gate_bwd", gate_bwd, gate_rows + [dmixed], [],
        [(D_MODEL, mx), (D_MODEL, mx), (2 * D_MODEL, mx), (D_MODEL, mx)], [2 * D_MODEL])
    ex.grad("w_att_up", _mm("att_up_dw", attb, dbattb, "tn"))
    datt = _mm("att_up_dx", dbattb, ex.weight("w_att_up"), "nt")

    def att_delta(datt, att, hs):
        dl = jnp.dot(datt * att, hs, precision=lax.Precision.HIGHEST, preferred_element_type=F32)
        return datt, dl, _stat_rows(dl)

    dattb, delta, delta_t = _rowwise("attn_delta", att_delta, [datt, att], [_head_sum_matrix()],
                                     [(ATT_GROUPW, mx), (ATT_STATW, F32)], touts=[(8, F32)])
    dq_g, dk_g, dv_g = [], [], []
    for g, (tag, per_seq, d) in enumerate(groups):
        do_d, lt_d, dl_d = _dilate(dattb, d), _dilate(lse_tot, d), _dilate(delta, d)
        dq_g.append(_undilate(_attn_dq(tag, per_seq, q_d[g], k_d[g], v_d[g], do_d, lt_d, dl_d), d))
        dk, dv = _attn_dkv(tag, per_seq, q_d[g], k_d[g], v_d[g], do_d, _dilate_rows(lse_tot_t, d), _dilate_rows(delta_t, d))
        dk_g.append(_undilate(dk, d))
        dv_g.append(_undilate(dv, d))
    dqkv = dq_g + dk_g + dv_g

    def rope_bwd(q0, q1, q2, k0, k1, k2, v0, v1, v2, cos, s_up, s_dn):
        tabs = [_widen(t) for t in (cos, s_up, s_dn)]
        return jnp.concatenate([_rope_t(t, *tabs) for t in (q0, q1, q2, k0, k1, k2)] + [v0, v1, v2], axis=-1)

    dqkvb, = _rowwise("rope_bwd", rope_bwd, dqkv + list(rope_tabs), [], [(9 * ATT_GROUPW, mx)])

    ex.grad("w_glu", _mm("glu_dw", ygb, dzb, "tn",
                         carry=ex.carry(swap=["w_xo", "w_xq", "w_xkv", "w_mix_out", "w_att_up"])))
    dyg = _mm("glu_dx", dzb, ex.weight("w_glu"), "nt", carry=ex.carry(swap=["w_glu"]))

    def gelu_bwd(y, dyg):
        _, vjp = jax.vjp(jax.nn.gelu, y)
        return vjp(dyg)[0]

    dy, = _rowwise("gelu_bwd", gelu_bwd, [y, dyg], [], [(SSM_WIDTH, F32)])
    dy_p = _time_perm(dy)
    s_re, s_im, du_p = _ssm_scan("ssm_scan_bwd", dy_p, c12, b12, a_re, a_im, sp["ssm_d"], reverse=True,
                                 carry=ex.carry(ici=["w_ff1", "w_xkv", "w_glu"]))
    g_bexp, g_cexp, d_abr, d_abi = _ssm_wgrads(u_p, dy_p, s_re, s_im, h_re, h_im, carry=ex.carry(ici=["w_ff2"]))
    g_ssm_d, = _rowwise("ssm_dd", lambda a, b: (_colsum(a * b),), [dy_p, u_p], [], [], [SSM_WIDTH])
    g_ldt, g_are, g_aim, g_bre, g_bim = _ssm_disc_bwd(
        ldt, are, aim, bre, bim, d_abr.reshape(N_STATE, 1), d_abi.reshape(N_STATE, 1),
        _diag_of_b(g_bexp[:, :, :CH_N]), _diag_of_b(g_bexp[:, :, CH_N:]))
    g_c_re = _diag_of_c(g_cexp[:, :CH_N, :])
    g_c_im = -_diag_of_c(g_cexp[:, CH_N:, :])

    def assemble(du, dqkv, dgs, dga):
        row = jnp.concatenate([du.astype(mx), dqkv, dgs, dga], axis=-1)
        return row, _colsum(row)

    dprojb, g_b_in = _rowwise("in_assemble", assemble, [_time_unperm(du_p), dqkvb, dgsb, dgab], [],
                              [(IN_COLS, mx)], [IN_COLS])
    ex.grad("w_in", _mm("in_dw", h0b, dprojb, "tn",
                        carry=ex.carry(ici=["w_xo", "w_xq", "w_mix_out", "w_att_up"])))
    dh0 = _mm("in_dx", dprojb, ex.weight("w_in"), "nt", extras=(dr1,), epilogue=lambda r, d: (r + al * d,),
              carry=ex.carry(ici=["w_in"]))

    def ln_in_bwd(x, dout, g, b):
        _, vjp = jax.vjp(_ln, x, g, b)
        return vjp(dout)

    dx, g_ln_in_g, g_ln_in_b = _rowwise("ln_in_bwd", ln_in_bwd, [x, dh0], [sp["ln_in_g"], sp["ln_in_b"]],
                                        [(D_MODEL, F32)], [D_MODEL, D_MODEL], carry=ex.finish_carry())

    small = {"ln_in_g": g_ln_in_g, "ln_in_b": g_ln_in_b, "b_in": g_b_in, "ssm_log_dt": g_ldt, "ssm_a_re": g_are,
             "ssm_a_im": g_aim, "ssm_b_re": g_bre, "ssm_b_im": g_bim, "ssm_c_re": g_c_re, "ssm_c_im": g_c_im,
             "ssm_d": g_ssm_d, "b_glu": g_b_glu, "b_mix_out": g_b_mix, "ln1_g": g_ln1_g, "ln1_b": g_ln1_b,
             "ln2_g": g_ln2_g, "ln2_b": g_ln2_b, "b_ff1": g_b_ff1, "b_ff2": g_b_ff2, "ln3_g": g_ln3_g,
             "ln3_b": g_ln3_b}
    return loss, dx, small


def _piece_shape(k, n, axis):
    return (k // 2, n // 4) if axis == 1 else (k // 8, n)


def _aligned(v, m):
    return v if isinstance(v, int) else pl.multiple_of(v, m)


def _full_piece(ref, k, n, axis, chip, half):
    pr, pc = _piece_shape(k, n, axis)
    if axis == 1:
        return ref.at[pl.ds(_aligned(half * pr, 8), pr), pl.ds(_aligned(chip * pc, 128), pc)]
    return ref.at[pl.ds(_aligned(chip * (2 * pr) + half * pr, 8), pr), :]


def _shard_piece(ref, k, n, axis, half):
    pr, _ = _piece_shape(k, n, axis)
    return ref.at[pl.ds(_aligned(half * pr, 8), pr), :]


def _mesh_pos():
    x, y, c = lax.axis_index("x"), lax.axis_index("y"), lax.axis_index("c")
    other_chips = [(1 - x, y), (x, 1 - y), (1 - x, 1 - y)]
    return x, y, c, other_chips


def _remote(src, dst, send_sem, recv_sem, dev):
    return pltpu.make_async_remote_copy(src_ref=src, dst_ref=dst, send_sem=send_sem, recv_sem=recv_sem,
                                        device_id=dev, device_id_type=MESH)


def _placed(name, fn, n_steps, where, ins, out_sds, out_block, out_index):
    def body(w_ref, *refs):
        o_ref = refs[-1]
        o_ref[...] = fn(*[r[...] for r in refs[:-1]]).astype(o_ref.dtype)

    grid_spec = pltpu.PrefetchScalarGridSpec(
        num_scalar_prefetch=1, grid=(n_steps,), in_specs=[pl.BlockSpec(bs, idx) for _, bs, idx in ins],
        out_specs=pl.BlockSpec(out_block, out_index))
    return pl.pallas_call(body, name=name, grid_spec=grid_spec, out_shape=out_sds,
                          compiler_params=_cparams(1))(where, *[a for a, _, _ in ins])


def _gather_copies(widx):
    geo = [BIG[i][1:] for i in widx]

    def ici(full, wi, j, chip, send_sems, recv_sems, c, dev):
        k, n, ax = geo[wi]
        piece = _full_piece(full[wi], k, n, ax, chip, c)
        return _remote(piece, piece, send_sems.at[wi * 6 + j], recv_sems.at[wi * 6 + j], dev)

    def d2d(full, wi, j, chip, half, send_sems, recv_sems, sib):
        k, n, ax = geo[wi]
        piece = _full_piece(full[wi], k, n, ax, chip, half)
        return _remote(piece, piece, send_sems.at[wi * 6 + 3 + j], recv_sems.at[wi * 6 + 3 + j], sib)

    def start(_, full, send_sems, recv_sems):
        x, y, c, chips = _mesh_pos()
        for wi in range(len(geo)):
            for j, (qx, qy) in enumerate(chips):
                ici(full, wi, j, 2 * x + y, send_sems, recv_sems, c, (qx, qy, c)).start()

    def finish(_, full, send_sems, recv_sems):
        x, y, c, chips = _mesh_pos()
        sib = (x, y, 1 - c)
        for wi in range(len(geo)):
            for j, (qx, qy) in enumerate(chips):
                ici(full, wi, j, 2 * qx + qy, send_sems, recv_sems, c, (qx, qy, c)).wait_recv()
                d2d(full, wi, j, 2 * qx + qy, c, send_sems, recv_sems, sib).start()
        for wi in range(len(geo)):
            for j, (qx, qy) in enumerate(chips):
                d2d(full, wi, j, 2 * qx + qy, 1 - c, send_sems, recv_sems, sib).wait_recv()
        for wi in range(len(geo)):
            for j, (qx, qy) in enumerate(chips):
                ici(full, wi, j, 2 * x + y, send_sems, recv_sems, c, (qx, qy, c)).wait_send()
                d2d(full, wi, j, 2 * qx + qy, c, send_sems, recv_sems, sib).wait_send()

    return start, finish, 6 * len(geo)


def _swap_copies(widx):
    geo = [BIG[i][1:] for i in widx]

    def copies(g, got, send_sems, recv_sems, base):
        x, y, c, _ = _mesh_pos()
        return [_remote(_full_piece(g[wi], k, n, ax, q, 1 - c), got[wi].at[q], send_sems.at[base + wi * 4 + q],
                        recv_sems.at[base + wi * 4 + q], (x, y, 1 - c))
                for wi, (k, n, ax) in enumerate(geo) for q in range(4)]

    def start(g, got, send_sems, recv_sems, base=0):
        for cp in copies(g, got, send_sems, recv_sems, base):
            cp.start()

    def finish(g, got, send_sems, recv_sems, base=0):
        for cp in copies(g, got, send_sems, recv_sems, base):
            cp.wait()

    return start, finish, 4 * len(geo)


def _swap_shapes(widx):
    return [jax.ShapeDtypeStruct((4,) + _piece_shape(*BIG[i][1:]), F32) for i in widx]


def _reduce_swap_halves(tag, grads, widx):
    nw = len(widx)
    start, finish, n_sems = _swap_copies(widx)

    def body(*refs):
        start(refs[:nw], refs[nw:2 * nw], *refs[2 * nw:])
        finish(refs[:nw], refs[nw:2 * nw], *refs[2 * nw:])

    return pl.pallas_call(
        body, name="reduce_swap_halves_" + tag, in_specs=[HBM_SPEC] * nw, out_specs=[HBM_SPEC] * nw,
        out_shape=_swap_shapes(widx),
        scratch_shapes=[pltpu.SemaphoreType.DMA((n_sems,)), pltpu.SemaphoreType.DMA((n_sems,))])(*grads)


def _owner_copies(nw):
    def copies(p, out, send_sems, recv_sems, base):
        x, y, c, chips = _mesh_pos()
        return [_remote(p[wi].at[2 * qx + qy], out[wi].at[j], send_sems.at[base + wi * 3 + j],
                        recv_sems.at[base + wi * 3 + j], (qx, qy, c))
                for wi in range(nw) for j, (qx, qy) in enumerate(chips)]

    def start(p, out, send_sems, recv_sems, base=0):
        for cp in copies(p, out, send_sems, recv_sems, base):
            cp.start()

    def finish(p, out, send_sems, recv_sems, base=0):
        for cp in copies(p, out, send_sems, recv_sems, base):
            cp.wait()

    return start, finish, 3 * nw


def _join_carries(a, b):
    if a is None or b is None:
        return a if b is None else b
    n_i, n_o = len(a.ins), len(a.outs)
    outs = list(a.outs) + [o + n_i if isinstance(o, int) else o for o in b.outs]

    def start(c_in, c_out, send_sems, recv_sems):
        a.start(c_in[:n_i], c_out[:n_o], send_sems, recv_sems)
        b.start(c_in[n_i:], c_out[n_o:], send_sems, recv_sems, base=a.n_sems)

    def finish(c_in, c_out, send_sems, recv_sems):
        a.finish(c_in[:n_i], c_out[:n_o], send_sems, recv_sems)
        b.finish(c_in[n_i:], c_out[n_o:], send_sems, recv_sems, base=a.n_sems)

    def done(res):
        a.done(res[:n_o])
        b.done(res[n_o:])

    return _Carry(a.ins + b.ins, outs, a.n_sems + b.n_sems, start, finish, done)


def _share_copies():
    def copy(out, wi, half, send_sems, recv_sems, sib):
        _, k, n, ax = BIG[wi]
        piece = _shard_piece(out[wi], k, n, ax, half)
        return _remote(piece, piece, send_sems.at[wi], recv_sems.at[wi], sib)

    def start(_, out, send_sems, recv_sems):
        x, y, c, _ = _mesh_pos()
        for wi in range(len(BIG)):
            copy(out, wi, c, send_sems, recv_sems, (x, y, 1 - c)).start()

    def finish(_, out, send_sems, recv_sems):
        x, y, c, _ = _mesh_pos()
        for wi in range(len(BIG)):
            copy(out, wi, 1 - c, send_sems, recv_sems, (x, y, 1 - c)).wait_recv()
            copy(out, wi, c, send_sems, recv_sems, (x, y, 1 - c)).wait_send()

    return start, finish, len(BIG)


def _allreduce_small(v):
    r = v.shape[0]
    rh = r // 2
    assert rh % 8 == 0

    def body(v_ref, o_ref, sib_buf, chip_buf, send_sems, recv_sems):
        x, y, c, chips = _mesh_pos()
        me = 2 * x + y
        sib = (x, y, 1 - c)
        mine = pl.ds(pl.multiple_of(c * rh, 8), rh)
        other = pl.ds(pl.multiple_of((1 - c) * rh, 8), rh)
        swap = _remote(v_ref.at[other], sib_buf, send_sems.at[0], recv_sems.at[0], sib)
        swap.start()
        swap.wait()
        chip_buf[me] = v_ref[mine, :] + sib_buf[...]
        cps = []
        for j, (qx, qy) in enumerate(chips):
            cp = _remote(chip_buf.at[me], chip_buf.at[me], send_sems.at[1 + j], recv_sems.at[1 + j], (qx, qy, c))
            cp.start()
            cps.append(cp)
        for j, (qx, qy) in enumerate(chips):
            slot = chip_buf.at[2 * qx + qy]
            _remote(slot, slot, send_sems.at[1 + j], recv_sems.at[1 + j], (qx, qy, c)).wait_recv()
        for cp in cps:
            cp.wait_send()
        o_ref[mine, :] = ((chip_buf[0] + chip_buf[1]) + chip_buf[2]) + chip_buf[3]
        back = _remote(o_ref.at[mine], o_ref.at[mine], send_sems.at[4], recv_sems.at[4], sib)
        back.start()
        _remote(o_ref.at[other], o_ref.at[other], send_sems.at[4], recv_sems.at[4], sib).wait_recv()
        back.wait_send()

    return pl.pallas_call(
        body, name="allreduce_small", in_specs=[VMEM_SPEC], out_specs=VMEM_SPEC,
        out_shape=jax.ShapeDtypeStruct((r, 128), F32),
        scratch_shapes=[pltpu.VMEM((rh, 128), F32), pltpu.VMEM((4, rh, 128), F32),
                        pltpu.SemaphoreType.DMA((5,)), pltpu.SemaphoreType.DMA((5,))],
        compiler_params=pltpu.CompilerParams(vmem_limit_bytes=VMEM_LIMIT))(v)


def _as2d(a):
    a = a.reshape((-1, a.shape[-1])) if a.ndim > 1 else a.reshape(1, -1)
    return a


def _adamw_small(quads):
    n = len(quads)

    def body(*refs):
        for i in range(n):
            w, g, m, v = (r[...] for r in refs[4 * i:4 * i + 4])
            for ref, val in zip(refs[4 * n + 3 * i:4 * n + 3 * i + 3], _adamw(w, g, m, v)):
                ref[...] = val

    return pl.pallas_call(
        body, name="adamw_small", in_specs=[VMEM_SPEC] * (4 * n), out_specs=[VMEM_SPEC] * (3 * n),
        out_shape=[jax.ShapeDtypeStruct(q[0].shape, F32) for q in quads for _ in range(3)],
        compiler_params=pltpu.CompilerParams(vmem_limit_bytes=VMEM_LIMIT))(*[a for q in quads for a in q])


def _where():
    return jnp.stack([2 * lax.axis_index("x") + lax.axis_index("y"), lax.axis_index("c")]).astype(jnp.int32)


_BIG_INDEX = {name: i for i, (name, _, _, _) in enumerate(BIG)}


class _Exchange:
    def __init__(self, inputs, where):
        self.inputs, self.where = inputs, where
        self.full, self.ready = {}, set()
        self.raw, self.got, self.parts, self.landed, self.geom = {}, {}, {}, {}, {}
        for name, k, n, ax in BIG:
            w2 = inputs[name][0]
            rs, cs = w2.shape
            tm = _tile(rs, 512)
            steps = rs // tm
            if ax == 1:
                blk, idx = (tm, cs), lambda i, w: (i, w[0])
            else:
                blk, idx = (tm, n), functools.partial(lambda i, w, steps: (w[0] * steps + i, 0), steps=steps)
            self.full[name] = _placed("cast_" + name, lambda w: w, steps, where, [(w2, (tm, cs), lambda i, w: (i, 0))],
                                      jax.ShapeDtypeStruct((k, n), MXU_DTYPE), blk, idx)

    def _gathered(self, names, outs):
        for name, o in zip(names, outs):
            self.full[name] = o
            self.ready.add(name)

    def gather_carry(self, names):
        start, finish, n_sems = _gather_copies([_BIG_INDEX[n] for n in names])
        return _Carry([self.full[n] for n in names], list(range(len(names))), n_sems, start, finish,
                      functools.partial(self._gathered, names))

    def weight(self, name):
        assert name in self.ready, name
        return self.full[name]

    def grad(self, name, g):
        self.raw[name] = g

    def _swapped(self, names, outs):
        for name, o in zip(names, outs):
            self.got[name] = o

    def _pair_sum(self, name):
        i = _BIG_INDEX[name]
        _, k, n, ax = BIG[i]
        g = self.raw[name]
        if name not in self.got:
            self._swapped([name], _reduce_swap_halves(name, [g], [i]))
        got = self.got[name]
        pr, pc = _piece_shape(k, n, ax)
        tm = _tile(pr, 512)
        spp = pr // tm
        self.geom[name] = (pr, pc, tm, spp)
        if ax == 1:
            g_idx = functools.partial(lambda i, w, spp: (w[1] * spp + i % spp, i // spp), spp=spp)
        else:
            g_idx = functools.partial(lambda i, w, spp: ((i // spp) * 2 * spp + w[1] * spp + i % spp, 0), spp=spp)
        self.parts[name] = _placed(
            "pair_sum_" + name, lambda a, b: a + b, 4 * spp, self.where,
            [(g, (tm, pc), g_idx), (got.reshape(4 * pr, pc), (tm, pc), lambda i, w: (i, 0))],
            jax.ShapeDtypeStruct((4 * pr, pc), BF16), (tm, pc), lambda i, w: (i, 0)).reshape(4, pr, pc)

    def _landed(self, names, outs):
        for name, o in zip(names, outs):
            self.landed[name] = o

    def carry(self, swap=(), ici=()):
        first = second = None
        if swap:
            widx = [_BIG_INDEX[n] for n in swap]
            start, finish, n_sems = _swap_copies(widx)
            first = _Carry([self.raw[n] for n in swap], _swap_shapes(widx), n_sems, start, finish,
                           functools.partial(self._swapped, list(swap)))
        if ici:
            for n in ici:
                self._pair_sum(n)
            start, finish, n_sems = _owner_copies(len(ici))
            parts = [self.parts[n] for n in ici]
            outs = [jax.ShapeDtypeStruct((3,) + p.shape[1:], p.dtype) for p in parts]
            second = _Carry(parts, outs, n_sems, start, finish, functools.partial(self._landed, list(ici)))
        return _join_carries(first, second)

    def _shared(self, outs):
        self.shards = dict(zip([b[0] for b in BIG], outs))

    def finish_carry(self):
        halves = []
        for name, _, _, _ in BIG:
            pr, pc, tm, spp = self.geom[name]
            ins = [(self.parts[name], (None, tm, pc), lambda i, w: (w[0], i, 0))]
            ins += [(self.landed[name], (None, tm, pc), functools.partial(lambda i, w, j: (j, i, 0), j=j))
                    for j in range(3)]
            halves.append(_placed("chip_sum_" + name,
                                  lambda a, b, c, d: ((a.astype(F32) + b.astype(F32)) + c.astype(F32)) + d.astype(F32),
                                  spp, self.where, ins, jax.ShapeDtypeStruct(self.inputs[name].shape[1:], F32), (tm, pc),
                                  functools.partial(lambda i, w, spp: (w[1] * spp + i, 0), spp=spp)))
        start, finish, n_sems = _share_copies()
        return _Carry(halves, list(range(len(halves))), n_sems, start, finish, self._shared)


def _step(inputs):
    x, mem, positions, target = inputs["x"][0], inputs["mem"][0], inputs["positions"], inputs["loss_target"][0]
    pos = positions.reshape(-1, 1)
    ex = _Exchange(inputs, _where())
    sp = {name: _as2d(inputs[name]) for name in SMALL}
    memb, = _rowwise("cast_mem", lambda m: (m,), [mem], [], [(D_MODEL, MXU_DTYPE)])

    loss, dx, gsmall = _local_step(x, memb, pos, target, sp, ex)
    gshard = ex.shards

    out = {}
    for name, _, _, _ in BIG:
        w2, m2, v2 = inputs[name][0], inputs["m_" + name][0], inputs["v_" + name][0]
        n = w2.shape[1]
        d, nm, nv = _rowwise("adamw_" + name, _adamw, [w2, gshard[name], m2, v2], [], [(n, F32)] * 3, tm=_tile(w2.shape[0], 512))
        lead = inputs[name].shape
        out[name] = (gshard[name].reshape(lead), d.reshape(lead), nm.reshape(lead), nv.reshape(lead))

    def tiles(a):
        flat = a.reshape(-1)
        n = -(-flat.shape[0] // 1024) * 1024
        return jnp.pad(flat, (0, n - flat.shape[0])).reshape(n // 128, 128)

    pieces = [tiles(loss[:, :1])] + [tiles(gsmall[name]) for name in SMALL]
    if sum(p.shape[0] for p in pieces) % 16:
        pieces.append(jnp.zeros((8, 128), F32))
    red = _allreduce_small(jnp.concatenate(pieces, axis=0))
    loss_total = red[0, 0]
    grads, off = {}, pieces[0].shape[0]
    for name, p in zip(SMALL, pieces[1:]):
        shp = _as2d(inputs[name]).shape
        grads[name] = red[off:off + p.shape[0]].reshape(-1)[:shp[0] * shp[1]].reshape(shp)
        off += p.shape[0]
    upd = _adamw_small([(_as2d(inputs[n]), grads[n], _as2d(inputs["m_" + n]), _as2d(inputs["v_" + n])) for n in SMALL])
    for i, name in enumerate(SMALL):
        shp = inputs[name].shape
        out[name] = (grads[name].reshape(shp),) + tuple(t.reshape(shp) for t in upd[3 * i:3 * i + 3])
    return loss_total, dx.reshape(inputs["x"].shape), out


_ARG_NAMES = (("x", "mem", "positions") + WEIGHT_ORDER + ("loss_target",) + tuple("m_" + n for n in WEIGHT_ORDER)
              + tuple("v_" + n for n in WEIGHT_ORDER))


def kernel(x, mem, positions, ln_in_g, ln_in_b, w_in, b_in, ssm_log_dt, ssm_a_re, ssm_a_im, ssm_b_re, ssm_b_im, ssm_c_re, ssm_c_im, ssm_d, w_glu, b_glu, w_att_up, w_mix_out, b_mix_out, ln1_g, ln1_b, w_xq, w_xkv, w_xo, ln2_g, ln2_b, w_ff1, b_ff1, w_ff2, b_ff2, ln3_g, ln3_b, loss_target, m_ln_in_g, m_ln_in_b, m_w_in, m_b_in, m_ssm_log_dt, m_ssm_a_re, m_ssm_a_im, m_ssm_b_re, m_ssm_b_im, m_ssm_c_re, m_ssm_c_im, m_ssm_d, m_w_glu, m_b_glu, m_w_att_up, m_w_mix_out, m_b_mix_out, m_ln1_g, m_ln1_b, m_w_xq, m_w_xkv, m_w_xo, m_ln2_g, m_ln2_b, m_w_ff1, m_b_ff1, m_w_ff2, m_b_ff2, m_ln3_g, m_ln3_b, v_ln_in_g, v_ln_in_b, v_w_in, v_b_in, v_ssm_log_dt, v_ssm_a_re, v_ssm_a_im, v_ssm_b_re, v_ssm_b_im, v_ssm_c_re, v_ssm_c_im, v_ssm_d, v_w_glu, v_b_glu, v_w_att_up, v_w_mix_out, v_b_mix_out, v_ln1_g, v_ln1_b, v_w_xq, v_w_xkv, v_w_xo, v_ln2_g, v_ln2_b, v_w_ff1, v_b_ff1, v_w_ff2, v_b_ff2, v_ln3_g, v_ln3_b):
    args = (x, mem, positions, ln_in_g, ln_in_b, w_in, b_in, ssm_log_dt, ssm_a_re, ssm_a_im, ssm_b_re, ssm_b_im, ssm_c_re, ssm_c_im, ssm_d, w_glu, b_glu, w_att_up, w_mix_out, b_mix_out, ln1_g, ln1_b, w_xq, w_xkv, w_xo, ln2_g, ln2_b, w_ff1, b_ff1, w_ff2, b_ff2, ln3_g, ln3_b, loss_target, m_ln_in_g, m_ln_in_b, m_w_in, m_b_in, m_ssm_log_dt, m_ssm_a_re, m_ssm_a_im, m_ssm_b_re, m_ssm_b_im, m_ssm_c_re, m_ssm_c_im, m_ssm_d, m_w_glu, m_b_glu, m_w_att_up, m_w_mix_out, m_b_mix_out, m_ln1_g, m_ln1_b, m_w_xq, m_w_xkv, m_w_xo, m_ln2_g, m_ln2_b, m_w_ff1, m_b_ff1, m_w_ff2, m_b_ff2, m_ln3_g, m_ln3_b, v_ln_in_g, v_ln_in_b, v_w_in, v_b_in, v_ssm_log_dt, v_ssm_a_re, v_ssm_a_im, v_ssm_b_re, v_ssm_b_im, v_ssm_c_re, v_ssm_c_im, v_ssm_d, v_w_glu, v_b_glu, v_w_att_up, v_w_mix_out, v_b_mix_out, v_ln1_g, v_ln1_b, v_w_xq, v_w_xkv, v_w_xo, v_ln2_g, v_ln2_b, v_w_ff1, v_b_ff1, v_w_ff2, v_b_ff2, v_ln3_g, v_ln3_b)
    assert len(args) == len(_ARG_NAMES)
    inputs = dict(zip(_ARG_NAMES, args))
    loss, dx, out = _step(inputs)
    res = [loss, dx]
    for k in range(4):
        res += [out[name][k] for name in WEIGHT_ORDER]
    return tuple(res)
```

```python
import functools
import math

import numpy as np
import jax
import jax.numpy as jnp
from jax import lax
from jax.experimental import pallas as pl
from jax.experimental.pallas import tpu as pltpu

F32 = jnp.float32
BF16 = jnp.bfloat16
MXU_DTYPE = jnp.bfloat16

D_MODEL = 1024
SSM_GROUP = 16
SSM_WIDTH = 768
SSM_GROUPS = 48
SSM_STATE = 64
N_STATE = SSM_GROUPS * SSM_STATE
SSM_CHUNKS = 6
CH_W = 128
CH_N = 512
ATT_HEAD_DIM = 64
ATT_HPG = 4
ATT_GROUPW = ATT_HPG * ATT_HEAD_DIM
DILATIONS = (1, 4, 16)
ATT_BLK = 128
ATT_SCALE = ATT_HEAD_DIM ** -0.5
ROT_DIM = 16
ROPE_THETA = 500000.0
XATT_HEADS = 4
XATT_HEAD_DIM = 256
XATT_SCALE = XATT_HEAD_DIM ** -0.5
D_FF = 4096
IN_COLS = 5120
DEEPNORM_ALPHA = 2.0 ** 0.25
LN_EPS = 1e-5
NEG_INF = -1e30
ADAM_LR = 0.001
ADAM_B1 = 0.9
ADAM_B2 = 0.999
ADAM_EPS = 1e-08
ADAM_WD = 0.01
ADAM_STEP = 10

N_SEG = 32
VMEM_LIMIT = 56 * 1024 * 1024
MESH = pl.DeviceIdType.MESH
HBM_SPEC = pl.BlockSpec(memory_space=pltpu.HBM)
VMEM_SPEC = pl.BlockSpec(memory_space=pltpu.VMEM)

BIG = (("w_in", 1024, 5120, 1), ("w_glu", 768, 2048, 1), ("w_att_up", 256, 1024, 1),
       ("w_mix_out", 1024, 1024, 0), ("w_xq", 1024, 1024, 0), ("w_xkv", 1024, 2048, 1),
       ("w_xo", 1024, 1024, 0), ("w_ff1", 1024, 4096, 1), ("w_ff2", 4096, 1024, 0))
SMALL = ("ln_in_g", "ln_in_b", "b_in", "ssm_log_dt", "ssm_a_re", "ssm_a_im", "ssm_b_re", "ssm_b_im",
         "ssm_c_re", "ssm_c_im", "ssm_d", "b_glu", "b_mix_out", "ln1_g", "ln1_b", "ln2_g", "ln2_b",
         "b_ff1", "b_ff2", "ln3_g", "ln3_b")
WEIGHT_ORDER = ("ln_in_g", "ln_in_b", "w_in", "b_in", "ssm_log_dt", "ssm_a_re", "ssm_a_im", "ssm_b_re",
                "ssm_b_im", "ssm_c_re", "ssm_c_im", "ssm_d", "w_glu", "b_glu", "w_att_up", "w_mix_out",
                "b_mix_out", "ln1_g", "ln1_b", "w_xq", "w_xkv", "w_xo", "ln2_g", "ln2_b", "w_ff1", "b_ff1",
                "w_ff2", "b_ff2", "ln3_g", "ln3_b")


def _cparams(n_axes):
    return pltpu.CompilerParams(dimension_semantics=("arbitrary",) * n_axes, vmem_limit_bytes=VMEM_LIMIT)


class _Carry:
    def __init__(self, ins, outs, n_sems, start, finish, done):
        self.ins, self.outs, self.n_sems, self.start, self.finish, self.done = ins, outs, n_sems, start, finish, done


def _call(name, body, grid, in_specs, out_specs, out_shape, args, scratch_shapes=(), carry=None):
    in_specs, out_specs, out_shape = list(in_specs), list(out_specs), list(out_shape)
    params = _cparams(len(grid))
    if carry is None:
        return pl.pallas_call(body, name=name, grid=grid, in_specs=in_specs, out_specs=out_specs, out_shape=out_shape,
                              scratch_shapes=list(scratch_shapes), compiler_params=params)(*args)
    n_in, n_out, n_ci, n_co = len(in_specs), len(out_specs), len(carry.ins), len(carry.outs)
    n_scr = len(scratch_shapes)

    def wrapped(*refs):
        ins, c_in = refs[:n_in], refs[n_in:n_in + n_ci]
        outs, c_out = refs[n_in + n_ci:n_in + n_ci + n_out], refs[n_in + n_ci + n_out:n_in + n_ci + n_out + n_co]
        scratch = refs[n_in + n_ci + n_out + n_co:n_in + n_ci + n_out + n_co + n_scr]
        send_sems, recv_sems = refs[-2:]
        ids = [pl.program_id(a) for a in range(len(grid))]
        first = functools.reduce(jnp.logical_and, [i == 0 for i in ids])
        last = functools.reduce(jnp.logical_and, [i == g - 1 for i, g in zip(ids, grid)])

        @pl.when(first)
        def _():
            carry.start(c_in, c_out, send_sems, recv_sems)

        body(*ins, *outs, *scratch)

        @pl.when(last)
        def _():
            carry.finish(c_in, c_out, send_sems, recv_sems)

    c_shapes = [jax.ShapeDtypeStruct(carry.ins[o].shape, carry.ins[o].dtype) if isinstance(o, int) else o
                for o in carry.outs]
    aliases = {n_in + o: n_out + i for i, o in enumerate(carry.outs) if isinstance(o, int)}
    res = pl.pallas_call(
        wrapped, name=name, grid=grid, in_specs=in_specs + [HBM_SPEC] * n_ci, out_specs=out_specs + [HBM_SPEC] * n_co,
        out_shape=out_shape + c_shapes, input_output_aliases=aliases,
        scratch_shapes=list(scratch_shapes) + [pltpu.SemaphoreType.DMA((carry.n_sems,))] * 2,
        compiler_params=params)(*args, *carry.ins)
    carry.done(res[n_out:])
    return res[:n_out]


def _rowwise(name, fn, rows, consts, outs, reds=(), tm=512, touts=(), carry=None):
    n_rows = (rows[0][0] if isinstance(rows[0], tuple) else rows[0]).shape[-2]
    tm = min(tm, n_rows)
    assert n_rows % tm == 0, (name, n_rows, tm)
    specs, args = [], []
    for r in rows:
        if isinstance(r, tuple) and len(r) == 3:
            arr, width, cb = r
            specs.append(pl.BlockSpec((tm, width), functools.partial(lambda i, cb: (i, cb), cb=cb)))
        elif isinstance(r, tuple):
            arr, slot = r
            specs.append(pl.BlockSpec((None, tm, arr.shape[2]), functools.partial(lambda i, s: (s, i, 0), s=slot)))
        else:
            arr = r
            specs.append(pl.BlockSpec((tm, arr.shape[1]), lambda i: (i, 0)))
        args.append(arr)
        assert arr.shape[-2] == n_rows, (name, arr.shape, n_rows)
    for cst in consts:
        specs.append(pl.BlockSpec(cst.shape, lambda i: (0, 0)))
        args.append(cst)
    n_r, n_c, n_o, n_d = len(rows), len(consts), len(outs) + len(touts), len(reds)
    out_shape = [jax.ShapeDtypeStruct((n_rows, c), dt) for c, dt in outs]
    out_specs = [pl.BlockSpec((tm, c), lambda i: (i, 0)) for c, _ in outs]
    out_shape += [jax.ShapeDtypeStruct((r, n_rows), dt) for r, dt in touts]
    out_specs += [pl.BlockSpec((r, tm), lambda i: (0, i)) for r, _ in touts]
    out_shape += [jax.ShapeDtypeStruct((1, c), F32) for c in reds]
    out_specs += [pl.BlockSpec((1, c), lambda i: (0, 0)) for c in reds]

    def body(*refs):
        ins = [r[...] for r in refs[:n_r + n_c]]
        o_refs = refs[n_r + n_c:n_r + n_c + n_o]
        d_refs = refs[n_r + n_c + n_o:]
        res = fn(*ins)
        res = res if isinstance(res, (tuple, list)) else (res,)
        assert len(res) == n_o + n_d, (name, len(res))
        for ref, val in zip(o_refs, res[:n_o]):
            ref[...] = val.astype(ref.dtype)
        first = pl.program_id(0) == 0
        for ref, val in zip(d_refs, res[n_o:]):
            @pl.when(first)
            def _(ref=ref, val=val):
                ref[...] = val

            @pl.when(jnp.logical_not(first))
            def _(ref=ref, val=val):
                ref[...] += val

    return _call(name, body, (n_rows // tm,), specs, out_specs, out_shape, args, carry=carry)


def _colsum(v):
    return jnp.sum(v.astype(F32), axis=0, keepdims=True)


_DIMS = {"nn": (((1,), (0,)), ((), ())), "nt": (((1,), (1,)), ((), ())), "tn": (((0,), (0,)), ((), ()))}


def _tile(dim, want):
    if dim <= want:
        return dim
    return max(t for t in range(128, want + 1, 128) if dim % t == 0)


def _dot(a, b, mode):
    return lax.dot_general(a.astype(MXU_DTYPE), b.astype(MXU_DTYPE), _DIMS[mode], preferred_element_type=F32)


def _mm(name, a, b, mode, *, bias=None, extras=(), epilogue=None, out_dtypes=(F32,), tm=1024, tn=1024, tk=1024,
        carry=None, colsum=False):
    if mode == "nn":
        (m, k), (_, n) = a.shape, b.shape
    elif mode == "nt":
        (m, k), (n, _) = a.shape, b.shape
    else:
        (k, m), (_, n) = a.shape, b.shape
    if k > tk:
        tk = 5 * tk
    tn = _tile(n, tn)
    tk = _tile(k, tk)
    nk = k // tk

    def vmem_bytes(rows):
        blocks = rows * tk * a.dtype.itemsize + tk * tn * b.dtype.itemsize
        blocks += sum(rows * tn * e.dtype.itemsize for e in extras)
        blocks += sum(rows * tn * jnp.dtype(dt).itemsize for dt in out_dtypes)
        return 2 * blocks + (rows * tn * 4 if nk > 1 else 0)

    tm = _tile(m, tm if mode == "tn" else 2 * tm)
    while vmem_bytes(tm) > 3 * VMEM_LIMIT // 4 and tm % 256 == 0:
        tm //= 2
    while nk == 1 and k > 1024 and (m // tm) * (n // tn) < 4 and tm % 256 == 0:
        tm //= 2
    assert m % tm == 0 and n % tn == 0 and k % tk == 0, (name, m, n, k)
    a_spec = {"nn": pl.BlockSpec((tm, tk), lambda i, j, kk: (i, kk)),
              "nt": pl.BlockSpec((tm, tk), lambda i, j, kk: (i, kk)),
              "tn": pl.BlockSpec((tk, tm), lambda i, j, kk: (kk, i))}[mode]
    b_spec = {"nn": pl.BlockSpec((tk, tn), lambda i, j, kk: (kk, j)),
              "nt": pl.BlockSpec((tn, tk), lambda i, j, kk: (j, kk)),
              "tn": pl.BlockSpec((tk, tn), lambda i, j, kk: (kk, j))}[mode]
    specs, args = [a_spec, b_spec], [a, b]
    if bias is not None:
        specs.append(pl.BlockSpec((1, tn), lambda i, j, kk: (0, j)))
        args.append(bias)
    for e in extras:
        specs.append(pl.BlockSpec((tm, tn), lambda i, j, kk: (i, j)))
        args.append(e)
    n_e, n_o = len(extras), len(out_dtypes)
    has_bias = bias is not None

    def body(*refs):
        a_ref, b_ref = refs[0], refs[1]
        pos = 2
        bias_ref = refs[pos] if has_bias else None
        pos += int(has_bias)
        e_refs = refs[pos:pos + n_e]
        o_refs = refs[pos + n_e:pos + n_e + n_o]
        sum_ref = refs[pos + n_e + n_o] if colsum else None
        acc_ref = refs[pos + n_e + n_o + int(colsum)] if nk > 1 else None
        part = _dot(a_ref[...], b_ref[...], mode)

        def finish(r):
            if has_bias:
                r = r + bias_ref[...]
            res = epilogue(r, *[e[...] for e in e_refs]) if epilogue is not None else (r,)
            for ref, val in zip(o_refs, res):
                ref[...] = val.astype(ref.dtype)
            if colsum:
                sum_ref[...] = _colsum(res[0])

        if nk == 1:
            finish(part)
        else:
            kk = pl.program_id(2)

            @pl.when(kk == 0)
            def _():
                acc_ref[...] = part

            @pl.when(kk > 0)
            def _():
                acc_ref[...] += part

            @pl.when(kk == nk - 1)
            def _():
                finish(acc_ref[...])

    out_specs = [pl.BlockSpec((tm, tn), lambda i, j, kk: (i, j)) for _ in out_dtypes]
    out_shape = [jax.ShapeDtypeStruct((m, n), dt) for dt in out_dtypes]
    if colsum:
        out_specs.append(pl.BlockSpec((None, 1, tn), lambda i, j, kk: (i, 0, j)))
        out_shape.append(jax.ShapeDtypeStruct((m // tm, 1, n), F32))
    res = _call(name, body, (m // tm, n // tn, nk), specs, out_specs, out_shape, args,
                scratch_shapes=[pltpu.VMEM((tm, tn), F32)] if nk > 1 else [], carry=carry)
    return res[0] if len(res) == 1 else res


def _ssm_wgrads(u, dy, g_re, g_im, h_re, h_im, tk=2048, carry=None):
    s = u.shape[0]
    tk = min(tk, s)
    nk = s // tk
    assert tk % N_SEG == 0

    def body(u_ref, dy_ref, gre_ref, gim_ref, hre_ref, him_ref, lre_ref, lim_ref, db_ref, dc_ref, dar_ref, dai_ref,
             pre_ref, pim_ref):
        kk = pl.program_id(1)
        u_blk, dy_blk = u_ref[...], dy_ref[...]
        g_r, g_i, h_r, h_i = gre_ref[...], gim_ref[...], hre_ref[...], him_ref[...]
        d_b = jnp.concatenate([_dot(u_blk, g_r, "tn"), _dot(u_blk, g_i, "tn")], axis=1)
        d_c = jnp.concatenate([_dot(h_r, dy_blk, "tn"), _dot(h_i, dy_blk, "tn")], axis=0)

        @pl.when(kk == 0)
        def _():
            first_row = lax.broadcasted_iota(jnp.int32, (N_SEG, CH_N), 0) == 0
            pre_ref[...] = jnp.where(first_row, 0.0, pltpu.roll(lre_ref[...], 1, 0))
            pim_ref[...] = jnp.where(first_row, 0.0, pltpu.roll(lim_ref[...], 1, 0))

        p_r = jnp.concatenate([pre_ref[...], h_r[:tk - N_SEG]], axis=0)
        p_i = jnp.concatenate([pim_ref[...], h_i[:tk - N_SEG]], axis=0)
        pre_ref[...] = h_r[tk - N_SEG:]
        pim_ref[...] = h_i[tk - N_SEG:]
        d_ar = jnp.sum(g_r * p_r + g_i * p_i, axis=0, keepdims=True)
        d_ai = jnp.sum(g_i * p_r - g_r * p_i, axis=0, keepdims=True)

        @pl.when(kk == 0)
        def _():
            db_ref[...] = d_b
            dc_ref[...] = d_c
            dar_ref[...] = d_ar
            dai_ref[...] = d_ai

        @pl.when(kk > 0)
        def _():
            db_ref[...] += d_b
            dc_ref[...] += d_c
            dar_ref[...] += d_ar
            dai_ref[...] += d_ai

    chan = pl.BlockSpec((tk, CH_W), lambda j, kk: (kk, j))
    state = pl.BlockSpec((tk, CH_N), lambda j, kk: (kk, j))
    last = pl.BlockSpec((N_SEG, CH_N), lambda j, kk: (s // N_SEG - 1, j))
    row = pl.BlockSpec((1, CH_N), lambda j, kk: (0, j))
    return _call(
        "ssm_wgrads", body, (SSM_CHUNKS, nk), [chan, chan, state, state, state, state, last, last],
        [pl.BlockSpec((None, CH_W, 2 * CH_N), lambda j, kk: (j, 0, 0)),
         pl.BlockSpec((None, 2 * CH_N, CH_W), lambda j, kk: (j, 0, 0)), row, row],
        [jax.ShapeDtypeStruct((SSM_CHUNKS, CH_W, 2 * CH_N), F32), jax.ShapeDtypeStruct((SSM_CHUNKS, 2 * CH_N, CH_W), F32),
         jax.ShapeDtypeStruct((1, N_STATE), F32), jax.ShapeDtypeStruct((1, N_STATE), F32)],
        (u, dy, g_re, g_im, h_re, h_im, h_re, h_im), scratch_shapes=[pltpu.VMEM((N_SEG, CH_N), F32)] * 2, carry=carry)


SCAN_LB = 256


def _split_by_scan_block(mat, axis):
    halves = []
    for l in range(CH_N // SCAN_LB):
        re = lax.slice_in_dim(mat, l * SCAN_LB, (l + 1) * SCAN_LB, axis=axis)
        im = lax.slice_in_dim(mat, CH_N + l * SCAN_LB, CH_N + (l + 1) * SCAN_LB, axis=axis)
        halves.append(jnp.concatenate([re, im], axis=axis))
    return jnp.stack(halves, axis=1).reshape((-1,) + halves[0].shape[1:])


def _ssm_scan(name, chan, expand12, contract12, a_re, a_im, d_row, reverse, carry=None):
    s = chan.shape[0]
    seg_len = s // N_SEG
    n_sq = int(math.log2(seg_len))
    assert 2 ** n_sq == seg_len
    rb = min(512, s)
    per_chunk = CH_N // SCAN_LB

    def body(are_ref, aim_ref, ch_ref, e_ref, k_ref, d_ref, hre_ref, him_ref, o_ref, wre_ref, wim_ref, ere, eim, cre, cim):
        e_mat, k_mat = e_ref[...], k_ref[...]
        for r in range(s // rb):
            rows = slice(r * rb, (r + 1) * rb)
            w = _dot(ch_ref[rows, :], e_mat, "nt" if reverse else "nn")
            wre_ref[rows, :] = w[:, :SCAN_LB]
            wim_ref[rows, :] = w[:, SCAN_LB:]

        ar1 = are_ref[...]
        ai1 = -aim_ref[...] if reverse else aim_ref[...]
        ar = jnp.broadcast_to(ar1, (N_SEG, SCAN_LB))
        ai = jnp.broadcast_to(ai1, (N_SEG, SCAN_LB))

        def rows_of(k):
            kk = seg_len - 1 - k if reverse else k
            return pl.ds(pl.multiple_of(kk * N_SEG, N_SEG), N_SEG)

        def local(k, carry):
            hr, hi = carry
            rows = rows_of(k)
            nr = ar * hr - ai * hi + wre_ref[rows, :]
            ni = ar * hi + ai * hr + wim_ref[rows, :]
            hre_ref[rows, :] = nr
            him_ref[rows, :] = ni
            return nr, ni

        zero = jnp.zeros((N_SEG, SCAN_LB), F32)
        er, ei = lax.fori_loop(0, seg_len, local, (zero, zero))
        ere[...] = er
        eim[...] = ei
        pr, pi = ar1, ai1
        for _ in range(n_sq):
            pr, pi = pr * pr - pi * pi, 2.0 * pr * pi
        cr = jnp.zeros((1, SCAN_LB), F32)
        ci = jnp.zeros((1, SCAN_LB), F32)
        for jj in range(N_SEG):
            j = N_SEG - 1 - jj if reverse else jj
            cre[j:j + 1, :] = cr
            cim[j:j + 1, :] = ci
            er_j, ei_j = ere[j:j + 1, :], eim[j:j + 1, :]
            cr, ci = pr * cr - pi * ci + er_j, pr * ci + pi * cr + ei_j
        c_r, c_i = cre[...], cim[...]

        def fix(k, carry):
            qr, qi = carry
            rows = rows_of(k)
            hre_ref[rows, :] = hre_ref[rows, :] + (qr * c_r - qi * c_i)
            him_ref[rows, :] = him_ref[rows, :] + (qr * c_i + qi * c_r)
            return qr * ar - qi * ai, qr * ai + qi * ar

        lax.fori_loop(0, seg_len, fix, (ar, ai))

        first_of_chunk = lax.rem(pl.program_id(0), per_chunk) == 0
        for r in range(s // rb):
            rows = slice(r * rb, (r + 1) * rb)
            h_cat = jnp.concatenate([hre_ref[rows, :], him_ref[rows, :]], axis=1)
            part = _dot(h_cat, k_mat, "nt" if reverse else "nn")

            @pl.when(first_of_chunk)
            def _(rows=rows, part=part):
                o_ref[rows, :] = part + d_ref[...] * ch_ref[rows, :]

            @pl.when(jnp.logical_not(first_of_chunk))
            def _(rows=rows, part=part):
                o_ref[rows, :] += part

    nblk = N_STATE // SCAN_LB
    blk = pl.BlockSpec((s, SCAN_LB), lambda b: (0, b))
    row = pl.BlockSpec((1, SCAN_LB), lambda b: (0, b))
    chan_blk = pl.BlockSpec((s, CH_W), lambda b: (0, b // per_chunk))
    res = _call(name, body, (nblk,),
                [row, row, chan_blk, pl.BlockSpec((None,) + expand12.shape[1:], lambda b: (b, 0, 0)),
                 pl.BlockSpec((None,) + contract12.shape[1:], lambda b: (b, 0, 0)),
                 pl.BlockSpec((1, CH_W), lambda b: (0, b // per_chunk))],
                [blk, blk, chan_blk],
                [jax.ShapeDtypeStruct((s, N_STATE), F32)] * 2 + [jax.ShapeDtypeStruct((s, SSM_WIDTH), F32)],
                (a_re, a_im, chan, expand12, contract12, d_row),
                scratch_shapes=[pltpu.VMEM((s, SCAN_LB), F32)] * 2 + [pltpu.VMEM((N_SEG, SCAN_LB), F32)] * 4, carry=carry)
    return res[0], res[1], res[2]


def _disc(ldt, are, aim, bre, bim):
    dt = jnp.exp(ldt)
    mag = jnp.exp(are * dt)
    abr = mag * jnp.cos(aim * dt)
    abi = mag * jnp.sin(aim * dt)
    den = jnp.square(are) + jnp.square(aim)
    nr = abr - 1.0
    fre = (nr * are + abi * aim) / den
    fim = (abi * are - nr * aim) / den
    return abr, abi, fre * bre - fim * bim, fre * bim + fim * bre


def _ssm_disc_fwd(ldt, are, aim, bre, bim):
    def body(l_ref, ar_ref, ai_ref, br_ref, bi_ref, o0, o1, o2, o3):
        res = _disc(l_ref[...], ar_ref[...], ai_ref[...], br_ref[...], bi_ref[...])
        for ref, val in zip((o0, o1, o2, o3), res):
            ref[...] = val

    col = jax.ShapeDtypeStruct((N_STATE, 1), F32)
    mat = jax.ShapeDtypeStruct((N_STATE, SSM_GROUP), F32)
    return pl.pallas_call(body, name="ssm_disc_fwd", out_shape=[col, col, mat, mat],
                          in_specs=[VMEM_SPEC] * 5, out_specs=[VMEM_SPEC] * 4)(ldt, are, aim, bre, bim)


def _ssm_disc_bwd(ldt, are, aim, bre, bim, d_abr, d_abi, d_bbr, d_bbi):
    def body(l_ref, ar_ref, ai_ref, br_ref, bi_ref, c0, c1, c2, c3, g_ldt, g_are, g_aim, g_bre, g_bim):
        _, vjp = jax.vjp(_disc, l_ref[...], ar_ref[...], ai_ref[...], br_ref[...], bi_ref[...])
        dl, dar, dai, dbr, dbi = vjp((c0[...], c1[...], c2[...], c3[...]))
        state = lax.broadcasted_iota(jnp.int32, (N_STATE, SSM_GROUPS), 0)
        group = lax.broadcasted_iota(jnp.int32, (N_STATE, SSM_GROUPS), 1)
        pick = jnp.right_shift(state, 6) == group
        g_ldt[...] = jnp.sum(jnp.where(pick, dl, 0.0), axis=0, keepdims=True)
        g_are[...] = dar
        g_aim[...] = dai
        g_bre[...] = dbr
        g_bim[...] = dbi

    col = jax.ShapeDtypeStruct((N_STATE, 1), F32)
    mat = jax.ShapeDtypeStruct((N_STATE, SSM_GROUP), F32)
    return pl.pallas_call(body, name="ssm_disc_bwd",
                          out_shape=[jax.ShapeDtypeStruct((1, SSM_GROUPS), F32), col, col, mat, mat],
                          in_specs=[VMEM_SPEC] * 9, out_specs=[VMEM_SPEC] * 5,
                          compiler_params=pltpu.CompilerParams(vmem_limit_bytes=VMEM_LIMIT))(
        ldt, are, aim, bre, bim, d_abr, d_abi, d_bbr, d_bbi)


_EYE8 = np.eye(8, dtype=np.float32)


def _blockdiag_b(bb):
    t = bb.reshape(SSM_CHUNKS, 8, SSM_STATE, SSM_GROUP).transpose(0, 1, 3, 2)
    return jnp.einsum("igcn,gh->igchn", t, _EYE8).reshape(SSM_CHUNKS, CH_W, CH_N)


def _diag_of_b(m):
    t = jnp.einsum("igchn,gh->igcn", m.reshape(SSM_CHUNKS, 8, SSM_GROUP, 8, SSM_STATE), _EYE8)
    return t.transpose(0, 1, 3, 2).reshape(N_STATE, SSM_GROUP)


def _blockdiag_c(c):
    t = c.reshape(SSM_CHUNKS, 8, SSM_GROUP, SSM_STATE).transpose(0, 1, 3, 2)
    return jnp.einsum("ignc,gh->ignhc", t, _EYE8).reshape(SSM_CHUNKS, CH_N, CH_W)


def _diag_of_c(m):
    t = jnp.einsum("ignhc,gh->ignc", m.reshape(SSM_CHUNKS, 8, SSM_STATE, 8, SSM_GROUP), _EYE8)
    return t.transpose(0, 1, 3, 2).reshape(SSM_GROUPS, SSM_GROUP, SSM_STATE)


def _time_perm(a):
    s, c = a.shape
    return a.reshape(N_SEG, s // N_SEG, c).transpose(1, 0, 2).reshape(s, c)


def _time_unperm(a):
    s, c = a.shape
    return a.reshape(s // N_SEG, N_SEG, c).transpose(1, 0, 2).reshape(s, c)


def _dilate(a, d):
    s, c = a.shape
    return a if d == 1 else a.reshape(s // d, d, c).transpose(1, 0, 2).reshape(s, c)


def _undilate(a, d):
    s, c = a.shape
    return a if d == 1 else a.reshape(d, s // d, c).transpose(1, 0, 2).reshape(s, c)


def _dilate_rows(a, d):
    r, s = a.shape
    return a if d == 1 else a.reshape(r, s // d, d).transpose(0, 2, 1).reshape(r, s)


ATT_T_FWD = 4
ATT_T_BWD = 8


def _window(prev_ref, cur_ref, i, sl):
    if i == 0:
        return jnp.concatenate([prev_ref[:, sl], cur_ref[0:ATT_BLK, sl]], axis=0)
    return cur_ref[(i - 1) * ATT_BLK:(i + 1) * ATT_BLK, sl]


def _band_valid(first_key):
    qi = lax.broadcasted_iota(jnp.int32, (ATT_BLK, 2 * ATT_BLK), 0)
    ki = lax.broadcasted_iota(jnp.int32, (ATT_BLK, 2 * ATT_BLK), 1)
    steps = qi + ATT_BLK - ki
    return (steps >= 0) & (steps <= ATT_BLK) & (ki >= first_key)


ATT_STATW = ATT_HPG * 128


def _stat(h):
    return slice(h * 128, (h + 1) * 128)


def _stat_rows(stat):
    n = stat.shape[0]
    heads = [stat[:, _stat(h)].T[0:1, :] for h in range(ATT_HPG)]
    return jnp.concatenate(heads + [jnp.zeros((8 - ATT_HPG, n), stat.dtype)], axis=0)


def _attn_specs(nb, t, width=ATT_GROUPW):
    cur = pl.BlockSpec((t * ATT_BLK, width), lambda b: (b, 0))
    prev = pl.BlockSpec((ATT_BLK, width), lambda b: (jnp.maximum(b * t - 1, 0), 0))
    nxt = pl.BlockSpec((ATT_BLK, width), lambda b: (jnp.minimum((b + 1) * t, nb - 1), 0))
    return cur, prev, nxt


def _attn_fwd(tag, per_seq, q, k, v):
    s = q.shape[0]
    nb = s // ATT_BLK

    def body(q_ref, kc_ref, kp_ref, vc_ref, vp_ref, o_ref, lse_ref):
        bt = pl.program_id(0)
        for i in range(ATT_T_FWD):
            has_prev = lax.rem(bt * ATT_T_FWD + i, per_seq) > 0
            valid = _band_valid(jnp.where(has_prev, 0, ATT_BLK))
            rows = slice(i * ATT_BLK, (i + 1) * ATT_BLK)
            for h in range(ATT_HPG):
                sl = slice(h * ATT_HEAD_DIM, (h + 1) * ATT_HEAD_DIM)
                kcat = _window(kp_ref, kc_ref, i, sl)
                vcat = _window(vp_ref, vc_ref, i, sl)
                sc = _dot(q_ref[rows, sl], kcat, "nt") * ATT_SCALE
                sc = jnp.where(valid, sc, NEG_INF)
                m = jnp.max(sc, axis=-1, keepdims=True)
                p = jnp.exp(sc - m)
                den = jnp.sum(p, axis=-1, keepdims=True)
                o_ref[rows, sl] = _dot(p, vcat, "nn") / den
                lse_ref[rows, _stat(h)] = jnp.broadcast_to(m + jnp.log(den), (ATT_BLK, 128))

    cur, prev, _ = _attn_specs(nb, ATT_T_FWD)
    stat, _, _ = _attn_specs(nb, ATT_T_FWD, ATT_STATW)
    return pl.pallas_call(
        body, name="attn_fwd_" + tag, grid=(nb // ATT_T_FWD,), in_specs=[cur, cur, prev, cur, prev], out_specs=[cur, stat],
        out_shape=[jax.ShapeDtypeStruct((s, ATT_GROUPW), F32), jax.ShapeDtypeStruct((s, ATT_STATW), F32)],
        compiler_params=_cparams(1))(q, k, k, v, v)


def _attn_dq(tag, per_seq, q, k, v, do, lse, delta):
    s = q.shape[0]
    nb = s // ATT_BLK

    def body(q_ref, kc_ref, kp_ref, vc_ref, vp_ref, do_ref, lse_ref, dl_ref, dq_ref):
        bt = pl.program_id(0)
        for i in range(ATT_T_BWD):
            has_prev = lax.rem(bt * ATT_T_BWD + i, per_seq) > 0
            valid = _band_valid(jnp.where(has_prev, 0, ATT_BLK))
            rows = slice(i * ATT_BLK, (i + 1) * ATT_BLK)
            for h in range(ATT_HPG):
                sl = slice(h * ATT_HEAD_DIM, (h + 1) * ATT_HEAD_DIM)
                kcat = _window(kp_ref, kc_ref, i, sl)
                vcat = _window(vp_ref, vc_ref, i, sl)
                lse = jnp.concatenate([lse_ref[rows, _stat(h)]] * 2, axis=1)
                dlt = jnp.concatenate([dl_ref[rows, _stat(h)]] * 2, axis=1)
                sc = _dot(q_ref[rows, sl], kcat, "nt") * ATT_SCALE
                p = jnp.exp(jnp.where(valid, sc, NEG_INF) - lse)
                dp = _dot(do_ref[rows, sl], vcat, "nt")
                ds = p * (dp - dlt) * ATT_SCALE
                dq_ref[rows, sl] = _dot(ds, kcat, "nn")

    cur, prev, _ = _attn_specs(nb, ATT_T_BWD)
    stat, _, _ = _attn_specs(nb, ATT_T_BWD, ATT_STATW)
    return pl.pallas_call(
        body, name="attn_dq_" + tag, grid=(nb // ATT_T_BWD,), in_specs=[cur, cur, prev, cur, prev, cur, stat, stat],
        out_specs=cur, out_shape=jax.ShapeDtypeStruct((s, ATT_GROUPW), F32),
        compiler_params=_cparams(1))(q, k, k, v, v, do, lse, delta)


def _attn_dkv(tag, per_seq, q, k, v, do, lse_t, delta_t):
    s = q.shape[0]
    nb = s // ATT_BLK

    def body(k_ref, v_ref, qc_ref, qn_ref, doc_ref, don_ref, lc_ref, ln_ref, dc_ref, dn_ref, dk_ref, dv_ref):
        bt = pl.program_id(0)
        ki = lax.broadcasted_iota(jnp.int32, (ATT_BLK, 2 * ATT_BLK), 0)
        ci = lax.broadcasted_iota(jnp.int32, (ATT_BLK, 2 * ATT_BLK), 1)

        def pair(edge_ref, cur_ref, i, sl):
            if i == ATT_T_BWD - 1:
                return jnp.concatenate([cur_ref[i * ATT_BLK:(i + 1) * ATT_BLK, sl], edge_ref[:, sl]], axis=0)
            return cur_ref[i * ATT_BLK:(i + 2) * ATT_BLK, sl]

        def pair_row(edge_ref, cur_ref, i, h):
            if i == ATT_T_BWD - 1:
                row = jnp.concatenate([cur_ref[h:h + 1, i * ATT_BLK:(i + 1) * ATT_BLK], edge_ref[h:h + 1, :]], axis=1)
            else:
                row = cur_ref[h:h + 1, i * ATT_BLK:(i + 2) * ATT_BLK]
            return jnp.broadcast_to(row, (ATT_BLK, 2 * ATT_BLK))

        for i in range(ATT_T_BWD):
            b = bt * ATT_T_BWD + i
            next_uses = (b + 1 < nb) & (lax.rem(b + 1, per_seq) > 0)
            reach = jnp.where(next_uses, 0, 4 * ATT_BLK)
            valid = ((ci < ATT_BLK) & (ci >= ki)) | ((ci >= ATT_BLK) & (ki - ci + ATT_BLK >= reach))
            rows = slice(i * ATT_BLK, (i + 1) * ATT_BLK)
            for h in range(ATT_HPG):
                sl = slice(h * ATT_HEAD_DIM, (h + 1) * ATT_HEAD_DIM)
                qcat, docat = pair(qn_ref, qc_ref, i, sl), pair(don_ref, doc_ref, i, sl)
                sc = _dot(k_ref[rows, sl], qcat, "nt") * ATT_SCALE
                p = jnp.exp(jnp.where(valid, sc, NEG_INF) - pair_row(ln_ref, lc_ref, i, h))
                dv_ref[rows, sl] = _dot(p, docat, "nn")
                dp = _dot(v_ref[rows, sl], docat, "nt")
                ds = p * (dp - pair_row(dn_ref, dc_ref, i, h)) * ATT_SCALE
                dk_ref[rows, sl] = _dot(ds, qcat, "nn")

    cur, _, nxt = _attn_specs(nb, ATT_T_BWD)
    stat = pl.BlockSpec((8, ATT_T_BWD * ATT_BLK), lambda b: (0, b))
    snxt = pl.BlockSpec((8, ATT_BLK), lambda b: (0, jnp.minimum((b + 1) * ATT_T_BWD, nb - 1)))
    return pl.pallas_call(
        body, name="attn_dkv_" + tag, grid=(nb // ATT_T_BWD,), in_specs=[cur, cur, cur, nxt, cur, nxt, stat, snxt, stat, snxt],
        out_specs=[cur, cur], out_shape=[jax.ShapeDtypeStruct((s, ATT_GROUPW), F32)] * 2,
        compiler_params=_cparams(1))(k, v, q, q, do, do, lse_t, lse_t, delta_t, delta_t)


def _xattn_probs(q, kh):
    sc = _dot(q, kh, "nt") * XATT_SCALE
    e = jnp.exp(sc - jnp.max(sc, axis=-1, keepdims=True))
    return e / jnp.sum(e, axis=-1, keepdims=True)


def _xattn_fwd(q, kv, tm=512):
    s = q.shape[0]
    tm = min(tm, s)

    def body(q_ref, kv_ref, o_ref):
        for h in range(XATT_HEADS):
            sl = slice(h * XATT_HEAD_DIM, (h + 1) * XATT_HEAD_DIM)
            vs = slice(D_MODEL + h * XATT_HEAD_DIM, D_MODEL + (h + 1) * XATT_HEAD_DIM)
            p = _xattn_probs(q_ref[:, sl], kv_ref[:, sl])
            o_ref[:, sl] = _dot(p, kv_ref[:, vs], "nn").astype(o_ref.dtype)

    return pl.pallas_call(
        body, name="xattn_fwd", grid=(s // tm,),
        in_specs=[pl.BlockSpec((tm, D_MODEL), lambda i: (i, 0)), pl.BlockSpec(kv.shape, lambda i: (0, 0))],
        out_specs=pl.BlockSpec((tm, D_MODEL), lambda i: (i, 0)),
        out_shape=jax.ShapeDtypeStruct((s, D_MODEL), MXU_DTYPE), compiler_params=_cparams(1))(q, kv)


def _xattn_bwd(q, kv, do, tm=1024):
    s = q.shape[0]
    tm = min(tm, s)

    def body(q_ref, kv_ref, do_ref, dq_ref, dkv_ref):
        first = pl.program_id(0) == 0

        @pl.when(first)
        def _():
            dkv_ref[...] = jnp.zeros_like(dkv_ref)

        for h in range(XATT_HEADS):
            sl = slice(h * XATT_HEAD_DIM, (h + 1) * XATT_HEAD_DIM)
            vs = slice(D_MODEL + h * XATT_HEAD_DIM, D_MODEL + (h + 1) * XATT_HEAD_DIM)
            p = _xattn_probs(q_ref[:, sl], kv_ref[:, sl])
            dkv_ref[:, vs] += _dot(p, do_ref[:, sl], "tn")
            dp = _dot(do_ref[:, sl], kv_ref[:, vs], "nt")
            ds = p * (dp - jnp.sum(dp * p, axis=-1, keepdims=True)) * XATT_SCALE
            dq_ref[:, sl] = _dot(ds, kv_ref[:, sl], "nn").astype(dq_ref.dtype)
            dkv_ref[:, sl] += _dot(ds, q_ref[:, sl], "tn")

    row = pl.BlockSpec((tm, D_MODEL), lambda i: (i, 0))
    whole = pl.BlockSpec(kv.shape, lambda i: (0, 0))
    return pl.pallas_call(
        body, name="xattn_bwd", grid=(s // tm,), in_specs=[row, whole, row], out_specs=[row, whole],
        out_shape=[jax.ShapeDtypeStruct((s, D_MODEL), MXU_DTYPE), jax.ShapeDtypeStruct(kv.shape, F32)],
        compiler_params=_cparams(1))(q, kv, do)


def _ln(x, g, b):
    mu = jnp.mean(x, axis=-1, keepdims=True)
    xc = x - mu
    var = jnp.mean(jnp.square(xc), axis=-1, keepdims=True)
    return xc * lax.rsqrt(var + LN_EPS) * g + b


def _res_ln(h, o, g, b):
    return _ln(DEEPNORM_ALPHA * h + o, g, b)


def _gate(gs, ga, z1, z2, batt):
    return jax.nn.sigmoid(gs) * (z1 * jax.nn.sigmoid(z2)) + jax.nn.sigmoid(ga) * batt


ROPE_TW = 2 * ATT_HEAD_DIM


def _rope_tables(pos, invf, m1, m2):
    ang = pos.astype(F32) * invf
    sin = jnp.sin(ang)
    return jnp.cos(ang), -sin * m1, sin * m2


def _widen(tab):
    return jnp.concatenate([tab] * (ATT_GROUPW // ROPE_TW), axis=1)


def _rope(t, cos, s_up, s_dn):
    w = t.shape[-1]
    return t * cos + pltpu.roll(t, w - ROT_DIM // 2, 1) * s_up + pltpu.roll(t, ROT_DIM // 2, 1) * s_dn


def _rope_t(dt, cos, s_up, s_dn):
    w = dt.shape[-1]
    return dt * cos + pltpu.roll(dt * s_up, ROT_DIM // 2, 1) + pltpu.roll(dt * s_dn, w - ROT_DIM // 2, 1)


def _rope_consts():
    inv_freq = ROPE_THETA ** (-jnp.arange(0, ROT_DIM, 2, dtype=F32) / ROT_DIM)
    d = np.arange(ROPE_TW) % ATT_HEAD_DIM
    invf = jnp.where(d < ROT_DIM, inv_freq[d % (ROT_DIM // 2)], 0.0).reshape(1, ROPE_TW).astype(F32)
    m1 = jnp.asarray((d < ROT_DIM // 2).astype(np.float32)).reshape(1, ROPE_TW)
    m2 = jnp.asarray(((d >= ROT_DIM // 2) & (d < ROT_DIM)).astype(np.float32)).reshape(1, ROPE_TW)
    return invf, m1, m2


def _head_sum_matrix():
    d = np.arange(ATT_GROUPW) // ATT_HEAD_DIM
    s = np.arange(ATT_STATW) // 128
    return jnp.asarray((d[:, None] == s[None, :]).astype(np.float32))


def _adamw(w, g, m, v):
    m = ADAM_B1 * m + (1.0 - ADAM_B1) * g
    v = ADAM_B2 * v + (1.0 - ADAM_B2) * jnp.square(g)
    m_hat = m / (1.0 - ADAM_B1 ** ADAM_STEP)
    v_hat = v / (1.0 - ADAM_B2 ** ADAM_STEP)
    delta = -ADAM_LR * (m_hat / (jnp.sqrt(v_hat) + ADAM_EPS) + ADAM_WD * w)
    return delta, m, v


def _local_step(x, mem, pos, target, sp, ex):
    s = x.shape[0]
    al = DEEPNORM_ALPHA
    mx = MXU_DTYPE

    h0, h0b = _rowwise("ln_in", lambda x, g, b: (lambda h: (h, h))(_ln(x, g, b)), [x],
                       [sp["ln_in_g"], sp["ln_in_b"]], [(D_MODEL, F32), (D_MODEL, mx)],
                       carry=ex.gather_carry(["w_in"]))
    proj = _mm("proj", h0b, ex.weight("w_in"), "nn", bias=sp["b_in"],
               carry=ex.gather_carry(["w_glu", "w_att_up", "w_mix_out", "w_xq", "w_xo"]))

    ldt = jnp.repeat(sp["ssm_log_dt"].reshape(SSM_GROUPS), SSM_STATE).reshape(N_STATE, 1)
    are, aim = sp["ssm_a_re"].reshape(N_STATE, 1), sp["ssm_a_im"].reshape(N_STATE, 1)
    bre, bim = sp["ssm_b_re"].reshape(N_STATE, SSM_GROUP), sp["ssm_b_im"].reshape(N_STATE, SSM_GROUP)
    abr, abi, bbr, bbi = _ssm_disc_fwd(ldt, are, aim, bre, bim)
    a_re, a_im = abr.reshape(1, N_STATE), abi.reshape(1, N_STATE)
    bexp = jnp.concatenate([_blockdiag_b(bbr), _blockdiag_b(bbi)], axis=2).astype(mx)
    cexp = jnp.concatenate([_blockdiag_c(sp["ssm_c_re"].reshape(SSM_GROUPS, SSM_GROUP, SSM_STATE)),
                            -_blockdiag_c(sp["ssm_c_im"].reshape(SSM_GROUPS, SSM_GROUP, SSM_STATE))],
                           axis=1).astype(mx)
    u_p = _time_perm(proj[:, :SSM_WIDTH])
    b12, c12 = _split_by_scan_block(bexp, 2), _split_by_scan_block(cexp, 1)
    h_re, h_im, y_p = _ssm_scan("ssm_scan_fwd", u_p, b12, c12, a_re, a_im, sp["ssm_d"], reverse=False,
                                carry=ex.gather_carry(["w_xkv", "w_ff1", "w_ff2"]))
    y = _time_unperm(y_p)
    ygb, = _rowwise("gelu", lambda y: jax.nn.gelu(y), [y], [], [(SSM_WIDTH, mx)])
    z = _mm("glu", ygb, ex.weight("w_glu"), "nn", bias=sp["b_glu"])

    invf, m1, m2 = _rope_consts()

    def rope_fwd(pos, q0, q1, q2, k0, k1, k2, v0, v1, v2, invf, m1, m2):
        narrow = _rope_tables(pos, invf, m1, m2)
        tabs = [_widen(t) for t in narrow]
        return tuple(_rope(t, *tabs) for t in (q0, q1, q2, k0, k1, k2)) + (v0, v1, v2) + tuple(narrow)

    qkv_cols = [(proj, ATT_GROUPW, 3 + i) for i in range(9)]
    qkv = _rowwise("rope", rope_fwd, [pos] + qkv_cols, [invf, m1, m2], [(ATT_GROUPW, mx)] * 9 + [(ROPE_TW, F32)] * 3)
    rope_tabs = qkv[9:]
    n_blocks = s // ATT_BLK
    groups = [(str(g), n_blocks // d, d) for g, d in enumerate(DILATIONS)]
    q_d = [_dilate(qkv[g], d) for g, d in enumerate(DILATIONS)]
    k_d = [_dilate(qkv[3 + g], d) for g, d in enumerate(DILATIONS)]
    v_d = [_dilate(qkv[6 + g], d) for g, d in enumerate(DILATIONS)]
    o_g, l_g = [], []
    for g, (tag, per_seq, d) in enumerate(groups):
        o, lse = _attn_fwd(tag, per_seq, q_d[g], k_d[g], v_d[g])
        o_g.append(_undilate(o, d))
        l_g.append(_undilate(lse, d))

    def merge(o0, o1, o2, l0, l1, l2):
        m = jnp.maximum(jnp.maximum(l0, l1), l2)
        e0, e1, e2 = jnp.exp(l0 - m), jnp.exp(l1 - m), jnp.exp(l2 - m)
        tot = e0 + e1 + e2

        def per_dim(e):
            w = e / tot
            return jnp.concatenate([w[:, h * 128:h * 128 + ATT_HEAD_DIM] for h in range(ATT_HPG)], axis=1)

        att = per_dim(e0) * o0 + per_dim(e1) * o1 + per_dim(e2) * o2
        lse = m + jnp.log(tot)
        return att, att, lse, _stat_rows(lse)

    att, attb, lse_tot, lse_tot_t = _rowwise("attn_merge", merge, o_g + l_g, [],
                                             [(ATT_GROUPW, F32), (ATT_GROUPW, mx), (ATT_STATW, F32)], touts=[(8, F32)])
    batt = _mm("att_up", attb, ex.weight("w_att_up"), "nn")

    gate_rows = [(proj, D_MODEL, 3), (proj, D_MODEL, 4), (z, D_MODEL, 0), (z, D_MODEL, 1), batt]
    mixedb, = _rowwise("gate", _gate, gate_rows, [], [(D_MODEL, mx)])
    o1 = _mm("mix_out", mixedb, ex.weight("w_mix_out"), "nn", bias=sp["b_mix_out"])
    h1, h1b = _rowwise("ln1", lambda h, o, g, b: (lambda r: (r, r))(_res_ln(h, o, g, b)), [h0, o1],
                       [sp["ln1_g"], sp["ln1_b"]], [(D_MODEL, F32), (D_MODEL, mx)])

    qx = _mm("xq", h1b, ex.weight("w_xq"), "nn", out_dtypes=(mx,))
    kvx = _mm("xkv", mem, ex.weight("w_xkv"), "nn", out_dtypes=(mx,))
    oxb = _xattn_fwd(qx, kvx)
    o2 = _mm("xo", oxb, ex.weight("w_xo"), "nn")
    h2, h2b = _rowwise("ln2", lambda h, o, g, b: (lambda r: (r, r))(_res_ln(h, o, g, b)), [h1, o2],
                       [sp["ln2_g"], sp["ln2_b"]], [(D_MODEL, F32), (D_MODEL, mx)])

    a_ff, fb = _mm("ff1", h2b, ex.weight("w_ff1"), "nn", bias=sp["b_ff1"],
                   epilogue=lambda r: (r, jnp.square(jnp.maximum(r, 0.0))), out_dtypes=(F32, mx))
    o3 = _mm("ff2", fb, ex.weight("w_ff2"), "nn", bias=sp["b_ff2"])

    def loss_bwd(h2, o3, tgt, g, b):
        def f(h2, o3, g, b):
            h3 = _res_ln(h2, o3, g, b)
            return 0.5 * jnp.sum(jnp.mean(jnp.square(h3 - tgt), axis=-1))

        loss, vjp = jax.vjp(f, h2, o3, g, b)
        _, dr, dg, db = vjp(jnp.ones((), F32))
        return dr, dr, dg, db, _colsum(dr), jnp.full((1, 128), loss, F32)

    dr3, dr3b, g_ln3_g, g_ln3_b, g_b_ff2, loss = _rowwise(
        "loss_ln3_bwd", loss_bwd, [h2, o3, target], [sp["ln3_g"], sp["ln3_b"]],
        [(D_MODEL, F32), (D_MODEL, mx)], [D_MODEL, D_MODEL, D_MODEL, 128])

    dab, da_sums = _mm("ff2_dx", dr3b, ex.weight("w_ff2"), "nt", extras=(a_ff,),
                       epilogue=lambda r, a: (r * (2.0 * jnp.maximum(a, 0.0)),), out_dtypes=(mx,), colsum=True)
    g_b_ff1 = jnp.sum(da_sums, axis=0)
    ex.grad("w_ff2", _mm("ff2_dw", fb, dr3b, "tn"))
    ex.grad("w_ff1", _mm("ff1_dw", h2b, dab, "tn", carry=ex.carry(swap=["w_ff2"])))
    dh2 = _mm("ff1_dx", dab, ex.weight("w_ff1"), "nt", extras=(dr3,), epilogue=lambda r, d: (r + al * d,),
              carry=ex.carry(swap=["w_ff1"]))

    def ln_bwd(h, o, dout, g, b):
        _, vjp = jax.vjp(_res_ln, h, o, g, b)
        _, dr, dg, db = vjp(dout)
        return dr, dr, dg, db, _colsum(dr)

    dr2, dr2b, g_ln2_g, g_ln2_b, _ = _rowwise(
        "ln2_bwd", ln_bwd, [h1, o2, dh2], [sp["ln2_g"], sp["ln2_b"]],
        [(D_MODEL, F32), (D_MODEL, mx)], [D_MODEL, D_MODEL, D_MODEL])
    ex.grad("w_xo", _mm("xo_dw", oxb, dr2b, "tn"))
    doxb = _mm("xo_dx", dr2b, ex.weight("w_xo"), "nt", out_dtypes=(mx,))
    dqxb, dkvx = _xattn_bwd(qx, kvx, doxb)
    ex.grad("w_xq", _mm("xq_dw", h1b, dqxb, "tn"))
    dh1 = _mm("xq_dx", dqxb, ex.weight("w_xq"), "nt", extras=(dr2,), epilogue=lambda r, d: (r + al * d,))
    ex.grad("w_xkv", _mm("xkv_dw", mem, dkvx, "tn"))

    dr1, dr1b, g_ln1_g, g_ln1_b, g_b_mix = _rowwise(
        "ln1_bwd", ln_bwd, [h0, o1, dh1], [sp["ln1_g"], sp["ln1_b"]],
        [(D_MODEL, F32), (D_MODEL, mx)], [D_MODEL, D_MODEL, D_MODEL])
    ex.grad("w_mix_out", _mm("mix_dw", mixedb, dr1b, "tn"))
    dmixed = _mm("mix_dx", dr1b, ex.weight("w_mix_out"), "nt")

    def gate_bwd(gs, ga, z1, z2, batt, dm):
        _, vjp = jax.vjp(_gate, gs, ga, z1, z2, batt)
        dgs, dga, dz1, dz2, dbatt = vjp(dm)
        dz = jnp.concatenate([dz1, dz2], axis=-1)
        return dgs, dga, dz, dbatt, _colsum(dz)

    dgsb, dgab, dzb, dbattb, g_b_glu = _rowwise(
        "gate_bwd", gate_bwd, gate_rows + [dmixed], [],
        [(D_MODEL, mx), (D_MODEL, mx), (2 * D_MODEL, mx), (D_MODEL, mx)], [2 * D_MODEL])
    ex.grad("w_att_up", _mm("att_up_dw", attb, dbattb, "tn"))
    datt = _mm("att_up_dx", dbattb, ex.weight("w_att_up"), "nt")

    def att_delta(datt, att, hs):
        dl = jnp.dot(datt * att, hs, precision=lax.Precision.HIGHEST, preferred_element_type=F32)
        return datt, dl, _stat_rows(dl)

    dattb, delta, delta_t = _rowwise("attn_delta", att_delta, [datt, att], [_head_sum_matrix()],
                                     [(ATT_GROUPW, mx), (ATT_STATW, F32)], touts=[(8, F32)])
    dq_g, dk_g, dv_g = [], [], []
    for g, (tag, per_seq, d) in enumerate(groups):
        do_d, lt_d, dl_d = _dilate(dattb, d), _dilate(lse_tot, d), _dilate(delta, d)
        dq_g.append(_undilate(_attn_dq(tag, per_seq, q_d[g], k_d[g], v_d[g], do_d, lt_d, dl_d), d))
        dk, dv = _attn_dkv(tag, per_seq, q_d[g], k_d[g], v_d[g], do_d, _dilate_rows(lse_tot_t, d), _dilate_rows(delta_t, d))
        dk_g.append(_undilate(dk, d))
        dv_g.append(_undilate(dv, d))
    dqkv = dq_g + dk_g + dv_g

    def rope_bwd(q0, q1, q2, k0, k1, k2, v0, v1, v2, cos, s_up, s_dn):
        tabs = [_widen(t) for t in (cos, s_up, s_dn)]
        return jnp.concatenate([_rope_t(t, *tabs) for t in (q0, q1, q2, k0, k1, k2)] + [v0, v1, v2], axis=-1)

    dqkvb, = _rowwise("rope_bwd", rope_bwd, dqkv + list(rope_tabs), [], [(9 * ATT_GROUPW, mx)])

    ex.grad("w_glu", _mm("glu_dw", ygb, dzb, "tn",
                         carry=ex.carry(swap=["w_xo", "w_xq", "w_xkv", "w_mix_out", "w_att_up"])))
    dyg = _mm("glu_dx", dzb, ex.weight("w_glu"), "nt", carry=ex.carry(swap=["w_glu"]))

    def gelu_bwd(y, dyg):
        _, vjp = jax.vjp(jax.nn.gelu, y)
        return vjp(dyg)[0]

    dy, = _rowwise("gelu_bwd", gelu_bwd, [y, dyg], [], [(SSM_WIDTH, F32)])
    dy_p = _time_perm(dy)
    s_re, s_im, du_p = _ssm_scan("ssm_scan_bwd", dy_p, c12, b12, a_re, a_im, sp["ssm_d"], reverse=True,
                                 carry=ex.carry(ici=["w_ff1", "w_xkv", "w_glu"]))
    g_bexp, g_cexp, d_abr, d_abi = _ssm_wgrads(u_p, dy_p, s_re, s_im, h_re, h_im, carry=ex.carry(ici=["w_ff2"]))
    g_ssm_d, = _rowwise("ssm_dd", lambda a, b: (_colsum(a * b),), [dy_p, u_p], [], [], [SSM_WIDTH])
    g_ldt, g_are, g_aim, g_bre, g_bim = _ssm_disc_bwd(
        ldt, are, aim, bre, bim, d_abr.reshape(N_STATE, 1), d_abi.reshape(N_STATE, 1),
        _diag_of_b(g_bexp[:, :, :CH_N]), _diag_of_b(g_bexp[:, :, CH_N:]))
    g_c_re = _diag_of_c(g_cexp[:, :CH_N, :])
    g_c_im = -_diag_of_c(g_cexp[:, CH_N:, :])

    def assemble(du, dqkv, dgs, dga):
        row = jnp.concatenate([du.astype(mx), dqkv, dgs, dga], axis=-1)
        return row, _colsum(row)

    dprojb, g_b_in = _rowwise("in_assemble", assemble, [_time_unperm(du_p), dqkvb, dgsb, dgab], [],
                              [(IN_COLS, mx)], [IN_COLS])
    ex.grad("w_in", _mm("in_dw", h0b, dprojb, "tn",
                        carry=ex.carry(ici=["w_xo", "w_xq", "w_mix_out", "w_att_up"])))
    dh0 = _mm("in_dx", dprojb, ex.weight("w_in"), "nt", extras=(dr1,), epilogue=lambda r, d: (r + al * d,),
              carry=ex.carry(ici=["w_in"]))

    def ln_in_bwd(x, dout, g, b):
        _, vjp = jax.vjp(_ln, x, g, b)
        return vjp(dout)

    dx, g_ln_in_g, g_ln_in_b = _rowwise("ln_in_bwd", ln_in_bwd, [x, dh0], [sp["ln_in_g"], sp["ln_in_b"]],
                                        [(D_MODEL, F32)], [D_MODEL, D_MODEL], carry=ex.finish_carry())

    small = {"ln_in_g": g_ln_in_g, "ln_in_b": g_ln_in_b, "b_in": g_b_in, "ssm_log_dt": g_ldt, "ssm_a_re": g_are,
             "ssm_a_im": g_aim, "ssm_b_re": g_bre, "ssm_b_im": g_bim, "ssm_c_re": g_c_re, "ssm_c_im": g_c_im,
             "ssm_d": g_ssm_d, "b_glu": g_b_glu, "b_mix_out": g_b_mix, "ln1_g": g_ln1_g, "ln1_b": g_ln1_b,
             "ln2_g": g_ln2_g, "ln2_b": g_ln2_b, "b_ff1": g_b_ff1, "b_ff2": g_b_ff2, "ln3_g": g_ln3_g,
             "ln3_b": g_ln3_b}
    return loss, dx, small


def _piece_shape(k, n, axis):
    return (k // 2, n // 4) if axis == 1 else (k // 8, n)


def _aligned(v, m):
    return v if isinstance(v, int) else pl.multiple_of(v, m)


def _full_piece(ref, k, n, axis, chip, half):
    pr, pc = _piece_shape(k, n, axis)
    if axis == 1:
        return ref.at[pl.ds(_aligned(half * pr, 8), pr), pl.ds(_aligned(chip * pc, 128), pc)]
    return ref.at[pl.ds(_aligned(chip * (2 * pr) + half * pr, 8), pr), :]


def _shard_piece(ref, k, n, axis, half):
    pr, _ = _piece_shape(k, n, axis)
    return ref.at[pl.ds(_aligned(half * pr, 8), pr), :]


def _mesh_pos():
    x, y, c = lax.axis_index("x"), lax.axis_index("y"), lax.axis_index("c")
    other_chips = [(1 - x, y), (x, 1 - y), (1 - x, 1 - y)]
    return x, y, c, other_chips


def _remote(src, dst, send_sem, recv_sem, dev):
    return pltpu.make_async_remote_copy(src_ref=src, dst_ref=dst, send_sem=send_sem, recv_sem=recv_sem,
                                        device_id=dev, device_id_type=MESH)


def _placed(name, fn, n_steps, where, ins, out_sds, out_block, out_index):
    def body(w_ref, *refs):
        o_ref = refs[-1]
        o_ref[...] = fn(*[r[...] for r in refs[:-1]]).astype(o_ref.dtype)

    grid_spec = pltpu.PrefetchScalarGridSpec(
        num_scalar_prefetch=1, grid=(n_steps,), in_specs=[pl.BlockSpec(bs, idx) for _, bs, idx in ins],
        out_specs=pl.BlockSpec(out_block, out_index))
    return pl.pallas_call(body, name=name, grid_spec=grid_spec, out_shape=out_sds,
                          compiler_params=_cparams(1))(where, *[a for a, _, _ in ins])


def _gather_copies(widx):
    geo = [BIG[i][1:] for i in widx]

    def ici(full, wi, j, chip, send_sems, recv_sems, c, dev):
        k, n, ax = geo[wi]
        piece = _full_piece(full[wi], k, n, ax, chip, c)
        return _remote(piece, piece, send_sems.at[wi * 6 + j], recv_sems.at[wi * 6 + j], dev)

    def d2d(full, wi, j, chip, half, send_sems, recv_sems, sib):
        k, n, ax = geo[wi]
        piece = _full_piece(full[wi], k, n, ax, chip, half)
        return _remote(piece, piece, send_sems.at[wi * 6 + 3 + j], recv_sems.at[wi * 6 + 3 + j], sib)

    def start(_, full, send_sems, recv_sems):
        x, y, c, chips = _mesh_pos()
        for wi in range(len(geo)):
            for j, (qx, qy) in enumerate(chips):
                ici(full, wi, j, 2 * x + y, send_sems, recv_sems, c, (qx, qy, c)).start()

    def finish(_, full, send_sems, recv_sems):
        x, y, c, chips = _mesh_pos()
        sib = (x, y, 1 - c)
        for wi in range(len(geo)):
            for j, (qx, qy) in enumerate(chips):
                ici(full, wi, j, 2 * qx + qy, send_sems, recv_sems, c, (qx, qy, c)).wait_recv()
                d2d(full, wi, j, 2 * qx + qy, c, send_sems, recv_sems, sib).start()
        for wi in range(len(geo)):
            for j, (qx, qy) in enumerate(chips):
                d2d(full, wi, j, 2 * qx + qy, 1 - c, send_sems, recv_sems, sib).wait_recv()
        for wi in range(len(geo)):
            for j, (qx, qy) in enumerate(chips):
                ici(full, wi, j, 2 * x + y, send_sems, recv_sems, c, (qx, qy, c)).wait_send()
                d2d(full, wi, j, 2 * qx + qy, c, send_sems, recv_sems, sib).wait_send()

    return start, finish, 6 * len(geo)


def _swap_copies(widx):
    geo = [BIG[i][1:] for i in widx]

    def copies(g, got, send_sems, recv_sems, base):
        x, y, c, _ = _mesh_pos()
        return [_remote(_full_piece(g[wi], k, n, ax, q, 1 - c), got[wi].at[q], send_sems.at[base + wi * 4 + q],
                        recv_sems.at[base + wi * 4 + q], (x, y, 1 - c))
                for wi, (k, n, ax) in enumerate(geo) for q in range(4)]

    def start(g, got, send_sems, recv_sems, base=0):
        for cp in copies(g, got, send_sems, recv_sems, base):
            cp.start()

    def finish(g, got, send_sems, recv_sems, base=0):
        for cp in copies(g, got, send_sems, recv_sems, base):
            cp.wait()

    return start, finish, 4 * len(geo)


def _swap_shapes(widx):
    return [jax.ShapeDtypeStruct((4,) + _piece_shape(*BIG[i][1:]), F32) for i in widx]


def _reduce_swap_halves(tag, grads, widx):
    nw = len(widx)
    start, finish, n_sems = _swap_copies(widx)

    def body(*refs):
        start(refs[:nw], refs[nw:2 * nw], *refs[2 * nw:])
        finish(refs[:nw], refs[nw:2 * nw], *refs[2 * nw:])

    return pl.pallas_call(
        body, name="reduce_swap_halves_" + tag, in_specs=[HBM_SPEC] * nw, out_specs=[HBM_SPEC] * nw,
        out_shape=_swap_shapes(widx),
        scratch_shapes=[pltpu.SemaphoreType.DMA((n_sems,)), pltpu.SemaphoreType.DMA((n_sems,))])(*grads)


def _owner_copies(nw):
    def copies(p, out, send_sems, recv_sems, base):
        x, y, c, chips = _mesh_pos()
        return [_remote(p[wi].at[2 * qx + qy], out[wi].at[j], send_sems.at[base + wi * 3 + j],
                        recv_sems.at[base + wi * 3 + j], (qx, qy, c))
                for wi in range(nw) for j, (qx, qy) in enumerate(chips)]

    def start(p, out, send_sems, recv_sems, base=0):
        for cp in copies(p, out, send_sems, recv_sems, base):
            cp.start()

    def finish(p, out, send_sems, recv_sems, base=0):
        for cp in copies(p, out, send_sems, recv_sems, base):
            cp.wait()

    return start, finish, 3 * nw


def _join_carries(a, b):
    if a is None or b is None:
        return a if b is None else b
    n_i, n_o = len(a.ins), len(a.outs)
    outs = list(a.outs) + [o + n_i if isinstance(o, int) else o for o in b.outs]

    def start(c_in, c_out, send_sems, recv_sems):
        a.start(c_in[:n_i], c_out[:n_o], send_sems, recv_sems)
        b.start(c_in[n_i:], c_out[n_o:], send_sems, recv_sems, base=a.n_sems)

    def finish(c_in, c_out, send_sems, recv_sems):
        a.finish(c_in[:n_i], c_out[:n_o], send_sems, recv_sems)
        b.finish(c_in[n_i:], c_out[n_o:], send_sems, recv_sems, base=a.n_sems)

    def done(res):
        a.done(res[:n_o])
        b.done(res[n_o:])

    return _Carry(a.ins + b.ins, outs, a.n_sems + b.n_sems, start, finish, done)


def _share_copies():
    def copy(out, wi, half, send_sems, recv_sems, sib):
        _, k, n, ax = BIG[wi]
        piece = _shard_piece(out[wi], k, n, ax, half)
        return _remote(piece, piece, send_sems.at[wi], recv_sems.at[wi], sib)

    def start(_, out, send_sems, recv_sems):
        x, y, c, _ = _mesh_pos()
        for wi in range(len(BIG)):
            copy(out, wi, c, send_sems, recv_sems, (x, y, 1 - c)).start()

    def finish(_, out, send_sems, recv_sems):
        x, y, c, _ = _mesh_pos()
        for wi in range(len(BIG)):
            copy(out, wi, 1 - c, send_sems, recv_sems, (x, y, 1 - c)).wait_recv()
            copy(out, wi, c, send_sems, recv_sems, (x, y, 1 - c)).wait_send()

    return start, finish, len(BIG)


def _allreduce_small(v):
    r = v.shape[0]
    rh = r // 2
    assert rh % 8 == 0

    def body(v_ref, o_ref, sib_buf, chip_buf, send_sems, recv_sems):
        x, y, c, chips = _mesh_pos()
        me = 2 * x + y
        sib = (x, y, 1 - c)
        mine = pl.ds(pl.multiple_of(c * rh, 8), rh)
        other = pl.ds(pl.multiple_of((1 - c) * rh, 8), rh)
        swap = _remote(v_ref.at[other], sib_buf, send_sems.at[0], recv_sems.at[0], sib)
        swap.start()
        swap.wait()
        chip_buf[me] = v_ref[mine, :] + sib_buf[...]
        cps = []
        for j, (qx, qy) in enumerate(chips):
            cp = _remote(chip_buf.at[me], chip_buf.at[me], send_sems.at[1 + j], recv_sems.at[1 + j], (qx, qy, c))
            cp.start()
            cps.append(cp)
        for j, (qx, qy) in enumerate(chips):
            slot = chip_buf.at[2 * qx + qy]
            _remote(slot, slot, send_sems.at[1 + j], recv_sems.at[1 + j], (qx, qy, c)).wait_recv()
        for cp in cps:
            cp.wait_send()
        o_ref[mine, :] = ((chip_buf[0] + chip_buf[1]) + chip_buf[2]) + chip_buf[3]
        back = _remote(o_ref.at[mine], o_ref.at[mine], send_sems.at[4], recv_sems.at[4], sib)
        back.start()
        _remote(o_ref.at[other], o_ref.at[other], send_sems.at[4], recv_sems.at[4], sib).wait_recv()
        back.wait_send()

    return pl.pallas_call(
        body, name="allreduce_small", in_specs=[VMEM_SPEC], out_specs=VMEM_SPEC,
        out_shape=jax.ShapeDtypeStruct((r, 128), F32),
        scratch_shapes=[pltpu.VMEM((rh, 128), F32), pltpu.VMEM((4, rh, 128), F32),
                        pltpu.SemaphoreType.DMA((5,)), pltpu.SemaphoreType.DMA((5,))],
        compiler_params=pltpu.CompilerParams(vmem_limit_bytes=VMEM_LIMIT))(v)


def _as2d(a):
    a = a.reshape((-1, a.shape[-1])) if a.ndim > 1 else a.reshape(1, -1)
    return a


def _adamw_small(quads):
    n = len(quads)

    def body(*refs):
        for i in range(n):
            w, g, m, v = (r[...] for r in refs[4 * i:4 * i + 4])
            for ref, val in zip(refs[4 * n + 3 * i:4 * n + 3 * i + 3], _adamw(w, g, m, v)):
                ref[...] = val

    return pl.pallas_call(
        body, name="adamw_small", in_specs=[VMEM_SPEC] * (4 * n), out_specs=[VMEM_SPEC] * (3 * n),
        out_shape=[jax.ShapeDtypeStruct(q[0].shape, F32) for q in quads for _ in range(3)],
        compiler_params=pltpu.CompilerParams(vmem_limit_bytes=VMEM_LIMIT))(*[a for q in quads for a in q])


def _where():
    return jnp.stack([2 * lax.axis_index("x") + lax.axis_index("y"), lax.axis_index("c")]).astype(jnp.int32)


_BIG_INDEX = {name: i for i, (name, _, _, _) in enumerate(BIG)}


class _Exchange:
    def __init__(self, inputs, where):
        self.inputs, self.where = inputs, where
        self.full, self.ready = {}, set()
        self.raw, self.got, self.parts, self.landed, self.geom = {}, {}, {}, {}, {}
        for name, k, n, ax in BIG:
            w2 = inputs[name][0]
            rs, cs = w2.shape
            tm = _tile(rs, 512)
            steps = rs // tm
            if ax == 1:
                blk, idx = (tm, cs), lambda i, w: (i, w[0])
            else:
                blk, idx = (tm, n), functools.partial(lambda i, w, steps: (w[0] * steps + i, 0), steps=steps)
            self.full[name] = _placed("cast_" + name, lambda w: w, steps, where, [(w2, (tm, cs), lambda i, w: (i, 0))],
                                      jax.ShapeDtypeStruct((k, n), MXU_DTYPE), blk, idx)

    def _gathered(self, names, outs):
        for name, o in zip(names, outs):
            self.full[name] = o
            self.ready.add(name)

    def gather_carry(self, names):
        start, finish, n_sems = _gather_copies([_BIG_INDEX[n] for n in names])
        return _Carry([self.full[n] for n in names], list(range(len(names))), n_sems, start, finish,
                      functools.partial(self._gathered, names))

    def weight(self, name):
        assert name in self.ready, name
        return self.full[name]

    def grad(self, name, g):
        self.raw[name] = g

    def _swapped(self, names, outs):
        for name, o in zip(names, outs):
            self.got[name] = o

    def _pair_sum(self, name):
        i = _BIG_INDEX[name]
        _, k, n, ax = BIG[i]
        g = self.raw[name]
        if name not in self.got:
            self._swapped([name], _reduce_swap_halves(name, [g], [i]))
        got = self.got[name]
        pr, pc = _piece_shape(k, n, ax)
        tm = _tile(pr, 512)
        spp = pr // tm
        self.geom[name] = (pr, pc, tm, spp)
        if ax == 1:
            g_idx = functools.partial(lambda i, w, spp: (w[1] * spp + i % spp, i // spp), spp=spp)
        else:
            g_idx = functools.partial(lambda i, w, spp: ((i // spp) * 2 * spp + w[1] * spp + i % spp, 0), spp=spp)
        self.parts[name] = _placed(
            "pair_sum_" + name, lambda a, b: a + b, 4 * spp, self.where,
            [(g, (tm, pc), g_idx), (got.reshape(4 * pr, pc), (tm, pc), lambda i, w: (i, 0))],
            jax.ShapeDtypeStruct((4 * pr, pc), BF16), (tm, pc), lambda i, w: (i, 0)).reshape(4, pr, pc)

    def _landed(self, names, outs):
        for name, o in zip(names, outs):
            self.landed[name] = o

    def carry(self, swap=(), ici=()):
        first = second = None
        if swap:
            widx = [_BIG_INDEX[n] for n in swap]
            start, finish, n_sems = _swap_copies(widx)
            first = _Carry([self.raw[n] for n in swap], _swap_shapes(widx), n_sems, start, finish,
                           functools.partial(self._swapped, list(swap)))
        if ici:
            for n in ici:
                self._pair_sum(n)
            start, finish, n_sems = _owner_copies(len(ici))
            parts = [self.parts[n] for n in ici]
            outs = [jax.ShapeDtypeStruct((3,) + p.shape[1:], p.dtype) for p in parts]
            second = _Carry(parts, outs, n_sems, start, finish, functools.partial(self._landed, list(ici)))
        return _join_carries(first, second)

    def _shared(self, outs):
        self.shards = dict(zip([b[0] for b in BIG], outs))

    def finish_carry(self):
        halves = []
        for name, _, _, _ in BIG:
            pr, pc, tm, spp = self.geom[name]
            ins = [(self.parts[name], (None, tm, pc), lambda i, w: (w[0], i, 0))]
            ins += [(self.landed[name], (None, tm, pc), functools.partial(lambda i, w, j: (j, i, 0), j=j))
                    for j in range(3)]
            halves.append(_placed("chip_sum_" + name,
                                  lambda a, b, c, d: ((a.astype(F32) + b.astype(F32)) + c.astype(F32)) + d.astype(F32),
                                  spp, self.where, ins, jax.ShapeDtypeStruct(self.inputs[name].shape[1:], F32), (tm, pc),
                                  functools.partial(lambda i, w, spp: (w[1] * spp + i, 0), spp=spp)))
        start, finish, n_sems = _share_copies()
        return _Carry(halves, list(range(len(halves))), n_sems, start, finish, self._shared)


def _step(inputs):
    x, mem, positions, target = inputs["x"][0], inputs["mem"][0], inputs["positions"], inputs["loss_target"][0]
    pos = positions.reshape(-1, 1)
    ex = _Exchange(inputs, _where())
    sp = {name: _as2d(inputs[name]) for name in SMALL}
    memb, = _rowwise("cast_mem", lambda m: (m,), [mem], [], [(D_MODEL, MXU_DTYPE)])

    loss, dx, gsmall = _local_step(x, memb, pos, target, sp, ex)
    gshard = ex.shards

    out = {}
    for name, _, _, _ in BIG:
        w2, m2, v2 = inputs[name][0], inputs["m_" + name][0], inputs["v_" + name][0]
        n = w2.shape[1]
        d, nm, nv = _rowwise("adamw_" + name, _adamw, [w2, gshard[name], m2, v2], [], [(n, F32)] * 3, tm=_tile(w2.shape[0], 512))
        lead = inputs[name].shape
        out[name] = (gshard[name].reshape(lead), d.reshape(lead), nm.reshape(lead), nv.reshape(lead))

    def tiles(a):
        flat = a.reshape(-1)
        n = -(-flat.shape[0] // 1024) * 1024
        return jnp.pad(flat, (0, n - flat.shape[0])).reshape(n // 128, 128)

    pieces = [tiles(loss[:, :1])] + [tiles(gsmall[name]) for name in SMALL]
    if sum(p.shape[0] for p in pieces) % 16:
        pieces.append(jnp.zeros((8, 128), F32))
    red = _allreduce_small(jnp.concatenate(pieces, axis=0))
    loss_total = red[0, 0]
    grads, off = {}, pieces[0].shape[0]
    for name, p in zip(SMALL, pieces[1:]):
        shp = _as2d(inputs[name]).shape
        grads[name] = red[off:off + p.shape[0]].reshape(-1)[:shp[0] * shp[1]].reshape(shp)
        off += p.shape[0]
    upd = _adamw_small([(_as2d(inputs[n]), grads[n], _as2d(inputs["m_" + n]), _as2d(inputs["v_" + n])) for n in SMALL])
    for i, name in enumerate(SMALL):
        shp = inputs[name].shape
        out[name] = (grads[name].reshape(shp),) + tuple(t.reshape(shp) for t in upd[3 * i:3 * i + 3])
    return loss_total, dx.reshape(inputs["x"].shape), out


_ARG_NAMES = (("x", "mem", "positions") + WEIGHT_ORDER + ("loss_target",) + tuple("m_" + n for n in WEIGHT_ORDER)
              + tuple("v_" + n for n in WEIGHT_ORDER))


def kernel(x, mem, positions, ln_in_g, ln_in_b, w_in, b_in, ssm_log_dt, ssm_a_re, ssm_a_im, ssm_b_re, ssm_b_im, ssm_c_re, ssm_c_im, ssm_d, w_glu, b_glu, w_att_up, w_mix_out, b_mix_out, ln1_g, ln1_b, w_xq, w_xkv, w_xo, ln2_g, ln2_b, w_ff1, b_ff1, w_ff2, b_ff2, ln3_g, ln3_b, loss_target, m_ln_in_g, m_ln_in_b, m_w_in, m_b_in, m_ssm_log_dt, m_ssm_a_re, m_ssm_a_im, m_ssm_b_re, m_ssm_b_im, m_ssm_c_re, m_ssm_c_im, m_ssm_d, m_w_glu, m_b_glu, m_w_att_up, m_w_mix_out, m_b_mix_out, m_ln1_g, m_ln1_b, m_w_xq, m_w_xkv, m_w_xo, m_ln2_g, m_ln2_b, m_w_ff1, m_b_ff1, m_w_ff2, m_b_ff2, m_ln3_g, m_ln3_b, v_ln_in_g, v_ln_in_b, v_w_in, v_b_in, v_ssm_log_dt, v_ssm_a_re, v_ssm_a_im, v_ssm_b_re, v_ssm_b_im, v_ssm_c_re, v_ssm_c_im, v_ssm_d, v_w_glu, v_b_glu, v_w_att_up, v_w_mix_out, v_b_mix_out, v_ln1_g, v_ln1_b, v_w_xq, v_w_xkv, v_w_xo, v_ln2_g, v_ln2_b, v_w_ff1, v_b_ff1, v_w_ff2, v_b_ff2, v_ln3_g, v_ln3_b):
    args = (x, mem, positions, ln_in_g, ln_in_b, w_in, b_in, ssm_log_dt, ssm_a_re, ssm_a_im, ssm_b_re, ssm_b_im, ssm_c_re, ssm_c_im, ssm_d, w_glu, b_glu, w_att_up, w_mix_out, b_mix_out, ln1_g, ln1_b, w_xq, w_xkv, w_xo, ln2_g, ln2_b, w_ff1, b_ff1, w_ff2, b_ff2, ln3_g, ln3_b, loss_target, m_ln_in_g, m_ln_in_b, m_w_in, m_b_in, m_ssm_log_dt, m_ssm_a_re, m_ssm_a_im, m_ssm_b_re, m_ssm_b_im, m_ssm_c_re, m_ssm_c_im, m_ssm_d, m_w_glu, m_b_glu, m_w_att_up, m_w_mix_out, m_b_mix_out, m_ln1_g, m_ln1_b, m_w_xq, m_w_xkv, m_w_xo, m_ln2_g, m_ln2_b, m_w_ff1, m_b_ff1, m_w_ff2, m_b_ff2, m_ln3_g, m_ln3_b, v_ln_in_g, v_ln_in_b, v_w_in, v_b_in, v_ssm_log_dt, v_ssm_a_re, v_ssm_a_im, v_ssm_b_re, v_ssm_b_im, v_ssm_c_re, v_ssm_c_im, v_ssm_d, v_w_glu, v_b_glu, v_w_att_up, v_w_mix_out, v_b_mix_out, v_ln1_g, v_ln1_b, v_w_xq, v_w_xkv, v_w_xo, v_ln2_g, v_ln2_b, v_w_ff1, v_b_ff1, v_w_ff2, v_b_ff2, v_ln3_g, v_ln3_b)
    assert len(args) == len(_ARG_NAMES)
    inputs = dict(zip(_ARG_NAMES, args))
    loss, dx, out = _step(inputs)
    res = [loss, dx]
    for k in range(4):
        res += [out[name][k] for name in WEIGHT_ORDER]
    return tuple(res)
```

```python
import functools
import math

import numpy as np
import jax
import jax.numpy as jnp
from jax import lax
from jax.experimental import pallas as pl
from jax.experimental.pallas import tpu as pltpu

F32 = jnp.float32
BF16 = jnp.bfloat16
MXU_DTYPE = jnp.bfloat16

D_MODEL = 1024
SSM_GROUP = 16
SSM_WIDTH = 768
SSM_GROUPS = 48
SSM_STATE = 64
N_STATE = SSM_GROUPS * SSM_STATE
SSM_CHUNKS = 6
CH_W = 128
CH_N = 512
ATT_HEAD_DIM = 64
ATT_HPG = 4
ATT_GROUPW = ATT_HPG * ATT_HEAD_DIM
DILATIONS = (1, 4, 16)
ATT_BLK = 128
ATT_SCALE = ATT_HEAD_DIM ** -0.5
ROT_DIM = 16
ROPE_THETA = 500000.0
XATT_HEADS = 4
XATT_HEAD_DIM = 256
XATT_SCALE = XATT_HEAD_DIM ** -0.5
D_FF = 4096
IN_COLS = 5120
DEEPNORM_ALPHA = 2.0 ** 0.25
LN_EPS = 1e-5
NEG_INF = -1e30
ADAM_LR = 0.001
ADAM_B1 = 0.9
ADAM_B2 = 0.999
ADAM_EPS = 1e-08
ADAM_WD = 0.01
ADAM_STEP = 10

N_SEG = 32
VMEM_LIMIT = 56 * 1024 * 1024
MESH = pl.DeviceIdType.MESH
HBM_SPEC = pl.BlockSpec(memory_space=pltpu.HBM)
VMEM_SPEC = pl.BlockSpec(memory_space=pltpu.VMEM)

BIG = (("w_in", 1024, 5120, 1), ("w_glu", 768, 2048, 1), ("w_att_up", 256, 1024, 1),
       ("w_mix_out", 1024, 1024, 0), ("w_xq", 1024, 1024, 0), ("w_xkv", 1024, 2048, 1),
       ("w_xo", 1024, 1024, 0), ("w_ff1", 1024, 4096, 1), ("w_ff2", 4096, 1024, 0))
SMALL = ("ln_in_g", "ln_in_b", "b_in", "ssm_log_dt", "ssm_a_re", "ssm_a_im", "ssm_b_re", "ssm_b_im",
         "ssm_c_re", "ssm_c_im", "ssm_d", "b_glu", "b_mix_out", "ln1_g", "ln1_b", "ln2_g", "ln2_b",
         "b_ff1", "b_ff2", "ln3_g", "ln3_b")
WEIGHT_ORDER = ("ln_in_g", "ln_in_b", "w_in", "b_in", "ssm_log_dt", "ssm_a_re", "ssm_a_im", "ssm_b_re",
                "ssm_b_im", "ssm_c_re", "ssm_c_im", "ssm_d", "w_glu", "b_glu", "w_att_up", "w_mix_out",
                "b_mix_out", "ln1_g", "ln1_b", "w_xq", "w_xkv", "w_xo", "ln2_g", "ln2_b", "w_ff1", "b_ff1",
                "w_ff2", "b_ff2", "ln3_g", "ln3_b")


def _cparams(n_axes):
    return pltpu.CompilerParams(dimension_semantics=("arbitrary",) * n_axes, vmem_limit_bytes=VMEM_LIMIT)


class _Carry:
    def __init__(self, ins, outs, n_sems, start, finish, done):
        self.ins, self.outs, self.n_sems, self.start, self.finish, self.done = ins, outs, n_sems, start, finish, done


def _call(name, body, grid, in_specs, out_specs, out_shape, args, scratch_shapes=(), carry=None):
    in_specs, out_specs, out_shape = list(in_specs), list(out_specs), list(out_shape)
    params = _cparams(len(grid))
    if carry is None:
        return pl.pallas_call(body, name=name, grid=grid, in_specs=in_specs, out_specs=out_specs, out_shape=out_shape,
                              scratch_shapes=list(scratch_shapes), compiler_params=params)(*args)
    n_in, n_out, n_ci, n_co = len(in_specs), len(out_specs), len(carry.ins), len(carry.outs)
    n_scr = len(scratch_shapes)

    def wrapped(*refs):
        ins, c_in = refs[:n_in], refs[n_in:n_in + n_ci]
        outs, c_out = refs[n_in + n_ci:n_in + n_ci + n_out], refs[n_in + n_ci + n_out:n_in + n_ci + n_out + n_co]
        scratch = refs[n_in + n_ci + n_out + n_co:n_in + n_ci + n_out + n_co + n_scr]
        send_sems, recv_sems = refs[-2:]
        ids = [pl.program_id(a) for a in range(len(grid))]
        first = functools.reduce(jnp.logical_and, [i == 0 for i in ids])
        last = functools.reduce(jnp.logical_and, [i == g - 1 for i, g in zip(ids, grid)])

        @pl.when(first)
        def _():
            carry.start(c_in, c_out, send_sems, recv_sems)

        body(*ins, *outs, *scratch)

        @pl.when(last)
        def _():
            carry.finish(c_in, c_out, send_sems, recv_sems)

    c_shapes = [jax.ShapeDtypeStruct(carry.ins[o].shape, carry.ins[o].dtype) if isinstance(o, int) else o
                for o in carry.outs]
    aliases = {n_in + o: n_out + i for i, o in enumerate(carry.outs) if isinstance(o, int)}
    res = pl.pallas_call(
        wrapped, name=name, grid=grid, in_specs=in_specs + [HBM_SPEC] * n_ci, out_specs=out_specs + [HBM_SPEC] * n_co,
        out_shape=out_shape + c_shapes, input_output_aliases=aliases,
        scratch_shapes=list(scratch_shapes) + [pltpu.SemaphoreType.DMA((carry.n_sems,))] * 2,
        compiler_params=params)(*args, *carry.ins)
    carry.done(res[n_out:])
    return res[:n_out]


def _rowwise(name, fn, rows, consts, outs, reds=(), tm=512, touts=(), carry=None):
    n_rows = (rows[0][0] if isinstance(rows[0], tuple) else rows[0]).shape[-2]
    tm = min(tm, n_rows)
    assert n_rows % tm == 0, (name, n_rows, tm)
    specs, args = [], []
    for r in rows:
        if isinstance(r, tuple) and len(r) == 3:
            arr, width, cb = r
            specs.append(pl.BlockSpec((tm, width), functools.partial(lambda i, cb: (i, cb), cb=cb)))
        elif isinstance(r, tuple):
            arr, slot = r
            specs.append(pl.BlockSpec((None, tm, arr.shape[2]), functools.partial(lambda i, s: (s, i, 0), s=slot)))
        else:
            arr = r
            specs.append(pl.BlockSpec((tm, arr.shape[1]), lambda i: (i, 0)))
        args.append(arr)
        assert arr.shape[-2] == n_rows, (name, arr.shape, n_rows)
    for cst in consts:
        specs.append(pl.BlockSpec(cst.shape, lambda i: (0, 0)))
        args.append(cst)
    n_r, n_c, n_o, n_d = len(rows), len(consts), len(outs) + len(touts), len(reds)
    out_shape = [jax.ShapeDtypeStruct((n_rows, c), dt) for c, dt in outs]
    out_specs = [pl.BlockSpec((tm, c), lambda i: (i, 0)) for c, _ in outs]
    out_shape += [jax.ShapeDtypeStruct((r, n_rows), dt) for r, dt in touts]
    out_specs += [pl.BlockSpec((r, tm), lambda i: (0, i)) for r, _ in touts]
    out_shape += [jax.ShapeDtypeStruct((1, c), F32) for c in reds]
    out_specs += [pl.BlockSpec((1, c), lambda i: (0, 0)) for c in reds]

    def body(*refs):
        ins = [r[...] for r in refs[:n_r + n_c]]
        o_refs = refs[n_r + n_c:n_r + n_c + n_o]
        d_refs = refs[n_r + n_c + n_o:]
        res = fn(*ins)
        res = res if isinstance(res, (tuple, list)) else (res,)
        assert len(res) == n_o + n_d, (name, len(res))
        for ref, val in zip(o_refs, res[:n_o]):
            ref[...] = val.astype(ref.dtype)
        first = pl.program_id(0) == 0
        for ref, val in zip(d_refs, res[n_o:]):
            @pl.when(first)
            def _(ref=ref, val=val):
                ref[...] = val

            @pl.when(jnp.logical_not(first))
            def _(ref=ref, val=val):
                ref[...] += val

    return _call(name, body, (n_rows // tm,), specs, out_specs, out_shape, args, carry=carry)


def _colsum(v):
    return jnp.sum(v.astype(F32), axis=0, keepdims=True)


_DIMS = {"nn": (((1,), (0,)), ((), ())), "nt": (((1,), (1,)), ((), ())), "tn": (((0,), (0,)), ((), ()))}


def _tile(dim, want):
    if dim <= want:
        return dim
    return max(t for t in range(128, want + 1, 128) if dim % t == 0)


def _dot(a, b, mode):
    return lax.dot_general(a.astype(MXU_DTYPE), b.astype(MXU_DTYPE), _DIMS[mode], preferred_element_type=F32)


def _mm(name, a, b, mode, *, bias=None, extras=(), epilogue=None, out_dtypes=(F32,), tm=1024, tn=1024, tk=1024,
        carry=None, colsum=False, rows=()):
    if mode == "nn":
        (m, k), (_, n) = a.shape, b.shape
    elif mode == "nt":
        (m, k), (n, _) = a.shape, b.shape
    else:
        (k, m), (_, n) = a.shape, b.shape
    if k > tk:
        tk = 5 * tk
    tn = _tile(n, tn)
    tk = _tile(k, tk)
    nk = k // tk

    def vmem_bytes(rows):
        blocks = rows * tk * a.dtype.itemsize + tk * tn * b.dtype.itemsize
        blocks += sum(rows * tn * e.dtype.itemsize for e in extras)
        blocks += sum(rows * tn * jnp.dtype(dt).itemsize for dt in out_dtypes)
        return 2 * blocks + (rows * tn * 4 if nk > 1 else 0)

    tm = _tile(m, tm if mode == "tn" else 2 * tm)
    while vmem_bytes(tm) > 3 * VMEM_LIMIT // 4 and tm % 256 == 0:
        tm //= 2
    while nk == 1 and k > 1024 and (m // tm) * (n // tn) < 4 and tm % 256 == 0:
        tm //= 2
    assert m % tm == 0 and n % tn == 0 and k % tk == 0, (name, m, n, k)
    a_spec = {"nn": pl.BlockSpec((tm, tk), lambda i, j, kk: (i, kk)),
              "nt": pl.BlockSpec((tm, tk), lambda i, j, kk: (i, kk)),
              "tn": pl.BlockSpec((tk, tm), lambda i, j, kk: (kk, i))}[mode]
    b_spec = {"nn": pl.BlockSpec((tk, tn), lambda i, j, kk: (kk, j)),
              "nt": pl.BlockSpec((tn, tk), lambda i, j, kk: (j, kk)),
              "tn": pl.BlockSpec((tk, tn), lambda i, j, kk: (kk, j))}[mode]
    specs, args = [a_spec, b_spec], [a, b]
    if bias is not None:
        specs.append(pl.BlockSpec((1, tn), lambda i, j, kk: (0, j)))
        args.append(bias)
    for e in extras:
        specs.append(pl.BlockSpec((tm, tn), lambda i, j, kk: (i, j)))
        args.append(e)
    for rc in rows:
        specs.append(pl.BlockSpec((1, tn), lambda i, j, kk: (0, j)))
        args.append(rc)
    n_e, n_o = len(extras) + len(rows), len(out_dtypes)
    has_bias = bias is not None

    def body(*refs):
        a_ref, b_ref = refs[0], refs[1]
        pos = 2
        bias_ref = refs[pos] if has_bias else None
        pos += int(has_bias)
        e_refs = refs[pos:pos + n_e]
        o_refs = refs[pos + n_e:pos + n_e + n_o]
        sum_ref = refs[pos + n_e + n_o] if colsum else None
        acc_ref = refs[pos + n_e + n_o + int(colsum)] if nk > 1 else None
        part = _dot(a_ref[...], b_ref[...], mode)

        def finish(r):
            if has_bias:
                r = r + bias_ref[...]
            res = epilogue(r, *[e[...] for e in e_refs]) if epilogue is not None else (r,)
            for ref, val in zip(o_refs, res):
                ref[...] = val.astype(ref.dtype)
            if colsum:
                sum_ref[...] = _colsum(res[0])

        if nk == 1:
            finish(part)
        else:
            kk = pl.program_id(2)

            @pl.when(kk == 0)
            def _():
                acc_ref[...] = part

            @pl.when(kk > 0)
            def _():
                acc_ref[...] += part

            @pl.when(kk == nk - 1)
            def _():
                finish(acc_ref[...])

    out_specs = [pl.BlockSpec((tm, tn), lambda i, j, kk: (i, j)) for _ in out_dtypes]
    out_shape = [jax.ShapeDtypeStruct((m, n), dt) for dt in out_dtypes]
    if colsum:
        out_specs.append(pl.BlockSpec((None, 1, tn), lambda i, j, kk: (i, 0, j)))
        out_shape.append(jax.ShapeDtypeStruct((m // tm, 1, n), F32))
    res = _call(name, body, (m // tm, n // tn, nk), specs, out_specs, out_shape, args,
                scratch_shapes=[pltpu.VMEM((tm, tn), F32)] if nk > 1 else [], carry=carry)
    return res[0] if len(res) == 1 else res


def _ssm_wgrads(u, dy, g_re, g_im, h_re, h_im, tk=2048, carry=None):
    s = u.shape[0]
    tk = min(tk, s)
    nk = s // tk
    assert tk % N_SEG == 0

    def body(u_ref, dy_ref, gre_ref, gim_ref, hre_ref, him_ref, lre_ref, lim_ref, db_ref, dc_ref, dar_ref, dai_ref,
             pre_ref, pim_ref):
        kk = pl.program_id(1)
        u_blk, dy_blk = u_ref[...], dy_ref[...]
        g_r, g_i, h_r, h_i = gre_ref[...], gim_ref[...], hre_ref[...], him_ref[...]
        d_b = jnp.concatenate([_dot(u_blk, g_r, "tn"), _dot(u_blk, g_i, "tn")], axis=1)
        d_c = jnp.concatenate([_dot(h_r, dy_blk, "tn"), _dot(h_i, dy_blk, "tn")], axis=0)

        @pl.when(kk == 0)
        def _():
            first_row = lax.broadcasted_iota(jnp.int32, (N_SEG, CH_N), 0) == 0
            pre_ref[...] = jnp.where(first_row, 0.0, pltpu.roll(lre_ref[...], 1, 0))
            pim_ref[...] = jnp.where(first_row, 0.0, pltpu.roll(lim_ref[...], 1, 0))

        p_r = jnp.concatenate([pre_ref[...], h_r[:tk - N_SEG]], axis=0)
        p_i = jnp.concatenate([pim_ref[...], h_i[:tk - N_SEG]], axis=0)
        pre_ref[...] = h_r[tk - N_SEG:]
        pim_ref[...] = h_i[tk - N_SEG:]
        d_ar = jnp.sum(g_r * p_r + g_i * p_i, axis=0, keepdims=True)
        d_ai = jnp.sum(g_i * p_r - g_r * p_i, axis=0, keepdims=True)

        @pl.when(kk == 0)
        def _():
            db_ref[...] = d_b
            dc_ref[...] = d_c
            dar_ref[...] = d_ar
            dai_ref[...] = d_ai

        @pl.when(kk > 0)
        def _():
            db_ref[...] += d_b
            dc_ref[...] += d_c
            dar_ref[...] += d_ar
            dai_ref[...] += d_ai

    chan = pl.BlockSpec((tk, CH_W), lambda j, kk: (kk, j))
    state = pl.BlockSpec((tk, CH_N), lambda j, kk: (kk, j))
    last = pl.BlockSpec((N_SEG, CH_N), lambda j, kk: (s // N_SEG - 1, j))
    row = pl.BlockSpec((1, CH_N), lambda j, kk: (0, j))
    return _call(
        "ssm_wgrads", body, (SSM_CHUNKS, nk), [chan, chan, state, state, state, state, last, last],
        [pl.BlockSpec((None, CH_W, 2 * CH_N), lambda j, kk: (j, 0, 0)),
         pl.BlockSpec((None, 2 * CH_N, CH_W), lambda j, kk: (j, 0, 0)), row, row],
        [jax.ShapeDtypeStruct((SSM_CHUNKS, CH_W, 2 * CH_N), F32), jax.ShapeDtypeStruct((SSM_CHUNKS, 2 * CH_N, CH_W), F32),
         jax.ShapeDtypeStruct((1, N_STATE), F32), jax.ShapeDtypeStruct((1, N_STATE), F32)],
        (u, dy, g_re, g_im, h_re, h_im, h_re, h_im), scratch_shapes=[pltpu.VMEM((N_SEG, CH_N), F32)] * 2, carry=carry)


SCAN_LB = 256


def _split_by_scan_block(mat, axis):
    halves = []
    for l in range(CH_N // SCAN_LB):
        re = lax.slice_in_dim(mat, l * SCAN_LB, (l + 1) * SCAN_LB, axis=axis)
        im = lax.slice_in_dim(mat, CH_N + l * SCAN_LB, CH_N + (l + 1) * SCAN_LB, axis=axis)
        halves.append(jnp.concatenate([re, im], axis=axis))
    return jnp.stack(halves, axis=1).reshape((-1,) + halves[0].shape[1:])


def _ssm_scan(name, chan, expand12, contract12, a_re, a_im, d_row, reverse, carry=None):
    s = chan.shape[0]
    seg_len = s // N_SEG
    n_sq = int(math.log2(seg_len))
    assert 2 ** n_sq == seg_len
    rb = min(512, s)
    per_chunk = CH_N // SCAN_LB

    def body(are_ref, aim_ref, ch_ref, e_ref, k_ref, d_ref, hre_ref, him_ref, o_ref, wre_ref, wim_ref, ere, eim, cre, cim):
        e_mat, k_mat = e_ref[...], k_ref[...]
        for r in range(s // rb):
            rows = slice(r * rb, (r + 1) * rb)
            w = _dot(ch_ref[rows, :], e_mat, "nt" if reverse else "nn")
            wre_ref[rows, :] = w[:, :SCAN_LB]
            wim_ref[rows, :] = w[:, SCAN_LB:]

        ar1 = are_ref[...]
        ai1 = -aim_ref[...] if reverse else aim_ref[...]
        ar = jnp.broadcast_to(ar1, (N_SEG, SCAN_LB))
        ai = jnp.broadcast_to(ai1, (N_SEG, SCAN_LB))

        def rows_of(k):
            kk = seg_len - 1 - k if reverse else k
            return pl.ds(pl.multiple_of(kk * N_SEG, N_SEG), N_SEG)

        def local(k, carry):
            hr, hi = carry
            rows = rows_of(k)
            nr = ar * hr - ai * hi + wre_ref[rows, :]
            ni = ar * hi + ai * hr + wim_ref[rows, :]
            hre_ref[rows, :] = nr
            him_ref[rows, :] = ni
            return nr, ni

        zero = jnp.zeros((N_SEG, SCAN_LB), F32)
        er, ei = lax.fori_loop(0, seg_len, local, (zero, zero))
        ere[...] = er
        eim[...] = ei
        pr, pi = ar1, ai1
        for _ in range(n_sq):
            pr, pi = pr * pr - pi * pi, 2.0 * pr * pi
        cr = jnp.zeros((1, SCAN_LB), F32)
        ci = jnp.zeros((1, SCAN_LB), F32)
        for jj in range(N_SEG):
            j = N_SEG - 1 - jj if reverse else jj
            cre[j:j + 1, :] = cr
            cim[j:j + 1, :] = ci
            er_j, ei_j = ere[j:j + 1, :], eim[j:j + 1, :]
            cr, ci = pr * cr - pi * ci + er_j, pr * ci + pi * cr + ei_j
        c_r, c_i = cre[...], cim[...]

        def fix(k, carry):
            qr, qi = carry
            rows = rows_of(k)
            hre_ref[rows, :] = hre_ref[rows, :] + (qr * c_r - qi * c_i)
            him_ref[rows, :] = him_ref[rows, :] + (qr * c_i + qi * c_r)
            return qr * ar - qi * ai, qr * ai + qi * ar

        lax.fori_loop(0, seg_len, fix, (ar, ai))

        first_of_chunk = lax.rem(pl.program_id(0), per_chunk) == 0
        for r in range(s // rb):
            rows = slice(r * rb, (r + 1) * rb)
            h_cat = jnp.concatenate([hre_ref[rows, :], him_ref[rows, :]], axis=1)
            part = _dot(h_cat, k_mat, "nt" if reverse else "nn")

            @pl.when(first_of_chunk)
            def _(rows=rows, part=part):
                o_ref[rows, :] = part + d_ref[...] * ch_ref[rows, :]

            @pl.when(jnp.logical_not(first_of_chunk))
            def _(rows=rows, part=part):
                o_ref[rows, :] += part

    nblk = N_STATE // SCAN_LB
    blk = pl.BlockSpec((s, SCAN_LB), lambda b: (0, b))
    row = pl.BlockSpec((1, SCAN_LB), lambda b: (0, b))
    chan_blk = pl.BlockSpec((s, CH_W), lambda b: (0, b // per_chunk))
    res = _call(name, body, (nblk,),
                [row, row, chan_blk, pl.BlockSpec((None,) + expand12.shape[1:], lambda b: (b, 0, 0)),
                 pl.BlockSpec((None,) + contract12.shape[1:], lambda b: (b, 0, 0)),
                 pl.BlockSpec((1, CH_W), lambda b: (0, b // per_chunk))],
                [blk, blk, chan_blk],
                [jax.ShapeDtypeStruct((s, N_STATE), F32)] * 2 + [jax.ShapeDtypeStruct((s, SSM_WIDTH), F32)],
                (a_re, a_im, chan, expand12, contract12, d_row),
                scratch_shapes=[pltpu.VMEM((s, SCAN_LB), F32)] * 2 + [pltpu.VMEM((N_SEG, SCAN_LB), F32)] * 4, carry=carry)
    return res[0], res[1], res[2]


def _disc(ldt, are, aim, bre, bim):
    dt = jnp.exp(ldt)
    mag = jnp.exp(are * dt)
    abr = mag * jnp.cos(aim * dt)
    abi = mag * jnp.sin(aim * dt)
    den = jnp.square(are) + jnp.square(aim)
    nr = abr - 1.0
    fre = (nr * are + abi * aim) / den
    fim = (abi * are - nr * aim) / den
    return abr, abi, fre * bre - fim * bim, fre * bim + fim * bre


def _ssm_disc_fwd(ldt, are, aim, bre, bim):
    def body(l_ref, ar_ref, ai_ref, br_ref, bi_ref, o0, o1, o2, o3):
        res = _disc(l_ref[...], ar_ref[...], ai_ref[...], br_ref[...], bi_ref[...])
        for ref, val in zip((o0, o1, o2, o3), res):
            ref[...] = val

    col = jax.ShapeDtypeStruct((N_STATE, 1), F32)
    mat = jax.ShapeDtypeStruct((N_STATE, SSM_GROUP), F32)
    return pl.pallas_call(body, name="ssm_disc_fwd", out_shape=[col, col, mat, mat],
                          in_specs=[VMEM_SPEC] * 5, out_specs=[VMEM_SPEC] * 4)(ldt, are, aim, bre, bim)


def _ssm_disc_bwd(ldt, are, aim, bre, bim, d_abr, d_abi, d_bbr, d_bbi):
    def body(l_ref, ar_ref, ai_ref, br_ref, bi_ref, c0, c1, c2, c3, g_ldt, g_are, g_aim, g_bre, g_bim):
        _, vjp = jax.vjp(_disc, l_ref[...], ar_ref[...], ai_ref[...], br_ref[...], bi_ref[...])
        dl, dar, dai, dbr, dbi = vjp((c0[...], c1[...], c2[...], c3[...]))
        state = lax.broadcasted_iota(jnp.int32, (N_STATE, SSM_GROUPS), 0)
        group = lax.broadcasted_iota(jnp.int32, (N_STATE, SSM_GROUPS), 1)
        pick = jnp.right_shift(state, 6) == group
        g_ldt[...] = jnp.sum(jnp.where(pick, dl, 0.0), axis=0, keepdims=True)
        g_are[...] = dar
        g_aim[...] = dai
        g_bre[...] = dbr
        g_bim[...] = dbi

    col = jax.ShapeDtypeStruct((N_STATE, 1), F32)
    mat = jax.ShapeDtypeStruct((N_STATE, SSM_GROUP), F32)
    return pl.pallas_call(body, name="ssm_disc_bwd",
                          out_shape=[jax.ShapeDtypeStruct((1, SSM_GROUPS), F32), col, col, mat, mat],
                          in_specs=[VMEM_SPEC] * 9, out_specs=[VMEM_SPEC] * 5,
                          compiler_params=pltpu.CompilerParams(vmem_limit_bytes=VMEM_LIMIT))(
        ldt, are, aim, bre, bim, d_abr, d_abi, d_bbr, d_bbi)


_EYE8 = np.eye(8, dtype=np.float32)


def _blockdiag_b(bb):
    t = bb.reshape(SSM_CHUNKS, 8, SSM_STATE, SSM_GROUP).transpose(0, 1, 3, 2)
    return jnp.einsum("igcn,gh->igchn", t, _EYE8).reshape(SSM_CHUNKS, CH_W, CH_N)


def _diag_of_b(m):
    t = jnp.einsum("igchn,gh->igcn", m.reshape(SSM_CHUNKS, 8, SSM_GROUP, 8, SSM_STATE), _EYE8)
    return t.transpose(0, 1, 3, 2).reshape(N_STATE, SSM_GROUP)


def _blockdiag_c(c):
    t = c.reshape(SSM_CHUNKS, 8, SSM_GROUP, SSM_STATE).transpose(0, 1, 3, 2)
    return jnp.einsum("ignc,gh->ignhc", t, _EYE8).reshape(SSM_CHUNKS, CH_N, CH_W)


def _diag_of_c(m):
    t = jnp.einsum("ignhc,gh->ignc", m.reshape(SSM_CHUNKS, 8, SSM_STATE, 8, SSM_GROUP), _EYE8)
    return t.transpose(0, 1, 3, 2).reshape(SSM_GROUPS, SSM_GROUP, SSM_STATE)


def _time_perm(a):
    s, c = a.shape
    return a.reshape(N_SEG, s // N_SEG, c).transpose(1, 0, 2).reshape(s, c)


def _time_unperm(a):
    s, c = a.shape
    return a.reshape(s // N_SEG, N_SEG, c).transpose(1, 0, 2).reshape(s, c)


def _dilate(a, d):
    s, c = a.shape
    return a if d == 1 else a.reshape(s // d, d, c).transpose(1, 0, 2).reshape(s, c)


def _undilate(a, d):
    s, c = a.shape
    return a if d == 1 else a.reshape(d, s // d, c).transpose(1, 0, 2).reshape(s, c)


def _dilate_rows(a, d):
    r, s = a.shape
    return a if d == 1 else a.reshape(r, s // d, d).transpose(0, 2, 1).reshape(r, s)


ATT_T_FWD = 4
ATT_T_BWD = 8


def _window(prev_ref, cur_ref, i, sl):
    if i == 0:
        return jnp.concatenate([prev_ref[:, sl], cur_ref[0:ATT_BLK, sl]], axis=0)
    return cur_ref[(i - 1) * ATT_BLK:(i + 1) * ATT_BLK, sl]


def _band_valid(first_key):
    qi = lax.broadcasted_iota(jnp.int32, (ATT_BLK, 2 * ATT_BLK), 0)
    ki = lax.broadcasted_iota(jnp.int32, (ATT_BLK, 2 * ATT_BLK), 1)
    steps = qi + ATT_BLK - ki
    return (steps >= 0) & (steps <= ATT_BLK) & (ki >= first_key)


ATT_STATW = ATT_HPG * 128


def _stat(h):
    return slice(h * 128, (h + 1) * 128)


def _stat_rows(stat):
    n = stat.shape[0]
    heads = [stat[:, _stat(h)].T[0:1, :] for h in range(ATT_HPG)]
    return jnp.concatenate(heads + [jnp.zeros((8 - ATT_HPG, n), stat.dtype)], axis=0)


def _attn_specs(nb, t, width=ATT_GROUPW):
    cur = pl.BlockSpec((t * ATT_BLK, width), lambda b: (b, 0))
    prev = pl.BlockSpec((ATT_BLK, width), lambda b: (jnp.maximum(b * t - 1, 0), 0))
    nxt = pl.BlockSpec((ATT_BLK, width), lambda b: (jnp.minimum((b + 1) * t, nb - 1), 0))
    return cur, prev, nxt


def _attn_fwd(tag, per_seq, q, k, v):
    s = q.shape[0]
    nb = s // ATT_BLK

    def body(q_ref, kc_ref, kp_ref, vc_ref, vp_ref, o_ref, lse_ref):
        bt = pl.program_id(0)
        for i in range(ATT_T_FWD):
            has_prev = lax.rem(bt * ATT_T_FWD + i, per_seq) > 0
            valid = _band_valid(jnp.where(has_prev, 0, ATT_BLK))
            rows = slice(i * ATT_BLK, (i + 1) * ATT_BLK)
            for h in range(ATT_HPG):
                sl = slice(h * ATT_HEAD_DIM, (h + 1) * ATT_HEAD_DIM)
                kcat = _window(kp_ref, kc_ref, i, sl)
                vcat = _window(vp_ref, vc_ref, i, sl)
                sc = _dot(q_ref[rows, sl], kcat, "nt") * ATT_SCALE
                sc = jnp.where(valid, sc, NEG_INF)
                m = jnp.max(sc, axis=-1, keepdims=True)
                p = jnp.exp(sc - m)
                den = jnp.sum(p, axis=-1, keepdims=True)
                o_ref[rows, sl] = _dot(p, vcat, "nn") / den
                lse_ref[rows, _stat(h)] = jnp.broadcast_to(m + jnp.log(den), (ATT_BLK, 128))

    cur, prev, _ = _attn_specs(nb, ATT_T_FWD)
    stat, _, _ = _attn_specs(nb, ATT_T_FWD, ATT_STATW)
    return pl.pallas_call(
        body, name="attn_fwd_" + tag, grid=(nb // ATT_T_FWD,), in_specs=[cur, cur, prev, cur, prev], out_specs=[cur, stat],
        out_shape=[jax.ShapeDtypeStruct((s, ATT_GROUPW), F32), jax.ShapeDtypeStruct((s, ATT_STATW), F32)],
        compiler_params=_cparams(1))(q, k, k, v, v)


def _attn_dq(tag, per_seq, q, k, v, do, lse, delta):
    s = q.shape[0]
    nb = s // ATT_BLK

    def body(q_ref, kc_ref, kp_ref, vc_ref, vp_ref, do_ref, lse_ref, dl_ref, dq_ref):
        bt = pl.program_id(0)
        for i in range(ATT_T_BWD):
            has_prev = lax.rem(bt * ATT_T_BWD + i, per_seq) > 0
            valid = _band_valid(jnp.where(has_prev, 0, ATT_BLK))
            rows = slice(i * ATT_BLK, (i + 1) * ATT_BLK)
            for h in range(ATT_HPG):
                sl = slice(h * ATT_HEAD_DIM, (h + 1) * ATT_HEAD_DIM)
                kcat = _window(kp_ref, kc_ref, i, sl)
                vcat = _window(vp_ref, vc_ref, i, sl)
                lse = jnp.concatenate([lse_ref[rows, _stat(h)]] * 2, axis=1)
                dlt = jnp.concatenate([dl_ref[rows, _stat(h)]] * 2, axis=1)
                sc = _dot(q_ref[rows, sl], kcat, "nt") * ATT_SCALE
                p = jnp.exp(jnp.where(valid, sc, NEG_INF) - lse)
                dp = _dot(do_ref[rows, sl], vcat, "nt")
                ds = p * (dp - dlt) * ATT_SCALE
                dq_ref[rows, sl] = _dot(ds, kcat, "nn")

    cur, prev, _ = _attn_specs(nb, ATT_T_BWD)
    stat, _, _ = _attn_specs(nb, ATT_T_BWD, ATT_STATW)
    return pl.pallas_call(
        body, name="attn_dq_" + tag, grid=(nb // ATT_T_BWD,), in_specs=[cur, cur, prev, cur, prev, cur, stat, stat],
        out_specs=cur, out_shape=jax.ShapeDtypeStruct((s, ATT_GROUPW), F32),
        compiler_params=_cparams(1))(q, k, k, v, v, do, lse, delta)


def _attn_dkv(tag, per_seq, q, k, v, do, lse_t, delta_t):
    s = q.shape[0]
    nb = s // ATT_BLK

    def body(k_ref, v_ref, qc_ref, qn_ref, doc_ref, don_ref, lc_ref, ln_ref, dc_ref, dn_ref, dk_ref, dv_ref):
        bt = pl.program_id(0)
        ki = lax.broadcasted_iota(jnp.int32, (ATT_BLK, 2 * ATT_BLK), 0)
        ci = lax.broadcasted_iota(jnp.int32, (ATT_BLK, 2 * ATT_BLK), 1)

        def pair(edge_ref, cur_ref, i, sl):
            if i == ATT_T_BWD - 1:
                return jnp.concatenate([cur_ref[i * ATT_BLK:(i + 1) * ATT_BLK, sl], edge_ref[:, sl]], axis=0)
            return cur_ref[i * ATT_BLK:(i + 2) * ATT_BLK, sl]

        def pair_row(edge_ref, cur_ref, i, h):
            if i == ATT_T_BWD - 1:
                row = jnp.concatenate([cur_ref[h:h + 1, i * ATT_BLK:(i + 1) * ATT_BLK], edge_ref[h:h + 1, :]], axis=1)
            else:
                row = cur_ref[h:h + 1, i * ATT_BLK:(i + 2) * ATT_BLK]
            return jnp.broadcast_to(row, (ATT_BLK, 2 * ATT_BLK))

        for i in range(ATT_T_BWD):
            b = bt * ATT_T_BWD + i
            next_uses = (b + 1 < nb) & (lax.rem(b + 1, per_seq) > 0)
            reach = jnp.where(next_uses, 0, 4 * ATT_BLK)
            valid = ((ci < ATT_BLK) & (ci >= ki)) | ((ci >= ATT_BLK) & (ki - ci + ATT_BLK >= reach))
            rows = slice(i * ATT_BLK, (i + 1) * ATT_BLK)
            for h in range(ATT_HPG):
                sl = slice(h * ATT_HEAD_DIM, (h + 1) * ATT_HEAD_DIM)
                qcat, docat = pair(qn_ref, qc_ref, i, sl), pair(don_ref, doc_ref, i, sl)
                sc = _dot(k_ref[rows, sl], qcat, "nt") * ATT_SCALE
                p = jnp.exp(jnp.where(valid, sc, NEG_INF) - pair_row(ln_ref, lc_ref, i, h))
                dv_ref[rows, sl] = _dot(p, docat, "nn")
                dp = _dot(v_ref[rows, sl], docat, "nt")
                ds = p * (dp - pair_row(dn_ref, dc_ref, i, h)) * ATT_SCALE
                dk_ref[rows, sl] = _dot(ds, qcat, "nn")

    cur, _, nxt = _attn_specs(nb, ATT_T_BWD)
    stat = pl.BlockSpec((8, ATT_T_BWD * ATT_BLK), lambda b: (0, b))
    snxt = pl.BlockSpec((8, ATT_BLK), lambda b: (0, jnp.minimum((b + 1) * ATT_T_BWD, nb - 1)))
    return pl.pallas_call(
        body, name="attn_dkv_" + tag, grid=(nb // ATT_T_BWD,), in_specs=[cur, cur, cur, nxt, cur, nxt, stat, snxt, stat, snxt],
        out_specs=[cur, cur], out_shape=[jax.ShapeDtypeStruct((s, ATT_GROUPW), F32)] * 2,
        compiler_params=_cparams(1))(k, v, q, q, do, do, lse_t, lse_t, delta_t, delta_t)


def _xattn_probs(q, kh):
    sc = _dot(q, kh, "nt") * XATT_SCALE
    e = jnp.exp(sc - jnp.max(sc, axis=-1, keepdims=True))
    return e / jnp.sum(e, axis=-1, keepdims=True)


def _xattn_fwd(q, kv, tm=512):
    s = q.shape[0]
    tm = min(tm, s)

    def body(q_ref, kv_ref, o_ref):
        for h in range(XATT_HEADS):
            sl = slice(h * XATT_HEAD_DIM, (h + 1) * XATT_HEAD_DIM)
            vs = slice(D_MODEL + h * XATT_HEAD_DIM, D_MODEL + (h + 1) * XATT_HEAD_DIM)
            p = _xattn_probs(q_ref[:, sl], kv_ref[:, sl])
            o_ref[:, sl] = _dot(p, kv_ref[:, vs], "nn").astype(o_ref.dtype)

    return pl.pallas_call(
        body, name="xattn_fwd", grid=(s // tm,),
        in_specs=[pl.BlockSpec((tm, D_MODEL), lambda i: (i, 0)), pl.BlockSpec(kv.shape, lambda i: (0, 0))],
        out_specs=pl.BlockSpec((tm, D_MODEL), lambda i: (i, 0)),
        out_shape=jax.ShapeDtypeStruct((s, D_MODEL), MXU_DTYPE), compiler_params=_cparams(1))(q, kv)


def _xattn_bwd(q, kv, do, tm=1024):
    s = q.shape[0]
    tm = min(tm, s)

    def body(q_ref, kv_ref, do_ref, dq_ref, dkv_ref):
        first = pl.program_id(0) == 0

        @pl.when(first)
        def _():
            dkv_ref[...] = jnp.zeros_like(dkv_ref)

        for h in range(XATT_HEADS):
            sl = slice(h * XATT_HEAD_DIM, (h + 1) * XATT_HEAD_DIM)
            vs = slice(D_MODEL + h * XATT_HEAD_DIM, D_MODEL + (h + 1) * XATT_HEAD_DIM)
            p = _xattn_probs(q_ref[:, sl], kv_ref[:, sl])
            dkv_ref[:, vs] += _dot(p, do_ref[:, sl], "tn")
            dp = _dot(do_ref[:, sl], kv_ref[:, vs], "nt")
            ds = p * (dp - jnp.sum(dp * p, axis=-1, keepdims=True)) * XATT_SCALE
            dq_ref[:, sl] = _dot(ds, kv_ref[:, sl], "nn").astype(dq_ref.dtype)
            dkv_ref[:, sl] += _dot(ds, q_ref[:, sl], "tn")

    row = pl.BlockSpec((tm, D_MODEL), lambda i: (i, 0))
    whole = pl.BlockSpec(kv.shape, lambda i: (0, 0))
    return pl.pallas_call(
        body, name="xattn_bwd", grid=(s // tm,), in_specs=[row, whole, row], out_specs=[row, whole],
        out_shape=[jax.ShapeDtypeStruct((s, D_MODEL), MXU_DTYPE), jax.ShapeDtypeStruct(kv.shape, F32)],
        compiler_params=_cparams(1))(q, kv, do)


def _ln(x, g, b):
    mu = jnp.mean(x, axis=-1, keepdims=True)
    xc = x - mu
    var = jnp.mean(jnp.square(xc), axis=-1, keepdims=True)
    return xc * lax.rsqrt(var + LN_EPS) * g + b


def _res_ln(h, o, g, b):
    return _ln(DEEPNORM_ALPHA * h + o, g, b)


def _gate(gs, ga, z1, z2, batt):
    return jax.nn.sigmoid(gs) * (z1 * jax.nn.sigmoid(z2)) + jax.nn.sigmoid(ga) * batt


ROPE_TW = 2 * ATT_HEAD_DIM


def _rope_tables(pos, invf, m1, m2):
    ang = pos.astype(F32) * invf
    sin = jnp.sin(ang)
    return jnp.cos(ang), -sin * m1, sin * m2


def _widen(tab):
    return jnp.concatenate([tab] * (ATT_GROUPW // ROPE_TW), axis=1)


def _rope(t, cos, s_up, s_dn):
    w = t.shape[-1]
    return t * cos + pltpu.roll(t, w - ROT_DIM // 2, 1) * s_up + pltpu.roll(t, ROT_DIM // 2, 1) * s_dn


def _rope_t(dt, cos, s_up, s_dn):
    w = dt.shape[-1]
    return dt * cos + pltpu.roll(dt * s_up, ROT_DIM // 2, 1) + pltpu.roll(dt * s_dn, w - ROT_DIM // 2, 1)


def _rope_consts():
    inv_freq = ROPE_THETA ** (-jnp.arange(0, ROT_DIM, 2, dtype=F32) / ROT_DIM)
    d = np.arange(ROPE_TW) % ATT_HEAD_DIM
    invf = jnp.where(d < ROT_DIM, inv_freq[d % (ROT_DIM // 2)], 0.0).reshape(1, ROPE_TW).astype(F32)
    m1 = jnp.asarray((d < ROT_DIM // 2).astype(np.float32)).reshape(1, ROPE_TW)
    m2 = jnp.asarray(((d >= ROT_DIM // 2) & (d < ROT_DIM)).astype(np.float32)).reshape(1, ROPE_TW)
    return invf, m1, m2


def _head_sum_matrix():
    d = np.arange(ATT_GROUPW) // ATT_HEAD_DIM
    s = np.arange(ATT_STATW) // 128
    return jnp.asarray((d[:, None] == s[None, :]).astype(np.float32))


def _adamw(w, g, m, v):
    m = ADAM_B1 * m + (1.0 - ADAM_B1) * g
    v = ADAM_B2 * v + (1.0 - ADAM_B2) * jnp.square(g)
    m_hat = m / (1.0 - ADAM_B1 ** ADAM_STEP)
    v_hat = v / (1.0 - ADAM_B2 ** ADAM_STEP)
    delta = -ADAM_LR * (m_hat / (jnp.sqrt(v_hat) + ADAM_EPS) + ADAM_WD * w)
    return delta, m, v


def _local_step(x, mem, pos, target, sp, ex):
    s = x.shape[0]
    al = DEEPNORM_ALPHA
    mx = MXU_DTYPE

    h0, h0b = _rowwise("ln_in", lambda x, g, b: (lambda h: (h, h))(_ln(x, g, b)), [x],
                       [sp["ln_in_g"], sp["ln_in_b"]], [(D_MODEL, F32), (D_MODEL, mx)],
                       carry=ex.gather_carry(["w_in"]))
    proj = _mm("proj", h0b, ex.weight("w_in"), "nn", bias=sp["b_in"],
               carry=ex.gather_carry(["w_glu", "w_att_up", "w_mix_out", "w_xq", "w_xo"]))

    ldt = jnp.repeat(sp["ssm_log_dt"].reshape(SSM_GROUPS), SSM_STATE).reshape(N_STATE, 1)
    are, aim = sp["ssm_a_re"].reshape(N_STATE, 1), sp["ssm_a_im"].reshape(N_STATE, 1)
    bre, bim = sp["ssm_b_re"].reshape(N_STATE, SSM_GROUP), sp["ssm_b_im"].reshape(N_STATE, SSM_GROUP)
    abr, abi, bbr, bbi = _ssm_disc_fwd(ldt, are, aim, bre, bim)
    a_re, a_im = abr.reshape(1, N_STATE), abi.reshape(1, N_STATE)
    bexp = jnp.concatenate([_blockdiag_b(bbr), _blockdiag_b(bbi)], axis=2).astype(mx)
    cexp = jnp.concatenate([_blockdiag_c(sp["ssm_c_re"].reshape(SSM_GROUPS, SSM_GROUP, SSM_STATE)),
                            -_blockdiag_c(sp["ssm_c_im"].reshape(SSM_GROUPS, SSM_GROUP, SSM_STATE))],
                           axis=1).astype(mx)
    u_p = _time_perm(proj[:, :SSM_WIDTH])
    b12, c12 = _split_by_scan_block(bexp, 2), _split_by_scan_block(cexp, 1)
    h_re, h_im, y_p = _ssm_scan("ssm_scan_fwd", u_p, b12, c12, a_re, a_im, sp["ssm_d"], reverse=False,
                                carry=ex.gather_carry(["w_xkv", "w_ff1", "w_ff2"]))
    y = _time_unperm(y_p)
    ygb, = _rowwise("gelu", lambda y: jax.nn.gelu(y), [y], [], [(SSM_WIDTH, mx)])
    z = _mm("glu", ygb, ex.weight("w_glu"), "nn", bias=sp["b_glu"])

    invf, m1, m2 = _rope_consts()

    def rope_fwd(pos, q0, q1, q2, k0, k1, k2, v0, v1, v2, invf, m1, m2):
        narrow = _rope_tables(pos, invf, m1, m2)
        tabs = [_widen(t) for t in narrow]
        return tuple(_rope(t, *tabs) for t in (q0, q1, q2, k0, k1, k2)) + (v0, v1, v2) + tuple(narrow)

    qkv_cols = [(proj, ATT_GROUPW, 3 + i) for i in range(9)]
    qkv = _rowwise("rope", rope_fwd, [pos] + qkv_cols, [invf, m1, m2], [(ATT_GROUPW, mx)] * 9 + [(ROPE_TW, F32)] * 3)
    rope_tabs = qkv[9:]
    n_blocks = s // ATT_BLK
    groups = [(str(g), n_blocks // d, d) for g, d in enumerate(DILATIONS)]
    q_d = [_dilate(qkv[g], d) for g, d in enumerate(DILATIONS)]
    k_d = [_dilate(qkv[3 + g], d) for g, d in enumerate(DILATIONS)]
    v_d = [_dilate(qkv[6 + g], d) for g, d in enumerate(DILATIONS)]
    o_g, l_g = [], []
    for g, (tag, per_seq, d) in enumerate(groups):
        o, lse = _attn_fwd(tag, per_seq, q_d[g], k_d[g], v_d[g])
        o_g.append(_undilate(o, d))
        l_g.append(_undilate(lse, d))

    def merge(o0, o1, o2, l0, l1, l2):
        m = jnp.maximum(jnp.maximum(l0, l1), l2)
        e0, e1, e2 = jnp.exp(l0 - m), jnp.exp(l1 - m), jnp.exp(l2 - m)
        tot = e0 + e1 + e2

        def per_dim(e):
            w = e / tot
            return jnp.concatenate([w[:, h * 128:h * 128 + ATT_HEAD_DIM] for h in range(ATT_HPG)], axis=1)

        att = per_dim(e0) * o0 + per_dim(e1) * o1 + per_dim(e2) * o2
        lse = m + jnp.log(tot)
        return att, att, lse, _stat_rows(lse)

    att, attb, lse_tot, lse_tot_t = _rowwise("attn_merge", merge, o_g + l_g, [],
                                             [(ATT_GROUPW, F32), (ATT_GROUPW, mx), (ATT_STATW, F32)], touts=[(8, F32)])
    batt = _mm("att_up", attb, ex.weight("w_att_up"), "nn")

    gate_rows = [(proj, D_MODEL, 3), (proj, D_MODEL, 4), (z, D_MODEL, 0), (z, D_MODEL, 1), batt]
    mixedb, = _rowwise("gate", _gate, gate_rows, [], [(D_MODEL, mx)])
    def branch_ln(o, h, g, b):
        hn = _res_ln(h, o, g, b)
        return o, hn, hn

    assert ex.weight("w_mix_out").shape[1] == D_MODEL
    o1, h1, h1b = _mm("mix_out", mixedb, ex.weight("w_mix_out"), "nn", bias=sp["b_mix_out"], extras=(h0,),
                      rows=(sp["ln1_g"], sp["ln1_b"]), epilogue=branch_ln, out_dtypes=(F32, F32, mx), tn=D_MODEL)

    qx = _mm("xq", h1b, ex.weight("w_xq"), "nn", out_dtypes=(mx,))
    kvx = _mm("xkv", mem, ex.weight("w_xkv"), "nn", out_dtypes=(mx,))
    oxb = _xattn_fwd(qx, kvx)
    o2, h2, h2b = _mm("xo", oxb, ex.weight("w_xo"), "nn", extras=(h1,), rows=(sp["ln2_g"], sp["ln2_b"]),
                      epilogue=branch_ln, out_dtypes=(F32, F32, mx), tn=D_MODEL)

    a_ff, fb = _mm("ff1", h2b, ex.weight("w_ff1"), "nn", bias=sp["b_ff1"],
                   epilogue=lambda r: (r, jnp.square(jnp.maximum(r, 0.0))), out_dtypes=(F32, mx))
    o3 = _mm("ff2", fb, ex.weight("w_ff2"), "nn", bias=sp["b_ff2"])

    def loss_bwd(h2, o3, tgt, g, b):
        def f(h2, o3, g, b):
            h3 = _res_ln(h2, o3, g, b)
            return 0.5 * jnp.sum(jnp.mean(jnp.square(h3 - tgt), axis=-1))

        loss, vjp = jax.vjp(f, h2, o3, g, b)
        _, dr, dg, db = vjp(jnp.ones((), F32))
        return dr, dr, dg, db, _colsum(dr), jnp.full((1, 128), loss, F32)

    dr3, dr3b, g_ln3_g, g_ln3_b, g_b_ff2, loss = _rowwise(
        "loss_ln3_bwd", loss_bwd, [h2, o3, target], [sp["ln3_g"], sp["ln3_b"]],
        [(D_MODEL, F32), (D_MODEL, mx)], [D_MODEL, D_MODEL, D_MODEL, 128])

    dab, da_sums = _mm("ff2_dx", dr3b, ex.weight("w_ff2"), "nt", extras=(a_ff,),
                       epilogue=lambda r, a: (r * (2.0 * jnp.maximum(a, 0.0)),), out_dtypes=(mx,), colsum=True)
    g_b_ff1 = jnp.sum(da_sums, axis=0)
    ex.grad("w_ff2", _mm("ff2_dw", fb, dr3b, "tn"))
    ex.grad("w_ff1", _mm("ff1_dw", h2b, dab, "tn", carry=ex.carry(swap=["w_ff2"])))
    dh2 = _mm("ff1_dx", dab, ex.weight("w_ff1"), "nt", extras=(dr3,), epilogue=lambda r, d: (r + al * d,),
              carry=ex.carry(swap=["w_ff1"]))

    def ln_bwd(h, o, dout, g, b):
        _, vjp = jax.vjp(_res_ln, h, o, g, b)
        _, dr, dg, db = vjp(dout)
        return dr, dr, dg, db, _colsum(dr)

    dr2, dr2b, g_ln2_g, g_ln2_b, _ = _rowwise(
        "ln2_bwd", ln_bwd, [h1, o2, dh2], [sp["ln2_g"], sp["ln2_b"]],
        [(D_MODEL, F32), (D_MODEL, mx)], [D_MODEL, D_MODEL, D_MODEL])
    ex.grad("w_xo", _mm("xo_dw", oxb, dr2b, "tn"))
    doxb = _mm("xo_dx", dr2b, ex.weight("w_xo"), "nt", out_dtypes=(mx,))
    dqxb, dkvx = _xattn_bwd(qx, kvx, doxb)
    ex.grad("w_xq", _mm("xq_dw", h1b, dqxb, "tn"))
    dh1 = _mm("xq_dx", dqxb, ex.weight("w_xq"), "nt", extras=(dr2,), epilogue=lambda r, d: (r + al * d,))
    ex.grad("w_xkv", _mm("xkv_dw", mem, dkvx, "tn"))

    dr1, dr1b, g_ln1_g, g_ln1_b, g_b_mix = _rowwise(
        "ln1_bwd", ln_bwd, [h0, o1, dh1], [sp["ln1_g"], sp["ln1_b"]],
        [(D_MODEL, F32), (D_MODEL, mx)], [D_MODEL, D_MODEL, D_MODEL])
    ex.grad("w_mix_out", _mm("mix_dw", mixedb, dr1b, "tn"))
    dmixed = _mm("mix_dx", dr1b, ex.weight("w_mix_out"), "nt")

    def gate_bwd(gs, ga, z1, z2, batt, dm):
        _, vjp = jax.vjp(_gate, gs, ga, z1, z2, batt)
        dgs, dga, dz1, dz2, dbatt = vjp(dm)
        dz = jnp.concatenate([dz1, dz2], axis=-1)
        return dgs, dga, dz, dbatt, _colsum(dz)

    dgsb, dgab, dzb, dbattb, g_b_glu = _rowwise(
        "gate_bwd", gate_bwd, gate_rows + [dmixed], [],
        [(D_MODEL, mx), (D_MODEL, mx), (2 * D_MODEL, mx), (D_MODEL, mx)], [2 * D_MODEL])
    ex.grad("w_att_up", _mm("att_up_dw", attb, dbattb, "tn"))
    datt = _mm("att_up_dx", dbattb, ex.weight("w_att_up"), "nt")

    def att_delta(datt, att, hs):
        dl = jnp.dot(datt * att, hs, precision=lax.Precision.HIGHEST, preferred_element_type=F32)
        return datt, dl, _stat_rows(dl)

    dattb, delta, delta_t = _rowwise("attn_delta", att_delta, [datt, att], [_head_sum_matrix()],
                                     [(ATT_GROUPW, mx), (ATT_STATW, F32)], touts=[(8, F32)])
    dq_g, dk_g, dv_g = [], [], []
    for g, (tag, per_seq, d) in enumerate(groups):
        do_d, lt_d, dl_d = _dilate(dattb, d), _dilate(lse_tot, d), _dilate(delta, d)
        dq_g.append(_undilate(_attn_dq(tag, per_seq, q_d[g], k_d[g], v_d[g], do_d, lt_d, dl_d), d))
        dk, dv = _attn_dkv(tag, per_seq, q_d[g], k_d[g], v_d[g], do_d, _dilate_rows(lse_tot_t, d), _dilate_rows(delta_t, d))
        dk_g.append(_undilate(dk, d))
        dv_g.append(_undilate(dv, d))
    dqkv = dq_g + dk_g + dv_g

    def rope_bwd(q0, q1, q2, k0, k1, k2, v0, v1, v2, cos, s_up, s_dn):
        tabs = [_widen(t) for t in (cos, s_up, s_dn)]
        return jnp.concatenate([_rope_t(t, *tabs) for t in (q0, q1, q2, k0, k1, k2)] + [v0, v1, v2], axis=-1)

    dqkvb, = _rowwise("rope_bwd", rope_bwd, dqkv + list(rope_tabs), [], [(9 * ATT_GROUPW, mx)])

    ex.grad("w_glu", _mm("glu_dw", ygb, dzb, "tn",
                         carry=ex.carry(swap=["w_xo", "w_xq", "w_xkv", "w_mix_out", "w_att_up"])))
    dyg = _mm("glu_dx", dzb, ex.weight("w_glu"), "nt", carry=ex.carry(swap=["w_glu"]))

    def gelu_bwd(y, dyg):
        _, vjp = jax.vjp(jax.nn.gelu, y)
        return vjp(dyg)[0]

    dy, = _rowwise("gelu_bwd", gelu_bwd, [y, dyg], [], [(SSM_WIDTH, F32)])
    dy_p = _time_perm(dy)
    s_re, s_im, du_p = _ssm_scan("ssm_scan_bwd", dy_p, c12, b12, a_re, a_im, sp["ssm_d"], reverse=True,
                                 carry=ex.carry(ici=["w_ff1", "w_xkv", "w_glu"]))
    g_bexp, g_cexp, d_abr, d_abi = _ssm_wgrads(u_p, dy_p, s_re, s_im, h_re, h_im, carry=ex.carry(ici=["w_ff2"]))
    g_ssm_d, = _rowwise("ssm_dd", lambda a, b: (_colsum(a * b),), [dy_p, u_p], [], [], [SSM_WIDTH])
    g_ldt, g_are, g_aim, g_bre, g_bim = _ssm_disc_bwd(
        ldt, are, aim, bre, bim, d_abr.reshape(N_STATE, 1), d_abi.reshape(N_STATE, 1),
        _diag_of_b(g_bexp[:, :, :CH_N]), _diag_of_b(g_bexp[:, :, CH_N:]))
    g_c_re = _diag_of_c(g_cexp[:, :CH_N, :])
    g_c_im = -_diag_of_c(g_cexp[:, CH_N:, :])

    def assemble(du, dqkv, dgs, dga):
        row = jnp.concatenate([du.astype(mx), dqkv, dgs, dga], axis=-1)
        return row, _colsum(row)

    dprojb, g_b_in = _rowwise("in_assemble", assemble, [_time_unperm(du_p), dqkvb, dgsb, dgab], [],
                              [(IN_COLS, mx)], [IN_COLS])
    ex.grad("w_in", _mm("in_dw", h0b, dprojb, "tn",
                        carry=ex.carry(ici=["w_xo", "w_xq", "w_mix_out", "w_att_up"])))
    dh0 = _mm("in_dx", dprojb, ex.weight("w_in"), "nt", extras=(dr1,), epilogue=lambda r, d: (r + al * d,),
              carry=ex.carry(ici=["w_in"]))

    def ln_in_bwd(x, dout, g, b):
        _, vjp = jax.vjp(_ln, x, g, b)
        return vjp(dout)

    dx, g_ln_in_g, g_ln_in_b = _rowwise("ln_in_bwd", ln_in_bwd, [x, dh0], [sp["ln_in_g"], sp["ln_in_b"]],
                                        [(D_MODEL, F32)], [D_MODEL, D_MODEL], carry=ex.finish_carry())

    small = {"ln_in_g": g_ln_in_g, "ln_in_b": g_ln_in_b, "b_in": g_b_in, "ssm_log_dt": g_ldt, "ssm_a_re": g_are,
             "ssm_a_im": g_aim, "ssm_b_re": g_bre, "ssm_b_im": g_bim, "ssm_c_re": g_c_re, "ssm_c_im": g_c_im,
             "ssm_d": g_ssm_d, "b_glu": g_b_glu, "b_mix_out": g_b_mix, "ln1_g": g_ln1_g, "ln1_b": g_ln1_b,
             "ln2_g": g_ln2_g, "ln2_b": g_ln2_b, "b_ff1": g_b_ff1, "b_ff2": g_b_ff2, "ln3_g": g_ln3_g,
             "ln3_b": g_ln3_b}
    return loss, dx, small


def _piece_shape(k, n, axis):
    return (k // 2, n // 4) if axis == 1 else (k // 8, n)


def _aligned(v, m):
    return v if isinstance(v, int) else pl.multiple_of(v, m)


def _full_piece(ref, k, n, axis, chip, half):
    pr, pc = _piece_shape(k, n, axis)
    if axis == 1:
        return ref.at[pl.ds(_aligned(half * pr, 8), pr), pl.ds(_aligned(chip * pc, 128), pc)]
    return ref.at[pl.ds(_aligned(chip * (2 * pr) + half * pr, 8), pr), :]


def _shard_piece(ref, k, n, axis, half):
    pr, _ = _piece_shape(k, n, axis)
    return ref.at[pl.ds(_aligned(half * pr, 8), pr), :]


def _mesh_pos():
    x, y, c = lax.axis_index("x"), lax.axis_index("y"), lax.axis_index("c")
    other_chips = [(1 - x, y), (x, 1 - y), (1 - x, 1 - y)]
    return x, y, c, other_chips


def _remote(src, dst, send_sem, recv_sem, dev):
    return pltpu.make_async_remote_copy(src_ref=src, dst_ref=dst, send_sem=send_sem, recv_sem=recv_sem,
                                        device_id=dev, device_id_type=MESH)


def _placed(name, fn, n_steps, where, ins, out_sds, out_block, out_index):
    def body(w_ref, *refs):
        o_ref = refs[-1]
        o_ref[...] = fn(*[r[...] for r in refs[:-1]]).astype(o_ref.dtype)

    grid_spec = pltpu.PrefetchScalarGridSpec(
        num_scalar_prefetch=1, grid=(n_steps,), in_specs=[pl.BlockSpec(bs, idx) for _, bs, idx in ins],
        out_specs=pl.BlockSpec(out_block, out_index))
    return pl.pallas_call(body, name=name, grid_spec=grid_spec, out_shape=out_sds,
                          compiler_params=_cparams(1))(where, *[a for a, _, _ in ins])


def _gather_copies(widx):
    geo = [BIG[i][1:] for i in widx]

    def ici(full, wi, j, chip, send_sems, recv_sems, c, dev):
        k, n, ax = geo[wi]
        piece = _full_piece(full[wi], k, n, ax, chip, c)
        return _remote(piece, piece, send_sems.at[wi * 6 + j], recv_sems.at[wi * 6 + j], dev)

    def d2d(full, wi, j, chip, half, send_sems, recv_sems, sib):
        k, n, ax = geo[wi]
        piece = _full_piece(full[wi], k, n, ax, chip, half)
        return _remote(piece, piece, send_sems.at[wi * 6 + 3 + j], recv_sems.at[wi * 6 + 3 + j], sib)

    def start(_, full, send_sems, recv_sems):
        x, y, c, chips = _mesh_pos()
        for wi in range(len(geo)):
            for j, (qx, qy) in enumerate(chips):
                ici(full, wi, j, 2 * x + y, send_sems, recv_sems, c, (qx, qy, c)).start()

    def finish(_, full, send_sems, recv_sems):
        x, y, c, chips = _mesh_pos()
        sib = (x, y, 1 - c)
        for wi in range(len(geo)):
            for j, (qx, qy) in enumerate(chips):
                ici(full, wi, j, 2 * qx + qy, send_sems, recv_sems, c, (qx, qy, c)).wait_recv()
                d2d(full, wi, j, 2 * qx + qy, c, send_sems, recv_sems, sib).start()
        for wi in range(len(geo)):
            for j, (qx, qy) in enumerate(chips):
                d2d(full, wi, j, 2 * qx + qy, 1 - c, send_sems, recv_sems, sib).wait_recv()
        for wi in range(len(geo)):
            for j, (qx, qy) in enumerate(chips):
                ici(full, wi, j, 2 * x + y, send_sems, recv_sems, c, (qx, qy, c)).wait_send()
                d2d(full, wi, j, 2 * qx + qy, c, send_sems, recv_sems, sib).wait_send()

    return start, finish, 6 * len(geo)


def _swap_copies(widx):
    geo = [BIG[i][1:] for i in widx]

    def copies(g, got, send_sems, recv_sems, base):
        x, y, c, _ = _mesh_pos()
        return [_remote(_full_piece(g[wi], k, n, ax, q, 1 - c), got[wi].at[q], send_sems.at[base + wi * 4 + q],
                        recv_sems.at[base + wi * 4 + q], (x, y, 1 - c))
                for wi, (k, n, ax) in enumerate(geo) for q in range(4)]

    def start(g, got, send_sems, recv_sems, base=0):
        for cp in copies(g, got, send_sems, recv_sems, base):
            cp.start()

    def finish(g, got, send_sems, recv_sems, base=0):
        for cp in copies(g, got, send_sems, recv_sems, base):
            cp.wait()

    return start, finish, 4 * len(geo)


def _swap_shapes(widx):
    return [jax.ShapeDtypeStruct((4,) + _piece_shape(*BIG[i][1:]), F32) for i in widx]


def _reduce_swap_halves(tag, grads, widx):
    nw = len(widx)
    start, finish, n_sems = _swap_copies(widx)

    def body(*refs):
        start(refs[:nw], refs[nw:2 * nw], *refs[2 * nw:])
        finish(refs[:nw], refs[nw:2 * nw], *refs[2 * nw:])

    return pl.pallas_call(
        body, name="reduce_swap_halves_" + tag, in_specs=[HBM_SPEC] * nw, out_specs=[HBM_SPEC] * nw,
        out_shape=_swap_shapes(widx),
        scratch_shapes=[pltpu.SemaphoreType.DMA((n_sems,)), pltpu.SemaphoreType.DMA((n_sems,))])(*grads)


def _owner_copies(nw):
    def copies(p, out, send_sems, recv_sems, base):
        x, y, c, chips = _mesh_pos()
        return [_remote(p[wi].at[2 * qx + qy], out[wi].at[j], send_sems.at[base + wi * 3 + j],
                        recv_sems.at[base + wi * 3 + j], (qx, qy, c))
                for wi in range(nw) for j, (qx, qy) in enumerate(chips)]

    def start(p, out, send_sems, recv_sems, base=0):
        for cp in copies(p, out, send_sems, recv_sems, base):
            cp.start()

    def finish(p, out, send_sems, recv_sems, base=0):
        for cp in copies(p, out, send_sems, recv_sems, base):
            cp.wait()

    return start, finish, 3 * nw


def _join_carries(a, b):
    if a is None or b is None:
        return a if b is None else b
    n_i, n_o = len(a.ins), len(a.outs)
    outs = list(a.outs) + [o + n_i if isinstance(o, int) else o for o in b.outs]

    def start(c_in, c_out, send_sems, recv_sems):
        a.start(c_in[:n_i], c_out[:n_o], send_sems, recv_sems)
        b.start(c_in[n_i:], c_out[n_o:], send_sems, recv_sems, base=a.n_sems)

    def finish(c_in, c_out, send_sems, recv_sems):
        a.finish(c_in[:n_i], c_out[:n_o], send_sems, recv_sems)
        b.finish(c_in[n_i:], c_out[n_o:], send_sems, recv_sems, base=a.n_sems)

    def done(res):
        a.done(res[:n_o])
        b.done(res[n_o:])

    return _Carry(a.ins + b.ins, outs, a.n_sems + b.n_sems, start, finish, done)


def _share_copies():
    def copy(out, wi, half, send_sems, recv_sems, sib):
        _, k, n, ax = BIG[wi]
        piece = _shard_piece(out[wi], k, n, ax, half)
        return _remote(piece, piece, send_sems.at[wi], recv_sems.at[wi], sib)

    def start(_, out, send_sems, recv_sems):
        x, y, c, _ = _mesh_pos()
        for wi in range(len(BIG)):
            copy(out, wi, c, send_sems, recv_sems, (x, y, 1 - c)).start()

    def finish(_, out, send_sems, recv_sems):
        x, y, c, _ = _mesh_pos()
        for wi in range(len(BIG)):
            copy(out, wi, 1 - c, send_sems, recv_sems, (x, y, 1 - c)).wait_recv()
            copy(out, wi, c, send_sems, recv_sems, (x, y, 1 - c)).wait_send()

    return start, finish, len(BIG)


def _allreduce_small(v):
    r = v.shape[0]
    rh = r // 2
    assert rh % 8 == 0

    def body(v_ref, o_ref, sib_buf, chip_buf, send_sems, recv_sems):
        x, y, c, chips = _mesh_pos()
        me = 2 * x + y
        sib = (x, y, 1 - c)
        mine = pl.ds(pl.multiple_of(c * rh, 8), rh)
        other = pl.ds(pl.multiple_of((1 - c) * rh, 8), rh)
        swap = _remote(v_ref.at[other], sib_buf, send_sems.at[0], recv_sems.at[0], sib)
        swap.start()
        swap.wait()
        chip_buf[me] = v_ref[mine, :] + sib_buf[...]
        cps = []
        for j, (qx, qy) in enumerate(chips):
            cp = _remote(chip_buf.at[me], chip_buf.at[me], send_sems.at[1 + j], recv_sems.at[1 + j], (qx, qy, c))
            cp.start()
            cps.append(cp)
        for j, (qx, qy) in enumerate(chips):
            slot = chip_buf.at[2 * qx + qy]
            _remote(slot, slot, send_sems.at[1 + j], recv_sems.at[1 + j], (qx, qy, c)).wait_recv()
        for cp in cps:
            cp.wait_send()
        o_ref[mine, :] = ((chip_buf[0] + chip_buf[1]) + chip_buf[2]) + chip_buf[3]
        back = _remote(o_ref.at[mine], o_ref.at[mine], send_sems.at[4], recv_sems.at[4], sib)
        back.start()
        _remote(o_ref.at[other], o_ref.at[other], send_sems.at[4], recv_sems.at[4], sib).wait_recv()
        back.wait_send()

    return pl.pallas_call(
        body, name="allreduce_small", in_specs=[VMEM_SPEC], out_specs=VMEM_SPEC,
        out_shape=jax.ShapeDtypeStruct((r, 128), F32),
        scratch_shapes=[pltpu.VMEM((rh, 128), F32), pltpu.VMEM((4, rh, 128), F32),
                        pltpu.SemaphoreType.DMA((5,)), pltpu.SemaphoreType.DMA((5,))],
        compiler_params=pltpu.CompilerParams(vmem_limit_bytes=VMEM_LIMIT))(v)


def _as2d(a):
    a = a.reshape((-1, a.shape[-1])) if a.ndim > 1 else a.reshape(1, -1)
    return a


def _adamw_small(quads):
    n = len(quads)

    def body(*refs):
        for i in range(n):
            w, g, m, v = (r[...] for r in refs[4 * i:4 * i + 4])
            for ref, val in zip(refs[4 * n + 3 * i:4 * n + 3 * i + 3], _adamw(w, g, m, v)):
                ref[...] = val

    return pl.pallas_call(
        body, name="adamw_small", in_specs=[VMEM_SPEC] * (4 * n), out_specs=[VMEM_SPEC] * (3 * n),
        out_shape=[jax.ShapeDtypeStruct(q[0].shape, F32) for q in quads for _ in range(3)],
        compiler_params=pltpu.CompilerParams(vmem_limit_bytes=VMEM_LIMIT))(*[a for q in quads for a in q])


def _where():
    return jnp.stack([2 * lax.axis_index("x") + lax.axis_index("y"), lax.axis_index("c")]).astype(jnp.int32)


_BIG_INDEX = {name: i for i, (name, _, _, _) in enumerate(BIG)}


class _Exchange:
    def __init__(self, inputs, where):
        self.inputs, self.where = inputs, where
        self.full, self.ready = {}, set()
        self.raw, self.got, self.parts, self.landed, self.geom = {}, {}, {}, {}, {}
        for name, k, n, ax in BIG:
            w2 = inputs[name][0]
            rs, cs = w2.shape
            tm = _tile(rs, 512)
            steps = rs // tm
            if ax == 1:
                blk, idx = (tm, cs), lambda i, w: (i, w[0])
            else:
                blk, idx = (tm, n), functools.partial(lambda i, w, steps: (w[0] * steps + i, 0), steps=steps)
            self.full[name] = _placed("cast_" + name, lambda w: w, steps, where, [(w2, (tm, cs), lambda i, w: (i, 0))],
                                      jax.ShapeDtypeStruct((k, n), MXU_DTYPE), blk, idx)

    def _gathered(self, names, outs):
        for name, o in zip(names, outs):
            self.full[name] = o
            self.ready.add(name)

    def gather_carry(self, names):
        start, finish, n_sems = _gather_copies([_BIG_INDEX[n] for n in names])
        return _Carry([self.full[n] for n in names], list(range(len(names))), n_sems, start, finish,
                      functools.partial(self._gathered, names))

    def weight(self, name):
        assert name in self.ready, name
        return self.full[name]

    def grad(self, name, g):
        self.raw[name] = g

    def _swapped(self, names, outs):
        for name, o in zip(names, outs):
            self.got[name] = o

    def _pair_sum(self, name):
        i = _BIG_INDEX[name]
        _, k, n, ax = BIG[i]
        g = self.raw[name]
        if name not in self.got:
            self._swapped([name], _reduce_swap_halves(name, [g], [i]))
        got = self.got[name]
        pr, pc = _piece_shape(k, n, ax)
        tm = _tile(pr, 512)
        spp = pr // tm
        self.geom[name] = (pr, pc, tm, spp)
        if ax == 1:
            g_idx = functools.partial(lambda i, w, spp: (w[1] * spp + i % spp, i // spp), spp=spp)
        else:
            g_idx = functools.partial(lambda i, w, spp: ((i // spp) * 2 * spp + w[1] * spp + i % spp, 0), spp=spp)
        self.parts[name] = _placed(
            "pair_sum_" + name, lambda a, b: a + b, 4 * spp, self.where,
            [(g, (tm, pc), g_idx), (got.reshape(4 * pr, pc), (tm, pc), lambda i, w: (i, 0))],
            jax.ShapeDtypeStruct((4 * pr, pc), BF16), (tm, pc), lambda i, w: (i, 0)).reshape(4, pr, pc)

    def _landed(self, names, outs):
        for name, o in zip(names, outs):
            self.landed[name] = o

    def carry(self, swap=(), ici=()):
        first = second = None
        if swap:
            widx = [_BIG_INDEX[n] for n in swap]
            start, finish, n_sems = _swap_copies(widx)
            first = _Carry([self.raw[n] for n in swap], _swap_shapes(widx), n_sems, start, finish,
                           functools.partial(self._swapped, list(swap)))
        if ici:
            for n in ici:
                self._pair_sum(n)
            start, finish, n_sems = _owner_copies(len(ici))
            parts = [self.parts[n] for n in ici]
            outs = [jax.ShapeDtypeStruct((3,) + p.shape[1:], p.dtype) for p in parts]
            second = _Carry(parts, outs, n_sems, start, finish, functools.partial(self._landed, list(ici)))
        return _join_carries(first, second)

    def _shared(self, outs):
        self.shards = dict(zip([b[0] for b in BIG], outs))

    def finish_carry(self):
        halves = []
        for name, _, _, _ in BIG:
            pr, pc, tm, spp = self.geom[name]
            ins = [(self.parts[name], (None, tm, pc), lambda i, w: (w[0], i, 0))]
            ins += [(self.landed[name], (None, tm, pc), functools.partial(lambda i, w, j: (j, i, 0), j=j))
                    for j in range(3)]
            halves.append(_placed("chip_sum_" + name,
                                  lambda a, b, c, d: ((a.astype(F32) + b.astype(F32)) + c.astype(F32)) + d.astype(F32),
                                  spp, self.where, ins, jax.ShapeDtypeStruct(self.inputs[name].shape[1:], F32), (tm, pc),
                                  functools.partial(lambda i, w, spp: (w[1] * spp + i, 0), spp=spp)))
        start, finish, n_sems = _share_copies()
        return _Carry(halves, list(range(len(halves))), n_sems, start, finish, self._shared)


def _step(inputs):
    x, mem, positions, target = inputs["x"][0], inputs["mem"][0], inputs["positions"], inputs["loss_target"][0]
    pos = positions.reshape(-1, 1)
    ex = _Exchange(inputs, _where())
    sp = {name: _as2d(inputs[name]) for name in SMALL}
    memb, = _rowwise("cast_mem", lambda m: (m,), [mem], [], [(D_MODEL, MXU_DTYPE)])

    loss, dx, gsmall = _local_step(x, memb, pos, target, sp, ex)
    gshard = ex.shards

    out = {}
    for name, _, _, _ in BIG:
        w2, m2, v2 = inputs[name][0], inputs["m_" + name][0], inputs["v_" + name][0]
        n = w2.shape[1]
        d, nm, nv = _rowwise("adamw_" + name, _adamw, [w2, gshard[name], m2, v2], [], [(n, F32)] * 3, tm=_tile(w2.shape[0], 512))
        lead = inputs[name].shape
        out[name] = (gshard[name].reshape(lead), d.reshape(lead), nm.reshape(lead), nv.reshape(lead))

    def tiles(a):
        flat = a.reshape(-1)
        n = -(-flat.shape[0] // 1024) * 1024
        return jnp.pad(flat, (0, n - flat.shape[0])).reshape(n // 128, 128)

    pieces = [tiles(loss[:, :1])] + [tiles(gsmall[name]) for name in SMALL]
    if sum(p.shape[0] for p in pieces) % 16:
        pieces.append(jnp.zeros((8, 128), F32))
    red = _allreduce_small(jnp.concatenate(pieces, axis=0))
    loss_total = red[0, 0]
    grads, off = {}, pieces[0].shape[0]
    for name, p in zip(SMALL, pieces[1:]):
        shp = _as2d(inputs[name]).shape
        grads[name] = red[off:off + p.shape[0]].reshape(-1)[:shp[0] * shp[1]].reshape(shp)
        off += p.shape[0]
    upd = _adamw_small([(_as2d(inputs[n]), grads[n], _as2d(inputs["m_" + n]), _as2d(inputs["v_" + n])) for n in SMALL])
    for i, name in enumerate(SMALL):
        shp = inputs[name].shape
        out[name] = (grads[name].reshape(shp),) + tuple(t.reshape(shp) for t in upd[3 * i:3 * i + 3])
    return loss_total, dx.reshape(inputs["x"].shape), out


_ARG_NAMES = (("x", "mem", "positions") + WEIGHT_ORDER + ("loss_target",) + tuple("m_" + n for n in WEIGHT_ORDER)
              + tuple("v_" + n for n in WEIGHT_ORDER))


def kernel(x, mem, positions, ln_in_g, ln_in_b, w_in, b_in, ssm_log_dt, ssm_a_re, ssm_a_im, ssm_b_re, ssm_b_im, ssm_c_re, ssm_c_im, ssm_d, w_glu, b_glu, w_att_up, w_mix_out, b_mix_out, ln1_g, ln1_b, w_xq, w_xkv, w_xo, ln2_g, ln2_b, w_ff1, b_ff1, w_ff2, b_ff2, ln3_g, ln3_b, loss_target, m_ln_in_g, m_ln_in_b, m_w_in, m_b_in, m_ssm_log_dt, m_ssm_a_re, m_ssm_a_im, m_ssm_b_re, m_ssm_b_im, m_ssm_c_re, m_ssm_c_im, m_ssm_d, m_w_glu, m_b_glu, m_w_att_up, m_w_mix_out, m_b_mix_out, m_ln1_g, m_ln1_b, m_w_xq, m_w_xkv, m_w_xo, m_ln2_g, m_ln2_b, m_w_ff1, m_b_ff1, m_w_ff2, m_b_ff2, m_ln3_g, m_ln3_b, v_ln_in_g, v_ln_in_b, v_w_in, v_b_in, v_ssm_log_dt, v_ssm_a_re, v_ssm_a_im, v_ssm_b_re, v_ssm_b_im, v_ssm_c_re, v_ssm_c_im, v_ssm_d, v_w_glu, v_b_glu, v_w_att_up, v_w_mix_out, v_b_mix_out, v_ln1_g, v_ln1_b, v_w_xq, v_w_xkv, v_w_xo, v_ln2_g, v_ln2_b, v_w_ff1, v_b_ff1, v_w_ff2, v_b_ff2, v_ln3_g, v_ln3_b):
    args = (x, mem, positions, ln_in_g, ln_in_b, w_in, b_in, ssm_log_dt, ssm_a_re, ssm_a_im, ssm_b_re, ssm_b_im, ssm_c_re, ssm_c_im, ssm_d, w_glu, b_glu, w_att_up, w_mix_out, b_mix_out, ln1_g, ln1_b, w_xq, w_xkv, w_xo, ln2_g, ln2_b, w_ff1, b_ff1, w_ff2, b_ff2, ln3_g, ln3_b, loss_target, m_ln_in_g, m_ln_in_b, m_w_in, m_b_in, m_ssm_log_dt, m_ssm_a_re, m_ssm_a_im, m_ssm_b_re, m_ssm_b_im, m_ssm_c_re, m_ssm_c_im, m_ssm_d, m_w_glu, m_b_glu, m_w_att_up, m_w_mix_out, m_b_mix_out, m_ln1_g, m_ln1_b, m_w_xq, m_w_xkv, m_w_xo, m_ln2_g, m_ln2_b, m_w_ff1, m_b_ff1, m_w_ff2, m_b_ff2, m_ln3_g, m_ln3_b, v_ln_in_g, v_ln_in_b, v_w_in, v_b_in, v_ssm_log_dt, v_ssm_a_re, v_ssm_a_im, v_ssm_b_re, v_ssm_b_im, v_ssm_c_re, v_ssm_c_im, v_ssm_d, v_w_glu, v_b_glu, v_w_att_up, v_w_mix_out, v_b_mix_out, v_ln1_g, v_ln1_b, v_w_xq, v_w_xkv, v_w_xo, v_ln2_g, v_ln2_b, v_w_ff1, v_b_ff1, v_w_ff2, v_b_ff2, v_ln3_g, v_ln3_b)
    assert len(args) == len(_ARG_NAMES)
    inputs = dict(zip(_ARG_NAMES, args))
    loss, dx, out = _step(inputs)
    res = [loss, dx]
    for k in range(4):
        res += [out[name][k] for name in WEIGHT_ORDER]
    return tuple(res)
```

```python
import functools
import math

import numpy as np
import jax
import jax.numpy as jnp
from jax import lax
from jax.experimental import pallas as pl
from jax.experimental.pallas import tpu as pltpu

F32 = jnp.float32
BF16 = jnp.bfloat16
MXU_DTYPE = jnp.bfloat16

D_MODEL = 1024
SSM_GROUP = 16
SSM_WIDTH = 768
SSM_GROUPS = 48
SSM_STATE = 64
N_STATE = SSM_GROUPS * SSM_STATE
SSM_CHUNKS = 6
CH_W = 128
CH_N = 512
ATT_HEAD_DIM = 64
ATT_HPG = 4
ATT_GROUPW = ATT_HPG * ATT_HEAD_DIM
DILATIONS = (1, 4, 16)
ATT_BLK = 128
ATT_SCALE = ATT_HEAD_DIM ** -0.5
ROT_DIM = 16
ROPE_THETA = 500000.0
XATT_HEADS = 4
XATT_HEAD_DIM = 256
XATT_SCALE = XATT_HEAD_DIM ** -0.5
D_FF = 4096
IN_COLS = 5120
DEEPNORM_ALPHA = 2.0 ** 0.25
LN_EPS = 1e-5
NEG_INF = -1e30
ADAM_LR = 0.001
ADAM_B1 = 0.9
ADAM_B2 = 0.999
ADAM_EPS = 1e-08
ADAM_WD = 0.01
ADAM_STEP = 10

N_SEG = 32
VMEM_LIMIT = 56 * 1024 * 1024
MESH = pl.DeviceIdType.MESH
HBM_SPEC = pl.BlockSpec(memory_space=pltpu.HBM)
VMEM_SPEC = pl.BlockSpec(memory_space=pltpu.VMEM)

BIG = (("w_in", 1024, 5120, 1), ("w_glu", 768, 2048, 1), ("w_att_up", 256, 1024, 1),
       ("w_mix_out", 1024, 1024, 0), ("w_xq", 1024, 1024, 0), ("w_xkv", 1024, 2048, 1),
       ("w_xo", 1024, 1024, 0), ("w_ff1", 1024, 4096, 1), ("w_ff2", 4096, 1024, 0))
SMALL = ("ln_in_g", "ln_in_b", "b_in", "ssm_log_dt", "ssm_a_re", "ssm_a_im", "ssm_b_re", "ssm_b_im",
         "ssm_c_re", "ssm_c_im", "ssm_d", "b_glu", "b_mix_out", "ln1_g", "ln1_b", "ln2_g", "ln2_b",
         "b_ff1", "b_ff2", "ln3_g", "ln3_b")
WEIGHT_ORDER = ("ln_in_g", "ln_in_b", "w_in", "b_in", "ssm_log_dt", "ssm_a_re", "ssm_a_im", "ssm_b_re",
                "ssm_b_im", "ssm_c_re", "ssm_c_im", "ssm_d", "w_glu", "b_glu", "w_att_up", "w_mix_out",
                "b_mix_out", "ln1_g", "ln1_b", "w_xq", "w_xkv", "w_xo", "ln2_g", "ln2_b", "w_ff1", "b_ff1",
                "w_ff2", "b_ff2", "ln3_g", "ln3_b")


def _cparams(n_axes):
    return pltpu.CompilerParams(dimension_semantics=("arbitrary",) * n_axes, vmem_limit_bytes=VMEM_LIMIT)


class _Carry:
    def __init__(self, ins, outs, n_sems, start, finish, done):
        self.ins, self.outs, self.n_sems, self.start, self.finish, self.done = ins, outs, n_sems, start, finish, done


def _call(name, body, grid, in_specs, out_specs, out_shape, args, scratch_shapes=(), carry=None):
    in_specs, out_specs, out_shape = list(in_specs), list(out_specs), list(out_shape)
    params = _cparams(len(grid))
    if carry is None:
        return pl.pallas_call(body, name=name, grid=grid, in_specs=in_specs, out_specs=out_specs, out_shape=out_shape,
                              scratch_shapes=list(scratch_shapes), compiler_params=params)(*args)
    n_in, n_out, n_ci, n_co = len(in_specs), len(out_specs), len(carry.ins), len(carry.outs)
    n_scr = len(scratch_shapes)

    def wrapped(*refs):
        ins, c_in = refs[:n_in], refs[n_in:n_in + n_ci]
        outs, c_out = refs[n_in + n_ci:n_in + n_ci + n_out], refs[n_in + n_ci + n_out:n_in + n_ci + n_out + n_co]
        scratch = refs[n_in + n_ci + n_out + n_co:n_in + n_ci + n_out + n_co + n_scr]
        send_sems, recv_sems = refs[-2:]
        ids = [pl.program_id(a) for a in range(len(grid))]
        first = functools.reduce(jnp.logical_and, [i == 0 for i in ids])
        last = functools.reduce(jnp.logical_and, [i == g - 1 for i, g in zip(ids, grid)])

        @pl.when(first)
        def _():
            carry.start(c_in, c_out, send_sems, recv_sems)

        body(*ins, *outs, *scratch)

        @pl.when(last)
        def _():
            carry.finish(c_in, c_out, send_sems, recv_sems)

    c_shapes = [jax.ShapeDtypeStruct(carry.ins[o].shape, carry.ins[o].dtype) if isinstance(o, int) else o
                for o in carry.outs]
    aliases = {n_in + o: n_out + i for i, o in enumerate(carry.outs) if isinstance(o, int)}
    res = pl.pallas_call(
        wrapped, name=name, grid=grid, in_specs=in_specs + [HBM_SPEC] * n_ci, out_specs=out_specs + [HBM_SPEC] * n_co,
        out_shape=out_shape + c_shapes, input_output_aliases=aliases,
        scratch_shapes=list(scratch_shapes) + [pltpu.SemaphoreType.DMA((carry.n_sems,))] * 2,
        compiler_params=params)(*args, *carry.ins)
    carry.done(res[n_out:])
    return res[:n_out]


def _rowwise(name, fn, rows, consts, outs, reds=(), tm=512, touts=(), carry=None):
    n_rows = (rows[0][0] if isinstance(rows[0], tuple) else rows[0]).shape[-2]
    tm = min(tm, n_rows)
    assert n_rows % tm == 0, (name, n_rows, tm)
    specs, args = [], []
    for r in rows:
        if isinstance(r, tuple) and len(r) == 3:
            arr, width, cb = r
            specs.append(pl.BlockSpec((tm, width), functools.partial(lambda i, cb: (i, cb), cb=cb)))
        elif isinstance(r, tuple):
            arr, slot = r
            specs.append(pl.BlockSpec((None, tm, arr.shape[2]), functools.partial(lambda i, s: (s, i, 0), s=slot)))
        else:
            arr = r
            specs.append(pl.BlockSpec((tm, arr.shape[1]), lambda i: (i, 0)))
        args.append(arr)
        assert arr.shape[-2] == n_rows, (name, arr.shape, n_rows)
    for cst in consts:
        specs.append(pl.BlockSpec(cst.shape, lambda i: (0, 0)))
        args.append(cst)
    n_r, n_c, n_o, n_d = len(rows), len(consts), len(outs) + len(touts), len(reds)
    out_shape = [jax.ShapeDtypeStruct((n_rows, c), dt) for c, dt in outs]
    out_specs = [pl.BlockSpec((tm, c), lambda i: (i, 0)) for c, _ in outs]
    out_shape += [jax.ShapeDtypeStruct((r, n_rows), dt) for r, dt in touts]
    out_specs += [pl.BlockSpec((r, tm), lambda i: (0, i)) for r, _ in touts]
    out_shape += [jax.ShapeDtypeStruct((1, c), F32) for c in reds]
    out_specs += [pl.BlockSpec((1, c), lambda i: (0, 0)) for c in reds]

    def body(*refs):
        ins = [r[...] for r in refs[:n_r + n_c]]
        o_refs = refs[n_r + n_c:n_r + n_c + n_o]
        d_refs = refs[n_r + n_c + n_o:]
        res = fn(*ins)
        res = res if isinstance(res, (tuple, list)) else (res,)
        assert len(res) == n_o + n_d, (name, len(res))
        for ref, val in zip(o_refs, res[:n_o]):
            ref[...] = val.astype(ref.dtype)
        first = pl.program_id(0) == 0
        for ref, val in zip(d_refs, res[n_o:]):
            @pl.when(first)
            def _(ref=ref, val=val):
                ref[...] = val

            @pl.when(jnp.logical_not(first))
            def _(ref=ref, val=val):
                ref[...] += val

    return _call(name, body, (n_rows // tm,), specs, out_specs, out_shape, args, carry=carry)


def _colsum(v):
    return jnp.sum(v.astype(F32), axis=0, keepdims=True)


_DIMS = {"nn": (((1,), (0,)), ((), ())), "nt": (((1,), (1,)), ((), ())), "tn": (((0,), (0,)), ((), ()))}


def _tile(dim, want):
    if dim <= want:
        return dim
    return max(t for t in range(128, want + 1, 128) if dim % t == 0)


def _dot(a, b, mode):
    return lax.dot_general(a.astype(MXU_DTYPE), b.astype(MXU_DTYPE), _DIMS[mode], preferred_element_type=F32)


def _mm(name, a, b, mode, *, bias=None, extras=(), epilogue=None, out_dtypes=(F32,), tm=1024, tn=1024, tk=1024,
        carry=None, colsum=False, rows=()):
    if mode == "nn":
        (m, k), (_, n) = a.shape, b.shape
    elif mode == "nt":
        (m, k), (n, _) = a.shape, b.shape
    else:
        (k, m), (_, n) = a.shape, b.shape
    if k > tk:
        tk = 5 * tk
    tn = _tile(n, tn)
    tk = _tile(k, tk)
    nk = k // tk

    def vmem_bytes(rows):
        blocks = rows * tk * a.dtype.itemsize + tk * tn * b.dtype.itemsize
        blocks += sum(rows * tn * (e[0] if isinstance(e, tuple) else e).dtype.itemsize for e in extras)
        blocks += sum(rows * tn * jnp.dtype(dt).itemsize for dt in out_dtypes)
        return 2 * blocks + (rows * tn * 4 if nk > 1 else 0)

    tm = _tile(m, tm if mode == "tn" else 2 * tm)
    while vmem_bytes(tm) > 3 * VMEM_LIMIT // 4 and tm % 256 == 0:
        tm //= 2
    while nk == 1 and k > 1024 and (m // tm) * (n // tn) < 4 and tm % 256 == 0:
        tm //= 2
    assert m % tm == 0 and n % tn == 0 and k % tk == 0, (name, m, n, k)
    a_spec = {"nn": pl.BlockSpec((tm, tk), lambda i, j, kk: (i, kk)),
              "nt": pl.BlockSpec((tm, tk), lambda i, j, kk: (i, kk)),
              "tn": pl.BlockSpec((tk, tm), lambda i, j, kk: (kk, i))}[mode]
    b_spec = {"nn": pl.BlockSpec((tk, tn), lambda i, j, kk: (kk, j)),
              "nt": pl.BlockSpec((tn, tk), lambda i, j, kk: (j, kk)),
              "tn": pl.BlockSpec((tk, tn), lambda i, j, kk: (kk, j))}[mode]
    specs, args = [a_spec, b_spec], [a, b]
    if bias is not None:
        specs.append(pl.BlockSpec((1, tn), lambda i, j, kk: (0, j)))
        args.append(bias)
    for e in extras:
        e, off = e if isinstance(e, tuple) else (e, 0)
        specs.append(pl.BlockSpec((tm, tn), functools.partial(lambda i, j, kk, off: (i, j + off), off=off)))
        args.append(e)
    for rc in rows:
        specs.append(pl.BlockSpec((1, tn), lambda i, j, kk: (0, j)))
        args.append(rc)
    n_e, n_o = len(extras) + len(rows), len(out_dtypes)
    has_bias = bias is not None

    def body(*refs):
        a_ref, b_ref = refs[0], refs[1]
        pos = 2
        bias_ref = refs[pos] if has_bias else None
        pos += int(has_bias)
        e_refs = refs[pos:pos + n_e]
        o_refs = refs[pos + n_e:pos + n_e + n_o]
        sum_ref = refs[pos + n_e + n_o] if colsum else None
        acc_ref = refs[pos + n_e + n_o + int(colsum)] if nk > 1 else None
        part = _dot(a_ref[...], b_ref[...], mode)

        def finish(r):
            if has_bias:
                r = r + bias_ref[...]
            res = epilogue(r, *[e[...] for e in e_refs]) if epilogue is not None else (r,)
            for ref, val in zip(o_refs, res):
                ref[...] = val.astype(ref.dtype)
            if colsum:
                sum_ref[...] = _colsum(res[0])

        if nk == 1:
            finish(part)
        else:
            kk = pl.program_id(2)

            @pl.when(kk == 0)
            def _():
                acc_ref[...] = part

            @pl.when(kk > 0)
            def _():
                acc_ref[...] += part

            @pl.when(kk == nk - 1)
            def _():
                finish(acc_ref[...])

    out_specs = [pl.BlockSpec((tm, tn), lambda i, j, kk: (i, j)) for _ in out_dtypes]
    out_shape = [jax.ShapeDtypeStruct((m, n), dt) for dt in out_dtypes]
    if colsum:
        out_specs.append(pl.BlockSpec((None, 1, tn), lambda i, j, kk: (i, 0, j)))
        out_shape.append(jax.ShapeDtypeStruct((m // tm, 1, n), F32))
    res = _call(name, body, (m // tm, n // tn, nk), specs, out_specs, out_shape, args,
                scratch_shapes=[pltpu.VMEM((tm, tn), F32)] if nk > 1 else [], carry=carry)
    return res[0] if len(res) == 1 else res


def _ssm_wgrads(u, dy, g_re, g_im, h_re, h_im, tk=2048, carry=None):
    s = u.shape[0]
    tk = min(tk, s)
    nk = s // tk
    assert tk % N_SEG == 0

    def body(u_ref, dy_ref, gre_ref, gim_ref, hre_ref, him_ref, lre_ref, lim_ref, db_ref, dc_ref, dar_ref, dai_ref,
             pre_ref, pim_ref):
        kk = pl.program_id(1)
        u_blk, dy_blk = u_ref[...], dy_ref[...]
        g_r, g_i, h_r, h_i = gre_ref[...], gim_ref[...], hre_ref[...], him_ref[...]
        d_b = jnp.concatenate([_dot(u_blk, g_r, "tn"), _dot(u_blk, g_i, "tn")], axis=1)
        d_c = jnp.concatenate([_dot(h_r, dy_blk, "tn"), _dot(h_i, dy_blk, "tn")], axis=0)

        @pl.when(kk == 0)
        def _():
            first_row = lax.broadcasted_iota(jnp.int32, (N_SEG, CH_N), 0) == 0
            pre_ref[...] = jnp.where(first_row, 0.0, pltpu.roll(lre_ref[...], 1, 0))
            pim_ref[...] = jnp.where(first_row, 0.0, pltpu.roll(lim_ref[...], 1, 0))

        p_r = jnp.concatenate([pre_ref[...], h_r[:tk - N_SEG]], axis=0)
        p_i = jnp.concatenate([pim_ref[...], h_i[:tk - N_SEG]], axis=0)
        pre_ref[...] = h_r[tk - N_SEG:]
        pim_ref[...] = h_i[tk - N_SEG:]
        d_ar = jnp.sum(g_r * p_r + g_i * p_i, axis=0, keepdims=True)
        d_ai = jnp.sum(g_i * p_r - g_r * p_i, axis=0, keepdims=True)

        @pl.when(kk == 0)
        def _():
            db_ref[...] = d_b
            dc_ref[...] = d_c
            dar_ref[...] = d_ar
            dai_ref[...] = d_ai

        @pl.when(kk > 0)
        def _():
            db_ref[...] += d_b
            dc_ref[...] += d_c
            dar_ref[...] += d_ar
            dai_ref[...] += d_ai

    chan = pl.BlockSpec((tk, CH_W), lambda j, kk: (kk, j))
    state = pl.BlockSpec((tk, CH_N), lambda j, kk: (kk, j))
    last = pl.BlockSpec((N_SEG, CH_N), lambda j, kk: (s // N_SEG - 1, j))
    row = pl.BlockSpec((1, CH_N), lambda j, kk: (0, j))
    return _call(
        "ssm_wgrads", body, (SSM_CHUNKS, nk), [chan, chan, state, state, state, state, last, last],
        [pl.BlockSpec((None, CH_W, 2 * CH_N), lambda j, kk: (j, 0, 0)),
         pl.BlockSpec((None, 2 * CH_N, CH_W), lambda j, kk: (j, 0, 0)), row, row],
        [jax.ShapeDtypeStruct((SSM_CHUNKS, CH_W, 2 * CH_N), F32), jax.ShapeDtypeStruct((SSM_CHUNKS, 2 * CH_N, CH_W), F32),
         jax.ShapeDtypeStruct((1, N_STATE), F32), jax.ShapeDtypeStruct((1, N_STATE), F32)],
        (u, dy, g_re, g_im, h_re, h_im, h_re, h_im), scratch_shapes=[pltpu.VMEM((N_SEG, CH_N), F32)] * 2, carry=carry)


SCAN_LB = 256


def _split_by_scan_block(mat, axis):
    halves = []
    for l in range(CH_N // SCAN_LB):
        re = lax.slice_in_dim(mat, l * SCAN_LB, (l + 1) * SCAN_LB, axis=axis)
        im = lax.slice_in_dim(mat, CH_N + l * SCAN_LB, CH_N + (l + 1) * SCAN_LB, axis=axis)
        halves.append(jnp.concatenate([re, im], axis=axis))
    return jnp.stack(halves, axis=1).reshape((-1,) + halves[0].shape[1:])


def _ssm_scan(name, chan, expand12, contract12, a_re, a_im, d_row, reverse, carry=None):
    s = chan.shape[0]
    seg_len = s // N_SEG
    n_sq = int(math.log2(seg_len))
    assert 2 ** n_sq == seg_len
    rb = min(512, s)
    per_chunk = CH_N // SCAN_LB

    def body(are_ref, aim_ref, ch_ref, e_ref, k_ref, d_ref, hre_ref, him_ref, o_ref, wre_ref, wim_ref, ere, eim, cre, cim):
        e_mat, k_mat = e_ref[...], k_ref[...]
        for r in range(s // rb):
            rows = slice(r * rb, (r + 1) * rb)
            w = _dot(ch_ref[rows, :], e_mat, "nt" if reverse else "nn")
            wre_ref[rows, :] = w[:, :SCAN_LB]
            wim_ref[rows, :] = w[:, SCAN_LB:]

        ar1 = are_ref[...]
        ai1 = -aim_ref[...] if reverse else aim_ref[...]
        ar = jnp.broadcast_to(ar1, (N_SEG, SCAN_LB))
        ai = jnp.broadcast_to(ai1, (N_SEG, SCAN_LB))

        def rows_of(k):
            kk = seg_len - 1 - k if reverse else k
            return pl.ds(pl.multiple_of(kk * N_SEG, N_SEG), N_SEG)

        def local(k, carry):
            hr, hi = carry
            rows = rows_of(k)
            nr = ar * hr - ai * hi + wre_ref[rows, :]
            ni = ar * hi + ai * hr + wim_ref[rows, :]
            hre_ref[rows, :] = nr
            him_ref[rows, :] = ni
            return nr, ni

        zero = jnp.zeros((N_SEG, SCAN_LB), F32)
        er, ei = lax.fori_loop(0, seg_len, local, (zero, zero))
        ere[...] = er
        eim[...] = ei
        pr, pi = ar1, ai1
        for _ in range(n_sq):
            pr, pi = pr * pr - pi * pi, 2.0 * pr * pi
        cr = jnp.zeros((1, SCAN_LB), F32)
        ci = jnp.zeros((1, SCAN_LB), F32)
        for jj in range(N_SEG):
            j = N_SEG - 1 - jj if reverse else jj
            cre[j:j + 1, :] = cr
            cim[j:j + 1, :] = ci
            er_j, ei_j = ere[j:j + 1, :], eim[j:j + 1, :]
            cr, ci = pr * cr - pi * ci + er_j, pr * ci + pi * cr + ei_j
        c_r, c_i = cre[...], cim[...]

        def fix(k, carry):
            qr, qi = carry
            rows = rows_of(k)
            hre_ref[rows, :] = hre_ref[rows, :] + (qr * c_r - qi * c_i)
            him_ref[rows, :] = him_ref[rows, :] + (qr * c_i + qi * c_r)
            return qr * ar - qi * ai, qr * ai + qi * ar

        lax.fori_loop(0, seg_len, fix, (ar, ai))

        first_of_chunk = lax.rem(pl.program_id(0), per_chunk) == 0
        for r in range(s // rb):
            rows = slice(r * rb, (r + 1) * rb)
            h_cat = jnp.concatenate([hre_ref[rows, :], him_ref[rows, :]], axis=1)
            part = _dot(h_cat, k_mat, "nt" if reverse else "nn")

            @pl.when(first_of_chunk)
            def _(rows=rows, part=part):
                o_ref[rows, :] = part + d_ref[...] * ch_ref[rows, :]

            @pl.when(jnp.logical_not(first_of_chunk))
            def _(rows=rows, part=part):
                o_ref[rows, :] += part

    nblk = N_STATE // SCAN_LB
    blk = pl.BlockSpec((s, SCAN_LB), lambda b: (0, b))
    row = pl.BlockSpec((1, SCAN_LB), lambda b: (0, b))
    chan_blk = pl.BlockSpec((s, CH_W), lambda b: (0, b // per_chunk))
    res = _call(name, body, (nblk,),
                [row, row, chan_blk, pl.BlockSpec((None,) + expand12.shape[1:], lambda b: (b, 0, 0)),
                 pl.BlockSpec((None,) + contract12.shape[1:], lambda b: (b, 0, 0)),
                 pl.BlockSpec((1, CH_W), lambda b: (0, b // per_chunk))],
                [blk, blk, chan_blk],
                [jax.ShapeDtypeStruct((s, N_STATE), F32)] * 2 + [jax.ShapeDtypeStruct((s, SSM_WIDTH), F32)],
                (a_re, a_im, chan, expand12, contract12, d_row),
                scratch_shapes=[pltpu.VMEM((s, SCAN_LB), F32)] * 2 + [pltpu.VMEM((N_SEG, SCAN_LB), F32)] * 4, carry=carry)
    return res[0], res[1], res[2]


def _disc(ldt, are, aim, bre, bim):
    dt = jnp.exp(ldt)
    mag = jnp.exp(are * dt)
    abr = mag * jnp.cos(aim * dt)
    abi = mag * jnp.sin(aim * dt)
    den = jnp.square(are) + jnp.square(aim)
    nr = abr - 1.0
    fre = (nr * are + abi * aim) / den
    fim = (abi * are - nr * aim) / den
    return abr, abi, fre * bre - fim * bim, fre * bim + fim * bre


def _ssm_disc_fwd(ldt, are, aim, bre, bim):
    def body(l_ref, ar_ref, ai_ref, br_ref, bi_ref, o0, o1, o2, o3):
        res = _disc(l_ref[...], ar_ref[...], ai_ref[...], br_ref[...], bi_ref[...])
        for ref, val in zip((o0, o1, o2, o3), res):
            ref[...] = val

    col = jax.ShapeDtypeStruct((N_STATE, 1), F32)
    mat = jax.ShapeDtypeStruct((N_STATE, SSM_GROUP), F32)
    return pl.pallas_call(body, name="ssm_disc_fwd", out_shape=[col, col, mat, mat],
                          in_specs=[VMEM_SPEC] * 5, out_specs=[VMEM_SPEC] * 4)(ldt, are, aim, bre, bim)


def _ssm_disc_bwd(ldt, are, aim, bre, bim, d_abr, d_abi, d_bbr, d_bbi):
    def body(l_ref, ar_ref, ai_ref, br_ref, bi_ref, c0, c1, c2, c3, g_ldt, g_are, g_aim, g_bre, g_bim):
        _, vjp = jax.vjp(_disc, l_ref[...], ar_ref[...], ai_ref[...], br_ref[...], bi_ref[...])
        dl, dar, dai, dbr, dbi = vjp((c0[...], c1[...], c2[...], c3[...]))
        state = lax.broadcasted_iota(jnp.int32, (N_STATE, SSM_GROUPS), 0)
        group = lax.broadcasted_iota(jnp.int32, (N_STATE, SSM_GROUPS), 1)
        pick = jnp.right_shift(state, 6) == group
        g_ldt[...] = jnp.sum(jnp.where(pick, dl, 0.0), axis=0, keepdims=True)
        g_are[...] = dar
        g_aim[...] = dai
        g_bre[...] = dbr
        g_bim[...] = dbi

    col = jax.ShapeDtypeStruct((N_STATE, 1), F32)
    mat = jax.ShapeDtypeStruct((N_STATE, SSM_GROUP), F32)
    return pl.pallas_call(body, name="ssm_disc_bwd",
                          out_shape=[jax.ShapeDtypeStruct((1, SSM_GROUPS), F32), col, col, mat, mat],
                          in_specs=[VMEM_SPEC] * 9, out_specs=[VMEM_SPEC] * 5,
                          compiler_params=pltpu.CompilerParams(vmem_limit_bytes=VMEM_LIMIT))(
        ldt, are, aim, bre, bim, d_abr, d_abi, d_bbr, d_bbi)


_EYE8 = np.eye(8, dtype=np.float32)


def _blockdiag_b(bb):
    t = bb.reshape(SSM_CHUNKS, 8, SSM_STATE, SSM_GROUP).transpose(0, 1, 3, 2)
    return jnp.einsum("igcn,gh->igchn", t, _EYE8).reshape(SSM_CHUNKS, CH_W, CH_N)


def _diag_of_b(m):
    t = jnp.einsum("igchn,gh->igcn", m.reshape(SSM_CHUNKS, 8, SSM_GROUP, 8, SSM_STATE), _EYE8)
    return t.transpose(0, 1, 3, 2).reshape(N_STATE, SSM_GROUP)


def _blockdiag_c(c):
    t = c.reshape(SSM_CHUNKS, 8, SSM_GROUP, SSM_STATE).transpose(0, 1, 3, 2)
    return jnp.einsum("ignc,gh->ignhc", t, _EYE8).reshape(SSM_CHUNKS, CH_N, CH_W)


def _diag_of_c(m):
    t = jnp.einsum("ignhc,gh->ignc", m.reshape(SSM_CHUNKS, 8, SSM_STATE, 8, SSM_GROUP), _EYE8)
    return t.transpose(0, 1, 3, 2).reshape(SSM_GROUPS, SSM_GROUP, SSM_STATE)


def _time_perm(a):
    s, c = a.shape
    return a.reshape(N_SEG, s // N_SEG, c).transpose(1, 0, 2).reshape(s, c)


def _time_unperm(a):
    s, c = a.shape
    return a.reshape(s // N_SEG, N_SEG, c).transpose(1, 0, 2).reshape(s, c)


def _dilate(a, d):
    s, c = a.shape
    return a if d == 1 else a.reshape(s // d, d, c).transpose(1, 0, 2).reshape(s, c)


def _undilate(a, d):
    s, c = a.shape
    return a if d == 1 else a.reshape(d, s // d, c).transpose(1, 0, 2).reshape(s, c)


def _dilate_rows(a, d):
    r, s = a.shape
    return a if d == 1 else a.reshape(r, s // d, d).transpose(0, 2, 1).reshape(r, s)


ATT_T_FWD = 4
ATT_T_BWD = 8


def _window(prev_ref, cur_ref, i, sl):
    if i == 0:
        return jnp.concatenate([prev_ref[:, sl], cur_ref[0:ATT_BLK, sl]], axis=0)
    return cur_ref[(i - 1) * ATT_BLK:(i + 1) * ATT_BLK, sl]


def _band_valid(first_key):
    qi = lax.broadcasted_iota(jnp.int32, (ATT_BLK, 2 * ATT_BLK), 0)
    ki = lax.broadcasted_iota(jnp.int32, (ATT_BLK, 2 * ATT_BLK), 1)
    steps = qi + ATT_BLK - ki
    return (steps >= 0) & (steps <= ATT_BLK) & (ki >= first_key)


ATT_STATW = ATT_HPG * 128


def _stat(h):
    return slice(h * 128, (h + 1) * 128)


def _stat_rows(stat):
    n = stat.shape[0]
    heads = [stat[:, _stat(h)].T[0:1, :] for h in range(ATT_HPG)]
    return jnp.concatenate(heads + [jnp.zeros((8 - ATT_HPG, n), stat.dtype)], axis=0)


def _attn_specs(nb, t, width=ATT_GROUPW):
    cur = pl.BlockSpec((t * ATT_BLK, width), lambda b: (b, 0))
    prev = pl.BlockSpec((ATT_BLK, width), lambda b: (jnp.maximum(b * t - 1, 0), 0))
    nxt = pl.BlockSpec((ATT_BLK, width), lambda b: (jnp.minimum((b + 1) * t, nb - 1), 0))
    return cur, prev, nxt


def _attn_fwd(tag, per_seq, q, k, v):
    s = q.shape[0]
    nb = s // ATT_BLK

    def body(q_ref, kc_ref, kp_ref, vc_ref, vp_ref, o_ref, lse_ref):
        bt = pl.program_id(0)
        for i in range(ATT_T_FWD):
            has_prev = lax.rem(bt * ATT_T_FWD + i, per_seq) > 0
            valid = _band_valid(jnp.where(has_prev, 0, ATT_BLK))
            rows = slice(i * ATT_BLK, (i + 1) * ATT_BLK)
            for h in range(ATT_HPG):
                sl = slice(h * ATT_HEAD_DIM, (h + 1) * ATT_HEAD_DIM)
                kcat = _window(kp_ref, kc_ref, i, sl)
                vcat = _window(vp_ref, vc_ref, i, sl)
                sc = _dot(q_ref[rows, sl], kcat, "nt") * ATT_SCALE
                sc = jnp.where(valid, sc, NEG_INF)
                m = jnp.max(sc, axis=-1, keepdims=True)
                p = jnp.exp(sc - m)
                den = jnp.sum(p, axis=-1, keepdims=True)
                o_ref[rows, sl] = _dot(p, vcat, "nn") / den
                lse_ref[rows, _stat(h)] = jnp.broadcast_to(m + jnp.log(den), (ATT_BLK, 128))

    cur, prev, _ = _attn_specs(nb, ATT_T_FWD)
    stat, _, _ = _attn_specs(nb, ATT_T_FWD, ATT_STATW)
    return pl.pallas_call(
        body, name="attn_fwd_" + tag, grid=(nb // ATT_T_FWD,), in_specs=[cur, cur, prev, cur, prev], out_specs=[cur, stat],
        out_shape=[jax.ShapeDtypeStruct((s, ATT_GROUPW), F32), jax.ShapeDtypeStruct((s, ATT_STATW), F32)],
        compiler_params=_cparams(1))(q, k, k, v, v)


def _attn_dq(tag, per_seq, q, k, v, do, lse, delta):
    s = q.shape[0]
    nb = s // ATT_BLK

    def body(q_ref, kc_ref, kp_ref, vc_ref, vp_ref, do_ref, lse_ref, dl_ref, dq_ref):
        bt = pl.program_id(0)
        for i in range(ATT_T_BWD):
            has_prev = lax.rem(bt * ATT_T_BWD + i, per_seq) > 0
            valid = _band_valid(jnp.where(has_prev, 0, ATT_BLK))
            rows = slice(i * ATT_BLK, (i + 1) * ATT_BLK)
            for h in range(ATT_HPG):
                sl = slice(h * ATT_HEAD_DIM, (h + 1) * ATT_HEAD_DIM)
                kcat = _window(kp_ref, kc_ref, i, sl)
                vcat = _window(vp_ref, vc_ref, i, sl)
                lse = jnp.concatenate([lse_ref[rows, _stat(h)]] * 2, axis=1)
                dlt = jnp.concatenate([dl_ref[rows, _stat(h)]] * 2, axis=1)
                sc = _dot(q_ref[rows, sl], kcat, "nt") * ATT_SCALE
                p = jnp.exp(jnp.where(valid, sc, NEG_INF) - lse)
                dp = _dot(do_ref[rows, sl], vcat, "nt")
                ds = p * (dp - dlt) * ATT_SCALE
                dq_ref[rows, sl] = _dot(ds, kcat, "nn")

    cur, prev, _ = _attn_specs(nb, ATT_T_BWD)
    stat, _, _ = _attn_specs(nb, ATT_T_BWD, ATT_STATW)
    return pl.pallas_call(
        body, name="attn_dq_" + tag, grid=(nb // ATT_T_BWD,), in_specs=[cur, cur, prev, cur, prev, cur, stat, stat],
        out_specs=cur, out_shape=jax.ShapeDtypeStruct((s, ATT_GROUPW), F32),
        compiler_params=_cparams(1))(q, k, k, v, v, do, lse, delta)


def _attn_dkv(tag, per_seq, q, k, v, do, lse_t, delta_t):
    s = q.shape[0]
    nb = s // ATT_BLK

    def body(k_ref, v_ref, qc_ref, qn_ref, doc_ref, don_ref, lc_ref, ln_ref, dc_ref, dn_ref, dk_ref, dv_ref):
        bt = pl.program_id(0)
        ki = lax.broadcasted_iota(jnp.int32, (ATT_BLK, 2 * ATT_BLK), 0)
        ci = lax.broadcasted_iota(jnp.int32, (ATT_BLK, 2 * ATT_BLK), 1)

        def pair(edge_ref, cur_ref, i, sl):
            if i == ATT_T_BWD - 1:
                return jnp.concatenate([cur_ref[i * ATT_BLK:(i + 1) * ATT_BLK, sl], edge_ref[:, sl]], axis=0)
            return cur_ref[i * ATT_BLK:(i + 2) * ATT_BLK, sl]

        def pair_row(edge_ref, cur_ref, i, h):
            if i == ATT_T_BWD - 1:
                row = jnp.concatenate([cur_ref[h:h + 1, i * ATT_BLK:(i + 1) * ATT_BLK], edge_ref[h:h + 1, :]], axis=1)
            else:
                row = cur_ref[h:h + 1, i * ATT_BLK:(i + 2) * ATT_BLK]
            return jnp.broadcast_to(row, (ATT_BLK, 2 * ATT_BLK))

        for i in range(ATT_T_BWD):
            b = bt * ATT_T_BWD + i
            next_uses = (b + 1 < nb) & (lax.rem(b + 1, per_seq) > 0)
            reach = jnp.where(next_uses, 0, 4 * ATT_BLK)
            valid = ((ci < ATT_BLK) & (ci >= ki)) | ((ci >= ATT_BLK) & (ki - ci + ATT_BLK >= reach))
            rows = slice(i * ATT_BLK, (i + 1) * ATT_BLK)
            for h in range(ATT_HPG):
                sl = slice(h * ATT_HEAD_DIM, (h + 1) * ATT_HEAD_DIM)
                qcat, docat = pair(qn_ref, qc_ref, i, sl), pair(don_ref, doc_ref, i, sl)
                sc = _dot(k_ref[rows, sl], qcat, "nt") * ATT_SCALE
                p = jnp.exp(jnp.where(valid, sc, NEG_INF) - pair_row(ln_ref, lc_ref, i, h))
                dv_ref[rows, sl] = _dot(p, docat, "nn")
                dp = _dot(v_ref[rows, sl], docat, "nt")
                ds = p * (dp - pair_row(dn_ref, dc_ref, i, h)) * ATT_SCALE
                dk_ref[rows, sl] = _dot(ds, qcat, "nn")

    cur, _, nxt = _attn_specs(nb, ATT_T_BWD)
    stat = pl.BlockSpec((8, ATT_T_BWD * ATT_BLK), lambda b: (0, b))
    snxt = pl.BlockSpec((8, ATT_BLK), lambda b: (0, jnp.minimum((b + 1) * ATT_T_BWD, nb - 1)))
    return pl.pallas_call(
        body, name="attn_dkv_" + tag, grid=(nb // ATT_T_BWD,), in_specs=[cur, cur, cur, nxt, cur, nxt, stat, snxt, stat, snxt],
        out_specs=[cur, cur], out_shape=[jax.ShapeDtypeStruct((s, ATT_GROUPW), F32)] * 2,
        compiler_params=_cparams(1))(k, v, q, q, do, do, lse_t, lse_t, delta_t, delta_t)


def _xattn_probs(q, kh):
    sc = _dot(q, kh, "nt") * XATT_SCALE
    e = jnp.exp(sc - jnp.max(sc, axis=-1, keepdims=True))
    return e / jnp.sum(e, axis=-1, keepdims=True)


def _xattn_fwd(q, kv, tm=512):
    s = q.shape[0]
    tm = min(tm, s)

    def body(q_ref, kv_ref, o_ref):
        for h in range(XATT_HEADS):
            sl = slice(h * XATT_HEAD_DIM, (h + 1) * XATT_HEAD_DIM)
            vs = slice(D_MODEL + h * XATT_HEAD_DIM, D_MODEL + (h + 1) * XATT_HEAD_DIM)
            p = _xattn_probs(q_ref[:, sl], kv_ref[:, sl])
            o_ref[:, sl] = _dot(p, kv_ref[:, vs], "nn").astype(o_ref.dtype)

    return pl.pallas_call(
        body, name="xattn_fwd", grid=(s // tm,),
        in_specs=[pl.BlockSpec((tm, D_MODEL), lambda i: (i, 0)), pl.BlockSpec(kv.shape, lambda i: (0, 0))],
        out_specs=pl.BlockSpec((tm, D_MODEL), lambda i: (i, 0)),
        out_shape=jax.ShapeDtypeStruct((s, D_MODEL), MXU_DTYPE), compiler_params=_cparams(1))(q, kv)


def _xattn_bwd(q, kv, do, tm=1024):
    s = q.shape[0]
    tm = min(tm, s)

    def body(q_ref, kv_ref, do_ref, dq_ref, dkv_ref):
        first = pl.program_id(0) == 0

        @pl.when(first)
        def _():
            dkv_ref[...] = jnp.zeros_like(dkv_ref)

        for h in range(XATT_HEADS):
            sl = slice(h * XATT_HEAD_DIM, (h + 1) * XATT_HEAD_DIM)
            vs = slice(D_MODEL + h * XATT_HEAD_DIM, D_MODEL + (h + 1) * XATT_HEAD_DIM)
            p = _xattn_probs(q_ref[:, sl], kv_ref[:, sl])
            dkv_ref[:, vs] += _dot(p, do_ref[:, sl], "tn")
            dp = _dot(do_ref[:, sl], kv_ref[:, vs], "nt")
            ds = p * (dp - jnp.sum(dp * p, axis=-1, keepdims=True)) * XATT_SCALE
            dq_ref[:, sl] = _dot(ds, kv_ref[:, sl], "nn").astype(dq_ref.dtype)
            dkv_ref[:, sl] += _dot(ds, q_ref[:, sl], "tn")

    row = pl.BlockSpec((tm, D_MODEL), lambda i: (i, 0))
    whole = pl.BlockSpec(kv.shape, lambda i: (0, 0))
    return pl.pallas_call(
        body, name="xattn_bwd", grid=(s // tm,), in_specs=[row, whole, row], out_specs=[row, whole],
        out_shape=[jax.ShapeDtypeStruct((s, D_MODEL), MXU_DTYPE), jax.ShapeDtypeStruct(kv.shape, F32)],
        compiler_params=_cparams(1))(q, kv, do)


def _ln(x, g, b):
    mu = jnp.mean(x, axis=-1, keepdims=True)
    xc = x - mu
    var = jnp.mean(jnp.square(xc), axis=-1, keepdims=True)
    return xc * lax.rsqrt(var + LN_EPS) * g + b


def _res_ln(h, o, g, b):
    return _ln(DEEPNORM_ALPHA * h + o, g, b)


def _gate(gs, ga, z1, z2, batt):
    return jax.nn.sigmoid(gs) * (z1 * jax.nn.sigmoid(z2)) + jax.nn.sigmoid(ga) * batt


ROPE_TW = 2 * ATT_HEAD_DIM


def _rope_tables(pos, invf, m1, m2):
    ang = pos.astype(F32) * invf
    sin = jnp.sin(ang)
    return jnp.cos(ang), -sin * m1, sin * m2


def _widen(tab):
    return jnp.concatenate([tab] * (ATT_GROUPW // ROPE_TW), axis=1)


def _rope(t, cos, s_up, s_dn):
    w = t.shape[-1]
    return t * cos + pltpu.roll(t, w - ROT_DIM // 2, 1) * s_up + pltpu.roll(t, ROT_DIM // 2, 1) * s_dn


def _rope_t(dt, cos, s_up, s_dn):
    w = dt.shape[-1]
    return dt * cos + pltpu.roll(dt * s_up, ROT_DIM // 2, 1) + pltpu.roll(dt * s_dn, w - ROT_DIM // 2, 1)


def _rope_consts():
    inv_freq = ROPE_THETA ** (-jnp.arange(0, ROT_DIM, 2, dtype=F32) / ROT_DIM)
    d = np.arange(ROPE_TW) % ATT_HEAD_DIM
    invf = jnp.where(d < ROT_DIM, inv_freq[d % (ROT_DIM // 2)], 0.0).reshape(1, ROPE_TW).astype(F32)
    m1 = jnp.asarray((d < ROT_DIM // 2).astype(np.float32)).reshape(1, ROPE_TW)
    m2 = jnp.asarray(((d >= ROT_DIM // 2) & (d < ROT_DIM)).astype(np.float32)).reshape(1, ROPE_TW)
    return invf, m1, m2


def _head_sum_matrix():
    d = np.arange(ATT_GROUPW) // ATT_HEAD_DIM
    s = np.arange(ATT_STATW) // 128
    return jnp.asarray((d[:, None] == s[None, :]).astype(np.float32))


def _adamw(w, g, m, v):
    m = ADAM_B1 * m + (1.0 - ADAM_B1) * g
    v = ADAM_B2 * v + (1.0 - ADAM_B2) * jnp.square(g)
    m_hat = m / (1.0 - ADAM_B1 ** ADAM_STEP)
    v_hat = v / (1.0 - ADAM_B2 ** ADAM_STEP)
    delta = -ADAM_LR * (m_hat / (jnp.sqrt(v_hat) + ADAM_EPS) + ADAM_WD * w)
    return delta, m, v


def _local_step(x, mem, pos, target, sp, ex):
    s = x.shape[0]
    al = DEEPNORM_ALPHA
    mx = MXU_DTYPE

    h0, h0b = _rowwise("ln_in", lambda x, g, b: (lambda h: (h, h))(_ln(x, g, b)), [x],
                       [sp["ln_in_g"], sp["ln_in_b"]], [(D_MODEL, F32), (D_MODEL, mx)],
                       carry=ex.gather_carry(["w_in"]))
    proj = _mm("proj", h0b, ex.weight("w_in"), "nn", bias=sp["b_in"],
               carry=ex.gather_carry(["w_glu", "w_att_up", "w_mix_out", "w_xq", "w_xo"]))

    ldt = jnp.repeat(sp["ssm_log_dt"].reshape(SSM_GROUPS), SSM_STATE).reshape(N_STATE, 1)
    are, aim = sp["ssm_a_re"].reshape(N_STATE, 1), sp["ssm_a_im"].reshape(N_STATE, 1)
    bre, bim = sp["ssm_b_re"].reshape(N_STATE, SSM_GROUP), sp["ssm_b_im"].reshape(N_STATE, SSM_GROUP)
    abr, abi, bbr, bbi = _ssm_disc_fwd(ldt, are, aim, bre, bim)
    a_re, a_im = abr.reshape(1, N_STATE), abi.reshape(1, N_STATE)
    bexp = jnp.concatenate([_blockdiag_b(bbr), _blockdiag_b(bbi)], axis=2).astype(mx)
    cexp = jnp.concatenate([_blockdiag_c(sp["ssm_c_re"].reshape(SSM_GROUPS, SSM_GROUP, SSM_STATE)),
                            -_blockdiag_c(sp["ssm_c_im"].reshape(SSM_GROUPS, SSM_GROUP, SSM_STATE))],
                           axis=1).astype(mx)
    u_p = _time_perm(proj[:, :SSM_WIDTH])
    b12, c12 = _split_by_scan_block(bexp, 2), _split_by_scan_block(cexp, 1)
    h_re, h_im, y_p = _ssm_scan("ssm_scan_fwd", u_p, b12, c12, a_re, a_im, sp["ssm_d"], reverse=False,
                                carry=ex.gather_carry(["w_xkv", "w_ff1", "w_ff2"]))
    y = _time_unperm(y_p)
    ygb, = _rowwise("gelu", lambda y: jax.nn.gelu(y), [y], [], [(SSM_WIDTH, mx)])
    z = _mm("glu", ygb, ex.weight("w_glu"), "nn", bias=sp["b_glu"])

    invf, m1, m2 = _rope_consts()

    def rope_fwd(pos, q0, q1, q2, k0, k1, k2, v0, v1, v2, invf, m1, m2):
        narrow = _rope_tables(pos, invf, m1, m2)
        tabs = [_widen(t) for t in narrow]
        return tuple(_rope(t, *tabs) for t in (q0, q1, q2, k0, k1, k2)) + (v0, v1, v2) + tuple(narrow)

    qkv_cols = [(proj, ATT_GROUPW, 3 + i) for i in range(9)]
    qkv = _rowwise("rope", rope_fwd, [pos] + qkv_cols, [invf, m1, m2], [(ATT_GROUPW, mx)] * 9 + [(ROPE_TW, F32)] * 3)
    rope_tabs = qkv[9:]
    n_blocks = s // ATT_BLK
    groups = [(str(g), n_blocks // d, d) for g, d in enumerate(DILATIONS)]
    q_d = [_dilate(qkv[g], d) for g, d in enumerate(DILATIONS)]
    k_d = [_dilate(qkv[3 + g], d) for g, d in enumerate(DILATIONS)]
    v_d = [_dilate(qkv[6 + g], d) for g, d in enumerate(DILATIONS)]
    o_g, l_g = [], []
    for g, (tag, per_seq, d) in enumerate(groups):
        o, lse = _attn_fwd(tag, per_seq, q_d[g], k_d[g], v_d[g])
        o_g.append(_undilate(o, d))
        l_g.append(_undilate(lse, d))

    def merge(o0, o1, o2, l0, l1, l2):
        m = jnp.maximum(jnp.maximum(l0, l1), l2)
        e0, e1, e2 = jnp.exp(l0 - m), jnp.exp(l1 - m), jnp.exp(l2 - m)
        tot = e0 + e1 + e2

        def per_dim(e):
            w = e / tot
            return jnp.concatenate([w[:, h * 128:h * 128 + ATT_HEAD_DIM] for h in range(ATT_HPG)], axis=1)

        att = per_dim(e0) * o0 + per_dim(e1) * o1 + per_dim(e2) * o2
        lse = m + jnp.log(tot)
        return att, att, lse, _stat_rows(lse)

    att, attb, lse_tot, lse_tot_t = _rowwise("attn_merge", merge, o_g + l_g, [],
                                             [(ATT_GROUPW, F32), (ATT_GROUPW, mx), (ATT_STATW, F32)], touts=[(8, F32)])
    batt, mixedb = _mm("att_up", attb, ex.weight("w_att_up"), "nn", tn=D_MODEL,
                       extras=((proj, 3), (proj, 4), (z, 0), (z, 1)), out_dtypes=(F32, mx),
                       epilogue=lambda r, gs, ga, z1, z2: (r, _gate(gs, ga, z1, z2, r)))
    gate_rows = [(proj, D_MODEL, 3), (proj, D_MODEL, 4), (z, D_MODEL, 0), (z, D_MODEL, 1), batt]
    def branch_ln(o, h, g, b):
        hn = _res_ln(h, o, g, b)
        return o, hn, hn

    assert ex.weight("w_mix_out").shape[1] == D_MODEL
    o1, h1, h1b = _mm("mix_out", mixedb, ex.weight("w_mix_out"), "nn", bias=sp["b_mix_out"], extras=(h0,),
                      rows=(sp["ln1_g"], sp["ln1_b"]), epilogue=branch_ln, out_dtypes=(F32, F32, mx), tn=D_MODEL)

    qx = _mm("xq", h1b, ex.weight("w_xq"), "nn", out_dtypes=(mx,))
    kvx = _mm("xkv", mem, ex.weight("w_xkv"), "nn", out_dtypes=(mx,))
    oxb = _xattn_fwd(qx, kvx)
    o2, h2, h2b = _mm("xo", oxb, ex.weight("w_xo"), "nn", extras=(h1,), rows=(sp["ln2_g"], sp["ln2_b"]),
                      epilogue=branch_ln, out_dtypes=(F32, F32, mx), tn=D_MODEL)

    a_ff, fb = _mm("ff1", h2b, ex.weight("w_ff1"), "nn", bias=sp["b_ff1"],
                   epilogue=lambda r: (r, jnp.square(jnp.maximum(r, 0.0))), out_dtypes=(F32, mx))
    o3 = _mm("ff2", fb, ex.weight("w_ff2"), "nn", bias=sp["b_ff2"])

    def loss_bwd(h2, o3, tgt, g, b):
        def f(h2, o3, g, b):
            h3 = _res_ln(h2, o3, g, b)
            return 0.5 * jnp.sum(jnp.mean(jnp.square(h3 - tgt), axis=-1))

        loss, vjp = jax.vjp(f, h2, o3, g, b)
        _, dr, dg, db = vjp(jnp.ones((), F32))
        return dr, dr, dg, db, _colsum(dr), jnp.full((1, 128), loss, F32)

    dr3, dr3b, g_ln3_g, g_ln3_b, g_b_ff2, loss = _rowwise(
        "loss_ln3_bwd", loss_bwd, [h2, o3, target], [sp["ln3_g"], sp["ln3_b"]],
        [(D_MODEL, F32), (D_MODEL, mx)], [D_MODEL, D_MODEL, D_MODEL, 128])

    dab, da_sums = _mm("ff2_dx", dr3b, ex.weight("w_ff2"), "nt", extras=(a_ff,),
                       epilogue=lambda r, a: (r * (2.0 * jnp.maximum(a, 0.0)),), out_dtypes=(mx,), colsum=True)
    g_b_ff1 = jnp.sum(da_sums, axis=0)
    ex.grad("w_ff2", _mm("ff2_dw", fb, dr3b, "tn"))
    ex.grad("w_ff1", _mm("ff1_dw", h2b, dab, "tn", carry=ex.carry(swap=["w_ff2"])))
    dh2 = _mm("ff1_dx", dab, ex.weight("w_ff1"), "nt", extras=(dr3,), epilogue=lambda r, d: (r + al * d,),
              carry=ex.carry(swap=["w_ff1"]))

    def ln_bwd(h, o, dout, g, b):
        _, vjp = jax.vjp(_res_ln, h, o, g, b)
        _, dr, dg, db = vjp(dout)
        return dr, dr, dg, db, _colsum(dr)

    dr2, dr2b, g_ln2_g, g_ln2_b, _ = _rowwise(
        "ln2_bwd", ln_bwd, [h1, o2, dh2], [sp["ln2_g"], sp["ln2_b"]],
        [(D_MODEL, F32), (D_MODEL, mx)], [D_MODEL, D_MODEL, D_MODEL])
    ex.grad("w_xo", _mm("xo_dw", oxb, dr2b, "tn"))
    doxb = _mm("xo_dx", dr2b, ex.weight("w_xo"), "nt", out_dtypes=(mx,))
    dqxb, dkvx = _xattn_bwd(qx, kvx, doxb)
    ex.grad("w_xq", _mm("xq_dw", h1b, dqxb, "tn"))
    dh1 = _mm("xq_dx", dqxb, ex.weight("w_xq"), "nt", extras=(dr2,), epilogue=lambda r, d: (r + al * d,))
    ex.grad("w_xkv", _mm("xkv_dw", mem, dkvx, "tn"))

    dr1, dr1b, g_ln1_g, g_ln1_b, g_b_mix = _rowwise(
        "ln1_bwd", ln_bwd, [h0, o1, dh1], [sp["ln1_g"], sp["ln1_b"]],
        [(D_MODEL, F32), (D_MODEL, mx)], [D_MODEL, D_MODEL, D_MODEL])
    ex.grad("w_mix_out", _mm("mix_dw", mixedb, dr1b, "tn"))
    dmixed = _mm("mix_dx", dr1b, ex.weight("w_mix_out"), "nt")

    def gate_bwd(gs, ga, z1, z2, batt, dm):
        _, vjp = jax.vjp(_gate, gs, ga, z1, z2, batt)
        dgs, dga, dz1, dz2, dbatt = vjp(dm)
        dz = jnp.concatenate([dz1, dz2], axis=-1)
        return dgs, dga, dz, dbatt, _colsum(dz)

    dgsb, dgab, dzb, dbattb, g_b_glu = _rowwise(
        "gate_bwd", gate_bwd, gate_rows + [dmixed], [],
        [(D_MODEL, mx), (D_MODEL, mx), (2 * D_MODEL, mx), (D_MODEL, mx)], [2 * D_MODEL])
    ex.grad("w_att_up", _mm("att_up_dw", attb, dbattb, "tn"))
    datt = _mm("att_up_dx", dbattb, ex.weight("w_att_up"), "nt")

    def att_delta(datt, att, hs):
        dl = jnp.dot(datt * att, hs, precision=lax.Precision.HIGHEST, preferred_element_type=F32)
        return datt, dl, _stat_rows(dl)

    dattb, delta, delta_t = _rowwise("attn_delta", att_delta, [datt, att], [_head_sum_matrix()],
                                     [(ATT_GROUPW, mx), (ATT_STATW, F32)], touts=[(8, F32)])
    dq_g, dk_g, dv_g = [], [], []
    for g, (tag, per_seq, d) in enumerate(groups):
        do_d, lt_d, dl_d = _dilate(dattb, d), _dilate(lse_tot, d), _dilate(delta, d)
        dq_g.append(_undilate(_attn_dq(tag, per_seq, q_d[g], k_d[g], v_d[g], do_d, lt_d, dl_d), d))
        dk, dv = _attn_dkv(tag, per_seq, q_d[g], k_d[g], v_d[g], do_d, _dilate_rows(lse_tot_t, d), _dilate_rows(delta_t, d))
        dk_g.append(_undilate(dk, d))
        dv_g.append(_undilate(dv, d))
    dqkv = dq_g + dk_g + dv_g

    def rope_bwd(q0, q1, q2, k0, k1, k2, v0, v1, v2, cos, s_up, s_dn):
        tabs = [_widen(t) for t in (cos, s_up, s_dn)]
        return jnp.concatenate([_rope_t(t, *tabs) for t in (q0, q1, q2, k0, k1, k2)] + [v0, v1, v2], axis=-1)

    dqkvb, = _rowwise("rope_bwd", rope_bwd, dqkv + list(rope_tabs), [], [(9 * ATT_GROUPW, mx)])

    ex.grad("w_glu", _mm("glu_dw", ygb, dzb, "tn",
                         carry=ex.carry(swap=["w_xo", "w_xq", "w_xkv", "w_mix_out", "w_att_up"])))
    dyg = _mm("glu_dx", dzb, ex.weight("w_glu"), "nt", carry=ex.carry(swap=["w_glu"]))

    def gelu_bwd(y, dyg):
        _, vjp = jax.vjp(jax.nn.gelu, y)
        return vjp(dyg)[0]

    dy, = _rowwise("gelu_bwd", gelu_bwd, [y, dyg], [], [(SSM_WIDTH, F32)])
    dy_p = _time_perm(dy)
    s_re, s_im, du_p = _ssm_scan("ssm_scan_bwd", dy_p, c12, b12, a_re, a_im, sp["ssm_d"], reverse=True,
                                 carry=ex.carry(ici=["w_ff1", "w_xkv", "w_glu"]))
    g_bexp, g_cexp, d_abr, d_abi = _ssm_wgrads(u_p, dy_p, s_re, s_im, h_re, h_im, carry=ex.carry(ici=["w_ff2"]))
    g_ssm_d, = _rowwise("ssm_dd", lambda a, b: (_colsum(a * b),), [dy_p, u_p], [], [], [SSM_WIDTH])
    g_ldt, g_are, g_aim, g_bre, g_bim = _ssm_disc_bwd(
        ldt, are, aim, bre, bim, d_abr.reshape(N_STATE, 1), d_abi.reshape(N_STATE, 1),
        _diag_of_b(g_bexp[:, :, :CH_N]), _diag_of_b(g_bexp[:, :, CH_N:]))
    g_c_re = _diag_of_c(g_cexp[:, :CH_N, :])
    g_c_im = -_diag_of_c(g_cexp[:, CH_N:, :])

    def assemble(du, dqkv, dgs, dga):
        row = jnp.concatenate([du.astype(mx), dqkv, dgs, dga], axis=-1)
        return row, _colsum(row)

    dprojb, g_b_in = _rowwise("in_assemble", assemble, [_time_unperm(du_p), dqkvb, dgsb, dgab], [],
                              [(IN_COLS, mx)], [IN_COLS])
    ex.grad("w_in", _mm("in_dw", h0b, dprojb, "tn",
                        carry=ex.carry(ici=["w_xo", "w_xq", "w_mix_out", "w_att_up"])))
    dh0 = _mm("in_dx", dprojb, ex.weight("w_in"), "nt", extras=(dr1,), epilogue=lambda r, d: (r + al * d,),
              carry=ex.carry(ici=["w_in"]))

    def ln_in_bwd(x, dout, g, b):
        _, vjp = jax.vjp(_ln, x, g, b)
        return vjp(dout)

    dx, g_ln_in_g, g_ln_in_b = _rowwise("ln_in_bwd", ln_in_bwd, [x, dh0], [sp["ln_in_g"], sp["ln_in_b"]],
                                        [(D_MODEL, F32)], [D_MODEL, D_MODEL], carry=ex.finish_carry())

    small = {"ln_in_g": g_ln_in_g, "ln_in_b": g_ln_in_b, "b_in": g_b_in, "ssm_log_dt": g_ldt, "ssm_a_re": g_are,
             "ssm_a_im": g_aim, "ssm_b_re": g_bre, "ssm_b_im": g_bim, "ssm_c_re": g_c_re, "ssm_c_im": g_c_im,
             "ssm_d": g_ssm_d, "b_glu": g_b_glu, "b_mix_out": g_b_mix, "ln1_g": g_ln1_g, "ln1_b": g_ln1_b,
             "ln2_g": g_ln2_g, "ln2_b": g_ln2_b, "b_ff1": g_b_ff1, "b_ff2": g_b_ff2, "ln3_g": g_ln3_g,
             "ln3_b": g_ln3_b}
    return loss, dx, small


def _piece_shape(k, n, axis):
    return (k // 2, n // 4) if axis == 1 else (k // 8, n)


def _aligned(v, m):
    return v if isinstance(v, int) else pl.multiple_of(v, m)


def _full_piece(ref, k, n, axis, chip, half):
    pr, pc = _piece_shape(k, n, axis)
    if axis == 1:
        return ref.at[pl.ds(_aligned(half * pr, 8), pr), pl.ds(_aligned(chip * pc, 128), pc)]
    return ref.at[pl.ds(_aligned(chip * (2 * pr) + half * pr, 8), pr), :]


def _shard_piece(ref, k, n, axis, half):
    pr, _ = _piece_shape(k, n, axis)
    return ref.at[pl.ds(_aligned(half * pr, 8), pr), :]


def _mesh_pos():
    x, y, c = lax.axis_index("x"), lax.axis_index("y"), lax.axis_index("c")
    other_chips = [(1 - x, y), (x, 1 - y), (1 - x, 1 - y)]
    return x, y, c, other_chips


def _remote(src, dst, send_sem, recv_sem, dev):
    return pltpu.make_async_remote_copy(src_ref=src, dst_ref=dst, send_sem=send_sem, recv_sem=recv_sem,
                                        device_id=dev, device_id_type=MESH)


def _placed(name, fn, n_steps, where, ins, out_sds, out_block, out_index):
    def body(w_ref, *refs):
        o_ref = refs[-1]
        o_ref[...] = fn(*[r[...] for r in refs[:-1]]).astype(o_ref.dtype)

    grid_spec = pltpu.PrefetchScalarGridSpec(
        num_scalar_prefetch=1, grid=(n_steps,), in_specs=[pl.BlockSpec(bs, idx) for _, bs, idx in ins],
        out_specs=pl.BlockSpec(out_block, out_index))
    return pl.pallas_call(body, name=name, grid_spec=grid_spec, out_shape=out_sds,
                          compiler_params=_cparams(1))(where, *[a for a, _, _ in ins])


def _gather_copies(widx):
    geo = [BIG[i][1:] for i in widx]

    def ici(full, wi, j, chip, send_sems, recv_sems, c, dev):
        k, n, ax = geo[wi]
        piece = _full_piece(full[wi], k, n, ax, chip, c)
        return _remote(piece, piece, send_sems.at[wi * 6 + j], recv_sems.at[wi * 6 + j], dev)

    def d2d(full, wi, j, chip, half, send_sems, recv_sems, sib):
        k, n, ax = geo[wi]
        piece = _full_piece(full[wi], k, n, ax, chip, half)
        return _remote(piece, piece, send_sems.at[wi * 6 + 3 + j], recv_sems.at[wi * 6 + 3 + j], sib)

    def start(_, full, send_sems, recv_sems):
        x, y, c, chips = _mesh_pos()
        for wi in range(len(geo)):
            for j, (qx, qy) in enumerate(chips):
                ici(full, wi, j, 2 * x + y, send_sems, recv_sems, c, (qx, qy, c)).start()

    def finish(_, full, send_sems, recv_sems):
        x, y, c, chips = _mesh_pos()
        sib = (x, y, 1 - c)
        for wi in range(len(geo)):
            for j, (qx, qy) in enumerate(chips):
                ici(full, wi, j, 2 * qx + qy, send_sems, recv_sems, c, (qx, qy, c)).wait_recv()
                d2d(full, wi, j, 2 * qx + qy, c, send_sems, recv_sems, sib).start()
        for wi in range(len(geo)):
            for j, (qx, qy) in enumerate(chips):
                d2d(full, wi, j, 2 * qx + qy, 1 - c, send_sems, recv_sems, sib).wait_recv()
        for wi in range(len(geo)):
            for j, (qx, qy) in enumerate(chips):
                ici(full, wi, j, 2 * x + y, send_sems, recv_sems, c, (qx, qy, c)).wait_send()
                d2d(full, wi, j, 2 * qx + qy, c, send_sems, recv_sems, sib).wait_send()

    return start, finish, 6 * len(geo)


def _swap_copies(widx):
    geo = [BIG[i][1:] for i in widx]

    def copies(g, got, send_sems, recv_sems, base):
        x, y, c, _ = _mesh_pos()
        return [_remote(_full_piece(g[wi], k, n, ax, q, 1 - c), got[wi].at[q], send_sems.at[base + wi * 4 + q],
                        recv_sems.at[base + wi * 4 + q], (x, y, 1 - c))
                for wi, (k, n, ax) in enumerate(geo) for q in range(4)]

    def start(g, got, send_sems, recv_sems, base=0):
        for cp in copies(g, got, send_sems, recv_sems, base):
            cp.start()

    def finish(g, got, send_sems, recv_sems, base=0):
        for cp in copies(g, got, send_sems, recv_sems, base):
            cp.wait()

    return start, finish, 4 * len(geo)


def _swap_shapes(widx):
    return [jax.ShapeDtypeStruct((4,) + _piece_shape(*BIG[i][1:]), F32) for i in widx]


def _reduce_swap_halves(tag, grads, widx):
    nw = len(widx)
    start, finish, n_sems = _swap_copies(widx)

    def body(*refs):
        start(refs[:nw], refs[nw:2 * nw], *refs[2 * nw:])
        finish(refs[:nw], refs[nw:2 * nw], *refs[2 * nw:])

    return pl.pallas_call(
        body, name="reduce_swap_halves_" + tag, in_specs=[HBM_SPEC] * nw, out_specs=[HBM_SPEC] * nw,
        out_shape=_swap_shapes(widx),
        scratch_shapes=[pltpu.SemaphoreType.DMA((n_sems,)), pltpu.SemaphoreType.DMA((n_sems,))])(*grads)


def _owner_copies(nw):
    def copies(p, out, send_sems, recv_sems, base):
        x, y, c, chips = _mesh_pos()
        return [_remote(p[wi].at[2 * qx + qy], out[wi].at[j], send_sems.at[base + wi * 3 + j],
                        recv_sems.at[base + wi * 3 + j], (qx, qy, c))
                for wi in range(nw) for j, (qx, qy) in enumerate(chips)]

    def start(p, out, send_sems, recv_sems, base=0):
        for cp in copies(p, out, send_sems, recv_sems, base):
            cp.start()

    def finish(p, out, send_sems, recv_sems, base=0):
        for cp in copies(p, out, send_sems, recv_sems, base):
            cp.wait()

    return start, finish, 3 * nw


def _join_carries(a, b):
    if a is None or b is None:
        return a if b is None else b
    n_i, n_o = len(a.ins), len(a.outs)
    outs = list(a.outs) + [o + n_i if isinstance(o, int) else o for o in b.outs]

    def start(c_in, c_out, send_sems, recv_sems):
        a.start(c_in[:n_i], c_out[:n_o], send_sems, recv_sems)
        b.start(c_in[n_i:], c_out[n_o:], send_sems, recv_sems, base=a.n_sems)

    def finish(c_in, c_out, send_sems, recv_sems):
        a.finish(c_in[:n_i], c_out[:n_o], send_sems, recv_sems)
        b.finish(c_in[n_i:], c_out[n_o:], send_sems, recv_sems, base=a.n_sems)

    def done(res):
        a.done(res[:n_o])
        b.done(res[n_o:])

    return _Carry(a.ins + b.ins, outs, a.n_sems + b.n_sems, start, finish, done)


def _share_copies():
    def copy(out, wi, half, send_sems, recv_sems, sib):
        _, k, n, ax = BIG[wi]
        piece = _shard_piece(out[wi], k, n, ax, half)
        return _remote(piece, piece, send_sems.at[wi], recv_sems.at[wi], sib)

    def start(_, out, send_sems, recv_sems):
        x, y, c, _ = _mesh_pos()
        for wi in range(len(BIG)):
            copy(out, wi, c, send_sems, recv_sems, (x, y, 1 - c)).start()

    def finish(_, out, send_sems, recv_sems):
        x, y, c, _ = _mesh_pos()
        for wi in range(len(BIG)):
            copy(out, wi, 1 - c, send_sems, recv_sems, (x, y, 1 - c)).wait_recv()
            copy(out, wi, c, send_sems, recv_sems, (x, y, 1 - c)).wait_send()

    return start, finish, len(BIG)


def _allreduce_small(v):
    r = v.shape[0]
    rh = r // 2
    assert rh % 8 == 0

    def body(v_ref, o_ref, sib_buf, chip_buf, send_sems, recv_sems):
        x, y, c, chips = _mesh_pos()
        me = 2 * x + y
        sib = (x, y, 1 - c)
        mine = pl.ds(pl.multiple_of(c * rh, 8), rh)
        other = pl.ds(pl.multiple_of((1 - c) * rh, 8), rh)
        swap = _remote(v_ref.at[other], sib_buf, send_sems.at[0], recv_sems.at[0], sib)
        swap.start()
        swap.wait()
        chip_buf[me] = v_ref[mine, :] + sib_buf[...]
        cps = []
        for j, (qx, qy) in enumerate(chips):
            cp = _remote(chip_buf.at[me], chip_buf.at[me], send_sems.at[1 + j], recv_sems.at[1 + j], (qx, qy, c))
            cp.start()
            cps.append(cp)
        for j, (qx, qy) in enumerate(chips):
            slot = chip_buf.at[2 * qx + qy]
            _remote(slot, slot, send_sems.at[1 + j], recv_sems.at[1 + j], (qx, qy, c)).wait_recv()
        for cp in cps:
            cp.wait_send()
        o_ref[mine, :] = ((chip_buf[0] + chip_buf[1]) + chip_buf[2]) + chip_buf[3]
        back = _remote(o_ref.at[mine], o_ref.at[mine], send_sems.at[4], recv_sems.at[4], sib)
        back.start()
        _remote(o_ref.at[other], o_ref.at[other], send_sems.at[4], recv_sems.at[4], sib).wait_recv()
        back.wait_send()

    return pl.pallas_call(
        body, name="allreduce_small", in_specs=[VMEM_SPEC], out_specs=VMEM_SPEC,
        out_shape=jax.ShapeDtypeStruct((r, 128), F32),
        scratch_shapes=[pltpu.VMEM((rh, 128), F32), pltpu.VMEM((4, rh, 128), F32),
                        pltpu.SemaphoreType.DMA((5,)), pltpu.SemaphoreType.DMA((5,))],
        compiler_params=pltpu.CompilerParams(vmem_limit_bytes=VMEM_LIMIT))(v)


def _as2d(a):
    a = a.reshape((-1, a.shape[-1])) if a.ndim > 1 else a.reshape(1, -1)
    return a


def _adamw_small(quads):
    n = len(quads)

    def body(*refs):
        for i in range(n):
            w, g, m, v = (r[...] for r in refs[4 * i:4 * i + 4])
            for ref, val in zip(refs[4 * n + 3 * i:4 * n + 3 * i + 3], _adamw(w, g, m, v)):
                ref[...] = val

    return pl.pallas_call(
        body, name="adamw_small", in_specs=[VMEM_SPEC] * (4 * n), out_specs=[VMEM_SPEC] * (3 * n),
        out_shape=[jax.ShapeDtypeStruct(q[0].shape, F32) for q in quads for _ in range(3)],
        compiler_params=pltpu.CompilerParams(vmem_limit_bytes=VMEM_LIMIT))(*[a for q in quads for a in q])


def _where():
    return jnp.stack([2 * lax.axis_index("x") + lax.axis_index("y"), lax.axis_index("c")]).astype(jnp.int32)


_BIG_INDEX = {name: i for i, (name, _, _, _) in enumerate(BIG)}


class _Exchange:
    def __init__(self, inputs, where):
        self.inputs, self.where = inputs, where
        self.full, self.ready = {}, set()
        self.raw, self.got, self.parts, self.landed, self.geom = {}, {}, {}, {}, {}
        for name, k, n, ax in BIG:
            w2 = inputs[name][0]
            rs, cs = w2.shape
            tm = _tile(rs, 512)
            steps = rs // tm
            if ax == 1:
                blk, idx = (tm, cs), lambda i, w: (i, w[0])
            else:
                blk, idx = (tm, n), functools.partial(lambda i, w, steps: (w[0] * steps + i, 0), steps=steps)
            self.full[name] = _placed("cast_" + name, lambda w: w, steps, where, [(w2, (tm, cs), lambda i, w: (i, 0))],
                                      jax.ShapeDtypeStruct((k, n), MXU_DTYPE), blk, idx)

    def _gathered(self, names, outs):
        for name, o in zip(names, outs):
            self.full[name] = o
            self.ready.add(name)

    def gather_carry(self, names):
        start, finish, n_sems = _gather_copies([_BIG_INDEX[n] for n in names])
        return _Carry([self.full[n] for n in names], list(range(len(names))), n_sems, start, finish,
                      functools.partial(self._gathered, names))

    def weight(self, name):
        assert name in self.ready, name
        return self.full[name]

    def grad(self, name, g):
        self.raw[name] = g

    def _swapped(self, names, outs):
        for name, o in zip(names, outs):
            self.got[name] = o

    def _pair_sum(self, name):
        i = _BIG_INDEX[name]
        _, k, n, ax = BIG[i]
        g = self.raw[name]
        if name not in self.got:
            self._swapped([name], _reduce_swap_halves(name, [g], [i]))
        got = self.got[name]
        pr, pc = _piece_shape(k, n, ax)
        tm = _tile(pr, 512)
        spp = pr // tm
        self.geom[name] = (pr, pc, tm, spp)
        if ax == 1:
            g_idx = functools.partial(lambda i, w, spp: (w[1] * spp + i % spp, i // spp), spp=spp)
        else:
            g_idx = functools.partial(lambda i, w, spp: ((i // spp) * 2 * spp + w[1] * spp + i % spp, 0), spp=spp)
        self.parts[name] = _placed(
            "pair_sum_" + name, lambda a, b: a + b, 4 * spp, self.where,
            [(g, (tm, pc), g_idx), (got.reshape(4 * pr, pc), (tm, pc), lambda i, w: (i, 0))],
            jax.ShapeDtypeStruct((4 * pr, pc), BF16), (tm, pc), lambda i, w: (i, 0)).reshape(4, pr, pc)

    def _landed(self, names, outs):
        for name, o in zip(names, outs):
            self.landed[name] = o

    def carry(self, swap=(), ici=()):
        first = second = None
        if swap:
            widx = [_BIG_INDEX[n] for n in swap]
            start, finish, n_sems = _swap_copies(widx)
            first = _Carry([self.raw[n] for n in swap], _swap_shapes(widx), n_sems, start, finish,
                           functools.partial(self._swapped, list(swap)))
        if ici:
            for n in ici:
                self._pair_sum(n)
            start, finish, n_sems = _owner_copies(len(ici))
            parts = [self.parts[n] for n in ici]
            outs = [jax.ShapeDtypeStruct((3,) + p.shape[1:], p.dtype) for p in parts]
            second = _Carry(parts, outs, n_sems, start, finish, functools.partial(self._landed, list(ici)))
        return _join_carries(first, second)

    def _shared(self, outs):
        self.shards = dict(zip([b[0] for b in BIG], outs))

    def finish_carry(self):
        halves = []
        for name, _, _, _ in BIG:
            pr, pc, tm, spp = self.geom[name]
            ins = [(self.parts[name], (None, tm, pc), lambda i, w: (w[0], i, 0))]
            ins += [(self.landed[name], (None, tm, pc), functools.partial(lambda i, w, j: (j, i, 0), j=j))
                    for j in range(3)]
            halves.append(_placed("chip_sum_" + name,
                                  lambda a, b, c, d: ((a.astype(F32) + b.astype(F32)) + c.astype(F32)) + d.astype(F32),
                                  spp, self.where, ins, jax.ShapeDtypeStruct(self.inputs[name].shape[1:], F32), (tm, pc),
                                  functools.partial(lambda i, w, spp: (w[1] * spp + i, 0), spp=spp)))
        start, finish, n_sems = _share_copies()
        return _Carry(halves, list(range(len(halves))), n_sems, start, finish, self._shared)


def _step(inputs):
    x, mem, positions, target = inputs["x"][0], inputs["mem"][0], inputs["positions"], inputs["loss_target"][0]
    pos = positions.reshape(-1, 1)
    ex = _Exchange(inputs, _where())
    sp = {name: _as2d(inputs[name]) for name in SMALL}
    memb, = _rowwise("cast_mem", lambda m: (m,), [mem], [], [(D_MODEL, MXU_DTYPE)])

    loss, dx, gsmall = _local_step(x, memb, pos, target, sp, ex)
    gshard = ex.shards

    out = {}
    for name, _, _, _ in BIG:
        w2, m2, v2 = inputs[name][0], inputs["m_" + name][0], inputs["v_" + name][0]
        n = w2.shape[1]
        d, nm, nv = _rowwise("adamw_" + name, _adamw, [w2, gshard[name], m2, v2], [], [(n, F32)] * 3, tm=_tile(w2.shape[0], 512))
        lead = inputs[name].shape
        out[name] = (gshard[name].reshape(lead), d.reshape(lead), nm.reshape(lead), nv.reshape(lead))

    def tiles(a):
        flat = a.reshape(-1)
        n = -(-flat.shape[0] // 1024) * 1024
        return jnp.pad(flat, (0, n - flat.shape[0])).reshape(n // 128, 128)

    pieces = [tiles(loss[:, :1])] + [tiles(gsmall[name]) for name in SMALL]
    if sum(p.shape[0] for p in pieces) % 16:
        pieces.append(jnp.zeros((8, 128), F32))
    red = _allreduce_small(jnp.concatenate(pieces, axis=0))
    loss_total = red[0, 0]
    grads, off = {}, pieces[0].shape[0]
    for name, p in zip(SMALL, pieces[1:]):
        shp = _as2d(inputs[name]).shape
        grads[name] = red[off:off + p.shape[0]].reshape(-1)[:shp[0] * shp[1]].reshape(shp)
        off += p.shape[0]
    upd = _adamw_small([(_as2d(inputs[n]), grads[n], _as2d(inputs["m_" + n]), _as2d(inputs["v_" + n])) for n in SMALL])
    for i, name in enumerate(SMALL):
        shp = inputs[name].shape
        out[name] = (grads[name].reshape(shp),) + tuple(t.reshape(shp) for t in upd[3 * i:3 * i + 3])
    return loss_total, dx.reshape(inputs["x"].shape), out


_ARG_NAMES = (("x", "mem", "positions") + WEIGHT_ORDER + ("loss_target",) + tuple("m_" + n for n in WEIGHT_ORDER)
              + tuple("v_" + n for n in WEIGHT_ORDER))


def kernel(x, mem, positions, ln_in_g, ln_in_b, w_in, b_in, ssm_log_dt, ssm_a_re, ssm_a_im, ssm_b_re, ssm_b_im, ssm_c_re, ssm_c_im, ssm_d, w_glu, b_glu, w_att_up, w_mix_out, b_mix_out, ln1_g, ln1_b, w_xq, w_xkv, w_xo, ln2_g, ln2_b, w_ff1, b_ff1, w_ff2, b_ff2, ln3_g, ln3_b, loss_target, m_ln_in_g, m_ln_in_b, m_w_in, m_b_in, m_ssm_log_dt, m_ssm_a_re, m_ssm_a_im, m_ssm_b_re, m_ssm_b_im, m_ssm_c_re, m_ssm_c_im, m_ssm_d, m_w_glu, m_b_glu, m_w_att_up, m_w_mix_out, m_b_mix_out, m_ln1_g, m_ln1_b, m_w_xq, m_w_xkv, m_w_xo, m_ln2_g, m_ln2_b, m_w_ff1, m_b_ff1, m_w_ff2, m_b_ff2, m_ln3_g, m_ln3_b, v_ln_in_g, v_ln_in_b, v_w_in, v_b_in, v_ssm_log_dt, v_ssm_a_re, v_ssm_a_im, v_ssm_b_re, v_ssm_b_im, v_ssm_c_re, v_ssm_c_im, v_ssm_d, v_w_glu, v_b_glu, v_w_att_up, v_w_mix_out, v_b_mix_out, v_ln1_g, v_ln1_b, v_w_xq, v_w_xkv, v_w_xo, v_ln2_g, v_ln2_b, v_w_ff1, v_b_ff1, v_w_ff2, v_b_ff2, v_ln3_g, v_ln3_b):
    args = (x, mem, positions, ln_in_g, ln_in_b, w_in, b_in, ssm_log_dt, ssm_a_re, ssm_a_im, ssm_b_re, ssm_b_im, ssm_c_re, ssm_c_im, ssm_d, w_glu, b_glu, w_att_up, w_mix_out, b_mix_out, ln1_g, ln1_b, w_xq, w_xkv, w_xo, ln2_g, ln2_b, w_ff1, b_ff1, w_ff2, b_ff2, ln3_g, ln3_b, loss_target, m_ln_in_g, m_ln_in_b, m_w_in, m_b_in, m_ssm_log_dt, m_ssm_a_re, m_ssm_a_im, m_ssm_b_re, m_ssm_b_im, m_ssm_c_re, m_ssm_c_im, m_ssm_d, m_w_glu, m_b_glu, m_w_att_up, m_w_mix_out, m_b_mix_out, m_ln1_g, m_ln1_b, m_w_xq, m_w_xkv, m_w_xo, m_ln2_g, m_ln2_b, m_w_ff1, m_b_ff1, m_w_ff2, m_b_ff2, m_ln3_g, m_ln3_b, v_ln_in_g, v_ln_in_b, v_w_in, v_b_in, v_ssm_log_dt, v_ssm_a_re, v_ssm_a_im, v_ssm_b_re, v_ssm_b_im, v_ssm_c_re, v_ssm_c_im, v_ssm_d, v_w_glu, v_b_glu, v_w_att_up, v_w_mix_out, v_b_mix_out, v_ln1_g, v_ln1_b, v_w_xq, v_w_xkv, v_w_xo, v_ln2_g, v_ln2_b, v_w_ff1, v_b_ff1, v_w_ff2, v_b_ff2, v_ln3_g, v_ln3_b)
    assert len(args) == len(_ARG_NAMES)
    inputs = dict(zip(_ARG_NAMES, args))
    loss, dx, out = _step(inputs)
    res = [loss, dx]
    for k in range(4):
        res += [out[name][k] for name in WEIGHT_ORDER]
    return tuple(res)
```
